```python
import math
import jax, jax.numpy as jnp
from jax import lax
import numpy as np

D_MODEL = 2048
BATCH = 8
SEQ = 2048
DEPTH = 2

N_A_LAYERS = DEPTH // 2
N_B_LAYERS = DEPTH - N_A_LAYERS

CONV_WIDTH = 31

HEAD_DIM = 128
N_HEADS = D_MODEL // HEAD_DIM
N_KV_HEADS = 4
GROUP = N_HEADS // N_KV_HEADS
WINDOWS = (128, 512, 2048)
DILATIONS = (1, 4, 16)
BLOCK = 128
PAD_UNIT = math.lcm(*DILATIONS) * BLOCK

ROPE_THETA = 500000.0
ROT_DIM = HEAD_DIM // 4

D_FF = 4 * D_MODEL

NORM_EPS = 1e-6
LN_EPS = 1e-5

kernel_name = "yoco_conformer_dilated_hybrid"


def rmsnorm(x, g):
    x32 = x.astype(jnp.float32)
    y = x32 * lax.rsqrt(jnp.mean(x32 * x32, axis=-1, keepdims=True) + NORM_EPS)
    return (y * g.astype(jnp.float32)).astype(x.dtype)


def layernorm(x, g, b):
    x32 = x.astype(jnp.float32)
    mu = jnp.mean(x32, axis=-1, keepdims=True)
    xc = x32 - mu
    var = jnp.mean(xc * xc, axis=-1, keepdims=True)
    y = xc * lax.rsqrt(var + LN_EPS) * g.astype(jnp.float32) + b.astype(jnp.float32)
    return y.astype(x.dtype)


def rope_tables(seq):
    pos = jnp.arange(seq, dtype=jnp.float32)
    inv = ROPE_THETA ** (-jnp.arange(0, ROT_DIM, 2, dtype=jnp.float32) / ROT_DIM)
    ang = pos[:, None] * inv[None, :]
    return jnp.cos(ang), jnp.sin(ang)


def partial_rope(x, cos, sin):
    half = ROT_DIM // 2
    shape = (1, x.shape[1]) + (1,) * (x.ndim - 3) + (half,)
    c = cos.reshape(shape).astype(x.dtype)
    s = sin.reshape(shape).astype(x.dtype)
    x1 = x[..., :half]
    x2 = x[..., half:ROT_DIM]
    return jnp.concatenate([x1 * c - x2 * s, x2 * c + x1 * s, x[..., ROT_DIM:]], axis=-1)


def conformer_conv(y, w_in, b_in, w_dw, b_dw, ln_g, ln_b, w_out, b_out):
    u = y @ w_in + b_in
    a, gate = jnp.split(u, 2, axis=-1)
    u = a * jax.nn.sigmoid(gate)
    u = lax.conv_general_dilated(
        u, w_dw[:, None, :], window_strides=(1,), padding=[(CONV_WIDTH - 1, 0)],
        dimension_numbers=("NWC", "WIO", "NWC"), feature_group_count=u.shape[-1]) + b_dw
    u = jax.nn.silu(layernorm(u, ln_g, ln_b))
    return u @ w_out + b_out


def sq_relu_mlp(y, w_in, w_out):
    h = jax.nn.relu(y @ w_in)
    return (h * h) @ w_out


def shared_kv(h, g, w_kv, cos, sin):
    B, S, _ = h.shape
    kv = rmsnorm(h, g) @ w_kv
    k, v = jnp.split(kv, 2, axis=-1)
    k = partial_rope(k.reshape(B, S, N_KV_HEADS, HEAD_DIM), cos, sin)
    v = v.reshape(B, S, N_KV_HEADS, HEAD_DIM)
    return k, v


def _with_prev_block(t):
    prev = jnp.pad(t[:, :-1], ((0, 0), (1, 0)) + ((0, 0),) * (t.ndim - 2))
    return jnp.concatenate([prev, t], axis=2)


def dilated_branch(q, k, v, dil, steps):
    B, Sp, KVH, G, hd = q.shape
    n = Sp // dil // BLOCK
    qb = q.reshape(B, n, BLOCK, dil, KVH, G, hd)
    kw = _with_prev_block(k.reshape(B, n, BLOCK, dil, KVH, hd))
    vw = _with_prev_block(v.reshape(B, n, BLOCK, dil, KVH, hd))
    s = jnp.einsum("bncrhgd,bnkrhd->bnrhgck", qb, kw,
                   preferred_element_type=jnp.float32) * (1.0 / math.sqrt(hd))
    c = jnp.arange(BLOCK)[:, None]
    j = jnp.arange(2 * BLOCK)[None, :]
    dist = c + BLOCK - j
    band = (dist >= 0) & (dist <= steps)
    first = (jnp.arange(n) == 0)[:, None, None]
    valid = band[None] & ~(first & (j < BLOCK)[None])
    s = jnp.where(valid[None, :, None, None, None], s, -jnp.inf)
    m = jnp.max(s, axis=-1, keepdims=True)
    p = jnp.exp(s - m)
    l = jnp.sum(p, axis=-1, keepdims=True)
    o = jnp.einsum("bnrhgck,bnkrhd->bncrhgd", p.astype(v.dtype), vw)
    l_t = jnp.transpose(l[..., 0], (0, 1, 5, 2, 3, 4))
    lse = jnp.transpose((m + jnp.log(l))[..., 0], (0, 1, 5, 2, 3, 4))
    o = o.astype(jnp.float32) / l_t[..., None]
    return o.reshape(B, Sp, KVH, G, hd), lse.reshape(B, Sp, KVH, G)


def dilated_mixture(q, k, v):
    S = q.shape[1]
    Sp = -(-S // PAD_UNIT) * PAD_UNIT
    pad = Sp - S
    q = jnp.pad(q, ((0, 0), (0, pad), (0, 0), (0, 0), (0, 0)))
    k = jnp.pad(k, ((0, 0), (0, pad), (0, 0), (0, 0)))
    v = jnp.pad(v, ((0, 0), (0, pad), (0, 0), (0, 0)))
    outs, lses = [], []
    for win, dil in zip(WINDOWS, DILATIONS):
        o, lse = dilated_branch(q, k, v, dil, win // dil)
        outs.append(o)
        lses.append(lse)
    w = jax.nn.softmax(jnp.stack(lses, axis=0), axis=0)
    o = jnp.einsum("ibshg,ibshgd->bshgd", w, jnp.stack(outs, axis=0))
    return o[:, :S].astype(q.dtype)


def dilated_attn_layer(y, k, v, w_q, w_o, cos, sin):
    B, S, _ = y.shape
    q = partial_rope((y @ w_q).reshape(B, S, N_KV_HEADS, GROUP, HEAD_DIM), cos, sin)
    o = dilated_mixture(q, k, v)
    return o.reshape(B, S, N_HEADS * HEAD_DIM) @ w_o


def _fwd_setup_inputs(seed: int = 0) -> dict:
    key = jax.random.key(seed)
    ks = jax.random.split(key, 20)
    D = D_MODEL
    f32 = jnp.float32

    def nrm(k, shape, scale):
        return jax.random.normal(k, shape, f32) * scale

    return {
        "x": nrm(ks[0], (BATCH, SEQ, D), 1.0),
        "norm_mix": 1.0 + nrm(ks[1], (DEPTH, D), 0.01),
        "norm_mlp": 1.0 + nrm(ks[2], (DEPTH, D), 0.01),
        "conv_w_in": nrm(ks[3], (N_A_LAYERS, D, 2 * D), D ** -0.5),
        "conv_b_in": nrm(ks[4], (N_A_LAYERS, 2 * D), 0.02),
        "conv_w_dw": nrm(ks[5], (N_A_LAYERS, CONV_WIDTH, D), CONV_WIDTH ** -0.5),
        "conv_b_dw": nrm(ks[6], (N_A_LAYERS, D), 0.02),
        "conv_ln_g": 1.0 + nrm(ks[7], (N_A_LAYERS, D), 0.01),
        "conv_ln_b": nrm(ks[8], (N_A_LAYERS, D), 0.02),
        "conv_w_out": nrm(ks[9], (N_A_LAYERS, D, D), D ** -0.5),
        "conv_b_out": nrm(ks[10], (N_A_LAYERS, D), 0.02),
        "kv_norm": 1.0 + nrm(ks[11], (D,), 0.01),
        "w_kv": nrm(ks[12], (D, 2 * N_KV_HEADS * HEAD_DIM), D ** -0.5),
        "attn_w_q": nrm(ks[13], (N_B_LAYERS, D, N_HEADS * HEAD_DIM), D ** -0.5),
        "attn_w_o": nrm(ks[14], (N_B_LAYERS, N_HEADS * HEAD_DIM, D), (N_HEADS * HEAD_DIM) ** -0.5),
        "mlp_w_in": nrm(ks[15], (DEPTH, D, D_FF), D ** -0.5),
        "mlp_w_out": nrm(ks[16], (DEPTH, D_FF, D), D_FF ** -0.5),
        "final_norm": 1.0 + nrm(ks[17], (D,), 0.01),
    }


def _fwd_reference(x, norm_mix, norm_mlp, conv_w_in, conv_b_in, conv_w_dw, conv_b_dw,
              conv_ln_g, conv_ln_b, conv_w_out, conv_b_out, kv_norm, w_kv,
              attn_w_q, attn_w_o, mlp_w_in, mlp_w_out, final_norm):
    S = x.shape[1]
    cos, sin = rope_tables(S)
    h = x
    k_sh = None
    v_sh = None
    for layer in range(DEPTH):
        y = rmsnorm(h, norm_mix[layer])
        if layer < N_A_LAYERS:
            a = layer
            h = h + conformer_conv(y, conv_w_in[a], conv_b_in[a], conv_w_dw[a], conv_b_dw[a],
                                   conv_ln_g[a], conv_ln_b[a], conv_w_out[a], conv_b_out[a])
        else:
            j = layer - N_A_LAYERS
            h = h + dilated_attn_layer(y, k_sh, v_sh, attn_w_q[j], attn_w_o[j], cos, sin)
        h = h + sq_relu_mlp(rmsnorm(h, norm_mlp[layer]), mlp_w_in[layer], mlp_w_out[layer])
        if layer == N_A_LAYERS - 1:
            k_sh, v_sh = shared_kv(h, kv_norm, w_kv, cos, sin)
    return rmsnorm(h, final_norm)


import jax as _jax
import jax.numpy as _jnp

TWIN_FORMAT = 'train_step'
FWD_PARAMS = ['x', 'norm_mix', 'norm_mlp', 'conv_w_in', 'conv_b_in', 'conv_w_dw', 'conv_b_dw', 'conv_ln_g', 'conv_ln_b', 'conv_w_out', 'conv_b_out', 'kv_norm', 'w_kv', 'attn_w_q', 'attn_w_o', 'mlp_w_in', 'mlp_w_out', 'final_norm']
TWIN_WEIGHTS = ['norm_mix', 'norm_mlp', 'conv_w_in', 'conv_b_in', 'conv_w_dw', 'conv_b_dw', 'conv_ln_g', 'conv_ln_b', 'conv_w_out', 'conv_b_out', 'kv_norm', 'w_kv', 'attn_w_q', 'attn_w_o', 'mlp_w_in', 'mlp_w_out', 'final_norm']
TWIN_DIFF_INPUT = 'x'
TWIN_INPUTS = ['x', 'norm_mix', 'norm_mlp', 'conv_w_in', 'conv_b_in', 'conv_w_dw', 'conv_b_dw', 'conv_ln_g', 'conv_ln_b', 'conv_w_out', 'conv_b_out', 'kv_norm', 'w_kv', 'attn_w_q', 'attn_w_o', 'mlp_w_in', 'mlp_w_out', 'final_norm', 'loss_target', 'm_norm_mix', 'm_norm_mlp', 'm_conv_w_in', 'm_conv_b_in', 'm_conv_w_dw', 'm_conv_b_dw', 'm_conv_ln_g', 'm_conv_ln_b', 'm_conv_w_out', 'm_conv_b_out', 'm_kv_norm', 'm_w_kv', 'm_attn_w_q', 'm_attn_w_o', 'm_mlp_w_in', 'm_mlp_w_out', 'm_final_norm', 'v_norm_mix', 'v_norm_mlp', 'v_conv_w_in', 'v_conv_b_in', 'v_conv_w_dw', 'v_conv_b_dw', 'v_conv_ln_g', 'v_conv_ln_b', 'v_conv_w_out', 'v_conv_b_out', 'v_kv_norm', 'v_w_kv', 'v_attn_w_q', 'v_attn_w_o', 'v_mlp_w_in', 'v_mlp_w_out', 'v_final_norm']
TWIN_OUTPUTS = ['loss', 'grad_x', 'grad_norm_mix', 'grad_norm_mlp', 'grad_conv_w_in', 'grad_conv_b_in', 'grad_conv_w_dw', 'grad_conv_b_dw', 'grad_conv_ln_g', 'grad_conv_ln_b', 'grad_conv_w_out', 'grad_conv_b_out', 'grad_kv_norm', 'grad_w_kv', 'grad_attn_w_q', 'grad_attn_w_o', 'grad_mlp_w_in', 'grad_mlp_w_out', 'grad_final_norm', 'delta_norm_mix', 'delta_norm_mlp', 'delta_conv_w_in', 'delta_conv_b_in', 'delta_conv_w_dw', 'delta_conv_b_dw', 'delta_conv_ln_g', 'delta_conv_ln_b', 'delta_conv_w_out', 'delta_conv_b_out', 'delta_kv_norm', 'delta_w_kv', 'delta_attn_w_q', 'delta_attn_w_o', 'delta_mlp_w_in', 'delta_mlp_w_out', 'delta_final_norm', 'new_m_norm_mix', 'new_m_norm_mlp', 'new_m_conv_w_in', 'new_m_conv_b_in', 'new_m_conv_w_dw', 'new_m_conv_b_dw', 'new_m_conv_ln_g', 'new_m_conv_ln_b', 'new_m_conv_w_out', 'new_m_conv_b_out', 'new_m_kv_norm', 'new_m_w_kv', 'new_m_attn_w_q', 'new_m_attn_w_o', 'new_m_mlp_w_in', 'new_m_mlp_w_out', 'new_m_final_norm', 'new_v_norm_mix', 'new_v_norm_mlp', 'new_v_conv_w_in', 'new_v_conv_b_in', 'new_v_conv_w_dw', 'new_v_conv_b_dw', 'new_v_conv_ln_g', 'new_v_conv_ln_b', 'new_v_conv_w_out', 'new_v_conv_b_out', 'new_v_kv_norm', 'new_v_w_kv', 'new_v_attn_w_q', 'new_v_attn_w_o', 'new_v_mlp_w_in', 'new_v_mlp_w_out', 'new_v_final_norm']
TWIN_LEAF_KINDS = {'loss': 'loss', 'grad_x': 'grad_x', 'grad_norm_mix': 'grad_w', 'grad_norm_mlp': 'grad_w', 'grad_conv_w_in': 'grad_w', 'grad_conv_b_in': 'grad_w', 'grad_conv_w_dw': 'grad_w', 'grad_conv_b_dw': 'grad_w', 'grad_conv_ln_g': 'grad_w', 'grad_conv_ln_b': 'grad_w', 'grad_conv_w_out': 'grad_w', 'grad_conv_b_out': 'grad_w', 'grad_kv_norm': 'grad_w', 'grad_w_kv': 'grad_w', 'grad_attn_w_q': 'grad_w', 'grad_attn_w_o': 'grad_w', 'grad_mlp_w_in': 'grad_w', 'grad_mlp_w_out': 'grad_w', 'grad_final_norm': 'grad_w', 'delta_norm_mix': 'delta_w', 'delta_norm_mlp': 'delta_w', 'delta_conv_w_in': 'delta_w', 'delta_conv_b_in': 'delta_w', 'delta_conv_w_dw': 'delta_w', 'delta_conv_b_dw': 'delta_w', 'delta_conv_ln_g': 'delta_w', 'delta_conv_ln_b': 'delta_w', 'delta_conv_w_out': 'delta_w', 'delta_conv_b_out': 'delta_w', 'delta_kv_norm': 'delta_w', 'delta_w_kv': 'delta_w', 'delta_attn_w_q': 'delta_w', 'delta_attn_w_o': 'delta_w', 'delta_mlp_w_in': 'delta_w', 'delta_mlp_w_out': 'delta_w', 'delta_final_norm': 'delta_w', 'new_m_norm_mix': 'new_m', 'new_m_norm_mlp': 'new_m', 'new_m_conv_w_in': 'new_m', 'new_m_conv_b_in': 'new_m', 'new_m_conv_w_dw': 'new_m', 'new_m_conv_b_dw': 'new_m', 'new_m_conv_ln_g': 'new_m', 'new_m_conv_ln_b': 'new_m', 'new_m_conv_w_out': 'new_m', 'new_m_conv_b_out': 'new_m', 'new_m_kv_norm': 'new_m', 'new_m_w_kv': 'new_m', 'new_m_attn_w_q': 'new_m', 'new_m_attn_w_o': 'new_m', 'new_m_mlp_w_in': 'new_m', 'new_m_mlp_w_out': 'new_m', 'new_m_final_norm': 'new_m', 'new_v_norm_mix': 'new_v', 'new_v_norm_mlp': 'new_v', 'new_v_conv_w_in': 'new_v', 'new_v_conv_b_in': 'new_v', 'new_v_conv_w_dw': 'new_v', 'new_v_conv_b_dw': 'new_v', 'new_v_conv_ln_g': 'new_v', 'new_v_conv_ln_b': 'new_v', 'new_v_conv_w_out': 'new_v', 'new_v_conv_b_out': 'new_v', 'new_v_kv_norm': 'new_v', 'new_v_w_kv': 'new_v', 'new_v_attn_w_q': 'new_v', 'new_v_attn_w_o': 'new_v', 'new_v_mlp_w_in': 'new_v', 'new_v_mlp_w_out': 'new_v', 'new_v_final_norm': 'new_v'}


def _forward(args):
    return _fwd_reference(*[args[k] for k in FWD_PARAMS])


def _output_shape():
    out = _jax.eval_shape(lambda: _forward(_fwd_setup_inputs(0)))
    return out.shape, out.dtype

N_MICROBATCH = 1
ADAM_LR = 0.001
ADAM_B1 = 0.9
ADAM_B2 = 0.999
ADAM_EPS = 1e-08
ADAM_WD = 0.01
ADAM_STEP = 10
PER_EXAMPLE_BATCH_AXIS = {'x': 0, 'loss_target': 0}
SHARED_INPUTS = []
_WEIGHT_DTYPES = {'norm_mix': _jnp.float32, 'norm_mlp': _jnp.float32, 'conv_w_in': _jnp.float32, 'conv_b_in': _jnp.float32, 'conv_w_dw': _jnp.float32, 'conv_b_dw': _jnp.float32, 'conv_ln_g': _jnp.float32, 'conv_ln_b': _jnp.float32, 'conv_w_out': _jnp.float32, 'conv_b_out': _jnp.float32, 'kv_norm': _jnp.float32, 'w_kv': _jnp.float32, 'attn_w_q': _jnp.float32, 'attn_w_o': _jnp.float32, 'mlp_w_in': _jnp.float32, 'mlp_w_out': _jnp.float32, 'final_norm': _jnp.float32}
MOMENT_SCALE = {'norm_mix': 2.912716e-02, 'norm_mlp': 4.988798e-02, 'conv_w_in': 2.871506e-02, 'conv_b_in': 3.408541e-02, 'conv_w_dw': 3.787800e-02, 'conv_b_dw': 8.833015e-02, 'conv_ln_g': 4.822902e-02, 'conv_ln_b': 4.381830e-02, 'conv_w_out': 3.744354e-02, 'conv_b_out': 7.660149e-02, 'kv_norm': 1.657427e-02, 'w_kv': 2.292753e-02, 'attn_w_q': 6.160710e-03, 'attn_w_o': 1.440650e-02, 'mlp_w_in': 2.448175e-02, 'mlp_w_out': 4.551636e-02, 'final_norm': 8.228366e+00}


def _to_microbatches(a, axis):
    t = _jnp.moveaxis(a, axis, 0)
    t = t.reshape((N_MICROBATCH, t.shape[0] // N_MICROBATCH) + t.shape[1:])
    return _jnp.moveaxis(t, 1, axis + 1)


def setup_inputs(seed: int = 0) -> dict:
    inp = _fwd_setup_inputs(seed)
    key = _jax.random.fold_in(_jax.random.key(seed), 7919)
    shape, _ = _output_shape()
    out = dict(inp)
    out["loss_target"] = _jax.random.normal(_jax.random.fold_in(key, 0), shape, _jnp.float32)
    for i, name in enumerate(TWIN_WEIGHTS):
        w = inp[name].astype(_jnp.float32)
        if MOMENT_SCALE is None:
            s = _jnp.sqrt(_jnp.mean(_jnp.square(w)) + 1e-30)
        else:
            s = MOMENT_SCALE[name]
        km, kv = _jax.random.split(_jax.random.fold_in(key, i + 1))
        out[name] = w
        out["m_" + name] = s * _jax.random.normal(km, w.shape, _jnp.float32)
        out["v_" + name] = (s * s) * _jax.random.uniform(kv, w.shape, _jnp.float32, 0.5, 1.5)
    if N_MICROBATCH > 1:
        for name, axis in PER_EXAMPLE_BATCH_AXIS.items():
            out[name] = _to_microbatches(out[name], axis)
    return {'x': out['x'], 'norm_mix': out['norm_mix'], 'norm_mlp': out['norm_mlp'], 'conv_w_in': out['conv_w_in'], 'conv_b_in': out['conv_b_in'], 'conv_w_dw': out['conv_w_dw'], 'conv_b_dw': out['conv_b_dw'], 'conv_ln_g': out['conv_ln_g'], 'conv_ln_b': out['conv_ln_b'], 'conv_w_out': out['conv_w_out'], 'conv_b_out': out['conv_b_out'], 'kv_norm': out['kv_norm'], 'w_kv': out['w_kv'], 'attn_w_q': out['attn_w_q'], 'attn_w_o': out['attn_w_o'], 'mlp_w_in': out['mlp_w_in'], 'mlp_w_out': out['mlp_w_out'], 'final_norm': out['final_norm'], 'loss_target': out['loss_target'], 'm_norm_mix': out['m_norm_mix'], 'm_norm_mlp': out['m_norm_mlp'], 'm_conv_w_in': out['m_conv_w_in'], 'm_conv_b_in': out['m_conv_b_in'], 'm_conv_w_dw': out['m_conv_w_dw'], 'm_conv_b_dw': out['m_conv_b_dw'], 'm_conv_ln_g': out['m_conv_ln_g'], 'm_conv_ln_b': out['m_conv_ln_b'], 'm_conv_w_out': out['m_conv_w_out'], 'm_conv_b_out': out['m_conv_b_out'], 'm_kv_norm': out['m_kv_norm'], 'm_w_kv': out['m_w_kv'], 'm_attn_w_q': out['m_attn_w_q'], 'm_attn_w_o': out['m_attn_w_o'], 'm_mlp_w_in': out['m_mlp_w_in'], 'm_mlp_w_out': out['m_mlp_w_out'], 'm_final_norm': out['m_final_norm'], 'v_norm_mix': out['v_norm_mix'], 'v_norm_mlp': out['v_norm_mlp'], 'v_conv_w_in': out['v_conv_w_in'], 'v_conv_b_in': out['v_conv_b_in'], 'v_conv_w_dw': out['v_conv_w_dw'], 'v_conv_b_dw': out['v_conv_b_dw'], 'v_conv_ln_g': out['v_conv_ln_g'], 'v_conv_ln_b': out['v_conv_ln_b'], 'v_conv_w_out': out['v_conv_w_out'], 'v_conv_b_out': out['v_conv_b_out'], 'v_kv_norm': out['v_kv_norm'], 'v_w_kv': out['v_w_kv'], 'v_attn_w_q': out['v_attn_w_q'], 'v_attn_w_o': out['v_attn_w_o'], 'v_mlp_w_in': out['v_mlp_w_in'], 'v_mlp_w_out': out['v_mlp_w_out'], 'v_final_norm': out['v_final_norm']}


def _loss(weights, diff, rest, loss_target):
    with _jax.named_scope("forward"):
        args = {**rest, TWIN_DIFF_INPUT: diff, **{k: w.astype(_WEIGHT_DTYPES[k]) for k, w in weights.items()}}
        y = _forward(args)
    with _jax.named_scope("loss_head"):
        err = _jnp.square(y.astype(_jnp.float32) - loss_target)
        return 0.5 * _jnp.sum(_jnp.mean(err, axis=-1)) if err.ndim else 0.5 * err


def _adamw(w, g, m, v):
    m = ADAM_B1 * m + (1.0 - ADAM_B1) * g
    v = ADAM_B2 * v + (1.0 - ADAM_B2) * _jnp.square(g)
    m_hat = m / (1.0 - ADAM_B1 ** ADAM_STEP)
    v_hat = v / (1.0 - ADAM_B2 ** ADAM_STEP)
    delta = -ADAM_LR * (m_hat / (_jnp.sqrt(v_hat) + ADAM_EPS) + ADAM_WD * w)
    return delta, m, v


def reference(x, norm_mix, norm_mlp, conv_w_in, conv_b_in, conv_w_dw, conv_b_dw, conv_ln_g, conv_ln_b, conv_w_out, conv_b_out, kv_norm, w_kv, attn_w_q, attn_w_o, mlp_w_in, mlp_w_out, final_norm, loss_target, m_norm_mix, m_norm_mlp, m_conv_w_in, m_conv_b_in, m_conv_w_dw, m_conv_b_dw, m_conv_ln_g, m_conv_ln_b, m_conv_w_out, m_conv_b_out, m_kv_norm, m_w_kv, m_attn_w_q, m_attn_w_o, m_mlp_w_in, m_mlp_w_out, m_final_norm, v_norm_mix, v_norm_mlp, v_conv_w_in, v_conv_b_in, v_conv_w_dw, v_conv_b_dw, v_conv_ln_g, v_conv_ln_b, v_conv_w_out, v_conv_b_out, v_kv_norm, v_w_kv, v_attn_w_q, v_attn_w_o, v_mlp_w_in, v_mlp_w_out, v_final_norm):
    given = dict(x=x, norm_mix=norm_mix, norm_mlp=norm_mlp, conv_w_in=conv_w_in, conv_b_in=conv_b_in, conv_w_dw=conv_w_dw, conv_b_dw=conv_b_dw, conv_ln_g=conv_ln_g, conv_ln_b=conv_ln_b, conv_w_out=conv_w_out, conv_b_out=conv_b_out, kv_norm=kv_norm, w_kv=w_kv, attn_w_q=attn_w_q, attn_w_o=attn_w_o, mlp_w_in=mlp_w_in, mlp_w_out=mlp_w_out, final_norm=final_norm, loss_target=loss_target, m_norm_mix=m_norm_mix, m_norm_mlp=m_norm_mlp, m_conv_w_in=m_conv_w_in, m_conv_b_in=m_conv_b_in, m_conv_w_dw=m_conv_w_dw, m_conv_b_dw=m_conv_b_dw, m_conv_ln_g=m_conv_ln_g, m_conv_ln_b=m_conv_ln_b, m_conv_w_out=m_conv_w_out, m_conv_b_out=m_conv_b_out, m_kv_norm=m_kv_norm, m_w_kv=m_w_kv, m_attn_w_q=m_attn_w_q, m_attn_w_o=m_attn_w_o, m_mlp_w_in=m_mlp_w_in, m_mlp_w_out=m_mlp_w_out, m_final_norm=m_final_norm, v_norm_mix=v_norm_mix, v_norm_mlp=v_norm_mlp, v_conv_w_in=v_conv_w_in, v_conv_b_in=v_conv_b_in, v_conv_w_dw=v_conv_w_dw, v_conv_b_dw=v_conv_b_dw, v_conv_ln_g=v_conv_ln_g, v_conv_ln_b=v_conv_ln_b, v_conv_w_out=v_conv_w_out, v_conv_b_out=v_conv_b_out, v_kv_norm=v_kv_norm, v_w_kv=v_w_kv, v_attn_w_q=v_attn_w_q, v_attn_w_o=v_attn_w_o, v_mlp_w_in=v_mlp_w_in, v_mlp_w_out=v_mlp_w_out, v_final_norm=v_final_norm)
    weights = {n: given[n] for n in TWIN_WEIGHTS}
    shared = {n: given[n] for n in SHARED_INPUTS}
    per_example = {n: given[n] for n in ['x']}
    grad_fn = _jax.value_and_grad(_loss, argnums=(0, 1))

    def one_microbatch(ex, loss_target):
        ex = dict(ex)
        diff = ex.pop(TWIN_DIFF_INPUT)
        return grad_fn(weights, diff, {**shared, **ex}, loss_target)

    if N_MICROBATCH == 1:
        loss, (grad_w, grad_x) = one_microbatch(per_example, given["loss_target"])
    else:
        def body(carry, xs):
            loss_sum, grad_sum = carry
            l_k, (gw_k, gx_k) = one_microbatch(xs[0], xs[1])
            with _jax.named_scope("update"):
                return (loss_sum + l_k, _jax.tree.map(_jnp.add, grad_sum, gw_k)), gx_k

        init = (_jnp.zeros((), _jnp.float32), _jax.tree.map(_jnp.zeros_like, weights))
        (loss, grad_w), grad_x = _jax.lax.scan(body, init, (per_example, given["loss_target"]))
    with _jax.named_scope("update"):
        delta_w, new_m, new_v = {}, {}, {}
        for n in TWIN_WEIGHTS:
            delta_w[n], new_m[n], new_v[n] = _adamw(weights[n], grad_w[n], given["m_" + n], given["v_" + n])
    return (loss, grad_x, *[grad_w[n] for n in TWIN_WEIGHTS], *[delta_w[n] for n in TWIN_WEIGHTS],
            *[new_m[n] for n in TWIN_WEIGHTS], *[new_v[n] for n in TWIN_WEIGHTS])
```

```python
import math

import jax
import jax.numpy as jnp
from jax import lax
from jax.experimental import pallas as pl
from jax.experimental.pallas import tpu as pltpu

F32 = jnp.float32
BF16 = jnp.bfloat16
I32 = jnp.int32

NORM_EPS = 1e-6
LN_EPS = 1e-5
HEAD_DIM = 128
N_KV_HEADS = 4
ROT_DIM = 32
ROPE_THETA = 500000.0
CONV_WIDTH = 31
CONV_PAD = 32
ATT_BLOCK = 128
DILATIONS = (1, 4, 16)
ADAM_LR = 0.001
ADAM_B1 = 0.9
ADAM_B2 = 0.999
ADAM_EPS = 1e-08
ADAM_WD = 0.01
ADAM_STEP = 10
N_SHARD = 4
N_DEV = 8
LANES = 128
VMEM_LIMIT = 48 * 1024 * 1024
ROW_TILE = 256
CONV_CB = 128
CONV_T = 128
SMALL_ROWS = 48
MESH = pl.DeviceIdType.MESH
ANY = pl.BlockSpec(memory_space=pl.ANY)


def _params(*sem):
    return pltpu.CompilerParams(dimension_semantics=sem, vmem_limit_bytes=VMEM_LIMIT)


def _sigmoid(x):
    return 1.0 / (1.0 + jnp.exp(-x))


def _wspec(kind, arr_shape, br, bc, pick):
    if kind == "plain":
        return pl.BlockSpec((br, bc), pick)
    per = arr_shape[2] // bc

    def idx(*g):
        rb, cb = pick(*g)
        return (cb // per, rb, cb % per)

    return pl.BlockSpec((None, br, bc), idx)


def _matmul(name, mode, a, b, *, m, n, k, tm, tn, tk, b_kind="plain", outs, extras=(), epilogue=None):
    tm, tn, tk = min(tm, m), min(tn, n), min(tk, k)
    if b_kind == "col" and mode == "nn":
        tn = min(tn, n // N_SHARD)
    if b_kind == "col" and mode == "nt":
        tk = min(tk, k // N_SHARD)
    if any(kind == "col" for _, kind in outs):
        tn = min(tn, n // N_SHARD)
    assert m % tm == 0 and n % tn == 0 and k % tk == 0, (name, m, n, k, tm, tn, tk)
    nk = k // tk
    grid = (m // tm, n // tn, nk)
    if mode == "nn":
        a_spec = pl.BlockSpec((tm, tk), lambda i, j, kk: (i, kk))
        b_spec = _wspec(b_kind, b.shape, tk, tn, lambda i, j, kk: (kk, j))
        dims = (((1,), (0,)), ((), ()))
    elif mode == "nt":
        a_spec = pl.BlockSpec((tm, tk), lambda i, j, kk: (i, kk))
        b_spec = _wspec(b_kind, b.shape, tn, tk, lambda i, j, kk: (j, kk))
        dims = (((1,), (1,)), ((), ()))
    else:
        a_spec = pl.BlockSpec((tk, tm), lambda i, j, kk: (kk, i))
        b_spec = pl.BlockSpec((tk, tn), lambda i, j, kk: (kk, j))
        dims = (((0,), (0,)), ((), ()))
    out_shape, out_specs = [], []
    for dtype, kind in outs:
        shape = (m, n) if kind == "plain" else (N_SHARD, m, n // N_SHARD)
        out_shape.append(jax.ShapeDtypeStruct(shape, dtype))
        out_specs.append(_wspec(kind, shape, tm, tn, lambda i, j, kk: (i, j)))
    n_ex = len(extras)
    ex_specs = {"ij": pl.BlockSpec((tm, tn), lambda i, j, kk: (i, j)),
                "vec": pl.BlockSpec((1, tn), lambda i, j, kk: (0, j)),
                "rows": pl.BlockSpec((tm, LANES), lambda i, j, kk: (i, 0))}

    def body(*refs):
        a_ref, b_ref = refs[0], refs[1]
        ex_refs = refs[2:2 + n_ex]
        out_refs = refs[2 + n_ex:-1]
        acc_ref = refs[-1]
        j = pl.program_id(1)
        kk = pl.program_id(2)

        @pl.when(kk == 0)
        def _():
            acc_ref[...] = jnp.zeros_like(acc_ref)

        acc_ref[...] += lax.dot_general(a_ref[...], b_ref[...], dims, preferred_element_type=F32)

        @pl.when(kk == nk - 1)
        def _():
            if epilogue is None:
                out_refs[0][...] = acc_ref[...].astype(out_refs[0].dtype)
            else:
                epilogue(acc_ref[...], ex_refs, out_refs, j)

    res = pl.pallas_call(
        body, name=name, grid=grid,
        in_specs=[a_spec, b_spec] + [ex_specs[how] for _, how in extras],
        out_specs=out_specs, out_shape=out_shape,
        scratch_shapes=[pltpu.VMEM((tm, tn), F32)],
        compiler_params=_params("parallel", "parallel", "arbitrary"),
    )(a, b, *[e for e, _ in extras])
    return res


def _rope_tables(seq):
    half = ROT_DIM // 2
    pos = jnp.arange(seq, dtype=F32)
    inv = ROPE_THETA ** (-jnp.arange(0, ROT_DIM, 2, dtype=F32) / ROT_DIM)
    ang = pos[:, None] * inv[None, :]
    cos, sin = jnp.cos(ang), jnp.sin(ang)
    zeros = jnp.zeros((seq, HEAD_DIM - ROT_DIM), F32)
    ctab = jnp.concatenate([cos, cos, zeros + 1.0], axis=1)
    atab = jnp.concatenate([-sin, jnp.zeros((seq, half), F32), zeros], axis=1)
    btab = jnp.concatenate([jnp.zeros((seq, half), F32), sin, zeros], axis=1)
    return ctab, atab, btab


def _rope_apply(x, ctab, atab, btab, sign):
    w = x.shape[1]
    reps = w // HEAD_DIM
    half = ROT_DIM // 2
    c = jnp.tile(ctab, (1, reps))
    a = jnp.tile(atab, (1, reps))
    b = jnp.tile(btab, (1, reps))
    up = pltpu.roll(x, w - half, 1)
    down = pltpu.roll(x, half, 1)
    return x * c + sign * (up * a + down * b)


def _rows(t, w):
    return pl.BlockSpec((t, w), lambda i: (i, 0))


def _fixed(shape):
    nd = len(shape)
    return pl.BlockSpec(shape, lambda i: (0,) * nd)


def _rms_fwd(name, x, gains):
    s, d = x.shape
    t = min(ROW_TILE, s)
    ng = len(gains)

    def body(x_ref, *refs):
        xv = x_ref[...]
        r = lax.rsqrt(jnp.mean(xv * xv, axis=-1, keepdims=True) + NORM_EPS)
        xn = xv * r
        for g_ref, y_ref in zip(refs[:ng], refs[ng:]):
            y_ref[...] = (xn * g_ref[...]).astype(BF16)

    return pl.pallas_call(
        body, name=name, grid=(s // t,),
        in_specs=[_rows(t, d)] + [_fixed((1, d))] * ng,
        out_specs=[_rows(t, d)] * ng,
        out_shape=[jax.ShapeDtypeStruct((s, d), BF16)] * ng,
        compiler_params=_params("parallel"),
    )(x, *gains)


def _rms_bwd(name, x, pairs, dh_in, want_colsum=False):
    s, d = x.shape
    t = min(ROW_TILE, s)
    n_p = len(pairs)

    def body(x_ref, dh_ref, *refs):
        g_refs = refs[:n_p]
        dy_refs = refs[n_p:2 * n_p]
        dh_out, dhb_out = refs[2 * n_p], refs[2 * n_p + 1]
        dg_refs = refs[2 * n_p + 2:2 * n_p + 2 + n_p]
        cs_ref = refs[-1] if want_colsum else None
        i = pl.program_id(0)
        xv = x_ref[...]
        r = lax.rsqrt(jnp.mean(xv * xv, axis=-1, keepdims=True) + NORM_EPS)
        xn = xv * r
        dh = dh_ref[...]
        for g_ref, dy_ref, dg_ref in zip(g_refs, dy_refs, dg_refs):
            dy = dy_ref[...].astype(F32)
            u = dy * g_ref[...]
            dh = dh + r * (u - xn * jnp.mean(u * xn, axis=-1, keepdims=True))
            part = jnp.sum(dy * xn, axis=0, keepdims=True)

            @pl.when(i == 0)
            def _():
                dg_ref[...] = part

            @pl.when(i > 0)
            def _():
                dg_ref[...] += part

        dh_out[...] = dh
        dhb_out[...] = dh.astype(BF16)
        if want_colsum:
            col = jnp.sum(dh, axis=0, keepdims=True)

            @pl.when(i == 0)
            def _():
                cs_ref[...] = col

            @pl.when(i > 0)
            def _():
                cs_ref[...] += col

    n_vec = n_p + (1 if want_colsum else 0)
    return pl.pallas_call(
        body, name=name, grid=(s // t,),
        in_specs=[_rows(t, d), _rows(t, d)] + [_fixed((1, d))] * n_p + [_rows(t, d)] * n_p,
        out_specs=[_rows(t, d), _rows(t, d)] + [_fixed((1, d))] * n_vec,
        out_shape=[jax.ShapeDtypeStruct((s, d), F32), jax.ShapeDtypeStruct((s, d), BF16)]
        + [jax.ShapeDtypeStruct((1, d), F32)] * n_vec,
        compiler_params=_params("arbitrary"),
    )(x, dh_in, *[g for g, _ in pairs], *[dy for _, dy in pairs])


def _final_loss(x, g, target):
    s, d = x.shape
    t = min(ROW_TILE, s)

    def body(x_ref, g_ref, t_ref, dh_out, dhb_out, dg_ref, loss_ref):
        i = pl.program_id(0)
        xv = x_ref[...]
        gv = g_ref[...]
        r = lax.rsqrt(jnp.mean(xv * xv, axis=-1, keepdims=True) + NORM_EPS)
        xn = xv * r
        diff = xn * gv - t_ref[...]
        dy = diff / d
        u = dy * gv
        dh = r * (u - xn * jnp.mean(u * xn, axis=-1, keepdims=True))
        dh_out[...] = dh
        dhb_out[...] = dh.astype(BF16)
        dg = jnp.sum(dy * xn, axis=0, keepdims=True)
        lc = jnp.sum(0.5 * diff * dy, axis=0, keepdims=True)

        @pl.when(i == 0)
        def _():
            dg_ref[...] = dg
            loss_ref[...] = lc

        @pl.when(i > 0)
        def _():
            dg_ref[...] += dg
            loss_ref[...] += lc

    return pl.pallas_call(
        body, name="final_loss", grid=(s // t,),
        in_specs=[_rows(t, d), _fixed((1, d)), _rows(t, d)],
        out_specs=[_rows(t, d), _rows(t, d), _fixed((1, d)), _fixed((1, d))],
        out_shape=[jax.ShapeDtypeStruct((s, d), F32), jax.ShapeDtypeStruct((s, d), BF16),
                   jax.ShapeDtypeStruct((1, d), F32), jax.ShapeDtypeStruct((1, d), F32)],
        compiler_params=_params("arbitrary"),
    )(x, g, target)


def _ln_silu_fwd(c, g, b):
    s, d = c.shape
    t = min(ROW_TILE, s)

    def body(c_ref, g_ref, b_ref, s_ref):
        cv = c_ref[...]
        mu = jnp.mean(cv, axis=-1, keepdims=True)
        xc = cv - mu
        rs = lax.rsqrt(jnp.mean(xc * xc, axis=-1, keepdims=True) + LN_EPS)
        ln = xc * rs * g_ref[...] + b_ref[...]
        s_ref[...] = (ln * _sigmoid(ln)).astype(BF16)

    return pl.pallas_call(
        body, name="ln_silu_fwd", grid=(s // t,),
        in_specs=[_rows(t, d), _fixed((1, d)), _fixed((1, d))],
        out_specs=_rows(t, d), out_shape=jax.ShapeDtypeStruct((s, d), BF16),
        compiler_params=_params("parallel"),
    )(c, g, b)


def _ln_silu_bwd(c, g, b, ds):
    s, d = c.shape
    t = min(ROW_TILE, s)

    def body(c_ref, g_ref, b_ref, ds_ref, dc_ref, dg_ref, db_ref, dbdw_ref):
        i = pl.program_id(0)
        cv = c_ref[...]
        gv = g_ref[...]
        mu = jnp.mean(cv, axis=-1, keepdims=True)
        xc = cv - mu
        rs = lax.rsqrt(jnp.mean(xc * xc, axis=-1, keepdims=True) + LN_EPS)
        nrm = xc * rs
        ln = nrm * gv + b_ref[...]
        sig = _sigmoid(ln)
        dln = ds_ref[...].astype(F32) * sig * (1.0 + ln * (1.0 - sig))
        dn = dln * gv
        dc = rs * (dn - jnp.mean(dn, axis=-1, keepdims=True)
                   - nrm * jnp.mean(dn * nrm, axis=-1, keepdims=True))
        dc_ref[...] = dc
        pg = jnp.sum(dln * nrm, axis=0, keepdims=True)
        pb = jnp.sum(dln, axis=0, keepdims=True)
        pc = jnp.sum(dc, axis=0, keepdims=True)

        @pl.when(i == 0)
        def _():
            dg_ref[...] = pg
            db_ref[...] = pb
            dbdw_ref[...] = pc

        @pl.when(i > 0)
        def _():
            dg_ref[...] += pg
            db_ref[...] += pb
            dbdw_ref[...] += pc

    return pl.pallas_call(
        body, name="ln_silu_bwd", grid=(s // t,),
        in_specs=[_rows(t, d), _fixed((1, d)), _fixed((1, d)), _rows(t, d)],
        out_specs=[_rows(t, d)] + [_fixed((1, d))] * 3,
        out_shape=[jax.ShapeDtypeStruct((s, d), F32)] + [jax.ShapeDtypeStruct((1, d), F32)] * 3,
        compiler_params=_params("arbitrary"),
    )(c, g, b, ds)


def _attn_combine(o_list, lse_list):
    s, d = o_list[0].shape
    nh = d // HEAD_DIM
    t = min(ROW_TILE, s)
    nb = len(o_list)

    def body(*refs):
        o_refs = refs[:nb]
        l_refs = refs[nb:2 * nb]
        o_out, l_out = refs[2 * nb], refs[2 * nb + 1]
        ls = [r[...] for r in l_refs]
        mx = ls[0]
        for l in ls[1:]:
            mx = jnp.maximum(mx, l)
        es = [jnp.exp(l - mx) for l in ls]
        den = es[0]
        for e in es[1:]:
            den = den + e
        l_out[...] = mx + jnp.log(den)
        ws = [e / den for e in es]
        for h in range(nh):
            cols = slice(h * HEAD_DIM, (h + 1) * HEAD_DIM)
            acc = jnp.zeros((t, HEAD_DIM), F32)
            for o_ref, w in zip(o_refs, ws):
                acc = acc + w[:, h:h + 1] * o_ref[:, cols].astype(F32)
            o_out[:, cols] = acc.astype(BF16)

    return pl.pallas_call(
        body, name="attn_combine", grid=(s // t,),
        in_specs=[_rows(t, d)] * nb + [_rows(t, nh)] * nb,
        out_specs=[_rows(t, d), _rows(t, nh)],
        out_shape=[jax.ShapeDtypeStruct((s, d), BF16), jax.ShapeDtypeStruct((s, nh), F32)],
        compiler_params=_params("parallel"),
    )(*o_list, *lse_list)


def _attn_delta(do, o):
    s, d = o.shape
    nh = d // HEAD_DIM
    t = min(ROW_TILE, s)

    def body(do_ref, o_ref, dl_ref):
        lane = lax.broadcasted_iota(I32, (t, nh), 1)
        out = jnp.zeros((t, nh), F32)
        for h in range(nh):
            cols = slice(h * HEAD_DIM, (h + 1) * HEAD_DIM)
            v = jnp.sum(do_ref[:, cols].astype(F32) * o_ref[:, cols].astype(F32), axis=-1, keepdims=True)
            out = jnp.where(lane == h, v, out)
        dl_ref[...] = out

    return pl.pallas_call(
        body, name="attn_delta", grid=(s // t,),
        in_specs=[_rows(t, d), _rows(t, d)],
        out_specs=_rows(t, nh), out_shape=jax.ShapeDtypeStruct((s, nh), F32),
        compiler_params=_params("parallel"),
    )(do, o)


def _rope_bwd_sum(name, parts, tabs, rope_cols):
    s, w = parts[0].shape
    t = min(ROW_TILE, s)
    n_p = len(parts)

    def body(*refs):
        p_refs = refs[:n_p]
        c_ref, a_ref, b_ref = refs[n_p:n_p + 3]
        out = refs[-1]
        tot = p_refs[0][...].astype(F32)
        for p in p_refs[1:]:
            tot = tot + p[...].astype(F32)
        rot = _rope_apply(tot[:, :rope_cols], c_ref[...], a_ref[...], b_ref[...], -1.0)
        out[:, :rope_cols] = rot.astype(BF16)
        if rope_cols < w:
            out[:, rope_cols:] = tot[:, rope_cols:].astype(BF16)

    return pl.pallas_call(
        body, name=name, grid=(s // t,),
        in_specs=[_rows(t, w)] * n_p + [_rows(t, HEAD_DIM)] * 3,
        out_specs=_rows(t, w), out_shape=jax.ShapeDtypeStruct((s, w), BF16),
        compiler_params=_params("parallel"),
    )(*parts, *tabs)


def _dwconv_fwd(u, w_dw, b_dw):
    s, d2 = u.shape
    d = d2 // 2
    cb = min(CONV_CB, d)
    nblk = d // cb
    tt = min(CONV_T, s)

    def body(ua_ref, ug_ref, w_ref, b_ref, c_ref, xp_ref):
        gl = ua_ref[...].astype(F32) * _sigmoid(ug_ref[...].astype(F32))
        xp_ref[0:CONV_PAD, :] = jnp.zeros((CONV_PAD, cb), F32)
        xp_ref[CONV_PAD:, :] = gl
        wv = w_ref[...]
        bv = b_ref[...]
        for t0 in range(0, s, tt):
            acc = jnp.zeros((tt, cb), F32) + bv
            for kk in range(CONV_WIDTH):
                off = t0 + CONV_PAD - (CONV_WIDTH - 1) + kk
                acc = acc + wv[kk:kk + 1, :] * xp_ref[off:off + tt, :]
            c_ref[t0:t0 + tt, :] = acc

    return pl.pallas_call(
        body, name="dwconv_fwd", grid=(nblk,),
        in_specs=[pl.BlockSpec((s, cb), lambda j: (0, j)), pl.BlockSpec((s, cb), lambda j: (0, j + nblk)),
                  pl.BlockSpec((CONV_PAD, cb), lambda j: (0, j)), pl.BlockSpec((1, cb), lambda j: (0, j))],
        out_specs=pl.BlockSpec((s, cb), lambda j: (0, j)),
        out_shape=jax.ShapeDtypeStruct((s, d), F32),
        scratch_shapes=[pltpu.VMEM((s + CONV_PAD, cb), F32)],
        compiler_params=_params("parallel"),
    )(u, u, w_dw, b_dw)


def _dwconv_bwd(u, w_dw, dc):
    s, d2 = u.shape
    d = d2 // 2
    cb = min(CONV_CB, d)
    nblk = d // cb
    tt = min(CONV_T, s)

    def body(ua_ref, ug_ref, w_ref, dc_ref, da_ref, dgt_ref, dw_ref, dba_ref, dbg_ref, glp_ref, dcp_ref, acc_ref):
        a = ua_ref[...].astype(F32)
        sig = _sigmoid(ug_ref[...].astype(F32))
        glp_ref[0:CONV_PAD, :] = jnp.zeros((CONV_PAD, cb), F32)
        glp_ref[CONV_PAD:, :] = a * sig
        dcp_ref[0:s, :] = dc_ref[...]
        dcp_ref[s:, :] = jnp.zeros((CONV_PAD, cb), F32)
        acc_ref[...] = jnp.zeros_like(acc_ref)
        wv = w_ref[...]
        dba = jnp.zeros((1, cb), F32)
        dbg = jnp.zeros((1, cb), F32)
        for t0 in range(0, s, tt):
            dgl = jnp.zeros((tt, cb), F32)
            dct = dc_ref[t0:t0 + tt, :]
            for kk in range(CONV_WIDTH):
                off = t0 + (CONV_WIDTH - 1) - kk
                dgl = dgl + wv[kk:kk + 1, :] * dcp_ref[off:off + tt, :]
                goff = t0 + CONV_PAD - (CONV_WIDTH - 1) + kk
                prod = dct * glp_ref[goff:goff + tt, :]
                acc_ref[8 * kk:8 * kk + 8, :] += jnp.sum(prod.reshape(tt // 8, 8, cb), axis=0)
            at = ua_ref[t0:t0 + tt, :].astype(F32)
            st = _sigmoid(ug_ref[t0:t0 + tt, :].astype(F32))
            da = dgl * st
            dg = dgl * at * st * (1.0 - st)
            da_ref[t0:t0 + tt, :] = da.astype(BF16)
            dgt_ref[t0:t0 + tt, :] = dg.astype(BF16)
            dba = dba + jnp.sum(da, axis=0, keepdims=True)
            dbg = dbg + jnp.sum(dg, axis=0, keepdims=True)
        dba_ref[...] = dba
        dbg_ref[...] = dbg
        for kk in range(CONV_WIDTH):
            dw_ref[kk:kk + 1, :] = jnp.sum(acc_ref[8 * kk:8 * kk + 8, :], axis=0, keepdims=True)
        dw_ref[CONV_WIDTH:, :] = jnp.zeros((CONV_PAD - CONV_WIDTH, cb), F32)

    blk = pl.BlockSpec((s, cb), lambda j: (0, j))
    vec = pl.BlockSpec((1, cb), lambda j: (0, j))
    return pl.pallas_call(
        body, name="dwconv_bwd", grid=(nblk,),
        in_specs=[blk, pl.BlockSpec((s, cb), lambda j: (0, j + nblk)),
                  pl.BlockSpec((CONV_PAD, cb), lambda j: (0, j)), blk],
        out_specs=[blk, blk, pl.BlockSpec((CONV_PAD, cb), lambda j: (0, j)), vec, vec],
        out_shape=[jax.ShapeDtypeStruct((s, d), BF16), jax.ShapeDtypeStruct((s, d), BF16),
                   jax.ShapeDtypeStruct((CONV_PAD, d), F32),
                   jax.ShapeDtypeStruct((1, d), F32), jax.ShapeDtypeStruct((1, d), F32)],
        scratch_shapes=[pltpu.VMEM((s + CONV_PAD, cb), F32), pltpu.VMEM((s + CONV_PAD, cb), F32),
                        pltpu.VMEM((8 * CONV_PAD, cb), F32)],
        compiler_params=_params("parallel"),
    )(u, u, w_dw, dc)


def _stack_heads(x, group):
    return jnp.concatenate([x[:, g * HEAD_DIM:(g + 1) * HEAD_DIM] for g in range(group)], axis=0)


def _unstack_heads(x, group):
    return jnp.concatenate([x[g * ATT_BLOCK:(g + 1) * ATT_BLOCK, :] for g in range(group)], axis=1)


def _stack_cols(x, group):
    return jnp.concatenate([x[:, g:g + 1] for g in range(group)], axis=0)


def _band_mask(nb, group):
    rows = group * ATT_BLOCK
    row = lax.broadcasted_iota(I32, (rows, 2 * ATT_BLOCK), 0) % ATT_BLOCK
    col = lax.broadcasted_iota(I32, (rows, 2 * ATT_BLOCK), 1)
    return (col >= row) & (col <= row + ATT_BLOCK) & ((col >= ATT_BLOCK) | (nb > 0))


def _window(ref, nb):
    prev = pl.multiple_of(jnp.maximum(nb - 1, 0) * ATT_BLOCK, ATT_BLOCK)
    cur = pl.multiple_of(nb * ATT_BLOCK, ATT_BLOCK)
    return jnp.concatenate([ref[pl.ds(prev, ATT_BLOCK), :], ref[pl.ds(cur, ATT_BLOCK), :]], axis=0)


def _attn_fwd(name, q, kv, dil, d):
    sd = q.shape[0]
    group = d // HEAD_DIM // N_KV_HEADS
    gw = group * HEAD_DIM
    nblk = sd // ATT_BLOCK
    scale = 1.0 / math.sqrt(HEAD_DIM)
    nt = (((1,), (1,)), ((), ()))

    def body(q_ref, k_ref, v_ref, o_ref, lse_ref):
        lane = lax.broadcasted_iota(I32, (ATT_BLOCK, group), 1)

        def step(nb, carry):
            rows = pl.ds(pl.multiple_of(nb * ATT_BLOCK, ATT_BLOCK), ATT_BLOCK)
            qs = _stack_heads(q_ref[rows, :], group)
            kw = _window(k_ref, nb)
            vw = _window(v_ref, nb)
            sc = lax.dot_general(qs, kw, nt, preferred_element_type=F32) * scale
            sc = jnp.where(_band_mask(nb, group), sc, -jnp.inf)
            mx = jnp.max(sc, axis=-1, keepdims=True)
            p = jnp.exp(sc - mx)
            l = jnp.sum(p, axis=-1, keepdims=True)
            o = jnp.dot(p.astype(BF16), vw, preferred_element_type=F32) / l
            o_ref[rows, :] = _unstack_heads(o, group).astype(BF16)
            lse = mx + jnp.log(l)
            out = jnp.zeros((ATT_BLOCK, group), F32)
            for g in range(group):
                out = jnp.where(lane == g, lse[g * ATT_BLOCK:(g + 1) * ATT_BLOCK, :], out)
            lse_ref[rows, :] = out
            return carry

        lax.fori_loop(0, nblk, step, 0)

    kvh = N_KV_HEADS
    return pl.pallas_call(
        body, name=name, grid=(dil, kvh),
        in_specs=[pl.BlockSpec((sd, gw), lambda r, h: (0, r * kvh + h)),
                  pl.BlockSpec((sd, HEAD_DIM), lambda r, h: (0, r * 2 * kvh + h)),
                  pl.BlockSpec((sd, HEAD_DIM), lambda r, h: (0, r * 2 * kvh + kvh + h))],
        out_specs=[pl.BlockSpec((sd, gw), lambda r, h: (0, r * kvh + h)),
                   pl.BlockSpec((None, sd, group), lambda r, h: (r * kvh + h, 0, 0))],
        out_shape=[jax.ShapeDtypeStruct((sd, dil * d), BF16),
                   jax.ShapeDtypeStruct((dil * kvh, sd, group), F32)],
        compiler_params=_params("parallel", "parallel"),
    )(q, kv, kv)


def _attn_bwd(name, q, kv, do, lse, delta, dil, d):
    sd = q.shape[0]
    group = d // HEAD_DIM // N_KV_HEADS
    gw = group * HEAD_DIM
    nblk = sd // ATT_BLOCK
    scale = 1.0 / math.sqrt(HEAD_DIM)
    nt = (((1,), (1,)), ((), ()))
    tn = (((0,), (0,)), ((), ()))

    def body(q_ref, k_ref, v_ref, do_ref, lse_ref, dl_ref, dq_ref, dk_ref, dv_ref, dk_acc, dv_acc):
        dk_acc[...] = jnp.zeros_like(dk_acc)
        dv_acc[...] = jnp.zeros_like(dv_acc)

        def step(nb, carry):
            rows = pl.ds(pl.multiple_of(nb * ATT_BLOCK, ATT_BLOCK), ATT_BLOCK)
            qs = _stack_heads(q_ref[rows, :], group)
            dos = _stack_heads(do_ref[rows, :], group)
            ls = _stack_cols(lse_ref[rows, :], group)
            dl = _stack_cols(dl_ref[rows, :], group)
            kw = _window(k_ref, nb)
            vw = _window(v_ref, nb)
            sc = lax.dot_general(qs, kw, nt, preferred_element_type=F32) * scale
            sc = jnp.where(_band_mask(nb, group), sc, -jnp.inf)
            p = jnp.exp(sc - ls)
            dp = lax.dot_general(dos, vw, nt, preferred_element_type=F32)
            ds = (p * (dp - dl) * scale).astype(BF16)
            dq = jnp.dot(ds, kw, preferred_element_type=F32)
            dq_ref[rows, :] = _unstack_heads(dq, group).astype(BF16)
            win = pl.ds(pl.multiple_of(nb * ATT_BLOCK, ATT_BLOCK), 2 * ATT_BLOCK)
            dk_acc[win, :] += lax.dot_general(ds, qs, tn, preferred_element_type=F32)
            dv_acc[win, :] += lax.dot_general(p.astype(BF16), dos, tn, preferred_element_type=F32)
            return carry

        lax.fori_loop(0, nblk, step, 0)
        dk_ref[...] = dk_acc[ATT_BLOCK:, :]
        dv_ref[...] = dv_acc[ATT_BLOCK:, :]

    kvh = N_KV_HEADS
    qspec = pl.BlockSpec((sd, gw), lambda r, h: (0, r * kvh + h))
    sspec = pl.BlockSpec((None, sd, group), lambda r, h: (r * kvh + h, 0, 0))
    kspec = pl.BlockSpec((sd, HEAD_DIM), lambda r, h: (0, r * kvh + h))
    return pl.pallas_call(
        body, name=name, grid=(dil, kvh),
        in_specs=[qspec,
                  pl.BlockSpec((sd, HEAD_DIM), lambda r, h: (0, r * 2 * kvh + h)),
                  pl.BlockSpec((sd, HEAD_DIM), lambda r, h: (0, r * 2 * kvh + kvh + h)),
                  qspec, sspec, sspec],
        out_specs=[qspec, kspec, kspec],
        out_shape=[jax.ShapeDtypeStruct((sd, dil * d), BF16),
                   jax.ShapeDtypeStruct((sd, dil * kvh * HEAD_DIM), F32),
                   jax.ShapeDtypeStruct((sd, dil * kvh * HEAD_DIM), F32)],
        scratch_shapes=[pltpu.VMEM((sd + ATT_BLOCK, HEAD_DIM), F32), pltpu.VMEM((sd + ATT_BLOCK, HEAD_DIM), F32)],
        compiler_params=_params("parallel", "parallel"),
    )(q, kv, kv, do, lse, delta)


def _to_branch(x, dil):
    s, w = x.shape
    return x.reshape(s // dil, dil * w)


def _heads_to_branch(x, dil, group):
    s = x.shape[0]
    x = x.reshape(s // dil, dil, N_KV_HEADS, group)
    return jnp.transpose(x, (1, 2, 0, 3)).reshape(dil * N_KV_HEADS, s // dil, group)


def _heads_from_branch(x, dil, group):
    sd = x.shape[1]
    x = x.reshape(dil, N_KV_HEADS, sd, group)
    return jnp.transpose(x, (2, 0, 1, 3)).reshape(sd * dil, N_KV_HEADS * group)


def _cast_bf16(name, w, layer):
    _, r, c = w.shape
    tr = min(512, r)

    def body(w_ref, o_ref):
        o_ref[...] = w_ref[...].astype(BF16)

    return pl.pallas_call(
        body, name=name, grid=(r // tr,),
        in_specs=[pl.BlockSpec((None, tr, c), lambda i: (layer, i, 0))],
        out_specs=pl.BlockSpec((tr, c), lambda i: (i, 0)),
        out_shape=jax.ShapeDtypeStruct((r, c), BF16),
        compiler_params=_params("parallel"),
    )(w)


def _chip_sum(name, g, rh, place):
    _, r, c = g.shape
    rh2 = r // 2
    tr = min(512, rh2)
    nb = rh2 // tr

    def body(pl_ref, g_ref, rh_ref, o_ref):
        o_ref[...] = (g_ref[...].astype(F32) + rh_ref[...].astype(F32)).astype(BF16)

    return pl.pallas_call(
        body, name=name,
        grid_spec=pltpu.PrefetchScalarGridSpec(
            num_scalar_prefetch=1, grid=(N_SHARD, nb),
            in_specs=[pl.BlockSpec((None, tr, c), lambda s, i, p: (s, p[0] * nb + i, 0)),
                      pl.BlockSpec((None, tr, c), lambda s, i, p: (s, i, 0))],
            out_specs=pl.BlockSpec((None, tr, c), lambda s, i, p: (s, i, 0))),
        out_shape=jax.ShapeDtypeStruct((N_SHARD, rh2, c), BF16),
        compiler_params=_params("parallel", "parallel"),
    )(place, g, rh)


def _owner_sum(name, cs, rp, place):
    _, rh2, c = cs.shape
    tr = min(512, rh2)
    nb = rh2 // tr

    def body(pl_ref, cs_ref, r0_ref, r1_ref, r2_ref, o_ref):
        o_ref[...] = ((cs_ref[...].astype(F32) + r0_ref[...].astype(F32))
                      + (r1_ref[...].astype(F32) + r2_ref[...].astype(F32)))

    def rspec(j):
        return pl.BlockSpec((None, tr, c), lambda i, p: (j, i, 0))

    return pl.pallas_call(
        body, name=name,
        grid_spec=pltpu.PrefetchScalarGridSpec(
            num_scalar_prefetch=1, grid=(nb,),
            in_specs=[pl.BlockSpec((None, tr, c), lambda i, p: (p[1], i, 0)), rspec(0), rspec(1), rspec(2)],
            out_specs=pl.BlockSpec((tr, c), lambda i, p: (p[0] * nb + i, 0))),
        out_shape=jax.ShapeDtypeStruct((2 * rh2, c), F32),
        compiler_params=_params("parallel"),
    )(place, cs, rp, rp, rp)


def _adam_math(w, g, m, v):
    m = ADAM_B1 * m + (1.0 - ADAM_B1) * g
    v = ADAM_B2 * v + (1.0 - ADAM_B2) * (g * g)
    m_hat = m / (1.0 - ADAM_B1 ** ADAM_STEP)
    v_hat = v / (1.0 - ADAM_B2 ** ADAM_STEP)
    delta = -ADAM_LR * (m_hat / (jnp.sqrt(v_hat) + ADAM_EPS) + ADAM_WD * w)
    return delta, m, v


def _adamw(name, w, m, v, gs):
    nl, r, c = w.shape
    tr = min(256, r)
    nb = r // tr

    def body(w_ref, m_ref, v_ref, *refs):
        g_refs = refs[:nl]
        go_ref, d_ref, mo_ref, vo_ref = refs[nl:]
        for l in range(nl):
            @pl.when(pl.program_id(0) == l)
            def _(l=l):
                g = g_refs[l][...]
                delta, m_new, v_new = _adam_math(w_ref[...], g, m_ref[...], v_ref[...])
                go_ref[...] = g
                d_ref[...] = delta
                mo_ref[...] = m_new
                vo_ref[...] = v_new

    def gspec(l):
        return pl.BlockSpec((tr, c), lambda ll, i: (jnp.clip(ll * nb + i - l * nb, 0, nb - 1), 0))

    wspec = pl.BlockSpec((None, tr, c), lambda ll, i: (ll, i, 0))
    return pl.pallas_call(
        body, name=name, grid=(nl, nb),
        in_specs=[wspec] * 3 + [gspec(l) for l in range(nl)],
        out_specs=[wspec] * 4,
        out_shape=[jax.ShapeDtypeStruct((nl, r, c), F32)] * 4,
        compiler_params=_params("arbitrary", "arbitrary"),
    )(w, m, v, *gs)


def _adam_small(ws, ms, vs, gs):
    n = len(ws)

    def body(*refs):
        w_refs, m_refs, v_refs, g_refs = refs[:n], refs[n:2 * n], refs[2 * n:3 * n], refs[3 * n:4 * n]
        d_refs, mo_refs, vo_refs = refs[4 * n:5 * n], refs[5 * n:6 * n], refs[6 * n:7 * n]
        for i in range(n):
            delta, m_new, v_new = _adam_math(w_refs[i][...], g_refs[i][...], m_refs[i][...], v_refs[i][...])
            d_refs[i][...] = delta
            mo_refs[i][...] = m_new
            vo_refs[i][...] = v_new

    shapes = [jax.ShapeDtypeStruct(w.shape, F32) for w in ws]
    res = pl.pallas_call(body, name="adam_small", out_shape=shapes * 3)(*ws, *ms, *vs, *gs)
    return res[:n], res[n:2 * n], res[2 * n:]


def _pack_small(b_in, w_dw, b_dw, ln_g, ln_b, b_out):
    cin = b_in.shape[1]
    cd = b_dw.shape[1]

    def body(bi, wd, bd, lg, lb, bo, out):
        out[...] = jnp.zeros_like(out)
        out[0:1, :] = bi[...]
        out[1:2, 0:cd] = bd[...]
        out[1:2, cd:2 * cd] = lg[...]
        out[2:3, 0:cd] = lb[...]
        out[2:3, cd:2 * cd] = bo[...]
        out[8:8 + CONV_WIDTH, 0:cd] = wd[...]

    return pl.pallas_call(body, name="pack_small", out_shape=jax.ShapeDtypeStruct((8 + CONV_PAD, cin), F32))(
        b_in, w_dw, b_dw, ln_g, ln_b, b_out)


def _place():
    x, y, c = lax.axis_index("x"), lax.axis_index("y"), lax.axis_index("c")
    return x, y, c


def _other_chips(x, y):
    return [(1 - x, y), (x, 1 - y), (1 - x, 1 - y)]


def _comm_all_gather(shards, small):
    n = len(shards)

    def body(*refs):
        ins, sm_in = refs[:n], refs[n]
        outs, sm_out = refs[n + 1:2 * n + 1], refs[2 * n + 1]
        lsem, ssem, rsem = refs[2 * n + 2:]
        x, y, c = _place()
        me = 2 * x + y
        sib = (x, y, 1 - c)
        chips = _other_chips(x, y)

        def half(a, who):
            rh = shards[a].shape[0] // 2
            return pl.ds(who * rh, rh)

        local = [pltpu.make_async_copy(ins[a], outs[a].at[me], lsem.at[a]) for a in range(n)]
        local.append(pltpu.make_async_copy(sm_in, sm_out.at[me], lsem.at[n]))
        for cp in local:
            cp.start()

        def ici(a, j, shard, to):
            return pltpu.make_async_remote_copy(
                src_ref=ins[a].at[half(a, c)], dst_ref=outs[a].at[shard, half(a, c)],
                send_sem=ssem.at[6 * a + j], recv_sem=rsem.at[6 * a + j],
                device_id=to, device_id_type=MESH)

        def d2d(a, j, src_shard, who):
            return pltpu.make_async_remote_copy(
                src_ref=outs[a].at[src_shard, half(a, who)], dst_ref=outs[a].at[src_shard, half(a, who)],
                send_sem=ssem.at[6 * a + 3 + j], recv_sem=rsem.at[6 * a + 3 + j],
                device_id=sib, device_id_type=MESH)

        def small_copy(j, src_shard, to):
            return pltpu.make_async_remote_copy(
                src_ref=sm_in, dst_ref=sm_out.at[src_shard],
                send_sem=ssem.at[6 * n + j], recv_sem=rsem.at[6 * n + j],
                device_id=to, device_id_type=MESH)

        sends = []
        for j, (px, py) in enumerate(chips):
            sends.append(small_copy(j, me, (px, py, c)))
        for a in range(n):
            for j, (px, py) in enumerate(chips):
                sends.append(ici(a, j, me, (px, py, c)))
        for cp in sends:
            cp.start()
        passed = []
        for a in range(n):
            for j, (px, py) in enumerate(chips):
                ici(a, j, 2 * px + py, (x, y, c)).wait_recv()
                fw = d2d(a, j, 2 * px + py, c)
                fw.start()
                passed.append(fw)
        for a in range(n):
            for j, (px, py) in enumerate(chips):
                d2d(a, j, 2 * px + py, 1 - c).wait_recv()
        for j, (px, py) in enumerate(chips):
            small_copy(j, 2 * px + py, (x, y, c)).wait_recv()
        for cp in sends + passed:
            cp.wait_send()
        for cp in local:
            cp.wait()

    out_shape = [jax.ShapeDtypeStruct((N_SHARD,) + s.shape, s.dtype) for s in shards]
    out_shape.append(jax.ShapeDtypeStruct((N_SHARD,) + small.shape, small.dtype))
    return pl.pallas_call(
        body, name="comm_all_gather",
        in_specs=[ANY] * (n + 1), out_specs=[ANY] * (n + 1), out_shape=out_shape,
        scratch_shapes=[pltpu.SemaphoreType.DMA((n + 1,)), pltpu.SemaphoreType.DMA((6 * n + 3,)),
                        pltpu.SemaphoreType.DMA((6 * n + 3,))],
    )(*shards, small)


def _comm_sibling_halves(grads):
    n = len(grads)

    def body(*refs):
        ins, outs = refs[:n], refs[n:2 * n]
        ssem, rsem = refs[2 * n:]
        x, y, c = _place()
        cps = []
        for a in range(n):
            rh = grads[a].shape[1] // 2
            cps.append(pltpu.make_async_remote_copy(
                src_ref=ins[a].at[:, pl.ds((1 - c) * rh, rh), :], dst_ref=outs[a],
                send_sem=ssem.at[a], recv_sem=rsem.at[a], device_id=(x, y, 1 - c), device_id_type=MESH))
        for cp in cps:
            cp.start()
        for cp in cps:
            cp.wait()

    return pl.pallas_call(
        body, name="comm_sibling_halves",
        in_specs=[ANY] * n, out_specs=[ANY] * n,
        out_shape=[jax.ShapeDtypeStruct((N_SHARD, g.shape[1] // 2, g.shape[2]), g.dtype) for g in grads],
        scratch_shapes=[pltpu.SemaphoreType.DMA((n,)), pltpu.SemaphoreType.DMA((n,))],
    )(*grads)


def _comm_to_owners(sums):
    n = len(sums)

    def body(*refs):
        ins, outs = refs[:n], refs[n:2 * n]
        ssem, rsem = refs[2 * n:]
        x, y, c = _place()
        cps = []
        for a in range(n):
            for j, (px, py) in enumerate(_other_chips(x, y)):
                cps.append(pltpu.make_async_remote_copy(
                    src_ref=ins[a].at[2 * px + py], dst_ref=outs[a].at[j],
                    send_sem=ssem.at[3 * a + j], recv_sem=rsem.at[3 * a + j],
                    device_id=(px, py, c), device_id_type=MESH))
        for cp in cps:
            cp.start()
        for cp in cps:
            cp.wait()

    return pl.pallas_call(
        body, name="comm_to_owners",
        in_specs=[ANY] * n, out_specs=[ANY] * n,
        out_shape=[jax.ShapeDtypeStruct((3,) + s.shape[1:], s.dtype) for s in sums],
        scratch_shapes=[pltpu.SemaphoreType.DMA((3 * n,)), pltpu.SemaphoreType.DMA((3 * n,))],
    )(*sums)


def _comm_swap_halves(fulls):
    n = len(fulls)

    def body(*refs):
        ins, outs = refs[:n], refs[n:2 * n]
        ssem, rsem = refs[2 * n:]
        x, y, c = _place()
        cps = []
        for a in range(n):
            rh = fulls[a].shape[0] // 2
            mine = pl.ds(c * rh, rh)
            cps.append(pltpu.make_async_remote_copy(
                src_ref=ins[a].at[mine], dst_ref=outs[a].at[mine],
                send_sem=ssem.at[a], recv_sem=rsem.at[a], device_id=(x, y, 1 - c), device_id_type=MESH))
        for cp in cps:
            cp.start()
        for a in range(n):
            rh = fulls[a].shape[0] // 2
            theirs = pl.ds((1 - c) * rh, rh)
            pltpu.make_async_remote_copy(
                src_ref=ins[a].at[theirs], dst_ref=outs[a].at[theirs],
                send_sem=ssem.at[a], recv_sem=rsem.at[a], device_id=(x, y, 1 - c), device_id_type=MESH).wait_recv()
        for cp in cps:
            cp.wait_send()

    return pl.pallas_call(
        body, name="comm_swap_halves",
        in_specs=[ANY] * n, out_specs=[ANY] * n,
        out_shape=[jax.ShapeDtypeStruct(f.shape, f.dtype) for f in fulls],
        input_output_aliases={a: a for a in range(n)},
        scratch_shapes=[pltpu.SemaphoreType.DMA((n,)), pltpu.SemaphoreType.DMA((n,))],
    )(*fulls)


def _comm_small_allreduce(rows, w_dw_grad, d):
    n = len(rows)
    loss_row = 12

    def body(*refs):
        vec_refs = refs[:n]
        wd_ref, out_ref, pack, slots, ssem, rsem = refs[n:]
        x, y, c = _place()
        me = 4 * x + 2 * y + c
        pack[...] = jnp.zeros_like(pack)
        for (r, _), ref in zip(rows, vec_refs):
            pack[r:r + 1, :] = ref[...]
        pack[16:16 + CONV_PAD, :] = wd_ref[...]
        slots[me] = pack[...]
        cps = []
        for rel in range(1, N_DEV):
            dx, dy, dc = (rel >> 2) & 1, (rel >> 1) & 1, rel & 1
            peer = (1 - x if dx else x, 1 - y if dy else y, 1 - c if dc else c)
            cps.append(pltpu.make_async_remote_copy(
                src_ref=pack, dst_ref=slots.at[me], send_sem=ssem.at[rel - 1], recv_sem=rsem.at[rel - 1],
                device_id=peer, device_id_type=MESH))
        for cp in cps:
            cp.start()
        for cp in cps:
            cp.wait()
        tot = slots[0]
        for i in range(1, N_DEV):
            tot = tot + slots[i]
        out_ref[...] = tot
        out_ref[loss_row:loss_row + 1, :] = jnp.zeros((1, d), F32) + jnp.sum(tot[loss_row:loss_row + 1, :])

    return pl.pallas_call(
        body, name="comm_small_allreduce",
        out_shape=jax.ShapeDtypeStruct((SMALL_ROWS, d), F32),
        scratch_shapes=[pltpu.VMEM((SMALL_ROWS, d), F32), pltpu.VMEM((N_DEV, SMALL_ROWS, d), F32),
                        pltpu.SemaphoreType.DMA((N_DEV - 1,)), pltpu.SemaphoreType.DMA((N_DEV - 1,))],
    )(*[v for _, v in rows], w_dw_grad)


def kernel(x, norm_mix, norm_mlp, conv_w_in, conv_b_in, conv_w_dw, conv_b_dw, conv_ln_g, conv_ln_b, conv_w_out, conv_b_out, kv_norm, w_kv, attn_w_q, attn_w_o, mlp_w_in, mlp_w_out, final_norm, loss_target, m_norm_mix, m_norm_mlp, m_conv_w_in, m_conv_b_in, m_conv_w_dw, m_conv_b_dw, m_conv_ln_g, m_conv_ln_b, m_conv_w_out, m_conv_b_out, m_kv_norm, m_w_kv, m_attn_w_q, m_attn_w_o, m_mlp_w_in, m_mlp_w_out, m_final_norm, v_norm_mix, v_norm_mlp, v_conv_w_in, v_conv_b_in, v_conv_w_dw, v_conv_b_dw, v_conv_ln_g, v_conv_ln_b, v_conv_w_out, v_conv_b_out, v_kv_norm, v_w_kv, v_attn_w_q, v_attn_w_o, v_mlp_w_in, v_mlp_w_out, v_final_norm):
    _, s, d = x.shape
    dff = mlp_w_in.shape[2] * N_SHARD
    kvw = w_kv.shape[1]
    nh = d // HEAD_DIM
    group = nh // N_KV_HEADS
    ds4 = d // N_SHARD
    xi, yi, ci = _place()
    me = 2 * xi + yi
    place = jnp.stack([ci, me]).astype(I32)

    h0 = x.reshape(s, d)
    target = loss_target.reshape(s, d)
    tabs = _rope_tables(s)

    shards = [
        _cast_bf16("cast_w_in", conv_w_in, 0),
        _cast_bf16("cast_w_out", conv_w_out, 0),
        _cast_bf16("cast_mlp_in0", mlp_w_in, 0),
        _cast_bf16("cast_mlp_out0", mlp_w_out, 0),
        _cast_bf16("cast_w_kv", w_kv.reshape(1, ds4, kvw), 0),
        _cast_bf16("cast_w_q", attn_w_q, 0),
        _cast_bf16("cast_w_o", attn_w_o, 0),
        _cast_bf16("cast_mlp_in1", mlp_w_in, 1),
        _cast_bf16("cast_mlp_out1", mlp_w_out, 1),
    ]
    small = _pack_small(conv_b_in, conv_w_dw.reshape(CONV_WIDTH, ds4), conv_b_dw, conv_ln_g, conv_ln_b, conv_b_out)
    gathered = _comm_all_gather(shards, small)
    w_in_g, w_out_g, wmi0_g, wmo0_g, wkv_g, wq_g, wo_g, wmi1_g, wmo1_g, small_g = gathered
    w_out_f = w_out_g.reshape(d, d)
    wmo_f = [wmo0_g.reshape(dff, d), wmo1_g.reshape(dff, d)]
    wmi_g = [wmi0_g, wmi1_g]
    wkv_f = wkv_g.reshape(d, kvw)
    wq_f = wq_g.reshape(d, d)
    wo_f = wo_g.reshape(d, d)
    b_in_f = small_g[:, 0, :].reshape(1, 2 * d)
    b_dw_f = small_g[:, 1, 0:ds4].reshape(1, d)
    ln_g_f = small_g[:, 1, ds4:2 * ds4].reshape(1, d)
    ln_b_f = small_g[:, 2, 0:ds4].reshape(1, d)
    b_out_f = small_g[:, 2, ds4:2 * ds4].reshape(1, d)
    w_dw_f = jnp.transpose(small_g[:, 8:8 + CONV_PAD, 0:ds4], (1, 0, 2)).reshape(CONV_PAD, d)

    nm = [norm_mix[0:1], norm_mix[1:2]]
    nmlp = [norm_mlp[0:1], norm_mlp[1:2]]
    kvn = kv_norm.reshape(1, d)
    fin = final_norm.reshape(1, d)

    def ep_bias(acc, ex, outs, j):
        outs[0][...] = (acc + ex[0][...]).astype(outs[0].dtype)

    def ep_residual(acc, ex, outs, j):
        outs[0][...] = ex[0][...] + acc

    def ep_residual_bias(acc, ex, outs, j):
        outs[0][...] = ex[0][...] + (acc + ex[1][...])

    def ep_relu2(acc, ex, outs, j):
        r = jnp.maximum(acc, 0.0)
        outs[0][...] = r.astype(BF16)
        outs[1][...] = (r * r).astype(BF16)

    def ep_rope(acc, ex, outs, j):
        outs[0][...] = _rope_apply(acc, ex[0][...], ex[1][...], ex[2][...], 1.0).astype(BF16)

    def ep_rope_k(acc, ex, outs, j):
        roped = _rope_apply(acc, ex[0][...], ex[1][...], ex[2][...], 1.0)
        outs[0][...] = jnp.where(j == 0, roped, acc).astype(BF16)

    tab_extras = [(t, "rows") for t in tabs]

    def mlp_fwd(idx, h, y):
        r, r2 = _matmul(f"mlp_in{idx}", "nn", y, wmi_g[idx], b_kind="col", m=s, n=dff, k=d,
                        tm=1024, tn=1024, tk=512, outs=[(BF16, "plain"), (BF16, "plain")], epilogue=ep_relu2)
        (h_new,) = _matmul(f"mlp_out{idx}", "nn", r2, wmo_f[idx], m=s, n=d, k=dff, tm=1024, tn=1024, tk=512,
                           outs=[(F32, "plain")], extras=[(h, "ij")], epilogue=ep_residual)
        return h_new, r, r2

    (y0,) = _rms_fwd("rms_mix0", h0, [nm[0]])
    tn_u = min(1024, 2 * d // N_SHARD)
    (u,) = _matmul("conv_in", "nn", y0, w_in_g, b_kind="col", m=s, n=2 * d, k=d, tm=1024, tn=tn_u, tk=512,
                   outs=[(BF16, "plain")], extras=[(b_in_f, "vec")], epilogue=ep_bias)
    cpre = _dwconv_fwd(u, w_dw_f, b_dw_f)
    sact = _ln_silu_fwd(cpre, ln_g_f, ln_b_f)
    (h1,) = _matmul("conv_out", "nn", sact, w_out_f, m=s, n=d, k=d, tm=1024, tn=1024, tk=512,
                    outs=[(F32, "plain")], extras=[(h0, "ij"), (b_out_f, "vec")], epilogue=ep_residual_bias)
    (y1,) = _rms_fwd("rms_mlp0", h1, [nmlp[0]])
    h2, r0, r0sq = mlp_fwd(0, h1, y1)
    ykv, y2 = _rms_fwd("rms_kv_mix1", h2, [kvn, nm[1]])
    (kv,) = _matmul("kv_proj", "nn", ykv, wkv_f, m=s, n=kvw, k=d, tm=1024, tn=kvw // 2, tk=512,
                    outs=[(BF16, "plain")], extras=tab_extras, epilogue=ep_rope_k)
    (q,) = _matmul("q_proj", "nn", y2, wq_f, m=s, n=d, k=d, tm=1024, tn=1024, tk=512,
                   outs=[(BF16, "plain")], extras=tab_extras, epilogue=ep_rope)
    o_parts, lse_parts = [], []
    for dil in DILATIONS:
        o_b, lse_b = _attn_fwd(f"attn_fwd_d{dil}", _to_branch(q, dil), _to_branch(kv, dil), dil, d)
        o_parts.append(o_b.reshape(s, d))
        lse_parts.append(_heads_from_branch(lse_b, dil, group))
    o, lse = _attn_combine(o_parts, lse_parts)
    (h3,) = _matmul("attn_out", "nn", o, wo_f, m=s, n=d, k=d, tm=1024, tn=1024, tk=512,
                    outs=[(F32, "plain")], extras=[(h2, "ij")], epilogue=ep_residual)
    (y3,) = _rms_fwd("rms_mlp1", h3, [nmlp[1]])
    h4, r1, r1sq = mlp_fwd(1, h3, y3)
    dh4, dh4b, d_fin, loss_cols = _final_loss(h4, fin, target)

    def ep_relu2_bwd(acc, ex, outs, j):
        outs[0][...] = (acc * (2.0 * ex[0][...].astype(F32))).astype(BF16)

    def mlp_bwd(idx, dhb, y, r, r2):
        (dz,) = _matmul(f"mlp_out{idx}_dx", "nt", dhb, wmo_f[idx], m=s, n=dff, k=d, tm=1024, tn=1024, tk=512,
                        outs=[(BF16, "plain")], extras=[(r, "ij")], epilogue=ep_relu2_bwd)
        (dwo,) = _matmul(f"mlp_out{idx}_dw", "tn", r2, dhb, m=dff, n=d, k=s, tm=1024, tn=1024, tk=512,
                         outs=[(BF16, "plain")])
        (dy,) = _matmul(f"mlp_in{idx}_dx", "nt", dz, wmi_g[idx], b_kind="col", m=s, n=d, k=dff,
                        tm=1024, tn=1024, tk=512, outs=[(BF16, "plain")])
        (dwi,) = _matmul(f"mlp_in{idx}_dw", "tn", y, dz, m=d, n=dff, k=s, tm=1024, tn=1024, tk=512,
                         outs=[(BF16, "col")])
        return dy, dwi, dwo.reshape(N_SHARD, dff // N_SHARD, d)

    dy3, g_wmi1, g_wmo1 = mlp_bwd(1, dh4b, y3, r1, r1sq)
    dh3, dh3b, d_nmlp1 = _rms_bwd("rms_mlp1_bwd", h3, [(nmlp[1], dy3)], dh4)

    (do,) = _matmul("attn_out_dx", "nt", dh3b, wo_f, m=s, n=d, k=d, tm=1024, tn=1024, tk=512, outs=[(BF16, "plain")])
    (g_wo,) = _matmul("attn_out_dw", "tn", o, dh3b, m=d, n=d, k=s, tm=1024, tn=1024, tk=512, outs=[(BF16, "plain")])
    delta = _attn_delta(do, o)
    dq_parts, dk_parts, dv_parts = [], [], []
    for dil in DILATIONS:
        dq_b, dk_b, dv_b = _attn_bwd(
            f"attn_bwd_d{dil}", _to_branch(q, dil), _to_branch(kv, dil), _to_branch(do, dil),
            _heads_to_branch(lse, dil, group), _heads_to_branch(delta, dil, group), dil, d)
        dq_parts.append(dq_b.reshape(s, d))
        dk_parts.append(dk_b.reshape(s, kvw // 2))
        dv_parts.append(dv_b.reshape(s, kvw // 2))
    dq = _rope_bwd_sum("rope_bwd_q", dq_parts, tabs, d)
    dkv_parts = [jnp.concatenate([a, b], axis=1) for a, b in zip(dk_parts, dv_parts)]
    dkv = _rope_bwd_sum("rope_bwd_kv", dkv_parts, tabs, kvw // 2)
    (g_wq,) = _matmul("q_proj_dw", "tn", y2, dq, m=d, n=d, k=s, tm=1024, tn=1024, tk=512, outs=[(BF16, "plain")])
    (dy2,) = _matmul("q_proj_dx", "nt", dq, wq_f, m=s, n=d, k=d, tm=1024, tn=1024, tk=512, outs=[(BF16, "plain")])
    (g_wkv,) = _matmul("kv_proj_dw", "tn", ykv, dkv, m=d, n=kvw, k=s, tm=1024, tn=1024, tk=512, outs=[(BF16, "plain")])
    (dykv,) = _matmul("kv_proj_dx", "nt", dkv, wkv_f, m=s, n=d, k=kvw, tm=1024, tn=1024, tk=512, outs=[(BF16, "plain")])
    dh2, dh2b, d_nm1, d_kvn = _rms_bwd("rms_kv_mix1_bwd", h2, [(nm[1], dy2), (kvn, dykv)], dh3)

    dy1, g_wmi0, g_wmo0 = mlp_bwd(0, dh2b, y1, r0, r0sq)
    dh1, dh1b, d_nmlp0, d_b_out = _rms_bwd("rms_mlp0_bwd", h1, [(nmlp[0], dy1)], dh2, want_colsum=True)

    (dsact,) = _matmul("conv_out_dx", "nt", dh1b, w_out_f, m=s, n=d, k=d, tm=1024, tn=1024, tk=512, outs=[(BF16, "plain")])
    (g_wout,) = _matmul("conv_out_dw", "tn", sact, dh1b, m=d, n=d, k=s, tm=1024, tn=1024, tk=512, outs=[(BF16, "plain")])
    dc, d_ln_g, d_ln_b, d_b_dw = _ln_silu_bwd(cpre, ln_g_f, ln_b_f, dsact)
    da, dgt, d_w_dw, d_b_in_a, d_b_in_g = _dwconv_bwd(u, w_dw_f, dc)
    du = jnp.concatenate([da, dgt], axis=1)
    (g_win,) = _matmul("conv_in_dw", "tn", y0, du, m=d, n=2 * d, k=s, tm=1024, tn=tn_u, tk=512, outs=[(BF16, "col")])
    (dy0,) = _matmul("conv_in_dx", "nt", du, w_in_g, b_kind="col", m=s, n=d, k=2 * d, tm=1024, tn=1024, tk=tn_u,
                     outs=[(BF16, "plain")])
    dx, _, d_nm0 = _rms_bwd("rms_mix0_bwd", h0, [(nm[0], dy0)], dh1)

    grads = [g_win,
             g_wout.reshape(N_SHARD, ds4, d),
             g_wmi0, g_wmo0,
             g_wkv.reshape(N_SHARD, ds4, kvw),
             g_wq.reshape(N_SHARD, ds4, d),
             g_wo.reshape(N_SHARD, ds4, d),
             g_wmi1, g_wmo1]
    names = ["w_in", "w_out", "mlp_in0", "mlp_out0", "w_kv", "w_q", "w_o", "mlp_in1", "mlp_out1"]
    recv_half = _comm_sibling_halves(grads)
    chip_sums = [_chip_sum(f"chip_sum_{nme}", g, rh, place) for nme, g, rh in zip(names, grads, recv_half)]
    recv_own = _comm_to_owners(chip_sums)
    own = [_owner_sum(f"owner_sum_{nme}", cs, rp, place) for nme, cs, rp in zip(names, chip_sums, recv_own)]
    full = _comm_swap_halves(own)
    f_win, f_wout, f_wmi0, f_wmo0, f_wkv, f_wq, f_wo, f_wmi1, f_wmo1 = full

    small_rows = [(0, d_nm0), (1, d_nm1), (2, d_nmlp0), (3, d_nmlp1), (4, d_kvn), (5, d_fin), (6, d_b_dw),
                  (7, d_ln_g), (8, d_ln_b), (9, d_b_out), (10, d_b_in_a), (11, d_b_in_g), (12, loss_cols)]
    red = _comm_small_allreduce(small_rows, d_w_dw, d)
    loss = red[12, 0]
    g_norm_mix = red[0:2]
    g_norm_mlp = red[2:4]
    g_kv_norm = red[4:5]
    g_final = red[5:6]

    def my_cols(row):
        return lax.dynamic_slice(red, (row, me * ds4), (1, ds4))

    g_b_dw, g_ln_g, g_ln_b, g_b_out = my_cols(6), my_cols(7), my_cols(8), my_cols(9)
    half_in = 2 * d // N_SHARD
    b_in_row = 10 + me // 2
    g_b_in = lax.dynamic_slice(red, (b_in_row, (me % 2) * half_in), (1, half_in))
    g_w_dw = lax.dynamic_slice(red, (16, me * ds4), (CONV_WIDTH, ds4))

    def big(name, w, m, v, gs):
        shape = w.shape
        w3, m3, v3 = [t.reshape((-1,) + shape[-2:]) for t in (w, m, v)]
        res = _adamw(name, w3, m3, v3, gs)
        return [t.reshape(shape) for t in res]

    r_win = big("adam_w_in", conv_w_in, m_conv_w_in, v_conv_w_in, [f_win])
    r_wout = big("adam_w_out", conv_w_out, m_conv_w_out, v_conv_w_out, [f_wout])
    r_wkv = big("adam_w_kv", w_kv, m_w_kv, v_w_kv, [f_wkv])
    r_wq = big("adam_w_q", attn_w_q, m_attn_w_q, v_attn_w_q, [f_wq])
    r_wo = big("adam_w_o", attn_w_o, m_attn_w_o, v_attn_w_o, [f_wo])
    r_wmi = big("adam_mlp_in", mlp_w_in, m_mlp_w_in, v_mlp_w_in, [f_wmi0, f_wmi1])
    r_wmo = big("adam_mlp_out", mlp_w_out, m_mlp_w_out, v_mlp_w_out, [f_wmo0, f_wmo1])

    sm_w = [norm_mix, norm_mlp, conv_b_in, conv_w_dw.reshape(CONV_WIDTH, ds4), conv_b_dw, conv_ln_g, conv_ln_b,
            conv_b_out, kv_norm.reshape(1, d), final_norm.reshape(1, d)]
    sm_m = [m_norm_mix, m_norm_mlp, m_conv_b_in, m_conv_w_dw.reshape(CONV_WIDTH, ds4), m_conv_b_dw, m_conv_ln_g,
            m_conv_ln_b, m_conv_b_out, m_kv_norm.reshape(1, d), m_final_norm.reshape(1, d)]
    sm_v = [v_norm_mix, v_norm_mlp, v_conv_b_in, v_conv_w_dw.reshape(CONV_WIDTH, ds4), v_conv_b_dw, v_conv_ln_g,
            v_conv_ln_b, v_conv_b_out, v_kv_norm.reshape(1, d), v_final_norm.reshape(1, d)]
    sm_g = [g_norm_mix, g_norm_mlp, g_b_in, g_w_dw, g_b_dw, g_ln_g, g_ln_b, g_b_out, g_kv_norm, g_final]
    sm_d, sm_nm, sm_nv = _adam_small(sm_w, sm_m, sm_v, sm_g)
    shapes = [norm_mix.shape, norm_mlp.shape, conv_b_in.shape, conv_w_dw.shape, conv_b_dw.shape, conv_ln_g.shape,
              conv_ln_b.shape, conv_b_out.shape, kv_norm.shape, final_norm.shape]
    sm_g, sm_d, sm_nm, sm_nv = [[t.reshape(sh) for t, sh in zip(lst, shapes)] for lst in (sm_g, sm_d, sm_nm, sm_nv)]

    def order(sm, idx):
        return [sm[0], sm[1], r_win[idx], sm[2], sm[3], sm[4], sm[5], sm[6], r_wout[idx], sm[7], sm[8],
                r_wkv[idx], r_wq[idx], r_wo[idx], r_wmi[idx], r_wmo[idx], sm[9]]

    return (loss, dx.reshape(x.shape), *order(sm_g, 0), *order(sm_d, 1), *order(sm_nm, 2), *order(sm_nv, 3))
```

```python
import math

import jax
import jax.numpy as jnp
from jax import lax
from jax.experimental import pallas as pl
from jax.experimental.pallas import tpu as pltpu

F32 = jnp.float32
BF16 = jnp.bfloat16
I32 = jnp.int32

NORM_EPS = 1e-6
LN_EPS = 1e-5
HEAD_DIM = 128
N_KV_HEADS = 4
ROT_DIM = 32
ROPE_THETA = 500000.0
CONV_WIDTH = 31
CONV_PAD = 32
ATT_BLOCK = 128
DILATIONS = (1, 4, 16)
ADAM_LR = 0.001
ADAM_B1 = 0.9
ADAM_B2 = 0.999
ADAM_EPS = 1e-08
ADAM_WD = 0.01
ADAM_STEP = 10
N_SHARD = 4
N_DEV = 8
LANES = 128
VMEM_LIMIT = 48 * 1024 * 1024
ROW_TILE = 256
CONV_CB = 128
CONV_T = 128
SMALL_ROWS = 48
MESH = pl.DeviceIdType.MESH
ANY = pl.BlockSpec(memory_space=pl.ANY)
HBM = pl.BlockSpec(memory_space=pltpu.HBM)
SEM = pl.BlockSpec(memory_space=pltpu.SEMAPHORE)
SPLIT_EFFECT = pltpu.SideEffectType.DATAFLOW_SIDE_EFFECTING


def _params(*sem):
    return pltpu.CompilerParams(dimension_semantics=sem, vmem_limit_bytes=VMEM_LIMIT)


def _sigmoid(x):
    return 1.0 / (1.0 + jnp.exp(-x))


def _wspec(kind, arr_shape, br, bc, pick):
    if kind == "plain":
        return pl.BlockSpec((br, bc), pick)
    per = arr_shape[2] // bc

    def idx(*g):
        rb, cb = pick(*g)
        return (cb // per, rb, cb % per)

    return pl.BlockSpec((None, br, bc), idx)


def _matmul(name, mode, a, b, *, m, n, k, tm, tn, tk, b_kind="plain", outs, extras=(), epilogue=None):
    tm, tn, tk = min(tm, m), min(tn, n), min(tk, k)
    if b_kind == "col" and mode == "nn":
        tn = min(tn, n // N_SHARD)
    if b_kind == "col" and mode == "nt":
        tk = min(tk, k // N_SHARD)
    if any(kind == "col" for _, kind in outs):
        tn = min(tn, n // N_SHARD)
    assert m % tm == 0 and n % tn == 0 and k % tk == 0, (name, m, n, k, tm, tn, tk)
    nk = k // tk
    grid = (m // tm, n // tn, nk)
    if mode == "nn":
        a_spec = pl.BlockSpec((tm, tk), lambda i, j, kk: (i, kk))
        b_spec = _wspec(b_kind, b.shape, tk, tn, lambda i, j, kk: (kk, j))
        dims = (((1,), (0,)), ((), ()))
    elif mode == "nt":
        a_spec = pl.BlockSpec((tm, tk), lambda i, j, kk: (i, kk))
        b_spec = _wspec(b_kind, b.shape, tn, tk, lambda i, j, kk: (j, kk))
        dims = (((1,), (1,)), ((), ()))
    else:
        a_spec = pl.BlockSpec((tk, tm), lambda i, j, kk: (kk, i))
        b_spec = pl.BlockSpec((tk, tn), lambda i, j, kk: (kk, j))
        dims = (((0,), (0,)), ((), ()))
    out_shape, out_specs = [], []
    for dtype, kind in outs:
        shape = (m, n) if kind == "plain" else (N_SHARD, m, n // N_SHARD)
        out_shape.append(jax.ShapeDtypeStruct(shape, dtype))
        out_specs.append(_wspec(kind, shape, tm, tn, lambda i, j, kk: (i, j)))
    n_ex = len(extras)
    ex_specs = {"ij": pl.BlockSpec((tm, tn), lambda i, j, kk: (i, j)),
                "vec": pl.BlockSpec((1, tn), lambda i, j, kk: (0, j)),
                "rows": pl.BlockSpec((tm, LANES), lambda i, j, kk: (i, 0))}

    def body(*refs):
        a_ref, b_ref = refs[0], refs[1]
        ex_refs = refs[2:2 + n_ex]
        out_refs = refs[2 + n_ex:-1]
        acc_ref = refs[-1]
        j = pl.program_id(1)
        kk = pl.program_id(2)

        @pl.when(kk == 0)
        def _():
            acc_ref[...] = jnp.zeros_like(acc_ref)

        acc_ref[...] += lax.dot_general(a_ref[...], b_ref[...], dims, preferred_element_type=F32)

        @pl.when(kk == nk - 1)
        def _():
            if epilogue is None:
                out_refs[0][...] = acc_ref[...].astype(out_refs[0].dtype)
            else:
                epilogue(acc_ref[...], ex_refs, out_refs, j)

    res = pl.pallas_call(
        body, name=name, grid=grid,
        in_specs=[a_spec, b_spec] + [ex_specs[how] for _, how in extras],
        out_specs=out_specs, out_shape=out_shape,
        scratch_shapes=[pltpu.VMEM((tm, tn), F32)],
        compiler_params=_params("parallel", "parallel", "arbitrary"),
    )(a, b, *[e for e, _ in extras])
    return res


def _rope_tables(seq):
    half = ROT_DIM // 2
    pos = jnp.arange(seq, dtype=F32)
    inv = ROPE_THETA ** (-jnp.arange(0, ROT_DIM, 2, dtype=F32) / ROT_DIM)
    ang = pos[:, None] * inv[None, :]
    cos, sin = jnp.cos(ang), jnp.sin(ang)
    zeros = jnp.zeros((seq, HEAD_DIM - ROT_DIM), F32)
    ctab = jnp.concatenate([cos, cos, zeros + 1.0], axis=1)
    atab = jnp.concatenate([-sin, jnp.zeros((seq, half), F32), zeros], axis=1)
    btab = jnp.concatenate([jnp.zeros((seq, half), F32), sin, zeros], axis=1)
    return ctab, atab, btab


def _rope_apply(x, ctab, atab, btab, sign):
    w = x.shape[1]
    reps = w // HEAD_DIM
    half = ROT_DIM // 2
    c = jnp.tile(ctab, (1, reps))
    a = jnp.tile(atab, (1, reps))
    b = jnp.tile(btab, (1, reps))
    up = pltpu.roll(x, w - half, 1)
    down = pltpu.roll(x, half, 1)
    return x * c + sign * (up * a + down * b)


def _rows(t, w):
    return pl.BlockSpec((t, w), lambda i: (i, 0))


def _fixed(shape):
    nd = len(shape)
    return pl.BlockSpec(shape, lambda i: (0,) * nd)


def _rms_fwd(name, x, gains):
    s, d = x.shape
    t = min(ROW_TILE, s)
    ng = len(gains)

    def body(x_ref, *refs):
        xv = x_ref[...]
        r = lax.rsqrt(jnp.mean(xv * xv, axis=-1, keepdims=True) + NORM_EPS)
        xn = xv * r
        for g_ref, y_ref in zip(refs[:ng], refs[ng:]):
            y_ref[...] = (xn * g_ref[...]).astype(BF16)

    return pl.pallas_call(
        body, name=name, grid=(s // t,),
        in_specs=[_rows(t, d)] + [_fixed((1, d))] * ng,
        out_specs=[_rows(t, d)] * ng,
        out_shape=[jax.ShapeDtypeStruct((s, d), BF16)] * ng,
        compiler_params=_params("parallel"),
    )(x, *gains)


def _rms_bwd(name, x, pairs, dh_in, want_colsum=False):
    s, d = x.shape
    t = min(ROW_TILE, s)
    n_p = len(pairs)

    def body(x_ref, dh_ref, *refs):
        g_refs = refs[:n_p]
        dy_refs = refs[n_p:2 * n_p]
        dh_out, dhb_out = refs[2 * n_p], refs[2 * n_p + 1]
        dg_refs = refs[2 * n_p + 2:2 * n_p + 2 + n_p]
        cs_ref = refs[-1] if want_colsum else None
        i = pl.program_id(0)
        xv = x_ref[...]
        r = lax.rsqrt(jnp.mean(xv * xv, axis=-1, keepdims=True) + NORM_EPS)
        xn = xv * r
        dh = dh_ref[...]
        for g_ref, dy_ref, dg_ref in zip(g_refs, dy_refs, dg_refs):
            dy = dy_ref[...].astype(F32)
            u = dy * g_ref[...]
            dh = dh + r * (u - xn * jnp.mean(u * xn, axis=-1, keepdims=True))
            part = jnp.sum(dy * xn, axis=0, keepdims=True)

            @pl.when(i == 0)
            def _():
                dg_ref[...] = part

            @pl.when(i > 0)
            def _():
                dg_ref[...] += part

        dh_out[...] = dh
        dhb_out[...] = dh.astype(BF16)
        if want_colsum:
            col = jnp.sum(dh, axis=0, keepdims=True)

            @pl.when(i == 0)
            def _():
                cs_ref[...] = col

            @pl.when(i > 0)
            def _():
                cs_ref[...] += col

    n_vec = n_p + (1 if want_colsum else 0)
    return pl.pallas_call(
        body, name=name, grid=(s // t,),
        in_specs=[_rows(t, d), _rows(t, d)] + [_fixed((1, d))] * n_p + [_rows(t, d)] * n_p,
        out_specs=[_rows(t, d), _rows(t, d)] + [_fixed((1, d))] * n_vec,
        out_shape=[jax.ShapeDtypeStruct((s, d), F32), jax.ShapeDtypeStruct((s, d), BF16)]
        + [jax.ShapeDtypeStruct((1, d), F32)] * n_vec,
        compiler_params=_params("arbitrary"),
    )(x, dh_in, *[g for g, _ in pairs], *[dy for _, dy in pairs])


def _final_loss(x, g, target):
    s, d = x.shape
    t = min(ROW_TILE, s)

    def body(x_ref, g_ref, t_ref, dh_out, dhb_out, dg_ref, loss_ref):
        i = pl.program_id(0)
        xv = x_ref[...]
        gv = g_ref[...]
        r = lax.rsqrt(jnp.mean(xv * xv, axis=-1, keepdims=True) + NORM_EPS)
        xn = xv * r
        diff = xn * gv - t_ref[...]
        dy = diff / d
        u = dy * gv
        dh = r * (u - xn * jnp.mean(u * xn, axis=-1, keepdims=True))
        dh_out[...] = dh
        dhb_out[...] = dh.astype(BF16)
        dg = jnp.sum(dy * xn, axis=0, keepdims=True)
        lc = jnp.sum(0.5 * diff * dy, axis=0, keepdims=True)

        @pl.when(i == 0)
        def _():
            dg_ref[...] = dg
            loss_ref[...] = lc

        @pl.when(i > 0)
        def _():
            dg_ref[...] += dg
            loss_ref[...] += lc

    return pl.pallas_call(
        body, name="final_loss", grid=(s // t,),
        in_specs=[_rows(t, d), _fixed((1, d)), _rows(t, d)],
        out_specs=[_rows(t, d), _rows(t, d), _fixed((1, d)), _fixed((1, d))],
        out_shape=[jax.ShapeDtypeStruct((s, d), F32), jax.ShapeDtypeStruct((s, d), BF16),
                   jax.ShapeDtypeStruct((1, d), F32), jax.ShapeDtypeStruct((1, d), F32)],
        compiler_params=_params("arbitrary"),
    )(x, g, target)


def _ln_silu_fwd(c, g, b):
    s, d = c.shape
    t = min(ROW_TILE, s)

    def body(c_ref, g_ref, b_ref, s_ref):
        cv = c_ref[...]
        mu = jnp.mean(cv, axis=-1, keepdims=True)
        xc = cv - mu
        rs = lax.rsqrt(jnp.mean(xc * xc, axis=-1, keepdims=True) + LN_EPS)
        ln = xc * rs * g_ref[...] + b_ref[...]
        s_ref[...] = (ln * _sigmoid(ln)).astype(BF16)

    return pl.pallas_call(
        body, name="ln_silu_fwd", grid=(s // t,),
        in_specs=[_rows(t, d), _fixed((1, d)), _fixed((1, d))],
        out_specs=_rows(t, d), out_shape=jax.ShapeDtypeStruct((s, d), BF16),
        compiler_params=_params("parallel"),
    )(c, g, b)


def _ln_silu_bwd(c, g, b, ds):
    s, d = c.shape
    t = min(ROW_TILE, s)

    def body(c_ref, g_ref, b_ref, ds_ref, dc_ref, dg_ref, db_ref, dbdw_ref):
        i = pl.program_id(0)
        cv = c_ref[...]
        gv = g_ref[...]
        mu = jnp.mean(cv, axis=-1, keepdims=True)
        xc = cv - mu
        rs = lax.rsqrt(jnp.mean(xc * xc, axis=-1, keepdims=True) + LN_EPS)
        nrm = xc * rs
        ln = nrm * gv + b_ref[...]
        sig = _sigmoid(ln)
        dln = ds_ref[...].astype(F32) * sig * (1.0 + ln * (1.0 - sig))
        dn = dln * gv
        dc = rs * (dn - jnp.mean(dn, axis=-1, keepdims=True)
                   - nrm * jnp.mean(dn * nrm, axis=-1, keepdims=True))
        dc_ref[...] = dc
        pg = jnp.sum(dln * nrm, axis=0, keepdims=True)
        pb = jnp.sum(dln, axis=0, keepdims=True)
        pc = jnp.sum(dc, axis=0, keepdims=True)

        @pl.when(i == 0)
        def _():
            dg_ref[...] = pg
            db_ref[...] = pb
            dbdw_ref[...] = pc

        @pl.when(i > 0)
        def _():
            dg_ref[...] += pg
            db_ref[...] += pb
            dbdw_ref[...] += pc

    return pl.pallas_call(
        body, name="ln_silu_bwd", grid=(s // t,),
        in_specs=[_rows(t, d), _fixed((1, d)), _fixed((1, d)), _rows(t, d)],
        out_specs=[_rows(t, d)] + [_fixed((1, d))] * 3,
        out_shape=[jax.ShapeDtypeStruct((s, d), F32)] + [jax.ShapeDtypeStruct((1, d), F32)] * 3,
        compiler_params=_params("arbitrary"),
    )(c, g, b, ds)


def _attn_combine(o_list, lse_list):
    s, d = o_list[0].shape
    nh = d // HEAD_DIM
    t = min(ROW_TILE, s)
    nb = len(o_list)

    def body(*refs):
        o_refs = refs[:nb]
        l_refs = refs[nb:2 * nb]
        o_out, l_out = refs[2 * nb], refs[2 * nb + 1]
        ls = [r[...] for r in l_refs]
        mx = ls[0]
        for l in ls[1:]:
            mx = jnp.maximum(mx, l)
        es = [jnp.exp(l - mx) for l in ls]
        den = es[0]
        for e in es[1:]:
            den = den + e
        l_out[...] = mx + jnp.log(den)
        ws = [e / den for e in es]
        for h in range(nh):
            cols = slice(h * HEAD_DIM, (h + 1) * HEAD_DIM)
            acc = jnp.zeros((t, HEAD_DIM), F32)
            for o_ref, w in zip(o_refs, ws):
                acc = acc + w[:, h:h + 1] * o_ref[:, cols].astype(F32)
            o_out[:, cols] = acc.astype(BF16)

    return pl.pallas_call(
        body, name="attn_combine", grid=(s // t,),
        in_specs=[_rows(t, d)] * nb + [_rows(t, nh)] * nb,
        out_specs=[_rows(t, d), _rows(t, nh)],
        out_shape=[jax.ShapeDtypeStruct((s, d), BF16), jax.ShapeDtypeStruct((s, nh), F32)],
        compiler_params=_params("parallel"),
    )(*o_list, *lse_list)


def _attn_delta(do, o):
    s, d = o.shape
    nh = d // HEAD_DIM
    t = min(ROW_TILE, s)

    def body(do_ref, o_ref, dl_ref):
        lane = lax.broadcasted_iota(I32, (t, nh), 1)
        out = jnp.zeros((t, nh), F32)
        for h in range(nh):
            cols = slice(h * HEAD_DIM, (h + 1) * HEAD_DIM)
            v = jnp.sum(do_ref[:, cols].astype(F32) * o_ref[:, cols].astype(F32), axis=-1, keepdims=True)
            out = jnp.where(lane == h, v, out)
        dl_ref[...] = out

    return pl.pallas_call(
        body, name="attn_delta", grid=(s // t,),
        in_specs=[_rows(t, d), _rows(t, d)],
        out_specs=_rows(t, nh), out_shape=jax.ShapeDtypeStruct((s, nh), F32),
        compiler_params=_params("parallel"),
    )(do, o)


def _rope_bwd_sum(name, parts, tabs, rope_cols):
    s, w = parts[0].shape
    t = min(ROW_TILE, s)
    n_p = len(parts)

    def body(*refs):
        p_refs = refs[:n_p]
        c_ref, a_ref, b_ref = refs[n_p:n_p + 3]
        out = refs[-1]
        tot = p_refs[0][...].astype(F32)
        for p in p_refs[1:]:
            tot = tot + p[...].astype(F32)
        rot = _rope_apply(tot[:, :rope_cols], c_ref[...], a_ref[...], b_ref[...], -1.0)
        out[:, :rope_cols] = rot.astype(BF16)
        if rope_cols < w:
            out[:, rope_cols:] = tot[:, rope_cols:].astype(BF16)

    return pl.pallas_call(
        body, name=name, grid=(s // t,),
        in_specs=[_rows(t, w)] * n_p + [_rows(t, HEAD_DIM)] * 3,
        out_specs=_rows(t, w), out_shape=jax.ShapeDtypeStruct((s, w), BF16),
        compiler_params=_params("parallel"),
    )(*parts, *tabs)


def _dwconv_fwd(u, w_dw, b_dw):
    s, d2 = u.shape
    d = d2 // 2
    cb = min(CONV_CB, d)
    nblk = d // cb
    tt = min(CONV_T, s)

    def body(ua_ref, ug_ref, w_ref, b_ref, c_ref, xp_ref):
        gl = ua_ref[...].astype(F32) * _sigmoid(ug_ref[...].astype(F32))
        xp_ref[0:CONV_PAD, :] = jnp.zeros((CONV_PAD, cb), F32)
        xp_ref[CONV_PAD:, :] = gl
        wv = w_ref[...]
        bv = b_ref[...]
        for t0 in range(0, s, tt):
            acc = jnp.zeros((tt, cb), F32) + bv
            for kk in range(CONV_WIDTH):
                off = t0 + CONV_PAD - (CONV_WIDTH - 1) + kk
                acc = acc + wv[kk:kk + 1, :] * xp_ref[off:off + tt, :]
            c_ref[t0:t0 + tt, :] = acc

    return pl.pallas_call(
        body, name="dwconv_fwd", grid=(nblk,),
        in_specs=[pl.BlockSpec((s, cb), lambda j: (0, j)), pl.BlockSpec((s, cb), lambda j: (0, j + nblk)),
                  pl.BlockSpec((CONV_PAD, cb), lambda j: (0, j)), pl.BlockSpec((1, cb), lambda j: (0, j))],
        out_specs=pl.BlockSpec((s, cb), lambda j: (0, j)),
        out_shape=jax.ShapeDtypeStruct((s, d), F32),
        scratch_shapes=[pltpu.VMEM((s + CONV_PAD, cb), F32)],
        compiler_params=_params("parallel"),
    )(u, u, w_dw, b_dw)


def _dwconv_bwd(u, w_dw, dc):
    s, d2 = u.shape
    d = d2 // 2
    cb = min(CONV_CB, d)
    nblk = d // cb
    tt = min(CONV_T, s)

    def body(ua_ref, ug_ref, w_ref, dc_ref, da_ref, dgt_ref, dw_ref, dba_ref, dbg_ref, glp_ref, dcp_ref, acc_ref):
        a = ua_ref[...].astype(F32)
        sig = _sigmoid(ug_ref[...].astype(F32))
        glp_ref[0:CONV_PAD, :] = jnp.zeros((CONV_PAD, cb), F32)
        glp_ref[CONV_PAD:, :] = a * sig
        dcp_ref[0:s, :] = dc_ref[...]
        dcp_ref[s:, :] = jnp.zeros((CONV_PAD, cb), F32)
        acc_ref[...] = jnp.zeros_like(acc_ref)
        wv = w_ref[...]
        dba = jnp.zeros((1, cb), F32)
        dbg = jnp.zeros((1, cb), F32)
        for t0 in range(0, s, tt):
            dgl = jnp.zeros((tt, cb), F32)
            dct = dc_ref[t0:t0 + tt, :]
            for kk in range(CONV_WIDTH):
                off = t0 + (CONV_WIDTH - 1) - kk
                dgl = dgl + wv[kk:kk + 1, :] * dcp_ref[off:off + tt, :]
                goff = t0 + CONV_PAD - (CONV_WIDTH - 1) + kk
                prod = dct * glp_ref[goff:goff + tt, :]
                acc_ref[8 * kk:8 * kk + 8, :] += jnp.sum(prod.reshape(tt // 8, 8, cb), axis=0)
            at = ua_ref[t0:t0 + tt, :].astype(F32)
            st = _sigmoid(ug_ref[t0:t0 + tt, :].astype(F32))
            da = dgl * st
            dg = dgl * at * st * (1.0 - st)
            da_ref[t0:t0 + tt, :] = da.astype(BF16)
            dgt_ref[t0:t0 + tt, :] = dg.astype(BF16)
            dba = dba + jnp.sum(da, axis=0, keepdims=True)
            dbg = dbg + jnp.sum(dg, axis=0, keepdims=True)
        dba_ref[...] = dba
        dbg_ref[...] = dbg
        for kk in range(CONV_WIDTH):
            dw_ref[kk:kk + 1, :] = jnp.sum(acc_ref[8 * kk:8 * kk + 8, :], axis=0, keepdims=True)
        dw_ref[CONV_WIDTH:, :] = jnp.zeros((CONV_PAD - CONV_WIDTH, cb), F32)

    blk = pl.BlockSpec((s, cb), lambda j: (0, j))
    vec = pl.BlockSpec((1, cb), lambda j: (0, j))
    return pl.pallas_call(
        body, name="dwconv_bwd", grid=(nblk,),
        in_specs=[blk, pl.BlockSpec((s, cb), lambda j: (0, j + nblk)),
                  pl.BlockSpec((CONV_PAD, cb), lambda j: (0, j)), blk],
        out_specs=[blk, blk, pl.BlockSpec((CONV_PAD, cb), lambda j: (0, j)), vec, vec],
        out_shape=[jax.ShapeDtypeStruct((s, d), BF16), jax.ShapeDtypeStruct((s, d), BF16),
                   jax.ShapeDtypeStruct((CONV_PAD, d), F32),
                   jax.ShapeDtypeStruct((1, d), F32), jax.ShapeDtypeStruct((1, d), F32)],
        scratch_shapes=[pltpu.VMEM((s + CONV_PAD, cb), F32), pltpu.VMEM((s + CONV_PAD, cb), F32),
                        pltpu.VMEM((8 * CONV_PAD, cb), F32)],
        compiler_params=_params("parallel"),
    )(u, u, w_dw, dc)


def _stack_heads(x, group):
    return jnp.concatenate([x[:, g * HEAD_DIM:(g + 1) * HEAD_DIM] for g in range(group)], axis=0)


def _unstack_heads(x, group):
    return jnp.concatenate([x[g * ATT_BLOCK:(g + 1) * ATT_BLOCK, :] for g in range(group)], axis=1)


def _stack_cols(x, group):
    return jnp.concatenate([x[:, g:g + 1] for g in range(group)], axis=0)


def _band_mask(nb, group):
    rows = group * ATT_BLOCK
    row = lax.broadcasted_iota(I32, (rows, 2 * ATT_BLOCK), 0) % ATT_BLOCK
    col = lax.broadcasted_iota(I32, (rows, 2 * ATT_BLOCK), 1)
    return (col >= row) & (col <= row + ATT_BLOCK) & ((col >= ATT_BLOCK) | (nb > 0))


def _window(ref, nb):
    prev = pl.multiple_of(jnp.maximum(nb - 1, 0) * ATT_BLOCK, ATT_BLOCK)
    cur = pl.multiple_of(nb * ATT_BLOCK, ATT_BLOCK)
    return jnp.concatenate([ref[pl.ds(prev, ATT_BLOCK), :], ref[pl.ds(cur, ATT_BLOCK), :]], axis=0)


def _attn_fwd(name, q, kv, dil, d):
    sd = q.shape[0]
    group = d // HEAD_DIM // N_KV_HEADS
    gw = group * HEAD_DIM
    nblk = sd // ATT_BLOCK
    scale = 1.0 / math.sqrt(HEAD_DIM)
    nt = (((1,), (1,)), ((), ()))

    def body(q_ref, k_ref, v_ref, o_ref, lse_ref):
        lane = lax.broadcasted_iota(I32, (ATT_BLOCK, group), 1)

        def step(nb, carry):
            rows = pl.ds(pl.multiple_of(nb * ATT_BLOCK, ATT_BLOCK), ATT_BLOCK)
            qs = _stack_heads(q_ref[rows, :], group)
            kw = _window(k_ref, nb)
            vw = _window(v_ref, nb)
            sc = lax.dot_general(qs, kw, nt, preferred_element_type=F32) * scale
            sc = jnp.where(_band_mask(nb, group), sc, -jnp.inf)
            mx = jnp.max(sc, axis=-1, keepdims=True)
            p = jnp.exp(sc - mx)
            l = jnp.sum(p, axis=-1, keepdims=True)
            o = jnp.dot(p.astype(BF16), vw, preferred_element_type=F32) / l
            o_ref[rows, :] = _unstack_heads(o, group).astype(BF16)
            lse = mx + jnp.log(l)
            out = jnp.zeros((ATT_BLOCK, group), F32)
            for g in range(group):
                out = jnp.where(lane == g, lse[g * ATT_BLOCK:(g + 1) * ATT_BLOCK, :], out)
            lse_ref[rows, :] = out
            return carry

        lax.fori_loop(0, nblk, step, 0)

    kvh = N_KV_HEADS
    return pl.pallas_call(
        body, name=name, grid=(dil, kvh),
        in_specs=[pl.BlockSpec((sd, gw), lambda r, h: (0, r * kvh + h)),
                  pl.BlockSpec((sd, HEAD_DIM), lambda r, h: (0, r * 2 * kvh + h)),
                  pl.BlockSpec((sd, HEAD_DIM), lambda r, h: (0, r * 2 * kvh + kvh + h))],
        out_specs=[pl.BlockSpec((sd, gw), lambda r, h: (0, r * kvh + h)),
                   pl.BlockSpec((None, sd, group), lambda r, h: (r * kvh + h, 0, 0))],
        out_shape=[jax.ShapeDtypeStruct((sd, dil * d), BF16),
                   jax.ShapeDtypeStruct((dil * kvh, sd, group), F32)],
        compiler_params=_params("parallel", "parallel"),
    )(q, kv, kv)


def _attn_bwd(name, q, kv, do, lse, delta, dil, d):
    sd = q.shape[0]
    group = d // HEAD_DIM // N_KV_HEADS
    gw = group * HEAD_DIM
    nblk = sd // ATT_BLOCK
    scale = 1.0 / math.sqrt(HEAD_DIM)
    nt = (((1,), (1,)), ((), ()))
    tn = (((0,), (0,)), ((), ()))

    def body(q_ref, k_ref, v_ref, do_ref, lse_ref, dl_ref, dq_ref, dk_ref, dv_ref, dk_acc, dv_acc):
        dk_acc[...] = jnp.zeros_like(dk_acc)
        dv_acc[...] = jnp.zeros_like(dv_acc)

        def step(nb, carry):
            rows = pl.ds(pl.multiple_of(nb * ATT_BLOCK, ATT_BLOCK), ATT_BLOCK)
            qs = _stack_heads(q_ref[rows, :], group)
            dos = _stack_heads(do_ref[rows, :], group)
            ls = _stack_cols(lse_ref[rows, :], group)
            dl = _stack_cols(dl_ref[rows, :], group)
            kw = _window(k_ref, nb)
            vw = _window(v_ref, nb)
            sc = lax.dot_general(qs, kw, nt, preferred_element_type=F32) * scale
            sc = jnp.where(_band_mask(nb, group), sc, -jnp.inf)
            p = jnp.exp(sc - ls)
            dp = lax.dot_general(dos, vw, nt, preferred_element_type=F32)
            ds = (p * (dp - dl) * scale).astype(BF16)
            dq = jnp.dot(ds, kw, preferred_element_type=F32)
            dq_ref[rows, :] = _unstack_heads(dq, group).astype(BF16)
            win = pl.ds(pl.multiple_of(nb * ATT_BLOCK, ATT_BLOCK), 2 * ATT_BLOCK)
            dk_acc[win, :] += lax.dot_general(ds, qs, tn, preferred_element_type=F32)
            dv_acc[win, :] += lax.dot_general(p.astype(BF16), dos, tn, preferred_element_type=F32)
            return carry

        lax.fori_loop(0, nblk, step, 0)
        dk_ref[...] = dk_acc[ATT_BLOCK:, :]
        dv_ref[...] = dv_acc[ATT_BLOCK:, :]

    kvh = N_KV_HEADS
    qspec = pl.BlockSpec((sd, gw), lambda r, h: (0, r * kvh + h))
    sspec = pl.BlockSpec((None, sd, group), lambda r, h: (r * kvh + h, 0, 0))
    kspec = pl.BlockSpec((sd, HEAD_DIM), lambda r, h: (0, r * kvh + h))
    return pl.pallas_call(
        body, name=name, grid=(dil, kvh),
        in_specs=[qspec,
                  pl.BlockSpec((sd, HEAD_DIM), lambda r, h: (0, r * 2 * kvh + h)),
                  pl.BlockSpec((sd, HEAD_DIM), lambda r, h: (0, r * 2 * kvh + kvh + h)),
                  qspec, sspec, sspec],
        out_specs=[qspec, kspec, kspec],
        out_shape=[jax.ShapeDtypeStruct((sd, dil * d), BF16),
                   jax.ShapeDtypeStruct((sd, dil * kvh * HEAD_DIM), F32),
                   jax.ShapeDtypeStruct((sd, dil * kvh * HEAD_DIM), F32)],
        scratch_shapes=[pltpu.VMEM((sd + ATT_BLOCK, HEAD_DIM), F32), pltpu.VMEM((sd + ATT_BLOCK, HEAD_DIM), F32)],
        compiler_params=_params("parallel", "parallel"),
    )(q, kv, kv, do, lse, delta)


def _to_branch(x, dil):
    s, w = x.shape
    return x.reshape(s // dil, dil * w)


def _heads_to_branch(x, dil, group):
    s = x.shape[0]
    x = x.reshape(s // dil, dil, N_KV_HEADS, group)
    return jnp.transpose(x, (1, 2, 0, 3)).reshape(dil * N_KV_HEADS, s // dil, group)


def _heads_from_branch(x, dil, group):
    sd = x.shape[1]
    x = x.reshape(dil, N_KV_HEADS, sd, group)
    return jnp.transpose(x, (2, 0, 1, 3)).reshape(sd * dil, N_KV_HEADS * group)


def _cast_bf16(name, w, layer):
    _, r, c = w.shape
    tr = min(512, r)

    def body(w_ref, o_ref):
        o_ref[...] = w_ref[...].astype(BF16)

    return pl.pallas_call(
        body, name=name, grid=(r // tr,),
        in_specs=[pl.BlockSpec((None, tr, c), lambda i: (layer, i, 0))],
        out_specs=pl.BlockSpec((tr, c), lambda i: (i, 0)),
        out_shape=jax.ShapeDtypeStruct((r, c), BF16),
        compiler_params=_params("parallel"),
    )(w)


def _chip_sum(name, g, rh, place):
    _, r, c = g.shape
    rh2 = r // 2
    tr = min(512, rh2)
    nb = rh2 // tr

    def body(pl_ref, g_ref, rh_ref, o_ref):
        o_ref[...] = (g_ref[...].astype(F32) + rh_ref[...].astype(F32)).astype(BF16)

    return pl.pallas_call(
        body, name=name,
        grid_spec=pltpu.PrefetchScalarGridSpec(
            num_scalar_prefetch=1, grid=(N_SHARD, nb),
            in_specs=[pl.BlockSpec((None, tr, c), lambda s, i, p: (s, p[0] * nb + i, 0)),
                      pl.BlockSpec((None, tr, c), lambda s, i, p: (s, i, 0))],
            out_specs=pl.BlockSpec((None, tr, c), lambda s, i, p: (s, i, 0))),
        out_shape=jax.ShapeDtypeStruct((N_SHARD, rh2, c), BF16),
        compiler_params=_params("parallel", "parallel"),
    )(place, g, rh)


def _owner_sum(name, cs, rp, place):
    _, rh2, c = cs.shape
    tr = min(512, rh2)
    nb = rh2 // tr

    def body(pl_ref, cs_ref, r0_ref, r1_ref, r2_ref, o_ref):
        o_ref[...] = ((cs_ref[...].astype(F32) + r0_ref[...].astype(F32))
                      + (r1_ref[...].astype(F32) + r2_ref[...].astype(F32)))

    def rspec(j):
        return pl.BlockSpec((None, tr, c), lambda i, p: (j, i, 0))

    return pl.pallas_call(
        body, name=name,
        grid_spec=pltpu.PrefetchScalarGridSpec(
            num_scalar_prefetch=1, grid=(nb,),
            in_specs=[pl.BlockSpec((None, tr, c), lambda i, p: (p[1], i, 0)), rspec(0), rspec(1), rspec(2)],
            out_specs=pl.BlockSpec((tr, c), lambda i, p: (p[0] * nb + i, 0))),
        out_shape=jax.ShapeDtypeStruct((2 * rh2, c), F32),
        compiler_params=_params("parallel"),
    )(place, cs, rp, rp, rp)


def _adam_math(w, g, m, v):
    m = ADAM_B1 * m + (1.0 - ADAM_B1) * g
    v = ADAM_B2 * v + (1.0 - ADAM_B2) * (g * g)
    m_hat = m / (1.0 - ADAM_B1 ** ADAM_STEP)
    v_hat = v / (1.0 - ADAM_B2 ** ADAM_STEP)
    delta = -ADAM_LR * (m_hat / (jnp.sqrt(v_hat) + ADAM_EPS) + ADAM_WD * w)
    return delta, m, v


def _adamw(name, w, m, v, g, layer, partial=None):
    nl, r, c = w.shape
    tr = min(256, r)

    def body(w_ref, m_ref, v_ref, g_ref, *refs):
        go_ref, d_ref, mo_ref, vo_ref = refs[-4:]
        gv = g_ref[...]
        delta, m_new, v_new = _adam_math(w_ref[...], gv, m_ref[...], v_ref[...])
        go_ref[...] = gv
        d_ref[...] = delta
        mo_ref[...] = m_new
        vo_ref[...] = v_new

    wspec = pl.BlockSpec((None, tr, c), lambda i: (layer, i, 0))
    prev = [] if partial is None else list(partial)
    return pl.pallas_call(
        body, name=name, grid=(r // tr,),
        in_specs=[wspec] * 3 + [pl.BlockSpec((tr, c), lambda i: (i, 0))] + [ANY] * len(prev),
        out_specs=[wspec] * 4,
        out_shape=[jax.ShapeDtypeStruct((nl, r, c), F32)] * 4,
        input_output_aliases={4 + i: i for i in range(len(prev))},
        compiler_params=_params("parallel"),
    )(w, m, v, g, *prev)


def _adam_small(ws, ms, vs, gs):
    n = len(ws)

    def body(*refs):
        w_refs, m_refs, v_refs, g_refs = refs[:n], refs[n:2 * n], refs[2 * n:3 * n], refs[3 * n:4 * n]
        d_refs, mo_refs, vo_refs = refs[4 * n:5 * n], refs[5 * n:6 * n], refs[6 * n:7 * n]
        for i in range(n):
            delta, m_new, v_new = _adam_math(w_refs[i][...], g_refs[i][...], m_refs[i][...], v_refs[i][...])
            d_refs[i][...] = delta
            mo_refs[i][...] = m_new
            vo_refs[i][...] = v_new

    shapes = [jax.ShapeDtypeStruct(w.shape, F32) for w in ws]
    res = pl.pallas_call(body, name="adam_small", out_shape=shapes * 3)(*ws, *ms, *vs, *gs)
    return res[:n], res[n:2 * n], res[2 * n:]


def _pack_small(b_in, w_dw, b_dw, ln_g, ln_b, b_out):
    cin = b_in.shape[1]
    cd = b_dw.shape[1]

    def body(bi, wd, bd, lg, lb, bo, out):
        out[...] = jnp.zeros_like(out)
        out[0:1, :] = bi[...]
        out[1:2, 0:cd] = bd[...]
        out[1:2, cd:2 * cd] = lg[...]
        out[2:3, 0:cd] = lb[...]
        out[2:3, cd:2 * cd] = bo[...]
        out[8:8 + CONV_WIDTH, 0:cd] = wd[...]

    return pl.pallas_call(body, name="pack_small", out_shape=jax.ShapeDtypeStruct((8 + CONV_PAD, cin), F32))(
        b_in, w_dw, b_dw, ln_g, ln_b, b_out)


def _place():
    x, y, c = lax.axis_index("x"), lax.axis_index("y"), lax.axis_index("c")
    return x, y, c


def _other_chips(x, y):
    return [(1 - x, y), (x, 1 - y), (1 - x, 1 - y)]


def _comm_all_gather(shards, small):
    n = len(shards)

    def body(*refs):
        ins, sm_in = refs[:n], refs[n]
        outs, sm_out = refs[n + 1:2 * n + 1], refs[2 * n + 1]
        lsem, ssem, rsem = refs[2 * n + 2:]
        x, y, c = _place()
        me = 2 * x + y
        sib = (x, y, 1 - c)
        chips = _other_chips(x, y)

        def half(a, who):
            rh = shards[a].shape[0] // 2
            return pl.ds(who * rh, rh)

        local = [pltpu.make_async_copy(ins[a], outs[a].at[me], lsem.at[a]) for a in range(n)]
        local.append(pltpu.make_async_copy(sm_in, sm_out.at[me], lsem.at[n]))
        for cp in local:
            cp.start()

        def ici(a, j, shard, to):
            return pltpu.make_async_remote_copy(
                src_ref=ins[a].at[half(a, c)], dst_ref=outs[a].at[shard, half(a, c)],
                send_sem=ssem.at[6 * a + j], recv_sem=rsem.at[6 * a + j],
                device_id=to, device_id_type=MESH)

        def d2d(a, j, src_shard, who):
            return pltpu.make_async_remote_copy(
                src_ref=outs[a].at[src_shard, half(a, who)], dst_ref=outs[a].at[src_shard, half(a, who)],
                send_sem=ssem.at[6 * a + 3 + j], recv_sem=rsem.at[6 * a + 3 + j],
                device_id=sib, device_id_type=MESH)

        def small_copy(j, src_shard, to):
            return pltpu.make_async_remote_copy(
                src_ref=sm_in, dst_ref=sm_out.at[src_shard],
                send_sem=ssem.at[6 * n + j], recv_sem=rsem.at[6 * n + j],
                device_id=to, device_id_type=MESH)

        sends = []
        for j, (px, py) in enumerate(chips):
            sends.append(small_copy(j, me, (px, py, c)))
        for a in range(n):
            for j, (px, py) in enumerate(chips):
                sends.append(ici(a, j, me, (px, py, c)))
        for cp in sends:
            cp.start()
        passed = []
        for a in range(n):
            for j, (px, py) in enumerate(chips):
                ici(a, j, 2 * px + py, (x, y, c)).wait_recv()
                fw = d2d(a, j, 2 * px + py, c)
                fw.start()
                passed.append(fw)
        for a in range(n):
            for j, (px, py) in enumerate(chips):
                d2d(a, j, 2 * px + py, 1 - c).wait_recv()
        for j, (px, py) in enumerate(chips):
            small_copy(j, 2 * px + py, (x, y, c)).wait_recv()
        for cp in sends + passed:
            cp.wait_send()
        for cp in local:
            cp.wait()

    out_shape = [jax.ShapeDtypeStruct((N_SHARD,) + s.shape, s.dtype) for s in shards]
    out_shape.append(jax.ShapeDtypeStruct((N_SHARD,) + small.shape, small.dtype))
    return pl.pallas_call(
        body, name="comm_all_gather",
        in_specs=[ANY] * (n + 1), out_specs=[ANY] * (n + 1), out_shape=out_shape,
        scratch_shapes=[pltpu.SemaphoreType.DMA((n + 1,)), pltpu.SemaphoreType.DMA((6 * n + 3,)),
                        pltpu.SemaphoreType.DMA((6 * n + 3,))],
    )(*shards, small)


def _comm_sibling_halves(tag, grads):
    n = len(grads)

    def body(*refs):
        ins, outs = refs[:n], refs[n:2 * n]
        ssem, rsem = refs[2 * n:]
        x, y, c = _place()
        cps = []
        for a in range(n):
            rh = grads[a].shape[1] // 2
            cps.append(pltpu.make_async_remote_copy(
                src_ref=ins[a].at[:, pl.ds((1 - c) * rh, rh), :], dst_ref=outs[a],
                send_sem=ssem.at[a], recv_sem=rsem.at[a], device_id=(x, y, 1 - c), device_id_type=MESH))
        for cp in cps:
            cp.start()
        for cp in cps:
            cp.wait()

    return pl.pallas_call(
        body, name=f"comm_sibling_halves_{tag}",
        in_specs=[ANY] * n, out_specs=[ANY] * n,
        out_shape=[jax.ShapeDtypeStruct((N_SHARD, g.shape[1] // 2, g.shape[2]), g.dtype) for g in grads],
        scratch_shapes=[pltpu.SemaphoreType.DMA((n,)), pltpu.SemaphoreType.DMA((n,))],
    )(*grads)


def _owner_copies(srcs, lands, ssem, rsem):
    x, y, c = _place()
    cps = []
    for a in range(len(srcs)):
        for j, (px, py) in enumerate(_other_chips(x, y)):
            cps.append(pltpu.make_async_remote_copy(
                src_ref=srcs[a].at[2 * px + py], dst_ref=lands[a].at[j],
                send_sem=ssem.at[3 * a + j], recv_sem=rsem.at[3 * a + j],
                device_id=(px, py, c), device_id_type=MESH))
    return cps


def _owners_start(tag, sums):
    n = len(sums)

    def body(*refs):
        srcs, lands = refs[:n], refs[n:2 * n]
        ssem, rsem = refs[2 * n], refs[2 * n + 1]
        token = refs[-1]
        for cp in _owner_copies(srcs, lands, ssem, rsem):
            cp.start()
        token[...] = jnp.zeros_like(token)

    lands = [lax.empty((3,) + s.shape[1:], s.dtype) for s in sums]
    bufs = list(sums) + lands
    res = pl.pallas_call(
        body, name=f"owners_start_{tag}",
        out_shape=(pltpu.SemaphoreType.DMA((3 * n,)), pltpu.SemaphoreType.DMA((3 * n,)),
                   *[pltpu.HBM(b.shape, b.dtype) for b in bufs], jax.ShapeDtypeStruct((8, LANES), F32)),
        in_specs=[HBM] * (2 * n),
        out_specs=(SEM, SEM, *[HBM] * (2 * n), pl.BlockSpec(memory_space=pltpu.VMEM)),
        input_output_aliases={i: 2 + i for i in range(2 * n)},
        compiler_params=pltpu.CompilerParams(has_side_effects=SPLIT_EFFECT),
    )(*[pltpu.with_memory_space_constraint(b, pltpu.HBM) for b in bufs])
    return res[0], res[1], list(res[2:2 + n]), list(res[2 + n:2 + 2 * n]), res[-1]


def _owners_wait(tag, handle, after):
    ssem, rsem, srcs, lands, _ = handle
    n = len(srcs)

    def body(*refs):
        for cp in _owner_copies(refs[:n], refs[n:2 * n], refs[2 * n], refs[2 * n + 1]):
            cp.wait_send()
            cp.wait_recv()

    bufs = srcs + lands
    res = pl.pallas_call(
        body, name=f"owners_wait_{tag}",
        out_shape=[pltpu.HBM(b.shape, b.dtype) for b in bufs],
        in_specs=[HBM] * (2 * n) + [SEM, SEM, ANY], out_specs=[HBM] * (2 * n),
        input_output_aliases={i: i for i in range(2 * n)},
        compiler_params=pltpu.CompilerParams(has_side_effects=SPLIT_EFFECT),
    )(*bufs, ssem, rsem, after)
    return list(res[:n]), list(res[n:])


def _comm_swap_halves(tag, fulls):
    n = len(fulls)

    def body(*refs):
        ins, outs = refs[:n], refs[n:2 * n]
        ssem, rsem = refs[2 * n:]
        x, y, c = _place()
        cps = []
        for a in range(n):
            rh = fulls[a].shape[0] // 2
            mine = pl.ds(c * rh, rh)
            cps.append(pltpu.make_async_remote_copy(
                src_ref=ins[a].at[mine], dst_ref=outs[a].at[mine],
                send_sem=ssem.at[a], recv_sem=rsem.at[a], device_id=(x, y, 1 - c), device_id_type=MESH))
        for cp in cps:
            cp.start()
        for a in range(n):
            rh = fulls[a].shape[0] // 2
            theirs = pl.ds((1 - c) * rh, rh)
            pltpu.make_async_remote_copy(
                src_ref=ins[a].at[theirs], dst_ref=outs[a].at[theirs],
                send_sem=ssem.at[a], recv_sem=rsem.at[a], device_id=(x, y, 1 - c), device_id_type=MESH).wait_recv()
        for cp in cps:
            cp.wait_send()

    return pl.pallas_call(
        body, name=f"comm_swap_halves_{tag}",
        in_specs=[ANY] * n, out_specs=[ANY] * n,
        out_shape=[jax.ShapeDtypeStruct(f.shape, f.dtype) for f in fulls],
        input_output_aliases={a: a for a in range(n)},
        scratch_shapes=[pltpu.SemaphoreType.DMA((n,)), pltpu.SemaphoreType.DMA((n,))],
    )(*fulls)


def _comm_small_allreduce(rows, w_dw_grad, d):
    n = len(rows)
    loss_row = 12

    def body(*refs):
        vec_refs = refs[:n]
        wd_ref, out_ref, pack, slots, ssem, rsem = refs[n:]
        x, y, c = _place()
        me = 4 * x + 2 * y + c
        pack[...] = jnp.zeros_like(pack)
        for (r, _), ref in zip(rows, vec_refs):
            pack[r:r + 1, :] = ref[...]
        pack[16:16 + CONV_PAD, :] = wd_ref[...]
        slots[me] = pack[...]
        cps = []
        for rel in range(1, N_DEV):
            dx, dy, dc = (rel >> 2) & 1, (rel >> 1) & 1, rel & 1
            peer = (1 - x if dx else x, 1 - y if dy else y, 1 - c if dc else c)
            cps.append(pltpu.make_async_remote_copy(
                src_ref=pack, dst_ref=slots.at[me], send_sem=ssem.at[rel - 1], recv_sem=rsem.at[rel - 1],
                device_id=peer, device_id_type=MESH))
        for cp in cps:
            cp.start()
        for cp in cps:
            cp.wait()
        tot = slots[0]
        for i in range(1, N_DEV):
            tot = tot + slots[i]
        out_ref[...] = tot
        out_ref[loss_row:loss_row + 1, :] = jnp.zeros((1, d), F32) + jnp.sum(tot[loss_row:loss_row + 1, :])

    return pl.pallas_call(
        body, name="comm_small_allreduce",
        out_shape=jax.ShapeDtypeStruct((SMALL_ROWS, d), F32),
        scratch_shapes=[pltpu.VMEM((SMALL_ROWS, d), F32), pltpu.VMEM((N_DEV, SMALL_ROWS, d), F32),
                        pltpu.SemaphoreType.DMA((N_DEV - 1,)), pltpu.SemaphoreType.DMA((N_DEV - 1,))],
    )(*[v for _, v in rows], w_dw_grad)


def kernel(x, norm_mix, norm_mlp, conv_w_in, conv_b_in, conv_w_dw, conv_b_dw, conv_ln_g, conv_ln_b, conv_w_out, conv_b_out, kv_norm, w_kv, attn_w_q, attn_w_o, mlp_w_in, mlp_w_out, final_norm, loss_target, m_norm_mix, m_norm_mlp, m_conv_w_in, m_conv_b_in, m_conv_w_dw, m_conv_b_dw, m_conv_ln_g, m_conv_ln_b, m_conv_w_out, m_conv_b_out, m_kv_norm, m_w_kv, m_attn_w_q, m_attn_w_o, m_mlp_w_in, m_mlp_w_out, m_final_norm, v_norm_mix, v_norm_mlp, v_conv_w_in, v_conv_b_in, v_conv_w_dw, v_conv_b_dw, v_conv_ln_g, v_conv_ln_b, v_conv_w_out, v_conv_b_out, v_kv_norm, v_w_kv, v_attn_w_q, v_attn_w_o, v_mlp_w_in, v_mlp_w_out, v_final_norm):
    _, s, d = x.shape
    dff = mlp_w_in.shape[2] * N_SHARD
    kvw = w_kv.shape[1]
    nh = d // HEAD_DIM
    group = nh // N_KV_HEADS
    ds4 = d // N_SHARD
    xi, yi, ci = _place()
    me = 2 * xi + yi
    place = jnp.stack([ci, me]).astype(I32)

    h0 = x.reshape(s, d)
    target = loss_target.reshape(s, d)
    tabs = _rope_tables(s)

    shards = [
        _cast_bf16("cast_w_in", conv_w_in, 0),
        _cast_bf16("cast_w_out", conv_w_out, 0),
        _cast_bf16("cast_mlp_in0", mlp_w_in, 0),
        _cast_bf16("cast_mlp_out0", mlp_w_out, 0),
        _cast_bf16("cast_w_kv", w_kv.reshape(1, ds4, kvw), 0),
        _cast_bf16("cast_w_q", attn_w_q, 0),
        _cast_bf16("cast_w_o", attn_w_o, 0),
        _cast_bf16("cast_mlp_in1", mlp_w_in, 1),
        _cast_bf16("cast_mlp_out1", mlp_w_out, 1),
    ]
    small = _pack_small(conv_b_in, conv_w_dw.reshape(CONV_WIDTH, ds4), conv_b_dw, conv_ln_g, conv_ln_b, conv_b_out)
    gathered = _comm_all_gather(shards, small)
    w_in_g, w_out_g, wmi0_g, wmo0_g, wkv_g, wq_g, wo_g, wmi1_g, wmo1_g, small_g = gathered
    w_out_f = w_out_g.reshape(d, d)
    wmo_f = [wmo0_g.reshape(dff, d), wmo1_g.reshape(dff, d)]
    wmi_g = [wmi0_g, wmi1_g]
    wkv_f = wkv_g.reshape(d, kvw)
    wq_f = wq_g.reshape(d, d)
    wo_f = wo_g.reshape(d, d)
    b_in_f = small_g[:, 0, :].reshape(1, 2 * d)
    b_dw_f = small_g[:, 1, 0:ds4].reshape(1, d)
    ln_g_f = small_g[:, 1, ds4:2 * ds4].reshape(1, d)
    ln_b_f = small_g[:, 2, 0:ds4].reshape(1, d)
    b_out_f = small_g[:, 2, ds4:2 * ds4].reshape(1, d)
    w_dw_f = jnp.transpose(small_g[:, 8:8 + CONV_PAD, 0:ds4], (1, 0, 2)).reshape(CONV_PAD, d)

    nm = [norm_mix[0:1], norm_mix[1:2]]
    nmlp = [norm_mlp[0:1], norm_mlp[1:2]]
    kvn = kv_norm.reshape(1, d)
    fin = final_norm.reshape(1, d)

    def ep_bias(acc, ex, outs, j):
        outs[0][...] = (acc + ex[0][...]).astype(outs[0].dtype)

    def ep_residual(acc, ex, outs, j):
        outs[0][...] = ex[0][...] + acc

    def ep_residual_bias(acc, ex, outs, j):
        outs[0][...] = ex[0][...] + (acc + ex[1][...])

    def ep_relu2(acc, ex, outs, j):
        r = jnp.maximum(acc, 0.0)
        outs[0][...] = r.astype(BF16)
        outs[1][...] = (r * r).astype(BF16)

    def ep_rope(acc, ex, outs, j):
        outs[0][...] = _rope_apply(acc, ex[0][...], ex[1][...], ex[2][...], 1.0).astype(BF16)

    def ep_rope_k(acc, ex, outs, j):
        roped = _rope_apply(acc, ex[0][...], ex[1][...], ex[2][...], 1.0)
        outs[0][...] = jnp.where(j == 0, roped, acc).astype(BF16)

    tab_extras = [(t, "rows") for t in tabs]

    def mlp_fwd(idx, h, y):
        r, r2 = _matmul(f"mlp_in{idx}", "nn", y, wmi_g[idx], b_kind="col", m=s, n=dff, k=d,
                        tm=1024, tn=1024, tk=512, outs=[(BF16, "plain"), (BF16, "plain")], epilogue=ep_relu2)
        (h_new,) = _matmul(f"mlp_out{idx}", "nn", r2, wmo_f[idx], m=s, n=d, k=dff, tm=1024, tn=1024, tk=512,
                           outs=[(F32, "plain")], extras=[(h, "ij")], epilogue=ep_residual)
        return h_new, r, r2

    (y0,) = _rms_fwd("rms_mix0", h0, [nm[0]])
    tn_u = min(1024, 2 * d // N_SHARD)
    (u,) = _matmul("conv_in", "nn", y0, w_in_g, b_kind="col", m=s, n=2 * d, k=d, tm=1024, tn=tn_u, tk=512,
                   outs=[(BF16, "plain")], extras=[(b_in_f, "vec")], epilogue=ep_bias)
    cpre = _dwconv_fwd(u, w_dw_f, b_dw_f)
    sact = _ln_silu_fwd(cpre, ln_g_f, ln_b_f)
    (h1,) = _matmul("conv_out", "nn", sact, w_out_f, m=s, n=d, k=d, tm=1024, tn=1024, tk=512,
                    outs=[(F32, "plain")], extras=[(h0, "ij"), (b_out_f, "vec")], epilogue=ep_residual_bias)
    (y1,) = _rms_fwd("rms_mlp0", h1, [nmlp[0]])
    h2, r0, r0sq = mlp_fwd(0, h1, y1)
    ykv, y2 = _rms_fwd("rms_kv_mix1", h2, [kvn, nm[1]])
    (kv,) = _matmul("kv_proj", "nn", ykv, wkv_f, m=s, n=kvw, k=d, tm=1024, tn=kvw // 2, tk=512,
                    outs=[(BF16, "plain")], extras=tab_extras, epilogue=ep_rope_k)
    (q,) = _matmul("q_proj", "nn", y2, wq_f, m=s, n=d, k=d, tm=1024, tn=1024, tk=512,
                   outs=[(BF16, "plain")], extras=tab_extras, epilogue=ep_rope)
    o_parts, lse_parts = [], []
    for dil in DILATIONS:
        o_b, lse_b = _attn_fwd(f"attn_fwd_d{dil}", _to_branch(q, dil), _to_branch(kv, dil), dil, d)
        o_parts.append(o_b.reshape(s, d))
        lse_parts.append(_heads_from_branch(lse_b, dil, group))
    o, lse = _attn_combine(o_parts, lse_parts)
    (h3,) = _matmul("attn_out", "nn", o, wo_f, m=s, n=d, k=d, tm=1024, tn=1024, tk=512,
                    outs=[(F32, "plain")], extras=[(h2, "ij")], epilogue=ep_residual)
    (y3,) = _rms_fwd("rms_mlp1", h3, [nmlp[1]])
    h4, r1, r1sq = mlp_fwd(1, h3, y3)
    dh4, dh4b, d_fin, loss_cols = _final_loss(h4, fin, target)

    def ep_relu2_bwd(acc, ex, outs, j):
        outs[0][...] = (acc * (2.0 * ex[0][...].astype(F32))).astype(BF16)

    def mlp_bwd(idx, dhb, y, r, r2):
        (dz,) = _matmul(f"mlp_out{idx}_dx", "nt", dhb, wmo_f[idx], m=s, n=dff, k=d, tm=1024, tn=1024, tk=512,
                        outs=[(BF16, "plain")], extras=[(r, "ij")], epilogue=ep_relu2_bwd)
        (dwo,) = _matmul(f"mlp_out{idx}_dw", "tn", r2, dhb, m=dff, n=d, k=s, tm=1024, tn=1024, tk=512,
                         outs=[(BF16, "plain")])
        (dy,) = _matmul(f"mlp_in{idx}_dx", "nt", dz, wmi_g[idx], b_kind="col", m=s, n=d, k=dff,
                        tm=1024, tn=1024, tk=512, outs=[(BF16, "plain")])
        (dwi,) = _matmul(f"mlp_in{idx}_dw", "tn", y, dz, m=d, n=dff, k=s, tm=1024, tn=1024, tk=512,
                         outs=[(BF16, "col")])
        return dy, dwi, dwo.reshape(N_SHARD, dff // N_SHARD, d)

    def rs_begin(tag, names, grads):
        recv_half = _comm_sibling_halves(tag, grads)
        sums = [_chip_sum(f"chip_sum_{nme}", g, rh, place) for nme, g, rh in zip(names, grads, recv_half)]
        return _owners_start(tag, sums)

    def after(vec, handle):
        return vec + handle[4][0:1, 0:1]

    def rs_end(tag, names, handle, later):
        sums, pieces = _owners_wait(tag, handle, later)
        own = [_owner_sum(f"owner_sum_{nme}", cs, rp, place) for nme, cs, rp in zip(names, sums, pieces)]
        return _comm_swap_halves(tag, own)

    dy3, g_wmi1, g_wmo1 = mlp_bwd(1, dh4b, y3, r1, r1sq)
    rs_mlp1 = rs_begin("mlp1", ["mlp_in1", "mlp_out1"], [g_wmi1, g_wmo1])
    dh3, dh3b, d_nmlp1 = _rms_bwd("rms_mlp1_bwd", h3, [(after(nmlp[1], rs_mlp1), dy3)], dh4)

    (do,) = _matmul("attn_out_dx", "nt", dh3b, wo_f, m=s, n=d, k=d, tm=1024, tn=1024, tk=512, outs=[(BF16, "plain")])
    (g_wo,) = _matmul("attn_out_dw", "tn", o, dh3b, m=d, n=d, k=s, tm=1024, tn=1024, tk=512, outs=[(BF16, "plain")])
    delta = _attn_delta(do, o)
    dq_parts, dk_parts, dv_parts = [], [], []
    for dil in DILATIONS:
        dq_b, dk_b, dv_b = _attn_bwd(
            f"attn_bwd_d{dil}", _to_branch(q, dil), _to_branch(kv, dil), _to_branch(do, dil),
            _heads_to_branch(lse, dil, group), _heads_to_branch(delta, dil, group), dil, d)
        dq_parts.append(dq_b.reshape(s, d))
        dk_parts.append(dk_b.reshape(s, kvw // 2))
        dv_parts.append(dv_b.reshape(s, kvw // 2))
    dq = _rope_bwd_sum("rope_bwd_q", dq_parts, tabs, d)
    dkv_parts = [jnp.concatenate([a, b], axis=1) for a, b in zip(dk_parts, dv_parts)]
    dkv = _rope_bwd_sum("rope_bwd_kv", dkv_parts, tabs, kvw // 2)
    (g_wq,) = _matmul("q_proj_dw", "tn", y2, dq, m=d, n=d, k=s, tm=1024, tn=1024, tk=512, outs=[(BF16, "plain")])
    (dy2,) = _matmul("q_proj_dx", "nt", dq, wq_f, m=s, n=d, k=d, tm=1024, tn=1024, tk=512, outs=[(BF16, "plain")])
    (g_wkv,) = _matmul("kv_proj_dw", "tn", ykv, dkv, m=d, n=kvw, k=s, tm=1024, tn=1024, tk=512, outs=[(BF16, "plain")])
    (dykv,) = _matmul("kv_proj_dx", "nt", dkv, wkv_f, m=s, n=d, k=kvw, tm=1024, tn=1024, tk=512, outs=[(BF16, "plain")])
    rs_attn = rs_begin("attn", ["w_kv", "w_q", "w_o"],
                       [g_wkv.reshape(N_SHARD, ds4, kvw), g_wq.reshape(N_SHARD, ds4, d), g_wo.reshape(N_SHARD, ds4, d)])
    dh2, dh2b, d_nm1, d_kvn = _rms_bwd("rms_kv_mix1_bwd", h2, [(after(nm[1], rs_attn), dy2), (kvn, dykv)], dh3)

    dy1, g_wmi0, g_wmo0 = mlp_bwd(0, dh2b, y1, r0, r0sq)
    rs_mlp0 = rs_begin("mlp0", ["mlp_in0", "mlp_out0"], [g_wmi0, g_wmo0])
    dh1, dh1b, d_nmlp0, d_b_out = _rms_bwd("rms_mlp0_bwd", h1, [(after(nmlp[0], rs_mlp0), dy1)], dh2,
                                           want_colsum=True)

    (dsact,) = _matmul("conv_out_dx", "nt", dh1b, w_out_f, m=s, n=d, k=d, tm=1024, tn=1024, tk=512, outs=[(BF16, "plain")])
    (g_wout,) = _matmul("conv_out_dw", "tn", sact, dh1b, m=d, n=d, k=s, tm=1024, tn=1024, tk=512, outs=[(BF16, "plain")])
    dc, d_ln_g, d_ln_b, d_b_dw = _ln_silu_bwd(cpre, ln_g_f, ln_b_f, dsact)
    da, dgt, d_w_dw, d_b_in_a, d_b_in_g = _dwconv_bwd(u, w_dw_f, dc)
    du = jnp.concatenate([da, dgt], axis=1)
    (g_win,) = _matmul("conv_in_dw", "tn", y0, du, m=d, n=2 * d, k=s, tm=1024, tn=tn_u, tk=512, outs=[(BF16, "col")])
    rs_conv = rs_begin("conv", ["w_in", "w_out"], [g_win, g_wout.reshape(N_SHARD, ds4, d)])
    (dy0,) = _matmul("conv_in_dx", "nt", du, w_in_g, b_kind="col", m=s, n=d, k=2 * d, tm=1024, tn=1024, tk=tn_u,
                     outs=[(BF16, "plain")])
    dx, _, d_nm0 = _rms_bwd("rms_mix0_bwd", h0, [(after(nm[0], rs_conv), dy0)], dh1)

    small_rows = [(0, d_nm0), (1, d_nm1), (2, d_nmlp0), (3, d_nmlp1), (4, d_kvn), (5, d_fin), (6, d_b_dw),
                  (7, d_ln_g), (8, d_ln_b), (9, d_b_out), (10, d_b_in_a), (11, d_b_in_g), (12, loss_cols)]
    red = _comm_small_allreduce(small_rows, d_w_dw, d)
    loss = red[12, 0]
    g_norm_mix = red[0:2]
    g_norm_mlp = red[2:4]
    g_kv_norm = red[4:5]
    g_final = red[5:6]

    def my_cols(row):
        return lax.dynamic_slice(red, (row, me * ds4), (1, ds4))

    g_b_dw, g_ln_g, g_ln_b, g_b_out = my_cols(6), my_cols(7), my_cols(8), my_cols(9)
    half_in = 2 * d // N_SHARD
    b_in_row = 10 + me // 2
    g_b_in = lax.dynamic_slice(red, (b_in_row, (me % 2) * half_in), (1, half_in))
    g_w_dw = lax.dynamic_slice(red, (16, me * ds4), (CONV_WIDTH, ds4))

    def big(name, w, m, v, g, layer=0, partial=None):
        shape = w.shape
        w3, m3, v3 = [t.reshape((-1,) + shape[-2:]) for t in (w, m, v)]
        if partial is not None:
            partial = [t.reshape(w3.shape) for t in partial]
        res = _adamw(name, w3, m3, v3, g, layer, partial)
        return [t.reshape(shape) for t in res]

    f_wmi1, f_wmo1 = rs_end("mlp1", ["mlp_in1", "mlp_out1"], rs_mlp1, rs_conv[4])
    p_wmi = big("adam_mlp_in1", mlp_w_in, m_mlp_w_in, v_mlp_w_in, f_wmi1, 1)
    p_wmo = big("adam_mlp_out1", mlp_w_out, m_mlp_w_out, v_mlp_w_out, f_wmo1, 1)
    f_wkv, f_wq, f_wo = rs_end("attn", ["w_kv", "w_q", "w_o"], rs_attn, p_wmo[0])
    r_wkv = big("adam_w_kv", w_kv, m_w_kv, v_w_kv, f_wkv)
    r_wq = big("adam_w_q", attn_w_q, m_attn_w_q, v_attn_w_q, f_wq)
    r_wo = big("adam_w_o", attn_w_o, m_attn_w_o, v_attn_w_o, f_wo)
    f_wmi0, f_wmo0 = rs_end("mlp0", ["mlp_in0", "mlp_out0"], rs_mlp0, r_wo[0])
    r_wmi = big("adam_mlp_in0", mlp_w_in, m_mlp_w_in, v_mlp_w_in, f_wmi0, 0, p_wmi)
    r_wmo = big("adam_mlp_out0", mlp_w_out, m_mlp_w_out, v_mlp_w_out, f_wmo0, 0, p_wmo)
    f_win, f_wout = rs_end("conv", ["w_in", "w_out"], rs_conv, r_wmo[0])
    r_win = big("adam_w_in", conv_w_in, m_conv_w_in, v_conv_w_in, f_win)
    r_wout = big("adam_w_out", conv_w_out, m_conv_w_out, v_conv_w_out, f_wout)

    sm_w = [norm_mix, norm_mlp, conv_b_in, conv_w_dw.reshape(CONV_WIDTH, ds4), conv_b_dw, conv_ln_g, conv_ln_b,
            conv_b_out, kv_norm.reshape(1, d), final_norm.reshape(1, d)]
    sm_m = [m_norm_mix, m_norm_mlp, m_conv_b_in, m_conv_w_dw.reshape(CONV_WIDTH, ds4), m_conv_b_dw, m_conv_ln_g,
            m_conv_ln_b, m_conv_b_out, m_kv_norm.reshape(1, d), m_final_norm.reshape(1, d)]
    sm_v = [v_norm_mix, v_norm_mlp, v_conv_b_in, v_conv_w_dw.reshape(CONV_WIDTH, ds4), v_conv_b_dw, v_conv_ln_g,
            v_conv_ln_b, v_conv_b_out, v_kv_norm.reshape(1, d), v_final_norm.reshape(1, d)]
    sm_g = [g_norm_mix, g_norm_mlp, g_b_in, g_w_dw, g_b_dw, g_ln_g, g_ln_b, g_b_out, g_kv_norm, g_final]
    sm_d, sm_nm, sm_nv = _adam_small(sm_w, sm_m, sm_v, sm_g)
    shapes = [norm_mix.shape, norm_mlp.shape, conv_b_in.shape, conv_w_dw.shape, conv_b_dw.shape, conv_ln_g.shape,
              conv_ln_b.shape, conv_b_out.shape, kv_norm.shape, final_norm.shape]
    sm_g, sm_d, sm_nm, sm_nv = [[t.reshape(sh) for t, sh in zip(lst, shapes)] for lst in (sm_g, sm_d, sm_nm, sm_nv)]

    def order(sm, idx):
        return [sm[0], sm[1], r_win[idx], sm[2], sm[3], sm[4], sm[5], sm[6], r_wout[idx], sm[7], sm[8],
                r_wkv[idx], r_wq[idx], r_wo[idx], r_wmi[idx], r_wmo[idx], sm[9]]

    return (loss, dx.reshape(x.shape), *order(sm_g, 0), *order(sm_d, 1), *order(sm_nm, 2), *order(sm_nv, 3))
```

```python
import math

import jax
import jax.numpy as jnp
from jax import lax
from jax.experimental import pallas as pl
from jax.experimental.pallas import tpu as pltpu

F32 = jnp.float32
BF16 = jnp.bfloat16
I32 = jnp.int32

NORM_EPS = 1e-6
LN_EPS = 1e-5
HEAD_DIM = 128
N_KV_HEADS = 4
ROT_DIM = 32
ROPE_THETA = 500000.0
CONV_WIDTH = 31
CONV_PAD = 32
ATT_BLOCK = 128
DILATIONS = (1, 4, 16)
ADAM_LR = 0.001
ADAM_B1 = 0.9
ADAM_B2 = 0.999
ADAM_EPS = 1e-08
ADAM_WD = 0.01
ADAM_STEP = 10
N_SHARD = 4
N_DEV = 8
LANES = 128
VMEM_LIMIT = 48 * 1024 * 1024
ROW_TILE = 256
CONV_CB = 128
CONV_T = 128
SMALL_ROWS = 48
MESH = pl.DeviceIdType.MESH
ANY = pl.BlockSpec(memory_space=pl.ANY)
HBM = pl.BlockSpec(memory_space=pltpu.HBM)
SEM = pl.BlockSpec(memory_space=pltpu.SEMAPHORE)
SPLIT_EFFECT = pltpu.SideEffectType.DATAFLOW_SIDE_EFFECTING


def _params(*sem):
    return pltpu.CompilerParams(dimension_semantics=sem, vmem_limit_bytes=VMEM_LIMIT)


def _sigmoid(x):
    return 1.0 / (1.0 + jnp.exp(-x))


def _wspec(kind, arr_shape, br, bc, pick):
    if kind == "plain":
        return pl.BlockSpec((br, bc), pick)
    per = arr_shape[2] // bc

    def idx(*g):
        rb, cb = pick(*g)
        return (cb // per, rb, cb % per)

    return pl.BlockSpec((None, br, bc), idx)


def _matmul(name, mode, a, b, *, m, n, k, tm, tn, tk, b_kind="plain", outs, extras=(), epilogue=None):
    tm, tn, tk = min(tm, m), min(tn, n), min(tk, k)
    if b_kind == "col" and mode == "nn":
        tn = min(tn, n // N_SHARD)
    if b_kind == "col" and mode == "nt":
        tk = min(tk, k // N_SHARD)
    if any(kind == "col" for _, kind in outs):
        tn = min(tn, n // N_SHARD)
    assert m % tm == 0 and n % tn == 0 and k % tk == 0, (name, m, n, k, tm, tn, tk)
    nk = k // tk
    grid = (m // tm, n // tn, nk)
    if mode == "nn":
        a_spec = pl.BlockSpec((tm, tk), lambda i, j, kk: (i, kk))
        b_spec = _wspec(b_kind, b.shape, tk, tn, lambda i, j, kk: (kk, j))
        dims = (((1,), (0,)), ((), ()))
    elif mode == "nt":
        a_spec = pl.BlockSpec((tm, tk), lambda i, j, kk: (i, kk))
        b_spec = _wspec(b_kind, b.shape, tn, tk, lambda i, j, kk: (j, kk))
        dims = (((1,), (1,)), ((), ()))
    else:
        a_spec = pl.BlockSpec((tk, tm), lambda i, j, kk: (kk, i))
        b_spec = pl.BlockSpec((tk, tn), lambda i, j, kk: (kk, j))
        dims = (((0,), (0,)), ((), ()))
    out_shape, out_specs = [], []
    for dtype, kind in outs:
        shape = (m, n) if kind == "plain" else (N_SHARD, m, n // N_SHARD)
        out_shape.append(jax.ShapeDtypeStruct(shape, dtype))
        out_specs.append(_wspec(kind, shape, tm, tn, lambda i, j, kk: (i, j)))
    n_ex = len(extras)
    ex_specs = {"ij": pl.BlockSpec((tm, tn), lambda i, j, kk: (i, j)),
                "vec": pl.BlockSpec((1, tn), lambda i, j, kk: (0, j)),
                "rows": pl.BlockSpec((tm, LANES), lambda i, j, kk: (i, 0))}

    def body(*refs):
        a_ref, b_ref = refs[0], refs[1]
        ex_refs = refs[2:2 + n_ex]
        out_refs = refs[2 + n_ex:-1]
        acc_ref = refs[-1]
        j = pl.program_id(1)
        kk = pl.program_id(2)

        @pl.when(kk == 0)
        def _():
            acc_ref[...] = jnp.zeros_like(acc_ref)

        acc_ref[...] += lax.dot_general(a_ref[...], b_ref[...], dims, preferred_element_type=F32)

        @pl.when(kk == nk - 1)
        def _():
            if epilogue is None:
                out_refs[0][...] = acc_ref[...].astype(out_refs[0].dtype)
            else:
                epilogue(acc_ref[...], ex_refs, out_refs, j)

    res = pl.pallas_call(
        body, name=name, grid=grid,
        in_specs=[a_spec, b_spec] + [ex_specs[how] for _, how in extras],
        out_specs=out_specs, out_shape=out_shape,
        scratch_shapes=[pltpu.VMEM((tm, tn), F32)],
        compiler_params=_params("parallel", "parallel", "arbitrary"),
    )(a, b, *[e for e, _ in extras])
    return res


def _rope_tables(seq):
    half = ROT_DIM // 2
    pos = jnp.arange(seq, dtype=F32)
    inv = ROPE_THETA ** (-jnp.arange(0, ROT_DIM, 2, dtype=F32) / ROT_DIM)
    ang = pos[:, None] * inv[None, :]
    cos, sin = jnp.cos(ang), jnp.sin(ang)
    zeros = jnp.zeros((seq, HEAD_DIM - ROT_DIM), F32)
    ctab = jnp.concatenate([cos, cos, zeros + 1.0], axis=1)
    atab = jnp.concatenate([-sin, jnp.zeros((seq, half), F32), zeros], axis=1)
    btab = jnp.concatenate([jnp.zeros((seq, half), F32), sin, zeros], axis=1)
    return ctab, atab, btab


def _rope_apply(x, ctab, atab, btab, sign):
    w = x.shape[1]
    reps = w // HEAD_DIM
    half = ROT_DIM // 2
    c = jnp.tile(ctab, (1, reps))
    a = jnp.tile(atab, (1, reps))
    b = jnp.tile(btab, (1, reps))
    up = pltpu.roll(x, w - half, 1)
    down = pltpu.roll(x, half, 1)
    return x * c + sign * (up * a + down * b)


def _rows(t, w):
    return pl.BlockSpec((t, w), lambda i: (i, 0))


def _fixed(shape):
    nd = len(shape)
    return pl.BlockSpec(shape, lambda i: (0,) * nd)


def _rms_fwd(name, x, gains):
    s, d = x.shape
    t = min(ROW_TILE, s)
    ng = len(gains)

    def body(x_ref, *refs):
        xv = x_ref[...]
        r = lax.rsqrt(jnp.mean(xv * xv, axis=-1, keepdims=True) + NORM_EPS)
        xn = xv * r
        for g_ref, y_ref in zip(refs[:ng], refs[ng:]):
            y_ref[...] = (xn * g_ref[...]).astype(BF16)

    return pl.pallas_call(
        body, name=name, grid=(s // t,),
        in_specs=[_rows(t, d)] + [_fixed((1, d))] * ng,
        out_specs=[_rows(t, d)] * ng,
        out_shape=[jax.ShapeDtypeStruct((s, d), BF16)] * ng,
        compiler_params=_params("parallel"),
    )(x, *gains)


def _rms_bwd(name, x, pairs, dh_in, want_colsum=False):
    s, d = x.shape
    t = min(ROW_TILE, s)
    n_p = len(pairs)

    def body(x_ref, dh_ref, *refs):
        g_refs = refs[:n_p]
        dy_refs = refs[n_p:2 * n_p]
        dh_out, dhb_out = refs[2 * n_p], refs[2 * n_p + 1]
        dg_refs = refs[2 * n_p + 2:2 * n_p + 2 + n_p]
        cs_ref = refs[-1] if want_colsum else None
        i = pl.program_id(0)
        xv = x_ref[...]
        r = lax.rsqrt(jnp.mean(xv * xv, axis=-1, keepdims=True) + NORM_EPS)
        xn = xv * r
        dh = dh_ref[...]
        for g_ref, dy_ref, dg_ref in zip(g_refs, dy_refs, dg_refs):
            dy = dy_ref[...].astype(F32)
            u = dy * g_ref[...]
            dh = dh + r * (u - xn * jnp.mean(u * xn, axis=-1, keepdims=True))
            part = jnp.sum(dy * xn, axis=0, keepdims=True)

            @pl.when(i == 0)
            def _():
                dg_ref[...] = part

            @pl.when(i > 0)
            def _():
                dg_ref[...] += part

        dh_out[...] = dh
        dhb_out[...] = dh.astype(BF16)
        if want_colsum:
            col = jnp.sum(dh, axis=0, keepdims=True)

            @pl.when(i == 0)
            def _():
                cs_ref[...] = col

            @pl.when(i > 0)
            def _():
                cs_ref[...] += col

    n_vec = n_p + (1 if want_colsum else 0)
    return pl.pallas_call(
        body, name=name, grid=(s // t,),
        in_specs=[_rows(t, d), _rows(t, d)] + [_fixed((1, d))] * n_p + [_rows(t, d)] * n_p,
        out_specs=[_rows(t, d), _rows(t, d)] + [_fixed((1, d))] * n_vec,
        out_shape=[jax.ShapeDtypeStruct((s, d), F32), jax.ShapeDtypeStruct((s, d), BF16)]
        + [jax.ShapeDtypeStruct((1, d), F32)] * n_vec,
        compiler_params=_params("arbitrary"),
    )(x, dh_in, *[g for g, _ in pairs], *[dy for _, dy in pairs])


def _final_loss(x, g, target):
    s, d = x.shape
    t = min(ROW_TILE, s)

    def body(x_ref, g_ref, t_ref, dh_out, dhb_out, dg_ref, loss_ref):
        i = pl.program_id(0)
        xv = x_ref[...]
        gv = g_ref[...]
        r = lax.rsqrt(jnp.mean(xv * xv, axis=-1, keepdims=True) + NORM_EPS)
        xn = xv * r
        diff = xn * gv - t_ref[...]
        dy = diff / d
        u = dy * gv
        dh = r * (u - xn * jnp.mean(u * xn, axis=-1, keepdims=True))
        dh_out[...] = dh
        dhb_out[...] = dh.astype(BF16)
        dg = jnp.sum(dy * xn, axis=0, keepdims=True)
        lc = jnp.sum(0.5 * diff * dy, axis=0, keepdims=True)

        @pl.when(i == 0)
        def _():
            dg_ref[...] = dg
            loss_ref[...] = lc

        @pl.when(i > 0)
        def _():
            dg_ref[...] += dg
            loss_ref[...] += lc

    return pl.pallas_call(
        body, name="final_loss", grid=(s // t,),
        in_specs=[_rows(t, d), _fixed((1, d)), _rows(t, d)],
        out_specs=[_rows(t, d), _rows(t, d), _fixed((1, d)), _fixed((1, d))],
        out_shape=[jax.ShapeDtypeStruct((s, d), F32), jax.ShapeDtypeStruct((s, d), BF16),
                   jax.ShapeDtypeStruct((1, d), F32), jax.ShapeDtypeStruct((1, d), F32)],
        compiler_params=_params("arbitrary"),
    )(x, g, target)


def _ln_silu_fwd(c, g, b):
    s, d = c.shape
    t = min(ROW_TILE, s)

    def body(c_ref, g_ref, b_ref, s_ref):
        cv = c_ref[...]
        mu = jnp.mean(cv, axis=-1, keepdims=True)
        xc = cv - mu
        rs = lax.rsqrt(jnp.mean(xc * xc, axis=-1, keepdims=True) + LN_EPS)
        ln = xc * rs * g_ref[...] + b_ref[...]
        s_ref[...] = (ln * _sigmoid(ln)).astype(BF16)

    return pl.pallas_call(
        body, name="ln_silu_fwd", grid=(s // t,),
        in_specs=[_rows(t, d), _fixed((1, d)), _fixed((1, d))],
        out_specs=_rows(t, d), out_shape=jax.ShapeDtypeStruct((s, d), BF16),
        compiler_params=_params("parallel"),
    )(c, g, b)


def _ln_silu_bwd(c, g, b, ds):
    s, d = c.shape
    t = min(ROW_TILE, s)

    def body(c_ref, g_ref, b_ref, ds_ref, dc_ref, dg_ref, db_ref, dbdw_ref):
        i = pl.program_id(0)
        cv = c_ref[...]
        gv = g_ref[...]
        mu = jnp.mean(cv, axis=-1, keepdims=True)
        xc = cv - mu
        rs = lax.rsqrt(jnp.mean(xc * xc, axis=-1, keepdims=True) + LN_EPS)
        nrm = xc * rs
        ln = nrm * gv + b_ref[...]
        sig = _sigmoid(ln)
        dln = ds_ref[...].astype(F32) * sig * (1.0 + ln * (1.0 - sig))
        dn = dln * gv
        dc = rs * (dn - jnp.mean(dn, axis=-1, keepdims=True)
                   - nrm * jnp.mean(dn * nrm, axis=-1, keepdims=True))
        dc_ref[...] = dc
        pg = jnp.sum(dln * nrm, axis=0, keepdims=True)
        pb = jnp.sum(dln, axis=0, keepdims=True)
        pc = jnp.sum(dc, axis=0, keepdims=True)

        @pl.when(i == 0)
        def _():
            dg_ref[...] = pg
            db_ref[...] = pb
            dbdw_ref[...] = pc

        @pl.when(i > 0)
        def _():
            dg_ref[...] += pg
            db_ref[...] += pb
            dbdw_ref[...] += pc

    return pl.pallas_call(
        body, name="ln_silu_bwd", grid=(s // t,),
        in_specs=[_rows(t, d), _fixed((1, d)), _fixed((1, d)), _rows(t, d)],
        out_specs=[_rows(t, d)] + [_fixed((1, d))] * 3,
        out_shape=[jax.ShapeDtypeStruct((s, d), F32)] + [jax.ShapeDtypeStruct((1, d), F32)] * 3,
        compiler_params=_params("arbitrary"),
    )(c, g, b, ds)


def _attn_combine(o_list, lse_list):
    s, d = o_list[0].shape
    nh = d // HEAD_DIM
    t = min(ROW_TILE, s)
    nb = len(o_list)

    def body(*refs):
        o_refs = refs[:nb]
        l_refs = refs[nb:2 * nb]
        o_out, l_out = refs[2 * nb], refs[2 * nb + 1]
        ls = [r[...] for r in l_refs]
        mx = ls[0]
        for l in ls[1:]:
            mx = jnp.maximum(mx, l)
        es = [jnp.exp(l - mx) for l in ls]
        den = es[0]
        for e in es[1:]:
            den = den + e
        l_out[...] = mx + jnp.log(den)
        ws = [e / den for e in es]
        for h in range(nh):
            cols = slice(h * HEAD_DIM, (h + 1) * HEAD_DIM)
            acc = jnp.zeros((t, HEAD_DIM), F32)
            for o_ref, w in zip(o_refs, ws):
                acc = acc + w[:, h:h + 1] * o_ref[:, cols].astype(F32)
            o_out[:, cols] = acc.astype(BF16)

    return pl.pallas_call(
        body, name="attn_combine", grid=(s // t,),
        in_specs=[_rows(t, d)] * nb + [_rows(t, nh)] * nb,
        out_specs=[_rows(t, d), _rows(t, nh)],
        out_shape=[jax.ShapeDtypeStruct((s, d), BF16), jax.ShapeDtypeStruct((s, nh), F32)],
        compiler_params=_params("parallel"),
    )(*o_list, *lse_list)


def _attn_delta(do, o):
    s, d = o.shape
    nh = d // HEAD_DIM
    t = min(ROW_TILE, s)

    def body(do_ref, o_ref, dl_ref):
        lane = lax.broadcasted_iota(I32, (t, nh), 1)
        out = jnp.zeros((t, nh), F32)
        for h in range(nh):
            cols = slice(h * HEAD_DIM, (h + 1) * HEAD_DIM)
            v = jnp.sum(do_ref[:, cols].astype(F32) * o_ref[:, cols].astype(F32), axis=-1, keepdims=True)
            out = jnp.where(lane == h, v, out)
        dl_ref[...] = out

    return pl.pallas_call(
        body, name="attn_delta", grid=(s // t,),
        in_specs=[_rows(t, d), _rows(t, d)],
        out_specs=_rows(t, nh), out_shape=jax.ShapeDtypeStruct((s, nh), F32),
        compiler_params=_params("parallel"),
    )(do, o)


def _rope_bwd_sum(name, parts, tabs, rope_cols):
    s, w = parts[0].shape
    t = min(ROW_TILE, s)
    n_p = len(parts)

    def body(*refs):
        p_refs = refs[:n_p]
        c_ref, a_ref, b_ref = refs[n_p:n_p + 3]
        out = refs[-1]
        tot = p_refs[0][...].astype(F32)
        for p in p_refs[1:]:
            tot = tot + p[...].astype(F32)
        rot = _rope_apply(tot[:, :rope_cols], c_ref[...], a_ref[...], b_ref[...], -1.0)
        out[:, :rope_cols] = rot.astype(BF16)
        if rope_cols < w:
            out[:, rope_cols:] = tot[:, rope_cols:].astype(BF16)

    return pl.pallas_call(
        body, name=name, grid=(s // t,),
        in_specs=[_rows(t, w)] * n_p + [_rows(t, HEAD_DIM)] * 3,
        out_specs=_rows(t, w), out_shape=jax.ShapeDtypeStruct((s, w), BF16),
        compiler_params=_params("parallel"),
    )(*parts, *tabs)


def _dwconv_fwd(u, w_dw, b_dw):
    s, d2 = u.shape
    d = d2 // 2
    cb = min(CONV_CB, d)
    nblk = d // cb
    tt = min(CONV_T, s)

    def body(ua_ref, ug_ref, w_ref, b_ref, c_ref, xp_ref):
        gl = ua_ref[...].astype(F32) * _sigmoid(ug_ref[...].astype(F32))
        xp_ref[0:CONV_PAD, :] = jnp.zeros((CONV_PAD, cb), F32)
        xp_ref[CONV_PAD:, :] = gl
        wv = w_ref[...]
        bv = b_ref[...]
        for t0 in range(0, s, tt):
            acc = jnp.zeros((tt, cb), F32) + bv
            for kk in range(CONV_WIDTH):
                off = t0 + CONV_PAD - (CONV_WIDTH - 1) + kk
                acc = acc + wv[kk:kk + 1, :] * xp_ref[off:off + tt, :]
            c_ref[t0:t0 + tt, :] = acc

    return pl.pallas_call(
        body, name="dwconv_fwd", grid=(nblk,),
        in_specs=[pl.BlockSpec((s, cb), lambda j: (0, j)), pl.BlockSpec((s, cb), lambda j: (0, j + nblk)),
                  pl.BlockSpec((CONV_PAD, cb), lambda j: (0, j)), pl.BlockSpec((1, cb), lambda j: (0, j))],
        out_specs=pl.BlockSpec((s, cb), lambda j: (0, j)),
        out_shape=jax.ShapeDtypeStruct((s, d), F32),
        scratch_shapes=[pltpu.VMEM((s + CONV_PAD, cb), F32)],
        compiler_params=_params("parallel"),
    )(u, u, w_dw, b_dw)


def _dwconv_bwd(u, w_dw, dc):
    s, d2 = u.shape
    d = d2 // 2
    cb = min(CONV_CB, d)
    nblk = d // cb
    tt = min(CONV_T, s)

    def body(ua_ref, ug_ref, w_ref, dc_ref, da_ref, dgt_ref, dw_ref, dba_ref, dbg_ref, glp_ref, dcp_ref, acc_ref):
        a = ua_ref[...].astype(F32)
        sig = _sigmoid(ug_ref[...].astype(F32))
        glp_ref[0:CONV_PAD, :] = jnp.zeros((CONV_PAD, cb), F32)
        glp_ref[CONV_PAD:, :] = a * sig
        dcp_ref[0:s, :] = dc_ref[...]
        dcp_ref[s:, :] = jnp.zeros((CONV_PAD, cb), F32)
        acc_ref[...] = jnp.zeros_like(acc_ref)
        wv = w_ref[...]
        dba = jnp.zeros((1, cb), F32)
        dbg = jnp.zeros((1, cb), F32)
        for t0 in range(0, s, tt):
            dgl = jnp.zeros((tt, cb), F32)
            dct = dc_ref[t0:t0 + tt, :]
            for kk in range(CONV_WIDTH):
                off = t0 + (CONV_WIDTH - 1) - kk
                dgl = dgl + wv[kk:kk + 1, :] * dcp_ref[off:off + tt, :]
                goff = t0 + CONV_PAD - (CONV_WIDTH - 1) + kk
                prod = dct * glp_ref[goff:goff + tt, :]
                acc_ref[8 * kk:8 * kk + 8, :] += jnp.sum(prod.reshape(tt // 8, 8, cb), axis=0)
            at = ua_ref[t0:t0 + tt, :].astype(F32)
            st = _sigmoid(ug_ref[t0:t0 + tt, :].astype(F32))
            da = dgl * st
            dg = dgl * at * st * (1.0 - st)
            da_ref[t0:t0 + tt, :] = da.astype(BF16)
            dgt_ref[t0:t0 + tt, :] = dg.astype(BF16)
            dba = dba + jnp.sum(da, axis=0, keepdims=True)
            dbg = dbg + jnp.sum(dg, axis=0, keepdims=True)
        dba_ref[...] = dba
        dbg_ref[...] = dbg
        for kk in range(CONV_WIDTH):
            dw_ref[kk:kk + 1, :] = jnp.sum(acc_ref[8 * kk:8 * kk + 8, :], axis=0, keepdims=True)
        dw_ref[CONV_WIDTH:, :] = jnp.zeros((CONV_PAD - CONV_WIDTH, cb), F32)

    blk = pl.BlockSpec((s, cb), lambda j: (0, j))
    vec = pl.BlockSpec((1, cb), lambda j: (0, j))
    return pl.pallas_call(
        body, name="dwconv_bwd", grid=(nblk,),
        in_specs=[blk, pl.BlockSpec((s, cb), lambda j: (0, j + nblk)),
                  pl.BlockSpec((CONV_PAD, cb), lambda j: (0, j)), blk],
        out_specs=[blk, blk, pl.BlockSpec((CONV_PAD, cb), lambda j: (0, j)), vec, vec],
        out_shape=[jax.ShapeDtypeStruct((s, d), BF16), jax.ShapeDtypeStruct((s, d), BF16),
                   jax.ShapeDtypeStruct((CONV_PAD, d), F32),
                   jax.ShapeDtypeStruct((1, d), F32), jax.ShapeDtypeStruct((1, d), F32)],
        scratch_shapes=[pltpu.VMEM((s + CONV_PAD, cb), F32), pltpu.VMEM((s + CONV_PAD, cb), F32),
                        pltpu.VMEM((8 * CONV_PAD, cb), F32)],
        compiler_params=_params("parallel"),
    )(u, u, w_dw, dc)


def _stack_heads(x, group):
    return jnp.concatenate([x[:, g * HEAD_DIM:(g + 1) * HEAD_DIM] for g in range(group)], axis=0)


def _unstack_heads(x, group):
    return jnp.concatenate([x[g * ATT_BLOCK:(g + 1) * ATT_BLOCK, :] for g in range(group)], axis=1)


def _stack_cols(x, group):
    return jnp.concatenate([x[:, g:g + 1] for g in range(group)], axis=0)


def _band_mask(nb, group):
    rows = group * ATT_BLOCK
    row = lax.broadcasted_iota(I32, (rows, 2 * ATT_BLOCK), 0) % ATT_BLOCK
    col = lax.broadcasted_iota(I32, (rows, 2 * ATT_BLOCK), 1)
    return (col >= row) & (col <= row + ATT_BLOCK) & ((col >= ATT_BLOCK) | (nb > 0))


def _window(ref, nb):
    prev = pl.multiple_of(jnp.maximum(nb - 1, 0) * ATT_BLOCK, ATT_BLOCK)
    cur = pl.multiple_of(nb * ATT_BLOCK, ATT_BLOCK)
    return jnp.concatenate([ref[pl.ds(prev, ATT_BLOCK), :], ref[pl.ds(cur, ATT_BLOCK), :]], axis=0)


def _attn_fwd(name, q, kv, dil, d):
    sd = q.shape[0]
    group = d // HEAD_DIM // N_KV_HEADS
    gw = group * HEAD_DIM
    nblk = sd // ATT_BLOCK
    scale = 1.0 / math.sqrt(HEAD_DIM)
    nt = (((1,), (1,)), ((), ()))

    def body(q_ref, k_ref, v_ref, o_ref, lse_ref):
        lane = lax.broadcasted_iota(I32, (ATT_BLOCK, group), 1)

        def step(nb, carry):
            rows = pl.ds(pl.multiple_of(nb * ATT_BLOCK, ATT_BLOCK), ATT_BLOCK)
            qs = _stack_heads(q_ref[rows, :], group)
            kw = _window(k_ref, nb)
            vw = _window(v_ref, nb)
            sc = lax.dot_general(qs, kw, nt, preferred_element_type=F32) * scale
            sc = jnp.where(_band_mask(nb, group), sc, -jnp.inf)
            mx = jnp.max(sc, axis=-1, keepdims=True)
            p = jnp.exp(sc - mx)
            l = jnp.sum(p, axis=-1, keepdims=True)
            o = jnp.dot(p.astype(BF16), vw, preferred_element_type=F32) / l
            o_ref[rows, :] = _unstack_heads(o, group).astype(BF16)
            lse = mx + jnp.log(l)
            out = jnp.zeros((ATT_BLOCK, group), F32)
            for g in range(group):
                out = jnp.where(lane == g, lse[g * ATT_BLOCK:(g + 1) * ATT_BLOCK, :], out)
            lse_ref[rows, :] = out
            return carry

        lax.fori_loop(0, nblk, step, 0)

    kvh = N_KV_HEADS
    return pl.pallas_call(
        body, name=name, grid=(dil, kvh),
        in_specs=[pl.BlockSpec((sd, gw), lambda r, h: (0, r * kvh + h)),
                  pl.BlockSpec((sd, HEAD_DIM), lambda r, h: (0, r * 2 * kvh + h)),
                  pl.BlockSpec((sd, HEAD_DIM), lambda r, h: (0, r * 2 * kvh + kvh + h))],
        out_specs=[pl.BlockSpec((sd, gw), lambda r, h: (0, r * kvh + h)),
                   pl.BlockSpec((None, sd, group), lambda r, h: (r * kvh + h, 0, 0))],
        out_shape=[jax.ShapeDtypeStruct((sd, dil * d), BF16),
                   jax.ShapeDtypeStruct((dil * kvh, sd, group), F32)],
        compiler_params=_params("parallel", "parallel"),
    )(q, kv, kv)


def _attn_bwd(name, q, kv, do, lse, delta, dil, d):
    sd = q.shape[0]
    group = d // HEAD_DIM // N_KV_HEADS
    gw = group * HEAD_DIM
    nblk = sd // ATT_BLOCK
    scale = 1.0 / math.sqrt(HEAD_DIM)
    nt = (((1,), (1,)), ((), ()))
    tn = (((0,), (0,)), ((), ()))

    def body(q_ref, k_ref, v_ref, do_ref, lse_ref, dl_ref, dq_ref, dk_ref, dv_ref, dk_acc, dv_acc):
        dk_acc[...] = jnp.zeros_like(dk_acc)
        dv_acc[...] = jnp.zeros_like(dv_acc)

        def step(nb, carry):
            rows = pl.ds(pl.multiple_of(nb * ATT_BLOCK, ATT_BLOCK), ATT_BLOCK)
            qs = _stack_heads(q_ref[rows, :], group)
            dos = _stack_heads(do_ref[rows, :], group)
            ls = _stack_cols(lse_ref[rows, :], group)
            dl = _stack_cols(dl_ref[rows, :], group)
            kw = _window(k_ref, nb)
            vw = _window(v_ref, nb)
            sc = lax.dot_general(qs, kw, nt, preferred_element_type=F32) * scale
            sc = jnp.where(_band_mask(nb, group), sc, -jnp.inf)
            p = jnp.exp(sc - ls)
            dp = lax.dot_general(dos, vw, nt, preferred_element_type=F32)
            ds = (p * (dp - dl) * scale).astype(BF16)
            dq = jnp.dot(ds, kw, preferred_element_type=F32)
            dq_ref[rows, :] = _unstack_heads(dq, group).astype(BF16)
            win = pl.ds(pl.multiple_of(nb * ATT_BLOCK, ATT_BLOCK), 2 * ATT_BLOCK)
            dk_acc[win, :] += lax.dot_general(ds, qs, tn, preferred_element_type=F32)
            dv_acc[win, :] += lax.dot_general(p.astype(BF16), dos, tn, preferred_element_type=F32)
            return carry

        lax.fori_loop(0, nblk, step, 0)
        dk_ref[...] = dk_acc[ATT_BLOCK:, :]
        dv_ref[...] = dv_acc[ATT_BLOCK:, :]

    kvh = N_KV_HEADS
    qspec = pl.BlockSpec((sd, gw), lambda r, h: (0, r * kvh + h))
    sspec = pl.BlockSpec((None, sd, group), lambda r, h: (r * kvh + h, 0, 0))
    kspec = pl.BlockSpec((sd, HEAD_DIM), lambda r, h: (0, r * kvh + h))
    return pl.pallas_call(
        body, name=name, grid=(dil, kvh),
        in_specs=[qspec,
                  pl.BlockSpec((sd, HEAD_DIM), lambda r, h: (0, r * 2 * kvh + h)),
                  pl.BlockSpec((sd, HEAD_DIM), lambda r, h: (0, r * 2 * kvh + kvh + h)),
                  qspec, sspec, sspec],
        out_specs=[qspec, kspec, kspec],
        out_shape=[jax.ShapeDtypeStruct((sd, dil * d), BF16),
                   jax.ShapeDtypeStruct((sd, dil * kvh * HEAD_DIM), F32),
                   jax.ShapeDtypeStruct((sd, dil * kvh * HEAD_DIM), F32)],
        scratch_shapes=[pltpu.VMEM((sd + ATT_BLOCK, HEAD_DIM), F32), pltpu.VMEM((sd + ATT_BLOCK, HEAD_DIM), F32)],
        compiler_params=_params("parallel", "parallel"),
    )(q, kv, kv, do, lse, delta)


def _to_branch(x, dil):
    s, w = x.shape
    return x.reshape(s // dil, dil * w)


def _heads_to_branch(x, dil, group):
    s = x.shape[0]
    x = x.reshape(s // dil, dil, N_KV_HEADS, group)
    return jnp.transpose(x, (1, 2, 0, 3)).reshape(dil * N_KV_HEADS, s // dil, group)


def _heads_from_branch(x, dil, group):
    sd = x.shape[1]
    x = x.reshape(dil, N_KV_HEADS, sd, group)
    return jnp.transpose(x, (2, 0, 1, 3)).reshape(sd * dil, N_KV_HEADS * group)


def _cast_bf16(name, w, layer, place):
    _, r, c = w.shape
    tr = min(512, r)

    def body(pl_ref, w_ref, o_ref):
        o_ref[...] = w_ref[...].astype(BF16)

    return pl.pallas_call(
        body, name=name,
        grid_spec=pltpu.PrefetchScalarGridSpec(
            num_scalar_prefetch=1, grid=(r // tr,),
            in_specs=[pl.BlockSpec((None, tr, c), lambda i, p: (layer, i, 0))],
            out_specs=pl.BlockSpec((None, tr, c), lambda i, p: (p[1], i, 0))),
        out_shape=jax.ShapeDtypeStruct((N_SHARD, r, c), BF16),
        compiler_params=_params("parallel"),
    )(place, w)


def _chip_sum(name, g, rh, place):
    _, r, c = g.shape
    rh2 = r // 2
    tr = min(512, rh2)
    nb = rh2 // tr

    def body(pl_ref, g_ref, rh_ref, o_ref):
        o_ref[...] = (g_ref[...].astype(F32) + rh_ref[...].astype(F32)).astype(BF16)

    return pl.pallas_call(
        body, name=name,
        grid_spec=pltpu.PrefetchScalarGridSpec(
            num_scalar_prefetch=1, grid=(N_SHARD, nb),
            in_specs=[pl.BlockSpec((None, tr, c), lambda s, i, p: (s, p[0] * nb + i, 0)),
                      pl.BlockSpec((None, tr, c), lambda s, i, p: (s, i, 0))],
            out_specs=pl.BlockSpec((None, tr, c), lambda s, i, p: (s, i, 0))),
        out_shape=jax.ShapeDtypeStruct((N_SHARD, rh2, c), BF16),
        compiler_params=_params("parallel", "parallel"),
    )(place, g, rh)


def _owner_sum(name, cs, rp, place):
    _, rh2, c = cs.shape
    tr = min(512, rh2)
    nb = rh2 // tr

    def body(pl_ref, cs_ref, r0_ref, r1_ref, r2_ref, o_ref):
        o_ref[...] = ((cs_ref[...].astype(F32) + r0_ref[...].astype(F32))
                      + (r1_ref[...].astype(F32) + r2_ref[...].astype(F32)))

    def rspec(j):
        return pl.BlockSpec((None, tr, c), lambda i, p: (j, i, 0))

    return pl.pallas_call(
        body, name=name,
        grid_spec=pltpu.PrefetchScalarGridSpec(
            num_scalar_prefetch=1, grid=(nb,),
            in_specs=[pl.BlockSpec((None, tr, c), lambda i, p: (p[1], i, 0)), rspec(0), rspec(1), rspec(2)],
            out_specs=pl.BlockSpec((tr, c), lambda i, p: (p[0] * nb + i, 0))),
        out_shape=jax.ShapeDtypeStruct((2 * rh2, c), F32),
        compiler_params=_params("parallel"),
    )(place, cs, rp, rp, rp)


def _adam_math(w, g, m, v):
    m = ADAM_B1 * m + (1.0 - ADAM_B1) * g
    v = ADAM_B2 * v + (1.0 - ADAM_B2) * (g * g)
    m_hat = m / (1.0 - ADAM_B1 ** ADAM_STEP)
    v_hat = v / (1.0 - ADAM_B2 ** ADAM_STEP)
    delta = -ADAM_LR * (m_hat / (jnp.sqrt(v_hat) + ADAM_EPS) + ADAM_WD * w)
    return delta, m, v


def _adamw(name, w, m, v, g, layer, partial=None):
    nl, r, c = w.shape
    tr = min(256, r)

    def body(w_ref, m_ref, v_ref, g_ref, *refs):
        go_ref, d_ref, mo_ref, vo_ref = refs[-4:]
        gv = g_ref[...]
        delta, m_new, v_new = _adam_math(w_ref[...], gv, m_ref[...], v_ref[...])
        go_ref[...] = gv
        d_ref[...] = delta
        mo_ref[...] = m_new
        vo_ref[...] = v_new

    wspec = pl.BlockSpec((None, tr, c), lambda i: (layer, i, 0))
    prev = [] if partial is None else list(partial)
    return pl.pallas_call(
        body, name=name, grid=(r // tr,),
        in_specs=[wspec] * 3 + [pl.BlockSpec((tr, c), lambda i: (i, 0))] + [ANY] * len(prev),
        out_specs=[wspec] * 4,
        out_shape=[jax.ShapeDtypeStruct((nl, r, c), F32)] * 4,
        input_output_aliases={4 + i: i for i in range(len(prev))},
        compiler_params=_params("parallel"),
    )(w, m, v, g, *prev)


def _adam_small(ws, ms, vs, gs):
    n = len(ws)

    def body(*refs):
        w_refs, m_refs, v_refs, g_refs = refs[:n], refs[n:2 * n], refs[2 * n:3 * n], refs[3 * n:4 * n]
        d_refs, mo_refs, vo_refs = refs[4 * n:5 * n], refs[5 * n:6 * n], refs[6 * n:7 * n]
        for i in range(n):
            delta, m_new, v_new = _adam_math(w_refs[i][...], g_refs[i][...], m_refs[i][...], v_refs[i][...])
            d_refs[i][...] = delta
            mo_refs[i][...] = m_new
            vo_refs[i][...] = v_new

    shapes = [jax.ShapeDtypeStruct(w.shape, F32) for w in ws]
    res = pl.pallas_call(body, name="adam_small", out_shape=shapes * 3)(*ws, *ms, *vs, *gs)
    return res[:n], res[n:2 * n], res[2 * n:]


def _pack_small(b_in, w_dw, b_dw, ln_g, ln_b, b_out, place):
    cin = b_in.shape[1]
    cd = b_dw.shape[1]
    rows = 8 + CONV_PAD

    def body(pl_ref, bi, wd, bd, lg, lb, bo, out):
        out[...] = jnp.zeros_like(out)
        out[0:1, :] = bi[...]
        out[1:2, 0:cd] = bd[...]
        out[1:2, cd:2 * cd] = lg[...]
        out[2:3, 0:cd] = lb[...]
        out[2:3, cd:2 * cd] = bo[...]
        out[8:8 + CONV_WIDTH, 0:cd] = wd[...]

    def whole(arr):
        return pl.BlockSpec(arr.shape, lambda i, p: (0,) * arr.ndim)

    ins = [b_in, w_dw, b_dw, ln_g, ln_b, b_out]
    return pl.pallas_call(
        body, name="pack_small",
        grid_spec=pltpu.PrefetchScalarGridSpec(
            num_scalar_prefetch=1, grid=(1,), in_specs=[whole(a) for a in ins],
            out_specs=pl.BlockSpec((None, rows, cin), lambda i, p: (p[1], 0, 0))),
        out_shape=jax.ShapeDtypeStruct((N_SHARD, rows, cin), F32),
        compiler_params=_params("arbitrary"),
    )(place, *ins)


def _place():
    x, y, c = lax.axis_index("x"), lax.axis_index("y"), lax.axis_index("c")
    return x, y, c


def _other_chips(x, y):
    return [(1 - x, y), (x, 1 - y), (1 - x, 1 - y)]


def _gather_copies(bufs, n_whole, ssem, rsem, landing):
    x, y, c = _place()
    me = 2 * x + y
    cps = []
    for a, ref in enumerate(bufs):
        whole = a >= len(bufs) - n_whole
        rh = ref.shape[1] // 2
        for j, (px, py) in enumerate(_other_chips(x, y)):
            shard = 2 * px + py if landing else me
            src = ref.at[me] if whole else ref.at[me, pl.ds(c * rh, rh)]
            dst = ref.at[shard] if whole else ref.at[shard, pl.ds(c * rh, rh)]
            cps.append(pltpu.make_async_remote_copy(
                src_ref=src, dst_ref=dst, send_sem=ssem.at[3 * a + j], recv_sem=rsem.at[3 * a + j],
                device_id=(px, py, c), device_id_type=MESH))
    return cps


def _gather_start(tag, bufs, n_whole=0):
    n = len(bufs)

    def body(*refs):
        for cp in _gather_copies(refs[:n], n_whole, refs[n], refs[n + 1], False):
            cp.start()
        refs[-1][...] = jnp.zeros_like(refs[-1])

    res = pl.pallas_call(
        body, name=f"gather_start_{tag}",
        out_shape=(pltpu.SemaphoreType.DMA((3 * n,)), pltpu.SemaphoreType.DMA((3 * n,)),
                   *[pltpu.HBM(b.shape, b.dtype) for b in bufs], jax.ShapeDtypeStruct((8, LANES), F32)),
        in_specs=[HBM] * n,
        out_specs=(SEM, SEM, *[HBM] * n, pl.BlockSpec(memory_space=pltpu.VMEM)),
        input_output_aliases={i: 2 + i for i in range(n)},
        compiler_params=pltpu.CompilerParams(has_side_effects=SPLIT_EFFECT),
    )(*[pltpu.with_memory_space_constraint(b, pltpu.HBM) for b in bufs])
    return res[0], res[1], list(res[2:2 + n]), res[-1]


def _gather_wait(tag, handle, n_whole, after):
    ssem, rsem, bufs, _ = handle
    n = len(bufs)

    def body(*refs):
        for cp in _gather_copies(refs[:n], n_whole, refs[n], refs[n + 1], True):
            cp.wait_send()
            cp.wait_recv()

    res = pl.pallas_call(
        body, name=f"gather_wait_{tag}",
        out_shape=[pltpu.HBM(b.shape, b.dtype) for b in bufs],
        in_specs=[HBM] * n + [SEM, SEM, ANY], out_specs=[HBM] * n,
        input_output_aliases={i: i for i in range(n)},
        compiler_params=pltpu.CompilerParams(has_side_effects=SPLIT_EFFECT),
    )(*bufs, ssem, rsem, after)
    return list(res)


def _comm_gather_forward(tag, bufs):
    n = len(bufs)

    def body(*refs):
        ins, outs = refs[:n], refs[n:2 * n]
        ssem, rsem = refs[2 * n:]
        x, y, c = _place()
        sends = []
        for a in range(n):
            rh = bufs[a].shape[1] // 2
            for j, (px, py) in enumerate(_other_chips(x, y)):
                rows = pl.ds(c * rh, rh)
                sends.append(pltpu.make_async_remote_copy(
                    src_ref=ins[a].at[2 * px + py, rows], dst_ref=outs[a].at[2 * px + py, rows],
                    send_sem=ssem.at[3 * a + j], recv_sem=rsem.at[3 * a + j],
                    device_id=(x, y, 1 - c), device_id_type=MESH))
        for cp in sends:
            cp.start()
        for a in range(n):
            rh = bufs[a].shape[1] // 2
            for j, (px, py) in enumerate(_other_chips(x, y)):
                rows = pl.ds((1 - c) * rh, rh)
                pltpu.make_async_remote_copy(
                    src_ref=ins[a].at[2 * px + py, rows], dst_ref=outs[a].at[2 * px + py, rows],
                    send_sem=ssem.at[3 * a + j], recv_sem=rsem.at[3 * a + j],
                    device_id=(x, y, 1 - c), device_id_type=MESH).wait_recv()
        for cp in sends:
            cp.wait_send()

    return pl.pallas_call(
        body, name=f"comm_gather_forward_{tag}",
        in_specs=[ANY] * n, out_specs=[ANY] * n,
        out_shape=[jax.ShapeDtypeStruct(b.shape, b.dtype) for b in bufs],
        input_output_aliases={a: a for a in range(n)},
        scratch_shapes=[pltpu.SemaphoreType.DMA((3 * n,)), pltpu.SemaphoreType.DMA((3 * n,))],
    )(*bufs)


def _comm_sibling_halves(tag, grads):
    n = len(grads)

    def body(*refs):
        ins, outs = refs[:n], refs[n:2 * n]
        ssem, rsem = refs[2 * n:]
        x, y, c = _place()
        cps = []
        for a in range(n):
            rh = grads[a].shape[1] // 2
            cps.append(pltpu.make_async_remote_copy(
                src_ref=ins[a].at[:, pl.ds((1 - c) * rh, rh), :], dst_ref=outs[a],
                send_sem=ssem.at[a], recv_sem=rsem.at[a], device_id=(x, y, 1 - c), device_id_type=MESH))
        for cp in cps:
            cp.start()
        for cp in cps:
            cp.wait()

    return pl.pallas_call(
        body, name=f"comm_sibling_halves_{tag}",
        in_specs=[ANY] * n, out_specs=[ANY] * n,
        out_shape=[jax.ShapeDtypeStruct((N_SHARD, g.shape[1] // 2, g.shape[2]), g.dtype) for g in grads],
        scratch_shapes=[pltpu.SemaphoreType.DMA((n,)), pltpu.SemaphoreType.DMA((n,))],
    )(*grads)


def _owner_copies(srcs, lands, ssem, rsem):
    x, y, c = _place()
    cps = []
    for a in range(len(srcs)):
        for j, (px, py) in enumerate(_other_chips(x, y)):
            cps.append(pltpu.make_async_remote_copy(
                src_ref=srcs[a].at[2 * px + py], dst_ref=lands[a].at[j],
                send_sem=ssem.at[3 * a + j], recv_sem=rsem.at[3 * a + j],
                device_id=(px, py, c), device_id_type=MESH))
    return cps


def _owners_start(tag, sums):
    n = len(sums)

    def body(*refs):
        srcs, lands = refs[:n], refs[n:2 * n]
        ssem, rsem = refs[2 * n], refs[2 * n + 1]
        token = refs[-1]
        for cp in _owner_copies(srcs, lands, ssem, rsem):
            cp.start()
        token[...] = jnp.zeros_like(token)

    lands = [lax.empty((3,) + s.shape[1:], s.dtype) for s in sums]
    bufs = list(sums) + lands
    res = pl.pallas_call(
        body, name=f"owners_start_{tag}",
        out_shape=(pltpu.SemaphoreType.DMA((3 * n,)), pltpu.SemaphoreType.DMA((3 * n,)),
                   *[pltpu.HBM(b.shape, b.dtype) for b in bufs], jax.ShapeDtypeStruct((8, LANES), F32)),
        in_specs=[HBM] * (2 * n),
        out_specs=(SEM, SEM, *[HBM] * (2 * n), pl.BlockSpec(memory_space=pltpu.VMEM)),
        input_output_aliases={i: 2 + i for i in range(2 * n)},
        compiler_params=pltpu.CompilerParams(has_side_effects=SPLIT_EFFECT),
    )(*[pltpu.with_memory_space_constraint(b, pltpu.HBM) for b in bufs])
    return res[0], res[1], list(res[2:2 + n]), list(res[2 + n:2 + 2 * n]), res[-1]


def _owners_wait(tag, handle, after):
    ssem, rsem, srcs, lands, _ = handle
    n = len(srcs)

    def body(*refs):
        for cp in _owner_copies(refs[:n], refs[n:2 * n], refs[2 * n], refs[2 * n + 1]):
            cp.wait_send()
            cp.wait_recv()

    bufs = srcs + lands
    res = pl.pallas_call(
        body, name=f"owners_wait_{tag}",
        out_shape=[pltpu.HBM(b.shape, b.dtype) for b in bufs],
        in_specs=[HBM] * (2 * n) + [SEM, SEM, ANY], out_specs=[HBM] * (2 * n),
        input_output_aliases={i: i for i in range(2 * n)},
        compiler_params=pltpu.CompilerParams(has_side_effects=SPLIT_EFFECT),
    )(*bufs, ssem, rsem, after)
    return list(res[:n]), list(res[n:])


def _comm_swap_halves(tag, fulls):
    n = len(fulls)

    def body(*refs):
        ins, outs = refs[:n], refs[n:2 * n]
        ssem, rsem = refs[2 * n:]
        x, y, c = _place()
        cps = []
        for a in range(n):
            rh = fulls[a].shape[0] // 2
            mine = pl.ds(c * rh, rh)
            cps.append(pltpu.make_async_remote_copy(
                src_ref=ins[a].at[mine], dst_ref=outs[a].at[mine],
                send_sem=ssem.at[a], recv_sem=rsem.at[a], device_id=(x, y, 1 - c), device_id_type=MESH))
        for cp in cps:
            cp.start()
        for a in range(n):
            rh = fulls[a].shape[0] // 2
            theirs = pl.ds((1 - c) * rh, rh)
            pltpu.make_async_remote_copy(
                src_ref=ins[a].at[theirs], dst_ref=outs[a].at[theirs],
                send_sem=ssem.at[a], recv_sem=rsem.at[a], device_id=(x, y, 1 - c), device_id_type=MESH).wait_recv()
        for cp in cps:
            cp.wait_send()

    return pl.pallas_call(
        body, name=f"comm_swap_halves_{tag}",
        in_specs=[ANY] * n, out_specs=[ANY] * n,
        out_shape=[jax.ShapeDtypeStruct(f.shape, f.dtype) for f in fulls],
        input_output_aliases={a: a for a in range(n)},
        scratch_shapes=[pltpu.SemaphoreType.DMA((n,)), pltpu.SemaphoreType.DMA((n,))],
    )(*fulls)


def _comm_small_allreduce(rows, w_dw_grad, d):
    n = len(rows)
    loss_row = 12

    def body(*refs):
        vec_refs = refs[:n]
        wd_ref, out_ref, pack, slots, ssem, rsem = refs[n:]
        x, y, c = _place()
        me = 4 * x + 2 * y + c
        pack[...] = jnp.zeros_like(pack)
        for (r, _), ref in zip(rows, vec_refs):
            pack[r:r + 1, :] = ref[...]
        pack[16:16 + CONV_PAD, :] = wd_ref[...]
        slots[me] = pack[...]
        cps = []
        for rel in range(1, N_DEV):
            dx, dy, dc = (rel >> 2) & 1, (rel >> 1) & 1, rel & 1
            peer = (1 - x if dx else x, 1 - y if dy else y, 1 - c if dc else c)
            cps.append(pltpu.make_async_remote_copy(
                src_ref=pack, dst_ref=slots.at[me], send_sem=ssem.at[rel - 1], recv_sem=rsem.at[rel - 1],
                device_id=peer, device_id_type=MESH))
        for cp in cps:
            cp.start()
        for cp in cps:
            cp.wait()
        tot = slots[0]
        for i in range(1, N_DEV):
            tot = tot + slots[i]
        out_ref[...] = tot
        out_ref[loss_row:loss_row + 1, :] = jnp.zeros((1, d), F32) + jnp.sum(tot[loss_row:loss_row + 1, :])

    return pl.pallas_call(
        body, name="comm_small_allreduce",
        out_shape=jax.ShapeDtypeStruct((SMALL_ROWS, d), F32),
        scratch_shapes=[pltpu.VMEM((SMALL_ROWS, d), F32), pltpu.VMEM((N_DEV, SMALL_ROWS, d), F32),
                        pltpu.SemaphoreType.DMA((N_DEV - 1,)), pltpu.SemaphoreType.DMA((N_DEV - 1,))],
    )(*[v for _, v in rows], w_dw_grad)


def kernel(x, norm_mix, norm_mlp, conv_w_in, conv_b_in, conv_w_dw, conv_b_dw, conv_ln_g, conv_ln_b, conv_w_out, conv_b_out, kv_norm, w_kv, attn_w_q, attn_w_o, mlp_w_in, mlp_w_out, final_norm, loss_target, m_norm_mix, m_norm_mlp, m_conv_w_in, m_conv_b_in, m_conv_w_dw, m_conv_b_dw, m_conv_ln_g, m_conv_ln_b, m_conv_w_out, m_conv_b_out, m_kv_norm, m_w_kv, m_attn_w_q, m_attn_w_o, m_mlp_w_in, m_mlp_w_out, m_final_norm, v_norm_mix, v_norm_mlp, v_conv_w_in, v_conv_b_in, v_conv_w_dw, v_conv_b_dw, v_conv_ln_g, v_conv_ln_b, v_conv_w_out, v_conv_b_out, v_kv_norm, v_w_kv, v_attn_w_q, v_attn_w_o, v_mlp_w_in, v_mlp_w_out, v_final_norm):
    _, s, d = x.shape
    dff = mlp_w_in.shape[2] * N_SHARD
    kvw = w_kv.shape[1]
    nh = d // HEAD_DIM
    group = nh // N_KV_HEADS
    ds4 = d // N_SHARD
    xi, yi, ci = _place()
    me = 2 * xi + yi
    place = jnp.stack([ci, me]).astype(I32)

    h0 = x.reshape(s, d)
    target = loss_target.reshape(s, d)
    tabs = _rope_tables(s)

    ag_conv = _gather_start("conv", [
        _cast_bf16("cast_w_in", conv_w_in, 0, place), _cast_bf16("cast_w_out", conv_w_out, 0, place),
        _pack_small(conv_b_in, conv_w_dw.reshape(CONV_WIDTH, ds4), conv_b_dw, conv_ln_g, conv_ln_b, conv_b_out, place),
    ], n_whole=1)
    ag_mlp0 = _gather_start("mlp0", [
        _cast_bf16("cast_mlp_in0", mlp_w_in, 0, place), _cast_bf16("cast_mlp_out0", mlp_w_out, 0, place)])
    ag_attn = _gather_start("attn", [
        _cast_bf16("cast_w_kv", w_kv.reshape(1, ds4, kvw), 0, place), _cast_bf16("cast_w_q", attn_w_q, 0, place),
        _cast_bf16("cast_w_o", attn_w_o, 0, place)])
    ag_mlp1 = _gather_start("mlp1", [
        _cast_bf16("cast_mlp_in1", mlp_w_in, 1, place), _cast_bf16("cast_mlp_out1", mlp_w_out, 1, place)])

    def gather_end(tag, handle, later, n_whole=0):
        bufs = _gather_wait(tag, handle, n_whole, later)
        n_half = len(bufs) - n_whole
        return list(_comm_gather_forward(tag, bufs[:n_half])) + bufs[n_half:]

    wmi_g = [None, None]
    wmo_f = [None, None]

    nm = [norm_mix[0:1], norm_mix[1:2]]
    nmlp = [norm_mlp[0:1], norm_mlp[1:2]]
    kvn = kv_norm.reshape(1, d)
    fin = final_norm.reshape(1, d)
    started = ag_conv[3][0:1, 0:1] + ag_mlp0[3][0:1, 0:1] + ag_attn[3][0:1, 0:1] + ag_mlp1[3][0:1, 0:1]
    (y0,) = _rms_fwd("rms_mix0", h0, [nm[0] + started])

    w_in_g, w_out_g, small_g = gather_end("conv", ag_conv, y0, n_whole=1)
    w_out_f = w_out_g.reshape(d, d)
    b_in_f = small_g[:, 0, :].reshape(1, 2 * d)
    b_dw_f = small_g[:, 1, 0:ds4].reshape(1, d)
    ln_g_f = small_g[:, 1, ds4:2 * ds4].reshape(1, d)
    ln_b_f = small_g[:, 2, 0:ds4].reshape(1, d)
    b_out_f = small_g[:, 2, ds4:2 * ds4].reshape(1, d)
    w_dw_f = jnp.transpose(small_g[:, 8:8 + CONV_PAD, 0:ds4], (1, 0, 2)).reshape(CONV_PAD, d)

    def ep_bias(acc, ex, outs, j):
        outs[0][...] = (acc + ex[0][...]).astype(outs[0].dtype)

    def ep_residual(acc, ex, outs, j):
        outs[0][...] = ex[0][...] + acc

    def ep_residual_bias(acc, ex, outs, j):
        outs[0][...] = ex[0][...] + (acc + ex[1][...])

    def ep_relu2(acc, ex, outs, j):
        r = jnp.maximum(acc, 0.0)
        outs[0][...] = r.astype(BF16)
        outs[1][...] = (r * r).astype(BF16)

    def ep_rope(acc, ex, outs, j):
        outs[0][...] = _rope_apply(acc, ex[0][...], ex[1][...], ex[2][...], 1.0).astype(BF16)

    def ep_rope_k(acc, ex, outs, j):
        roped = _rope_apply(acc, ex[0][...], ex[1][...], ex[2][...], 1.0)
        outs[0][...] = jnp.where(j == 0, roped, acc).astype(BF16)

    tab_extras = [(t, "rows") for t in tabs]

    def mlp_fwd(idx, h, y):
        r, r2 = _matmul(f"mlp_in{idx}", "nn", y, wmi_g[idx], b_kind="col", m=s, n=dff, k=d,
                        tm=1024, tn=1024, tk=512, outs=[(BF16, "plain"), (BF16, "plain")], epilogue=ep_relu2)
        (h_new,) = _matmul(f"mlp_out{idx}", "nn", r2, wmo_f[idx], m=s, n=d, k=dff, tm=1024, tn=1024, tk=512,
                           outs=[(F32, "plain")], extras=[(h, "ij")], epilogue=ep_residual)
        return h_new, r, r2

    tn_u = min(1024, 2 * d // N_SHARD)
    (u,) = _matmul("conv_in", "nn", y0, w_in_g, b_kind="col", m=s, n=2 * d, k=d, tm=1024, tn=tn_u, tk=512,
                   outs=[(BF16, "plain")], extras=[(b_in_f, "vec")], epilogue=ep_bias)
    cpre = _dwconv_fwd(u, w_dw_f, b_dw_f)
    sact = _ln_silu_fwd(cpre, ln_g_f, ln_b_f)
    (h1,) = _matmul("conv_out", "nn", sact, w_out_f, m=s, n=d, k=d, tm=1024, tn=1024, tk=512,
                    outs=[(F32, "plain")], extras=[(h0, "ij"), (b_out_f, "vec")], epilogue=ep_residual_bias)
    (y1,) = _rms_fwd("rms_mlp0", h1, [nmlp[0]])
    wmi_g[0], wmo0_g = gather_end("mlp0", ag_mlp0, y1)
    wmo_f[0] = wmo0_g.reshape(dff, d)
    h2, r0, r0sq = mlp_fwd(0, h1, y1)
    ykv, y2 = _rms_fwd("rms_kv_mix1", h2, [kvn, nm[1]])
    wkv_g, wq_g, wo_g = gather_end("attn", ag_attn, y2)
    wkv_f, wq_f, wo_f = wkv_g.reshape(d, kvw), wq_g.reshape(d, d), wo_g.reshape(d, d)
    (kv,) = _matmul("kv_proj", "nn", ykv, wkv_f, m=s, n=kvw, k=d, tm=1024, tn=kvw // 2, tk=512,
                    outs=[(BF16, "plain")], extras=tab_extras, epilogue=ep_rope_k)
    (q,) = _matmul("q_proj", "nn", y2, wq_f, m=s, n=d, k=d, tm=1024, tn=1024, tk=512,
                   outs=[(BF16, "plain")], extras=tab_extras, epilogue=ep_rope)
    o_parts, lse_parts = [], []
    for dil in DILATIONS:
        o_b, lse_b = _attn_fwd(f"attn_fwd_d{dil}", _to_branch(q, dil), _to_branch(kv, dil), dil, d)
        o_parts.append(o_b.reshape(s, d))
        lse_parts.append(_heads_from_branch(lse_b, dil, group))
    o, lse = _attn_combine(o_parts, lse_parts)
    (h3,) = _matmul("attn_out", "nn", o, wo_f, m=s, n=d, k=d, tm=1024, tn=1024, tk=512,
                    outs=[(F32, "plain")], extras=[(h2, "ij")], epilogue=ep_residual)
    (y3,) = _rms_fwd("rms_mlp1", h3, [nmlp[1]])
    wmi_g[1], wmo1_g = gather_end("mlp1", ag_mlp1, y3)
    wmo_f[1] = wmo1_g.reshape(dff, d)
    h4, r1, r1sq = mlp_fwd(1, h3, y3)
    dh4, dh4b, d_fin, loss_cols = _final_loss(h4, fin, target)

    def ep_relu2_bwd(acc, ex, outs, j):
        outs[0][...] = (acc * (2.0 * ex[0][...].astype(F32))).astype(BF16)

    def mlp_bwd(idx, dhb, y, r, r2):
        (dz,) = _matmul(f"mlp_out{idx}_dx", "nt", dhb, wmo_f[idx], m=s, n=dff, k=d, tm=1024, tn=1024, tk=512,
                        outs=[(BF16, "plain")], extras=[(r, "ij")], epilogue=ep_relu2_bwd)
        (dwo,) = _matmul(f"mlp_out{idx}_dw", "tn", r2, dhb, m=dff, n=d, k=s, tm=1024, tn=1024, tk=512,
                         outs=[(BF16, "plain")])
        (dy,) = _matmul(f"mlp_in{idx}_dx", "nt", dz, wmi_g[idx], b_kind="col", m=s, n=d, k=dff,
                        tm=1024, tn=1024, tk=512, outs=[(BF16, "plain")])
        (dwi,) = _matmul(f"mlp_in{idx}_dw", "tn", y, dz, m=d, n=dff, k=s, tm=1024, tn=1024, tk=512,
                         outs=[(BF16, "col")])
        return dy, dwi, dwo.reshape(N_SHARD, dff // N_SHARD, d)

    def rs_begin(tag, names, grads):
        recv_half = _comm_sibling_halves(tag, grads)
        sums = [_chip_sum(f"chip_sum_{nme}", g, rh, place) for nme, g, rh in zip(names, grads, recv_half)]
        return _owners_start(tag, sums)

    def after(vec, handle):
        return vec + handle[4][0:1, 0:1]

    def rs_end(tag, names, handle, later):
        sums, pieces = _owners_wait(tag, handle, later)
        own = [_owner_sum(f"owner_sum_{nme}", cs, rp, place) for nme, cs, rp in zip(names, sums, pieces)]
        return _comm_swap_halves(tag, own)

    dy3, g_wmi1, g_wmo1 = mlp_bwd(1, dh4b, y3, r1, r1sq)
    rs_mlp1 = rs_begin("mlp1", ["mlp_in1", "mlp_out1"], [g_wmi1, g_wmo1])
    dh3, dh3b, d_nmlp1 = _rms_bwd("rms_mlp1_bwd", h3, [(after(nmlp[1], rs_mlp1), dy3)], dh4)

    (do,) = _matmul("attn_out_dx", "nt", dh3b, wo_f, m=s, n=d, k=d, tm=1024, tn=1024, tk=512, outs=[(BF16, "plain")])
    (g_wo,) = _matmul("attn_out_dw", "tn", o, dh3b, m=d, n=d, k=s, tm=1024, tn=1024, tk=512, outs=[(BF16, "plain")])
    delta = _attn_delta(do, o)
    dq_parts, dk_parts, dv_parts = [], [], []
    for dil in DILATIONS:
        dq_b, dk_b, dv_b = _attn_bwd(
            f"attn_bwd_d{dil}", _to_branch(q, dil), _to_branch(kv, dil), _to_branch(do, dil),
            _heads_to_branch(lse, dil, group), _heads_to_branch(delta, dil, group), dil, d)
        dq_parts.append(dq_b.reshape(s, d))
        dk_parts.append(dk_b.reshape(s, kvw // 2))
        dv_parts.append(dv_b.reshape(s, kvw // 2))
    dq = _rope_bwd_sum("rope_bwd_q", dq_parts, tabs, d)
    dkv_parts = [jnp.concatenate([a, b], axis=1) for a, b in zip(dk_parts, dv_parts)]
    dkv = _rope_bwd_sum("rope_bwd_kv", dkv_parts, tabs, kvw // 2)
    (g_wq,) = _matmul("q_proj_dw", "tn", y2, dq, m=d, n=d, k=s, tm=1024, tn=1024, tk=512, outs=[(BF16, "plain")])
    (dy2,) = _matmul("q_proj_dx", "nt", dq, wq_f, m=s, n=d, k=d, tm=1024, tn=1024, tk=512, outs=[(BF16, "plain")])
    (g_wkv,) = _matmul("kv_proj_dw", "tn", ykv, dkv, m=d, n=kvw, k=s, tm=1024, tn=1024, tk=512, outs=[(BF16, "plain")])
    (dykv,) = _matmul("kv_proj_dx", "nt", dkv, wkv_f, m=s, n=d, k=kvw, tm=1024, tn=1024, tk=512, outs=[(BF16, "plain")])
    rs_attn = rs_begin("attn", ["w_kv", "w_q", "w_o"],
                       [g_wkv.reshape(N_SHARD, ds4, kvw), g_wq.reshape(N_SHARD, ds4, d), g_wo.reshape(N_SHARD, ds4, d)])
    dh2, dh2b, d_nm1, d_kvn = _rms_bwd("rms_kv_mix1_bwd", h2, [(after(nm[1], rs_attn), dy2), (kvn, dykv)], dh3)

    dy1, g_wmi0, g_wmo0 = mlp_bwd(0, dh2b, y1, r0, r0sq)
    rs_mlp0 = rs_begin("mlp0", ["mlp_in0", "mlp_out0"], [g_wmi0, g_wmo0])
    dh1, dh1b, d_nmlp0, d_b_out = _rms_bwd("rms_mlp0_bwd", h1, [(after(nmlp[0], rs_mlp0), dy1)], dh2,
                                           want_colsum=True)

    (dsact,) = _matmul("conv_out_dx", "nt", dh1b, w_out_f, m=s, n=d, k=d, tm=1024, tn=1024, tk=512, outs=[(BF16, "plain")])
    (g_wout,) = _matmul("conv_out_dw", "tn", sact, dh1b, m=d, n=d, k=s, tm=1024, tn=1024, tk=512, outs=[(BF16, "plain")])
    dc, d_ln_g, d_ln_b, d_b_dw = _ln_silu_bwd(cpre, ln_g_f, ln_b_f, dsact)
    da, dgt, d_w_dw, d_b_in_a, d_b_in_g = _dwconv_bwd(u, w_dw_f, dc)
    du = jnp.concatenate([da, dgt], axis=1)
    (g_win,) = _matmul("conv_in_dw", "tn", y0, du, m=d, n=2 * d, k=s, tm=1024, tn=tn_u, tk=512, outs=[(BF16, "col")])
    rs_conv = rs_begin("conv", ["w_in", "w_out"], [g_win, g_wout.reshape(N_SHARD, ds4, d)])
    (dy0,) = _matmul("conv_in_dx", "nt", du, w_in_g, b_kind="col", m=s, n=d, k=2 * d, tm=1024, tn=1024, tk=tn_u,
                     outs=[(BF16, "plain")])
    dx, _, d_nm0 = _rms_bwd("rms_mix0_bwd", h0, [(after(nm[0], rs_conv), dy0)], dh1)

    small_rows = [(0, d_nm0), (1, d_nm1), (2, d_nmlp0), (3, d_nmlp1), (4, d_kvn), (5, d_fin), (6, d_b_dw),
                  (7, d_ln_g), (8, d_ln_b), (9, d_b_out), (10, d_b_in_a), (11, d_b_in_g), (12, loss_cols)]
    red = _comm_small_allreduce(small_rows, d_w_dw, d)
    loss = red[12, 0]
    g_norm_mix = red[0:2]
    g_norm_mlp = red[2:4]
    g_kv_norm = red[4:5]
    g_final = red[5:6]

    def my_cols(row):
        return lax.dynamic_slice(red, (row, me * ds4), (1, ds4))

    g_b_dw, g_ln_g, g_ln_b, g_b_out = my_cols(6), my_cols(7), my_cols(8), my_cols(9)
    half_in = 2 * d // N_SHARD
    b_in_row = 10 + me // 2
    g_b_in = lax.dynamic_slice(red, (b_in_row, (me % 2) * half_in), (1, half_in))
    g_w_dw = lax.dynamic_slice(red, (16, me * ds4), (CONV_WIDTH, ds4))

    def big(name, w, m, v, g, layer=0, partial=None):
        shape = w.shape
        w3, m3, v3 = [t.reshape((-1,) + shape[-2:]) for t in (w, m, v)]
        if partial is not None:
            partial = [t.reshape(w3.shape) for t in partial]
        res = _adamw(name, w3, m3, v3, g, layer, partial)
        return [t.reshape(shape) for t in res]

    f_wmi1, f_wmo1 = rs_end("mlp1", ["mlp_in1", "mlp_out1"], rs_mlp1, rs_conv[4])
    p_wmi = big("adam_mlp_in1", mlp_w_in, m_mlp_w_in, v_mlp_w_in, f_wmi1, 1)
    p_wmo = big("adam_mlp_out1", mlp_w_out, m_mlp_w_out, v_mlp_w_out, f_wmo1, 1)
    f_wkv, f_wq, f_wo = rs_end("attn", ["w_kv", "w_q", "w_o"], rs_attn, p_wmo[0])
    r_wkv = big("adam_w_kv", w_kv, m_w_kv, v_w_kv, f_wkv)
    r_wq = big("adam_w_q", attn_w_q, m_attn_w_q, v_attn_w_q, f_wq)
    r_wo = big("adam_w_o", attn_w_o, m_attn_w_o, v_attn_w_o, f_wo)
    f_wmi0, f_wmo0 = rs_end("mlp0", ["mlp_in0", "mlp_out0"], rs_mlp0, r_wo[0])
    r_wmi = big("adam_mlp_in0", mlp_w_in, m_mlp_w_in, v_mlp_w_in, f_wmi0, 0, p_wmi)
    r_wmo = big("adam_mlp_out0", mlp_w_out, m_mlp_w_out, v_mlp_w_out, f_wmo0, 0, p_wmo)
    f_win, f_wout = rs_end("conv", ["w_in", "w_out"], rs_conv, r_wmo[0])
    r_win = big("adam_w_in", conv_w_in, m_conv_w_in, v_conv_w_in, f_win)
    r_wout = big("adam_w_out", conv_w_out, m_conv_w_out, v_conv_w_out, f_wout)

    sm_w = [norm_mix, norm_mlp, conv_b_in, conv_w_dw.reshape(CONV_WIDTH, ds4), conv_b_dw, conv_ln_g, conv_ln_b,
            conv_b_out, kv_norm.reshape(1, d), final_norm.reshape(1, d)]
    sm_m = [m_norm_mix, m_norm_mlp, m_conv_b_in, m_conv_w_dw.reshape(CONV_WIDTH, ds4), m_conv_b_dw, m_conv_ln_g,
            m_conv_ln_b, m_conv_b_out, m_kv_norm.reshape(1, d), m_final_norm.reshape(1, d)]
    sm_v = [v_norm_mix, v_norm_mlp, v_conv_b_in, v_conv_w_dw.reshape(CONV_WIDTH, ds4), v_conv_b_dw, v_conv_ln_g,
            v_conv_ln_b, v_conv_b_out, v_kv_norm.reshape(1, d), v_final_norm.reshape(1, d)]
    sm_g = [g_norm_mix, g_norm_mlp, g_b_in, g_w_dw, g_b_dw, g_ln_g, g_ln_b, g_b_out, g_kv_norm, g_final]
    sm_d, sm_nm, sm_nv = _adam_small(sm_w, sm_m, sm_v, sm_g)
    shapes = [norm_mix.shape, norm_mlp.shape, conv_b_in.shape, conv_w_dw.shape, conv_b_dw.shape, conv_ln_g.shape,
              conv_ln_b.shape, conv_b_out.shape, kv_norm.shape, final_norm.shape]
    sm_g, sm_d, sm_nm, sm_nv = [[t.reshape(sh) for t, sh in zip(lst, shapes)] for lst in (sm_g, sm_d, sm_nm, sm_nv)]

    def order(sm, idx):
        return [sm[0], sm[1], r_win[idx], sm[2], sm[3], sm[4], sm[5], sm[6], r_wout[idx], sm[7], sm[8],
                r_wkv[idx], r_wq[idx], r_wo[idx], r_wmi[idx], r_wmo[idx], sm[9]]

    return (loss, dx.reshape(x.shape), *order(sm_g, 0), *order(sm_d, 1), *order(sm_nm, 2), *order(sm_nv, 3))
```

```python
import math

import jax
import jax.numpy as jnp
from jax import lax
from jax.experimental import pallas as pl
from jax.experimental.pallas import tpu as pltpu

F32 = jnp.float32
BF16 = jnp.bfloat16
I32 = jnp.int32

NORM_EPS = 1e-6
LN_EPS = 1e-5
HEAD_DIM = 128
N_KV_HEADS = 4
ROT_DIM = 32
ROPE_THETA = 500000.0
CONV_WIDTH = 31
CONV_PAD = 32
ATT_BLOCK = 128
DILATIONS = (1, 4, 16)
ADAM_LR = 0.001
ADAM_B1 = 0.9
ADAM_B2 = 0.999
ADAM_EPS = 1e-08
ADAM_WD = 0.01
ADAM_STEP = 10
N_SHARD = 4
N_DEV = 8
LANES = 128
VMEM_LIMIT = 48 * 1024 * 1024
MM_TM, MM_TN, MM_TK = 1024, 1024, 2048
ROW_TILE = 256
CONV_CB = 128
CONV_T = 128
SMALL_ROWS = 48
MESH = pl.DeviceIdType.MESH
ANY = pl.BlockSpec(memory_space=pl.ANY)
HBM = pl.BlockSpec(memory_space=pltpu.HBM)
SEM = pl.BlockSpec(memory_space=pltpu.SEMAPHORE)
SPLIT_EFFECT = pltpu.SideEffectType.DATAFLOW_SIDE_EFFECTING


def _params(*sem):
    return pltpu.CompilerParams(dimension_semantics=sem, vmem_limit_bytes=VMEM_LIMIT)


def _sigmoid(x):
    return 1.0 / (1.0 + jnp.exp(-x))


def _wspec(kind, arr_shape, br, bc, pick):
    if kind == "plain":
        return pl.BlockSpec((br, bc), pick)
    per = arr_shape[2] // bc

    def idx(*g):
        rb, cb = pick(*g)
        return (cb // per, rb, cb % per)

    return pl.BlockSpec((None, br, bc), idx)


def _matmul(name, mode, a, b, *, m, n, k, tn=MM_TN, b_kind="plain", outs, extras=(), epilogue=None):
    tm, tn, tk = min(MM_TM, m), min(tn, n), min(MM_TK, k)
    if b_kind == "col" and mode == "nn":
        tn = min(tn, n // N_SHARD)
    if b_kind == "col" and mode == "nt":
        tk = min(tk, k // N_SHARD)
    if any(kind == "col" for _, kind in outs):
        tn = min(tn, n // N_SHARD)
    assert m % tm == 0 and n % tn == 0 and k % tk == 0, (name, m, n, k, tm, tn, tk)
    nk = k // tk
    grid = (m // tm, n // tn, nk)
    if mode == "nn":
        a_spec = pl.BlockSpec((tm, tk), lambda i, j, kk: (i, kk))
        b_spec = _wspec(b_kind, b.shape, tk, tn, lambda i, j, kk: (kk, j))
        dims = (((1,), (0,)), ((), ()))
    elif mode == "nt":
        a_spec = pl.BlockSpec((tm, tk), lambda i, j, kk: (i, kk))
        b_spec = _wspec(b_kind, b.shape, tn, tk, lambda i, j, kk: (j, kk))
        dims = (((1,), (1,)), ((), ()))
    else:
        a_spec = pl.BlockSpec((tk, tm), lambda i, j, kk: (kk, i))
        b_spec = pl.BlockSpec((tk, tn), lambda i, j, kk: (kk, j))
        dims = (((0,), (0,)), ((), ()))
    out_shape, out_specs = [], []
    for dtype, kind in outs:
        shape = (m, n) if kind == "plain" else (N_SHARD, m, n // N_SHARD)
        out_shape.append(jax.ShapeDtypeStruct(shape, dtype))
        out_specs.append(_wspec(kind, shape, tm, tn, lambda i, j, kk: (i, j)))
    n_ex = len(extras)
    ex_specs = {"ij": pl.BlockSpec((tm, tn), lambda i, j, kk: (i, j)),
                "vec": pl.BlockSpec((1, tn), lambda i, j, kk: (0, j)),
                "rows": pl.BlockSpec((tm, LANES), lambda i, j, kk: (i, 0))}

    def body(*refs):
        a_ref, b_ref = refs[0], refs[1]
        ex_refs = refs[2:2 + n_ex]
        out_refs = refs[2 + n_ex:2 + n_ex + len(outs)]
        j = pl.program_id(1)

        def finish(res):
            if epilogue is None:
                out_refs[0][...] = res.astype(out_refs[0].dtype)
            else:
                epilogue(res, ex_refs, out_refs, j)

        prod = lax.dot_general(a_ref[...], b_ref[...], dims, preferred_element_type=F32)
        if nk == 1:
            finish(prod)
            return
        acc_ref = refs[-1]
        kk = pl.program_id(2)

        @pl.when(kk == 0)
        def _():
            acc_ref[...] = prod

        @pl.when(kk > 0)
        def _():
            acc_ref[...] += prod

        @pl.when(kk == nk - 1)
        def _():
            finish(acc_ref[...])

    res = pl.pallas_call(
        body, name=name, grid=grid,
        in_specs=[a_spec, b_spec] + [ex_specs[how] for _, how in extras],
        out_specs=out_specs, out_shape=out_shape,
        scratch_shapes=[pltpu.VMEM((tm, tn), F32)] if nk > 1 else [],
        compiler_params=_params("parallel", "parallel", "arbitrary"),
    )(a, b, *[e for e, _ in extras])
    return res


def _rope_tables(seq):
    half = ROT_DIM // 2
    pos = jnp.arange(seq, dtype=F32)
    inv = ROPE_THETA ** (-jnp.arange(0, ROT_DIM, 2, dtype=F32) / ROT_DIM)
    ang = pos[:, None] * inv[None, :]
    cos, sin = jnp.cos(ang), jnp.sin(ang)
    zeros = jnp.zeros((seq, HEAD_DIM - ROT_DIM), F32)
    ctab = jnp.concatenate([cos, cos, zeros + 1.0], axis=1)
    atab = jnp.concatenate([-sin, jnp.zeros((seq, half), F32), zeros], axis=1)
    btab = jnp.concatenate([jnp.zeros((seq, half), F32), sin, zeros], axis=1)
    return ctab, atab, btab


def _rope_apply(x, ctab, atab, btab, sign):
    w = x.shape[1]
    reps = w // HEAD_DIM
    half = ROT_DIM // 2
    c = jnp.tile(ctab, (1, reps))
    a = jnp.tile(atab, (1, reps))
    b = jnp.tile(btab, (1, reps))
    up = pltpu.roll(x, w - half, 1)
    down = pltpu.roll(x, half, 1)
    return x * c + sign * (up * a + down * b)


def _rows(t, w):
    return pl.BlockSpec((t, w), lambda i: (i, 0))


def _fixed(shape):
    nd = len(shape)
    return pl.BlockSpec(shape, lambda i: (0,) * nd)


def _rms_fwd(name, x, gains):
    s, d = x.shape
    t = min(ROW_TILE, s)
    ng = len(gains)

    def body(x_ref, *refs):
        xv = x_ref[...]
        r = lax.rsqrt(jnp.mean(xv * xv, axis=-1, keepdims=True) + NORM_EPS)
        xn = xv * r
        for g_ref, y_ref in zip(refs[:ng], refs[ng:]):
            y_ref[...] = (xn * g_ref[...]).astype(BF16)

    return pl.pallas_call(
        body, name=name, grid=(s // t,),
        in_specs=[_rows(t, d)] + [_fixed((1, d))] * ng,
        out_specs=[_rows(t, d)] * ng,
        out_shape=[jax.ShapeDtypeStruct((s, d), BF16)] * ng,
        compiler_params=_params("parallel"),
    )(x, *gains)


def _rms_bwd(name, x, pairs, dh_in, want_colsum=False):
    s, d = x.shape
    t = min(ROW_TILE, s)
    n_p = len(pairs)

    def body(x_ref, dh_ref, *refs):
        g_refs = refs[:n_p]
        dy_refs = refs[n_p:2 * n_p]
        dh_out, dhb_out = refs[2 * n_p], refs[2 * n_p + 1]
        dg_refs = refs[2 * n_p + 2:2 * n_p + 2 + n_p]
        cs_ref = refs[-1] if want_colsum else None
        i = pl.program_id(0)
        xv = x_ref[...]
        r = lax.rsqrt(jnp.mean(xv * xv, axis=-1, keepdims=True) + NORM_EPS)
        xn = xv * r
        dh = dh_ref[...]
        for g_ref, dy_ref, dg_ref in zip(g_refs, dy_refs, dg_refs):
            dy = dy_ref[...].astype(F32)
            u = dy * g_ref[...]
            dh = dh + r * (u - xn * jnp.mean(u * xn, axis=-1, keepdims=True))
            part = jnp.sum(dy * xn, axis=0, keepdims=True)

            @pl.when(i == 0)
            def _():
                dg_ref[...] = part

            @pl.when(i > 0)
            def _():
                dg_ref[...] += part

        dh_out[...] = dh
        dhb_out[...] = dh.astype(BF16)
        if want_colsum:
            col = jnp.sum(dh, axis=0, keepdims=True)

            @pl.when(i == 0)
            def _():
                cs_ref[...] = col

            @pl.when(i > 0)
            def _():
                cs_ref[...] += col

    n_vec = n_p + (1 if want_colsum else 0)
    return pl.pallas_call(
        body, name=name, grid=(s // t,),
        in_specs=[_rows(t, d), _rows(t, d)] + [_fixed((1, d))] * n_p + [_rows(t, d)] * n_p,
        out_specs=[_rows(t, d), _rows(t, d)] + [_fixed((1, d))] * n_vec,
        out_shape=[jax.ShapeDtypeStruct((s, d), F32), jax.ShapeDtypeStruct((s, d), BF16)]
        + [jax.ShapeDtypeStruct((1, d), F32)] * n_vec,
        compiler_params=_params("arbitrary"),
    )(x, dh_in, *[g for g, _ in pairs], *[dy for _, dy in pairs])


def _final_loss(x, g, target):
    s, d = x.shape
    t = min(ROW_TILE, s)

    def body(x_ref, g_ref, t_ref, dh_out, dhb_out, dg_ref, loss_ref):
        i = pl.program_id(0)
        xv = x_ref[...]
        gv = g_ref[...]
        r = lax.rsqrt(jnp.mean(xv * xv, axis=-1, keepdims=True) + NORM_EPS)
        xn = xv * r
        diff = xn * gv - t_ref[...]
        dy = diff / d
        u = dy * gv
        dh = r * (u - xn * jnp.mean(u * xn, axis=-1, keepdims=True))
        dh_out[...] = dh
        dhb_out[...] = dh.astype(BF16)
        dg = jnp.sum(dy * xn, axis=0, keepdims=True)
        lc = jnp.sum(0.5 * diff * dy, axis=0, keepdims=True)

        @pl.when(i == 0)
        def _():
            dg_ref[...] = dg
            loss_ref[...] = lc

        @pl.when(i > 0)
        def _():
            dg_ref[...] += dg
            loss_ref[...] += lc

    return pl.pallas_call(
        body, name="final_loss", grid=(s // t,),
        in_specs=[_rows(t, d), _fixed((1, d)), _rows(t, d)],
        out_specs=[_rows(t, d), _rows(t, d), _fixed((1, d)), _fixed((1, d))],
        out_shape=[jax.ShapeDtypeStruct((s, d), F32), jax.ShapeDtypeStruct((s, d), BF16),
                   jax.ShapeDtypeStruct((1, d), F32), jax.ShapeDtypeStruct((1, d), F32)],
        compiler_params=_params("arbitrary"),
    )(x, g, target)


def _ln_silu_fwd(c, g, b):
    s, d = c.shape
    t = min(ROW_TILE, s)

    def body(c_ref, g_ref, b_ref, s_ref):
        cv = c_ref[...]
        mu = jnp.mean(cv, axis=-1, keepdims=True)
        xc = cv - mu
        rs = lax.rsqrt(jnp.mean(xc * xc, axis=-1, keepdims=True) + LN_EPS)
        ln = xc * rs * g_ref[...] + b_ref[...]
        s_ref[...] = (ln * _sigmoid(ln)).astype(BF16)

    return pl.pallas_call(
        body, name="ln_silu_fwd", grid=(s // t,),
        in_specs=[_rows(t, d), _fixed((1, d)), _fixed((1, d))],
        out_specs=_rows(t, d), out_shape=jax.ShapeDtypeStruct((s, d), BF16),
        compiler_params=_params("parallel"),
    )(c, g, b)


def _ln_silu_bwd(c, g, b, ds):
    s, d = c.shape
    t = min(ROW_TILE, s)

    def body(c_ref, g_ref, b_ref, ds_ref, dc_ref, dg_ref, db_ref, dbdw_ref):
        i = pl.program_id(0)
        cv = c_ref[...]
        gv = g_ref[...]
        mu = jnp.mean(cv, axis=-1, keepdims=True)
        xc = cv - mu
        rs = lax.rsqrt(jnp.mean(xc * xc, axis=-1, keepdims=True) + LN_EPS)
        nrm = xc * rs
        ln = nrm * gv + b_ref[...]
        sig = _sigmoid(ln)
        dln = ds_ref[...].astype(F32) * sig * (1.0 + ln * (1.0 - sig))
        dn = dln * gv
        dc = rs * (dn - jnp.mean(dn, axis=-1, keepdims=True)
                   - nrm * jnp.mean(dn * nrm, axis=-1, keepdims=True))
        dc_ref[...] = dc
        pg = jnp.sum(dln * nrm, axis=0, keepdims=True)
        pb = jnp.sum(dln, axis=0, keepdims=True)
        pc = jnp.sum(dc, axis=0, keepdims=True)

        @pl.when(i == 0)
        def _():
            dg_ref[...] = pg
            db_ref[...] = pb
            dbdw_ref[...] = pc

        @pl.when(i > 0)
        def _():
            dg_ref[...] += pg
            db_ref[...] += pb
            dbdw_ref[...] += pc

    return pl.pallas_call(
        body, name="ln_silu_bwd", grid=(s // t,),
        in_specs=[_rows(t, d), _fixed((1, d)), _fixed((1, d)), _rows(t, d)],
        out_specs=[_rows(t, d)] + [_fixed((1, d))] * 3,
        out_shape=[jax.ShapeDtypeStruct((s, d), F32)] + [jax.ShapeDtypeStruct((1, d), F32)] * 3,
        compiler_params=_params("arbitrary"),
    )(c, g, b, ds)


def _attn_combine(o_list, lse_list):
    s, d = o_list[0].shape
    nh = d // HEAD_DIM
    t = min(ROW_TILE, s)
    nb = len(o_list)

    def body(*refs):
        o_refs = refs[:nb]
        l_refs = refs[nb:2 * nb]
        o_out, l_out = refs[2 * nb], refs[2 * nb + 1]
        ls = [r[...] for r in l_refs]
        mx = ls[0]
        for l in ls[1:]:
            mx = jnp.maximum(mx, l)
        es = [jnp.exp(l - mx) for l in ls]
        den = es[0]
        for e in es[1:]:
            den = den + e
        l_out[...] = mx + jnp.log(den)
        ws = [e / den for e in es]
        for h in range(nh):
            cols = slice(h * HEAD_DIM, (h + 1) * HEAD_DIM)
            acc = jnp.zeros((t, HEAD_DIM), F32)
            for o_ref, w in zip(o_refs, ws):
                acc = acc + w[:, h:h + 1] * o_ref[:, cols].astype(F32)
            o_out[:, cols] = acc.astype(BF16)

    return pl.pallas_call(
        body, name="attn_combine", grid=(s // t,),
        in_specs=[_rows(t, d)] * nb + [_rows(t, nh)] * nb,
        out_specs=[_rows(t, d), _rows(t, nh)],
        out_shape=[jax.ShapeDtypeStruct((s, d), BF16), jax.ShapeDtypeStruct((s, nh), F32)],
        compiler_params=_params("parallel"),
    )(*o_list, *lse_list)


def _attn_delta(do, o):
    s, d = o.shape
    nh = d // HEAD_DIM
    t = min(ROW_TILE, s)

    def body(do_ref, o_ref, dl_ref):
        lane = lax.broadcasted_iota(I32, (t, nh), 1)
        out = jnp.zeros((t, nh), F32)
        for h in range(nh):
            cols = slice(h * HEAD_DIM, (h + 1) * HEAD_DIM)
            v = jnp.sum(do_ref[:, cols].astype(F32) * o_ref[:, cols].astype(F32), axis=-1, keepdims=True)
            out = jnp.where(lane == h, v, out)
        dl_ref[...] = out

    return pl.pallas_call(
        body, name="attn_delta", grid=(s // t,),
        in_specs=[_rows(t, d), _rows(t, d)],
        out_specs=_rows(t, nh), out_shape=jax.ShapeDtypeStruct((s, nh), F32),
        compiler_params=_params("parallel"),
    )(do, o)


def _rope_bwd_sum(name, parts, tabs, rope_cols):
    s, w = parts[0].shape
    t = min(ROW_TILE, s)
    n_p = len(parts)

    def body(*refs):
        p_refs = refs[:n_p]
        c_ref, a_ref, b_ref = refs[n_p:n_p + 3]
        out = refs[-1]
        tot = p_refs[0][...].astype(F32)
        for p in p_refs[1:]:
            tot = tot + p[...].astype(F32)
        rot = _rope_apply(tot[:, :rope_cols], c_ref[...], a_ref[...], b_ref[...], -1.0)
        out[:, :rope_cols] = rot.astype(BF16)
        if rope_cols < w:
            out[:, rope_cols:] = tot[:, rope_cols:].astype(BF16)

    return pl.pallas_call(
        body, name=name, grid=(s // t,),
        in_specs=[_rows(t, w)] * n_p + [_rows(t, HEAD_DIM)] * 3,
        out_specs=_rows(t, w), out_shape=jax.ShapeDtypeStruct((s, w), BF16),
        compiler_params=_params("parallel"),
    )(*parts, *tabs)


def _dwconv_fwd(u, w_dw, b_dw):
    s, d2 = u.shape
    d = d2 // 2
    cb = min(CONV_CB, d)
    nblk = d // cb
    tt = min(CONV_T, s)

    def body(ua_ref, ug_ref, w_ref, b_ref, c_ref, xp_ref):
        gl = ua_ref[...].astype(F32) * _sigmoid(ug_ref[...].astype(F32))
        xp_ref[0:CONV_PAD, :] = jnp.zeros((CONV_PAD, cb), F32)
        xp_ref[CONV_PAD:, :] = gl
        wv = w_ref[...]
        bv = b_ref[...]
        for t0 in range(0, s, tt):
            acc = jnp.zeros((tt, cb), F32) + bv
            for kk in range(CONV_WIDTH):
                off = t0 + CONV_PAD - (CONV_WIDTH - 1) + kk
                acc = acc + wv[kk:kk + 1, :] * xp_ref[off:off + tt, :]
            c_ref[t0:t0 + tt, :] = acc

    return pl.pallas_call(
        body, name="dwconv_fwd", grid=(nblk,),
        in_specs=[pl.BlockSpec((s, cb), lambda j: (0, j)), pl.BlockSpec((s, cb), lambda j: (0, j + nblk)),
                  pl.BlockSpec((CONV_PAD, cb), lambda j: (0, j)), pl.BlockSpec((1, cb), lambda j: (0, j))],
        out_specs=pl.BlockSpec((s, cb), lambda j: (0, j)),
        out_shape=jax.ShapeDtypeStruct((s, d), F32),
        scratch_shapes=[pltpu.VMEM((s + CONV_PAD, cb), F32)],
        compiler_params=_params("parallel"),
    )(u, u, w_dw, b_dw)


def _dwconv_bwd(u, w_dw, dc):
    s, d2 = u.shape
    d = d2 // 2
    cb = min(CONV_CB, d)
    nblk = d // cb
    tt = min(CONV_T, s)

    def body(ua_ref, ug_ref, w_ref, dc_ref, da_ref, dgt_ref, dw_ref, dba_ref, dbg_ref, glp_ref, dcp_ref, acc_ref):
        a = ua_ref[...].astype(F32)
        sig = _sigmoid(ug_ref[...].astype(F32))
        glp_ref[0:CONV_PAD, :] = jnp.zeros((CONV_PAD, cb), F32)
        glp_ref[CONV_PAD:, :] = a * sig
        dcp_ref[0:s, :] = dc_ref[...]
        dcp_ref[s:, :] = jnp.zeros((CONV_PAD, cb), F32)
        acc_ref[...] = jnp.zeros_like(acc_ref)
        wv = w_ref[...]
        dba = jnp.zeros((1, cb), F32)
        dbg = jnp.zeros((1, cb), F32)
        for t0 in range(0, s, tt):
            dgl = jnp.zeros((tt, cb), F32)
            dct = dc_ref[t0:t0 + tt, :]
            for kk in range(CONV_WIDTH):
                off = t0 + (CONV_WIDTH - 1) - kk
                dgl = dgl + wv[kk:kk + 1, :] * dcp_ref[off:off + tt, :]
                goff = t0 + CONV_PAD - (CONV_WIDTH - 1) + kk
                prod = dct * glp_ref[goff:goff + tt, :]
                acc_ref[8 * kk:8 * kk + 8, :] += jnp.sum(prod.reshape(tt // 8, 8, cb), axis=0)
            at = ua_ref[t0:t0 + tt, :].astype(F32)
            st = _sigmoid(ug_ref[t0:t0 + tt, :].astype(F32))
            da = dgl * st
            dg = dgl * at * st * (1.0 - st)
            da_ref[t0:t0 + tt, :] = da.astype(BF16)
            dgt_ref[t0:t0 + tt, :] = dg.astype(BF16)
            dba = dba + jnp.sum(da, axis=0, keepdims=True)
            dbg = dbg + jnp.sum(dg, axis=0, keepdims=True)
        dba_ref[...] = dba
        dbg_ref[...] = dbg
        for kk in range(CONV_WIDTH):
            dw_ref[kk:kk + 1, :] = jnp.sum(acc_ref[8 * kk:8 * kk + 8, :], axis=0, keepdims=True)
        dw_ref[CONV_WIDTH:, :] = jnp.zeros((CONV_PAD - CONV_WIDTH, cb), F32)

    blk = pl.BlockSpec((s, cb), lambda j: (0, j))
    vec = pl.BlockSpec((1, cb), lambda j: (0, j))
    return pl.pallas_call(
        body, name="dwconv_bwd", grid=(nblk,),
        in_specs=[blk, pl.BlockSpec((s, cb), lambda j: (0, j + nblk)),
                  pl.BlockSpec((CONV_PAD, cb), lambda j: (0, j)), blk],
        out_specs=[blk, blk, pl.BlockSpec((CONV_PAD, cb), lambda j: (0, j)), vec, vec],
        out_shape=[jax.ShapeDtypeStruct((s, d), BF16), jax.ShapeDtypeStruct((s, d), BF16),
                   jax.ShapeDtypeStruct((CONV_PAD, d), F32),
                   jax.ShapeDtypeStruct((1, d), F32), jax.ShapeDtypeStruct((1, d), F32)],
        scratch_shapes=[pltpu.VMEM((s + CONV_PAD, cb), F32), pltpu.VMEM((s + CONV_PAD, cb), F32),
                        pltpu.VMEM((8 * CONV_PAD, cb), F32)],
        compiler_params=_params("parallel"),
    )(u, u, w_dw, dc)


def _stack_heads(x, group):
    return jnp.concatenate([x[:, g * HEAD_DIM:(g + 1) * HEAD_DIM] for g in range(group)], axis=0)


def _unstack_heads(x, group):
    return jnp.concatenate([x[g * ATT_BLOCK:(g + 1) * ATT_BLOCK, :] for g in range(group)], axis=1)


def _stack_cols(x, group):
    return jnp.concatenate([x[:, g:g + 1] for g in range(group)], axis=0)


def _band_mask(nb, group):
    rows = group * ATT_BLOCK
    row = lax.broadcasted_iota(I32, (rows, 2 * ATT_BLOCK), 0) % ATT_BLOCK
    col = lax.broadcasted_iota(I32, (rows, 2 * ATT_BLOCK), 1)
    return (col >= row) & (col <= row + ATT_BLOCK) & ((col >= ATT_BLOCK) | (nb > 0))


def _window(ref, nb):
    prev = pl.multiple_of(jnp.maximum(nb - 1, 0) * ATT_BLOCK, ATT_BLOCK)
    cur = pl.multiple_of(nb * ATT_BLOCK, ATT_BLOCK)
    return jnp.concatenate([ref[pl.ds(prev, ATT_BLOCK), :], ref[pl.ds(cur, ATT_BLOCK), :]], axis=0)


def _attn_fwd(name, q, kv, dil, d):
    sd = q.shape[0]
    group = d // HEAD_DIM // N_KV_HEADS
    gw = group * HEAD_DIM
    nblk = sd // ATT_BLOCK
    scale = 1.0 / math.sqrt(HEAD_DIM)
    nt = (((1,), (1,)), ((), ()))

    def body(q_ref, k_ref, v_ref, o_ref, lse_ref):
        lane = lax.broadcasted_iota(I32, (ATT_BLOCK, group), 1)

        def step(nb, carry):
            rows = pl.ds(pl.multiple_of(nb * ATT_BLOCK, ATT_BLOCK), ATT_BLOCK)
            qs = _stack_heads(q_ref[rows, :], group)
            kw = _window(k_ref, nb)
            vw = _window(v_ref, nb)
            sc = lax.dot_general(qs, kw, nt, preferred_element_type=F32) * scale
            sc = jnp.where(_band_mask(nb, group), sc, -jnp.inf)
            mx = jnp.max(sc, axis=-1, keepdims=True)
            p = jnp.exp(sc - mx)
            l = jnp.sum(p, axis=-1, keepdims=True)
            o = jnp.dot(p.astype(BF16), vw, preferred_element_type=F32) / l
            o_ref[rows, :] = _unstack_heads(o, group).astype(BF16)
            lse = mx + jnp.log(l)
            out = jnp.zeros((ATT_BLOCK, group), F32)
            for g in range(group):
                out = jnp.where(lane == g, lse[g * ATT_BLOCK:(g + 1) * ATT_BLOCK, :], out)
            lse_ref[rows, :] = out
            return carry

        lax.fori_loop(0, nblk, step, 0)

    kvh = N_KV_HEADS
    return pl.pallas_call(
        body, name=name, grid=(dil, kvh),
        in_specs=[pl.BlockSpec((sd, gw), lambda r, h: (0, r * kvh + h)),
                  pl.BlockSpec((sd, HEAD_DIM), lambda r, h: (0, r * 2 * kvh + h)),
                  pl.BlockSpec((sd, HEAD_DIM), lambda r, h: (0, r * 2 * kvh + kvh + h))],
        out_specs=[pl.BlockSpec((sd, gw), lambda r, h: (0, r * kvh + h)),
                   pl.BlockSpec((None, sd, group), lambda r, h: (r * kvh + h, 0, 0))],
        out_shape=[jax.ShapeDtypeStruct((sd, dil * d), BF16),
                   jax.ShapeDtypeStruct((dil * kvh, sd, group), F32)],
        compiler_params=_params("parallel", "parallel"),
    )(q, kv, kv)


def _attn_bwd(name, q, kv, do, lse, delta, dil, d):
    sd = q.shape[0]
    group = d // HEAD_DIM // N_KV_HEADS
    gw = group * HEAD_DIM
    nblk = sd // ATT_BLOCK
    scale = 1.0 / math.sqrt(HEAD_DIM)
    nt = (((1,), (1,)), ((), ()))
    tn = (((0,), (0,)), ((), ()))

    def body(q_ref, k_ref, v_ref, do_ref, lse_ref, dl_ref, dq_ref, dk_ref, dv_ref, dk_acc, dv_acc):
        dk_acc[...] = jnp.zeros_like(dk_acc)
        dv_acc[...] = jnp.zeros_like(dv_acc)

        def step(nb, carry):
            rows = pl.ds(pl.multiple_of(nb * ATT_BLOCK, ATT_BLOCK), ATT_BLOCK)
            qs = _stack_heads(q_ref[rows, :], group)
            dos = _stack_heads(do_ref[rows, :], group)
            ls = _stack_cols(lse_ref[rows, :], group)
            dl = _stack_cols(dl_ref[rows, :], group)
            kw = _window(k_ref, nb)
            vw = _window(v_ref, nb)
            sc = lax.dot_general(qs, kw, nt, preferred_element_type=F32) * scale
            sc = jnp.where(_band_mask(nb, group), sc, -jnp.inf)
            p = jnp.exp(sc - ls)
            dp = lax.dot_general(dos, vw, nt, preferred_element_type=F32)
            ds = (p * (dp - dl) * scale).astype(BF16)
            dq = jnp.dot(ds, kw, preferred_element_type=F32)
            dq_ref[rows, :] = _unstack_heads(dq, group).astype(BF16)
            win = pl.ds(pl.multiple_of(nb * ATT_BLOCK, ATT_BLOCK), 2 * ATT_BLOCK)
            dk_acc[win, :] += lax.dot_general(ds, qs, tn, preferred_element_type=F32)
            dv_acc[win, :] += lax.dot_general(p.astype(BF16), dos, tn, preferred_element_type=F32)
            return carry

        lax.fori_loop(0, nblk, step, 0)
        dk_ref[...] = dk_acc[ATT_BLOCK:, :]
        dv_ref[...] = dv_acc[ATT_BLOCK:, :]

    kvh = N_KV_HEADS
    qspec = pl.BlockSpec((sd, gw), lambda r, h: (0, r * kvh + h))
    sspec = pl.BlockSpec((None, sd, group), lambda r, h: (r * kvh + h, 0, 0))
    kspec = pl.BlockSpec((sd, HEAD_DIM), lambda r, h: (0, r * kvh + h))
    return pl.pallas_call(
        body, name=name, grid=(dil, kvh),
        in_specs=[qspec,
                  pl.BlockSpec((sd, HEAD_DIM), lambda r, h: (0, r * 2 * kvh + h)),
                  pl.BlockSpec((sd, HEAD_DIM), lambda r, h: (0, r * 2 * kvh + kvh + h)),
                  qspec, sspec, sspec],
        out_specs=[qspec, kspec, kspec],
        out_shape=[jax.ShapeDtypeStruct((sd, dil * d), BF16),
                   jax.ShapeDtypeStruct((sd, dil * kvh * HEAD_DIM), F32),
                   jax.ShapeDtypeStruct((sd, dil * kvh * HEAD_DIM), F32)],
        scratch_shapes=[pltpu.VMEM((sd + ATT_BLOCK, HEAD_DIM), F32), pltpu.VMEM((sd + ATT_BLOCK, HEAD_DIM), F32)],
        compiler_params=_params("parallel", "parallel"),
    )(q, kv, kv, do, lse, delta)


def _to_branch(x, dil):
    s, w = x.shape
    return x.reshape(s // dil, dil * w)


def _heads_to_branch(x, dil, group):
    s = x.shape[0]
    x = x.reshape(s // dil, dil, N_KV_HEADS, group)
    return jnp.transpose(x, (1, 2, 0, 3)).reshape(dil * N_KV_HEADS, s // dil, group)


def _heads_from_branch(x, dil, group):
    sd = x.shape[1]
    x = x.reshape(dil, N_KV_HEADS, sd, group)
    return jnp.transpose(x, (2, 0, 1, 3)).reshape(sd * dil, N_KV_HEADS * group)


def _cast_bf16(name, w, layer, place):
    _, r, c = w.shape
    tr = min(512, r)

    def body(pl_ref, w_ref, o_ref):
        o_ref[...] = w_ref[...].astype(BF16)

    return pl.pallas_call(
        body, name=name,
        grid_spec=pltpu.PrefetchScalarGridSpec(
            num_scalar_prefetch=1, grid=(r // tr,),
            in_specs=[pl.BlockSpec((None, tr, c), lambda i, p: (layer, i, 0))],
            out_specs=pl.BlockSpec((None, tr, c), lambda i, p: (p[1], i, 0))),
        out_shape=jax.ShapeDtypeStruct((N_SHARD, r, c), BF16),
        compiler_params=_params("parallel"),
    )(place, w)


def _chip_sum(name, g, rh, place):
    _, r, c = g.shape
    rh2 = r // 2
    tr = min(512, rh2)
    nb = rh2 // tr

    def body(pl_ref, g_ref, rh_ref, o_ref):
        o_ref[...] = (g_ref[...].astype(F32) + rh_ref[...].astype(F32)).astype(BF16)

    return pl.pallas_call(
        body, name=name,
        grid_spec=pltpu.PrefetchScalarGridSpec(
            num_scalar_prefetch=1, grid=(N_SHARD, nb),
            in_specs=[pl.BlockSpec((None, tr, c), lambda s, i, p: (s, p[0] * nb + i, 0)),
                      pl.BlockSpec((None, tr, c), lambda s, i, p: (s, i, 0))],
            out_specs=pl.BlockSpec((None, tr, c), lambda s, i, p: (s, i, 0))),
        out_shape=jax.ShapeDtypeStruct((N_SHARD, rh2, c), BF16),
        compiler_params=_params("parallel", "parallel"),
    )(place, g, rh)


def _owner_sum(name, cs, rp, place):
    _, rh2, c = cs.shape
    tr = min(512, rh2)
    nb = rh2 // tr

    def body(pl_ref, cs_ref, r0_ref, r1_ref, r2_ref, o_ref):
        o_ref[...] = ((cs_ref[...].astype(F32) + r0_ref[...].astype(F32))
                      + (r1_ref[...].astype(F32) + r2_ref[...].astype(F32)))

    def rspec(j):
        return pl.BlockSpec((None, tr, c), lambda i, p: (j, i, 0))

    return pl.pallas_call(
        body, name=name,
        grid_spec=pltpu.PrefetchScalarGridSpec(
            num_scalar_prefetch=1, grid=(nb,),
            in_specs=[pl.BlockSpec((None, tr, c), lambda i, p: (p[1], i, 0)), rspec(0), rspec(1), rspec(2)],
            out_specs=pl.BlockSpec((tr, c), lambda i, p: (p[0] * nb + i, 0))),
        out_shape=jax.ShapeDtypeStruct((2 * rh2, c), F32),
        compiler_params=_params("parallel"),
    )(place, cs, rp, rp, rp)


def _adam_math(w, g, m, v):
    m = ADAM_B1 * m + (1.0 - ADAM_B1) * g
    v = ADAM_B2 * v + (1.0 - ADAM_B2) * (g * g)
    m_hat = m / (1.0 - ADAM_B1 ** ADAM_STEP)
    v_hat = v / (1.0 - ADAM_B2 ** ADAM_STEP)
    delta = -ADAM_LR * (m_hat / (jnp.sqrt(v_hat) + ADAM_EPS) + ADAM_WD * w)
    return delta, m, v


def _adamw(name, w, m, v, g, layer, partial=None):
    nl, r, c = w.shape
    tr = min(256, r)

    def body(w_ref, m_ref, v_ref, g_ref, *refs):
        go_ref, d_ref, mo_ref, vo_ref = refs[-4:]
        gv = g_ref[...]
        delta, m_new, v_new = _adam_math(w_ref[...], gv, m_ref[...], v_ref[...])
        go_ref[...] = gv
        d_ref[...] = delta
        mo_ref[...] = m_new
        vo_ref[...] = v_new

    wspec = pl.BlockSpec((None, tr, c), lambda i: (layer, i, 0))
    prev = [] if partial is None else list(partial)
    return pl.pallas_call(
        body, name=name, grid=(r // tr,),
        in_specs=[wspec] * 3 + [pl.BlockSpec((tr, c), lambda i: (i, 0))] + [ANY] * len(prev),
        out_specs=[wspec] * 4,
        out_shape=[jax.ShapeDtypeStruct((nl, r, c), F32)] * 4,
        input_output_aliases={4 + i: i for i in range(len(prev))},
        compiler_params=_params("parallel"),
    )(w, m, v, g, *prev)


def _adam_small(ws, ms, vs, gs):
    n = len(ws)

    def body(*refs):
        w_refs, m_refs, v_refs, g_refs = refs[:n], refs[n:2 * n], refs[2 * n:3 * n], refs[3 * n:4 * n]
        d_refs, mo_refs, vo_refs = refs[4 * n:5 * n], refs[5 * n:6 * n], refs[6 * n:7 * n]
        for i in range(n):
            delta, m_new, v_new = _adam_math(w_refs[i][...], g_refs[i][...], m_refs[i][...], v_refs[i][...])
            d_refs[i][...] = delta
            mo_refs[i][...] = m_new
            vo_refs[i][...] = v_new

    shapes = [jax.ShapeDtypeStruct(w.shape, F32) for w in ws]
    res = pl.pallas_call(body, name="adam_small", out_shape=shapes * 3)(*ws, *ms, *vs, *gs)
    return res[:n], res[n:2 * n], res[2 * n:]


def _pack_small(b_in, w_dw, b_dw, ln_g, ln_b, b_out, place):
    cin = b_in.shape[1]
    cd = b_dw.shape[1]
    rows = 8 + CONV_PAD

    def body(pl_ref, bi, wd, bd, lg, lb, bo, out):
        out[...] = jnp.zeros_like(out)
        out[0:1, :] = bi[...]
        out[1:2, 0:cd] = bd[...]
        out[1:2, cd:2 * cd] = lg[...]
        out[2:3, 0:cd] = lb[...]
        out[2:3, cd:2 * cd] = bo[...]
        out[8:8 + CONV_WIDTH, 0:cd] = wd[...]

    def whole(arr):
        return pl.BlockSpec(arr.shape, lambda i, p: (0,) * arr.ndim)

    ins = [b_in, w_dw, b_dw, ln_g, ln_b, b_out]
    return pl.pallas_call(
        body, name="pack_small",
        grid_spec=pltpu.PrefetchScalarGridSpec(
            num_scalar_prefetch=1, grid=(1,), in_specs=[whole(a) for a in ins],
            out_specs=pl.BlockSpec((None, rows, cin), lambda i, p: (p[1], 0, 0))),
        out_shape=jax.ShapeDtypeStruct((N_SHARD, rows, cin), F32),
        compiler_params=_params("arbitrary"),
    )(place, *ins)


def _place():
    x, y, c = lax.axis_index("x"), lax.axis_index("y"), lax.axis_index("c")
    return x, y, c


def _other_chips(x, y):
    return [(1 - x, y), (x, 1 - y), (1 - x, 1 - y)]


def _gather_copies(bufs, n_whole, ssem, rsem, landing):
    x, y, c = _place()
    me = 2 * x + y
    cps = []
    for a, ref in enumerate(bufs):
        whole = a >= len(bufs) - n_whole
        rh = ref.shape[1] // 2
        for j, (px, py) in enumerate(_other_chips(x, y)):
            shard = 2 * px + py if landing else me
            src = ref.at[me] if whole else ref.at[me, pl.ds(c * rh, rh)]
            dst = ref.at[shard] if whole else ref.at[shard, pl.ds(c * rh, rh)]
            cps.append(pltpu.make_async_remote_copy(
                src_ref=src, dst_ref=dst, send_sem=ssem.at[3 * a + j], recv_sem=rsem.at[3 * a + j],
                device_id=(px, py, c), device_id_type=MESH))
    return cps


def _gather_start(tag, bufs, n_whole=0):
    n = len(bufs)

    def body(*refs):
        for cp in _gather_copies(refs[:n], n_whole, refs[n], refs[n + 1], False):
            cp.start()
        refs[-1][...] = jnp.zeros_like(refs[-1])

    res = pl.pallas_call(
        body, name=f"gather_start_{tag}",
        out_shape=(pltpu.SemaphoreType.DMA((3 * n,)), pltpu.SemaphoreType.DMA((3 * n,)),
                   *[pltpu.HBM(b.shape, b.dtype) for b in bufs], jax.ShapeDtypeStruct((8, LANES), F32)),
        in_specs=[HBM] * n,
        out_specs=(SEM, SEM, *[HBM] * n, pl.BlockSpec(memory_space=pltpu.VMEM)),
        input_output_aliases={i: 2 + i for i in range(n)},
        compiler_params=pltpu.CompilerParams(has_side_effects=SPLIT_EFFECT),
    )(*[pltpu.with_memory_space_constraint(b, pltpu.HBM) for b in bufs])
    return res[0], res[1], list(res[2:2 + n]), res[-1]


def _gather_wait(tag, handle, n_whole, after):
    ssem, rsem, bufs, _ = handle
    n = len(bufs)

    def body(*refs):
        for cp in _gather_copies(refs[:n], n_whole, refs[n], refs[n + 1], True):
            cp.wait_send()
            cp.wait_recv()

    res = pl.pallas_call(
        body, name=f"gather_wait_{tag}",
        out_shape=[pltpu.HBM(b.shape, b.dtype) for b in bufs],
        in_specs=[HBM] * n + [SEM, SEM, ANY], out_specs=[HBM] * n,
        input_output_aliases={i: i for i in range(n)},
        compiler_params=pltpu.CompilerParams(has_side_effects=SPLIT_EFFECT),
    )(*bufs, ssem, rsem, after)
    return list(res)


def _comm_gather_forward(tag, bufs):
    n = len(bufs)

    def body(*refs):
        ins, outs = refs[:n], refs[n:2 * n]
        ssem, rsem = refs[2 * n:]
        x, y, c = _place()
        sends = []
        for a in range(n):
            rh = bufs[a].shape[1] // 2
            for j, (px, py) in enumerate(_other_chips(x, y)):
                rows = pl.ds(c * rh, rh)
                sends.append(pltpu.make_async_remote_copy(
                    src_ref=ins[a].at[2 * px + py, rows], dst_ref=outs[a].at[2 * px + py, rows],
                    send_sem=ssem.at[3 * a + j], recv_sem=rsem.at[3 * a + j],
                    device_id=(x, y, 1 - c), device_id_type=MESH))
        for cp in sends:
            cp.start()
        for a in range(n):
            rh = bufs[a].shape[1] // 2
            for j, (px, py) in enumerate(_other_chips(x, y)):
                rows = pl.ds((1 - c) * rh, rh)
                pltpu.make_async_remote_copy(
                    src_ref=ins[a].at[2 * px + py, rows], dst_ref=outs[a].at[2 * px + py, rows],
                    send_sem=ssem.at[3 * a + j], recv_sem=rsem.at[3 * a + j],
                    device_id=(x, y, 1 - c), device_id_type=MESH).wait_recv()
        for cp in sends:
            cp.wait_send()

    return pl.pallas_call(
        body, name=f"comm_gather_forward_{tag}",
        in_specs=[ANY] * n, out_specs=[ANY] * n,
        out_shape=[jax.ShapeDtypeStruct(b.shape, b.dtype) for b in bufs],
        input_output_aliases={a: a for a in range(n)},
        scratch_shapes=[pltpu.SemaphoreType.DMA((3 * n,)), pltpu.SemaphoreType.DMA((3 * n,))],
    )(*bufs)


def _comm_sibling_halves(tag, grads, later=None):
    n = len(grads)
    deps = [] if later is None else [later]

    def body(*refs):
        ins, outs = refs[:n], refs[n + len(deps):2 * n + len(deps)]
        ssem, rsem = refs[2 * n + len(deps):]
        x, y, c = _place()
        cps = []
        for a in range(n):
            rh = grads[a].shape[1] // 2
            cps.append(pltpu.make_async_remote_copy(
                src_ref=ins[a].at[:, pl.ds((1 - c) * rh, rh), :], dst_ref=outs[a],
                send_sem=ssem.at[a], recv_sem=rsem.at[a], device_id=(x, y, 1 - c), device_id_type=MESH))
        for cp in cps:
            cp.start()
        for cp in cps:
            cp.wait()

    return pl.pallas_call(
        body, name=f"comm_sibling_halves_{tag}",
        in_specs=[ANY] * (n + len(deps)), out_specs=[ANY] * n,
        out_shape=[jax.ShapeDtypeStruct((N_SHARD, g.shape[1] // 2, g.shape[2]), g.dtype) for g in grads],
        scratch_shapes=[pltpu.SemaphoreType.DMA((n,)), pltpu.SemaphoreType.DMA((n,))],
    )(*grads, *deps)


def _owner_copies(srcs, lands, ssem, rsem):
    x, y, c = _place()
    cps = []
    for a in range(len(srcs)):
        for j, (px, py) in enumerate(_other_chips(x, y)):
            cps.append(pltpu.make_async_remote_copy(
                src_ref=srcs[a].at[2 * px + py], dst_ref=lands[a].at[j],
                send_sem=ssem.at[3 * a + j], recv_sem=rsem.at[3 * a + j],
                device_id=(px, py, c), device_id_type=MESH))
    return cps


def _owners_start(tag, sums):
    n = len(sums)

    def body(*refs):
        srcs, lands = refs[:n], refs[n:2 * n]
        ssem, rsem = refs[2 * n], refs[2 * n + 1]
        token = refs[-1]
        for cp in _owner_copies(srcs, lands, ssem, rsem):
            cp.start()
        token[...] = jnp.zeros_like(token)

    lands = [lax.empty((3,) + s.shape[1:], s.dtype) for s in sums]
    bufs = list(sums) + lands
    res = pl.pallas_call(
        body, name=f"owners_start_{tag}",
        out_shape=(pltpu.SemaphoreType.DMA((3 * n,)), pltpu.SemaphoreType.DMA((3 * n,)),
                   *[pltpu.HBM(b.shape, b.dtype) for b in bufs], jax.ShapeDtypeStruct((8, LANES), F32)),
        in_specs=[HBM] * (2 * n),
        out_specs=(SEM, SEM, *[HBM] * (2 * n), pl.BlockSpec(memory_space=pltpu.VMEM)),
        input_output_aliases={i: 2 + i for i in range(2 * n)},
        compiler_params=pltpu.CompilerParams(has_side_effects=SPLIT_EFFECT),
    )(*[pltpu.with_memory_space_constraint(b, pltpu.HBM) for b in bufs])
    return res[0], res[1], list(res[2:2 + n]), list(res[2 + n:2 + 2 * n]), res[-1]


def _owners_wait(tag, handle, after):
    ssem, rsem, srcs, lands, _ = handle
    n = len(srcs)

    def body(*refs):
        for cp in _owner_copies(refs[:n], refs[n:2 * n], refs[2 * n], refs[2 * n + 1]):
            cp.wait_send()
            cp.wait_recv()

    bufs = srcs + lands
    res = pl.pallas_call(
        body, name=f"owners_wait_{tag}",
        out_shape=[pltpu.HBM(b.shape, b.dtype) for b in bufs],
        in_specs=[HBM] * (2 * n) + [SEM, SEM, ANY], out_specs=[HBM] * (2 * n),
        input_output_aliases={i: i for i in range(2 * n)},
        compiler_params=pltpu.CompilerParams(has_side_effects=SPLIT_EFFECT),
    )(*bufs, ssem, rsem, after)
    return list(res[:n]), list(res[n:])


def _comm_swap_halves(tag, fulls):
    n = len(fulls)

    def body(*refs):
        ins, outs = refs[:n], refs[n:2 * n]
        ssem, rsem = refs[2 * n:]
        x, y, c = _place()
        cps = []
        for a in range(n):
            rh = fulls[a].shape[0] // 2
            mine = pl.ds(c * rh, rh)
            cps.append(pltpu.make_async_remote_copy(
                src_ref=ins[a].at[mine], dst_ref=outs[a].at[mine],
                send_sem=ssem.at[a], recv_sem=rsem.at[a], device_id=(x, y, 1 - c), device_id_type=MESH))
        for cp in cps:
            cp.start()
        for a in range(n):
            rh = fulls[a].shape[0] // 2
            theirs = pl.ds((1 - c) * rh, rh)
            pltpu.make_async_remote_copy(
                src_ref=ins[a].at[theirs], dst_ref=outs[a].at[theirs],
                send_sem=ssem.at[a], recv_sem=rsem.at[a], device_id=(x, y, 1 - c), device_id_type=MESH).wait_recv()
        for cp in cps:
            cp.wait_send()

    return pl.pallas_call(
        body, name=f"comm_swap_halves_{tag}",
        in_specs=[ANY] * n, out_specs=[ANY] * n,
        out_shape=[jax.ShapeDtypeStruct(f.shape, f.dtype) for f in fulls],
        input_output_aliases={a: a for a in range(n)},
        scratch_shapes=[pltpu.SemaphoreType.DMA((n,)), pltpu.SemaphoreType.DMA((n,))],
    )(*fulls)


def _comm_small_allreduce(rows, w_dw_grad, d):
    n = len(rows)
    loss_row = 12

    def body(*refs):
        vec_refs = refs[:n]
        wd_ref, out_ref, pack, slots, ssem, rsem = refs[n:]
        x, y, c = _place()
        me = 4 * x + 2 * y + c
        pack[...] = jnp.zeros_like(pack)
        for (r, _), ref in zip(rows, vec_refs):
            pack[r:r + 1, :] = ref[...]
        pack[16:16 + CONV_PAD, :] = wd_ref[...]
        slots[me] = pack[...]
        cps = []
        for rel in range(1, N_DEV):
            dx, dy, dc = (rel >> 2) & 1, (rel >> 1) & 1, rel & 1
            peer = (1 - x if dx else x, 1 - y if dy else y, 1 - c if dc else c)
            cps.append(pltpu.make_async_remote_copy(
                src_ref=pack, dst_ref=slots.at[me], send_sem=ssem.at[rel - 1], recv_sem=rsem.at[rel - 1],
                device_id=peer, device_id_type=MESH))
        for cp in cps:
            cp.start()
        for cp in cps:
            cp.wait()
        tot = slots[0]
        for i in range(1, N_DEV):
            tot = tot + slots[i]
        out_ref[...] = tot
        out_ref[loss_row:loss_row + 1, :] = jnp.zeros((1, d), F32) + jnp.sum(tot[loss_row:loss_row + 1, :])

    return pl.pallas_call(
        body, name="comm_small_allreduce",
        out_shape=jax.ShapeDtypeStruct((SMALL_ROWS, d), F32),
        scratch_shapes=[pltpu.VMEM((SMALL_ROWS, d), F32), pltpu.VMEM((N_DEV, SMALL_ROWS, d), F32),
                        pltpu.SemaphoreType.DMA((N_DEV - 1,)), pltpu.SemaphoreType.DMA((N_DEV - 1,))],
    )(*[v for _, v in rows], w_dw_grad)


def kernel(x, norm_mix, norm_mlp, conv_w_in, conv_b_in, conv_w_dw, conv_b_dw, conv_ln_g, conv_ln_b, conv_w_out, conv_b_out, kv_norm, w_kv, attn_w_q, attn_w_o, mlp_w_in, mlp_w_out, final_norm, loss_target, m_norm_mix, m_norm_mlp, m_conv_w_in, m_conv_b_in, m_conv_w_dw, m_conv_b_dw, m_conv_ln_g, m_conv_ln_b, m_conv_w_out, m_conv_b_out, m_kv_norm, m_w_kv, m_attn_w_q, m_attn_w_o, m_mlp_w_in, m_mlp_w_out, m_final_norm, v_norm_mix, v_norm_mlp, v_conv_w_in, v_conv_b_in, v_conv_w_dw, v_conv_b_dw, v_conv_ln_g, v_conv_ln_b, v_conv_w_out, v_conv_b_out, v_kv_norm, v_w_kv, v_attn_w_q, v_attn_w_o, v_mlp_w_in, v_mlp_w_out, v_final_norm):
    _, s, d = x.shape
    dff = mlp_w_in.shape[2] * N_SHARD
    kvw = w_kv.shape[1]
    nh = d // HEAD_DIM
    group = nh // N_KV_HEADS
    ds4 = d // N_SHARD
    xi, yi, ci = _place()
    me = 2 * xi + yi
    place = jnp.stack([ci, me]).astype(I32)

    h0 = x.reshape(s, d)
    target = loss_target.reshape(s, d)
    tabs = _rope_tables(s)

    ag_cin = _gather_start("conv_in", [
        _cast_bf16("cast_w_in", conv_w_in, 0, place),
        _pack_small(conv_b_in, conv_w_dw.reshape(CONV_WIDTH, ds4), conv_b_dw, conv_ln_g, conv_ln_b, conv_b_out, place),
    ], n_whole=1)
    ag_cout = _gather_start("conv_out", [_cast_bf16("cast_w_out", conv_w_out, 0, place)])
    ag_mi0 = _gather_start("mlp_in0", [_cast_bf16("cast_mlp_in0", mlp_w_in, 0, place)])
    ag_mo0 = _gather_start("mlp_out0", [_cast_bf16("cast_mlp_out0", mlp_w_out, 0, place)])
    ag_attn = _gather_start("attn", [
        _cast_bf16("cast_w_kv", w_kv.reshape(1, ds4, kvw), 0, place), _cast_bf16("cast_w_q", attn_w_q, 0, place),
        _cast_bf16("cast_w_o", attn_w_o, 0, place)])
    ag_mlp1 = _gather_start("mlp1", [
        _cast_bf16("cast_mlp_in1", mlp_w_in, 1, place), _cast_bf16("cast_mlp_out1", mlp_w_out, 1, place)])

    def gather_end(tag, handle, later, n_whole=0):
        bufs = _gather_wait(tag, handle, n_whole, later)
        n_half = len(bufs) - n_whole
        return list(_comm_gather_forward(tag, bufs[:n_half])) + bufs[n_half:]

    wmi_g = [None, None]
    wmo_f = [None, None]

    nm = [norm_mix[0:1], norm_mix[1:2]]
    nmlp = [norm_mlp[0:1], norm_mlp[1:2]]
    kvn = kv_norm.reshape(1, d)
    fin = final_norm.reshape(1, d)
    started = sum(h[3][0:1, 0:1] for h in (ag_cin, ag_cout, ag_mi0, ag_mo0, ag_attn, ag_mlp1))
    (y0,) = _rms_fwd("rms_mix0", h0, [nm[0] + started])

    w_in_g, small_g = gather_end("conv_in", ag_cin, y0, n_whole=1)
    b_in_f = small_g[:, 0, :].reshape(1, 2 * d)
    b_dw_f = small_g[:, 1, 0:ds4].reshape(1, d)
    ln_g_f = small_g[:, 1, ds4:2 * ds4].reshape(1, d)
    ln_b_f = small_g[:, 2, 0:ds4].reshape(1, d)
    b_out_f = small_g[:, 2, ds4:2 * ds4].reshape(1, d)
    w_dw_f = jnp.transpose(small_g[:, 8:8 + CONV_PAD, 0:ds4], (1, 0, 2)).reshape(CONV_PAD, d)

    def ep_bias(acc, ex, outs, j):
        outs[0][...] = (acc + ex[0][...]).astype(outs[0].dtype)

    def ep_residual(acc, ex, outs, j):
        outs[0][...] = ex[0][...] + acc

    def ep_residual_bias(acc, ex, outs, j):
        outs[0][...] = ex[0][...] + (acc + ex[1][...])

    def ep_relu2(acc, ex, outs, j):
        r = jnp.maximum(acc, 0.0)
        outs[0][...] = r.astype(BF16)
        outs[1][...] = (r * r).astype(BF16)

    def ep_rope(acc, ex, outs, j):
        outs[0][...] = _rope_apply(acc, ex[0][...], ex[1][...], ex[2][...], 1.0).astype(BF16)

    def ep_rope_k(acc, ex, outs, j):
        roped = _rope_apply(acc, ex[0][...], ex[1][...], ex[2][...], 1.0)
        outs[0][...] = jnp.where(j == 0, roped, acc).astype(BF16)

    tab_extras = [(t, "rows") for t in tabs]

    def mlp_fwd(idx, h, y, out_weight):
        r, r2 = _matmul(f"mlp_in{idx}", "nn", y, wmi_g[idx], b_kind="col", m=s, n=dff, k=d,
                        outs=[(BF16, "plain"), (BF16, "plain")], epilogue=ep_relu2)
        wmo_f[idx] = out_weight(r2).reshape(dff, d)
        (h_new,) = _matmul(f"mlp_out{idx}", "nn", r2, wmo_f[idx], m=s, n=d, k=dff,
                           outs=[(F32, "plain")], extras=[(h, "ij")], epilogue=ep_residual)
        return h_new, r, r2

    (u,) = _matmul("conv_in", "nn", y0, w_in_g, b_kind="col", m=s, n=2 * d, k=d,
                   outs=[(BF16, "plain")], extras=[(b_in_f, "vec")], epilogue=ep_bias)
    cpre = _dwconv_fwd(u, w_dw_f, b_dw_f)
    sact = _ln_silu_fwd(cpre, ln_g_f, ln_b_f)
    (w_out_g,) = gather_end("conv_out", ag_cout, sact)
    w_out_f = w_out_g.reshape(d, d)
    (h1,) = _matmul("conv_out", "nn", sact, w_out_f, m=s, n=d, k=d,
                    outs=[(F32, "plain")], extras=[(h0, "ij"), (b_out_f, "vec")], epilogue=ep_residual_bias)
    (y1,) = _rms_fwd("rms_mlp0", h1, [nmlp[0]])
    (wmi_g[0],) = gather_end("mlp_in0", ag_mi0, y1)
    h2, r0, r0sq = mlp_fwd(0, h1, y1, lambda r2: gather_end("mlp_out0", ag_mo0, r2)[0])
    ykv, y2 = _rms_fwd("rms_kv_mix1", h2, [kvn, nm[1]])
    wkv_g, wq_g, wo_g = gather_end("attn", ag_attn, y2)
    wkv_f, wq_f, wo_f = wkv_g.reshape(d, kvw), wq_g.reshape(d, d), wo_g.reshape(d, d)
    (kv,) = _matmul("kv_proj", "nn", ykv, wkv_f, m=s, n=kvw, k=d, tn=kvw // 2,
                    outs=[(BF16, "plain")], extras=tab_extras, epilogue=ep_rope_k)
    (q,) = _matmul("q_proj", "nn", y2, wq_f, m=s, n=d, k=d,
                   outs=[(BF16, "plain")], extras=tab_extras, epilogue=ep_rope)
    o_parts, lse_parts = [], []
    for dil in DILATIONS:
        o_b, lse_b = _attn_fwd(f"attn_fwd_d{dil}", _to_branch(q, dil), _to_branch(kv, dil), dil, d)
        o_parts.append(o_b.reshape(s, d))
        lse_parts.append(_heads_from_branch(lse_b, dil, group))
    o, lse = _attn_combine(o_parts, lse_parts)
    (h3,) = _matmul("attn_out", "nn", o, wo_f, m=s, n=d, k=d,
                    outs=[(F32, "plain")], extras=[(h2, "ij")], epilogue=ep_residual)
    (y3,) = _rms_fwd("rms_mlp1", h3, [nmlp[1]])
    wmi_g[1], wmo1_g = gather_end("mlp1", ag_mlp1, y3)
    h4, r1, r1sq = mlp_fwd(1, h3, y3, lambda r2: wmo1_g)
    dh4, dh4b, d_fin, loss_cols = _final_loss(h4, fin, target)

    def ep_relu2_bwd(acc, ex, outs, j):
        outs[0][...] = (acc * (2.0 * ex[0][...].astype(F32))).astype(BF16)

    def mlp_bwd(idx, dhb, y, r, r2):
        (dz,) = _matmul(f"mlp_out{idx}_dx", "nt", dhb, wmo_f[idx], m=s, n=dff, k=d,
                        outs=[(BF16, "plain")], extras=[(r, "ij")], epilogue=ep_relu2_bwd)
        (dwo,) = _matmul(f"mlp_out{idx}_dw", "tn", r2, dhb, m=dff, n=d, k=s,
                         outs=[(BF16, "plain")])
        (dy,) = _matmul(f"mlp_in{idx}_dx", "nt", dz, wmi_g[idx], b_kind="col", m=s, n=d, k=dff,
                        outs=[(BF16, "plain")])
        (dwi,) = _matmul(f"mlp_in{idx}_dw", "tn", y, dz, m=d, n=dff, k=s,
                         outs=[(BF16, "col")])
        return dy, dwi, dwo.reshape(N_SHARD, dff // N_SHARD, d)

    def rs_begin(tag, names, grads, later=None):
        recv_half = _comm_sibling_halves(tag, grads, later)
        sums = [_chip_sum(f"chip_sum_{nme}", g, rh, place) for nme, g, rh in zip(names, grads, recv_half)]
        return _owners_start(tag, sums)

    def after(vec, handle):
        return vec + handle[4][0:1, 0:1]

    def rs_end(tag, names, handle, later):
        sums, pieces = _owners_wait(tag, handle, later)
        own = [_owner_sum(f"owner_sum_{nme}", cs, rp, place) for nme, cs, rp in zip(names, sums, pieces)]
        return _comm_swap_halves(tag, own)

    dy3, g_wmi1, g_wmo1 = mlp_bwd(1, dh4b, y3, r1, r1sq)
    rs_mlp1 = rs_begin("mlp1", ["mlp_in1", "mlp_out1"], [g_wmi1, g_wmo1])
    dh3, dh3b, d_nmlp1 = _rms_bwd("rms_mlp1_bwd", h3, [(after(nmlp[1], rs_mlp1), dy3)], dh4)

    (do,) = _matmul("attn_out_dx", "nt", dh3b, wo_f, m=s, n=d, k=d, outs=[(BF16, "plain")])
    (g_wo,) = _matmul("attn_out_dw", "tn", o, dh3b, m=d, n=d, k=s, outs=[(BF16, "plain")])
    delta = _attn_delta(do, o)
    dq_parts, dk_parts, dv_parts = [], [], []
    for dil in DILATIONS:
        dq_b, dk_b, dv_b = _attn_bwd(
            f"attn_bwd_d{dil}", _to_branch(q, dil), _to_branch(kv, dil), _to_branch(do, dil),
            _heads_to_branch(lse, dil, group), _heads_to_branch(delta, dil, group), dil, d)
        dq_parts.append(dq_b.reshape(s, d))
        dk_parts.append(dk_b.reshape(s, kvw // 2))
        dv_parts.append(dv_b.reshape(s, kvw // 2))
    dq = _rope_bwd_sum("rope_bwd_q", dq_parts, tabs, d)
    dkv_parts = [jnp.concatenate([a, b], axis=1) for a, b in zip(dk_parts, dv_parts)]
    dkv = _rope_bwd_sum("rope_bwd_kv", dkv_parts, tabs, kvw // 2)
    (g_wq,) = _matmul("q_proj_dw", "tn", y2, dq, m=d, n=d, k=s, outs=[(BF16, "plain")])
    (dy2,) = _matmul("q_proj_dx", "nt", dq, wq_f, m=s, n=d, k=d, outs=[(BF16, "plain")])
    (g_wkv,) = _matmul("kv_proj_dw", "tn", ykv, dkv, m=d, n=kvw, k=s, outs=[(BF16, "plain")])
    (dykv,) = _matmul("kv_proj_dx", "nt", dkv, wkv_f, m=s, n=d, k=kvw, outs=[(BF16, "plain")])
    rs_attn = rs_begin("attn", ["w_kv", "w_q", "w_o"],
                       [g_wkv.reshape(N_SHARD, ds4, kvw), g_wq.reshape(N_SHARD, ds4, d), g_wo.reshape(N_SHARD, ds4, d)])
    dh2, dh2b, d_nm1, d_kvn = _rms_bwd("rms_kv_mix1_bwd", h2, [(after(nm[1], rs_attn), dy2), (kvn, dykv)], dh3)

    dy1, g_wmi0, g_wmo0 = mlp_bwd(0, dh2b, y1, r0, r0sq)
    rs_mlp0 = rs_begin("mlp0", ["mlp_in0", "mlp_out0"], [g_wmi0, g_wmo0])
    dh1, dh1b, d_nmlp0, d_b_out = _rms_bwd("rms_mlp0_bwd", h1, [(after(nmlp[0], rs_mlp0), dy1)], dh2,
                                           want_colsum=True)

    (dsact,) = _matmul("conv_out_dx", "nt", dh1b, w_out_f, m=s, n=d, k=d, outs=[(BF16, "plain")])
    (g_wout,) = _matmul("conv_out_dw", "tn", sact, dh1b, m=d, n=d, k=s, outs=[(BF16, "plain")])
    dc, d_ln_g, d_ln_b, d_b_dw = _ln_silu_bwd(cpre, ln_g_f, ln_b_f, dsact)
    da, dgt, d_w_dw, d_b_in_a, d_b_in_g = _dwconv_bwd(u, w_dw_f, dc)
    du = jnp.concatenate([da, dgt], axis=1)
    (g_win,) = _matmul("conv_in_dw", "tn", y0, du, m=d, n=2 * d, k=s, outs=[(BF16, "col")])
    (dy0,) = _matmul("conv_in_dx", "nt", du, w_in_g, b_kind="col", m=s, n=d, k=2 * d,
                     outs=[(BF16, "plain")])
    dx, _, d_nm0 = _rms_bwd("rms_mix0_bwd", h0, [(nm[0], dy0)], dh1)

    small_rows = [(0, d_nm0), (1, d_nm1), (2, d_nmlp0), (3, d_nmlp1), (4, d_kvn), (5, d_fin), (6, d_b_dw),
                  (7, d_ln_g), (8, d_ln_b), (9, d_b_out), (10, d_b_in_a), (11, d_b_in_g), (12, loss_cols)]
    red = _comm_small_allreduce(small_rows, d_w_dw, d)
    rs_conv = rs_begin("conv", ["w_in", "w_out"], [g_win, g_wout.reshape(N_SHARD, ds4, d)], red)
    loss = red[12, 0]
    g_norm_mix = red[0:2]
    g_norm_mlp = red[2:4]
    g_kv_norm = red[4:5]
    g_final = red[5:6]

    def my_cols(row):
        return lax.dynamic_slice(red, (row, me * ds4), (1, ds4))

    g_b_dw, g_ln_g, g_ln_b, g_b_out = my_cols(6), my_cols(7), my_cols(8), my_cols(9)
    half_in = 2 * d // N_SHARD
    b_in_row = 10 + me // 2
    g_b_in = lax.dynamic_slice(red, (b_in_row, (me % 2) * half_in), (1, half_in))
    g_w_dw = lax.dynamic_slice(red, (16, me * ds4), (CONV_WIDTH, ds4))

    def big(name, w, m, v, g, layer=0, partial=None):
        shape = w.shape
        w3, m3, v3 = [t.reshape((-1,) + shape[-2:]) for t in (w, m, v)]
        if partial is not None:
            partial = [t.reshape(w3.shape) for t in partial]
        res = _adamw(name, w3, m3, v3, g, layer, partial)
        return [t.reshape(shape) for t in res]

    f_wmi1, f_wmo1 = rs_end("mlp1", ["mlp_in1", "mlp_out1"], rs_mlp1, rs_conv[4])
    p_wmi = big("adam_mlp_in1", mlp_w_in, m_mlp_w_in, v_mlp_w_in, f_wmi1, 1)
    p_wmo = big("adam_mlp_out1", mlp_w_out, m_mlp_w_out, v_mlp_w_out, f_wmo1, 1)
    f_wkv, f_wq, f_wo = rs_end("attn", ["w_kv", "w_q", "w_o"], rs_attn, p_wmo[0])
    r_wkv = big("adam_w_kv", w_kv, m_w_kv, v_w_kv, f_wkv)
    r_wq = big("adam_w_q", attn_w_q, m_attn_w_q, v_attn_w_q, f_wq)
    r_wo = big("adam_w_o", attn_w_o, m_attn_w_o, v_attn_w_o, f_wo)
    f_wmi0, f_wmo0 = rs_end("mlp0", ["mlp_in0", "mlp_out0"], rs_mlp0, r_wo[0])
    r_wmi = big("adam_mlp_in0", mlp_w_in, m_mlp_w_in, v_mlp_w_in, f_wmi0, 0, p_wmi)
    r_wmo = big("adam_mlp_out0", mlp_w_out, m_mlp_w_out, v_mlp_w_out, f_wmo0, 0, p_wmo)
    f_win, f_wout = rs_end("conv", ["w_in", "w_out"], rs_conv, r_wmo[0])
    r_win = big("adam_w_in", conv_w_in, m_conv_w_in, v_conv_w_in, f_win)
    r_wout = big("adam_w_out", conv_w_out, m_conv_w_out, v_conv_w_out, f_wout)

    sm_w = [norm_mix, norm_mlp, conv_b_in, conv_w_dw.reshape(CONV_WIDTH, ds4), conv_b_dw, conv_ln_g, conv_ln_b,
            conv_b_out, kv_norm.reshape(1, d), final_norm.reshape(1, d)]
    sm_m = [m_norm_mix, m_norm_mlp, m_conv_b_in, m_conv_w_dw.reshape(CONV_WIDTH, ds4), m_conv_b_dw, m_conv_ln_g,
            m_conv_ln_b, m_conv_b_out, m_kv_norm.reshape(1, d), m_final_norm.reshape(1, d)]
    sm_v = [v_norm_mix, v_norm_mlp, v_conv_b_in, v_conv_w_dw.reshape(CONV_WIDTH, ds4), v_conv_b_dw, v_conv_ln_g,
            v_conv_ln_b, v_conv_b_out, v_kv_norm.reshape(1, d), v_final_norm.reshape(1, d)]
    sm_g = [g_norm_mix, g_norm_mlp, g_b_in, g_w_dw, g_b_dw, g_ln_g, g_ln_b, g_b_out, g_kv_norm, g_final]
    sm_d, sm_nm, sm_nv = _adam_small(sm_w, sm_m, sm_v, sm_g)
    shapes = [norm_mix.shape, norm_mlp.shape, conv_b_in.shape, conv_w_dw.shape, conv_b_dw.shape, conv_ln_g.shape,
              conv_ln_b.shape, conv_b_out.shape, kv_norm.shape, final_norm.shape]
    sm_g, sm_d, sm_nm, sm_nv = [[t.reshape(sh) for t, sh in zip(lst, shapes)] for lst in (sm_g, sm_d, sm_nm, sm_nv)]

    def order(sm, idx):
        return [sm[0], sm[1], r_win[idx], sm[2], sm[3], sm[4], sm[5], sm[6], r_wout[idx], sm[7], sm[8],
                r_wkv[idx], r_wq[idx], r_wo[idx], r_wmi[idx], r_wmo[idx], sm[9]]

    return (loss, dx.reshape(x.shape), *order(sm_g, 0), *order(sm_d, 1), *order(sm_nm, 2), *order(sm_nv, 3))
```

```python
import functools
import math

import jax
import jax.numpy as jnp
from jax import lax
from jax.experimental import pallas as pl
from jax.experimental.pallas import tpu as pltpu

F32 = jnp.float32
BF16 = jnp.bfloat16
I32 = jnp.int32

NORM_EPS = 1e-6
LN_EPS = 1e-5
HEAD_DIM = 128
N_KV_HEADS = 4
ROT_DIM = 32
ROPE_THETA = 500000.0
CONV_WIDTH = 31
CONV_PAD = 32
ATT_BLOCK = 128
DILATIONS = (1, 4, 16)
ADAM_LR = 0.001
ADAM_B1 = 0.9
ADAM_B2 = 0.999
ADAM_EPS = 1e-08
ADAM_WD = 0.01
ADAM_STEP = 10
N_SHARD = 4
N_DEV = 8
LANES = 128
VMEM_LIMIT = 48 * 1024 * 1024
MM_TM, MM_TN, MM_TK = 1024, 1024, 2048
ROW_TILE = 256
CONV_CB = 128
CONV_T = 128
SMALL_ROWS = 48
MESH = pl.DeviceIdType.MESH
ANY = pl.BlockSpec(memory_space=pl.ANY)
HBM = pl.BlockSpec(memory_space=pltpu.HBM)
SEM = pl.BlockSpec(memory_space=pltpu.SEMAPHORE)
SPLIT_EFFECT = pltpu.SideEffectType.DATAFLOW_SIDE_EFFECTING


def _params(*sem):
    return pltpu.CompilerParams(dimension_semantics=sem, vmem_limit_bytes=VMEM_LIMIT)


def _sigmoid(x):
    return 1.0 / (1.0 + jnp.exp(-x))


def _wspec(kind, arr_shape, br, bc, pick):
    if kind == "plain":
        return pl.BlockSpec((br, bc), pick)
    per = arr_shape[2] // bc

    def idx(*g):
        rb, cb = pick(*g)
        return (cb // per, rb, cb % per)

    return pl.BlockSpec((None, br, bc), idx)


def _matmul(name, mode, a, b, *, m, n, k, tn=MM_TN, b_kind="plain", outs, extras=(), epilogue=None):
    tm, tn, tk = min(MM_TM, m), min(tn, n), min(MM_TK, k)
    if b_kind == "col" and mode == "nn":
        tn = min(tn, n // N_SHARD)
    if b_kind == "col" and mode == "nt":
        tk = min(tk, k // N_SHARD)
    if any(kind == "col" for _, kind in outs):
        tn = min(tn, n // N_SHARD)
    assert m % tm == 0 and n % tn == 0 and k % tk == 0, (name, m, n, k, tm, tn, tk)
    nk = k // tk
    grid = (m // tm, n // tn, nk)
    if mode == "nn":
        a_spec = pl.BlockSpec((tm, tk), lambda i, j, kk: (i, kk))
        b_spec = _wspec(b_kind, b.shape, tk, tn, lambda i, j, kk: (kk, j))
        dims = (((1,), (0,)), ((), ()))
    elif mode == "nt":
        a_spec = pl.BlockSpec((tm, tk), lambda i, j, kk: (i, kk))
        b_spec = _wspec(b_kind, b.shape, tn, tk, lambda i, j, kk: (j, kk))
        dims = (((1,), (1,)), ((), ()))
    else:
        a_spec = pl.BlockSpec((tk, tm), lambda i, j, kk: (kk, i))
        b_spec = pl.BlockSpec((tk, tn), lambda i, j, kk: (kk, j))
        dims = (((0,), (0,)), ((), ()))
    out_shape, out_specs = [], []
    for dtype, kind in outs:
        shape = (m, n) if kind == "plain" else (N_SHARD, m, n // N_SHARD)
        out_shape.append(jax.ShapeDtypeStruct(shape, dtype))
        out_specs.append(_wspec(kind, shape, tm, tn, lambda i, j, kk: (i, j)))
    n_ex = len(extras)
    ex_specs = {"ij": pl.BlockSpec((tm, tn), lambda i, j, kk: (i, j)),
                "vec": pl.BlockSpec((1, tn), lambda i, j, kk: (0, j)),
                "rows": pl.BlockSpec((tm, LANES), lambda i, j, kk: (i, 0))}

    def body(*refs):
        a_ref, b_ref = refs[0], refs[1]
        ex_refs = refs[2:2 + n_ex]
        out_refs = refs[2 + n_ex:2 + n_ex + len(outs)]
        j = pl.program_id(1)

        def finish(res):
            if epilogue is None:
                out_refs[0][...] = res.astype(out_refs[0].dtype)
            else:
                epilogue(res, ex_refs, out_refs, j)

        prod = lax.dot_general(a_ref[...], b_ref[...], dims, preferred_element_type=F32)
        if nk == 1:
            finish(prod)
            return
        acc_ref = refs[-1]
        kk = pl.program_id(2)

        @pl.when(kk == 0)
        def _():
            acc_ref[...] = prod

        @pl.when(kk > 0)
        def _():
            acc_ref[...] += prod

        @pl.when(kk == nk - 1)
        def _():
            finish(acc_ref[...])

    res = pl.pallas_call(
        body, name=name, grid=grid,
        in_specs=[a_spec, b_spec] + [ex_specs[how] for _, how in extras],
        out_specs=out_specs, out_shape=out_shape,
        scratch_shapes=[pltpu.VMEM((tm, tn), F32)] if nk > 1 else [],
        compiler_params=_params("parallel", "parallel", "arbitrary"),
    )(a, b, *[e for e, _ in extras])
    return res


def _rope_tables(seq):
    half = ROT_DIM // 2
    pos = jnp.arange(seq, dtype=F32)
    inv = ROPE_THETA ** (-jnp.arange(0, ROT_DIM, 2, dtype=F32) / ROT_DIM)
    ang = pos[:, None] * inv[None, :]
    cos, sin = jnp.cos(ang), jnp.sin(ang)
    zeros = jnp.zeros((seq, HEAD_DIM - ROT_DIM), F32)
    ctab = jnp.concatenate([cos, cos, zeros + 1.0], axis=1)
    atab = jnp.concatenate([-sin, jnp.zeros((seq, half), F32), zeros], axis=1)
    btab = jnp.concatenate([jnp.zeros((seq, half), F32), sin, zeros], axis=1)
    return ctab, atab, btab


def _rope_apply(x, ctab, atab, btab, sign):
    w = x.shape[1]
    reps = w // HEAD_DIM
    half = ROT_DIM // 2
    c = jnp.tile(ctab, (1, reps))
    a = jnp.tile(atab, (1, reps))
    b = jnp.tile(btab, (1, reps))
    up = pltpu.roll(x, w - half, 1)
    down = pltpu.roll(x, half, 1)
    return x * c + sign * (up * a + down * b)


def _rows(t, w):
    return pl.BlockSpec((t, w), lambda i: (i, 0))


def _fixed(shape):
    nd = len(shape)
    return pl.BlockSpec(shape, lambda i: (0,) * nd)


def _rms_fwd(name, x, gains):
    s, d = x.shape
    t = min(ROW_TILE, s)
    ng = len(gains)

    def body(x_ref, *refs):
        xv = x_ref[...]
        r = lax.rsqrt(jnp.mean(xv * xv, axis=-1, keepdims=True) + NORM_EPS)
        xn = xv * r
        for g_ref, y_ref in zip(refs[:ng], refs[ng:]):
            y_ref[...] = (xn * g_ref[...]).astype(BF16)

    return pl.pallas_call(
        body, name=name, grid=(s // t,),
        in_specs=[_rows(t, d)] + [_fixed((1, d))] * ng,
        out_specs=[_rows(t, d)] * ng,
        out_shape=[jax.ShapeDtypeStruct((s, d), BF16)] * ng,
        compiler_params=_params("parallel"),
    )(x, *gains)


def _rms_bwd(name, x, pairs, dh_in, want_colsum=False):
    s, d = x.shape
    t = min(ROW_TILE, s)
    n_p = len(pairs)

    def body(x_ref, dh_ref, *refs):
        g_refs = refs[:n_p]
        dy_refs = refs[n_p:2 * n_p]
        dh_out, dhb_out = refs[2 * n_p], refs[2 * n_p + 1]
        dg_refs = refs[2 * n_p + 2:2 * n_p + 2 + n_p]
        cs_ref = refs[-1] if want_colsum else None
        i = pl.program_id(0)
        xv = x_ref[...]
        r = lax.rsqrt(jnp.mean(xv * xv, axis=-1, keepdims=True) + NORM_EPS)
        xn = xv * r
        dh = dh_ref[...]
        for g_ref, dy_ref, dg_ref in zip(g_refs, dy_refs, dg_refs):
            dy = dy_ref[...].astype(F32)
            u = dy * g_ref[...]
            dh = dh + r * (u - xn * jnp.mean(u * xn, axis=-1, keepdims=True))
            part = jnp.sum(dy * xn, axis=0, keepdims=True)

            @pl.when(i == 0)
            def _():
                dg_ref[...] = part

            @pl.when(i > 0)
            def _():
                dg_ref[...] += part

        dh_out[...] = dh
        dhb_out[...] = dh.astype(BF16)
        if want_colsum:
            col = jnp.sum(dh, axis=0, keepdims=True)

            @pl.when(i == 0)
            def _():
                cs_ref[...] = col

            @pl.when(i > 0)
            def _():
                cs_ref[...] += col

    n_vec = n_p + (1 if want_colsum else 0)
    return pl.pallas_call(
        body, name=name, grid=(s // t,),
        in_specs=[_rows(t, d), _rows(t, d)] + [_fixed((1, d))] * n_p + [_rows(t, d)] * n_p,
        out_specs=[_rows(t, d), _rows(t, d)] + [_fixed((1, d))] * n_vec,
        out_shape=[jax.ShapeDtypeStruct((s, d), F32), jax.ShapeDtypeStruct((s, d), BF16)]
        + [jax.ShapeDtypeStruct((1, d), F32)] * n_vec,
        compiler_params=_params("arbitrary"),
    )(x, dh_in, *[g for g, _ in pairs], *[dy for _, dy in pairs])


def _final_loss(x, g, target):
    s, d = x.shape
    t = min(ROW_TILE, s)

    def body(x_ref, g_ref, t_ref, dh_out, dhb_out, dg_ref, loss_ref):
        i = pl.program_id(0)
        xv = x_ref[...]
        gv = g_ref[...]
        r = lax.rsqrt(jnp.mean(xv * xv, axis=-1, keepdims=True) + NORM_EPS)
        xn = xv * r
        diff = xn * gv - t_ref[...]
        dy = diff / d
        u = dy * gv
        dh = r * (u - xn * jnp.mean(u * xn, axis=-1, keepdims=True))
        dh_out[...] = dh
        dhb_out[...] = dh.astype(BF16)
        dg = jnp.sum(dy * xn, axis=0, keepdims=True)
        lc = jnp.sum(0.5 * diff * dy, axis=0, keepdims=True)

        @pl.when(i == 0)
        def _():
            dg_ref[...] = dg
            loss_ref[...] = lc

        @pl.when(i > 0)
        def _():
            dg_ref[...] += dg
            loss_ref[...] += lc

    return pl.pallas_call(
        body, name="final_loss", grid=(s // t,),
        in_specs=[_rows(t, d), _fixed((1, d)), _rows(t, d)],
        out_specs=[_rows(t, d), _rows(t, d), _fixed((1, d)), _fixed((1, d))],
        out_shape=[jax.ShapeDtypeStruct((s, d), F32), jax.ShapeDtypeStruct((s, d), BF16),
                   jax.ShapeDtypeStruct((1, d), F32), jax.ShapeDtypeStruct((1, d), F32)],
        compiler_params=_params("arbitrary"),
    )(x, g, target)


def _ln_silu_fwd(c, g, b):
    s, d = c.shape
    t = min(ROW_TILE, s)

    def body(c_ref, g_ref, b_ref, s_ref):
        cv = c_ref[...]
        mu = jnp.mean(cv, axis=-1, keepdims=True)
        xc = cv - mu
        rs = lax.rsqrt(jnp.mean(xc * xc, axis=-1, keepdims=True) + LN_EPS)
        ln = xc * rs * g_ref[...] + b_ref[...]
        s_ref[...] = (ln * _sigmoid(ln)).astype(BF16)

    return pl.pallas_call(
        body, name="ln_silu_fwd", grid=(s // t,),
        in_specs=[_rows(t, d), _fixed((1, d)), _fixed((1, d))],
        out_specs=_rows(t, d), out_shape=jax.ShapeDtypeStruct((s, d), BF16),
        compiler_params=_params("parallel"),
    )(c, g, b)


def _ln_silu_bwd(c, g, b, ds):
    s, d = c.shape
    t = min(ROW_TILE, s)

    def body(c_ref, g_ref, b_ref, ds_ref, dc_ref, dg_ref, db_ref, dbdw_ref):
        i = pl.program_id(0)
        cv = c_ref[...]
        gv = g_ref[...]
        mu = jnp.mean(cv, axis=-1, keepdims=True)
        xc = cv - mu
        rs = lax.rsqrt(jnp.mean(xc * xc, axis=-1, keepdims=True) + LN_EPS)
        nrm = xc * rs
        ln = nrm * gv + b_ref[...]
        sig = _sigmoid(ln)
        dln = ds_ref[...].astype(F32) * sig * (1.0 + ln * (1.0 - sig))
        dn = dln * gv
        dc = rs * (dn - jnp.mean(dn, axis=-1, keepdims=True)
                   - nrm * jnp.mean(dn * nrm, axis=-1, keepdims=True))
        dc_ref[...] = dc
        pg = jnp.sum(dln * nrm, axis=0, keepdims=True)
        pb = jnp.sum(dln, axis=0, keepdims=True)
        pc = jnp.sum(dc, axis=0, keepdims=True)

        @pl.when(i == 0)
        def _():
            dg_ref[...] = pg
            db_ref[...] = pb
            dbdw_ref[...] = pc

        @pl.when(i > 0)
        def _():
            dg_ref[...] += pg
            db_ref[...] += pb
            dbdw_ref[...] += pc

    return pl.pallas_call(
        body, name="ln_silu_bwd", grid=(s // t,),
        in_specs=[_rows(t, d), _fixed((1, d)), _fixed((1, d)), _rows(t, d)],
        out_specs=[_rows(t, d)] + [_fixed((1, d))] * 3,
        out_shape=[jax.ShapeDtypeStruct((s, d), F32)] + [jax.ShapeDtypeStruct((1, d), F32)] * 3,
        compiler_params=_params("arbitrary"),
    )(c, g, b, ds)


def _attn_combine(o_list, lse_list):
    s, d = o_list[0].shape
    nh = d // HEAD_DIM
    t = min(ROW_TILE, s)
    nb = len(o_list)

    def body(*refs):
        o_refs = refs[:nb]
        l_refs = refs[nb:2 * nb]
        o_out, l_out = refs[2 * nb], refs[2 * nb + 1]
        ls = [r[...] for r in l_refs]
        mx = ls[0]
        for l in ls[1:]:
            mx = jnp.maximum(mx, l)
        es = [jnp.exp(l - mx) for l in ls]
        den = es[0]
        for e in es[1:]:
            den = den + e
        l_out[...] = mx + jnp.log(den)
        ws = [e / den for e in es]
        for h in range(nh):
            cols = slice(h * HEAD_DIM, (h + 1) * HEAD_DIM)
            acc = jnp.zeros((t, HEAD_DIM), F32)
            for o_ref, w in zip(o_refs, ws):
                acc = acc + w[:, h:h + 1] * o_ref[:, cols].astype(F32)
            o_out[:, cols] = acc.astype(BF16)

    return pl.pallas_call(
        body, name="attn_combine", grid=(s // t,),
        in_specs=[_rows(t, d)] * nb + [_rows(t, nh)] * nb,
        out_specs=[_rows(t, d), _rows(t, nh)],
        out_shape=[jax.ShapeDtypeStruct((s, d), BF16), jax.ShapeDtypeStruct((s, nh), F32)],
        compiler_params=_params("parallel"),
    )(*o_list, *lse_list)


def _attn_delta(do, o):
    s, d = o.shape
    nh = d // HEAD_DIM
    t = min(ROW_TILE, s)

    def body(do_ref, o_ref, dl_ref):
        lane = lax.broadcasted_iota(I32, (t, nh), 1)
        out = jnp.zeros((t, nh), F32)
        for h in range(nh):
            cols = slice(h * HEAD_DIM, (h + 1) * HEAD_DIM)
            v = jnp.sum(do_ref[:, cols].astype(F32) * o_ref[:, cols].astype(F32), axis=-1, keepdims=True)
            out = jnp.where(lane == h, v, out)
        dl_ref[...] = out

    return pl.pallas_call(
        body, name="attn_delta", grid=(s // t,),
        in_specs=[_rows(t, d), _rows(t, d)],
        out_specs=_rows(t, nh), out_shape=jax.ShapeDtypeStruct((s, nh), F32),
        compiler_params=_params("parallel"),
    )(do, o)


def _rope_bwd_sum(name, parts, tabs, rope_cols):
    s, w = parts[0].shape
    t = min(ROW_TILE, s)
    n_p = len(parts)

    def body(*refs):
        p_refs = refs[:n_p]
        c_ref, a_ref, b_ref = refs[n_p:n_p + 3]
        out = refs[-1]
        tot = p_refs[0][...].astype(F32)
        for p in p_refs[1:]:
            tot = tot + p[...].astype(F32)
        rot = _rope_apply(tot[:, :rope_cols], c_ref[...], a_ref[...], b_ref[...], -1.0)
        out[:, :rope_cols] = rot.astype(BF16)
        if rope_cols < w:
            out[:, rope_cols:] = tot[:, rope_cols:].astype(BF16)

    return pl.pallas_call(
        body, name=name, grid=(s // t,),
        in_specs=[_rows(t, w)] * n_p + [_rows(t, HEAD_DIM)] * 3,
        out_specs=_rows(t, w), out_shape=jax.ShapeDtypeStruct((s, w), BF16),
        compiler_params=_params("parallel"),
    )(*parts, *tabs)


def _dwconv_fwd(u, w_dw, b_dw):
    s, d2 = u.shape
    d = d2 // 2
    cb = min(CONV_CB, d)
    nblk = d // cb
    tt = min(CONV_T, s)

    def body(ua_ref, ug_ref, w_ref, b_ref, c_ref, xp_ref):
        gl = ua_ref[...].astype(F32) * _sigmoid(ug_ref[...].astype(F32))
        xp_ref[0:CONV_PAD, :] = jnp.zeros((CONV_PAD, cb), F32)
        xp_ref[CONV_PAD:, :] = gl
        wv = w_ref[...]
        bv = b_ref[...]
        for t0 in range(0, s, tt):
            acc = jnp.zeros((tt, cb), F32) + bv
            for kk in range(CONV_WIDTH):
                off = t0 + CONV_PAD - (CONV_WIDTH - 1) + kk
                acc = acc + wv[kk:kk + 1, :] * xp_ref[off:off + tt, :]
            c_ref[t0:t0 + tt, :] = acc

    return pl.pallas_call(
        body, name="dwconv_fwd", grid=(nblk,),
        in_specs=[pl.BlockSpec((s, cb), lambda j: (0, j)), pl.BlockSpec((s, cb), lambda j: (0, j + nblk)),
                  pl.BlockSpec((CONV_PAD, cb), lambda j: (0, j)), pl.BlockSpec((1, cb), lambda j: (0, j))],
        out_specs=pl.BlockSpec((s, cb), lambda j: (0, j)),
        out_shape=jax.ShapeDtypeStruct((s, d), F32),
        scratch_shapes=[pltpu.VMEM((s + CONV_PAD, cb), F32)],
        compiler_params=_params("parallel"),
    )(u, u, w_dw, b_dw)


def _dwconv_bwd(u, w_dw, dc):
    s, d2 = u.shape
    d = d2 // 2
    cb = min(CONV_CB, d)
    nblk = d // cb
    tt = min(CONV_T, s)

    def body(ua_ref, ug_ref, w_ref, dc_ref, da_ref, dgt_ref, dw_ref, dba_ref, dbg_ref, glp_ref, dcp_ref, acc_ref):
        a = ua_ref[...].astype(F32)
        sig = _sigmoid(ug_ref[...].astype(F32))
        glp_ref[0:CONV_PAD, :] = jnp.zeros((CONV_PAD, cb), F32)
        glp_ref[CONV_PAD:, :] = a * sig
        dcp_ref[0:s, :] = dc_ref[...]
        dcp_ref[s:, :] = jnp.zeros((CONV_PAD, cb), F32)
        acc_ref[...] = jnp.zeros_like(acc_ref)
        wv = w_ref[...]
        dba = jnp.zeros((1, cb), F32)
        dbg = jnp.zeros((1, cb), F32)
        for t0 in range(0, s, tt):
            dgl = jnp.zeros((tt, cb), F32)
            dct = dc_ref[t0:t0 + tt, :]
            for kk in range(CONV_WIDTH):
                off = t0 + (CONV_WIDTH - 1) - kk
                dgl = dgl + wv[kk:kk + 1, :] * dcp_ref[off:off + tt, :]
                goff = t0 + CONV_PAD - (CONV_WIDTH - 1) + kk
                prod = dct * glp_ref[goff:goff + tt, :]
                acc_ref[8 * kk:8 * kk + 8, :] += jnp.sum(prod.reshape(tt // 8, 8, cb), axis=0)
            at = ua_ref[t0:t0 + tt, :].astype(F32)
            st = _sigmoid(ug_ref[t0:t0 + tt, :].astype(F32))
            da = dgl * st
            dg = dgl * at * st * (1.0 - st)
            da_ref[t0:t0 + tt, :] = da.astype(BF16)
            dgt_ref[t0:t0 + tt, :] = dg.astype(BF16)
            dba = dba + jnp.sum(da, axis=0, keepdims=True)
            dbg = dbg + jnp.sum(dg, axis=0, keepdims=True)
        dba_ref[...] = dba
        dbg_ref[...] = dbg
        for kk in range(CONV_WIDTH):
            dw_ref[kk:kk + 1, :] = jnp.sum(acc_ref[8 * kk:8 * kk + 8, :], axis=0, keepdims=True)
        dw_ref[CONV_WIDTH:, :] = jnp.zeros((CONV_PAD - CONV_WIDTH, cb), F32)

    blk = pl.BlockSpec((s, cb), lambda j: (0, j))
    vec = pl.BlockSpec((1, cb), lambda j: (0, j))
    return pl.pallas_call(
        body, name="dwconv_bwd", grid=(nblk,),
        in_specs=[blk, pl.BlockSpec((s, cb), lambda j: (0, j + nblk)),
                  pl.BlockSpec((CONV_PAD, cb), lambda j: (0, j)), blk],
        out_specs=[blk, blk, pl.BlockSpec((CONV_PAD, cb), lambda j: (0, j)), vec, vec],
        out_shape=[jax.ShapeDtypeStruct((s, d), BF16), jax.ShapeDtypeStruct((s, d), BF16),
                   jax.ShapeDtypeStruct((CONV_PAD, d), F32),
                   jax.ShapeDtypeStruct((1, d), F32), jax.ShapeDtypeStruct((1, d), F32)],
        scratch_shapes=[pltpu.VMEM((s + CONV_PAD, cb), F32), pltpu.VMEM((s + CONV_PAD, cb), F32),
                        pltpu.VMEM((8 * CONV_PAD, cb), F32)],
        compiler_params=_params("parallel"),
    )(u, u, w_dw, dc)


def _stack_heads(x, group):
    return jnp.concatenate([x[:, g * HEAD_DIM:(g + 1) * HEAD_DIM] for g in range(group)], axis=0)


def _unstack_heads(x, group):
    return jnp.concatenate([x[g * ATT_BLOCK:(g + 1) * ATT_BLOCK, :] for g in range(group)], axis=1)


def _stack_cols(x, group):
    return jnp.concatenate([x[:, g:g + 1] for g in range(group)], axis=0)


def _band_mask(nb, group):
    rows = group * ATT_BLOCK
    row = lax.broadcasted_iota(I32, (rows, 2 * ATT_BLOCK), 0) % ATT_BLOCK
    col = lax.broadcasted_iota(I32, (rows, 2 * ATT_BLOCK), 1)
    return (col >= row) & (col <= row + ATT_BLOCK) & ((col >= ATT_BLOCK) | (nb > 0))


def _window(ref, nb):
    prev = pl.multiple_of(jnp.maximum(nb - 1, 0) * ATT_BLOCK, ATT_BLOCK)
    cur = pl.multiple_of(nb * ATT_BLOCK, ATT_BLOCK)
    return jnp.concatenate([ref[pl.ds(prev, ATT_BLOCK), :], ref[pl.ds(cur, ATT_BLOCK), :]], axis=0)


def _attn_fwd(name, q, kv, dil, d):
    sd = q.shape[0]
    group = d // HEAD_DIM // N_KV_HEADS
    gw = group * HEAD_DIM
    nblk = sd // ATT_BLOCK
    scale = 1.0 / math.sqrt(HEAD_DIM)
    nt = (((1,), (1,)), ((), ()))

    def body(q_ref, k_ref, v_ref, o_ref, lse_ref):
        lane = lax.broadcasted_iota(I32, (ATT_BLOCK, group), 1)

        def step(nb, carry):
            rows = pl.ds(pl.multiple_of(nb * ATT_BLOCK, ATT_BLOCK), ATT_BLOCK)
            qs = _stack_heads(q_ref[rows, :], group)
            kw = _window(k_ref, nb)
            vw = _window(v_ref, nb)
            sc = lax.dot_general(qs, kw, nt, preferred_element_type=F32) * scale
            sc = jnp.where(_band_mask(nb, group), sc, -jnp.inf)
            mx = jnp.max(sc, axis=-1, keepdims=True)
            p = jnp.exp(sc - mx)
            l = jnp.sum(p, axis=-1, keepdims=True)
            o = jnp.dot(p.astype(BF16), vw, preferred_element_type=F32) / l
            o_ref[rows, :] = _unstack_heads(o, group).astype(BF16)
            lse = mx + jnp.log(l)
            out = jnp.zeros((ATT_BLOCK, group), F32)
            for g in range(group):
                out = jnp.where(lane == g, lse[g * ATT_BLOCK:(g + 1) * ATT_BLOCK, :], out)
            lse_ref[rows, :] = out
            return carry

        lax.fori_loop(0, nblk, step, 0)

    kvh = N_KV_HEADS
    return pl.pallas_call(
        body, name=name, grid=(dil, kvh),
        in_specs=[pl.BlockSpec((sd, gw), lambda r, h: (0, r * kvh + h)),
                  pl.BlockSpec((sd, HEAD_DIM), lambda r, h: (0, r * 2 * kvh + h)),
                  pl.BlockSpec((sd, HEAD_DIM), lambda r, h: (0, r * 2 * kvh + kvh + h))],
        out_specs=[pl.BlockSpec((sd, gw), lambda r, h: (0, r * kvh + h)),
                   pl.BlockSpec((None, sd, group), lambda r, h: (r * kvh + h, 0, 0))],
        out_shape=[jax.ShapeDtypeStruct((sd, dil * d), BF16),
                   jax.ShapeDtypeStruct((dil * kvh, sd, group), F32)],
        compiler_params=_params("parallel", "parallel"),
    )(q, kv, kv)


def _attn_bwd(name, q, kv, do, lse, delta, dil, d):
    sd = q.shape[0]
    group = d // HEAD_DIM // N_KV_HEADS
    gw = group * HEAD_DIM
    nblk = sd // ATT_BLOCK
    scale = 1.0 / math.sqrt(HEAD_DIM)
    nt = (((1,), (1,)), ((), ()))
    tn = (((0,), (0,)), ((), ()))

    def body(q_ref, k_ref, v_ref, do_ref, lse_ref, dl_ref, dq_ref, dk_ref, dv_ref, dk_acc, dv_acc):
        dk_acc[...] = jnp.zeros_like(dk_acc)
        dv_acc[...] = jnp.zeros_like(dv_acc)

        def step(nb, carry):
            rows = pl.ds(pl.multiple_of(nb * ATT_BLOCK, ATT_BLOCK), ATT_BLOCK)
            qs = _stack_heads(q_ref[rows, :], group)
            dos = _stack_heads(do_ref[rows, :], group)
            ls = _stack_cols(lse_ref[rows, :], group)
            dl = _stack_cols(dl_ref[rows, :], group)
            kw = _window(k_ref, nb)
            vw = _window(v_ref, nb)
            sc = lax.dot_general(qs, kw, nt, preferred_element_type=F32) * scale
            sc = jnp.where(_band_mask(nb, group), sc, -jnp.inf)
            p = jnp.exp(sc - ls)
            dp = lax.dot_general(dos, vw, nt, preferred_element_type=F32)
            ds = (p * (dp - dl) * scale).astype(BF16)
            dq = jnp.dot(ds, kw, preferred_element_type=F32)
            dq_ref[rows, :] = _unstack_heads(dq, group).astype(BF16)
            win = pl.ds(pl.multiple_of(nb * ATT_BLOCK, ATT_BLOCK), 2 * ATT_BLOCK)
            dk_acc[win, :] += lax.dot_general(ds, qs, tn, preferred_element_type=F32)
            dv_acc[win, :] += lax.dot_general(p.astype(BF16), dos, tn, preferred_element_type=F32)
            return carry

        lax.fori_loop(0, nblk, step, 0)
        dk_ref[...] = dk_acc[ATT_BLOCK:, :]
        dv_ref[...] = dv_acc[ATT_BLOCK:, :]

    kvh = N_KV_HEADS
    qspec = pl.BlockSpec((sd, gw), lambda r, h: (0, r * kvh + h))
    sspec = pl.BlockSpec((None, sd, group), lambda r, h: (r * kvh + h, 0, 0))
    kspec = pl.BlockSpec((sd, HEAD_DIM), lambda r, h: (0, r * kvh + h))
    return pl.pallas_call(
        body, name=name, grid=(dil, kvh),
        in_specs=[qspec,
                  pl.BlockSpec((sd, HEAD_DIM), lambda r, h: (0, r * 2 * kvh + h)),
                  pl.BlockSpec((sd, HEAD_DIM), lambda r, h: (0, r * 2 * kvh + kvh + h)),
                  qspec, sspec, sspec],
        out_specs=[qspec, kspec, kspec],
        out_shape=[jax.ShapeDtypeStruct((sd, dil * d), BF16),
                   jax.ShapeDtypeStruct((sd, dil * kvh * HEAD_DIM), F32),
                   jax.ShapeDtypeStruct((sd, dil * kvh * HEAD_DIM), F32)],
        scratch_shapes=[pltpu.VMEM((sd + ATT_BLOCK, HEAD_DIM), F32), pltpu.VMEM((sd + ATT_BLOCK, HEAD_DIM), F32)],
        compiler_params=_params("parallel", "parallel"),
    )(q, kv, kv, do, lse, delta)


def _to_branch(x, dil):
    s, w = x.shape
    return x.reshape(s // dil, dil * w)


def _heads_to_branch(x, dil, group):
    s = x.shape[0]
    x = x.reshape(s // dil, dil, N_KV_HEADS, group)
    return jnp.transpose(x, (1, 2, 0, 3)).reshape(dil * N_KV_HEADS, s // dil, group)


def _heads_from_branch(x, dil, group):
    sd = x.shape[1]
    x = x.reshape(dil, N_KV_HEADS, sd, group)
    return jnp.transpose(x, (2, 0, 1, 3)).reshape(sd * dil, N_KV_HEADS * group)


def _cast_bf16(name, w, layer, place):
    _, r, c = w.shape
    tr = min(512, r)

    def body(pl_ref, w_ref, o_ref):
        o_ref[...] = w_ref[...].astype(BF16)

    return pl.pallas_call(
        body, name=name,
        grid_spec=pltpu.PrefetchScalarGridSpec(
            num_scalar_prefetch=1, grid=(r // tr,),
            in_specs=[pl.BlockSpec((None, tr, c), lambda i, p: (layer, i, 0))],
            out_specs=pl.BlockSpec((None, tr, c), lambda i, p: (p[1], i, 0))),
        out_shape=jax.ShapeDtypeStruct((N_SHARD, r, c), BF16),
        compiler_params=_params("parallel"),
    )(place, w)


def _chip_sum(name, g, rh, place):
    _, r, c = g.shape
    rh2 = r // 2
    tr = min(512, rh2)
    nb = rh2 // tr

    def body(pl_ref, g_ref, rh_ref, o_ref):
        o_ref[...] = (g_ref[...].astype(F32) + rh_ref[...].astype(F32)).astype(BF16)

    return pl.pallas_call(
        body, name=name,
        grid_spec=pltpu.PrefetchScalarGridSpec(
            num_scalar_prefetch=1, grid=(N_SHARD, nb),
            in_specs=[pl.BlockSpec((None, tr, c), lambda s, i, p: (s, p[0] * nb + i, 0)),
                      pl.BlockSpec((None, tr, c), lambda s, i, p: (s, i, 0))],
            out_specs=pl.BlockSpec((None, tr, c), lambda s, i, p: (s, i, 0))),
        out_shape=jax.ShapeDtypeStruct((N_SHARD, rh2, c), BF16),
        compiler_params=_params("parallel", "parallel"),
    )(place, g, rh)


def _owner_sum(name, cs, rp, place):
    _, rh2, c = cs.shape
    tr = min(512, rh2)
    nb = rh2 // tr

    def body(pl_ref, cs_ref, r0_ref, r1_ref, r2_ref, o_ref):
        o_ref[...] = ((cs_ref[...].astype(F32) + r0_ref[...].astype(F32))
                      + (r1_ref[...].astype(F32) + r2_ref[...].astype(F32)))

    def rspec(j):
        return pl.BlockSpec((None, tr, c), lambda i, p: (j, i, 0))

    return pl.pallas_call(
        body, name=name,
        grid_spec=pltpu.PrefetchScalarGridSpec(
            num_scalar_prefetch=1, grid=(nb,),
            in_specs=[pl.BlockSpec((None, tr, c), lambda i, p: (p[1], i, 0)), rspec(0), rspec(1), rspec(2)],
            out_specs=pl.BlockSpec((tr, c), lambda i, p: (p[0] * nb + i, 0))),
        out_shape=jax.ShapeDtypeStruct((2 * rh2, c), F32),
        compiler_params=_params("parallel"),
    )(place, cs, rp, rp, rp)


def _adam_math(w, g, m, v):
    m = ADAM_B1 * m + (1.0 - ADAM_B1) * g
    v = ADAM_B2 * v + (1.0 - ADAM_B2) * (g * g)
    m_hat = m / (1.0 - ADAM_B1 ** ADAM_STEP)
    v_hat = v / (1.0 - ADAM_B2 ** ADAM_STEP)
    delta = -ADAM_LR * (m_hat / (jnp.sqrt(v_hat) + ADAM_EPS) + ADAM_WD * w)
    return delta, m, v


def _adamw(name, w, m, v, g, layer, partial=None):
    nl, r, c = w.shape
    tr = min(256, r)

    def body(w_ref, m_ref, v_ref, g_ref, *refs):
        go_ref, d_ref, mo_ref, vo_ref = refs[-4:]
        gv = g_ref[...]
        delta, m_new, v_new = _adam_math(w_ref[...], gv, m_ref[...], v_ref[...])
        go_ref[...] = gv
        d_ref[...] = delta
        mo_ref[...] = m_new
        vo_ref[...] = v_new

    wspec = pl.BlockSpec((None, tr, c), lambda i: (layer, i, 0))
    prev = [] if partial is None else list(partial)
    return pl.pallas_call(
        body, name=name, grid=(r // tr,),
        in_specs=[wspec] * 3 + [pl.BlockSpec((tr, c), lambda i: (i, 0))] + [ANY] * len(prev),
        out_specs=[wspec] * 4,
        out_shape=[jax.ShapeDtypeStruct((nl, r, c), F32)] * 4,
        input_output_aliases={4 + i: i for i in range(len(prev))},
        compiler_params=_params("parallel"),
    )(w, m, v, g, *prev)


def _adam_small(ws, ms, vs, gs):
    n = len(ws)

    def body(*refs):
        w_refs, m_refs, v_refs, g_refs = refs[:n], refs[n:2 * n], refs[2 * n:3 * n], refs[3 * n:4 * n]
        d_refs, mo_refs, vo_refs = refs[4 * n:5 * n], refs[5 * n:6 * n], refs[6 * n:7 * n]
        for i in range(n):
            delta, m_new, v_new = _adam_math(w_refs[i][...], g_refs[i][...], m_refs[i][...], v_refs[i][...])
            d_refs[i][...] = delta
            mo_refs[i][...] = m_new
            vo_refs[i][...] = v_new

    shapes = [jax.ShapeDtypeStruct(w.shape, F32) for w in ws]
    res = pl.pallas_call(body, name="adam_small", out_shape=shapes * 3)(*ws, *ms, *vs, *gs)
    return res[:n], res[n:2 * n], res[2 * n:]


def _pack_small(b_in, w_dw, b_dw, ln_g, ln_b, b_out, place):
    cin = b_in.shape[1]
    cd = b_dw.shape[1]
    rows = 8 + CONV_PAD

    def body(pl_ref, bi, wd, bd, lg, lb, bo, out):
        out[...] = jnp.zeros_like(out)
        out[0:1, :] = bi[...]
        out[1:2, 0:cd] = bd[...]
        out[1:2, cd:2 * cd] = lg[...]
        out[2:3, 0:cd] = lb[...]
        out[2:3, cd:2 * cd] = bo[...]
        out[8:8 + CONV_WIDTH, 0:cd] = wd[...]

    def whole(arr):
        return pl.BlockSpec(arr.shape, lambda i, p: (0,) * arr.ndim)

    ins = [b_in, w_dw, b_dw, ln_g, ln_b, b_out]
    return pl.pallas_call(
        body, name="pack_small",
        grid_spec=pltpu.PrefetchScalarGridSpec(
            num_scalar_prefetch=1, grid=(1,), in_specs=[whole(a) for a in ins],
            out_specs=pl.BlockSpec((None, rows, cin), lambda i, p: (p[1], 0, 0))),
        out_shape=jax.ShapeDtypeStruct((N_SHARD, rows, cin), F32),
        compiler_params=_params("arbitrary"),
    )(place, *ins)


def _place():
    x, y, c = lax.axis_index("x"), lax.axis_index("y"), lax.axis_index("c")
    return x, y, c


def _other_chips(x, y):
    return [(1 - x, y), (x, 1 - y), (1 - x, 1 - y)]


def _split_start(name, bufs, n_sem, copies, after=None):
    n = len(bufs)
    deps = [] if after is None else [after]

    def body(*refs):
        out0 = n + len(deps)
        for cp in copies(refs[:n], refs[out0], refs[out0 + 1], False):
            cp.start()
        refs[-1][...] = jnp.zeros_like(refs[-1])

    res = pl.pallas_call(
        body, name=name,
        out_shape=(pltpu.SemaphoreType.DMA((n_sem,)), pltpu.SemaphoreType.DMA((n_sem,)),
                   *[pltpu.HBM(b.shape, b.dtype) for b in bufs], jax.ShapeDtypeStruct((8, LANES), F32)),
        in_specs=[HBM] * n + [ANY] * len(deps),
        out_specs=(SEM, SEM, *[HBM] * n, pl.BlockSpec(memory_space=pltpu.VMEM)),
        input_output_aliases={i: 2 + i for i in range(n)},
        compiler_params=pltpu.CompilerParams(has_side_effects=SPLIT_EFFECT),
    )(*[pltpu.with_memory_space_constraint(b, pltpu.HBM) for b in bufs], *deps)
    return res[0], res[1], list(res[2:2 + n]), res[-1]


def _split_wait(name, handle, copies, after):
    ssem, rsem, bufs, _ = handle
    n = len(bufs)

    def body(*refs):
        for cp in copies(refs[:n], refs[n], refs[n + 1], True):
            cp.wait_send()
            cp.wait_recv()

    res = pl.pallas_call(
        body, name=name,
        out_shape=[pltpu.HBM(b.shape, b.dtype) for b in bufs],
        in_specs=[HBM] * n + [SEM, SEM, ANY], out_specs=[HBM] * n,
        input_output_aliases={i: i for i in range(n)},
        compiler_params=pltpu.CompilerParams(has_side_effects=SPLIT_EFFECT),
    )(*bufs, ssem, rsem, after)
    return list(res)


def _remote(src, dst, ssem, rsem, k, to):
    return pltpu.make_async_remote_copy(src_ref=src, dst_ref=dst, send_sem=ssem.at[k], recv_sem=rsem.at[k],
                                        device_id=to, device_id_type=MESH)


def _gather_copies(refs, ssem, rsem, landing, n_whole=0):
    x, y, c = _place()
    me = 2 * x + y
    cps = []
    for a, ref in enumerate(refs):
        whole = a >= len(refs) - n_whole
        rh = ref.shape[1] // 2
        for j, (px, py) in enumerate(_other_chips(x, y)):
            shard = 2 * px + py if landing else me
            src = ref.at[me] if whole else ref.at[me, pl.ds(c * rh, rh)]
            dst = ref.at[shard] if whole else ref.at[shard, pl.ds(c * rh, rh)]
            cps.append(_remote(src, dst, ssem, rsem, 3 * a + j, (px, py, c)))
    return cps


def _forward_copies(refs, ssem, rsem, landing):
    x, y, c = _place()
    who = 1 - c if landing else c
    cps = []
    for a, ref in enumerate(refs):
        rh = ref.shape[1] // 2
        for j, (px, py) in enumerate(_other_chips(x, y)):
            piece = ref.at[2 * px + py, pl.ds(who * rh, rh)]
            cps.append(_remote(piece, piece, ssem, rsem, 3 * a + j, (x, y, 1 - c)))
    return cps


def _sibling_copies(refs, ssem, rsem, landing):
    x, y, c = _place()
    n = len(refs) // 2
    cps = []
    for a in range(n):
        rh = refs[a].shape[1] // 2
        cps.append(_remote(refs[a].at[:, pl.ds((1 - c) * rh, rh), :], refs[n + a], ssem, rsem, a, (x, y, 1 - c)))
    return cps


def _owner_copies(refs, ssem, rsem, landing):
    x, y, c = _place()
    n = len(refs) // 2
    cps = []
    for a in range(n):
        for j, (px, py) in enumerate(_other_chips(x, y)):
            cps.append(_remote(refs[a].at[2 * px + py], refs[n + a].at[j], ssem, rsem, 3 * a + j, (px, py, c)))
    return cps


def _swap_copies(refs, ssem, rsem, landing):
    x, y, c = _place()
    who = 1 - c if landing else c
    cps = []
    for a, ref in enumerate(refs):
        rh = ref.shape[0] // 2
        rows = ref.at[pl.ds(who * rh, rh)]
        cps.append(_remote(rows, rows, ssem, rsem, a, (x, y, 1 - c)))
    return cps


def _small_copies(refs, ssem, rsem, landing):
    pack, slots = refs
    x, y, c = _place()
    cps = []
    for rel in range(1, N_DEV):
        px = 1 - x if (rel >> 2) & 1 else x
        py = 1 - y if (rel >> 1) & 1 else y
        pc = 1 - c if rel & 1 else c
        slot = 4 * px + 2 * py + pc if landing else 4 * x + 2 * y + c
        cps.append(_remote(pack, slots.at[slot], ssem, rsem, rel - 1, (px, py, pc)))
    return cps


def _small_pack(rows, w_dw_grad, d):
    n = len(rows)

    def body(*refs):
        pack = refs[-1]
        pack[...] = jnp.zeros_like(pack)
        for (r, _), ref in zip(rows, refs[:n]):
            pack[r:r + 1, :] = ref[...]
        pack[16:16 + CONV_PAD, :] = refs[n][...]

    return pl.pallas_call(body, name="small_pack", out_shape=jax.ShapeDtypeStruct((SMALL_ROWS, d), F32))(
        *[v for _, v in rows], w_dw_grad)


def _small_sum(pack, slots, place):
    rows, d = pack.shape
    loss_row = 12

    def body(pl_ref, pack_ref, slots_ref, out_ref):
        me = pl_ref[2]
        tot = jnp.where(me == 0, pack_ref[...], slots_ref[0])
        for i in range(1, N_DEV):
            tot = tot + jnp.where(me == i, pack_ref[...], slots_ref[i])
        out_ref[...] = tot
        out_ref[loss_row:loss_row + 1, :] = jnp.zeros((1, d), F32) + jnp.sum(tot[loss_row:loss_row + 1, :])

    return pl.pallas_call(
        body, name="small_sum",
        grid_spec=pltpu.PrefetchScalarGridSpec(
            num_scalar_prefetch=1, grid=(1,),
            in_specs=[pl.BlockSpec((rows, d), lambda i, p: (0, 0)), pl.BlockSpec((N_DEV, rows, d), lambda i, p: (0, 0, 0))],
            out_specs=pl.BlockSpec((rows, d), lambda i, p: (0, 0))),
        out_shape=jax.ShapeDtypeStruct((rows, d), F32),
        compiler_params=_params("arbitrary"),
    )(place, pack, slots)


def kernel(x, norm_mix, norm_mlp, conv_w_in, conv_b_in, conv_w_dw, conv_b_dw, conv_ln_g, conv_ln_b, conv_w_out, conv_b_out, kv_norm, w_kv, attn_w_q, attn_w_o, mlp_w_in, mlp_w_out, final_norm, loss_target, m_norm_mix, m_norm_mlp, m_conv_w_in, m_conv_b_in, m_conv_w_dw, m_conv_b_dw, m_conv_ln_g, m_conv_ln_b, m_conv_w_out, m_conv_b_out, m_kv_norm, m_w_kv, m_attn_w_q, m_attn_w_o, m_mlp_w_in, m_mlp_w_out, m_final_norm, v_norm_mix, v_norm_mlp, v_conv_w_in, v_conv_b_in, v_conv_w_dw, v_conv_b_dw, v_conv_ln_g, v_conv_ln_b, v_conv_w_out, v_conv_b_out, v_kv_norm, v_w_kv, v_attn_w_q, v_attn_w_o, v_mlp_w_in, v_mlp_w_out, v_final_norm):
    _, s, d = x.shape
    dff = mlp_w_in.shape[2] * N_SHARD
    kvw = w_kv.shape[1]
    nh = d // HEAD_DIM
    group = nh // N_KV_HEADS
    ds4 = d // N_SHARD
    xi, yi, ci = _place()
    me = 2 * xi + yi
    place = jnp.stack([ci, me, 2 * me + ci]).astype(I32)

    h0 = x.reshape(s, d)
    target = loss_target.reshape(s, d)
    tabs = _rope_tables(s)

    def gather_begin(tag, bufs, n_whole=0):
        return _split_start(f"gather_start_{tag}", bufs, 3 * len(bufs),
                            functools.partial(_gather_copies, n_whole=n_whole)), n_whole

    def gather_land(tag, begun, later):
        handle, n_whole = begun
        bufs = _split_wait(f"gather_wait_{tag}", handle, functools.partial(_gather_copies, n_whole=n_whole), later)
        n_half = len(bufs) - n_whole
        fwd = _split_start(f"forward_start_{tag}", bufs[:n_half], 3 * n_half, _forward_copies)
        return fwd, bufs[n_half:]

    def gather_end(tag, landed, later):
        fwd, whole = landed
        return _split_wait(f"forward_wait_{tag}", fwd, _forward_copies, later) + whole

    ag_cin = gather_begin("conv_in", [
        _cast_bf16("cast_w_in", conv_w_in, 0, place),
        _pack_small(conv_b_in, conv_w_dw.reshape(CONV_WIDTH, ds4), conv_b_dw, conv_ln_g, conv_ln_b, conv_b_out, place),
    ], n_whole=1)
    ag_cout = gather_begin("conv_out", [_cast_bf16("cast_w_out", conv_w_out, 0, place)])
    ag_mi0 = gather_begin("mlp_in0", [_cast_bf16("cast_mlp_in0", mlp_w_in, 0, place)])
    ag_mo0 = gather_begin("mlp_out0", [_cast_bf16("cast_mlp_out0", mlp_w_out, 0, place)])
    ag_attn = gather_begin("attn", [
        _cast_bf16("cast_w_kv", w_kv.reshape(1, ds4, kvw), 0, place), _cast_bf16("cast_w_q", attn_w_q, 0, place),
        _cast_bf16("cast_w_o", attn_w_o, 0, place)])
    ag_mlp1 = gather_begin("mlp1", [
        _cast_bf16("cast_mlp_in1", mlp_w_in, 1, place), _cast_bf16("cast_mlp_out1", mlp_w_out, 1, place)])

    wmi_g = [None, None]
    wmo_f = [None, None]

    nm = [norm_mix[0:1], norm_mix[1:2]]
    nmlp = [norm_mlp[0:1], norm_mlp[1:2]]
    kvn = kv_norm.reshape(1, d)
    fin = final_norm.reshape(1, d)
    started = sum(h[0][3][0:1, 0:1] for h in (ag_cin, ag_cout, ag_mi0, ag_mo0, ag_attn, ag_mlp1))
    (y0,) = _rms_fwd("rms_mix0", h0, [nm[0] + started])

    w_in_g, small_g = gather_end("conv_in", gather_land("conv_in", ag_cin, y0), y0)
    b_in_f = small_g[:, 0, :].reshape(1, 2 * d)
    b_dw_f = small_g[:, 1, 0:ds4].reshape(1, d)
    ln_g_f = small_g[:, 1, ds4:2 * ds4].reshape(1, d)
    ln_b_f = small_g[:, 2, 0:ds4].reshape(1, d)
    b_out_f = small_g[:, 2, ds4:2 * ds4].reshape(1, d)
    w_dw_f = jnp.transpose(small_g[:, 8:8 + CONV_PAD, 0:ds4], (1, 0, 2)).reshape(CONV_PAD, d)

    def ep_bias(acc, ex, outs, j):
        outs[0][...] = (acc + ex[0][...]).astype(outs[0].dtype)

    def ep_residual(acc, ex, outs, j):
        outs[0][...] = ex[0][...] + acc

    def ep_residual_bias(acc, ex, outs, j):
        outs[0][...] = ex[0][...] + (acc + ex[1][...])

    def ep_relu2(acc, ex, outs, j):
        r = jnp.maximum(acc, 0.0)
        outs[0][...] = r.astype(BF16)
        outs[1][...] = (r * r).astype(BF16)

    def ep_rope(acc, ex, outs, j):
        outs[0][...] = _rope_apply(acc, ex[0][...], ex[1][...], ex[2][...], 1.0).astype(BF16)

    def ep_rope_k(acc, ex, outs, j):
        roped = _rope_apply(acc, ex[0][...], ex[1][...], ex[2][...], 1.0)
        outs[0][...] = jnp.where(j == 0, roped, acc).astype(BF16)

    tab_extras = [(t, "rows") for t in tabs]

    def mlp_fwd(idx, h, y, out_weight):
        r, r2 = _matmul(f"mlp_in{idx}", "nn", y, wmi_g[idx], b_kind="col", m=s, n=dff, k=d,
                        outs=[(BF16, "plain"), (BF16, "plain")], epilogue=ep_relu2)
        wmo_f[idx] = out_weight(r2).reshape(dff, d)
        (h_new,) = _matmul(f"mlp_out{idx}", "nn", r2, wmo_f[idx], m=s, n=d, k=dff,
                           outs=[(F32, "plain")], extras=[(h, "ij")], epilogue=ep_residual)
        return h_new, r, r2

    (u,) = _matmul("conv_in", "nn", y0, w_in_g, b_kind="col", m=s, n=2 * d, k=d,
                   outs=[(BF16, "plain")], extras=[(b_in_f, "vec")], epilogue=ep_bias)
    land_cout = gather_land("conv_out", ag_cout, u)
    cpre = _dwconv_fwd(u, w_dw_f, b_dw_f + land_cout[0][3][0:1, 0:1])
    sact = _ln_silu_fwd(cpre, ln_g_f, ln_b_f)
    (w_out_g,) = gather_end("conv_out", land_cout, sact)
    w_out_f = w_out_g.reshape(d, d)
    (h1,) = _matmul("conv_out", "nn", sact, w_out_f, m=s, n=d, k=d,
                    outs=[(F32, "plain")], extras=[(h0, "ij"), (b_out_f, "vec")], epilogue=ep_residual_bias)
    (y1,) = _rms_fwd("rms_mlp0", h1, [nmlp[0]])
    (wmi_g[0],) = gather_end("mlp_in0", gather_land("mlp_in0", ag_mi0, y1), y1)
    h2, r0, r0sq = mlp_fwd(0, h1, y1, lambda r2: gather_end("mlp_out0", gather_land("mlp_out0", ag_mo0, r2), r2)[0])
    land_attn = gather_land("attn", ag_attn, h2)
    ykv, y2 = _rms_fwd("rms_kv_mix1", h2, [kvn + land_attn[0][3][0:1, 0:1], nm[1]])
    wkv_g, wq_g, wo_g = gather_end("attn", land_attn, y2)
    wkv_f, wq_f, wo_f = wkv_g.reshape(d, kvw), wq_g.reshape(d, d), wo_g.reshape(d, d)
    (kv,) = _matmul("kv_proj", "nn", ykv, wkv_f, m=s, n=kvw, k=d, tn=kvw // 2,
                    outs=[(BF16, "plain")], extras=tab_extras, epilogue=ep_rope_k)
    (q,) = _matmul("q_proj", "nn", y2, wq_f, m=s, n=d, k=d,
                   outs=[(BF16, "plain")], extras=tab_extras, epilogue=ep_rope)
    o_parts, lse_parts = [], []
    for dil in DILATIONS:
        o_b, lse_b = _attn_fwd(f"attn_fwd_d{dil}", _to_branch(q, dil), _to_branch(kv, dil), dil, d)
        o_parts.append(o_b.reshape(s, d))
        lse_parts.append(_heads_from_branch(lse_b, dil, group))
    o, lse = _attn_combine(o_parts, lse_parts)
    land_mlp1 = gather_land("mlp1", ag_mlp1, o)
    (h3,) = _matmul("attn_out", "nn", o, wo_f, m=s, n=d, k=d,
                    outs=[(F32, "plain")], extras=[(h2, "ij")], epilogue=ep_residual)
    (y3,) = _rms_fwd("rms_mlp1", h3, [nmlp[1]])
    wmi_g[1], wmo1_g = gather_end("mlp1", land_mlp1, y3)
    h4, r1, r1sq = mlp_fwd(1, h3, y3, lambda r2: wmo1_g)
    dh4, dh4b, d_fin, loss_cols = _final_loss(h4, fin, target)

    def ep_relu2_bwd(acc, ex, outs, j):
        outs[0][...] = (acc * (2.0 * ex[0][...].astype(F32))).astype(BF16)

    def mlp_bwd(idx, dhb, y, r, r2):
        (dz,) = _matmul(f"mlp_out{idx}_dx", "nt", dhb, wmo_f[idx], m=s, n=dff, k=d,
                        outs=[(BF16, "plain")], extras=[(r, "ij")], epilogue=ep_relu2_bwd)
        (dwo,) = _matmul(f"mlp_out{idx}_dw", "tn", r2, dhb, m=dff, n=d, k=s,
                         outs=[(BF16, "plain")])
        (dy,) = _matmul(f"mlp_in{idx}_dx", "nt", dz, wmi_g[idx], b_kind="col", m=s, n=d, k=dff,
                        outs=[(BF16, "plain")])
        (dwi,) = _matmul(f"mlp_in{idx}_dw", "tn", y, dz, m=d, n=dff, k=s,
                         outs=[(BF16, "col")])
        return dy, dwi, dwo.reshape(N_SHARD, dff // N_SHARD, d)

    def token(handle):
        return handle[3][0:1, 0:1]

    def rs_exchange(tag, grads):
        lands = [lax.empty((N_SHARD, g.shape[1] // 2, g.shape[2]), g.dtype) for g in grads]
        return _split_start(f"sibling_start_{tag}", list(grads) + lands, len(grads), _sibling_copies)

    def rs_send(tag, names, exchanged, later):
        bufs = _split_wait(f"sibling_wait_{tag}", exchanged, _sibling_copies, later)
        n = len(names)
        sums = [_chip_sum(f"chip_sum_{nme}", g, rh, place) for nme, g, rh in zip(names, bufs[:n], bufs[n:])]
        lands = [lax.empty((N_SHARD - 1,) + cs.shape[1:], cs.dtype) for cs in sums]
        return _split_start(f"owners_start_{tag}", sums + lands, 3 * n, _owner_copies)

    def rs_sum(tag, names, sent, later):
        bufs = _split_wait(f"owners_wait_{tag}", sent, _owner_copies, later)
        n = len(names)
        own = [_owner_sum(f"owner_sum_{nme}", cs, rp, place) for nme, cs, rp in zip(names, bufs[:n], bufs[n:])]
        return _split_start(f"swap_start_{tag}", own, n, _swap_copies)

    def rs_end(tag, swapped, later):
        return _split_wait(f"swap_wait_{tag}", swapped, _swap_copies, later)

    dy3, g_wmi1, g_wmo1 = mlp_bwd(1, dh4b, y3, r1, r1sq)
    x_mlp1 = rs_exchange("mlp1", [g_wmi1, g_wmo1])
    dh3, dh3b, d_nmlp1 = _rms_bwd("rms_mlp1_bwd", h3, [(nmlp[1] + token(x_mlp1), dy3)], dh4)

    (do,) = _matmul("attn_out_dx", "nt", dh3b, wo_f, m=s, n=d, k=d, outs=[(BF16, "plain")])
    (g_wo,) = _matmul("attn_out_dw", "tn", o, dh3b, m=d, n=d, k=s, outs=[(BF16, "plain")])
    rs_mlp1 = rs_send("mlp1", ["mlp_in1", "mlp_out1"], x_mlp1, g_wo)
    delta = _attn_delta(do, o)
    lse = lse + token(rs_mlp1)
    dq_parts, dk_parts, dv_parts = [], [], []
    for dil in DILATIONS:
        dq_b, dk_b, dv_b = _attn_bwd(
            f"attn_bwd_d{dil}", _to_branch(q, dil), _to_branch(kv, dil), _to_branch(do, dil),
            _heads_to_branch(lse, dil, group), _heads_to_branch(delta, dil, group), dil, d)
        dq_parts.append(dq_b.reshape(s, d))
        dk_parts.append(dk_b.reshape(s, kvw // 2))
        dv_parts.append(dv_b.reshape(s, kvw // 2))
    dq = _rope_bwd_sum("rope_bwd_q", dq_parts, tabs, d)
    dkv_parts = [jnp.concatenate([a, b], axis=1) for a, b in zip(dk_parts, dv_parts)]
    dkv = _rope_bwd_sum("rope_bwd_kv", dkv_parts, tabs, kvw // 2)
    (g_wq,) = _matmul("q_proj_dw", "tn", y2, dq, m=d, n=d, k=s, outs=[(BF16, "plain")])
    (dy2,) = _matmul("q_proj_dx", "nt", dq, wq_f, m=s, n=d, k=d, outs=[(BF16, "plain")])
    (g_wkv,) = _matmul("kv_proj_dw", "tn", ykv, dkv, m=d, n=kvw, k=s, outs=[(BF16, "plain")])
    (dykv,) = _matmul("kv_proj_dx", "nt", dkv, wkv_f, m=s, n=d, k=kvw, outs=[(BF16, "plain")])
    x_attn = rs_exchange("attn", [g_wkv.reshape(N_SHARD, ds4, kvw), g_wq.reshape(N_SHARD, ds4, d),
                                  g_wo.reshape(N_SHARD, ds4, d)])
    dh2, dh2b, d_nm1, d_kvn = _rms_bwd("rms_kv_mix1_bwd", h2, [(nm[1] + token(x_attn), dy2), (kvn, dykv)], dh3)
    rs_attn = rs_send("attn", ["w_kv", "w_q", "w_o"], x_attn, dh2b)

    dy1, g_wmi0, g_wmo0 = mlp_bwd(0, dh2b, y1, r0, r0sq)
    x_mlp0 = rs_exchange("mlp0", [g_wmi0, g_wmo0])
    dh1, dh1b, d_nmlp0, d_b_out = _rms_bwd("rms_mlp0_bwd", h1, [(nmlp[0] + token(x_mlp0) + token(rs_attn), dy1)],
                                           dh2, want_colsum=True)

    (dsact,) = _matmul("conv_out_dx", "nt", dh1b, w_out_f, m=s, n=d, k=d, outs=[(BF16, "plain")])
    (g_wout,) = _matmul("conv_out_dw", "tn", sact, dh1b, m=d, n=d, k=s, outs=[(BF16, "plain")])
    rs_mlp0 = rs_send("mlp0", ["mlp_in0", "mlp_out0"], x_mlp0, g_wout)
    dc, d_ln_g, d_ln_b, d_b_dw = _ln_silu_bwd(cpre, ln_g_f + token(rs_mlp0), ln_b_f, dsact)
    da, dgt, d_w_dw, d_b_in_a, d_b_in_g = _dwconv_bwd(u, w_dw_f, dc)
    du = jnp.concatenate([da, dgt], axis=1)
    (g_win,) = _matmul("conv_in_dw", "tn", y0, du, m=d, n=2 * d, k=s, outs=[(BF16, "col")])
    x_conv = rs_exchange("conv", [g_win, g_wout.reshape(N_SHARD, ds4, d)])
    (dy0,) = _matmul("conv_in_dx", "nt", du, w_in_g, b_kind="col", m=s, n=d, k=2 * d,
                     outs=[(BF16, "plain")])
    dx, _, d_nm0 = _rms_bwd("rms_mix0_bwd", h0, [(nm[0] + token(x_conv), dy0)], dh1)

    small_rows = [(0, d_nm0), (1, d_nm1), (2, d_nmlp0), (3, d_nmlp1), (4, d_kvn), (5, d_fin), (6, d_b_dw),
                  (7, d_ln_g), (8, d_ln_b), (9, d_b_out), (10, d_b_in_a), (11, d_b_in_g), (12, loss_cols)]
    x_small = _split_start("small_start", [_small_pack(small_rows, d_w_dw, d),
                                           lax.empty((N_DEV, SMALL_ROWS, d), F32)], N_DEV - 1, _small_copies)
    rs_conv = rs_send("conv", ["w_in", "w_out"], x_conv, x_small[3])

    def big(name, w, m, v, g, layer=0, partial=None):
        shape = w.shape
        w3, m3, v3 = [t.reshape((-1,) + shape[-2:]) for t in (w, m, v)]
        if partial is not None:
            partial = [t.reshape(w3.shape) for t in partial]
        res = _adamw(name, w3, m3, v3, g, layer, partial)
        return [t.reshape(shape) for t in res]

    sw_mlp1 = rs_sum("mlp1", ["mlp_in1", "mlp_out1"], rs_mlp1, rs_conv[3])
    sw_attn = rs_sum("attn", ["w_kv", "w_q", "w_o"], rs_attn, sw_mlp1[3])
    f_wmi1, f_wmo1 = rs_end("mlp1", sw_mlp1, sw_attn[3])
    p_wmi = big("adam_mlp_in1", mlp_w_in, m_mlp_w_in, v_mlp_w_in, f_wmi1, 1)
    p_wmo = big("adam_mlp_out1", mlp_w_out, m_mlp_w_out, v_mlp_w_out, f_wmo1, 1)
    sw_mlp0 = rs_sum("mlp0", ["mlp_in0", "mlp_out0"], rs_mlp0, p_wmo[0])
    f_wkv, f_wq, f_wo = rs_end("attn", sw_attn, sw_mlp0[3])
    r_wkv = big("adam_w_kv", w_kv, m_w_kv, v_w_kv, f_wkv)
    r_wq = big("adam_w_q", attn_w_q, m_attn_w_q, v_attn_w_q, f_wq)
    r_wo = big("adam_w_o", attn_w_o, m_attn_w_o, v_attn_w_o, f_wo)
    sw_conv = rs_sum("conv", ["w_in", "w_out"], rs_conv, r_wo[0])
    f_wmi0, f_wmo0 = rs_end("mlp0", sw_mlp0, sw_conv[3])
    r_wmi = big("adam_mlp_in0", mlp_w_in, m_mlp_w_in, v_mlp_w_in, f_wmi0, 0, p_wmi)
    r_wmo = big("adam_mlp_out0", mlp_w_out, m_mlp_w_out, v_mlp_w_out, f_wmo0, 0, p_wmo)
    f_win, f_wout = rs_end("conv", sw_conv, r_wmo[0])
    r_win = big("adam_w_in", conv_w_in, m_conv_w_in, v_conv_w_in, f_win)
    r_wout = big("adam_w_out", conv_w_out, m_conv_w_out, v_conv_w_out, f_wout)

    small_pack, small_slots = _split_wait("small_wait", x_small, _small_copies, r_wout[0])
    red = _small_sum(small_pack, small_slots, place)
    loss = red[12, 0]
    g_norm_mix = red[0:2]
    g_norm_mlp = red[2:4]
    g_kv_norm = red[4:5]
    g_final = red[5:6]

    def my_cols(row):
        return lax.dynamic_slice(red, (row, me * ds4), (1, ds4))

    g_b_dw, g_ln_g, g_ln_b, g_b_out = my_cols(6), my_cols(7), my_cols(8), my_cols(9)
    half_in = 2 * d // N_SHARD
    b_in_row = 10 + me // 2
    g_b_in = lax.dynamic_slice(red, (b_in_row, (me % 2) * half_in), (1, half_in))
    g_w_dw = lax.dynamic_slice(red, (16, me * ds4), (CONV_WIDTH, ds4))

    sm_w =[norm_mix, norm_mlp, conv_b_in, conv_w_dw.reshape(CONV_WIDTH, ds4), conv_b_dw, conv_ln_g, conv_ln_b,
            conv_b_out, kv_norm.reshape(1, d), final_norm.reshape(1, d)]
    sm_m = [m_norm_mix, m_norm_mlp, m_conv_b_in, m_conv_w_dw.reshape(CONV_WIDTH, ds4), m_conv_b_dw, m_conv_ln_g,
            m_conv_ln_b, m_conv_b_out, m_kv_norm.reshape(1, d), m_final_norm.reshape(1, d)]
    sm_v = [v_norm_mix, v_norm_mlp, v_conv_b_in, v_conv_w_dw.reshape(CONV_WIDTH, ds4), v_conv_b_dw, v_conv_ln_g,
            v_conv_ln_b, v_conv_b_out, v_kv_norm.reshape(1, d), v_final_norm.reshape(1, d)]
    sm_g = [g_norm_mix, g_norm_mlp, g_b_in, g_w_dw, g_b_dw, g_ln_g, g_ln_b, g_b_out, g_kv_norm, g_final]
    sm_d, sm_nm, sm_nv = _adam_small(sm_w, sm_m, sm_v, sm_g)
    shapes = [norm_mix.shape, norm_mlp.shape, conv_b_in.shape, conv_w_dw.shape, conv_b_dw.shape, conv_ln_g.shape,
              conv_ln_b.shape, conv_b_out.shape, kv_norm.shape, final_norm.shape]
    sm_g, sm_d, sm_nm, sm_nv = [[t.reshape(sh) for t, sh in zip(lst, shapes)] for lst in (sm_g, sm_d, sm_nm, sm_nv)]

    def order(sm, idx):
        return [sm[0], sm[1], r_win[idx], sm[2], sm[3], sm[4], sm[5], sm[6], r_wout[idx], sm[7], sm[8],
                r_wkv[idx], r_wq[idx], r_wo[idx], r_wmi[idx], r_wmo[idx], sm[9]]

    return (loss, dx.reshape(x.shape), *order(sm_g, 0), *order(sm_d, 1), *order(sm_nm, 2), *order(sm_nv, 3))
```

```python
import functools
import math

import jax
import jax.numpy as jnp
from jax import lax
from jax.experimental import pallas as pl
from jax.experimental.pallas import tpu as pltpu

F32 = jnp.float32
BF16 = jnp.bfloat16
I32 = jnp.int32

NORM_EPS = 1e-6
LN_EPS = 1e-5
HEAD_DIM = 128
N_KV_HEADS = 4
ROT_DIM = 32
ROPE_THETA = 500000.0
CONV_WIDTH = 31
CONV_PAD = 32
ATT_BLOCK = 128
DILATIONS = (1, 4, 16)
ADAM_LR = 0.001
ADAM_B1 = 0.9
ADAM_B2 = 0.999
ADAM_EPS = 1e-08
ADAM_WD = 0.01
ADAM_STEP = 10
N_SHARD = 4
N_DEV = 8
LANES = 128
VMEM_LIMIT = 48 * 1024 * 1024
MM_TM, MM_TN, MM_TK = 1024, 1024, 2048
ROW_TILE = 256
CONV_CB = 128
CONV_T = 128
SMALL_ROWS = 48
MESH = pl.DeviceIdType.MESH
ANY = pl.BlockSpec(memory_space=pl.ANY)
HBM = pl.BlockSpec(memory_space=pltpu.HBM)
SEM = pl.BlockSpec(memory_space=pltpu.SEMAPHORE)
SPLIT_EFFECT = pltpu.SideEffectType.DATAFLOW_SIDE_EFFECTING


def _params(*sem):
    return pltpu.CompilerParams(dimension_semantics=sem, vmem_limit_bytes=VMEM_LIMIT)


def _sigmoid(x):
    return 1.0 / (1.0 + jnp.exp(-x))


def _wspec(kind, arr_shape, br, bc, pick):
    if kind == "plain":
        return pl.BlockSpec((br, bc), pick)
    per = arr_shape[2] // bc

    def idx(*g):
        rb, cb = pick(*g)
        return (cb // per, rb, cb % per)

    return pl.BlockSpec((None, br, bc), idx)


def _stage_shape(rows, w):
    return (w // LANES, rows, LANES)


def _to_residues(val, stage_ref, out_refs, dils):
    planes, rows, _ = stage_ref.shape
    for c in range(planes):
        stage_ref[c] = val[:, c * LANES:(c + 1) * LANES]
    for out_ref, dil in zip(out_refs, dils):
        if dil == 1:
            out_ref[0] = val.astype(out_ref.dtype)
            continue
        for r in range(dil):
            for c in range(planes):
                out_ref[r, :, c * LANES:(c + 1) * LANES] = stage_ref.at[c][pl.ds(r, rows // dil, stride=dil), :].astype(
                    out_ref.dtype)


def _from_residues(src_ref, stage_ref, dil):
    planes, rows, _ = stage_ref.shape
    if dil == 1:
        return lambda c: src_ref[0, :, c * LANES:(c + 1) * LANES].astype(F32)
    for r in range(dil):
        for c in range(planes):
            stage_ref.at[c][pl.ds(r, rows // dil, stride=dil), :] = src_ref[r, :, c * LANES:(c + 1) * LANES].astype(F32)
    return lambda c: stage_ref[c]


def _matmul(name, mode, a, b, *, m, n, k, tn=MM_TN, b_kind="plain", outs, extras=(), epilogue=None, stage=False):
    tm, tn, tk = min(MM_TM, m), min(tn, n), min(MM_TK, k)
    if b_kind == "col" and mode == "nn":
        tn = min(tn, n // N_SHARD)
    if b_kind == "col" and mode == "nt":
        tk = min(tk, k // N_SHARD)
    if any(kind == "col" for _, kind in outs):
        tn = min(tn, n // N_SHARD)
    assert m % tm == 0 and n % tn == 0 and k % tk == 0, (name, m, n, k, tm, tn, tk)
    nk = k // tk
    grid = (m // tm, n // tn, nk)
    if mode == "nn":
        a_spec = pl.BlockSpec((tm, tk), lambda i, j, kk: (i, kk))
        b_spec = _wspec(b_kind, b.shape, tk, tn, lambda i, j, kk: (kk, j))
        dims = (((1,), (0,)), ((), ()))
    elif mode == "nt":
        a_spec = pl.BlockSpec((tm, tk), lambda i, j, kk: (i, kk))
        b_spec = _wspec(b_kind, b.shape, tn, tk, lambda i, j, kk: (j, kk))
        dims = (((1,), (1,)), ((), ()))
    else:
        a_spec = pl.BlockSpec((tk, tm), lambda i, j, kk: (kk, i))
        b_spec = pl.BlockSpec((tk, tn), lambda i, j, kk: (kk, j))
        dims = (((0,), (0,)), ((), ()))
    out_shape, out_specs = [], []
    for dtype, kind in outs:
        if isinstance(kind, tuple):
            dil = kind[1]
            out_shape.append(jax.ShapeDtypeStruct((dil, m // dil, n), dtype))
            out_specs.append(pl.BlockSpec((dil, tm // dil, tn), lambda i, j, kk: (0, i, j)))
            continue
        shape = (m, n) if kind == "plain" else (N_SHARD, m, n // N_SHARD)
        out_shape.append(jax.ShapeDtypeStruct(shape, dtype))
        out_specs.append(_wspec(kind, shape, tm, tn, lambda i, j, kk: (i, j)))
    n_ex = len(extras)
    ex_specs = {"ij": pl.BlockSpec((tm, tn), lambda i, j, kk: (i, j)),
                "vec": pl.BlockSpec((1, tn), lambda i, j, kk: (0, j)),
                "rows": pl.BlockSpec((tm, LANES), lambda i, j, kk: (i, 0))}

    def body(*refs):
        a_ref, b_ref = refs[0], refs[1]
        ex_refs = refs[2:2 + n_ex]
        out_refs = refs[2 + n_ex:2 + n_ex + len(outs)]
        j = pl.program_id(1)

        def finish(res):
            if epilogue is None:
                out_refs[0][...] = res.astype(out_refs[0].dtype)
            elif stage:
                epilogue(res, ex_refs, out_refs, j, refs[-1])
            else:
                epilogue(res, ex_refs, out_refs, j)

        prod = lax.dot_general(a_ref[...], b_ref[...], dims, preferred_element_type=F32)
        if nk == 1:
            finish(prod)
            return
        acc_ref = refs[2 + n_ex + len(outs)]
        kk = pl.program_id(2)

        @pl.when(kk == 0)
        def _():
            acc_ref[...] = prod

        @pl.when(kk > 0)
        def _():
            acc_ref[...] += prod

        @pl.when(kk == nk - 1)
        def _():
            finish(acc_ref[...])

    res = pl.pallas_call(
        body, name=name, grid=grid,
        in_specs=[a_spec, b_spec] + [ex_specs[how] for _, how in extras],
        out_specs=out_specs, out_shape=out_shape,
        scratch_shapes=[pltpu.VMEM((tm, tn), F32)] * (nk > 1) + [pltpu.VMEM(_stage_shape(tm, tn), F32)] * bool(stage),
        compiler_params=_params("parallel", "parallel", "arbitrary"),
    )(a, b, *[e for e, _ in extras])
    return res


def _rope_tables(seq):
    half = ROT_DIM // 2
    pos = jnp.arange(seq, dtype=F32)
    inv = ROPE_THETA ** (-jnp.arange(0, ROT_DIM, 2, dtype=F32) / ROT_DIM)
    ang = pos[:, None] * inv[None, :]
    cos, sin = jnp.cos(ang), jnp.sin(ang)
    zeros = jnp.zeros((seq, HEAD_DIM - ROT_DIM), F32)
    ctab = jnp.concatenate([cos, cos, zeros + 1.0], axis=1)
    atab = jnp.concatenate([-sin, jnp.zeros((seq, half), F32), zeros], axis=1)
    btab = jnp.concatenate([jnp.zeros((seq, half), F32), sin, zeros], axis=1)
    return ctab, atab, btab


def _rope_apply(x, ctab, atab, btab, sign):
    w = x.shape[1]
    reps = w // HEAD_DIM
    half = ROT_DIM // 2
    c = jnp.tile(ctab, (1, reps))
    a = jnp.tile(atab, (1, reps))
    b = jnp.tile(btab, (1, reps))
    up = pltpu.roll(x, w - half, 1)
    down = pltpu.roll(x, half, 1)
    return x * c + sign * (up * a + down * b)


def _rows(t, w):
    return pl.BlockSpec((t, w), lambda i: (i, 0))


def _fixed(shape):
    nd = len(shape)
    return pl.BlockSpec(shape, lambda i: (0,) * nd)


def _rms_fwd(name, x, gains):
    s, d = x.shape
    t = min(ROW_TILE, s)
    ng = len(gains)

    def body(x_ref, *refs):
        xv = x_ref[...]
        r = lax.rsqrt(jnp.mean(xv * xv, axis=-1, keepdims=True) + NORM_EPS)
        xn = xv * r
        for g_ref, y_ref in zip(refs[:ng], refs[ng:]):
            y_ref[...] = (xn * g_ref[...]).astype(BF16)

    return pl.pallas_call(
        body, name=name, grid=(s // t,),
        in_specs=[_rows(t, d)] + [_fixed((1, d))] * ng,
        out_specs=[_rows(t, d)] * ng,
        out_shape=[jax.ShapeDtypeStruct((s, d), BF16)] * ng,
        compiler_params=_params("parallel"),
    )(x, *gains)


def _rms_bwd(name, x, pairs, dh_in, want_colsum=False):
    s, d = x.shape
    t = min(ROW_TILE, s)
    n_p = len(pairs)

    def body(x_ref, dh_ref, *refs):
        g_refs = refs[:n_p]
        dy_refs = refs[n_p:2 * n_p]
        dh_out, dhb_out = refs[2 * n_p], refs[2 * n_p + 1]
        dg_refs = refs[2 * n_p + 2:2 * n_p + 2 + n_p]
        cs_ref = refs[-1] if want_colsum else None
        i = pl.program_id(0)
        xv = x_ref[...]
        r = lax.rsqrt(jnp.mean(xv * xv, axis=-1, keepdims=True) + NORM_EPS)
        xn = xv * r
        dh = dh_ref[...]
        for g_ref, dy_ref, dg_ref in zip(g_refs, dy_refs, dg_refs):
            dy = dy_ref[...].astype(F32)
            u = dy * g_ref[...]
            dh = dh + r * (u - xn * jnp.mean(u * xn, axis=-1, keepdims=True))
            part = jnp.sum(dy * xn, axis=0, keepdims=True)

            @pl.when(i == 0)
            def _():
                dg_ref[...] = part

            @pl.when(i > 0)
            def _():
                dg_ref[...] += part

        dh_out[...] = dh
        dhb_out[...] = dh.astype(BF16)
        if want_colsum:
            col = jnp.sum(dh, axis=0, keepdims=True)

            @pl.when(i == 0)
            def _():
                cs_ref[...] = col

            @pl.when(i > 0)
            def _():
                cs_ref[...] += col

    n_vec = n_p + (1 if want_colsum else 0)
    return pl.pallas_call(
        body, name=name, grid=(s // t,),
        in_specs=[_rows(t, d), _rows(t, d)] + [_fixed((1, d))] * n_p + [_rows(t, d)] * n_p,
        out_specs=[_rows(t, d), _rows(t, d)] + [_fixed((1, d))] * n_vec,
        out_shape=[jax.ShapeDtypeStruct((s, d), F32), jax.ShapeDtypeStruct((s, d), BF16)]
        + [jax.ShapeDtypeStruct((1, d), F32)] * n_vec,
        compiler_params=_params("arbitrary"),
    )(x, dh_in, *[g for g, _ in pairs], *[dy for _, dy in pairs])


def _final_loss(x, g, target):
    s, d = x.shape
    t = min(ROW_TILE, s)

    def body(x_ref, g_ref, t_ref, dh_out, dhb_out, dg_ref, loss_ref):
        i = pl.program_id(0)
        xv = x_ref[...]
        gv = g_ref[...]
        r = lax.rsqrt(jnp.mean(xv * xv, axis=-1, keepdims=True) + NORM_EPS)
        xn = xv * r
        diff = xn * gv - t_ref[...]
        dy = diff / d
        u = dy * gv
        dh = r * (u - xn * jnp.mean(u * xn, axis=-1, keepdims=True))
        dh_out[...] = dh
        dhb_out[...] = dh.astype(BF16)
        dg = jnp.sum(dy * xn, axis=0, keepdims=True)
        lc = jnp.sum(0.5 * diff * dy, axis=0, keepdims=True)

        @pl.when(i == 0)
        def _():
            dg_ref[...] = dg
            loss_ref[...] = lc

        @pl.when(i > 0)
        def _():
            dg_ref[...] += dg
            loss_ref[...] += lc

    return pl.pallas_call(
        body, name="final_loss", grid=(s // t,),
        in_specs=[_rows(t, d), _fixed((1, d)), _rows(t, d)],
        out_specs=[_rows(t, d), _rows(t, d), _fixed((1, d)), _fixed((1, d))],
        out_shape=[jax.ShapeDtypeStruct((s, d), F32), jax.ShapeDtypeStruct((s, d), BF16),
                   jax.ShapeDtypeStruct((1, d), F32), jax.ShapeDtypeStruct((1, d), F32)],
        compiler_params=_params("arbitrary"),
    )(x, g, target)


def _ln_silu_fwd(c, g, b):
    s, d = c.shape
    t = min(ROW_TILE, s)

    def body(c_ref, g_ref, b_ref, s_ref):
        cv = c_ref[...]
        mu = jnp.mean(cv, axis=-1, keepdims=True)
        xc = cv - mu
        rs = lax.rsqrt(jnp.mean(xc * xc, axis=-1, keepdims=True) + LN_EPS)
        ln = xc * rs * g_ref[...] + b_ref[...]
        s_ref[...] = (ln * _sigmoid(ln)).astype(BF16)

    return pl.pallas_call(
        body, name="ln_silu_fwd", grid=(s // t,),
        in_specs=[_rows(t, d), _fixed((1, d)), _fixed((1, d))],
        out_specs=_rows(t, d), out_shape=jax.ShapeDtypeStruct((s, d), BF16),
        compiler_params=_params("parallel"),
    )(c, g, b)


def _ln_silu_bwd(c, g, b, ds):
    s, d = c.shape
    t = min(ROW_TILE, s)

    def body(c_ref, g_ref, b_ref, ds_ref, dc_ref, dg_ref, db_ref, dbdw_ref):
        i = pl.program_id(0)
        cv = c_ref[...]
        gv = g_ref[...]
        mu = jnp.mean(cv, axis=-1, keepdims=True)
        xc = cv - mu
        rs = lax.rsqrt(jnp.mean(xc * xc, axis=-1, keepdims=True) + LN_EPS)
        nrm = xc * rs
        ln = nrm * gv + b_ref[...]
        sig = _sigmoid(ln)
        dln = ds_ref[...].astype(F32) * sig * (1.0 + ln * (1.0 - sig))
        dn = dln * gv
        dc = rs * (dn - jnp.mean(dn, axis=-1, keepdims=True)
                   - nrm * jnp.mean(dn * nrm, axis=-1, keepdims=True))
        dc_ref[...] = dc
        pg = jnp.sum(dln * nrm, axis=0, keepdims=True)
        pb = jnp.sum(dln, axis=0, keepdims=True)
        pc = jnp.sum(dc, axis=0, keepdims=True)

        @pl.when(i == 0)
        def _():
            dg_ref[...] = pg
            db_ref[...] = pb
            dbdw_ref[...] = pc

        @pl.when(i > 0)
        def _():
            dg_ref[...] += pg
            db_ref[...] += pb
            dbdw_ref[...] += pc

    return pl.pallas_call(
        body, name="ln_silu_bwd", grid=(s // t,),
        in_specs=[_rows(t, d), _fixed((1, d)), _fixed((1, d)), _rows(t, d)],
        out_specs=[_rows(t, d)] + [_fixed((1, d))] * 3,
        out_shape=[jax.ShapeDtypeStruct((s, d), F32)] + [jax.ShapeDtypeStruct((1, d), F32)] * 3,
        compiler_params=_params("arbitrary"),
    )(c, g, b, ds)


def _residue_spec(dil, t, w):
    return pl.BlockSpec((dil, t // dil, w), lambda i: (0, i, 0))


def _attn_combine(o_list, lse_list):
    s, nh = lse_list[0].shape
    d = nh * HEAD_DIM
    t = min(ROW_TILE, s)
    nb = len(o_list)
    dils = [o.shape[0] for o in o_list]

    def body(*refs):
        l_refs = refs[nb:2 * nb]
        o_out, l_out = refs[2 * nb], refs[2 * nb + 1]
        planes = [_from_residues(src, stage, dil) for src, stage, dil in zip(refs[:nb], refs[2 * nb + 2:], dils)]
        ls = [r[...] for r in l_refs]
        mx = ls[0]
        for l in ls[1:]:
            mx = jnp.maximum(mx, l)
        es = [jnp.exp(l - mx) for l in ls]
        den = es[0]
        for e in es[1:]:
            den = den + e
        l_out[...] = mx + jnp.log(den)
        ws = [e / den for e in es]
        for h in range(nh):
            acc = jnp.zeros((t, HEAD_DIM), F32)
            for plane, w in zip(planes, ws):
                acc = acc + w[:, h:h + 1] * plane(h)
            o_out[:, h * HEAD_DIM:(h + 1) * HEAD_DIM] = acc.astype(BF16)

    return pl.pallas_call(
        body, name="attn_combine", grid=(s // t,),
        in_specs=[_residue_spec(dil, t, d) for dil in dils] + [_rows(t, nh)] * nb,
        out_specs=[_rows(t, d), _rows(t, nh)],
        out_shape=[jax.ShapeDtypeStruct((s, d), BF16), jax.ShapeDtypeStruct((s, nh), F32)],
        scratch_shapes=[pltpu.VMEM(_stage_shape(t, d), F32)] * nb,
        compiler_params=_params("parallel"),
    )(*o_list, *lse_list)


def _attn_delta(do, o):
    s, d = o.shape
    nh = d // HEAD_DIM
    t = min(ROW_TILE, s)

    def body(do_ref, o_ref, dl_ref):
        lane = lax.broadcasted_iota(I32, (t, nh), 1)
        out = jnp.zeros((t, nh), F32)
        for h in range(nh):
            cols = slice(h * HEAD_DIM, (h + 1) * HEAD_DIM)
            v = jnp.sum(do_ref[:, cols].astype(F32) * o_ref[:, cols].astype(F32), axis=-1, keepdims=True)
            out = jnp.where(lane == h, v, out)
        dl_ref[...] = out

    return pl.pallas_call(
        body, name="attn_delta", grid=(s // t,),
        in_specs=[_rows(t, d), _rows(t, d)],
        out_specs=_rows(t, nh), out_shape=jax.ShapeDtypeStruct((s, nh), F32),
        compiler_params=_params("parallel"),
    )(do, o)


def _residue_sum(name, groups, tabs):
    first = groups[0][0][0]
    s, w = first.shape[0] * first.shape[1], first.shape[2]
    t = min(ROW_TILE, s)
    flat = [p for parts, _ in groups for p in parts]

    def body(*refs):
        c_ref, a_ref, b_ref = refs[len(flat):len(flat) + 3]
        out = refs[len(flat) + 3]
        stages = refs[len(flat) + 4:]
        k = 0
        for gi, (parts, rotate) in enumerate(groups):
            planes = [_from_residues(refs[k + i], stages[k + i], p.shape[0]) for i, p in enumerate(parts)]
            k += len(parts)
            for c in range(w // LANES):
                tot = planes[0](c)
                for plane in planes[1:]:
                    tot = tot + plane(c)
                if rotate:
                    tot = _rope_apply(tot, c_ref[...], a_ref[...], b_ref[...], -1.0)
                out[:, gi * w + c * LANES:gi * w + (c + 1) * LANES] = tot.astype(BF16)

    return pl.pallas_call(
        body, name=name, grid=(s // t,),
        in_specs=[_residue_spec(p.shape[0], t, w) for p in flat] + [_rows(t, HEAD_DIM)] * 3,
        out_specs=_rows(t, len(groups) * w), out_shape=jax.ShapeDtypeStruct((s, len(groups) * w), BF16),
        scratch_shapes=[pltpu.VMEM(_stage_shape(t, w), F32) for _ in flat],
        compiler_params=_params("parallel"),
    )(*flat, *tabs)


def _dwconv_fwd(u, w_dw, b_dw):
    s, d2 = u.shape
    d = d2 // 2
    cb = min(CONV_CB, d)
    nblk = d // cb
    tt = min(CONV_T, s)

    def body(ua_ref, ug_ref, w_ref, b_ref, c_ref, xp_ref):
        gl = ua_ref[...].astype(F32) * _sigmoid(ug_ref[...].astype(F32))
        xp_ref[0:CONV_PAD, :] = jnp.zeros((CONV_PAD, cb), F32)
        xp_ref[CONV_PAD:, :] = gl
        wv = w_ref[...]
        bv = b_ref[...]
        for t0 in range(0, s, tt):
            acc = jnp.zeros((tt, cb), F32) + bv
            for kk in range(CONV_WIDTH):
                off = t0 + CONV_PAD - (CONV_WIDTH - 1) + kk
                acc = acc + wv[kk:kk + 1, :] * xp_ref[off:off + tt, :]
            c_ref[t0:t0 + tt, :] = acc

    return pl.pallas_call(
        body, name="dwconv_fwd", grid=(nblk,),
        in_specs=[pl.BlockSpec((s, cb), lambda j: (0, j)), pl.BlockSpec((s, cb), lambda j: (0, j + nblk)),
                  pl.BlockSpec((CONV_PAD, cb), lambda j: (0, j)), pl.BlockSpec((1, cb), lambda j: (0, j))],
        out_specs=pl.BlockSpec((s, cb), lambda j: (0, j)),
        out_shape=jax.ShapeDtypeStruct((s, d), F32),
        scratch_shapes=[pltpu.VMEM((s + CONV_PAD, cb), F32)],
        compiler_params=_params("parallel"),
    )(u, u, w_dw, b_dw)


def _dwconv_bwd(u, w_dw, dc):
    s, d2 = u.shape
    d = d2 // 2
    cb = min(CONV_CB, d)
    nblk = d // cb
    tt = min(CONV_T, s)

    def body(ua_ref, ug_ref, w_ref, dc_ref, da_ref, dgt_ref, dw_ref, dba_ref, dbg_ref, glp_ref, dcp_ref, acc_ref):
        a = ua_ref[...].astype(F32)
        sig = _sigmoid(ug_ref[...].astype(F32))
        glp_ref[0:CONV_PAD, :] = jnp.zeros((CONV_PAD, cb), F32)
        glp_ref[CONV_PAD:, :] = a * sig
        dcp_ref[0:s, :] = dc_ref[...]
        dcp_ref[s:, :] = jnp.zeros((CONV_PAD, cb), F32)
        acc_ref[...] = jnp.zeros_like(acc_ref)
        wv = w_ref[...]
        dba = jnp.zeros((1, cb), F32)
        dbg = jnp.zeros((1, cb), F32)
        for t0 in range(0, s, tt):
            dgl = jnp.zeros((tt, cb), F32)
            dct = dc_ref[t0:t0 + tt, :]
            for kk in range(CONV_WIDTH):
                off = t0 + (CONV_WIDTH - 1) - kk
                dgl = dgl + wv[kk:kk + 1, :] * dcp_ref[off:off + tt, :]
                goff = t0 + CONV_PAD - (CONV_WIDTH - 1) + kk
                prod = dct * glp_ref[goff:goff + tt, :]
                acc_ref[8 * kk:8 * kk + 8, :] += jnp.sum(prod.reshape(tt // 8, 8, cb), axis=0)
            at = ua_ref[t0:t0 + tt, :].astype(F32)
            st = _sigmoid(ug_ref[t0:t0 + tt, :].astype(F32))
            da = dgl * st
            dg = dgl * at * st * (1.0 - st)
            da_ref[t0:t0 + tt, :] = da.astype(BF16)
            dgt_ref[t0:t0 + tt, :] = dg.astype(BF16)
            dba = dba + jnp.sum(da, axis=0, keepdims=True)
            dbg = dbg + jnp.sum(dg, axis=0, keepdims=True)
        dba_ref[...] = dba
        dbg_ref[...] = dbg
        for kk in range(CONV_WIDTH):
            dw_ref[kk:kk + 1, :] = jnp.sum(acc_ref[8 * kk:8 * kk + 8, :], axis=0, keepdims=True)
        dw_ref[CONV_WIDTH:, :] = jnp.zeros((CONV_PAD - CONV_WIDTH, cb), F32)

    blk = pl.BlockSpec((s, cb), lambda j: (0, j))
    vec = pl.BlockSpec((1, cb), lambda j: (0, j))
    return pl.pallas_call(
        body, name="dwconv_bwd", grid=(nblk,),
        in_specs=[blk, pl.BlockSpec((s, cb), lambda j: (0, j + nblk)),
                  pl.BlockSpec((CONV_PAD, cb), lambda j: (0, j)), blk],
        out_specs=[blk, blk, pl.BlockSpec((CONV_PAD, cb), lambda j: (0, j)), vec, vec],
        out_shape=[jax.ShapeDtypeStruct((s, d), BF16), jax.ShapeDtypeStruct((s, d), BF16),
                   jax.ShapeDtypeStruct((CONV_PAD, d), F32),
                   jax.ShapeDtypeStruct((1, d), F32), jax.ShapeDtypeStruct((1, d), F32)],
        scratch_shapes=[pltpu.VMEM((s + CONV_PAD, cb), F32), pltpu.VMEM((s + CONV_PAD, cb), F32),
                        pltpu.VMEM((8 * CONV_PAD, cb), F32)],
        compiler_params=_params("parallel"),
    )(u, u, w_dw, dc)


def _stack_heads(x, group):
    return jnp.concatenate([x[:, g * HEAD_DIM:(g + 1) * HEAD_DIM] for g in range(group)], axis=0)


def _unstack_heads(x, group):
    return jnp.concatenate([x[g * ATT_BLOCK:(g + 1) * ATT_BLOCK, :] for g in range(group)], axis=1)


def _stack_cols(x, group):
    return jnp.concatenate([x[:, g:g + 1] for g in range(group)], axis=0)


def _band_mask(nb, group):
    rows = group * ATT_BLOCK
    row = lax.broadcasted_iota(I32, (rows, 2 * ATT_BLOCK), 0) % ATT_BLOCK
    col = lax.broadcasted_iota(I32, (rows, 2 * ATT_BLOCK), 1)
    return (col >= row) & (col <= row + ATT_BLOCK) & ((col >= ATT_BLOCK) | (nb > 0))


def _window(ref, nb):
    prev = pl.multiple_of(jnp.maximum(nb - 1, 0) * ATT_BLOCK, ATT_BLOCK)
    cur = pl.multiple_of(nb * ATT_BLOCK, ATT_BLOCK)
    return jnp.concatenate([ref[pl.ds(prev, ATT_BLOCK), :], ref[pl.ds(cur, ATT_BLOCK), :]], axis=0)


def _attn_fwd(name, q, kv):
    dil, sd, d = q.shape
    group = d // HEAD_DIM // N_KV_HEADS
    gw = group * HEAD_DIM
    nblk = sd // ATT_BLOCK
    scale = 1.0 / math.sqrt(HEAD_DIM)
    nt = (((1,), (1,)), ((), ()))

    def body(q_ref, k_ref, v_ref, o_ref, lse_ref):
        lane = lax.broadcasted_iota(I32, (ATT_BLOCK, group), 1)

        def step(nb, carry):
            rows = pl.ds(pl.multiple_of(nb * ATT_BLOCK, ATT_BLOCK), ATT_BLOCK)
            qs = _stack_heads(q_ref[rows, :], group)
            kw = _window(k_ref, nb)
            vw = _window(v_ref, nb)
            sc = lax.dot_general(qs, kw, nt, preferred_element_type=F32) * scale
            sc = jnp.where(_band_mask(nb, group), sc, -jnp.inf)
            mx = jnp.max(sc, axis=-1, keepdims=True)
            p = jnp.exp(sc - mx)
            l = jnp.sum(p, axis=-1, keepdims=True)
            o = jnp.dot(p.astype(BF16), vw, preferred_element_type=F32) / l
            o_ref[rows, :] = _unstack_heads(o, group).astype(BF16)
            lse = mx + jnp.log(l)
            out = jnp.zeros((ATT_BLOCK, group), F32)
            for g in range(group):
                out = jnp.where(lane == g, lse[g * ATT_BLOCK:(g + 1) * ATT_BLOCK, :], out)
            lse_ref[rows, :] = out
            return carry

        lax.fori_loop(0, nblk, step, 0, unroll=min(2, nblk))

    kvh = N_KV_HEADS
    qspec = pl.BlockSpec((None, sd, gw), lambda r, h: (r, 0, h))
    return pl.pallas_call(
        body, name=name, grid=(dil, kvh),
        in_specs=[qspec,
                  pl.BlockSpec((None, sd, HEAD_DIM), lambda r, h: (r, 0, h)),
                  pl.BlockSpec((None, sd, HEAD_DIM), lambda r, h: (r, 0, kvh + h))],
        out_specs=[qspec, pl.BlockSpec((None, sd, group), lambda r, h: (r * kvh + h, 0, 0))],
        out_shape=[jax.ShapeDtypeStruct((dil, sd, d), BF16),
                   jax.ShapeDtypeStruct((dil * kvh, sd, group), F32)],
        compiler_params=_params("parallel", "parallel"),
    )(q, kv, kv)


def _attn_bwd(name, q, kv, do, lse, delta):
    dil, sd, d = q.shape
    group = d // HEAD_DIM // N_KV_HEADS
    gw = group * HEAD_DIM
    nblk = sd // ATT_BLOCK
    scale = 1.0 / math.sqrt(HEAD_DIM)
    nt = (((1,), (1,)), ((), ()))
    tn = (((0,), (0,)), ((), ()))

    def body(q_ref, k_ref, v_ref, do_ref, lse_ref, dl_ref, dq_ref, dk_ref, dv_ref, dk_acc, dv_acc):
        dk_acc[...] = jnp.zeros_like(dk_acc)
        dv_acc[...] = jnp.zeros_like(dv_acc)

        def step(nb, carry):
            rows = pl.ds(pl.multiple_of(nb * ATT_BLOCK, ATT_BLOCK), ATT_BLOCK)
            qs = _stack_heads(q_ref[rows, :], group)
            dos = _stack_heads(do_ref[rows, :], group)
            ls = _stack_cols(lse_ref[rows, :], group)
            dl = _stack_cols(dl_ref[rows, :], group)
            kw = _window(k_ref, nb)
            vw = _window(v_ref, nb)
            sc = lax.dot_general(qs, kw, nt, preferred_element_type=F32) * scale
            sc = jnp.where(_band_mask(nb, group), sc, -jnp.inf)
            p = jnp.exp(sc - ls)
            dp = lax.dot_general(dos, vw, nt, preferred_element_type=F32)
            ds = (p * (dp - dl) * scale).astype(BF16)
            dq = jnp.dot(ds, kw, preferred_element_type=F32)
            dq_ref[rows, :] = _unstack_heads(dq, group).astype(BF16)
            win = pl.ds(pl.multiple_of(nb * ATT_BLOCK, ATT_BLOCK), 2 * ATT_BLOCK)
            dk_acc[win, :] += lax.dot_general(ds, qs, tn, preferred_element_type=F32)
            dv_acc[win, :] += lax.dot_general(p.astype(BF16), dos, tn, preferred_element_type=F32)
            return carry

        lax.fori_loop(0, nblk, step, 0, unroll=min(2, nblk))
        dk_ref[...] = dk_acc[ATT_BLOCK:, :]
        dv_ref[...] = dv_acc[ATT_BLOCK:, :]

    kvh = N_KV_HEADS
    qspec = pl.BlockSpec((None, sd, gw), lambda r, h: (r, 0, h))
    sspec = pl.BlockSpec((None, sd, group), lambda r, h: (r * kvh + h, 0, 0))
    kspec = pl.BlockSpec((None, sd, HEAD_DIM), lambda r, h: (r, 0, h))
    return pl.pallas_call(
        body, name=name, grid=(dil, kvh),
        in_specs=[qspec, kspec, pl.BlockSpec((None, sd, HEAD_DIM), lambda r, h: (r, 0, kvh + h)),
                  qspec, sspec, sspec],
        out_specs=[qspec, kspec, kspec],
        out_shape=[jax.ShapeDtypeStruct((dil, sd, d), BF16),
                   jax.ShapeDtypeStruct((dil, sd, kvh * HEAD_DIM), F32),
                   jax.ShapeDtypeStruct((dil, sd, kvh * HEAD_DIM), F32)],
        scratch_shapes=[pltpu.VMEM((sd + ATT_BLOCK, HEAD_DIM), F32), pltpu.VMEM((sd + ATT_BLOCK, HEAD_DIM), F32)],
        compiler_params=_params("parallel", "parallel"),
    )(q, kv, kv, do, lse, delta)


def _heads_to_branch(x, dil, group):
    s = x.shape[0]
    x = x.reshape(s // dil, dil, N_KV_HEADS, group)
    return jnp.transpose(x, (1, 2, 0, 3)).reshape(dil * N_KV_HEADS, s // dil, group)


def _heads_from_branch(x, dil, group):
    sd = x.shape[1]
    x = x.reshape(dil, N_KV_HEADS, sd, group)
    return jnp.transpose(x, (2, 0, 1, 3)).reshape(sd * dil, N_KV_HEADS * group)


def _cast_bf16(name, w, layer, place):
    _, r, c = w.shape
    tr = min(512, r)

    def body(pl_ref, w_ref, o_ref):
        o_ref[...] = w_ref[...].astype(BF16)

    return pl.pallas_call(
        body, name=name,
        grid_spec=pltpu.PrefetchScalarGridSpec(
            num_scalar_prefetch=1, grid=(r // tr,),
            in_specs=[pl.BlockSpec((None, tr, c), lambda i, p: (layer, i, 0))],
            out_specs=pl.BlockSpec((None, tr, c), lambda i, p: (p[1], i, 0))),
        out_shape=jax.ShapeDtypeStruct((N_SHARD, r, c), BF16),
        compiler_params=_params("parallel"),
    )(place, w)


def _chip_sum(name, g, rh, place):
    _, r, c = g.shape
    rh2 = r // 2
    tr = min(512, rh2)
    nb = rh2 // tr

    def body(pl_ref, g_ref, rh_ref, o_ref):
        o_ref[...] = (g_ref[...].astype(F32) + rh_ref[...].astype(F32)).astype(BF16)

    return pl.pallas_call(
        body, name=name,
        grid_spec=pltpu.PrefetchScalarGridSpec(
            num_scalar_prefetch=1, grid=(N_SHARD, nb),
            in_specs=[pl.BlockSpec((None, tr, c), lambda s, i, p: (s, p[0] * nb + i, 0)),
                      pl.BlockSpec((None, tr, c), lambda s, i, p: (s, i, 0))],
            out_specs=pl.BlockSpec((None, tr, c), lambda s, i, p: (s, i, 0))),
        out_shape=jax.ShapeDtypeStruct((N_SHARD, rh2, c), BF16),
        compiler_params=_params("parallel", "parallel"),
    )(place, g, rh)


def _owner_sum(name, cs, rp, place):
    _, rh2, c = cs.shape
    tr = min(512, rh2)
    nb = rh2 // tr

    def body(pl_ref, cs_ref, r0_ref, r1_ref, r2_ref, o_ref):
        o_ref[...] = ((cs_ref[...].astype(F32) + r0_ref[...].astype(F32))
                      + (r1_ref[...].astype(F32) + r2_ref[...].astype(F32)))

    def rspec(j):
        return pl.BlockSpec((None, tr, c), lambda i, p: (j, i, 0))

    return pl.pallas_call(
        body, name=name,
        grid_spec=pltpu.PrefetchScalarGridSpec(
            num_scalar_prefetch=1, grid=(nb,),
            in_specs=[pl.BlockSpec((None, tr, c), lambda i, p: (p[1], i, 0)), rspec(0), rspec(1), rspec(2)],
            out_specs=pl.BlockSpec((tr, c), lambda i, p: (p[0] * nb + i, 0))),
        out_shape=jax.ShapeDtypeStruct((2 * rh2, c), F32),
        compiler_params=_params("parallel"),
    )(place, cs, rp, rp, rp)


def _adam_math(w, g, m, v):
    m = ADAM_B1 * m + (1.0 - ADAM_B1) * g
    v = ADAM_B2 * v + (1.0 - ADAM_B2) * (g * g)
    m_hat = m / (1.0 - ADAM_B1 ** ADAM_STEP)
    v_hat = v / (1.0 - ADAM_B2 ** ADAM_STEP)
    delta = -ADAM_LR * (m_hat / (jnp.sqrt(v_hat) + ADAM_EPS) + ADAM_WD * w)
    return delta, m, v


def _adamw(name, w, m, v, g, layer, partial=None):
    nl, r, c = w.shape
    tr = min(256, r)

    def body(w_ref, m_ref, v_ref, g_ref, *refs):
        go_ref, d_ref, mo_ref, vo_ref = refs[-4:]
        gv = g_ref[...]
        delta, m_new, v_new = _adam_math(w_ref[...], gv, m_ref[...], v_ref[...])
        go_ref[...] = gv
        d_ref[...] = delta
        mo_ref[...] = m_new
        vo_ref[...] = v_new

    wspec = pl.BlockSpec((None, tr, c), lambda i: (layer, i, 0))
    prev = [] if partial is None else list(partial)
    return pl.pallas_call(
        body, name=name, grid=(r // tr,),
        in_specs=[wspec] * 3 + [pl.BlockSpec((tr, c), lambda i: (i, 0))] + [ANY] * len(prev),
        out_specs=[wspec] * 4,
        out_shape=[jax.ShapeDtypeStruct((nl, r, c), F32)] * 4,
        input_output_aliases={4 + i: i for i in range(len(prev))},
        compiler_params=_params("parallel"),
    )(w, m, v, g, *prev)


def _adam_small(ws, ms, vs, gs):
    n = len(ws)

    def body(*refs):
        w_refs, m_refs, v_refs, g_refs = refs[:n], refs[n:2 * n], refs[2 * n:3 * n], refs[3 * n:4 * n]
        d_refs, mo_refs, vo_refs = refs[4 * n:5 * n], refs[5 * n:6 * n], refs[6 * n:7 * n]
        for i in range(n):
            delta, m_new, v_new = _adam_math(w_refs[i][...], g_refs[i][...], m_refs[i][...], v_refs[i][...])
            d_refs[i][...] = delta
            mo_refs[i][...] = m_new
            vo_refs[i][...] = v_new

    shapes = [jax.ShapeDtypeStruct(w.shape, F32) for w in ws]
    res = pl.pallas_call(body, name="adam_small", out_shape=shapes * 3)(*ws, *ms, *vs, *gs)
    return res[:n], res[n:2 * n], res[2 * n:]


def _pack_small(b_in, w_dw, b_dw, ln_g, ln_b, b_out, place):
    cin = b_in.shape[1]
    cd = b_dw.shape[1]
    rows = 8 + CONV_PAD

    def body(pl_ref, bi, wd, bd, lg, lb, bo, out):
        out[...] = jnp.zeros_like(out)
        out[0:1, :] = bi[...]
        out[1:2, 0:cd] = bd[...]
        out[1:2, cd:2 * cd] = lg[...]
        out[2:3, 0:cd] = lb[...]
        out[2:3, cd:2 * cd] = bo[...]
        out[8:8 + CONV_WIDTH, 0:cd] = wd[...]

    def whole(arr):
        return pl.BlockSpec(arr.shape, lambda i, p: (0,) * arr.ndim)

    ins = [b_in, w_dw, b_dw, ln_g, ln_b, b_out]
    return pl.pallas_call(
        body, name="pack_small",
        grid_spec=pltpu.PrefetchScalarGridSpec(
            num_scalar_prefetch=1, grid=(1,), in_specs=[whole(a) for a in ins],
            out_specs=pl.BlockSpec((None, rows, cin), lambda i, p: (p[1], 0, 0))),
        out_shape=jax.ShapeDtypeStruct((N_SHARD, rows, cin), F32),
        compiler_params=_params("arbitrary"),
    )(place, *ins)


def _place():
    x, y, c = lax.axis_index("x"), lax.axis_index("y"), lax.axis_index("c")
    return x, y, c


def _other_chips(x, y):
    return [(1 - x, y), (x, 1 - y), (1 - x, 1 - y)]


def _split_start(name, bufs, n_sem, copies, after=None):
    n = len(bufs)
    deps = [] if after is None else [after]

    def body(*refs):
        out0 = n + len(deps)
        for cp in copies(refs[:n], refs[out0], refs[out0 + 1], False):
            cp.start()
        refs[-1][...] = jnp.zeros_like(refs[-1])

    res = pl.pallas_call(
        body, name=name,
        out_shape=(pltpu.SemaphoreType.DMA((n_sem,)), pltpu.SemaphoreType.DMA((n_sem,)),
                   *[pltpu.HBM(b.shape, b.dtype) for b in bufs], jax.ShapeDtypeStruct((8, LANES), F32)),
        in_specs=[HBM] * n + [ANY] * len(deps),
        out_specs=(SEM, SEM, *[HBM] * n, pl.BlockSpec(memory_space=pltpu.VMEM)),
        input_output_aliases={i: 2 + i for i in range(n)},
        compiler_params=pltpu.CompilerParams(has_side_effects=SPLIT_EFFECT),
    )(*[pltpu.with_memory_space_constraint(b, pltpu.HBM) for b in bufs], *deps)
    return res[0], res[1], list(res[2:2 + n]), res[-1]


def _split_wait(name, handle, copies, after):
    ssem, rsem, bufs, _ = handle
    n = len(bufs)

    def body(*refs):
        for cp in copies(refs[:n], refs[n], refs[n + 1], True):
            cp.wait_send()
            cp.wait_recv()

    res = pl.pallas_call(
        body, name=name,
        out_shape=[pltpu.HBM(b.shape, b.dtype) for b in bufs],
        in_specs=[HBM] * n + [SEM, SEM, ANY], out_specs=[HBM] * n,
        input_output_aliases={i: i for i in range(n)},
        compiler_params=pltpu.CompilerParams(has_side_effects=SPLIT_EFFECT),
    )(*bufs, ssem, rsem, after)
    return list(res)


def _remote(src, dst, ssem, rsem, k, to):
    return pltpu.make_async_remote_copy(src_ref=src, dst_ref=dst, send_sem=ssem.at[k], recv_sem=rsem.at[k],
                                        device_id=to, device_id_type=MESH)


def _gather_copies(refs, ssem, rsem, landing, n_whole=0):
    x, y, c = _place()
    me = 2 * x + y
    cps = []
    for a, ref in enumerate(refs):
        whole = a >= len(refs) - n_whole
        rh = ref.shape[1] // 2
        for j, (px, py) in enumerate(_other_chips(x, y)):
            shard = 2 * px + py if landing else me
            src = ref.at[me] if whole else ref.at[me, pl.ds(c * rh, rh)]
            dst = ref.at[shard] if whole else ref.at[shard, pl.ds(c * rh, rh)]
            cps.append(_remote(src, dst, ssem, rsem, 3 * a + j, (px, py, c)))
    return cps


def _forward_copies(refs, ssem, rsem, landing):
    x, y, c = _place()
    who = 1 - c if landing else c
    cps = []
    for a, ref in enumerate(refs):
        rh = ref.shape[1] // 2
        for j, (px, py) in enumerate(_other_chips(x, y)):
            piece = ref.at[2 * px + py, pl.ds(who * rh, rh)]
            cps.append(_remote(piece, piece, ssem, rsem, 3 * a + j, (x, y, 1 - c)))
    return cps


def _sibling_copies(refs, ssem, rsem, landing):
    x, y, c = _place()
    n = len(refs) // 2
    cps = []
    for a in range(n):
        rh = refs[a].shape[1] // 2
        cps.append(_remote(refs[a].at[:, pl.ds((1 - c) * rh, rh), :], refs[n + a], ssem, rsem, a, (x, y, 1 - c)))
    return cps


def _owner_copies(refs, ssem, rsem, landing):
    x, y, c = _place()
    n = len(refs) // 2
    cps = []
    for a in range(n):
        for j, (px, py) in enumerate(_other_chips(x, y)):
            cps.append(_remote(refs[a].at[2 * px + py], refs[n + a].at[j], ssem, rsem, 3 * a + j, (px, py, c)))
    return cps


def _swap_copies(refs, ssem, rsem, landing):
    x, y, c = _place()
    who = 1 - c if landing else c
    cps = []
    for a, ref in enumerate(refs):
        rh = ref.shape[0] // 2
        rows = ref.at[pl.ds(who * rh, rh)]
        cps.append(_remote(rows, rows, ssem, rsem, a, (x, y, 1 - c)))
    return cps


def _small_copies(refs, ssem, rsem, landing):
    pack, slots = refs
    x, y, c = _place()
    cps = []
    for rel in range(1, N_DEV):
        px = 1 - x if (rel >> 2) & 1 else x
        py = 1 - y if (rel >> 1) & 1 else y
        pc = 1 - c if rel & 1 else c
        slot = 4 * px + 2 * py + pc if landing else 4 * x + 2 * y + c
        cps.append(_remote(pack, slots.at[slot], ssem, rsem, rel - 1, (px, py, pc)))
    return cps


def _small_pack(rows, w_dw_grad, d):
    n = len(rows)

    def body(*refs):
        pack = refs[-1]
        pack[...] = jnp.zeros_like(pack)
        for (r, _), ref in zip(rows, refs[:n]):
            pack[r:r + 1, :] = ref[...]
        pack[16:16 + CONV_PAD, :] = refs[n][...]

    return pl.pallas_call(body, name="small_pack", out_shape=jax.ShapeDtypeStruct((SMALL_ROWS, d), F32))(
        *[v for _, v in rows], w_dw_grad)


def _small_sum(pack, slots, place):
    rows, d = pack.shape
    loss_row = 12

    def body(pl_ref, pack_ref, slots_ref, out_ref):
        me = pl_ref[2]
        tot = jnp.where(me == 0, pack_ref[...], slots_ref[0])
        for i in range(1, N_DEV):
            tot = tot + jnp.where(me == i, pack_ref[...], slots_ref[i])
        out_ref[...] = tot
        out_ref[loss_row:loss_row + 1, :] = jnp.zeros((1, d), F32) + jnp.sum(tot[loss_row:loss_row + 1, :])

    return pl.pallas_call(
        body, name="small_sum",
        grid_spec=pltpu.PrefetchScalarGridSpec(
            num_scalar_prefetch=1, grid=(1,),
            in_specs=[pl.BlockSpec((rows, d), lambda i, p: (0, 0)), pl.BlockSpec((N_DEV, rows, d), lambda i, p: (0, 0, 0))],
            out_specs=pl.BlockSpec((rows, d), lambda i, p: (0, 0))),
        out_shape=jax.ShapeDtypeStruct((rows, d), F32),
        compiler_params=_params("arbitrary"),
    )(place, pack, slots)


def kernel(x, norm_mix, norm_mlp, conv_w_in, conv_b_in, conv_w_dw, conv_b_dw, conv_ln_g, conv_ln_b, conv_w_out, conv_b_out, kv_norm, w_kv, attn_w_q, attn_w_o, mlp_w_in, mlp_w_out, final_norm, loss_target, m_norm_mix, m_norm_mlp, m_conv_w_in, m_conv_b_in, m_conv_w_dw, m_conv_b_dw, m_conv_ln_g, m_conv_ln_b, m_conv_w_out, m_conv_b_out, m_kv_norm, m_w_kv, m_attn_w_q, m_attn_w_o, m_mlp_w_in, m_mlp_w_out, m_final_norm, v_norm_mix, v_norm_mlp, v_conv_w_in, v_conv_b_in, v_conv_w_dw, v_conv_b_dw, v_conv_ln_g, v_conv_ln_b, v_conv_w_out, v_conv_b_out, v_kv_norm, v_w_kv, v_attn_w_q, v_attn_w_o, v_mlp_w_in, v_mlp_w_out, v_final_norm):
    _, s, d = x.shape
    dff = mlp_w_in.shape[2] * N_SHARD
    kvw = w_kv.shape[1]
    nh = d // HEAD_DIM
    group = nh // N_KV_HEADS
    ds4 = d // N_SHARD
    xi, yi, ci = _place()
    me = 2 * xi + yi
    place = jnp.stack([ci, me, 2 * me + ci]).astype(I32)

    h0 = x.reshape(s, d)
    target = loss_target.reshape(s, d)
    tabs = _rope_tables(s)

    def gather_begin(tag, bufs, n_whole=0):
        return _split_start(f"gather_start_{tag}", bufs, 3 * len(bufs),
                            functools.partial(_gather_copies, n_whole=n_whole)), n_whole

    def gather_land(tag, begun, later):
        handle, n_whole = begun
        bufs = _split_wait(f"gather_wait_{tag}", handle, functools.partial(_gather_copies, n_whole=n_whole), later)
        n_half = len(bufs) - n_whole
        fwd = _split_start(f"forward_start_{tag}", bufs[:n_half], 3 * n_half, _forward_copies)
        return fwd, bufs[n_half:]

    def gather_end(tag, landed, later):
        fwd, whole = landed
        return _split_wait(f"forward_wait_{tag}", fwd, _forward_copies, later) + whole

    ag_cin = gather_begin("conv_in", [
        _cast_bf16("cast_w_in", conv_w_in, 0, place),
        _pack_small(conv_b_in, conv_w_dw.reshape(CONV_WIDTH, ds4), conv_b_dw, conv_ln_g, conv_ln_b, conv_b_out, place),
    ], n_whole=1)
    ag_cout = gather_begin("conv_out", [_cast_bf16("cast_w_out", conv_w_out, 0, place)])
    ag_mi0 = gather_begin("mlp_in0", [_cast_bf16("cast_mlp_in0", mlp_w_in, 0, place)])
    ag_mo0 = gather_begin("mlp_out0", [_cast_bf16("cast_mlp_out0", mlp_w_out, 0, place)])
    ag_attn = gather_begin("attn", [
        _cast_bf16("cast_w_kv", w_kv.reshape(1, ds4, kvw), 0, place), _cast_bf16("cast_w_q", attn_w_q, 0, place),
        _cast_bf16("cast_w_o", attn_w_o, 0, place)])
    ag_mlp1 = gather_begin("mlp1", [
        _cast_bf16("cast_mlp_in1", mlp_w_in, 1, place), _cast_bf16("cast_mlp_out1", mlp_w_out, 1, place)])

    wmi_g = [None, None]
    wmo_f = [None, None]

    nm = [norm_mix[0:1], norm_mix[1:2]]
    nmlp = [norm_mlp[0:1], norm_mlp[1:2]]
    kvn = kv_norm.reshape(1, d)
    fin = final_norm.reshape(1, d)
    started = sum(h[0][3][0:1, 0:1] for h in (ag_cin, ag_cout, ag_mi0, ag_mo0, ag_attn, ag_mlp1))
    (y0,) = _rms_fwd("rms_mix0", h0, [nm[0] + started])

    w_in_g, small_g = gather_end("conv_in", gather_land("conv_in", ag_cin, y0), y0)
    b_in_f = small_g[:, 0, :].reshape(1, 2 * d)
    b_dw_f = small_g[:, 1, 0:ds4].reshape(1, d)
    ln_g_f = small_g[:, 1, ds4:2 * ds4].reshape(1, d)
    ln_b_f = small_g[:, 2, 0:ds4].reshape(1, d)
    b_out_f = small_g[:, 2, ds4:2 * ds4].reshape(1, d)
    w_dw_f = jnp.transpose(small_g[:, 8:8 + CONV_PAD, 0:ds4], (1, 0, 2)).reshape(CONV_PAD, d)

    def ep_bias(acc, ex, outs, j):
        outs[0][...] = (acc + ex[0][...]).astype(outs[0].dtype)

    def ep_residual(acc, ex, outs, j):
        outs[0][...] = ex[0][...] + acc

    def ep_residual_bias(acc, ex, outs, j):
        outs[0][...] = ex[0][...] + (acc + ex[1][...])

    def ep_relu2(acc, ex, outs, j):
        r = jnp.maximum(acc, 0.0)
        outs[0][...] = r.astype(BF16)
        outs[1][...] = (r * r).astype(BF16)

    by_residue = [(BF16, ("residues", dil)) for dil in DILATIONS]

    def put_by_residue(val, outs, stage):
        _to_residues(val, stage, outs, DILATIONS)

    def ep_rope(acc, ex, outs, j, stage):
        put_by_residue(_rope_apply(acc, ex[0][...], ex[1][...], ex[2][...], 1.0), outs, stage)

    def ep_rope_k(acc, ex, outs, j, stage):
        roped = _rope_apply(acc, ex[0][...], ex[1][...], ex[2][...], 1.0)
        put_by_residue(jnp.where(j == 0, roped, acc), outs, stage)

    def ep_by_residue(acc, ex, outs, j, stage):
        put_by_residue(acc, outs, stage)

    tab_extras = [(t, "rows") for t in tabs]

    def mlp_fwd(idx, h, y, out_weight):
        r, r2 = _matmul(f"mlp_in{idx}", "nn", y, wmi_g[idx], b_kind="col", m=s, n=dff, k=d,
                        outs=[(BF16, "plain"), (BF16, "plain")], epilogue=ep_relu2)
        wmo_f[idx] = out_weight(r2).reshape(dff, d)
        (h_new,) = _matmul(f"mlp_out{idx}", "nn", r2, wmo_f[idx], m=s, n=d, k=dff,
                           outs=[(F32, "plain")], extras=[(h, "ij")], epilogue=ep_residual)
        return h_new, r, r2

    (u,) = _matmul("conv_in", "nn", y0, w_in_g, b_kind="col", m=s, n=2 * d, k=d,
                   outs=[(BF16, "plain")], extras=[(b_in_f, "vec")], epilogue=ep_bias)
    land_cout = gather_land("conv_out", ag_cout, u)
    cpre = _dwconv_fwd(u, w_dw_f, b_dw_f + land_cout[0][3][0:1, 0:1])
    sact = _ln_silu_fwd(cpre, ln_g_f, ln_b_f)
    (w_out_g,) = gather_end("conv_out", land_cout, sact)
    w_out_f = w_out_g.reshape(d, d)
    (h1,) = _matmul("conv_out", "nn", sact, w_out_f, m=s, n=d, k=d,
                    outs=[(F32, "plain")], extras=[(h0, "ij"), (b_out_f, "vec")], epilogue=ep_residual_bias)
    (y1,) = _rms_fwd("rms_mlp0", h1, [nmlp[0]])
    (wmi_g[0],) = gather_end("mlp_in0", gather_land("mlp_in0", ag_mi0, y1), y1)
    h2, r0, r0sq = mlp_fwd(0, h1, y1, lambda r2: gather_end("mlp_out0", gather_land("mlp_out0", ag_mo0, r2), r2)[0])
    land_attn = gather_land("attn", ag_attn, h2)
    ykv, y2 = _rms_fwd("rms_kv_mix1", h2, [kvn + land_attn[0][3][0:1, 0:1], nm[1]])
    wkv_g, wq_g, wo_g = gather_end("attn", land_attn, y2)
    wkv_f, wq_f, wo_f = wkv_g.reshape(d, kvw), wq_g.reshape(d, d), wo_g.reshape(d, d)
    kv_parts = _matmul("kv_proj", "nn", ykv, wkv_f, m=s, n=kvw, k=d, tn=kvw // 2,
                       outs=by_residue, extras=tab_extras, epilogue=ep_rope_k, stage=True)
    q_parts = _matmul("q_proj", "nn", y2, wq_f, m=s, n=d, k=d,
                      outs=by_residue, extras=tab_extras, epilogue=ep_rope, stage=True)
    o_parts, lse_parts = [], []
    for dil, q_b, kv_b in zip(DILATIONS, q_parts, kv_parts):
        o_b, lse_b = _attn_fwd(f"attn_fwd_d{dil}", q_b, kv_b)
        o_parts.append(o_b)
        lse_parts.append(_heads_from_branch(lse_b, dil, group))
    o, lse = _attn_combine(o_parts, lse_parts)
    land_mlp1 = gather_land("mlp1", ag_mlp1, o)
    (h3,) = _matmul("attn_out", "nn", o, wo_f, m=s, n=d, k=d,
                    outs=[(F32, "plain")], extras=[(h2, "ij")], epilogue=ep_residual)
    (y3,) = _rms_fwd("rms_mlp1", h3, [nmlp[1]])
    wmi_g[1], wmo1_g = gather_end("mlp1", land_mlp1, y3)
    h4, r1, r1sq = mlp_fwd(1, h3, y3, lambda r2: wmo1_g)
    dh4, dh4b, d_fin, loss_cols = _final_loss(h4, fin, target)

    def ep_relu2_bwd(acc, ex, outs, j):
        outs[0][...] = (acc * (2.0 * ex[0][...].astype(F32))).astype(BF16)

    def mlp_bwd(idx, dhb, y, r, r2):
        (dz,) = _matmul(f"mlp_out{idx}_dx", "nt", dhb, wmo_f[idx], m=s, n=dff, k=d,
                        outs=[(BF16, "plain")], extras=[(r, "ij")], epilogue=ep_relu2_bwd)
        (dwo,) = _matmul(f"mlp_out{idx}_dw", "tn", r2, dhb, m=dff, n=d, k=s,
                         outs=[(BF16, "plain")])
        (dy,) = _matmul(f"mlp_in{idx}_dx", "nt", dz, wmi_g[idx], b_kind="col", m=s, n=d, k=dff,
                        outs=[(BF16, "plain")])
        (dwi,) = _matmul(f"mlp_in{idx}_dw", "tn", y, dz, m=d, n=dff, k=s,
                         outs=[(BF16, "col")])
        return dy, dwi, dwo.reshape(N_SHARD, dff // N_SHARD, d)

    def token(handle):
        return handle[3][0:1, 0:1]

    def rs_exchange(tag, grads):
        lands = [lax.empty((N_SHARD, g.shape[1] // 2, g.shape[2]), g.dtype) for g in grads]
        return _split_start(f"sibling_start_{tag}", list(grads) + lands, len(grads), _sibling_copies)

    def rs_send(tag, names, exchanged, later):
        bufs = _split_wait(f"sibling_wait_{tag}", exchanged, _sibling_copies, later)
        n = len(names)
        sums = [_chip_sum(f"chip_sum_{nme}", g, rh, place) for nme, g, rh in zip(names, bufs[:n], bufs[n:])]
        lands = [lax.empty((N_SHARD - 1,) + cs.shape[1:], cs.dtype) for cs in sums]
        return _split_start(f"owners_start_{tag}", sums + lands, 3 * n, _owner_copies)

    def rs_sum(tag, names, sent, later):
        bufs = _split_wait(f"owners_wait_{tag}", sent, _owner_copies, later)
        n = len(names)
        own = [_owner_sum(f"owner_sum_{nme}", cs, rp, place) for nme, cs, rp in zip(names, bufs[:n], bufs[n:])]
        return _split_start(f"swap_start_{tag}", own, n, _swap_copies)

    def rs_end(tag, swapped, later):
        return _split_wait(f"swap_wait_{tag}", swapped, _swap_copies, later)

    dy3, g_wmi1, g_wmo1 = mlp_bwd(1, dh4b, y3, r1, r1sq)
    x_mlp1 = rs_exchange("mlp1", [g_wmi1, g_wmo1])
    dh3, dh3b, d_nmlp1 = _rms_bwd("rms_mlp1_bwd", h3, [(nmlp[1] + token(x_mlp1), dy3)], dh4)

    do_parts = _matmul("attn_out_dx", "nt", dh3b, wo_f, m=s, n=d, k=d, outs=by_residue, epilogue=ep_by_residue,
                       stage=True)
    (g_wo,) = _matmul("attn_out_dw", "tn", o, dh3b, m=d, n=d, k=s, outs=[(BF16, "plain")])
    rs_mlp1 = rs_send("mlp1", ["mlp_in1", "mlp_out1"], x_mlp1, g_wo)
    delta = _attn_delta(do_parts[0].reshape(s, d), o)
    lse = lse + token(rs_mlp1)
    dq_parts, dk_parts, dv_parts = [], [], []
    for dil, q_b, kv_b, do_b in zip(DILATIONS, q_parts, kv_parts, do_parts):
        dq_b, dk_b, dv_b = _attn_bwd(f"attn_bwd_d{dil}", q_b, kv_b, do_b, _heads_to_branch(lse, dil, group),
                                     _heads_to_branch(delta, dil, group))
        dq_parts.append(dq_b)
        dk_parts.append(dk_b)
        dv_parts.append(dv_b)
    dq = _residue_sum("rope_bwd_q", [(dq_parts, True)], tabs)
    dkv = _residue_sum("rope_bwd_kv", [(dk_parts, True), (dv_parts, False)], tabs)
    (g_wq,) = _matmul("q_proj_dw", "tn", y2, dq, m=d, n=d, k=s, outs=[(BF16, "plain")])
    (dy2,) = _matmul("q_proj_dx", "nt", dq, wq_f, m=s, n=d, k=d, outs=[(BF16, "plain")])
    (g_wkv,) = _matmul("kv_proj_dw", "tn", ykv, dkv, m=d, n=kvw, k=s, outs=[(BF16, "plain")])
    (dykv,) = _matmul("kv_proj_dx", "nt", dkv, wkv_f, m=s, n=d, k=kvw, outs=[(BF16, "plain")])
    x_attn = rs_exchange("attn", [g_wkv.reshape(N_SHARD, ds4, kvw), g_wq.reshape(N_SHARD, ds4, d),
                                  g_wo.reshape(N_SHARD, ds4, d)])
    dh2, dh2b, d_nm1, d_kvn = _rms_bwd("rms_kv_mix1_bwd", h2, [(nm[1] + token(x_attn), dy2), (kvn, dykv)], dh3)
    rs_attn = rs_send("attn", ["w_kv", "w_q", "w_o"], x_attn, dh2b)

    dy1, g_wmi0, g_wmo0 = mlp_bwd(0, dh2b, y1, r0, r0sq)
    x_mlp0 = rs_exchange("mlp0", [g_wmi0, g_wmo0])
    dh1, dh1b, d_nmlp0, d_b_out = _rms_bwd("rms_mlp0_bwd", h1, [(nmlp[0] + token(x_mlp0) + token(rs_attn), dy1)],
                                           dh2, want_colsum=True)

    (dsact,) = _matmul("conv_out_dx", "nt", dh1b, w_out_f, m=s, n=d, k=d, outs=[(BF16, "plain")])
    (g_wout,) = _matmul("conv_out_dw", "tn", sact, dh1b, m=d, n=d, k=s, outs=[(BF16, "plain")])
    rs_mlp0 = rs_send("mlp0", ["mlp_in0", "mlp_out0"], x_mlp0, g_wout)
    dc, d_ln_g, d_ln_b, d_b_dw = _ln_silu_bwd(cpre, ln_g_f + token(rs_mlp0), ln_b_f, dsact)
    da, dgt, d_w_dw, d_b_in_a, d_b_in_g = _dwconv_bwd(u, w_dw_f, dc)
    du = jnp.concatenate([da, dgt], axis=1)
    (g_win,) = _matmul("conv_in_dw", "tn", y0, du, m=d, n=2 * d, k=s, outs=[(BF16, "col")])
    x_conv = rs_exchange("conv", [g_win, g_wout.reshape(N_SHARD, ds4, d)])
    (dy0,) = _matmul("conv_in_dx", "nt", du, w_in_g, b_kind="col", m=s, n=d, k=2 * d,
                     outs=[(BF16, "plain")])
    dx, _, d_nm0 = _rms_bwd("rms_mix0_bwd", h0, [(nm[0] + token(x_conv), dy0)], dh1)

    small_rows = [(0, d_nm0), (1, d_nm1), (2, d_nmlp0), (3, d_nmlp1), (4, d_kvn), (5, d_fin), (6, d_b_dw),
                  (7, d_ln_g), (8, d_ln_b), (9, d_b_out), (10, d_b_in_a), (11, d_b_in_g), (12, loss_cols)]
    x_small = _split_start("small_start", [_small_pack(small_rows, d_w_dw, d),
                                           lax.empty((N_DEV, SMALL_ROWS, d), F32)], N_DEV - 1, _small_copies)
    rs_conv = rs_send("conv", ["w_in", "w_out"], x_conv, x_small[3])

    def big(name, w, m, v, g, layer=0, partial=None):
        shape = w.shape
        w3, m3, v3 = [t.reshape((-1,) + shape[-2:]) for t in (w, m, v)]
        if partial is not None:
            partial = [t.reshape(w3.shape) for t in partial]
        res = _adamw(name, w3, m3, v3, g, layer, partial)
        return [t.reshape(shape) for t in res]

    sw_mlp1 = rs_sum("mlp1", ["mlp_in1", "mlp_out1"], rs_mlp1, rs_conv[3])
    sw_attn = rs_sum("attn", ["w_kv", "w_q", "w_o"], rs_attn, sw_mlp1[3])
    f_wmi1, f_wmo1 = rs_end("mlp1", sw_mlp1, sw_attn[3])
    p_wmi = big("adam_mlp_in1", mlp_w_in, m_mlp_w_in, v_mlp_w_in, f_wmi1, 1)
    p_wmo = big("adam_mlp_out1", mlp_w_out, m_mlp_w_out, v_mlp_w_out, f_wmo1, 1)
    sw_mlp0 = rs_sum("mlp0", ["mlp_in0", "mlp_out0"], rs_mlp0, p_wmo[0])
    f_wkv, f_wq, f_wo = rs_end("attn", sw_attn, sw_mlp0[3])
    r_wkv = big("adam_w_kv", w_kv, m_w_kv, v_w_kv, f_wkv)
    r_wq = big("adam_w_q", attn_w_q, m_attn_w_q, v_attn_w_q, f_wq)
    r_wo = big("adam_w_o", attn_w_o, m_attn_w_o, v_attn_w_o, f_wo)
    sw_conv = rs_sum("conv", ["w_in", "w_out"], rs_conv, r_wo[0])
    f_wmi0, f_wmo0 = rs_end("mlp0", sw_mlp0, sw_conv[3])
    r_wmi = big("adam_mlp_in0", mlp_w_in, m_mlp_w_in, v_mlp_w_in, f_wmi0, 0, p_wmi)
    r_wmo = big("adam_mlp_out0", mlp_w_out, m_mlp_w_out, v_mlp_w_out, f_wmo0, 0, p_wmo)
    f_win, f_wout = rs_end("conv", sw_conv, r_wmo[0])
    r_win = big("adam_w_in", conv_w_in, m_conv_w_in, v_conv_w_in, f_win)
    r_wout = big("adam_w_out", conv_w_out, m_conv_w_out, v_conv_w_out, f_wout)

    small_pack, small_slots = _split_wait("small_wait", x_small, _small_copies, r_wout[0])
    red = _small_sum(small_pack, small_slots, place)
    loss = red[12, 0]
    g_norm_mix = red[0:2]
    g_norm_mlp = red[2:4]
    g_kv_norm = red[4:5]
    g_final = red[5:6]

    def my_cols(row):
        return lax.dynamic_slice(red, (row, me * ds4), (1, ds4))

    g_b_dw, g_ln_g, g_ln_b, g_b_out = my_cols(6), my_cols(7), my_cols(8), my_cols(9)
    half_in = 2 * d // N_SHARD
    b_in_row = 10 + me // 2
    g_b_in = lax.dynamic_slice(red, (b_in_row, (me % 2) * half_in), (1, half_in))
    g_w_dw = lax.dynamic_slice(red, (16, me * ds4), (CONV_WIDTH, ds4))

    sm_w =[norm_mix, norm_mlp, conv_b_in, conv_w_dw.reshape(CONV_WIDTH, ds4), conv_b_dw, conv_ln_g, conv_ln_b,
            conv_b_out, kv_norm.reshape(1, d), final_norm.reshape(1, d)]
    sm_m = [m_norm_mix, m_norm_mlp, m_conv_b_in, m_conv_w_dw.reshape(CONV_WIDTH, ds4), m_conv_b_dw, m_conv_ln_g,
            m_conv_ln_b, m_conv_b_out, m_kv_norm.reshape(1, d), m_final_norm.reshape(1, d)]
    sm_v = [v_norm_mix, v_norm_mlp, v_conv_b_in, v_conv_w_dw.reshape(CONV_WIDTH, ds4), v_conv_b_dw, v_conv_ln_g,
            v_conv_ln_b, v_conv_b_out, v_kv_norm.reshape(1, d), v_final_norm.reshape(1, d)]
    sm_g = [g_norm_mix, g_norm_mlp, g_b_in, g_w_dw, g_b_dw, g_ln_g, g_ln_b, g_b_out, g_kv_norm, g_final]
    sm_d, sm_nm, sm_nv = _adam_small(sm_w, sm_m, sm_v, sm_g)
    shapes = [norm_mix.shape, norm_mlp.shape, conv_b_in.shape, conv_w_dw.shape, conv_b_dw.shape, conv_ln_g.shape,
              conv_ln_b.shape, conv_b_out.shape, kv_norm.shape, final_norm.shape]
    sm_g, sm_d, sm_nm, sm_nv = [[t.reshape(sh) for t, sh in zip(lst, shapes)] for lst in (sm_g, sm_d, sm_nm, sm_nv)]

    def order(sm, idx):
        return [sm[0], sm[1], r_win[idx], sm[2], sm[3], sm[4], sm[5], sm[6], r_wout[idx], sm[7], sm[8],
                r_wkv[idx], r_wq[idx], r_wo[idx], r_wmi[idx], r_wmo[idx], sm[9]]

    return (loss, dx.reshape(x.shape), *order(sm_g, 0), *order(sm_d, 1), *order(sm_nm, 2), *order(sm_nv, 3))
```

```python
import functools
import math

import jax
import jax.numpy as jnp
from jax import lax
from jax.experimental import pallas as pl
from jax.experimental.pallas import tpu as pltpu

F32 = jnp.float32
BF16 = jnp.bfloat16
I32 = jnp.int32

NORM_EPS = 1e-6
LN_EPS = 1e-5
HEAD_DIM = 128
N_KV_HEADS = 4
ROT_DIM = 32
ROPE_THETA = 500000.0
CONV_WIDTH = 31
CONV_PAD = 32
ATT_BLOCK = 128
DILATIONS = (1, 4, 16)
ADAM_LR = 0.001
ADAM_B1 = 0.9
ADAM_B2 = 0.999
ADAM_EPS = 1e-08
ADAM_WD = 0.01
ADAM_STEP = 10
N_SHARD = 4
N_DEV = 8
LANES = 128
VMEM_LIMIT = 48 * 1024 * 1024
MM_TM, MM_TN, MM_TK = 1024, 1024, 2048
ROW_TILE = 256
CONV_CB = 128
CONV_T = 128
SMALL_ROWS = 48
MESH = pl.DeviceIdType.MESH
ANY = pl.BlockSpec(memory_space=pl.ANY)
HBM = pl.BlockSpec(memory_space=pltpu.HBM)
SEM = pl.BlockSpec(memory_space=pltpu.SEMAPHORE)
SPLIT_EFFECT = pltpu.SideEffectType.DATAFLOW_SIDE_EFFECTING


def _params(*sem):
    return pltpu.CompilerParams(dimension_semantics=sem, vmem_limit_bytes=VMEM_LIMIT)


def _sigmoid(x):
    return 1.0 / (1.0 + jnp.exp(-x))


def _wspec(kind, arr_shape, br, bc, pick):
    if kind == "plain":
        return pl.BlockSpec((br, bc), pick)
    per = arr_shape[2] // bc

    def idx(*g):
        rb, cb = pick(*g)
        return (cb // per, rb, cb % per)

    return pl.BlockSpec((None, br, bc), idx)


def _stage_shape(rows, w):
    return (w // LANES, rows, LANES)


def _to_residues(val, stage_ref, out_refs, dils):
    planes, rows, _ = stage_ref.shape
    for c in range(planes):
        stage_ref[c] = val[:, c * LANES:(c + 1) * LANES]
    for out_ref, dil in zip(out_refs, dils):
        if dil == 1:
            out_ref[0] = val.astype(out_ref.dtype)
            continue
        for r in range(dil):
            for c in range(planes):
                out_ref[r, :, c * LANES:(c + 1) * LANES] = stage_ref.at[c][pl.ds(r, rows // dil, stride=dil), :].astype(
                    out_ref.dtype)


def _from_residues(src_ref, stage_ref, dil):
    planes, rows, _ = stage_ref.shape
    if dil == 1:
        return lambda c: src_ref[0, :, c * LANES:(c + 1) * LANES].astype(F32)
    for r in range(dil):
        for c in range(planes):
            stage_ref.at[c][pl.ds(r, rows // dil, stride=dil), :] = src_ref[r, :, c * LANES:(c + 1) * LANES].astype(F32)
    return lambda c: stage_ref[c]


def _matmul(name, mode, a, b, *, m, n, k, tn=MM_TN, a_kind="plain", b_kind="plain", outs, extras=(), epilogue=None,
            stage=False):
    tm, tn, tk = min(MM_TM, m), min(tn, n), min(MM_TK, k)
    if b_kind == "col" and mode in ("nn", "tn"):
        tn = min(tn, n // b.shape[0])
    if b_kind == "col" and mode == "nt":
        tk = min(tk, k // b.shape[0])
    if a_kind == "col":
        assert mode == "nt"
        tk = min(tk, k // a.shape[0])
    if any(kind == "col" for _, kind in outs):
        tn = min(tn, n // N_SHARD)
    assert m % tm == 0 and n % tn == 0 and k % tk == 0, (name, m, n, k, tm, tn, tk)
    nk = k // tk
    grid = (m // tm, n // tn, nk)
    if mode == "nn":
        a_spec = pl.BlockSpec((tm, tk), lambda i, j, kk: (i, kk))
        b_spec = _wspec(b_kind, b.shape, tk, tn, lambda i, j, kk: (kk, j))
        dims = (((1,), (0,)), ((), ()))
    elif mode == "nt":
        a_spec = _wspec(a_kind, a.shape, tm, tk, lambda i, j, kk: (i, kk))
        b_spec = _wspec(b_kind, b.shape, tn, tk, lambda i, j, kk: (j, kk))
        dims = (((1,), (1,)), ((), ()))
    else:
        a_spec = pl.BlockSpec((tk, tm), lambda i, j, kk: (kk, i))
        b_spec = _wspec(b_kind, b.shape, tk, tn, lambda i, j, kk: (kk, j))
        dims = (((0,), (0,)), ((), ()))
    out_shape, out_specs = [], []
    for dtype, kind in outs:
        if isinstance(kind, tuple):
            dil = kind[1]
            out_shape.append(jax.ShapeDtypeStruct((dil, m // dil, n), dtype))
            out_specs.append(pl.BlockSpec((dil, tm // dil, tn), lambda i, j, kk: (0, i, j)))
            continue
        shape = (m, n) if kind == "plain" else (N_SHARD, m, n // N_SHARD)
        out_shape.append(jax.ShapeDtypeStruct(shape, dtype))
        out_specs.append(_wspec(kind, shape, tm, tn, lambda i, j, kk: (i, j)))
    n_ex = len(extras)
    ex_specs = {"ij": pl.BlockSpec((tm, tn), lambda i, j, kk: (i, j)),
                "vec": pl.BlockSpec((1, tn), lambda i, j, kk: (0, j)),
                "rows": pl.BlockSpec((tm, LANES), lambda i, j, kk: (i, 0))}

    def body(*refs):
        a_ref, b_ref = refs[0], refs[1]
        ex_refs = refs[2:2 + n_ex]
        out_refs = refs[2 + n_ex:2 + n_ex + len(outs)]
        j = pl.program_id(1)

        def finish(res):
            if epilogue is None:
                out_refs[0][...] = res.astype(out_refs[0].dtype)
            elif stage:
                epilogue(res, ex_refs, out_refs, j, refs[-1])
            else:
                epilogue(res, ex_refs, out_refs, j)

        prod = lax.dot_general(a_ref[...], b_ref[...], dims, preferred_element_type=F32)
        if nk == 1:
            finish(prod)
            return
        acc_ref = refs[2 + n_ex + len(outs)]
        kk = pl.program_id(2)

        @pl.when(kk == 0)
        def _():
            acc_ref[...] = prod

        @pl.when(kk > 0)
        def _():
            acc_ref[...] += prod

        @pl.when(kk == nk - 1)
        def _():
            finish(acc_ref[...])

    res = pl.pallas_call(
        body, name=name, grid=grid,
        in_specs=[a_spec, b_spec] + [ex_specs[how] for _, how in extras],
        out_specs=out_specs, out_shape=out_shape,
        scratch_shapes=[pltpu.VMEM((tm, tn), F32)] * (nk > 1) + [pltpu.VMEM(_stage_shape(tm, tn), F32)] * bool(stage),
        compiler_params=_params("parallel", "parallel", "arbitrary"),
    )(a, b, *[e for e, _ in extras])
    return res


def _rope_tables(seq):
    half = ROT_DIM // 2
    pos = jnp.arange(seq, dtype=F32)
    inv = ROPE_THETA ** (-jnp.arange(0, ROT_DIM, 2, dtype=F32) / ROT_DIM)
    ang = pos[:, None] * inv[None, :]
    cos, sin = jnp.cos(ang), jnp.sin(ang)
    zeros = jnp.zeros((seq, HEAD_DIM - ROT_DIM), F32)
    ctab = jnp.concatenate([cos, cos, zeros + 1.0], axis=1)
    atab = jnp.concatenate([-sin, jnp.zeros((seq, half), F32), zeros], axis=1)
    btab = jnp.concatenate([jnp.zeros((seq, half), F32), sin, zeros], axis=1)
    return ctab, atab, btab


def _rope_apply(x, ctab, atab, btab, sign):
    w = x.shape[1]
    reps = w // HEAD_DIM
    half = ROT_DIM // 2
    c = jnp.tile(ctab, (1, reps))
    a = jnp.tile(atab, (1, reps))
    b = jnp.tile(btab, (1, reps))
    up = pltpu.roll(x, w - half, 1)
    down = pltpu.roll(x, half, 1)
    return x * c + sign * (up * a + down * b)


def _rows(t, w):
    return pl.BlockSpec((t, w), lambda i: (i, 0))


def _fixed(shape):
    nd = len(shape)
    return pl.BlockSpec(shape, lambda i: (0,) * nd)


def _rms_fwd(name, x, gains):
    s, d = x.shape
    t = min(ROW_TILE, s)
    ng = len(gains)

    def body(x_ref, *refs):
        xv = x_ref[...]
        r = lax.rsqrt(jnp.mean(xv * xv, axis=-1, keepdims=True) + NORM_EPS)
        xn = xv * r
        for g_ref, y_ref in zip(refs[:ng], refs[ng:]):
            y_ref[...] = (xn * g_ref[...]).astype(BF16)

    return pl.pallas_call(
        body, name=name, grid=(s // t,),
        in_specs=[_rows(t, d)] + [_fixed((1, d))] * ng,
        out_specs=[_rows(t, d)] * ng,
        out_shape=[jax.ShapeDtypeStruct((s, d), BF16)] * ng,
        compiler_params=_params("parallel"),
    )(x, *gains)


def _rms_bwd(name, x, pairs, dh_in, want_colsum=False):
    s, d = x.shape
    t = min(ROW_TILE, s)
    n_p = len(pairs)

    def body(x_ref, dh_ref, *refs):
        g_refs = refs[:n_p]
        dy_refs = refs[n_p:2 * n_p]
        dh_out, dhb_out = refs[2 * n_p], refs[2 * n_p + 1]
        dg_refs = refs[2 * n_p + 2:2 * n_p + 2 + n_p]
        cs_ref = refs[-1] if want_colsum else None
        i = pl.program_id(0)
        xv = x_ref[...]
        r = lax.rsqrt(jnp.mean(xv * xv, axis=-1, keepdims=True) + NORM_EPS)
        xn = xv * r
        dh = dh_ref[...]
        for g_ref, dy_ref, dg_ref in zip(g_refs, dy_refs, dg_refs):
            dy = dy_ref[...].astype(F32)
            u = dy * g_ref[...]
            dh = dh + r * (u - xn * jnp.mean(u * xn, axis=-1, keepdims=True))
            part = jnp.sum(dy * xn, axis=0, keepdims=True)

            @pl.when(i == 0)
            def _():
                dg_ref[...] = part

            @pl.when(i > 0)
            def _():
                dg_ref[...] += part

        dh_out[...] = dh
        dhb_out[...] = dh.astype(BF16)
        if want_colsum:
            col = jnp.sum(dh, axis=0, keepdims=True)

            @pl.when(i == 0)
            def _():
                cs_ref[...] = col

            @pl.when(i > 0)
            def _():
                cs_ref[...] += col

    n_vec = n_p + (1 if want_colsum else 0)
    return pl.pallas_call(
        body, name=name, grid=(s // t,),
        in_specs=[_rows(t, d), _rows(t, d)] + [_fixed((1, d))] * n_p + [_rows(t, d)] * n_p,
        out_specs=[_rows(t, d), _rows(t, d)] + [_fixed((1, d))] * n_vec,
        out_shape=[jax.ShapeDtypeStruct((s, d), F32), jax.ShapeDtypeStruct((s, d), BF16)]
        + [jax.ShapeDtypeStruct((1, d), F32)] * n_vec,
        compiler_params=_params("arbitrary"),
    )(x, dh_in, *[g for g, _ in pairs], *[dy for _, dy in pairs])


def _final_loss(x, g, target):
    s, d = x.shape
    t = min(ROW_TILE, s)

    def body(x_ref, g_ref, t_ref, dh_out, dhb_out, dg_ref, loss_ref):
        i = pl.program_id(0)
        xv = x_ref[...]
        gv = g_ref[...]
        r = lax.rsqrt(jnp.mean(xv * xv, axis=-1, keepdims=True) + NORM_EPS)
        xn = xv * r
        diff = xn * gv - t_ref[...]
        dy = diff / d
        u = dy * gv
        dh = r * (u - xn * jnp.mean(u * xn, axis=-1, keepdims=True))
        dh_out[...] = dh
        dhb_out[...] = dh.astype(BF16)
        dg = jnp.sum(dy * xn, axis=0, keepdims=True)
        lc = jnp.sum(0.5 * diff * dy, axis=0, keepdims=True)

        @pl.when(i == 0)
        def _():
            dg_ref[...] = dg
            loss_ref[...] = lc

        @pl.when(i > 0)
        def _():
            dg_ref[...] += dg
            loss_ref[...] += lc

    return pl.pallas_call(
        body, name="final_loss", grid=(s // t,),
        in_specs=[_rows(t, d), _fixed((1, d)), _rows(t, d)],
        out_specs=[_rows(t, d), _rows(t, d), _fixed((1, d)), _fixed((1, d))],
        out_shape=[jax.ShapeDtypeStruct((s, d), F32), jax.ShapeDtypeStruct((s, d), BF16),
                   jax.ShapeDtypeStruct((1, d), F32), jax.ShapeDtypeStruct((1, d), F32)],
        compiler_params=_params("arbitrary"),
    )(x, g, target)


def _ln_silu_fwd(c, g, b):
    s, d = c.shape
    t = min(ROW_TILE, s)

    def body(c_ref, g_ref, b_ref, s_ref):
        cv = c_ref[...]
        mu = jnp.mean(cv, axis=-1, keepdims=True)
        xc = cv - mu
        rs = lax.rsqrt(jnp.mean(xc * xc, axis=-1, keepdims=True) + LN_EPS)
        ln = xc * rs * g_ref[...] + b_ref[...]
        s_ref[...] = (ln * _sigmoid(ln)).astype(BF16)

    return pl.pallas_call(
        body, name="ln_silu_fwd", grid=(s // t,),
        in_specs=[_rows(t, d), _fixed((1, d)), _fixed((1, d))],
        out_specs=_rows(t, d), out_shape=jax.ShapeDtypeStruct((s, d), BF16),
        compiler_params=_params("parallel"),
    )(c, g, b)


def _ln_silu_bwd(c, g, b, ds):
    s, d = c.shape
    t = min(ROW_TILE, s)

    def body(c_ref, g_ref, b_ref, ds_ref, dc_ref, dg_ref, db_ref, dbdw_ref):
        i = pl.program_id(0)
        cv = c_ref[...]
        gv = g_ref[...]
        mu = jnp.mean(cv, axis=-1, keepdims=True)
        xc = cv - mu
        rs = lax.rsqrt(jnp.mean(xc * xc, axis=-1, keepdims=True) + LN_EPS)
        nrm = xc * rs
        ln = nrm * gv + b_ref[...]
        sig = _sigmoid(ln)
        dln = ds_ref[...].astype(F32) * sig * (1.0 + ln * (1.0 - sig))
        dn = dln * gv
        dc = rs * (dn - jnp.mean(dn, axis=-1, keepdims=True)
                   - nrm * jnp.mean(dn * nrm, axis=-1, keepdims=True))
        dc_ref[...] = dc
        pg = jnp.sum(dln * nrm, axis=0, keepdims=True)
        pb = jnp.sum(dln, axis=0, keepdims=True)
        pc = jnp.sum(dc, axis=0, keepdims=True)

        @pl.when(i == 0)
        def _():
            dg_ref[...] = pg
            db_ref[...] = pb
            dbdw_ref[...] = pc

        @pl.when(i > 0)
        def _():
            dg_ref[...] += pg
            db_ref[...] += pb
            dbdw_ref[...] += pc

    return pl.pallas_call(
        body, name="ln_silu_bwd", grid=(s // t,),
        in_specs=[_rows(t, d), _fixed((1, d)), _fixed((1, d)), _rows(t, d)],
        out_specs=[_rows(t, d)] + [_fixed((1, d))] * 3,
        out_shape=[jax.ShapeDtypeStruct((s, d), F32)] + [jax.ShapeDtypeStruct((1, d), F32)] * 3,
        compiler_params=_params("arbitrary"),
    )(c, g, b, ds)


def _residue_spec(dil, t, w):
    return pl.BlockSpec((dil, t // dil, w), lambda i: (0, i, 0))


def _attn_combine(o_list, lse_list):
    dil0, sd0, d = o_list[0].shape
    s = dil0 * sd0
    lw = lse_list[0].shape[2]
    group = d // HEAD_DIM // N_KV_HEADS
    t = min(ROW_TILE, s)
    nb = len(o_list)
    dils = [o.shape[0] for o in o_list]

    def body(*refs):
        o_out, l_out = refs[2 * nb], refs[2 * nb + 1]
        o_stage, l_stage = refs[2 * nb + 2:3 * nb + 2], refs[3 * nb + 2:]
        o_planes = [_from_residues(src, stage, dil) for src, stage, dil in zip(refs[:nb], o_stage, dils)]
        l_planes = [_from_residues(src, stage, dil) for src, stage, dil in zip(refs[nb:2 * nb], l_stage, dils)]
        for kh in range(N_KV_HEADS):
            ls = [plane(kh) for plane in l_planes]
            mx = ls[0]
            for l in ls[1:]:
                mx = jnp.maximum(mx, l)
            es = [jnp.exp(l - mx) for l in ls]
            den = es[0]
            for e in es[1:]:
                den = den + e
            l_out[:, kh * LANES:(kh + 1) * LANES] = mx + jnp.log(den)
            ws = [e / den for e in es]
            for g in range(group):
                h = kh * group + g
                acc = jnp.zeros((t, HEAD_DIM), F32)
                for plane, w in zip(o_planes, ws):
                    acc = acc + w[:, g:g + 1] * plane(h)
                o_out[:, h * HEAD_DIM:(h + 1) * HEAD_DIM] = acc.astype(BF16)

    return pl.pallas_call(
        body, name="attn_combine", grid=(s // t,),
        in_specs=[_residue_spec(dil, t, d) for dil in dils] + [_residue_spec(dil, t, lw) for dil in dils],
        out_specs=[_rows(t, d), _rows(t, lw)],
        out_shape=[jax.ShapeDtypeStruct((s, d), BF16), jax.ShapeDtypeStruct((s, lw), F32)],
        scratch_shapes=[pltpu.VMEM(_stage_shape(t, d), F32)] * nb + [pltpu.VMEM(_stage_shape(t, lw), F32)] * nb,
        compiler_params=_params("parallel"),
    )(*o_list, *lse_list)


def _attn_delta(do, o, lse, dils):
    s, d = o.shape
    lw = lse.shape[1]
    group = d // HEAD_DIM // N_KV_HEADS
    t = min(ROW_TILE, s)
    nd = len(dils)

    def body(do_ref, o_ref, lse_ref, *refs):
        stage = refs[-1]
        lane = lax.broadcasted_iota(I32, (t, LANES), 1)
        planes = []
        for kh in range(N_KV_HEADS):
            out = jnp.zeros((t, LANES), F32)
            for g in range(group):
                cols = slice((kh * group + g) * HEAD_DIM, (kh * group + g + 1) * HEAD_DIM)
                v = jnp.sum(do_ref[:, cols].astype(F32) * o_ref[:, cols].astype(F32), axis=-1, keepdims=True)
                out = jnp.where(lane == g, v, out)
            planes.append(out)
        _to_residues(lse_ref[...], stage, refs[:nd], dils)
        _to_residues(jnp.concatenate(planes, axis=1), stage, refs[nd:2 * nd], dils)

    res = pl.pallas_call(
        body, name="attn_delta", grid=(s // t,),
        in_specs=[_rows(t, d), _rows(t, d), _rows(t, lw)],
        out_specs=[_residue_spec(dil, t, lw) for dil in dils] * 2,
        out_shape=[jax.ShapeDtypeStruct((dil, s // dil, lw), F32) for dil in dils] * 2,
        scratch_shapes=[pltpu.VMEM(_stage_shape(t, lw), F32)],
        compiler_params=_params("parallel"),
    )(do, o, lse)
    return res[:nd], res[nd:]


def _residue_sum(name, groups, tabs):
    first = groups[0][0][0]
    s, w = first.shape[0] * first.shape[1], first.shape[2]
    t = min(ROW_TILE, s)
    flat = [p for parts, _ in groups for p in parts]

    def body(*refs):
        c_ref, a_ref, b_ref = refs[len(flat):len(flat) + 3]
        out = refs[len(flat) + 3]
        stages = refs[len(flat) + 4:]
        k = 0
        for gi, (parts, rotate) in enumerate(groups):
            planes = [_from_residues(refs[k + i], stages[k + i], p.shape[0]) for i, p in enumerate(parts)]
            k += len(parts)
            for c in range(w // LANES):
                tot = planes[0](c)
                for plane in planes[1:]:
                    tot = tot + plane(c)
                if rotate:
                    tot = _rope_apply(tot, c_ref[...], a_ref[...], b_ref[...], -1.0)
                out[:, gi * w + c * LANES:gi * w + (c + 1) * LANES] = tot.astype(BF16)

    return pl.pallas_call(
        body, name=name, grid=(s // t,),
        in_specs=[_residue_spec(p.shape[0], t, w) for p in flat] + [_rows(t, HEAD_DIM)] * 3,
        out_specs=_rows(t, len(groups) * w), out_shape=jax.ShapeDtypeStruct((s, len(groups) * w), BF16),
        scratch_shapes=[pltpu.VMEM(_stage_shape(t, w), F32) for _ in flat],
        compiler_params=_params("parallel"),
    )(*flat, *tabs)


def _dwconv_fwd(u, w_dw, b_dw):
    s, d2 = u.shape
    d = d2 // 2
    cb = min(CONV_CB, d)
    nblk = d // cb
    tt = min(CONV_T, s)

    def body(ua_ref, ug_ref, w_ref, b_ref, c_ref, xp_ref):
        gl = ua_ref[...].astype(F32) * _sigmoid(ug_ref[...].astype(F32))
        xp_ref[0:CONV_PAD, :] = jnp.zeros((CONV_PAD, cb), F32)
        xp_ref[CONV_PAD:, :] = gl
        wv = w_ref[...]
        bv = b_ref[...]
        for t0 in range(0, s, tt):
            acc = jnp.zeros((tt, cb), F32) + bv
            for kk in range(CONV_WIDTH):
                off = t0 + CONV_PAD - (CONV_WIDTH - 1) + kk
                acc = acc + wv[kk:kk + 1, :] * xp_ref[off:off + tt, :]
            c_ref[t0:t0 + tt, :] = acc

    return pl.pallas_call(
        body, name="dwconv_fwd", grid=(nblk,),
        in_specs=[pl.BlockSpec((s, cb), lambda j: (0, j)), pl.BlockSpec((s, cb), lambda j: (0, j + nblk)),
                  pl.BlockSpec((CONV_PAD, cb), lambda j: (0, j)), pl.BlockSpec((1, cb), lambda j: (0, j))],
        out_specs=pl.BlockSpec((s, cb), lambda j: (0, j)),
        out_shape=jax.ShapeDtypeStruct((s, d), F32),
        scratch_shapes=[pltpu.VMEM((s + CONV_PAD, cb), F32)],
        compiler_params=_params("parallel"),
    )(u, u, w_dw, b_dw)


def _dwconv_bwd(u, w_dw, dc):
    s, d2 = u.shape
    d = d2 // 2
    cb = min(CONV_CB, d)
    nblk = d // cb
    tt = min(CONV_T, s)

    def body(ua_ref, ug_ref, w_ref, dc_ref, du_ref, dw_ref, dba_ref, dbg_ref, glp_ref, dcp_ref, acc_ref):
        a = ua_ref[...].astype(F32)
        sig = _sigmoid(ug_ref[...].astype(F32))
        glp_ref[0:CONV_PAD, :] = jnp.zeros((CONV_PAD, cb), F32)
        glp_ref[CONV_PAD:, :] = a * sig
        dcp_ref[0:s, :] = dc_ref[...]
        dcp_ref[s:, :] = jnp.zeros((CONV_PAD, cb), F32)
        acc_ref[...] = jnp.zeros_like(acc_ref)
        wv = w_ref[...]
        dba = jnp.zeros((1, cb), F32)
        dbg = jnp.zeros((1, cb), F32)
        for t0 in range(0, s, tt):
            dgl = jnp.zeros((tt, cb), F32)
            dct = dc_ref[t0:t0 + tt, :]
            for kk in range(CONV_WIDTH):
                off = t0 + (CONV_WIDTH - 1) - kk
                dgl = dgl + wv[kk:kk + 1, :] * dcp_ref[off:off + tt, :]
                goff = t0 + CONV_PAD - (CONV_WIDTH - 1) + kk
                prod = dct * glp_ref[goff:goff + tt, :]
                acc_ref[8 * kk:8 * kk + 8, :] += jnp.sum(prod.reshape(tt // 8, 8, cb), axis=0)
            at = ua_ref[t0:t0 + tt, :].astype(F32)
            st = _sigmoid(ug_ref[t0:t0 + tt, :].astype(F32))
            da = dgl * st
            dg = dgl * at * st * (1.0 - st)
            du_ref[0, t0:t0 + tt, :] = da.astype(BF16)
            du_ref[1, t0:t0 + tt, :] = dg.astype(BF16)
            dba = dba + jnp.sum(da, axis=0, keepdims=True)
            dbg = dbg + jnp.sum(dg, axis=0, keepdims=True)
        dba_ref[...] = dba
        dbg_ref[...] = dbg
        for kk in range(CONV_WIDTH):
            dw_ref[kk:kk + 1, :] = jnp.sum(acc_ref[8 * kk:8 * kk + 8, :], axis=0, keepdims=True)
        dw_ref[CONV_WIDTH:, :] = jnp.zeros((CONV_PAD - CONV_WIDTH, cb), F32)

    blk = pl.BlockSpec((s, cb), lambda j: (0, j))
    vec = pl.BlockSpec((1, cb), lambda j: (0, j))
    return pl.pallas_call(
        body, name="dwconv_bwd", grid=(nblk,),
        in_specs=[blk, pl.BlockSpec((s, cb), lambda j: (0, j + nblk)),
                  pl.BlockSpec((CONV_PAD, cb), lambda j: (0, j)), blk],
        out_specs=[pl.BlockSpec((2, s, cb), lambda j: (0, 0, j)), pl.BlockSpec((CONV_PAD, cb), lambda j: (0, j)),
                   vec, vec],
        out_shape=[jax.ShapeDtypeStruct((2, s, d), BF16), jax.ShapeDtypeStruct((CONV_PAD, d), F32),
                   jax.ShapeDtypeStruct((1, d), F32), jax.ShapeDtypeStruct((1, d), F32)],
        scratch_shapes=[pltpu.VMEM((s + CONV_PAD, cb), F32), pltpu.VMEM((s + CONV_PAD, cb), F32),
                        pltpu.VMEM((8 * CONV_PAD, cb), F32)],
        compiler_params=_params("parallel"),
    )(u, u, w_dw, dc)


def _stack_heads(x, group):
    return jnp.concatenate([x[:, g * HEAD_DIM:(g + 1) * HEAD_DIM] for g in range(group)], axis=0)


def _unstack_heads(x, group):
    return jnp.concatenate([x[g * ATT_BLOCK:(g + 1) * ATT_BLOCK, :] for g in range(group)], axis=1)


def _stack_cols(x, group):
    return jnp.concatenate([x[:, g:g + 1] for g in range(group)], axis=0)


def _band_mask(nb, group):
    rows = group * ATT_BLOCK
    row = lax.broadcasted_iota(I32, (rows, 2 * ATT_BLOCK), 0) % ATT_BLOCK
    col = lax.broadcasted_iota(I32, (rows, 2 * ATT_BLOCK), 1)
    return (col >= row) & (col <= row + ATT_BLOCK) & ((col >= ATT_BLOCK) | (nb > 0))


def _window(ref, nb):
    prev = pl.multiple_of(jnp.maximum(nb - 1, 0) * ATT_BLOCK, ATT_BLOCK)
    cur = pl.multiple_of(nb * ATT_BLOCK, ATT_BLOCK)
    return jnp.concatenate([ref[pl.ds(prev, ATT_BLOCK), :], ref[pl.ds(cur, ATT_BLOCK), :]], axis=0)


def _attn_fwd(name, q, kv):
    dil, sd, d = q.shape
    group = d // HEAD_DIM // N_KV_HEADS
    gw = group * HEAD_DIM
    nblk = sd // ATT_BLOCK
    scale = 1.0 / math.sqrt(HEAD_DIM)
    nt = (((1,), (1,)), ((), ()))

    def body(q_ref, k_ref, v_ref, o_ref, lse_ref):
        lane = lax.broadcasted_iota(I32, (ATT_BLOCK, LANES), 1)

        def step(nb, carry):
            rows = pl.ds(pl.multiple_of(nb * ATT_BLOCK, ATT_BLOCK), ATT_BLOCK)
            qs = _stack_heads(q_ref[rows, :], group)
            kw = _window(k_ref, nb)
            vw = _window(v_ref, nb)
            sc = lax.dot_general(qs, kw, nt, preferred_element_type=F32) * scale
            sc = jnp.where(_band_mask(nb, group), sc, -jnp.inf)
            mx = jnp.max(sc, axis=-1, keepdims=True)
            p = jnp.exp(sc - mx)
            l = jnp.sum(p, axis=-1, keepdims=True)
            o = jnp.dot(p.astype(BF16), vw, preferred_element_type=F32) / l
            o_ref[rows, :] = _unstack_heads(o, group).astype(BF16)
            lse = mx + jnp.log(l)
            out = jnp.zeros((ATT_BLOCK, LANES), F32)
            for g in range(group):
                out = jnp.where(lane == g, lse[g * ATT_BLOCK:(g + 1) * ATT_BLOCK, :], out)
            lse_ref[rows, :] = out
            return carry

        lax.fori_loop(0, nblk, step, 0, unroll=min(2, nblk))

    kvh = N_KV_HEADS
    qspec = pl.BlockSpec((None, sd, gw), lambda r, h: (r, 0, h))
    kspec = pl.BlockSpec((None, sd, HEAD_DIM), lambda r, h: (r, 0, h))
    return pl.pallas_call(
        body, name=name, grid=(dil, kvh),
        in_specs=[qspec, kspec, pl.BlockSpec((None, sd, HEAD_DIM), lambda r, h: (r, 0, kvh + h))],
        out_specs=[qspec, kspec],
        out_shape=[jax.ShapeDtypeStruct((dil, sd, d), BF16),
                   jax.ShapeDtypeStruct((dil, sd, kvh * LANES), F32)],
        compiler_params=_params("parallel", "parallel"),
    )(q, kv, kv)


def _attn_bwd(name, q, kv, do, lse, delta):
    dil, sd, d = q.shape
    group = d // HEAD_DIM // N_KV_HEADS
    gw = group * HEAD_DIM
    nblk = sd // ATT_BLOCK
    scale = 1.0 / math.sqrt(HEAD_DIM)
    nt = (((1,), (1,)), ((), ()))
    tn = (((0,), (0,)), ((), ()))

    def body(q_ref, k_ref, v_ref, do_ref, lse_ref, dl_ref, dq_ref, dk_ref, dv_ref, dk_acc, dv_acc):
        dk_acc[...] = jnp.zeros_like(dk_acc)
        dv_acc[...] = jnp.zeros_like(dv_acc)

        def step(nb, carry):
            rows = pl.ds(pl.multiple_of(nb * ATT_BLOCK, ATT_BLOCK), ATT_BLOCK)
            qs = _stack_heads(q_ref[rows, :], group)
            dos = _stack_heads(do_ref[rows, :], group)
            ls = _stack_cols(lse_ref[rows, :], group)
            dl = _stack_cols(dl_ref[rows, :], group)
            kw = _window(k_ref, nb)
            vw = _window(v_ref, nb)
            sc = lax.dot_general(qs, kw, nt, preferred_element_type=F32) * scale
            sc = jnp.where(_band_mask(nb, group), sc, -jnp.inf)
            p = jnp.exp(sc - ls)
            dp = lax.dot_general(dos, vw, nt, preferred_element_type=F32)
            ds = (p * (dp - dl) * scale).astype(BF16)
            dq = jnp.dot(ds, kw, preferred_element_type=F32)
            dq_ref[rows, :] = _unstack_heads(dq, group).astype(BF16)
            win = pl.ds(pl.multiple_of(nb * ATT_BLOCK, ATT_BLOCK), 2 * ATT_BLOCK)
            dk_acc[win, :] += lax.dot_general(ds, qs, tn, preferred_element_type=F32)
            dv_acc[win, :] += lax.dot_general(p.astype(BF16), dos, tn, preferred_element_type=F32)
            return carry

        lax.fori_loop(0, nblk, step, 0, unroll=min(2, nblk))
        dk_ref[...] = dk_acc[ATT_BLOCK:, :]
        dv_ref[...] = dv_acc[ATT_BLOCK:, :]

    kvh = N_KV_HEADS
    qspec = pl.BlockSpec((None, sd, gw), lambda r, h: (r, 0, h))
    kspec = pl.BlockSpec((None, sd, HEAD_DIM), lambda r, h: (r, 0, h))
    return pl.pallas_call(
        body, name=name, grid=(dil, kvh),
        in_specs=[qspec, kspec, pl.BlockSpec((None, sd, HEAD_DIM), lambda r, h: (r, 0, kvh + h)),
                  qspec, kspec, kspec],
        out_specs=[qspec, kspec, kspec],
        out_shape=[jax.ShapeDtypeStruct((dil, sd, d), BF16),
                   jax.ShapeDtypeStruct((dil, sd, kvh * HEAD_DIM), F32),
                   jax.ShapeDtypeStruct((dil, sd, kvh * HEAD_DIM), F32)],
        scratch_shapes=[pltpu.VMEM((sd + ATT_BLOCK, HEAD_DIM), F32), pltpu.VMEM((sd + ATT_BLOCK, HEAD_DIM), F32)],
        compiler_params=_params("parallel", "parallel"),
    )(q, kv, kv, do, lse, delta)


def _cast_bf16(name, w, layer, place):
    _, r, c = w.shape
    tr = min(512, r)

    def body(pl_ref, w_ref, o_ref):
        o_ref[...] = w_ref[...].astype(BF16)

    return pl.pallas_call(
        body, name=name,
        grid_spec=pltpu.PrefetchScalarGridSpec(
            num_scalar_prefetch=1, grid=(r // tr,),
            in_specs=[pl.BlockSpec((None, tr, c), lambda i, p: (layer, i, 0))],
            out_specs=pl.BlockSpec((None, tr, c), lambda i, p: (p[1], i, 0))),
        out_shape=jax.ShapeDtypeStruct((N_SHARD, r, c), BF16),
        compiler_params=_params("parallel"),
    )(place, w)


def _chip_sum(name, g, rh, place):
    _, r, c = g.shape
    rh2 = r // 2
    tr = min(512, rh2)
    nb = rh2 // tr

    def body(pl_ref, g_ref, rh_ref, o_ref):
        o_ref[...] = (g_ref[...].astype(F32) + rh_ref[...].astype(F32)).astype(BF16)

    return pl.pallas_call(
        body, name=name,
        grid_spec=pltpu.PrefetchScalarGridSpec(
            num_scalar_prefetch=1, grid=(N_SHARD, nb),
            in_specs=[pl.BlockSpec((None, tr, c), lambda s, i, p: (s, p[0] * nb + i, 0)),
                      pl.BlockSpec((None, tr, c), lambda s, i, p: (s, i, 0))],
            out_specs=pl.BlockSpec((None, tr, c), lambda s, i, p: (s, i, 0))),
        out_shape=jax.ShapeDtypeStruct((N_SHARD, rh2, c), BF16),
        compiler_params=_params("parallel", "parallel"),
    )(place, g, rh)


def _owner_sum(name, cs, rp, place):
    _, rh2, c = cs.shape
    tr = min(512, rh2)
    nb = rh2 // tr

    def body(pl_ref, cs_ref, r0_ref, r1_ref, r2_ref, o_ref):
        o_ref[...] = ((cs_ref[...].astype(F32) + r0_ref[...].astype(F32))
                      + (r1_ref[...].astype(F32) + r2_ref[...].astype(F32)))

    def rspec(j):
        return pl.BlockSpec((None, tr, c), lambda i, p: (j, i, 0))

    return pl.pallas_call(
        body, name=name,
        grid_spec=pltpu.PrefetchScalarGridSpec(
            num_scalar_prefetch=1, grid=(nb,),
            in_specs=[pl.BlockSpec((None, tr, c), lambda i, p: (p[1], i, 0)), rspec(0), rspec(1), rspec(2)],
            out_specs=pl.BlockSpec((tr, c), lambda i, p: (p[0] * nb + i, 0))),
        out_shape=jax.ShapeDtypeStruct((2 * rh2, c), F32),
        compiler_params=_params("parallel"),
    )(place, cs, rp, rp, rp)


def _adam_math(w, g, m, v):
    m = ADAM_B1 * m + (1.0 - ADAM_B1) * g
    v = ADAM_B2 * v + (1.0 - ADAM_B2) * (g * g)
    m_hat = m / (1.0 - ADAM_B1 ** ADAM_STEP)
    v_hat = v / (1.0 - ADAM_B2 ** ADAM_STEP)
    delta = -ADAM_LR * (m_hat / (jnp.sqrt(v_hat) + ADAM_EPS) + ADAM_WD * w)
    return delta, m, v


def _adamw(name, w, m, v, g, layer, partial=None):
    nl, r, c = w.shape
    tr = min(256, r)

    def body(w_ref, m_ref, v_ref, g_ref, *refs):
        go_ref, d_ref, mo_ref, vo_ref = refs[-4:]
        gv = g_ref[...]
        delta, m_new, v_new = _adam_math(w_ref[...], gv, m_ref[...], v_ref[...])
        go_ref[...] = gv
        d_ref[...] = delta
        mo_ref[...] = m_new
        vo_ref[...] = v_new

    wspec = pl.BlockSpec((None, tr, c), lambda i: (layer, i, 0))
    prev = [] if partial is None else list(partial)
    return pl.pallas_call(
        body, name=name, grid=(r // tr,),
        in_specs=[wspec] * 3 + [pl.BlockSpec((tr, c), lambda i: (i, 0))] + [ANY] * len(prev),
        out_specs=[wspec] * 4,
        out_shape=[jax.ShapeDtypeStruct((nl, r, c), F32)] * 4,
        input_output_aliases={4 + i: i for i in range(len(prev))},
        compiler_params=_params("parallel"),
    )(w, m, v, g, *prev)


def _adam_small(ws, ms, vs, gs):
    n = len(ws)

    def body(*refs):
        w_refs, m_refs, v_refs, g_refs = refs[:n], refs[n:2 * n], refs[2 * n:3 * n], refs[3 * n:4 * n]
        d_refs, mo_refs, vo_refs = refs[4 * n:5 * n], refs[5 * n:6 * n], refs[6 * n:7 * n]
        for i in range(n):
            delta, m_new, v_new = _adam_math(w_refs[i][...], g_refs[i][...], m_refs[i][...], v_refs[i][...])
            d_refs[i][...] = delta
            mo_refs[i][...] = m_new
            vo_refs[i][...] = v_new

    shapes = [jax.ShapeDtypeStruct(w.shape, F32) for w in ws]
    res = pl.pallas_call(body, name="adam_small", out_shape=shapes * 3)(*ws, *ms, *vs, *gs)
    return res[:n], res[n:2 * n], res[2 * n:]


def _pack_small(b_in, w_dw, b_dw, ln_g, ln_b, b_out, place):
    cin = b_in.shape[1]
    cd = b_dw.shape[1]
    rows = 8 + CONV_PAD

    def body(pl_ref, bi, wd, bd, lg, lb, bo, out):
        out[...] = jnp.zeros_like(out)
        out[0:1, :] = bi[...]
        out[1:2, 0:cd] = bd[...]
        out[1:2, cd:2 * cd] = lg[...]
        out[2:3, 0:cd] = lb[...]
        out[2:3, cd:2 * cd] = bo[...]
        out[8:8 + CONV_WIDTH, 0:cd] = wd[...]

    def whole(arr):
        return pl.BlockSpec(arr.shape, lambda i, p: (0,) * arr.ndim)

    ins = [b_in, w_dw, b_dw, ln_g, ln_b, b_out]
    return pl.pallas_call(
        body, name="pack_small",
        grid_spec=pltpu.PrefetchScalarGridSpec(
            num_scalar_prefetch=1, grid=(1,), in_specs=[whole(a) for a in ins],
            out_specs=pl.BlockSpec((None, rows, cin), lambda i, p: (p[1], 0, 0))),
        out_shape=jax.ShapeDtypeStruct((N_SHARD, rows, cin), F32),
        compiler_params=_params("arbitrary"),
    )(place, *ins)


def _place():
    x, y, c = lax.axis_index("x"), lax.axis_index("y"), lax.axis_index("c")
    return x, y, c


def _other_chips(x, y):
    return [(1 - x, y), (x, 1 - y), (1 - x, 1 - y)]


def _split_start(name, bufs, n_sem, copies, after=None):
    n = len(bufs)
    deps = [] if after is None else [after]

    def body(*refs):
        out0 = n + len(deps)
        for cp in copies(refs[:n], refs[out0], refs[out0 + 1], False):
            cp.start()
        refs[-1][...] = jnp.zeros_like(refs[-1])

    res = pl.pallas_call(
        body, name=name,
        out_shape=(pltpu.SemaphoreType.DMA((n_sem,)), pltpu.SemaphoreType.DMA((n_sem,)),
                   *[pltpu.HBM(b.shape, b.dtype) for b in bufs], jax.ShapeDtypeStruct((8, LANES), F32)),
        in_specs=[HBM] * n + [ANY] * len(deps),
        out_specs=(SEM, SEM, *[HBM] * n, pl.BlockSpec(memory_space=pltpu.VMEM)),
        input_output_aliases={i: 2 + i for i in range(n)},
        compiler_params=pltpu.CompilerParams(has_side_effects=SPLIT_EFFECT),
    )(*[pltpu.with_memory_space_constraint(b, pltpu.HBM) for b in bufs], *deps)
    return res[0], res[1], list(res[2:2 + n]), res[-1]


def _split_wait(name, handle, copies, after):
    ssem, rsem, bufs, _ = handle
    n = len(bufs)
    deps = list(after) if isinstance(after, (list, tuple)) else [after]

    def body(*refs):
        for cp in copies(refs[:n], refs[n], refs[n + 1], True):
            cp.wait_send()
            cp.wait_recv()

    res = pl.pallas_call(
        body, name=name,
        out_shape=[pltpu.HBM(b.shape, b.dtype) for b in bufs],
        in_specs=[HBM] * n + [SEM, SEM] + [ANY] * len(deps), out_specs=[HBM] * n,
        input_output_aliases={i: i for i in range(n)},
        compiler_params=pltpu.CompilerParams(has_side_effects=SPLIT_EFFECT),
    )(*bufs, ssem, rsem, *deps)
    return list(res)


def _remote(src, dst, ssem, rsem, k, to):
    return pltpu.make_async_remote_copy(src_ref=src, dst_ref=dst, send_sem=ssem.at[k], recv_sem=rsem.at[k],
                                        device_id=to, device_id_type=MESH)


def _gather_copies(refs, ssem, rsem, landing, n_whole=0):
    x, y, c = _place()
    me = 2 * x + y
    cps = []
    for a, ref in enumerate(refs):
        whole = a >= len(refs) - n_whole
        rh = ref.shape[1] // 2
        for j, (px, py) in enumerate(_other_chips(x, y)):
            shard = 2 * px + py if landing else me
            src = ref.at[me] if whole else ref.at[me, pl.ds(c * rh, rh)]
            dst = ref.at[shard] if whole else ref.at[shard, pl.ds(c * rh, rh)]
            cps.append(_remote(src, dst, ssem, rsem, 3 * a + j, (px, py, c)))
    return cps


def _forward_copies(refs, ssem, rsem, landing):
    x, y, c = _place()
    who = 1 - c if landing else c
    cps = []
    for a, ref in enumerate(refs):
        rh = ref.shape[1] // 2
        for j, (px, py) in enumerate(_other_chips(x, y)):
            piece = ref.at[2 * px + py, pl.ds(who * rh, rh)]
            cps.append(_remote(piece, piece, ssem, rsem, 3 * a + j, (x, y, 1 - c)))
    return cps


def _sibling_copies(refs, ssem, rsem, landing):
    x, y, c = _place()
    n = len(refs) // 2
    cps = []
    for a in range(n):
        rh = refs[a].shape[1] // 2
        cps.append(_remote(refs[a].at[:, pl.ds((1 - c) * rh, rh), :], refs[n + a], ssem, rsem, a, (x, y, 1 - c)))
    return cps


def _owner_copies(refs, ssem, rsem, landing):
    x, y, c = _place()
    n = len(refs) // 2
    cps = []
    for a in range(n):
        for j, (px, py) in enumerate(_other_chips(x, y)):
            cps.append(_remote(refs[a].at[2 * px + py], refs[n + a].at[j], ssem, rsem, 3 * a + j, (px, py, c)))
    return cps


def _swap_copies(refs, ssem, rsem, landing):
    x, y, c = _place()
    who = 1 - c if landing else c
    cps = []
    for a, ref in enumerate(refs):
        rh = ref.shape[0] // 2
        rows = ref.at[pl.ds(who * rh, rh)]
        cps.append(_remote(rows, rows, ssem, rsem, a, (x, y, 1 - c)))
    return cps


def _small_copies(refs, ssem, rsem, landing):
    pack, slots = refs
    x, y, c = _place()
    cps = []
    for rel in range(1, N_DEV):
        px = 1 - x if (rel >> 2) & 1 else x
        py = 1 - y if (rel >> 1) & 1 else y
        pc = 1 - c if rel & 1 else c
        slot = 4 * px + 2 * py + pc if landing else 4 * x + 2 * y + c
        cps.append(_remote(pack, slots.at[slot], ssem, rsem, rel - 1, (px, py, pc)))
    return cps


def _small_pack(rows, w_dw_grad, d):
    n = len(rows)

    def body(*refs):
        pack = refs[-1]
        pack[...] = jnp.zeros_like(pack)
        for (r, _), ref in zip(rows, refs[:n]):
            pack[r:r + 1, :] = ref[...]
        pack[16:16 + CONV_PAD, :] = refs[n][...]

    return pl.pallas_call(body, name="small_pack", out_shape=jax.ShapeDtypeStruct((SMALL_ROWS, d), F32))(
        *[v for _, v in rows], w_dw_grad)


def _small_sum(pack, slots, place):
    rows, d = pack.shape
    loss_row = 12

    def body(pl_ref, pack_ref, slots_ref, out_ref):
        me = pl_ref[2]
        tot = jnp.where(me == 0, pack_ref[...], slots_ref[0])
        for i in range(1, N_DEV):
            tot = tot + jnp.where(me == i, pack_ref[...], slots_ref[i])
        out_ref[...] = tot
        out_ref[loss_row:loss_row + 1, :] = jnp.zeros((1, d), F32) + jnp.sum(tot[loss_row:loss_row + 1, :])

    return pl.pallas_call(
        body, name="small_sum",
        grid_spec=pltpu.PrefetchScalarGridSpec(
            num_scalar_prefetch=1, grid=(1,),
            in_specs=[pl.BlockSpec((rows, d), lambda i, p: (0, 0)), pl.BlockSpec((N_DEV, rows, d), lambda i, p: (0, 0, 0))],
            out_specs=pl.BlockSpec((rows, d), lambda i, p: (0, 0))),
        out_shape=jax.ShapeDtypeStruct((rows, d), F32),
        compiler_params=_params("arbitrary"),
    )(place, pack, slots)


def kernel(x, norm_mix, norm_mlp, conv_w_in, conv_b_in, conv_w_dw, conv_b_dw, conv_ln_g, conv_ln_b, conv_w_out, conv_b_out, kv_norm, w_kv, attn_w_q, attn_w_o, mlp_w_in, mlp_w_out, final_norm, loss_target, m_norm_mix, m_norm_mlp, m_conv_w_in, m_conv_b_in, m_conv_w_dw, m_conv_b_dw, m_conv_ln_g, m_conv_ln_b, m_conv_w_out, m_conv_b_out, m_kv_norm, m_w_kv, m_attn_w_q, m_attn_w_o, m_mlp_w_in, m_mlp_w_out, m_final_norm, v_norm_mix, v_norm_mlp, v_conv_w_in, v_conv_b_in, v_conv_w_dw, v_conv_b_dw, v_conv_ln_g, v_conv_ln_b, v_conv_w_out, v_conv_b_out, v_kv_norm, v_w_kv, v_attn_w_q, v_attn_w_o, v_mlp_w_in, v_mlp_w_out, v_final_norm):
    _, s, d = x.shape
    dff = mlp_w_in.shape[2] * N_SHARD
    kvw = w_kv.shape[1]
    nh = d // HEAD_DIM
    group = nh // N_KV_HEADS
    ds4 = d // N_SHARD
    xi, yi, ci = _place()
    me = 2 * xi + yi
    place = jnp.stack([ci, me, 2 * me + ci]).astype(I32)

    h0 = x.reshape(s, d)
    target = loss_target.reshape(s, d)
    tabs = _rope_tables(s)

    def gather_begin(tag, bufs, n_whole=0):
        return _split_start(f"gather_start_{tag}", bufs, 3 * len(bufs),
                            functools.partial(_gather_copies, n_whole=n_whole)), n_whole

    def gather_land(tag, begun, later):
        handle, n_whole = begun
        bufs = _split_wait(f"gather_wait_{tag}", handle, functools.partial(_gather_copies, n_whole=n_whole), later)
        n_half = len(bufs) - n_whole
        fwd = _split_start(f"forward_start_{tag}", bufs[:n_half], 3 * n_half, _forward_copies)
        return fwd, bufs[n_half:]

    def gather_end(tag, landed, later):
        fwd, whole = landed
        return _split_wait(f"forward_wait_{tag}", fwd, _forward_copies, later) + whole

    ag_cin = gather_begin("conv_in", [
        _cast_bf16("cast_w_in", conv_w_in, 0, place),
        _pack_small(conv_b_in, conv_w_dw.reshape(CONV_WIDTH, ds4), conv_b_dw, conv_ln_g, conv_ln_b, conv_b_out, place),
    ], n_whole=1)
    ag_cout = gather_begin("conv_out", [_cast_bf16("cast_w_out", conv_w_out, 0, place)])
    ag_mi0 = gather_begin("mlp_in0", [_cast_bf16("cast_mlp_in0", mlp_w_in, 0, place)])
    ag_mo0 = gather_begin("mlp_out0", [_cast_bf16("cast_mlp_out0", mlp_w_out, 0, place)])
    ag_attn = gather_begin("attn", [
        _cast_bf16("cast_w_kv", w_kv.reshape(1, ds4, kvw), 0, place), _cast_bf16("cast_w_q", attn_w_q, 0, place),
        _cast_bf16("cast_w_o", attn_w_o, 0, place)])
    ag_mlp1 = gather_begin("mlp1", [
        _cast_bf16("cast_mlp_in1", mlp_w_in, 1, place), _cast_bf16("cast_mlp_out1", mlp_w_out, 1, place)])

    wmi_g = [None, None]
    wmo_f = [None, None]

    nm = [norm_mix[0:1], norm_mix[1:2]]
    nmlp = [norm_mlp[0:1], norm_mlp[1:2]]
    kvn = kv_norm.reshape(1, d)
    fin = final_norm.reshape(1, d)
    started = sum(h[0][3][0:1, 0:1] for h in (ag_cin, ag_cout, ag_mi0, ag_mo0, ag_attn, ag_mlp1))
    (y0,) = _rms_fwd("rms_mix0", h0, [nm[0] + started])

    w_in_g, small_g = gather_end("conv_in", gather_land("conv_in", ag_cin, y0), y0)
    b_in_f = small_g[:, 0, :].reshape(1, 2 * d)
    b_dw_f = small_g[:, 1, 0:ds4].reshape(1, d)
    ln_g_f = small_g[:, 1, ds4:2 * ds4].reshape(1, d)
    ln_b_f = small_g[:, 2, 0:ds4].reshape(1, d)
    b_out_f = small_g[:, 2, ds4:2 * ds4].reshape(1, d)
    w_dw_f = jnp.transpose(small_g[:, 8:8 + CONV_PAD, 0:ds4], (1, 0, 2)).reshape(CONV_PAD, d)

    def ep_bias(acc, ex, outs, j):
        outs[0][...] = (acc + ex[0][...]).astype(outs[0].dtype)

    def ep_residual(acc, ex, outs, j):
        outs[0][...] = ex[0][...] + acc

    def ep_residual_bias(acc, ex, outs, j):
        outs[0][...] = ex[0][...] + (acc + ex[1][...])

    def ep_relu2(acc, ex, outs, j):
        r = jnp.maximum(acc, 0.0)
        outs[0][...] = r.astype(BF16)
        outs[1][...] = (r * r).astype(BF16)

    by_residue = [(BF16, ("residues", dil)) for dil in DILATIONS]

    def put_by_residue(val, outs, stage):
        _to_residues(val, stage, outs, DILATIONS)

    def ep_rope(acc, ex, outs, j, stage):
        put_by_residue(_rope_apply(acc, ex[0][...], ex[1][...], ex[2][...], 1.0), outs, stage)

    def ep_rope_k(acc, ex, outs, j, stage):
        roped = _rope_apply(acc, ex[0][...], ex[1][...], ex[2][...], 1.0)
        put_by_residue(jnp.where(j == 0, roped, acc), outs, stage)

    def ep_by_residue(acc, ex, outs, j, stage):
        put_by_residue(acc, outs, stage)

    tab_extras = [(t, "rows") for t in tabs]

    def mlp_fwd(idx, h, y, out_weight):
        r, r2 = _matmul(f"mlp_in{idx}", "nn", y, wmi_g[idx], b_kind="col", m=s, n=dff, k=d,
                        outs=[(BF16, "plain"), (BF16, "plain")], epilogue=ep_relu2)
        wmo_f[idx] = out_weight(r2).reshape(dff, d)
        (h_new,) = _matmul(f"mlp_out{idx}", "nn", r2, wmo_f[idx], m=s, n=d, k=dff,
                           outs=[(F32, "plain")], extras=[(h, "ij")], epilogue=ep_residual)
        return h_new, r, r2

    (u,) = _matmul("conv_in", "nn", y0, w_in_g, b_kind="col", m=s, n=2 * d, k=d,
                   outs=[(BF16, "plain")], extras=[(b_in_f, "vec")], epilogue=ep_bias)
    land_cout = gather_land("conv_out", ag_cout, u)
    cpre = _dwconv_fwd(u, w_dw_f, b_dw_f + land_cout[0][3][0:1, 0:1])
    sact = _ln_silu_fwd(cpre, ln_g_f, ln_b_f)
    (w_out_g,) = gather_end("conv_out", land_cout, sact)
    w_out_f = w_out_g.reshape(d, d)
    (h1,) = _matmul("conv_out", "nn", sact, w_out_f, m=s, n=d, k=d,
                    outs=[(F32, "plain")], extras=[(h0, "ij"), (b_out_f, "vec")], epilogue=ep_residual_bias)
    (y1,) = _rms_fwd("rms_mlp0", h1, [nmlp[0]])
    (wmi_g[0],) = gather_end("mlp_in0", gather_land("mlp_in0", ag_mi0, y1), y1)
    h2, r0, r0sq = mlp_fwd(0, h1, y1, lambda r2: gather_end("mlp_out0", gather_land("mlp_out0", ag_mo0, r2), r2)[0])
    land_attn = gather_land("attn", ag_attn, h2)
    ykv, y2 = _rms_fwd("rms_kv_mix1", h2, [kvn + land_attn[0][3][0:1, 0:1], nm[1]])
    wkv_g, wq_g, wo_g = gather_end("attn", land_attn, y2)
    wkv_f, wq_f, wo_f = wkv_g.reshape(d, kvw), wq_g.reshape(d, d), wo_g.reshape(d, d)
    kv_parts = _matmul("kv_proj", "nn", ykv, wkv_f, m=s, n=kvw, k=d, tn=kvw // 2,
                       outs=by_residue, extras=tab_extras, epilogue=ep_rope_k, stage=True)
    q_parts = _matmul("q_proj", "nn", y2, wq_f, m=s, n=d, k=d,
                      outs=by_residue, extras=tab_extras, epilogue=ep_rope, stage=True)
    o_parts, lse_parts = [], []
    for dil, q_b, kv_b in zip(DILATIONS, q_parts, kv_parts):
        o_b, lse_b = _attn_fwd(f"attn_fwd_d{dil}", q_b, kv_b)
        o_parts.append(o_b)
        lse_parts.append(lse_b)
    o, lse = _attn_combine(o_parts, lse_parts)
    land_mlp1 = gather_land("mlp1", ag_mlp1, o)
    (h3,) = _matmul("attn_out", "nn", o, wo_f, m=s, n=d, k=d,
                    outs=[(F32, "plain")], extras=[(h2, "ij")], epilogue=ep_residual)
    (y3,) = _rms_fwd("rms_mlp1", h3, [nmlp[1]])
    wmi_g[1], wmo1_g = gather_end("mlp1", land_mlp1, y3)
    h4, r1, r1sq = mlp_fwd(1, h3, y3, lambda r2: wmo1_g)
    dh4, dh4b, d_fin, loss_cols = _final_loss(h4, fin, target)

    def ep_relu2_bwd(acc, ex, outs, j):
        outs[0][...] = (acc * (2.0 * ex[0][...].astype(F32))).astype(BF16)

    def mlp_bwd(idx, dhb, y, r, r2):
        (dz,) = _matmul(f"mlp_out{idx}_dx", "nt", dhb, wmo_f[idx], m=s, n=dff, k=d,
                        outs=[(BF16, "plain")], extras=[(r, "ij")], epilogue=ep_relu2_bwd)
        (dwo,) = _matmul(f"mlp_out{idx}_dw", "tn", r2, dhb, m=dff, n=d, k=s,
                         outs=[(BF16, "plain")])
        (dy,) = _matmul(f"mlp_in{idx}_dx", "nt", dz, wmi_g[idx], b_kind="col", m=s, n=d, k=dff,
                        outs=[(BF16, "plain")])
        (dwi,) = _matmul(f"mlp_in{idx}_dw", "tn", y, dz, m=d, n=dff, k=s,
                         outs=[(BF16, "col")])
        return dy, dwi, dwo.reshape(N_SHARD, dff // N_SHARD, d)

    def token(handle):
        return handle[3][0:1, 0:1]

    def rs_exchange(tag, grads):
        lands = [lax.empty((N_SHARD, g.shape[1] // 2, g.shape[2]), g.dtype) for g in grads]
        return _split_start(f"sibling_start_{tag}", list(grads) + lands, len(grads), _sibling_copies)

    def rs_send(tag, names, exchanged, later):
        bufs = _split_wait(f"sibling_wait_{tag}", exchanged, _sibling_copies, later)
        n = len(names)
        sums = [_chip_sum(f"chip_sum_{nme}", g, rh, place) for nme, g, rh in zip(names, bufs[:n], bufs[n:])]
        lands = [lax.empty((N_SHARD - 1,) + cs.shape[1:], cs.dtype) for cs in sums]
        return _split_start(f"owners_start_{tag}", sums + lands, 3 * n, _owner_copies)

    def rs_sum(tag, names, sent, later):
        bufs = _split_wait(f"owners_wait_{tag}", sent, _owner_copies, later)
        n = len(names)
        own = [_owner_sum(f"owner_sum_{nme}", cs, rp, place) for nme, cs, rp in zip(names, bufs[:n], bufs[n:])]
        return _split_start(f"swap_start_{tag}", own, n, _swap_copies)

    def rs_end(tag, swapped, later):
        return _split_wait(f"swap_wait_{tag}", swapped, _swap_copies, later)

    dy3, g_wmi1, g_wmo1 = mlp_bwd(1, dh4b, y3, r1, r1sq)
    x_mlp1 = rs_exchange("mlp1", [g_wmi1, g_wmo1])
    dh3, dh3b, d_nmlp1 = _rms_bwd("rms_mlp1_bwd", h3, [(nmlp[1] + token(x_mlp1), dy3)], dh4)

    do_parts = _matmul("attn_out_dx", "nt", dh3b, wo_f, m=s, n=d, k=d, outs=by_residue, epilogue=ep_by_residue,
                       stage=True)
    (g_wo,) = _matmul("attn_out_dw", "tn", o, dh3b, m=d, n=d, k=s, outs=[(BF16, "plain")])
    rs_mlp1 = rs_send("mlp1", ["mlp_in1", "mlp_out1"], x_mlp1, g_wo)
    lse_res, delta_res = _attn_delta(do_parts[0].reshape(s, d), o, lse, DILATIONS)
    dq_parts, dk_parts, dv_parts = [], [], []
    for dil, q_b, kv_b, do_b, lse_b, dl_b in zip(DILATIONS, q_parts, kv_parts, do_parts, lse_res, delta_res):
        dq_b, dk_b, dv_b = _attn_bwd(f"attn_bwd_d{dil}", q_b, kv_b, do_b, lse_b, dl_b)
        dq_parts.append(dq_b)
        dk_parts.append(dk_b)
        dv_parts.append(dv_b)
    dq = _residue_sum("rope_bwd_q", [(dq_parts, True)], tabs)
    dkv = _residue_sum("rope_bwd_kv", [(dk_parts, True), (dv_parts, False)], tabs)
    (g_wq,) = _matmul("q_proj_dw", "tn", y2, dq, m=d, n=d, k=s, outs=[(BF16, "plain")])
    (dy2,) = _matmul("q_proj_dx", "nt", dq, wq_f, m=s, n=d, k=d, outs=[(BF16, "plain")])
    (g_wkv,) = _matmul("kv_proj_dw", "tn", ykv, dkv, m=d, n=kvw, k=s, outs=[(BF16, "plain")])
    (dykv,) = _matmul("kv_proj_dx", "nt", dkv, wkv_f, m=s, n=d, k=kvw, outs=[(BF16, "plain")])
    x_attn = rs_exchange("attn", [g_wkv.reshape(N_SHARD, ds4, kvw), g_wq.reshape(N_SHARD, ds4, d),
                                  g_wo.reshape(N_SHARD, ds4, d)])
    dh2, dh2b, d_nm1, d_kvn = _rms_bwd("rms_kv_mix1_bwd", h2, [(nm[1] + token(x_attn), dy2), (kvn, dykv)], dh3)
    rs_attn = rs_send("attn", ["w_kv", "w_q", "w_o"], x_attn, dh2b)

    dy1, g_wmi0, g_wmo0 = mlp_bwd(0, dh2b, y1, r0, r0sq)
    x_mlp0 = rs_exchange("mlp0", [g_wmi0, g_wmo0])
    dh1, dh1b, d_nmlp0, d_b_out = _rms_bwd("rms_mlp0_bwd", h1, [(nmlp[0] + token(x_mlp0) + token(rs_attn), dy1)],
                                           dh2, want_colsum=True)

    (dsact,) = _matmul("conv_out_dx", "nt", dh1b, w_out_f, m=s, n=d, k=d, outs=[(BF16, "plain")])
    (g_wout,) = _matmul("conv_out_dw", "tn", sact, dh1b, m=d, n=d, k=s, outs=[(BF16, "plain")])
    rs_mlp0 = rs_send("mlp0", ["mlp_in0", "mlp_out0"], x_mlp0, g_wout)
    dc, d_ln_g, d_ln_b, d_b_dw = _ln_silu_bwd(cpre, ln_g_f + token(rs_mlp0), ln_b_f, dsact)
    du, d_w_dw, d_b_in_a, d_b_in_g = _dwconv_bwd(u, w_dw_f, dc)
    (g_win,) = _matmul("conv_in_dw", "tn", y0, du, b_kind="col", m=d, n=2 * d, k=s, outs=[(BF16, "col")])
    x_conv = rs_exchange("conv", [g_win, g_wout.reshape(N_SHARD, ds4, d)])
    (dy0,) = _matmul("conv_in_dx", "nt", du, w_in_g, a_kind="col", b_kind="col", m=s, n=d, k=2 * d,
                     outs=[(BF16, "plain")])
    dx, _, d_nm0 = _rms_bwd("rms_mix0_bwd", h0, [(nm[0] + token(x_conv), dy0)], dh1)

    small_rows = [(0, d_nm0), (1, d_nm1), (2, d_nmlp0), (3, d_nmlp1), (4, d_kvn), (5, d_fin), (6, d_b_dw),
                  (7, d_ln_g), (8, d_ln_b), (9, d_b_out), (10, d_b_in_a), (11, d_b_in_g), (12, loss_cols)]
    x_small = _split_start("small_start", [_small_pack(small_rows, d_w_dw, d),
                                           lax.empty((N_DEV, SMALL_ROWS, d), F32)], N_DEV - 1, _small_copies)
    rs_conv = rs_send("conv", ["w_in", "w_out"], x_conv, x_small[3])

    def big(name, w, m, v, g, layer=0, partial=None):
        shape = w.shape
        w3, m3, v3 = [t.reshape((-1,) + shape[-2:]) for t in (w, m, v)]
        if partial is not None:
            partial = [t.reshape(w3.shape) for t in partial]
        res = _adamw(name, w3, m3, v3, g, layer, partial)
        return [t.reshape(shape) for t in res]

    sw_mlp1 = rs_sum("mlp1", ["mlp_in1", "mlp_out1"], rs_mlp1, rs_conv[3])
    sw_attn = rs_sum("attn", ["w_kv", "w_q", "w_o"], rs_attn, sw_mlp1[3])
    f_wmi1, f_wmo1 = rs_end("mlp1", sw_mlp1, sw_attn[3])
    p_wmi = big("adam_mlp_in1", mlp_w_in, m_mlp_w_in, v_mlp_w_in, f_wmi1, 1)
    p_wmo = big("adam_mlp_out1", mlp_w_out, m_mlp_w_out, v_mlp_w_out, f_wmo1, 1)
    sw_mlp0 = rs_sum("mlp0", ["mlp_in0", "mlp_out0"], rs_mlp0, [p_wmi[0], p_wmo[0]])
    f_wkv, f_wq, f_wo = rs_end("attn", sw_attn, sw_mlp0[3])
    r_wkv = big("adam_w_kv", w_kv, m_w_kv, v_w_kv, f_wkv)
    r_wq = big("adam_w_q", attn_w_q, m_attn_w_q, v_attn_w_q, f_wq)
    r_wo = big("adam_w_o", attn_w_o, m_attn_w_o, v_attn_w_o, f_wo)
    sw_conv = rs_sum("conv", ["w_in", "w_out"], rs_conv, [r_wkv[0], r_wq[0], r_wo[0]])
    f_wmi0, f_wmo0 = rs_end("mlp0", sw_mlp0, sw_conv[3])
    r_wmi = big("adam_mlp_in0", mlp_w_in, m_mlp_w_in, v_mlp_w_in, f_wmi0, 0, p_wmi)
    r_wmo = big("adam_mlp_out0", mlp_w_out, m_mlp_w_out, v_mlp_w_out, f_wmo0, 0, p_wmo)
    f_win, f_wout = rs_end("conv", sw_conv, [r_wmi[0], r_wmo[0]])
    r_win = big("adam_w_in", conv_w_in, m_conv_w_in, v_conv_w_in, f_win)
    r_wout = big("adam_w_out", conv_w_out, m_conv_w_out, v_conv_w_out, f_wout)

    small_pack, small_slots = _split_wait("small_wait", x_small, _small_copies, r_wout[0])
    red = _small_sum(small_pack, small_slots, place)
    loss = red[12, 0]
    g_norm_mix = red[0:2]
    g_norm_mlp = red[2:4]
    g_kv_norm = red[4:5]
    g_final = red[5:6]

    def my_cols(row):
        return lax.dynamic_slice(red, (row, me * ds4), (1, ds4))

    g_b_dw, g_ln_g, g_ln_b, g_b_out = my_cols(6), my_cols(7), my_cols(8), my_cols(9)
    half_in = 2 * d // N_SHARD
    b_in_row = 10 + me // 2
    g_b_in = lax.dynamic_slice(red, (b_in_row, (me % 2) * half_in), (1, half_in))
    g_w_dw = lax.dynamic_slice(red, (16, me * ds4), (CONV_WIDTH, ds4))

    sm_w =[norm_mix, norm_mlp, conv_b_in, conv_w_dw.reshape(CONV_WIDTH, ds4), conv_b_dw, conv_ln_g, conv_ln_b,
            conv_b_out, kv_norm.reshape(1, d), final_norm.reshape(1, d)]
    sm_m = [m_norm_mix, m_norm_mlp, m_conv_b_in, m_conv_w_dw.reshape(CONV_WIDTH, ds4), m_conv_b_dw, m_conv_ln_g,
            m_conv_ln_b, m_conv_b_out, m_kv_norm.reshape(1, d), m_final_norm.reshape(1, d)]
    sm_v = [v_norm_mix, v_norm_mlp, v_conv_b_in, v_conv_w_dw.reshape(CONV_WIDTH, ds4), v_conv_b_dw, v_conv_ln_g,
            v_conv_ln_b, v_conv_b_out, v_kv_norm.reshape(1, d), v_final_norm.reshape(1, d)]
    sm_g = [g_norm_mix, g_norm_mlp, g_b_in, g_w_dw, g_b_dw, g_ln_g, g_ln_b, g_b_out, g_kv_norm, g_final]
    sm_d, sm_nm, sm_nv = _adam_small(sm_w, sm_m, sm_v, sm_g)
    shapes = [norm_mix.shape, norm_mlp.shape, conv_b_in.shape, conv_w_dw.shape, conv_b_dw.shape, conv_ln_g.shape,
              conv_ln_b.shape, conv_b_out.shape, kv_norm.shape, final_norm.shape]
    sm_g, sm_d, sm_nm, sm_nv = [[t.reshape(sh) for t, sh in zip(lst, shapes)] for lst in (sm_g, sm_d, sm_nm, sm_nv)]

    def order(sm, idx):
        return [sm[0], sm[1], r_win[idx], sm[2], sm[3], sm[4], sm[5], sm[6], r_wout[idx], sm[7], sm[8],
                r_wkv[idx], r_wq[idx], r_wo[idx], r_wmi[idx], r_wmo[idx], sm[9]]

    return (loss, dx.reshape(x.shape), *order(sm_g, 0), *order(sm_d, 1), *order(sm_nm, 2), *order(sm_nv, 3))
```

```python
import functools
import math

import jax
import jax.numpy as jnp
from jax import lax
from jax.experimental import pallas as pl
from jax.experimental.pallas import tpu as pltpu

F32 = jnp.float32
BF16 = jnp.bfloat16
I32 = jnp.int32

NORM_EPS = 1e-6
LN_EPS = 1e-5
HEAD_DIM = 128
N_KV_HEADS = 4
ROT_DIM = 32
ROPE_THETA = 500000.0
CONV_WIDTH = 31
CONV_PAD = 32
ATT_BLOCK = 128
ATT_STEP_BLOCKS = 8
DILATIONS = (1, 4, 16)
ADAM_LR = 0.001
ADAM_B1 = 0.9
ADAM_B2 = 0.999
ADAM_EPS = 1e-08
ADAM_WD = 0.01
ADAM_STEP = 10
N_SHARD = 4
N_DEV = 8
LANES = 128
VMEM_LIMIT = 48 * 1024 * 1024
MM_TM, MM_TN, MM_TK = 1024, 1024, 2048
ROW_TILE = 256
CONV_CB = 128
CONV_T = 128
SMALL_ROWS = 48
MESH = pl.DeviceIdType.MESH
ANY = pl.BlockSpec(memory_space=pl.ANY)
HBM = pl.BlockSpec(memory_space=pltpu.HBM)
SEM = pl.BlockSpec(memory_space=pltpu.SEMAPHORE)
SPLIT_EFFECT = pltpu.SideEffectType.DATAFLOW_SIDE_EFFECTING


def _params(*sem):
    return pltpu.CompilerParams(dimension_semantics=sem, vmem_limit_bytes=VMEM_LIMIT)


def _sigmoid(x):
    return 1.0 / (1.0 + jnp.exp(-x))


def _wspec(kind, arr_shape, br, bc, pick):
    if kind == "plain":
        return pl.BlockSpec((br, bc), pick)
    per = arr_shape[2] // bc

    def idx(*g):
        rb, cb = pick(*g)
        return (cb // per, rb, cb % per)

    return pl.BlockSpec((None, br, bc), idx)


def _stage_shape(rows, w):
    return (w // LANES, rows, LANES)


def _to_residues(val, stage_ref, out_refs, dils):
    planes, rows, _ = stage_ref.shape
    for c in range(planes):
        stage_ref[c] = val[:, c * LANES:(c + 1) * LANES]
    for out_ref, dil in zip(out_refs, dils):
        if dil == 1:
            out_ref[0] = val.astype(out_ref.dtype)
            continue
        for r in range(dil):
            for c in range(planes):
                out_ref[r, :, c * LANES:(c + 1) * LANES] = stage_ref.at[c][pl.ds(r, rows // dil, stride=dil), :].astype(
                    out_ref.dtype)


def _from_residues(src_ref, stage_ref, dil):
    planes, rows, _ = stage_ref.shape
    if dil == 1:
        return lambda c: src_ref[0, :, c * LANES:(c + 1) * LANES].astype(F32)
    for r in range(dil):
        for c in range(planes):
            stage_ref.at[c][pl.ds(r, rows // dil, stride=dil), :] = src_ref[r, :, c * LANES:(c + 1) * LANES].astype(F32)
    return lambda c: stage_ref[c]


def _matmul(name, mode, a, b, *, m, n, k, tn=MM_TN, a_kind="plain", b_kind="plain", outs, extras=(), epilogue=None,
            stage=False):
    tm, tn, tk = min(MM_TM, m), min(tn, n), min(MM_TK, k)
    if b_kind == "col" and mode in ("nn", "tn"):
        tn = min(tn, n // b.shape[0])
    if b_kind == "col" and mode == "nt":
        tk = min(tk, k // b.shape[0])
    if a_kind == "col":
        assert mode == "nt"
        tk = min(tk, k // a.shape[0])
    if any(kind == "col" for _, kind in outs):
        tn = min(tn, n // N_SHARD)
    assert m % tm == 0 and n % tn == 0 and k % tk == 0, (name, m, n, k, tm, tn, tk)
    nk = k // tk
    grid = (m // tm, n // tn, nk)
    if mode == "nn":
        a_spec = pl.BlockSpec((tm, tk), lambda i, j, kk: (i, kk))
        b_spec = _wspec(b_kind, b.shape, tk, tn, lambda i, j, kk: (kk, j))
        dims = (((1,), (0,)), ((), ()))
    elif mode == "nt":
        a_spec = _wspec(a_kind, a.shape, tm, tk, lambda i, j, kk: (i, kk))
        b_spec = _wspec(b_kind, b.shape, tn, tk, lambda i, j, kk: (j, kk))
        dims = (((1,), (1,)), ((), ()))
    else:
        a_spec = pl.BlockSpec((tk, tm), lambda i, j, kk: (kk, i))
        b_spec = _wspec(b_kind, b.shape, tk, tn, lambda i, j, kk: (kk, j))
        dims = (((0,), (0,)), ((), ()))
    out_shape, out_specs = [], []
    for dtype, kind in outs:
        if isinstance(kind, tuple):
            dil = kind[1]
            out_shape.append(jax.ShapeDtypeStruct((dil, m // dil, n), dtype))
            out_specs.append(pl.BlockSpec((dil, tm // dil, tn), lambda i, j, kk: (0, i, j)))
            continue
        shape = (m, n) if kind == "plain" else (N_SHARD, m, n // N_SHARD)
        out_shape.append(jax.ShapeDtypeStruct(shape, dtype))
        out_specs.append(_wspec(kind, shape, tm, tn, lambda i, j, kk: (i, j)))
    n_ex = len(extras)
    ex_specs = {"ij": pl.BlockSpec((tm, tn), lambda i, j, kk: (i, j)),
                "vec": pl.BlockSpec((1, tn), lambda i, j, kk: (0, j)),
                "rows": pl.BlockSpec((tm, LANES), lambda i, j, kk: (i, 0))}

    def body(*refs):
        a_ref, b_ref = refs[0], refs[1]
        ex_refs = refs[2:2 + n_ex]
        out_refs = refs[2 + n_ex:2 + n_ex + len(outs)]
        j = pl.program_id(1)

        def finish(res):
            if epilogue is None:
                out_refs[0][...] = res.astype(out_refs[0].dtype)
            elif stage:
                epilogue(res, ex_refs, out_refs, j, refs[-1])
            else:
                epilogue(res, ex_refs, out_refs, j)

        prod = lax.dot_general(a_ref[...], b_ref[...], dims, preferred_element_type=F32)
        if nk == 1:
            finish(prod)
            return
        acc_ref = refs[2 + n_ex + len(outs)]
        kk = pl.program_id(2)

        @pl.when(kk == 0)
        def _():
            acc_ref[...] = prod

        @pl.when(kk > 0)
        def _():
            acc_ref[...] += prod

        @pl.when(kk == nk - 1)
        def _():
            finish(acc_ref[...])

    res = pl.pallas_call(
        body, name=name, grid=grid,
        in_specs=[a_spec, b_spec] + [ex_specs[how] for _, how in extras],
        out_specs=out_specs, out_shape=out_shape,
        scratch_shapes=[pltpu.VMEM((tm, tn), F32)] * (nk > 1) + [pltpu.VMEM(_stage_shape(tm, tn), F32)] * bool(stage),
        compiler_params=_params("parallel", "parallel", "arbitrary"),
    )(a, b, *[e for e, _ in extras])
    return res


def _rope_tables(seq):
    half = ROT_DIM // 2
    pos = jnp.arange(seq, dtype=F32)
    inv = ROPE_THETA ** (-jnp.arange(0, ROT_DIM, 2, dtype=F32) / ROT_DIM)
    ang = pos[:, None] * inv[None, :]
    cos, sin = jnp.cos(ang), jnp.sin(ang)
    zeros = jnp.zeros((seq, HEAD_DIM - ROT_DIM), F32)
    ctab = jnp.concatenate([cos, cos, zeros + 1.0], axis=1)
    atab = jnp.concatenate([-sin, jnp.zeros((seq, half), F32), zeros], axis=1)
    btab = jnp.concatenate([jnp.zeros((seq, half), F32), sin, zeros], axis=1)
    return ctab, atab, btab


def _rope_apply(x, ctab, atab, btab, sign):
    w = x.shape[1]
    reps = w // HEAD_DIM
    half = ROT_DIM // 2
    c = jnp.tile(ctab, (1, reps))
    a = jnp.tile(atab, (1, reps))
    b = jnp.tile(btab, (1, reps))
    up = pltpu.roll(x, w - half, 1)
    down = pltpu.roll(x, half, 1)
    return x * c + sign * (up * a + down * b)


def _rows(t, w):
    return pl.BlockSpec((t, w), lambda i: (i, 0))


def _fixed(shape):
    nd = len(shape)
    return pl.BlockSpec(shape, lambda i: (0,) * nd)


def _rms_fwd(name, x, gains):
    s, d = x.shape
    t = min(ROW_TILE, s)
    ng = len(gains)

    def body(x_ref, *refs):
        xv = x_ref[...]
        r = lax.rsqrt(jnp.mean(xv * xv, axis=-1, keepdims=True) + NORM_EPS)
        xn = xv * r
        for g_ref, y_ref in zip(refs[:ng], refs[ng:]):
            y_ref[...] = (xn * g_ref[...]).astype(BF16)

    return pl.pallas_call(
        body, name=name, grid=(s // t,),
        in_specs=[_rows(t, d)] + [_fixed((1, d))] * ng,
        out_specs=[_rows(t, d)] * ng,
        out_shape=[jax.ShapeDtypeStruct((s, d), BF16)] * ng,
        compiler_params=_params("parallel"),
    )(x, *gains)


def _rms_bwd(name, x, pairs, dh_in, want_colsum=False):
    s, d = x.shape
    t = min(ROW_TILE, s)
    n_p = len(pairs)

    def body(x_ref, dh_ref, *refs):
        g_refs = refs[:n_p]
        dy_refs = refs[n_p:2 * n_p]
        dh_out, dhb_out = refs[2 * n_p], refs[2 * n_p + 1]
        dg_refs = refs[2 * n_p + 2:2 * n_p + 2 + n_p]
        cs_ref = refs[-1] if want_colsum else None
        i = pl.program_id(0)
        xv = x_ref[...]
        r = lax.rsqrt(jnp.mean(xv * xv, axis=-1, keepdims=True) + NORM_EPS)
        xn = xv * r
        dh = dh_ref[...]
        for g_ref, dy_ref, dg_ref in zip(g_refs, dy_refs, dg_refs):
            dy = dy_ref[...].astype(F32)
            u = dy * g_ref[...]
            dh = dh + r * (u - xn * jnp.mean(u * xn, axis=-1, keepdims=True))
            part = jnp.sum(dy * xn, axis=0, keepdims=True)

            @pl.when(i == 0)
            def _():
                dg_ref[...] = part

            @pl.when(i > 0)
            def _():
                dg_ref[...] += part

        dh_out[...] = dh
        dhb_out[...] = dh.astype(BF16)
        if want_colsum:
            col = jnp.sum(dh, axis=0, keepdims=True)

            @pl.when(i == 0)
            def _():
                cs_ref[...] = col

            @pl.when(i > 0)
            def _():
                cs_ref[...] += col

    n_vec = n_p + (1 if want_colsum else 0)
    return pl.pallas_call(
        body, name=name, grid=(s // t,),
        in_specs=[_rows(t, d), _rows(t, d)] + [_fixed((1, d))] * n_p + [_rows(t, d)] * n_p,
        out_specs=[_rows(t, d), _rows(t, d)] + [_fixed((1, d))] * n_vec,
        out_shape=[jax.ShapeDtypeStruct((s, d), F32), jax.ShapeDtypeStruct((s, d), BF16)]
        + [jax.ShapeDtypeStruct((1, d), F32)] * n_vec,
        compiler_params=_params("arbitrary"),
    )(x, dh_in, *[g for g, _ in pairs], *[dy for _, dy in pairs])


def _final_loss(x, g, target):
    s, d = x.shape
    t = min(ROW_TILE, s)

    def body(x_ref, g_ref, t_ref, dh_out, dhb_out, dg_ref, loss_ref):
        i = pl.program_id(0)
        xv = x_ref[...]
        gv = g_ref[...]
        r = lax.rsqrt(jnp.mean(xv * xv, axis=-1, keepdims=True) + NORM_EPS)
        xn = xv * r
        diff = xn * gv - t_ref[...]
        dy = diff / d
        u = dy * gv
        dh = r * (u - xn * jnp.mean(u * xn, axis=-1, keepdims=True))
        dh_out[...] = dh
        dhb_out[...] = dh.astype(BF16)
        dg = jnp.sum(dy * xn, axis=0, keepdims=True)
        lc = jnp.sum(0.5 * diff * dy, axis=0, keepdims=True)

        @pl.when(i == 0)
        def _():
            dg_ref[...] = dg
            loss_ref[...] = lc

        @pl.when(i > 0)
        def _():
            dg_ref[...] += dg
            loss_ref[...] += lc

    return pl.pallas_call(
        body, name="final_loss", grid=(s // t,),
        in_specs=[_rows(t, d), _fixed((1, d)), _rows(t, d)],
        out_specs=[_rows(t, d), _rows(t, d), _fixed((1, d)), _fixed((1, d))],
        out_shape=[jax.ShapeDtypeStruct((s, d), F32), jax.ShapeDtypeStruct((s, d), BF16),
                   jax.ShapeDtypeStruct((1, d), F32), jax.ShapeDtypeStruct((1, d), F32)],
        compiler_params=_params("arbitrary"),
    )(x, g, target)


def _ln_silu_fwd(c, g, b):
    s, d = c.shape
    t = min(ROW_TILE, s)

    def body(c_ref, g_ref, b_ref, s_ref):
        cv = c_ref[...]
        mu = jnp.mean(cv, axis=-1, keepdims=True)
        xc = cv - mu
        rs = lax.rsqrt(jnp.mean(xc * xc, axis=-1, keepdims=True) + LN_EPS)
        ln = xc * rs * g_ref[...] + b_ref[...]
        s_ref[...] = (ln * _sigmoid(ln)).astype(BF16)

    return pl.pallas_call(
        body, name="ln_silu_fwd", grid=(s // t,),
        in_specs=[_rows(t, d), _fixed((1, d)), _fixed((1, d))],
        out_specs=_rows(t, d), out_shape=jax.ShapeDtypeStruct((s, d), BF16),
        compiler_params=_params("parallel"),
    )(c, g, b)


def _ln_silu_bwd(c, g, b, ds):
    s, d = c.shape
    t = min(ROW_TILE, s)

    def body(c_ref, g_ref, b_ref, ds_ref, dc_ref, dg_ref, db_ref, dbdw_ref):
        i = pl.program_id(0)
        cv = c_ref[...]
        gv = g_ref[...]
        mu = jnp.mean(cv, axis=-1, keepdims=True)
        xc = cv - mu
        rs = lax.rsqrt(jnp.mean(xc * xc, axis=-1, keepdims=True) + LN_EPS)
        nrm = xc * rs
        ln = nrm * gv + b_ref[...]
        sig = _sigmoid(ln)
        dln = ds_ref[...].astype(F32) * sig * (1.0 + ln * (1.0 - sig))
        dn = dln * gv
        dc = rs * (dn - jnp.mean(dn, axis=-1, keepdims=True)
                   - nrm * jnp.mean(dn * nrm, axis=-1, keepdims=True))
        dc_ref[...] = dc
        pg = jnp.sum(dln * nrm, axis=0, keepdims=True)
        pb = jnp.sum(dln, axis=0, keepdims=True)
        pc = jnp.sum(dc, axis=0, keepdims=True)

        @pl.when(i == 0)
        def _():
            dg_ref[...] = pg
            db_ref[...] = pb
            dbdw_ref[...] = pc

        @pl.when(i > 0)
        def _():
            dg_ref[...] += pg
            db_ref[...] += pb
            dbdw_ref[...] += pc

    return pl.pallas_call(
        body, name="ln_silu_bwd", grid=(s // t,),
        in_specs=[_rows(t, d), _fixed((1, d)), _fixed((1, d)), _rows(t, d)],
        out_specs=[_rows(t, d)] + [_fixed((1, d))] * 3,
        out_shape=[jax.ShapeDtypeStruct((s, d), F32)] + [jax.ShapeDtypeStruct((1, d), F32)] * 3,
        compiler_params=_params("arbitrary"),
    )(c, g, b, ds)


def _residue_spec(dil, t, w):
    return pl.BlockSpec((dil, t // dil, w), lambda i: (0, i, 0))


def _attn_combine(o_list, lse_list):
    dil0, sd0, d = o_list[0].shape
    s = dil0 * sd0
    lw = lse_list[0].shape[2]
    group = d // HEAD_DIM // N_KV_HEADS
    t = min(ROW_TILE, s)
    nb = len(o_list)
    dils = [o.shape[0] for o in o_list]

    def body(*refs):
        o_out, l_out = refs[2 * nb], refs[2 * nb + 1]
        o_stage, l_stage = refs[2 * nb + 2:3 * nb + 2], refs[3 * nb + 2:]
        o_planes = [_from_residues(src, stage, dil) for src, stage, dil in zip(refs[:nb], o_stage, dils)]
        l_planes = [_from_residues(src, stage, dil) for src, stage, dil in zip(refs[nb:2 * nb], l_stage, dils)]
        for kh in range(N_KV_HEADS):
            ls = [plane(kh) for plane in l_planes]
            mx = ls[0]
            for l in ls[1:]:
                mx = jnp.maximum(mx, l)
            es = [jnp.exp(l - mx) for l in ls]
            den = es[0]
            for e in es[1:]:
                den = den + e
            l_out[:, kh * LANES:(kh + 1) * LANES] = mx + jnp.log(den)
            ws = [e / den for e in es]
            for g in range(group):
                h = kh * group + g
                acc = jnp.zeros((t, HEAD_DIM), F32)
                for plane, w in zip(o_planes, ws):
                    acc = acc + w[:, g:g + 1] * plane(h)
                o_out[:, h * HEAD_DIM:(h + 1) * HEAD_DIM] = acc.astype(BF16)

    return pl.pallas_call(
        body, name="attn_combine", grid=(s // t,),
        in_specs=[_residue_spec(dil, t, d) for dil in dils] + [_residue_spec(dil, t, lw) for dil in dils],
        out_specs=[_rows(t, d), _rows(t, lw)],
        out_shape=[jax.ShapeDtypeStruct((s, d), BF16), jax.ShapeDtypeStruct((s, lw), F32)],
        scratch_shapes=[pltpu.VMEM(_stage_shape(t, d), F32)] * nb + [pltpu.VMEM(_stage_shape(t, lw), F32)] * nb,
        compiler_params=_params("parallel"),
    )(*o_list, *lse_list)


def _attn_delta(do, o, lse, dils):
    s, d = o.shape
    lw = lse.shape[1]
    group = d // HEAD_DIM // N_KV_HEADS
    t = min(ROW_TILE, s)
    nd = len(dils)

    def body(do_ref, o_ref, lse_ref, *refs):
        stage = refs[-1]
        lane = lax.broadcasted_iota(I32, (t, LANES), 1)
        planes = []
        for kh in range(N_KV_HEADS):
            out = jnp.zeros((t, LANES), F32)
            for g in range(group):
                cols = slice((kh * group + g) * HEAD_DIM, (kh * group + g + 1) * HEAD_DIM)
                v = jnp.sum(do_ref[:, cols].astype(F32) * o_ref[:, cols].astype(F32), axis=-1, keepdims=True)
                out = jnp.where(lane == g, v, out)
            planes.append(out)
        _to_residues(lse_ref[...], stage, refs[:nd], dils)
        _to_residues(jnp.concatenate(planes, axis=1), stage, refs[nd:2 * nd], dils)

    res = pl.pallas_call(
        body, name="attn_delta", grid=(s // t,),
        in_specs=[_rows(t, d), _rows(t, d), _rows(t, lw)],
        out_specs=[_residue_spec(dil, t, lw) for dil in dils] * 2,
        out_shape=[jax.ShapeDtypeStruct((dil, s // dil, lw), F32) for dil in dils] * 2,
        scratch_shapes=[pltpu.VMEM(_stage_shape(t, lw), F32)],
        compiler_params=_params("parallel"),
    )(do, o, lse)
    return res[:nd], res[nd:]


def _residue_sum(name, groups, tabs):
    first = groups[0][0][0]
    s, w = first.shape[0] * first.shape[1], first.shape[2]
    t = min(ROW_TILE, s)
    flat = [p for parts, _ in groups for p in parts]

    def body(*refs):
        c_ref, a_ref, b_ref = refs[len(flat):len(flat) + 3]
        out = refs[len(flat) + 3]
        stages = refs[len(flat) + 4:]
        k = 0
        for gi, (parts, rotate) in enumerate(groups):
            planes = [_from_residues(refs[k + i], stages[k + i], p.shape[0]) for i, p in enumerate(parts)]
            k += len(parts)
            for c in range(w // LANES):
                tot = planes[0](c)
                for plane in planes[1:]:
                    tot = tot + plane(c)
                if rotate:
                    tot = _rope_apply(tot, c_ref[...], a_ref[...], b_ref[...], -1.0)
                out[:, gi * w + c * LANES:gi * w + (c + 1) * LANES] = tot.astype(BF16)

    return pl.pallas_call(
        body, name=name, grid=(s // t,),
        in_specs=[_residue_spec(p.shape[0], t, w) for p in flat] + [_rows(t, HEAD_DIM)] * 3,
        out_specs=_rows(t, len(groups) * w), out_shape=jax.ShapeDtypeStruct((s, len(groups) * w), BF16),
        scratch_shapes=[pltpu.VMEM(_stage_shape(t, w), F32) for _ in flat],
        compiler_params=_params("parallel"),
    )(*flat, *tabs)


def _dwconv_fwd(u, w_dw, b_dw):
    s, d2 = u.shape
    d = d2 // 2
    cb = min(CONV_CB, d)
    nblk = d // cb
    tt = min(CONV_T, s)

    def body(ua_ref, ug_ref, w_ref, b_ref, c_ref, xp_ref):
        gl = ua_ref[...].astype(F32) * _sigmoid(ug_ref[...].astype(F32))
        xp_ref[0:CONV_PAD, :] = jnp.zeros((CONV_PAD, cb), F32)
        xp_ref[CONV_PAD:, :] = gl
        wv = w_ref[...]
        bv = b_ref[...]
        for t0 in range(0, s, tt):
            acc = jnp.zeros((tt, cb), F32) + bv
            for kk in range(CONV_WIDTH):
                off = t0 + CONV_PAD - (CONV_WIDTH - 1) + kk
                acc = acc + wv[kk:kk + 1, :] * xp_ref[off:off + tt, :]
            c_ref[t0:t0 + tt, :] = acc

    return pl.pallas_call(
        body, name="dwconv_fwd", grid=(nblk,),
        in_specs=[pl.BlockSpec((s, cb), lambda j: (0, j)), pl.BlockSpec((s, cb), lambda j: (0, j + nblk)),
                  pl.BlockSpec((CONV_PAD, cb), lambda j: (0, j)), pl.BlockSpec((1, cb), lambda j: (0, j))],
        out_specs=pl.BlockSpec((s, cb), lambda j: (0, j)),
        out_shape=jax.ShapeDtypeStruct((s, d), F32),
        scratch_shapes=[pltpu.VMEM((s + CONV_PAD, cb), F32)],
        compiler_params=_params("parallel"),
    )(u, u, w_dw, b_dw)


def _dwconv_bwd(u, w_dw, dc):
    s, d2 = u.shape
    d = d2 // 2
    cb = min(CONV_CB, d)
    nblk = d // cb
    tt = min(CONV_T, s)

    def body(ua_ref, ug_ref, w_ref, dc_ref, du_ref, dw_ref, dba_ref, dbg_ref, glp_ref, dcp_ref, acc_ref):
        a = ua_ref[...].astype(F32)
        sig = _sigmoid(ug_ref[...].astype(F32))
        glp_ref[0:CONV_PAD, :] = jnp.zeros((CONV_PAD, cb), F32)
        glp_ref[CONV_PAD:, :] = a * sig
        dcp_ref[0:s, :] = dc_ref[...]
        dcp_ref[s:, :] = jnp.zeros((CONV_PAD, cb), F32)
        acc_ref[...] = jnp.zeros_like(acc_ref)
        wv = w_ref[...]
        dba = jnp.zeros((1, cb), F32)
        dbg = jnp.zeros((1, cb), F32)
        for t0 in range(0, s, tt):
            dgl = jnp.zeros((tt, cb), F32)
            dct = dc_ref[t0:t0 + tt, :]
            for kk in range(CONV_WIDTH):
                off = t0 + (CONV_WIDTH - 1) - kk
                dgl = dgl + wv[kk:kk + 1, :] * dcp_ref[off:off + tt, :]
                goff = t0 + CONV_PAD - (CONV_WIDTH - 1) + kk
                prod = dct * glp_ref[goff:goff + tt, :]
                acc_ref[8 * kk:8 * kk + 8, :] += jnp.sum(prod.reshape(tt // 8, 8, cb), axis=0)
            at = ua_ref[t0:t0 + tt, :].astype(F32)
            st = _sigmoid(ug_ref[t0:t0 + tt, :].astype(F32))
            da = dgl * st
            dg = dgl * at * st * (1.0 - st)
            du_ref[0, t0:t0 + tt, :] = da.astype(BF16)
            du_ref[1, t0:t0 + tt, :] = dg.astype(BF16)
            dba = dba + jnp.sum(da, axis=0, keepdims=True)
            dbg = dbg + jnp.sum(dg, axis=0, keepdims=True)
        dba_ref[...] = dba
        dbg_ref[...] = dbg
        for kk in range(CONV_WIDTH):
            dw_ref[kk:kk + 1, :] = jnp.sum(acc_ref[8 * kk:8 * kk + 8, :], axis=0, keepdims=True)
        dw_ref[CONV_WIDTH:, :] = jnp.zeros((CONV_PAD - CONV_WIDTH, cb), F32)

    blk = pl.BlockSpec((s, cb), lambda j: (0, j))
    vec = pl.BlockSpec((1, cb), lambda j: (0, j))
    return pl.pallas_call(
        body, name="dwconv_bwd", grid=(nblk,),
        in_specs=[blk, pl.BlockSpec((s, cb), lambda j: (0, j + nblk)),
                  pl.BlockSpec((CONV_PAD, cb), lambda j: (0, j)), blk],
        out_specs=[pl.BlockSpec((2, s, cb), lambda j: (0, 0, j)), pl.BlockSpec((CONV_PAD, cb), lambda j: (0, j)),
                   vec, vec],
        out_shape=[jax.ShapeDtypeStruct((2, s, d), BF16), jax.ShapeDtypeStruct((CONV_PAD, d), F32),
                   jax.ShapeDtypeStruct((1, d), F32), jax.ShapeDtypeStruct((1, d), F32)],
        scratch_shapes=[pltpu.VMEM((s + CONV_PAD, cb), F32), pltpu.VMEM((s + CONV_PAD, cb), F32),
                        pltpu.VMEM((8 * CONV_PAD, cb), F32)],
        compiler_params=_params("parallel"),
    )(u, u, w_dw, dc)


def _stack_heads(x, group):
    return jnp.concatenate([x[:, g * HEAD_DIM:(g + 1) * HEAD_DIM] for g in range(group)], axis=0)


def _unstack_heads(x, group):
    return jnp.concatenate([x[g * ATT_BLOCK:(g + 1) * ATT_BLOCK, :] for g in range(group)], axis=1)


def _stack_cols(x, group):
    return jnp.concatenate([x[:, g:g + 1] for g in range(group)], axis=0)


def _band_mask(nb, group):
    rows = group * ATT_BLOCK
    row = lax.broadcasted_iota(I32, (rows, 2 * ATT_BLOCK), 0) % ATT_BLOCK
    col = lax.broadcasted_iota(I32, (rows, 2 * ATT_BLOCK), 1)
    return (col >= row) & (col <= row + ATT_BLOCK) & ((col >= ATT_BLOCK) | (nb > 0))


def _window(ref, nb):
    prev = pl.multiple_of(jnp.maximum(nb - 1, 0) * ATT_BLOCK, ATT_BLOCK)
    cur = pl.multiple_of(nb * ATT_BLOCK, ATT_BLOCK)
    return jnp.concatenate([ref[pl.ds(prev, ATT_BLOCK), :], ref[pl.ds(cur, ATT_BLOCK), :]], axis=0)


def _residues_per_step(dil, nblk):
    return max(1, min(dil, ATT_STEP_BLOCKS // nblk))


def _attn_fwd(name, q, kv):
    dil, sd, d = q.shape
    group = d // HEAD_DIM // N_KV_HEADS
    gw = group * HEAD_DIM
    nblk = sd // ATT_BLOCK
    scale = 1.0 / math.sqrt(HEAD_DIM)
    nt = (((1,), (1,)), ((), ()))

    rb = _residues_per_step(dil, nblk)

    def body(q_all, k_all, v_all, o_all, lse_all):
        lane = lax.broadcasted_iota(I32, (ATT_BLOCK, LANES), 1)
        for rr in range(rb):
            q_ref, k_ref, v_ref, o_ref, lse_ref = [ref.at[rr] for ref in (q_all, k_all, v_all, o_all, lse_all)]

            def step(nb, carry):
                rows = pl.ds(pl.multiple_of(nb * ATT_BLOCK, ATT_BLOCK), ATT_BLOCK)
                qs = _stack_heads(q_ref[rows, :], group)
                kw = _window(k_ref, nb)
                vw = _window(v_ref, nb)
                sc = lax.dot_general(qs, kw, nt, preferred_element_type=F32) * scale
                sc = jnp.where(_band_mask(nb, group), sc, -jnp.inf)
                mx = jnp.max(sc, axis=-1, keepdims=True)
                p = jnp.exp(sc - mx)
                l = jnp.sum(p, axis=-1, keepdims=True)
                o = jnp.dot(p.astype(BF16), vw, preferred_element_type=F32) / l
                o_ref[rows, :] = _unstack_heads(o, group).astype(BF16)
                lse = mx + jnp.log(l)
                out = jnp.zeros((ATT_BLOCK, LANES), F32)
                for g in range(group):
                    out = jnp.where(lane == g, lse[g * ATT_BLOCK:(g + 1) * ATT_BLOCK, :], out)
                lse_ref[rows, :] = out
                return carry

            lax.fori_loop(0, nblk, step, 0, unroll=min(2, nblk))

    kvh = N_KV_HEADS
    qspec = pl.BlockSpec((rb, sd, gw), lambda r, h: (r, 0, h))
    kspec = pl.BlockSpec((rb, sd, HEAD_DIM), lambda r, h: (r, 0, h))
    return pl.pallas_call(
        body, name=name, grid=(dil // rb, kvh),
        in_specs=[qspec, kspec, pl.BlockSpec((rb, sd, HEAD_DIM), lambda r, h: (r, 0, kvh + h))],
        out_specs=[qspec, kspec],
        out_shape=[jax.ShapeDtypeStruct((dil, sd, d), BF16),
                   jax.ShapeDtypeStruct((dil, sd, kvh * LANES), F32)],
        compiler_params=_params("parallel", "parallel"),
    )(q, kv, kv)


def _attn_bwd(name, q, kv, do, lse, delta):
    dil, sd, d = q.shape
    group = d // HEAD_DIM // N_KV_HEADS
    gw = group * HEAD_DIM
    nblk = sd // ATT_BLOCK
    scale = 1.0 / math.sqrt(HEAD_DIM)
    nt = (((1,), (1,)), ((), ()))
    tn = (((0,), (0,)), ((), ()))

    rb = _residues_per_step(dil, nblk)

    def body(q_all, k_all, v_all, do_all, lse_all, dl_all, dq_all, dk_all, dv_all, dk_accs, dv_accs):
        dk_accs[...] = jnp.zeros_like(dk_accs)
        dv_accs[...] = jnp.zeros_like(dv_accs)
        for rr in range(rb):
            q_ref, k_ref, v_ref, do_ref, lse_ref, dl_ref, dq_ref, dk_ref, dv_ref, dk_acc, dv_acc = [
                ref.at[rr] for ref in (q_all, k_all, v_all, do_all, lse_all, dl_all, dq_all, dk_all, dv_all,
                                       dk_accs, dv_accs)]

            def step(nb, carry):
                rows = pl.ds(pl.multiple_of(nb * ATT_BLOCK, ATT_BLOCK), ATT_BLOCK)
                qs = _stack_heads(q_ref[rows, :], group)
                dos = _stack_heads(do_ref[rows, :], group)
                ls = _stack_cols(lse_ref[rows, :], group)
                dl = _stack_cols(dl_ref[rows, :], group)
                kw = _window(k_ref, nb)
                vw = _window(v_ref, nb)
                sc = lax.dot_general(qs, kw, nt, preferred_element_type=F32) * scale
                sc = jnp.where(_band_mask(nb, group), sc, -jnp.inf)
                p = jnp.exp(sc - ls)
                dp = lax.dot_general(dos, vw, nt, preferred_element_type=F32)
                ds = (p * (dp - dl) * scale).astype(BF16)
                dq = jnp.dot(ds, kw, preferred_element_type=F32)
                dq_ref[rows, :] = _unstack_heads(dq, group).astype(BF16)
                win = pl.ds(pl.multiple_of(nb * ATT_BLOCK, ATT_BLOCK), 2 * ATT_BLOCK)
                dk_acc[win, :] += lax.dot_general(ds, qs, tn, preferred_element_type=F32)
                dv_acc[win, :] += lax.dot_general(p.astype(BF16), dos, tn, preferred_element_type=F32)
                return carry

            lax.fori_loop(0, nblk, step, 0, unroll=min(2, nblk))
            dk_ref[...] = dk_acc[ATT_BLOCK:, :]
            dv_ref[...] = dv_acc[ATT_BLOCK:, :]

    kvh = N_KV_HEADS
    qspec = pl.BlockSpec((rb, sd, gw), lambda r, h: (r, 0, h))
    kspec = pl.BlockSpec((rb, sd, HEAD_DIM), lambda r, h: (r, 0, h))
    return pl.pallas_call(
        body, name=name, grid=(dil // rb, kvh),
        in_specs=[qspec, kspec, pl.BlockSpec((rb, sd, HEAD_DIM), lambda r, h: (r, 0, kvh + h)),
                  qspec, kspec, kspec],
        out_specs=[qspec, kspec, kspec],
        out_shape=[jax.ShapeDtypeStruct((dil, sd, d), BF16),
                   jax.ShapeDtypeStruct((dil, sd, kvh * HEAD_DIM), F32),
                   jax.ShapeDtypeStruct((dil, sd, kvh * HEAD_DIM), F32)],
        scratch_shapes=[pltpu.VMEM((rb, sd + ATT_BLOCK, HEAD_DIM), F32)] * 2,
        compiler_params=_params("parallel", "parallel"),
    )(q, kv, kv, do, lse, delta)


def _cast_bf16(name, w, layer, place):
    _, r, c = w.shape
    tr = min(512, r)

    def body(pl_ref, w_ref, o_ref):
        o_ref[...] = w_ref[...].astype(BF16)

    return pl.pallas_call(
        body, name=name,
        grid_spec=pltpu.PrefetchScalarGridSpec(
            num_scalar_prefetch=1, grid=(r // tr,),
            in_specs=[pl.BlockSpec((None, tr, c), lambda i, p: (layer, i, 0))],
            out_specs=pl.BlockSpec((None, tr, c), lambda i, p: (p[1], i, 0))),
        out_shape=jax.ShapeDtypeStruct((N_SHARD, r, c), BF16),
        compiler_params=_params("parallel"),
    )(place, w)


def _chip_sum(name, g, rh, place):
    _, r, c = g.shape
    rh2 = r // 2
    tr = min(512, rh2)
    nb = rh2 // tr

    def body(pl_ref, g_ref, rh_ref, o_ref):
        o_ref[...] = (g_ref[...].astype(F32) + rh_ref[...].astype(F32)).astype(BF16)

    return pl.pallas_call(
        body, name=name,
        grid_spec=pltpu.PrefetchScalarGridSpec(
            num_scalar_prefetch=1, grid=(N_SHARD, nb),
            in_specs=[pl.BlockSpec((None, tr, c), lambda s, i, p: (s, p[0] * nb + i, 0)),
                      pl.BlockSpec((None, tr, c), lambda s, i, p: (s, i, 0))],
            out_specs=pl.BlockSpec((None, tr, c), lambda s, i, p: (s, i, 0))),
        out_shape=jax.ShapeDtypeStruct((N_SHARD, rh2, c), BF16),
        compiler_params=_params("parallel", "parallel"),
    )(place, g, rh)


def _owner_sum(name, cs, rp, place):
    _, rh2, c = cs.shape
    tr = min(512, rh2)
    nb = rh2 // tr

    def body(pl_ref, cs_ref, r0_ref, r1_ref, r2_ref, o_ref):
        o_ref[...] = ((cs_ref[...].astype(F32) + r0_ref[...].astype(F32))
                      + (r1_ref[...].astype(F32) + r2_ref[...].astype(F32)))

    def rspec(j):
        return pl.BlockSpec((None, tr, c), lambda i, p: (j, i, 0))

    return pl.pallas_call(
        body, name=name,
        grid_spec=pltpu.PrefetchScalarGridSpec(
            num_scalar_prefetch=1, grid=(nb,),
            in_specs=[pl.BlockSpec((None, tr, c), lambda i, p: (p[1], i, 0)), rspec(0), rspec(1), rspec(2)],
            out_specs=pl.BlockSpec((tr, c), lambda i, p: (p[0] * nb + i, 0))),
        out_shape=jax.ShapeDtypeStruct((2 * rh2, c), F32),
        compiler_params=_params("parallel"),
    )(place, cs, rp, rp, rp)


def _adam_math(w, g, m, v):
    m = ADAM_B1 * m + (1.0 - ADAM_B1) * g
    v = ADAM_B2 * v + (1.0 - ADAM_B2) * (g * g)
    m_hat = m / (1.0 - ADAM_B1 ** ADAM_STEP)
    v_hat = v / (1.0 - ADAM_B2 ** ADAM_STEP)
    delta = -ADAM_LR * (m_hat / (jnp.sqrt(v_hat) + ADAM_EPS) + ADAM_WD * w)
    return delta, m, v


def _adamw(name, w, m, v, g, layer, partial=None):
    nl, r, c = w.shape
    tr = min(256, r)

    def body(w_ref, m_ref, v_ref, g_ref, *refs):
        go_ref, d_ref, mo_ref, vo_ref = refs[-4:]
        gv = g_ref[...]
        delta, m_new, v_new = _adam_math(w_ref[...], gv, m_ref[...], v_ref[...])
        go_ref[...] = gv
        d_ref[...] = delta
        mo_ref[...] = m_new
        vo_ref[...] = v_new

    wspec = pl.BlockSpec((None, tr, c), lambda i: (layer, i, 0))
    prev = [] if partial is None else list(partial)
    return pl.pallas_call(
        body, name=name, grid=(r // tr,),
        in_specs=[wspec] * 3 + [pl.BlockSpec((tr, c), lambda i: (i, 0))] + [ANY] * len(prev),
        out_specs=[wspec] * 4,
        out_shape=[jax.ShapeDtypeStruct((nl, r, c), F32)] * 4,
        input_output_aliases={4 + i: i for i in range(len(prev))},
        compiler_params=_params("parallel"),
    )(w, m, v, g, *prev)


def _adam_small(ws, ms, vs, gs):
    n = len(ws)

    def body(*refs):
        w_refs, m_refs, v_refs, g_refs = refs[:n], refs[n:2 * n], refs[2 * n:3 * n], refs[3 * n:4 * n]
        d_refs, mo_refs, vo_refs = refs[4 * n:5 * n], refs[5 * n:6 * n], refs[6 * n:7 * n]
        for i in range(n):
            delta, m_new, v_new = _adam_math(w_refs[i][...], g_refs[i][...], m_refs[i][...], v_refs[i][...])
            d_refs[i][...] = delta
            mo_refs[i][...] = m_new
            vo_refs[i][...] = v_new

    shapes = [jax.ShapeDtypeStruct(w.shape, F32) for w in ws]
    res = pl.pallas_call(body, name="adam_small", out_shape=shapes * 3)(*ws, *ms, *vs, *gs)
    return res[:n], res[n:2 * n], res[2 * n:]


def _pack_small(b_in, w_dw, b_dw, ln_g, ln_b, b_out, place):
    cin = b_in.shape[1]
    cd = b_dw.shape[1]
    rows = 8 + CONV_PAD

    def body(pl_ref, bi, wd, bd, lg, lb, bo, out):
        out[...] = jnp.zeros_like(out)
        out[0:1, :] = bi[...]
        out[1:2, 0:cd] = bd[...]
        out[1:2, cd:2 * cd] = lg[...]
        out[2:3, 0:cd] = lb[...]
        out[2:3, cd:2 * cd] = bo[...]
        out[8:8 + CONV_WIDTH, 0:cd] = wd[...]

    def whole(arr):
        return pl.BlockSpec(arr.shape, lambda i, p: (0,) * arr.ndim)

    ins = [b_in, w_dw, b_dw, ln_g, ln_b, b_out]
    return pl.pallas_call(
        body, name="pack_small",
        grid_spec=pltpu.PrefetchScalarGridSpec(
            num_scalar_prefetch=1, grid=(1,), in_specs=[whole(a) for a in ins],
            out_specs=pl.BlockSpec((None, rows, cin), lambda i, p: (p[1], 0, 0))),
        out_shape=jax.ShapeDtypeStruct((N_SHARD, rows, cin), F32),
        compiler_params=_params("arbitrary"),
    )(place, *ins)


def _place():
    x, y, c = lax.axis_index("x"), lax.axis_index("y"), lax.axis_index("c")
    return x, y, c


def _other_chips(x, y):
    return [(1 - x, y), (x, 1 - y), (1 - x, 1 - y)]


def _split_start(name, bufs, n_sem, copies, after=None):
    n = len(bufs)
    deps = [] if after is None else [after]

    def body(*refs):
        out0 = n + len(deps)
        for cp in copies(refs[:n], refs[out0], refs[out0 + 1], False):
            cp.start()
        refs[-1][...] = jnp.zeros_like(refs[-1])

    res = pl.pallas_call(
        body, name=name,
        out_shape=(pltpu.SemaphoreType.DMA((n_sem,)), pltpu.SemaphoreType.DMA((n_sem,)),
                   *[pltpu.HBM(b.shape, b.dtype) for b in bufs], jax.ShapeDtypeStruct((8, LANES), F32)),
        in_specs=[HBM] * n + [ANY] * len(deps),
        out_specs=(SEM, SEM, *[HBM] * n, pl.BlockSpec(memory_space=pltpu.VMEM)),
        input_output_aliases={i: 2 + i for i in range(n)},
        compiler_params=pltpu.CompilerParams(has_side_effects=SPLIT_EFFECT),
    )(*[pltpu.with_memory_space_constraint(b, pltpu.HBM) for b in bufs], *deps)
    return res[0], res[1], list(res[2:2 + n]), res[-1]


def _split_wait(name, handle, copies, after):
    ssem, rsem, bufs, _ = handle
    n = len(bufs)
    deps = list(after) if isinstance(after, (list, tuple)) else [after]

    def body(*refs):
        for cp in copies(refs[:n], refs[n], refs[n + 1], True):
            cp.wait_send()
            cp.wait_recv()

    res = pl.pallas_call(
        body, name=name,
        out_shape=[pltpu.HBM(b.shape, b.dtype) for b in bufs],
        in_specs=[HBM] * n + [SEM, SEM] + [ANY] * len(deps), out_specs=[HBM] * n,
        input_output_aliases={i: i for i in range(n)},
        compiler_params=pltpu.CompilerParams(has_side_effects=SPLIT_EFFECT),
    )(*bufs, ssem, rsem, *deps)
    return list(res)


def _remote(src, dst, ssem, rsem, k, to):
    return pltpu.make_async_remote_copy(src_ref=src, dst_ref=dst, send_sem=ssem.at[k], recv_sem=rsem.at[k],
                                        device_id=to, device_id_type=MESH)


def _gather_copies(refs, ssem, rsem, landing, n_whole=0):
    x, y, c = _place()
    me = 2 * x + y
    cps = []
    for a, ref in enumerate(refs):
        whole = a >= len(refs) - n_whole
        rh = ref.shape[1] // 2
        for j, (px, py) in enumerate(_other_chips(x, y)):
            shard = 2 * px + py if landing else me
            src = ref.at[me] if whole else ref.at[me, pl.ds(c * rh, rh)]
            dst = ref.at[shard] if whole else ref.at[shard, pl.ds(c * rh, rh)]
            cps.append(_remote(src, dst, ssem, rsem, 3 * a + j, (px, py, c)))
    return cps


def _forward_copies(refs, ssem, rsem, landing):
    x, y, c = _place()
    who = 1 - c if landing else c
    cps = []
    for a, ref in enumerate(refs):
        rh = ref.shape[1] // 2
        for j, (px, py) in enumerate(_other_chips(x, y)):
            piece = ref.at[2 * px + py, pl.ds(who * rh, rh)]
            cps.append(_remote(piece, piece, ssem, rsem, 3 * a + j, (x, y, 1 - c)))
    return cps


def _sibling_copies(refs, ssem, rsem, landing):
    x, y, c = _place()
    n = len(refs) // 2
    cps = []
    for a in range(n):
        rh = refs[a].shape[1] // 2
        cps.append(_remote(refs[a].at[:, pl.ds((1 - c) * rh, rh), :], refs[n + a], ssem, rsem, a, (x, y, 1 - c)))
    return cps


def _owner_copies(refs, ssem, rsem, landing):
    x, y, c = _place()
    n = len(refs) // 2
    cps = []
    for a in range(n):
        for j, (px, py) in enumerate(_other_chips(x, y)):
            cps.append(_remote(refs[a].at[2 * px + py], refs[n + a].at[j], ssem, rsem, 3 * a + j, (px, py, c)))
    return cps


def _swap_copies(refs, ssem, rsem, landing):
    x, y, c = _place()
    who = 1 - c if landing else c
    cps = []
    for a, ref in enumerate(refs):
        rh = ref.shape[0] // 2
        rows = ref.at[pl.ds(who * rh, rh)]
        cps.append(_remote(rows, rows, ssem, rsem, a, (x, y, 1 - c)))
    return cps


def _small_copies(refs, ssem, rsem, landing):
    pack, slots = refs
    x, y, c = _place()
    cps = []
    for rel in range(1, N_DEV):
        px = 1 - x if (rel >> 2) & 1 else x
        py = 1 - y if (rel >> 1) & 1 else y
        pc = 1 - c if rel & 1 else c
        slot = 4 * px + 2 * py + pc if landing else 4 * x + 2 * y + c
        cps.append(_remote(pack, slots.at[slot], ssem, rsem, rel - 1, (px, py, pc)))
    return cps


def _small_pack(rows, w_dw_grad, d):
    n = len(rows)

    def body(*refs):
        pack = refs[-1]
        pack[...] = jnp.zeros_like(pack)
        for (r, _), ref in zip(rows, refs[:n]):
            pack[r:r + 1, :] = ref[...]
        pack[16:16 + CONV_PAD, :] = refs[n][...]

    return pl.pallas_call(body, name="small_pack", out_shape=jax.ShapeDtypeStruct((SMALL_ROWS, d), F32))(
        *[v for _, v in rows], w_dw_grad)


def _small_sum(pack, slots, place):
    rows, d = pack.shape
    loss_row = 12

    def body(pl_ref, pack_ref, slots_ref, out_ref):
        me = pl_ref[2]
        tot = jnp.where(me == 0, pack_ref[...], slots_ref[0])
        for i in range(1, N_DEV):
            tot = tot + jnp.where(me == i, pack_ref[...], slots_ref[i])
        out_ref[...] = tot
        out_ref[loss_row:loss_row + 1, :] = jnp.zeros((1, d), F32) + jnp.sum(tot[loss_row:loss_row + 1, :])

    return pl.pallas_call(
        body, name="small_sum",
        grid_spec=pltpu.PrefetchScalarGridSpec(
            num_scalar_prefetch=1, grid=(1,),
            in_specs=[pl.BlockSpec((rows, d), lambda i, p: (0, 0)), pl.BlockSpec((N_DEV, rows, d), lambda i, p: (0, 0, 0))],
            out_specs=pl.BlockSpec((rows, d), lambda i, p: (0, 0))),
        out_shape=jax.ShapeDtypeStruct((rows, d), F32),
        compiler_params=_params("arbitrary"),
    )(place, pack, slots)


def kernel(x, norm_mix, norm_mlp, conv_w_in, conv_b_in, conv_w_dw, conv_b_dw, conv_ln_g, conv_ln_b, conv_w_out, conv_b_out, kv_norm, w_kv, attn_w_q, attn_w_o, mlp_w_in, mlp_w_out, final_norm, loss_target, m_norm_mix, m_norm_mlp, m_conv_w_in, m_conv_b_in, m_conv_w_dw, m_conv_b_dw, m_conv_ln_g, m_conv_ln_b, m_conv_w_out, m_conv_b_out, m_kv_norm, m_w_kv, m_attn_w_q, m_attn_w_o, m_mlp_w_in, m_mlp_w_out, m_final_norm, v_norm_mix, v_norm_mlp, v_conv_w_in, v_conv_b_in, v_conv_w_dw, v_conv_b_dw, v_conv_ln_g, v_conv_ln_b, v_conv_w_out, v_conv_b_out, v_kv_norm, v_w_kv, v_attn_w_q, v_attn_w_o, v_mlp_w_in, v_mlp_w_out, v_final_norm):
    _, s, d = x.shape
    dff = mlp_w_in.shape[2] * N_SHARD
    kvw = w_kv.shape[1]
    nh = d // HEAD_DIM
    group = nh // N_KV_HEADS
    ds4 = d // N_SHARD
    xi, yi, ci = _place()
    me = 2 * xi + yi
    place = jnp.stack([ci, me, 2 * me + ci]).astype(I32)

    h0 = x.reshape(s, d)
    target = loss_target.reshape(s, d)
    tabs = _rope_tables(s)

    def gather_begin(tag, bufs, n_whole=0):
        return _split_start(f"gather_start_{tag}", bufs, 3 * len(bufs),
                            functools.partial(_gather_copies, n_whole=n_whole)), n_whole

    def gather_land(tag, begun, later):
        handle, n_whole = begun
        bufs = _split_wait(f"gather_wait_{tag}", handle, functools.partial(_gather_copies, n_whole=n_whole), later)
        n_half = len(bufs) - n_whole
        fwd = _split_start(f"forward_start_{tag}", bufs[:n_half], 3 * n_half, _forward_copies)
        return fwd, bufs[n_half:]

    def gather_end(tag, landed, later):
        fwd, whole = landed
        return _split_wait(f"forward_wait_{tag}", fwd, _forward_copies, later) + whole

    ag_cin = gather_begin("conv_in", [
        _cast_bf16("cast_w_in", conv_w_in, 0, place),
        _pack_small(conv_b_in, conv_w_dw.reshape(CONV_WIDTH, ds4), conv_b_dw, conv_ln_g, conv_ln_b, conv_b_out, place),
    ], n_whole=1)
    ag_cout = gather_begin("conv_out", [_cast_bf16("cast_w_out", conv_w_out, 0, place)])
    ag_mi0 = gather_begin("mlp_in0", [_cast_bf16("cast_mlp_in0", mlp_w_in, 0, place)])
    ag_mo0 = gather_begin("mlp_out0", [_cast_bf16("cast_mlp_out0", mlp_w_out, 0, place)])
    ag_attn = gather_begin("attn", [
        _cast_bf16("cast_w_kv", w_kv.reshape(1, ds4, kvw), 0, place), _cast_bf16("cast_w_q", attn_w_q, 0, place),
        _cast_bf16("cast_w_o", attn_w_o, 0, place)])
    ag_mi1 = gather_begin("mlp_in1", [_cast_bf16("cast_mlp_in1", mlp_w_in, 1, place)])
    ag_mo1 = gather_begin("mlp_out1", [_cast_bf16("cast_mlp_out1", mlp_w_out, 1, place)])

    wmi_g = [None, None]
    wmo_f = [None, None]

    nm = [norm_mix[0:1], norm_mix[1:2]]
    nmlp = [norm_mlp[0:1], norm_mlp[1:2]]
    kvn = kv_norm.reshape(1, d)
    fin = final_norm.reshape(1, d)
    started = sum(h[0][3][0:1, 0:1] for h in (ag_cin, ag_cout, ag_mi0, ag_mo0, ag_attn, ag_mi1, ag_mo1))
    (y0,) = _rms_fwd("rms_mix0", h0, [nm[0] + started])

    w_in_g, small_g = gather_end("conv_in", gather_land("conv_in", ag_cin, y0), y0)
    b_in_f = small_g[:, 0, :].reshape(1, 2 * d)
    b_dw_f = small_g[:, 1, 0:ds4].reshape(1, d)
    ln_g_f = small_g[:, 1, ds4:2 * ds4].reshape(1, d)
    ln_b_f = small_g[:, 2, 0:ds4].reshape(1, d)
    b_out_f = small_g[:, 2, ds4:2 * ds4].reshape(1, d)
    w_dw_f = jnp.transpose(small_g[:, 8:8 + CONV_PAD, 0:ds4], (1, 0, 2)).reshape(CONV_PAD, d)

    def ep_bias(acc, ex, outs, j):
        outs[0][...] = (acc + ex[0][...]).astype(outs[0].dtype)

    def ep_residual(acc, ex, outs, j):
        outs[0][...] = ex[0][...] + acc

    def ep_residual_bias(acc, ex, outs, j):
        outs[0][...] = ex[0][...] + (acc + ex[1][...])

    def ep_relu2(acc, ex, outs, j):
        r = jnp.maximum(acc, 0.0)
        outs[0][...] = r.astype(BF16)
        outs[1][...] = (r * r).astype(BF16)

    by_residue = [(BF16, ("residues", dil)) for dil in DILATIONS]

    def put_by_residue(val, outs, stage):
        _to_residues(val, stage, outs, DILATIONS)

    def ep_rope(acc, ex, outs, j, stage):
        put_by_residue(_rope_apply(acc, ex[0][...], ex[1][...], ex[2][...], 1.0), outs, stage)

    def ep_rope_k(acc, ex, outs, j, stage):
        roped = _rope_apply(acc, ex[0][...], ex[1][...], ex[2][...], 1.0)
        put_by_residue(jnp.where(j == 0, roped, acc), outs, stage)

    def ep_by_residue(acc, ex, outs, j, stage):
        put_by_residue(acc, outs, stage)

    tab_extras = [(t, "rows") for t in tabs]

    def mlp_fwd(idx, h, y, out_weight):
        r, r2 = _matmul(f"mlp_in{idx}", "nn", y, wmi_g[idx], b_kind="col", m=s, n=dff, k=d,
                        outs=[(BF16, "plain"), (BF16, "plain")], epilogue=ep_relu2)
        wmo_f[idx] = out_weight(r2).reshape(dff, d)
        (h_new,) = _matmul(f"mlp_out{idx}", "nn", r2, wmo_f[idx], m=s, n=d, k=dff,
                           outs=[(F32, "plain")], extras=[(h, "ij")], epilogue=ep_residual)
        return h_new, r, r2

    (u,) = _matmul("conv_in", "nn", y0, w_in_g, b_kind="col", m=s, n=2 * d, k=d,
                   outs=[(BF16, "plain")], extras=[(b_in_f, "vec")], epilogue=ep_bias)
    land_cout = gather_land("conv_out", ag_cout, u)
    cpre = _dwconv_fwd(u, w_dw_f, b_dw_f + land_cout[0][3][0:1, 0:1])
    sact = _ln_silu_fwd(cpre, ln_g_f, ln_b_f)
    (w_out_g,) = gather_end("conv_out", land_cout, sact)
    w_out_f = w_out_g.reshape(d, d)
    (h1,) = _matmul("conv_out", "nn", sact, w_out_f, m=s, n=d, k=d,
                    outs=[(F32, "plain")], extras=[(h0, "ij"), (b_out_f, "vec")], epilogue=ep_residual_bias)
    (y1,) = _rms_fwd("rms_mlp0", h1, [nmlp[0]])
    (wmi_g[0],) = gather_end("mlp_in0", gather_land("mlp_in0", ag_mi0, y1), y1)
    h2, r0, r0sq = mlp_fwd(0, h1, y1, lambda r2: gather_end("mlp_out0", gather_land("mlp_out0", ag_mo0, r2), r2)[0])
    land_attn = gather_land("attn", ag_attn, h2)
    ykv, y2 = _rms_fwd("rms_kv_mix1", h2, [kvn + land_attn[0][3][0:1, 0:1], nm[1]])
    wkv_g, wq_g, wo_g = gather_end("attn", land_attn, y2)
    wkv_f, wq_f, wo_f = wkv_g.reshape(d, kvw), wq_g.reshape(d, d), wo_g.reshape(d, d)
    kv_parts = _matmul("kv_proj", "nn", ykv, wkv_f, m=s, n=kvw, k=d, tn=kvw // 2,
                       outs=by_residue, extras=tab_extras, epilogue=ep_rope_k, stage=True)
    q_parts = _matmul("q_proj", "nn", y2, wq_f, m=s, n=d, k=d,
                      outs=by_residue, extras=tab_extras, epilogue=ep_rope, stage=True)
    o_parts, lse_parts = [], []
    for dil, q_b, kv_b in zip(DILATIONS, q_parts, kv_parts):
        o_b, lse_b = _attn_fwd(f"attn_fwd_d{dil}", q_b, kv_b)
        o_parts.append(o_b)
        lse_parts.append(lse_b)
    o, lse = _attn_combine(o_parts, lse_parts)
    land_mi1 = gather_land("mlp_in1", ag_mi1, o)
    (h3,) = _matmul("attn_out", "nn", o, wo_f, m=s, n=d, k=d,
                    outs=[(F32, "plain")], extras=[(h2, "ij")], epilogue=ep_residual)
    (y3,) = _rms_fwd("rms_mlp1", h3, [nmlp[1]])
    (wmi_g[1],) = gather_end("mlp_in1", land_mi1, y3)
    h4, r1, r1sq = mlp_fwd(1, h3, y3, lambda r2: gather_end("mlp_out1", gather_land("mlp_out1", ag_mo1, r2), r2)[0])
    dh4, dh4b, d_fin, loss_cols = _final_loss(h4, fin, target)

    def ep_relu2_bwd(acc, ex, outs, j):
        outs[0][...] = (acc * (2.0 * ex[0][...].astype(F32))).astype(BF16)

    def mlp_bwd(idx, dhb, y, r, r2):
        (dz,) = _matmul(f"mlp_out{idx}_dx", "nt", dhb, wmo_f[idx], m=s, n=dff, k=d,
                        outs=[(BF16, "plain")], extras=[(r, "ij")], epilogue=ep_relu2_bwd)
        (dwo,) = _matmul(f"mlp_out{idx}_dw", "tn", r2, dhb, m=dff, n=d, k=s,
                         outs=[(BF16, "plain")])
        (dy,) = _matmul(f"mlp_in{idx}_dx", "nt", dz, wmi_g[idx], b_kind="col", m=s, n=d, k=dff,
                        outs=[(BF16, "plain")])
        (dwi,) = _matmul(f"mlp_in{idx}_dw", "tn", y, dz, m=d, n=dff, k=s,
                         outs=[(BF16, "col")])
        return dy, dwi, dwo.reshape(N_SHARD, dff // N_SHARD, d)

    def token(handle):
        return handle[3][0:1, 0:1]

    def rs_exchange(tag, grads):
        lands = [lax.empty((N_SHARD, g.shape[1] // 2, g.shape[2]), g.dtype) for g in grads]
        return _split_start(f"sibling_start_{tag}", list(grads) + lands, len(grads), _sibling_copies)

    def rs_send(tag, names, exchanged, later):
        bufs = _split_wait(f"sibling_wait_{tag}", exchanged, _sibling_copies, later)
        n = len(names)
        sums = [_chip_sum(f"chip_sum_{nme}", g, rh, place) for nme, g, rh in zip(names, bufs[:n], bufs[n:])]
        lands = [lax.empty((N_SHARD - 1,) + cs.shape[1:], cs.dtype) for cs in sums]
        return _split_start(f"owners_start_{tag}", sums + lands, 3 * n, _owner_copies)

    def rs_sum(tag, names, sent, later):
        bufs = _split_wait(f"owners_wait_{tag}", sent, _owner_copies, later)
        n = len(names)
        own = [_owner_sum(f"owner_sum_{nme}", cs, rp, place) for nme, cs, rp in zip(names, bufs[:n], bufs[n:])]
        return _split_start(f"swap_start_{tag}", own, n, _swap_copies)

    def rs_end(tag, swapped, later):
        return _split_wait(f"swap_wait_{tag}", swapped, _swap_copies, later)

    dy3, g_wmi1, g_wmo1 = mlp_bwd(1, dh4b, y3, r1, r1sq)
    x_mlp1 = rs_exchange("mlp1", [g_wmi1, g_wmo1])
    dh3, dh3b, d_nmlp1 = _rms_bwd("rms_mlp1_bwd", h3, [(nmlp[1] + token(x_mlp1), dy3)], dh4)

    do_parts = _matmul("attn_out_dx", "nt", dh3b, wo_f, m=s, n=d, k=d, outs=by_residue, epilogue=ep_by_residue,
                       stage=True)
    (g_wo,) = _matmul("attn_out_dw", "tn", o, dh3b, m=d, n=d, k=s, outs=[(BF16, "plain")])
    rs_mlp1 = rs_send("mlp1", ["mlp_in1", "mlp_out1"], x_mlp1, g_wo)
    lse_res, delta_res = _attn_delta(do_parts[0].reshape(s, d), o, lse, DILATIONS)
    dq_parts, dk_parts, dv_parts = [], [], []
    for dil, q_b, kv_b, do_b, lse_b, dl_b in zip(DILATIONS, q_parts, kv_parts, do_parts, lse_res, delta_res):
        dq_b, dk_b, dv_b = _attn_bwd(f"attn_bwd_d{dil}", q_b, kv_b, do_b, lse_b, dl_b)
        dq_parts.append(dq_b)
        dk_parts.append(dk_b)
        dv_parts.append(dv_b)
    dq = _residue_sum("rope_bwd_q", [(dq_parts, True)], tabs)
    dkv = _residue_sum("rope_bwd_kv", [(dk_parts, True), (dv_parts, False)], tabs)
    (g_wq,) = _matmul("q_proj_dw", "tn", y2, dq, m=d, n=d, k=s, outs=[(BF16, "plain")])
    (dy2,) = _matmul("q_proj_dx", "nt", dq, wq_f, m=s, n=d, k=d, outs=[(BF16, "plain")])
    (g_wkv,) = _matmul("kv_proj_dw", "tn", ykv, dkv, m=d, n=kvw, k=s, outs=[(BF16, "plain")])
    (dykv,) = _matmul("kv_proj_dx", "nt", dkv, wkv_f, m=s, n=d, k=kvw, outs=[(BF16, "plain")])
    x_attn = rs_exchange("attn", [g_wkv.reshape(N_SHARD, ds4, kvw), g_wq.reshape(N_SHARD, ds4, d),
                                  g_wo.reshape(N_SHARD, ds4, d)])
    dh2, dh2b, d_nm1, d_kvn = _rms_bwd("rms_kv_mix1_bwd", h2, [(nm[1] + token(x_attn), dy2), (kvn, dykv)], dh3)
    rs_attn = rs_send("attn", ["w_kv", "w_q", "w_o"], x_attn, dh2b)

    dy1, g_wmi0, g_wmo0 = mlp_bwd(0, dh2b, y1, r0, r0sq)
    x_mlp0 = rs_exchange("mlp0", [g_wmi0, g_wmo0])
    dh1, dh1b, d_nmlp0, d_b_out = _rms_bwd("rms_mlp0_bwd", h1, [(nmlp[0] + token(x_mlp0) + token(rs_attn), dy1)],
                                           dh2, want_colsum=True)

    (dsact,) = _matmul("conv_out_dx", "nt", dh1b, w_out_f, m=s, n=d, k=d, outs=[(BF16, "plain")])
    (g_wout,) = _matmul("conv_out_dw", "tn", sact, dh1b, m=d, n=d, k=s, outs=[(BF16, "plain")])
    rs_mlp0 = rs_send("mlp0", ["mlp_in0", "mlp_out0"], x_mlp0, g_wout)
    dc, d_ln_g, d_ln_b, d_b_dw = _ln_silu_bwd(cpre, ln_g_f + token(rs_mlp0), ln_b_f, dsact)
    du, d_w_dw, d_b_in_a, d_b_in_g = _dwconv_bwd(u, w_dw_f, dc)
    (g_win,) = _matmul("conv_in_dw", "tn", y0, du, b_kind="col", m=d, n=2 * d, k=s, outs=[(BF16, "col")])
    x_conv = rs_exchange("conv", [g_win, g_wout.reshape(N_SHARD, ds4, d)])
    (dy0,) = _matmul("conv_in_dx", "nt", du, w_in_g, a_kind="col", b_kind="col", m=s, n=d, k=2 * d,
                     outs=[(BF16, "plain")])
    dx, _, d_nm0 = _rms_bwd("rms_mix0_bwd", h0, [(nm[0] + token(x_conv), dy0)], dh1)

    small_rows = [(0, d_nm0), (1, d_nm1), (2, d_nmlp0), (3, d_nmlp1), (4, d_kvn), (5, d_fin), (6, d_b_dw),
                  (7, d_ln_g), (8, d_ln_b), (9, d_b_out), (10, d_b_in_a), (11, d_b_in_g), (12, loss_cols)]
    x_small = _split_start("small_start", [_small_pack(small_rows, d_w_dw, d),
                                           lax.empty((N_DEV, SMALL_ROWS, d), F32)], N_DEV - 1, _small_copies)
    rs_conv = rs_send("conv", ["w_in", "w_out"], x_conv, x_small[3])

    def big(name, w, m, v, g, layer=0, partial=None):
        shape = w.shape
        w3, m3, v3 = [t.reshape((-1,) + shape[-2:]) for t in (w, m, v)]
        if partial is not None:
            partial = [t.reshape(w3.shape) for t in partial]
        res = _adamw(name, w3, m3, v3, g, layer, partial)
        return [t.reshape(shape) for t in res]

    sw_mlp1 = rs_sum("mlp1", ["mlp_in1", "mlp_out1"], rs_mlp1, rs_conv[3])
    sw_attn = rs_sum("attn", ["w_kv", "w_q", "w_o"], rs_attn, sw_mlp1[3])
    f_wmi1, f_wmo1 = rs_end("mlp1", sw_mlp1, sw_attn[3])
    p_wmi = big("adam_mlp_in1", mlp_w_in, m_mlp_w_in, v_mlp_w_in, f_wmi1, 1)
    p_wmo = big("adam_mlp_out1", mlp_w_out, m_mlp_w_out, v_mlp_w_out, f_wmo1, 1)
    sw_mlp0 = rs_sum("mlp0", ["mlp_in0", "mlp_out0"], rs_mlp0, [p_wmi[0], p_wmo[0]])
    f_wkv, f_wq, f_wo = rs_end("attn", sw_attn, sw_mlp0[3])
    r_wkv = big("adam_w_kv", w_kv, m_w_kv, v_w_kv, f_wkv)
    r_wq = big("adam_w_q", attn_w_q, m_attn_w_q, v_attn_w_q, f_wq)
    r_wo = big("adam_w_o", attn_w_o, m_attn_w_o, v_attn_w_o, f_wo)
    sw_conv = rs_sum("conv", ["w_in", "w_out"], rs_conv, [r_wkv[0], r_wq[0], r_wo[0]])
    f_wmi0, f_wmo0 = rs_end("mlp0", sw_mlp0, sw_conv[3])
    r_wmi = big("adam_mlp_in0", mlp_w_in, m_mlp_w_in, v_mlp_w_in, f_wmi0, 0, p_wmi)
    r_wmo = big("adam_mlp_out0", mlp_w_out, m_mlp_w_out, v_mlp_w_out, f_wmo0, 0, p_wmo)
    f_win, f_wout = rs_end("conv", sw_conv, [r_wmi[0], r_wmo[0]])
    r_win = big("adam_w_in", conv_w_in, m_conv_w_in, v_conv_w_in, f_win)
    r_wout = big("adam_w_out", conv_w_out, m_conv_w_out, v_conv_w_out, f_wout)

    small_pack, small_slots = _split_wait("small_wait", x_small, _small_copies, r_wout[0])
    red = _small_sum(small_pack, small_slots, place)
    loss = red[12, 0]
    g_norm_mix = red[0:2]
    g_norm_mlp = red[2:4]
    g_kv_norm = red[4:5]
    g_final = red[5:6]

    def my_cols(row):
        return lax.dynamic_slice(red, (row, me * ds4), (1, ds4))

    g_b_dw, g_ln_g, g_ln_b, g_b_out = my_cols(6), my_cols(7), my_cols(8), my_cols(9)
    half_in = 2 * d // N_SHARD
    b_in_row = 10 + me // 2
    g_b_in = lax.dynamic_slice(red, (b_in_row, (me % 2) * half_in), (1, half_in))
    g_w_dw = lax.dynamic_slice(red, (16, me * ds4), (CONV_WIDTH, ds4))

    sm_w =[norm_mix, norm_mlp, conv_b_in, conv_w_dw.reshape(CONV_WIDTH, ds4), conv_b_dw, conv_ln_g, conv_ln_b,
            conv_b_out, kv_norm.reshape(1, d), final_norm.reshape(1, d)]
    sm_m = [m_norm_mix, m_norm_mlp, m_conv_b_in, m_conv_w_dw.reshape(CONV_WIDTH, ds4), m_conv_b_dw, m_conv_ln_g,
            m_conv_ln_b, m_conv_b_out, m_kv_norm.reshape(1, d), m_final_norm.reshape(1, d)]
    sm_v = [v_norm_mix, v_norm_mlp, v_conv_b_in, v_conv_w_dw.reshape(CONV_WIDTH, ds4), v_conv_b_dw, v_conv_ln_g,
            v_conv_ln_b, v_conv_b_out, v_kv_norm.reshape(1, d), v_final_norm.reshape(1, d)]
    sm_g = [g_norm_mix, g_norm_mlp, g_b_in, g_w_dw, g_b_dw, g_ln_g, g_ln_b, g_b_out, g_kv_norm, g_final]
    sm_d, sm_nm, sm_nv = _adam_small(sm_w, sm_m, sm_v, sm_g)
    shapes = [norm_mix.shape, norm_mlp.shape, conv_b_in.shape, conv_w_dw.shape, conv_b_dw.shape, conv_ln_g.shape,
              conv_ln_b.shape, conv_b_out.shape, kv_norm.shape, final_norm.shape]
    sm_g, sm_d, sm_nm, sm_nv = [[t.reshape(sh) for t, sh in zip(lst, shapes)] for lst in (sm_g, sm_d, sm_nm, sm_nv)]

    def order(sm, idx):
        return [sm[0], sm[1], r_win[idx], sm[2], sm[3], sm[4], sm[5], sm[6], r_wout[idx], sm[7], sm[8],
                r_wkv[idx], r_wq[idx], r_wo[idx], r_wmi[idx], r_wmo[idx], sm[9]]

    return (loss, dx.reshape(x.shape), *order(sm_g, 0), *order(sm_d, 1), *order(sm_nm, 2), *order(sm_nv, 3))
```

```python
import functools
import math

import jax
import jax.numpy as jnp
from jax import lax
from jax.experimental import pallas as pl
from jax.experimental.pallas import tpu as pltpu

F32 = jnp.float32
BF16 = jnp.bfloat16
I32 = jnp.int32

NORM_EPS = 1e-6
LN_EPS = 1e-5
HEAD_DIM = 128
N_KV_HEADS = 4
ROT_DIM = 32
ROPE_THETA = 500000.0
CONV_WIDTH = 31
CONV_PAD = 32
ATT_BLOCK = 128
ATT_STEP_BLOCKS = 8
DILATIONS = (1, 4, 16)
ADAM_LR = 0.001
ADAM_B1 = 0.9
ADAM_B2 = 0.999
ADAM_EPS = 1e-08
ADAM_WD = 0.01
ADAM_STEP = 10
N_SHARD = 4
N_DEV = 8
LANES = 128
VMEM_LIMIT = 48 * 1024 * 1024
MM_TM, MM_TN, MM_TK = 1024, 1024, 2048
ROW_TILE = 256
CONV_CB = 128
CONV_T = 128
SMALL_ROWS = 48
MESH = pl.DeviceIdType.MESH
ANY = pl.BlockSpec(memory_space=pl.ANY)
HBM = pl.BlockSpec(memory_space=pltpu.HBM)
SEM = pl.BlockSpec(memory_space=pltpu.SEMAPHORE)
SPLIT_EFFECT = pltpu.SideEffectType.DATAFLOW_SIDE_EFFECTING


def _params(*sem):
    return pltpu.CompilerParams(dimension_semantics=sem, vmem_limit_bytes=VMEM_LIMIT)


def _sigmoid(x):
    return 1.0 / (1.0 + jnp.exp(-x))


def _wspec(kind, arr_shape, br, bc, pick):
    if kind == "plain":
        return pl.BlockSpec((br, bc), pick)
    per = arr_shape[2] // bc

    def idx(*g):
        rb, cb = pick(*g)
        return (cb // per, rb, cb % per)

    return pl.BlockSpec((None, br, bc), idx)


def _stage_shape(rows, w):
    return (w // LANES, rows, LANES)


def _to_residues(val, stage_ref, out_refs, dils):
    planes, rows, _ = stage_ref.shape
    for c in range(planes):
        stage_ref[c] = val[:, c * LANES:(c + 1) * LANES]
    for out_ref, dil in zip(out_refs, dils):
        if dil == 1:
            out_ref[0] = val.astype(out_ref.dtype)
            continue
        for r in range(dil):
            for c in range(planes):
                out_ref[r, :, c * LANES:(c + 1) * LANES] = stage_ref.at[c][pl.ds(r, rows // dil, stride=dil), :].astype(
                    out_ref.dtype)


def _from_residues(src_ref, stage_ref, dil):
    planes, rows, _ = stage_ref.shape
    if dil == 1:
        return lambda c: src_ref[0, :, c * LANES:(c + 1) * LANES].astype(F32)
    for r in range(dil):
        for c in range(planes):
            stage_ref.at[c][pl.ds(r, rows // dil, stride=dil), :] = src_ref[r, :, c * LANES:(c + 1) * LANES].astype(F32)
    return lambda c: stage_ref[c]


def _matmul(name, mode, a, b, *, m, n, k, tn=MM_TN, a_kind="plain", b_kind="plain", outs, extras=(), epilogue=None,
            stage=False):
    tm, tn, tk = min(MM_TM, m), min(tn, n), min(MM_TK, k)
    if b_kind == "col" and mode in ("nn", "tn"):
        tn = min(tn, n // b.shape[0])
    if b_kind == "col" and mode == "nt":
        tk = min(tk, k // b.shape[0])
    if a_kind == "col":
        assert mode == "nt"
        tk = min(tk, k // a.shape[0])
    if any(kind == "col" for _, kind in outs):
        tn = min(tn, n // N_SHARD)
    assert m % tm == 0 and n % tn == 0 and k % tk == 0, (name, m, n, k, tm, tn, tk)
    nk = k // tk
    grid = (m // tm, n // tn, nk)
    if mode == "nn":
        a_spec = pl.BlockSpec((tm, tk), lambda i, j, kk: (i, kk))
        b_spec = _wspec(b_kind, b.shape, tk, tn, lambda i, j, kk: (kk, j))
        dims = (((1,), (0,)), ((), ()))
    elif mode == "nt":
        a_spec = _wspec(a_kind, a.shape, tm, tk, lambda i, j, kk: (i, kk))
        b_spec = _wspec(b_kind, b.shape, tn, tk, lambda i, j, kk: (j, kk))
        dims = (((1,), (1,)), ((), ()))
    else:
        a_spec = pl.BlockSpec((tk, tm), lambda i, j, kk: (kk, i))
        b_spec = _wspec(b_kind, b.shape, tk, tn, lambda i, j, kk: (kk, j))
        dims = (((0,), (0,)), ((), ()))
    out_shape, out_specs = [], []
    for dtype, kind in outs:
        if isinstance(kind, tuple):
            dil = kind[1]
            out_shape.append(jax.ShapeDtypeStruct((dil, m // dil, n), dtype))
            out_specs.append(pl.BlockSpec((dil, tm // dil, tn), lambda i, j, kk: (0, i, j)))
            continue
        shape = (m, n) if kind == "plain" else (N_SHARD, m, n // N_SHARD)
        out_shape.append(jax.ShapeDtypeStruct(shape, dtype))
        out_specs.append(_wspec(kind, shape, tm, tn, lambda i, j, kk: (i, j)))
    n_ex = len(extras)
    ex_specs = {"ij": pl.BlockSpec((tm, tn), lambda i, j, kk: (i, j)),
                "vec": pl.BlockSpec((1, tn), lambda i, j, kk: (0, j)),
                "rows": pl.BlockSpec((tm, LANES), lambda i, j, kk: (i, 0))}

    def body(*refs):
        a_ref, b_ref = refs[0], refs[1]
        ex_refs = refs[2:2 + n_ex]
        out_refs = refs[2 + n_ex:2 + n_ex + len(outs)]
        j = pl.program_id(1)

        def finish(res):
            if epilogue is None:
                out_refs[0][...] = res.astype(out_refs[0].dtype)
            elif stage:
                epilogue(res, ex_refs, out_refs, j, refs[-1])
            else:
                epilogue(res, ex_refs, out_refs, j)

        prod = lax.dot_general(a_ref[...], b_ref[...], dims, preferred_element_type=F32)
        if nk == 1:
            finish(prod)
            return
        acc_ref = refs[2 + n_ex + len(outs)]
        kk = pl.program_id(2)

        @pl.when(kk == 0)
        def _():
            acc_ref[...] = prod

        @pl.when(kk > 0)
        def _():
            acc_ref[...] += prod

        @pl.when(kk == nk - 1)
        def _():
            finish(acc_ref[...])

    res = pl.pallas_call(
        body, name=name, grid=grid,
        in_specs=[a_spec, b_spec] + [ex_specs[how] for _, how in extras],
        out_specs=out_specs, out_shape=out_shape,
        scratch_shapes=[pltpu.VMEM((tm, tn), F32)] * (nk > 1) + [pltpu.VMEM(_stage_shape(tm, tn), F32)] * bool(stage),
        compiler_params=_params("parallel", "parallel", "arbitrary"),
    )(a, b, *[e for e, _ in extras])
    return res


def _rope_tables(seq):
    half = ROT_DIM // 2
    pos = jnp.arange(seq, dtype=F32)
    inv = ROPE_THETA ** (-jnp.arange(0, ROT_DIM, 2, dtype=F32) / ROT_DIM)
    ang = pos[:, None] * inv[None, :]
    cos, sin = jnp.cos(ang), jnp.sin(ang)
    zeros = jnp.zeros((seq, HEAD_DIM - ROT_DIM), F32)
    ctab = jnp.concatenate([cos, cos, zeros + 1.0], axis=1)
    atab = jnp.concatenate([-sin, jnp.zeros((seq, half), F32), zeros], axis=1)
    btab = jnp.concatenate([jnp.zeros((seq, half), F32), sin, zeros], axis=1)
    return ctab, atab, btab


def _rope_apply(x, ctab, atab, btab, sign):
    w = x.shape[1]
    reps = w // HEAD_DIM
    half = ROT_DIM // 2
    c = jnp.tile(ctab, (1, reps))
    a = jnp.tile(atab, (1, reps))
    b = jnp.tile(btab, (1, reps))
    up = pltpu.roll(x, w - half, 1)
    down = pltpu.roll(x, half, 1)
    return x * c + sign * (up * a + down * b)


def _rows(t, w):
    return pl.BlockSpec((t, w), lambda i: (i, 0))


def _fixed(shape):
    nd = len(shape)
    return pl.BlockSpec(shape, lambda i: (0,) * nd)


def _rms_fwd(name, x, gains):
    s, d = x.shape
    t = min(ROW_TILE, s)
    ng = len(gains)

    def body(x_ref, *refs):
        xv = x_ref[...]
        r = lax.rsqrt(jnp.mean(xv * xv, axis=-1, keepdims=True) + NORM_EPS)
        xn = xv * r
        for g_ref, y_ref in zip(refs[:ng], refs[ng:]):
            y_ref[...] = (xn * g_ref[...]).astype(BF16)

    return pl.pallas_call(
        body, name=name, grid=(s // t,),
        in_specs=[_rows(t, d)] + [_fixed((1, d))] * ng,
        out_specs=[_rows(t, d)] * ng,
        out_shape=[jax.ShapeDtypeStruct((s, d), BF16)] * ng,
        compiler_params=_params("parallel"),
    )(x, *gains)


def _rms_bwd(name, x, pairs, dh_in, want_colsum=False):
    s, d = x.shape
    t = min(ROW_TILE, s)
    n_p = len(pairs)

    def body(x_ref, dh_ref, *refs):
        g_refs = refs[:n_p]
        dy_refs = refs[n_p:2 * n_p]
        dh_out, dhb_out = refs[2 * n_p], refs[2 * n_p + 1]
        dg_refs = refs[2 * n_p + 2:2 * n_p + 2 + n_p]
        cs_ref = refs[-1] if want_colsum else None
        i = pl.program_id(0)
        xv = x_ref[...]
        r = lax.rsqrt(jnp.mean(xv * xv, axis=-1, keepdims=True) + NORM_EPS)
        xn = xv * r
        dh = dh_ref[...]
        for g_ref, dy_ref, dg_ref in zip(g_refs, dy_refs, dg_refs):
            dy = dy_ref[...].astype(F32)
            u = dy * g_ref[...]
            dh = dh + r * (u - xn * jnp.mean(u * xn, axis=-1, keepdims=True))
            part = jnp.sum(dy * xn, axis=0, keepdims=True)

            @pl.when(i == 0)
            def _():
                dg_ref[...] = part

            @pl.when(i > 0)
            def _():
                dg_ref[...] += part

        dh_out[...] = dh
        dhb_out[...] = dh.astype(BF16)
        if want_colsum:
            col = jnp.sum(dh, axis=0, keepdims=True)

            @pl.when(i == 0)
            def _():
                cs_ref[...] = col

            @pl.when(i > 0)
            def _():
                cs_ref[...] += col

    n_vec = n_p + (1 if want_colsum else 0)
    return pl.pallas_call(
        body, name=name, grid=(s // t,),
        in_specs=[_rows(t, d), _rows(t, d)] + [_fixed((1, d))] * n_p + [_rows(t, d)] * n_p,
        out_specs=[_rows(t, d), _rows(t, d)] + [_fixed((1, d))] * n_vec,
        out_shape=[jax.ShapeDtypeStruct((s, d), F32), jax.ShapeDtypeStruct((s, d), BF16)]
        + [jax.ShapeDtypeStruct((1, d), F32)] * n_vec,
        compiler_params=_params("arbitrary"),
    )(x, dh_in, *[g for g, _ in pairs], *[dy for _, dy in pairs])


def _final_loss(x, g, target):
    s, d = x.shape
    t = min(ROW_TILE, s)

    def body(x_ref, g_ref, t_ref, dh_out, dhb_out, dg_ref, loss_ref):
        i = pl.program_id(0)
        xv = x_ref[...]
        gv = g_ref[...]
        r = lax.rsqrt(jnp.mean(xv * xv, axis=-1, keepdims=True) + NORM_EPS)
        xn = xv * r
        diff = xn * gv - t_ref[...]
        dy = diff / d
        u = dy * gv
        dh = r * (u - xn * jnp.mean(u * xn, axis=-1, keepdims=True))
        dh_out[...] = dh
        dhb_out[...] = dh.astype(BF16)
        dg = jnp.sum(dy * xn, axis=0, keepdims=True)
        lc = jnp.sum(0.5 * diff * dy, axis=0, keepdims=True)

        @pl.when(i == 0)
        def _():
            dg_ref[...] = dg
            loss_ref[...] = lc

        @pl.when(i > 0)
        def _():
            dg_ref[...] += dg
            loss_ref[...] += lc

    return pl.pallas_call(
        body, name="final_loss", grid=(s // t,),
        in_specs=[_rows(t, d), _fixed((1, d)), _rows(t, d)],
        out_specs=[_rows(t, d), _rows(t, d), _fixed((1, d)), _fixed((1, d))],
        out_shape=[jax.ShapeDtypeStruct((s, d), F32), jax.ShapeDtypeStruct((s, d), BF16),
                   jax.ShapeDtypeStruct((1, d), F32), jax.ShapeDtypeStruct((1, d), F32)],
        compiler_params=_params("arbitrary"),
    )(x, g, target)


def _ln_silu_fwd(c, g, b):
    s, d = c.shape
    t = min(ROW_TILE, s)

    def body(c_ref, g_ref, b_ref, s_ref):
        cv = c_ref[...]
        mu = jnp.mean(cv, axis=-1, keepdims=True)
        xc = cv - mu
        rs = lax.rsqrt(jnp.mean(xc * xc, axis=-1, keepdims=True) + LN_EPS)
        ln = xc * rs * g_ref[...] + b_ref[...]
        s_ref[...] = (ln * _sigmoid(ln)).astype(BF16)

    return pl.pallas_call(
        body, name="ln_silu_fwd", grid=(s // t,),
        in_specs=[_rows(t, d), _fixed((1, d)), _fixed((1, d))],
        out_specs=_rows(t, d), out_shape=jax.ShapeDtypeStruct((s, d), BF16),
        compiler_params=_params("parallel"),
    )(c, g, b)


def _ln_silu_bwd(c, g, b, ds):
    s, d = c.shape
    t = min(ROW_TILE, s)

    def body(c_ref, g_ref, b_ref, ds_ref, dc_ref, dg_ref, db_ref, dbdw_ref):
        i = pl.program_id(0)
        cv = c_ref[...]
        gv = g_ref[...]
        mu = jnp.mean(cv, axis=-1, keepdims=True)
        xc = cv - mu
        rs = lax.rsqrt(jnp.mean(xc * xc, axis=-1, keepdims=True) + LN_EPS)
        nrm = xc * rs
        ln = nrm * gv + b_ref[...]
        sig = _sigmoid(ln)
        dln = ds_ref[...].astype(F32) * sig * (1.0 + ln * (1.0 - sig))
        dn = dln * gv
        dc = rs * (dn - jnp.mean(dn, axis=-1, keepdims=True)
                   - nrm * jnp.mean(dn * nrm, axis=-1, keepdims=True))
        dc_ref[...] = dc
        pg = jnp.sum(dln * nrm, axis=0, keepdims=True)
        pb = jnp.sum(dln, axis=0, keepdims=True)
        pc = jnp.sum(dc, axis=0, keepdims=True)

        @pl.when(i == 0)
        def _():
            dg_ref[...] = pg
            db_ref[...] = pb
            dbdw_ref[...] = pc

        @pl.when(i > 0)
        def _():
            dg_ref[...] += pg
            db_ref[...] += pb
            dbdw_ref[...] += pc

    return pl.pallas_call(
        body, name="ln_silu_bwd", grid=(s // t,),
        in_specs=[_rows(t, d), _fixed((1, d)), _fixed((1, d)), _rows(t, d)],
        out_specs=[_rows(t, d)] + [_fixed((1, d))] * 3,
        out_shape=[jax.ShapeDtypeStruct((s, d), F32)] + [jax.ShapeDtypeStruct((1, d), F32)] * 3,
        compiler_params=_params("arbitrary"),
    )(c, g, b, ds)


def _residue_spec(dil, t, w):
    return pl.BlockSpec((dil, t // dil, w), lambda i: (0, i, 0))


def _attn_combine(o_list, lse_list):
    dil0, sd0, d = o_list[0].shape
    s = dil0 * sd0
    lw = lse_list[0].shape[2]
    group = d // HEAD_DIM // N_KV_HEADS
    t = min(ROW_TILE, s)
    nb = len(o_list)
    dils = [o.shape[0] for o in o_list]

    def body(*refs):
        o_out, l_out = refs[2 * nb], refs[2 * nb + 1]
        o_stage, l_stage = refs[2 * nb + 2:3 * nb + 2], refs[3 * nb + 2:]
        o_planes = [_from_residues(src, stage, dil) for src, stage, dil in zip(refs[:nb], o_stage, dils)]
        l_planes = [_from_residues(src, stage, dil) for src, stage, dil in zip(refs[nb:2 * nb], l_stage, dils)]
        for kh in range(N_KV_HEADS):
            ls = [plane(kh) for plane in l_planes]
            mx = ls[0]
            for l in ls[1:]:
                mx = jnp.maximum(mx, l)
            es = [jnp.exp(l - mx) for l in ls]
            den = es[0]
            for e in es[1:]:
                den = den + e
            l_out[:, kh * LANES:(kh + 1) * LANES] = mx + jnp.log(den)
            ws = [e / den for e in es]
            for g in range(group):
                h = kh * group + g
                acc = jnp.zeros((t, HEAD_DIM), F32)
                for plane, w in zip(o_planes, ws):
                    acc = acc + w[:, g:g + 1] * plane(h)
                o_out[:, h * HEAD_DIM:(h + 1) * HEAD_DIM] = acc.astype(BF16)

    return pl.pallas_call(
        body, name="attn_combine", grid=(s // t,),
        in_specs=[_residue_spec(dil, t, d) for dil in dils] + [_residue_spec(dil, t, lw) for dil in dils],
        out_specs=[_rows(t, d), _rows(t, lw)],
        out_shape=[jax.ShapeDtypeStruct((s, d), BF16), jax.ShapeDtypeStruct((s, lw), F32)],
        scratch_shapes=[pltpu.VMEM(_stage_shape(t, d), F32)] * nb + [pltpu.VMEM(_stage_shape(t, lw), F32)] * nb,
        compiler_params=_params("parallel"),
    )(*o_list, *lse_list)


def _attn_delta(do, o, lse, dils):
    s, d = o.shape
    lw = lse.shape[1]
    group = d // HEAD_DIM // N_KV_HEADS
    t = min(ROW_TILE, s)
    nd = len(dils)

    def body(do_ref, o_ref, lse_ref, *refs):
        stage = refs[-1]
        lane = lax.broadcasted_iota(I32, (t, LANES), 1)
        planes = []
        for kh in range(N_KV_HEADS):
            out = jnp.zeros((t, LANES), F32)
            for g in range(group):
                cols = slice((kh * group + g) * HEAD_DIM, (kh * group + g + 1) * HEAD_DIM)
                v = jnp.sum(do_ref[:, cols].astype(F32) * o_ref[:, cols].astype(F32), axis=-1, keepdims=True)
                out = jnp.where(lane == g, v, out)
            planes.append(out)
        _to_residues(lse_ref[...], stage, refs[:nd], dils)
        _to_residues(jnp.concatenate(planes, axis=1), stage, refs[nd:2 * nd], dils)

    res = pl.pallas_call(
        body, name="attn_delta", grid=(s // t,),
        in_specs=[_rows(t, d), _rows(t, d), _rows(t, lw)],
        out_specs=[_residue_spec(dil, t, lw) for dil in dils] * 2,
        out_shape=[jax.ShapeDtypeStruct((dil, s // dil, lw), F32) for dil in dils] * 2,
        scratch_shapes=[pltpu.VMEM(_stage_shape(t, lw), F32)],
        compiler_params=_params("parallel"),
    )(do, o, lse)
    return res[:nd], res[nd:]


def _residue_sum(name, groups, tabs):
    first = groups[0][0][0]
    s, w = first.shape[0] * first.shape[1], first.shape[2]
    t = min(ROW_TILE, s)
    flat = [p for parts, _ in groups for p in parts]

    def body(*refs):
        c_ref, a_ref, b_ref = refs[len(flat):len(flat) + 3]
        out = refs[len(flat) + 3]
        stages = refs[len(flat) + 4:]
        k = 0
        for gi, (parts, rotate) in enumerate(groups):
            planes = [_from_residues(refs[k + i], stages[k + i], p.shape[0]) for i, p in enumerate(parts)]
            k += len(parts)
            for c in range(w // LANES):
                tot = planes[0](c)
                for plane in planes[1:]:
                    tot = tot + plane(c)
                if rotate:
                    tot = _rope_apply(tot, c_ref[...], a_ref[...], b_ref[...], -1.0)
                out[:, gi * w + c * LANES:gi * w + (c + 1) * LANES] = tot.astype(BF16)

    return pl.pallas_call(
        body, name=name, grid=(s // t,),
        in_specs=[_residue_spec(p.shape[0], t, w) for p in flat] + [_rows(t, HEAD_DIM)] * 3,
        out_specs=_rows(t, len(groups) * w), out_shape=jax.ShapeDtypeStruct((s, len(groups) * w), BF16),
        scratch_shapes=[pltpu.VMEM(_stage_shape(t, w), F32) for _ in flat],
        compiler_params=_params("parallel"),
    )(*flat, *tabs)


def _dwconv_fwd(u, w_dw, b_dw):
    s, d2 = u.shape
    d = d2 // 2
    cb = min(CONV_CB, d)
    nblk = d // cb
    tt = min(CONV_T, s)

    def body(ua_ref, ug_ref, w_ref, b_ref, c_ref, xp_ref):
        gl = ua_ref[...].astype(F32) * _sigmoid(ug_ref[...].astype(F32))
        xp_ref[0:CONV_PAD, :] = jnp.zeros((CONV_PAD, cb), F32)
        xp_ref[CONV_PAD:, :] = gl
        wv = w_ref[...]
        bv = b_ref[...]
        for t0 in range(0, s, tt):
            acc = jnp.zeros((tt, cb), F32) + bv
            for kk in range(CONV_WIDTH):
                off = t0 + CONV_PAD - (CONV_WIDTH - 1) + kk
                acc = acc + wv[kk:kk + 1, :] * xp_ref[off:off + tt, :]
            c_ref[t0:t0 + tt, :] = acc

    return pl.pallas_call(
        body, name="dwconv_fwd", grid=(nblk,),
        in_specs=[pl.BlockSpec((s, cb), lambda j: (0, j)), pl.BlockSpec((s, cb), lambda j: (0, j + nblk)),
                  pl.BlockSpec((CONV_PAD, cb), lambda j: (0, j)), pl.BlockSpec((1, cb), lambda j: (0, j))],
        out_specs=pl.BlockSpec((s, cb), lambda j: (0, j)),
        out_shape=jax.ShapeDtypeStruct((s, d), F32),
        scratch_shapes=[pltpu.VMEM((s + CONV_PAD, cb), F32)],
        compiler_params=_params("parallel"),
    )(u, u, w_dw, b_dw)


def _dwconv_bwd(u, w_dw, dc):
    s, d2 = u.shape
    d = d2 // 2
    cb = min(CONV_CB, d)
    nblk = d // cb
    tt = min(CONV_T, s)

    def body(ua_ref, ug_ref, w_ref, dc_ref, du_ref, dw_ref, dba_ref, dbg_ref, glp_ref, dcp_ref, acc_ref):
        a = ua_ref[...].astype(F32)
        sig = _sigmoid(ug_ref[...].astype(F32))
        glp_ref[0:CONV_PAD, :] = jnp.zeros((CONV_PAD, cb), F32)
        glp_ref[CONV_PAD:, :] = a * sig
        dcp_ref[0:s, :] = dc_ref[...]
        dcp_ref[s:, :] = jnp.zeros((CONV_PAD, cb), F32)
        acc_ref[...] = jnp.zeros_like(acc_ref)
        wv = w_ref[...]
        dba = jnp.zeros((1, cb), F32)
        dbg = jnp.zeros((1, cb), F32)
        for t0 in range(0, s, tt):
            dgl = jnp.zeros((tt, cb), F32)
            dct = dc_ref[t0:t0 + tt, :]
            for kk in range(CONV_WIDTH):
                off = t0 + (CONV_WIDTH - 1) - kk
                dgl = dgl + wv[kk:kk + 1, :] * dcp_ref[off:off + tt, :]
                goff = t0 + CONV_PAD - (CONV_WIDTH - 1) + kk
                prod = dct * glp_ref[goff:goff + tt, :]
                acc_ref[8 * kk:8 * kk + 8, :] += jnp.sum(prod.reshape(tt // 8, 8, cb), axis=0)
            at = ua_ref[t0:t0 + tt, :].astype(F32)
            st = _sigmoid(ug_ref[t0:t0 + tt, :].astype(F32))
            da = dgl * st
            dg = dgl * at * st * (1.0 - st)
            du_ref[0, t0:t0 + tt, :] = da.astype(BF16)
            du_ref[1, t0:t0 + tt, :] = dg.astype(BF16)
            dba = dba + jnp.sum(da, axis=0, keepdims=True)
            dbg = dbg + jnp.sum(dg, axis=0, keepdims=True)
        dba_ref[...] = dba
        dbg_ref[...] = dbg
        for kk in range(CONV_WIDTH):
            dw_ref[kk:kk + 1, :] = jnp.sum(acc_ref[8 * kk:8 * kk + 8, :], axis=0, keepdims=True)
        dw_ref[CONV_WIDTH:, :] = jnp.zeros((CONV_PAD - CONV_WIDTH, cb), F32)

    blk = pl.BlockSpec((s, cb), lambda j: (0, j))
    vec = pl.BlockSpec((1, cb), lambda j: (0, j))
    return pl.pallas_call(
        body, name="dwconv_bwd", grid=(nblk,),
        in_specs=[blk, pl.BlockSpec((s, cb), lambda j: (0, j + nblk)),
                  pl.BlockSpec((CONV_PAD, cb), lambda j: (0, j)), blk],
        out_specs=[pl.BlockSpec((2, s, cb), lambda j: (0, 0, j)), pl.BlockSpec((CONV_PAD, cb), lambda j: (0, j)),
                   vec, vec],
        out_shape=[jax.ShapeDtypeStruct((2, s, d), BF16), jax.ShapeDtypeStruct((CONV_PAD, d), F32),
                   jax.ShapeDtypeStruct((1, d), F32), jax.ShapeDtypeStruct((1, d), F32)],
        scratch_shapes=[pltpu.VMEM((s + CONV_PAD, cb), F32), pltpu.VMEM((s + CONV_PAD, cb), F32),
                        pltpu.VMEM((8 * CONV_PAD, cb), F32)],
        compiler_params=_params("parallel"),
    )(u, u, w_dw, dc)


def _stack_heads(x, group):
    return jnp.concatenate([x[:, g * HEAD_DIM:(g + 1) * HEAD_DIM] for g in range(group)], axis=0)


def _unstack_heads(x, group):
    return jnp.concatenate([x[g * ATT_BLOCK:(g + 1) * ATT_BLOCK, :] for g in range(group)], axis=1)


def _stack_cols(x, group):
    return jnp.concatenate([x[:, g:g + 1] for g in range(group)], axis=0)


def _band_mask(nb, group):
    rows = group * ATT_BLOCK
    row = lax.broadcasted_iota(I32, (rows, 2 * ATT_BLOCK), 0) % ATT_BLOCK
    col = lax.broadcasted_iota(I32, (rows, 2 * ATT_BLOCK), 1)
    return (col >= row) & (col <= row + ATT_BLOCK) & ((col >= ATT_BLOCK) | (nb > 0))


def _window(ref, nb):
    prev = pl.multiple_of(jnp.maximum(nb - 1, 0) * ATT_BLOCK, ATT_BLOCK)
    cur = pl.multiple_of(nb * ATT_BLOCK, ATT_BLOCK)
    return jnp.concatenate([ref[pl.ds(prev, ATT_BLOCK), :], ref[pl.ds(cur, ATT_BLOCK), :]], axis=0)


def _residues_per_step(dil, nblk):
    return max(1, min(dil, ATT_STEP_BLOCKS // nblk))


def _attn_fwd(name, q, kv):
    dil, sd, d = q.shape
    group = d // HEAD_DIM // N_KV_HEADS
    gw = group * HEAD_DIM
    nblk = sd // ATT_BLOCK
    scale = 1.0 / math.sqrt(HEAD_DIM)
    nt = (((1,), (1,)), ((), ()))

    rb = _residues_per_step(dil, nblk)

    def body(q_all, k_all, v_all, o_all, lse_all):
        lane = lax.broadcasted_iota(I32, (ATT_BLOCK, LANES), 1)
        for rr in range(rb):
            q_ref, k_ref, v_ref, o_ref, lse_ref = [ref.at[rr] for ref in (q_all, k_all, v_all, o_all, lse_all)]

            def step(nb, carry):
                rows = pl.ds(pl.multiple_of(nb * ATT_BLOCK, ATT_BLOCK), ATT_BLOCK)
                qs = _stack_heads(q_ref[rows, :], group)
                kw = _window(k_ref, nb)
                vw = _window(v_ref, nb)
                sc = lax.dot_general(qs, kw, nt, preferred_element_type=F32) * scale
                sc = jnp.where(_band_mask(nb, group), sc, -jnp.inf)
                mx = jnp.max(sc, axis=-1, keepdims=True)
                p = jnp.exp(sc - mx)
                l = jnp.sum(p, axis=-1, keepdims=True)
                o = jnp.dot(p.astype(BF16), vw, preferred_element_type=F32) / l
                o_ref[rows, :] = _unstack_heads(o, group).astype(BF16)
                lse = mx + jnp.log(l)
                out = jnp.zeros((ATT_BLOCK, LANES), F32)
                for g in range(group):
                    out = jnp.where(lane == g, lse[g * ATT_BLOCK:(g + 1) * ATT_BLOCK, :], out)
                lse_ref[rows, :] = out
                return carry

            lax.fori_loop(0, nblk, step, 0, unroll=min(2, nblk))

    kvh = N_KV_HEADS
    qspec = pl.BlockSpec((rb, sd, gw), lambda r, h: (r, 0, h))
    kspec = pl.BlockSpec((rb, sd, HEAD_DIM), lambda r, h: (r, 0, h))
    return pl.pallas_call(
        body, name=name, grid=(dil // rb, kvh),
        in_specs=[qspec, kspec, pl.BlockSpec((rb, sd, HEAD_DIM), lambda r, h: (r, 0, kvh + h))],
        out_specs=[qspec, kspec],
        out_shape=[jax.ShapeDtypeStruct((dil, sd, d), BF16),
                   jax.ShapeDtypeStruct((dil, sd, kvh * LANES), F32)],
        compiler_params=_params("parallel", "parallel"),
    )(q, kv, kv)


def _attn_bwd(name, q, kv, do, lse, delta):
    dil, sd, d = q.shape
    group = d // HEAD_DIM // N_KV_HEADS
    gw = group * HEAD_DIM
    nblk = sd // ATT_BLOCK
    scale = 1.0 / math.sqrt(HEAD_DIM)
    nt = (((1,), (1,)), ((), ()))
    tn = (((0,), (0,)), ((), ()))

    rb = _residues_per_step(dil, nblk)

    def body(q_all, k_all, v_all, do_all, lse_all, dl_all, dq_all, dk_all, dv_all, dk_accs, dv_accs):
        dk_accs[...] = jnp.zeros_like(dk_accs)
        dv_accs[...] = jnp.zeros_like(dv_accs)
        for rr in range(rb):
            q_ref, k_ref, v_ref, do_ref, lse_ref, dl_ref, dq_ref, dk_ref, dv_ref, dk_acc, dv_acc = [
                ref.at[rr] for ref in (q_all, k_all, v_all, do_all, lse_all, dl_all, dq_all, dk_all, dv_all,
                                       dk_accs, dv_accs)]

            def step(nb, carry):
                rows = pl.ds(pl.multiple_of(nb * ATT_BLOCK, ATT_BLOCK), ATT_BLOCK)
                qs = _stack_heads(q_ref[rows, :], group)
                dos = _stack_heads(do_ref[rows, :], group)
                ls = _stack_cols(lse_ref[rows, :], group)
                dl = _stack_cols(dl_ref[rows, :], group)
                kw = _window(k_ref, nb)
                vw = _window(v_ref, nb)
                sc = lax.dot_general(qs, kw, nt, preferred_element_type=F32) * scale
                sc = jnp.where(_band_mask(nb, group), sc, -jnp.inf)
                p = jnp.exp(sc - ls)
                dp = lax.dot_general(dos, vw, nt, preferred_element_type=F32)
                ds = (p * (dp - dl) * scale).astype(BF16)
                dq = jnp.dot(ds, kw, preferred_element_type=F32)
                dq_ref[rows, :] = _unstack_heads(dq, group).astype(BF16)
                win = pl.ds(pl.multiple_of(nb * ATT_BLOCK, ATT_BLOCK), 2 * ATT_BLOCK)
                dk_acc[win, :] += lax.dot_general(ds, qs, tn, preferred_element_type=F32)
                dv_acc[win, :] += lax.dot_general(p.astype(BF16), dos, tn, preferred_element_type=F32)
                return carry

            lax.fori_loop(0, nblk, step, 0, unroll=min(2, nblk))
            dk_ref[...] = dk_acc[ATT_BLOCK:, :]
            dv_ref[...] = dv_acc[ATT_BLOCK:, :]

    kvh = N_KV_HEADS
    qspec = pl.BlockSpec((rb, sd, gw), lambda r, h: (r, 0, h))
    kspec = pl.BlockSpec((rb, sd, HEAD_DIM), lambda r, h: (r, 0, h))
    return pl.pallas_call(
        body, name=name, grid=(dil // rb, kvh),
        in_specs=[qspec, kspec, pl.BlockSpec((rb, sd, HEAD_DIM), lambda r, h: (r, 0, kvh + h)),
                  qspec, kspec, kspec],
        out_specs=[qspec, kspec, kspec],
        out_shape=[jax.ShapeDtypeStruct((dil, sd, d), BF16),
                   jax.ShapeDtypeStruct((dil, sd, kvh * HEAD_DIM), F32),
                   jax.ShapeDtypeStruct((dil, sd, kvh * HEAD_DIM), F32)],
        scratch_shapes=[pltpu.VMEM((rb, sd + ATT_BLOCK, HEAD_DIM), F32)] * 2,
        compiler_params=_params("parallel", "parallel"),
    )(q, kv, kv, do, lse, delta)


def _cast_bf16(name, w, layer, place):
    _, r, c = w.shape
    tr = min(512, r)

    def body(pl_ref, w_ref, o_ref):
        o_ref[...] = w_ref[...].astype(BF16)

    return pl.pallas_call(
        body, name=name,
        grid_spec=pltpu.PrefetchScalarGridSpec(
            num_scalar_prefetch=1, grid=(r // tr,),
            in_specs=[pl.BlockSpec((None, tr, c), lambda i, p: (layer, i, 0))],
            out_specs=pl.BlockSpec((None, tr, c), lambda i, p: (p[1], i, 0))),
        out_shape=jax.ShapeDtypeStruct((N_SHARD, r, c), BF16),
        compiler_params=_params("parallel"),
    )(place, w)


def _chip_sum(name, g, rh, place):
    _, r, c = g.shape
    rh2 = r // 2
    tr = min(512, rh2)
    nb = rh2 // tr

    def body(pl_ref, g_ref, rh_ref, o_ref):
        o_ref[...] = (g_ref[...].astype(F32) + rh_ref[...].astype(F32)).astype(BF16)

    return pl.pallas_call(
        body, name=name,
        grid_spec=pltpu.PrefetchScalarGridSpec(
            num_scalar_prefetch=1, grid=(N_SHARD, nb),
            in_specs=[pl.BlockSpec((None, tr, c), lambda s, i, p: (s, p[0] * nb + i, 0)),
                      pl.BlockSpec((None, tr, c), lambda s, i, p: (s, i, 0))],
            out_specs=pl.BlockSpec((None, tr, c), lambda s, i, p: (s, i, 0))),
        out_shape=jax.ShapeDtypeStruct((N_SHARD, rh2, c), BF16),
        compiler_params=_params("parallel", "parallel"),
    )(place, g, rh)


def _owner_sum(name, cs, rp, place):
    _, rh2, c = cs.shape
    tr = min(512, rh2)
    nb = rh2 // tr

    def body(pl_ref, cs_ref, r0_ref, r1_ref, r2_ref, o_ref):
        o_ref[...] = ((cs_ref[...].astype(F32) + r0_ref[...].astype(F32))
                      + (r1_ref[...].astype(F32) + r2_ref[...].astype(F32)))

    def rspec(j):
        return pl.BlockSpec((None, tr, c), lambda i, p: (j, i, 0))

    return pl.pallas_call(
        body, name=name,
        grid_spec=pltpu.PrefetchScalarGridSpec(
            num_scalar_prefetch=1, grid=(nb,),
            in_specs=[pl.BlockSpec((None, tr, c), lambda i, p: (p[1], i, 0)), rspec(0), rspec(1), rspec(2)],
            out_specs=pl.BlockSpec((tr, c), lambda i, p: (p[0] * nb + i, 0))),
        out_shape=jax.ShapeDtypeStruct((2 * rh2, c), F32),
        compiler_params=_params("parallel"),
    )(place, cs, rp, rp, rp)


def _adam_math(w, g, m, v):
    m = ADAM_B1 * m + (1.0 - ADAM_B1) * g
    v = ADAM_B2 * v + (1.0 - ADAM_B2) * (g * g)
    m_hat = m / (1.0 - ADAM_B1 ** ADAM_STEP)
    v_hat = v / (1.0 - ADAM_B2 ** ADAM_STEP)
    delta = -ADAM_LR * (m_hat / (jnp.sqrt(v_hat) + ADAM_EPS) + ADAM_WD * w)
    return delta, m, v


def _adamw(name, w, m, v, g, layer, partial=None):
    nl, r, c = w.shape
    tr = min(256, r)

    def body(w_ref, m_ref, v_ref, g_ref, *refs):
        go_ref, d_ref, mo_ref, vo_ref = refs[-4:]
        gv = g_ref[...]
        delta, m_new, v_new = _adam_math(w_ref[...], gv, m_ref[...], v_ref[...])
        go_ref[...] = gv
        d_ref[...] = delta
        mo_ref[...] = m_new
        vo_ref[...] = v_new

    wspec = pl.BlockSpec((None, tr, c), lambda i: (layer, i, 0))
    prev = [] if partial is None else list(partial)
    return pl.pallas_call(
        body, name=name, grid=(r // tr,),
        in_specs=[wspec] * 3 + [pl.BlockSpec((tr, c), lambda i: (i, 0))] + [ANY] * len(prev),
        out_specs=[wspec] * 4,
        out_shape=[jax.ShapeDtypeStruct((nl, r, c), F32)] * 4,
        input_output_aliases={4 + i: i for i in range(len(prev))},
        compiler_params=_params("parallel"),
    )(w, m, v, g, *prev)


def _adam_small(ws, ms, vs, gs):
    n = len(ws)

    def body(*refs):
        w_refs, m_refs, v_refs, g_refs = refs[:n], refs[n:2 * n], refs[2 * n:3 * n], refs[3 * n:4 * n]
        d_refs, mo_refs, vo_refs = refs[4 * n:5 * n], refs[5 * n:6 * n], refs[6 * n:7 * n]
        for i in range(n):
            delta, m_new, v_new = _adam_math(w_refs[i][...], g_refs[i][...], m_refs[i][...], v_refs[i][...])
            d_refs[i][...] = delta
            mo_refs[i][...] = m_new
            vo_refs[i][...] = v_new

    shapes = [jax.ShapeDtypeStruct(w.shape, F32) for w in ws]
    res = pl.pallas_call(body, name="adam_small", out_shape=shapes * 3)(*ws, *ms, *vs, *gs)
    return res[:n], res[n:2 * n], res[2 * n:]


def _pack_small(b_in, w_dw, b_dw, ln_g, ln_b, b_out, place):
    cin = b_in.shape[1]
    cd = b_dw.shape[1]
    rows = 8 + CONV_PAD

    def body(pl_ref, bi, wd, bd, lg, lb, bo, out):
        out[...] = jnp.zeros_like(out)
        out[0:1, :] = bi[...]
        out[1:2, 0:cd] = bd[...]
        out[1:2, cd:2 * cd] = lg[...]
        out[2:3, 0:cd] = lb[...]
        out[2:3, cd:2 * cd] = bo[...]
        out[8:8 + CONV_WIDTH, 0:cd] = wd[...]

    def whole(arr):
        return pl.BlockSpec(arr.shape, lambda i, p: (0,) * arr.ndim)

    ins = [b_in, w_dw, b_dw, ln_g, ln_b, b_out]
    return pl.pallas_call(
        body, name="pack_small",
        grid_spec=pltpu.PrefetchScalarGridSpec(
            num_scalar_prefetch=1, grid=(1,), in_specs=[whole(a) for a in ins],
            out_specs=pl.BlockSpec((None, rows, cin), lambda i, p: (p[1], 0, 0))),
        out_shape=jax.ShapeDtypeStruct((N_SHARD, rows, cin), F32),
        compiler_params=_params("arbitrary"),
    )(place, *ins)


def _place():
    x, y, c = lax.axis_index("x"), lax.axis_index("y"), lax.axis_index("c")
    return x, y, c


def _other_chips(x, y):
    return [(1 - x, y), (x, 1 - y), (1 - x, 1 - y)]


def _split_start(name, bufs, n_sem, copies, after=None):
    n = len(bufs)
    deps = [] if after is None else [after]

    def body(*refs):
        out0 = n + len(deps)
        for cp in copies(refs[:n], refs[out0], refs[out0 + 1], False):
            cp.start()
        refs[-1][...] = jnp.zeros_like(refs[-1])

    res = pl.pallas_call(
        body, name=name,
        out_shape=(pltpu.SemaphoreType.DMA((n_sem,)), pltpu.SemaphoreType.DMA((n_sem,)),
                   *[pltpu.HBM(b.shape, b.dtype) for b in bufs], jax.ShapeDtypeStruct((8, LANES), F32)),
        in_specs=[HBM] * n + [ANY] * len(deps),
        out_specs=(SEM, SEM, *[HBM] * n, pl.BlockSpec(memory_space=pltpu.VMEM)),
        input_output_aliases={i: 2 + i for i in range(n)},
        compiler_params=pltpu.CompilerParams(has_side_effects=SPLIT_EFFECT),
    )(*[pltpu.with_memory_space_constraint(b, pltpu.HBM) for b in bufs], *deps)
    return res[0], res[1], list(res[2:2 + n]), res[-1]


def _split_wait(name, handle, copies, after):
    ssem, rsem, bufs, _ = handle
    n = len(bufs)
    deps = list(after) if isinstance(after, (list, tuple)) else [after]

    def body(*refs):
        for cp in copies(refs[:n], refs[n], refs[n + 1], True):
            cp.wait_send()
            cp.wait_recv()

    res = pl.pallas_call(
        body, name=name,
        out_shape=[pltpu.HBM(b.shape, b.dtype) for b in bufs],
        in_specs=[HBM] * n + [SEM, SEM] + [ANY] * len(deps), out_specs=[HBM] * n,
        input_output_aliases={i: i for i in range(n)},
        compiler_params=pltpu.CompilerParams(has_side_effects=SPLIT_EFFECT),
    )(*bufs, ssem, rsem, *deps)
    return list(res)


def _remote(src, dst, ssem, rsem, k, to):
    return pltpu.make_async_remote_copy(src_ref=src, dst_ref=dst, send_sem=ssem.at[k], recv_sem=rsem.at[k],
                                        device_id=to, device_id_type=MESH)


def _gather_chips(x, y, c):
    nx, ny = x + (1 - c) - 2 * x * (1 - c), y + c - 2 * y * c
    fx, fy = x + c - 2 * x * c, y + (1 - c) - 2 * y * (1 - c)
    return (nx, ny), (fx, fy), 2 * nx + ny, 2 * fx + fy, 2 * (1 - x) + (1 - y)


def _direct_copies(refs, ssem, rsem, landing, n_whole=0):
    x, y, c = _place()
    me = 2 * x + y
    (nx, ny), _, near, _, _ = _gather_chips(x, y, c)
    n = len(refs) - n_whole
    cps = []
    for a, ref in enumerate(refs[:n]):
        cps.append(_remote(ref.at[me], ref.at[near if landing else me], ssem, rsem, a, (nx, ny, c)))
    for b, ref in enumerate(refs[n:]):
        for j, (px, py) in enumerate(_other_chips(x, y)):
            cps.append(_remote(ref.at[me], ref.at[2 * px + py if landing else me], ssem, rsem, n + 3 * b + j,
                               (px, py, c)))
    return cps


def _relay_copies(refs, ssem, rsem, landing):
    x, y, c = _place()
    _, (fx, fy), near, far, diag = _gather_chips(x, y, c)
    n = len(refs)
    cps = []
    for a, ref in enumerate(refs):
        rh = ref.shape[1] // 2
        rows = pl.ds(c * rh, rh)
        cps.append(_remote(ref.at[near, rows], ref.at[diag if landing else near, rows], ssem, rsem, a, (fx, fy, c)))
        cps.append(_remote(ref.at[near], ref.at[far if landing else near], ssem, rsem, n + a, (x, y, 1 - c)))
    return cps


def _diagonal_copies(refs, ssem, rsem, landing):
    x, y, c = _place()
    diag = 2 * (1 - x) + (1 - y)
    who = 1 - c if landing else c
    cps = []
    for a, ref in enumerate(refs):
        rh = ref.shape[1] // 2
        piece = ref.at[diag, pl.ds(who * rh, rh)]
        cps.append(_remote(piece, piece, ssem, rsem, a, (x, y, 1 - c)))
    return cps


def _sibling_copies(refs, ssem, rsem, landing):
    x, y, c = _place()
    n = len(refs) // 2
    cps = []
    for a in range(n):
        rh = refs[a].shape[1] // 2
        cps.append(_remote(refs[a].at[:, pl.ds((1 - c) * rh, rh), :], refs[n + a], ssem, rsem, a, (x, y, 1 - c)))
    return cps


def _owner_copies(refs, ssem, rsem, landing):
    x, y, c = _place()
    n = len(refs) // 2
    cps = []
    for a in range(n):
        for j, (px, py) in enumerate(_other_chips(x, y)):
            cps.append(_remote(refs[a].at[2 * px + py], refs[n + a].at[j], ssem, rsem, 3 * a + j, (px, py, c)))
    return cps


def _swap_copies(refs, ssem, rsem, landing):
    x, y, c = _place()
    who = 1 - c if landing else c
    cps = []
    for a, ref in enumerate(refs):
        rh = ref.shape[0] // 2
        rows = ref.at[pl.ds(who * rh, rh)]
        cps.append(_remote(rows, rows, ssem, rsem, a, (x, y, 1 - c)))
    return cps


def _small_copies(refs, ssem, rsem, landing):
    pack, slots = refs
    x, y, c = _place()
    cps = []
    for rel in range(1, N_DEV):
        px = 1 - x if (rel >> 2) & 1 else x
        py = 1 - y if (rel >> 1) & 1 else y
        pc = 1 - c if rel & 1 else c
        slot = 4 * px + 2 * py + pc if landing else 4 * x + 2 * y + c
        cps.append(_remote(pack, slots.at[slot], ssem, rsem, rel - 1, (px, py, pc)))
    return cps


def _small_pack(rows, w_dw_grad, d):
    n = len(rows)

    def body(*refs):
        pack = refs[-1]
        pack[...] = jnp.zeros_like(pack)
        for (r, _), ref in zip(rows, refs[:n]):
            pack[r:r + 1, :] = ref[...]
        pack[16:16 + CONV_PAD, :] = refs[n][...]

    return pl.pallas_call(body, name="small_pack", out_shape=jax.ShapeDtypeStruct((SMALL_ROWS, d), F32))(
        *[v for _, v in rows], w_dw_grad)


def _small_sum(pack, slots, place):
    rows, d = pack.shape
    loss_row = 12

    def body(pl_ref, pack_ref, slots_ref, out_ref):
        me = pl_ref[2]
        tot = jnp.where(me == 0, pack_ref[...], slots_ref[0])
        for i in range(1, N_DEV):
            tot = tot + jnp.where(me == i, pack_ref[...], slots_ref[i])
        out_ref[...] = tot
        out_ref[loss_row:loss_row + 1, :] = jnp.zeros((1, d), F32) + jnp.sum(tot[loss_row:loss_row + 1, :])

    return pl.pallas_call(
        body, name="small_sum",
        grid_spec=pltpu.PrefetchScalarGridSpec(
            num_scalar_prefetch=1, grid=(1,),
            in_specs=[pl.BlockSpec((rows, d), lambda i, p: (0, 0)), pl.BlockSpec((N_DEV, rows, d), lambda i, p: (0, 0, 0))],
            out_specs=pl.BlockSpec((rows, d), lambda i, p: (0, 0))),
        out_shape=jax.ShapeDtypeStruct((rows, d), F32),
        compiler_params=_params("arbitrary"),
    )(place, pack, slots)


def kernel(x, norm_mix, norm_mlp, conv_w_in, conv_b_in, conv_w_dw, conv_b_dw, conv_ln_g, conv_ln_b, conv_w_out, conv_b_out, kv_norm, w_kv, attn_w_q, attn_w_o, mlp_w_in, mlp_w_out, final_norm, loss_target, m_norm_mix, m_norm_mlp, m_conv_w_in, m_conv_b_in, m_conv_w_dw, m_conv_b_dw, m_conv_ln_g, m_conv_ln_b, m_conv_w_out, m_conv_b_out, m_kv_norm, m_w_kv, m_attn_w_q, m_attn_w_o, m_mlp_w_in, m_mlp_w_out, m_final_norm, v_norm_mix, v_norm_mlp, v_conv_w_in, v_conv_b_in, v_conv_w_dw, v_conv_b_dw, v_conv_ln_g, v_conv_ln_b, v_conv_w_out, v_conv_b_out, v_kv_norm, v_w_kv, v_attn_w_q, v_attn_w_o, v_mlp_w_in, v_mlp_w_out, v_final_norm):
    _, s, d = x.shape
    dff = mlp_w_in.shape[2] * N_SHARD
    kvw = w_kv.shape[1]
    nh = d // HEAD_DIM
    group = nh // N_KV_HEADS
    ds4 = d // N_SHARD
    xi, yi, ci = _place()
    me = 2 * xi + yi
    place = jnp.stack([ci, me, 2 * me + ci]).astype(I32)

    h0 = x.reshape(s, d)
    target = loss_target.reshape(s, d)
    tabs = _rope_tables(s)

    def gather_begin(tag, bufs, n_whole=0):
        plan = functools.partial(_direct_copies, n_whole=n_whole)
        return _split_start(f"gather_start_{tag}", bufs, len(bufs) + 2 * n_whole, plan), plan, n_whole

    def gather_land(tag, begun, later):
        handle, plan, n_whole = begun
        bufs = _split_wait(f"gather_wait_{tag}", handle, plan, later)
        n = len(bufs) - n_whole
        return _split_start(f"relay_start_{tag}", bufs[:n], 2 * n, _relay_copies), bufs[n:]

    def gather_end(tag, landed, later):
        relayed, whole = landed
        bufs = _split_wait(f"relay_wait_{tag}", relayed, _relay_copies, later)
        swapped = _split_start(f"diagonal_start_{tag}", bufs, len(bufs), _diagonal_copies)
        return _split_wait(f"diagonal_wait_{tag}", swapped, _diagonal_copies, later) + whole

    ag_cin = gather_begin("conv_in", [
        _cast_bf16("cast_w_in", conv_w_in, 0, place),
        _pack_small(conv_b_in, conv_w_dw.reshape(CONV_WIDTH, ds4), conv_b_dw, conv_ln_g, conv_ln_b, conv_b_out, place),
    ], n_whole=1)
    ag_cout = gather_begin("conv_out", [_cast_bf16("cast_w_out", conv_w_out, 0, place)])
    ag_mi0 = gather_begin("mlp_in0", [_cast_bf16("cast_mlp_in0", mlp_w_in, 0, place)])
    ag_mo0 = gather_begin("mlp_out0", [_cast_bf16("cast_mlp_out0", mlp_w_out, 0, place)])
    ag_attn = gather_begin("attn", [
        _cast_bf16("cast_w_kv", w_kv.reshape(1, ds4, kvw), 0, place), _cast_bf16("cast_w_q", attn_w_q, 0, place),
        _cast_bf16("cast_w_o", attn_w_o, 0, place)])
    ag_mi1 = gather_begin("mlp_in1", [_cast_bf16("cast_mlp_in1", mlp_w_in, 1, place)])
    ag_mo1 = gather_begin("mlp_out1", [_cast_bf16("cast_mlp_out1", mlp_w_out, 1, place)])

    wmi_g = [None, None]
    wmo_f = [None, None]

    nm = [norm_mix[0:1], norm_mix[1:2]]
    nmlp = [norm_mlp[0:1], norm_mlp[1:2]]
    kvn = kv_norm.reshape(1, d)
    fin = final_norm.reshape(1, d)
    started = sum(h[0][3][0:1, 0:1] for h in (ag_cin, ag_cout, ag_mi0, ag_mo0, ag_attn, ag_mi1, ag_mo1))
    (y0,) = _rms_fwd("rms_mix0", h0, [nm[0] + started])

    def tied(vec, landed):
        return vec + landed[0][3][0:1, 0:1]

    land_cin = gather_land("conv_in", ag_cin, y0)
    land_cout = gather_land("conv_out", ag_cout, land_cin[0][3])
    w_in_g, small_g = gather_end("conv_in", land_cin, land_cout[0][3])
    b_in_f = small_g[:, 0, :].reshape(1, 2 * d)
    b_dw_f = small_g[:, 1, 0:ds4].reshape(1, d)
    ln_g_f = small_g[:, 1, ds4:2 * ds4].reshape(1, d)
    ln_b_f = small_g[:, 2, 0:ds4].reshape(1, d)
    b_out_f = small_g[:, 2, ds4:2 * ds4].reshape(1, d)
    w_dw_f = jnp.transpose(small_g[:, 8:8 + CONV_PAD, 0:ds4], (1, 0, 2)).reshape(CONV_PAD, d)

    def ep_bias(acc, ex, outs, j):
        outs[0][...] = (acc + ex[0][...]).astype(outs[0].dtype)

    def ep_residual(acc, ex, outs, j):
        outs[0][...] = ex[0][...] + acc

    def ep_residual_bias(acc, ex, outs, j):
        outs[0][...] = ex[0][...] + (acc + ex[1][...])

    def ep_relu2(acc, ex, outs, j):
        r = jnp.maximum(acc, 0.0)
        outs[0][...] = r.astype(BF16)
        outs[1][...] = (r * r).astype(BF16)

    by_residue = [(BF16, ("residues", dil)) for dil in DILATIONS]

    def put_by_residue(val, outs, stage):
        _to_residues(val, stage, outs, DILATIONS)

    def ep_rope(acc, ex, outs, j, stage):
        put_by_residue(_rope_apply(acc, ex[0][...], ex[1][...], ex[2][...], 1.0), outs, stage)

    def ep_rope_k(acc, ex, outs, j, stage):
        roped = _rope_apply(acc, ex[0][...], ex[1][...], ex[2][...], 1.0)
        put_by_residue(jnp.where(j == 0, roped, acc), outs, stage)

    def ep_by_residue(acc, ex, outs, j, stage):
        put_by_residue(acc, outs, stage)

    tab_extras = [(t, "rows") for t in tabs]

    def mlp_fwd(idx, h, y, out_weight):
        r, r2 = _matmul(f"mlp_in{idx}", "nn", y, wmi_g[idx], b_kind="col", m=s, n=dff, k=d,
                        outs=[(BF16, "plain"), (BF16, "plain")], epilogue=ep_relu2)
        wmo_f[idx] = out_weight(r2).reshape(dff, d)
        (h_new,) = _matmul(f"mlp_out{idx}", "nn", r2, wmo_f[idx], m=s, n=d, k=dff,
                           outs=[(F32, "plain")], extras=[(h, "ij")], epilogue=ep_residual)
        return h_new, r, r2

    (u,) = _matmul("conv_in", "nn", y0, w_in_g, b_kind="col", m=s, n=2 * d, k=d,
                   outs=[(BF16, "plain")], extras=[(b_in_f, "vec")], epilogue=ep_bias)
    land_mi0 = gather_land("mlp_in0", ag_mi0, u)
    cpre = _dwconv_fwd(u, w_dw_f, tied(b_dw_f, land_mi0))
    sact = _ln_silu_fwd(cpre, ln_g_f, ln_b_f)
    (w_out_g,) = gather_end("conv_out", land_cout, sact)
    w_out_f = w_out_g.reshape(d, d)
    (h1,) = _matmul("conv_out", "nn", sact, w_out_f, m=s, n=d, k=d,
                    outs=[(F32, "plain")], extras=[(h0, "ij"), (b_out_f, "vec")], epilogue=ep_residual_bias)
    land_mo0 = gather_land("mlp_out0", ag_mo0, h1)
    (y1,) = _rms_fwd("rms_mlp0", h1, [tied(nmlp[0], land_mo0)])
    (wmi_g[0],) = gather_end("mlp_in0", land_mi0, y1)
    land_attn = None

    def out_weight0(r2):
        nonlocal land_attn
        land_attn = gather_land("attn", ag_attn, r2)
        return gather_end("mlp_out0", land_mo0, land_attn[0][3])[0]

    h2, r0, r0sq = mlp_fwd(0, h1, y1, out_weight0)
    land_mi1 = gather_land("mlp_in1", ag_mi1, h2)
    ykv, y2 = _rms_fwd("rms_kv_mix1", h2, [tied(kvn, land_mi1), nm[1]])
    wkv_g, wq_g, wo_g = gather_end("attn", land_attn, y2)
    wkv_f, wq_f, wo_f = wkv_g.reshape(d, kvw), wq_g.reshape(d, d), wo_g.reshape(d, d)
    kv_parts = _matmul("kv_proj", "nn", ykv, wkv_f, m=s, n=kvw, k=d, tn=kvw // 2,
                       outs=by_residue, extras=tab_extras, epilogue=ep_rope_k, stage=True)
    q_parts = _matmul("q_proj", "nn", y2, wq_f, m=s, n=d, k=d,
                      outs=by_residue, extras=tab_extras, epilogue=ep_rope, stage=True)
    o_parts, lse_parts = [], []
    for dil, q_b, kv_b in zip(DILATIONS, q_parts, kv_parts):
        o_b, lse_b = _attn_fwd(f"attn_fwd_d{dil}", q_b, kv_b)
        o_parts.append(o_b)
        lse_parts.append(lse_b)
    o, lse = _attn_combine(o_parts, lse_parts)
    land_mo1 = gather_land("mlp_out1", ag_mo1, o)
    (h3,) = _matmul("attn_out", "nn", o, wo_f, m=s, n=d, k=d,
                    outs=[(F32, "plain")], extras=[(h2, "ij")], epilogue=ep_residual)
    (y3,) = _rms_fwd("rms_mlp1", h3, [tied(nmlp[1], land_mo1)])
    (wmi_g[1],) = gather_end("mlp_in1", land_mi1, y3)
    h4, r1, r1sq = mlp_fwd(1, h3, y3, lambda r2: gather_end("mlp_out1", land_mo1, r2)[0])
    dh4, dh4b, d_fin, loss_cols = _final_loss(h4, fin, target)

    def ep_relu2_bwd(acc, ex, outs, j):
        outs[0][...] = (acc * (2.0 * ex[0][...].astype(F32))).astype(BF16)

    def mlp_bwd(idx, dhb, y, r, r2):
        (dz,) = _matmul(f"mlp_out{idx}_dx", "nt", dhb, wmo_f[idx], m=s, n=dff, k=d,
                        outs=[(BF16, "plain")], extras=[(r, "ij")], epilogue=ep_relu2_bwd)
        (dwo,) = _matmul(f"mlp_out{idx}_dw", "tn", r2, dhb, m=dff, n=d, k=s,
                         outs=[(BF16, "plain")])
        (dy,) = _matmul(f"mlp_in{idx}_dx", "nt", dz, wmi_g[idx], b_kind="col", m=s, n=d, k=dff,
                        outs=[(BF16, "plain")])
        (dwi,) = _matmul(f"mlp_in{idx}_dw", "tn", y, dz, m=d, n=dff, k=s,
                         outs=[(BF16, "col")])
        return dy, dwi, dwo.reshape(N_SHARD, dff // N_SHARD, d)

    def token(handle):
        return handle[3][0:1, 0:1]

    def rs_exchange(tag, grads):
        lands = [lax.empty((N_SHARD, g.shape[1] // 2, g.shape[2]), g.dtype) for g in grads]
        return _split_start(f"sibling_start_{tag}", list(grads) + lands, len(grads), _sibling_copies)

    def rs_send(tag, names, exchanged, later):
        bufs = _split_wait(f"sibling_wait_{tag}", exchanged, _sibling_copies, later)
        n = len(names)
        sums = [_chip_sum(f"chip_sum_{nme}", g, rh, place) for nme, g, rh in zip(names, bufs[:n], bufs[n:])]
        lands = [lax.empty((N_SHARD - 1,) + cs.shape[1:], cs.dtype) for cs in sums]
        return _split_start(f"owners_start_{tag}", sums + lands, 3 * n, _owner_copies)

    def rs_sum(tag, names, sent, later):
        bufs = _split_wait(f"owners_wait_{tag}", sent, _owner_copies, later)
        n = len(names)
        own = [_owner_sum(f"owner_sum_{nme}", cs, rp, place) for nme, cs, rp in zip(names, bufs[:n], bufs[n:])]
        return _split_start(f"swap_start_{tag}", own, n, _swap_copies)

    def rs_end(tag, swapped, later):
        return _split_wait(f"swap_wait_{tag}", swapped, _swap_copies, later)

    dy3, g_wmi1, g_wmo1 = mlp_bwd(1, dh4b, y3, r1, r1sq)
    x_mlp1 = rs_exchange("mlp1", [g_wmi1, g_wmo1])
    dh3, dh3b, d_nmlp1 = _rms_bwd("rms_mlp1_bwd", h3, [(nmlp[1] + token(x_mlp1), dy3)], dh4)

    do_parts = _matmul("attn_out_dx", "nt", dh3b, wo_f, m=s, n=d, k=d, outs=by_residue, epilogue=ep_by_residue,
                       stage=True)
    (g_wo,) = _matmul("attn_out_dw", "tn", o, dh3b, m=d, n=d, k=s, outs=[(BF16, "plain")])
    rs_mlp1 = rs_send("mlp1", ["mlp_in1", "mlp_out1"], x_mlp1, g_wo)
    lse_res, delta_res = _attn_delta(do_parts[0].reshape(s, d), o, lse, DILATIONS)
    dq_parts, dk_parts, dv_parts = [], [], []
    for dil, q_b, kv_b, do_b, lse_b, dl_b in zip(DILATIONS, q_parts, kv_parts, do_parts, lse_res, delta_res):
        dq_b, dk_b, dv_b = _attn_bwd(f"attn_bwd_d{dil}", q_b, kv_b, do_b, lse_b, dl_b)
        dq_parts.append(dq_b)
        dk_parts.append(dk_b)
        dv_parts.append(dv_b)
    dq = _residue_sum("rope_bwd_q", [(dq_parts, True)], tabs)
    dkv = _residue_sum("rope_bwd_kv", [(dk_parts, True), (dv_parts, False)], tabs)
    (g_wq,) = _matmul("q_proj_dw", "tn", y2, dq, m=d, n=d, k=s, outs=[(BF16, "plain")])
    (dy2,) = _matmul("q_proj_dx", "nt", dq, wq_f, m=s, n=d, k=d, outs=[(BF16, "plain")])
    (g_wkv,) = _matmul("kv_proj_dw", "tn", ykv, dkv, m=d, n=kvw, k=s, outs=[(BF16, "plain")])
    (dykv,) = _matmul("kv_proj_dx", "nt", dkv, wkv_f, m=s, n=d, k=kvw, outs=[(BF16, "plain")])
    x_attn = rs_exchange("attn", [g_wkv.reshape(N_SHARD, ds4, kvw), g_wq.reshape(N_SHARD, ds4, d),
                                  g_wo.reshape(N_SHARD, ds4, d)])
    dh2, dh2b, d_nm1, d_kvn = _rms_bwd("rms_kv_mix1_bwd", h2, [(nm[1] + token(x_attn), dy2), (kvn, dykv)], dh3)
    rs_attn = rs_send("attn", ["w_kv", "w_q", "w_o"], x_attn, dh2b)

    dy1, g_wmi0, g_wmo0 = mlp_bwd(0, dh2b, y1, r0, r0sq)
    x_mlp0 = rs_exchange("mlp0", [g_wmi0, g_wmo0])
    dh1, dh1b, d_nmlp0, d_b_out = _rms_bwd("rms_mlp0_bwd", h1, [(nmlp[0] + token(x_mlp0) + token(rs_attn), dy1)],
                                           dh2, want_colsum=True)

    (dsact,) = _matmul("conv_out_dx", "nt", dh1b, w_out_f, m=s, n=d, k=d, outs=[(BF16, "plain")])
    (g_wout,) = _matmul("conv_out_dw", "tn", sact, dh1b, m=d, n=d, k=s, outs=[(BF16, "plain")])
    rs_mlp0 = rs_send("mlp0", ["mlp_in0", "mlp_out0"], x_mlp0, g_wout)
    dc, d_ln_g, d_ln_b, d_b_dw = _ln_silu_bwd(cpre, ln_g_f + token(rs_mlp0), ln_b_f, dsact)
    du, d_w_dw, d_b_in_a, d_b_in_g = _dwconv_bwd(u, w_dw_f, dc)
    (g_win,) = _matmul("conv_in_dw", "tn", y0, du, b_kind="col", m=d, n=2 * d, k=s, outs=[(BF16, "col")])
    x_conv = rs_exchange("conv", [g_win, g_wout.reshape(N_SHARD, ds4, d)])
    (dy0,) = _matmul("conv_in_dx", "nt", du, w_in_g, a_kind="col", b_kind="col", m=s, n=d, k=2 * d,
                     outs=[(BF16, "plain")])
    dx, _, d_nm0 = _rms_bwd("rms_mix0_bwd", h0, [(nm[0] + token(x_conv), dy0)], dh1)

    small_rows = [(0, d_nm0), (1, d_nm1), (2, d_nmlp0), (3, d_nmlp1), (4, d_kvn), (5, d_fin), (6, d_b_dw),
                  (7, d_ln_g), (8, d_ln_b), (9, d_b_out), (10, d_b_in_a), (11, d_b_in_g), (12, loss_cols)]
    x_small = _split_start("small_start", [_small_pack(small_rows, d_w_dw, d),
                                           lax.empty((N_DEV, SMALL_ROWS, d), F32)], N_DEV - 1, _small_copies)
    rs_conv = rs_send("conv", ["w_in", "w_out"], x_conv, x_small[3])

    def big(name, w, m, v, g, layer=0, partial=None):
        shape = w.shape
        w3, m3, v3 = [t.reshape((-1,) + shape[-2:]) for t in (w, m, v)]
        if partial is not None:
            partial = [t.reshape(w3.shape) for t in partial]
        res = _adamw(name, w3, m3, v3, g, layer, partial)
        return [t.reshape(shape) for t in res]

    sw_mlp1 = rs_sum("mlp1", ["mlp_in1", "mlp_out1"], rs_mlp1, rs_conv[3])
    sw_attn = rs_sum("attn", ["w_kv", "w_q", "w_o"], rs_attn, sw_mlp1[3])
    f_wmi1, f_wmo1 = rs_end("mlp1", sw_mlp1, sw_attn[3])
    p_wmi = big("adam_mlp_in1", mlp_w_in, m_mlp_w_in, v_mlp_w_in, f_wmi1, 1)
    p_wmo = big("adam_mlp_out1", mlp_w_out, m_mlp_w_out, v_mlp_w_out, f_wmo1, 1)
    sw_mlp0 = rs_sum("mlp0", ["mlp_in0", "mlp_out0"], rs_mlp0, [p_wmi[0], p_wmo[0]])
    f_wkv, f_wq, f_wo = rs_end("attn", sw_attn, sw_mlp0[3])
    r_wkv = big("adam_w_kv", w_kv, m_w_kv, v_w_kv, f_wkv)
    r_wq = big("adam_w_q", attn_w_q, m_attn_w_q, v_attn_w_q, f_wq)
    r_wo = big("adam_w_o", attn_w_o, m_attn_w_o, v_attn_w_o, f_wo)
    sw_conv = rs_sum("conv", ["w_in", "w_out"], rs_conv, [r_wkv[0], r_wq[0], r_wo[0]])
    f_wmi0, f_wmo0 = rs_end("mlp0", sw_mlp0, sw_conv[3])
    r_wmi = big("adam_mlp_in0", mlp_w_in, m_mlp_w_in, v_mlp_w_in, f_wmi0, 0, p_wmi)
    r_wmo = big("adam_mlp_out0", mlp_w_out, m_mlp_w_out, v_mlp_w_out, f_wmo0, 0, p_wmo)
    f_win, f_wout = rs_end("conv", sw_conv, [r_wmi[0], r_wmo[0]])
    r_win = big("adam_w_in", conv_w_in, m_conv_w_in, v_conv_w_in, f_win)
    r_wout = big("adam_w_out", conv_w_out, m_conv_w_out, v_conv_w_out, f_wout)

    small_pack, small_slots = _split_wait("small_wait", x_small, _small_copies, r_wout[0])
    red = _small_sum(small_pack, small_slots, place)
    loss = red[12, 0]
    g_norm_mix = red[0:2]
    g_norm_mlp = red[2:4]
    g_kv_norm = red[4:5]
    g_final = red[5:6]

    def my_cols(row):
        return lax.dynamic_slice(red, (row, me * ds4), (1, ds4))

    g_b_dw, g_ln_g, g_ln_b, g_b_out = my_cols(6), my_cols(7), my_cols(8), my_cols(9)
    half_in = 2 * d // N_SHARD
    b_in_row = 10 + me // 2
    g_b_in = lax.dynamic_slice(red, (b_in_row, (me % 2) * half_in), (1, half_in))
    g_w_dw = lax.dynamic_slice(red, (16, me * ds4), (CONV_WIDTH, ds4))

    sm_w =[norm_mix, norm_mlp, conv_b_in, conv_w_dw.reshape(CONV_WIDTH, ds4), conv_b_dw, conv_ln_g, conv_ln_b,
            conv_b_out, kv_norm.reshape(1, d), final_norm.reshape(1, d)]
    sm_m = [m_norm_mix, m_norm_mlp, m_conv_b_in, m_conv_w_dw.reshape(CONV_WIDTH, ds4), m_conv_b_dw, m_conv_ln_g,
            m_conv_ln_b, m_conv_b_out, m_kv_norm.reshape(1, d), m_final_norm.reshape(1, d)]
    sm_v = [v_norm_mix, v_norm_mlp, v_conv_b_in, v_conv_w_dw.reshape(CONV_WIDTH, ds4), v_conv_b_dw, v_conv_ln_g,
            v_conv_ln_b, v_conv_b_out, v_kv_norm.reshape(1, d), v_final_norm.reshape(1, d)]
    sm_g = [g_norm_mix, g_norm_mlp, g_b_in, g_w_dw, g_b_dw, g_ln_g, g_ln_b, g_b_out, g_kv_norm, g_final]
    sm_d, sm_nm, sm_nv = _adam_small(sm_w, sm_m, sm_v, sm_g)
    shapes = [norm_mix.shape, norm_mlp.shape, conv_b_in.shape, conv_w_dw.shape, conv_b_dw.shape, conv_ln_g.shape,
              conv_ln_b.shape, conv_b_out.shape, kv_norm.shape, final_norm.shape]
    sm_g, sm_d, sm_nm, sm_nv = [[t.reshape(sh) for t, sh in zip(lst, shapes)] for lst in (sm_g, sm_d, sm_nm, sm_nv)]

    def order(sm, idx):
        return [sm[0], sm[1], r_win[idx], sm[2], sm[3], sm[4], sm[5], sm[6], r_wout[idx], sm[7], sm[8],
                r_wkv[idx], r_wq[idx], r_wo[idx], r_wmi[idx], r_wmo[idx], sm[9]]

    return (loss, dx.reshape(x.shape), *order(sm_g, 0), *order(sm_d, 1), *order(sm_nm, 2), *order(sm_nv, 3))
```

```python
import functools
import math

import jax
import jax.numpy as jnp
from jax import lax
from jax.experimental import pallas as pl
from jax.experimental.pallas import tpu as pltpu

F32 = jnp.float32
BF16 = jnp.bfloat16
I32 = jnp.int32

NORM_EPS = 1e-6
LN_EPS = 1e-5
HEAD_DIM = 128
N_KV_HEADS = 4
ROT_DIM = 32
ROPE_THETA = 500000.0
CONV_WIDTH = 31
CONV_PAD = 32
ATT_BLOCK = 128
ATT_STEP_BLOCKS = 8
DILATIONS = (1, 4, 16)
ADAM_LR = 0.001
ADAM_B1 = 0.9
ADAM_B2 = 0.999
ADAM_EPS = 1e-08
ADAM_WD = 0.01
ADAM_STEP = 10
N_SHARD = 4
N_DEV = 8
LANES = 128
VMEM_LIMIT = 48 * 1024 * 1024
MM_TM, MM_TN, MM_TK = 1024, 1024, 2048
ROW_TILE = 256
CONV_CB = 128
CONV_T = 128
SMALL_ROWS = 48
MESH = pl.DeviceIdType.MESH
ANY = pl.BlockSpec(memory_space=pl.ANY)
HBM = pl.BlockSpec(memory_space=pltpu.HBM)
SEM = pl.BlockSpec(memory_space=pltpu.SEMAPHORE)
SPLIT_EFFECT = pltpu.SideEffectType.DATAFLOW_SIDE_EFFECTING


def _params(*sem):
    return pltpu.CompilerParams(dimension_semantics=sem, vmem_limit_bytes=VMEM_LIMIT)


def _sigmoid(x):
    return 1.0 / (1.0 + jnp.exp(-x))


def _wspec(kind, arr_shape, br, bc, pick):
    if kind == "plain":
        return pl.BlockSpec((br, bc), pick)
    per = arr_shape[2] // bc

    def idx(*g):
        rb, cb = pick(*g)
        return (cb // per, rb, cb % per)

    return pl.BlockSpec((None, br, bc), idx)


def _stage_shape(rows, w):
    return (w // LANES, rows, LANES)


def _to_residues(val, stage_ref, out_refs, dils):
    planes, rows, _ = stage_ref.shape
    for c in range(planes):
        stage_ref[c] = val[:, c * LANES:(c + 1) * LANES]
    for out_ref, dil in zip(out_refs, dils):
        if dil == 1:
            out_ref[0] = val.astype(out_ref.dtype)
            continue
        for r in range(dil):
            for c in range(planes):
                out_ref[r, :, c * LANES:(c + 1) * LANES] = stage_ref.at[c][pl.ds(r, rows // dil, stride=dil), :].astype(
                    out_ref.dtype)


def _from_residues(src_ref, stage_ref, dil):
    planes, rows, _ = stage_ref.shape
    if dil == 1:
        return lambda c: src_ref[0, :, c * LANES:(c + 1) * LANES].astype(F32)
    for r in range(dil):
        for c in range(planes):
            stage_ref.at[c][pl.ds(r, rows // dil, stride=dil), :] = src_ref[r, :, c * LANES:(c + 1) * LANES].astype(F32)
    return lambda c: stage_ref[c]


def _matmul(name, mode, a, b, *, m, n, k, tn=MM_TN, a_kind="plain", b_kind="plain", outs, extras=(), epilogue=None,
            stage=False):
    tm, tn, tk = min(MM_TM, m), min(tn, n), min(MM_TK, k)
    if b_kind == "col" and mode in ("nn", "tn"):
        tn = min(tn, n // b.shape[0])
    if b_kind == "col" and mode == "nt":
        tk = min(tk, k // b.shape[0])
    if a_kind == "col":
        assert mode == "nt"
        tk = min(tk, k // a.shape[0])
    if any(kind == "col" for _, kind in outs):
        tn = min(tn, n // N_SHARD)
    assert m % tm == 0 and n % tn == 0 and k % tk == 0, (name, m, n, k, tm, tn, tk)
    nk = k // tk
    grid = (m // tm, n // tn, nk)
    if mode == "nn":
        a_spec = pl.BlockSpec((tm, tk), lambda i, j, kk: (i, kk))
        b_spec = _wspec(b_kind, b.shape, tk, tn, lambda i, j, kk: (kk, j))
        dims = (((1,), (0,)), ((), ()))
    elif mode == "nt":
        a_spec = _wspec(a_kind, a.shape, tm, tk, lambda i, j, kk: (i, kk))
        b_spec = _wspec(b_kind, b.shape, tn, tk, lambda i, j, kk: (j, kk))
        dims = (((1,), (1,)), ((), ()))
    else:
        a_spec = pl.BlockSpec((tk, tm), lambda i, j, kk: (kk, i))
        b_spec = _wspec(b_kind, b.shape, tk, tn, lambda i, j, kk: (kk, j))
        dims = (((0,), (0,)), ((), ()))
    out_shape, out_specs = [], []
    for dtype, kind in outs:
        if isinstance(kind, tuple):
            dil = kind[1]
            out_shape.append(jax.ShapeDtypeStruct((dil, m // dil, n), dtype))
            out_specs.append(pl.BlockSpec((dil, tm // dil, tn), lambda i, j, kk: (0, i, j)))
            continue
        shape = (m, n) if kind == "plain" else (N_SHARD, m, n // N_SHARD)
        out_shape.append(jax.ShapeDtypeStruct(shape, dtype))
        out_specs.append(_wspec(kind, shape, tm, tn, lambda i, j, kk: (i, j)))
    n_ex = len(extras)
    ex_specs = {"ij": pl.BlockSpec((tm, tn), lambda i, j, kk: (i, j)),
                "vec": pl.BlockSpec((1, tn), lambda i, j, kk: (0, j)),
                "rows": pl.BlockSpec((tm, LANES), lambda i, j, kk: (i, 0))}

    def body(*refs):
        a_ref, b_ref = refs[0], refs[1]
        ex_refs = refs[2:2 + n_ex]
        out_refs = refs[2 + n_ex:2 + n_ex + len(outs)]
        j = pl.program_id(1)

        def finish(res):
            if epilogue is None:
                out_refs[0][...] = res.astype(out_refs[0].dtype)
            elif stage:
                epilogue(res, ex_refs, out_refs, j, refs[-1])
            else:
                epilogue(res, ex_refs, out_refs, j)

        prod = lax.dot_general(a_ref[...], b_ref[...], dims, preferred_element_type=F32)
        if nk == 1:
            finish(prod)
            return
        acc_ref = refs[2 + n_ex + len(outs)]
        kk = pl.program_id(2)

        @pl.when(kk == 0)
        def _():
            acc_ref[...] = prod

        @pl.when(kk > 0)
        def _():
            acc_ref[...] += prod

        @pl.when(kk == nk - 1)
        def _():
            finish(acc_ref[...])

    res = pl.pallas_call(
        body, name=name, grid=grid,
        in_specs=[a_spec, b_spec] + [ex_specs[how] for _, how in extras],
        out_specs=out_specs, out_shape=out_shape,
        scratch_shapes=[pltpu.VMEM((tm, tn), F32)] * (nk > 1) + [pltpu.VMEM(_stage_shape(tm, tn), F32)] * bool(stage),
        compiler_params=_params("parallel", "parallel", "arbitrary"),
    )(a, b, *[e for e, _ in extras])
    return res


def _rope_tables(seq):
    half = ROT_DIM // 2
    pos = jnp.arange(seq, dtype=F32)
    inv = ROPE_THETA ** (-jnp.arange(0, ROT_DIM, 2, dtype=F32) / ROT_DIM)
    ang = pos[:, None] * inv[None, :]
    cos, sin = jnp.cos(ang), jnp.sin(ang)
    zeros = jnp.zeros((seq, HEAD_DIM - ROT_DIM), F32)
    ctab = jnp.concatenate([cos, cos, zeros + 1.0], axis=1)
    atab = jnp.concatenate([-sin, jnp.zeros((seq, half), F32), zeros], axis=1)
    btab = jnp.concatenate([jnp.zeros((seq, half), F32), sin, zeros], axis=1)
    return ctab, atab, btab


def _rope_apply(x, ctab, atab, btab, sign):
    w = x.shape[1]
    reps = w // HEAD_DIM
    half = ROT_DIM // 2
    c = jnp.tile(ctab, (1, reps))
    a = jnp.tile(atab, (1, reps))
    b = jnp.tile(btab, (1, reps))
    up = pltpu.roll(x, w - half, 1)
    down = pltpu.roll(x, half, 1)
    return x * c + sign * (up * a + down * b)


def _rows(t, w):
    return pl.BlockSpec((t, w), lambda i: (i, 0))


def _fixed(shape):
    nd = len(shape)
    return pl.BlockSpec(shape, lambda i: (0,) * nd)


def _rms_fwd(name, x, gains):
    s, d = x.shape
    t = min(ROW_TILE, s)
    ng = len(gains)

    def body(x_ref, *refs):
        xv = x_ref[...]
        r = lax.rsqrt(jnp.mean(xv * xv, axis=-1, keepdims=True) + NORM_EPS)
        xn = xv * r
        for g_ref, y_ref in zip(refs[:ng], refs[ng:]):
            y_ref[...] = (xn * g_ref[...]).astype(BF16)

    return pl.pallas_call(
        body, name=name, grid=(s // t,),
        in_specs=[_rows(t, d)] + [_fixed((1, d))] * ng,
        out_specs=[_rows(t, d)] * ng,
        out_shape=[jax.ShapeDtypeStruct((s, d), BF16)] * ng,
        compiler_params=_params("parallel"),
    )(x, *gains)


def _rms_bwd(name, x, pairs, dh_in, want_colsum=False):
    s, d = x.shape
    t = min(ROW_TILE, s)
    n_p = len(pairs)

    def body(x_ref, dh_ref, *refs):
        g_refs = refs[:n_p]
        dy_refs = refs[n_p:2 * n_p]
        dh_out, dhb_out = refs[2 * n_p], refs[2 * n_p + 1]
        dg_refs = refs[2 * n_p + 2:2 * n_p + 2 + n_p]
        cs_ref = refs[-1] if want_colsum else None
        i = pl.program_id(0)
        xv = x_ref[...]
        r = lax.rsqrt(jnp.mean(xv * xv, axis=-1, keepdims=True) + NORM_EPS)
        xn = xv * r
        dh = dh_ref[...]
        for g_ref, dy_ref, dg_ref in zip(g_refs, dy_refs, dg_refs):
            dy = dy_ref[...].astype(F32)
            u = dy * g_ref[...]
            dh = dh + r * (u - xn * jnp.mean(u * xn, axis=-1, keepdims=True))
            part = jnp.sum(dy * xn, axis=0, keepdims=True)

            @pl.when(i == 0)
            def _():
                dg_ref[...] = part

            @pl.when(i > 0)
            def _():
                dg_ref[...] += part

        dh_out[...] = dh
        dhb_out[...] = dh.astype(BF16)
        if want_colsum:
            col = jnp.sum(dh, axis=0, keepdims=True)

            @pl.when(i == 0)
            def _():
                cs_ref[...] = col

            @pl.when(i > 0)
            def _():
                cs_ref[...] += col

    n_vec = n_p + (1 if want_colsum else 0)
    return pl.pallas_call(
        body, name=name, grid=(s // t,),
        in_specs=[_rows(t, d), _rows(t, d)] + [_fixed((1, d))] * n_p + [_rows(t, d)] * n_p,
        out_specs=[_rows(t, d), _rows(t, d)] + [_fixed((1, d))] * n_vec,
        out_shape=[jax.ShapeDtypeStruct((s, d), F32), jax.ShapeDtypeStruct((s, d), BF16)]
        + [jax.ShapeDtypeStruct((1, d), F32)] * n_vec,
        compiler_params=_params("arbitrary"),
    )(x, dh_in, *[g for g, _ in pairs], *[dy for _, dy in pairs])


def _final_loss(x, g, target):
    s, d = x.shape
    t = min(ROW_TILE, s)

    def body(x_ref, g_ref, t_ref, dh_out, dhb_out, dg_ref, loss_ref):
        i = pl.program_id(0)
        xv = x_ref[...]
        gv = g_ref[...]
        r = lax.rsqrt(jnp.mean(xv * xv, axis=-1, keepdims=True) + NORM_EPS)
        xn = xv * r
        diff = xn * gv - t_ref[...]
        dy = diff / d
        u = dy * gv
        dh = r * (u - xn * jnp.mean(u * xn, axis=-1, keepdims=True))
        dh_out[...] = dh
        dhb_out[...] = dh.astype(BF16)
        dg = jnp.sum(dy * xn, axis=0, keepdims=True)
        lc = jnp.sum(0.5 * diff * dy, axis=0, keepdims=True)

        @pl.when(i == 0)
        def _():
            dg_ref[...] = dg
            loss_ref[...] = lc

        @pl.when(i > 0)
        def _():
            dg_ref[...] += dg
            loss_ref[...] += lc

    return pl.pallas_call(
        body, name="final_loss", grid=(s // t,),
        in_specs=[_rows(t, d), _fixed((1, d)), _rows(t, d)],
        out_specs=[_rows(t, d), _rows(t, d), _fixed((1, d)), _fixed((1, d))],
        out_shape=[jax.ShapeDtypeStruct((s, d), F32), jax.ShapeDtypeStruct((s, d), BF16),
                   jax.ShapeDtypeStruct((1, d), F32), jax.ShapeDtypeStruct((1, d), F32)],
        compiler_params=_params("arbitrary"),
    )(x, g, target)


def _ln_silu_fwd(c, g, b):
    s, d = c.shape
    t = min(ROW_TILE, s)

    def body(c_ref, g_ref, b_ref, s_ref):
        cv = c_ref[...]
        mu = jnp.mean(cv, axis=-1, keepdims=True)
        xc = cv - mu
        rs = lax.rsqrt(jnp.mean(xc * xc, axis=-1, keepdims=True) + LN_EPS)
        ln = xc * rs * g_ref[...] + b_ref[...]
        s_ref[...] = (ln * _sigmoid(ln)).astype(BF16)

    return pl.pallas_call(
        body, name="ln_silu_fwd", grid=(s // t,),
        in_specs=[_rows(t, d), _fixed((1, d)), _fixed((1, d))],
        out_specs=_rows(t, d), out_shape=jax.ShapeDtypeStruct((s, d), BF16),
        compiler_params=_params("parallel"),
    )(c, g, b)


def _ln_silu_bwd(c, g, b, ds):
    s, d = c.shape
    t = min(ROW_TILE, s)

    def body(c_ref, g_ref, b_ref, ds_ref, dc_ref, dg_ref, db_ref, dbdw_ref):
        i = pl.program_id(0)
        cv = c_ref[...]
        gv = g_ref[...]
        mu = jnp.mean(cv, axis=-1, keepdims=True)
        xc = cv - mu
        rs = lax.rsqrt(jnp.mean(xc * xc, axis=-1, keepdims=True) + LN_EPS)
        nrm = xc * rs
        ln = nrm * gv + b_ref[...]
        sig = _sigmoid(ln)
        dln = ds_ref[...].astype(F32) * sig * (1.0 + ln * (1.0 - sig))
        dn = dln * gv
        dc = rs * (dn - jnp.mean(dn, axis=-1, keepdims=True)
                   - nrm * jnp.mean(dn * nrm, axis=-1, keepdims=True))
        dc_ref[...] = dc
        pg = jnp.sum(dln * nrm, axis=0, keepdims=True)
        pb = jnp.sum(dln, axis=0, keepdims=True)
        pc = jnp.sum(dc, axis=0, keepdims=True)

        @pl.when(i == 0)
        def _():
            dg_ref[...] = pg
            db_ref[...] = pb
            dbdw_ref[...] = pc

        @pl.when(i > 0)
        def _():
            dg_ref[...] += pg
            db_ref[...] += pb
            dbdw_ref[...] += pc

    return pl.pallas_call(
        body, name="ln_silu_bwd", grid=(s // t,),
        in_specs=[_rows(t, d), _fixed((1, d)), _fixed((1, d)), _rows(t, d)],
        out_specs=[_rows(t, d)] + [_fixed((1, d))] * 3,
        out_shape=[jax.ShapeDtypeStruct((s, d), F32)] + [jax.ShapeDtypeStruct((1, d), F32)] * 3,
        compiler_params=_params("arbitrary"),
    )(c, g, b, ds)


def _residue_spec(dil, t, w):
    return pl.BlockSpec((dil, t // dil, w), lambda i: (0, i, 0))


def _attn_combine(o_list, lse_list):
    dil0, sd0, d = o_list[0].shape
    s = dil0 * sd0
    lw = lse_list[0].shape[2]
    group = d // HEAD_DIM // N_KV_HEADS
    t = min(ROW_TILE, s)
    nb = len(o_list)
    dils = [o.shape[0] for o in o_list]

    def body(*refs):
        o_out, l_out = refs[2 * nb], refs[2 * nb + 1]
        o_stage, l_stage = refs[2 * nb + 2:3 * nb + 2], refs[3 * nb + 2:]
        o_planes = [_from_residues(src, stage, dil) for src, stage, dil in zip(refs[:nb], o_stage, dils)]
        l_planes = [_from_residues(src, stage, dil) for src, stage, dil in zip(refs[nb:2 * nb], l_stage, dils)]
        for kh in range(N_KV_HEADS):
            ls = [plane(kh) for plane in l_planes]
            mx = ls[0]
            for l in ls[1:]:
                mx = jnp.maximum(mx, l)
            es = [jnp.exp(l - mx) for l in ls]
            den = es[0]
            for e in es[1:]:
                den = den + e
            l_out[:, kh * LANES:(kh + 1) * LANES] = mx + jnp.log(den)
            ws = [e / den for e in es]
            for g in range(group):
                h = kh * group + g
                acc = jnp.zeros((t, HEAD_DIM), F32)
                for plane, w in zip(o_planes, ws):
                    acc = acc + w[:, g:g + 1] * plane(h)
                o_out[:, h * HEAD_DIM:(h + 1) * HEAD_DIM] = acc.astype(BF16)

    return pl.pallas_call(
        body, name="attn_combine", grid=(s // t,),
        in_specs=[_residue_spec(dil, t, d) for dil in dils] + [_residue_spec(dil, t, lw) for dil in dils],
        out_specs=[_rows(t, d), _rows(t, lw)],
        out_shape=[jax.ShapeDtypeStruct((s, d), BF16), jax.ShapeDtypeStruct((s, lw), F32)],
        scratch_shapes=[pltpu.VMEM(_stage_shape(t, d), F32)] * nb + [pltpu.VMEM(_stage_shape(t, lw), F32)] * nb,
        compiler_params=_params("parallel"),
    )(*o_list, *lse_list)


def _attn_delta(do, o, lse, dils):
    s, d = o.shape
    lw = lse.shape[1]
    group = d // HEAD_DIM // N_KV_HEADS
    t = min(ROW_TILE, s)
    nd = len(dils)

    def body(do_ref, o_ref, lse_ref, *refs):
        stage = refs[-1]
        lane = lax.broadcasted_iota(I32, (t, LANES), 1)
        planes = []
        for kh in range(N_KV_HEADS):
            out = jnp.zeros((t, LANES), F32)
            for g in range(group):
                cols = slice((kh * group + g) * HEAD_DIM, (kh * group + g + 1) * HEAD_DIM)
                v = jnp.sum(do_ref[:, cols].astype(F32) * o_ref[:, cols].astype(F32), axis=-1, keepdims=True)
                out = jnp.where(lane == g, v, out)
            planes.append(out)
        _to_residues(lse_ref[...], stage, refs[:nd], dils)
        _to_residues(jnp.concatenate(planes, axis=1), stage, refs[nd:2 * nd], dils)

    res = pl.pallas_call(
        body, name="attn_delta", grid=(s // t,),
        in_specs=[_rows(t, d), _rows(t, d), _rows(t, lw)],
        out_specs=[_residue_spec(dil, t, lw) for dil in dils] * 2,
        out_shape=[jax.ShapeDtypeStruct((dil, s // dil, lw), F32) for dil in dils] * 2,
        scratch_shapes=[pltpu.VMEM(_stage_shape(t, lw), F32)],
        compiler_params=_params("parallel"),
    )(do, o, lse)
    return res[:nd], res[nd:]


def _residue_sum(name, groups, tabs):
    first = groups[0][0][0]
    s, w = first.shape[0] * first.shape[1], first.shape[2]
    t = min(ROW_TILE, s)
    flat = [p for parts, _ in groups for p in parts]

    def body(*refs):
        c_ref, a_ref, b_ref = refs[len(flat):len(flat) + 3]
        out = refs[len(flat) + 3]
        stages = refs[len(flat) + 4:]
        k = 0
        for gi, (parts, rotate) in enumerate(groups):
            planes = [_from_residues(refs[k + i], stages[k + i], p.shape[0]) for i, p in enumerate(parts)]
            k += len(parts)
            for c in range(w // LANES):
                tot = planes[0](c)
                for plane in planes[1:]:
                    tot = tot + plane(c)
                if rotate:
                    tot = _rope_apply(tot, c_ref[...], a_ref[...], b_ref[...], -1.0)
                out[:, gi * w + c * LANES:gi * w + (c + 1) * LANES] = tot.astype(BF16)

    return pl.pallas_call(
        body, name=name, grid=(s // t,),
        in_specs=[_residue_spec(p.shape[0], t, w) for p in flat] + [_rows(t, HEAD_DIM)] * 3,
        out_specs=_rows(t, len(groups) * w), out_shape=jax.ShapeDtypeStruct((s, len(groups) * w), BF16),
        scratch_shapes=[pltpu.VMEM(_stage_shape(t, w), F32) for _ in flat],
        compiler_params=_params("parallel"),
    )(*flat, *tabs)


def _dwconv_fwd(u, w_dw, b_dw):
    s, d2 = u.shape
    d = d2 // 2
    cb = min(CONV_CB, d)
    nblk = d // cb
    tt = min(CONV_T, s)

    def body(ua_ref, ug_ref, w_ref, b_ref, c_ref, xp_ref):
        gl = ua_ref[...].astype(F32) * _sigmoid(ug_ref[...].astype(F32))
        xp_ref[0:CONV_PAD, :] = jnp.zeros((CONV_PAD, cb), F32)
        xp_ref[CONV_PAD:, :] = gl
        wv = w_ref[...]
        bv = b_ref[...]
        for t0 in range(0, s, tt):
            acc = jnp.zeros((tt, cb), F32) + bv
            for kk in range(CONV_WIDTH):
                off = t0 + CONV_PAD - (CONV_WIDTH - 1) + kk
                acc = acc + wv[kk:kk + 1, :] * xp_ref[off:off + tt, :]
            c_ref[t0:t0 + tt, :] = acc

    return pl.pallas_call(
        body, name="dwconv_fwd", grid=(nblk,),
        in_specs=[pl.BlockSpec((s, cb), lambda j: (0, j)), pl.BlockSpec((s, cb), lambda j: (0, j + nblk)),
                  pl.BlockSpec((CONV_PAD, cb), lambda j: (0, j)), pl.BlockSpec((1, cb), lambda j: (0, j))],
        out_specs=pl.BlockSpec((s, cb), lambda j: (0, j)),
        out_shape=jax.ShapeDtypeStruct((s, d), F32),
        scratch_shapes=[pltpu.VMEM((s + CONV_PAD, cb), F32)],
        compiler_params=_params("parallel"),
    )(u, u, w_dw, b_dw)


def _dwconv_bwd(u, w_dw, dc):
    s, d2 = u.shape
    d = d2 // 2
    cb = min(CONV_CB, d)
    nblk = d // cb
    tt = min(CONV_T, s)

    def body(ua_ref, ug_ref, w_ref, dc_ref, du_ref, dw_ref, dba_ref, dbg_ref, glp_ref, dcp_ref, acc_ref):
        a = ua_ref[...].astype(F32)
        sig = _sigmoid(ug_ref[...].astype(F32))
        glp_ref[0:CONV_PAD, :] = jnp.zeros((CONV_PAD, cb), F32)
        glp_ref[CONV_PAD:, :] = a * sig
        dcp_ref[0:s, :] = dc_ref[...]
        dcp_ref[s:, :] = jnp.zeros((CONV_PAD, cb), F32)
        acc_ref[...] = jnp.zeros_like(acc_ref)
        wv = w_ref[...]
        dba = jnp.zeros((1, cb), F32)
        dbg = jnp.zeros((1, cb), F32)
        for t0 in range(0, s, tt):
            dgl = jnp.zeros((tt, cb), F32)
            dct = dc_ref[t0:t0 + tt, :]
            for kk in range(CONV_WIDTH):
                off = t0 + (CONV_WIDTH - 1) - kk
                dgl = dgl + wv[kk:kk + 1, :] * dcp_ref[off:off + tt, :]
                goff = t0 + CONV_PAD - (CONV_WIDTH - 1) + kk
                prod = dct * glp_ref[goff:goff + tt, :]
                acc_ref[8 * kk:8 * kk + 8, :] += jnp.sum(prod.reshape(tt // 8, 8, cb), axis=0)
            at = ua_ref[t0:t0 + tt, :].astype(F32)
            st = _sigmoid(ug_ref[t0:t0 + tt, :].astype(F32))
            da = dgl * st
            dg = dgl * at * st * (1.0 - st)
            du_ref[0, t0:t0 + tt, :] = da.astype(BF16)
            du_ref[1, t0:t0 + tt, :] = dg.astype(BF16)
            dba = dba + jnp.sum(da, axis=0, keepdims=True)
            dbg = dbg + jnp.sum(dg, axis=0, keepdims=True)
        dba_ref[...] = dba
        dbg_ref[...] = dbg
        for kk in range(CONV_WIDTH):
            dw_ref[kk:kk + 1, :] = jnp.sum(acc_ref[8 * kk:8 * kk + 8, :], axis=0, keepdims=True)
        dw_ref[CONV_WIDTH:, :] = jnp.zeros((CONV_PAD - CONV_WIDTH, cb), F32)

    blk = pl.BlockSpec((s, cb), lambda j: (0, j))
    vec = pl.BlockSpec((1, cb), lambda j: (0, j))
    return pl.pallas_call(
        body, name="dwconv_bwd", grid=(nblk,),
        in_specs=[blk, pl.BlockSpec((s, cb), lambda j: (0, j + nblk)),
                  pl.BlockSpec((CONV_PAD, cb), lambda j: (0, j)), blk],
        out_specs=[pl.BlockSpec((2, s, cb), lambda j: (0, 0, j)), pl.BlockSpec((CONV_PAD, cb), lambda j: (0, j)),
                   vec, vec],
        out_shape=[jax.ShapeDtypeStruct((2, s, d), BF16), jax.ShapeDtypeStruct((CONV_PAD, d), F32),
                   jax.ShapeDtypeStruct((1, d), F32), jax.ShapeDtypeStruct((1, d), F32)],
        scratch_shapes=[pltpu.VMEM((s + CONV_PAD, cb), F32), pltpu.VMEM((s + CONV_PAD, cb), F32),
                        pltpu.VMEM((8 * CONV_PAD, cb), F32)],
        compiler_params=_params("parallel"),
    )(u, u, w_dw, dc)


def _stack_heads(x, group):
    return jnp.concatenate([x[:, g * HEAD_DIM:(g + 1) * HEAD_DIM] for g in range(group)], axis=0)


def _unstack_heads(x, group):
    return jnp.concatenate([x[g * ATT_BLOCK:(g + 1) * ATT_BLOCK, :] for g in range(group)], axis=1)


def _stack_cols(x, group):
    return jnp.concatenate([x[:, g:g + 1] for g in range(group)], axis=0)


def _band_mask(nb, group):
    rows = group * ATT_BLOCK
    row = lax.broadcasted_iota(I32, (rows, 2 * ATT_BLOCK), 0) % ATT_BLOCK
    col = lax.broadcasted_iota(I32, (rows, 2 * ATT_BLOCK), 1)
    return (col >= row) & (col <= row + ATT_BLOCK) & ((col >= ATT_BLOCK) | (nb > 0))


def _window(ref, nb):
    prev = pl.multiple_of(jnp.maximum(nb - 1, 0) * ATT_BLOCK, ATT_BLOCK)
    cur = pl.multiple_of(nb * ATT_BLOCK, ATT_BLOCK)
    return jnp.concatenate([ref[pl.ds(prev, ATT_BLOCK), :], ref[pl.ds(cur, ATT_BLOCK), :]], axis=0)


def _residues_per_step(dil, nblk):
    return max(1, min(dil, ATT_STEP_BLOCKS // nblk))


def _attn_fwd(name, q, kv):
    dil, sd, d = q.shape
    group = d // HEAD_DIM // N_KV_HEADS
    gw = group * HEAD_DIM
    nblk = sd // ATT_BLOCK
    scale = 1.0 / math.sqrt(HEAD_DIM)
    nt = (((1,), (1,)), ((), ()))

    rb = _residues_per_step(dil, nblk)

    def body(q_all, k_all, v_all, o_all, lse_all):
        lane = lax.broadcasted_iota(I32, (ATT_BLOCK, LANES), 1)
        for rr in range(rb):
            q_ref, k_ref, v_ref, o_ref, lse_ref = [ref.at[rr] for ref in (q_all, k_all, v_all, o_all, lse_all)]

            def step(nb, carry):
                rows = pl.ds(pl.multiple_of(nb * ATT_BLOCK, ATT_BLOCK), ATT_BLOCK)
                qs = _stack_heads(q_ref[rows, :], group)
                kw = _window(k_ref, nb)
                vw = _window(v_ref, nb)
                sc = lax.dot_general(qs, kw, nt, preferred_element_type=F32) * scale
                sc = jnp.where(_band_mask(nb, group), sc, -jnp.inf)
                mx = jnp.max(sc, axis=-1, keepdims=True)
                p = jnp.exp(sc - mx)
                l = jnp.sum(p, axis=-1, keepdims=True)
                o = jnp.dot(p.astype(BF16), vw, preferred_element_type=F32) / l
                o_ref[rows, :] = _unstack_heads(o, group).astype(BF16)
                lse = mx + jnp.log(l)
                out = jnp.zeros((ATT_BLOCK, LANES), F32)
                for g in range(group):
                    out = jnp.where(lane == g, lse[g * ATT_BLOCK:(g + 1) * ATT_BLOCK, :], out)
                lse_ref[rows, :] = out
                return carry

            lax.fori_loop(0, nblk, step, 0, unroll=min(2, nblk))

    kvh = N_KV_HEADS
    qspec = pl.BlockSpec((rb, sd, gw), lambda r, h: (r, 0, h))
    kspec = pl.BlockSpec((rb, sd, HEAD_DIM), lambda r, h: (r, 0, h))
    return pl.pallas_call(
        body, name=name, grid=(dil // rb, kvh),
        in_specs=[qspec, kspec, pl.BlockSpec((rb, sd, HEAD_DIM), lambda r, h: (r, 0, kvh + h))],
        out_specs=[qspec, kspec],
        out_shape=[jax.ShapeDtypeStruct((dil, sd, d), BF16),
                   jax.ShapeDtypeStruct((dil, sd, kvh * LANES), F32)],
        compiler_params=_params("parallel", "parallel"),
    )(q, kv, kv)


def _attn_bwd(name, q, kv, do, lse, delta):
    dil, sd, d = q.shape
    group = d // HEAD_DIM // N_KV_HEADS
    gw = group * HEAD_DIM
    nblk = sd // ATT_BLOCK
    scale = 1.0 / math.sqrt(HEAD_DIM)
    nt = (((1,), (1,)), ((), ()))
    tn = (((0,), (0,)), ((), ()))

    rb = _residues_per_step(dil, nblk)

    def body(q_all, k_all, v_all, do_all, lse_all, dl_all, dq_all, dk_all, dv_all, dk_accs, dv_accs):
        dk_accs[...] = jnp.zeros_like(dk_accs)
        dv_accs[...] = jnp.zeros_like(dv_accs)
        for rr in range(rb):
            q_ref, k_ref, v_ref, do_ref, lse_ref, dl_ref, dq_ref, dk_ref, dv_ref, dk_acc, dv_acc = [
                ref.at[rr] for ref in (q_all, k_all, v_all, do_all, lse_all, dl_all, dq_all, dk_all, dv_all,
                                       dk_accs, dv_accs)]

            def step(nb, carry):
                rows = pl.ds(pl.multiple_of(nb * ATT_BLOCK, ATT_BLOCK), ATT_BLOCK)
                qs = _stack_heads(q_ref[rows, :], group)
                dos = _stack_heads(do_ref[rows, :], group)
                ls = _stack_cols(lse_ref[rows, :], group)
                dl = _stack_cols(dl_ref[rows, :], group)
                kw = _window(k_ref, nb)
                vw = _window(v_ref, nb)
                sc = lax.dot_general(qs, kw, nt, preferred_element_type=F32) * scale
                sc = jnp.where(_band_mask(nb, group), sc, -jnp.inf)
                p = jnp.exp(sc - ls)
                dp = lax.dot_general(dos, vw, nt, preferred_element_type=F32)
                ds = (p * (dp - dl) * scale).astype(BF16)
                dq = jnp.dot(ds, kw, preferred_element_type=F32)
                dq_ref[rows, :] = _unstack_heads(dq, group).astype(BF16)
                win = pl.ds(pl.multiple_of(nb * ATT_BLOCK, ATT_BLOCK), 2 * ATT_BLOCK)
                dk_acc[win, :] += lax.dot_general(ds, qs, tn, preferred_element_type=F32)
                dv_acc[win, :] += lax.dot_general(p.astype(BF16), dos, tn, preferred_element_type=F32)
                return carry

            lax.fori_loop(0, nblk, step, 0, unroll=min(2, nblk))
            dk_ref[...] = dk_acc[ATT_BLOCK:, :]
            dv_ref[...] = dv_acc[ATT_BLOCK:, :]

    kvh = N_KV_HEADS
    qspec = pl.BlockSpec((rb, sd, gw), lambda r, h: (r, 0, h))
    kspec = pl.BlockSpec((rb, sd, HEAD_DIM), lambda r, h: (r, 0, h))
    return pl.pallas_call(
        body, name=name, grid=(dil // rb, kvh),
        in_specs=[qspec, kspec, pl.BlockSpec((rb, sd, HEAD_DIM), lambda r, h: (r, 0, kvh + h)),
                  qspec, kspec, kspec],
        out_specs=[qspec, kspec, kspec],
        out_shape=[jax.ShapeDtypeStruct((dil, sd, d), BF16),
                   jax.ShapeDtypeStruct((dil, sd, kvh * HEAD_DIM), F32),
                   jax.ShapeDtypeStruct((dil, sd, kvh * HEAD_DIM), F32)],
        scratch_shapes=[pltpu.VMEM((rb, sd + ATT_BLOCK, HEAD_DIM), F32)] * 2,
        compiler_params=_params("parallel", "parallel"),
    )(q, kv, kv, do, lse, delta)


def _cast_bf16(name, w, layer, place, after=None):
    _, r, c = w.shape
    tr = min(512, r)
    deps = [] if after is None else [after]

    def body(pl_ref, w_ref, *refs):
        refs[-1][...] = w_ref[...].astype(BF16)

    return pl.pallas_call(
        body, name=name,
        grid_spec=pltpu.PrefetchScalarGridSpec(
            num_scalar_prefetch=1, grid=(r // tr,),
            in_specs=[pl.BlockSpec((None, tr, c), lambda i, p: (layer, i, 0))] + [ANY] * len(deps),
            out_specs=pl.BlockSpec((None, tr, c), lambda i, p: (p[1], i, 0))),
        out_shape=jax.ShapeDtypeStruct((N_SHARD, r, c), BF16),
        compiler_params=_params("parallel"),
    )(place, w, *deps)


def _chip_sum(name, g, rh, place):
    _, r, c = g.shape
    rh2 = r // 2
    tr = min(512, rh2)
    nb = rh2 // tr

    def body(pl_ref, g_ref, rh_ref, o_ref):
        o_ref[...] = (g_ref[...].astype(F32) + rh_ref[...].astype(F32)).astype(BF16)

    return pl.pallas_call(
        body, name=name,
        grid_spec=pltpu.PrefetchScalarGridSpec(
            num_scalar_prefetch=1, grid=(N_SHARD, nb),
            in_specs=[pl.BlockSpec((None, tr, c), lambda s, i, p: (s, p[0] * nb + i, 0)),
                      pl.BlockSpec((None, tr, c), lambda s, i, p: (s, i, 0))],
            out_specs=pl.BlockSpec((None, tr, c), lambda s, i, p: (s, i, 0))),
        out_shape=jax.ShapeDtypeStruct((N_SHARD, rh2, c), BF16),
        compiler_params=_params("parallel", "parallel"),
    )(place, g, rh)


def _owner_sum(name, cs, rp, place):
    _, rh2, c = cs.shape
    tr = min(512, rh2)
    nb = rh2 // tr

    def body(pl_ref, cs_ref, r0_ref, r1_ref, r2_ref, o_ref):
        o_ref[...] = ((cs_ref[...].astype(F32) + r0_ref[...].astype(F32))
                      + (r1_ref[...].astype(F32) + r2_ref[...].astype(F32)))

    def rspec(j):
        return pl.BlockSpec((None, tr, c), lambda i, p: (j, i, 0))

    return pl.pallas_call(
        body, name=name,
        grid_spec=pltpu.PrefetchScalarGridSpec(
            num_scalar_prefetch=1, grid=(nb,),
            in_specs=[pl.BlockSpec((None, tr, c), lambda i, p: (p[1], i, 0)), rspec(0), rspec(1), rspec(2)],
            out_specs=pl.BlockSpec((tr, c), lambda i, p: (p[0] * nb + i, 0))),
        out_shape=jax.ShapeDtypeStruct((2 * rh2, c), F32),
        compiler_params=_params("parallel"),
    )(place, cs, rp, rp, rp)


def _adam_math(w, g, m, v):
    m = ADAM_B1 * m + (1.0 - ADAM_B1) * g
    v = ADAM_B2 * v + (1.0 - ADAM_B2) * (g * g)
    m_hat = m / (1.0 - ADAM_B1 ** ADAM_STEP)
    v_hat = v / (1.0 - ADAM_B2 ** ADAM_STEP)
    delta = -ADAM_LR * (m_hat / (jnp.sqrt(v_hat) + ADAM_EPS) + ADAM_WD * w)
    return delta, m, v


def _adamw(name, w, m, v, g, layer, partial=None):
    nl, r, c = w.shape
    tr = min(256, r)

    def body(w_ref, m_ref, v_ref, g_ref, *refs):
        go_ref, d_ref, mo_ref, vo_ref = refs[-4:]
        gv = g_ref[...]
        delta, m_new, v_new = _adam_math(w_ref[...], gv, m_ref[...], v_ref[...])
        go_ref[...] = gv
        d_ref[...] = delta
        mo_ref[...] = m_new
        vo_ref[...] = v_new

    wspec = pl.BlockSpec((None, tr, c), lambda i: (layer, i, 0))
    prev = [] if partial is None else list(partial)
    return pl.pallas_call(
        body, name=name, grid=(r // tr,),
        in_specs=[wspec] * 3 + [pl.BlockSpec((tr, c), lambda i: (i, 0))] + [ANY] * len(prev),
        out_specs=[wspec] * 4,
        out_shape=[jax.ShapeDtypeStruct((nl, r, c), F32)] * 4,
        input_output_aliases={4 + i: i for i in range(len(prev))},
        compiler_params=_params("parallel"),
    )(w, m, v, g, *prev)


def _adam_small(ws, ms, vs, gs):
    n = len(ws)

    def body(*refs):
        w_refs, m_refs, v_refs, g_refs = refs[:n], refs[n:2 * n], refs[2 * n:3 * n], refs[3 * n:4 * n]
        d_refs, mo_refs, vo_refs = refs[4 * n:5 * n], refs[5 * n:6 * n], refs[6 * n:7 * n]
        for i in range(n):
            delta, m_new, v_new = _adam_math(w_refs[i][...], g_refs[i][...], m_refs[i][...], v_refs[i][...])
            d_refs[i][...] = delta
            mo_refs[i][...] = m_new
            vo_refs[i][...] = v_new

    shapes = [jax.ShapeDtypeStruct(w.shape, F32) for w in ws]
    res = pl.pallas_call(body, name="adam_small", out_shape=shapes * 3)(*ws, *ms, *vs, *gs)
    return res[:n], res[n:2 * n], res[2 * n:]


def _pack_small(b_in, w_dw, b_dw, ln_g, ln_b, b_out, place):
    cin = b_in.shape[1]
    cd = b_dw.shape[1]
    rows = 8 + CONV_PAD

    def body(pl_ref, bi, wd, bd, lg, lb, bo, out):
        out[...] = jnp.zeros_like(out)
        out[0:1, :] = bi[...]
        out[1:2, 0:cd] = bd[...]
        out[1:2, cd:2 * cd] = lg[...]
        out[2:3, 0:cd] = lb[...]
        out[2:3, cd:2 * cd] = bo[...]
        out[8:8 + CONV_WIDTH, 0:cd] = wd[...]

    def whole(arr):
        return pl.BlockSpec(arr.shape, lambda i, p: (0,) * arr.ndim)

    ins = [b_in, w_dw, b_dw, ln_g, ln_b, b_out]
    return pl.pallas_call(
        body, name="pack_small",
        grid_spec=pltpu.PrefetchScalarGridSpec(
            num_scalar_prefetch=1, grid=(1,), in_specs=[whole(a) for a in ins],
            out_specs=pl.BlockSpec((None, rows, cin), lambda i, p: (p[1], 0, 0))),
        out_shape=jax.ShapeDtypeStruct((N_SHARD, rows, cin), F32),
        compiler_params=_params("arbitrary"),
    )(place, *ins)


def _place():
    x, y, c = lax.axis_index("x"), lax.axis_index("y"), lax.axis_index("c")
    return x, y, c


def _other_chips(x, y):
    return [(1 - x, y), (x, 1 - y), (1 - x, 1 - y)]


def _split_start(name, bufs, n_sem, copies, after=None):
    n = len(bufs)
    deps = [] if after is None else [after]

    def body(*refs):
        out0 = n + len(deps)
        for cp in copies(refs[:n], refs[out0], refs[out0 + 1], False):
            cp.start()
        refs[-1][...] = jnp.zeros_like(refs[-1])

    res = pl.pallas_call(
        body, name=name,
        out_shape=(pltpu.SemaphoreType.DMA((n_sem,)), pltpu.SemaphoreType.DMA((n_sem,)),
                   *[pltpu.HBM(b.shape, b.dtype) for b in bufs], jax.ShapeDtypeStruct((8, LANES), F32)),
        in_specs=[HBM] * n + [ANY] * len(deps),
        out_specs=(SEM, SEM, *[HBM] * n, pl.BlockSpec(memory_space=pltpu.VMEM)),
        input_output_aliases={i: 2 + i for i in range(n)},
        compiler_params=pltpu.CompilerParams(has_side_effects=SPLIT_EFFECT),
    )(*[pltpu.with_memory_space_constraint(b, pltpu.HBM) for b in bufs], *deps)
    return res[0], res[1], list(res[2:2 + n]), res[-1]


def _split_wait(name, handle, copies, after):
    ssem, rsem, bufs, _ = handle
    n = len(bufs)
    deps = list(after) if isinstance(after, (list, tuple)) else [after]

    def body(*refs):
        for cp in copies(refs[:n], refs[n], refs[n + 1], True):
            cp.wait_send()
            cp.wait_recv()

    res = pl.pallas_call(
        body, name=name,
        out_shape=[pltpu.HBM(b.shape, b.dtype) for b in bufs],
        in_specs=[HBM] * n + [SEM, SEM] + [ANY] * len(deps), out_specs=[HBM] * n,
        input_output_aliases={i: i for i in range(n)},
        compiler_params=pltpu.CompilerParams(has_side_effects=SPLIT_EFFECT),
    )(*bufs, ssem, rsem, *deps)
    return list(res)


def _remote(src, dst, ssem, rsem, k, to):
    return pltpu.make_async_remote_copy(src_ref=src, dst_ref=dst, send_sem=ssem.at[k], recv_sem=rsem.at[k],
                                        device_id=to, device_id_type=MESH)


def _gather_chips(x, y, c):
    nx, ny = x + (1 - c) - 2 * x * (1 - c), y + c - 2 * y * c
    fx, fy = x + c - 2 * x * c, y + (1 - c) - 2 * y * (1 - c)
    return (nx, ny), (fx, fy), 2 * nx + ny, 2 * fx + fy, 2 * (1 - x) + (1 - y)


def _direct_copies(refs, ssem, rsem, landing, n_whole=0):
    x, y, c = _place()
    me = 2 * x + y
    (nx, ny), _, near, _, _ = _gather_chips(x, y, c)
    n = len(refs) - n_whole
    cps = []
    for a, ref in enumerate(refs[:n]):
        cps.append(_remote(ref.at[me], ref.at[near if landing else me], ssem, rsem, a, (nx, ny, c)))
    for b, ref in enumerate(refs[n:]):
        for j, (px, py) in enumerate(_other_chips(x, y)):
            cps.append(_remote(ref.at[me], ref.at[2 * px + py if landing else me], ssem, rsem, n + 3 * b + j,
                               (px, py, c)))
    return cps


def _relay_copies(refs, ssem, rsem, landing):
    x, y, c = _place()
    _, (fx, fy), near, far, diag = _gather_chips(x, y, c)
    n = len(refs)
    cps = []
    for a, ref in enumerate(refs):
        rh = ref.shape[1] // 2
        rows = pl.ds(c * rh, rh)
        cps.append(_remote(ref.at[near, rows], ref.at[diag if landing else near, rows], ssem, rsem, a, (fx, fy, c)))
        cps.append(_remote(ref.at[near], ref.at[far if landing else near], ssem, rsem, n + a, (x, y, 1 - c)))
    return cps


def _diagonal_copies(refs, ssem, rsem, landing):
    x, y, c = _place()
    diag = 2 * (1 - x) + (1 - y)
    who = 1 - c if landing else c
    cps = []
    for a, ref in enumerate(refs):
        rh = ref.shape[1] // 2
        piece = ref.at[diag, pl.ds(who * rh, rh)]
        cps.append(_remote(piece, piece, ssem, rsem, a, (x, y, 1 - c)))
    return cps


def _sibling_copies(refs, ssem, rsem, landing):
    x, y, c = _place()
    n = len(refs) // 2
    cps = []
    for a in range(n):
        rh = refs[a].shape[1] // 2
        cps.append(_remote(refs[a].at[:, pl.ds((1 - c) * rh, rh), :], refs[n + a], ssem, rsem, a, (x, y, 1 - c)))
    return cps


def _owner_copies(refs, ssem, rsem, landing):
    x, y, c = _place()
    n = len(refs) // 2
    cps = []
    for a in range(n):
        for j, (px, py) in enumerate(_other_chips(x, y)):
            cps.append(_remote(refs[a].at[2 * px + py], refs[n + a].at[j], ssem, rsem, 3 * a + j, (px, py, c)))
    return cps


def _swap_copies(refs, ssem, rsem, landing):
    x, y, c = _place()
    who = 1 - c if landing else c
    cps = []
    for a, ref in enumerate(refs):
        rh = ref.shape[0] // 2
        rows = ref.at[pl.ds(who * rh, rh)]
        cps.append(_remote(rows, rows, ssem, rsem, a, (x, y, 1 - c)))
    return cps


def _small_copies(refs, ssem, rsem, landing):
    pack, slots = refs
    x, y, c = _place()
    cps = []
    for rel in range(1, N_DEV):
        px = 1 - x if (rel >> 2) & 1 else x
        py = 1 - y if (rel >> 1) & 1 else y
        pc = 1 - c if rel & 1 else c
        slot = 4 * px + 2 * py + pc if landing else 4 * x + 2 * y + c
        cps.append(_remote(pack, slots.at[slot], ssem, rsem, rel - 1, (px, py, pc)))
    return cps


def _small_pack(rows, w_dw_grad, d):
    n = len(rows)

    def body(*refs):
        pack = refs[-1]
        pack[...] = jnp.zeros_like(pack)
        for (r, _), ref in zip(rows, refs[:n]):
            pack[r:r + 1, :] = ref[...]
        pack[16:16 + CONV_PAD, :] = refs[n][...]

    return pl.pallas_call(body, name="small_pack", out_shape=jax.ShapeDtypeStruct((SMALL_ROWS, d), F32))(
        *[v for _, v in rows], w_dw_grad)


def _small_sum(pack, slots, place):
    rows, d = pack.shape
    loss_row = 12

    def body(pl_ref, pack_ref, slots_ref, out_ref):
        me = pl_ref[2]
        tot = jnp.where(me == 0, pack_ref[...], slots_ref[0])
        for i in range(1, N_DEV):
            tot = tot + jnp.where(me == i, pack_ref[...], slots_ref[i])
        out_ref[...] = tot
        out_ref[loss_row:loss_row + 1, :] = jnp.zeros((1, d), F32) + jnp.sum(tot[loss_row:loss_row + 1, :])

    return pl.pallas_call(
        body, name="small_sum",
        grid_spec=pltpu.PrefetchScalarGridSpec(
            num_scalar_prefetch=1, grid=(1,),
            in_specs=[pl.BlockSpec((rows, d), lambda i, p: (0, 0)), pl.BlockSpec((N_DEV, rows, d), lambda i, p: (0, 0, 0))],
            out_specs=pl.BlockSpec((rows, d), lambda i, p: (0, 0))),
        out_shape=jax.ShapeDtypeStruct((rows, d), F32),
        compiler_params=_params("arbitrary"),
    )(place, pack, slots)


def kernel(x, norm_mix, norm_mlp, conv_w_in, conv_b_in, conv_w_dw, conv_b_dw, conv_ln_g, conv_ln_b, conv_w_out, conv_b_out, kv_norm, w_kv, attn_w_q, attn_w_o, mlp_w_in, mlp_w_out, final_norm, loss_target, m_norm_mix, m_norm_mlp, m_conv_w_in, m_conv_b_in, m_conv_w_dw, m_conv_b_dw, m_conv_ln_g, m_conv_ln_b, m_conv_w_out, m_conv_b_out, m_kv_norm, m_w_kv, m_attn_w_q, m_attn_w_o, m_mlp_w_in, m_mlp_w_out, m_final_norm, v_norm_mix, v_norm_mlp, v_conv_w_in, v_conv_b_in, v_conv_w_dw, v_conv_b_dw, v_conv_ln_g, v_conv_ln_b, v_conv_w_out, v_conv_b_out, v_kv_norm, v_w_kv, v_attn_w_q, v_attn_w_o, v_mlp_w_in, v_mlp_w_out, v_final_norm):
    _, s, d = x.shape
    dff = mlp_w_in.shape[2] * N_SHARD
    kvw = w_kv.shape[1]
    nh = d // HEAD_DIM
    group = nh // N_KV_HEADS
    ds4 = d // N_SHARD
    xi, yi, ci = _place()
    me = 2 * xi + yi
    place = jnp.stack([ci, me, 2 * me + ci]).astype(I32)

    h0 = x.reshape(s, d)
    target = loss_target.reshape(s, d)
    tabs = _rope_tables(s)

    def gather_begin(tag, bufs, n_whole=0):
        plan = functools.partial(_direct_copies, n_whole=n_whole)
        return _split_start(f"gather_start_{tag}", bufs, len(bufs) + 2 * n_whole, plan), plan, n_whole

    def gather_land(tag, begun, later):
        handle, plan, n_whole = begun
        bufs = _split_wait(f"gather_wait_{tag}", handle, plan, later)
        n = len(bufs) - n_whole
        return _split_start(f"relay_start_{tag}", bufs[:n], 2 * n, _relay_copies), bufs[n:]

    def gather_swap(tag, landed, later):
        relayed, whole = landed
        bufs = _split_wait(f"relay_wait_{tag}", relayed, _relay_copies, later)
        return _split_start(f"diagonal_start_{tag}", bufs, len(bufs), _diagonal_copies), whole

    def gather_end(tag, swapped, later):
        handle, whole = swapped
        return _split_wait(f"diagonal_wait_{tag}", handle, _diagonal_copies, later) + whole

    def tied(vec, begun):
        return vec + begun[0][3][0:1, 0:1]

    c_win = _cast_bf16("cast_w_in", conv_w_in, 0, place)
    ag_cin = gather_begin("conv_in", [
        c_win,
        _pack_small(conv_b_in, conv_w_dw.reshape(CONV_WIDTH, ds4), conv_b_dw, conv_ln_g, conv_ln_b, conv_b_out, place),
    ], n_whole=1)
    ag_cout = gather_begin("conv_out", [_cast_bf16("cast_w_out", conv_w_out, 0, place)])
    ag_mi0 = gather_begin("mlp_in0", [_cast_bf16("cast_mlp_in0", mlp_w_in, 0, place)])
    c_mo0 = _cast_bf16("cast_mlp_out0", mlp_w_out, 0, place)
    ag_mo0 = gather_begin("mlp_out0", [c_mo0])
    land_cin = gather_land("conv_in", ag_cin, ag_mo0[0][3])
    land_cout = gather_land("conv_out", ag_cout, land_cin[0][3])
    ag_attn = gather_begin("attn", [
        _cast_bf16("cast_w_kv", w_kv.reshape(1, ds4, kvw), 0, place, land_cout[0][3]),
        _cast_bf16("cast_w_q", attn_w_q, 0, place), _cast_bf16("cast_w_o", attn_w_o, 0, place)])
    ag_mi1 = gather_begin("mlp_in1", [_cast_bf16("cast_mlp_in1", mlp_w_in, 1, place, ag_attn[0][3])])
    ag_mo1 = gather_begin("mlp_out1", [_cast_bf16("cast_mlp_out1", mlp_w_out, 1, place, ag_mi1[0][3])])

    wmi_g = [None, None]
    wmo_f = [None, None]

    nm = [norm_mix[0:1], norm_mix[1:2]]
    nmlp = [norm_mlp[0:1], norm_mlp[1:2]]
    kvn = kv_norm.reshape(1, d)
    fin = final_norm.reshape(1, d)
    (y0,) = _rms_fwd("rms_mix0", h0, [tied(nm[0], ag_mo1)])

    swap_cin = gather_swap("conv_in", land_cin, y0)
    w_in_g, small_g = gather_end("conv_in", swap_cin, swap_cin[0][3])
    b_in_f = small_g[:, 0, :].reshape(1, 2 * d)
    b_dw_f = small_g[:, 1, 0:ds4].reshape(1, d)
    ln_g_f = small_g[:, 1, ds4:2 * ds4].reshape(1, d)
    ln_b_f = small_g[:, 2, 0:ds4].reshape(1, d)
    b_out_f = small_g[:, 2, ds4:2 * ds4].reshape(1, d)
    w_dw_f = jnp.transpose(small_g[:, 8:8 + CONV_PAD, 0:ds4], (1, 0, 2)).reshape(CONV_PAD, d)

    def ep_bias(acc, ex, outs, j):
        outs[0][...] = (acc + ex[0][...]).astype(outs[0].dtype)

    def ep_residual(acc, ex, outs, j):
        outs[0][...] = ex[0][...] + acc

    def ep_residual_bias(acc, ex, outs, j):
        outs[0][...] = ex[0][...] + (acc + ex[1][...])

    def ep_relu2(acc, ex, outs, j):
        r = jnp.maximum(acc, 0.0)
        outs[0][...] = r.astype(BF16)
        outs[1][...] = (r * r).astype(BF16)

    by_residue = [(BF16, ("residues", dil)) for dil in DILATIONS]

    def put_by_residue(val, outs, stage):
        _to_residues(val, stage, outs, DILATIONS)

    def ep_rope(acc, ex, outs, j, stage):
        put_by_residue(_rope_apply(acc, ex[0][...], ex[1][...], ex[2][...], 1.0), outs, stage)

    def ep_rope_k(acc, ex, outs, j, stage):
        roped = _rope_apply(acc, ex[0][...], ex[1][...], ex[2][...], 1.0)
        put_by_residue(jnp.where(j == 0, roped, acc), outs, stage)

    def ep_by_residue(acc, ex, outs, j, stage):
        put_by_residue(acc, outs, stage)

    tab_extras = [(t, "rows") for t in tabs]

    def mlp_fwd(idx, h, y, out_weight):
        r, r2 = _matmul(f"mlp_in{idx}", "nn", y, wmi_g[idx], b_kind="col", m=s, n=dff, k=d,
                        outs=[(BF16, "plain"), (BF16, "plain")], epilogue=ep_relu2)
        wmo_f[idx] = out_weight(r2).reshape(dff, d)
        (h_new,) = _matmul(f"mlp_out{idx}", "nn", r2, wmo_f[idx], m=s, n=d, k=dff,
                           outs=[(F32, "plain")], extras=[(h, "ij")], epilogue=ep_residual)
        return h_new, r, r2

    (u,) = _matmul("conv_in", "nn", y0, w_in_g, b_kind="col", m=s, n=2 * d, k=d,
                   outs=[(BF16, "plain")], extras=[(b_in_f, "vec")], epilogue=ep_bias)
    land_mi0 = gather_land("mlp_in0", ag_mi0, u)
    swap_cout = gather_swap("conv_out", land_cout, land_mi0[0][3])
    cpre = _dwconv_fwd(u, w_dw_f, tied(b_dw_f, swap_cout))
    sact = _ln_silu_fwd(cpre, ln_g_f, ln_b_f)
    (w_out_g,) = gather_end("conv_out", swap_cout, sact)
    w_out_f = w_out_g.reshape(d, d)
    (h1,) = _matmul("conv_out", "nn", sact, w_out_f, m=s, n=d, k=d,
                    outs=[(F32, "plain")], extras=[(h0, "ij"), (b_out_f, "vec")], epilogue=ep_residual_bias)
    swap_mi0 = gather_swap("mlp_in0", land_mi0, h1)
    land_mo0 = gather_land("mlp_out0", ag_mo0, swap_mi0[0][3])
    (y1,) = _rms_fwd("rms_mlp0", h1, [tied(nmlp[0], land_mo0)])
    (wmi_g[0],) = gather_end("mlp_in0", swap_mi0, y1)
    land_attn = None

    def out_weight0(r2):
        nonlocal land_attn
        land_attn = gather_land("attn", ag_attn, r2)
        swap_mo0 = gather_swap("mlp_out0", land_mo0, land_attn[0][3])
        return gather_end("mlp_out0", swap_mo0, swap_mo0[0][3])[0]

    h2, r0, r0sq = mlp_fwd(0, h1, y1, out_weight0)
    swap_attn = gather_swap("attn", land_attn, h2)
    land_mi1 = gather_land("mlp_in1", ag_mi1, swap_attn[0][3])
    ykv, y2 = _rms_fwd("rms_kv_mix1", h2, [tied(kvn, land_mi1), nm[1]])
    wkv_g, wq_g, wo_g = gather_end("attn", swap_attn, y2)
    wkv_f, wq_f, wo_f = wkv_g.reshape(d, kvw), wq_g.reshape(d, d), wo_g.reshape(d, d)
    kv_parts = _matmul("kv_proj", "nn", ykv, wkv_f, m=s, n=kvw, k=d, tn=kvw // 2,
                       outs=by_residue, extras=tab_extras, epilogue=ep_rope_k, stage=True)
    q_parts = _matmul("q_proj", "nn", y2, wq_f, m=s, n=d, k=d,
                      outs=by_residue, extras=tab_extras, epilogue=ep_rope, stage=True)
    swap_mi1 = gather_swap("mlp_in1", land_mi1, q_parts[0])
    o_parts, lse_parts = [], []
    for dil, q_b, kv_b in zip(DILATIONS, q_parts, kv_parts):
        o_b, lse_b = _attn_fwd(f"attn_fwd_d{dil}", q_b, kv_b)
        o_parts.append(o_b)
        lse_parts.append(lse_b)
    o, lse = _attn_combine(o_parts, lse_parts)
    land_mo1 = gather_land("mlp_out1", ag_mo1, o)
    (h3,) = _matmul("attn_out", "nn", o, wo_f, m=s, n=d, k=d,
                    outs=[(F32, "plain")], extras=[(h2, "ij")], epilogue=ep_residual)
    (y3,) = _rms_fwd("rms_mlp1", h3, [tied(nmlp[1], land_mo1)])
    (wmi_g[1],) = gather_end("mlp_in1", swap_mi1, y3)

    def out_weight1(r2):
        swap_mo1 = gather_swap("mlp_out1", land_mo1, r2)
        return gather_end("mlp_out1", swap_mo1, swap_mo1[0][3])[0]

    h4, r1, r1sq = mlp_fwd(1, h3, y3, out_weight1)
    dh4, dh4b, d_fin, loss_cols = _final_loss(h4, fin, target)

    def ep_relu2_bwd(acc, ex, outs, j):
        outs[0][...] = (acc * (2.0 * ex[0][...].astype(F32))).astype(BF16)

    def mlp_bwd(idx, dhb, y, r, r2):
        (dz,) = _matmul(f"mlp_out{idx}_dx", "nt", dhb, wmo_f[idx], m=s, n=dff, k=d,
                        outs=[(BF16, "plain")], extras=[(r, "ij")], epilogue=ep_relu2_bwd)
        (dwo,) = _matmul(f"mlp_out{idx}_dw", "tn", r2, dhb, m=dff, n=d, k=s,
                         outs=[(BF16, "plain")])
        (dy,) = _matmul(f"mlp_in{idx}_dx", "nt", dz, wmi_g[idx], b_kind="col", m=s, n=d, k=dff,
                        outs=[(BF16, "plain")])
        (dwi,) = _matmul(f"mlp_in{idx}_dw", "tn", y, dz, m=d, n=dff, k=s,
                         outs=[(BF16, "col")])
        return dy, dwi, dwo.reshape(N_SHARD, dff // N_SHARD, d)

    def token(handle):
        return handle[3][0:1, 0:1]

    def rs_exchange(tag, grads):
        lands = [lax.empty((N_SHARD, g.shape[1] // 2, g.shape[2]), g.dtype) for g in grads]
        return _split_start(f"sibling_start_{tag}", list(grads) + lands, len(grads), _sibling_copies)

    def rs_send(tag, names, exchanged, later):
        bufs = _split_wait(f"sibling_wait_{tag}", exchanged, _sibling_copies, later)
        n = len(names)
        sums = [_chip_sum(f"chip_sum_{nme}", g, rh, place) for nme, g, rh in zip(names, bufs[:n], bufs[n:])]
        lands = [lax.empty((N_SHARD - 1,) + cs.shape[1:], cs.dtype) for cs in sums]
        return _split_start(f"owners_start_{tag}", sums + lands, 3 * n, _owner_copies)

    def rs_sum(tag, names, sent, later):
        bufs = _split_wait(f"owners_wait_{tag}", sent, _owner_copies, later)
        n = len(names)
        own = [_owner_sum(f"owner_sum_{nme}", cs, rp, place) for nme, cs, rp in zip(names, bufs[:n], bufs[n:])]
        return _split_start(f"swap_start_{tag}", own, n, _swap_copies)

    def rs_end(tag, swapped, later):
        return _split_wait(f"swap_wait_{tag}", swapped, _swap_copies, later)

    dy3, g_wmi1, g_wmo1 = mlp_bwd(1, dh4b, y3, r1, r1sq)
    x_mlp1 = rs_exchange("mlp1", [g_wmi1, g_wmo1])
    dh3, dh3b, d_nmlp1 = _rms_bwd("rms_mlp1_bwd", h3, [(nmlp[1] + token(x_mlp1), dy3)], dh4)

    do_parts = _matmul("attn_out_dx", "nt", dh3b, wo_f, m=s, n=d, k=d, outs=by_residue, epilogue=ep_by_residue,
                       stage=True)
    (g_wo,) = _matmul("attn_out_dw", "tn", o, dh3b, m=d, n=d, k=s, outs=[(BF16, "plain")])
    rs_mlp1 = rs_send("mlp1", ["mlp_in1", "mlp_out1"], x_mlp1, g_wo)
    lse_res, delta_res = _attn_delta(do_parts[0].reshape(s, d), o, lse, DILATIONS)
    dq_parts, dk_parts, dv_parts = [], [], []
    for dil, q_b, kv_b, do_b, lse_b, dl_b in zip(DILATIONS, q_parts, kv_parts, do_parts, lse_res, delta_res):
        dq_b, dk_b, dv_b = _attn_bwd(f"attn_bwd_d{dil}", q_b, kv_b, do_b, lse_b, dl_b)
        dq_parts.append(dq_b)
        dk_parts.append(dk_b)
        dv_parts.append(dv_b)
    dq = _residue_sum("rope_bwd_q", [(dq_parts, True)], tabs)
    dkv = _residue_sum("rope_bwd_kv", [(dk_parts, True), (dv_parts, False)], tabs)
    (g_wq,) = _matmul("q_proj_dw", "tn", y2, dq, m=d, n=d, k=s, outs=[(BF16, "plain")])
    (dy2,) = _matmul("q_proj_dx", "nt", dq, wq_f, m=s, n=d, k=d, outs=[(BF16, "plain")])
    (g_wkv,) = _matmul("kv_proj_dw", "tn", ykv, dkv, m=d, n=kvw, k=s, outs=[(BF16, "plain")])
    (dykv,) = _matmul("kv_proj_dx", "nt", dkv, wkv_f, m=s, n=d, k=kvw, outs=[(BF16, "plain")])
    x_attn = rs_exchange("attn", [g_wkv.reshape(N_SHARD, ds4, kvw), g_wq.reshape(N_SHARD, ds4, d),
                                  g_wo.reshape(N_SHARD, ds4, d)])
    dh2, dh2b, d_nm1, d_kvn = _rms_bwd("rms_kv_mix1_bwd", h2, [(nm[1] + token(x_attn), dy2), (kvn, dykv)], dh3)
    rs_attn = rs_send("attn", ["w_kv", "w_q", "w_o"], x_attn, dh2b)

    dy1, g_wmi0, g_wmo0 = mlp_bwd(0, dh2b, y1, r0, r0sq)
    x_mlp0 = rs_exchange("mlp0", [g_wmi0, g_wmo0])
    dh1, dh1b, d_nmlp0, d_b_out = _rms_bwd("rms_mlp0_bwd", h1, [(nmlp[0] + token(x_mlp0) + token(rs_attn), dy1)],
                                           dh2, want_colsum=True)

    (dsact,) = _matmul("conv_out_dx", "nt", dh1b, w_out_f, m=s, n=d, k=d, outs=[(BF16, "plain")])
    (g_wout,) = _matmul("conv_out_dw", "tn", sact, dh1b, m=d, n=d, k=s, outs=[(BF16, "plain")])
    rs_mlp0 = rs_send("mlp0", ["mlp_in0", "mlp_out0"], x_mlp0, g_wout)
    dc, d_ln_g, d_ln_b, d_b_dw = _ln_silu_bwd(cpre, ln_g_f + token(rs_mlp0), ln_b_f, dsact)
    du, d_w_dw, d_b_in_a, d_b_in_g = _dwconv_bwd(u, w_dw_f, dc)
    (g_win,) = _matmul("conv_in_dw", "tn", y0, du, b_kind="col", m=d, n=2 * d, k=s, outs=[(BF16, "col")])
    x_conv = rs_exchange("conv", [g_win, g_wout.reshape(N_SHARD, ds4, d)])
    (dy0,) = _matmul("conv_in_dx", "nt", du, w_in_g, a_kind="col", b_kind="col", m=s, n=d, k=2 * d,
                     outs=[(BF16, "plain")])
    dx, _, d_nm0 = _rms_bwd("rms_mix0_bwd", h0, [(nm[0] + token(x_conv), dy0)], dh1)

    small_rows = [(0, d_nm0), (1, d_nm1), (2, d_nmlp0), (3, d_nmlp1), (4, d_kvn), (5, d_fin), (6, d_b_dw),
                  (7, d_ln_g), (8, d_ln_b), (9, d_b_out), (10, d_b_in_a), (11, d_b_in_g), (12, loss_cols)]
    x_small = _split_start("small_start", [_small_pack(small_rows, d_w_dw, d),
                                           lax.empty((N_DEV, SMALL_ROWS, d), F32)], N_DEV - 1, _small_copies)
    rs_conv = rs_send("conv", ["w_in", "w_out"], x_conv, x_small[3])

    def big(name, w, m, v, g, layer=0, partial=None):
        shape = w.shape
        w3, m3, v3 = [t.reshape((-1,) + shape[-2:]) for t in (w, m, v)]
        if partial is not None:
            partial = [t.reshape(w3.shape) for t in partial]
        res = _adamw(name, w3, m3, v3, g, layer, partial)
        return [t.reshape(shape) for t in res]

    sw_mlp1 = rs_sum("mlp1", ["mlp_in1", "mlp_out1"], rs_mlp1, rs_conv[3])
    sw_attn = rs_sum("attn", ["w_kv", "w_q", "w_o"], rs_attn, sw_mlp1[3])
    f_wmi1, f_wmo1 = rs_end("mlp1", sw_mlp1, sw_attn[3])
    p_wmi = big("adam_mlp_in1", mlp_w_in, m_mlp_w_in, v_mlp_w_in, f_wmi1, 1)
    p_wmo = big("adam_mlp_out1", mlp_w_out, m_mlp_w_out, v_mlp_w_out, f_wmo1, 1)
    sw_mlp0 = rs_sum("mlp0", ["mlp_in0", "mlp_out0"], rs_mlp0, [p_wmi[0], p_wmo[0]])
    f_wkv, f_wq, f_wo = rs_end("attn", sw_attn, sw_mlp0[3])
    r_wkv = big("adam_w_kv", w_kv, m_w_kv, v_w_kv, f_wkv)
    r_wq = big("adam_w_q", attn_w_q, m_attn_w_q, v_attn_w_q, f_wq)
    r_wo = big("adam_w_o", attn_w_o, m_attn_w_o, v_attn_w_o, f_wo)
    sw_conv = rs_sum("conv", ["w_in", "w_out"], rs_conv, [r_wkv[0], r_wq[0], r_wo[0]])
    f_wmi0, f_wmo0 = rs_end("mlp0", sw_mlp0, sw_conv[3])
    r_wmi = big("adam_mlp_in0", mlp_w_in, m_mlp_w_in, v_mlp_w_in, f_wmi0, 0, p_wmi)
    r_wmo = big("adam_mlp_out0", mlp_w_out, m_mlp_w_out, v_mlp_w_out, f_wmo0, 0, p_wmo)
    f_win, f_wout = rs_end("conv", sw_conv, [r_wmi[0], r_wmo[0]])
    r_win = big("adam_w_in", conv_w_in, m_conv_w_in, v_conv_w_in, f_win)
    r_wout = big("adam_w_out", conv_w_out, m_conv_w_out, v_conv_w_out, f_wout)

    small_pack, small_slots = _split_wait("small_wait", x_small, _small_copies, r_wout[0])
    red = _small_sum(small_pack, small_slots, place)
    loss = red[12, 0]
    g_norm_mix = red[0:2]
    g_norm_mlp = red[2:4]
    g_kv_norm = red[4:5]
    g_final = red[5:6]

    def my_cols(row):
        return lax.dynamic_slice(red, (row, me * ds4), (1, ds4))

    g_b_dw, g_ln_g, g_ln_b, g_b_out = my_cols(6), my_cols(7), my_cols(8), my_cols(9)
    half_in = 2 * d // N_SHARD
    b_in_row = 10 + me // 2
    g_b_in = lax.dynamic_slice(red, (b_in_row, (me % 2) * half_in), (1, half_in))
    g_w_dw = lax.dynamic_slice(red, (16, me * ds4), (CONV_WIDTH, ds4))

    sm_w =[norm_mix, norm_mlp, conv_b_in, conv_w_dw.reshape(CONV_WIDTH, ds4), conv_b_dw, conv_ln_g, conv_ln_b,
            conv_b_out, kv_norm.reshape(1, d), final_norm.reshape(1, d)]
    sm_m = [m_norm_mix, m_norm_mlp, m_conv_b_in, m_conv_w_dw.reshape(CONV_WIDTH, ds4), m_conv_b_dw, m_conv_ln_g,
            m_conv_ln_b, m_conv_b_out, m_kv_norm.reshape(1, d), m_final_norm.reshape(1, d)]
    sm_v = [v_norm_mix, v_norm_mlp, v_conv_b_in, v_conv_w_dw.reshape(CONV_WIDTH, ds4), v_conv_b_dw, v_conv_ln_g,
            v_conv_ln_b, v_conv_b_out, v_kv_norm.reshape(1, d), v_final_norm.reshape(1, d)]
    sm_g = [g_norm_mix, g_norm_mlp, g_b_in, g_w_dw, g_b_dw, g_ln_g, g_ln_b, g_b_out, g_kv_norm, g_final]
    sm_d, sm_nm, sm_nv = _adam_small(sm_w, sm_m, sm_v, sm_g)
    shapes = [norm_mix.shape, norm_mlp.shape, conv_b_in.shape, conv_w_dw.shape, conv_b_dw.shape, conv_ln_g.shape,
              conv_ln_b.shape, conv_b_out.shape, kv_norm.shape, final_norm.shape]
    sm_g, sm_d, sm_nm, sm_nv = [[t.reshape(sh) for t, sh in zip(lst, shapes)] for lst in (sm_g, sm_d, sm_nm, sm_nv)]

    def order(sm, idx):
        return [sm[0], sm[1], r_win[idx], sm[2], sm[3], sm[4], sm[5], sm[6], r_wout[idx], sm[7], sm[8],
                r_wkv[idx], r_wq[idx], r_wo[idx], r_wmi[idx], r_wmo[idx], sm[9]]

    return (loss, dx.reshape(x.shape), *order(sm_g, 0), *order(sm_d, 1), *order(sm_nm, 2), *order(sm_nv, 3))
```

```python
import functools
import math

import jax
import jax.numpy as jnp
from jax import lax
from jax.experimental import pallas as pl
from jax.experimental.pallas import tpu as pltpu

F32 = jnp.float32
BF16 = jnp.bfloat16
I32 = jnp.int32

NORM_EPS = 1e-6
LN_EPS = 1e-5
HEAD_DIM = 128
N_KV_HEADS = 4
ROT_DIM = 32
ROPE_THETA = 500000.0
CONV_WIDTH = 31
CONV_PAD = 32
ATT_BLOCK = 128
ATT_STEP_BLOCKS = 8
DILATIONS = (1, 4, 16)
ADAM_LR = 0.001
ADAM_B1 = 0.9
ADAM_B2 = 0.999
ADAM_EPS = 1e-08
ADAM_WD = 0.01
ADAM_STEP = 10
N_SHARD = 4
N_DEV = 8
LANES = 128
VMEM_LIMIT = 48 * 1024 * 1024
MM_TM, MM_TN, MM_TK = 1024, 1024, 2048
ROW_TILE = 256
CONV_CB = 128
CONV_T = 128
SMALL_ROWS = 48
MESH = pl.DeviceIdType.MESH
ANY = pl.BlockSpec(memory_space=pl.ANY)
HBM = pl.BlockSpec(memory_space=pltpu.HBM)
SEM = pl.BlockSpec(memory_space=pltpu.SEMAPHORE)
SPLIT_EFFECT = pltpu.SideEffectType.DATAFLOW_SIDE_EFFECTING


def _params(*sem):
    return pltpu.CompilerParams(dimension_semantics=sem, vmem_limit_bytes=VMEM_LIMIT)


def _sigmoid(x):
    return 1.0 / (1.0 + jnp.exp(-x))


def _wspec(kind, arr_shape, br, bc, pick):
    if kind == "plain":
        return pl.BlockSpec((br, bc), pick)
    per = arr_shape[2] // bc

    def idx(*g):
        rb, cb = pick(*g)
        return (cb // per, rb, cb % per)

    return pl.BlockSpec((None, br, bc), idx)


def _stage_shape(rows, w):
    return (w // LANES, rows, LANES)


def _to_residues(val, stage_ref, out_refs, dils):
    planes, rows, _ = stage_ref.shape
    for c in range(planes):
        stage_ref[c] = val[:, c * LANES:(c + 1) * LANES]
    for out_ref, dil in zip(out_refs, dils):
        if dil == 1:
            out_ref[0] = val.astype(out_ref.dtype)
            continue
        for r in range(dil):
            for c in range(planes):
                out_ref[r, :, c * LANES:(c + 1) * LANES] = stage_ref.at[c][pl.ds(r, rows // dil, stride=dil), :].astype(
                    out_ref.dtype)


def _from_residues(src_ref, stage_ref, dil):
    planes, rows, _ = stage_ref.shape
    if dil == 1:
        return lambda c: src_ref[0, :, c * LANES:(c + 1) * LANES].astype(F32)
    for r in range(dil):
        for c in range(planes):
            stage_ref.at[c][pl.ds(r, rows // dil, stride=dil), :] = src_ref[r, :, c * LANES:(c + 1) * LANES].astype(F32)
    return lambda c: stage_ref[c]


def _matmul(name, mode, a, b, *, m, n, k, tn=MM_TN, a_kind="plain", b_kind="plain", outs, extras=(), epilogue=None,
            stage=False):
    tm, tn, tk = min(MM_TM, m), min(tn, n), min(MM_TK, k)
    if b_kind == "col" and mode in ("nn", "tn"):
        tn = min(tn, n // b.shape[0])
    if b_kind == "col" and mode == "nt":
        tk = min(tk, k // b.shape[0])
    if a_kind == "col":
        assert mode == "nt"
        tk = min(tk, k // a.shape[0])
    if any(kind == "col" for _, kind in outs):
        tn = min(tn, n // N_SHARD)
    assert m % tm == 0 and n % tn == 0 and k % tk == 0, (name, m, n, k, tm, tn, tk)
    nk = k // tk
    grid = (m // tm, n // tn, nk)
    if mode == "nn":
        a_spec = pl.BlockSpec((tm, tk), lambda i, j, kk: (i, kk))
        b_spec = _wspec(b_kind, b.shape, tk, tn, lambda i, j, kk: (kk, j))
        dims = (((1,), (0,)), ((), ()))
    elif mode == "nt":
        a_spec = _wspec(a_kind, a.shape, tm, tk, lambda i, j, kk: (i, kk))
        b_spec = _wspec(b_kind, b.shape, tn, tk, lambda i, j, kk: (j, kk))
        dims = (((1,), (1,)), ((), ()))
    else:
        a_spec = pl.BlockSpec((tk, tm), lambda i, j, kk: (kk, i))
        b_spec = _wspec(b_kind, b.shape, tk, tn, lambda i, j, kk: (kk, j))
        dims = (((0,), (0,)), ((), ()))
    out_shape, out_specs = [], []
    for dtype, kind in outs:
        if isinstance(kind, tuple):
            dil = kind[1]
            out_shape.append(jax.ShapeDtypeStruct((dil, m // dil, n), dtype))
            out_specs.append(pl.BlockSpec((dil, tm // dil, tn), lambda i, j, kk: (0, i, j)))
            continue
        shape = (m, n) if kind == "plain" else (N_SHARD, m, n // N_SHARD)
        out_shape.append(jax.ShapeDtypeStruct(shape, dtype))
        out_specs.append(_wspec(kind, shape, tm, tn, lambda i, j, kk: (i, j)))
    n_ex = len(extras)
    ex_specs = {"ij": pl.BlockSpec((tm, tn), lambda i, j, kk: (i, j)),
                "vec": pl.BlockSpec((1, tn), lambda i, j, kk: (0, j)),
                "rows": pl.BlockSpec((tm, LANES), lambda i, j, kk: (i, 0))}

    def body(*refs):
        a_ref, b_ref = refs[0], refs[1]
        ex_refs = refs[2:2 + n_ex]
        out_refs = refs[2 + n_ex:2 + n_ex + len(outs)]
        j = pl.program_id(1)

        def finish(res):
            if epilogue is None:
                out_refs[0][...] = res.astype(out_refs[0].dtype)
            elif stage:
                epilogue(res, ex_refs, out_refs, j, refs[-1])
            else:
                epilogue(res, ex_refs, out_refs, j)

        prod = lax.dot_general(a_ref[...], b_ref[...], dims, preferred_element_type=F32)
        if nk == 1:
            finish(prod)
            return
        acc_ref = refs[2 + n_ex + len(outs)]
        kk = pl.program_id(2)

        @pl.when(kk == 0)
        def _():
            acc_ref[...] = prod

        @pl.when(kk > 0)
        def _():
            acc_ref[...] += prod

        @pl.when(kk == nk - 1)
        def _():
            finish(acc_ref[...])

    res = pl.pallas_call(
        body, name=name, grid=grid,
        in_specs=[a_spec, b_spec] + [ex_specs[how] for _, how in extras],
        out_specs=out_specs, out_shape=out_shape,
        scratch_shapes=[pltpu.VMEM((tm, tn), F32)] * (nk > 1) + [pltpu.VMEM(_stage_shape(tm, tn), F32)] * bool(stage),
        compiler_params=_params("parallel", "parallel", "arbitrary"),
    )(a, b, *[e for e, _ in extras])
    return res


def _rope_tables(seq):
    half = ROT_DIM // 2
    pos = jnp.arange(seq, dtype=F32)
    inv = ROPE_THETA ** (-jnp.arange(0, ROT_DIM, 2, dtype=F32) / ROT_DIM)
    ang = pos[:, None] * inv[None, :]
    cos, sin = jnp.cos(ang), jnp.sin(ang)
    zeros = jnp.zeros((seq, HEAD_DIM - ROT_DIM), F32)
    ctab = jnp.concatenate([cos, cos, zeros + 1.0], axis=1)
    atab = jnp.concatenate([-sin, jnp.zeros((seq, half), F32), zeros], axis=1)
    btab = jnp.concatenate([jnp.zeros((seq, half), F32), sin, zeros], axis=1)
    return ctab, atab, btab


def _rope_apply(x, ctab, atab, btab, sign):
    w = x.shape[1]
    reps = w // HEAD_DIM
    half = ROT_DIM // 2
    c = jnp.tile(ctab, (1, reps))
    a = jnp.tile(atab, (1, reps))
    b = jnp.tile(btab, (1, reps))
    up = pltpu.roll(x, w - half, 1)
    down = pltpu.roll(x, half, 1)
    return x * c + sign * (up * a + down * b)


def _rows(t, w):
    return pl.BlockSpec((t, w), lambda i: (i, 0))


def _fixed(shape):
    nd = len(shape)
    return pl.BlockSpec(shape, lambda i: (0,) * nd)


def _rms_fwd(name, x, gains):
    s, d = x.shape
    t = min(ROW_TILE, s)
    ng = len(gains)

    def body(x_ref, *refs):
        xv = x_ref[...]
        r = lax.rsqrt(jnp.mean(xv * xv, axis=-1, keepdims=True) + NORM_EPS)
        xn = xv * r
        for g_ref, y_ref in zip(refs[:ng], refs[ng:]):
            y_ref[...] = (xn * g_ref[...]).astype(BF16)

    return pl.pallas_call(
        body, name=name, grid=(s // t,),
        in_specs=[_rows(t, d)] + [_fixed((1, d))] * ng,
        out_specs=[_rows(t, d)] * ng,
        out_shape=[jax.ShapeDtypeStruct((s, d), BF16)] * ng,
        compiler_params=_params("parallel"),
    )(x, *gains)


def _rms_bwd(name, x, pairs, dh_in, want_colsum=False):
    s, d = x.shape
    t = min(ROW_TILE, s)
    n_p = len(pairs)

    def body(x_ref, dh_ref, *refs):
        g_refs = refs[:n_p]
        dy_refs = refs[n_p:2 * n_p]
        dh_out, dhb_out = refs[2 * n_p], refs[2 * n_p + 1]
        dg_refs = refs[2 * n_p + 2:2 * n_p + 2 + n_p]
        cs_ref = refs[-1] if want_colsum else None
        i = pl.program_id(0)
        xv = x_ref[...]
        r = lax.rsqrt(jnp.mean(xv * xv, axis=-1, keepdims=True) + NORM_EPS)
        xn = xv * r
        dh = dh_ref[...]
        for g_ref, dy_ref, dg_ref in zip(g_refs, dy_refs, dg_refs):
            dy = dy_ref[...].astype(F32)
            u = dy * g_ref[...]
            dh = dh + r * (u - xn * jnp.mean(u * xn, axis=-1, keepdims=True))
            part = jnp.sum(dy * xn, axis=0, keepdims=True)

            @pl.when(i == 0)
            def _():
                dg_ref[...] = part

            @pl.when(i > 0)
            def _():
                dg_ref[...] += part

        dh_out[...] = dh
        dhb_out[...] = dh.astype(BF16)
        if want_colsum:
            col = jnp.sum(dh, axis=0, keepdims=True)

            @pl.when(i == 0)
            def _():
                cs_ref[...] = col

            @pl.when(i > 0)
            def _():
                cs_ref[...] += col

    n_vec = n_p + (1 if want_colsum else 0)
    return pl.pallas_call(
        body, name=name, grid=(s // t,),
        in_specs=[_rows(t, d), _rows(t, d)] + [_fixed((1, d))] * n_p + [_rows(t, d)] * n_p,
        out_specs=[_rows(t, d), _rows(t, d)] + [_fixed((1, d))] * n_vec,
        out_shape=[jax.ShapeDtypeStruct((s, d), F32), jax.ShapeDtypeStruct((s, d), BF16)]
        + [jax.ShapeDtypeStruct((1, d), F32)] * n_vec,
        compiler_params=_params("arbitrary"),
    )(x, dh_in, *[g for g, _ in pairs], *[dy for _, dy in pairs])


def _final_loss(x, g, target):
    s, d = x.shape
    t = min(ROW_TILE, s)

    def body(x_ref, g_ref, t_ref, dh_out, dhb_out, dg_ref, loss_ref):
        i = pl.program_id(0)
        xv = x_ref[...]
        gv = g_ref[...]
        r = lax.rsqrt(jnp.mean(xv * xv, axis=-1, keepdims=True) + NORM_EPS)
        xn = xv * r
        diff = xn * gv - t_ref[...]
        dy = diff / d
        u = dy * gv
        dh = r * (u - xn * jnp.mean(u * xn, axis=-1, keepdims=True))
        dh_out[...] = dh
        dhb_out[...] = dh.astype(BF16)
        dg = jnp.sum(dy * xn, axis=0, keepdims=True)
        lc = jnp.sum(0.5 * diff * dy, axis=0, keepdims=True)

        @pl.when(i == 0)
        def _():
            dg_ref[...] = dg
            loss_ref[...] = lc

        @pl.when(i > 0)
        def _():
            dg_ref[...] += dg
            loss_ref[...] += lc

    return pl.pallas_call(
        body, name="final_loss", grid=(s // t,),
        in_specs=[_rows(t, d), _fixed((1, d)), _rows(t, d)],
        out_specs=[_rows(t, d), _rows(t, d), _fixed((1, d)), _fixed((1, d))],
        out_shape=[jax.ShapeDtypeStruct((s, d), F32), jax.ShapeDtypeStruct((s, d), BF16),
                   jax.ShapeDtypeStruct((1, d), F32), jax.ShapeDtypeStruct((1, d), F32)],
        compiler_params=_params("arbitrary"),
    )(x, g, target)


def _ln_silu_fwd(c, g, b):
    s, d = c.shape
    t = min(ROW_TILE, s)

    def body(c_ref, g_ref, b_ref, s_ref):
        cv = c_ref[...]
        mu = jnp.mean(cv, axis=-1, keepdims=True)
        xc = cv - mu
        rs = lax.rsqrt(jnp.mean(xc * xc, axis=-1, keepdims=True) + LN_EPS)
        ln = xc * rs * g_ref[...] + b_ref[...]
        s_ref[...] = (ln * _sigmoid(ln)).astype(BF16)

    return pl.pallas_call(
        body, name="ln_silu_fwd", grid=(s // t,),
        in_specs=[_rows(t, d), _fixed((1, d)), _fixed((1, d))],
        out_specs=_rows(t, d), out_shape=jax.ShapeDtypeStruct((s, d), BF16),
        compiler_params=_params("parallel"),
    )(c, g, b)


def _ln_silu_bwd(c, g, b, ds):
    s, d = c.shape
    t = min(ROW_TILE, s)

    def body(c_ref, g_ref, b_ref, ds_ref, dc_ref, dg_ref, db_ref, dbdw_ref):
        i = pl.program_id(0)
        cv = c_ref[...]
        gv = g_ref[...]
        mu = jnp.mean(cv, axis=-1, keepdims=True)
        xc = cv - mu
        rs = lax.rsqrt(jnp.mean(xc * xc, axis=-1, keepdims=True) + LN_EPS)
        nrm = xc * rs
        ln = nrm * gv + b_ref[...]
        sig = _sigmoid(ln)
        dln = ds_ref[...].astype(F32) * sig * (1.0 + ln * (1.0 - sig))
        dn = dln * gv
        dc = rs * (dn - jnp.mean(dn, axis=-1, keepdims=True)
                   - nrm * jnp.mean(dn * nrm, axis=-1, keepdims=True))
        dc_ref[...] = dc
        pg = jnp.sum(dln * nrm, axis=0, keepdims=True)
        pb = jnp.sum(dln, axis=0, keepdims=True)
        pc = jnp.sum(dc, axis=0, keepdims=True)

        @pl.when(i == 0)
        def _():
            dg_ref[...] = pg
            db_ref[...] = pb
            dbdw_ref[...] = pc

        @pl.when(i > 0)
        def _():
            dg_ref[...] += pg
            db_ref[...] += pb
            dbdw_ref[...] += pc

    return pl.pallas_call(
        body, name="ln_silu_bwd", grid=(s // t,),
        in_specs=[_rows(t, d), _fixed((1, d)), _fixed((1, d)), _rows(t, d)],
        out_specs=[_rows(t, d)] + [_fixed((1, d))] * 3,
        out_shape=[jax.ShapeDtypeStruct((s, d), F32)] + [jax.ShapeDtypeStruct((1, d), F32)] * 3,
        compiler_params=_params("arbitrary"),
    )(c, g, b, ds)


def _residue_spec(dil, t, w):
    return pl.BlockSpec((dil, t // dil, w), lambda i: (0, i, 0))


def _attn_combine(o_list, lse_list):
    dil0, sd0, d = o_list[0].shape
    s = dil0 * sd0
    lw = lse_list[0].shape[2]
    group = d // HEAD_DIM // N_KV_HEADS
    t = min(ROW_TILE, s)
    nb = len(o_list)
    dils = [o.shape[0] for o in o_list]

    def body(*refs):
        o_out, l_out = refs[2 * nb], refs[2 * nb + 1]
        o_stage, l_stage = refs[2 * nb + 2:3 * nb + 2], refs[3 * nb + 2:]
        o_planes = [_from_residues(src, stage, dil) for src, stage, dil in zip(refs[:nb], o_stage, dils)]
        l_planes = [_from_residues(src, stage, dil) for src, stage, dil in zip(refs[nb:2 * nb], l_stage, dils)]
        for kh in range(N_KV_HEADS):
            ls = [plane(kh) for plane in l_planes]
            mx = ls[0]
            for l in ls[1:]:
                mx = jnp.maximum(mx, l)
            es = [jnp.exp(l - mx) for l in ls]
            den = es[0]
            for e in es[1:]:
                den = den + e
            l_out[:, kh * LANES:(kh + 1) * LANES] = mx + jnp.log(den)
            ws = [e / den for e in es]
            for g in range(group):
                h = kh * group + g
                acc = jnp.zeros((t, HEAD_DIM), F32)
                for plane, w in zip(o_planes, ws):
                    acc = acc + w[:, g:g + 1] * plane(h)
                o_out[:, h * HEAD_DIM:(h + 1) * HEAD_DIM] = acc.astype(BF16)

    return pl.pallas_call(
        body, name="attn_combine", grid=(s // t,),
        in_specs=[_residue_spec(dil, t, d) for dil in dils] + [_residue_spec(dil, t, lw) for dil in dils],
        out_specs=[_rows(t, d), _rows(t, lw)],
        out_shape=[jax.ShapeDtypeStruct((s, d), BF16), jax.ShapeDtypeStruct((s, lw), F32)],
        scratch_shapes=[pltpu.VMEM(_stage_shape(t, d), F32)] * nb + [pltpu.VMEM(_stage_shape(t, lw), F32)] * nb,
        compiler_params=_params("parallel"),
    )(*o_list, *lse_list)


def _attn_delta(do, o, lse, dils):
    s, d = o.shape
    lw = lse.shape[1]
    group = d // HEAD_DIM // N_KV_HEADS
    t = min(ROW_TILE, s)
    nd = len(dils)

    def body(do_ref, o_ref, lse_ref, *refs):
        stage = refs[-1]
        lane = lax.broadcasted_iota(I32, (t, LANES), 1)
        planes = []
        for kh in range(N_KV_HEADS):
            out = jnp.zeros((t, LANES), F32)
            for g in range(group):
                cols = slice((kh * group + g) * HEAD_DIM, (kh * group + g + 1) * HEAD_DIM)
                v = jnp.sum(do_ref[:, cols].astype(F32) * o_ref[:, cols].astype(F32), axis=-1, keepdims=True)
                out = jnp.where(lane == g, v, out)
            planes.append(out)
        _to_residues(lse_ref[...], stage, refs[:nd], dils)
        _to_residues(jnp.concatenate(planes, axis=1), stage, refs[nd:2 * nd], dils)

    res = pl.pallas_call(
        body, name="attn_delta", grid=(s // t,),
        in_specs=[_rows(t, d), _rows(t, d), _rows(t, lw)],
        out_specs=[_residue_spec(dil, t, lw) for dil in dils] * 2,
        out_shape=[jax.ShapeDtypeStruct((dil, s // dil, lw), F32) for dil in dils] * 2,
        scratch_shapes=[pltpu.VMEM(_stage_shape(t, lw), F32)],
        compiler_params=_params("parallel"),
    )(do, o, lse)
    return res[:nd], res[nd:]


def _residue_sum(name, groups, tabs):
    first = groups[0][0][0]
    s, w = first.shape[0] * first.shape[1], first.shape[2]
    t = min(ROW_TILE, s)
    flat = [p for parts, _ in groups for p in parts]

    def body(*refs):
        c_ref, a_ref, b_ref = refs[len(flat):len(flat) + 3]
        out = refs[len(flat) + 3]
        stages = refs[len(flat) + 4:]
        k = 0
        for gi, (parts, rotate) in enumerate(groups):
            planes = [_from_residues(refs[k + i], stages[k + i], p.shape[0]) for i, p in enumerate(parts)]
            k += len(parts)
            for c in range(w // LANES):
                tot = planes[0](c)
                for plane in planes[1:]:
                    tot = tot + plane(c)
                if rotate:
                    tot = _rope_apply(tot, c_ref[...], a_ref[...], b_ref[...], -1.0)
                out[:, gi * w + c * LANES:gi * w + (c + 1) * LANES] = tot.astype(BF16)

    return pl.pallas_call(
        body, name=name, grid=(s // t,),
        in_specs=[_residue_spec(p.shape[0], t, w) for p in flat] + [_rows(t, HEAD_DIM)] * 3,
        out_specs=_rows(t, len(groups) * w), out_shape=jax.ShapeDtypeStruct((s, len(groups) * w), BF16),
        scratch_shapes=[pltpu.VMEM(_stage_shape(t, w), F32) for _ in flat],
        compiler_params=_params("parallel"),
    )(*flat, *tabs)


def _dwconv_fwd(u, w_dw, b_dw):
    s, d2 = u.shape
    d = d2 // 2
    cb = min(CONV_CB, d)
    nblk = d // cb
    tt = min(CONV_T, s)

    def body(ua_ref, ug_ref, w_ref, b_ref, c_ref, xp_ref):
        gl = ua_ref[...].astype(F32) * _sigmoid(ug_ref[...].astype(F32))
        xp_ref[0:CONV_PAD, :] = jnp.zeros((CONV_PAD, cb), F32)
        xp_ref[CONV_PAD:, :] = gl
        wv = w_ref[...]
        bv = b_ref[...]
        for t0 in range(0, s, tt):
            acc = jnp.zeros((tt, cb), F32) + bv
            for kk in range(CONV_WIDTH):
                off = t0 + CONV_PAD - (CONV_WIDTH - 1) + kk
                acc = acc + wv[kk:kk + 1, :] * xp_ref[off:off + tt, :]
            c_ref[t0:t0 + tt, :] = acc

    return pl.pallas_call(
        body, name="dwconv_fwd", grid=(nblk,),
        in_specs=[pl.BlockSpec((s, cb), lambda j: (0, j)), pl.BlockSpec((s, cb), lambda j: (0, j + nblk)),
                  pl.BlockSpec((CONV_PAD, cb), lambda j: (0, j)), pl.BlockSpec((1, cb), lambda j: (0, j))],
        out_specs=pl.BlockSpec((s, cb), lambda j: (0, j)),
        out_shape=jax.ShapeDtypeStruct((s, d), F32),
        scratch_shapes=[pltpu.VMEM((s + CONV_PAD, cb), F32)],
        compiler_params=_params("parallel"),
    )(u, u, w_dw, b_dw)


def _dwconv_bwd(u, w_dw, dc):
    s, d2 = u.shape
    d = d2 // 2
    cb = min(CONV_CB, d)
    nblk = d // cb
    tt = min(CONV_T, s)

    def body(ua_ref, ug_ref, w_ref, dc_ref, du_ref, dw_ref, dba_ref, dbg_ref, glp_ref, dcp_ref, acc_ref):
        a = ua_ref[...].astype(F32)
        sig = _sigmoid(ug_ref[...].astype(F32))
        glp_ref[0:CONV_PAD, :] = jnp.zeros((CONV_PAD, cb), F32)
        glp_ref[CONV_PAD:, :] = a * sig
        dcp_ref[0:s, :] = dc_ref[...]
        dcp_ref[s:, :] = jnp.zeros((CONV_PAD, cb), F32)
        acc_ref[...] = jnp.zeros_like(acc_ref)
        wv = w_ref[...]
        dba = jnp.zeros((1, cb), F32)
        dbg = jnp.zeros((1, cb), F32)
        for t0 in range(0, s, tt):
            dgl = jnp.zeros((tt, cb), F32)
            dct = dc_ref[t0:t0 + tt, :]
            for kk in range(CONV_WIDTH):
                off = t0 + (CONV_WIDTH - 1) - kk
                dgl = dgl + wv[kk:kk + 1, :] * dcp_ref[off:off + tt, :]
                goff = t0 + CONV_PAD - (CONV_WIDTH - 1) + kk
                prod = dct * glp_ref[goff:goff + tt, :]
                acc_ref[8 * kk:8 * kk + 8, :] += jnp.sum(prod.reshape(tt // 8, 8, cb), axis=0)
            at = ua_ref[t0:t0 + tt, :].astype(F32)
            st = _sigmoid(ug_ref[t0:t0 + tt, :].astype(F32))
            da = dgl * st
            dg = dgl * at * st * (1.0 - st)
            du_ref[0, t0:t0 + tt, :] = da.astype(BF16)
            du_ref[1, t0:t0 + tt, :] = dg.astype(BF16)
            dba = dba + jnp.sum(da, axis=0, keepdims=True)
            dbg = dbg + jnp.sum(dg, axis=0, keepdims=True)
        dba_ref[...] = dba
        dbg_ref[...] = dbg
        for kk in range(CONV_WIDTH):
            dw_ref[kk:kk + 1, :] = jnp.sum(acc_ref[8 * kk:8 * kk + 8, :], axis=0, keepdims=True)
        dw_ref[CONV_WIDTH:, :] = jnp.zeros((CONV_PAD - CONV_WIDTH, cb), F32)

    blk = pl.BlockSpec((s, cb), lambda j: (0, j))
    vec = pl.BlockSpec((1, cb), lambda j: (0, j))
    return pl.pallas_call(
        body, name="dwconv_bwd", grid=(nblk,),
        in_specs=[blk, pl.BlockSpec((s, cb), lambda j: (0, j + nblk)),
                  pl.BlockSpec((CONV_PAD, cb), lambda j: (0, j)), blk],
        out_specs=[pl.BlockSpec((2, s, cb), lambda j: (0, 0, j)), pl.BlockSpec((CONV_PAD, cb), lambda j: (0, j)),
                   vec, vec],
        out_shape=[jax.ShapeDtypeStruct((2, s, d), BF16), jax.ShapeDtypeStruct((CONV_PAD, d), F32),
                   jax.ShapeDtypeStruct((1, d), F32), jax.ShapeDtypeStruct((1, d), F32)],
        scratch_shapes=[pltpu.VMEM((s + CONV_PAD, cb), F32), pltpu.VMEM((s + CONV_PAD, cb), F32),
                        pltpu.VMEM((8 * CONV_PAD, cb), F32)],
        compiler_params=_params("parallel"),
    )(u, u, w_dw, dc)


def _stack_heads(x, group):
    return jnp.concatenate([x[:, g * HEAD_DIM:(g + 1) * HEAD_DIM] for g in range(group)], axis=0)


def _unstack_heads(x, group):
    return jnp.concatenate([x[g * ATT_BLOCK:(g + 1) * ATT_BLOCK, :] for g in range(group)], axis=1)


def _stack_cols(x, group):
    return jnp.concatenate([x[:, g:g + 1] for g in range(group)], axis=0)


def _band_mask(nb, group):
    rows = group * ATT_BLOCK
    row = lax.broadcasted_iota(I32, (rows, 2 * ATT_BLOCK), 0) % ATT_BLOCK
    col = lax.broadcasted_iota(I32, (rows, 2 * ATT_BLOCK), 1)
    return (col >= row) & (col <= row + ATT_BLOCK) & ((col >= ATT_BLOCK) | (nb > 0))


def _window(ref, nb):
    prev = pl.multiple_of(jnp.maximum(nb - 1, 0) * ATT_BLOCK, ATT_BLOCK)
    cur = pl.multiple_of(nb * ATT_BLOCK, ATT_BLOCK)
    return jnp.concatenate([ref[pl.ds(prev, ATT_BLOCK), :], ref[pl.ds(cur, ATT_BLOCK), :]], axis=0)


def _residues_per_step(dil, nblk):
    return max(1, min(dil, ATT_STEP_BLOCKS // nblk))


def _attn_fwd(name, q, kv):
    dil, sd, d = q.shape
    group = d // HEAD_DIM // N_KV_HEADS
    gw = group * HEAD_DIM
    nblk = sd // ATT_BLOCK
    scale = 1.0 / math.sqrt(HEAD_DIM)
    nt = (((1,), (1,)), ((), ()))

    rb = _residues_per_step(dil, nblk)

    def body(q_all, k_all, v_all, o_all, lse_all):
        lane = lax.broadcasted_iota(I32, (ATT_BLOCK, LANES), 1)
        for rr in range(rb):
            q_ref, k_ref, v_ref, o_ref, lse_ref = [ref.at[rr] for ref in (q_all, k_all, v_all, o_all, lse_all)]

            def step(nb, carry):
                rows = pl.ds(pl.multiple_of(nb * ATT_BLOCK, ATT_BLOCK), ATT_BLOCK)
                qs = _stack_heads(q_ref[rows, :], group)
                kw = _window(k_ref, nb)
                vw = _window(v_ref, nb)
                sc = lax.dot_general(qs, kw, nt, preferred_element_type=F32) * scale
                sc = jnp.where(_band_mask(nb, group), sc, -jnp.inf)
                mx = jnp.max(sc, axis=-1, keepdims=True)
                p = jnp.exp(sc - mx)
                l = jnp.sum(p, axis=-1, keepdims=True)
                o = jnp.dot(p.astype(BF16), vw, preferred_element_type=F32) / l
                o_ref[rows, :] = _unstack_heads(o, group).astype(BF16)
                lse = mx + jnp.log(l)
                out = jnp.zeros((ATT_BLOCK, LANES), F32)
                for g in range(group):
                    out = jnp.where(lane == g, lse[g * ATT_BLOCK:(g + 1) * ATT_BLOCK, :], out)
                lse_ref[rows, :] = out
                return carry

            lax.fori_loop(0, nblk, step, 0, unroll=min(2, nblk))

    kvh = N_KV_HEADS
    qspec = pl.BlockSpec((rb, sd, gw), lambda r, h: (r, 0, h))
    kspec = pl.BlockSpec((rb, sd, HEAD_DIM), lambda r, h: (r, 0, h))
    return pl.pallas_call(
        body, name=name, grid=(dil // rb, kvh),
        in_specs=[qspec, kspec, pl.BlockSpec((rb, sd, HEAD_DIM), lambda r, h: (r, 0, kvh + h))],
        out_specs=[qspec, kspec],
        out_shape=[jax.ShapeDtypeStruct((dil, sd, d), BF16),
                   jax.ShapeDtypeStruct((dil, sd, kvh * LANES), F32)],
        compiler_params=_params("parallel", "parallel"),
    )(q, kv, kv)


def _attn_bwd(name, q, kv, do, lse, delta):
    dil, sd, d = q.shape
    group = d // HEAD_DIM // N_KV_HEADS
    gw = group * HEAD_DIM
    nblk = sd // ATT_BLOCK
    scale = 1.0 / math.sqrt(HEAD_DIM)
    nt = (((1,), (1,)), ((), ()))
    tn = (((0,), (0,)), ((), ()))

    rb = _residues_per_step(dil, nblk)

    def body(q_all, k_all, v_all, do_all, lse_all, dl_all, dq_all, dk_all, dv_all, dk_accs, dv_accs):
        dk_accs[...] = jnp.zeros_like(dk_accs)
        dv_accs[...] = jnp.zeros_like(dv_accs)
        for rr in range(rb):
            q_ref, k_ref, v_ref, do_ref, lse_ref, dl_ref, dq_ref, dk_ref, dv_ref, dk_acc, dv_acc = [
                ref.at[rr] for ref in (q_all, k_all, v_all, do_all, lse_all, dl_all, dq_all, dk_all, dv_all,
                                       dk_accs, dv_accs)]

            def step(nb, carry):
                rows = pl.ds(pl.multiple_of(nb * ATT_BLOCK, ATT_BLOCK), ATT_BLOCK)
                qs = _stack_heads(q_ref[rows, :], group)
                dos = _stack_heads(do_ref[rows, :], group)
                ls = _stack_cols(lse_ref[rows, :], group)
                dl = _stack_cols(dl_ref[rows, :], group)
                kw = _window(k_ref, nb)
                vw = _window(v_ref, nb)
                sc = lax.dot_general(qs, kw, nt, preferred_element_type=F32) * scale
                sc = jnp.where(_band_mask(nb, group), sc, -jnp.inf)
                p = jnp.exp(sc - ls)
                dp = lax.dot_general(dos, vw, nt, preferred_element_type=F32)
                ds = (p * (dp - dl) * scale).astype(BF16)
                dq = jnp.dot(ds, kw, preferred_element_type=F32)
                dq_ref[rows, :] = _unstack_heads(dq, group).astype(BF16)
                win = pl.ds(pl.multiple_of(nb * ATT_BLOCK, ATT_BLOCK), 2 * ATT_BLOCK)
                dk_acc[win, :] += lax.dot_general(ds, qs, tn, preferred_element_type=F32)
                dv_acc[win, :] += lax.dot_general(p.astype(BF16), dos, tn, preferred_element_type=F32)
                return carry

            lax.fori_loop(0, nblk, step, 0, unroll=min(2, nblk))
            dk_ref[...] = dk_acc[ATT_BLOCK:, :]
            dv_ref[...] = dv_acc[ATT_BLOCK:, :]

    kvh = N_KV_HEADS
    qspec = pl.BlockSpec((rb, sd, gw), lambda r, h: (r, 0, h))
    kspec = pl.BlockSpec((rb, sd, HEAD_DIM), lambda r, h: (r, 0, h))
    return pl.pallas_call(
        body, name=name, grid=(dil // rb, kvh),
        in_specs=[qspec, kspec, pl.BlockSpec((rb, sd, HEAD_DIM), lambda r, h: (r, 0, kvh + h)),
                  qspec, kspec, kspec],
        out_specs=[qspec, kspec, kspec],
        out_shape=[jax.ShapeDtypeStruct((dil, sd, d), BF16),
                   jax.ShapeDtypeStruct((dil, sd, kvh * HEAD_DIM), F32),
                   jax.ShapeDtypeStruct((dil, sd, kvh * HEAD_DIM), F32)],
        scratch_shapes=[pltpu.VMEM((rb, sd + ATT_BLOCK, HEAD_DIM), F32)] * 2,
        compiler_params=_params("parallel", "parallel"),
    )(q, kv, kv, do, lse, delta)


def _cast_bf16(name, w, layer, place, after=None):
    _, r, c = w.shape
    tr = min(512, r)
    deps = [] if after is None else [after]

    def body(pl_ref, w_ref, *refs):
        refs[-1][...] = w_ref[...].astype(BF16)

    return pl.pallas_call(
        body, name=name,
        grid_spec=pltpu.PrefetchScalarGridSpec(
            num_scalar_prefetch=1, grid=(r // tr,),
            in_specs=[pl.BlockSpec((None, tr, c), lambda i, p: (layer, i, 0))] + [ANY] * len(deps),
            out_specs=pl.BlockSpec((None, tr, c), lambda i, p: (p[1], i, 0))),
        out_shape=jax.ShapeDtypeStruct((N_SHARD, r, c), BF16),
        compiler_params=_params("parallel"),
    )(place, w, *deps)


def _chip_sum(name, g, rh, place):
    _, r, c = g.shape
    rh2 = r // 2
    tr = min(512, rh2)
    nb = rh2 // tr

    def body(pl_ref, g_ref, rh_ref, o_ref):
        o_ref[...] = (g_ref[...].astype(F32) + rh_ref[...].astype(F32)).astype(BF16)

    return pl.pallas_call(
        body, name=name,
        grid_spec=pltpu.PrefetchScalarGridSpec(
            num_scalar_prefetch=1, grid=(N_SHARD, nb),
            in_specs=[pl.BlockSpec((None, tr, c), lambda s, i, p: (s, p[0] * nb + i, 0)),
                      pl.BlockSpec((None, tr, c), lambda s, i, p: (s, i, 0))],
            out_specs=pl.BlockSpec((None, tr, c), lambda s, i, p: (s, i, 0))),
        out_shape=jax.ShapeDtypeStruct((N_SHARD, rh2, c), BF16),
        compiler_params=_params("parallel", "parallel"),
    )(place, g, rh)


def _owner_sum(name, cs, rp, place):
    _, rh2, c = cs.shape
    tr = min(512, rh2)
    nb = rh2 // tr

    def body(pl_ref, cs_ref, r0_ref, r1_ref, r2_ref, o_ref):
        o_ref[...] = ((cs_ref[...].astype(F32) + r0_ref[...].astype(F32))
                      + (r1_ref[...].astype(F32) + r2_ref[...].astype(F32)))

    def rspec(j):
        return pl.BlockSpec((None, tr, c), lambda i, p: (j, i, 0))

    return pl.pallas_call(
        body, name=name,
        grid_spec=pltpu.PrefetchScalarGridSpec(
            num_scalar_prefetch=1, grid=(nb,),
            in_specs=[pl.BlockSpec((None, tr, c), lambda i, p: (p[1], i, 0)), rspec(0), rspec(1), rspec(2)],
            out_specs=pl.BlockSpec((tr, c), lambda i, p: (p[0] * nb + i, 0))),
        out_shape=jax.ShapeDtypeStruct((2 * rh2, c), F32),
        compiler_params=_params("parallel"),
    )(place, cs, rp, rp, rp)


def _adam_math(w, g, m, v):
    m = ADAM_B1 * m + (1.0 - ADAM_B1) * g
    v = ADAM_B2 * v + (1.0 - ADAM_B2) * (g * g)
    m_hat = m / (1.0 - ADAM_B1 ** ADAM_STEP)
    v_hat = v / (1.0 - ADAM_B2 ** ADAM_STEP)
    delta = -ADAM_LR * (m_hat / (jnp.sqrt(v_hat) + ADAM_EPS) + ADAM_WD * w)
    return delta, m, v


def _adamw(name, w, m, v, g, layer, partial=None):
    nl, r, c = w.shape
    tr = min(256, r)

    def body(w_ref, m_ref, v_ref, g_ref, *refs):
        go_ref, d_ref, mo_ref, vo_ref = refs[-4:]
        gv = g_ref[...]
        delta, m_new, v_new = _adam_math(w_ref[...], gv, m_ref[...], v_ref[...])
        go_ref[...] = gv
        d_ref[...] = delta
        mo_ref[...] = m_new
        vo_ref[...] = v_new

    wspec = pl.BlockSpec((None, tr, c), lambda i: (layer, i, 0))
    prev = [] if partial is None else list(partial)
    return pl.pallas_call(
        body, name=name, grid=(r // tr,),
        in_specs=[wspec] * 3 + [pl.BlockSpec((tr, c), lambda i: (i, 0))] + [ANY] * len(prev),
        out_specs=[wspec] * 4,
        out_shape=[jax.ShapeDtypeStruct((nl, r, c), F32)] * 4,
        input_output_aliases={4 + i: i for i in range(len(prev))},
        compiler_params=_params("parallel"),
    )(w, m, v, g, *prev)


def _adam_small(ws, ms, vs, gs):
    n = len(ws)

    def body(*refs):
        w_refs, m_refs, v_refs, g_refs = refs[:n], refs[n:2 * n], refs[2 * n:3 * n], refs[3 * n:4 * n]
        d_refs, mo_refs, vo_refs = refs[4 * n:5 * n], refs[5 * n:6 * n], refs[6 * n:7 * n]
        for i in range(n):
            delta, m_new, v_new = _adam_math(w_refs[i][...], g_refs[i][...], m_refs[i][...], v_refs[i][...])
            d_refs[i][...] = delta
            mo_refs[i][...] = m_new
            vo_refs[i][...] = v_new

    shapes = [jax.ShapeDtypeStruct(w.shape, F32) for w in ws]
    res = pl.pallas_call(body, name="adam_small", out_shape=shapes * 3)(*ws, *ms, *vs, *gs)
    return res[:n], res[n:2 * n], res[2 * n:]


def _pack_small(b_in, w_dw, b_dw, ln_g, ln_b, b_out, place):
    cin = b_in.shape[1]
    cd = b_dw.shape[1]
    rows = 8 + CONV_PAD

    def body(pl_ref, bi, wd, bd, lg, lb, bo, out):
        out[...] = jnp.zeros_like(out)
        out[0:1, :] = bi[...]
        out[1:2, 0:cd] = bd[...]
        out[1:2, cd:2 * cd] = lg[...]
        out[2:3, 0:cd] = lb[...]
        out[2:3, cd:2 * cd] = bo[...]
        out[8:8 + CONV_WIDTH, 0:cd] = wd[...]

    def whole(arr):
        return pl.BlockSpec(arr.shape, lambda i, p: (0,) * arr.ndim)

    ins = [b_in, w_dw, b_dw, ln_g, ln_b, b_out]
    return pl.pallas_call(
        body, name="pack_small",
        grid_spec=pltpu.PrefetchScalarGridSpec(
            num_scalar_prefetch=1, grid=(1,), in_specs=[whole(a) for a in ins],
            out_specs=pl.BlockSpec((None, rows, cin), lambda i, p: (p[1], 0, 0))),
        out_shape=jax.ShapeDtypeStruct((N_SHARD, rows, cin), F32),
        compiler_params=_params("arbitrary"),
    )(place, *ins)


def _place():
    x, y, c = lax.axis_index("x"), lax.axis_index("y"), lax.axis_index("c")
    return x, y, c


def _other_chips(x, y):
    return [(1 - x, y), (x, 1 - y), (1 - x, 1 - y)]


def _split_start(name, bufs, n_sem, copies, after=None):
    n = len(bufs)
    deps = [] if after is None else [after]

    def body(*refs):
        out0 = n + len(deps)
        for cp in copies(refs[:n], refs[out0], refs[out0 + 1], False):
            cp.start()
        refs[-1][...] = jnp.zeros_like(refs[-1])

    res = pl.pallas_call(
        body, name=name,
        out_shape=(pltpu.SemaphoreType.DMA((n_sem,)), pltpu.SemaphoreType.DMA((n_sem,)),
                   *[pltpu.HBM(b.shape, b.dtype) for b in bufs], jax.ShapeDtypeStruct((8, LANES), F32)),
        in_specs=[HBM] * n + [ANY] * len(deps),
        out_specs=(SEM, SEM, *[HBM] * n, pl.BlockSpec(memory_space=pltpu.VMEM)),
        input_output_aliases={i: 2 + i for i in range(n)},
        compiler_params=pltpu.CompilerParams(has_side_effects=SPLIT_EFFECT),
    )(*[pltpu.with_memory_space_constraint(b, pltpu.HBM) for b in bufs], *deps)
    return res[0], res[1], list(res[2:2 + n]), res[-1]


def _split_wait(name, handle, copies, after):
    ssem, rsem, bufs, _ = handle
    n = len(bufs)
    deps = list(after) if isinstance(after, (list, tuple)) else [after]

    def body(*refs):
        for cp in copies(refs[:n], refs[n], refs[n + 1], True):
            cp.wait_send()
            cp.wait_recv()

    res = pl.pallas_call(
        body, name=name,
        out_shape=[pltpu.HBM(b.shape, b.dtype) for b in bufs],
        in_specs=[HBM] * n + [SEM, SEM] + [ANY] * len(deps), out_specs=[HBM] * n,
        input_output_aliases={i: i for i in range(n)},
        compiler_params=pltpu.CompilerParams(has_side_effects=SPLIT_EFFECT),
    )(*bufs, ssem, rsem, *deps)
    return list(res)


def _remote(src, dst, ssem, rsem, k, to):
    return pltpu.make_async_remote_copy(src_ref=src, dst_ref=dst, send_sem=ssem.at[k], recv_sem=rsem.at[k],
                                        device_id=to, device_id_type=MESH)


def _gather_chips(x, y, c):
    nx, ny = x + (1 - c) - 2 * x * (1 - c), y + c - 2 * y * c
    fx, fy = x + c - 2 * x * c, y + (1 - c) - 2 * y * (1 - c)
    return (nx, ny), (fx, fy), 2 * nx + ny, 2 * fx + fy, 2 * (1 - x) + (1 - y)


def _direct_copies(refs, ssem, rsem, landing, n_whole=0):
    x, y, c = _place()
    me = 2 * x + y
    (nx, ny), _, near, _, _ = _gather_chips(x, y, c)
    n = len(refs) - n_whole
    cps = []
    for a, ref in enumerate(refs[:n]):
        cps.append(_remote(ref.at[me], ref.at[near if landing else me], ssem, rsem, a, (nx, ny, c)))
    for b, ref in enumerate(refs[n:]):
        for j, (px, py) in enumerate(_other_chips(x, y)):
            cps.append(_remote(ref.at[me], ref.at[2 * px + py if landing else me], ssem, rsem, n + 3 * b + j,
                               (px, py, c)))
    return cps


def _relay_copies(refs, ssem, rsem, landing):
    x, y, c = _place()
    _, (fx, fy), near, far, diag = _gather_chips(x, y, c)
    n = len(refs)
    cps = []
    for a, ref in enumerate(refs):
        rh = ref.shape[1] // 2
        rows = pl.ds(c * rh, rh)
        cps.append(_remote(ref.at[near, rows], ref.at[diag if landing else near, rows], ssem, rsem, a, (fx, fy, c)))
        cps.append(_remote(ref.at[near], ref.at[far if landing else near], ssem, rsem, n + a, (x, y, 1 - c)))
    return cps


def _diagonal_copies(refs, ssem, rsem, landing):
    x, y, c = _place()
    diag = 2 * (1 - x) + (1 - y)
    who = 1 - c if landing else c
    cps = []
    for a, ref in enumerate(refs):
        rh = ref.shape[1] // 2
        piece = ref.at[diag, pl.ds(who * rh, rh)]
        cps.append(_remote(piece, piece, ssem, rsem, a, (x, y, 1 - c)))
    return cps


def _sibling_copies(refs, ssem, rsem, landing):
    x, y, c = _place()
    n = len(refs) // 2
    cps = []
    for a in range(n):
        rh = refs[a].shape[1] // 2
        cps.append(_remote(refs[a].at[:, pl.ds((1 - c) * rh, rh), :], refs[n + a], ssem, rsem, a, (x, y, 1 - c)))
    return cps


def _owner_copies(refs, ssem, rsem, landing):
    x, y, c = _place()
    n = len(refs) // 2
    cps = []
    for a in range(n):
        for j, (px, py) in enumerate(_other_chips(x, y)):
            cps.append(_remote(refs[a].at[2 * px + py], refs[n + a].at[j], ssem, rsem, 3 * a + j, (px, py, c)))
    return cps


def _swap_copies(refs, ssem, rsem, landing):
    x, y, c = _place()
    who = 1 - c if landing else c
    cps = []
    for a, ref in enumerate(refs):
        rh = ref.shape[0] // 2
        rows = ref.at[pl.ds(who * rh, rh)]
        cps.append(_remote(rows, rows, ssem, rsem, a, (x, y, 1 - c)))
    return cps


def _small_copies(refs, ssem, rsem, landing):
    pack, slots = refs
    x, y, c = _place()
    cps = []
    for rel in range(1, N_DEV):
        px = 1 - x if (rel >> 2) & 1 else x
        py = 1 - y if (rel >> 1) & 1 else y
        pc = 1 - c if rel & 1 else c
        slot = 4 * px + 2 * py + pc if landing else 4 * x + 2 * y + c
        cps.append(_remote(pack, slots.at[slot], ssem, rsem, rel - 1, (px, py, pc)))
    return cps


def _small_pack(rows, w_dw_grad, d):
    n = len(rows)

    def body(*refs):
        pack = refs[-1]
        pack[...] = jnp.zeros_like(pack)
        for (r, _), ref in zip(rows, refs[:n]):
            pack[r:r + 1, :] = ref[...]
        pack[16:16 + CONV_PAD, :] = refs[n][...]

    return pl.pallas_call(body, name="small_pack", out_shape=jax.ShapeDtypeStruct((SMALL_ROWS, d), F32))(
        *[v for _, v in rows], w_dw_grad)


def _small_sum(pack, slots, place):
    rows, d = pack.shape
    loss_row = 12

    def body(pl_ref, pack_ref, slots_ref, out_ref):
        me = pl_ref[2]
        tot = jnp.where(me == 0, pack_ref[...], slots_ref[0])
        for i in range(1, N_DEV):
            tot = tot + jnp.where(me == i, pack_ref[...], slots_ref[i])
        out_ref[...] = tot
        out_ref[loss_row:loss_row + 1, :] = jnp.zeros((1, d), F32) + jnp.sum(tot[loss_row:loss_row + 1, :])

    return pl.pallas_call(
        body, name="small_sum",
        grid_spec=pltpu.PrefetchScalarGridSpec(
            num_scalar_prefetch=1, grid=(1,),
            in_specs=[pl.BlockSpec((rows, d), lambda i, p: (0, 0)), pl.BlockSpec((N_DEV, rows, d), lambda i, p: (0, 0, 0))],
            out_specs=pl.BlockSpec((rows, d), lambda i, p: (0, 0))),
        out_shape=jax.ShapeDtypeStruct((rows, d), F32),
        compiler_params=_params("arbitrary"),
    )(place, pack, slots)


def kernel(x, norm_mix, norm_mlp, conv_w_in, conv_b_in, conv_w_dw, conv_b_dw, conv_ln_g, conv_ln_b, conv_w_out, conv_b_out, kv_norm, w_kv, attn_w_q, attn_w_o, mlp_w_in, mlp_w_out, final_norm, loss_target, m_norm_mix, m_norm_mlp, m_conv_w_in, m_conv_b_in, m_conv_w_dw, m_conv_b_dw, m_conv_ln_g, m_conv_ln_b, m_conv_w_out, m_conv_b_out, m_kv_norm, m_w_kv, m_attn_w_q, m_attn_w_o, m_mlp_w_in, m_mlp_w_out, m_final_norm, v_norm_mix, v_norm_mlp, v_conv_w_in, v_conv_b_in, v_conv_w_dw, v_conv_b_dw, v_conv_ln_g, v_conv_ln_b, v_conv_w_out, v_conv_b_out, v_kv_norm, v_w_kv, v_attn_w_q, v_attn_w_o, v_mlp_w_in, v_mlp_w_out, v_final_norm):
    _, s, d = x.shape
    dff = mlp_w_in.shape[2] * N_SHARD
    kvw = w_kv.shape[1]
    nh = d // HEAD_DIM
    group = nh // N_KV_HEADS
    ds4 = d // N_SHARD
    xi, yi, ci = _place()
    me = 2 * xi + yi
    place = jnp.stack([ci, me, 2 * me + ci]).astype(I32)

    h0 = x.reshape(s, d)
    target = loss_target.reshape(s, d)
    tabs = _rope_tables(s)

    def gather_begin(tag, bufs, n_whole=0):
        plan = functools.partial(_direct_copies, n_whole=n_whole)
        return _split_start(f"gather_start_{tag}", bufs, len(bufs) + 2 * n_whole, plan), plan, n_whole

    def gather_land(tag, begun, later):
        handle, plan, n_whole = begun
        bufs = _split_wait(f"gather_wait_{tag}", handle, plan, later)
        n = len(bufs) - n_whole
        return _split_start(f"relay_start_{tag}", bufs[:n], 2 * n, _relay_copies), bufs[n:]

    def gather_swap(tag, landed, later):
        relayed, whole = landed
        bufs = _split_wait(f"relay_wait_{tag}", relayed, _relay_copies, later)
        return _split_start(f"diagonal_start_{tag}", bufs, len(bufs), _diagonal_copies), whole

    def gather_end(tag, swapped, later):
        handle, whole = swapped
        return _split_wait(f"diagonal_wait_{tag}", handle, _diagonal_copies, later) + whole

    def tied(vec, begun):
        return vec + begun[0][3][0:1, 0:1]

    ag_cin = gather_begin("conv_in", [
        _cast_bf16("cast_w_in", conv_w_in, 0, place),
        _pack_small(conv_b_in, conv_w_dw.reshape(CONV_WIDTH, ds4), conv_b_dw, conv_ln_g, conv_ln_b, conv_b_out, place),
    ], n_whole=1)
    ag_cout = gather_begin("conv_out", [_cast_bf16("cast_w_out", conv_w_out, 0, place, ag_cin[0][3])])
    ag_mi0 = gather_begin("mlp_in0", [_cast_bf16("cast_mlp_in0", mlp_w_in, 0, place, ag_cout[0][3])])
    ag_mo0 = gather_begin("mlp_out0", [_cast_bf16("cast_mlp_out0", mlp_w_out, 0, place, ag_mi0[0][3])])
    land_cin = gather_land("conv_in", ag_cin, ag_mo0[0][3])
    land_cout = gather_land("conv_out", ag_cout, land_cin[0][3])
    ag_attn = gather_begin("attn", [
        _cast_bf16("cast_w_kv", w_kv.reshape(1, ds4, kvw), 0, place, land_cout[0][3]),
        _cast_bf16("cast_w_q", attn_w_q, 0, place), _cast_bf16("cast_w_o", attn_w_o, 0, place)])
    ag_mi1 = gather_begin("mlp_in1", [_cast_bf16("cast_mlp_in1", mlp_w_in, 1, place, ag_attn[0][3])])
    ag_mo1 = gather_begin("mlp_out1", [_cast_bf16("cast_mlp_out1", mlp_w_out, 1, place, ag_mi1[0][3])])

    wmi_g = [None, None]
    wmo_f = [None, None]

    nm = [norm_mix[0:1], norm_mix[1:2]]
    nmlp = [norm_mlp[0:1], norm_mlp[1:2]]
    kvn = kv_norm.reshape(1, d)
    fin = final_norm.reshape(1, d)
    (y0,) = _rms_fwd("rms_mix0", h0, [tied(nm[0], ag_mo1)])

    swap_cin = gather_swap("conv_in", land_cin, y0)
    w_in_g, small_g = gather_end("conv_in", swap_cin, swap_cin[0][3])
    b_in_f = small_g[:, 0, :].reshape(1, 2 * d)
    b_dw_f = small_g[:, 1, 0:ds4].reshape(1, d)
    ln_g_f = small_g[:, 1, ds4:2 * ds4].reshape(1, d)
    ln_b_f = small_g[:, 2, 0:ds4].reshape(1, d)
    b_out_f = small_g[:, 2, ds4:2 * ds4].reshape(1, d)
    w_dw_f = jnp.transpose(small_g[:, 8:8 + CONV_PAD, 0:ds4], (1, 0, 2)).reshape(CONV_PAD, d)

    def ep_bias(acc, ex, outs, j):
        outs[0][...] = (acc + ex[0][...]).astype(outs[0].dtype)

    def ep_residual(acc, ex, outs, j):
        outs[0][...] = ex[0][...] + acc

    def ep_residual_bias(acc, ex, outs, j):
        outs[0][...] = ex[0][...] + (acc + ex[1][...])

    def ep_relu2(acc, ex, outs, j):
        r = jnp.maximum(acc, 0.0)
        outs[0][...] = r.astype(BF16)
        outs[1][...] = (r * r).astype(BF16)

    by_residue = [(BF16, ("residues", dil)) for dil in DILATIONS]

    def put_by_residue(val, outs, stage):
        _to_residues(val, stage, outs, DILATIONS)

    def ep_rope(acc, ex, outs, j, stage):
        put_by_residue(_rope_apply(acc, ex[0][...], ex[1][...], ex[2][...], 1.0), outs, stage)

    def ep_rope_k(acc, ex, outs, j, stage):
        roped = _rope_apply(acc, ex[0][...], ex[1][...], ex[2][...], 1.0)
        put_by_residue(jnp.where(j == 0, roped, acc), outs, stage)

    def ep_by_residue(acc, ex, outs, j, stage):
        put_by_residue(acc, outs, stage)

    tab_extras = [(t, "rows") for t in tabs]

    def mlp_fwd(idx, h, y, out_weight):
        r, r2 = _matmul(f"mlp_in{idx}", "nn", y, wmi_g[idx], b_kind="col", m=s, n=dff, k=d,
                        outs=[(BF16, "plain"), (BF16, "plain")], epilogue=ep_relu2)
        wmo_f[idx] = out_weight(r2).reshape(dff, d)
        (h_new,) = _matmul(f"mlp_out{idx}", "nn", r2, wmo_f[idx], m=s, n=d, k=dff,
                           outs=[(F32, "plain")], extras=[(h, "ij")], epilogue=ep_residual)
        return h_new, r, r2

    (u,) = _matmul("conv_in", "nn", y0, w_in_g, b_kind="col", m=s, n=2 * d, k=d,
                   outs=[(BF16, "plain")], extras=[(b_in_f, "vec")], epilogue=ep_bias)
    land_mi0 = gather_land("mlp_in0", ag_mi0, u)
    swap_cout = gather_swap("conv_out", land_cout, land_mi0[0][3])
    cpre = _dwconv_fwd(u, w_dw_f, tied(b_dw_f, swap_cout))
    sact = _ln_silu_fwd(cpre, ln_g_f, ln_b_f)
    (w_out_g,) = gather_end("conv_out", swap_cout, sact)
    w_out_f = w_out_g.reshape(d, d)
    (h1,) = _matmul("conv_out", "nn", sact, w_out_f, m=s, n=d, k=d,
                    outs=[(F32, "plain")], extras=[(h0, "ij"), (b_out_f, "vec")], epilogue=ep_residual_bias)
    swap_mi0 = gather_swap("mlp_in0", land_mi0, h1)
    land_mo0 = gather_land("mlp_out0", ag_mo0, swap_mi0[0][3])
    (y1,) = _rms_fwd("rms_mlp0", h1, [tied(nmlp[0], land_mo0)])
    (wmi_g[0],) = gather_end("mlp_in0", swap_mi0, y1)
    land_attn = None

    def out_weight0(r2):
        nonlocal land_attn
        land_attn = gather_land("attn", ag_attn, r2)
        swap_mo0 = gather_swap("mlp_out0", land_mo0, land_attn[0][3])
        return gather_end("mlp_out0", swap_mo0, swap_mo0[0][3])[0]

    h2, r0, r0sq = mlp_fwd(0, h1, y1, out_weight0)
    swap_attn = gather_swap("attn", land_attn, h2)
    land_mi1 = gather_land("mlp_in1", ag_mi1, swap_attn[0][3])
    ykv, y2 = _rms_fwd("rms_kv_mix1", h2, [tied(kvn, land_mi1), nm[1]])
    wkv_g, wq_g, wo_g = gather_end("attn", swap_attn, y2)
    wkv_f, wq_f, wo_f = wkv_g.reshape(d, kvw), wq_g.reshape(d, d), wo_g.reshape(d, d)
    kv_parts = _matmul("kv_proj", "nn", ykv, wkv_f, m=s, n=kvw, k=d, tn=kvw // 2,
                       outs=by_residue, extras=tab_extras, epilogue=ep_rope_k, stage=True)
    q_parts = _matmul("q_proj", "nn", y2, wq_f, m=s, n=d, k=d,
                      outs=by_residue, extras=tab_extras, epilogue=ep_rope, stage=True)
    swap_mi1 = gather_swap("mlp_in1", land_mi1, q_parts[0])
    o_parts, lse_parts = [], []
    for dil, q_b, kv_b in zip(DILATIONS, q_parts, kv_parts):
        o_b, lse_b = _attn_fwd(f"attn_fwd_d{dil}", q_b, kv_b)
        o_parts.append(o_b)
        lse_parts.append(lse_b)
    o, lse = _attn_combine(o_parts, lse_parts)
    land_mo1 = gather_land("mlp_out1", ag_mo1, o)
    (h3,) = _matmul("attn_out", "nn", o, wo_f, m=s, n=d, k=d,
                    outs=[(F32, "plain")], extras=[(h2, "ij")], epilogue=ep_residual)
    (y3,) = _rms_fwd("rms_mlp1", h3, [tied(nmlp[1], land_mo1)])
    (wmi_g[1],) = gather_end("mlp_in1", swap_mi1, y3)

    def out_weight1(r2):
        swap_mo1 = gather_swap("mlp_out1", land_mo1, r2)
        return gather_end("mlp_out1", swap_mo1, swap_mo1[0][3])[0]

    h4, r1, r1sq = mlp_fwd(1, h3, y3, out_weight1)
    dh4, dh4b, d_fin, loss_cols = _final_loss(h4, fin, target)

    def ep_relu2_bwd(acc, ex, outs, j):
        outs[0][...] = (acc * (2.0 * ex[0][...].astype(F32))).astype(BF16)

    def mlp_bwd(idx, dhb, y, r, r2):
        (dz,) = _matmul(f"mlp_out{idx}_dx", "nt", dhb, wmo_f[idx], m=s, n=dff, k=d,
                        outs=[(BF16, "plain")], extras=[(r, "ij")], epilogue=ep_relu2_bwd)
        (dwo,) = _matmul(f"mlp_out{idx}_dw", "tn", r2, dhb, m=dff, n=d, k=s,
                         outs=[(BF16, "plain")])
        (dy,) = _matmul(f"mlp_in{idx}_dx", "nt", dz, wmi_g[idx], b_kind="col", m=s, n=d, k=dff,
                        outs=[(BF16, "plain")])
        (dwi,) = _matmul(f"mlp_in{idx}_dw", "tn", y, dz, m=d, n=dff, k=s,
                         outs=[(BF16, "col")])
        return dy, dwi, dwo.reshape(N_SHARD, dff // N_SHARD, d)

    def token(handle):
        return handle[3][0:1, 0:1]

    def rs_exchange(tag, grads):
        lands = [lax.empty((N_SHARD, g.shape[1] // 2, g.shape[2]), g.dtype) for g in grads]
        return _split_start(f"sibling_start_{tag}", list(grads) + lands, len(grads), _sibling_copies)

    def rs_send(tag, names, exchanged, later):
        bufs = _split_wait(f"sibling_wait_{tag}", exchanged, _sibling_copies, later)
        n = len(names)
        sums = [_chip_sum(f"chip_sum_{nme}", g, rh, place) for nme, g, rh in zip(names, bufs[:n], bufs[n:])]
        lands = [lax.empty((N_SHARD - 1,) + cs.shape[1:], cs.dtype) for cs in sums]
        return _split_start(f"owners_start_{tag}", sums + lands, 3 * n, _owner_copies)

    def rs_sum(tag, names, sent, later):
        bufs = _split_wait(f"owners_wait_{tag}", sent, _owner_copies, later)
        n = len(names)
        own = [_owner_sum(f"owner_sum_{nme}", cs, rp, place) for nme, cs, rp in zip(names, bufs[:n], bufs[n:])]
        return _split_start(f"swap_start_{tag}", own, n, _swap_copies)

    def rs_end(tag, swapped, later):
        return _split_wait(f"swap_wait_{tag}", swapped, _swap_copies, later)

    dy3, g_wmi1, g_wmo1 = mlp_bwd(1, dh4b, y3, r1, r1sq)
    x_mlp1 = rs_exchange("mlp1", [g_wmi1, g_wmo1])
    dh3, dh3b, d_nmlp1 = _rms_bwd("rms_mlp1_bwd", h3, [(nmlp[1] + token(x_mlp1), dy3)], dh4)

    do_parts = _matmul("attn_out_dx", "nt", dh3b, wo_f, m=s, n=d, k=d, outs=by_residue, epilogue=ep_by_residue,
                       stage=True)
    (g_wo,) = _matmul("attn_out_dw", "tn", o, dh3b, m=d, n=d, k=s, outs=[(BF16, "plain")])
    rs_mlp1 = rs_send("mlp1", ["mlp_in1", "mlp_out1"], x_mlp1, g_wo)
    lse_res, delta_res = _attn_delta(do_parts[0].reshape(s, d), o, lse, DILATIONS)
    dq_parts, dk_parts, dv_parts = [], [], []
    for dil, q_b, kv_b, do_b, lse_b, dl_b in zip(DILATIONS, q_parts, kv_parts, do_parts, lse_res, delta_res):
        dq_b, dk_b, dv_b = _attn_bwd(f"attn_bwd_d{dil}", q_b, kv_b, do_b, lse_b, dl_b)
        dq_parts.append(dq_b)
        dk_parts.append(dk_b)
        dv_parts.append(dv_b)
    dq = _residue_sum("rope_bwd_q", [(dq_parts, True)], tabs)
    dkv = _residue_sum("rope_bwd_kv", [(dk_parts, True), (dv_parts, False)], tabs)
    (g_wq,) = _matmul("q_proj_dw", "tn", y2, dq, m=d, n=d, k=s, outs=[(BF16, "plain")])
    (dy2,) = _matmul("q_proj_dx", "nt", dq, wq_f, m=s, n=d, k=d, outs=[(BF16, "plain")])
    (g_wkv,) = _matmul("kv_proj_dw", "tn", ykv, dkv, m=d, n=kvw, k=s, outs=[(BF16, "plain")])
    (dykv,) = _matmul("kv_proj_dx", "nt", dkv, wkv_f, m=s, n=d, k=kvw, outs=[(BF16, "plain")])
    x_attn = rs_exchange("attn", [g_wkv.reshape(N_SHARD, ds4, kvw), g_wq.reshape(N_SHARD, ds4, d),
                                  g_wo.reshape(N_SHARD, ds4, d)])
    dh2, dh2b, d_nm1, d_kvn = _rms_bwd("rms_kv_mix1_bwd", h2, [(nm[1] + token(x_attn), dy2), (kvn, dykv)], dh3)
    rs_attn = rs_send("attn", ["w_kv", "w_q", "w_o"], x_attn, dh2b)

    dy1, g_wmi0, g_wmo0 = mlp_bwd(0, dh2b, y1, r0, r0sq)
    x_mlp0 = rs_exchange("mlp0", [g_wmi0, g_wmo0])
    dh1, dh1b, d_nmlp0, d_b_out = _rms_bwd("rms_mlp0_bwd", h1, [(nmlp[0] + token(x_mlp0) + token(rs_attn), dy1)],
                                           dh2, want_colsum=True)

    (dsact,) = _matmul("conv_out_dx", "nt", dh1b, w_out_f, m=s, n=d, k=d, outs=[(BF16, "plain")])
    (g_wout,) = _matmul("conv_out_dw", "tn", sact, dh1b, m=d, n=d, k=s, outs=[(BF16, "plain")])
    rs_mlp0 = rs_send("mlp0", ["mlp_in0", "mlp_out0"], x_mlp0, g_wout)
    dc, d_ln_g, d_ln_b, d_b_dw = _ln_silu_bwd(cpre, ln_g_f + token(rs_mlp0), ln_b_f, dsact)
    du, d_w_dw, d_b_in_a, d_b_in_g = _dwconv_bwd(u, w_dw_f, dc)
    (g_win,) = _matmul("conv_in_dw", "tn", y0, du, b_kind="col", m=d, n=2 * d, k=s, outs=[(BF16, "col")])
    x_conv = rs_exchange("conv", [g_win, g_wout.reshape(N_SHARD, ds4, d)])
    (dy0,) = _matmul("conv_in_dx", "nt", du, w_in_g, a_kind="col", b_kind="col", m=s, n=d, k=2 * d,
                     outs=[(BF16, "plain")])
    dx, _, d_nm0 = _rms_bwd("rms_mix0_bwd", h0, [(nm[0] + token(x_conv), dy0)], dh1)

    small_rows = [(0, d_nm0), (1, d_nm1), (2, d_nmlp0), (3, d_nmlp1), (4, d_kvn), (5, d_fin), (6, d_b_dw),
                  (7, d_ln_g), (8, d_ln_b), (9, d_b_out), (10, d_b_in_a), (11, d_b_in_g), (12, loss_cols)]
    x_small = _split_start("small_start", [_small_pack(small_rows, d_w_dw, d),
                                           lax.empty((N_DEV, SMALL_ROWS, d), F32)], N_DEV - 1, _small_copies)
    rs_conv = rs_send("conv", ["w_in", "w_out"], x_conv, x_small[3])

    def big(name, w, m, v, g, layer=0, partial=None):
        shape = w.shape
        w3, m3, v3 = [t.reshape((-1,) + shape[-2:]) for t in (w, m, v)]
        if partial is not None:
            partial = [t.reshape(w3.shape) for t in partial]
        res = _adamw(name, w3, m3, v3, g, layer, partial)
        return [t.reshape(shape) for t in res]

    sw_mlp1 = rs_sum("mlp1", ["mlp_in1", "mlp_out1"], rs_mlp1, rs_conv[3])
    sw_attn = rs_sum("attn", ["w_kv", "w_q", "w_o"], rs_attn, sw_mlp1[3])
    f_wmi1, f_wmo1 = rs_end("mlp1", sw_mlp1, sw_attn[3])
    p_wmi = big("adam_mlp_in1", mlp_w_in, m_mlp_w_in, v_mlp_w_in, f_wmi1, 1)
    p_wmo = big("adam_mlp_out1", mlp_w_out, m_mlp_w_out, v_mlp_w_out, f_wmo1, 1)
    sw_mlp0 = rs_sum("mlp0", ["mlp_in0", "mlp_out0"], rs_mlp0, [p_wmi[0], p_wmo[0]])
    f_wkv, f_wq, f_wo = rs_end("attn", sw_attn, sw_mlp0[3])
    r_wkv = big("adam_w_kv", w_kv, m_w_kv, v_w_kv, f_wkv)
    r_wq = big("adam_w_q", attn_w_q, m_attn_w_q, v_attn_w_q, f_wq)
    r_wo = big("adam_w_o", attn_w_o, m_attn_w_o, v_attn_w_o, f_wo)
    sw_conv = rs_sum("conv", ["w_in", "w_out"], rs_conv, [r_wkv[0], r_wq[0], r_wo[0]])
    f_wmi0, f_wmo0 = rs_end("mlp0", sw_mlp0, sw_conv[3])
    r_wmi = big("adam_mlp_in0", mlp_w_in, m_mlp_w_in, v_mlp_w_in, f_wmi0, 0, p_wmi)
    r_wmo = big("adam_mlp_out0", mlp_w_out, m_mlp_w_out, v_mlp_w_out, f_wmo0, 0, p_wmo)
    f_win, f_wout = rs_end("conv", sw_conv, [r_wmi[0], r_wmo[0]])
    r_win = big("adam_w_in", conv_w_in, m_conv_w_in, v_conv_w_in, f_win)
    r_wout = big("adam_w_out", conv_w_out, m_conv_w_out, v_conv_w_out, f_wout)

    small_pack, small_slots = _split_wait("small_wait", x_small, _small_copies, r_wout[0])
    red = _small_sum(small_pack, small_slots, place)
    loss = red[12, 0]
    g_norm_mix = red[0:2]
    g_norm_mlp = red[2:4]
    g_kv_norm = red[4:5]
    g_final = red[5:6]

    def my_cols(row):
        return lax.dynamic_slice(red, (row, me * ds4), (1, ds4))

    g_b_dw, g_ln_g, g_ln_b, g_b_out = my_cols(6), my_cols(7), my_cols(8), my_cols(9)
    half_in = 2 * d // N_SHARD
    b_in_row = 10 + me // 2
    g_b_in = lax.dynamic_slice(red, (b_in_row, (me % 2) * half_in), (1, half_in))
    g_w_dw = lax.dynamic_slice(red, (16, me * ds4), (CONV_WIDTH, ds4))

    sm_w =[norm_mix, norm_mlp, conv_b_in, conv_w_dw.reshape(CONV_WIDTH, ds4), conv_b_dw, conv_ln_g, conv_ln_b,
            conv_b_out, kv_norm.reshape(1, d), final_norm.reshape(1, d)]
    sm_m = [m_norm_mix, m_norm_mlp, m_conv_b_in, m_conv_w_dw.reshape(CONV_WIDTH, ds4), m_conv_b_dw, m_conv_ln_g,
            m_conv_ln_b, m_conv_b_out, m_kv_norm.reshape(1, d), m_final_norm.reshape(1, d)]
    sm_v = [v_norm_mix, v_norm_mlp, v_conv_b_in, v_conv_w_dw.reshape(CONV_WIDTH, ds4), v_conv_b_dw, v_conv_ln_g,
            v_conv_ln_b, v_conv_b_out, v_kv_norm.reshape(1, d), v_final_norm.reshape(1, d)]
    sm_g = [g_norm_mix, g_norm_mlp, g_b_in, g_w_dw, g_b_dw, g_ln_g, g_ln_b, g_b_out, g_kv_norm, g_final]
    sm_d, sm_nm, sm_nv = _adam_small(sm_w, sm_m, sm_v, sm_g)
    shapes = [norm_mix.shape, norm_mlp.shape, conv_b_in.shape, conv_w_dw.shape, conv_b_dw.shape, conv_ln_g.shape,
              conv_ln_b.shape, conv_b_out.shape, kv_norm.shape, final_norm.shape]
    sm_g, sm_d, sm_nm, sm_nv = [[t.reshape(sh) for t, sh in zip(lst, shapes)] for lst in (sm_g, sm_d, sm_nm, sm_nv)]

    def order(sm, idx):
        return [sm[0], sm[1], r_win[idx], sm[2], sm[3], sm[4], sm[5], sm[6], r_wout[idx], sm[7], sm[8],
                r_wkv[idx], r_wq[idx], r_wo[idx], r_wmi[idx], r_wmo[idx], sm[9]]

    return (loss, dx.reshape(x.shape), *order(sm_g, 0), *order(sm_d, 1), *order(sm_nm, 2), *order(sm_nv, 3))
```

```python
import functools
import math

import jax
import jax.numpy as jnp
from jax import lax
from jax.experimental import pallas as pl
from jax.experimental.pallas import tpu as pltpu

F32 = jnp.float32
BF16 = jnp.bfloat16
I32 = jnp.int32

NORM_EPS = 1e-6
LN_EPS = 1e-5
HEAD_DIM = 128
N_KV_HEADS = 4
ROT_DIM = 32
ROPE_THETA = 500000.0
CONV_WIDTH = 31
CONV_PAD = 32
ATT_BLOCK = 128
ATT_STEP_BLOCKS = 16
DILATIONS = (1, 4, 16)
ADAM_LR = 0.001
ADAM_B1 = 0.9
ADAM_B2 = 0.999
ADAM_EPS = 1e-08
ADAM_WD = 0.01
ADAM_STEP = 10
N_SHARD = 4
N_DEV = 8
LANES = 128
VMEM_LIMIT = 48 * 1024 * 1024
MM_TM, MM_TN, MM_TK = 1024, 1024, 2048
ROW_TILE = 256
CONV_CB = 128
CONV_T = 128
SMALL_ROWS = 48
MESH = pl.DeviceIdType.MESH
ANY = pl.BlockSpec(memory_space=pl.ANY)
HBM = pl.BlockSpec(memory_space=pltpu.HBM)
SEM = pl.BlockSpec(memory_space=pltpu.SEMAPHORE)
SPLIT_EFFECT = pltpu.SideEffectType.DATAFLOW_SIDE_EFFECTING


def _params(*sem):
    return pltpu.CompilerParams(dimension_semantics=sem, vmem_limit_bytes=VMEM_LIMIT)


def _sigmoid(x):
    return 1.0 / (1.0 + jnp.exp(-x))


def _wspec(kind, arr_shape, br, bc, pick):
    if kind == "plain":
        return pl.BlockSpec((br, bc), pick)
    per = arr_shape[2] // bc

    def idx(*g):
        rb, cb = pick(*g)
        return (cb // per, rb, cb % per)

    return pl.BlockSpec((None, br, bc), idx)


def _stage_shape(rows, w):
    return (w // LANES, rows, LANES)


def _to_residues(val, stage_ref, out_refs, dils):
    planes, rows, _ = stage_ref.shape
    for c in range(planes):
        stage_ref[c] = val[:, c * LANES:(c + 1) * LANES]
    for out_ref, dil in zip(out_refs, dils):
        if dil == 1:
            out_ref[0] = val.astype(out_ref.dtype)
            continue
        for r in range(dil):
            for c in range(planes):
                out_ref[r, :, c * LANES:(c + 1) * LANES] = stage_ref.at[c][pl.ds(r, rows // dil, stride=dil), :].astype(
                    out_ref.dtype)


def _from_residues(src_ref, stage_ref, dil):
    planes, rows, _ = stage_ref.shape
    if dil == 1:
        return lambda c: src_ref[0, :, c * LANES:(c + 1) * LANES].astype(F32)
    for r in range(dil):
        for c in range(planes):
            stage_ref.at[c][pl.ds(r, rows // dil, stride=dil), :] = src_ref[r, :, c * LANES:(c + 1) * LANES].astype(F32)
    return lambda c: stage_ref[c]


def _matmul(name, mode, a, b, *, m, n, k, tn=MM_TN, a_kind="plain", b_kind="plain", outs, extras=(), epilogue=None,
            stage=False):
    tm, tn, tk = min(MM_TM, m), min(tn, n), min(MM_TK, k)
    if b_kind == "col" and mode in ("nn", "tn"):
        tn = min(tn, n // b.shape[0])
    if b_kind == "col" and mode == "nt":
        tk = min(tk, k // b.shape[0])
    if a_kind == "col":
        assert mode == "nt"
        tk = min(tk, k // a.shape[0])
    if any(kind == "col" for _, kind in outs):
        tn = min(tn, n // N_SHARD)
    assert m % tm == 0 and n % tn == 0 and k % tk == 0, (name, m, n, k, tm, tn, tk)
    nk = k // tk
    grid = (m // tm, n // tn, nk)
    if mode == "nn":
        a_spec = pl.BlockSpec((tm, tk), lambda i, j, kk: (i, kk))
        b_spec = _wspec(b_kind, b.shape, tk, tn, lambda i, j, kk: (kk, j))
        dims = (((1,), (0,)), ((), ()))
    elif mode == "nt":
        a_spec = _wspec(a_kind, a.shape, tm, tk, lambda i, j, kk: (i, kk))
        b_spec = _wspec(b_kind, b.shape, tn, tk, lambda i, j, kk: (j, kk))
        dims = (((1,), (1,)), ((), ()))
    else:
        a_spec = pl.BlockSpec((tk, tm), lambda i, j, kk: (kk, i))
        b_spec = _wspec(b_kind, b.shape, tk, tn, lambda i, j, kk: (kk, j))
        dims = (((0,), (0,)), ((), ()))
    out_shape, out_specs = [], []
    for dtype, kind in outs:
        if isinstance(kind, tuple):
            dil = kind[1]
            out_shape.append(jax.ShapeDtypeStruct((dil, m // dil, n), dtype))
            out_specs.append(pl.BlockSpec((dil, tm // dil, tn), lambda i, j, kk: (0, i, j)))
            continue
        shape = (m, n) if kind == "plain" else (N_SHARD, m, n // N_SHARD)
        out_shape.append(jax.ShapeDtypeStruct(shape, dtype))
        out_specs.append(_wspec(kind, shape, tm, tn, lambda i, j, kk: (i, j)))
    n_ex = len(extras)
    ex_specs = {"ij": pl.BlockSpec((tm, tn), lambda i, j, kk: (i, j)),
                "vec": pl.BlockSpec((1, tn), lambda i, j, kk: (0, j)),
                "rows": pl.BlockSpec((tm, LANES), lambda i, j, kk: (i, 0))}

    def body(*refs):
        a_ref, b_ref = refs[0], refs[1]
        ex_refs = refs[2:2 + n_ex]
        out_refs = refs[2 + n_ex:2 + n_ex + len(outs)]
        j = pl.program_id(1)

        def finish(res):
            if epilogue is None:
                out_refs[0][...] = res.astype(out_refs[0].dtype)
            elif stage:
                epilogue(res, ex_refs, out_refs, j, refs[-1])
            else:
                epilogue(res, ex_refs, out_refs, j)

        prod = lax.dot_general(a_ref[...], b_ref[...], dims, preferred_element_type=F32)
        if nk == 1:
            finish(prod)
            return
        acc_ref = refs[2 + n_ex + len(outs)]
        kk = pl.program_id(2)

        @pl.when(kk == 0)
        def _():
            acc_ref[...] = prod

        @pl.when(kk > 0)
        def _():
            acc_ref[...] += prod

        @pl.when(kk == nk - 1)
        def _():
            finish(acc_ref[...])

    res = pl.pallas_call(
        body, name=name, grid=grid,
        in_specs=[a_spec, b_spec] + [ex_specs[how] for _, how in extras],
        out_specs=out_specs, out_shape=out_shape,
        scratch_shapes=[pltpu.VMEM((tm, tn), F32)] * (nk > 1) + [pltpu.VMEM(_stage_shape(tm, tn), F32)] * bool(stage),
        compiler_params=_params("parallel", "parallel", "arbitrary"),
    )(a, b, *[e for e, _ in extras])
    return res


def _rope_tables(seq):
    half = ROT_DIM // 2
    pos = jnp.arange(seq, dtype=F32)
    inv = ROPE_THETA ** (-jnp.arange(0, ROT_DIM, 2, dtype=F32) / ROT_DIM)
    ang = pos[:, None] * inv[None, :]
    cos, sin = jnp.cos(ang), jnp.sin(ang)
    zeros = jnp.zeros((seq, HEAD_DIM - ROT_DIM), F32)
    ctab = jnp.concatenate([cos, cos, zeros + 1.0], axis=1)
    atab = jnp.concatenate([-sin, jnp.zeros((seq, half), F32), zeros], axis=1)
    btab = jnp.concatenate([jnp.zeros((seq, half), F32), sin, zeros], axis=1)
    return ctab, atab, btab


def _rope_apply(x, ctab, atab, btab, sign):
    w = x.shape[1]
    reps = w // HEAD_DIM
    half = ROT_DIM // 2
    c = jnp.tile(ctab, (1, reps))
    a = jnp.tile(atab, (1, reps))
    b = jnp.tile(btab, (1, reps))
    up = pltpu.roll(x, w - half, 1)
    down = pltpu.roll(x, half, 1)
    return x * c + sign * (up * a + down * b)


def _rows(t, w):
    return pl.BlockSpec((t, w), lambda i: (i, 0))


def _fixed(shape):
    nd = len(shape)
    return pl.BlockSpec(shape, lambda i: (0,) * nd)


def _rms_fwd(name, x, gains):
    s, d = x.shape
    t = min(ROW_TILE, s)
    ng = len(gains)

    def body(x_ref, *refs):
        xv = x_ref[...]
        r = lax.rsqrt(jnp.mean(xv * xv, axis=-1, keepdims=True) + NORM_EPS)
        xn = xv * r
        for g_ref, y_ref in zip(refs[:ng], refs[ng:]):
            y_ref[...] = (xn * g_ref[...]).astype(BF16)

    return pl.pallas_call(
        body, name=name, grid=(s // t,),
        in_specs=[_rows(t, d)] + [_fixed((1, d))] * ng,
        out_specs=[_rows(t, d)] * ng,
        out_shape=[jax.ShapeDtypeStruct((s, d), BF16)] * ng,
        compiler_params=_params("parallel"),
    )(x, *gains)


def _rms_bwd(name, x, pairs, dh_in, want_colsum=False):
    s, d = x.shape
    t = min(ROW_TILE, s)
    n_p = len(pairs)

    def body(x_ref, dh_ref, *refs):
        g_refs = refs[:n_p]
        dy_refs = refs[n_p:2 * n_p]
        dh_out, dhb_out = refs[2 * n_p], refs[2 * n_p + 1]
        dg_refs = refs[2 * n_p + 2:2 * n_p + 2 + n_p]
        cs_ref = refs[-1] if want_colsum else None
        i = pl.program_id(0)
        xv = x_ref[...]
        r = lax.rsqrt(jnp.mean(xv * xv, axis=-1, keepdims=True) + NORM_EPS)
        xn = xv * r
        dh = dh_ref[...]
        for g_ref, dy_ref, dg_ref in zip(g_refs, dy_refs, dg_refs):
            dy = dy_ref[...].astype(F32)
            u = dy * g_ref[...]
            dh = dh + r * (u - xn * jnp.mean(u * xn, axis=-1, keepdims=True))
            part = jnp.sum(dy * xn, axis=0, keepdims=True)

            @pl.when(i == 0)
            def _():
                dg_ref[...] = part

            @pl.when(i > 0)
            def _():
                dg_ref[...] += part

        dh_out[...] = dh
        dhb_out[...] = dh.astype(BF16)
        if want_colsum:
            col = jnp.sum(dh, axis=0, keepdims=True)

            @pl.when(i == 0)
            def _():
                cs_ref[...] = col

            @pl.when(i > 0)
            def _():
                cs_ref[...] += col

    n_vec = n_p + (1 if want_colsum else 0)
    return pl.pallas_call(
        body, name=name, grid=(s // t,),
        in_specs=[_rows(t, d), _rows(t, d)] + [_fixed((1, d))] * n_p + [_rows(t, d)] * n_p,
        out_specs=[_rows(t, d), _rows(t, d)] + [_fixed((1, d))] * n_vec,
        out_shape=[jax.ShapeDtypeStruct((s, d), F32), jax.ShapeDtypeStruct((s, d), BF16)]
        + [jax.ShapeDtypeStruct((1, d), F32)] * n_vec,
        compiler_params=_params("arbitrary"),
    )(x, dh_in, *[g for g, _ in pairs], *[dy for _, dy in pairs])


def _final_loss(x, g, target):
    s, d = x.shape
    t = min(ROW_TILE, s)

    def body(x_ref, g_ref, t_ref, dh_out, dhb_out, dg_ref, loss_ref):
        i = pl.program_id(0)
        xv = x_ref[...]
        gv = g_ref[...]
        r = lax.rsqrt(jnp.mean(xv * xv, axis=-1, keepdims=True) + NORM_EPS)
        xn = xv * r
        diff = xn * gv - t_ref[...]
        dy = diff / d
        u = dy * gv
        dh = r * (u - xn * jnp.mean(u * xn, axis=-1, keepdims=True))
        dh_out[...] = dh
        dhb_out[...] = dh.astype(BF16)
        dg = jnp.sum(dy * xn, axis=0, keepdims=True)
        lc = jnp.sum(0.5 * diff * dy, axis=0, keepdims=True)

        @pl.when(i == 0)
        def _():
            dg_ref[...] = dg
            loss_ref[...] = lc

        @pl.when(i > 0)
        def _():
            dg_ref[...] += dg
            loss_ref[...] += lc

    return pl.pallas_call(
        body, name="final_loss", grid=(s // t,),
        in_specs=[_rows(t, d), _fixed((1, d)), _rows(t, d)],
        out_specs=[_rows(t, d), _rows(t, d), _fixed((1, d)), _fixed((1, d))],
        out_shape=[jax.ShapeDtypeStruct((s, d), F32), jax.ShapeDtypeStruct((s, d), BF16),
                   jax.ShapeDtypeStruct((1, d), F32), jax.ShapeDtypeStruct((1, d), F32)],
        compiler_params=_params("arbitrary"),
    )(x, g, target)


def _ln_silu_fwd(c, g, b):
    s, d = c.shape
    t = min(ROW_TILE, s)

    def body(c_ref, g_ref, b_ref, s_ref):
        cv = c_ref[...]
        mu = jnp.mean(cv, axis=-1, keepdims=True)
        xc = cv - mu
        rs = lax.rsqrt(jnp.mean(xc * xc, axis=-1, keepdims=True) + LN_EPS)
        ln = xc * rs * g_ref[...] + b_ref[...]
        s_ref[...] = (ln * _sigmoid(ln)).astype(BF16)

    return pl.pallas_call(
        body, name="ln_silu_fwd", grid=(s // t,),
        in_specs=[_rows(t, d), _fixed((1, d)), _fixed((1, d))],
        out_specs=_rows(t, d), out_shape=jax.ShapeDtypeStruct((s, d), BF16),
        compiler_params=_params("parallel"),
    )(c, g, b)


def _ln_silu_bwd(c, g, b, ds):
    s, d = c.shape
    t = min(ROW_TILE, s)

    def body(c_ref, g_ref, b_ref, ds_ref, dc_ref, dg_ref, db_ref, dbdw_ref):
        i = pl.program_id(0)
        cv = c_ref[...]
        gv = g_ref[...]
        mu = jnp.mean(cv, axis=-1, keepdims=True)
        xc = cv - mu
        rs = lax.rsqrt(jnp.mean(xc * xc, axis=-1, keepdims=True) + LN_EPS)
        nrm = xc * rs
        ln = nrm * gv + b_ref[...]
        sig = _sigmoid(ln)
        dln = ds_ref[...].astype(F32) * sig * (1.0 + ln * (1.0 - sig))
        dn = dln * gv
        dc = rs * (dn - jnp.mean(dn, axis=-1, keepdims=True)
                   - nrm * jnp.mean(dn * nrm, axis=-1, keepdims=True))
        dc_ref[...] = dc
        pg = jnp.sum(dln * nrm, axis=0, keepdims=True)
        pb = jnp.sum(dln, axis=0, keepdims=True)
        pc = jnp.sum(dc, axis=0, keepdims=True)

        @pl.when(i == 0)
        def _():
            dg_ref[...] = pg
            db_ref[...] = pb
            dbdw_ref[...] = pc

        @pl.when(i > 0)
        def _():
            dg_ref[...] += pg
            db_ref[...] += pb
            dbdw_ref[...] += pc

    return pl.pallas_call(
        body, name="ln_silu_bwd", grid=(s // t,),
        in_specs=[_rows(t, d), _fixed((1, d)), _fixed((1, d)), _rows(t, d)],
        out_specs=[_rows(t, d)] + [_fixed((1, d))] * 3,
        out_shape=[jax.ShapeDtypeStruct((s, d), F32)] + [jax.ShapeDtypeStruct((1, d), F32)] * 3,
        compiler_params=_params("arbitrary"),
    )(c, g, b, ds)


def _residue_spec(dil, t, w):
    return pl.BlockSpec((dil, t // dil, w), lambda i: (0, i, 0))


def _attn_combine(o_list, lse_list):
    dil0, sd0, d = o_list[0].shape
    s = dil0 * sd0
    lw = lse_list[0].shape[2]
    group = d // HEAD_DIM // N_KV_HEADS
    t = min(ROW_TILE, s)
    nb = len(o_list)
    dils = [o.shape[0] for o in o_list]

    def body(*refs):
        o_out, l_out = refs[2 * nb], refs[2 * nb + 1]
        o_stage, l_stage = refs[2 * nb + 2:3 * nb + 2], refs[3 * nb + 2:]
        o_planes = [_from_residues(src, stage, dil) for src, stage, dil in zip(refs[:nb], o_stage, dils)]
        l_planes = [_from_residues(src, stage, dil) for src, stage, dil in zip(refs[nb:2 * nb], l_stage, dils)]
        for kh in range(N_KV_HEADS):
            ls = [plane(kh) for plane in l_planes]
            mx = ls[0]
            for l in ls[1:]:
                mx = jnp.maximum(mx, l)
            es = [jnp.exp(l - mx) for l in ls]
            den = es[0]
            for e in es[1:]:
                den = den + e
            l_out[:, kh * LANES:(kh + 1) * LANES] = mx + jnp.log(den)
            ws = [e / den for e in es]
            for g in range(group):
                h = kh * group + g
                acc = jnp.zeros((t, HEAD_DIM), F32)
                for plane, w in zip(o_planes, ws):
                    acc = acc + w[:, g:g + 1] * plane(h)
                o_out[:, h * HEAD_DIM:(h + 1) * HEAD_DIM] = acc.astype(BF16)

    return pl.pallas_call(
        body, name="attn_combine", grid=(s // t,),
        in_specs=[_residue_spec(dil, t, d) for dil in dils] + [_residue_spec(dil, t, lw) for dil in dils],
        out_specs=[_rows(t, d), _rows(t, lw)],
        out_shape=[jax.ShapeDtypeStruct((s, d), BF16), jax.ShapeDtypeStruct((s, lw), F32)],
        scratch_shapes=[pltpu.VMEM(_stage_shape(t, d), F32)] * nb + [pltpu.VMEM(_stage_shape(t, lw), F32)] * nb,
        compiler_params=_params("parallel"),
    )(*o_list, *lse_list)


def _attn_delta(do, o, lse, dils):
    s, d = o.shape
    lw = lse.shape[1]
    group = d // HEAD_DIM // N_KV_HEADS
    t = min(ROW_TILE, s)
    nd = len(dils)

    def body(do_ref, o_ref, lse_ref, *refs):
        stage = refs[-1]
        lane = lax.broadcasted_iota(I32, (t, LANES), 1)
        planes = []
        for kh in range(N_KV_HEADS):
            out = jnp.zeros((t, LANES), F32)
            for g in range(group):
                cols = slice((kh * group + g) * HEAD_DIM, (kh * group + g + 1) * HEAD_DIM)
                v = jnp.sum(do_ref[:, cols].astype(F32) * o_ref[:, cols].astype(F32), axis=-1, keepdims=True)
                out = jnp.where(lane == g, v, out)
            planes.append(out)
        _to_residues(lse_ref[...], stage, refs[:nd], dils)
        _to_residues(jnp.concatenate(planes, axis=1), stage, refs[nd:2 * nd], dils)

    res = pl.pallas_call(
        body, name="attn_delta", grid=(s // t,),
        in_specs=[_rows(t, d), _rows(t, d), _rows(t, lw)],
        out_specs=[_residue_spec(dil, t, lw) for dil in dils] * 2,
        out_shape=[jax.ShapeDtypeStruct((dil, s // dil, lw), F32) for dil in dils] * 2,
        scratch_shapes=[pltpu.VMEM(_stage_shape(t, lw), F32)],
        compiler_params=_params("parallel"),
    )(do, o, lse)
    return res[:nd], res[nd:]


def _residue_sum(name, groups, tabs):
    first = groups[0][0][0]
    s, w = first.shape[0] * first.shape[1], first.shape[2]
    t = min(ROW_TILE, s)
    flat = [p for parts, _ in groups for p in parts]

    def body(*refs):
        c_ref, a_ref, b_ref = refs[len(flat):len(flat) + 3]
        out = refs[len(flat) + 3]
        stages = refs[len(flat) + 4:]
        k = 0
        for gi, (parts, rotate) in enumerate(groups):
            planes = [_from_residues(refs[k + i], stages[k + i], p.shape[0]) for i, p in enumerate(parts)]
            k += len(parts)
            for c in range(w // LANES):
                tot = planes[0](c)
                for plane in planes[1:]:
                    tot = tot + plane(c)
                if rotate:
                    tot = _rope_apply(tot, c_ref[...], a_ref[...], b_ref[...], -1.0)
                out[:, gi * w + c * LANES:gi * w + (c + 1) * LANES] = tot.astype(BF16)

    return pl.pallas_call(
        body, name=name, grid=(s // t,),
        in_specs=[_residue_spec(p.shape[0], t, w) for p in flat] + [_rows(t, HEAD_DIM)] * 3,
        out_specs=_rows(t, len(groups) * w), out_shape=jax.ShapeDtypeStruct((s, len(groups) * w), BF16),
        scratch_shapes=[pltpu.VMEM(_stage_shape(t, w), F32) for _ in flat],
        compiler_params=_params("parallel"),
    )(*flat, *tabs)


def _dwconv_fwd(u, w_dw, b_dw):
    s, d2 = u.shape
    d = d2 // 2
    cb = min(CONV_CB, d)
    nblk = d // cb
    tt = min(CONV_T, s)

    def body(ua_ref, ug_ref, w_ref, b_ref, c_ref, xp_ref):
        gl = ua_ref[...].astype(F32) * _sigmoid(ug_ref[...].astype(F32))
        xp_ref[0:CONV_PAD, :] = jnp.zeros((CONV_PAD, cb), F32)
        xp_ref[CONV_PAD:, :] = gl
        wv = w_ref[...]
        bv = b_ref[...]
        for t0 in range(0, s, tt):
            acc = jnp.zeros((tt, cb), F32) + bv
            for kk in range(CONV_WIDTH):
                off = t0 + CONV_PAD - (CONV_WIDTH - 1) + kk
                acc = acc + wv[kk:kk + 1, :] * xp_ref[off:off + tt, :]
            c_ref[t0:t0 + tt, :] = acc

    return pl.pallas_call(
        body, name="dwconv_fwd", grid=(nblk,),
        in_specs=[pl.BlockSpec((s, cb), lambda j: (0, j)), pl.BlockSpec((s, cb), lambda j: (0, j + nblk)),
                  pl.BlockSpec((CONV_PAD, cb), lambda j: (0, j)), pl.BlockSpec((1, cb), lambda j: (0, j))],
        out_specs=pl.BlockSpec((s, cb), lambda j: (0, j)),
        out_shape=jax.ShapeDtypeStruct((s, d), F32),
        scratch_shapes=[pltpu.VMEM((s + CONV_PAD, cb), F32)],
        compiler_params=_params("parallel"),
    )(u, u, w_dw, b_dw)


def _dwconv_bwd(u, w_dw, dc):
    s, d2 = u.shape
    d = d2 // 2
    cb = min(CONV_CB, d)
    nblk = d // cb
    tt = min(CONV_T, s)

    def body(ua_ref, ug_ref, w_ref, dc_ref, du_ref, dw_ref, dba_ref, dbg_ref, glp_ref, dcp_ref, acc_ref):
        a = ua_ref[...].astype(F32)
        sig = _sigmoid(ug_ref[...].astype(F32))
        glp_ref[0:CONV_PAD, :] = jnp.zeros((CONV_PAD, cb), F32)
        glp_ref[CONV_PAD:, :] = a * sig
        dcp_ref[0:s, :] = dc_ref[...]
        dcp_ref[s:, :] = jnp.zeros((CONV_PAD, cb), F32)
        acc_ref[...] = jnp.zeros_like(acc_ref)
        wv = w_ref[...]
        dba = jnp.zeros((1, cb), F32)
        dbg = jnp.zeros((1, cb), F32)
        for t0 in range(0, s, tt):
            dgl = jnp.zeros((tt, cb), F32)
            dct = dc_ref[t0:t0 + tt, :]
            for kk in range(CONV_WIDTH):
                off = t0 + (CONV_WIDTH - 1) - kk
                dgl = dgl + wv[kk:kk + 1, :] * dcp_ref[off:off + tt, :]
                goff = t0 + CONV_PAD - (CONV_WIDTH - 1) + kk
                prod = dct * glp_ref[goff:goff + tt, :]
                acc_ref[8 * kk:8 * kk + 8, :] += jnp.sum(prod.reshape(tt // 8, 8, cb), axis=0)
            at = ua_ref[t0:t0 + tt, :].astype(F32)
            st = _sigmoid(ug_ref[t0:t0 + tt, :].astype(F32))
            da = dgl * st
            dg = dgl * at * st * (1.0 - st)
            du_ref[0, t0:t0 + tt, :] = da.astype(BF16)
            du_ref[1, t0:t0 + tt, :] = dg.astype(BF16)
            dba = dba + jnp.sum(da, axis=0, keepdims=True)
            dbg = dbg + jnp.sum(dg, axis=0, keepdims=True)
        dba_ref[...] = dba
        dbg_ref[...] = dbg
        for kk in range(CONV_WIDTH):
            dw_ref[kk:kk + 1, :] = jnp.sum(acc_ref[8 * kk:8 * kk + 8, :], axis=0, keepdims=True)
        dw_ref[CONV_WIDTH:, :] = jnp.zeros((CONV_PAD - CONV_WIDTH, cb), F32)

    blk = pl.BlockSpec((s, cb), lambda j: (0, j))
    vec = pl.BlockSpec((1, cb), lambda j: (0, j))
    return pl.pallas_call(
        body, name="dwconv_bwd", grid=(nblk,),
        in_specs=[blk, pl.BlockSpec((s, cb), lambda j: (0, j + nblk)),
                  pl.BlockSpec((CONV_PAD, cb), lambda j: (0, j)), blk],
        out_specs=[pl.BlockSpec((2, s, cb), lambda j: (0, 0, j)), pl.BlockSpec((CONV_PAD, cb), lambda j: (0, j)),
                   vec, vec],
        out_shape=[jax.ShapeDtypeStruct((2, s, d), BF16), jax.ShapeDtypeStruct((CONV_PAD, d), F32),
                   jax.ShapeDtypeStruct((1, d), F32), jax.ShapeDtypeStruct((1, d), F32)],
        scratch_shapes=[pltpu.VMEM((s + CONV_PAD, cb), F32), pltpu.VMEM((s + CONV_PAD, cb), F32),
                        pltpu.VMEM((8 * CONV_PAD, cb), F32)],
        compiler_params=_params("parallel"),
    )(u, u, w_dw, dc)


def _stack_heads(x, group):
    return jnp.concatenate([x[:, g * HEAD_DIM:(g + 1) * HEAD_DIM] for g in range(group)], axis=0)


def _unstack_heads(x, group):
    return jnp.concatenate([x[g * ATT_BLOCK:(g + 1) * ATT_BLOCK, :] for g in range(group)], axis=1)


def _stack_cols(x, group):
    return jnp.concatenate([x[:, g:g + 1] for g in range(group)], axis=0)


def _band_mask(nb, group):
    rows = group * ATT_BLOCK
    row = lax.broadcasted_iota(I32, (rows, 2 * ATT_BLOCK), 0) % ATT_BLOCK
    col = lax.broadcasted_iota(I32, (rows, 2 * ATT_BLOCK), 1)
    return (col >= row) & (col <= row + ATT_BLOCK) & ((col >= ATT_BLOCK) | (nb > 0))


def _window(ref, nb):
    prev = pl.multiple_of(jnp.maximum(nb - 1, 0) * ATT_BLOCK, ATT_BLOCK)
    cur = pl.multiple_of(nb * ATT_BLOCK, ATT_BLOCK)
    return jnp.concatenate([ref[pl.ds(prev, ATT_BLOCK), :], ref[pl.ds(cur, ATT_BLOCK), :]], axis=0)


def _residues_per_step(dil, nblk):
    return max(1, min(dil, ATT_STEP_BLOCKS // nblk))


def _attn_fwd(name, q, kv):
    dil, sd, d = q.shape
    group = d // HEAD_DIM // N_KV_HEADS
    gw = group * HEAD_DIM
    nblk = sd // ATT_BLOCK
    scale = 1.0 / math.sqrt(HEAD_DIM)
    nt = (((1,), (1,)), ((), ()))

    rb = _residues_per_step(dil, nblk)

    def body(q_all, k_all, v_all, o_all, lse_all):
        lane = lax.broadcasted_iota(I32, (ATT_BLOCK, LANES), 1)
        for rr in range(rb):
            q_ref, k_ref, v_ref, o_ref, lse_ref = [ref.at[rr] for ref in (q_all, k_all, v_all, o_all, lse_all)]

            def step(nb, carry):
                rows = pl.ds(pl.multiple_of(nb * ATT_BLOCK, ATT_BLOCK), ATT_BLOCK)
                qs = _stack_heads(q_ref[rows, :], group)
                kw = _window(k_ref, nb)
                vw = _window(v_ref, nb)
                sc = lax.dot_general(qs, kw, nt, preferred_element_type=F32) * scale
                sc = jnp.where(_band_mask(nb, group), sc, -jnp.inf)
                mx = jnp.max(sc, axis=-1, keepdims=True)
                p = jnp.exp(sc - mx)
                l = jnp.sum(p, axis=-1, keepdims=True)
                o = jnp.dot(p.astype(BF16), vw, preferred_element_type=F32) / l
                o_ref[rows, :] = _unstack_heads(o, group).astype(BF16)
                lse = mx + jnp.log(l)
                out = jnp.zeros((ATT_BLOCK, LANES), F32)
                for g in range(group):
                    out = jnp.where(lane == g, lse[g * ATT_BLOCK:(g + 1) * ATT_BLOCK, :], out)
                lse_ref[rows, :] = out
                return carry

            lax.fori_loop(0, nblk, step, 0, unroll=min(2, nblk))

    kvh = N_KV_HEADS
    qspec = pl.BlockSpec((rb, sd, gw), lambda r, h: (r, 0, h))
    kspec = pl.BlockSpec((rb, sd, HEAD_DIM), lambda r, h: (r, 0, h))
    return pl.pallas_call(
        body, name=name, grid=(dil // rb, kvh),
        in_specs=[qspec, kspec, pl.BlockSpec((rb, sd, HEAD_DIM), lambda r, h: (r, 0, kvh + h))],
        out_specs=[qspec, kspec],
        out_shape=[jax.ShapeDtypeStruct((dil, sd, d), BF16),
                   jax.ShapeDtypeStruct((dil, sd, kvh * LANES), F32)],
        compiler_params=_params("parallel", "parallel"),
    )(q, kv, kv)


def _attn_bwd(name, q, kv, do, lse, delta):
    dil, sd, d = q.shape
    group = d // HEAD_DIM // N_KV_HEADS
    gw = group * HEAD_DIM
    nblk = sd // ATT_BLOCK
    scale = 1.0 / math.sqrt(HEAD_DIM)
    nt = (((1,), (1,)), ((), ()))
    tn = (((0,), (0,)), ((), ()))

    rb = _residues_per_step(dil, nblk)

    def body(q_all, k_all, v_all, do_all, lse_all, dl_all, dq_all, dk_all, dv_all, dk_accs, dv_accs):
        dk_accs[...] = jnp.zeros_like(dk_accs)
        dv_accs[...] = jnp.zeros_like(dv_accs)
        for rr in range(rb):
            q_ref, k_ref, v_ref, do_ref, lse_ref, dl_ref, dq_ref, dk_ref, dv_ref, dk_acc, dv_acc = [
                ref.at[rr] for ref in (q_all, k_all, v_all, do_all, lse_all, dl_all, dq_all, dk_all, dv_all,
                                       dk_accs, dv_accs)]

            def step(nb, carry):
                rows = pl.ds(pl.multiple_of(nb * ATT_BLOCK, ATT_BLOCK), ATT_BLOCK)
                qs = _stack_heads(q_ref[rows, :], group)
                dos = _stack_heads(do_ref[rows, :], group)
                ls = _stack_cols(lse_ref[rows, :], group)
                dl = _stack_cols(dl_ref[rows, :], group)
                kw = _window(k_ref, nb)
                vw = _window(v_ref, nb)
                sc = lax.dot_general(qs, kw, nt, preferred_element_type=F32) * scale
                sc = jnp.where(_band_mask(nb, group), sc, -jnp.inf)
                p = jnp.exp(sc - ls)
                dp = lax.dot_general(dos, vw, nt, preferred_element_type=F32)
                ds = (p * (dp - dl) * scale).astype(BF16)
                dq = jnp.dot(ds, kw, preferred_element_type=F32)
                dq_ref[rows, :] = _unstack_heads(dq, group).astype(BF16)
                win = pl.ds(pl.multiple_of(nb * ATT_BLOCK, ATT_BLOCK), 2 * ATT_BLOCK)
                dk_acc[win, :] += lax.dot_general(ds, qs, tn, preferred_element_type=F32)
                dv_acc[win, :] += lax.dot_general(p.astype(BF16), dos, tn, preferred_element_type=F32)
                return carry

            lax.fori_loop(0, nblk, step, 0, unroll=min(2, nblk))
            dk_ref[...] = dk_acc[ATT_BLOCK:, :]
            dv_ref[...] = dv_acc[ATT_BLOCK:, :]

    kvh = N_KV_HEADS
    qspec = pl.BlockSpec((rb, sd, gw), lambda r, h: (r, 0, h))
    kspec = pl.BlockSpec((rb, sd, HEAD_DIM), lambda r, h: (r, 0, h))
    return pl.pallas_call(
        body, name=name, grid=(dil // rb, kvh),
        in_specs=[qspec, kspec, pl.BlockSpec((rb, sd, HEAD_DIM), lambda r, h: (r, 0, kvh + h)),
                  qspec, kspec, kspec],
        out_specs=[qspec, kspec, kspec],
        out_shape=[jax.ShapeDtypeStruct((dil, sd, d), BF16),
                   jax.ShapeDtypeStruct((dil, sd, kvh * HEAD_DIM), F32),
                   jax.ShapeDtypeStruct((dil, sd, kvh * HEAD_DIM), F32)],
        scratch_shapes=[pltpu.VMEM((rb, sd + ATT_BLOCK, HEAD_DIM), F32)] * 2,
        compiler_params=_params("parallel", "parallel"),
    )(q, kv, kv, do, lse, delta)


def _cast_bf16(name, w, layer, place, after=None):
    _, r, c = w.shape
    tr = min(512, r)
    deps = [] if after is None else [after]

    def body(pl_ref, w_ref, *refs):
        refs[-1][...] = w_ref[...].astype(BF16)

    return pl.pallas_call(
        body, name=name,
        grid_spec=pltpu.PrefetchScalarGridSpec(
            num_scalar_prefetch=1, grid=(r // tr,),
            in_specs=[pl.BlockSpec((None, tr, c), lambda i, p: (layer, i, 0))] + [ANY] * len(deps),
            out_specs=pl.BlockSpec((None, tr, c), lambda i, p: (p[1], i, 0))),
        out_shape=jax.ShapeDtypeStruct((N_SHARD, r, c), BF16),
        compiler_params=_params("parallel"),
    )(place, w, *deps)


def _chip_sum(name, g, rh, place):
    _, r, c = g.shape
    rh2 = r // 2
    tr = min(512, rh2)
    nb = rh2 // tr

    def body(pl_ref, g_ref, rh_ref, o_ref):
        o_ref[...] = (g_ref[...].astype(F32) + rh_ref[...].astype(F32)).astype(BF16)

    return pl.pallas_call(
        body, name=name,
        grid_spec=pltpu.PrefetchScalarGridSpec(
            num_scalar_prefetch=1, grid=(N_SHARD, nb),
            in_specs=[pl.BlockSpec((None, tr, c), lambda s, i, p: (s, p[0] * nb + i, 0)),
                      pl.BlockSpec((None, tr, c), lambda s, i, p: (s, i, 0))],
            out_specs=pl.BlockSpec((None, tr, c), lambda s, i, p: (s, i, 0))),
        out_shape=jax.ShapeDtypeStruct((N_SHARD, rh2, c), BF16),
        compiler_params=_params("parallel", "parallel"),
    )(place, g, rh)


def _owner_sum(name, cs, rp, place):
    _, rh2, c = cs.shape
    tr = min(512, rh2)
    nb = rh2 // tr

    def body(pl_ref, cs_ref, r0_ref, r1_ref, r2_ref, o_ref):
        o_ref[...] = ((cs_ref[...].astype(F32) + r0_ref[...].astype(F32))
                      + (r1_ref[...].astype(F32) + r2_ref[...].astype(F32)))

    def rspec(j):
        return pl.BlockSpec((None, tr, c), lambda i, p: (j, i, 0))

    return pl.pallas_call(
        body, name=name,
        grid_spec=pltpu.PrefetchScalarGridSpec(
            num_scalar_prefetch=1, grid=(nb,),
            in_specs=[pl.BlockSpec((None, tr, c), lambda i, p: (p[1], i, 0)), rspec(0), rspec(1), rspec(2)],
            out_specs=pl.BlockSpec((tr, c), lambda i, p: (p[0] * nb + i, 0))),
        out_shape=jax.ShapeDtypeStruct((2 * rh2, c), F32),
        compiler_params=_params("parallel"),
    )(place, cs, rp, rp, rp)


def _adam_math(w, g, m, v):
    m = ADAM_B1 * m + (1.0 - ADAM_B1) * g
    v = ADAM_B2 * v + (1.0 - ADAM_B2) * (g * g)
    m_hat = m / (1.0 - ADAM_B1 ** ADAM_STEP)
    v_hat = v / (1.0 - ADAM_B2 ** ADAM_STEP)
    delta = -ADAM_LR * (m_hat / (jnp.sqrt(v_hat) + ADAM_EPS) + ADAM_WD * w)
    return delta, m, v


def _adamw(name, w, m, v, g, layer, partial=None):
    nl, r, c = w.shape
    tr = min(256, r)

    def body(w_ref, m_ref, v_ref, g_ref, *refs):
        go_ref, d_ref, mo_ref, vo_ref = refs[-4:]
        gv = g_ref[...]
        delta, m_new, v_new = _adam_math(w_ref[...], gv, m_ref[...], v_ref[...])
        go_ref[...] = gv
        d_ref[...] = delta
        mo_ref[...] = m_new
        vo_ref[...] = v_new

    wspec = pl.BlockSpec((None, tr, c), lambda i: (layer, i, 0))
    prev = [] if partial is None else list(partial)
    return pl.pallas_call(
        body, name=name, grid=(r // tr,),
        in_specs=[wspec] * 3 + [pl.BlockSpec((tr, c), lambda i: (i, 0))] + [ANY] * len(prev),
        out_specs=[wspec] * 4,
        out_shape=[jax.ShapeDtypeStruct((nl, r, c), F32)] * 4,
        input_output_aliases={4 + i: i for i in range(len(prev))},
        compiler_params=_params("parallel"),
    )(w, m, v, g, *prev)


def _adam_small(ws, ms, vs, gs):
    n = len(ws)

    def body(*refs):
        w_refs, m_refs, v_refs, g_refs = refs[:n], refs[n:2 * n], refs[2 * n:3 * n], refs[3 * n:4 * n]
        d_refs, mo_refs, vo_refs = refs[4 * n:5 * n], refs[5 * n:6 * n], refs[6 * n:7 * n]
        for i in range(n):
            delta, m_new, v_new = _adam_math(w_refs[i][...], g_refs[i][...], m_refs[i][...], v_refs[i][...])
            d_refs[i][...] = delta
            mo_refs[i][...] = m_new
            vo_refs[i][...] = v_new

    shapes = [jax.ShapeDtypeStruct(w.shape, F32) for w in ws]
    res = pl.pallas_call(body, name="adam_small", out_shape=shapes * 3)(*ws, *ms, *vs, *gs)
    return res[:n], res[n:2 * n], res[2 * n:]


def _pack_small(b_in, w_dw, b_dw, ln_g, ln_b, b_out, place):
    cin = b_in.shape[1]
    cd = b_dw.shape[1]
    rows = 8 + CONV_PAD

    def body(pl_ref, bi, wd, bd, lg, lb, bo, out):
        out[...] = jnp.zeros_like(out)
        out[0:1, :] = bi[...]
        out[1:2, 0:cd] = bd[...]
        out[1:2, cd:2 * cd] = lg[...]
        out[2:3, 0:cd] = lb[...]
        out[2:3, cd:2 * cd] = bo[...]
        out[8:8 + CONV_WIDTH, 0:cd] = wd[...]

    def whole(arr):
        return pl.BlockSpec(arr.shape, lambda i, p: (0,) * arr.ndim)

    ins = [b_in, w_dw, b_dw, ln_g, ln_b, b_out]
    return pl.pallas_call(
        body, name="pack_small",
        grid_spec=pltpu.PrefetchScalarGridSpec(
            num_scalar_prefetch=1, grid=(1,), in_specs=[whole(a) for a in ins],
            out_specs=pl.BlockSpec((None, rows, cin), lambda i, p: (p[1], 0, 0))),
        out_shape=jax.ShapeDtypeStruct((N_SHARD, rows, cin), F32),
        compiler_params=_params("arbitrary"),
    )(place, *ins)


def _place():
    x, y, c = lax.axis_index("x"), lax.axis_index("y"), lax.axis_index("c")
    return x, y, c


def _other_chips(x, y):
    return [(1 - x, y), (x, 1 - y), (1 - x, 1 - y)]


def _split_start(name, bufs, n_sem, copies, after=None):
    n = len(bufs)
    deps = [] if after is None else [after]

    def body(*refs):
        out0 = n + len(deps)
        for cp in copies(refs[:n], refs[out0], refs[out0 + 1], False):
            cp.start()
        refs[-1][...] = jnp.zeros_like(refs[-1])

    res = pl.pallas_call(
        body, name=name,
        out_shape=(pltpu.SemaphoreType.DMA((n_sem,)), pltpu.SemaphoreType.DMA((n_sem,)),
                   *[pltpu.HBM(b.shape, b.dtype) for b in bufs], jax.ShapeDtypeStruct((8, LANES), F32)),
        in_specs=[HBM] * n + [ANY] * len(deps),
        out_specs=(SEM, SEM, *[HBM] * n, pl.BlockSpec(memory_space=pltpu.VMEM)),
        input_output_aliases={i: 2 + i for i in range(n)},
        compiler_params=pltpu.CompilerParams(has_side_effects=SPLIT_EFFECT),
    )(*[pltpu.with_memory_space_constraint(b, pltpu.HBM) for b in bufs], *deps)
    return res[0], res[1], list(res[2:2 + n]), res[-1]


def _split_wait(name, handle, copies, after):
    ssem, rsem, bufs, _ = handle
    n = len(bufs)
    deps = list(after) if isinstance(after, (list, tuple)) else [after]

    def body(*refs):
        for cp in copies(refs[:n], refs[n], refs[n + 1], True):
            cp.wait_send()
            cp.wait_recv()

    res = pl.pallas_call(
        body, name=name,
        out_shape=[pltpu.HBM(b.shape, b.dtype) for b in bufs],
        in_specs=[HBM] * n + [SEM, SEM] + [ANY] * len(deps), out_specs=[HBM] * n,
        input_output_aliases={i: i for i in range(n)},
        compiler_params=pltpu.CompilerParams(has_side_effects=SPLIT_EFFECT),
    )(*bufs, ssem, rsem, *deps)
    return list(res)


def _remote(src, dst, ssem, rsem, k, to):
    return pltpu.make_async_remote_copy(src_ref=src, dst_ref=dst, send_sem=ssem.at[k], recv_sem=rsem.at[k],
                                        device_id=to, device_id_type=MESH)


def _gather_chips(x, y, c):
    nx, ny = x + (1 - c) - 2 * x * (1 - c), y + c - 2 * y * c
    fx, fy = x + c - 2 * x * c, y + (1 - c) - 2 * y * (1 - c)
    return (nx, ny), (fx, fy), 2 * nx + ny, 2 * fx + fy, 2 * (1 - x) + (1 - y)


def _direct_copies(refs, ssem, rsem, landing, n_whole=0):
    x, y, c = _place()
    me = 2 * x + y
    (nx, ny), _, near, _, _ = _gather_chips(x, y, c)
    n = len(refs) - n_whole
    cps = []
    for a, ref in enumerate(refs[:n]):
        cps.append(_remote(ref.at[me], ref.at[near if landing else me], ssem, rsem, a, (nx, ny, c)))
    for b, ref in enumerate(refs[n:]):
        for j, (px, py) in enumerate(_other_chips(x, y)):
            cps.append(_remote(ref.at[me], ref.at[2 * px + py if landing else me], ssem, rsem, n + 3 * b + j,
                               (px, py, c)))
    return cps


def _relay_copies(refs, ssem, rsem, landing):
    x, y, c = _place()
    _, (fx, fy), near, far, diag = _gather_chips(x, y, c)
    n = len(refs)
    cps = []
    for a, ref in enumerate(refs):
        rh = ref.shape[1] // 2
        rows = pl.ds(c * rh, rh)
        cps.append(_remote(ref.at[near, rows], ref.at[diag if landing else near, rows], ssem, rsem, a, (fx, fy, c)))
        cps.append(_remote(ref.at[near], ref.at[far if landing else near], ssem, rsem, n + a, (x, y, 1 - c)))
    return cps


def _diagonal_copies(refs, ssem, rsem, landing):
    x, y, c = _place()
    diag = 2 * (1 - x) + (1 - y)
    who = 1 - c if landing else c
    cps = []
    for a, ref in enumerate(refs):
        rh = ref.shape[1] // 2
        piece = ref.at[diag, pl.ds(who * rh, rh)]
        cps.append(_remote(piece, piece, ssem, rsem, a, (x, y, 1 - c)))
    return cps


def _sibling_copies(refs, ssem, rsem, landing):
    x, y, c = _place()
    n = len(refs) // 2
    cps = []
    for a in range(n):
        rh = refs[a].shape[1] // 2
        cps.append(_remote(refs[a].at[:, pl.ds((1 - c) * rh, rh), :], refs[n + a], ssem, rsem, a, (x, y, 1 - c)))
    return cps


def _owner_copies(refs, ssem, rsem, landing):
    x, y, c = _place()
    n = len(refs) // 2
    cps = []
    for a in range(n):
        for j, (px, py) in enumerate(_other_chips(x, y)):
            cps.append(_remote(refs[a].at[2 * px + py], refs[n + a].at[j], ssem, rsem, 3 * a + j, (px, py, c)))
    return cps


def _swap_copies(refs, ssem, rsem, landing):
    x, y, c = _place()
    who = 1 - c if landing else c
    cps = []
    for a, ref in enumerate(refs):
        rh = ref.shape[0] // 2
        rows = ref.at[pl.ds(who * rh, rh)]
        cps.append(_remote(rows, rows, ssem, rsem, a, (x, y, 1 - c)))
    return cps


def _small_copies(refs, ssem, rsem, landing):
    pack, slots = refs
    x, y, c = _place()
    cps = []
    for rel in range(1, N_DEV):
        px = 1 - x if (rel >> 2) & 1 else x
        py = 1 - y if (rel >> 1) & 1 else y
        pc = 1 - c if rel & 1 else c
        slot = 4 * px + 2 * py + pc if landing else 4 * x + 2 * y + c
        cps.append(_remote(pack, slots.at[slot], ssem, rsem, rel - 1, (px, py, pc)))
    return cps


def _small_pack(rows, w_dw_grad, d):
    n = len(rows)

    def body(*refs):
        pack = refs[-1]
        pack[...] = jnp.zeros_like(pack)
        for (r, _), ref in zip(rows, refs[:n]):
            pack[r:r + 1, :] = ref[...]
        pack[16:16 + CONV_PAD, :] = refs[n][...]

    return pl.pallas_call(body, name="small_pack", out_shape=jax.ShapeDtypeStruct((SMALL_ROWS, d), F32))(
        *[v for _, v in rows], w_dw_grad)


def _small_sum(pack, slots, place):
    rows, d = pack.shape
    loss_row = 12

    def body(pl_ref, pack_ref, slots_ref, out_ref):
        me = pl_ref[2]
        tot = jnp.where(me == 0, pack_ref[...], slots_ref[0])
        for i in range(1, N_DEV):
            tot = tot + jnp.where(me == i, pack_ref[...], slots_ref[i])
        out_ref[...] = tot
        out_ref[loss_row:loss_row + 1, :] = jnp.zeros((1, d), F32) + jnp.sum(tot[loss_row:loss_row + 1, :])

    return pl.pallas_call(
        body, name="small_sum",
        grid_spec=pltpu.PrefetchScalarGridSpec(
            num_scalar_prefetch=1, grid=(1,),
            in_specs=[pl.BlockSpec((rows, d), lambda i, p: (0, 0)), pl.BlockSpec((N_DEV, rows, d), lambda i, p: (0, 0, 0))],
            out_specs=pl.BlockSpec((rows, d), lambda i, p: (0, 0))),
        out_shape=jax.ShapeDtypeStruct((rows, d), F32),
        compiler_params=_params("arbitrary"),
    )(place, pack, slots)


def kernel(x, norm_mix, norm_mlp, conv_w_in, conv_b_in, conv_w_dw, conv_b_dw, conv_ln_g, conv_ln_b, conv_w_out, conv_b_out, kv_norm, w_kv, attn_w_q, attn_w_o, mlp_w_in, mlp_w_out, final_norm, loss_target, m_norm_mix, m_norm_mlp, m_conv_w_in, m_conv_b_in, m_conv_w_dw, m_conv_b_dw, m_conv_ln_g, m_conv_ln_b, m_conv_w_out, m_conv_b_out, m_kv_norm, m_w_kv, m_attn_w_q, m_attn_w_o, m_mlp_w_in, m_mlp_w_out, m_final_norm, v_norm_mix, v_norm_mlp, v_conv_w_in, v_conv_b_in, v_conv_w_dw, v_conv_b_dw, v_conv_ln_g, v_conv_ln_b, v_conv_w_out, v_conv_b_out, v_kv_norm, v_w_kv, v_attn_w_q, v_attn_w_o, v_mlp_w_in, v_mlp_w_out, v_final_norm):
    _, s, d = x.shape
    dff = mlp_w_in.shape[2] * N_SHARD
    kvw = w_kv.shape[1]
    nh = d // HEAD_DIM
    group = nh // N_KV_HEADS
    ds4 = d // N_SHARD
    xi, yi, ci = _place()
    me = 2 * xi + yi
    place = jnp.stack([ci, me, 2 * me + ci]).astype(I32)

    h0 = x.reshape(s, d)
    target = loss_target.reshape(s, d)
    tabs = _rope_tables(s)

    def gather_begin(tag, bufs, n_whole=0):
        plan = functools.partial(_direct_copies, n_whole=n_whole)
        return _split_start(f"gather_start_{tag}", bufs, len(bufs) + 2 * n_whole, plan), plan, n_whole

    def gather_land(tag, begun, later):
        handle, plan, n_whole = begun
        bufs = _split_wait(f"gather_wait_{tag}", handle, plan, later)
        n = len(bufs) - n_whole
        return _split_start(f"relay_start_{tag}", bufs[:n], 2 * n, _relay_copies), bufs[n:]

    def gather_swap(tag, landed, later):
        relayed, whole = landed
        bufs = _split_wait(f"relay_wait_{tag}", relayed, _relay_copies, later)
        return _split_start(f"diagonal_start_{tag}", bufs, len(bufs), _diagonal_copies), whole

    def gather_end(tag, swapped, later):
        handle, whole = swapped
        return _split_wait(f"diagonal_wait_{tag}", handle, _diagonal_copies, later) + whole

    def tied(vec, begun):
        return vec + begun[0][3][0:1, 0:1]

    ag_cin = gather_begin("conv_in", [
        _cast_bf16("cast_w_in", conv_w_in, 0, place),
        _pack_small(conv_b_in, conv_w_dw.reshape(CONV_WIDTH, ds4), conv_b_dw, conv_ln_g, conv_ln_b, conv_b_out, place),
    ], n_whole=1)
    ag_cout = gather_begin("conv_out", [_cast_bf16("cast_w_out", conv_w_out, 0, place, ag_cin[0][3])])
    ag_mi0 = gather_begin("mlp_in0", [_cast_bf16("cast_mlp_in0", mlp_w_in, 0, place, ag_cout[0][3])])
    ag_mo0 = gather_begin("mlp_out0", [_cast_bf16("cast_mlp_out0", mlp_w_out, 0, place, ag_mi0[0][3])])
    nm = [norm_mix[0:1], norm_mix[1:2]]
    nmlp = [norm_mlp[0:1], norm_mlp[1:2]]
    kvn = kv_norm.reshape(1, d)
    fin = final_norm.reshape(1, d)
    (y0,) = _rms_fwd("rms_mix0", h0, [tied(nm[0], ag_mo0)])
    land_cin = gather_land("conv_in", ag_cin, y0)
    ag_attn = gather_begin("attn", [
        _cast_bf16("cast_w_kv", w_kv.reshape(1, ds4, kvw), 0, place, land_cin[0][3]),
        _cast_bf16("cast_w_q", attn_w_q, 0, place), _cast_bf16("cast_w_o", attn_w_o, 0, place)])
    ag_mi1 = gather_begin("mlp_in1", [_cast_bf16("cast_mlp_in1", mlp_w_in, 1, place, ag_attn[0][3])])
    ag_mo1 = gather_begin("mlp_out1", [_cast_bf16("cast_mlp_out1", mlp_w_out, 1, place, ag_mi1[0][3])])
    land_cout = gather_land("conv_out", ag_cout, ag_mo1[0][3])

    wmi_g = [None, None]
    wmo_f = [None, None]

    swap_cin = gather_swap("conv_in", land_cin, land_cout[0][3])
    w_in_g, small_g = gather_end("conv_in", swap_cin, swap_cin[0][3])
    b_in_f = small_g[:, 0, :].reshape(1, 2 * d)
    b_dw_f = small_g[:, 1, 0:ds4].reshape(1, d)
    ln_g_f = small_g[:, 1, ds4:2 * ds4].reshape(1, d)
    ln_b_f = small_g[:, 2, 0:ds4].reshape(1, d)
    b_out_f = small_g[:, 2, ds4:2 * ds4].reshape(1, d)
    w_dw_f = jnp.transpose(small_g[:, 8:8 + CONV_PAD, 0:ds4], (1, 0, 2)).reshape(CONV_PAD, d)

    def ep_bias(acc, ex, outs, j):
        outs[0][...] = (acc + ex[0][...]).astype(outs[0].dtype)

    def ep_residual(acc, ex, outs, j):
        outs[0][...] = ex[0][...] + acc

    def ep_residual_bias(acc, ex, outs, j):
        outs[0][...] = ex[0][...] + (acc + ex[1][...])

    def ep_relu2(acc, ex, outs, j):
        r = jnp.maximum(acc, 0.0)
        outs[0][...] = r.astype(BF16)
        outs[1][...] = (r * r).astype(BF16)

    by_residue = [(BF16, ("residues", dil)) for dil in DILATIONS]

    def put_by_residue(val, outs, stage):
        _to_residues(val, stage, outs, DILATIONS)

    def ep_rope(acc, ex, outs, j, stage):
        put_by_residue(_rope_apply(acc, ex[0][...], ex[1][...], ex[2][...], 1.0), outs, stage)

    def ep_rope_k(acc, ex, outs, j, stage):
        roped = _rope_apply(acc, ex[0][...], ex[1][...], ex[2][...], 1.0)
        put_by_residue(jnp.where(j == 0, roped, acc), outs, stage)

    def ep_by_residue(acc, ex, outs, j, stage):
        put_by_residue(acc, outs, stage)

    tab_extras = [(t, "rows") for t in tabs]

    def mlp_fwd(idx, h, y, out_weight):
        r, r2 = _matmul(f"mlp_in{idx}", "nn", y, wmi_g[idx], b_kind="col", m=s, n=dff, k=d,
                        outs=[(BF16, "plain"), (BF16, "plain")], epilogue=ep_relu2)
        wmo_f[idx] = out_weight(r2).reshape(dff, d)
        (h_new,) = _matmul(f"mlp_out{idx}", "nn", r2, wmo_f[idx], m=s, n=d, k=dff,
                           outs=[(F32, "plain")], extras=[(h, "ij")], epilogue=ep_residual)
        return h_new, r, r2

    (u,) = _matmul("conv_in", "nn", y0, w_in_g, b_kind="col", m=s, n=2 * d, k=d,
                   outs=[(BF16, "plain")], extras=[(b_in_f, "vec")], epilogue=ep_bias)
    land_mi0 = gather_land("mlp_in0", ag_mi0, u)
    swap_cout = gather_swap("conv_out", land_cout, land_mi0[0][3])
    cpre = _dwconv_fwd(u, w_dw_f, tied(b_dw_f, swap_cout))
    sact = _ln_silu_fwd(cpre, ln_g_f, ln_b_f)
    (w_out_g,) = gather_end("conv_out", swap_cout, sact)
    w_out_f = w_out_g.reshape(d, d)
    (h1,) = _matmul("conv_out", "nn", sact, w_out_f, m=s, n=d, k=d,
                    outs=[(F32, "plain")], extras=[(h0, "ij"), (b_out_f, "vec")], epilogue=ep_residual_bias)
    swap_mi0 = gather_swap("mlp_in0", land_mi0, h1)
    (y1,) = _rms_fwd("rms_mlp0", h1, [tied(nmlp[0], swap_mi0)])
    land_mo0 = gather_land("mlp_out0", ag_mo0, y1)
    (wmi_g[0],) = gather_end("mlp_in0", swap_mi0, land_mo0[0][3])
    land_attn = None

    def out_weight0(r2):
        nonlocal land_attn
        land_attn = gather_land("attn", ag_attn, r2)
        swap_mo0 = gather_swap("mlp_out0", land_mo0, land_attn[0][3])
        return gather_end("mlp_out0", swap_mo0, swap_mo0[0][3])[0]

    h2, r0, r0sq = mlp_fwd(0, h1, y1, out_weight0)
    swap_attn = gather_swap("attn", land_attn, h2)
    land_mi1 = gather_land("mlp_in1", ag_mi1, swap_attn[0][3])
    ykv, y2 = _rms_fwd("rms_kv_mix1", h2, [tied(kvn, land_mi1), nm[1]])
    wkv_g, wq_g, wo_g = gather_end("attn", swap_attn, y2)
    wkv_f, wq_f, wo_f = wkv_g.reshape(d, kvw), wq_g.reshape(d, d), wo_g.reshape(d, d)
    kv_parts = _matmul("kv_proj", "nn", ykv, wkv_f, m=s, n=kvw, k=d, tn=kvw // 2,
                       outs=by_residue, extras=tab_extras, epilogue=ep_rope_k, stage=True)
    q_parts = _matmul("q_proj", "nn", y2, wq_f, m=s, n=d, k=d,
                      outs=by_residue, extras=tab_extras, epilogue=ep_rope, stage=True)
    o_parts, lse_parts = [], []
    for dil, q_b, kv_b in zip(DILATIONS, q_parts, kv_parts):
        o_b, lse_b = _attn_fwd(f"attn_fwd_d{dil}", q_b, kv_b)
        o_parts.append(o_b)
        lse_parts.append(lse_b)
    swap_mi1 = gather_swap("mlp_in1", land_mi1, o_parts)
    o, lse = _attn_combine(o_parts, lse_parts)
    land_mo1 = gather_land("mlp_out1", ag_mo1, o)
    (h3,) = _matmul("attn_out", "nn", o, wo_f, m=s, n=d, k=d,
                    outs=[(F32, "plain")], extras=[(h2, "ij")], epilogue=ep_residual)
    (y3,) = _rms_fwd("rms_mlp1", h3, [tied(nmlp[1], land_mo1)])
    (wmi_g[1],) = gather_end("mlp_in1", swap_mi1, y3)

    def out_weight1(r2):
        swap_mo1 = gather_swap("mlp_out1", land_mo1, r2)
        return gather_end("mlp_out1", swap_mo1, swap_mo1[0][3])[0]

    h4, r1, r1sq = mlp_fwd(1, h3, y3, out_weight1)
    dh4, dh4b, d_fin, loss_cols = _final_loss(h4, fin, target)

    def ep_relu2_bwd(acc, ex, outs, j):
        outs[0][...] = (acc * (2.0 * ex[0][...].astype(F32))).astype(BF16)

    def mlp_bwd(idx, dhb, y, r, r2):
        (dz,) = _matmul(f"mlp_out{idx}_dx", "nt", dhb, wmo_f[idx], m=s, n=dff, k=d,
                        outs=[(BF16, "plain")], extras=[(r, "ij")], epilogue=ep_relu2_bwd)
        (dwo,) = _matmul(f"mlp_out{idx}_dw", "tn", r2, dhb, m=dff, n=d, k=s,
                         outs=[(BF16, "plain")])
        (dy,) = _matmul(f"mlp_in{idx}_dx", "nt", dz, wmi_g[idx], b_kind="col", m=s, n=d, k=dff,
                        outs=[(BF16, "plain")])
        (dwi,) = _matmul(f"mlp_in{idx}_dw", "tn", y, dz, m=d, n=dff, k=s,
                         outs=[(BF16, "col")])
        return dy, dwi, dwo.reshape(N_SHARD, dff // N_SHARD, d)

    def token(handle):
        return handle[3][0:1, 0:1]

    def rs_exchange(tag, grads):
        lands = [lax.empty((N_SHARD, g.shape[1] // 2, g.shape[2]), g.dtype) for g in grads]
        return _split_start(f"sibling_start_{tag}", list(grads) + lands, len(grads), _sibling_copies)

    def rs_send(tag, names, exchanged, later):
        bufs = _split_wait(f"sibling_wait_{tag}", exchanged, _sibling_copies, later)
        n = len(names)
        sums = [_chip_sum(f"chip_sum_{nme}", g, rh, place) for nme, g, rh in zip(names, bufs[:n], bufs[n:])]
        lands = [lax.empty((N_SHARD - 1,) + cs.shape[1:], cs.dtype) for cs in sums]
        return _split_start(f"owners_start_{tag}", sums + lands, 3 * n, _owner_copies)

    def rs_sum(tag, names, sent, later):
        bufs = _split_wait(f"owners_wait_{tag}", sent, _owner_copies, later)
        n = len(names)
        own = [_owner_sum(f"owner_sum_{nme}", cs, rp, place) for nme, cs, rp in zip(names, bufs[:n], bufs[n:])]
        return _split_start(f"swap_start_{tag}", own, n, _swap_copies)

    def rs_end(tag, swapped, later):
        return _split_wait(f"swap_wait_{tag}", swapped, _swap_copies, later)

    dy3, g_wmi1, g_wmo1 = mlp_bwd(1, dh4b, y3, r1, r1sq)
    x_mlp1 = rs_exchange("mlp1", [g_wmi1, g_wmo1])
    dh3, dh3b, d_nmlp1 = _rms_bwd("rms_mlp1_bwd", h3, [(nmlp[1] + token(x_mlp1), dy3)], dh4)

    do_parts = _matmul("attn_out_dx", "nt", dh3b, wo_f, m=s, n=d, k=d, outs=by_residue, epilogue=ep_by_residue,
                       stage=True)
    (g_wo,) = _matmul("attn_out_dw", "tn", o, dh3b, m=d, n=d, k=s, outs=[(BF16, "plain")])
    rs_mlp1 = rs_send("mlp1", ["mlp_in1", "mlp_out1"], x_mlp1, g_wo)
    lse_res, delta_res = _attn_delta(do_parts[0].reshape(s, d), o, lse, DILATIONS)
    dq_parts, dk_parts, dv_parts = [], [], []
    for dil, q_b, kv_b, do_b, lse_b, dl_b in zip(DILATIONS, q_parts, kv_parts, do_parts, lse_res, delta_res):
        dq_b, dk_b, dv_b = _attn_bwd(f"attn_bwd_d{dil}", q_b, kv_b, do_b, lse_b, dl_b)
        dq_parts.append(dq_b)
        dk_parts.append(dk_b)
        dv_parts.append(dv_b)
    dq = _residue_sum("rope_bwd_q", [(dq_parts, True)], tabs)
    dkv = _residue_sum("rope_bwd_kv", [(dk_parts, True), (dv_parts, False)], tabs)
    (g_wq,) = _matmul("q_proj_dw", "tn", y2, dq, m=d, n=d, k=s, outs=[(BF16, "plain")])
    (dy2,) = _matmul("q_proj_dx", "nt", dq, wq_f, m=s, n=d, k=d, outs=[(BF16, "plain")])
    (g_wkv,) = _matmul("kv_proj_dw", "tn", ykv, dkv, m=d, n=kvw, k=s, outs=[(BF16, "plain")])
    (dykv,) = _matmul("kv_proj_dx", "nt", dkv, wkv_f, m=s, n=d, k=kvw, outs=[(BF16, "plain")])
    x_attn = rs_exchange("attn", [g_wkv.reshape(N_SHARD, ds4, kvw), g_wq.reshape(N_SHARD, ds4, d),
                                  g_wo.reshape(N_SHARD, ds4, d)])
    dh2, dh2b, d_nm1, d_kvn = _rms_bwd("rms_kv_mix1_bwd", h2, [(nm[1] + token(x_attn), dy2), (kvn, dykv)], dh3)
    rs_attn = rs_send("attn", ["w_kv", "w_q", "w_o"], x_attn, dh2b)

    dy1, g_wmi0, g_wmo0 = mlp_bwd(0, dh2b, y1, r0, r0sq)
    x_mlp0 = rs_exchange("mlp0", [g_wmi0, g_wmo0])
    dh1, dh1b, d_nmlp0, d_b_out = _rms_bwd("rms_mlp0_bwd", h1, [(nmlp[0] + token(x_mlp0) + token(rs_attn), dy1)],
                                           dh2, want_colsum=True)

    (dsact,) = _matmul("conv_out_dx", "nt", dh1b, w_out_f, m=s, n=d, k=d, outs=[(BF16, "plain")])
    (g_wout,) = _matmul("conv_out_dw", "tn", sact, dh1b, m=d, n=d, k=s, outs=[(BF16, "plain")])
    rs_mlp0 = rs_send("mlp0", ["mlp_in0", "mlp_out0"], x_mlp0, g_wout)
    dc, d_ln_g, d_ln_b, d_b_dw = _ln_silu_bwd(cpre, ln_g_f + token(rs_mlp0), ln_b_f, dsact)
    du, d_w_dw, d_b_in_a, d_b_in_g = _dwconv_bwd(u, w_dw_f, dc)
    (g_win,) = _matmul("conv_in_dw", "tn", y0, du, b_kind="col", m=d, n=2 * d, k=s, outs=[(BF16, "col")])
    x_conv = rs_exchange("conv", [g_win, g_wout.reshape(N_SHARD, ds4, d)])
    (dy0,) = _matmul("conv_in_dx", "nt", du, w_in_g, a_kind="col", b_kind="col", m=s, n=d, k=2 * d,
                     outs=[(BF16, "plain")])
    dx, _, d_nm0 = _rms_bwd("rms_mix0_bwd", h0, [(nm[0] + token(x_conv), dy0)], dh1)

    small_rows = [(0, d_nm0), (1, d_nm1), (2, d_nmlp0), (3, d_nmlp1), (4, d_kvn), (5, d_fin), (6, d_b_dw),
                  (7, d_ln_g), (8, d_ln_b), (9, d_b_out), (10, d_b_in_a), (11, d_b_in_g), (12, loss_cols)]
    x_small = _split_start("small_start", [_small_pack(small_rows, d_w_dw, d),
                                           lax.empty((N_DEV, SMALL_ROWS, d), F32)], N_DEV - 1, _small_copies)
    rs_conv = rs_send("conv", ["w_in", "w_out"], x_conv, x_small[3])

    def big(name, w, m, v, g, layer=0, partial=None):
        shape = w.shape
        w3, m3, v3 = [t.reshape((-1,) + shape[-2:]) for t in (w, m, v)]
        if partial is not None:
            partial = [t.reshape(w3.shape) for t in partial]
        res = _adamw(name, w3, m3, v3, g, layer, partial)
        return [t.reshape(shape) for t in res]

    sw_mlp1 = rs_sum("mlp1", ["mlp_in1", "mlp_out1"], rs_mlp1, rs_conv[3])
    sw_attn = rs_sum("attn", ["w_kv", "w_q", "w_o"], rs_attn, sw_mlp1[3])
    f_wmi1, f_wmo1 = rs_end("mlp1", sw_mlp1, sw_attn[3])
    p_wmi = big("adam_mlp_in1", mlp_w_in, m_mlp_w_in, v_mlp_w_in, f_wmi1, 1)
    p_wmo = big("adam_mlp_out1", mlp_w_out, m_mlp_w_out, v_mlp_w_out, f_wmo1, 1)
    sw_mlp0 = rs_sum("mlp0", ["mlp_in0", "mlp_out0"], rs_mlp0, [p_wmi[0], p_wmo[0]])
    f_wkv, f_wq, f_wo = rs_end("attn", sw_attn, sw_mlp0[3])
    r_wkv = big("adam_w_kv", w_kv, m_w_kv, v_w_kv, f_wkv)
    r_wq = big("adam_w_q", attn_w_q, m_attn_w_q, v_attn_w_q, f_wq)
    r_wo = big("adam_w_o", attn_w_o, m_attn_w_o, v_attn_w_o, f_wo)
    sw_conv = rs_sum("conv", ["w_in", "w_out"], rs_conv, [r_wkv[0], r_wq[0], r_wo[0]])
    f_wmi0, f_wmo0 = rs_end("mlp0", sw_mlp0, sw_conv[3])
    r_wmi = big("adam_mlp_in0", mlp_w_in, m_mlp_w_in, v_mlp_w_in, f_wmi0, 0, p_wmi)
    r_wmo = big("adam_mlp_out0", mlp_w_out, m_mlp_w_out, v_mlp_w_out, f_wmo0, 0, p_wmo)
    f_win, f_wout = rs_end("conv", sw_conv, [r_wmi[0], r_wmo[0]])
    r_win = big("adam_w_in", conv_w_in, m_conv_w_in, v_conv_w_in, f_win)
    r_wout = big("adam_w_out", conv_w_out, m_conv_w_out, v_conv_w_out, f_wout)

    small_pack, small_slots = _split_wait("small_wait", x_small, _small_copies, r_wout[0])
    red = _small_sum(small_pack, small_slots, place)
    loss = red[12, 0]
    g_norm_mix = red[0:2]
    g_norm_mlp = red[2:4]
    g_kv_norm = red[4:5]
    g_final = red[5:6]

    def my_cols(row):
        return lax.dynamic_slice(red, (row, me * ds4), (1, ds4))

    g_b_dw, g_ln_g, g_ln_b, g_b_out = my_cols(6), my_cols(7), my_cols(8), my_cols(9)
    half_in = 2 * d // N_SHARD
    b_in_row = 10 + me // 2
    g_b_in = lax.dynamic_slice(red, (b_in_row, (me % 2) * half_in), (1, half_in))
    g_w_dw = lax.dynamic_slice(red, (16, me * ds4), (CONV_WIDTH, ds4))

    sm_w =[norm_mix, norm_mlp, conv_b_in, conv_w_dw.reshape(CONV_WIDTH, ds4), conv_b_dw, conv_ln_g, conv_ln_b,
            conv_b_out, kv_norm.reshape(1, d), final_norm.reshape(1, d)]
    sm_m = [m_norm_mix, m_norm_mlp, m_conv_b_in, m_conv_w_dw.reshape(CONV_WIDTH, ds4), m_conv_b_dw, m_conv_ln_g,
            m_conv_ln_b, m_conv_b_out, m_kv_norm.reshape(1, d), m_final_norm.reshape(1, d)]
    sm_v = [v_norm_mix, v_norm_mlp, v_conv_b_in, v_conv_w_dw.reshape(CONV_WIDTH, ds4), v_conv_b_dw, v_conv_ln_g,
            v_conv_ln_b, v_conv_b_out, v_kv_norm.reshape(1, d), v_final_norm.reshape(1, d)]
    sm_g = [g_norm_mix, g_norm_mlp, g_b_in, g_w_dw, g_b_dw, g_ln_g, g_ln_b, g_b_out, g_kv_norm, g_final]
    sm_d, sm_nm, sm_nv = _adam_small(sm_w, sm_m, sm_v, sm_g)
    shapes = [norm_mix.shape, norm_mlp.shape, conv_b_in.shape, conv_w_dw.shape, conv_b_dw.shape, conv_ln_g.shape,
              conv_ln_b.shape, conv_b_out.shape, kv_norm.shape, final_norm.shape]
    sm_g, sm_d, sm_nm, sm_nv = [[t.reshape(sh) for t, sh in zip(lst, shapes)] for lst in (sm_g, sm_d, sm_nm, sm_nv)]

    def order(sm, idx):
        return [sm[0], sm[1], r_win[idx], sm[2], sm[3], sm[4], sm[5], sm[6], r_wout[idx], sm[7], sm[8],
                r_wkv[idx], r_wq[idx], r_wo[idx], r_wmi[idx], r_wmo[idx], sm[9]]

    return (loss, dx.reshape(x.shape), *order(sm_g, 0), *order(sm_d, 1), *order(sm_nm, 2), *order(sm_nv, 3))
```

```python
import functools
import math

import jax
import jax.numpy as jnp
from jax import lax
from jax.experimental import pallas as pl
from jax.experimental.pallas import tpu as pltpu

F32 = jnp.float32
BF16 = jnp.bfloat16
I32 = jnp.int32

NORM_EPS = 1e-6
LN_EPS = 1e-5
HEAD_DIM = 128
N_KV_HEADS = 4
ROT_DIM = 32
ROPE_THETA = 500000.0
CONV_WIDTH = 31
CONV_PAD = 32
ATT_BLOCK = 128
ATT_STEP_BLOCKS = 16
DILATIONS = (1, 4, 16)
ADAM_LR = 0.001
ADAM_B1 = 0.9
ADAM_B2 = 0.999
ADAM_EPS = 1e-08
ADAM_WD = 0.01
ADAM_STEP = 10
N_SHARD = 4
N_DEV = 8
LANES = 128
VMEM_LIMIT = 48 * 1024 * 1024
MM_TM, MM_TN, MM_TK = 1024, 1024, 2048
ROW_TILE = 256
CONV_CB = 128
CONV_T = 128
SMALL_ROWS = 48
MESH = pl.DeviceIdType.MESH
ANY = pl.BlockSpec(memory_space=pl.ANY)
HBM = pl.BlockSpec(memory_space=pltpu.HBM)
SEM = pl.BlockSpec(memory_space=pltpu.SEMAPHORE)
SPLIT_EFFECT = pltpu.SideEffectType.DATAFLOW_SIDE_EFFECTING


def _params(*sem):
    return pltpu.CompilerParams(dimension_semantics=sem, vmem_limit_bytes=VMEM_LIMIT)


def _sigmoid(x):
    return 1.0 / (1.0 + jnp.exp(-x))


def _wspec(kind, arr_shape, br, bc, pick):
    if kind == "plain":
        return pl.BlockSpec((br, bc), pick)
    per = arr_shape[2] // bc

    def idx(*g):
        rb, cb = pick(*g)
        return (cb // per, rb, cb % per)

    return pl.BlockSpec((None, br, bc), idx)


def _stage_shape(rows, w):
    return (w // LANES, rows, LANES)


def _to_residues(val, stage_ref, out_refs, dils):
    planes, rows, _ = stage_ref.shape
    for c in range(planes):
        stage_ref[c] = val[:, c * LANES:(c + 1) * LANES]
    for out_ref, dil in zip(out_refs, dils):
        if dil == 1:
            out_ref[0] = val.astype(out_ref.dtype)
            continue
        for r in range(dil):
            for c in range(planes):
                out_ref[r, :, c * LANES:(c + 1) * LANES] = stage_ref.at[c][pl.ds(r, rows // dil, stride=dil), :].astype(
                    out_ref.dtype)


def _from_residues(src_ref, stage_ref, dil):
    planes, rows, _ = stage_ref.shape
    if dil == 1:
        return lambda c: src_ref[0, :, c * LANES:(c + 1) * LANES].astype(F32)
    for r in range(dil):
        for c in range(planes):
            stage_ref.at[c][pl.ds(r, rows // dil, stride=dil), :] = src_ref[r, :, c * LANES:(c + 1) * LANES].astype(F32)
    return lambda c: stage_ref[c]


def _matmul(name, mode, a, b, *, m, n, k, tm=MM_TM, tn=MM_TN, tk=MM_TK, a_kind="plain", b_kind="plain", outs,
            extras=(), epilogue=None, stage=False):
    tm, tn, tk = min(tm, m), min(tn, n), min(tk, k)
    if b_kind == "col" and mode in ("nn", "tn"):
        tn = min(tn, n // b.shape[0])
    if b_kind == "col" and mode == "nt":
        tk = min(tk, k // b.shape[0])
    if a_kind == "col":
        assert mode == "nt"
        tk = min(tk, k // a.shape[0])
    if any(kind == "col" for _, kind in outs):
        tn = min(tn, n // N_SHARD)
    assert m % tm == 0 and n % tn == 0 and k % tk == 0, (name, m, n, k, tm, tn, tk)
    nk = k // tk
    grid = (m // tm, n // tn, nk)
    if mode == "nn":
        a_spec = pl.BlockSpec((tm, tk), lambda i, j, kk: (i, kk))
        b_spec = _wspec(b_kind, b.shape, tk, tn, lambda i, j, kk: (kk, j))
        dims = (((1,), (0,)), ((), ()))
    elif mode == "nt":
        a_spec = _wspec(a_kind, a.shape, tm, tk, lambda i, j, kk: (i, kk))
        b_spec = _wspec(b_kind, b.shape, tn, tk, lambda i, j, kk: (j, kk))
        dims = (((1,), (1,)), ((), ()))
    else:
        a_spec = pl.BlockSpec((tk, tm), lambda i, j, kk: (kk, i))
        b_spec = _wspec(b_kind, b.shape, tk, tn, lambda i, j, kk: (kk, j))
        dims = (((0,), (0,)), ((), ()))
    out_shape, out_specs = [], []
    for dtype, kind in outs:
        if isinstance(kind, tuple):
            dil = kind[1]
            out_shape.append(jax.ShapeDtypeStruct((dil, m // dil, n), dtype))
            out_specs.append(pl.BlockSpec((dil, tm // dil, tn), lambda i, j, kk: (0, i, j)))
            continue
        shape = (m, n) if kind == "plain" else (N_SHARD, m, n // N_SHARD)
        out_shape.append(jax.ShapeDtypeStruct(shape, dtype))
        out_specs.append(_wspec(kind, shape, tm, tn, lambda i, j, kk: (i, j)))
    n_ex = len(extras)
    ex_specs = {"ij": pl.BlockSpec((tm, tn), lambda i, j, kk: (i, j)),
                "vec": pl.BlockSpec((1, tn), lambda i, j, kk: (0, j)),
                "rows": pl.BlockSpec((tm, LANES), lambda i, j, kk: (i, 0))}

    def body(*refs):
        a_ref, b_ref = refs[0], refs[1]
        ex_refs = refs[2:2 + n_ex]
        out_refs = refs[2 + n_ex:2 + n_ex + len(outs)]
        j = pl.program_id(1)

        def finish(res):
            if epilogue is None:
                out_refs[0][...] = res.astype(out_refs[0].dtype)
            elif stage:
                epilogue(res, ex_refs, out_refs, j, refs[-1])
            else:
                epilogue(res, ex_refs, out_refs, j)

        prod = lax.dot_general(a_ref[...], b_ref[...], dims, preferred_element_type=F32)
        if nk == 1:
            finish(prod)
            return
        acc_ref = refs[2 + n_ex + len(outs)]
        kk = pl.program_id(2)

        @pl.when(kk == 0)
        def _():
            acc_ref[...] = prod

        @pl.when(kk > 0)
        def _():
            acc_ref[...] += prod

        @pl.when(kk == nk - 1)
        def _():
            finish(acc_ref[...])

    res = pl.pallas_call(
        body, name=name, grid=grid,
        in_specs=[a_spec, b_spec] + [ex_specs[how] for _, how in extras],
        out_specs=out_specs, out_shape=out_shape,
        scratch_shapes=[pltpu.VMEM((tm, tn), F32)] * (nk > 1) + [pltpu.VMEM(_stage_shape(tm, tn), F32)] * bool(stage),
        compiler_params=_params("parallel", "parallel", "arbitrary"),
    )(a, b, *[e for e, _ in extras])
    return res


def _rope_tables(seq):
    half = ROT_DIM // 2
    pos = jnp.arange(seq, dtype=F32)
    inv = ROPE_THETA ** (-jnp.arange(0, ROT_DIM, 2, dtype=F32) / ROT_DIM)
    ang = pos[:, None] * inv[None, :]
    cos, sin = jnp.cos(ang), jnp.sin(ang)
    zeros = jnp.zeros((seq, HEAD_DIM - ROT_DIM), F32)
    ctab = jnp.concatenate([cos, cos, zeros + 1.0], axis=1)
    atab = jnp.concatenate([-sin, jnp.zeros((seq, half), F32), zeros], axis=1)
    btab = jnp.concatenate([jnp.zeros((seq, half), F32), sin, zeros], axis=1)
    return ctab, atab, btab


def _rope_apply(x, ctab, atab, btab, sign):
    w = x.shape[1]
    reps = w // HEAD_DIM
    half = ROT_DIM // 2
    c = jnp.tile(ctab, (1, reps))
    a = jnp.tile(atab, (1, reps))
    b = jnp.tile(btab, (1, reps))
    up = pltpu.roll(x, w - half, 1)
    down = pltpu.roll(x, half, 1)
    return x * c + sign * (up * a + down * b)


def _rows(t, w):
    return pl.BlockSpec((t, w), lambda i: (i, 0))


def _fixed(shape):
    nd = len(shape)
    return pl.BlockSpec(shape, lambda i: (0,) * nd)


def _rms_fwd(name, x, gains):
    s, d = x.shape
    t = min(ROW_TILE, s)
    ng = len(gains)

    def body(x_ref, *refs):
        xv = x_ref[...]
        r = lax.rsqrt(jnp.mean(xv * xv, axis=-1, keepdims=True) + NORM_EPS)
        xn = xv * r
        for g_ref, y_ref in zip(refs[:ng], refs[ng:]):
            y_ref[...] = (xn * g_ref[...]).astype(BF16)

    return pl.pallas_call(
        body, name=name, grid=(s // t,),
        in_specs=[_rows(t, d)] + [_fixed((1, d))] * ng,
        out_specs=[_rows(t, d)] * ng,
        out_shape=[jax.ShapeDtypeStruct((s, d), BF16)] * ng,
        compiler_params=_params("parallel"),
    )(x, *gains)


def _rms_bwd(name, x, pairs, dh_in, want_colsum=False):
    s, d = x.shape
    t = min(ROW_TILE, s)
    n_p = len(pairs)

    def body(x_ref, dh_ref, *refs):
        g_refs = refs[:n_p]
        dy_refs = refs[n_p:2 * n_p]
        dh_out, dhb_out = refs[2 * n_p], refs[2 * n_p + 1]
        dg_refs = refs[2 * n_p + 2:2 * n_p + 2 + n_p]
        cs_ref = refs[-1] if want_colsum else None
        i = pl.program_id(0)
        xv = x_ref[...]
        r = lax.rsqrt(jnp.mean(xv * xv, axis=-1, keepdims=True) + NORM_EPS)
        xn = xv * r
        dh = dh_ref[...]
        for g_ref, dy_ref, dg_ref in zip(g_refs, dy_refs, dg_refs):
            dy = dy_ref[...].astype(F32)
            u = dy * g_ref[...]
            dh = dh + r * (u - xn * jnp.mean(u * xn, axis=-1, keepdims=True))
            part = jnp.sum(dy * xn, axis=0, keepdims=True)

            @pl.when(i == 0)
            def _():
                dg_ref[...] = part

            @pl.when(i > 0)
            def _():
                dg_ref[...] += part

        dh_out[...] = dh
        dhb_out[...] = dh.astype(BF16)
        if want_colsum:
            col = jnp.sum(dh, axis=0, keepdims=True)

            @pl.when(i == 0)
            def _():
                cs_ref[...] = col

            @pl.when(i > 0)
            def _():
                cs_ref[...] += col

    n_vec = n_p + (1 if want_colsum else 0)
    return pl.pallas_call(
        body, name=name, grid=(s // t,),
        in_specs=[_rows(t, d), _rows(t, d)] + [_fixed((1, d))] * n_p + [_rows(t, d)] * n_p,
        out_specs=[_rows(t, d), _rows(t, d)] + [_fixed((1, d))] * n_vec,
        out_shape=[jax.ShapeDtypeStruct((s, d), F32), jax.ShapeDtypeStruct((s, d), BF16)]
        + [jax.ShapeDtypeStruct((1, d), F32)] * n_vec,
        compiler_params=_params("arbitrary"),
    )(x, dh_in, *[g for g, _ in pairs], *[dy for _, dy in pairs])


def _final_loss(x, g, target):
    s, d = x.shape
    t = min(ROW_TILE, s)

    def body(x_ref, g_ref, t_ref, dh_out, dhb_out, dg_ref, loss_ref):
        i = pl.program_id(0)
        xv = x_ref[...]
        gv = g_ref[...]
        r = lax.rsqrt(jnp.mean(xv * xv, axis=-1, keepdims=True) + NORM_EPS)
        xn = xv * r
        diff = xn * gv - t_ref[...]
        dy = diff / d
        u = dy * gv
        dh = r * (u - xn * jnp.mean(u * xn, axis=-1, keepdims=True))
        dh_out[...] = dh
        dhb_out[...] = dh.astype(BF16)
        dg = jnp.sum(dy * xn, axis=0, keepdims=True)
        lc = jnp.sum(0.5 * diff * dy, axis=0, keepdims=True)

        @pl.when(i == 0)
        def _():
            dg_ref[...] = dg
            loss_ref[...] = lc

        @pl.when(i > 0)
        def _():
            dg_ref[...] += dg
            loss_ref[...] += lc

    return pl.pallas_call(
        body, name="final_loss", grid=(s // t,),
        in_specs=[_rows(t, d), _fixed((1, d)), _rows(t, d)],
        out_specs=[_rows(t, d), _rows(t, d), _fixed((1, d)), _fixed((1, d))],
        out_shape=[jax.ShapeDtypeStruct((s, d), F32), jax.ShapeDtypeStruct((s, d), BF16),
                   jax.ShapeDtypeStruct((1, d), F32), jax.ShapeDtypeStruct((1, d), F32)],
        compiler_params=_params("arbitrary"),
    )(x, g, target)


def _ln_silu_fwd(c, g, b):
    s, d = c.shape
    t = min(ROW_TILE, s)

    def body(c_ref, g_ref, b_ref, s_ref):
        cv = c_ref[...]
        mu = jnp.mean(cv, axis=-1, keepdims=True)
        xc = cv - mu
        rs = lax.rsqrt(jnp.mean(xc * xc, axis=-1, keepdims=True) + LN_EPS)
        ln = xc * rs * g_ref[...] + b_ref[...]
        s_ref[...] = (ln * _sigmoid(ln)).astype(BF16)

    return pl.pallas_call(
        body, name="ln_silu_fwd", grid=(s // t,),
        in_specs=[_rows(t, d), _fixed((1, d)), _fixed((1, d))],
        out_specs=_rows(t, d), out_shape=jax.ShapeDtypeStruct((s, d), BF16),
        compiler_params=_params("parallel"),
    )(c, g, b)


def _ln_silu_bwd(c, g, b, ds):
    s, d = c.shape
    t = min(ROW_TILE, s)

    def body(c_ref, g_ref, b_ref, ds_ref, dc_ref, dg_ref, db_ref, dbdw_ref):
        i = pl.program_id(0)
        cv = c_ref[...]
        gv = g_ref[...]
        mu = jnp.mean(cv, axis=-1, keepdims=True)
        xc = cv - mu
        rs = lax.rsqrt(jnp.mean(xc * xc, axis=-1, keepdims=True) + LN_EPS)
        nrm = xc * rs
        ln = nrm * gv + b_ref[...]
        sig = _sigmoid(ln)
        dln = ds_ref[...].astype(F32) * sig * (1.0 + ln * (1.0 - sig))
        dn = dln * gv
        dc = rs * (dn - jnp.mean(dn, axis=-1, keepdims=True)
                   - nrm * jnp.mean(dn * nrm, axis=-1, keepdims=True))
        dc_ref[...] = dc
        pg = jnp.sum(dln * nrm, axis=0, keepdims=True)
        pb = jnp.sum(dln, axis=0, keepdims=True)
        pc = jnp.sum(dc, axis=0, keepdims=True)

        @pl.when(i == 0)
        def _():
            dg_ref[...] = pg
            db_ref[...] = pb
            dbdw_ref[...] = pc

        @pl.when(i > 0)
        def _():
            dg_ref[...] += pg
            db_ref[...] += pb
            dbdw_ref[...] += pc

    return pl.pallas_call(
        body, name="ln_silu_bwd", grid=(s // t,),
        in_specs=[_rows(t, d), _fixed((1, d)), _fixed((1, d)), _rows(t, d)],
        out_specs=[_rows(t, d)] + [_fixed((1, d))] * 3,
        out_shape=[jax.ShapeDtypeStruct((s, d), F32)] + [jax.ShapeDtypeStruct((1, d), F32)] * 3,
        compiler_params=_params("arbitrary"),
    )(c, g, b, ds)


def _residue_spec(dil, t, w):
    return pl.BlockSpec((dil, t // dil, w), lambda i: (0, i, 0))


def _attn_combine(o_list, lse_list):
    dil0, sd0, d = o_list[0].shape
    s = dil0 * sd0
    lw = lse_list[0].shape[2]
    group = d // HEAD_DIM // N_KV_HEADS
    t = min(ROW_TILE, s)
    nb = len(o_list)
    dils = [o.shape[0] for o in o_list]

    def body(*refs):
        o_out, l_out = refs[2 * nb], refs[2 * nb + 1]
        o_stage, l_stage = refs[2 * nb + 2:3 * nb + 2], refs[3 * nb + 2:]
        o_planes = [_from_residues(src, stage, dil) for src, stage, dil in zip(refs[:nb], o_stage, dils)]
        l_planes = [_from_residues(src, stage, dil) for src, stage, dil in zip(refs[nb:2 * nb], l_stage, dils)]
        for kh in range(N_KV_HEADS):
            ls = [plane(kh) for plane in l_planes]
            mx = ls[0]
            for l in ls[1:]:
                mx = jnp.maximum(mx, l)
            es = [jnp.exp(l - mx) for l in ls]
            den = es[0]
            for e in es[1:]:
                den = den + e
            l_out[:, kh * LANES:(kh + 1) * LANES] = mx + jnp.log(den)
            ws = [e / den for e in es]
            for g in range(group):
                h = kh * group + g
                acc = jnp.zeros((t, HEAD_DIM), F32)
                for plane, w in zip(o_planes, ws):
                    acc = acc + w[:, g:g + 1] * plane(h)
                o_out[:, h * HEAD_DIM:(h + 1) * HEAD_DIM] = acc.astype(BF16)

    return pl.pallas_call(
        body, name="attn_combine", grid=(s // t,),
        in_specs=[_residue_spec(dil, t, d) for dil in dils] + [_residue_spec(dil, t, lw) for dil in dils],
        out_specs=[_rows(t, d), _rows(t, lw)],
        out_shape=[jax.ShapeDtypeStruct((s, d), BF16), jax.ShapeDtypeStruct((s, lw), F32)],
        scratch_shapes=[pltpu.VMEM(_stage_shape(t, d), F32)] * nb + [pltpu.VMEM(_stage_shape(t, lw), F32)] * nb,
        compiler_params=_params("parallel"),
    )(*o_list, *lse_list)


def _attn_delta(do, o, lse, dils):
    s, d = o.shape
    lw = lse.shape[1]
    group = d // HEAD_DIM // N_KV_HEADS
    t = min(ROW_TILE, s)
    nd = len(dils)

    def body(do_ref, o_ref, lse_ref, *refs):
        stage = refs[-1]
        lane = lax.broadcasted_iota(I32, (t, LANES), 1)
        planes = []
        for kh in range(N_KV_HEADS):
            out = jnp.zeros((t, LANES), F32)
            for g in range(group):
                cols = slice((kh * group + g) * HEAD_DIM, (kh * group + g + 1) * HEAD_DIM)
                v = jnp.sum(do_ref[:, cols].astype(F32) * o_ref[:, cols].astype(F32), axis=-1, keepdims=True)
                out = jnp.where(lane == g, v, out)
            planes.append(out)
        _to_residues(lse_ref[...], stage, refs[:nd], dils)
        _to_residues(jnp.concatenate(planes, axis=1), stage, refs[nd:2 * nd], dils)

    res = pl.pallas_call(
        body, name="attn_delta", grid=(s // t,),
        in_specs=[_rows(t, d), _rows(t, d), _rows(t, lw)],
        out_specs=[_residue_spec(dil, t, lw) for dil in dils] * 2,
        out_shape=[jax.ShapeDtypeStruct((dil, s // dil, lw), F32) for dil in dils] * 2,
        scratch_shapes=[pltpu.VMEM(_stage_shape(t, lw), F32)],
        compiler_params=_params("parallel"),
    )(do, o, lse)
    return res[:nd], res[nd:]


def _residue_sum(name, groups, tabs):
    first = groups[0][0][0]
    s, w = first.shape[0] * first.shape[1], first.shape[2]
    t = min(ROW_TILE, s)
    flat = [p for parts, _ in groups for p in parts]

    def body(*refs):
        c_ref, a_ref, b_ref = refs[len(flat):len(flat) + 3]
        out = refs[len(flat) + 3]
        stages = refs[len(flat) + 4:]
        k = 0
        for gi, (parts, rotate) in enumerate(groups):
            planes = [_from_residues(refs[k + i], stages[k + i], p.shape[0]) for i, p in enumerate(parts)]
            k += len(parts)
            for c in range(w // LANES):
                tot = planes[0](c)
                for plane in planes[1:]:
                    tot = tot + plane(c)
                if rotate:
                    tot = _rope_apply(tot, c_ref[...], a_ref[...], b_ref[...], -1.0)
                out[:, gi * w + c * LANES:gi * w + (c + 1) * LANES] = tot.astype(BF16)

    return pl.pallas_call(
        body, name=name, grid=(s // t,),
        in_specs=[_residue_spec(p.shape[0], t, w) for p in flat] + [_rows(t, HEAD_DIM)] * 3,
        out_specs=_rows(t, len(groups) * w), out_shape=jax.ShapeDtypeStruct((s, len(groups) * w), BF16),
        scratch_shapes=[pltpu.VMEM(_stage_shape(t, w), F32) for _ in flat],
        compiler_params=_params("parallel"),
    )(*flat, *tabs)


def _dwconv_fwd(u, w_dw, b_dw):
    s, d2 = u.shape
    d = d2 // 2
    cb = min(CONV_CB, d)
    nblk = d // cb
    tt = min(CONV_T, s)

    def body(ua_ref, ug_ref, w_ref, b_ref, c_ref, xp_ref):
        gl = ua_ref[...].astype(F32) * _sigmoid(ug_ref[...].astype(F32))
        xp_ref[0:CONV_PAD, :] = jnp.zeros((CONV_PAD, cb), F32)
        xp_ref[CONV_PAD:, :] = gl
        wv = w_ref[...]
        bv = b_ref[...]
        for t0 in range(0, s, tt):
            acc = jnp.zeros((tt, cb), F32) + bv
            for kk in range(CONV_WIDTH):
                off = t0 + CONV_PAD - (CONV_WIDTH - 1) + kk
                acc = acc + wv[kk:kk + 1, :] * xp_ref[off:off + tt, :]
            c_ref[t0:t0 + tt, :] = acc

    return pl.pallas_call(
        body, name="dwconv_fwd", grid=(nblk,),
        in_specs=[pl.BlockSpec((s, cb), lambda j: (0, j)), pl.BlockSpec((s, cb), lambda j: (0, j + nblk)),
                  pl.BlockSpec((CONV_PAD, cb), lambda j: (0, j)), pl.BlockSpec((1, cb), lambda j: (0, j))],
        out_specs=pl.BlockSpec((s, cb), lambda j: (0, j)),
        out_shape=jax.ShapeDtypeStruct((s, d), F32),
        scratch_shapes=[pltpu.VMEM((s + CONV_PAD, cb), F32)],
        compiler_params=_params("parallel"),
    )(u, u, w_dw, b_dw)


def _dwconv_bwd(u, w_dw, dc):
    s, d2 = u.shape
    d = d2 // 2
    cb = min(CONV_CB, d)
    nblk = d // cb
    tt = min(CONV_T, s)

    def body(ua_ref, ug_ref, w_ref, dc_ref, du_ref, dw_ref, dba_ref, dbg_ref, glp_ref, dcp_ref, acc_ref):
        a = ua_ref[...].astype(F32)
        sig = _sigmoid(ug_ref[...].astype(F32))
        glp_ref[0:CONV_PAD, :] = jnp.zeros((CONV_PAD, cb), F32)
        glp_ref[CONV_PAD:, :] = a * sig
        dcp_ref[0:s, :] = dc_ref[...]
        dcp_ref[s:, :] = jnp.zeros((CONV_PAD, cb), F32)
        acc_ref[...] = jnp.zeros_like(acc_ref)
        wv = w_ref[...]
        dba = jnp.zeros((1, cb), F32)
        dbg = jnp.zeros((1, cb), F32)
        for t0 in range(0, s, tt):
            dgl = jnp.zeros((tt, cb), F32)
            dct = dc_ref[t0:t0 + tt, :]
            for kk in range(CONV_WIDTH):
                off = t0 + (CONV_WIDTH - 1) - kk
                dgl = dgl + wv[kk:kk + 1, :] * dcp_ref[off:off + tt, :]
                goff = t0 + CONV_PAD - (CONV_WIDTH - 1) + kk
                prod = dct * glp_ref[goff:goff + tt, :]
                acc_ref[8 * kk:8 * kk + 8, :] += jnp.sum(prod.reshape(tt // 8, 8, cb), axis=0)
            at = ua_ref[t0:t0 + tt, :].astype(F32)
            st = _sigmoid(ug_ref[t0:t0 + tt, :].astype(F32))
            da = dgl * st
            dg = dgl * at * st * (1.0 - st)
            du_ref[0, t0:t0 + tt, :] = da.astype(BF16)
            du_ref[1, t0:t0 + tt, :] = dg.astype(BF16)
            dba = dba + jnp.sum(da, axis=0, keepdims=True)
            dbg = dbg + jnp.sum(dg, axis=0, keepdims=True)
        dba_ref[...] = dba
        dbg_ref[...] = dbg
        for kk in range(CONV_WIDTH):
            dw_ref[kk:kk + 1, :] = jnp.sum(acc_ref[8 * kk:8 * kk + 8, :], axis=0, keepdims=True)
        dw_ref[CONV_WIDTH:, :] = jnp.zeros((CONV_PAD - CONV_WIDTH, cb), F32)

    blk = pl.BlockSpec((s, cb), lambda j: (0, j))
    vec = pl.BlockSpec((1, cb), lambda j: (0, j))
    return pl.pallas_call(
        body, name="dwconv_bwd", grid=(nblk,),
        in_specs=[blk, pl.BlockSpec((s, cb), lambda j: (0, j + nblk)),
                  pl.BlockSpec((CONV_PAD, cb), lambda j: (0, j)), blk],
        out_specs=[pl.BlockSpec((2, s, cb), lambda j: (0, 0, j)), pl.BlockSpec((CONV_PAD, cb), lambda j: (0, j)),
                   vec, vec],
        out_shape=[jax.ShapeDtypeStruct((2, s, d), BF16), jax.ShapeDtypeStruct((CONV_PAD, d), F32),
                   jax.ShapeDtypeStruct((1, d), F32), jax.ShapeDtypeStruct((1, d), F32)],
        scratch_shapes=[pltpu.VMEM((s + CONV_PAD, cb), F32), pltpu.VMEM((s + CONV_PAD, cb), F32),
                        pltpu.VMEM((8 * CONV_PAD, cb), F32)],
        compiler_params=_params("parallel"),
    )(u, u, w_dw, dc)


def _stack_heads(x, group):
    return jnp.concatenate([x[:, g * HEAD_DIM:(g + 1) * HEAD_DIM] for g in range(group)], axis=0)


def _unstack_heads(x, group):
    return jnp.concatenate([x[g * ATT_BLOCK:(g + 1) * ATT_BLOCK, :] for g in range(group)], axis=1)


def _stack_cols(x, group):
    return jnp.concatenate([x[:, g:g + 1] for g in range(group)], axis=0)


def _band_bias(group):
    rows = group * ATT_BLOCK
    row = lax.broadcasted_iota(I32, (rows, 2 * ATT_BLOCK), 0) % ATT_BLOCK
    col = lax.broadcasted_iota(I32, (rows, 2 * ATT_BLOCK), 1)
    band = jnp.where((col >= row) & (col <= row + ATT_BLOCK), 0.0, -jnp.inf).astype(F32)
    first = jnp.where(lax.broadcasted_iota(I32, (1, 2 * ATT_BLOCK), 1) >= ATT_BLOCK, 0.0, -jnp.inf).astype(F32)
    return band, first


def _masked_scores(qs, kw, band_ref, first_ref, nb, scale):
    sc = lax.dot_general(qs, kw, (((1,), (1,)), ((), ())), preferred_element_type=F32) * scale + band_ref[...]
    return sc + jnp.where(nb > 0, 0.0, first_ref[...])


def _window(ref, nb):
    prev = pl.multiple_of(jnp.maximum(nb - 1, 0) * ATT_BLOCK, ATT_BLOCK)
    cur = pl.multiple_of(nb * ATT_BLOCK, ATT_BLOCK)
    return jnp.concatenate([ref[pl.ds(prev, ATT_BLOCK), :], ref[pl.ds(cur, ATT_BLOCK), :]], axis=0)


def _residues_per_step(dil, nblk):
    return max(1, min(dil, ATT_STEP_BLOCKS // nblk))


def _attn_fwd(name, q, kv):
    dil, sd, d = q.shape
    group = d // HEAD_DIM // N_KV_HEADS
    gw = group * HEAD_DIM
    nblk = sd // ATT_BLOCK
    scale = 1.0 / math.sqrt(HEAD_DIM)
    nt = (((1,), (1,)), ((), ()))

    rb = _residues_per_step(dil, nblk)

    def body(q_all, k_all, v_all, band_ref, first_ref, o_all, lse_all):
        lane = lax.broadcasted_iota(I32, (ATT_BLOCK, LANES), 1)
        for rr in range(rb):
            q_ref, k_ref, v_ref, o_ref, lse_ref = [ref.at[rr] for ref in (q_all, k_all, v_all, o_all, lse_all)]

            def step(nb, carry):
                rows = pl.ds(pl.multiple_of(nb * ATT_BLOCK, ATT_BLOCK), ATT_BLOCK)
                qs = _stack_heads(q_ref[rows, :], group)
                kw = _window(k_ref, nb)
                vw = _window(v_ref, nb)
                sc = _masked_scores(qs, kw, band_ref, first_ref, nb, scale)
                mx = jnp.max(sc, axis=-1, keepdims=True)
                p = jnp.exp(sc - mx)
                l = jnp.sum(p, axis=-1, keepdims=True)
                o = jnp.dot(p.astype(BF16), vw, preferred_element_type=F32) / l
                o_ref[rows, :] = _unstack_heads(o, group).astype(BF16)
                lse = mx + jnp.log(l)
                out = jnp.zeros((ATT_BLOCK, LANES), F32)
                for g in range(group):
                    out = jnp.where(lane == g, lse[g * ATT_BLOCK:(g + 1) * ATT_BLOCK, :], out)
                lse_ref[rows, :] = out
                return carry

            lax.fori_loop(0, nblk, step, 0, unroll=min(2, nblk))

    kvh = N_KV_HEADS
    band, first = _band_bias(group)
    qspec = pl.BlockSpec((rb, sd, gw), lambda r, h: (r, 0, h))
    kspec = pl.BlockSpec((rb, sd, HEAD_DIM), lambda r, h: (r, 0, h))
    return pl.pallas_call(
        body, name=name, grid=(dil // rb, kvh),
        in_specs=[qspec, kspec, pl.BlockSpec((rb, sd, HEAD_DIM), lambda r, h: (r, 0, kvh + h)),
                  pl.BlockSpec(band.shape, lambda r, h: (0, 0)), pl.BlockSpec(first.shape, lambda r, h: (0, 0))],
        out_specs=[qspec, kspec],
        out_shape=[jax.ShapeDtypeStruct((dil, sd, d), BF16),
                   jax.ShapeDtypeStruct((dil, sd, kvh * LANES), F32)],
        compiler_params=_params("parallel", "parallel"),
    )(q, kv, kv, band, first)


def _attn_bwd(name, q, kv, do, lse, delta):
    dil, sd, d = q.shape
    group = d // HEAD_DIM // N_KV_HEADS
    gw = group * HEAD_DIM
    nblk = sd // ATT_BLOCK
    scale = 1.0 / math.sqrt(HEAD_DIM)
    nt = (((1,), (1,)), ((), ()))
    tn = (((0,), (0,)), ((), ()))

    rb = _residues_per_step(dil, nblk)

    def body(q_all, k_all, v_all, do_all, lse_all, dl_all, band_ref, first_ref, dq_all, dk_all, dv_all, dk_accs,
             dv_accs):
        dk_accs[...] = jnp.zeros_like(dk_accs)
        dv_accs[...] = jnp.zeros_like(dv_accs)
        for rr in range(rb):
            q_ref, k_ref, v_ref, do_ref, lse_ref, dl_ref, dq_ref, dk_ref, dv_ref, dk_acc, dv_acc = [
                ref.at[rr] for ref in (q_all, k_all, v_all, do_all, lse_all, dl_all, dq_all, dk_all, dv_all,
                                       dk_accs, dv_accs)]

            def step(nb, carry):
                rows = pl.ds(pl.multiple_of(nb * ATT_BLOCK, ATT_BLOCK), ATT_BLOCK)
                qs = _stack_heads(q_ref[rows, :], group)
                dos = _stack_heads(do_ref[rows, :], group)
                ls = _stack_cols(lse_ref[rows, :], group)
                dl = _stack_cols(dl_ref[rows, :], group)
                kw = _window(k_ref, nb)
                vw = _window(v_ref, nb)
                p = jnp.exp(_masked_scores(qs, kw, band_ref, first_ref, nb, scale) - ls)
                dp = lax.dot_general(dos, vw, nt, preferred_element_type=F32)
                ds = (p * (dp - dl) * scale).astype(BF16)
                dq = jnp.dot(ds, kw, preferred_element_type=F32)
                dq_ref[rows, :] = _unstack_heads(dq, group).astype(BF16)
                win = pl.ds(pl.multiple_of(nb * ATT_BLOCK, ATT_BLOCK), 2 * ATT_BLOCK)
                dk_acc[win, :] += lax.dot_general(ds, qs, tn, preferred_element_type=F32)
                dv_acc[win, :] += lax.dot_general(p.astype(BF16), dos, tn, preferred_element_type=F32)
                return carry

            lax.fori_loop(0, nblk, step, 0, unroll=min(2, nblk))
            dk_ref[...] = dk_acc[ATT_BLOCK:, :]
            dv_ref[...] = dv_acc[ATT_BLOCK:, :]

    kvh = N_KV_HEADS
    band, first = _band_bias(group)
    qspec = pl.BlockSpec((rb, sd, gw), lambda r, h: (r, 0, h))
    kspec = pl.BlockSpec((rb, sd, HEAD_DIM), lambda r, h: (r, 0, h))
    return pl.pallas_call(
        body, name=name, grid=(dil // rb, kvh),
        in_specs=[qspec, kspec, pl.BlockSpec((rb, sd, HEAD_DIM), lambda r, h: (r, 0, kvh + h)),
                  qspec, kspec, kspec,
                  pl.BlockSpec(band.shape, lambda r, h: (0, 0)), pl.BlockSpec(first.shape, lambda r, h: (0, 0))],
        out_specs=[qspec, kspec, kspec],
        out_shape=[jax.ShapeDtypeStruct((dil, sd, d), BF16),
                   jax.ShapeDtypeStruct((dil, sd, kvh * HEAD_DIM), F32),
                   jax.ShapeDtypeStruct((dil, sd, kvh * HEAD_DIM), F32)],
        scratch_shapes=[pltpu.VMEM((rb, sd + ATT_BLOCK, HEAD_DIM), F32)] * 2,
        compiler_params=_params("parallel", "parallel"),
    )(q, kv, kv, do, lse, delta, band, first)


def _cast_bf16(name, w, layer, place, after=None):
    _, r, c = w.shape
    tr = min(512, r)
    deps = [] if after is None else [after]

    def body(pl_ref, w_ref, *refs):
        refs[-1][...] = w_ref[...].astype(BF16)

    return pl.pallas_call(
        body, name=name,
        grid_spec=pltpu.PrefetchScalarGridSpec(
            num_scalar_prefetch=1, grid=(r // tr,),
            in_specs=[pl.BlockSpec((None, tr, c), lambda i, p: (layer, i, 0))] + [ANY] * len(deps),
            out_specs=pl.BlockSpec((None, tr, c), lambda i, p: (p[1], i, 0))),
        out_shape=jax.ShapeDtypeStruct((N_SHARD, r, c), BF16),
        compiler_params=_params("parallel"),
    )(place, w, *deps)


def _chip_sum(name, g, rh, place):
    _, r, c = g.shape
    rh2 = r // 2
    tr = min(512, rh2)
    nb = rh2 // tr

    def body(pl_ref, g_ref, rh_ref, o_ref):
        o_ref[...] = (g_ref[...].astype(F32) + rh_ref[...].astype(F32)).astype(BF16)

    return pl.pallas_call(
        body, name=name,
        grid_spec=pltpu.PrefetchScalarGridSpec(
            num_scalar_prefetch=1, grid=(N_SHARD, nb),
            in_specs=[pl.BlockSpec((None, tr, c), lambda s, i, p: (s, p[0] * nb + i, 0)),
                      pl.BlockSpec((None, tr, c), lambda s, i, p: (s, i, 0))],
            out_specs=pl.BlockSpec((None, tr, c), lambda s, i, p: (s, i, 0))),
        out_shape=jax.ShapeDtypeStruct((N_SHARD, rh2, c), BF16),
        compiler_params=_params("parallel", "parallel"),
    )(place, g, rh)


def _owner_sum(name, cs, rp, place):
    _, rh2, c = cs.shape
    tr = min(512, rh2)
    nb = rh2 // tr

    def body(pl_ref, cs_ref, r0_ref, r1_ref, r2_ref, o_ref):
        o_ref[...] = ((cs_ref[...].astype(F32) + r0_ref[...].astype(F32))
                      + (r1_ref[...].astype(F32) + r2_ref[...].astype(F32)))

    def rspec(j):
        return pl.BlockSpec((None, tr, c), lambda i, p: (j, i, 0))

    return pl.pallas_call(
        body, name=name,
        grid_spec=pltpu.PrefetchScalarGridSpec(
            num_scalar_prefetch=1, grid=(nb,),
            in_specs=[pl.BlockSpec((None, tr, c), lambda i, p: (p[1], i, 0)), rspec(0), rspec(1), rspec(2)],
            out_specs=pl.BlockSpec((tr, c), lambda i, p: (p[0] * nb + i, 0))),
        out_shape=jax.ShapeDtypeStruct((2 * rh2, c), F32),
        compiler_params=_params("parallel"),
    )(place, cs, rp, rp, rp)


def _adam_math(w, g, m, v):
    m = ADAM_B1 * m + (1.0 - ADAM_B1) * g
    v = ADAM_B2 * v + (1.0 - ADAM_B2) * (g * g)
    m_hat = m / (1.0 - ADAM_B1 ** ADAM_STEP)
    v_hat = v / (1.0 - ADAM_B2 ** ADAM_STEP)
    delta = -ADAM_LR * (m_hat / (jnp.sqrt(v_hat) + ADAM_EPS) + ADAM_WD * w)
    return delta, m, v


def _adamw(name, w, m, v, g, layer, partial=None):
    nl, r, c = w.shape
    tr = min(256, r)

    def body(w_ref, m_ref, v_ref, g_ref, *refs):
        go_ref, d_ref, mo_ref, vo_ref = refs[-4:]
        gv = g_ref[...]
        delta, m_new, v_new = _adam_math(w_ref[...], gv, m_ref[...], v_ref[...])
        go_ref[...] = gv
        d_ref[...] = delta
        mo_ref[...] = m_new
        vo_ref[...] = v_new

    wspec = pl.BlockSpec((None, tr, c), lambda i: (layer, i, 0))
    prev = [] if partial is None else list(partial)
    return pl.pallas_call(
        body, name=name, grid=(r // tr,),
        in_specs=[wspec] * 3 + [pl.BlockSpec((tr, c), lambda i: (i, 0))] + [ANY] * len(prev),
        out_specs=[wspec] * 4,
        out_shape=[jax.ShapeDtypeStruct((nl, r, c), F32)] * 4,
        input_output_aliases={4 + i: i for i in range(len(prev))},
        compiler_params=_params("parallel"),
    )(w, m, v, g, *prev)


def _adam_small(ws, ms, vs, gs):
    n = len(ws)

    def body(*refs):
        w_refs, m_refs, v_refs, g_refs = refs[:n], refs[n:2 * n], refs[2 * n:3 * n], refs[3 * n:4 * n]
        d_refs, mo_refs, vo_refs = refs[4 * n:5 * n], refs[5 * n:6 * n], refs[6 * n:7 * n]
        for i in range(n):
            delta, m_new, v_new = _adam_math(w_refs[i][...], g_refs[i][...], m_refs[i][...], v_refs[i][...])
            d_refs[i][...] = delta
            mo_refs[i][...] = m_new
            vo_refs[i][...] = v_new

    shapes = [jax.ShapeDtypeStruct(w.shape, F32) for w in ws]
    res = pl.pallas_call(body, name="adam_small", out_shape=shapes * 3)(*ws, *ms, *vs, *gs)
    return res[:n], res[n:2 * n], res[2 * n:]


def _pack_small(b_in, w_dw, b_dw, ln_g, ln_b, b_out, place):
    cin = b_in.shape[1]
    cd = b_dw.shape[1]
    rows = 8 + CONV_PAD

    def body(pl_ref, bi, wd, bd, lg, lb, bo, out):
        out[...] = jnp.zeros_like(out)
        out[0:1, :] = bi[...]
        out[1:2, 0:cd] = bd[...]
        out[1:2, cd:2 * cd] = lg[...]
        out[2:3, 0:cd] = lb[...]
        out[2:3, cd:2 * cd] = bo[...]
        out[8:8 + CONV_WIDTH, 0:cd] = wd[...]

    def whole(arr):
        return pl.BlockSpec(arr.shape, lambda i, p: (0,) * arr.ndim)

    ins = [b_in, w_dw, b_dw, ln_g, ln_b, b_out]
    return pl.pallas_call(
        body, name="pack_small",
        grid_spec=pltpu.PrefetchScalarGridSpec(
            num_scalar_prefetch=1, grid=(1,), in_specs=[whole(a) for a in ins],
            out_specs=pl.BlockSpec((None, rows, cin), lambda i, p: (p[1], 0, 0))),
        out_shape=jax.ShapeDtypeStruct((N_SHARD, rows, cin), F32),
        compiler_params=_params("arbitrary"),
    )(place, *ins)


def _place():
    x, y, c = lax.axis_index("x"), lax.axis_index("y"), lax.axis_index("c")
    return x, y, c


def _other_chips(x, y):
    return [(1 - x, y), (x, 1 - y), (1 - x, 1 - y)]


def _split_start(name, bufs, n_sem, copies, after=None):
    n = len(bufs)
    deps = [] if after is None else [after]

    def body(*refs):
        out0 = n + len(deps)
        for cp in copies(refs[:n], refs[out0], refs[out0 + 1], False):
            cp.start()
        refs[-1][...] = jnp.zeros_like(refs[-1])

    res = pl.pallas_call(
        body, name=name,
        out_shape=(pltpu.SemaphoreType.DMA((n_sem,)), pltpu.SemaphoreType.DMA((n_sem,)),
                   *[pltpu.HBM(b.shape, b.dtype) for b in bufs], jax.ShapeDtypeStruct((8, LANES), F32)),
        in_specs=[HBM] * n + [ANY] * len(deps),
        out_specs=(SEM, SEM, *[HBM] * n, pl.BlockSpec(memory_space=pltpu.VMEM)),
        input_output_aliases={i: 2 + i for i in range(n)},
        compiler_params=pltpu.CompilerParams(has_side_effects=SPLIT_EFFECT),
    )(*[pltpu.with_memory_space_constraint(b, pltpu.HBM) for b in bufs], *deps)
    return res[0], res[1], list(res[2:2 + n]), res[-1]


def _split_wait(name, handle, copies, after):
    ssem, rsem, bufs, _ = handle
    n = len(bufs)
    deps = list(after) if isinstance(after, (list, tuple)) else [after]

    def body(*refs):
        for cp in copies(refs[:n], refs[n], refs[n + 1], True):
            cp.wait_send()
            cp.wait_recv()

    res = pl.pallas_call(
        body, name=name,
        out_shape=[pltpu.HBM(b.shape, b.dtype) for b in bufs],
        in_specs=[HBM] * n + [SEM, SEM] + [ANY] * len(deps), out_specs=[HBM] * n,
        input_output_aliases={i: i for i in range(n)},
        compiler_params=pltpu.CompilerParams(has_side_effects=SPLIT_EFFECT),
    )(*bufs, ssem, rsem, *deps)
    return list(res)


def _remote(src, dst, ssem, rsem, k, to):
    return pltpu.make_async_remote_copy(src_ref=src, dst_ref=dst, send_sem=ssem.at[k], recv_sem=rsem.at[k],
                                        device_id=to, device_id_type=MESH)


def _gather_chips(x, y, c):
    nx, ny = x + (1 - c) - 2 * x * (1 - c), y + c - 2 * y * c
    fx, fy = x + c - 2 * x * c, y + (1 - c) - 2 * y * (1 - c)
    return (nx, ny), (fx, fy), 2 * nx + ny, 2 * fx + fy, 2 * (1 - x) + (1 - y)


def _direct_copies(refs, ssem, rsem, landing, n_whole=0):
    x, y, c = _place()
    me = 2 * x + y
    (nx, ny), _, near, _, _ = _gather_chips(x, y, c)
    n = len(refs) - n_whole
    cps = []
    for a, ref in enumerate(refs[:n]):
        cps.append(_remote(ref.at[me], ref.at[near if landing else me], ssem, rsem, a, (nx, ny, c)))
    for b, ref in enumerate(refs[n:]):
        for j, (px, py) in enumerate(_other_chips(x, y)):
            cps.append(_remote(ref.at[me], ref.at[2 * px + py if landing else me], ssem, rsem, n + 3 * b + j,
                               (px, py, c)))
    return cps


def _relay_copies(refs, ssem, rsem, landing):
    x, y, c = _place()
    _, (fx, fy), near, far, diag = _gather_chips(x, y, c)
    n = len(refs)
    cps = []
    for a, ref in enumerate(refs):
        rh = ref.shape[1] // 2
        rows = pl.ds(c * rh, rh)
        cps.append(_remote(ref.at[near, rows], ref.at[diag if landing else near, rows], ssem, rsem, a, (fx, fy, c)))
        cps.append(_remote(ref.at[near], ref.at[far if landing else near], ssem, rsem, n + a, (x, y, 1 - c)))
    return cps


def _diagonal_copies(refs, ssem, rsem, landing):
    x, y, c = _place()
    diag = 2 * (1 - x) + (1 - y)
    who = 1 - c if landing else c
    cps = []
    for a, ref in enumerate(refs):
        rh = ref.shape[1] // 2
        piece = ref.at[diag, pl.ds(who * rh, rh)]
        cps.append(_remote(piece, piece, ssem, rsem, a, (x, y, 1 - c)))
    return cps


def _sibling_copies(refs, ssem, rsem, landing):
    x, y, c = _place()
    n = len(refs) // 2
    cps = []
    for a in range(n):
        rh = refs[a].shape[1] // 2
        cps.append(_remote(refs[a].at[:, pl.ds((1 - c) * rh, rh), :], refs[n + a], ssem, rsem, a, (x, y, 1 - c)))
    return cps


def _owner_copies(refs, ssem, rsem, landing):
    x, y, c = _place()
    n = len(refs) // 2
    cps = []
    for a in range(n):
        for j, (px, py) in enumerate(_other_chips(x, y)):
            cps.append(_remote(refs[a].at[2 * px + py], refs[n + a].at[j], ssem, rsem, 3 * a + j, (px, py, c)))
    return cps


def _swap_copies(refs, ssem, rsem, landing):
    x, y, c = _place()
    who = 1 - c if landing else c
    cps = []
    for a, ref in enumerate(refs):
        rh = ref.shape[0] // 2
        rows = ref.at[pl.ds(who * rh, rh)]
        cps.append(_remote(rows, rows, ssem, rsem, a, (x, y, 1 - c)))
    return cps


def _small_copies(refs, ssem, rsem, landing):
    pack, slots = refs
    x, y, c = _place()
    cps = []
    for rel in range(1, N_DEV):
        px = 1 - x if (rel >> 2) & 1 else x
        py = 1 - y if (rel >> 1) & 1 else y
        pc = 1 - c if rel & 1 else c
        slot = 4 * px + 2 * py + pc if landing else 4 * x + 2 * y + c
        cps.append(_remote(pack, slots.at[slot], ssem, rsem, rel - 1, (px, py, pc)))
    return cps


def _small_pack(rows, w_dw_grad, d):
    n = len(rows)

    def body(*refs):
        pack = refs[-1]
        pack[...] = jnp.zeros_like(pack)
        for (r, _), ref in zip(rows, refs[:n]):
            pack[r:r + 1, :] = ref[...]
        pack[16:16 + CONV_PAD, :] = refs[n][...]

    return pl.pallas_call(body, name="small_pack", out_shape=jax.ShapeDtypeStruct((SMALL_ROWS, d), F32))(
        *[v for _, v in rows], w_dw_grad)


def _small_sum(pack, slots, place):
    rows, d = pack.shape
    loss_row = 12

    def body(pl_ref, pack_ref, slots_ref, out_ref):
        me = pl_ref[2]
        tot = jnp.where(me == 0, pack_ref[...], slots_ref[0])
        for i in range(1, N_DEV):
            tot = tot + jnp.where(me == i, pack_ref[...], slots_ref[i])
        out_ref[...] = tot
        out_ref[loss_row:loss_row + 1, :] = jnp.zeros((1, d), F32) + jnp.sum(tot[loss_row:loss_row + 1, :])

    return pl.pallas_call(
        body, name="small_sum",
        grid_spec=pltpu.PrefetchScalarGridSpec(
            num_scalar_prefetch=1, grid=(1,),
            in_specs=[pl.BlockSpec((rows, d), lambda i, p: (0, 0)), pl.BlockSpec((N_DEV, rows, d), lambda i, p: (0, 0, 0))],
            out_specs=pl.BlockSpec((rows, d), lambda i, p: (0, 0))),
        out_shape=jax.ShapeDtypeStruct((rows, d), F32),
        compiler_params=_params("arbitrary"),
    )(place, pack, slots)


def kernel(x, norm_mix, norm_mlp, conv_w_in, conv_b_in, conv_w_dw, conv_b_dw, conv_ln_g, conv_ln_b, conv_w_out, conv_b_out, kv_norm, w_kv, attn_w_q, attn_w_o, mlp_w_in, mlp_w_out, final_norm, loss_target, m_norm_mix, m_norm_mlp, m_conv_w_in, m_conv_b_in, m_conv_w_dw, m_conv_b_dw, m_conv_ln_g, m_conv_ln_b, m_conv_w_out, m_conv_b_out, m_kv_norm, m_w_kv, m_attn_w_q, m_attn_w_o, m_mlp_w_in, m_mlp_w_out, m_final_norm, v_norm_mix, v_norm_mlp, v_conv_w_in, v_conv_b_in, v_conv_w_dw, v_conv_b_dw, v_conv_ln_g, v_conv_ln_b, v_conv_w_out, v_conv_b_out, v_kv_norm, v_w_kv, v_attn_w_q, v_attn_w_o, v_mlp_w_in, v_mlp_w_out, v_final_norm):
    _, s, d = x.shape
    dff = mlp_w_in.shape[2] * N_SHARD
    kvw = w_kv.shape[1]
    nh = d // HEAD_DIM
    group = nh // N_KV_HEADS
    ds4 = d // N_SHARD
    xi, yi, ci = _place()
    me = 2 * xi + yi
    place = jnp.stack([ci, me, 2 * me + ci]).astype(I32)

    h0 = x.reshape(s, d)
    target = loss_target.reshape(s, d)
    tabs = _rope_tables(s)

    def gather_begin(tag, bufs, n_whole=0):
        plan = functools.partial(_direct_copies, n_whole=n_whole)
        return _split_start(f"gather_start_{tag}", bufs, len(bufs) + 2 * n_whole, plan), plan, n_whole

    def gather_land(tag, begun, later):
        handle, plan, n_whole = begun
        bufs = _split_wait(f"gather_wait_{tag}", handle, plan, later)
        n = len(bufs) - n_whole
        return _split_start(f"relay_start_{tag}", bufs[:n], 2 * n, _relay_copies), bufs[n:]

    def gather_swap(tag, landed, later):
        relayed, whole = landed
        bufs = _split_wait(f"relay_wait_{tag}", relayed, _relay_copies, later)
        return _split_start(f"diagonal_start_{tag}", bufs, len(bufs), _diagonal_copies), whole

    def gather_end(tag, swapped, later):
        handle, whole = swapped
        return _split_wait(f"diagonal_wait_{tag}", handle, _diagonal_copies, later) + whole

    def tied(vec, begun):
        return vec + begun[0][3][0:1, 0:1]

    ag_cin = gather_begin("conv_in", [
        _cast_bf16("cast_w_in", conv_w_in, 0, place),
        _pack_small(conv_b_in, conv_w_dw.reshape(CONV_WIDTH, ds4), conv_b_dw, conv_ln_g, conv_ln_b, conv_b_out, place),
    ], n_whole=1)
    ag_cout = gather_begin("conv_out", [_cast_bf16("cast_w_out", conv_w_out, 0, place, ag_cin[0][3])])
    ag_mi0 = gather_begin("mlp_in0", [_cast_bf16("cast_mlp_in0", mlp_w_in, 0, place, ag_cout[0][3])])
    ag_mo0 = gather_begin("mlp_out0", [_cast_bf16("cast_mlp_out0", mlp_w_out, 0, place, ag_mi0[0][3])])
    nm = [norm_mix[0:1], norm_mix[1:2]]
    nmlp = [norm_mlp[0:1], norm_mlp[1:2]]
    kvn = kv_norm.reshape(1, d)
    fin = final_norm.reshape(1, d)
    (y0,) = _rms_fwd("rms_mix0", h0, [tied(nm[0], ag_mo0)])
    land_cin = gather_land("conv_in", ag_cin, y0)
    ag_attn = gather_begin("attn", [
        _cast_bf16("cast_w_kv", w_kv.reshape(1, ds4, kvw), 0, place, land_cin[0][3]),
        _cast_bf16("cast_w_q", attn_w_q, 0, place), _cast_bf16("cast_w_o", attn_w_o, 0, place)])
    ag_mi1 = gather_begin("mlp_in1", [_cast_bf16("cast_mlp_in1", mlp_w_in, 1, place, ag_attn[0][3])])
    ag_mo1 = gather_begin("mlp_out1", [_cast_bf16("cast_mlp_out1", mlp_w_out, 1, place, ag_mi1[0][3])])
    land_cout = gather_land("conv_out", ag_cout, ag_mo1[0][3])

    wmi_g = [None, None]
    wmo_f = [None, None]

    swap_cin = gather_swap("conv_in", land_cin, land_cout[0][3])
    w_in_g, small_g = gather_end("conv_in", swap_cin, swap_cin[0][3])
    b_in_f = small_g[:, 0, :].reshape(1, 2 * d)
    b_dw_f = small_g[:, 1, 0:ds4].reshape(1, d)
    ln_g_f = small_g[:, 1, ds4:2 * ds4].reshape(1, d)
    ln_b_f = small_g[:, 2, 0:ds4].reshape(1, d)
    b_out_f = small_g[:, 2, ds4:2 * ds4].reshape(1, d)
    w_dw_f = jnp.transpose(small_g[:, 8:8 + CONV_PAD, 0:ds4], (1, 0, 2)).reshape(CONV_PAD, d)

    def ep_bias(acc, ex, outs, j):
        outs[0][...] = (acc + ex[0][...]).astype(outs[0].dtype)

    def ep_residual(acc, ex, outs, j):
        outs[0][...] = ex[0][...] + acc

    def ep_residual_bias(acc, ex, outs, j):
        outs[0][...] = ex[0][...] + (acc + ex[1][...])

    def ep_relu2(acc, ex, outs, j):
        r = jnp.maximum(acc, 0.0)
        outs[0][...] = r.astype(BF16)
        outs[1][...] = (r * r).astype(BF16)

    by_residue = [(BF16, ("residues", dil)) for dil in DILATIONS]

    def put_by_residue(val, outs, stage):
        _to_residues(val, stage, outs, DILATIONS)

    def ep_rope(acc, ex, outs, j, stage):
        put_by_residue(_rope_apply(acc, ex[0][...], ex[1][...], ex[2][...], 1.0), outs, stage)

    def ep_rope_k(acc, ex, outs, j, stage):
        roped = _rope_apply(acc, ex[0][...], ex[1][...], ex[2][...], 1.0)
        put_by_residue(jnp.where(j == 0, roped, acc), outs, stage)

    def ep_by_residue(acc, ex, outs, j, stage):
        put_by_residue(acc, outs, stage)

    tab_extras = [(t, "rows") for t in tabs]

    def mlp_fwd(idx, h, y, out_weight):
        r, r2 = _matmul(f"mlp_in{idx}", "nn", y, wmi_g[idx], b_kind="col", m=s, n=dff, k=d,
                        outs=[(BF16, "plain"), (BF16, "plain")], epilogue=ep_relu2)
        wmo_f[idx] = out_weight(r2).reshape(dff, d)
        (h_new,) = _matmul(f"mlp_out{idx}", "nn", r2, wmo_f[idx], m=s, n=d, k=dff, tm=MM_TM // 2, tk=2 * MM_TK,
                           outs=[(F32, "plain")], extras=[(h, "ij")], epilogue=ep_residual)
        return h_new, r, r2

    (u,) = _matmul("conv_in", "nn", y0, w_in_g, b_kind="col", m=s, n=2 * d, k=d,
                   outs=[(BF16, "plain")], extras=[(b_in_f, "vec")], epilogue=ep_bias)
    land_mi0 = gather_land("mlp_in0", ag_mi0, u)
    swap_cout = gather_swap("conv_out", land_cout, land_mi0[0][3])
    cpre = _dwconv_fwd(u, w_dw_f, tied(b_dw_f, swap_cout))
    sact = _ln_silu_fwd(cpre, ln_g_f, ln_b_f)
    (w_out_g,) = gather_end("conv_out", swap_cout, sact)
    w_out_f = w_out_g.reshape(d, d)
    (h1,) = _matmul("conv_out", "nn", sact, w_out_f, m=s, n=d, k=d,
                    outs=[(F32, "plain")], extras=[(h0, "ij"), (b_out_f, "vec")], epilogue=ep_residual_bias)
    swap_mi0 = gather_swap("mlp_in0", land_mi0, h1)
    (y1,) = _rms_fwd("rms_mlp0", h1, [tied(nmlp[0], swap_mi0)])
    land_mo0 = gather_land("mlp_out0", ag_mo0, y1)
    (wmi_g[0],) = gather_end("mlp_in0", swap_mi0, land_mo0[0][3])
    land_attn = None

    def out_weight0(r2):
        nonlocal land_attn
        land_attn = gather_land("attn", ag_attn, r2)
        swap_mo0 = gather_swap("mlp_out0", land_mo0, land_attn[0][3])
        return gather_end("mlp_out0", swap_mo0, swap_mo0[0][3])[0]

    h2, r0, r0sq = mlp_fwd(0, h1, y1, out_weight0)
    swap_attn = gather_swap("attn", land_attn, h2)
    land_mi1 = gather_land("mlp_in1", ag_mi1, swap_attn[0][3])
    ykv, y2 = _rms_fwd("rms_kv_mix1", h2, [tied(kvn, land_mi1), nm[1]])
    wkv_g, wq_g, wo_g = gather_end("attn", swap_attn, y2)
    wkv_f, wq_f, wo_f = wkv_g.reshape(d, kvw), wq_g.reshape(d, d), wo_g.reshape(d, d)
    kv_parts = _matmul("kv_proj", "nn", ykv, wkv_f, m=s, n=kvw, k=d, tn=kvw // 2,
                       outs=by_residue, extras=tab_extras, epilogue=ep_rope_k, stage=True)
    q_parts = _matmul("q_proj", "nn", y2, wq_f, m=s, n=d, k=d,
                      outs=by_residue, extras=tab_extras, epilogue=ep_rope, stage=True)
    o_parts, lse_parts = [], []
    for dil, q_b, kv_b in zip(DILATIONS, q_parts, kv_parts):
        o_b, lse_b = _attn_fwd(f"attn_fwd_d{dil}", q_b, kv_b)
        o_parts.append(o_b)
        lse_parts.append(lse_b)
    swap_mi1 = gather_swap("mlp_in1", land_mi1, o_parts)
    o, lse = _attn_combine(o_parts, lse_parts)
    land_mo1 = gather_land("mlp_out1", ag_mo1, o)
    (h3,) = _matmul("attn_out", "nn", o, wo_f, m=s, n=d, k=d,
                    outs=[(F32, "plain")], extras=[(h2, "ij")], epilogue=ep_residual)
    (y3,) = _rms_fwd("rms_mlp1", h3, [tied(nmlp[1], land_mo1)])
    (wmi_g[1],) = gather_end("mlp_in1", swap_mi1, y3)

    def out_weight1(r2):
        swap_mo1 = gather_swap("mlp_out1", land_mo1, r2)
        return gather_end("mlp_out1", swap_mo1, swap_mo1[0][3])[0]

    h4, r1, r1sq = mlp_fwd(1, h3, y3, out_weight1)
    dh4, dh4b, d_fin, loss_cols = _final_loss(h4, fin, target)

    def ep_relu2_bwd(acc, ex, outs, j):
        outs[0][...] = (acc * (2.0 * ex[0][...].astype(F32))).astype(BF16)

    def mlp_bwd(idx, dhb, y, r, r2):
        (dz,) = _matmul(f"mlp_out{idx}_dx", "nt", dhb, wmo_f[idx], m=s, n=dff, k=d,
                        outs=[(BF16, "plain")], extras=[(r, "ij")], epilogue=ep_relu2_bwd)
        (dwo,) = _matmul(f"mlp_out{idx}_dw", "tn", r2, dhb, m=dff, n=d, k=s,
                         outs=[(BF16, "plain")])
        (dy,) = _matmul(f"mlp_in{idx}_dx", "nt", dz, wmi_g[idx], b_kind="col", m=s, n=d, k=dff,
                        outs=[(BF16, "plain")])
        (dwi,) = _matmul(f"mlp_in{idx}_dw", "tn", y, dz, m=d, n=dff, k=s,
                         outs=[(BF16, "col")])
        return dy, dwi, dwo.reshape(N_SHARD, dff // N_SHARD, d)

    def token(handle):
        return handle[3][0:1, 0:1]

    def rs_exchange(tag, grads):
        lands = [lax.empty((N_SHARD, g.shape[1] // 2, g.shape[2]), g.dtype) for g in grads]
        return _split_start(f"sibling_start_{tag}", list(grads) + lands, len(grads), _sibling_copies)

    def rs_send(tag, names, exchanged, later):
        bufs = _split_wait(f"sibling_wait_{tag}", exchanged, _sibling_copies, later)
        n = len(names)
        sums = [_chip_sum(f"chip_sum_{nme}", g, rh, place) for nme, g, rh in zip(names, bufs[:n], bufs[n:])]
        lands = [lax.empty((N_SHARD - 1,) + cs.shape[1:], cs.dtype) for cs in sums]
        return _split_start(f"owners_start_{tag}", sums + lands, 3 * n, _owner_copies)

    def rs_sum(tag, names, sent, later):
        bufs = _split_wait(f"owners_wait_{tag}", sent, _owner_copies, later)
        n = len(names)
        own = [_owner_sum(f"owner_sum_{nme}", cs, rp, place) for nme, cs, rp in zip(names, bufs[:n], bufs[n:])]
        return _split_start(f"swap_start_{tag}", own, n, _swap_copies)

    def rs_end(tag, swapped, later):
        return _split_wait(f"swap_wait_{tag}", swapped, _swap_copies, later)

    dy3, g_wmi1, g_wmo1 = mlp_bwd(1, dh4b, y3, r1, r1sq)
    x_mlp1 = rs_exchange("mlp1", [g_wmi1, g_wmo1])
    dh3, dh3b, d_nmlp1 = _rms_bwd("rms_mlp1_bwd", h3, [(nmlp[1] + token(x_mlp1), dy3)], dh4)

    do_parts = _matmul("attn_out_dx", "nt", dh3b, wo_f, m=s, n=d, k=d, outs=by_residue, epilogue=ep_by_residue,
                       stage=True)
    (g_wo,) = _matmul("attn_out_dw", "tn", o, dh3b, m=d, n=d, k=s, outs=[(BF16, "plain")])
    rs_mlp1 = rs_send("mlp1", ["mlp_in1", "mlp_out1"], x_mlp1, g_wo)
    lse_res, delta_res = _attn_delta(do_parts[0].reshape(s, d), o, lse, DILATIONS)
    dq_parts, dk_parts, dv_parts = [], [], []
    for dil, q_b, kv_b, do_b, lse_b, dl_b in zip(DILATIONS, q_parts, kv_parts, do_parts, lse_res, delta_res):
        dq_b, dk_b, dv_b = _attn_bwd(f"attn_bwd_d{dil}", q_b, kv_b, do_b, lse_b, dl_b)
        dq_parts.append(dq_b)
        dk_parts.append(dk_b)
        dv_parts.append(dv_b)
    dq = _residue_sum("rope_bwd_q", [(dq_parts, True)], tabs)
    dkv = _residue_sum("rope_bwd_kv", [(dk_parts, True), (dv_parts, False)], tabs)
    (g_wq,) = _matmul("q_proj_dw", "tn", y2, dq, m=d, n=d, k=s, outs=[(BF16, "plain")])
    (dy2,) = _matmul("q_proj_dx", "nt", dq, wq_f, m=s, n=d, k=d, outs=[(BF16, "plain")])
    (g_wkv,) = _matmul("kv_proj_dw", "tn", ykv, dkv, m=d, n=kvw, k=s, outs=[(BF16, "plain")])
    (dykv,) = _matmul("kv_proj_dx", "nt", dkv, wkv_f, m=s, n=d, k=kvw, outs=[(BF16, "plain")])
    x_attn = rs_exchange("attn", [g_wkv.reshape(N_SHARD, ds4, kvw), g_wq.reshape(N_SHARD, ds4, d),
                                  g_wo.reshape(N_SHARD, ds4, d)])
    dh2, dh2b, d_nm1, d_kvn = _rms_bwd("rms_kv_mix1_bwd", h2, [(nm[1] + token(x_attn), dy2), (kvn, dykv)], dh3)
    rs_attn = rs_send("attn", ["w_kv", "w_q", "w_o"], x_attn, dh2b)

    dy1, g_wmi0, g_wmo0 = mlp_bwd(0, dh2b, y1, r0, r0sq)
    x_mlp0 = rs_exchange("mlp0", [g_wmi0, g_wmo0])
    dh1, dh1b, d_nmlp0, d_b_out = _rms_bwd("rms_mlp0_bwd", h1, [(nmlp[0] + token(x_mlp0) + token(rs_attn), dy1)],
                                           dh2, want_colsum=True)

    (dsact,) = _matmul("conv_out_dx", "nt", dh1b, w_out_f, m=s, n=d, k=d, outs=[(BF16, "plain")])
    (g_wout,) = _matmul("conv_out_dw", "tn", sact, dh1b, m=d, n=d, k=s, outs=[(BF16, "plain")])
    rs_mlp0 = rs_send("mlp0", ["mlp_in0", "mlp_out0"], x_mlp0, g_wout)
    dc, d_ln_g, d_ln_b, d_b_dw = _ln_silu_bwd(cpre, ln_g_f + token(rs_mlp0), ln_b_f, dsact)
    du, d_w_dw, d_b_in_a, d_b_in_g = _dwconv_bwd(u, w_dw_f, dc)
    (g_win,) = _matmul("conv_in_dw", "tn", y0, du, b_kind="col", m=d, n=2 * d, k=s, outs=[(BF16, "col")])
    x_conv = rs_exchange("conv", [g_win, g_wout.reshape(N_SHARD, ds4, d)])
    (dy0,) = _matmul("conv_in_dx", "nt", du, w_in_g, a_kind="col", b_kind="col", m=s, n=d, k=2 * d,
                     outs=[(BF16, "plain")])
    dx, _, d_nm0 = _rms_bwd("rms_mix0_bwd", h0, [(nm[0] + token(x_conv), dy0)], dh1)

    small_rows = [(0, d_nm0), (1, d_nm1), (2, d_nmlp0), (3, d_nmlp1), (4, d_kvn), (5, d_fin), (6, d_b_dw),
                  (7, d_ln_g), (8, d_ln_b), (9, d_b_out), (10, d_b_in_a), (11, d_b_in_g), (12, loss_cols)]
    x_small = _split_start("small_start", [_small_pack(small_rows, d_w_dw, d),
                                           lax.empty((N_DEV, SMALL_ROWS, d), F32)], N_DEV - 1, _small_copies)
    rs_conv = rs_send("conv", ["w_in", "w_out"], x_conv, x_small[3])

    def big(name, w, m, v, g, layer=0, partial=None):
        shape = w.shape
        w3, m3, v3 = [t.reshape((-1,) + shape[-2:]) for t in (w, m, v)]
        if partial is not None:
            partial = [t.reshape(w3.shape) for t in partial]
        res = _adamw(name, w3, m3, v3, g, layer, partial)
        return [t.reshape(shape) for t in res]

    sw_mlp1 = rs_sum("mlp1", ["mlp_in1", "mlp_out1"], rs_mlp1, rs_conv[3])
    sw_attn = rs_sum("attn", ["w_kv", "w_q", "w_o"], rs_attn, sw_mlp1[3])
    f_wmi1, f_wmo1 = rs_end("mlp1", sw_mlp1, sw_attn[3])
    p_wmi = big("adam_mlp_in1", mlp_w_in, m_mlp_w_in, v_mlp_w_in, f_wmi1, 1)
    p_wmo = big("adam_mlp_out1", mlp_w_out, m_mlp_w_out, v_mlp_w_out, f_wmo1, 1)
    sw_mlp0 = rs_sum("mlp0", ["mlp_in0", "mlp_out0"], rs_mlp0, [p_wmi[0], p_wmo[0]])
    f_wkv, f_wq, f_wo = rs_end("attn", sw_attn, sw_mlp0[3])
    r_wkv = big("adam_w_kv", w_kv, m_w_kv, v_w_kv, f_wkv)
    r_wq = big("adam_w_q", attn_w_q, m_attn_w_q, v_attn_w_q, f_wq)
    r_wo = big("adam_w_o", attn_w_o, m_attn_w_o, v_attn_w_o, f_wo)
    sw_conv = rs_sum("conv", ["w_in", "w_out"], rs_conv, [r_wkv[0], r_wq[0], r_wo[0]])
    f_wmi0, f_wmo0 = rs_end("mlp0", sw_mlp0, sw_conv[3])
    r_wmi = big("adam_mlp_in0", mlp_w_in, m_mlp_w_in, v_mlp_w_in, f_wmi0, 0, p_wmi)
    r_wmo = big("adam_mlp_out0", mlp_w_out, m_mlp_w_out, v_mlp_w_out, f_wmo0, 0, p_wmo)
    f_win, f_wout = rs_end("conv", sw_conv, [r_wmi[0], r_wmo[0]])
    r_win = big("adam_w_in", conv_w_in, m_conv_w_in, v_conv_w_in, f_win)
    r_wout = big("adam_w_out", conv_w_out, m_conv_w_out, v_conv_w_out, f_wout)

    small_pack, small_slots = _split_wait("small_wait", x_small, _small_copies, r_wout[0])
    red = _small_sum(small_pack, small_slots, place)
    loss = red[12, 0]
    g_norm_mix = red[0:2]
    g_norm_mlp = red[2:4]
    g_kv_norm = red[4:5]
    g_final = red[5:6]

    def my_cols(row):
        return lax.dynamic_slice(red, (row, me * ds4), (1, ds4))

    g_b_dw, g_ln_g, g_ln_b, g_b_out = my_cols(6), my_cols(7), my_cols(8), my_cols(9)
    half_in = 2 * d // N_SHARD
    b_in_row = 10 + me // 2
    g_b_in = lax.dynamic_slice(red, (b_in_row, (me % 2) * half_in), (1, half_in))
    g_w_dw = lax.dynamic_slice(red, (16, me * ds4), (CONV_WIDTH, ds4))

    sm_w =[norm_mix, norm_mlp, conv_b_in, conv_w_dw.reshape(CONV_WIDTH, ds4), conv_b_dw, conv_ln_g, conv_ln_b,
            conv_b_out, kv_norm.reshape(1, d), final_norm.reshape(1, d)]
    sm_m = [m_norm_mix, m_norm_mlp, m_conv_b_in, m_conv_w_dw.reshape(CONV_WIDTH, ds4), m_conv_b_dw, m_conv_ln_g,
            m_conv_ln_b, m_conv_b_out, m_kv_norm.reshape(1, d), m_final_norm.reshape(1, d)]
    sm_v = [v_norm_mix, v_norm_mlp, v_conv_b_in, v_conv_w_dw.reshape(CONV_WIDTH, ds4), v_conv_b_dw, v_conv_ln_g,
            v_conv_ln_b, v_conv_b_out, v_kv_norm.reshape(1, d), v_final_norm.reshape(1, d)]
    sm_g = [g_norm_mix, g_norm_mlp, g_b_in, g_w_dw, g_b_dw, g_ln_g, g_ln_b, g_b_out, g_kv_norm, g_final]
    sm_d, sm_nm, sm_nv = _adam_small(sm_w, sm_m, sm_v, sm_g)
    shapes = [norm_mix.shape, norm_mlp.shape, conv_b_in.shape, conv_w_dw.shape, conv_b_dw.shape, conv_ln_g.shape,
              conv_ln_b.shape, conv_b_out.shape, kv_norm.shape, final_norm.shape]
    sm_g, sm_d, sm_nm, sm_nv = [[t.reshape(sh) for t, sh in zip(lst, shapes)] for lst in (sm_g, sm_d, sm_nm, sm_nv)]

    def order(sm, idx):
        return [sm[0], sm[1], r_win[idx], sm[2], sm[3], sm[4], sm[5], sm[6], r_wout[idx], sm[7], sm[8],
                r_wkv[idx], r_wq[idx], r_wo[idx], r_wmi[idx], r_wmo[idx], sm[9]]

    return (loss, dx.reshape(x.shape), *order(sm_g, 0), *order(sm_d, 1), *order(sm_nm, 2), *order(sm_nv, 3))
```

```python
import functools
import math

import jax
import jax.numpy as jnp
from jax import lax
from jax.experimental import pallas as pl
from jax.experimental.pallas import tpu as pltpu

F32 = jnp.float32
BF16 = jnp.bfloat16
I32 = jnp.int32

NORM_EPS = 1e-6
LN_EPS = 1e-5
HEAD_DIM = 128
N_KV_HEADS = 4
ROT_DIM = 32
ROPE_THETA = 500000.0
CONV_WIDTH = 31
CONV_PAD = 32
ATT_BLOCK = 128
ATT_STEP_BLOCKS = 16
DILATIONS = (1, 4, 16)
ADAM_LR = 0.001
ADAM_B1 = 0.9
ADAM_B2 = 0.999
ADAM_EPS = 1e-08
ADAM_WD = 0.01
ADAM_STEP = 10
N_SHARD = 4
N_DEV = 8
LANES = 128
VMEM_LIMIT = 48 * 1024 * 1024
MM_TM, MM_TN, MM_TK = 1024, 1024, 2048
ROW_TILE = 512
CONV_CB = 128
CONV_T = 128
SMALL_ROWS = 48
MESH = pl.DeviceIdType.MESH
ANY = pl.BlockSpec(memory_space=pl.ANY)
HBM = pl.BlockSpec(memory_space=pltpu.HBM)
SEM = pl.BlockSpec(memory_space=pltpu.SEMAPHORE)
SPLIT_EFFECT = pltpu.SideEffectType.DATAFLOW_SIDE_EFFECTING


def _params(*sem):
    return pltpu.CompilerParams(dimension_semantics=sem, vmem_limit_bytes=VMEM_LIMIT)


def _sigmoid(x):
    return 1.0 / (1.0 + jnp.exp(-x))


def _wspec(kind, arr_shape, br, bc, pick):
    if kind == "plain":
        return pl.BlockSpec((br, bc), pick)
    per = arr_shape[2] // bc

    def idx(*g):
        rb, cb = pick(*g)
        return (cb // per, rb, cb % per)

    return pl.BlockSpec((None, br, bc), idx)


def _stage_shape(rows, w):
    return (w // LANES, rows, LANES)


def _to_residues(val, stage_ref, out_refs, dils):
    planes, rows, _ = stage_ref.shape
    for c in range(planes):
        stage_ref[c] = val[:, c * LANES:(c + 1) * LANES]
    for out_ref, dil in zip(out_refs, dils):
        if dil == 1:
            out_ref[0] = val.astype(out_ref.dtype)
            continue
        for r in range(dil):
            for c in range(planes):
                out_ref[r, :, c * LANES:(c + 1) * LANES] = stage_ref.at[c][pl.ds(r, rows // dil, stride=dil), :].astype(
                    out_ref.dtype)


def _from_residues(src_ref, stage_ref, dil):
    planes, rows, _ = stage_ref.shape
    if dil == 1:
        return lambda c: src_ref[0, :, c * LANES:(c + 1) * LANES].astype(F32)
    for r in range(dil):
        for c in range(planes):
            stage_ref.at[c][pl.ds(r, rows // dil, stride=dil), :] = src_ref[r, :, c * LANES:(c + 1) * LANES].astype(F32)
    return lambda c: stage_ref[c]


def _matmul(name, mode, a, b, *, m, n, k, tm=MM_TM, tn=MM_TN, tk=MM_TK, a_kind="plain", b_kind="plain", outs,
            extras=(), epilogue=None, stage=False):
    tm, tn, tk = min(tm, m), min(tn, n), min(tk, k)
    if b_kind == "col" and mode in ("nn", "tn"):
        tn = min(tn, n // b.shape[0])
    if b_kind == "col" and mode == "nt":
        tk = min(tk, k // b.shape[0])
    if a_kind == "col":
        assert mode == "nt"
        tk = min(tk, k // a.shape[0])
    if any(kind == "col" for _, kind in outs):
        tn = min(tn, n // N_SHARD)
    assert m % tm == 0 and n % tn == 0 and k % tk == 0, (name, m, n, k, tm, tn, tk)
    nk = k // tk
    grid = (m // tm, n // tn, nk)
    if mode == "nn":
        a_spec = pl.BlockSpec((tm, tk), lambda i, j, kk: (i, kk))
        b_spec = _wspec(b_kind, b.shape, tk, tn, lambda i, j, kk: (kk, j))
        dims = (((1,), (0,)), ((), ()))
    elif mode == "nt":
        a_spec = _wspec(a_kind, a.shape, tm, tk, lambda i, j, kk: (i, kk))
        b_spec = _wspec(b_kind, b.shape, tn, tk, lambda i, j, kk: (j, kk))
        dims = (((1,), (1,)), ((), ()))
    else:
        a_spec = pl.BlockSpec((tk, tm), lambda i, j, kk: (kk, i))
        b_spec = _wspec(b_kind, b.shape, tk, tn, lambda i, j, kk: (kk, j))
        dims = (((0,), (0,)), ((), ()))
    out_shape, out_specs = [], []
    for dtype, kind in outs:
        if isinstance(kind, tuple):
            dil = kind[1]
            out_shape.append(jax.ShapeDtypeStruct((dil, m // dil, n), dtype))
            out_specs.append(pl.BlockSpec((dil, tm // dil, tn), lambda i, j, kk: (0, i, j)))
            continue
        shape = (m, n) if kind == "plain" else (N_SHARD, m, n // N_SHARD)
        out_shape.append(jax.ShapeDtypeStruct(shape, dtype))
        out_specs.append(_wspec(kind, shape, tm, tn, lambda i, j, kk: (i, j)))
    n_ex = len(extras)
    ex_specs = {"ij": pl.BlockSpec((tm, tn), lambda i, j, kk: (i, j)),
                "vec": pl.BlockSpec((1, tn), lambda i, j, kk: (0, j)),
                "rows": pl.BlockSpec((tm, LANES), lambda i, j, kk: (i, 0))}

    def body(*refs):
        a_ref, b_ref = refs[0], refs[1]
        ex_refs = refs[2:2 + n_ex]
        out_refs = refs[2 + n_ex:2 + n_ex + len(outs)]
        j = pl.program_id(1)

        def finish(res):
            if epilogue is None:
                out_refs[0][...] = res.astype(out_refs[0].dtype)
            elif stage:
                epilogue(res, ex_refs, out_refs, j, refs[-1])
            else:
                epilogue(res, ex_refs, out_refs, j)

        prod = lax.dot_general(a_ref[...], b_ref[...], dims, preferred_element_type=F32)
        if nk == 1:
            finish(prod)
            return
        acc_ref = refs[2 + n_ex + len(outs)]
        kk = pl.program_id(2)

        @pl.when(kk == 0)
        def _():
            acc_ref[...] = prod

        @pl.when(kk > 0)
        def _():
            acc_ref[...] += prod

        @pl.when(kk == nk - 1)
        def _():
            finish(acc_ref[...])

    res = pl.pallas_call(
        body, name=name, grid=grid,
        in_specs=[a_spec, b_spec] + [ex_specs[how] for _, how in extras],
        out_specs=out_specs, out_shape=out_shape,
        scratch_shapes=[pltpu.VMEM((tm, tn), F32)] * (nk > 1) + [pltpu.VMEM(_stage_shape(tm, tn), F32)] * bool(stage),
        compiler_params=_params("parallel", "parallel", "arbitrary"),
    )(a, b, *[e for e, _ in extras])
    return res


def _rope_tables(seq):
    half = ROT_DIM // 2
    pos = jnp.arange(seq, dtype=F32)
    inv = ROPE_THETA ** (-jnp.arange(0, ROT_DIM, 2, dtype=F32) / ROT_DIM)
    ang = pos[:, None] * inv[None, :]
    cos, sin = jnp.cos(ang), jnp.sin(ang)
    zeros = jnp.zeros((seq, HEAD_DIM - ROT_DIM), F32)
    ctab = jnp.concatenate([cos, cos, zeros + 1.0], axis=1)
    atab = jnp.concatenate([-sin, jnp.zeros((seq, half), F32), zeros], axis=1)
    btab = jnp.concatenate([jnp.zeros((seq, half), F32), sin, zeros], axis=1)
    return ctab, atab, btab


def _rope_apply(x, ctab, atab, btab, sign):
    w = x.shape[1]
    reps = w // HEAD_DIM
    half = ROT_DIM // 2
    c = jnp.tile(ctab, (1, reps))
    a = jnp.tile(atab, (1, reps))
    b = jnp.tile(btab, (1, reps))
    up = pltpu.roll(x, w - half, 1)
    down = pltpu.roll(x, half, 1)
    return x * c + sign * (up * a + down * b)


def _rows(t, w):
    return pl.BlockSpec((t, w), lambda i: (i, 0))


def _fixed(shape):
    nd = len(shape)
    return pl.BlockSpec(shape, lambda i: (0,) * nd)


def _rms_fwd(name, x, gains):
    s, d = x.shape
    t = min(ROW_TILE, s)
    ng = len(gains)

    def body(x_ref, *refs):
        xv = x_ref[...]
        r = lax.rsqrt(jnp.mean(xv * xv, axis=-1, keepdims=True) + NORM_EPS)
        xn = xv * r
        for g_ref, y_ref in zip(refs[:ng], refs[ng:]):
            y_ref[...] = (xn * g_ref[...]).astype(BF16)

    return pl.pallas_call(
        body, name=name, grid=(s // t,),
        in_specs=[_rows(t, d)] + [_fixed((1, d))] * ng,
        out_specs=[_rows(t, d)] * ng,
        out_shape=[jax.ShapeDtypeStruct((s, d), BF16)] * ng,
        compiler_params=_params("parallel"),
    )(x, *gains)


def _rms_bwd(name, x, pairs, dh_in, want_colsum=False):
    s, d = x.shape
    n_p = len(pairs)
    t = min(ROW_TILE // n_p, s)

    def body(x_ref, dh_ref, *refs):
        g_refs = refs[:n_p]
        dy_refs = refs[n_p:2 * n_p]
        dh_out, dhb_out = refs[2 * n_p], refs[2 * n_p + 1]
        dg_refs = refs[2 * n_p + 2:2 * n_p + 2 + n_p]
        cs_ref = refs[-1] if want_colsum else None
        i = pl.program_id(0)
        xv = x_ref[...]
        r = lax.rsqrt(jnp.mean(xv * xv, axis=-1, keepdims=True) + NORM_EPS)
        xn = xv * r
        dh = dh_ref[...]
        for g_ref, dy_ref, dg_ref in zip(g_refs, dy_refs, dg_refs):
            dy = dy_ref[...].astype(F32)
            u = dy * g_ref[...]
            dh = dh + r * (u - xn * jnp.mean(u * xn, axis=-1, keepdims=True))
            part = jnp.sum(dy * xn, axis=0, keepdims=True)

            @pl.when(i == 0)
            def _():
                dg_ref[...] = part

            @pl.when(i > 0)
            def _():
                dg_ref[...] += part

        dh_out[...] = dh
        dhb_out[...] = dh.astype(BF16)
        if want_colsum:
            col = jnp.sum(dh, axis=0, keepdims=True)

            @pl.when(i == 0)
            def _():
                cs_ref[...] = col

            @pl.when(i > 0)
            def _():
                cs_ref[...] += col

    n_vec = n_p + (1 if want_colsum else 0)
    return pl.pallas_call(
        body, name=name, grid=(s // t,),
        in_specs=[_rows(t, d), _rows(t, d)] + [_fixed((1, d))] * n_p + [_rows(t, d)] * n_p,
        out_specs=[_rows(t, d), _rows(t, d)] + [_fixed((1, d))] * n_vec,
        out_shape=[jax.ShapeDtypeStruct((s, d), F32), jax.ShapeDtypeStruct((s, d), BF16)]
        + [jax.ShapeDtypeStruct((1, d), F32)] * n_vec,
        compiler_params=_params("arbitrary"),
    )(x, dh_in, *[g for g, _ in pairs], *[dy for _, dy in pairs])


def _final_loss(x, g, target):
    s, d = x.shape
    t = min(ROW_TILE, s)

    def body(x_ref, g_ref, t_ref, dh_out, dhb_out, dg_ref, loss_ref):
        i = pl.program_id(0)
        xv = x_ref[...]
        gv = g_ref[...]
        r = lax.rsqrt(jnp.mean(xv * xv, axis=-1, keepdims=True) + NORM_EPS)
        xn = xv * r
        diff = xn * gv - t_ref[...]
        dy = diff / d
        u = dy * gv
        dh = r * (u - xn * jnp.mean(u * xn, axis=-1, keepdims=True))
        dh_out[...] = dh
        dhb_out[...] = dh.astype(BF16)
        dg = jnp.sum(dy * xn, axis=0, keepdims=True)
        lc = jnp.sum(0.5 * diff * dy, axis=0, keepdims=True)

        @pl.when(i == 0)
        def _():
            dg_ref[...] = dg
            loss_ref[...] = lc

        @pl.when(i > 0)
        def _():
            dg_ref[...] += dg
            loss_ref[...] += lc

    return pl.pallas_call(
        body, name="final_loss", grid=(s // t,),
        in_specs=[_rows(t, d), _fixed((1, d)), _rows(t, d)],
        out_specs=[_rows(t, d), _rows(t, d), _fixed((1, d)), _fixed((1, d))],
        out_shape=[jax.ShapeDtypeStruct((s, d), F32), jax.ShapeDtypeStruct((s, d), BF16),
                   jax.ShapeDtypeStruct((1, d), F32), jax.ShapeDtypeStruct((1, d), F32)],
        compiler_params=_params("arbitrary"),
    )(x, g, target)


def _ln_silu_fwd(c, g, b):
    s, d = c.shape
    t = min(ROW_TILE, s)

    def body(c_ref, g_ref, b_ref, s_ref):
        cv = c_ref[...]
        mu = jnp.mean(cv, axis=-1, keepdims=True)
        xc = cv - mu
        rs = lax.rsqrt(jnp.mean(xc * xc, axis=-1, keepdims=True) + LN_EPS)
        ln = xc * rs * g_ref[...] + b_ref[...]
        s_ref[...] = (ln * _sigmoid(ln)).astype(BF16)

    return pl.pallas_call(
        body, name="ln_silu_fwd", grid=(s // t,),
        in_specs=[_rows(t, d), _fixed((1, d)), _fixed((1, d))],
        out_specs=_rows(t, d), out_shape=jax.ShapeDtypeStruct((s, d), BF16),
        compiler_params=_params("parallel"),
    )(c, g, b)


def _ln_silu_bwd(c, g, b, ds):
    s, d = c.shape
    t = min(ROW_TILE, s)

    def body(c_ref, g_ref, b_ref, ds_ref, dc_ref, dg_ref, db_ref, dbdw_ref):
        i = pl.program_id(0)
        cv = c_ref[...]
        gv = g_ref[...]
        mu = jnp.mean(cv, axis=-1, keepdims=True)
        xc = cv - mu
        rs = lax.rsqrt(jnp.mean(xc * xc, axis=-1, keepdims=True) + LN_EPS)
        nrm = xc * rs
        ln = nrm * gv + b_ref[...]
        sig = _sigmoid(ln)
        dln = ds_ref[...].astype(F32) * sig * (1.0 + ln * (1.0 - sig))
        dn = dln * gv
        dc = rs * (dn - jnp.mean(dn, axis=-1, keepdims=True)
                   - nrm * jnp.mean(dn * nrm, axis=-1, keepdims=True))
        dc_ref[...] = dc
        pg = jnp.sum(dln * nrm, axis=0, keepdims=True)
        pb = jnp.sum(dln, axis=0, keepdims=True)
        pc = jnp.sum(dc, axis=0, keepdims=True)

        @pl.when(i == 0)
        def _():
            dg_ref[...] = pg
            db_ref[...] = pb
            dbdw_ref[...] = pc

        @pl.when(i > 0)
        def _():
            dg_ref[...] += pg
            db_ref[...] += pb
            dbdw_ref[...] += pc

    return pl.pallas_call(
        body, name="ln_silu_bwd", grid=(s // t,),
        in_specs=[_rows(t, d), _fixed((1, d)), _fixed((1, d)), _rows(t, d)],
        out_specs=[_rows(t, d)] + [_fixed((1, d))] * 3,
        out_shape=[jax.ShapeDtypeStruct((s, d), F32)] + [jax.ShapeDtypeStruct((1, d), F32)] * 3,
        compiler_params=_params("arbitrary"),
    )(c, g, b, ds)


def _residue_spec(dil, t, w):
    return pl.BlockSpec((dil, t // dil, w), lambda i: (0, i, 0))


def _attn_combine(o_list, lse_list):
    dil0, sd0, d = o_list[0].shape
    s = dil0 * sd0
    lw = lse_list[0].shape[2]
    group = d // HEAD_DIM // N_KV_HEADS
    t = min(ROW_TILE, s)
    nb = len(o_list)
    dils = [o.shape[0] for o in o_list]

    def body(*refs):
        o_out, l_out = refs[2 * nb], refs[2 * nb + 1]
        o_stage, l_stage = refs[2 * nb + 2:3 * nb + 2], refs[3 * nb + 2:]
        o_planes = [_from_residues(src, stage, dil) for src, stage, dil in zip(refs[:nb], o_stage, dils)]
        l_planes = [_from_residues(src, stage, dil) for src, stage, dil in zip(refs[nb:2 * nb], l_stage, dils)]
        for kh in range(N_KV_HEADS):
            ls = [plane(kh) for plane in l_planes]
            mx = ls[0]
            for l in ls[1:]:
                mx = jnp.maximum(mx, l)
            es = [jnp.exp(l - mx) for l in ls]
            den = es[0]
            for e in es[1:]:
                den = den + e
            l_out[:, kh * LANES:(kh + 1) * LANES] = mx + jnp.log(den)
            ws = [e / den for e in es]
            for g in range(group):
                h = kh * group + g
                acc = jnp.zeros((t, HEAD_DIM), F32)
                for plane, w in zip(o_planes, ws):
                    acc = acc + w[:, g:g + 1] * plane(h)
                o_out[:, h * HEAD_DIM:(h + 1) * HEAD_DIM] = acc.astype(BF16)

    return pl.pallas_call(
        body, name="attn_combine", grid=(s // t,),
        in_specs=[_residue_spec(dil, t, d) for dil in dils] + [_residue_spec(dil, t, lw) for dil in dils],
        out_specs=[_rows(t, d), _rows(t, lw)],
        out_shape=[jax.ShapeDtypeStruct((s, d), BF16), jax.ShapeDtypeStruct((s, lw), F32)],
        scratch_shapes=[pltpu.VMEM(_stage_shape(t, d), F32)] * nb + [pltpu.VMEM(_stage_shape(t, lw), F32)] * nb,
        compiler_params=_params("parallel"),
    )(*o_list, *lse_list)


def _attn_delta(do, o, lse, dils):
    s, d = o.shape
    lw = lse.shape[1]
    group = d // HEAD_DIM // N_KV_HEADS
    t = min(ROW_TILE, s)
    nd = len(dils)

    def body(do_ref, o_ref, lse_ref, *refs):
        stage = refs[-1]
        lane = lax.broadcasted_iota(I32, (t, LANES), 1)
        planes = []
        for kh in range(N_KV_HEADS):
            out = jnp.zeros((t, LANES), F32)
            for g in range(group):
                cols = slice((kh * group + g) * HEAD_DIM, (kh * group + g + 1) * HEAD_DIM)
                v = jnp.sum(do_ref[:, cols].astype(F32) * o_ref[:, cols].astype(F32), axis=-1, keepdims=True)
                out = jnp.where(lane == g, v, out)
            planes.append(out)
        _to_residues(lse_ref[...], stage, refs[:nd], dils)
        _to_residues(jnp.concatenate(planes, axis=1), stage, refs[nd:2 * nd], dils)

    res = pl.pallas_call(
        body, name="attn_delta", grid=(s // t,),
        in_specs=[_rows(t, d), _rows(t, d), _rows(t, lw)],
        out_specs=[_residue_spec(dil, t, lw) for dil in dils] * 2,
        out_shape=[jax.ShapeDtypeStruct((dil, s // dil, lw), F32) for dil in dils] * 2,
        scratch_shapes=[pltpu.VMEM(_stage_shape(t, lw), F32)],
        compiler_params=_params("parallel"),
    )(do, o, lse)
    return res[:nd], res[nd:]


def _residue_sum(name, groups, tabs):
    first = groups[0][0][0]
    s, w = first.shape[0] * first.shape[1], first.shape[2]
    t = min(ROW_TILE, s)
    flat = [p for parts, _ in groups for p in parts]

    def body(*refs):
        c_ref, a_ref, b_ref = refs[len(flat):len(flat) + 3]
        out = refs[len(flat) + 3]
        stages = refs[len(flat) + 4:]
        k = 0
        for gi, (parts, rotate) in enumerate(groups):
            planes = [_from_residues(refs[k + i], stages[k + i], p.shape[0]) for i, p in enumerate(parts)]
            k += len(parts)
            for c in range(w // LANES):
                tot = planes[0](c)
                for plane in planes[1:]:
                    tot = tot + plane(c)
                if rotate:
                    tot = _rope_apply(tot, c_ref[...], a_ref[...], b_ref[...], -1.0)
                out[:, gi * w + c * LANES:gi * w + (c + 1) * LANES] = tot.astype(BF16)

    return pl.pallas_call(
        body, name=name, grid=(s // t,),
        in_specs=[_residue_spec(p.shape[0], t, w) for p in flat] + [_rows(t, HEAD_DIM)] * 3,
        out_specs=_rows(t, len(groups) * w), out_shape=jax.ShapeDtypeStruct((s, len(groups) * w), BF16),
        scratch_shapes=[pltpu.VMEM(_stage_shape(t, w), F32) for _ in flat],
        compiler_params=_params("parallel"),
    )(*flat, *tabs)


def _dwconv_fwd(u, w_dw, b_dw):
    s, d2 = u.shape
    d = d2 // 2
    cb = min(CONV_CB, d)
    nblk = d // cb
    tt = min(CONV_T, s)

    def body(ua_ref, ug_ref, w_ref, b_ref, c_ref, xp_ref):
        gl = ua_ref[...].astype(F32) * _sigmoid(ug_ref[...].astype(F32))
        xp_ref[0:CONV_PAD, :] = jnp.zeros((CONV_PAD, cb), F32)
        xp_ref[CONV_PAD:, :] = gl
        wv = w_ref[...]
        bv = b_ref[...]
        for t0 in range(0, s, tt):
            acc = jnp.zeros((tt, cb), F32) + bv
            for kk in range(CONV_WIDTH):
                off = t0 + CONV_PAD - (CONV_WIDTH - 1) + kk
                acc = acc + wv[kk:kk + 1, :] * xp_ref[off:off + tt, :]
            c_ref[t0:t0 + tt, :] = acc

    return pl.pallas_call(
        body, name="dwconv_fwd", grid=(nblk,),
        in_specs=[pl.BlockSpec((s, cb), lambda j: (0, j)), pl.BlockSpec((s, cb), lambda j: (0, j + nblk)),
                  pl.BlockSpec((CONV_PAD, cb), lambda j: (0, j)), pl.BlockSpec((1, cb), lambda j: (0, j))],
        out_specs=pl.BlockSpec((s, cb), lambda j: (0, j)),
        out_shape=jax.ShapeDtypeStruct((s, d), F32),
        scratch_shapes=[pltpu.VMEM((s + CONV_PAD, cb), F32)],
        compiler_params=_params("parallel"),
    )(u, u, w_dw, b_dw)


def _dwconv_bwd(u, w_dw, dc):
    s, d2 = u.shape
    d = d2 // 2
    cb = min(CONV_CB, d)
    nblk = d // cb
    tt = min(CONV_T, s)

    def body(ua_ref, ug_ref, w_ref, dc_ref, du_ref, dw_ref, dba_ref, dbg_ref, glp_ref, dcp_ref, acc_ref):
        a = ua_ref[...].astype(F32)
        sig = _sigmoid(ug_ref[...].astype(F32))
        glp_ref[0:CONV_PAD, :] = jnp.zeros((CONV_PAD, cb), F32)
        glp_ref[CONV_PAD:, :] = a * sig
        dcp_ref[0:s, :] = dc_ref[...]
        dcp_ref[s:, :] = jnp.zeros((CONV_PAD, cb), F32)
        acc_ref[...] = jnp.zeros_like(acc_ref)
        wv = w_ref[...]
        dba = jnp.zeros((1, cb), F32)
        dbg = jnp.zeros((1, cb), F32)
        for t0 in range(0, s, tt):
            dgl = jnp.zeros((tt, cb), F32)
            dct = dc_ref[t0:t0 + tt, :]
            for kk in range(CONV_WIDTH):
                off = t0 + (CONV_WIDTH - 1) - kk
                dgl = dgl + wv[kk:kk + 1, :] * dcp_ref[off:off + tt, :]
                goff = t0 + CONV_PAD - (CONV_WIDTH - 1) + kk
                prod = dct * glp_ref[goff:goff + tt, :]
                acc_ref[8 * kk:8 * kk + 8, :] += jnp.sum(prod.reshape(tt // 8, 8, cb), axis=0)
            at = ua_ref[t0:t0 + tt, :].astype(F32)
            st = _sigmoid(ug_ref[t0:t0 + tt, :].astype(F32))
            da = dgl * st
            dg = dgl * at * st * (1.0 - st)
            du_ref[0, t0:t0 + tt, :] = da.astype(BF16)
            du_ref[1, t0:t0 + tt, :] = dg.astype(BF16)
            dba = dba + jnp.sum(da, axis=0, keepdims=True)
            dbg = dbg + jnp.sum(dg, axis=0, keepdims=True)
        dba_ref[...] = dba
        dbg_ref[...] = dbg
        for kk in range(CONV_WIDTH):
            dw_ref[kk:kk + 1, :] = jnp.sum(acc_ref[8 * kk:8 * kk + 8, :], axis=0, keepdims=True)
        dw_ref[CONV_WIDTH:, :] = jnp.zeros((CONV_PAD - CONV_WIDTH, cb), F32)

    blk = pl.BlockSpec((s, cb), lambda j: (0, j))
    vec = pl.BlockSpec((1, cb), lambda j: (0, j))
    return pl.pallas_call(
        body, name="dwconv_bwd", grid=(nblk,),
        in_specs=[blk, pl.BlockSpec((s, cb), lambda j: (0, j + nblk)),
                  pl.BlockSpec((CONV_PAD, cb), lambda j: (0, j)), blk],
        out_specs=[pl.BlockSpec((2, s, cb), lambda j: (0, 0, j)), pl.BlockSpec((CONV_PAD, cb), lambda j: (0, j)),
                   vec, vec],
        out_shape=[jax.ShapeDtypeStruct((2, s, d), BF16), jax.ShapeDtypeStruct((CONV_PAD, d), F32),
                   jax.ShapeDtypeStruct((1, d), F32), jax.ShapeDtypeStruct((1, d), F32)],
        scratch_shapes=[pltpu.VMEM((s + CONV_PAD, cb), F32), pltpu.VMEM((s + CONV_PAD, cb), F32),
                        pltpu.VMEM((8 * CONV_PAD, cb), F32)],
        compiler_params=_params("parallel"),
    )(u, u, w_dw, dc)


def _stack_heads(x, group):
    return jnp.concatenate([x[:, g * HEAD_DIM:(g + 1) * HEAD_DIM] for g in range(group)], axis=0)


def _unstack_heads(x, group):
    return jnp.concatenate([x[g * ATT_BLOCK:(g + 1) * ATT_BLOCK, :] for g in range(group)], axis=1)


def _stack_cols(x, group):
    return jnp.concatenate([x[:, g:g + 1] for g in range(group)], axis=0)


def _band_bias(group):
    rows = group * ATT_BLOCK
    row = lax.broadcasted_iota(I32, (rows, 2 * ATT_BLOCK), 0) % ATT_BLOCK
    col = lax.broadcasted_iota(I32, (rows, 2 * ATT_BLOCK), 1)
    band = jnp.where((col >= row) & (col <= row + ATT_BLOCK), 0.0, -jnp.inf).astype(F32)
    first = jnp.where(lax.broadcasted_iota(I32, (1, 2 * ATT_BLOCK), 1) >= ATT_BLOCK, 0.0, -jnp.inf).astype(F32)
    return band, first


def _masked_scores(qs, kw, band_ref, first_ref, nb, scale):
    sc = lax.dot_general(qs, kw, (((1,), (1,)), ((), ())), preferred_element_type=F32) * scale + band_ref[...]
    return sc + jnp.where(nb > 0, 0.0, first_ref[...])


def _window(ref, nb):
    prev = pl.multiple_of(jnp.maximum(nb - 1, 0) * ATT_BLOCK, ATT_BLOCK)
    cur = pl.multiple_of(nb * ATT_BLOCK, ATT_BLOCK)
    return jnp.concatenate([ref[pl.ds(prev, ATT_BLOCK), :], ref[pl.ds(cur, ATT_BLOCK), :]], axis=0)


def _residues_per_step(dil, nblk):
    return max(1, min(dil, ATT_STEP_BLOCKS // nblk))


def _attn_fwd(name, q, kv):
    dil, sd, d = q.shape
    group = d // HEAD_DIM // N_KV_HEADS
    gw = group * HEAD_DIM
    nblk = sd // ATT_BLOCK
    scale = 1.0 / math.sqrt(HEAD_DIM)
    nt = (((1,), (1,)), ((), ()))

    rb = _residues_per_step(dil, nblk)

    def body(q_all, k_all, v_all, band_ref, first_ref, o_all, lse_all):
        lane = lax.broadcasted_iota(I32, (ATT_BLOCK, LANES), 1)
        for rr in range(rb):
            q_ref, k_ref, v_ref, o_ref, lse_ref = [ref.at[rr] for ref in (q_all, k_all, v_all, o_all, lse_all)]

            def step(nb, carry):
                rows = pl.ds(pl.multiple_of(nb * ATT_BLOCK, ATT_BLOCK), ATT_BLOCK)
                qs = _stack_heads(q_ref[rows, :], group)
                kw = _window(k_ref, nb)
                vw = _window(v_ref, nb)
                sc = _masked_scores(qs, kw, band_ref, first_ref, nb, scale)
                mx = jnp.max(sc, axis=-1, keepdims=True)
                p = jnp.exp(sc - mx)
                l = jnp.sum(p, axis=-1, keepdims=True)
                o = jnp.dot(p.astype(BF16), vw, preferred_element_type=F32) / l
                o_ref[rows, :] = _unstack_heads(o, group).astype(BF16)
                lse = mx + jnp.log(l)
                out = jnp.zeros((ATT_BLOCK, LANES), F32)
                for g in range(group):
                    out = jnp.where(lane == g, lse[g * ATT_BLOCK:(g + 1) * ATT_BLOCK, :], out)
                lse_ref[rows, :] = out
                return carry

            lax.fori_loop(0, nblk, step, 0, unroll=min(2, nblk))

    kvh = N_KV_HEADS
    band, first = _band_bias(group)
    qspec = pl.BlockSpec((rb, sd, gw), lambda r, h: (r, 0, h))
    kspec = pl.BlockSpec((rb, sd, HEAD_DIM), lambda r, h: (r, 0, h))
    return pl.pallas_call(
        body, name=name, grid=(dil // rb, kvh),
        in_specs=[qspec, kspec, pl.BlockSpec((rb, sd, HEAD_DIM), lambda r, h: (r, 0, kvh + h)),
                  pl.BlockSpec(band.shape, lambda r, h: (0, 0)), pl.BlockSpec(first.shape, lambda r, h: (0, 0))],
        out_specs=[qspec, kspec],
        out_shape=[jax.ShapeDtypeStruct((dil, sd, d), BF16),
                   jax.ShapeDtypeStruct((dil, sd, kvh * LANES), F32)],
        compiler_params=_params("parallel", "parallel"),
    )(q, kv, kv, band, first)


def _attn_bwd(name, q, kv, do, lse, delta):
    dil, sd, d = q.shape
    group = d // HEAD_DIM // N_KV_HEADS
    gw = group * HEAD_DIM
    nblk = sd // ATT_BLOCK
    scale = 1.0 / math.sqrt(HEAD_DIM)
    nt = (((1,), (1,)), ((), ()))
    tn = (((0,), (0,)), ((), ()))

    rb = _residues_per_step(dil, nblk)

    def body(q_all, k_all, v_all, do_all, lse_all, dl_all, band_ref, first_ref, dq_all, dk_all, dv_all, dk_accs,
             dv_accs):
        dk_accs[...] = jnp.zeros_like(dk_accs)
        dv_accs[...] = jnp.zeros_like(dv_accs)
        for rr in range(rb):
            q_ref, k_ref, v_ref, do_ref, lse_ref, dl_ref, dq_ref, dk_ref, dv_ref, dk_acc, dv_acc = [
                ref.at[rr] for ref in (q_all, k_all, v_all, do_all, lse_all, dl_all, dq_all, dk_all, dv_all,
                                       dk_accs, dv_accs)]

            def step(nb, carry):
                rows = pl.ds(pl.multiple_of(nb * ATT_BLOCK, ATT_BLOCK), ATT_BLOCK)
                qs = _stack_heads(q_ref[rows, :], group)
                dos = _stack_heads(do_ref[rows, :], group)
                ls = _stack_cols(lse_ref[rows, :], group)
                dl = _stack_cols(dl_ref[rows, :], group)
                kw = _window(k_ref, nb)
                vw = _window(v_ref, nb)
                p = jnp.exp(_masked_scores(qs, kw, band_ref, first_ref, nb, scale) - ls)
                dp = lax.dot_general(dos, vw, nt, preferred_element_type=F32)
                ds = (p * (dp - dl) * scale).astype(BF16)
                dq = jnp.dot(ds, kw, preferred_element_type=F32)
                dq_ref[rows, :] = _unstack_heads(dq, group).astype(BF16)
                win = pl.ds(pl.multiple_of(nb * ATT_BLOCK, ATT_BLOCK), 2 * ATT_BLOCK)
                dk_acc[win, :] += lax.dot_general(ds, qs, tn, preferred_element_type=F32)
                dv_acc[win, :] += lax.dot_general(p.astype(BF16), dos, tn, preferred_element_type=F32)
                return carry

            lax.fori_loop(0, nblk, step, 0, unroll=min(2, nblk))
            dk_ref[...] = dk_acc[ATT_BLOCK:, :]
            dv_ref[...] = dv_acc[ATT_BLOCK:, :]

    kvh = N_KV_HEADS
    band, first = _band_bias(group)
    qspec = pl.BlockSpec((rb, sd, gw), lambda r, h: (r, 0, h))
    kspec = pl.BlockSpec((rb, sd, HEAD_DIM), lambda r, h: (r, 0, h))
    return pl.pallas_call(
        body, name=name, grid=(dil // rb, kvh),
        in_specs=[qspec, kspec, pl.BlockSpec((rb, sd, HEAD_DIM), lambda r, h: (r, 0, kvh + h)),
                  qspec, kspec, kspec,
                  pl.BlockSpec(band.shape, lambda r, h: (0, 0)), pl.BlockSpec(first.shape, lambda r, h: (0, 0))],
        out_specs=[qspec, kspec, kspec],
        out_shape=[jax.ShapeDtypeStruct((dil, sd, d), BF16),
                   jax.ShapeDtypeStruct((dil, sd, kvh * HEAD_DIM), F32),
                   jax.ShapeDtypeStruct((dil, sd, kvh * HEAD_DIM), F32)],
        scratch_shapes=[pltpu.VMEM((rb, sd + ATT_BLOCK, HEAD_DIM), F32)] * 2,
        compiler_params=_params("parallel", "parallel"),
    )(q, kv, kv, do, lse, delta, band, first)


def _cast_bf16(name, w, layer, place, after=None):
    _, r, c = w.shape
    tr = min(512, r)
    deps = [] if after is None else [after]

    def body(pl_ref, w_ref, *refs):
        refs[-1][...] = w_ref[...].astype(BF16)

    return pl.pallas_call(
        body, name=name,
        grid_spec=pltpu.PrefetchScalarGridSpec(
            num_scalar_prefetch=1, grid=(r // tr,),
            in_specs=[pl.BlockSpec((None, tr, c), lambda i, p: (layer, i, 0))] + [ANY] * len(deps),
            out_specs=pl.BlockSpec((None, tr, c), lambda i, p: (p[1], i, 0))),
        out_shape=jax.ShapeDtypeStruct((N_SHARD, r, c), BF16),
        compiler_params=_params("parallel"),
    )(place, w, *deps)


def _chip_sum(name, g, rh, place):
    _, r, c = g.shape
    rh2 = r // 2
    tr = min(512, rh2)
    nb = rh2 // tr

    def body(pl_ref, g_ref, rh_ref, o_ref):
        o_ref[...] = (g_ref[...].astype(F32) + rh_ref[...].astype(F32)).astype(BF16)

    return pl.pallas_call(
        body, name=name,
        grid_spec=pltpu.PrefetchScalarGridSpec(
            num_scalar_prefetch=1, grid=(N_SHARD, nb),
            in_specs=[pl.BlockSpec((None, tr, c), lambda s, i, p: (s, p[0] * nb + i, 0)),
                      pl.BlockSpec((None, tr, c), lambda s, i, p: (s, i, 0))],
            out_specs=pl.BlockSpec((None, tr, c), lambda s, i, p: (s, i, 0))),
        out_shape=jax.ShapeDtypeStruct((N_SHARD, rh2, c), BF16),
        compiler_params=_params("parallel", "parallel"),
    )(place, g, rh)


def _owner_sum(name, cs, rp, place):
    _, rh2, c = cs.shape
    tr = min(512, rh2)
    nb = rh2 // tr

    def body(pl_ref, cs_ref, r0_ref, r1_ref, r2_ref, o_ref):
        o_ref[...] = ((cs_ref[...].astype(F32) + r0_ref[...].astype(F32))
                      + (r1_ref[...].astype(F32) + r2_ref[...].astype(F32)))

    def rspec(j):
        return pl.BlockSpec((None, tr, c), lambda i, p: (j, i, 0))

    return pl.pallas_call(
        body, name=name,
        grid_spec=pltpu.PrefetchScalarGridSpec(
            num_scalar_prefetch=1, grid=(nb,),
            in_specs=[pl.BlockSpec((None, tr, c), lambda i, p: (p[1], i, 0)), rspec(0), rspec(1), rspec(2)],
            out_specs=pl.BlockSpec((tr, c), lambda i, p: (p[0] * nb + i, 0))),
        out_shape=jax.ShapeDtypeStruct((2 * rh2, c), F32),
        compiler_params=_params("parallel"),
    )(place, cs, rp, rp, rp)


def _adam_math(w, g, m, v):
    m = ADAM_B1 * m + (1.0 - ADAM_B1) * g
    v = ADAM_B2 * v + (1.0 - ADAM_B2) * (g * g)
    m_hat = m / (1.0 - ADAM_B1 ** ADAM_STEP)
    v_hat = v / (1.0 - ADAM_B2 ** ADAM_STEP)
    delta = -ADAM_LR * (m_hat / (jnp.sqrt(v_hat) + ADAM_EPS) + ADAM_WD * w)
    return delta, m, v


def _adamw(name, w, m, v, g, layer, partial=None):
    nl, r, c = w.shape
    tr = min(256, r)

    def body(w_ref, m_ref, v_ref, g_ref, *refs):
        go_ref, d_ref, mo_ref, vo_ref = refs[-4:]
        gv = g_ref[...]
        delta, m_new, v_new = _adam_math(w_ref[...], gv, m_ref[...], v_ref[...])
        go_ref[...] = gv
        d_ref[...] = delta
        mo_ref[...] = m_new
        vo_ref[...] = v_new

    wspec = pl.BlockSpec((None, tr, c), lambda i: (layer, i, 0))
    prev = [] if partial is None else list(partial)
    return pl.pallas_call(
        body, name=name, grid=(r // tr,),
        in_specs=[wspec] * 3 + [pl.BlockSpec((tr, c), lambda i: (i, 0))] + [ANY] * len(prev),
        out_specs=[wspec] * 4,
        out_shape=[jax.ShapeDtypeStruct((nl, r, c), F32)] * 4,
        input_output_aliases={4 + i: i for i in range(len(prev))},
        compiler_params=_params("parallel"),
    )(w, m, v, g, *prev)


def _adam_small(ws, ms, vs, gs):
    n = len(ws)

    def body(*refs):
        w_refs, m_refs, v_refs, g_refs = refs[:n], refs[n:2 * n], refs[2 * n:3 * n], refs[3 * n:4 * n]
        d_refs, mo_refs, vo_refs = refs[4 * n:5 * n], refs[5 * n:6 * n], refs[6 * n:7 * n]
        for i in range(n):
            delta, m_new, v_new = _adam_math(w_refs[i][...], g_refs[i][...], m_refs[i][...], v_refs[i][...])
            d_refs[i][...] = delta
            mo_refs[i][...] = m_new
            vo_refs[i][...] = v_new

    shapes = [jax.ShapeDtypeStruct(w.shape, F32) for w in ws]
    res = pl.pallas_call(body, name="adam_small", out_shape=shapes * 3)(*ws, *ms, *vs, *gs)
    return res[:n], res[n:2 * n], res[2 * n:]


def _pack_small(b_in, w_dw, b_dw, ln_g, ln_b, b_out, place):
    cin = b_in.shape[1]
    cd = b_dw.shape[1]
    rows = 8 + CONV_PAD

    def body(pl_ref, bi, wd, bd, lg, lb, bo, out):
        out[...] = jnp.zeros_like(out)
        out[0:1, :] = bi[...]
        out[1:2, 0:cd] = bd[...]
        out[1:2, cd:2 * cd] = lg[...]
        out[2:3, 0:cd] = lb[...]
        out[2:3, cd:2 * cd] = bo[...]
        out[8:8 + CONV_WIDTH, 0:cd] = wd[...]

    def whole(arr):
        return pl.BlockSpec(arr.shape, lambda i, p: (0,) * arr.ndim)

    ins = [b_in, w_dw, b_dw, ln_g, ln_b, b_out]
    return pl.pallas_call(
        body, name="pack_small",
        grid_spec=pltpu.PrefetchScalarGridSpec(
            num_scalar_prefetch=1, grid=(1,), in_specs=[whole(a) for a in ins],
            out_specs=pl.BlockSpec((None, rows, cin), lambda i, p: (p[1], 0, 0))),
        out_shape=jax.ShapeDtypeStruct((N_SHARD, rows, cin), F32),
        compiler_params=_params("arbitrary"),
    )(place, *ins)


def _place():
    x, y, c = lax.axis_index("x"), lax.axis_index("y"), lax.axis_index("c")
    return x, y, c


def _other_chips(x, y):
    return [(1 - x, y), (x, 1 - y), (1 - x, 1 - y)]


def _split_start(name, bufs, n_sem, copies, after=None):
    n = len(bufs)
    deps = [] if after is None else [after]

    def body(*refs):
        out0 = n + len(deps)
        for cp in copies(refs[:n], refs[out0], refs[out0 + 1], False):
            cp.start()
        refs[-1][...] = jnp.zeros_like(refs[-1])

    res = pl.pallas_call(
        body, name=name,
        out_shape=(pltpu.SemaphoreType.DMA((n_sem,)), pltpu.SemaphoreType.DMA((n_sem,)),
                   *[pltpu.HBM(b.shape, b.dtype) for b in bufs], jax.ShapeDtypeStruct((8, LANES), F32)),
        in_specs=[HBM] * n + [ANY] * len(deps),
        out_specs=(SEM, SEM, *[HBM] * n, pl.BlockSpec(memory_space=pltpu.VMEM)),
        input_output_aliases={i: 2 + i for i in range(n)},
        compiler_params=pltpu.CompilerParams(has_side_effects=SPLIT_EFFECT),
    )(*[pltpu.with_memory_space_constraint(b, pltpu.HBM) for b in bufs], *deps)
    return res[0], res[1], list(res[2:2 + n]), res[-1]


def _split_wait(name, handle, copies, after):
    ssem, rsem, bufs, _ = handle
    n = len(bufs)
    deps = list(after) if isinstance(after, (list, tuple)) else [after]

    def body(*refs):
        for cp in copies(refs[:n], refs[n], refs[n + 1], True):
            cp.wait_send()
            cp.wait_recv()

    res = pl.pallas_call(
        body, name=name,
        out_shape=[pltpu.HBM(b.shape, b.dtype) for b in bufs],
        in_specs=[HBM] * n + [SEM, SEM] + [ANY] * len(deps), out_specs=[HBM] * n,
        input_output_aliases={i: i for i in range(n)},
        compiler_params=pltpu.CompilerParams(has_side_effects=SPLIT_EFFECT),
    )(*bufs, ssem, rsem, *deps)
    return list(res)


def _remote(src, dst, ssem, rsem, k, to):
    return pltpu.make_async_remote_copy(src_ref=src, dst_ref=dst, send_sem=ssem.at[k], recv_sem=rsem.at[k],
                                        device_id=to, device_id_type=MESH)


def _gather_chips(x, y, c):
    nx, ny = x + (1 - c) - 2 * x * (1 - c), y + c - 2 * y * c
    fx, fy = x + c - 2 * x * c, y + (1 - c) - 2 * y * (1 - c)
    return (nx, ny), (fx, fy), 2 * nx + ny, 2 * fx + fy, 2 * (1 - x) + (1 - y)


def _direct_copies(refs, ssem, rsem, landing, n_whole=0):
    x, y, c = _place()
    me = 2 * x + y
    (nx, ny), _, near, _, _ = _gather_chips(x, y, c)
    n = len(refs) - n_whole
    cps = []
    for a, ref in enumerate(refs[:n]):
        cps.append(_remote(ref.at[me], ref.at[near if landing else me], ssem, rsem, a, (nx, ny, c)))
    for b, ref in enumerate(refs[n:]):
        for j, (px, py) in enumerate(_other_chips(x, y)):
            cps.append(_remote(ref.at[me], ref.at[2 * px + py if landing else me], ssem, rsem, n + 3 * b + j,
                               (px, py, c)))
    return cps


def _relay_copies(refs, ssem, rsem, landing):
    x, y, c = _place()
    _, (fx, fy), near, far, diag = _gather_chips(x, y, c)
    n = len(refs)
    cps = []
    for a, ref in enumerate(refs):
        rh = ref.shape[1] // 2
        rows = pl.ds(c * rh, rh)
        cps.append(_remote(ref.at[near, rows], ref.at[diag if landing else near, rows], ssem, rsem, a, (fx, fy, c)))
        cps.append(_remote(ref.at[near], ref.at[far if landing else near], ssem, rsem, n + a, (x, y, 1 - c)))
    return cps


def _diagonal_copies(refs, ssem, rsem, landing):
    x, y, c = _place()
    diag = 2 * (1 - x) + (1 - y)
    who = 1 - c if landing else c
    cps = []
    for a, ref in enumerate(refs):
        rh = ref.shape[1] // 2
        piece = ref.at[diag, pl.ds(who * rh, rh)]
        cps.append(_remote(piece, piece, ssem, rsem, a, (x, y, 1 - c)))
    return cps


def _sibling_copies(refs, ssem, rsem, landing):
    x, y, c = _place()
    n = len(refs) // 2
    cps = []
    for a in range(n):
        rh = refs[a].shape[1] // 2
        cps.append(_remote(refs[a].at[:, pl.ds((1 - c) * rh, rh), :], refs[n + a], ssem, rsem, a, (x, y, 1 - c)))
    return cps


def _owner_copies(refs, ssem, rsem, landing):
    x, y, c = _place()
    n = len(refs) // 2
    cps = []
    for a in range(n):
        for j, (px, py) in enumerate(_other_chips(x, y)):
            cps.append(_remote(refs[a].at[2 * px + py], refs[n + a].at[j], ssem, rsem, 3 * a + j, (px, py, c)))
    return cps


def _swap_copies(refs, ssem, rsem, landing):
    x, y, c = _place()
    who = 1 - c if landing else c
    cps = []
    for a, ref in enumerate(refs):
        rh = ref.shape[0] // 2
        rows = ref.at[pl.ds(who * rh, rh)]
        cps.append(_remote(rows, rows, ssem, rsem, a, (x, y, 1 - c)))
    return cps


def _small_copies(refs, ssem, rsem, landing):
    pack, slots = refs
    x, y, c = _place()
    cps = []
    for rel in range(1, N_DEV):
        px = 1 - x if (rel >> 2) & 1 else x
        py = 1 - y if (rel >> 1) & 1 else y
        pc = 1 - c if rel & 1 else c
        slot = 4 * px + 2 * py + pc if landing else 4 * x + 2 * y + c
        cps.append(_remote(pack, slots.at[slot], ssem, rsem, rel - 1, (px, py, pc)))
    return cps


def _small_pack(rows, w_dw_grad, d):
    n = len(rows)

    def body(*refs):
        pack = refs[-1]
        pack[...] = jnp.zeros_like(pack)
        for (r, _), ref in zip(rows, refs[:n]):
            pack[r:r + 1, :] = ref[...]
        pack[16:16 + CONV_PAD, :] = refs[n][...]

    return pl.pallas_call(body, name="small_pack", out_shape=jax.ShapeDtypeStruct((SMALL_ROWS, d), F32))(
        *[v for _, v in rows], w_dw_grad)


def _small_sum(pack, slots, place):
    rows, d = pack.shape
    loss_row = 12

    def body(pl_ref, pack_ref, slots_ref, out_ref):
        me = pl_ref[2]
        tot = jnp.where(me == 0, pack_ref[...], slots_ref[0])
        for i in range(1, N_DEV):
            tot = tot + jnp.where(me == i, pack_ref[...], slots_ref[i])
        out_ref[...] = tot
        out_ref[loss_row:loss_row + 1, :] = jnp.zeros((1, d), F32) + jnp.sum(tot[loss_row:loss_row + 1, :])

    return pl.pallas_call(
        body, name="small_sum",
        grid_spec=pltpu.PrefetchScalarGridSpec(
            num_scalar_prefetch=1, grid=(1,),
            in_specs=[pl.BlockSpec((rows, d), lambda i, p: (0, 0)), pl.BlockSpec((N_DEV, rows, d), lambda i, p: (0, 0, 0))],
            out_specs=pl.BlockSpec((rows, d), lambda i, p: (0, 0))),
        out_shape=jax.ShapeDtypeStruct((rows, d), F32),
        compiler_params=_params("arbitrary"),
    )(place, pack, slots)


def kernel(x, norm_mix, norm_mlp, conv_w_in, conv_b_in, conv_w_dw, conv_b_dw, conv_ln_g, conv_ln_b, conv_w_out, conv_b_out, kv_norm, w_kv, attn_w_q, attn_w_o, mlp_w_in, mlp_w_out, final_norm, loss_target, m_norm_mix, m_norm_mlp, m_conv_w_in, m_conv_b_in, m_conv_w_dw, m_conv_b_dw, m_conv_ln_g, m_conv_ln_b, m_conv_w_out, m_conv_b_out, m_kv_norm, m_w_kv, m_attn_w_q, m_attn_w_o, m_mlp_w_in, m_mlp_w_out, m_final_norm, v_norm_mix, v_norm_mlp, v_conv_w_in, v_conv_b_in, v_conv_w_dw, v_conv_b_dw, v_conv_ln_g, v_conv_ln_b, v_conv_w_out, v_conv_b_out, v_kv_norm, v_w_kv, v_attn_w_q, v_attn_w_o, v_mlp_w_in, v_mlp_w_out, v_final_norm):
    _, s, d = x.shape
    dff = mlp_w_in.shape[2] * N_SHARD
    kvw = w_kv.shape[1]
    nh = d // HEAD_DIM
    group = nh // N_KV_HEADS
    ds4 = d // N_SHARD
    xi, yi, ci = _place()
    me = 2 * xi + yi
    place = jnp.stack([ci, me, 2 * me + ci]).astype(I32)

    h0 = x.reshape(s, d)
    target = loss_target.reshape(s, d)
    tabs = _rope_tables(s)

    def gather_begin(tag, bufs, n_whole=0):
        plan = functools.partial(_direct_copies, n_whole=n_whole)
        return _split_start(f"gather_start_{tag}", bufs, len(bufs) + 2 * n_whole, plan), plan, n_whole

    def gather_land(tag, begun, later):
        handle, plan, n_whole = begun
        bufs = _split_wait(f"gather_wait_{tag}", handle, plan, later)
        n = len(bufs) - n_whole
        return _split_start(f"relay_start_{tag}", bufs[:n], 2 * n, _relay_copies), bufs[n:]

    def gather_swap(tag, landed, later):
        relayed, whole = landed
        bufs = _split_wait(f"relay_wait_{tag}", relayed, _relay_copies, later)
        return _split_start(f"diagonal_start_{tag}", bufs, len(bufs), _diagonal_copies), whole

    def gather_end(tag, swapped, later):
        handle, whole = swapped
        return _split_wait(f"diagonal_wait_{tag}", handle, _diagonal_copies, later) + whole

    def tied(vec, begun):
        return vec + begun[0][3][0:1, 0:1]

    ag_cin = gather_begin("conv_in", [
        _cast_bf16("cast_w_in", conv_w_in, 0, place),
        _pack_small(conv_b_in, conv_w_dw.reshape(CONV_WIDTH, ds4), conv_b_dw, conv_ln_g, conv_ln_b, conv_b_out, place),
    ], n_whole=1)
    ag_cout = gather_begin("conv_out", [_cast_bf16("cast_w_out", conv_w_out, 0, place, ag_cin[0][3])])
    ag_mi0 = gather_begin("mlp_in0", [_cast_bf16("cast_mlp_in0", mlp_w_in, 0, place, ag_cout[0][3])])
    ag_mo0 = gather_begin("mlp_out0", [_cast_bf16("cast_mlp_out0", mlp_w_out, 0, place, ag_mi0[0][3])])
    nm = [norm_mix[0:1], norm_mix[1:2]]
    nmlp = [norm_mlp[0:1], norm_mlp[1:2]]
    kvn = kv_norm.reshape(1, d)
    fin = final_norm.reshape(1, d)
    (y0,) = _rms_fwd("rms_mix0", h0, [tied(nm[0], ag_mo0)])
    land_cin = gather_land("conv_in", ag_cin, y0)
    ag_attn = gather_begin("attn", [
        _cast_bf16("cast_w_kv", w_kv.reshape(1, ds4, kvw), 0, place, land_cin[0][3]),
        _cast_bf16("cast_w_q", attn_w_q, 0, place), _cast_bf16("cast_w_o", attn_w_o, 0, place)])
    ag_mi1 = gather_begin("mlp_in1", [_cast_bf16("cast_mlp_in1", mlp_w_in, 1, place, ag_attn[0][3])])
    ag_mo1 = gather_begin("mlp_out1", [_cast_bf16("cast_mlp_out1", mlp_w_out, 1, place, ag_mi1[0][3])])
    land_cout = gather_land("conv_out", ag_cout, ag_mo1[0][3])

    wmi_g = [None, None]
    wmo_f = [None, None]

    swap_cin = gather_swap("conv_in", land_cin, land_cout[0][3])
    w_in_g, small_g = gather_end("conv_in", swap_cin, swap_cin[0][3])
    b_in_f = small_g[:, 0, :].reshape(1, 2 * d)
    b_dw_f = small_g[:, 1, 0:ds4].reshape(1, d)
    ln_g_f = small_g[:, 1, ds4:2 * ds4].reshape(1, d)
    ln_b_f = small_g[:, 2, 0:ds4].reshape(1, d)
    b_out_f = small_g[:, 2, ds4:2 * ds4].reshape(1, d)
    w_dw_f = jnp.transpose(small_g[:, 8:8 + CONV_PAD, 0:ds4], (1, 0, 2)).reshape(CONV_PAD, d)

    def ep_bias(acc, ex, outs, j):
        outs[0][...] = (acc + ex[0][...]).astype(outs[0].dtype)

    def ep_residual(acc, ex, outs, j):
        outs[0][...] = ex[0][...] + acc

    def ep_residual_bias(acc, ex, outs, j):
        outs[0][...] = ex[0][...] + (acc + ex[1][...])

    def ep_relu2(acc, ex, outs, j):
        r = jnp.maximum(acc, 0.0)
        outs[0][...] = r.astype(BF16)
        outs[1][...] = (r * r).astype(BF16)

    by_residue = [(BF16, ("residues", dil)) for dil in DILATIONS]

    def put_by_residue(val, outs, stage):
        _to_residues(val, stage, outs, DILATIONS)

    def ep_rope(acc, ex, outs, j, stage):
        put_by_residue(_rope_apply(acc, ex[0][...], ex[1][...], ex[2][...], 1.0), outs, stage)

    def ep_rope_k(acc, ex, outs, j, stage):
        roped = _rope_apply(acc, ex[0][...], ex[1][...], ex[2][...], 1.0)
        put_by_residue(jnp.where(j == 0, roped, acc), outs, stage)

    def ep_by_residue(acc, ex, outs, j, stage):
        put_by_residue(acc, outs, stage)

    tab_extras = [(t, "rows") for t in tabs]

    def mlp_fwd(idx, h, y, out_weight):
        r, r2 = _matmul(f"mlp_in{idx}", "nn", y, wmi_g[idx], b_kind="col", m=s, n=dff, k=d,
                        outs=[(BF16, "plain"), (BF16, "plain")], epilogue=ep_relu2)
        wmo_f[idx] = out_weight(r2).reshape(dff, d)
        (h_new,) = _matmul(f"mlp_out{idx}", "nn", r2, wmo_f[idx], m=s, n=d, k=dff,
                           outs=[(F32, "plain")], extras=[(h, "ij")], epilogue=ep_residual)
        return h_new, r, r2

    (u,) = _matmul("conv_in", "nn", y0, w_in_g, b_kind="col", m=s, n=2 * d, k=d,
                   outs=[(BF16, "plain")], extras=[(b_in_f, "vec")], epilogue=ep_bias)
    land_mi0 = gather_land("mlp_in0", ag_mi0, u)
    swap_cout = gather_swap("conv_out", land_cout, land_mi0[0][3])
    cpre = _dwconv_fwd(u, w_dw_f, tied(b_dw_f, swap_cout))
    sact = _ln_silu_fwd(cpre, ln_g_f, ln_b_f)
    (w_out_g,) = gather_end("conv_out", swap_cout, sact)
    w_out_f = w_out_g.reshape(d, d)
    (h1,) = _matmul("conv_out", "nn", sact, w_out_f, m=s, n=d, k=d,
                    outs=[(F32, "plain")], extras=[(h0, "ij"), (b_out_f, "vec")], epilogue=ep_residual_bias)
    swap_mi0 = gather_swap("mlp_in0", land_mi0, h1)
    (y1,) = _rms_fwd("rms_mlp0", h1, [tied(nmlp[0], swap_mi0)])
    land_mo0 = gather_land("mlp_out0", ag_mo0, y1)
    (wmi_g[0],) = gather_end("mlp_in0", swap_mi0, land_mo0[0][3])
    land_attn = None

    def out_weight0(r2):
        nonlocal land_attn
        land_attn = gather_land("attn", ag_attn, r2)
        swap_mo0 = gather_swap("mlp_out0", land_mo0, land_attn[0][3])
        return gather_end("mlp_out0", swap_mo0, swap_mo0[0][3])[0]

    h2, r0, r0sq = mlp_fwd(0, h1, y1, out_weight0)
    swap_attn = gather_swap("attn", land_attn, h2)
    land_mi1 = gather_land("mlp_in1", ag_mi1, swap_attn[0][3])
    ykv, y2 = _rms_fwd("rms_kv_mix1", h2, [tied(kvn, land_mi1), nm[1]])
    wkv_g, wq_g, wo_g = gather_end("attn", swap_attn, y2)
    wkv_f, wq_f, wo_f = wkv_g.reshape(d, kvw), wq_g.reshape(d, d), wo_g.reshape(d, d)
    kv_parts = _matmul("kv_proj", "nn", ykv, wkv_f, m=s, n=kvw, k=d, tn=kvw // 2,
                       outs=by_residue, extras=tab_extras, epilogue=ep_rope_k, stage=True)
    q_parts = _matmul("q_proj", "nn", y2, wq_f, m=s, n=d, k=d,
                      outs=by_residue, extras=tab_extras, epilogue=ep_rope, stage=True)
    o_parts, lse_parts = [], []
    for dil, q_b, kv_b in zip(DILATIONS, q_parts, kv_parts):
        o_b, lse_b = _attn_fwd(f"attn_fwd_d{dil}", q_b, kv_b)
        o_parts.append(o_b)
        lse_parts.append(lse_b)
    swap_mi1 = gather_swap("mlp_in1", land_mi1, o_parts)
    o, lse = _attn_combine(o_parts, lse_parts)
    land_mo1 = gather_land("mlp_out1", ag_mo1, o)
    (h3,) = _matmul("attn_out", "nn", o, wo_f, m=s, n=d, k=d,
                    outs=[(F32, "plain")], extras=[(h2, "ij")], epilogue=ep_residual)
    (y3,) = _rms_fwd("rms_mlp1", h3, [tied(nmlp[1], land_mo1)])
    (wmi_g[1],) = gather_end("mlp_in1", swap_mi1, y3)

    def out_weight1(r2):
        swap_mo1 = gather_swap("mlp_out1", land_mo1, r2)
        return gather_end("mlp_out1", swap_mo1, swap_mo1[0][3])[0]

    h4, r1, r1sq = mlp_fwd(1, h3, y3, out_weight1)
    dh4, dh4b, d_fin, loss_cols = _final_loss(h4, fin, target)

    def ep_relu2_bwd(acc, ex, outs, j):
        outs[0][...] = (acc * (2.0 * ex[0][...].astype(F32))).astype(BF16)

    def mlp_bwd(idx, dhb, y, r, r2):
        (dz,) = _matmul(f"mlp_out{idx}_dx", "nt", dhb, wmo_f[idx], m=s, n=dff, k=d,
                        outs=[(BF16, "plain")], extras=[(r, "ij")], epilogue=ep_relu2_bwd)
        (dwo,) = _matmul(f"mlp_out{idx}_dw", "tn", r2, dhb, m=dff, n=d, k=s,
                         outs=[(BF16, "plain")])
        (dy,) = _matmul(f"mlp_in{idx}_dx", "nt", dz, wmi_g[idx], b_kind="col", m=s, n=d, k=dff,
                        outs=[(BF16, "plain")])
        (dwi,) = _matmul(f"mlp_in{idx}_dw", "tn", y, dz, m=d, n=dff, k=s,
                         outs=[(BF16, "col")])
        return dy, dwi, dwo.reshape(N_SHARD, dff // N_SHARD, d)

    def token(handle):
        return handle[3][0:1, 0:1]

    def rs_exchange(tag, grads):
        lands = [lax.empty((N_SHARD, g.shape[1] // 2, g.shape[2]), g.dtype) for g in grads]
        return _split_start(f"sibling_start_{tag}", list(grads) + lands, len(grads), _sibling_copies)

    def rs_send(tag, names, exchanged, later):
        bufs = _split_wait(f"sibling_wait_{tag}", exchanged, _sibling_copies, later)
        n = len(names)
        sums = [_chip_sum(f"chip_sum_{nme}", g, rh, place) for nme, g, rh in zip(names, bufs[:n], bufs[n:])]
        lands = [lax.empty((N_SHARD - 1,) + cs.shape[1:], cs.dtype) for cs in sums]
        return _split_start(f"owners_start_{tag}", sums + lands, 3 * n, _owner_copies)

    def rs_sum(tag, names, sent, later):
        bufs = _split_wait(f"owners_wait_{tag}", sent, _owner_copies, later)
        n = len(names)
        own = [_owner_sum(f"owner_sum_{nme}", cs, rp, place) for nme, cs, rp in zip(names, bufs[:n], bufs[n:])]
        return _split_start(f"swap_start_{tag}", own, n, _swap_copies)

    def rs_end(tag, swapped, later):
        return _split_wait(f"swap_wait_{tag}", swapped, _swap_copies, later)

    dy3, g_wmi1, g_wmo1 = mlp_bwd(1, dh4b, y3, r1, r1sq)
    x_mlp1 = rs_exchange("mlp1", [g_wmi1, g_wmo1])
    dh3, dh3b, d_nmlp1 = _rms_bwd("rms_mlp1_bwd", h3, [(nmlp[1] + token(x_mlp1), dy3)], dh4)

    do_parts = _matmul("attn_out_dx", "nt", dh3b, wo_f, m=s, n=d, k=d, outs=by_residue, epilogue=ep_by_residue,
                       stage=True)
    (g_wo,) = _matmul("attn_out_dw", "tn", o, dh3b, m=d, n=d, k=s, outs=[(BF16, "plain")])
    rs_mlp1 = rs_send("mlp1", ["mlp_in1", "mlp_out1"], x_mlp1, g_wo)
    lse_res, delta_res = _attn_delta(do_parts[0].reshape(s, d), o, lse, DILATIONS)
    dq_parts, dk_parts, dv_parts = [], [], []
    for dil, q_b, kv_b, do_b, lse_b, dl_b in zip(DILATIONS, q_parts, kv_parts, do_parts, lse_res, delta_res):
        dq_b, dk_b, dv_b = _attn_bwd(f"attn_bwd_d{dil}", q_b, kv_b, do_b, lse_b, dl_b)
        dq_parts.append(dq_b)
        dk_parts.append(dk_b)
        dv_parts.append(dv_b)
    dq = _residue_sum("rope_bwd_q", [(dq_parts, True)], tabs)
    dkv = _residue_sum("rope_bwd_kv", [(dk_parts, True), (dv_parts, False)], tabs)
    (g_wq,) = _matmul("q_proj_dw", "tn", y2, dq, m=d, n=d, k=s, outs=[(BF16, "plain")])
    (dy2,) = _matmul("q_proj_dx", "nt", dq, wq_f, m=s, n=d, k=d, outs=[(BF16, "plain")])
    (g_wkv,) = _matmul("kv_proj_dw", "tn", ykv, dkv, m=d, n=kvw, k=s, outs=[(BF16, "plain")])
    (dykv,) = _matmul("kv_proj_dx", "nt", dkv, wkv_f, m=s, n=d, k=kvw, outs=[(BF16, "plain")])
    x_attn = rs_exchange("attn", [g_wkv.reshape(N_SHARD, ds4, kvw), g_wq.reshape(N_SHARD, ds4, d),
                                  g_wo.reshape(N_SHARD, ds4, d)])
    dh2, dh2b, d_nm1, d_kvn = _rms_bwd("rms_kv_mix1_bwd", h2, [(nm[1] + token(x_attn), dy2), (kvn, dykv)], dh3)
    rs_attn = rs_send("attn", ["w_kv", "w_q", "w_o"], x_attn, dh2b)

    dy1, g_wmi0, g_wmo0 = mlp_bwd(0, dh2b, y1, r0, r0sq)
    x_mlp0 = rs_exchange("mlp0", [g_wmi0, g_wmo0])
    dh1, dh1b, d_nmlp0, d_b_out = _rms_bwd("rms_mlp0_bwd", h1, [(nmlp[0] + token(x_mlp0) + token(rs_attn), dy1)],
                                           dh2, want_colsum=True)

    (dsact,) = _matmul("conv_out_dx", "nt", dh1b, w_out_f, m=s, n=d, k=d, outs=[(BF16, "plain")])
    (g_wout,) = _matmul("conv_out_dw", "tn", sact, dh1b, m=d, n=d, k=s, outs=[(BF16, "plain")])
    rs_mlp0 = rs_send("mlp0", ["mlp_in0", "mlp_out0"], x_mlp0, g_wout)
    dc, d_ln_g, d_ln_b, d_b_dw = _ln_silu_bwd(cpre, ln_g_f + token(rs_mlp0), ln_b_f, dsact)
    du, d_w_dw, d_b_in_a, d_b_in_g = _dwconv_bwd(u, w_dw_f, dc)
    (g_win,) = _matmul("conv_in_dw", "tn", y0, du, b_kind="col", m=d, n=2 * d, k=s, outs=[(BF16, "col")])
    x_conv = rs_exchange("conv", [g_win, g_wout.reshape(N_SHARD, ds4, d)])
    (dy0,) = _matmul("conv_in_dx", "nt", du, w_in_g, a_kind="col", b_kind="col", m=s, n=d, k=2 * d,
                     outs=[(BF16, "plain")])
    dx, _, d_nm0 = _rms_bwd("rms_mix0_bwd", h0, [(nm[0] + token(x_conv), dy0)], dh1)

    small_rows = [(0, d_nm0), (1, d_nm1), (2, d_nmlp0), (3, d_nmlp1), (4, d_kvn), (5, d_fin), (6, d_b_dw),
                  (7, d_ln_g), (8, d_ln_b), (9, d_b_out), (10, d_b_in_a), (11, d_b_in_g), (12, loss_cols)]
    x_small = _split_start("small_start", [_small_pack(small_rows, d_w_dw, d),
                                           lax.empty((N_DEV, SMALL_ROWS, d), F32)], N_DEV - 1, _small_copies)
    rs_conv = rs_send("conv", ["w_in", "w_out"], x_conv, x_small[3])

    def big(name, w, m, v, g, layer=0, partial=None):
        shape = w.shape
        w3, m3, v3 = [t.reshape((-1,) + shape[-2:]) for t in (w, m, v)]
        if partial is not None:
            partial = [t.reshape(w3.shape) for t in partial]
        res = _adamw(name, w3, m3, v3, g, layer, partial)
        return [t.reshape(shape) for t in res]

    sw_mlp1 = rs_sum("mlp1", ["mlp_in1", "mlp_out1"], rs_mlp1, rs_conv[3])
    sw_attn = rs_sum("attn", ["w_kv", "w_q", "w_o"], rs_attn, sw_mlp1[3])
    f_wmi1, f_wmo1 = rs_end("mlp1", sw_mlp1, sw_attn[3])
    p_wmi = big("adam_mlp_in1", mlp_w_in, m_mlp_w_in, v_mlp_w_in, f_wmi1, 1)
    p_wmo = big("adam_mlp_out1", mlp_w_out, m_mlp_w_out, v_mlp_w_out, f_wmo1, 1)
    sw_mlp0 = rs_sum("mlp0", ["mlp_in0", "mlp_out0"], rs_mlp0, [p_wmi[0], p_wmo[0]])
    f_wkv, f_wq, f_wo = rs_end("attn", sw_attn, sw_mlp0[3])
    r_wkv = big("adam_w_kv", w_kv, m_w_kv, v_w_kv, f_wkv)
    r_wq = big("adam_w_q", attn_w_q, m_attn_w_q, v_attn_w_q, f_wq)
    r_wo = big("adam_w_o", attn_w_o, m_attn_w_o, v_attn_w_o, f_wo)
    sw_conv = rs_sum("conv", ["w_in", "w_out"], rs_conv, [r_wkv[0], r_wq[0], r_wo[0]])
    f_wmi0, f_wmo0 = rs_end("mlp0", sw_mlp0, sw_conv[3])
    r_wmi = big("adam_mlp_in0", mlp_w_in, m_mlp_w_in, v_mlp_w_in, f_wmi0, 0, p_wmi)
    r_wmo = big("adam_mlp_out0", mlp_w_out, m_mlp_w_out, v_mlp_w_out, f_wmo0, 0, p_wmo)
    f_win, f_wout = rs_end("conv", sw_conv, [r_wmi[0], r_wmo[0]])
    r_win = big("adam_w_in", conv_w_in, m_conv_w_in, v_conv_w_in, f_win)
    r_wout = big("adam_w_out", conv_w_out, m_conv_w_out, v_conv_w_out, f_wout)

    small_pack, small_slots = _split_wait("small_wait", x_small, _small_copies, r_wout[0])
    red = _small_sum(small_pack, small_slots, place)
    loss = red[12, 0]
    g_norm_mix = red[0:2]
    g_norm_mlp = red[2:4]
    g_kv_norm = red[4:5]
    g_final = red[5:6]

    def my_cols(row):
        return lax.dynamic_slice(red, (row, me * ds4), (1, ds4))

    g_b_dw, g_ln_g, g_ln_b, g_b_out = my_cols(6), my_cols(7), my_cols(8), my_cols(9)
    half_in = 2 * d // N_SHARD
    b_in_row = 10 + me // 2
    g_b_in = lax.dynamic_slice(red, (b_in_row, (me % 2) * half_in), (1, half_in))
    g_w_dw = lax.dynamic_slice(red, (16, me * ds4), (CONV_WIDTH, ds4))

    sm_w =[norm_mix, norm_mlp, conv_b_in, conv_w_dw.reshape(CONV_WIDTH, ds4), conv_b_dw, conv_ln_g, conv_ln_b,
            conv_b_out, kv_norm.reshape(1, d), final_norm.reshape(1, d)]
    sm_m = [m_norm_mix, m_norm_mlp, m_conv_b_in, m_conv_w_dw.reshape(CONV_WIDTH, ds4), m_conv_b_dw, m_conv_ln_g,
            m_conv_ln_b, m_conv_b_out, m_kv_norm.reshape(1, d), m_final_norm.reshape(1, d)]
    sm_v = [v_norm_mix, v_norm_mlp, v_conv_b_in, v_conv_w_dw.reshape(CONV_WIDTH, ds4), v_conv_b_dw, v_conv_ln_g,
            v_conv_ln_b, v_conv_b_out, v_kv_norm.reshape(1, d), v_final_norm.reshape(1, d)]
    sm_g = [g_norm_mix, g_norm_mlp, g_b_in, g_w_dw, g_b_dw, g_ln_g, g_ln_b, g_b_out, g_kv_norm, g_final]
    sm_d, sm_nm, sm_nv = _adam_small(sm_w, sm_m, sm_v, sm_g)
    shapes = [norm_mix.shape, norm_mlp.shape, conv_b_in.shape, conv_w_dw.shape, conv_b_dw.shape, conv_ln_g.shape,
              conv_ln_b.shape, conv_b_out.shape, kv_norm.shape, final_norm.shape]
    sm_g, sm_d, sm_nm, sm_nv = [[t.reshape(sh) for t, sh in zip(lst, shapes)] for lst in (sm_g, sm_d, sm_nm, sm_nv)]

    def order(sm, idx):
        return [sm[0], sm[1], r_win[idx], sm[2], sm[3], sm[4], sm[5], sm[6], r_wout[idx], sm[7], sm[8],
                r_wkv[idx], r_wq[idx], r_wo[idx], r_wmi[idx], r_wmo[idx], sm[9]]

    return (loss, dx.reshape(x.shape), *order(sm_g, 0), *order(sm_d, 1), *order(sm_nm, 2), *order(sm_nv, 3))
```

```python
import functools
import math

import jax
import jax.numpy as jnp
from jax import lax
from jax.experimental import pallas as pl
from jax.experimental.pallas import tpu as pltpu

F32 = jnp.float32
BF16 = jnp.bfloat16
I32 = jnp.int32

NORM_EPS = 1e-6
LN_EPS = 1e-5
HEAD_DIM = 128
N_KV_HEADS = 4
ROT_DIM = 32
ROPE_THETA = 500000.0
CONV_WIDTH = 31
CONV_PAD = 32
ATT_BLOCK = 128
ATT_STEP_BLOCKS = 16
DILATIONS = (1, 4, 16)
ADAM_LR = 0.001
ADAM_B1 = 0.9
ADAM_B2 = 0.999
ADAM_EPS = 1e-08
ADAM_WD = 0.01
ADAM_STEP = 10
N_SHARD = 4
N_DEV = 8
LANES = 128
VMEM_LIMIT = 48 * 1024 * 1024
MM_TM, MM_TN, MM_TK = 1024, 1024, 2048
ROW_TILE = 512
CONV_CB = 128
CONV_T = 128
SMALL_ROWS = 48
MESH = pl.DeviceIdType.MESH
ANY = pl.BlockSpec(memory_space=pl.ANY)
HBM = pl.BlockSpec(memory_space=pltpu.HBM)
SEM = pl.BlockSpec(memory_space=pltpu.SEMAPHORE)
SPLIT_EFFECT = pltpu.SideEffectType.DATAFLOW_SIDE_EFFECTING


def _params(*sem):
    return pltpu.CompilerParams(dimension_semantics=sem, vmem_limit_bytes=VMEM_LIMIT)


def _sigmoid(x):
    return 1.0 / (1.0 + jnp.exp(-x))


def _wspec(kind, arr_shape, br, bc, pick):
    if kind == "plain":
        return pl.BlockSpec((br, bc), pick)
    per = arr_shape[2] // bc

    def idx(*g):
        rb, cb = pick(*g)
        return (cb // per, rb, cb % per)

    return pl.BlockSpec((None, br, bc), idx)


def _stage_shape(rows, w):
    return (w // LANES, rows, LANES)


def _to_residues(val, stage_ref, out_refs, dils):
    planes, rows, _ = stage_ref.shape
    for c in range(planes):
        stage_ref[c] = val[:, c * LANES:(c + 1) * LANES]
    for out_ref, dil in zip(out_refs, dils):
        if dil == 1:
            out_ref[0] = val.astype(out_ref.dtype)
            continue
        for r in range(dil):
            for c in range(planes):
                out_ref[r, :, c * LANES:(c + 1) * LANES] = stage_ref.at[c][pl.ds(r, rows // dil, stride=dil), :].astype(
                    out_ref.dtype)


def _from_residues(src_ref, stage_ref, dil):
    planes, rows, _ = stage_ref.shape
    if dil == 1:
        return lambda c: src_ref[0, :, c * LANES:(c + 1) * LANES].astype(F32)
    for r in range(dil):
        for c in range(planes):
            stage_ref.at[c][pl.ds(r, rows // dil, stride=dil), :] = src_ref[r, :, c * LANES:(c + 1) * LANES].astype(F32)
    return lambda c: stage_ref[c]


def _matmul(name, mode, a, b, *, m, n, k, tm=MM_TM, tn=MM_TN, tk=MM_TK, a_kind="plain", b_kind="plain", outs,
            extras=(), epilogue=None, stage=False):
    tm, tn, tk = min(tm, m), min(tn, n), min(tk, k)
    if b_kind == "col" and mode in ("nn", "tn"):
        tn = min(tn, n // b.shape[0])
    if b_kind == "col" and mode == "nt":
        tk = min(tk, k // b.shape[0])
    if a_kind == "col":
        assert mode == "nt"
        tk = min(tk, k // a.shape[0])
    if any(kind == "col" for _, kind in outs):
        tn = min(tn, n // N_SHARD)
    assert m % tm == 0 and n % tn == 0 and k % tk == 0, (name, m, n, k, tm, tn, tk)
    nk = k // tk
    grid = (m // tm, n // tn, nk)
    if mode == "nn":
        a_spec = pl.BlockSpec((tm, tk), lambda i, j, kk: (i, kk))
        b_spec = _wspec(b_kind, b.shape, tk, tn, lambda i, j, kk: (kk, j))
        dims = (((1,), (0,)), ((), ()))
    elif mode == "nt":
        a_spec = _wspec(a_kind, a.shape, tm, tk, lambda i, j, kk: (i, kk))
        b_spec = _wspec(b_kind, b.shape, tn, tk, lambda i, j, kk: (j, kk))
        dims = (((1,), (1,)), ((), ()))
    else:
        a_spec = pl.BlockSpec((tk, tm), lambda i, j, kk: (kk, i))
        b_spec = _wspec(b_kind, b.shape, tk, tn, lambda i, j, kk: (kk, j))
        dims = (((0,), (0,)), ((), ()))
    out_shape, out_specs = [], []
    for dtype, kind in outs:
        if isinstance(kind, tuple):
            dil = kind[1]
            out_shape.append(jax.ShapeDtypeStruct((dil, m // dil, n), dtype))
            out_specs.append(pl.BlockSpec((dil, tm // dil, tn), lambda i, j, kk: (0, i, j)))
            continue
        shape = (m, n) if kind == "plain" else (N_SHARD, m, n // N_SHARD)
        out_shape.append(jax.ShapeDtypeStruct(shape, dtype))
        out_specs.append(_wspec(kind, shape, tm, tn, lambda i, j, kk: (i, j)))
    n_ex = len(extras)
    ex_specs = {"ij": pl.BlockSpec((tm, tn), lambda i, j, kk: (i, j)),
                "vec": pl.BlockSpec((1, tn), lambda i, j, kk: (0, j)),
                "rows": pl.BlockSpec((tm, LANES), lambda i, j, kk: (i, 0))}

    def body(*refs):
        a_ref, b_ref = refs[0], refs[1]
        ex_refs = refs[2:2 + n_ex]
        out_refs = refs[2 + n_ex:2 + n_ex + len(outs)]
        j = pl.program_id(1)

        def finish(res):
            if epilogue is None:
                out_refs[0][...] = res.astype(out_refs[0].dtype)
            elif stage:
                epilogue(res, ex_refs, out_refs, j, refs[-1])
            else:
                epilogue(res, ex_refs, out_refs, j)

        prod = lax.dot_general(a_ref[...], b_ref[...], dims, preferred_element_type=F32)
        if nk == 1:
            finish(prod)
            return
        acc_ref = refs[2 + n_ex + len(outs)]
        kk = pl.program_id(2)

        @pl.when(kk == 0)
        def _():
            acc_ref[...] = prod

        @pl.when(kk > 0)
        def _():
            acc_ref[...] += prod

        @pl.when(kk == nk - 1)
        def _():
            finish(acc_ref[...])

    res = pl.pallas_call(
        body, name=name, grid=grid,
        in_specs=[a_spec, b_spec] + [ex_specs[how] for _, how in extras],
        out_specs=out_specs, out_shape=out_shape,
        scratch_shapes=[pltpu.VMEM((tm, tn), F32)] * (nk > 1) + [pltpu.VMEM(_stage_shape(tm, tn), F32)] * bool(stage),
        compiler_params=_params("parallel", "parallel", "arbitrary"),
    )(a, b, *[e for e, _ in extras])
    return res


def _rope_tables(seq):
    half = ROT_DIM // 2
    pos = jnp.arange(seq, dtype=F32)
    inv = ROPE_THETA ** (-jnp.arange(0, ROT_DIM, 2, dtype=F32) / ROT_DIM)
    ang = pos[:, None] * inv[None, :]
    cos, sin = jnp.cos(ang), jnp.sin(ang)
    zeros = jnp.zeros((seq, HEAD_DIM - ROT_DIM), F32)
    ctab = jnp.concatenate([cos, cos, zeros + 1.0], axis=1)
    atab = jnp.concatenate([-sin, jnp.zeros((seq, half), F32), zeros], axis=1)
    btab = jnp.concatenate([jnp.zeros((seq, half), F32), sin, zeros], axis=1)
    return ctab, atab, btab


def _rope_apply(x, ctab, atab, btab, sign):
    w = x.shape[1]
    reps = w // HEAD_DIM
    half = ROT_DIM // 2
    c = jnp.tile(ctab, (1, reps))
    a = jnp.tile(atab, (1, reps))
    b = jnp.tile(btab, (1, reps))
    up = pltpu.roll(x, w - half, 1)
    down = pltpu.roll(x, half, 1)
    return x * c + sign * (up * a + down * b)


def _rows(t, w):
    return pl.BlockSpec((t, w), lambda i: (i, 0))


def _fixed(shape):
    nd = len(shape)
    return pl.BlockSpec(shape, lambda i: (0,) * nd)


def _rms_fwd(name, x, gains):
    s, d = x.shape
    t = min(ROW_TILE, s)
    ng = len(gains)

    def body(x_ref, *refs):
        xv = x_ref[...]
        r = lax.rsqrt(jnp.mean(xv * xv, axis=-1, keepdims=True) + NORM_EPS)
        xn = xv * r
        for g_ref, y_ref in zip(refs[:ng], refs[ng:]):
            y_ref[...] = (xn * g_ref[...]).astype(BF16)

    return pl.pallas_call(
        body, name=name, grid=(s // t,),
        in_specs=[_rows(t, d)] + [_fixed((1, d))] * ng,
        out_specs=[_rows(t, d)] * ng,
        out_shape=[jax.ShapeDtypeStruct((s, d), BF16)] * ng,
        compiler_params=_params("parallel"),
    )(x, *gains)


def _rms_bwd(name, x, pairs, dh_in, want_colsum=False):
    s, d = x.shape
    n_p = len(pairs)
    t = min(ROW_TILE // n_p, s)

    def body(x_ref, dh_ref, *refs):
        g_refs = refs[:n_p]
        dy_refs = refs[n_p:2 * n_p]
        dh_out, dhb_out = refs[2 * n_p], refs[2 * n_p + 1]
        dg_refs = refs[2 * n_p + 2:2 * n_p + 2 + n_p]
        cs_ref = refs[-1] if want_colsum else None
        i = pl.program_id(0)
        xv = x_ref[...]
        r = lax.rsqrt(jnp.mean(xv * xv, axis=-1, keepdims=True) + NORM_EPS)
        xn = xv * r
        dh = dh_ref[...]
        for g_ref, dy_ref, dg_ref in zip(g_refs, dy_refs, dg_refs):
            dy = dy_ref[...].astype(F32)
            u = dy * g_ref[...]
            dh = dh + r * (u - xn * jnp.mean(u * xn, axis=-1, keepdims=True))
            part = jnp.sum(dy * xn, axis=0, keepdims=True)

            @pl.when(i == 0)
            def _():
                dg_ref[...] = part

            @pl.when(i > 0)
            def _():
                dg_ref[...] += part

        dh_out[...] = dh
        dhb_out[...] = dh.astype(BF16)
        if want_colsum:
            col = jnp.sum(dh, axis=0, keepdims=True)

            @pl.when(i == 0)
            def _():
                cs_ref[...] = col

            @pl.when(i > 0)
            def _():
                cs_ref[...] += col

    n_vec = n_p + (1 if want_colsum else 0)
    return pl.pallas_call(
        body, name=name, grid=(s // t,),
        in_specs=[_rows(t, d), _rows(t, d)] + [_fixed((1, d))] * n_p + [_rows(t, d)] * n_p,
        out_specs=[_rows(t, d), _rows(t, d)] + [_fixed((1, d))] * n_vec,
        out_shape=[jax.ShapeDtypeStruct((s, d), F32), jax.ShapeDtypeStruct((s, d), BF16)]
        + [jax.ShapeDtypeStruct((1, d), F32)] * n_vec,
        compiler_params=_params("arbitrary"),
    )(x, dh_in, *[g for g, _ in pairs], *[dy for _, dy in pairs])


def _final_loss(x, g, target):
    s, d = x.shape
    t = min(ROW_TILE, s)

    def body(x_ref, g_ref, t_ref, dh_out, dhb_out, dg_ref, loss_ref):
        i = pl.program_id(0)
        xv = x_ref[...]
        gv = g_ref[...]
        r = lax.rsqrt(jnp.mean(xv * xv, axis=-1, keepdims=True) + NORM_EPS)
        xn = xv * r
        diff = xn * gv - t_ref[...]
        dy = diff / d
        u = dy * gv
        dh = r * (u - xn * jnp.mean(u * xn, axis=-1, keepdims=True))
        dh_out[...] = dh
        dhb_out[...] = dh.astype(BF16)
        dg = jnp.sum(dy * xn, axis=0, keepdims=True)
        lc = jnp.sum(0.5 * diff * dy, axis=0, keepdims=True)

        @pl.when(i == 0)
        def _():
            dg_ref[...] = dg
            loss_ref[...] = lc

        @pl.when(i > 0)
        def _():
            dg_ref[...] += dg
            loss_ref[...] += lc

    return pl.pallas_call(
        body, name="final_loss", grid=(s // t,),
        in_specs=[_rows(t, d), _fixed((1, d)), _rows(t, d)],
        out_specs=[_rows(t, d), _rows(t, d), _fixed((1, d)), _fixed((1, d))],
        out_shape=[jax.ShapeDtypeStruct((s, d), F32), jax.ShapeDtypeStruct((s, d), BF16),
                   jax.ShapeDtypeStruct((1, d), F32), jax.ShapeDtypeStruct((1, d), F32)],
        compiler_params=_params("arbitrary"),
    )(x, g, target)


def _ln_silu_fwd(c, g, b):
    s, d = c.shape
    t = min(ROW_TILE, s)

    def body(c_ref, g_ref, b_ref, s_ref):
        cv = c_ref[...]
        mu = jnp.mean(cv, axis=-1, keepdims=True)
        xc = cv - mu
        rs = lax.rsqrt(jnp.mean(xc * xc, axis=-1, keepdims=True) + LN_EPS)
        ln = xc * rs * g_ref[...] + b_ref[...]
        s_ref[...] = (ln * _sigmoid(ln)).astype(BF16)

    return pl.pallas_call(
        body, name="ln_silu_fwd", grid=(s // t,),
        in_specs=[_rows(t, d), _fixed((1, d)), _fixed((1, d))],
        out_specs=_rows(t, d), out_shape=jax.ShapeDtypeStruct((s, d), BF16),
        compiler_params=_params("parallel"),
    )(c, g, b)


def _ln_silu_bwd(c, g, b, ds):
    s, d = c.shape
    t = min(ROW_TILE, s)

    def body(c_ref, g_ref, b_ref, ds_ref, dc_ref, dg_ref, db_ref, dbdw_ref):
        i = pl.program_id(0)
        cv = c_ref[...]
        gv = g_ref[...]
        mu = jnp.mean(cv, axis=-1, keepdims=True)
        xc = cv - mu
        rs = lax.rsqrt(jnp.mean(xc * xc, axis=-1, keepdims=True) + LN_EPS)
        nrm = xc * rs
        ln = nrm * gv + b_ref[...]
        sig = _sigmoid(ln)
        dln = ds_ref[...].astype(F32) * sig * (1.0 + ln * (1.0 - sig))
        dn = dln * gv
        dc = rs * (dn - jnp.mean(dn, axis=-1, keepdims=True)
                   - nrm * jnp.mean(dn * nrm, axis=-1, keepdims=True))
        dc_ref[...] = dc
        pg = jnp.sum(dln * nrm, axis=0, keepdims=True)
        pb = jnp.sum(dln, axis=0, keepdims=True)
        pc = jnp.sum(dc, axis=0, keepdims=True)

        @pl.when(i == 0)
        def _():
            dg_ref[...] = pg
            db_ref[...] = pb
            dbdw_ref[...] = pc

        @pl.when(i > 0)
        def _():
            dg_ref[...] += pg
            db_ref[...] += pb
            dbdw_ref[...] += pc

    return pl.pallas_call(
        body, name="ln_silu_bwd", grid=(s // t,),
        in_specs=[_rows(t, d), _fixed((1, d)), _fixed((1, d)), _rows(t, d)],
        out_specs=[_rows(t, d)] + [_fixed((1, d))] * 3,
        out_shape=[jax.ShapeDtypeStruct((s, d), F32)] + [jax.ShapeDtypeStruct((1, d), F32)] * 3,
        compiler_params=_params("arbitrary"),
    )(c, g, b, ds)


def _residue_spec(dil, t, w):
    return pl.BlockSpec((dil, t // dil, w), lambda i: (0, i, 0))


def _attn_combine(o_list, lse_list):
    dil0, sd0, d = o_list[0].shape
    s = dil0 * sd0
    lw = lse_list[0].shape[2]
    group = d // HEAD_DIM // N_KV_HEADS
    t = min(ROW_TILE, s)
    nb = len(o_list)
    dils = [o.shape[0] for o in o_list]

    def body(*refs):
        o_out, l_out = refs[2 * nb], refs[2 * nb + 1]
        o_stage, l_stage = refs[2 * nb + 2:3 * nb + 2], refs[3 * nb + 2:]
        o_planes = [_from_residues(src, stage, dil) for src, stage, dil in zip(refs[:nb], o_stage, dils)]
        l_planes = [_from_residues(src, stage, dil) for src, stage, dil in zip(refs[nb:2 * nb], l_stage, dils)]
        for kh in range(N_KV_HEADS):
            ls = [plane(kh) for plane in l_planes]
            mx = ls[0]
            for l in ls[1:]:
                mx = jnp.maximum(mx, l)
            es = [jnp.exp(l - mx) for l in ls]
            den = es[0]
            for e in es[1:]:
                den = den + e
            l_out[:, kh * LANES:(kh + 1) * LANES] = mx + jnp.log(den)
            ws = [e / den for e in es]
            for g in range(group):
                h = kh * group + g
                acc = jnp.zeros((t, HEAD_DIM), F32)
                for plane, w in zip(o_planes, ws):
                    acc = acc + w[:, g:g + 1] * plane(h)
                o_out[:, h * HEAD_DIM:(h + 1) * HEAD_DIM] = acc.astype(BF16)

    return pl.pallas_call(
        body, name="attn_combine", grid=(s // t,),
        in_specs=[_residue_spec(dil, t, d) for dil in dils] + [_residue_spec(dil, t, lw) for dil in dils],
        out_specs=[_rows(t, d), _rows(t, lw)],
        out_shape=[jax.ShapeDtypeStruct((s, d), BF16), jax.ShapeDtypeStruct((s, lw), F32)],
        scratch_shapes=[pltpu.VMEM(_stage_shape(t, d), F32)] * nb + [pltpu.VMEM(_stage_shape(t, lw), F32)] * nb,
        compiler_params=_params("parallel"),
    )(*o_list, *lse_list)


def _attn_delta(do, o, lse, dils):
    s, d = o.shape
    lw = lse.shape[1]
    group = d // HEAD_DIM // N_KV_HEADS
    t = min(ROW_TILE, s)
    nd = len(dils)

    def body(do_ref, o_ref, lse_ref, *refs):
        stage = refs[-1]
        lane = lax.broadcasted_iota(I32, (t, LANES), 1)
        planes = []
        for kh in range(N_KV_HEADS):
            out = jnp.zeros((t, LANES), F32)
            for g in range(group):
                cols = slice((kh * group + g) * HEAD_DIM, (kh * group + g + 1) * HEAD_DIM)
                v = jnp.sum(do_ref[:, cols].astype(F32) * o_ref[:, cols].astype(F32), axis=-1, keepdims=True)
                out = jnp.where(lane == g, v, out)
            planes.append(out)
        _to_residues(lse_ref[...], stage, refs[:nd], dils)
        _to_residues(jnp.concatenate(planes, axis=1), stage, refs[nd:2 * nd], dils)

    res = pl.pallas_call(
        body, name="attn_delta", grid=(s // t,),
        in_specs=[_rows(t, d), _rows(t, d), _rows(t, lw)],
        out_specs=[_residue_spec(dil, t, lw) for dil in dils] * 2,
        out_shape=[jax.ShapeDtypeStruct((dil, s // dil, lw), F32) for dil in dils] * 2,
        scratch_shapes=[pltpu.VMEM(_stage_shape(t, lw), F32)],
        compiler_params=_params("parallel"),
    )(do, o, lse)
    return res[:nd], res[nd:]


def _residue_sum(name, groups, tabs):
    first = groups[0][0][0]
    s, w = first.shape[0] * first.shape[1], first.shape[2]
    t = min(ROW_TILE, s)
    flat = [p for parts, _ in groups for p in parts]

    def body(*refs):
        c_ref, a_ref, b_ref = refs[len(flat):len(flat) + 3]
        out = refs[len(flat) + 3]
        stages = refs[len(flat) + 4:]
        k = 0
        for gi, (parts, rotate) in enumerate(groups):
            planes = [_from_residues(refs[k + i], stages[k + i], p.shape[0]) for i, p in enumerate(parts)]
            k += len(parts)
            for c in range(w // LANES):
                tot = planes[0](c)
                for plane in planes[1:]:
                    tot = tot + plane(c)
                if rotate:
                    tot = _rope_apply(tot, c_ref[...], a_ref[...], b_ref[...], -1.0)
                out[:, gi * w + c * LANES:gi * w + (c + 1) * LANES] = tot.astype(BF16)

    return pl.pallas_call(
        body, name=name, grid=(s // t,),
        in_specs=[_residue_spec(p.shape[0], t, w) for p in flat] + [_rows(t, HEAD_DIM)] * 3,
        out_specs=_rows(t, len(groups) * w), out_shape=jax.ShapeDtypeStruct((s, len(groups) * w), BF16),
        scratch_shapes=[pltpu.VMEM(_stage_shape(t, w), F32) for _ in flat],
        compiler_params=_params("parallel"),
    )(*flat, *tabs)


def _dwconv_fwd(u, w_dw, b_dw):
    s, d2 = u.shape
    d = d2 // 2
    cb = min(CONV_CB, d)
    nblk = d // cb
    tt = min(CONV_T, s)

    def body(ua_ref, ug_ref, w_ref, b_ref, c_ref, xp_ref):
        gl = ua_ref[...].astype(F32) * _sigmoid(ug_ref[...].astype(F32))
        xp_ref[0:CONV_PAD, :] = jnp.zeros((CONV_PAD, cb), F32)
        xp_ref[CONV_PAD:, :] = gl
        wv = w_ref[...]
        bv = b_ref[...]
        for t0 in range(0, s, tt):
            acc = jnp.zeros((tt, cb), F32) + bv
            for kk in range(CONV_WIDTH):
                off = t0 + CONV_PAD - (CONV_WIDTH - 1) + kk
                acc = acc + wv[kk:kk + 1, :] * xp_ref[off:off + tt, :]
            c_ref[t0:t0 + tt, :] = acc

    return pl.pallas_call(
        body, name="dwconv_fwd", grid=(nblk,),
        in_specs=[pl.BlockSpec((s, cb), lambda j: (0, j)), pl.BlockSpec((s, cb), lambda j: (0, j + nblk)),
                  pl.BlockSpec((CONV_PAD, cb), lambda j: (0, j)), pl.BlockSpec((1, cb), lambda j: (0, j))],
        out_specs=pl.BlockSpec((s, cb), lambda j: (0, j)),
        out_shape=jax.ShapeDtypeStruct((s, d), F32),
        scratch_shapes=[pltpu.VMEM((s + CONV_PAD, cb), F32)],
        compiler_params=_params("parallel"),
    )(u, u, w_dw, b_dw)


def _dwconv_bwd(u, w_dw, dc):
    s, d2 = u.shape
    d = d2 // 2
    cb = min(CONV_CB, d)
    nblk = d // cb
    tt = min(CONV_T, s)

    def body(ua_ref, ug_ref, w_ref, dc_ref, du_ref, dw_ref, dba_ref, dbg_ref, glp_ref, dcp_ref, acc_ref):
        a = ua_ref[...].astype(F32)
        sig = _sigmoid(ug_ref[...].astype(F32))
        glp_ref[0:CONV_PAD, :] = jnp.zeros((CONV_PAD, cb), F32)
        glp_ref[CONV_PAD:, :] = a * sig
        dcp_ref[0:s, :] = dc_ref[...]
        dcp_ref[s:, :] = jnp.zeros((CONV_PAD, cb), F32)
        acc_ref[...] = jnp.zeros_like(acc_ref)
        wv = w_ref[...]
        dba = jnp.zeros((1, cb), F32)
        dbg = jnp.zeros((1, cb), F32)
        for t0 in range(0, s, tt):
            dgl = jnp.zeros((tt, cb), F32)
            dct = dc_ref[t0:t0 + tt, :]
            for kk in range(CONV_WIDTH):
                off = t0 + (CONV_WIDTH - 1) - kk
                dgl = dgl + wv[kk:kk + 1, :] * dcp_ref[off:off + tt, :]
                goff = t0 + CONV_PAD - (CONV_WIDTH - 1) + kk
                prod = dct * glp_ref[goff:goff + tt, :]
                acc_ref[8 * kk:8 * kk + 8, :] += jnp.sum(prod.reshape(tt // 8, 8, cb), axis=0)
            at = ua_ref[t0:t0 + tt, :].astype(F32)
            st = _sigmoid(ug_ref[t0:t0 + tt, :].astype(F32))
            da = dgl * st
            dg = dgl * at * st * (1.0 - st)
            du_ref[0, t0:t0 + tt, :] = da.astype(BF16)
            du_ref[1, t0:t0 + tt, :] = dg.astype(BF16)
            dba = dba + jnp.sum(da, axis=0, keepdims=True)
            dbg = dbg + jnp.sum(dg, axis=0, keepdims=True)
        dba_ref[...] = dba
        dbg_ref[...] = dbg
        for kk in range(CONV_WIDTH):
            dw_ref[kk:kk + 1, :] = jnp.sum(acc_ref[8 * kk:8 * kk + 8, :], axis=0, keepdims=True)
        dw_ref[CONV_WIDTH:, :] = jnp.zeros((CONV_PAD - CONV_WIDTH, cb), F32)

    blk = pl.BlockSpec((s, cb), lambda j: (0, j))
    vec = pl.BlockSpec((1, cb), lambda j: (0, j))
    return pl.pallas_call(
        body, name="dwconv_bwd", grid=(nblk,),
        in_specs=[blk, pl.BlockSpec((s, cb), lambda j: (0, j + nblk)),
                  pl.BlockSpec((CONV_PAD, cb), lambda j: (0, j)), blk],
        out_specs=[pl.BlockSpec((2, s, cb), lambda j: (0, 0, j)), pl.BlockSpec((CONV_PAD, cb), lambda j: (0, j)),
                   vec, vec],
        out_shape=[jax.ShapeDtypeStruct((2, s, d), BF16), jax.ShapeDtypeStruct((CONV_PAD, d), F32),
                   jax.ShapeDtypeStruct((1, d), F32), jax.ShapeDtypeStruct((1, d), F32)],
        scratch_shapes=[pltpu.VMEM((s + CONV_PAD, cb), F32), pltpu.VMEM((s + CONV_PAD, cb), F32),
                        pltpu.VMEM((8 * CONV_PAD, cb), F32)],
        compiler_params=_params("parallel"),
    )(u, u, w_dw, dc)


def _stack_heads(x, group):
    return jnp.concatenate([x[:, g * HEAD_DIM:(g + 1) * HEAD_DIM] for g in range(group)], axis=0)


def _unstack_heads(x, group):
    return jnp.concatenate([x[g * ATT_BLOCK:(g + 1) * ATT_BLOCK, :] for g in range(group)], axis=1)


def _stack_cols(x, group):
    return jnp.concatenate([x[:, g:g + 1] for g in range(group)], axis=0)


def _band_bias(group):
    rows = group * ATT_BLOCK
    row = lax.broadcasted_iota(I32, (rows, 2 * ATT_BLOCK), 0) % ATT_BLOCK
    col = lax.broadcasted_iota(I32, (rows, 2 * ATT_BLOCK), 1)
    band = jnp.where((col >= row) & (col <= row + ATT_BLOCK), 0.0, -jnp.inf).astype(F32)
    first = jnp.where(lax.broadcasted_iota(I32, (1, 2 * ATT_BLOCK), 1) >= ATT_BLOCK, 0.0, -jnp.inf).astype(F32)
    return band, first


def _masked_scores(qs, kw, band_ref, first_ref, nb, scale):
    sc = lax.dot_general(qs, kw, (((1,), (1,)), ((), ())), preferred_element_type=F32) * scale + band_ref[...]
    return sc + jnp.where(nb > 0, 0.0, first_ref[...])


def _window(ref, nb):
    prev = pl.multiple_of(jnp.maximum(nb - 1, 0) * ATT_BLOCK, ATT_BLOCK)
    cur = pl.multiple_of(nb * ATT_BLOCK, ATT_BLOCK)
    return jnp.concatenate([ref[pl.ds(prev, ATT_BLOCK), :], ref[pl.ds(cur, ATT_BLOCK), :]], axis=0)


def _residues_per_step(dil, nblk):
    return max(1, min(dil, ATT_STEP_BLOCKS // nblk))


def _attn_fwd(name, q, kv):
    dil, sd, d = q.shape
    group = d // HEAD_DIM // N_KV_HEADS
    gw = group * HEAD_DIM
    nblk = sd // ATT_BLOCK
    scale = 1.0 / math.sqrt(HEAD_DIM)
    nt = (((1,), (1,)), ((), ()))

    rb = _residues_per_step(dil, nblk)

    def body(q_all, k_all, v_all, band_ref, first_ref, o_all, lse_all):
        lane = lax.broadcasted_iota(I32, (ATT_BLOCK, LANES), 1)
        for rr in range(rb):
            q_ref, k_ref, v_ref, o_ref, lse_ref = [ref.at[rr] for ref in (q_all, k_all, v_all, o_all, lse_all)]

            def step(nb, carry):
                rows = pl.ds(pl.multiple_of(nb * ATT_BLOCK, ATT_BLOCK), ATT_BLOCK)
                qs = _stack_heads(q_ref[rows, :], group)
                kw = _window(k_ref, nb)
                vw = _window(v_ref, nb)
                sc = _masked_scores(qs, kw, band_ref, first_ref, nb, scale)
                mx = jnp.max(sc, axis=-1, keepdims=True)
                p = jnp.exp(sc - mx)
                l = jnp.sum(p, axis=-1, keepdims=True)
                o = jnp.dot(p.astype(BF16), vw, preferred_element_type=F32) / l
                o_ref[rows, :] = _unstack_heads(o, group).astype(BF16)
                lse = mx + jnp.log(l)
                out = jnp.zeros((ATT_BLOCK, LANES), F32)
                for g in range(group):
                    out = jnp.where(lane == g, lse[g * ATT_BLOCK:(g + 1) * ATT_BLOCK, :], out)
                lse_ref[rows, :] = out
                return carry

            lax.fori_loop(0, nblk, step, 0, unroll=min(2, nblk))

    kvh = N_KV_HEADS
    band, first = _band_bias(group)
    qspec = pl.BlockSpec((rb, sd, gw), lambda r, h: (r, 0, h))
    kspec = pl.BlockSpec((rb, sd, HEAD_DIM), lambda r, h: (r, 0, h))
    return pl.pallas_call(
        body, name=name, grid=(dil // rb, kvh),
        in_specs=[qspec, kspec, pl.BlockSpec((rb, sd, HEAD_DIM), lambda r, h: (r, 0, kvh + h)),
                  pl.BlockSpec(band.shape, lambda r, h: (0, 0)), pl.BlockSpec(first.shape, lambda r, h: (0, 0))],
        out_specs=[qspec, kspec],
        out_shape=[jax.ShapeDtypeStruct((dil, sd, d), BF16),
                   jax.ShapeDtypeStruct((dil, sd, kvh * LANES), F32)],
        compiler_params=_params("parallel", "parallel"),
    )(q, kv, kv, band, first)


def _attn_bwd(name, q, kv, do, lse, delta):
    dil, sd, d = q.shape
    group = d // HEAD_DIM // N_KV_HEADS
    gw = group * HEAD_DIM
    nblk = sd // ATT_BLOCK
    scale = 1.0 / math.sqrt(HEAD_DIM)
    nt = (((1,), (1,)), ((), ()))
    tn = (((0,), (0,)), ((), ()))

    rb = _residues_per_step(dil, nblk)

    def body(q_all, k_all, v_all, do_all, lse_all, dl_all, band_ref, first_ref, dq_all, dk_all, dv_all, dk_accs,
             dv_accs):
        dk_accs[...] = jnp.zeros_like(dk_accs)
        dv_accs[...] = jnp.zeros_like(dv_accs)
        for rr in range(rb):
            q_ref, k_ref, v_ref, do_ref, lse_ref, dl_ref, dq_ref, dk_ref, dv_ref, dk_acc, dv_acc = [
                ref.at[rr] for ref in (q_all, k_all, v_all, do_all, lse_all, dl_all, dq_all, dk_all, dv_all,
                                       dk_accs, dv_accs)]

            def step(nb, carry):
                rows = pl.ds(pl.multiple_of(nb * ATT_BLOCK, ATT_BLOCK), ATT_BLOCK)
                qs = _stack_heads(q_ref[rows, :], group)
                dos = _stack_heads(do_ref[rows, :], group)
                ls = _stack_cols(lse_ref[rows, :], group)
                dl = _stack_cols(dl_ref[rows, :], group)
                kw = _window(k_ref, nb)
                vw = _window(v_ref, nb)
                p = jnp.exp(_masked_scores(qs, kw, band_ref, first_ref, nb, scale) - ls)
                dp = lax.dot_general(dos, vw, nt, preferred_element_type=F32)
                ds = (p * (dp - dl) * scale).astype(BF16)
                dq = jnp.dot(ds, kw, preferred_element_type=F32)
                dq_ref[rows, :] = _unstack_heads(dq, group).astype(BF16)
                win = pl.ds(pl.multiple_of(nb * ATT_BLOCK, ATT_BLOCK), 2 * ATT_BLOCK)
                dk_acc[win, :] += lax.dot_general(ds, qs, tn, preferred_element_type=F32)
                dv_acc[win, :] += lax.dot_general(p.astype(BF16), dos, tn, preferred_element_type=F32)
                return carry

            lax.fori_loop(0, nblk, step, 0, unroll=min(2, nblk))
            dk_ref[...] = dk_acc[ATT_BLOCK:, :]
            dv_ref[...] = dv_acc[ATT_BLOCK:, :]

    kvh = N_KV_HEADS
    band, first = _band_bias(group)
    qspec = pl.BlockSpec((rb, sd, gw), lambda r, h: (r, 0, h))
    kspec = pl.BlockSpec((rb, sd, HEAD_DIM), lambda r, h: (r, 0, h))
    return pl.pallas_call(
        body, name=name, grid=(dil // rb, kvh),
        in_specs=[qspec, kspec, pl.BlockSpec((rb, sd, HEAD_DIM), lambda r, h: (r, 0, kvh + h)),
                  qspec, kspec, kspec,
                  pl.BlockSpec(band.shape, lambda r, h: (0, 0)), pl.BlockSpec(first.shape, lambda r, h: (0, 0))],
        out_specs=[qspec, kspec, kspec],
        out_shape=[jax.ShapeDtypeStruct((dil, sd, d), BF16),
                   jax.ShapeDtypeStruct((dil, sd, kvh * HEAD_DIM), F32),
                   jax.ShapeDtypeStruct((dil, sd, kvh * HEAD_DIM), F32)],
        scratch_shapes=[pltpu.VMEM((rb, sd + ATT_BLOCK, HEAD_DIM), F32)] * 2,
        compiler_params=_params("parallel", "parallel"),
    )(q, kv, kv, do, lse, delta, band, first)


def _cast_bf16(name, w, layer, place, after=None):
    _, r, c = w.shape
    tr = min(512, r)
    deps = [] if after is None else [after]

    def body(pl_ref, w_ref, *refs):
        refs[-1][...] = w_ref[...].astype(BF16)

    return pl.pallas_call(
        body, name=name,
        grid_spec=pltpu.PrefetchScalarGridSpec(
            num_scalar_prefetch=1, grid=(r // tr,),
            in_specs=[pl.BlockSpec((None, tr, c), lambda i, p: (layer, i, 0))] + [ANY] * len(deps),
            out_specs=pl.BlockSpec((None, tr, c), lambda i, p: (p[1], i, 0))),
        out_shape=jax.ShapeDtypeStruct((N_SHARD, r, c), BF16),
        compiler_params=_params("parallel"),
    )(place, w, *deps)


def _chip_sum(name, g, rh, place):
    _, r, c = g.shape
    rh2 = r // 2
    tr = min(512, rh2)
    nb = rh2 // tr

    def body(pl_ref, g_ref, rh_ref, o_ref):
        o_ref[...] = (g_ref[...].astype(F32) + rh_ref[...].astype(F32)).astype(BF16)

    return pl.pallas_call(
        body, name=name,
        grid_spec=pltpu.PrefetchScalarGridSpec(
            num_scalar_prefetch=1, grid=(N_SHARD, nb),
            in_specs=[pl.BlockSpec((None, tr, c), lambda s, i, p: (s, p[0] * nb + i, 0)),
                      pl.BlockSpec((None, tr, c), lambda s, i, p: (s, i, 0))],
            out_specs=pl.BlockSpec((None, tr, c), lambda s, i, p: (s, i, 0))),
        out_shape=jax.ShapeDtypeStruct((N_SHARD, rh2, c), BF16),
        compiler_params=_params("parallel", "parallel"),
    )(place, g, rh)


def _owner_sum(name, cs, rp, place):
    _, rh2, c = cs.shape
    tr = min(512, rh2)
    nb = rh2 // tr

    def body(pl_ref, cs_ref, r0_ref, r1_ref, r2_ref, o_ref):
        o_ref[...] = ((cs_ref[...].astype(F32) + r0_ref[...].astype(F32))
                      + (r1_ref[...].astype(F32) + r2_ref[...].astype(F32)))

    def rspec(j):
        return pl.BlockSpec((None, tr, c), lambda i, p: (j, i, 0))

    return pl.pallas_call(
        body, name=name,
        grid_spec=pltpu.PrefetchScalarGridSpec(
            num_scalar_prefetch=1, grid=(nb,),
            in_specs=[pl.BlockSpec((None, tr, c), lambda i, p: (p[1], i, 0)), rspec(0), rspec(1), rspec(2)],
            out_specs=pl.BlockSpec((tr, c), lambda i, p: (p[0] * nb + i, 0))),
        out_shape=jax.ShapeDtypeStruct((2 * rh2, c), F32),
        compiler_params=_params("parallel"),
    )(place, cs, rp, rp, rp)


def _adam_math(w, g, m, v):
    m = ADAM_B1 * m + (1.0 - ADAM_B1) * g
    v = ADAM_B2 * v + (1.0 - ADAM_B2) * (g * g)
    m_hat = m / (1.0 - ADAM_B1 ** ADAM_STEP)
    v_hat = v / (1.0 - ADAM_B2 ** ADAM_STEP)
    delta = -ADAM_LR * (m_hat / (jnp.sqrt(v_hat) + ADAM_EPS) + ADAM_WD * w)
    return delta, m, v


def _adamw(name, w, m, v, g, layer, partial=None):
    nl, r, c = w.shape
    tr = min(256, r)

    def body(w_ref, m_ref, v_ref, g_ref, *refs):
        go_ref, d_ref, mo_ref, vo_ref = refs[-4:]
        gv = g_ref[...]
        delta, m_new, v_new = _adam_math(w_ref[...], gv, m_ref[...], v_ref[...])
        go_ref[...] = gv
        d_ref[...] = delta
        mo_ref[...] = m_new
        vo_ref[...] = v_new

    wspec = pl.BlockSpec((None, tr, c), lambda i: (layer, i, 0))
    prev = [] if partial is None else list(partial)
    return pl.pallas_call(
        body, name=name, grid=(r // tr,),
        in_specs=[wspec] * 3 + [pl.BlockSpec((tr, c), lambda i: (i, 0))] + [ANY] * len(prev),
        out_specs=[wspec] * 4,
        out_shape=[jax.ShapeDtypeStruct((nl, r, c), F32)] * 4,
        input_output_aliases={4 + i: i for i in range(len(prev))},
        compiler_params=_params("parallel"),
    )(w, m, v, g, *prev)


def _adam_small(ws, ms, vs, gs):
    n = len(ws)

    def body(*refs):
        w_refs, m_refs, v_refs, g_refs = refs[:n], refs[n:2 * n], refs[2 * n:3 * n], refs[3 * n:4 * n]
        d_refs, mo_refs, vo_refs = refs[4 * n:5 * n], refs[5 * n:6 * n], refs[6 * n:7 * n]
        for i in range(n):
            delta, m_new, v_new = _adam_math(w_refs[i][...], g_refs[i][...], m_refs[i][...], v_refs[i][...])
            d_refs[i][...] = delta
            mo_refs[i][...] = m_new
            vo_refs[i][...] = v_new

    shapes = [jax.ShapeDtypeStruct(w.shape, F32) for w in ws]
    res = pl.pallas_call(body, name="adam_small", out_shape=shapes * 3)(*ws, *ms, *vs, *gs)
    return res[:n], res[n:2 * n], res[2 * n:]


def _pack_small(b_in, w_dw, b_dw, ln_g, ln_b, b_out, place):
    cin = b_in.shape[1]
    cd = b_dw.shape[1]
    rows = 8 + CONV_PAD

    def body(pl_ref, bi, wd, bd, lg, lb, bo, out):
        out[...] = jnp.zeros_like(out)
        out[0:1, :] = bi[...]
        out[1:2, 0:cd] = bd[...]
        out[1:2, cd:2 * cd] = lg[...]
        out[2:3, 0:cd] = lb[...]
        out[2:3, cd:2 * cd] = bo[...]
        out[8:8 + CONV_WIDTH, 0:cd] = wd[...]

    def whole(arr):
        return pl.BlockSpec(arr.shape, lambda i, p: (0,) * arr.ndim)

    ins = [b_in, w_dw, b_dw, ln_g, ln_b, b_out]
    return pl.pallas_call(
        body, name="pack_small",
        grid_spec=pltpu.PrefetchScalarGridSpec(
            num_scalar_prefetch=1, grid=(1,), in_specs=[whole(a) for a in ins],
            out_specs=pl.BlockSpec((None, rows, cin), lambda i, p: (p[1], 0, 0))),
        out_shape=jax.ShapeDtypeStruct((N_SHARD, rows, cin), F32),
        compiler_params=_params("arbitrary"),
    )(place, *ins)


def _place():
    x, y, c = lax.axis_index("x"), lax.axis_index("y"), lax.axis_index("c")
    return x, y, c


def _other_chips(x, y):
    return [(1 - x, y), (x, 1 - y), (1 - x, 1 - y)]


def _split_start_many(name, parts, after=None):
    flat = [b for bufs, _, _ in parts for b in bufs]
    n, n_parts = len(flat), len(parts)
    deps = [] if after is None else [after]

    def body(*refs):
        out0 = n + len(deps)
        pos = 0
        for i, (bufs, _, copies) in enumerate(parts):
            for cp in copies(refs[pos:pos + len(bufs)], refs[out0 + 2 * i], refs[out0 + 2 * i + 1], False):
                cp.start()
            pos += len(bufs)
        refs[-1][...] = jnp.zeros_like(refs[-1])

    sems = [pltpu.SemaphoreType.DMA((n_sem,)) for _, n_sem, _ in parts for _ in range(2)]
    res = pl.pallas_call(
        body, name=name,
        out_shape=(*sems, *[pltpu.HBM(b.shape, b.dtype) for b in flat], jax.ShapeDtypeStruct((8, LANES), F32)),
        in_specs=[HBM] * n + [ANY] * len(deps),
        out_specs=(*[SEM] * (2 * n_parts), *[HBM] * n, pl.BlockSpec(memory_space=pltpu.VMEM)),
        input_output_aliases={i: 2 * n_parts + i for i in range(n)},
        compiler_params=pltpu.CompilerParams(has_side_effects=SPLIT_EFFECT),
    )(*[pltpu.with_memory_space_constraint(b, pltpu.HBM) for b in flat], *deps)
    handles, pos = [], 2 * n_parts
    for i, (bufs, _, _) in enumerate(parts):
        handles.append((res[2 * i], res[2 * i + 1], list(res[pos:pos + len(bufs)]), res[-1]))
        pos += len(bufs)
    return handles


def _split_start(name, bufs, n_sem, copies, after=None):
    return _split_start_many(name, [(bufs, n_sem, copies)], after)[0]


def _split_wait(name, handle, copies, after):
    ssem, rsem, bufs, _ = handle
    n = len(bufs)
    deps = list(after) if isinstance(after, (list, tuple)) else [after]

    def body(*refs):
        for cp in copies(refs[:n], refs[n], refs[n + 1], True):
            cp.wait_send()
            cp.wait_recv()

    res = pl.pallas_call(
        body, name=name,
        out_shape=[pltpu.HBM(b.shape, b.dtype) for b in bufs],
        in_specs=[HBM] * n + [SEM, SEM] + [ANY] * len(deps), out_specs=[HBM] * n,
        input_output_aliases={i: i for i in range(n)},
        compiler_params=pltpu.CompilerParams(has_side_effects=SPLIT_EFFECT),
    )(*bufs, ssem, rsem, *deps)
    return list(res)


def _remote(src, dst, ssem, rsem, k, to):
    return pltpu.make_async_remote_copy(src_ref=src, dst_ref=dst, send_sem=ssem.at[k], recv_sem=rsem.at[k],
                                        device_id=to, device_id_type=MESH)


def _gather_chips(x, y, c):
    nx, ny = x + (1 - c) - 2 * x * (1 - c), y + c - 2 * y * c
    fx, fy = x + c - 2 * x * c, y + (1 - c) - 2 * y * (1 - c)
    return (nx, ny), (fx, fy), 2 * nx + ny, 2 * fx + fy, 2 * (1 - x) + (1 - y)


def _direct_copies(refs, ssem, rsem, landing, n_whole=0):
    x, y, c = _place()
    me = 2 * x + y
    (nx, ny), _, near, _, _ = _gather_chips(x, y, c)
    n = len(refs) - n_whole
    cps = []
    for a, ref in enumerate(refs[:n]):
        cps.append(_remote(ref.at[me], ref.at[near if landing else me], ssem, rsem, a, (nx, ny, c)))
    for b, ref in enumerate(refs[n:]):
        for j, (px, py) in enumerate(_other_chips(x, y)):
            cps.append(_remote(ref.at[me], ref.at[2 * px + py if landing else me], ssem, rsem, n + 3 * b + j,
                               (px, py, c)))
    return cps


def _relay_copies(refs, ssem, rsem, landing):
    x, y, c = _place()
    _, (fx, fy), near, far, diag = _gather_chips(x, y, c)
    n = len(refs)
    cps = []
    for a, ref in enumerate(refs):
        rh = ref.shape[1] // 2
        rows = pl.ds(c * rh, rh)
        cps.append(_remote(ref.at[near, rows], ref.at[diag if landing else near, rows], ssem, rsem, a, (fx, fy, c)))
        cps.append(_remote(ref.at[near], ref.at[far if landing else near], ssem, rsem, n + a, (x, y, 1 - c)))
    return cps


def _diagonal_copies(refs, ssem, rsem, landing):
    x, y, c = _place()
    diag = 2 * (1 - x) + (1 - y)
    who = 1 - c if landing else c
    cps = []
    for a, ref in enumerate(refs):
        rh = ref.shape[1] // 2
        piece = ref.at[diag, pl.ds(who * rh, rh)]
        cps.append(_remote(piece, piece, ssem, rsem, a, (x, y, 1 - c)))
    return cps


def _sibling_copies(refs, ssem, rsem, landing):
    x, y, c = _place()
    n = len(refs) // 2
    cps = []
    for a in range(n):
        rh = refs[a].shape[1] // 2
        cps.append(_remote(refs[a].at[:, pl.ds((1 - c) * rh, rh), :], refs[n + a], ssem, rsem, a, (x, y, 1 - c)))
    return cps


def _owner_copies(refs, ssem, rsem, landing):
    x, y, c = _place()
    n = len(refs) // 2
    cps = []
    for a in range(n):
        for j, (px, py) in enumerate(_other_chips(x, y)):
            cps.append(_remote(refs[a].at[2 * px + py], refs[n + a].at[j], ssem, rsem, 3 * a + j, (px, py, c)))
    return cps


def _swap_copies(refs, ssem, rsem, landing):
    x, y, c = _place()
    who = 1 - c if landing else c
    cps = []
    for a, ref in enumerate(refs):
        rh = ref.shape[0] // 2
        rows = ref.at[pl.ds(who * rh, rh)]
        cps.append(_remote(rows, rows, ssem, rsem, a, (x, y, 1 - c)))
    return cps


def _small_copies(refs, ssem, rsem, landing):
    pack, slots = refs
    x, y, c = _place()
    cps = []
    for rel in range(1, N_DEV):
        px = 1 - x if (rel >> 2) & 1 else x
        py = 1 - y if (rel >> 1) & 1 else y
        pc = 1 - c if rel & 1 else c
        slot = 4 * px + 2 * py + pc if landing else 4 * x + 2 * y + c
        cps.append(_remote(pack, slots.at[slot], ssem, rsem, rel - 1, (px, py, pc)))
    return cps


def _small_pack(rows, w_dw_grad, d):
    n = len(rows)

    def body(*refs):
        pack = refs[-1]
        pack[...] = jnp.zeros_like(pack)
        for (r, _), ref in zip(rows, refs[:n]):
            pack[r:r + 1, :] = ref[...]
        pack[16:16 + CONV_PAD, :] = refs[n][...]

    return pl.pallas_call(body, name="small_pack", out_shape=jax.ShapeDtypeStruct((SMALL_ROWS, d), F32))(
        *[v for _, v in rows], w_dw_grad)


def _small_sum(pack, slots, place):
    rows, d = pack.shape
    loss_row = 12

    def body(pl_ref, pack_ref, slots_ref, out_ref):
        me = pl_ref[2]
        tot = jnp.where(me == 0, pack_ref[...], slots_ref[0])
        for i in range(1, N_DEV):
            tot = tot + jnp.where(me == i, pack_ref[...], slots_ref[i])
        out_ref[...] = tot
        out_ref[loss_row:loss_row + 1, :] = jnp.zeros((1, d), F32) + jnp.sum(tot[loss_row:loss_row + 1, :])

    return pl.pallas_call(
        body, name="small_sum",
        grid_spec=pltpu.PrefetchScalarGridSpec(
            num_scalar_prefetch=1, grid=(1,),
            in_specs=[pl.BlockSpec((rows, d), lambda i, p: (0, 0)), pl.BlockSpec((N_DEV, rows, d), lambda i, p: (0, 0, 0))],
            out_specs=pl.BlockSpec((rows, d), lambda i, p: (0, 0))),
        out_shape=jax.ShapeDtypeStruct((rows, d), F32),
        compiler_params=_params("arbitrary"),
    )(place, pack, slots)


def kernel(x, norm_mix, norm_mlp, conv_w_in, conv_b_in, conv_w_dw, conv_b_dw, conv_ln_g, conv_ln_b, conv_w_out, conv_b_out, kv_norm, w_kv, attn_w_q, attn_w_o, mlp_w_in, mlp_w_out, final_norm, loss_target, m_norm_mix, m_norm_mlp, m_conv_w_in, m_conv_b_in, m_conv_w_dw, m_conv_b_dw, m_conv_ln_g, m_conv_ln_b, m_conv_w_out, m_conv_b_out, m_kv_norm, m_w_kv, m_attn_w_q, m_attn_w_o, m_mlp_w_in, m_mlp_w_out, m_final_norm, v_norm_mix, v_norm_mlp, v_conv_w_in, v_conv_b_in, v_conv_w_dw, v_conv_b_dw, v_conv_ln_g, v_conv_ln_b, v_conv_w_out, v_conv_b_out, v_kv_norm, v_w_kv, v_attn_w_q, v_attn_w_o, v_mlp_w_in, v_mlp_w_out, v_final_norm):
    _, s, d = x.shape
    dff = mlp_w_in.shape[2] * N_SHARD
    kvw = w_kv.shape[1]
    nh = d // HEAD_DIM
    group = nh // N_KV_HEADS
    ds4 = d // N_SHARD
    xi, yi, ci = _place()
    me = 2 * xi + yi
    place = jnp.stack([ci, me, 2 * me + ci]).astype(I32)

    h0 = x.reshape(s, d)
    target = loss_target.reshape(s, d)
    tabs = _rope_tables(s)

    def gather_begin(tag, bufs, n_whole=0):
        plan = functools.partial(_direct_copies, n_whole=n_whole)
        return _split_start(f"gather_start_{tag}", bufs, len(bufs) + 2 * n_whole, plan), plan, n_whole

    def gather_step(later, land=None, swap=None):
        parts, names, whole = [], [], {}
        if land is not None:
            tag, (handle, plan, n_whole) = land
            bufs = _split_wait(f"gather_wait_{tag}", handle, plan, later)
            n = len(bufs) - n_whole
            parts.append((bufs[:n], 2 * n, _relay_copies))
            whole["land"] = bufs[n:]
            names.append(f"relay_{tag}")
        if swap is not None:
            tag, (relayed, whole["swap"]) = swap
            bufs = _split_wait(f"relay_wait_{tag}", relayed, _relay_copies, later)
            parts.append((bufs, len(bufs), _diagonal_copies))
            names.append(f"diagonal_{tag}")
        handles = _split_start_many("start_" + "_".join(names), parts)
        landed = (handles[0], whole["land"]) if land is not None else None
        swapped = (handles[-1], whole["swap"]) if swap is not None else None
        return landed, swapped

    def gather_land(tag, begun, later):
        return gather_step(later, land=(tag, begun))[0]

    def gather_swap(tag, landed, later):
        return gather_step(later, swap=(tag, landed))[1]

    def gather_end(tag, swapped, later):
        handle, whole = swapped
        return _split_wait(f"diagonal_wait_{tag}", handle, _diagonal_copies, later) + whole

    def tied(vec, begun):
        return vec + begun[0][3][0:1, 0:1]

    ag_cin = gather_begin("conv_in", [
        _cast_bf16("cast_w_in", conv_w_in, 0, place),
        _pack_small(conv_b_in, conv_w_dw.reshape(CONV_WIDTH, ds4), conv_b_dw, conv_ln_g, conv_ln_b, conv_b_out, place),
    ], n_whole=1)
    ag_cout = gather_begin("conv_out", [_cast_bf16("cast_w_out", conv_w_out, 0, place, ag_cin[0][3])])
    ag_mi0 = gather_begin("mlp_in0", [_cast_bf16("cast_mlp_in0", mlp_w_in, 0, place, ag_cout[0][3])])
    ag_mo0 = gather_begin("mlp_out0", [_cast_bf16("cast_mlp_out0", mlp_w_out, 0, place, ag_mi0[0][3])])
    nm = [norm_mix[0:1], norm_mix[1:2]]
    nmlp = [norm_mlp[0:1], norm_mlp[1:2]]
    kvn = kv_norm.reshape(1, d)
    fin = final_norm.reshape(1, d)
    (y0,) = _rms_fwd("rms_mix0", h0, [tied(nm[0], ag_mo0)])
    land_cin = gather_land("conv_in", ag_cin, y0)
    ag_attn = gather_begin("attn", [
        _cast_bf16("cast_w_kv", w_kv.reshape(1, ds4, kvw), 0, place, land_cin[0][3]),
        _cast_bf16("cast_w_q", attn_w_q, 0, place), _cast_bf16("cast_w_o", attn_w_o, 0, place)])
    ag_mi1 = gather_begin("mlp_in1", [_cast_bf16("cast_mlp_in1", mlp_w_in, 1, place, ag_attn[0][3])])
    ag_mo1 = gather_begin("mlp_out1", [_cast_bf16("cast_mlp_out1", mlp_w_out, 1, place, ag_mi1[0][3])])
    land_cout, swap_cin = gather_step(ag_mo1[0][3], land=("conv_out", ag_cout), swap=("conv_in", land_cin))

    wmi_g = [None, None]
    wmo_f = [None, None]

    w_in_g, small_g = gather_end("conv_in", swap_cin, swap_cin[0][3])
    b_in_f = small_g[:, 0, :].reshape(1, 2 * d)
    b_dw_f = small_g[:, 1, 0:ds4].reshape(1, d)
    ln_g_f = small_g[:, 1, ds4:2 * ds4].reshape(1, d)
    ln_b_f = small_g[:, 2, 0:ds4].reshape(1, d)
    b_out_f = small_g[:, 2, ds4:2 * ds4].reshape(1, d)
    w_dw_f = jnp.transpose(small_g[:, 8:8 + CONV_PAD, 0:ds4], (1, 0, 2)).reshape(CONV_PAD, d)

    def ep_bias(acc, ex, outs, j):
        outs[0][...] = (acc + ex[0][...]).astype(outs[0].dtype)

    def ep_residual(acc, ex, outs, j):
        outs[0][...] = ex[0][...] + acc

    def ep_residual_bias(acc, ex, outs, j):
        outs[0][...] = ex[0][...] + (acc + ex[1][...])

    def ep_relu2(acc, ex, outs, j):
        r = jnp.maximum(acc, 0.0)
        outs[0][...] = r.astype(BF16)
        outs[1][...] = (r * r).astype(BF16)

    by_residue = [(BF16, ("residues", dil)) for dil in DILATIONS]

    def put_by_residue(val, outs, stage):
        _to_residues(val, stage, outs, DILATIONS)

    def ep_rope(acc, ex, outs, j, stage):
        put_by_residue(_rope_apply(acc, ex[0][...], ex[1][...], ex[2][...], 1.0), outs, stage)

    def ep_rope_k(acc, ex, outs, j, stage):
        roped = _rope_apply(acc, ex[0][...], ex[1][...], ex[2][...], 1.0)
        put_by_residue(jnp.where(j == 0, roped, acc), outs, stage)

    def ep_by_residue(acc, ex, outs, j, stage):
        put_by_residue(acc, outs, stage)

    tab_extras = [(t, "rows") for t in tabs]

    def mlp_fwd(idx, h, y, out_weight):
        r, r2 = _matmul(f"mlp_in{idx}", "nn", y, wmi_g[idx], b_kind="col", m=s, n=dff, k=d,
                        outs=[(BF16, "plain"), (BF16, "plain")], epilogue=ep_relu2)
        wmo_f[idx] = out_weight(r2).reshape(dff, d)
        (h_new,) = _matmul(f"mlp_out{idx}", "nn", r2, wmo_f[idx], m=s, n=d, k=dff,
                           outs=[(F32, "plain")], extras=[(h, "ij")], epilogue=ep_residual)
        return h_new, r, r2

    (u,) = _matmul("conv_in", "nn", y0, w_in_g, b_kind="col", m=s, n=2 * d, k=d,
                   outs=[(BF16, "plain")], extras=[(b_in_f, "vec")], epilogue=ep_bias)
    land_mi0, swap_cout = gather_step(u, land=("mlp_in0", ag_mi0), swap=("conv_out", land_cout))
    cpre = _dwconv_fwd(u, w_dw_f, tied(b_dw_f, swap_cout))
    sact = _ln_silu_fwd(cpre, ln_g_f, ln_b_f)
    (w_out_g,) = gather_end("conv_out", swap_cout, sact)
    w_out_f = w_out_g.reshape(d, d)
    (h1,) = _matmul("conv_out", "nn", sact, w_out_f, m=s, n=d, k=d,
                    outs=[(F32, "plain")], extras=[(h0, "ij"), (b_out_f, "vec")], epilogue=ep_residual_bias)
    swap_mi0 = gather_swap("mlp_in0", land_mi0, h1)
    (y1,) = _rms_fwd("rms_mlp0", h1, [tied(nmlp[0], swap_mi0)])
    land_mo0 = gather_land("mlp_out0", ag_mo0, y1)
    (wmi_g[0],) = gather_end("mlp_in0", swap_mi0, land_mo0[0][3])
    land_attn = None

    def out_weight0(r2):
        nonlocal land_attn
        land_attn, swap_mo0 = gather_step(r2, land=("attn", ag_attn), swap=("mlp_out0", land_mo0))
        return gather_end("mlp_out0", swap_mo0, swap_mo0[0][3])[0]

    h2, r0, r0sq = mlp_fwd(0, h1, y1, out_weight0)
    land_mi1, swap_attn = gather_step(h2, land=("mlp_in1", ag_mi1), swap=("attn", land_attn))
    ykv, y2 = _rms_fwd("rms_kv_mix1", h2, [tied(kvn, land_mi1), nm[1]])
    wkv_g, wq_g, wo_g = gather_end("attn", swap_attn, y2)
    wkv_f, wq_f, wo_f = wkv_g.reshape(d, kvw), wq_g.reshape(d, d), wo_g.reshape(d, d)
    kv_parts = _matmul("kv_proj", "nn", ykv, wkv_f, m=s, n=kvw, k=d, tn=kvw // 2,
                       outs=by_residue, extras=tab_extras, epilogue=ep_rope_k, stage=True)
    q_parts = _matmul("q_proj", "nn", y2, wq_f, m=s, n=d, k=d,
                      outs=by_residue, extras=tab_extras, epilogue=ep_rope, stage=True)
    o_parts, lse_parts = [], []
    for dil, q_b, kv_b in zip(DILATIONS, q_parts, kv_parts):
        o_b, lse_b = _attn_fwd(f"attn_fwd_d{dil}", q_b, kv_b)
        o_parts.append(o_b)
        lse_parts.append(lse_b)
    o, lse = _attn_combine(o_parts, lse_parts)
    land_mo1, swap_mi1 = gather_step(o, land=("mlp_out1", ag_mo1), swap=("mlp_in1", land_mi1))
    (h3,) = _matmul("attn_out", "nn", o, wo_f, m=s, n=d, k=d,
                    outs=[(F32, "plain")], extras=[(h2, "ij")], epilogue=ep_residual)
    (y3,) = _rms_fwd("rms_mlp1", h3, [tied(nmlp[1], land_mo1)])
    (wmi_g[1],) = gather_end("mlp_in1", swap_mi1, y3)

    def out_weight1(r2):
        swap_mo1 = gather_swap("mlp_out1", land_mo1, r2)
        return gather_end("mlp_out1", swap_mo1, swap_mo1[0][3])[0]

    h4, r1, r1sq = mlp_fwd(1, h3, y3, out_weight1)
    dh4, dh4b, d_fin, loss_cols = _final_loss(h4, fin, target)

    def ep_relu2_bwd(acc, ex, outs, j):
        outs[0][...] = (acc * (2.0 * ex[0][...].astype(F32))).astype(BF16)

    def mlp_bwd(idx, dhb, y, r, r2):
        (dz,) = _matmul(f"mlp_out{idx}_dx", "nt", dhb, wmo_f[idx], m=s, n=dff, k=d,
                        outs=[(BF16, "plain")], extras=[(r, "ij")], epilogue=ep_relu2_bwd)
        (dwo,) = _matmul(f"mlp_out{idx}_dw", "tn", r2, dhb, m=dff, n=d, k=s,
                         outs=[(BF16, "plain")])
        (dy,) = _matmul(f"mlp_in{idx}_dx", "nt", dz, wmi_g[idx], b_kind="col", m=s, n=d, k=dff,
                        outs=[(BF16, "plain")])
        (dwi,) = _matmul(f"mlp_in{idx}_dw", "tn", y, dz, m=d, n=dff, k=s,
                         outs=[(BF16, "col")])
        return dy, dwi, dwo.reshape(N_SHARD, dff // N_SHARD, d)

    def token(handle):
        return handle[3][0:1, 0:1]

    def rs_exchange(tag, grads):
        lands = [lax.empty((N_SHARD, g.shape[1] // 2, g.shape[2]), g.dtype) for g in grads]
        return _split_start(f"sibling_start_{tag}", list(grads) + lands, len(grads), _sibling_copies)

    def rs_send(tag, names, exchanged, later):
        bufs = _split_wait(f"sibling_wait_{tag}", exchanged, _sibling_copies, later)
        n = len(names)
        sums = [_chip_sum(f"chip_sum_{nme}", g, rh, place) for nme, g, rh in zip(names, bufs[:n], bufs[n:])]
        lands = [lax.empty((N_SHARD - 1,) + cs.shape[1:], cs.dtype) for cs in sums]
        return _split_start(f"owners_start_{tag}", sums + lands, 3 * n, _owner_copies)

    def rs_sum(tag, names, sent, later):
        bufs = _split_wait(f"owners_wait_{tag}", sent, _owner_copies, later)
        n = len(names)
        own = [_owner_sum(f"owner_sum_{nme}", cs, rp, place) for nme, cs, rp in zip(names, bufs[:n], bufs[n:])]
        return _split_start(f"swap_start_{tag}", own, n, _swap_copies)

    def rs_end(tag, swapped, later):
        return _split_wait(f"swap_wait_{tag}", swapped, _swap_copies, later)

    dy3, g_wmi1, g_wmo1 = mlp_bwd(1, dh4b, y3, r1, r1sq)
    x_mlp1 = rs_exchange("mlp1", [g_wmi1, g_wmo1])
    dh3, dh3b, d_nmlp1 = _rms_bwd("rms_mlp1_bwd", h3, [(nmlp[1] + token(x_mlp1), dy3)], dh4)

    do_parts = _matmul("attn_out_dx", "nt", dh3b, wo_f, m=s, n=d, k=d, outs=by_residue, epilogue=ep_by_residue,
                       stage=True)
    (g_wo,) = _matmul("attn_out_dw", "tn", o, dh3b, m=d, n=d, k=s, outs=[(BF16, "plain")])
    rs_mlp1 = rs_send("mlp1", ["mlp_in1", "mlp_out1"], x_mlp1, g_wo)
    lse_res, delta_res = _attn_delta(do_parts[0].reshape(s, d), o, lse, DILATIONS)
    dq_parts, dk_parts, dv_parts = [], [], []
    for dil, q_b, kv_b, do_b, lse_b, dl_b in zip(DILATIONS, q_parts, kv_parts, do_parts, lse_res, delta_res):
        dq_b, dk_b, dv_b = _attn_bwd(f"attn_bwd_d{dil}", q_b, kv_b, do_b, lse_b, dl_b)
        dq_parts.append(dq_b)
        dk_parts.append(dk_b)
        dv_parts.append(dv_b)
    dq = _residue_sum("rope_bwd_q", [(dq_parts, True)], tabs)
    dkv = _residue_sum("rope_bwd_kv", [(dk_parts, True), (dv_parts, False)], tabs)
    (g_wq,) = _matmul("q_proj_dw", "tn", y2, dq, m=d, n=d, k=s, outs=[(BF16, "plain")])
    (dy2,) = _matmul("q_proj_dx", "nt", dq, wq_f, m=s, n=d, k=d, outs=[(BF16, "plain")])
    (g_wkv,) = _matmul("kv_proj_dw", "tn", ykv, dkv, m=d, n=kvw, k=s, outs=[(BF16, "plain")])
    (dykv,) = _matmul("kv_proj_dx", "nt", dkv, wkv_f, m=s, n=d, k=kvw, outs=[(BF16, "plain")])
    x_attn = rs_exchange("attn", [g_wkv.reshape(N_SHARD, ds4, kvw), g_wq.reshape(N_SHARD, ds4, d),
                                  g_wo.reshape(N_SHARD, ds4, d)])
    dh2, dh2b, d_nm1, d_kvn = _rms_bwd("rms_kv_mix1_bwd", h2, [(nm[1] + token(x_attn), dy2), (kvn, dykv)], dh3)
    rs_attn = rs_send("attn", ["w_kv", "w_q", "w_o"], x_attn, dh2b)

    dy1, g_wmi0, g_wmo0 = mlp_bwd(0, dh2b, y1, r0, r0sq)
    x_mlp0 = rs_exchange("mlp0", [g_wmi0, g_wmo0])
    dh1, dh1b, d_nmlp0, d_b_out = _rms_bwd("rms_mlp0_bwd", h1, [(nmlp[0] + token(x_mlp0) + token(rs_attn), dy1)],
                                           dh2, want_colsum=True)

    (dsact,) = _matmul("conv_out_dx", "nt", dh1b, w_out_f, m=s, n=d, k=d, outs=[(BF16, "plain")])
    (g_wout,) = _matmul("conv_out_dw", "tn", sact, dh1b, m=d, n=d, k=s, outs=[(BF16, "plain")])
    rs_mlp0 = rs_send("mlp0", ["mlp_in0", "mlp_out0"], x_mlp0, g_wout)
    dc, d_ln_g, d_ln_b, d_b_dw = _ln_silu_bwd(cpre, ln_g_f + token(rs_mlp0), ln_b_f, dsact)
    du, d_w_dw, d_b_in_a, d_b_in_g = _dwconv_bwd(u, w_dw_f, dc)
    (g_win,) = _matmul("conv_in_dw", "tn", y0, du, b_kind="col", m=d, n=2 * d, k=s, outs=[(BF16, "col")])
    x_conv = rs_exchange("conv", [g_win, g_wout.reshape(N_SHARD, ds4, d)])
    (dy0,) = _matmul("conv_in_dx", "nt", du, w_in_g, a_kind="col", b_kind="col", m=s, n=d, k=2 * d,
                     outs=[(BF16, "plain")])
    rs_conv = rs_send("conv", ["w_in", "w_out"], x_conv, dy0)
    dx, _, d_nm0 = _rms_bwd("rms_mix0_bwd", h0, [(nm[0] + token(rs_conv), dy0)], dh1)

    small_rows = [(0, d_nm0), (1, d_nm1), (2, d_nmlp0), (3, d_nmlp1), (4, d_kvn), (5, d_fin), (6, d_b_dw),
                  (7, d_ln_g), (8, d_ln_b), (9, d_b_out), (10, d_b_in_a), (11, d_b_in_g), (12, loss_cols)]
    x_small = _split_start("small_start", [_small_pack(small_rows, d_w_dw, d),
                                           lax.empty((N_DEV, SMALL_ROWS, d), F32)], N_DEV - 1, _small_copies)

    def big(name, w, m, v, g, layer=0, partial=None):
        shape = w.shape
        w3, m3, v3 = [t.reshape((-1,) + shape[-2:]) for t in (w, m, v)]
        if partial is not None:
            partial = [t.reshape(w3.shape) for t in partial]
        res = _adamw(name, w3, m3, v3, g, layer, partial)
        return [t.reshape(shape) for t in res]

    sw_mlp1 = rs_sum("mlp1", ["mlp_in1", "mlp_out1"], rs_mlp1, x_small[3])
    sw_attn = rs_sum("attn", ["w_kv", "w_q", "w_o"], rs_attn, sw_mlp1[3])
    f_wmi1, f_wmo1 = rs_end("mlp1", sw_mlp1, sw_attn[3])
    p_wmi = big("adam_mlp_in1", mlp_w_in, m_mlp_w_in, v_mlp_w_in, f_wmi1, 1)
    p_wmo = big("adam_mlp_out1", mlp_w_out, m_mlp_w_out, v_mlp_w_out, f_wmo1, 1)
    sw_mlp0 = rs_sum("mlp0", ["mlp_in0", "mlp_out0"], rs_mlp0, [p_wmi[0], p_wmo[0]])
    f_wkv, f_wq, f_wo = rs_end("attn", sw_attn, sw_mlp0[3])
    r_wkv = big("adam_w_kv", w_kv, m_w_kv, v_w_kv, f_wkv)
    r_wq = big("adam_w_q", attn_w_q, m_attn_w_q, v_attn_w_q, f_wq)
    r_wo = big("adam_w_o", attn_w_o, m_attn_w_o, v_attn_w_o, f_wo)
    sw_conv = rs_sum("conv", ["w_in", "w_out"], rs_conv, [r_wkv[0], r_wq[0], r_wo[0]])
    f_wmi0, f_wmo0 = rs_end("mlp0", sw_mlp0, sw_conv[3])
    r_wmi = big("adam_mlp_in0", mlp_w_in, m_mlp_w_in, v_mlp_w_in, f_wmi0, 0, p_wmi)
    r_wmo = big("adam_mlp_out0", mlp_w_out, m_mlp_w_out, v_mlp_w_out, f_wmo0, 0, p_wmo)
    f_win, f_wout = rs_end("conv", sw_conv, [r_wmi[0], r_wmo[0]])
    r_win = big("adam_w_in", conv_w_in, m_conv_w_in, v_conv_w_in, f_win)
    r_wout = big("adam_w_out", conv_w_out, m_conv_w_out, v_conv_w_out, f_wout)

    small_pack, small_slots = _split_wait("small_wait", x_small, _small_copies, r_wout[0])
    red = _small_sum(small_pack, small_slots, place)
    loss = red[12, 0]
    g_norm_mix = red[0:2]
    g_norm_mlp = red[2:4]
    g_kv_norm = red[4:5]
    g_final = red[5:6]

    def my_cols(row):
        return lax.dynamic_slice(red, (row, me * ds4), (1, ds4))

    g_b_dw, g_ln_g, g_ln_b, g_b_out = my_cols(6), my_cols(7), my_cols(8), my_cols(9)
    half_in = 2 * d // N_SHARD
    b_in_row = 10 + me // 2
    g_b_in = lax.dynamic_slice(red, (b_in_row, (me % 2) * half_in), (1, half_in))
    g_w_dw = lax.dynamic_slice(red, (16, me * ds4), (CONV_WIDTH, ds4))

    sm_w =[norm_mix, norm_mlp, conv_b_in, conv_w_dw.reshape(CONV_WIDTH, ds4), conv_b_dw, conv_ln_g, conv_ln_b,
            conv_b_out, kv_norm.reshape(1, d), final_norm.reshape(1, d)]
    sm_m = [m_norm_mix, m_norm_mlp, m_conv_b_in, m_conv_w_dw.reshape(CONV_WIDTH, ds4), m_conv_b_dw, m_conv_ln_g,
            m_conv_ln_b, m_conv_b_out, m_kv_norm.reshape(1, d), m_final_norm.reshape(1, d)]
    sm_v = [v_norm_mix, v_norm_mlp, v_conv_b_in, v_conv_w_dw.reshape(CONV_WIDTH, ds4), v_conv_b_dw, v_conv_ln_g,
            v_conv_ln_b, v_conv_b_out, v_kv_norm.reshape(1, d), v_final_norm.reshape(1, d)]
    sm_g = [g_norm_mix, g_norm_mlp, g_b_in, g_w_dw, g_b_dw, g_ln_g, g_ln_b, g_b_out, g_kv_norm, g_final]
    sm_d, sm_nm, sm_nv = _adam_small(sm_w, sm_m, sm_v, sm_g)
    shapes = [norm_mix.shape, norm_mlp.shape, conv_b_in.shape, conv_w_dw.shape, conv_b_dw.shape, conv_ln_g.shape,
              conv_ln_b.shape, conv_b_out.shape, kv_norm.shape, final_norm.shape]
    sm_g, sm_d, sm_nm, sm_nv = [[t.reshape(sh) for t, sh in zip(lst, shapes)] for lst in (sm_g, sm_d, sm_nm, sm_nv)]

    def order(sm, idx):
        return [sm[0], sm[1], r_win[idx], sm[2], sm[3], sm[4], sm[5], sm[6], r_wout[idx], sm[7], sm[8],
                r_wkv[idx], r_wq[idx], r_wo[idx], r_wmi[idx], r_wmo[idx], sm[9]]

    return (loss, dx.reshape(x.shape), *order(sm_g, 0), *order(sm_d, 1), *order(sm_nm, 2), *order(sm_nv, 3))
```

```python
import functools
import math

import jax
import jax.numpy as jnp
from jax import lax
from jax.experimental import pallas as pl
from jax.experimental.pallas import tpu as pltpu

F32 = jnp.float32
BF16 = jnp.bfloat16
I32 = jnp.int32

NORM_EPS = 1e-6
LN_EPS = 1e-5
HEAD_DIM = 128
N_KV_HEADS = 4
ROT_DIM = 32
ROPE_THETA = 500000.0
CONV_WIDTH = 31
CONV_PAD = 32
ATT_BLOCK = 128
ATT_STEP_BLOCKS = 16
DILATIONS = (1, 4, 16)
ADAM_LR = 0.001
ADAM_B1 = 0.9
ADAM_B2 = 0.999
ADAM_EPS = 1e-08
ADAM_WD = 0.01
ADAM_STEP = 10
N_SHARD = 4
N_DEV = 8
LANES = 128
VMEM_LIMIT = 48 * 1024 * 1024
MM_TM, MM_TN, MM_TK = 1024, 1024, 2048
ROW_TILE = 512
CONV_CB = 128
CONV_T = 128
SMALL_ROWS = 48
MESH = pl.DeviceIdType.MESH
ANY = pl.BlockSpec(memory_space=pl.ANY)
HBM = pl.BlockSpec(memory_space=pltpu.HBM)
SEM = pl.BlockSpec(memory_space=pltpu.SEMAPHORE)
SPLIT_EFFECT = pltpu.SideEffectType.DATAFLOW_SIDE_EFFECTING


def _params(*sem):
    return pltpu.CompilerParams(dimension_semantics=sem, vmem_limit_bytes=VMEM_LIMIT)


def _sigmoid(x):
    return 1.0 / (1.0 + jnp.exp(-x))


def _wspec(kind, arr_shape, br, bc, pick):
    if kind == "plain":
        return pl.BlockSpec((br, bc), pick)
    per = arr_shape[2] // bc

    def idx(*g):
        rb, cb = pick(*g)
        return (cb // per, rb, cb % per)

    return pl.BlockSpec((None, br, bc), idx)


def _stage_shape(rows, w):
    return (w // LANES, rows, LANES)


def _to_residues(val, stage_ref, out_refs, dils):
    planes, rows, _ = stage_ref.shape
    for c in range(planes):
        stage_ref[c] = val[:, c * LANES:(c + 1) * LANES]
    for out_ref, dil in zip(out_refs, dils):
        if dil == 1:
            out_ref[0] = val.astype(out_ref.dtype)
            continue
        for r in range(dil):
            for c in range(planes):
                out_ref[r, :, c * LANES:(c + 1) * LANES] = stage_ref.at[c][pl.ds(r, rows // dil, stride=dil), :].astype(
                    out_ref.dtype)


def _from_residues(src_ref, stage_ref, dil):
    planes, rows, _ = stage_ref.shape
    if dil == 1:
        return lambda c: src_ref[0, :, c * LANES:(c + 1) * LANES].astype(F32)
    for r in range(dil):
        for c in range(planes):
            stage_ref.at[c][pl.ds(r, rows // dil, stride=dil), :] = src_ref[r, :, c * LANES:(c + 1) * LANES].astype(F32)
    return lambda c: stage_ref[c]


def _matmul(name, mode, a, b, *, m, n, k, tm=MM_TM, tn=MM_TN, tk=MM_TK, a_kind="plain", b_kind="plain", outs,
            extras=(), epilogue=None, stage=False, after=None):
    tm, tn, tk = min(tm, m), min(tn, n), min(tk, k)
    if b_kind == "col" and mode in ("nn", "tn"):
        tn = min(tn, n // b.shape[0])
    if b_kind == "col" and mode == "nt":
        tk = min(tk, k // b.shape[0])
    if a_kind == "col":
        assert mode == "nt"
        tk = min(tk, k // a.shape[0])
    if any(kind == "col" for _, kind in outs):
        tn = min(tn, n // N_SHARD)
    assert m % tm == 0 and n % tn == 0 and k % tk == 0, (name, m, n, k, tm, tn, tk)
    nk = k // tk
    grid = (m // tm, n // tn, nk)
    if mode == "nn":
        a_spec = pl.BlockSpec((tm, tk), lambda i, j, kk: (i, kk))
        b_spec = _wspec(b_kind, b.shape, tk, tn, lambda i, j, kk: (kk, j))
        dims = (((1,), (0,)), ((), ()))
    elif mode == "nt":
        a_spec = _wspec(a_kind, a.shape, tm, tk, lambda i, j, kk: (i, kk))
        b_spec = _wspec(b_kind, b.shape, tn, tk, lambda i, j, kk: (j, kk))
        dims = (((1,), (1,)), ((), ()))
    else:
        a_spec = pl.BlockSpec((tk, tm), lambda i, j, kk: (kk, i))
        b_spec = _wspec(b_kind, b.shape, tk, tn, lambda i, j, kk: (kk, j))
        dims = (((0,), (0,)), ((), ()))
    out_shape, out_specs = [], []
    for dtype, kind in outs:
        if isinstance(kind, tuple):
            dil = kind[1]
            out_shape.append(jax.ShapeDtypeStruct((dil, m // dil, n), dtype))
            out_specs.append(pl.BlockSpec((dil, tm // dil, tn), lambda i, j, kk: (0, i, j)))
            continue
        shape = (m, n) if kind == "plain" else (N_SHARD, m, n // N_SHARD)
        out_shape.append(jax.ShapeDtypeStruct(shape, dtype))
        out_specs.append(_wspec(kind, shape, tm, tn, lambda i, j, kk: (i, j)))
    n_ex = len(extras)
    deps = [] if after is None else [after]
    ex_specs = {"ij": pl.BlockSpec((tm, tn), lambda i, j, kk: (i, j)),
                "vec": pl.BlockSpec((1, tn), lambda i, j, kk: (0, j)),
                "rows": pl.BlockSpec((tm, LANES), lambda i, j, kk: (i, 0))}
    out0 = 2 + n_ex + len(deps)

    def body(*refs):
        a_ref, b_ref = refs[0], refs[1]
        ex_refs = refs[2:2 + n_ex]
        out_refs = refs[out0:out0 + len(outs)]
        j = pl.program_id(1)

        def finish(res):
            if epilogue is None:
                out_refs[0][...] = res.astype(out_refs[0].dtype)
            elif stage:
                epilogue(res, ex_refs, out_refs, j, refs[-1])
            else:
                epilogue(res, ex_refs, out_refs, j)

        prod = lax.dot_general(a_ref[...], b_ref[...], dims, preferred_element_type=F32)
        if nk == 1:
            finish(prod)
            return
        acc_ref = refs[out0 + len(outs)]
        kk = pl.program_id(2)

        @pl.when(kk == 0)
        def _():
            acc_ref[...] = prod

        @pl.when(kk > 0)
        def _():
            acc_ref[...] += prod

        @pl.when(kk == nk - 1)
        def _():
            finish(acc_ref[...])

    res = pl.pallas_call(
        body, name=name, grid=grid,
        in_specs=[a_spec, b_spec] + [ex_specs[how] for _, how in extras] + [ANY] * len(deps),
        out_specs=out_specs, out_shape=out_shape,
        scratch_shapes=[pltpu.VMEM((tm, tn), F32)] * (nk > 1) + [pltpu.VMEM(_stage_shape(tm, tn), F32)] * bool(stage),
        compiler_params=_params("parallel", "parallel", "arbitrary"),
    )(a, b, *[e for e, _ in extras], *deps)
    return res


def _rope_tables(seq):
    half = ROT_DIM // 2
    pos = jnp.arange(seq, dtype=F32)
    inv = ROPE_THETA ** (-jnp.arange(0, ROT_DIM, 2, dtype=F32) / ROT_DIM)
    ang = pos[:, None] * inv[None, :]
    cos, sin = jnp.cos(ang), jnp.sin(ang)
    zeros = jnp.zeros((seq, HEAD_DIM - ROT_DIM), F32)
    ctab = jnp.concatenate([cos, cos, zeros + 1.0], axis=1)
    atab = jnp.concatenate([-sin, jnp.zeros((seq, half), F32), zeros], axis=1)
    btab = jnp.concatenate([jnp.zeros((seq, half), F32), sin, zeros], axis=1)
    return ctab, atab, btab


def _rope_apply(x, ctab, atab, btab, sign):
    w = x.shape[1]
    reps = w // HEAD_DIM
    half = ROT_DIM // 2
    c = jnp.tile(ctab, (1, reps))
    a = jnp.tile(atab, (1, reps))
    b = jnp.tile(btab, (1, reps))
    up = pltpu.roll(x, w - half, 1)
    down = pltpu.roll(x, half, 1)
    return x * c + sign * (up * a + down * b)


def _rows(t, w):
    return pl.BlockSpec((t, w), lambda i: (i, 0))


def _fixed(shape):
    nd = len(shape)
    return pl.BlockSpec(shape, lambda i: (0,) * nd)


def _rms_fwd(name, x, gains):
    s, d = x.shape
    t = min(ROW_TILE, s)
    ng = len(gains)

    def body(x_ref, *refs):
        xv = x_ref[...]
        r = lax.rsqrt(jnp.mean(xv * xv, axis=-1, keepdims=True) + NORM_EPS)
        xn = xv * r
        for g_ref, y_ref in zip(refs[:ng], refs[ng:]):
            y_ref[...] = (xn * g_ref[...]).astype(BF16)

    return pl.pallas_call(
        body, name=name, grid=(s // t,),
        in_specs=[_rows(t, d)] + [_fixed((1, d))] * ng,
        out_specs=[_rows(t, d)] * ng,
        out_shape=[jax.ShapeDtypeStruct((s, d), BF16)] * ng,
        compiler_params=_params("parallel"),
    )(x, *gains)


def _rms_bwd(name, x, pairs, dh_in, want_colsum=False):
    s, d = x.shape
    n_p = len(pairs)
    t = min(ROW_TILE // n_p, s)

    def body(x_ref, dh_ref, *refs):
        g_refs = refs[:n_p]
        dy_refs = refs[n_p:2 * n_p]
        dh_out, dhb_out = refs[2 * n_p], refs[2 * n_p + 1]
        dg_refs = refs[2 * n_p + 2:2 * n_p + 2 + n_p]
        cs_ref = refs[-1] if want_colsum else None
        i = pl.program_id(0)
        xv = x_ref[...]
        r = lax.rsqrt(jnp.mean(xv * xv, axis=-1, keepdims=True) + NORM_EPS)
        xn = xv * r
        dh = dh_ref[...]
        for g_ref, dy_ref, dg_ref in zip(g_refs, dy_refs, dg_refs):
            dy = dy_ref[...].astype(F32)
            u = dy * g_ref[...]
            dh = dh + r * (u - xn * jnp.mean(u * xn, axis=-1, keepdims=True))
            part = jnp.sum(dy * xn, axis=0, keepdims=True)

            @pl.when(i == 0)
            def _():
                dg_ref[...] = part

            @pl.when(i > 0)
            def _():
                dg_ref[...] += part

        dh_out[...] = dh
        dhb_out[...] = dh.astype(BF16)
        if want_colsum:
            col = jnp.sum(dh, axis=0, keepdims=True)

            @pl.when(i == 0)
            def _():
                cs_ref[...] = col

            @pl.when(i > 0)
            def _():
                cs_ref[...] += col

    n_vec = n_p + (1 if want_colsum else 0)
    return pl.pallas_call(
        body, name=name, grid=(s // t,),
        in_specs=[_rows(t, d), _rows(t, d)] + [_fixed((1, d))] * n_p + [_rows(t, d)] * n_p,
        out_specs=[_rows(t, d), _rows(t, d)] + [_fixed((1, d))] * n_vec,
        out_shape=[jax.ShapeDtypeStruct((s, d), F32), jax.ShapeDtypeStruct((s, d), BF16)]
        + [jax.ShapeDtypeStruct((1, d), F32)] * n_vec,
        compiler_params=_params("arbitrary"),
    )(x, dh_in, *[g for g, _ in pairs], *[dy for _, dy in pairs])


def _final_loss(x, g, target):
    s, d = x.shape
    t = min(ROW_TILE, s)

    def body(x_ref, g_ref, t_ref, dh_out, dhb_out, dg_ref, loss_ref):
        i = pl.program_id(0)
        xv = x_ref[...]
        gv = g_ref[...]
        r = lax.rsqrt(jnp.mean(xv * xv, axis=-1, keepdims=True) + NORM_EPS)
        xn = xv * r
        diff = xn * gv - t_ref[...]
        dy = diff / d
        u = dy * gv
        dh = r * (u - xn * jnp.mean(u * xn, axis=-1, keepdims=True))
        dh_out[...] = dh
        dhb_out[...] = dh.astype(BF16)
        dg = jnp.sum(dy * xn, axis=0, keepdims=True)
        lc = jnp.sum(0.5 * diff * dy, axis=0, keepdims=True)

        @pl.when(i == 0)
        def _():
            dg_ref[...] = dg
            loss_ref[...] = lc

        @pl.when(i > 0)
        def _():
            dg_ref[...] += dg
            loss_ref[...] += lc

    return pl.pallas_call(
        body, name="final_loss", grid=(s // t,),
        in_specs=[_rows(t, d), _fixed((1, d)), _rows(t, d)],
        out_specs=[_rows(t, d), _rows(t, d), _fixed((1, d)), _fixed((1, d))],
        out_shape=[jax.ShapeDtypeStruct((s, d), F32), jax.ShapeDtypeStruct((s, d), BF16),
                   jax.ShapeDtypeStruct((1, d), F32), jax.ShapeDtypeStruct((1, d), F32)],
        compiler_params=_params("arbitrary"),
    )(x, g, target)


def _ln_silu_fwd(c, g, b):
    s, d = c.shape
    t = min(ROW_TILE, s)

    def body(c_ref, g_ref, b_ref, s_ref):
        cv = c_ref[...]
        mu = jnp.mean(cv, axis=-1, keepdims=True)
        xc = cv - mu
        rs = lax.rsqrt(jnp.mean(xc * xc, axis=-1, keepdims=True) + LN_EPS)
        ln = xc * rs * g_ref[...] + b_ref[...]
        s_ref[...] = (ln * _sigmoid(ln)).astype(BF16)

    return pl.pallas_call(
        body, name="ln_silu_fwd", grid=(s // t,),
        in_specs=[_rows(t, d), _fixed((1, d)), _fixed((1, d))],
        out_specs=_rows(t, d), out_shape=jax.ShapeDtypeStruct((s, d), BF16),
        compiler_params=_params("parallel"),
    )(c, g, b)


def _ln_silu_bwd(c, g, b, ds):
    s, d = c.shape
    t = min(ROW_TILE, s)

    def body(c_ref, g_ref, b_ref, ds_ref, dc_ref, dg_ref, db_ref, dbdw_ref):
        i = pl.program_id(0)
        cv = c_ref[...]
        gv = g_ref[...]
        mu = jnp.mean(cv, axis=-1, keepdims=True)
        xc = cv - mu
        rs = lax.rsqrt(jnp.mean(xc * xc, axis=-1, keepdims=True) + LN_EPS)
        nrm = xc * rs
        ln = nrm * gv + b_ref[...]
        sig = _sigmoid(ln)
        dln = ds_ref[...].astype(F32) * sig * (1.0 + ln * (1.0 - sig))
        dn = dln * gv
        dc = rs * (dn - jnp.mean(dn, axis=-1, keepdims=True)
                   - nrm * jnp.mean(dn * nrm, axis=-1, keepdims=True))
        dc_ref[...] = dc
        pg = jnp.sum(dln * nrm, axis=0, keepdims=True)
        pb = jnp.sum(dln, axis=0, keepdims=True)
        pc = jnp.sum(dc, axis=0, keepdims=True)

        @pl.when(i == 0)
        def _():
            dg_ref[...] = pg
            db_ref[...] = pb
            dbdw_ref[...] = pc

        @pl.when(i > 0)
        def _():
            dg_ref[...] += pg
            db_ref[...] += pb
            dbdw_ref[...] += pc

    return pl.pallas_call(
        body, name="ln_silu_bwd", grid=(s // t,),
        in_specs=[_rows(t, d), _fixed((1, d)), _fixed((1, d)), _rows(t, d)],
        out_specs=[_rows(t, d)] + [_fixed((1, d))] * 3,
        out_shape=[jax.ShapeDtypeStruct((s, d), F32)] + [jax.ShapeDtypeStruct((1, d), F32)] * 3,
        compiler_params=_params("arbitrary"),
    )(c, g, b, ds)


def _residue_spec(dil, t, w):
    return pl.BlockSpec((dil, t // dil, w), lambda i: (0, i, 0))


def _attn_combine(o_list, lse_list):
    dil0, sd0, d = o_list[0].shape
    s = dil0 * sd0
    lw = lse_list[0].shape[2]
    group = d // HEAD_DIM // N_KV_HEADS
    t = min(ROW_TILE, s)
    nb = len(o_list)
    dils = [o.shape[0] for o in o_list]

    def body(*refs):
        o_out, l_out = refs[2 * nb], refs[2 * nb + 1]
        o_stage, l_stage = refs[2 * nb + 2:3 * nb + 2], refs[3 * nb + 2:]
        o_planes = [_from_residues(src, stage, dil) for src, stage, dil in zip(refs[:nb], o_stage, dils)]
        l_planes = [_from_residues(src, stage, dil) for src, stage, dil in zip(refs[nb:2 * nb], l_stage, dils)]
        for kh in range(N_KV_HEADS):
            ls = [plane(kh) for plane in l_planes]
            mx = ls[0]
            for l in ls[1:]:
                mx = jnp.maximum(mx, l)
            es = [jnp.exp(l - mx) for l in ls]
            den = es[0]
            for e in es[1:]:
                den = den + e
            l_out[:, kh * LANES:(kh + 1) * LANES] = mx + jnp.log(den)
            ws = [e / den for e in es]
            for g in range(group):
                h = kh * group + g
                acc = jnp.zeros((t, HEAD_DIM), F32)
                for plane, w in zip(o_planes, ws):
                    acc = acc + w[:, g:g + 1] * plane(h)
                o_out[:, h * HEAD_DIM:(h + 1) * HEAD_DIM] = acc.astype(BF16)

    return pl.pallas_call(
        body, name="attn_combine", grid=(s // t,),
        in_specs=[_residue_spec(dil, t, d) for dil in dils] + [_residue_spec(dil, t, lw) for dil in dils],
        out_specs=[_rows(t, d), _rows(t, lw)],
        out_shape=[jax.ShapeDtypeStruct((s, d), BF16), jax.ShapeDtypeStruct((s, lw), F32)],
        scratch_shapes=[pltpu.VMEM(_stage_shape(t, d), F32)] * nb + [pltpu.VMEM(_stage_shape(t, lw), F32)] * nb,
        compiler_params=_params("parallel"),
    )(*o_list, *lse_list)


def _attn_delta(do, o, lse, dils):
    s, d = o.shape
    lw = lse.shape[1]
    group = d // HEAD_DIM // N_KV_HEADS
    t = min(ROW_TILE, s)
    nd = len(dils)

    def body(do_ref, o_ref, lse_ref, *refs):
        stage = refs[-1]
        lane = lax.broadcasted_iota(I32, (t, LANES), 1)
        planes = []
        for kh in range(N_KV_HEADS):
            out = jnp.zeros((t, LANES), F32)
            for g in range(group):
                cols = slice((kh * group + g) * HEAD_DIM, (kh * group + g + 1) * HEAD_DIM)
                v = jnp.sum(do_ref[:, cols].astype(F32) * o_ref[:, cols].astype(F32), axis=-1, keepdims=True)
                out = jnp.where(lane == g, v, out)
            planes.append(out)
        _to_residues(lse_ref[...], stage, refs[:nd], dils)
        _to_residues(jnp.concatenate(planes, axis=1), stage, refs[nd:2 * nd], dils)

    res = pl.pallas_call(
        body, name="attn_delta", grid=(s // t,),
        in_specs=[_rows(t, d), _rows(t, d), _rows(t, lw)],
        out_specs=[_residue_spec(dil, t, lw) for dil in dils] * 2,
        out_shape=[jax.ShapeDtypeStruct((dil, s // dil, lw), F32) for dil in dils] * 2,
        scratch_shapes=[pltpu.VMEM(_stage_shape(t, lw), F32)],
        compiler_params=_params("parallel"),
    )(do, o, lse)
    return res[:nd], res[nd:]


def _residue_sum(name, groups, tabs):
    first = groups[0][0][0]
    s, w = first.shape[0] * first.shape[1], first.shape[2]
    t = min(ROW_TILE, s)
    flat = [p for parts, _ in groups for p in parts]

    def body(*refs):
        c_ref, a_ref, b_ref = refs[len(flat):len(flat) + 3]
        out = refs[len(flat) + 3]
        stages = refs[len(flat) + 4:]
        k = 0
        for gi, (parts, rotate) in enumerate(groups):
            planes = [_from_residues(refs[k + i], stages[k + i], p.shape[0]) for i, p in enumerate(parts)]
            k += len(parts)
            for c in range(w // LANES):
                tot = planes[0](c)
                for plane in planes[1:]:
                    tot = tot + plane(c)
                if rotate:
                    tot = _rope_apply(tot, c_ref[...], a_ref[...], b_ref[...], -1.0)
                out[:, gi * w + c * LANES:gi * w + (c + 1) * LANES] = tot.astype(BF16)

    return pl.pallas_call(
        body, name=name, grid=(s // t,),
        in_specs=[_residue_spec(p.shape[0], t, w) for p in flat] + [_rows(t, HEAD_DIM)] * 3,
        out_specs=_rows(t, len(groups) * w), out_shape=jax.ShapeDtypeStruct((s, len(groups) * w), BF16),
        scratch_shapes=[pltpu.VMEM(_stage_shape(t, w), F32) for _ in flat],
        compiler_params=_params("parallel"),
    )(*flat, *tabs)


def _dwconv_fwd(u, w_dw, b_dw):
    s, d2 = u.shape
    d = d2 // 2
    cb = min(CONV_CB, d)
    nblk = d // cb
    tt = min(CONV_T, s)

    def body(ua_ref, ug_ref, w_ref, b_ref, c_ref, xp_ref):
        gl = ua_ref[...].astype(F32) * _sigmoid(ug_ref[...].astype(F32))
        xp_ref[0:CONV_PAD, :] = jnp.zeros((CONV_PAD, cb), F32)
        xp_ref[CONV_PAD:, :] = gl
        wv = w_ref[...]
        bv = b_ref[...]
        for t0 in range(0, s, tt):
            acc = jnp.zeros((tt, cb), F32) + bv
            for kk in range(CONV_WIDTH):
                off = t0 + CONV_PAD - (CONV_WIDTH - 1) + kk
                acc = acc + wv[kk:kk + 1, :] * xp_ref[off:off + tt, :]
            c_ref[t0:t0 + tt, :] = acc

    return pl.pallas_call(
        body, name="dwconv_fwd", grid=(nblk,),
        in_specs=[pl.BlockSpec((s, cb), lambda j: (0, j)), pl.BlockSpec((s, cb), lambda j: (0, j + nblk)),
                  pl.BlockSpec((CONV_PAD, cb), lambda j: (0, j)), pl.BlockSpec((1, cb), lambda j: (0, j))],
        out_specs=pl.BlockSpec((s, cb), lambda j: (0, j)),
        out_shape=jax.ShapeDtypeStruct((s, d), F32),
        scratch_shapes=[pltpu.VMEM((s + CONV_PAD, cb), F32)],
        compiler_params=_params("parallel"),
    )(u, u, w_dw, b_dw)


def _dwconv_bwd(u, w_dw, dc):
    s, d2 = u.shape
    d = d2 // 2
    cb = min(CONV_CB, d)
    nblk = d // cb
    tt = min(CONV_T, s)

    def body(ua_ref, ug_ref, w_ref, dc_ref, du_ref, dw_ref, dba_ref, dbg_ref, glp_ref, dcp_ref, acc_ref):
        a = ua_ref[...].astype(F32)
        sig = _sigmoid(ug_ref[...].astype(F32))
        glp_ref[0:CONV_PAD, :] = jnp.zeros((CONV_PAD, cb), F32)
        glp_ref[CONV_PAD:, :] = a * sig
        dcp_ref[0:s, :] = dc_ref[...]
        dcp_ref[s:, :] = jnp.zeros((CONV_PAD, cb), F32)
        acc_ref[...] = jnp.zeros_like(acc_ref)
        wv = w_ref[...]
        dba = jnp.zeros((1, cb), F32)
        dbg = jnp.zeros((1, cb), F32)
        for t0 in range(0, s, tt):
            dgl = jnp.zeros((tt, cb), F32)
            dct = dc_ref[t0:t0 + tt, :]
            for kk in range(CONV_WIDTH):
                off = t0 + (CONV_WIDTH - 1) - kk
                dgl = dgl + wv[kk:kk + 1, :] * dcp_ref[off:off + tt, :]
                goff = t0 + CONV_PAD - (CONV_WIDTH - 1) + kk
                prod = dct * glp_ref[goff:goff + tt, :]
                acc_ref[8 * kk:8 * kk + 8, :] += jnp.sum(prod.reshape(tt // 8, 8, cb), axis=0)
            at = ua_ref[t0:t0 + tt, :].astype(F32)
            st = _sigmoid(ug_ref[t0:t0 + tt, :].astype(F32))
            da = dgl * st
            dg = dgl * at * st * (1.0 - st)
            du_ref[0, t0:t0 + tt, :] = da.astype(BF16)
            du_ref[1, t0:t0 + tt, :] = dg.astype(BF16)
            dba = dba + jnp.sum(da, axis=0, keepdims=True)
            dbg = dbg + jnp.sum(dg, axis=0, keepdims=True)
        dba_ref[...] = dba
        dbg_ref[...] = dbg
        for kk in range(CONV_WIDTH):
            dw_ref[kk:kk + 1, :] = jnp.sum(acc_ref[8 * kk:8 * kk + 8, :], axis=0, keepdims=True)
        dw_ref[CONV_WIDTH:, :] = jnp.zeros((CONV_PAD - CONV_WIDTH, cb), F32)

    blk = pl.BlockSpec((s, cb), lambda j: (0, j))
    vec = pl.BlockSpec((1, cb), lambda j: (0, j))
    return pl.pallas_call(
        body, name="dwconv_bwd", grid=(nblk,),
        in_specs=[blk, pl.BlockSpec((s, cb), lambda j: (0, j + nblk)),
                  pl.BlockSpec((CONV_PAD, cb), lambda j: (0, j)), blk],
        out_specs=[pl.BlockSpec((2, s, cb), lambda j: (0, 0, j)), pl.BlockSpec((CONV_PAD, cb), lambda j: (0, j)),
                   vec, vec],
        out_shape=[jax.ShapeDtypeStruct((2, s, d), BF16), jax.ShapeDtypeStruct((CONV_PAD, d), F32),
                   jax.ShapeDtypeStruct((1, d), F32), jax.ShapeDtypeStruct((1, d), F32)],
        scratch_shapes=[pltpu.VMEM((s + CONV_PAD, cb), F32), pltpu.VMEM((s + CONV_PAD, cb), F32),
                        pltpu.VMEM((8 * CONV_PAD, cb), F32)],
        compiler_params=_params("parallel"),
    )(u, u, w_dw, dc)


def _stack_heads(x, group):
    return jnp.concatenate([x[:, g * HEAD_DIM:(g + 1) * HEAD_DIM] for g in range(group)], axis=0)


def _unstack_heads(x, group):
    return jnp.concatenate([x[g * ATT_BLOCK:(g + 1) * ATT_BLOCK, :] for g in range(group)], axis=1)


def _stack_cols(x, group):
    return jnp.concatenate([x[:, g:g + 1] for g in range(group)], axis=0)


def _band_bias(group):
    rows = group * ATT_BLOCK
    row = lax.broadcasted_iota(I32, (rows, 2 * ATT_BLOCK), 0) % ATT_BLOCK
    col = lax.broadcasted_iota(I32, (rows, 2 * ATT_BLOCK), 1)
    band = jnp.where((col >= row) & (col <= row + ATT_BLOCK), 0.0, -jnp.inf).astype(F32)
    first = jnp.where(lax.broadcasted_iota(I32, (1, 2 * ATT_BLOCK), 1) >= ATT_BLOCK, 0.0, -jnp.inf).astype(F32)
    return band, first


def _masked_scores(qs, kw, band_ref, first_ref, nb, scale):
    sc = lax.dot_general(qs, kw, (((1,), (1,)), ((), ())), preferred_element_type=F32) * scale + band_ref[...]
    return sc + jnp.where(nb > 0, 0.0, first_ref[...])


def _window(ref, nb):
    prev = pl.multiple_of(jnp.maximum(nb - 1, 0) * ATT_BLOCK, ATT_BLOCK)
    cur = pl.multiple_of(nb * ATT_BLOCK, ATT_BLOCK)
    return jnp.concatenate([ref[pl.ds(prev, ATT_BLOCK), :], ref[pl.ds(cur, ATT_BLOCK), :]], axis=0)


def _residues_per_step(dil, nblk):
    return max(1, min(dil, ATT_STEP_BLOCKS // nblk))


def _attn_fwd(name, q, kv):
    dil, sd, d = q.shape
    group = d // HEAD_DIM // N_KV_HEADS
    gw = group * HEAD_DIM
    nblk = sd // ATT_BLOCK
    scale = 1.0 / math.sqrt(HEAD_DIM)
    nt = (((1,), (1,)), ((), ()))

    rb = _residues_per_step(dil, nblk)

    def body(q_all, k_all, v_all, band_ref, first_ref, o_all, lse_all):
        lane = lax.broadcasted_iota(I32, (ATT_BLOCK, LANES), 1)
        for rr in range(rb):
            q_ref, k_ref, v_ref, o_ref, lse_ref = [ref.at[rr] for ref in (q_all, k_all, v_all, o_all, lse_all)]

            def step(nb, carry):
                rows = pl.ds(pl.multiple_of(nb * ATT_BLOCK, ATT_BLOCK), ATT_BLOCK)
                qs = _stack_heads(q_ref[rows, :], group)
                kw = _window(k_ref, nb)
                vw = _window(v_ref, nb)
                sc = _masked_scores(qs, kw, band_ref, first_ref, nb, scale)
                mx = jnp.max(sc, axis=-1, keepdims=True)
                p = jnp.exp(sc - mx)
                l = jnp.sum(p, axis=-1, keepdims=True)
                o = jnp.dot(p.astype(BF16), vw, preferred_element_type=F32) / l
                o_ref[rows, :] = _unstack_heads(o, group).astype(BF16)
                lse = mx + jnp.log(l)
                out = jnp.zeros((ATT_BLOCK, LANES), F32)
                for g in range(group):
                    out = jnp.where(lane == g, lse[g * ATT_BLOCK:(g + 1) * ATT_BLOCK, :], out)
                lse_ref[rows, :] = out
                return carry

            lax.fori_loop(0, nblk, step, 0, unroll=min(2, nblk))

    kvh = N_KV_HEADS
    band, first = _band_bias(group)
    qspec = pl.BlockSpec((rb, sd, gw), lambda r, h: (r, 0, h))
    kspec = pl.BlockSpec((rb, sd, HEAD_DIM), lambda r, h: (r, 0, h))
    return pl.pallas_call(
        body, name=name, grid=(dil // rb, kvh),
        in_specs=[qspec, kspec, pl.BlockSpec((rb, sd, HEAD_DIM), lambda r, h: (r, 0, kvh + h)),
                  pl.BlockSpec(band.shape, lambda r, h: (0, 0)), pl.BlockSpec(first.shape, lambda r, h: (0, 0))],
        out_specs=[qspec, kspec],
        out_shape=[jax.ShapeDtypeStruct((dil, sd, d), BF16),
                   jax.ShapeDtypeStruct((dil, sd, kvh * LANES), F32)],
        compiler_params=_params("parallel", "parallel"),
    )(q, kv, kv, band, first)


def _attn_bwd(name, q, kv, do, lse, delta):
    dil, sd, d = q.shape
    group = d // HEAD_DIM // N_KV_HEADS
    gw = group * HEAD_DIM
    nblk = sd // ATT_BLOCK
    scale = 1.0 / math.sqrt(HEAD_DIM)
    nt = (((1,), (1,)), ((), ()))
    tn = (((0,), (0,)), ((), ()))

    rb = _residues_per_step(dil, nblk)

    def body(q_all, k_all, v_all, do_all, lse_all, dl_all, band_ref, first_ref, dq_all, dk_all, dv_all, dk_accs,
             dv_accs):
        dk_accs[...] = jnp.zeros_like(dk_accs)
        dv_accs[...] = jnp.zeros_like(dv_accs)
        for rr in range(rb):
            q_ref, k_ref, v_ref, do_ref, lse_ref, dl_ref, dq_ref, dk_ref, dv_ref, dk_acc, dv_acc = [
                ref.at[rr] for ref in (q_all, k_all, v_all, do_all, lse_all, dl_all, dq_all, dk_all, dv_all,
                                       dk_accs, dv_accs)]

            def step(nb, carry):
                rows = pl.ds(pl.multiple_of(nb * ATT_BLOCK, ATT_BLOCK), ATT_BLOCK)
                qs = _stack_heads(q_ref[rows, :], group)
                dos = _stack_heads(do_ref[rows, :], group)
                ls = _stack_cols(lse_ref[rows, :], group)
                dl = _stack_cols(dl_ref[rows, :], group)
                kw = _window(k_ref, nb)
                vw = _window(v_ref, nb)
                p = jnp.exp(_masked_scores(qs, kw, band_ref, first_ref, nb, scale) - ls)
                dp = lax.dot_general(dos, vw, nt, preferred_element_type=F32)
                ds = (p * (dp - dl) * scale).astype(BF16)
                dq = jnp.dot(ds, kw, preferred_element_type=F32)
                dq_ref[rows, :] = _unstack_heads(dq, group).astype(BF16)
                win = pl.ds(pl.multiple_of(nb * ATT_BLOCK, ATT_BLOCK), 2 * ATT_BLOCK)
                dk_acc[win, :] += lax.dot_general(ds, qs, tn, preferred_element_type=F32)
                dv_acc[win, :] += lax.dot_general(p.astype(BF16), dos, tn, preferred_element_type=F32)
                return carry

            lax.fori_loop(0, nblk, step, 0, unroll=min(2, nblk))
            dk_ref[...] = dk_acc[ATT_BLOCK:, :]
            dv_ref[...] = dv_acc[ATT_BLOCK:, :]

    kvh = N_KV_HEADS
    band, first = _band_bias(group)
    qspec = pl.BlockSpec((rb, sd, gw), lambda r, h: (r, 0, h))
    kspec = pl.BlockSpec((rb, sd, HEAD_DIM), lambda r, h: (r, 0, h))
    return pl.pallas_call(
        body, name=name, grid=(dil // rb, kvh),
        in_specs=[qspec, kspec, pl.BlockSpec((rb, sd, HEAD_DIM), lambda r, h: (r, 0, kvh + h)),
                  qspec, kspec, kspec,
                  pl.BlockSpec(band.shape, lambda r, h: (0, 0)), pl.BlockSpec(first.shape, lambda r, h: (0, 0))],
        out_specs=[qspec, kspec, kspec],
        out_shape=[jax.ShapeDtypeStruct((dil, sd, d), BF16),
                   jax.ShapeDtypeStruct((dil, sd, kvh * HEAD_DIM), F32),
                   jax.ShapeDtypeStruct((dil, sd, kvh * HEAD_DIM), F32)],
        scratch_shapes=[pltpu.VMEM((rb, sd + ATT_BLOCK, HEAD_DIM), F32)] * 2,
        compiler_params=_params("parallel", "parallel"),
    )(q, kv, kv, do, lse, delta, band, first)


def _cast_bf16(name, w, layer, place, after=None):
    _, r, c = w.shape
    tr = min(512, r)
    deps = [] if after is None else [after]

    def body(pl_ref, w_ref, *refs):
        refs[-1][...] = w_ref[...].astype(BF16)

    return pl.pallas_call(
        body, name=name,
        grid_spec=pltpu.PrefetchScalarGridSpec(
            num_scalar_prefetch=1, grid=(r // tr,),
            in_specs=[pl.BlockSpec((None, tr, c), lambda i, p: (layer, i, 0))] + [ANY] * len(deps),
            out_specs=pl.BlockSpec((None, tr, c), lambda i, p: (p[1], i, 0))),
        out_shape=jax.ShapeDtypeStruct((N_SHARD, r, c), BF16),
        compiler_params=_params("parallel"),
    )(place, w, *deps)


def _chip_sum(name, g, rh, place):
    _, r, c = g.shape
    rh2 = r // 2
    tr = min(512, rh2)
    nb = rh2 // tr

    def body(pl_ref, g_ref, rh_ref, o_ref):
        o_ref[...] = (g_ref[...].astype(F32) + rh_ref[...].astype(F32)).astype(BF16)

    return pl.pallas_call(
        body, name=name,
        grid_spec=pltpu.PrefetchScalarGridSpec(
            num_scalar_prefetch=1, grid=(N_SHARD, nb),
            in_specs=[pl.BlockSpec((None, tr, c), lambda s, i, p: (s, p[0] * nb + i, 0)),
                      pl.BlockSpec((None, tr, c), lambda s, i, p: (s, i, 0))],
            out_specs=pl.BlockSpec((None, tr, c), lambda s, i, p: (s, i, 0))),
        out_shape=jax.ShapeDtypeStruct((N_SHARD, rh2, c), BF16),
        compiler_params=_params("parallel", "parallel"),
    )(place, g, rh)


def _owner_sum(name, cs, rp, place):
    _, rh2, c = cs.shape
    tr = min(512, rh2)
    nb = rh2 // tr

    def body(pl_ref, cs_ref, r0_ref, r1_ref, r2_ref, o_ref):
        o_ref[...] = ((cs_ref[...].astype(F32) + r0_ref[...].astype(F32))
                      + (r1_ref[...].astype(F32) + r2_ref[...].astype(F32)))

    def rspec(j):
        return pl.BlockSpec((None, tr, c), lambda i, p: (j, i, 0))

    return pl.pallas_call(
        body, name=name,
        grid_spec=pltpu.PrefetchScalarGridSpec(
            num_scalar_prefetch=1, grid=(nb,),
            in_specs=[pl.BlockSpec((None, tr, c), lambda i, p: (p[1], i, 0)), rspec(0), rspec(1), rspec(2)],
            out_specs=pl.BlockSpec((tr, c), lambda i, p: (p[0] * nb + i, 0))),
        out_shape=jax.ShapeDtypeStruct((2 * rh2, c), F32),
        compiler_params=_params("parallel"),
    )(place, cs, rp, rp, rp)


def _adam_math(w, g, m, v):
    m = ADAM_B1 * m + (1.0 - ADAM_B1) * g
    v = ADAM_B2 * v + (1.0 - ADAM_B2) * (g * g)
    m_hat = m / (1.0 - ADAM_B1 ** ADAM_STEP)
    v_hat = v / (1.0 - ADAM_B2 ** ADAM_STEP)
    delta = -ADAM_LR * (m_hat / (jnp.sqrt(v_hat) + ADAM_EPS) + ADAM_WD * w)
    return delta, m, v


def _adamw(name, w, m, v, g, layer, partial=None):
    nl, r, c = w.shape
    tr = min(256, r)

    def body(w_ref, m_ref, v_ref, g_ref, *refs):
        go_ref, d_ref, mo_ref, vo_ref = refs[-4:]
        gv = g_ref[...]
        delta, m_new, v_new = _adam_math(w_ref[...], gv, m_ref[...], v_ref[...])
        go_ref[...] = gv
        d_ref[...] = delta
        mo_ref[...] = m_new
        vo_ref[...] = v_new

    wspec = pl.BlockSpec((None, tr, c), lambda i: (layer, i, 0))
    prev = [] if partial is None else list(partial)
    return pl.pallas_call(
        body, name=name, grid=(r // tr,),
        in_specs=[wspec] * 3 + [pl.BlockSpec((tr, c), lambda i: (i, 0))] + [ANY] * len(prev),
        out_specs=[wspec] * 4,
        out_shape=[jax.ShapeDtypeStruct((nl, r, c), F32)] * 4,
        input_output_aliases={4 + i: i for i in range(len(prev))},
        compiler_params=_params("parallel"),
    )(w, m, v, g, *prev)


def _adam_small(ws, ms, vs, gs):
    n = len(ws)

    def body(*refs):
        w_refs, m_refs, v_refs, g_refs = refs[:n], refs[n:2 * n], refs[2 * n:3 * n], refs[3 * n:4 * n]
        d_refs, mo_refs, vo_refs = refs[4 * n:5 * n], refs[5 * n:6 * n], refs[6 * n:7 * n]
        for i in range(n):
            delta, m_new, v_new = _adam_math(w_refs[i][...], g_refs[i][...], m_refs[i][...], v_refs[i][...])
            d_refs[i][...] = delta
            mo_refs[i][...] = m_new
            vo_refs[i][...] = v_new

    shapes = [jax.ShapeDtypeStruct(w.shape, F32) for w in ws]
    res = pl.pallas_call(body, name="adam_small", out_shape=shapes * 3)(*ws, *ms, *vs, *gs)
    return res[:n], res[n:2 * n], res[2 * n:]


def _pack_small(b_in, w_dw, b_dw, ln_g, ln_b, b_out, place):
    cin = b_in.shape[1]
    cd = b_dw.shape[1]
    rows = 8 + CONV_PAD

    def body(pl_ref, bi, wd, bd, lg, lb, bo, out):
        out[...] = jnp.zeros_like(out)
        out[0:1, :] = bi[...]
        out[1:2, 0:cd] = bd[...]
        out[1:2, cd:2 * cd] = lg[...]
        out[2:3, 0:cd] = lb[...]
        out[2:3, cd:2 * cd] = bo[...]
        out[8:8 + CONV_WIDTH, 0:cd] = wd[...]

    def whole(arr):
        return pl.BlockSpec(arr.shape, lambda i, p: (0,) * arr.ndim)

    ins = [b_in, w_dw, b_dw, ln_g, ln_b, b_out]
    return pl.pallas_call(
        body, name="pack_small",
        grid_spec=pltpu.PrefetchScalarGridSpec(
            num_scalar_prefetch=1, grid=(1,), in_specs=[whole(a) for a in ins],
            out_specs=pl.BlockSpec((None, rows, cin), lambda i, p: (p[1], 0, 0))),
        out_shape=jax.ShapeDtypeStruct((N_SHARD, rows, cin), F32),
        compiler_params=_params("arbitrary"),
    )(place, *ins)


def _place():
    x, y, c = lax.axis_index("x"), lax.axis_index("y"), lax.axis_index("c")
    return x, y, c


def _other_chips(x, y):
    return [(1 - x, y), (x, 1 - y), (1 - x, 1 - y)]


def _split_start_many(name, parts, after=None):
    flat = [b for bufs, _, _ in parts for b in bufs]
    n, n_parts = len(flat), len(parts)
    deps = [] if after is None else [after]

    def body(*refs):
        out0 = n + len(deps)
        pos = 0
        for i, (bufs, _, copies) in enumerate(parts):
            for cp in copies(refs[pos:pos + len(bufs)], refs[out0 + 2 * i], refs[out0 + 2 * i + 1], False):
                cp.start()
            pos += len(bufs)
        refs[-1][...] = jnp.zeros_like(refs[-1])

    sems = [pltpu.SemaphoreType.DMA((n_sem,)) for _, n_sem, _ in parts for _ in range(2)]
    res = pl.pallas_call(
        body, name=name,
        out_shape=(*sems, *[pltpu.HBM(b.shape, b.dtype) for b in flat], jax.ShapeDtypeStruct((8, LANES), F32)),
        in_specs=[HBM] * n + [ANY] * len(deps),
        out_specs=(*[SEM] * (2 * n_parts), *[HBM] * n, pl.BlockSpec(memory_space=pltpu.VMEM)),
        input_output_aliases={i: 2 * n_parts + i for i in range(n)},
        compiler_params=pltpu.CompilerParams(has_side_effects=SPLIT_EFFECT),
    )(*[pltpu.with_memory_space_constraint(b, pltpu.HBM) for b in flat], *deps)
    handles, pos = [], 2 * n_parts
    for i, (bufs, _, _) in enumerate(parts):
        handles.append((res[2 * i], res[2 * i + 1], list(res[pos:pos + len(bufs)]), res[-1]))
        pos += len(bufs)
    return handles


def _split_start(name, bufs, n_sem, copies, after=None):
    return _split_start_many(name, [(bufs, n_sem, copies)], after)[0]


def _split_wait(name, handle, copies, after):
    ssem, rsem, bufs, _ = handle
    n = len(bufs)
    deps = list(after) if isinstance(after, (list, tuple)) else [after]

    def body(*refs):
        for cp in copies(refs[:n], refs[n], refs[n + 1], True):
            cp.wait_send()
            cp.wait_recv()

    res = pl.pallas_call(
        body, name=name,
        out_shape=[pltpu.HBM(b.shape, b.dtype) for b in bufs],
        in_specs=[HBM] * n + [SEM, SEM] + [ANY] * len(deps), out_specs=[HBM] * n,
        input_output_aliases={i: i for i in range(n)},
        compiler_params=pltpu.CompilerParams(has_side_effects=SPLIT_EFFECT),
    )(*bufs, ssem, rsem, *deps)
    return list(res)


def _remote(src, dst, ssem, rsem, k, to):
    return pltpu.make_async_remote_copy(src_ref=src, dst_ref=dst, send_sem=ssem.at[k], recv_sem=rsem.at[k],
                                        device_id=to, device_id_type=MESH)


def _gather_chips(x, y, c):
    nx, ny = x + (1 - c) - 2 * x * (1 - c), y + c - 2 * y * c
    fx, fy = x + c - 2 * x * c, y + (1 - c) - 2 * y * (1 - c)
    return (nx, ny), (fx, fy), 2 * nx + ny, 2 * fx + fy, 2 * (1 - x) + (1 - y)


def _direct_copies(refs, ssem, rsem, landing, n_whole=0):
    x, y, c = _place()
    me = 2 * x + y
    (nx, ny), _, near, _, _ = _gather_chips(x, y, c)
    n = len(refs) - n_whole
    cps = []
    for a, ref in enumerate(refs[:n]):
        cps.append(_remote(ref.at[me], ref.at[near if landing else me], ssem, rsem, a, (nx, ny, c)))
    for b, ref in enumerate(refs[n:]):
        for j, (px, py) in enumerate(_other_chips(x, y)):
            cps.append(_remote(ref.at[me], ref.at[2 * px + py if landing else me], ssem, rsem, n + 3 * b + j,
                               (px, py, c)))
    return cps


def _relay_copies(refs, ssem, rsem, landing):
    x, y, c = _place()
    _, (fx, fy), near, far, diag = _gather_chips(x, y, c)
    n = len(refs)
    cps = []
    for a, ref in enumerate(refs):
        rh = ref.shape[1] // 2
        rows = pl.ds(c * rh, rh)
        cps.append(_remote(ref.at[near, rows], ref.at[diag if landing else near, rows], ssem, rsem, a, (fx, fy, c)))
        cps.append(_remote(ref.at[near], ref.at[far if landing else near], ssem, rsem, n + a, (x, y, 1 - c)))
    return cps


def _diagonal_copies(refs, ssem, rsem, landing):
    x, y, c = _place()
    diag = 2 * (1 - x) + (1 - y)
    who = 1 - c if landing else c
    cps = []
    for a, ref in enumerate(refs):
        rh = ref.shape[1] // 2
        piece = ref.at[diag, pl.ds(who * rh, rh)]
        cps.append(_remote(piece, piece, ssem, rsem, a, (x, y, 1 - c)))
    return cps


def _sibling_copies(refs, ssem, rsem, landing):
    x, y, c = _place()
    n = len(refs) // 2
    cps = []
    for a in range(n):
        rh = refs[a].shape[1] // 2
        cps.append(_remote(refs[a].at[:, pl.ds((1 - c) * rh, rh), :], refs[n + a], ssem, rsem, a, (x, y, 1 - c)))
    return cps


def _owner_copies(refs, ssem, rsem, landing):
    x, y, c = _place()
    n = len(refs) // 2
    cps = []
    for a in range(n):
        for j, (px, py) in enumerate(_other_chips(x, y)):
            cps.append(_remote(refs[a].at[2 * px + py], refs[n + a].at[j], ssem, rsem, 3 * a + j, (px, py, c)))
    return cps


def _swap_copies(refs, ssem, rsem, landing):
    x, y, c = _place()
    who = 1 - c if landing else c
    cps = []
    for a, ref in enumerate(refs):
        rh = ref.shape[0] // 2
        rows = ref.at[pl.ds(who * rh, rh)]
        cps.append(_remote(rows, rows, ssem, rsem, a, (x, y, 1 - c)))
    return cps


def _small_copies(refs, ssem, rsem, landing):
    pack, slots = refs
    x, y, c = _place()
    cps = []
    for rel in range(1, N_DEV):
        px = 1 - x if (rel >> 2) & 1 else x
        py = 1 - y if (rel >> 1) & 1 else y
        pc = 1 - c if rel & 1 else c
        slot = 4 * px + 2 * py + pc if landing else 4 * x + 2 * y + c
        cps.append(_remote(pack, slots.at[slot], ssem, rsem, rel - 1, (px, py, pc)))
    return cps


def _small_pack(rows, w_dw_grad, d):
    n = len(rows)

    def body(*refs):
        pack = refs[-1]
        pack[...] = jnp.zeros_like(pack)
        for (r, _), ref in zip(rows, refs[:n]):
            pack[r:r + 1, :] = ref[...]
        pack[16:16 + CONV_PAD, :] = refs[n][...]

    return pl.pallas_call(body, name="small_pack", out_shape=jax.ShapeDtypeStruct((SMALL_ROWS, d), F32))(
        *[v for _, v in rows], w_dw_grad)


def _small_sum(pack, slots, place):
    rows, d = pack.shape
    loss_row = 12

    def body(pl_ref, pack_ref, slots_ref, out_ref):
        me = pl_ref[2]
        tot = jnp.where(me == 0, pack_ref[...], slots_ref[0])
        for i in range(1, N_DEV):
            tot = tot + jnp.where(me == i, pack_ref[...], slots_ref[i])
        out_ref[...] = tot
        out_ref[loss_row:loss_row + 1, :] = jnp.zeros((1, d), F32) + jnp.sum(tot[loss_row:loss_row + 1, :])

    return pl.pallas_call(
        body, name="small_sum",
        grid_spec=pltpu.PrefetchScalarGridSpec(
            num_scalar_prefetch=1, grid=(1,),
            in_specs=[pl.BlockSpec((rows, d), lambda i, p: (0, 0)), pl.BlockSpec((N_DEV, rows, d), lambda i, p: (0, 0, 0))],
            out_specs=pl.BlockSpec((rows, d), lambda i, p: (0, 0))),
        out_shape=jax.ShapeDtypeStruct((rows, d), F32),
        compiler_params=_params("arbitrary"),
    )(place, pack, slots)


def kernel(x, norm_mix, norm_mlp, conv_w_in, conv_b_in, conv_w_dw, conv_b_dw, conv_ln_g, conv_ln_b, conv_w_out, conv_b_out, kv_norm, w_kv, attn_w_q, attn_w_o, mlp_w_in, mlp_w_out, final_norm, loss_target, m_norm_mix, m_norm_mlp, m_conv_w_in, m_conv_b_in, m_conv_w_dw, m_conv_b_dw, m_conv_ln_g, m_conv_ln_b, m_conv_w_out, m_conv_b_out, m_kv_norm, m_w_kv, m_attn_w_q, m_attn_w_o, m_mlp_w_in, m_mlp_w_out, m_final_norm, v_norm_mix, v_norm_mlp, v_conv_w_in, v_conv_b_in, v_conv_w_dw, v_conv_b_dw, v_conv_ln_g, v_conv_ln_b, v_conv_w_out, v_conv_b_out, v_kv_norm, v_w_kv, v_attn_w_q, v_attn_w_o, v_mlp_w_in, v_mlp_w_out, v_final_norm):
    _, s, d = x.shape
    dff = mlp_w_in.shape[2] * N_SHARD
    kvw = w_kv.shape[1]
    nh = d // HEAD_DIM
    group = nh // N_KV_HEADS
    ds4 = d // N_SHARD
    xi, yi, ci = _place()
    me = 2 * xi + yi
    place = jnp.stack([ci, me, 2 * me + ci]).astype(I32)

    h0 = x.reshape(s, d)
    target = loss_target.reshape(s, d)
    tabs = _rope_tables(s)

    def gather_begin(tag, bufs, n_whole=0):
        plan = functools.partial(_direct_copies, n_whole=n_whole)
        return _split_start(f"gather_start_{tag}", bufs, len(bufs) + 2 * n_whole, plan), plan, n_whole

    def gather_step(later, land=None, swap=None):
        parts, names, whole = [], [], {}
        if land is not None:
            tag, (handle, plan, n_whole) = land
            bufs = _split_wait(f"gather_wait_{tag}", handle, plan, later)
            n = len(bufs) - n_whole
            parts.append((bufs[:n], 2 * n, _relay_copies))
            whole["land"] = bufs[n:]
            names.append(f"relay_{tag}")
        if swap is not None:
            tag, (relayed, whole["swap"]) = swap
            bufs = _split_wait(f"relay_wait_{tag}", relayed, _relay_copies, later)
            parts.append((bufs, len(bufs), _diagonal_copies))
            names.append(f"diagonal_{tag}")
        handles = _split_start_many("start_" + "_".join(names), parts)
        landed = (handles[0], whole["land"]) if land is not None else None
        swapped = (handles[-1], whole["swap"]) if swap is not None else None
        return landed, swapped

    def gather_land(tag, begun, later):
        return gather_step(later, land=(tag, begun))[0]

    def gather_swap(tag, landed, later):
        return gather_step(later, swap=(tag, landed))[1]

    def gather_end(tag, swapped, later):
        handle, whole = swapped
        return _split_wait(f"diagonal_wait_{tag}", handle, _diagonal_copies, later) + whole

    def tied(vec, begun):
        return vec + begun[0][3][0:1, 0:1]

    ag_cin = gather_begin("conv_in", [
        _cast_bf16("cast_w_in", conv_w_in, 0, place),
        _pack_small(conv_b_in, conv_w_dw.reshape(CONV_WIDTH, ds4), conv_b_dw, conv_ln_g, conv_ln_b, conv_b_out, place),
    ], n_whole=1)
    ag_cout = gather_begin("conv_out", [_cast_bf16("cast_w_out", conv_w_out, 0, place, ag_cin[0][3])])
    ag_mi0 = gather_begin("mlp_in0", [_cast_bf16("cast_mlp_in0", mlp_w_in, 0, place, ag_cout[0][3])])
    ag_mo0 = gather_begin("mlp_out0", [_cast_bf16("cast_mlp_out0", mlp_w_out, 0, place, ag_mi0[0][3])])
    nm = [norm_mix[0:1], norm_mix[1:2]]
    nmlp = [norm_mlp[0:1], norm_mlp[1:2]]
    kvn = kv_norm.reshape(1, d)
    fin = final_norm.reshape(1, d)
    (y0,) = _rms_fwd("rms_mix0", h0, [tied(nm[0], ag_mo0)])
    land_cin = gather_land("conv_in", ag_cin, y0)
    ag_attn = gather_begin("attn", [
        _cast_bf16("cast_w_kv", w_kv.reshape(1, ds4, kvw), 0, place, land_cin[0][3]),
        _cast_bf16("cast_w_q", attn_w_q, 0, place), _cast_bf16("cast_w_o", attn_w_o, 0, place)])
    ag_mi1 = gather_begin("mlp_in1", [_cast_bf16("cast_mlp_in1", mlp_w_in, 1, place, ag_attn[0][3])])
    ag_mo1 = gather_begin("mlp_out1", [_cast_bf16("cast_mlp_out1", mlp_w_out, 1, place, ag_mi1[0][3])])
    land_cout, swap_cin = gather_step(ag_mo1[0][3], land=("conv_out", ag_cout), swap=("conv_in", land_cin))

    wmi_g = [None, None]
    wmo_f = [None, None]

    w_in_g, small_g = gather_end("conv_in", swap_cin, swap_cin[0][3])
    b_in_f = small_g[:, 0, :].reshape(1, 2 * d)
    b_dw_f = small_g[:, 1, 0:ds4].reshape(1, d)
    ln_g_f = small_g[:, 1, ds4:2 * ds4].reshape(1, d)
    ln_b_f = small_g[:, 2, 0:ds4].reshape(1, d)
    b_out_f = small_g[:, 2, ds4:2 * ds4].reshape(1, d)
    w_dw_f = jnp.transpose(small_g[:, 8:8 + CONV_PAD, 0:ds4], (1, 0, 2)).reshape(CONV_PAD, d)

    def ep_bias(acc, ex, outs, j):
        outs[0][...] = (acc + ex[0][...]).astype(outs[0].dtype)

    def ep_residual(acc, ex, outs, j):
        outs[0][...] = ex[0][...] + acc

    def ep_residual_bias(acc, ex, outs, j):
        outs[0][...] = ex[0][...] + (acc + ex[1][...])

    def ep_relu2(acc, ex, outs, j):
        r = jnp.maximum(acc, 0.0)
        outs[0][...] = r.astype(BF16)
        outs[1][...] = (r * r).astype(BF16)

    by_residue = [(BF16, ("residues", dil)) for dil in DILATIONS]

    def put_by_residue(val, outs, stage):
        _to_residues(val, stage, outs, DILATIONS)

    def ep_rope(acc, ex, outs, j, stage):
        put_by_residue(_rope_apply(acc, ex[0][...], ex[1][...], ex[2][...], 1.0), outs, stage)

    def ep_rope_k(acc, ex, outs, j, stage):
        roped = _rope_apply(acc, ex[0][...], ex[1][...], ex[2][...], 1.0)
        put_by_residue(jnp.where(j == 0, roped, acc), outs, stage)

    def ep_by_residue(acc, ex, outs, j, stage):
        put_by_residue(acc, outs, stage)

    tab_extras = [(t, "rows") for t in tabs]

    def mlp_fwd(idx, h, y, out_weight):
        r, r2 = _matmul(f"mlp_in{idx}", "nn", y, wmi_g[idx], b_kind="col", m=s, n=dff, k=d,
                        outs=[(BF16, "plain"), (BF16, "plain")], epilogue=ep_relu2)
        wmo_f[idx] = out_weight(r2).reshape(dff, d)
        (h_new,) = _matmul(f"mlp_out{idx}", "nn", r2, wmo_f[idx], m=s, n=d, k=dff,
                           outs=[(F32, "plain")], extras=[(h, "ij")], epilogue=ep_residual)
        return h_new, r, r2

    (u,) = _matmul("conv_in", "nn", y0, w_in_g, b_kind="col", m=s, n=2 * d, k=d,
                   outs=[(BF16, "plain")], extras=[(b_in_f, "vec")], epilogue=ep_bias)
    land_mi0, swap_cout = gather_step(u, land=("mlp_in0", ag_mi0), swap=("conv_out", land_cout))
    cpre = _dwconv_fwd(u, w_dw_f, tied(b_dw_f, swap_cout))
    sact = _ln_silu_fwd(cpre, ln_g_f, ln_b_f)
    (w_out_g,) = gather_end("conv_out", swap_cout, sact)
    w_out_f = w_out_g.reshape(d, d)
    (h1,) = _matmul("conv_out", "nn", sact, w_out_f, m=s, n=d, k=d,
                    outs=[(F32, "plain")], extras=[(h0, "ij"), (b_out_f, "vec")], epilogue=ep_residual_bias)
    swap_mi0 = gather_swap("mlp_in0", land_mi0, h1)
    (y1,) = _rms_fwd("rms_mlp0", h1, [tied(nmlp[0], swap_mi0)])
    land_mo0 = gather_land("mlp_out0", ag_mo0, y1)
    (wmi_g[0],) = gather_end("mlp_in0", swap_mi0, land_mo0[0][3])
    land_attn = None

    def out_weight0(r2):
        nonlocal land_attn
        land_attn, swap_mo0 = gather_step(r2, land=("attn", ag_attn), swap=("mlp_out0", land_mo0))
        return gather_end("mlp_out0", swap_mo0, swap_mo0[0][3])[0]

    h2, r0, r0sq = mlp_fwd(0, h1, y1, out_weight0)
    land_mi1, swap_attn = gather_step(h2, land=("mlp_in1", ag_mi1), swap=("attn", land_attn))
    ykv, y2 = _rms_fwd("rms_kv_mix1", h2, [tied(kvn, land_mi1), nm[1]])
    wkv_g, wq_g, wo_g = gather_end("attn", swap_attn, y2)
    wkv_f, wq_f, wo_f = wkv_g.reshape(d, kvw), wq_g.reshape(d, d), wo_g.reshape(d, d)
    kv_parts = _matmul("kv_proj", "nn", ykv, wkv_f, m=s, n=kvw, k=d, tn=kvw // 2,
                       outs=by_residue, extras=tab_extras, epilogue=ep_rope_k, stage=True)
    q_parts = _matmul("q_proj", "nn", y2, wq_f, m=s, n=d, k=d,
                      outs=by_residue, extras=tab_extras, epilogue=ep_rope, stage=True)
    o_parts, lse_parts = [], []
    for dil, q_b, kv_b in zip(DILATIONS, q_parts, kv_parts):
        o_b, lse_b = _attn_fwd(f"attn_fwd_d{dil}", q_b, kv_b)
        o_parts.append(o_b)
        lse_parts.append(lse_b)
    o, lse = _attn_combine(o_parts, lse_parts)
    land_mo1, swap_mi1 = gather_step(o, land=("mlp_out1", ag_mo1), swap=("mlp_in1", land_mi1))
    (h3,) = _matmul("attn_out", "nn", o, wo_f, m=s, n=d, k=d,
                    outs=[(F32, "plain")], extras=[(h2, "ij")], epilogue=ep_residual)
    (y3,) = _rms_fwd("rms_mlp1", h3, [tied(nmlp[1], land_mo1)])
    (wmi_g[1],) = gather_end("mlp_in1", swap_mi1, y3)

    def out_weight1(r2):
        swap_mo1 = gather_swap("mlp_out1", land_mo1, r2)
        return gather_end("mlp_out1", swap_mo1, swap_mo1[0][3])[0]

    h4, r1, r1sq = mlp_fwd(1, h3, y3, out_weight1)
    dh4, dh4b, d_fin, loss_cols = _final_loss(h4, fin, target)

    def ep_relu2_bwd(acc, ex, outs, j):
        outs[0][...] = (acc * (2.0 * ex[0][...].astype(F32))).astype(BF16)

    def mlp_bwd(idx, dhb, y, r, r2):
        (dz,) = _matmul(f"mlp_out{idx}_dx", "nt", dhb, wmo_f[idx], m=s, n=dff, k=d,
                        outs=[(BF16, "plain")], extras=[(r, "ij")], epilogue=ep_relu2_bwd)
        (dwo,) = _matmul(f"mlp_out{idx}_dw", "tn", r2, dhb, m=dff, n=d, k=s,
                         outs=[(BF16, "plain")])
        (dy,) = _matmul(f"mlp_in{idx}_dx", "nt", dz, wmi_g[idx], b_kind="col", m=s, n=d, k=dff,
                        outs=[(BF16, "plain")])
        (dwi,) = _matmul(f"mlp_in{idx}_dw", "tn", y, dz, m=d, n=dff, k=s,
                         outs=[(BF16, "col")])
        return dy, dwi, dwo.reshape(N_SHARD, dff // N_SHARD, d)

    def token(handle):
        return handle[3][0:1, 0:1]

    def rs_exchange(tag, grads):
        lands = [lax.empty((N_SHARD, g.shape[1] // 2, g.shape[2]), g.dtype) for g in grads]
        return _split_start(f"sibling_start_{tag}", list(grads) + lands, len(grads), _sibling_copies)

    def rs_send(tag, names, exchanged, later):
        bufs = _split_wait(f"sibling_wait_{tag}", exchanged, _sibling_copies, later)
        n = len(names)
        sums = [_chip_sum(f"chip_sum_{nme}", g, rh, place) for nme, g, rh in zip(names, bufs[:n], bufs[n:])]
        lands = [lax.empty((N_SHARD - 1,) + cs.shape[1:], cs.dtype) for cs in sums]
        return _split_start(f"owners_start_{tag}", sums + lands, 3 * n, _owner_copies)

    def rs_sum(tag, names, sent, later):
        bufs = _split_wait(f"owners_wait_{tag}", sent, _owner_copies, later)
        n = len(names)
        own = [_owner_sum(f"owner_sum_{nme}", cs, rp, place) for nme, cs, rp in zip(names, bufs[:n], bufs[n:])]
        return _split_start(f"swap_start_{tag}", own, n, _swap_copies)

    def rs_end(tag, swapped, later):
        return _split_wait(f"swap_wait_{tag}", swapped, _swap_copies, later)

    dy3, g_wmi1, g_wmo1 = mlp_bwd(1, dh4b, y3, r1, r1sq)
    x_mlp1 = rs_exchange("mlp1", [g_wmi1, g_wmo1])
    dh3, dh3b, d_nmlp1 = _rms_bwd("rms_mlp1_bwd", h3, [(nmlp[1] + token(x_mlp1), dy3)], dh4)

    do_parts = _matmul("attn_out_dx", "nt", dh3b, wo_f, m=s, n=d, k=d, outs=by_residue, epilogue=ep_by_residue,
                       stage=True)
    (g_wo,) = _matmul("attn_out_dw", "tn", o, dh3b, m=d, n=d, k=s, outs=[(BF16, "plain")])
    rs_mlp1 = rs_send("mlp1", ["mlp_in1", "mlp_out1"], x_mlp1, g_wo)
    lse_res, delta_res = _attn_delta(do_parts[0].reshape(s, d), o, lse, DILATIONS)
    dq_parts, dk_parts, dv_parts = [], [], []
    for dil, q_b, kv_b, do_b, lse_b, dl_b in zip(DILATIONS, q_parts, kv_parts, do_parts, lse_res, delta_res):
        dq_b, dk_b, dv_b = _attn_bwd(f"attn_bwd_d{dil}", q_b, kv_b, do_b, lse_b, dl_b)
        dq_parts.append(dq_b)
        dk_parts.append(dk_b)
        dv_parts.append(dv_b)
    dq = _residue_sum("rope_bwd_q", [(dq_parts, True)], tabs)
    dkv = _residue_sum("rope_bwd_kv", [(dk_parts, True), (dv_parts, False)], tabs)
    (g_wq,) = _matmul("q_proj_dw", "tn", y2, dq, m=d, n=d, k=s, outs=[(BF16, "plain")])
    (dy2,) = _matmul("q_proj_dx", "nt", dq, wq_f, m=s, n=d, k=d, outs=[(BF16, "plain")])
    (g_wkv,) = _matmul("kv_proj_dw", "tn", ykv, dkv, m=d, n=kvw, k=s, outs=[(BF16, "plain")])
    (dykv,) = _matmul("kv_proj_dx", "nt", dkv, wkv_f, m=s, n=d, k=kvw, outs=[(BF16, "plain")])
    x_attn = rs_exchange("attn", [g_wkv.reshape(N_SHARD, ds4, kvw), g_wq.reshape(N_SHARD, ds4, d),
                                  g_wo.reshape(N_SHARD, ds4, d)])
    dh2, dh2b, d_nm1, d_kvn = _rms_bwd("rms_kv_mix1_bwd", h2, [(nm[1] + token(x_attn), dy2), (kvn, dykv)], dh3)
    rs_attn = rs_send("attn", ["w_kv", "w_q", "w_o"], x_attn, dh2b)

    dy1, g_wmi0, g_wmo0 = mlp_bwd(0, dh2b, y1, r0, r0sq)
    x_mlp0 = rs_exchange("mlp0", [g_wmi0, g_wmo0])
    dh1, dh1b, d_nmlp0, d_b_out = _rms_bwd("rms_mlp0_bwd", h1, [(nmlp[0] + token(x_mlp0) + token(rs_attn), dy1)],
                                           dh2, want_colsum=True)

    (dsact,) = _matmul("conv_out_dx", "nt", dh1b, w_out_f, m=s, n=d, k=d, outs=[(BF16, "plain")])
    (g_wout,) = _matmul("conv_out_dw", "tn", sact, dh1b, m=d, n=d, k=s, outs=[(BF16, "plain")])
    rs_mlp0 = rs_send("mlp0", ["mlp_in0", "mlp_out0"], x_mlp0, g_wout)
    dc, d_ln_g, d_ln_b, d_b_dw = _ln_silu_bwd(cpre, ln_g_f + token(rs_mlp0), ln_b_f, dsact)
    du, d_w_dw, d_b_in_a, d_b_in_g = _dwconv_bwd(u, w_dw_f, dc)
    (g_win,) = _matmul("conv_in_dw", "tn", y0, du, b_kind="col", m=d, n=2 * d, k=s, outs=[(BF16, "col")])
    x_conv = rs_exchange("conv", [g_win, g_wout.reshape(N_SHARD, ds4, d)])
    (dy0,) = _matmul("conv_in_dx", "nt", du, w_in_g, a_kind="col", b_kind="col", m=s, n=d, k=2 * d,
                     outs=[(BF16, "plain")], after=x_conv[3])
    rs_conv = rs_send("conv", ["w_in", "w_out"], x_conv, dy0)
    dx, _, d_nm0 = _rms_bwd("rms_mix0_bwd", h0, [(nm[0] + token(rs_conv), dy0)], dh1)

    small_rows = [(0, d_nm0), (1, d_nm1), (2, d_nmlp0), (3, d_nmlp1), (4, d_kvn), (5, d_fin), (6, d_b_dw),
                  (7, d_ln_g), (8, d_ln_b), (9, d_b_out), (10, d_b_in_a), (11, d_b_in_g), (12, loss_cols)]
    x_small = _split_start("small_start", [_small_pack(small_rows, d_w_dw, d),
                                           lax.empty((N_DEV, SMALL_ROWS, d), F32)], N_DEV - 1, _small_copies)

    def big(name, w, m, v, g, layer=0, partial=None):
        shape = w.shape
        w3, m3, v3 = [t.reshape((-1,) + shape[-2:]) for t in (w, m, v)]
        if partial is not None:
            partial = [t.reshape(w3.shape) for t in partial]
        res = _adamw(name, w3, m3, v3, g, layer, partial)
        return [t.reshape(shape) for t in res]

    sw_mlp1 = rs_sum("mlp1", ["mlp_in1", "mlp_out1"], rs_mlp1, x_small[3])
    sw_attn = rs_sum("attn", ["w_kv", "w_q", "w_o"], rs_attn, sw_mlp1[3])
    f_wmi1, f_wmo1 = rs_end("mlp1", sw_mlp1, sw_attn[3])
    p_wmi = big("adam_mlp_in1", mlp_w_in, m_mlp_w_in, v_mlp_w_in, f_wmi1, 1)
    p_wmo = big("adam_mlp_out1", mlp_w_out, m_mlp_w_out, v_mlp_w_out, f_wmo1, 1)
    sw_mlp0 = rs_sum("mlp0", ["mlp_in0", "mlp_out0"], rs_mlp0, [p_wmi[0], p_wmo[0]])
    f_wkv, f_wq, f_wo = rs_end("attn", sw_attn, sw_mlp0[3])
    r_wkv = big("adam_w_kv", w_kv, m_w_kv, v_w_kv, f_wkv)
    r_wq = big("adam_w_q", attn_w_q, m_attn_w_q, v_attn_w_q, f_wq)
    r_wo = big("adam_w_o", attn_w_o, m_attn_w_o, v_attn_w_o, f_wo)
    sw_conv = rs_sum("conv", ["w_in", "w_out"], rs_conv, [r_wkv[0], r_wq[0], r_wo[0]])
    f_wmi0, f_wmo0 = rs_end("mlp0", sw_mlp0, sw_conv[3])
    r_wmi = big("adam_mlp_in0", mlp_w_in, m_mlp_w_in, v_mlp_w_in, f_wmi0, 0, p_wmi)
    r_wmo = big("adam_mlp_out0", mlp_w_out, m_mlp_w_out, v_mlp_w_out, f_wmo0, 0, p_wmo)
    f_win, f_wout = rs_end("conv", sw_conv, [r_wmi[0], r_wmo[0]])
    r_win = big("adam_w_in", conv_w_in, m_conv_w_in, v_conv_w_in, f_win)
    r_wout = big("adam_w_out", conv_w_out, m_conv_w_out, v_conv_w_out, f_wout)

    small_pack, small_slots = _split_wait("small_wait", x_small, _small_copies, r_wout[0])
    red = _small_sum(small_pack, small_slots, place)
    loss = red[12, 0]
    g_norm_mix = red[0:2]
    g_norm_mlp = red[2:4]
    g_kv_norm = red[4:5]
    g_final = red[5:6]

    def my_cols(row):
        return lax.dynamic_slice(red, (row, me * ds4), (1, ds4))

    g_b_dw, g_ln_g, g_ln_b, g_b_out = my_cols(6), my_cols(7), my_cols(8), my_cols(9)
    half_in = 2 * d // N_SHARD
    b_in_row = 10 + me // 2
    g_b_in = lax.dynamic_slice(red, (b_in_row, (me % 2) * half_in), (1, half_in))
    g_w_dw = lax.dynamic_slice(red, (16, me * ds4), (CONV_WIDTH, ds4))

    sm_w =[norm_mix, norm_mlp, conv_b_in, conv_w_dw.reshape(CONV_WIDTH, ds4), conv_b_dw, conv_ln_g, conv_ln_b,
            conv_b_out, kv_norm.reshape(1, d), final_norm.reshape(1, d)]
    sm_m = [m_norm_mix, m_norm_mlp, m_conv_b_in, m_conv_w_dw.reshape(CONV_WIDTH, ds4), m_conv_b_dw, m_conv_ln_g,
            m_conv_ln_b, m_conv_b_out, m_kv_norm.reshape(1, d), m_final_norm.reshape(1, d)]
    sm_v = [v_norm_mix, v_norm_mlp, v_conv_b_in, v_conv_w_dw.reshape(CONV_WIDTH, ds4), v_conv_b_dw, v_conv_ln_g,
            v_conv_ln_b, v_conv_b_out, v_kv_norm.reshape(1, d), v_final_norm.reshape(1, d)]
    sm_g = [g_norm_mix, g_norm_mlp, g_b_in, g_w_dw, g_b_dw, g_ln_g, g_ln_b, g_b_out, g_kv_norm, g_final]
    sm_d, sm_nm, sm_nv = _adam_small(sm_w, sm_m, sm_v, sm_g)
    shapes = [norm_mix.shape, norm_mlp.shape, conv_b_in.shape, conv_w_dw.shape, conv_b_dw.shape, conv_ln_g.shape,
              conv_ln_b.shape, conv_b_out.shape, kv_norm.shape, final_norm.shape]
    sm_g, sm_d, sm_nm, sm_nv = [[t.reshape(sh) for t, sh in zip(lst, shapes)] for lst in (sm_g, sm_d, sm_nm, sm_nv)]

    def order(sm, idx):
        return [sm[0], sm[1], r_win[idx], sm[2], sm[3], sm[4], sm[5], sm[6], r_wout[idx], sm[7], sm[8],
                r_wkv[idx], r_wq[idx], r_wo[idx], r_wmi[idx], r_wmo[idx], sm[9]]

    return (loss, dx.reshape(x.shape), *order(sm_g, 0), *order(sm_d, 1), *order(sm_nm, 2), *order(sm_nv, 3))
```

```python
import functools
import math

import jax
import jax.numpy as jnp
from jax import lax
from jax.experimental import pallas as pl
from jax.experimental.pallas import tpu as pltpu

F32 = jnp.float32
BF16 = jnp.bfloat16
I32 = jnp.int32

NORM_EPS = 1e-6
LN_EPS = 1e-5
HEAD_DIM = 128
N_KV_HEADS = 4
ROT_DIM = 32
ROPE_THETA = 500000.0
CONV_WIDTH = 31
CONV_PAD = 32
ATT_BLOCK = 128
ATT_STEP_BLOCKS = 16
DILATIONS = (1, 4, 16)
ADAM_LR = 0.001
ADAM_B1 = 0.9
ADAM_B2 = 0.999
ADAM_EPS = 1e-08
ADAM_WD = 0.01
ADAM_STEP = 10
N_SHARD = 4
N_DEV = 8
LANES = 128
VMEM_LIMIT = 48 * 1024 * 1024
MM_TM, MM_TN, MM_TK = 1024, 1024, 2048
ROW_TILE = 512
CONV_CB = 128
CONV_T = 128
SMALL_ROWS = 48
MESH = pl.DeviceIdType.MESH
ANY = pl.BlockSpec(memory_space=pl.ANY)
HBM = pl.BlockSpec(memory_space=pltpu.HBM)
SEM = pl.BlockSpec(memory_space=pltpu.SEMAPHORE)
SPLIT_EFFECT = pltpu.SideEffectType.DATAFLOW_SIDE_EFFECTING


def _params(*sem):
    return pltpu.CompilerParams(dimension_semantics=sem, vmem_limit_bytes=VMEM_LIMIT)


def _sigmoid(x):
    return 1.0 / (1.0 + jnp.exp(-x))


def _wspec(kind, arr_shape, br, bc, pick):
    if kind == "plain":
        return pl.BlockSpec((br, bc), pick)
    per = arr_shape[2] // bc

    def idx(*g):
        rb, cb = pick(*g)
        return (cb // per, rb, cb % per)

    return pl.BlockSpec((None, br, bc), idx)


def _stage_shape(rows, w):
    return (w // LANES, rows, LANES)


def _to_residues(val, stage_ref, out_refs, dils):
    planes, rows, _ = stage_ref.shape
    for c in range(planes):
        stage_ref[c] = val[:, c * LANES:(c + 1) * LANES]
    for out_ref, dil in zip(out_refs, dils):
        if dil == 1:
            out_ref[0] = val.astype(out_ref.dtype)
            continue
        for r in range(dil):
            for c in range(planes):
                out_ref[r, :, c * LANES:(c + 1) * LANES] = stage_ref.at[c][pl.ds(r, rows // dil, stride=dil), :].astype(
                    out_ref.dtype)


def _from_residues(src_ref, stage_ref, dil):
    planes, rows, _ = stage_ref.shape
    if dil == 1:
        return lambda c: src_ref[0, :, c * LANES:(c + 1) * LANES].astype(F32)
    for r in range(dil):
        for c in range(planes):
            stage_ref.at[c][pl.ds(r, rows // dil, stride=dil), :] = src_ref[r, :, c * LANES:(c + 1) * LANES].astype(F32)
    return lambda c: stage_ref[c]


def _matmul(name, mode, a, b, *, m, n, k, tm=MM_TM, tn=MM_TN, tk=MM_TK, a_kind="plain", b_kind="plain", outs,
            extras=(), epilogue=None, stage=False, after=None):
    tm, tn, tk = min(tm, m), min(tn, n), min(tk, k)
    if b_kind == "col" and mode in ("nn", "tn"):
        tn = min(tn, n // b.shape[0])
    if b_kind == "col" and mode == "nt":
        tk = min(tk, k // b.shape[0])
    if a_kind == "col":
        assert mode == "nt"
        tk = min(tk, k // a.shape[0])
    if any(kind == "col" for _, kind in outs):
        tn = min(tn, n // N_SHARD)
    assert m % tm == 0 and n % tn == 0 and k % tk == 0, (name, m, n, k, tm, tn, tk)
    nk = k // tk
    grid = (m // tm, n // tn, nk)
    if mode == "nn":
        a_spec = pl.BlockSpec((tm, tk), lambda i, j, kk: (i, kk))
        b_spec = _wspec(b_kind, b.shape, tk, tn, lambda i, j, kk: (kk, j))
        dims = (((1,), (0,)), ((), ()))
    elif mode == "nt":
        a_spec = _wspec(a_kind, a.shape, tm, tk, lambda i, j, kk: (i, kk))
        b_spec = _wspec(b_kind, b.shape, tn, tk, lambda i, j, kk: (j, kk))
        dims = (((1,), (1,)), ((), ()))
    else:
        a_spec = pl.BlockSpec((tk, tm), lambda i, j, kk: (kk, i))
        b_spec = _wspec(b_kind, b.shape, tk, tn, lambda i, j, kk: (kk, j))
        dims = (((0,), (0,)), ((), ()))
    out_shape, out_specs = [], []
    for dtype, kind in outs:
        if isinstance(kind, tuple):
            dil = kind[1]
            out_shape.append(jax.ShapeDtypeStruct((dil, m // dil, n), dtype))
            out_specs.append(pl.BlockSpec((dil, tm // dil, tn), lambda i, j, kk: (0, i, j)))
            continue
        shape = (m, n) if kind == "plain" else (N_SHARD, m, n // N_SHARD)
        out_shape.append(jax.ShapeDtypeStruct(shape, dtype))
        out_specs.append(_wspec(kind, shape, tm, tn, lambda i, j, kk: (i, j)))
    n_ex = len(extras)
    deps = [] if after is None else [after]
    ex_specs = {"ij": pl.BlockSpec((tm, tn), lambda i, j, kk: (i, j)),
                "vec": pl.BlockSpec((1, tn), lambda i, j, kk: (0, j)),
                "rows": pl.BlockSpec((tm, LANES), lambda i, j, kk: (i, 0))}
    out0 = 2 + n_ex + len(deps)

    def body(*refs):
        a_ref, b_ref = refs[0], refs[1]
        ex_refs = refs[2:2 + n_ex]
        out_refs = refs[out0:out0 + len(outs)]
        j = pl.program_id(1)

        def finish(res):
            if epilogue is None:
                out_refs[0][...] = res.astype(out_refs[0].dtype)
            elif stage:
                epilogue(res, ex_refs, out_refs, j, refs[-1])
            else:
                epilogue(res, ex_refs, out_refs, j)

        prod = lax.dot_general(a_ref[...], b_ref[...], dims, preferred_element_type=F32)
        if nk == 1:
            finish(prod)
            return
        acc_ref = refs[out0 + len(outs)]
        kk = pl.program_id(2)

        @pl.when(kk == 0)
        def _():
            acc_ref[...] = prod

        @pl.when(kk > 0)
        def _():
            acc_ref[...] += prod

        @pl.when(kk == nk - 1)
        def _():
            finish(acc_ref[...])

    res = pl.pallas_call(
        body, name=name, grid=grid,
        in_specs=[a_spec, b_spec] + [ex_specs[how] for _, how in extras] + [ANY] * len(deps),
        out_specs=out_specs, out_shape=out_shape,
        scratch_shapes=[pltpu.VMEM((tm, tn), F32)] * (nk > 1) + [pltpu.VMEM(_stage_shape(tm, tn), F32)] * bool(stage),
        compiler_params=_params("parallel", "parallel", "arbitrary"),
    )(a, b, *[e for e, _ in extras], *deps)
    return res


def _rope_tables(seq):
    half = ROT_DIM // 2
    pos = jnp.arange(seq, dtype=F32)
    inv = ROPE_THETA ** (-jnp.arange(0, ROT_DIM, 2, dtype=F32) / ROT_DIM)
    ang = pos[:, None] * inv[None, :]
    cos, sin = jnp.cos(ang), jnp.sin(ang)
    zeros = jnp.zeros((seq, HEAD_DIM - ROT_DIM), F32)
    ctab = jnp.concatenate([cos, cos, zeros + 1.0], axis=1)
    atab = jnp.concatenate([-sin, jnp.zeros((seq, half), F32), zeros], axis=1)
    btab = jnp.concatenate([jnp.zeros((seq, half), F32), sin, zeros], axis=1)
    return ctab, atab, btab


def _rope_apply(x, ctab, atab, btab, sign):
    w = x.shape[1]
    reps = w // HEAD_DIM
    half = ROT_DIM // 2
    c = jnp.tile(ctab, (1, reps))
    a = jnp.tile(atab, (1, reps))
    b = jnp.tile(btab, (1, reps))
    up = pltpu.roll(x, w - half, 1)
    down = pltpu.roll(x, half, 1)
    return x * c + sign * (up * a + down * b)


def _rows(t, w):
    return pl.BlockSpec((t, w), lambda i: (i, 0))


def _fixed(shape):
    nd = len(shape)
    return pl.BlockSpec(shape, lambda i: (0,) * nd)


def _behind(after):
    deps = [] if after is None else (list(after) if isinstance(after, (list, tuple)) else [after])
    return deps, [ANY] * len(deps)


def _rms_fwd(name, x, gains, after=None):
    s, d = x.shape
    t = min(ROW_TILE, s)
    ng = len(gains)
    deps, dep_specs = _behind(after)

    def body(*all_refs):
        x_ref, refs = all_refs[len(deps)], all_refs[len(deps) + 1:]
        xv = x_ref[...]
        r = lax.rsqrt(jnp.mean(xv * xv, axis=-1, keepdims=True) + NORM_EPS)
        xn = xv * r
        for g_ref, y_ref in zip(refs[:ng], refs[ng:]):
            y_ref[...] = (xn * g_ref[...]).astype(BF16)

    return pl.pallas_call(
        body, name=name, grid=(s // t,),
        in_specs=dep_specs + [_rows(t, d)] + [_fixed((1, d))] * ng,
        out_specs=[_rows(t, d)] * ng,
        out_shape=[jax.ShapeDtypeStruct((s, d), BF16)] * ng,
        compiler_params=_params("parallel"),
    )(*deps, x, *gains)


def _rms_bwd(name, x, pairs, dh_in, want_colsum=False, after=None):
    s, d = x.shape
    n_p = len(pairs)
    t = min(ROW_TILE // n_p, s)
    deps, dep_specs = _behind(after)

    def body(*all_refs):
        x_ref, dh_ref, refs = all_refs[len(deps)], all_refs[len(deps) + 1], all_refs[len(deps) + 2:]
        g_refs = refs[:n_p]
        dy_refs = refs[n_p:2 * n_p]
        dh_out, dhb_out = refs[2 * n_p], refs[2 * n_p + 1]
        dg_refs = refs[2 * n_p + 2:2 * n_p + 2 + n_p]
        cs_ref = refs[-1] if want_colsum else None
        i = pl.program_id(0)
        xv = x_ref[...]
        r = lax.rsqrt(jnp.mean(xv * xv, axis=-1, keepdims=True) + NORM_EPS)
        xn = xv * r
        dh = dh_ref[...]
        for g_ref, dy_ref, dg_ref in zip(g_refs, dy_refs, dg_refs):
            dy = dy_ref[...].astype(F32)
            u = dy * g_ref[...]
            dh = dh + r * (u - xn * jnp.mean(u * xn, axis=-1, keepdims=True))
            part = jnp.sum(dy * xn, axis=0, keepdims=True)

            @pl.when(i == 0)
            def _():
                dg_ref[...] = part

            @pl.when(i > 0)
            def _():
                dg_ref[...] += part

        dh_out[...] = dh
        dhb_out[...] = dh.astype(BF16)
        if want_colsum:
            col = jnp.sum(dh, axis=0, keepdims=True)

            @pl.when(i == 0)
            def _():
                cs_ref[...] = col

            @pl.when(i > 0)
            def _():
                cs_ref[...] += col

    n_vec = n_p + (1 if want_colsum else 0)
    return pl.pallas_call(
        body, name=name, grid=(s // t,),
        in_specs=dep_specs + [_rows(t, d), _rows(t, d)] + [_fixed((1, d))] * n_p + [_rows(t, d)] * n_p,
        out_specs=[_rows(t, d), _rows(t, d)] + [_fixed((1, d))] * n_vec,
        out_shape=[jax.ShapeDtypeStruct((s, d), F32), jax.ShapeDtypeStruct((s, d), BF16)]
        + [jax.ShapeDtypeStruct((1, d), F32)] * n_vec,
        compiler_params=_params("arbitrary"),
    )(*deps, x, dh_in, *[g for g, _ in pairs], *[dy for _, dy in pairs])


def _final_loss(x, g, target):
    s, d = x.shape
    t = min(ROW_TILE, s)

    def body(x_ref, g_ref, t_ref, dh_out, dhb_out, dg_ref, loss_ref):
        i = pl.program_id(0)
        xv = x_ref[...]
        gv = g_ref[...]
        r = lax.rsqrt(jnp.mean(xv * xv, axis=-1, keepdims=True) + NORM_EPS)
        xn = xv * r
        diff = xn * gv - t_ref[...]
        dy = diff / d
        u = dy * gv
        dh = r * (u - xn * jnp.mean(u * xn, axis=-1, keepdims=True))
        dh_out[...] = dh
        dhb_out[...] = dh.astype(BF16)
        dg = jnp.sum(dy * xn, axis=0, keepdims=True)
        lc = jnp.sum(0.5 * diff * dy, axis=0, keepdims=True)

        @pl.when(i == 0)
        def _():
            dg_ref[...] = dg
            loss_ref[...] = lc

        @pl.when(i > 0)
        def _():
            dg_ref[...] += dg
            loss_ref[...] += lc

    return pl.pallas_call(
        body, name="final_loss", grid=(s // t,),
        in_specs=[_rows(t, d), _fixed((1, d)), _rows(t, d)],
        out_specs=[_rows(t, d), _rows(t, d), _fixed((1, d)), _fixed((1, d))],
        out_shape=[jax.ShapeDtypeStruct((s, d), F32), jax.ShapeDtypeStruct((s, d), BF16),
                   jax.ShapeDtypeStruct((1, d), F32), jax.ShapeDtypeStruct((1, d), F32)],
        compiler_params=_params("arbitrary"),
    )(x, g, target)


def _ln_silu_fwd(c, g, b):
    s, d = c.shape
    t = min(ROW_TILE, s)

    def body(c_ref, g_ref, b_ref, s_ref):
        cv = c_ref[...]
        mu = jnp.mean(cv, axis=-1, keepdims=True)
        xc = cv - mu
        rs = lax.rsqrt(jnp.mean(xc * xc, axis=-1, keepdims=True) + LN_EPS)
        ln = xc * rs * g_ref[...] + b_ref[...]
        s_ref[...] = (ln * _sigmoid(ln)).astype(BF16)

    return pl.pallas_call(
        body, name="ln_silu_fwd", grid=(s // t,),
        in_specs=[_rows(t, d), _fixed((1, d)), _fixed((1, d))],
        out_specs=_rows(t, d), out_shape=jax.ShapeDtypeStruct((s, d), BF16),
        compiler_params=_params("parallel"),
    )(c, g, b)


def _ln_silu_bwd(c, g, b, ds, after=None):
    s, d = c.shape
    t = min(ROW_TILE, s)
    deps, dep_specs = _behind(after)

    def body(*all_refs):
        c_ref, g_ref, b_ref, ds_ref, dc_ref, dg_ref, db_ref, dbdw_ref = all_refs[len(deps):]
        i = pl.program_id(0)
        cv = c_ref[...]
        gv = g_ref[...]
        mu = jnp.mean(cv, axis=-1, keepdims=True)
        xc = cv - mu
        rs = lax.rsqrt(jnp.mean(xc * xc, axis=-1, keepdims=True) + LN_EPS)
        nrm = xc * rs
        ln = nrm * gv + b_ref[...]
        sig = _sigmoid(ln)
        dln = ds_ref[...].astype(F32) * sig * (1.0 + ln * (1.0 - sig))
        dn = dln * gv
        dc = rs * (dn - jnp.mean(dn, axis=-1, keepdims=True)
                   - nrm * jnp.mean(dn * nrm, axis=-1, keepdims=True))
        dc_ref[...] = dc
        pg = jnp.sum(dln * nrm, axis=0, keepdims=True)
        pb = jnp.sum(dln, axis=0, keepdims=True)
        pc = jnp.sum(dc, axis=0, keepdims=True)

        @pl.when(i == 0)
        def _():
            dg_ref[...] = pg
            db_ref[...] = pb
            dbdw_ref[...] = pc

        @pl.when(i > 0)
        def _():
            dg_ref[...] += pg
            db_ref[...] += pb
            dbdw_ref[...] += pc

    return pl.pallas_call(
        body, name="ln_silu_bwd", grid=(s // t,),
        in_specs=dep_specs + [_rows(t, d), _fixed((1, d)), _fixed((1, d)), _rows(t, d)],
        out_specs=[_rows(t, d)] + [_fixed((1, d))] * 3,
        out_shape=[jax.ShapeDtypeStruct((s, d), F32)] + [jax.ShapeDtypeStruct((1, d), F32)] * 3,
        compiler_params=_params("arbitrary"),
    )(*deps, c, g, b, ds)


def _residue_spec(dil, t, w):
    return pl.BlockSpec((dil, t // dil, w), lambda i: (0, i, 0))


def _attn_combine(o_list, lse_list):
    dil0, sd0, d = o_list[0].shape
    s = dil0 * sd0
    lw = lse_list[0].shape[2]
    group = d // HEAD_DIM // N_KV_HEADS
    t = min(ROW_TILE, s)
    nb = len(o_list)
    dils = [o.shape[0] for o in o_list]

    def body(*refs):
        o_out, l_out = refs[2 * nb], refs[2 * nb + 1]
        o_stage, l_stage = refs[2 * nb + 2:3 * nb + 2], refs[3 * nb + 2:]
        o_planes = [_from_residues(src, stage, dil) for src, stage, dil in zip(refs[:nb], o_stage, dils)]
        l_planes = [_from_residues(src, stage, dil) for src, stage, dil in zip(refs[nb:2 * nb], l_stage, dils)]
        for kh in range(N_KV_HEADS):
            ls = [plane(kh) for plane in l_planes]
            mx = ls[0]
            for l in ls[1:]:
                mx = jnp.maximum(mx, l)
            es = [jnp.exp(l - mx) for l in ls]
            den = es[0]
            for e in es[1:]:
                den = den + e
            l_out[:, kh * LANES:(kh + 1) * LANES] = mx + jnp.log(den)
            ws = [e / den for e in es]
            for g in range(group):
                h = kh * group + g
                acc = jnp.zeros((t, HEAD_DIM), F32)
                for plane, w in zip(o_planes, ws):
                    acc = acc + w[:, g:g + 1] * plane(h)
                o_out[:, h * HEAD_DIM:(h + 1) * HEAD_DIM] = acc.astype(BF16)

    return pl.pallas_call(
        body, name="attn_combine", grid=(s // t,),
        in_specs=[_residue_spec(dil, t, d) for dil in dils] + [_residue_spec(dil, t, lw) for dil in dils],
        out_specs=[_rows(t, d), _rows(t, lw)],
        out_shape=[jax.ShapeDtypeStruct((s, d), BF16), jax.ShapeDtypeStruct((s, lw), F32)],
        scratch_shapes=[pltpu.VMEM(_stage_shape(t, d), F32)] * nb + [pltpu.VMEM(_stage_shape(t, lw), F32)] * nb,
        compiler_params=_params("parallel"),
    )(*o_list, *lse_list)


def _attn_delta(do, o, lse, dils):
    s, d = o.shape
    lw = lse.shape[1]
    group = d // HEAD_DIM // N_KV_HEADS
    t = min(ROW_TILE, s)
    nd = len(dils)

    def body(do_ref, o_ref, lse_ref, *refs):
        stage = refs[-1]
        lane = lax.broadcasted_iota(I32, (t, LANES), 1)
        planes = []
        for kh in range(N_KV_HEADS):
            out = jnp.zeros((t, LANES), F32)
            for g in range(group):
                cols = slice((kh * group + g) * HEAD_DIM, (kh * group + g + 1) * HEAD_DIM)
                v = jnp.sum(do_ref[:, cols].astype(F32) * o_ref[:, cols].astype(F32), axis=-1, keepdims=True)
                out = jnp.where(lane == g, v, out)
            planes.append(out)
        _to_residues(lse_ref[...], stage, refs[:nd], dils)
        _to_residues(jnp.concatenate(planes, axis=1), stage, refs[nd:2 * nd], dils)

    res = pl.pallas_call(
        body, name="attn_delta", grid=(s // t,),
        in_specs=[_rows(t, d), _rows(t, d), _rows(t, lw)],
        out_specs=[_residue_spec(dil, t, lw) for dil in dils] * 2,
        out_shape=[jax.ShapeDtypeStruct((dil, s // dil, lw), F32) for dil in dils] * 2,
        scratch_shapes=[pltpu.VMEM(_stage_shape(t, lw), F32)],
        compiler_params=_params("parallel"),
    )(do, o, lse)
    return res[:nd], res[nd:]


def _residue_sum(name, groups, tabs):
    first = groups[0][0][0]
    s, w = first.shape[0] * first.shape[1], first.shape[2]
    t = min(ROW_TILE, s)
    flat = [p for parts, _ in groups for p in parts]

    def body(*refs):
        c_ref, a_ref, b_ref = refs[len(flat):len(flat) + 3]
        out = refs[len(flat) + 3]
        stages = refs[len(flat) + 4:]
        k = 0
        for gi, (parts, rotate) in enumerate(groups):
            planes = [_from_residues(refs[k + i], stages[k + i], p.shape[0]) for i, p in enumerate(parts)]
            k += len(parts)
            for c in range(w // LANES):
                tot = planes[0](c)
                for plane in planes[1:]:
                    tot = tot + plane(c)
                if rotate:
                    tot = _rope_apply(tot, c_ref[...], a_ref[...], b_ref[...], -1.0)
                out[:, gi * w + c * LANES:gi * w + (c + 1) * LANES] = tot.astype(BF16)

    return pl.pallas_call(
        body, name=name, grid=(s // t,),
        in_specs=[_residue_spec(p.shape[0], t, w) for p in flat] + [_rows(t, HEAD_DIM)] * 3,
        out_specs=_rows(t, len(groups) * w), out_shape=jax.ShapeDtypeStruct((s, len(groups) * w), BF16),
        scratch_shapes=[pltpu.VMEM(_stage_shape(t, w), F32) for _ in flat],
        compiler_params=_params("parallel"),
    )(*flat, *tabs)


def _dwconv_fwd(u, w_dw, b_dw, after=None):
    s, d2 = u.shape
    d = d2 // 2
    cb = min(CONV_CB, d)
    nblk = d // cb
    tt = min(CONV_T, s)
    deps, dep_specs = _behind(after)

    def body(*all_refs):
        ua_ref, ug_ref, w_ref, b_ref, c_ref, xp_ref = all_refs[len(deps):]
        gl =ua_ref[...].astype(F32) * _sigmoid(ug_ref[...].astype(F32))
        xp_ref[0:CONV_PAD, :] = jnp.zeros((CONV_PAD, cb), F32)
        xp_ref[CONV_PAD:, :] = gl
        wv = w_ref[...]
        bv = b_ref[...]
        for t0 in range(0, s, tt):
            acc = jnp.zeros((tt, cb), F32) + bv
            for kk in range(CONV_WIDTH):
                off = t0 + CONV_PAD - (CONV_WIDTH - 1) + kk
                acc = acc + wv[kk:kk + 1, :] * xp_ref[off:off + tt, :]
            c_ref[t0:t0 + tt, :] = acc

    return pl.pallas_call(
        body, name="dwconv_fwd", grid=(nblk,),
        in_specs=dep_specs + [pl.BlockSpec((s, cb), lambda j: (0, j)), pl.BlockSpec((s, cb), lambda j: (0, j + nblk)),
                              pl.BlockSpec((CONV_PAD, cb), lambda j: (0, j)), pl.BlockSpec((1, cb), lambda j: (0, j))],
        out_specs=pl.BlockSpec((s, cb), lambda j: (0, j)),
        out_shape=jax.ShapeDtypeStruct((s, d), F32),
        scratch_shapes=[pltpu.VMEM((s + CONV_PAD, cb), F32)],
        compiler_params=_params("parallel"),
    )(*deps, u, u, w_dw, b_dw)


def _dwconv_bwd(u, w_dw, dc):
    s, d2 = u.shape
    d = d2 // 2
    cb = min(CONV_CB, d)
    nblk = d // cb
    tt = min(CONV_T, s)

    def body(ua_ref, ug_ref, w_ref, dc_ref, du_ref, dw_ref, dba_ref, dbg_ref, glp_ref, dcp_ref, acc_ref):
        a = ua_ref[...].astype(F32)
        sig = _sigmoid(ug_ref[...].astype(F32))
        glp_ref[0:CONV_PAD, :] = jnp.zeros((CONV_PAD, cb), F32)
        glp_ref[CONV_PAD:, :] = a * sig
        dcp_ref[0:s, :] = dc_ref[...]
        dcp_ref[s:, :] = jnp.zeros((CONV_PAD, cb), F32)
        acc_ref[...] = jnp.zeros_like(acc_ref)
        wv = w_ref[...]
        dba = jnp.zeros((1, cb), F32)
        dbg = jnp.zeros((1, cb), F32)
        for t0 in range(0, s, tt):
            dgl = jnp.zeros((tt, cb), F32)
            dct = dc_ref[t0:t0 + tt, :]
            for kk in range(CONV_WIDTH):
                off = t0 + (CONV_WIDTH - 1) - kk
                dgl = dgl + wv[kk:kk + 1, :] * dcp_ref[off:off + tt, :]
                goff = t0 + CONV_PAD - (CONV_WIDTH - 1) + kk
                prod = dct * glp_ref[goff:goff + tt, :]
                acc_ref[8 * kk:8 * kk + 8, :] += jnp.sum(prod.reshape(tt // 8, 8, cb), axis=0)
            at = ua_ref[t0:t0 + tt, :].astype(F32)
            st = _sigmoid(ug_ref[t0:t0 + tt, :].astype(F32))
            da = dgl * st
            dg = dgl * at * st * (1.0 - st)
            du_ref[0, t0:t0 + tt, :] = da.astype(BF16)
            du_ref[1, t0:t0 + tt, :] = dg.astype(BF16)
            dba = dba + jnp.sum(da, axis=0, keepdims=True)
            dbg = dbg + jnp.sum(dg, axis=0, keepdims=True)
        dba_ref[...] = dba
        dbg_ref[...] = dbg
        for kk in range(CONV_WIDTH):
            dw_ref[kk:kk + 1, :] = jnp.sum(acc_ref[8 * kk:8 * kk + 8, :], axis=0, keepdims=True)
        dw_ref[CONV_WIDTH:, :] = jnp.zeros((CONV_PAD - CONV_WIDTH, cb), F32)

    blk = pl.BlockSpec((s, cb), lambda j: (0, j))
    vec = pl.BlockSpec((1, cb), lambda j: (0, j))
    return pl.pallas_call(
        body, name="dwconv_bwd", grid=(nblk,),
        in_specs=[blk, pl.BlockSpec((s, cb), lambda j: (0, j + nblk)),
                  pl.BlockSpec((CONV_PAD, cb), lambda j: (0, j)), blk],
        out_specs=[pl.BlockSpec((2, s, cb), lambda j: (0, 0, j)), pl.BlockSpec((CONV_PAD, cb), lambda j: (0, j)),
                   vec, vec],
        out_shape=[jax.ShapeDtypeStruct((2, s, d), BF16), jax.ShapeDtypeStruct((CONV_PAD, d), F32),
                   jax.ShapeDtypeStruct((1, d), F32), jax.ShapeDtypeStruct((1, d), F32)],
        scratch_shapes=[pltpu.VMEM((s + CONV_PAD, cb), F32), pltpu.VMEM((s + CONV_PAD, cb), F32),
                        pltpu.VMEM((8 * CONV_PAD, cb), F32)],
        compiler_params=_params("parallel"),
    )(u, u, w_dw, dc)


def _stack_heads(x, group):
    return jnp.concatenate([x[:, g * HEAD_DIM:(g + 1) * HEAD_DIM] for g in range(group)], axis=0)


def _unstack_heads(x, group):
    return jnp.concatenate([x[g * ATT_BLOCK:(g + 1) * ATT_BLOCK, :] for g in range(group)], axis=1)


def _stack_cols(x, group):
    return jnp.concatenate([x[:, g:g + 1] for g in range(group)], axis=0)


def _band_bias(group):
    rows = group * ATT_BLOCK
    row = lax.broadcasted_iota(I32, (rows, 2 * ATT_BLOCK), 0) % ATT_BLOCK
    col = lax.broadcasted_iota(I32, (rows, 2 * ATT_BLOCK), 1)
    band = jnp.where((col >= row) & (col <= row + ATT_BLOCK), 0.0, -jnp.inf).astype(F32)
    first = jnp.where(lax.broadcasted_iota(I32, (1, 2 * ATT_BLOCK), 1) >= ATT_BLOCK, 0.0, -jnp.inf).astype(F32)
    return band, first


def _masked_scores(qs, kw, band_ref, first_ref, nb, scale):
    sc = lax.dot_general(qs, kw, (((1,), (1,)), ((), ())), preferred_element_type=F32) * scale + band_ref[...]
    return sc + jnp.where(nb > 0, 0.0, first_ref[...])


def _window(ref, nb):
    prev = pl.multiple_of(jnp.maximum(nb - 1, 0) * ATT_BLOCK, ATT_BLOCK)
    cur = pl.multiple_of(nb * ATT_BLOCK, ATT_BLOCK)
    return jnp.concatenate([ref[pl.ds(prev, ATT_BLOCK), :], ref[pl.ds(cur, ATT_BLOCK), :]], axis=0)


def _residues_per_step(dil, nblk):
    return max(1, min(dil, ATT_STEP_BLOCKS // nblk))


def _attn_fwd(name, q, kv):
    dil, sd, d = q.shape
    group = d // HEAD_DIM // N_KV_HEADS
    gw = group * HEAD_DIM
    nblk = sd // ATT_BLOCK
    scale = 1.0 / math.sqrt(HEAD_DIM)
    nt = (((1,), (1,)), ((), ()))

    rb = _residues_per_step(dil, nblk)

    def body(q_all, k_all, v_all, band_ref, first_ref, o_all, lse_all):
        lane = lax.broadcasted_iota(I32, (ATT_BLOCK, LANES), 1)
        for rr in range(rb):
            q_ref, k_ref, v_ref, o_ref, lse_ref = [ref.at[rr] for ref in (q_all, k_all, v_all, o_all, lse_all)]

            def step(nb, carry):
                rows = pl.ds(pl.multiple_of(nb * ATT_BLOCK, ATT_BLOCK), ATT_BLOCK)
                qs = _stack_heads(q_ref[rows, :], group)
                kw = _window(k_ref, nb)
                vw = _window(v_ref, nb)
                sc = _masked_scores(qs, kw, band_ref, first_ref, nb, scale)
                mx = jnp.max(sc, axis=-1, keepdims=True)
                p = jnp.exp(sc - mx)
                l = jnp.sum(p, axis=-1, keepdims=True)
                o = jnp.dot(p.astype(BF16), vw, preferred_element_type=F32) / l
                o_ref[rows, :] = _unstack_heads(o, group).astype(BF16)
                lse = mx + jnp.log(l)
                out = jnp.zeros((ATT_BLOCK, LANES), F32)
                for g in range(group):
                    out = jnp.where(lane == g, lse[g * ATT_BLOCK:(g + 1) * ATT_BLOCK, :], out)
                lse_ref[rows, :] = out
                return carry

            lax.fori_loop(0, nblk, step, 0, unroll=min(2, nblk))

    kvh = N_KV_HEADS
    band, first = _band_bias(group)
    qspec = pl.BlockSpec((rb, sd, gw), lambda r, h: (r, 0, h))
    kspec = pl.BlockSpec((rb, sd, HEAD_DIM), lambda r, h: (r, 0, h))
    return pl.pallas_call(
        body, name=name, grid=(dil // rb, kvh),
        in_specs=[qspec, kspec, pl.BlockSpec((rb, sd, HEAD_DIM), lambda r, h: (r, 0, kvh + h)),
                  pl.BlockSpec(band.shape, lambda r, h: (0, 0)), pl.BlockSpec(first.shape, lambda r, h: (0, 0))],
        out_specs=[qspec, kspec],
        out_shape=[jax.ShapeDtypeStruct((dil, sd, d), BF16),
                   jax.ShapeDtypeStruct((dil, sd, kvh * LANES), F32)],
        compiler_params=_params("parallel", "parallel"),
    )(q, kv, kv, band, first)


def _attn_bwd(name, q, kv, do, lse, delta):
    dil, sd, d = q.shape
    group = d // HEAD_DIM // N_KV_HEADS
    gw = group * HEAD_DIM
    nblk = sd // ATT_BLOCK
    scale = 1.0 / math.sqrt(HEAD_DIM)
    nt = (((1,), (1,)), ((), ()))
    tn = (((0,), (0,)), ((), ()))

    rb = _residues_per_step(dil, nblk)

    def body(q_all, k_all, v_all, do_all, lse_all, dl_all, band_ref, first_ref, dq_all, dk_all, dv_all, dk_accs,
             dv_accs):
        dk_accs[...] = jnp.zeros_like(dk_accs)
        dv_accs[...] = jnp.zeros_like(dv_accs)
        for rr in range(rb):
            q_ref, k_ref, v_ref, do_ref, lse_ref, dl_ref, dq_ref, dk_ref, dv_ref, dk_acc, dv_acc = [
                ref.at[rr] for ref in (q_all, k_all, v_all, do_all, lse_all, dl_all, dq_all, dk_all, dv_all,
                                       dk_accs, dv_accs)]

            def step(nb, carry):
                rows = pl.ds(pl.multiple_of(nb * ATT_BLOCK, ATT_BLOCK), ATT_BLOCK)
                qs = _stack_heads(q_ref[rows, :], group)
                dos = _stack_heads(do_ref[rows, :], group)
                ls = _stack_cols(lse_ref[rows, :], group)
                dl = _stack_cols(dl_ref[rows, :], group)
                kw = _window(k_ref, nb)
                vw = _window(v_ref, nb)
                p = jnp.exp(_masked_scores(qs, kw, band_ref, first_ref, nb, scale) - ls)
                dp = lax.dot_general(dos, vw, nt, preferred_element_type=F32)
                ds = (p * (dp - dl) * scale).astype(BF16)
                dq = jnp.dot(ds, kw, preferred_element_type=F32)
                dq_ref[rows, :] = _unstack_heads(dq, group).astype(BF16)
                win = pl.ds(pl.multiple_of(nb * ATT_BLOCK, ATT_BLOCK), 2 * ATT_BLOCK)
                dk_acc[win, :] += lax.dot_general(ds, qs, tn, preferred_element_type=F32)
                dv_acc[win, :] += lax.dot_general(p.astype(BF16), dos, tn, preferred_element_type=F32)
                return carry

            lax.fori_loop(0, nblk, step, 0, unroll=min(2, nblk))
            dk_ref[...] = dk_acc[ATT_BLOCK:, :]
            dv_ref[...] = dv_acc[ATT_BLOCK:, :]

    kvh = N_KV_HEADS
    band, first = _band_bias(group)
    qspec = pl.BlockSpec((rb, sd, gw), lambda r, h: (r, 0, h))
    kspec = pl.BlockSpec((rb, sd, HEAD_DIM), lambda r, h: (r, 0, h))
    return pl.pallas_call(
        body, name=name, grid=(dil // rb, kvh),
        in_specs=[qspec, kspec, pl.BlockSpec((rb, sd, HEAD_DIM), lambda r, h: (r, 0, kvh + h)),
                  qspec, kspec, kspec,
                  pl.BlockSpec(band.shape, lambda r, h: (0, 0)), pl.BlockSpec(first.shape, lambda r, h: (0, 0))],
        out_specs=[qspec, kspec, kspec],
        out_shape=[jax.ShapeDtypeStruct((dil, sd, d), BF16),
                   jax.ShapeDtypeStruct((dil, sd, kvh * HEAD_DIM), F32),
                   jax.ShapeDtypeStruct((dil, sd, kvh * HEAD_DIM), F32)],
        scratch_shapes=[pltpu.VMEM((rb, sd + ATT_BLOCK, HEAD_DIM), F32)] * 2,
        compiler_params=_params("parallel", "parallel"),
    )(q, kv, kv, do, lse, delta, band, first)


def _cast_bf16(name, w, layer, place, after=None):
    _, r, c = w.shape
    tr = min(512, r)
    deps = [] if after is None else [after]

    def body(pl_ref, w_ref, *refs):
        refs[-1][...] = w_ref[...].astype(BF16)

    return pl.pallas_call(
        body, name=name,
        grid_spec=pltpu.PrefetchScalarGridSpec(
            num_scalar_prefetch=1, grid=(r // tr,),
            in_specs=[pl.BlockSpec((None, tr, c), lambda i, p: (layer, i, 0))] + [ANY] * len(deps),
            out_specs=pl.BlockSpec((None, tr, c), lambda i, p: (p[1], i, 0))),
        out_shape=jax.ShapeDtypeStruct((N_SHARD, r, c), BF16),
        compiler_params=_params("parallel"),
    )(place, w, *deps)


def _chip_sum(name, g, rh, place):
    _, r, c = g.shape
    rh2 = r // 2
    tr = min(512, rh2)
    nb = rh2 // tr

    def body(pl_ref, g_ref, rh_ref, o_ref):
        o_ref[...] = (g_ref[...].astype(F32) + rh_ref[...].astype(F32)).astype(BF16)

    return pl.pallas_call(
        body, name=name,
        grid_spec=pltpu.PrefetchScalarGridSpec(
            num_scalar_prefetch=1, grid=(N_SHARD, nb),
            in_specs=[pl.BlockSpec((None, tr, c), lambda s, i, p: (s, p[0] * nb + i, 0)),
                      pl.BlockSpec((None, tr, c), lambda s, i, p: (s, i, 0))],
            out_specs=pl.BlockSpec((None, tr, c), lambda s, i, p: (s, i, 0))),
        out_shape=jax.ShapeDtypeStruct((N_SHARD, rh2, c), BF16),
        compiler_params=_params("parallel", "parallel"),
    )(place, g, rh)


def _owner_sum(name, cs, rp, place):
    _, rh2, c = cs.shape
    tr = min(512, rh2)
    nb = rh2 // tr

    def body(pl_ref, cs_ref, r0_ref, r1_ref, r2_ref, o_ref):
        o_ref[...] = ((cs_ref[...].astype(F32) + r0_ref[...].astype(F32))
                      + (r1_ref[...].astype(F32) + r2_ref[...].astype(F32)))

    def rspec(j):
        return pl.BlockSpec((None, tr, c), lambda i, p: (j, i, 0))

    return pl.pallas_call(
        body, name=name,
        grid_spec=pltpu.PrefetchScalarGridSpec(
            num_scalar_prefetch=1, grid=(nb,),
            in_specs=[pl.BlockSpec((None, tr, c), lambda i, p: (p[1], i, 0)), rspec(0), rspec(1), rspec(2)],
            out_specs=pl.BlockSpec((tr, c), lambda i, p: (p[0] * nb + i, 0))),
        out_shape=jax.ShapeDtypeStruct((2 * rh2, c), F32),
        compiler_params=_params("parallel"),
    )(place, cs, rp, rp, rp)


def _adam_math(w, g, m, v):
    m = ADAM_B1 * m + (1.0 - ADAM_B1) * g
    v = ADAM_B2 * v + (1.0 - ADAM_B2) * (g * g)
    m_hat = m / (1.0 - ADAM_B1 ** ADAM_STEP)
    v_hat = v / (1.0 - ADAM_B2 ** ADAM_STEP)
    delta = -ADAM_LR * (m_hat / (jnp.sqrt(v_hat) + ADAM_EPS) + ADAM_WD * w)
    return delta, m, v


def _adamw(name, w, m, v, g, layer, partial=None):
    nl, r, c = w.shape
    tr = min(256, r)

    def body(w_ref, m_ref, v_ref, g_ref, *refs):
        go_ref, d_ref, mo_ref, vo_ref = refs[-4:]
        gv = g_ref[...]
        delta, m_new, v_new = _adam_math(w_ref[...], gv, m_ref[...], v_ref[...])
        go_ref[...] = gv
        d_ref[...] = delta
        mo_ref[...] = m_new
        vo_ref[...] = v_new

    wspec = pl.BlockSpec((None, tr, c), lambda i: (layer, i, 0))
    prev = [] if partial is None else list(partial)
    return pl.pallas_call(
        body, name=name, grid=(r // tr,),
        in_specs=[wspec] * 3 + [pl.BlockSpec((tr, c), lambda i: (i, 0))] + [ANY] * len(prev),
        out_specs=[wspec] * 4,
        out_shape=[jax.ShapeDtypeStruct((nl, r, c), F32)] * 4,
        input_output_aliases={4 + i: i for i in range(len(prev))},
        compiler_params=_params("parallel"),
    )(w, m, v, g, *prev)


def _adam_small(ws, ms, vs, gs):
    n = len(ws)

    def body(*refs):
        w_refs, m_refs, v_refs, g_refs = refs[:n], refs[n:2 * n], refs[2 * n:3 * n], refs[3 * n:4 * n]
        d_refs, mo_refs, vo_refs = refs[4 * n:5 * n], refs[5 * n:6 * n], refs[6 * n:7 * n]
        for i in range(n):
            delta, m_new, v_new = _adam_math(w_refs[i][...], g_refs[i][...], m_refs[i][...], v_refs[i][...])
            d_refs[i][...] = delta
            mo_refs[i][...] = m_new
            vo_refs[i][...] = v_new

    shapes = [jax.ShapeDtypeStruct(w.shape, F32) for w in ws]
    res = pl.pallas_call(body, name="adam_small", out_shape=shapes * 3)(*ws, *ms, *vs, *gs)
    return res[:n], res[n:2 * n], res[2 * n:]


def _pack_small(b_in, w_dw, b_dw, ln_g, ln_b, b_out, place):
    cin = b_in.shape[1]
    cd = b_dw.shape[1]
    rows = 8 + CONV_PAD

    def body(pl_ref, bi, wd, bd, lg, lb, bo, out):
        out[...] = jnp.zeros_like(out)
        out[0:1, :] = bi[...]
        out[1:2, 0:cd] = bd[...]
        out[1:2, cd:2 * cd] = lg[...]
        out[2:3, 0:cd] = lb[...]
        out[2:3, cd:2 * cd] = bo[...]
        out[8:8 + CONV_WIDTH, 0:cd] = wd[...]

    def whole(arr):
        return pl.BlockSpec(arr.shape, lambda i, p: (0,) * arr.ndim)

    ins = [b_in, w_dw, b_dw, ln_g, ln_b, b_out]
    return pl.pallas_call(
        body, name="pack_small",
        grid_spec=pltpu.PrefetchScalarGridSpec(
            num_scalar_prefetch=1, grid=(1,), in_specs=[whole(a) for a in ins],
            out_specs=pl.BlockSpec((None, rows, cin), lambda i, p: (p[1], 0, 0))),
        out_shape=jax.ShapeDtypeStruct((N_SHARD, rows, cin), F32),
        compiler_params=_params("arbitrary"),
    )(place, *ins)


def _place():
    x, y, c = lax.axis_index("x"), lax.axis_index("y"), lax.axis_index("c")
    return x, y, c


def _other_chips(x, y):
    return [(1 - x, y), (x, 1 - y), (1 - x, 1 - y)]


def _split_start_many(name, parts, after=None):
    flat = [b for bufs, _, _ in parts for b in bufs]
    n, n_parts = len(flat), len(parts)
    deps = [] if after is None else [after]

    def body(*refs):
        out0 = n + len(deps)
        pos = 0
        for i, (bufs, _, copies) in enumerate(parts):
            for cp in copies(refs[pos:pos + len(bufs)], refs[out0 + 2 * i], refs[out0 + 2 * i + 1], False):
                cp.start()
            pos += len(bufs)
        refs[-1][...] = jnp.zeros_like(refs[-1])

    sems = [pltpu.SemaphoreType.DMA((n_sem,)) for _, n_sem, _ in parts for _ in range(2)]
    res = pl.pallas_call(
        body, name=name,
        out_shape=(*sems, *[pltpu.HBM(b.shape, b.dtype) for b in flat], jax.ShapeDtypeStruct((8, LANES), F32)),
        in_specs=[HBM] * n + [ANY] * len(deps),
        out_specs=(*[SEM] * (2 * n_parts), *[HBM] * n, pl.BlockSpec(memory_space=pltpu.VMEM)),
        input_output_aliases={i: 2 * n_parts + i for i in range(n)},
        compiler_params=pltpu.CompilerParams(has_side_effects=SPLIT_EFFECT),
    )(*[pltpu.with_memory_space_constraint(b, pltpu.HBM) for b in flat], *deps)
    handles, pos = [], 2 * n_parts
    for i, (bufs, _, _) in enumerate(parts):
        handles.append((res[2 * i], res[2 * i + 1], list(res[pos:pos + len(bufs)]), res[-1]))
        pos += len(bufs)
    return handles


def _split_start(name, bufs, n_sem, copies, after=None):
    return _split_start_many(name, [(bufs, n_sem, copies)], after)[0]


def _split_wait(name, handle, copies, after):
    ssem, rsem, bufs, _ = handle
    n = len(bufs)
    deps = list(after) if isinstance(after, (list, tuple)) else [after]

    def body(*refs):
        for cp in copies(refs[:n], refs[n], refs[n + 1], True):
            cp.wait_send()
            cp.wait_recv()

    res = pl.pallas_call(
        body, name=name,
        out_shape=[pltpu.HBM(b.shape, b.dtype) for b in bufs],
        in_specs=[HBM] * n + [SEM, SEM] + [ANY] * len(deps), out_specs=[HBM] * n,
        input_output_aliases={i: i for i in range(n)},
        compiler_params=pltpu.CompilerParams(has_side_effects=SPLIT_EFFECT),
    )(*bufs, ssem, rsem, *deps)
    return list(res)


def _remote(src, dst, ssem, rsem, k, to):
    return pltpu.make_async_remote_copy(src_ref=src, dst_ref=dst, send_sem=ssem.at[k], recv_sem=rsem.at[k],
                                        device_id=to, device_id_type=MESH)


def _gather_chips(x, y, c):
    nx, ny = x + (1 - c) - 2 * x * (1 - c), y + c - 2 * y * c
    fx, fy = x + c - 2 * x * c, y + (1 - c) - 2 * y * (1 - c)
    return (nx, ny), (fx, fy), 2 * nx + ny, 2 * fx + fy, 2 * (1 - x) + (1 - y)


def _direct_copies(refs, ssem, rsem, landing, n_whole=0):
    x, y, c = _place()
    me = 2 * x + y
    (nx, ny), _, near, _, _ = _gather_chips(x, y, c)
    n = len(refs) - n_whole
    cps = []
    for a, ref in enumerate(refs[:n]):
        cps.append(_remote(ref.at[me], ref.at[near if landing else me], ssem, rsem, a, (nx, ny, c)))
    for b, ref in enumerate(refs[n:]):
        for j, (px, py) in enumerate(_other_chips(x, y)):
            cps.append(_remote(ref.at[me], ref.at[2 * px + py if landing else me], ssem, rsem, n + 3 * b + j,
                               (px, py, c)))
    return cps


def _relay_copies(refs, ssem, rsem, landing):
    x, y, c = _place()
    _, (fx, fy), near, far, diag = _gather_chips(x, y, c)
    n = len(refs)
    cps = []
    for a, ref in enumerate(refs):
        rh = ref.shape[1] // 2
        rows = pl.ds(c * rh, rh)
        cps.append(_remote(ref.at[near, rows], ref.at[diag if landing else near, rows], ssem, rsem, a, (fx, fy, c)))
        cps.append(_remote(ref.at[near], ref.at[far if landing else near], ssem, rsem, n + a, (x, y, 1 - c)))
    return cps


def _diagonal_copies(refs, ssem, rsem, landing):
    x, y, c = _place()
    diag = 2 * (1 - x) + (1 - y)
    who = 1 - c if landing else c
    cps = []
    for a, ref in enumerate(refs):
        rh = ref.shape[1] // 2
        piece = ref.at[diag, pl.ds(who * rh, rh)]
        cps.append(_remote(piece, piece, ssem, rsem, a, (x, y, 1 - c)))
    return cps


def _sibling_copies(refs, ssem, rsem, landing):
    x, y, c = _place()
    n = len(refs) // 2
    cps = []
    for a in range(n):
        rh = refs[a].shape[1] // 2
        cps.append(_remote(refs[a].at[:, pl.ds((1 - c) * rh, rh), :], refs[n + a], ssem, rsem, a, (x, y, 1 - c)))
    return cps


def _owner_copies(refs, ssem, rsem, landing):
    x, y, c = _place()
    n = len(refs) // 2
    cps = []
    for a in range(n):
        for j, (px, py) in enumerate(_other_chips(x, y)):
            cps.append(_remote(refs[a].at[2 * px + py], refs[n + a].at[j], ssem, rsem, 3 * a + j, (px, py, c)))
    return cps


def _swap_copies(refs, ssem, rsem, landing):
    x, y, c = _place()
    who = 1 - c if landing else c
    cps = []
    for a, ref in enumerate(refs):
        rh = ref.shape[0] // 2
        rows = ref.at[pl.ds(who * rh, rh)]
        cps.append(_remote(rows, rows, ssem, rsem, a, (x, y, 1 - c)))
    return cps


def _small_copies(refs, ssem, rsem, landing):
    pack, slots = refs
    x, y, c = _place()
    cps = []
    for rel in range(1, N_DEV):
        px = 1 - x if (rel >> 2) & 1 else x
        py = 1 - y if (rel >> 1) & 1 else y
        pc = 1 - c if rel & 1 else c
        slot = 4 * px + 2 * py + pc if landing else 4 * x + 2 * y + c
        cps.append(_remote(pack, slots.at[slot], ssem, rsem, rel - 1, (px, py, pc)))
    return cps


def _small_pack(rows, w_dw_grad, d):
    n = len(rows)

    def body(*refs):
        pack = refs[-1]
        pack[...] = jnp.zeros_like(pack)
        for (r, _), ref in zip(rows, refs[:n]):
            pack[r:r + 1, :] = ref[...]
        pack[16:16 + CONV_PAD, :] = refs[n][...]

    return pl.pallas_call(body, name="small_pack", out_shape=jax.ShapeDtypeStruct((SMALL_ROWS, d), F32))(
        *[v for _, v in rows], w_dw_grad)


def _small_sum(pack, slots, place):
    rows, d = pack.shape
    loss_row = 12

    def body(pl_ref, pack_ref, slots_ref, out_ref):
        me = pl_ref[2]
        tot = jnp.where(me == 0, pack_ref[...], slots_ref[0])
        for i in range(1, N_DEV):
            tot = tot + jnp.where(me == i, pack_ref[...], slots_ref[i])
        out_ref[...] = tot
        out_ref[loss_row:loss_row + 1, :] = jnp.zeros((1, d), F32) + jnp.sum(tot[loss_row:loss_row + 1, :])

    return pl.pallas_call(
        body, name="small_sum",
        grid_spec=pltpu.PrefetchScalarGridSpec(
            num_scalar_prefetch=1, grid=(1,),
            in_specs=[pl.BlockSpec((rows, d), lambda i, p: (0, 0)), pl.BlockSpec((N_DEV, rows, d), lambda i, p: (0, 0, 0))],
            out_specs=pl.BlockSpec((rows, d), lambda i, p: (0, 0))),
        out_shape=jax.ShapeDtypeStruct((rows, d), F32),
        compiler_params=_params("arbitrary"),
    )(place, pack, slots)


def kernel(x, norm_mix, norm_mlp, conv_w_in, conv_b_in, conv_w_dw, conv_b_dw, conv_ln_g, conv_ln_b, conv_w_out, conv_b_out, kv_norm, w_kv, attn_w_q, attn_w_o, mlp_w_in, mlp_w_out, final_norm, loss_target, m_norm_mix, m_norm_mlp, m_conv_w_in, m_conv_b_in, m_conv_w_dw, m_conv_b_dw, m_conv_ln_g, m_conv_ln_b, m_conv_w_out, m_conv_b_out, m_kv_norm, m_w_kv, m_attn_w_q, m_attn_w_o, m_mlp_w_in, m_mlp_w_out, m_final_norm, v_norm_mix, v_norm_mlp, v_conv_w_in, v_conv_b_in, v_conv_w_dw, v_conv_b_dw, v_conv_ln_g, v_conv_ln_b, v_conv_w_out, v_conv_b_out, v_kv_norm, v_w_kv, v_attn_w_q, v_attn_w_o, v_mlp_w_in, v_mlp_w_out, v_final_norm):
    _, s, d = x.shape
    dff = mlp_w_in.shape[2] * N_SHARD
    kvw = w_kv.shape[1]
    nh = d // HEAD_DIM
    group = nh // N_KV_HEADS
    ds4 = d // N_SHARD
    xi, yi, ci = _place()
    me = 2 * xi + yi
    place = jnp.stack([ci, me, 2 * me + ci]).astype(I32)

    h0 = x.reshape(s, d)
    target = loss_target.reshape(s, d)
    tabs = _rope_tables(s)

    def gather_begin(tag, bufs, n_whole=0):
        plan = functools.partial(_direct_copies, n_whole=n_whole)
        return _split_start(f"gather_start_{tag}", bufs, len(bufs) + 2 * n_whole, plan), plan, n_whole

    def gather_step(later, land=None, swap=None):
        parts, names, whole = [], [], {}
        if land is not None:
            tag, (handle, plan, n_whole) = land
            bufs = _split_wait(f"gather_wait_{tag}", handle, plan, later)
            n = len(bufs) - n_whole
            parts.append((bufs[:n], 2 * n, _relay_copies))
            whole["land"] = bufs[n:]
            names.append(f"relay_{tag}")
        if swap is not None:
            tag, (relayed, whole["swap"]) = swap
            bufs = _split_wait(f"relay_wait_{tag}", relayed, _relay_copies, later)
            parts.append((bufs, len(bufs), _diagonal_copies))
            names.append(f"diagonal_{tag}")
        handles = _split_start_many("start_" + "_".join(names), parts)
        landed = (handles[0], whole["land"]) if land is not None else None
        swapped = (handles[-1], whole["swap"]) if swap is not None else None
        return landed, swapped

    def gather_land(tag, begun, later):
        return gather_step(later, land=(tag, begun))[0]

    def gather_swap(tag, landed, later):
        return gather_step(later, swap=(tag, landed))[1]

    def gather_end(tag, swapped, later):
        handle, whole = swapped
        return _split_wait(f"diagonal_wait_{tag}", handle, _diagonal_copies, later) + whole

    ag_cin = gather_begin("conv_in", [
        _cast_bf16("cast_w_in", conv_w_in, 0, place),
        _pack_small(conv_b_in, conv_w_dw.reshape(CONV_WIDTH, ds4), conv_b_dw, conv_ln_g, conv_ln_b, conv_b_out, place),
    ], n_whole=1)
    ag_cout = gather_begin("conv_out", [_cast_bf16("cast_w_out", conv_w_out, 0, place, ag_cin[0][3])])
    ag_mi0 = gather_begin("mlp_in0", [_cast_bf16("cast_mlp_in0", mlp_w_in, 0, place, ag_cout[0][3])])
    ag_mo0 = gather_begin("mlp_out0", [_cast_bf16("cast_mlp_out0", mlp_w_out, 0, place, ag_mi0[0][3])])
    nm = [norm_mix[0:1], norm_mix[1:2]]
    nmlp = [norm_mlp[0:1], norm_mlp[1:2]]
    kvn = kv_norm.reshape(1, d)
    fin = final_norm.reshape(1, d)
    (y0,) = _rms_fwd("rms_mix0", h0, [nm[0]], after=ag_mo0[0][3])
    land_cin = gather_land("conv_in", ag_cin, y0)
    ag_attn = gather_begin("attn", [
        _cast_bf16("cast_w_kv", w_kv.reshape(1, ds4, kvw), 0, place, land_cin[0][3]),
        _cast_bf16("cast_w_q", attn_w_q, 0, place), _cast_bf16("cast_w_o", attn_w_o, 0, place)])
    ag_mi1 = gather_begin("mlp_in1", [_cast_bf16("cast_mlp_in1", mlp_w_in, 1, place, ag_attn[0][3])])
    ag_mo1 = gather_begin("mlp_out1", [_cast_bf16("cast_mlp_out1", mlp_w_out, 1, place, ag_mi1[0][3])])
    land_cout, swap_cin = gather_step(ag_mo1[0][3], land=("conv_out", ag_cout), swap=("conv_in", land_cin))

    wmi_g = [None, None]
    wmo_f = [None, None]

    w_in_g, small_g = gather_end("conv_in", swap_cin, swap_cin[0][3])
    b_in_f = small_g[:, 0, :].reshape(1, 2 * d)
    b_dw_f = small_g[:, 1, 0:ds4].reshape(1, d)
    ln_g_f = small_g[:, 1, ds4:2 * ds4].reshape(1, d)
    ln_b_f = small_g[:, 2, 0:ds4].reshape(1, d)
    b_out_f = small_g[:, 2, ds4:2 * ds4].reshape(1, d)
    w_dw_f = jnp.transpose(small_g[:, 8:8 + CONV_PAD, 0:ds4], (1, 0, 2)).reshape(CONV_PAD, d)

    def ep_bias(acc, ex, outs, j):
        outs[0][...] = (acc + ex[0][...]).astype(outs[0].dtype)

    def ep_residual(acc, ex, outs, j):
        outs[0][...] = ex[0][...] + acc

    def ep_residual_bias(acc, ex, outs, j):
        outs[0][...] = ex[0][...] + (acc + ex[1][...])

    def ep_relu2(acc, ex, outs, j):
        r = jnp.maximum(acc, 0.0)
        outs[0][...] = r.astype(BF16)
        outs[1][...] = (r * r).astype(BF16)

    by_residue = [(BF16, ("residues", dil)) for dil in DILATIONS]

    def put_by_residue(val, outs, stage):
        _to_residues(val, stage, outs, DILATIONS)

    def ep_rope(acc, ex, outs, j, stage):
        put_by_residue(_rope_apply(acc, ex[0][...], ex[1][...], ex[2][...], 1.0), outs, stage)

    def ep_rope_k(acc, ex, outs, j, stage):
        roped = _rope_apply(acc, ex[0][...], ex[1][...], ex[2][...], 1.0)
        put_by_residue(jnp.where(j == 0, roped, acc), outs, stage)

    def ep_by_residue(acc, ex, outs, j, stage):
        put_by_residue(acc, outs, stage)

    tab_extras = [(t, "rows") for t in tabs]

    def mlp_fwd(idx, h, y, out_weight):
        r, r2 = _matmul(f"mlp_in{idx}", "nn", y, wmi_g[idx], b_kind="col", m=s, n=dff, k=d,
                        outs=[(BF16, "plain"), (BF16, "plain")], epilogue=ep_relu2)
        wmo_f[idx] = out_weight(r2).reshape(dff, d)
        (h_new,) = _matmul(f"mlp_out{idx}", "nn", r2, wmo_f[idx], m=s, n=d, k=dff,
                           outs=[(F32, "plain")], extras=[(h, "ij")], epilogue=ep_residual)
        return h_new, r, r2

    (u,) = _matmul("conv_in", "nn", y0, w_in_g, b_kind="col", m=s, n=2 * d, k=d,
                   outs=[(BF16, "plain")], extras=[(b_in_f, "vec")], epilogue=ep_bias)
    land_mi0, swap_cout = gather_step(u, land=("mlp_in0", ag_mi0), swap=("conv_out", land_cout))
    cpre = _dwconv_fwd(u, w_dw_f, b_dw_f, after=swap_cout[0][3])
    sact = _ln_silu_fwd(cpre, ln_g_f, ln_b_f)
    (w_out_g,) = gather_end("conv_out", swap_cout, sact)
    w_out_f = w_out_g.reshape(d, d)
    (h1,) = _matmul("conv_out", "nn", sact, w_out_f, m=s, n=d, k=d,
                    outs=[(F32, "plain")], extras=[(h0, "ij"), (b_out_f, "vec")], epilogue=ep_residual_bias)
    swap_mi0 = gather_swap("mlp_in0", land_mi0, h1)
    (y1,) = _rms_fwd("rms_mlp0", h1, [nmlp[0]], after=swap_mi0[0][3])
    land_mo0 = gather_land("mlp_out0", ag_mo0, y1)
    (wmi_g[0],) = gather_end("mlp_in0", swap_mi0, land_mo0[0][3])
    land_attn = None

    def out_weight0(r2):
        nonlocal land_attn
        land_attn, swap_mo0 = gather_step(r2, land=("attn", ag_attn), swap=("mlp_out0", land_mo0))
        return gather_end("mlp_out0", swap_mo0, swap_mo0[0][3])[0]

    h2, r0, r0sq = mlp_fwd(0, h1, y1, out_weight0)
    land_mi1, swap_attn = gather_step(h2, land=("mlp_in1", ag_mi1), swap=("attn", land_attn))
    ykv, y2 = _rms_fwd("rms_kv_mix1", h2, [kvn, nm[1]], after=land_mi1[0][3])
    wkv_g, wq_g, wo_g = gather_end("attn", swap_attn, y2)
    wkv_f, wq_f, wo_f = wkv_g.reshape(d, kvw), wq_g.reshape(d, d), wo_g.reshape(d, d)
    kv_parts = _matmul("kv_proj", "nn", ykv, wkv_f, m=s, n=kvw, k=d, tn=kvw // 2,
                       outs=by_residue, extras=tab_extras, epilogue=ep_rope_k, stage=True)
    q_parts = _matmul("q_proj", "nn", y2, wq_f, m=s, n=d, k=d,
                      outs=by_residue, extras=tab_extras, epilogue=ep_rope, stage=True)
    o_parts, lse_parts = [], []
    for dil, q_b, kv_b in zip(DILATIONS, q_parts, kv_parts):
        o_b, lse_b = _attn_fwd(f"attn_fwd_d{dil}", q_b, kv_b)
        o_parts.append(o_b)
        lse_parts.append(lse_b)
    o, lse = _attn_combine(o_parts, lse_parts)
    land_mo1, swap_mi1 = gather_step(o, land=("mlp_out1", ag_mo1), swap=("mlp_in1", land_mi1))
    (h3,) = _matmul("attn_out", "nn", o, wo_f, m=s, n=d, k=d,
                    outs=[(F32, "plain")], extras=[(h2, "ij")], epilogue=ep_residual)
    (y3,) = _rms_fwd("rms_mlp1", h3, [nmlp[1]], after=land_mo1[0][3])
    (wmi_g[1],) = gather_end("mlp_in1", swap_mi1, y3)

    def out_weight1(r2):
        swap_mo1 = gather_swap("mlp_out1", land_mo1, r2)
        return gather_end("mlp_out1", swap_mo1, swap_mo1[0][3])[0]

    h4, r1, r1sq = mlp_fwd(1, h3, y3, out_weight1)
    dh4, dh4b, d_fin, loss_cols = _final_loss(h4, fin, target)

    def ep_relu2_bwd(acc, ex, outs, j):
        outs[0][...] = (acc * (2.0 * ex[0][...].astype(F32))).astype(BF16)

    def mlp_bwd(idx, dhb, y, r, r2):
        (dz,) = _matmul(f"mlp_out{idx}_dx", "nt", dhb, wmo_f[idx], m=s, n=dff, k=d,
                        outs=[(BF16, "plain")], extras=[(r, "ij")], epilogue=ep_relu2_bwd)
        (dwo,) = _matmul(f"mlp_out{idx}_dw", "tn", r2, dhb, m=dff, n=d, k=s,
                         outs=[(BF16, "plain")])
        (dy,) = _matmul(f"mlp_in{idx}_dx", "nt", dz, wmi_g[idx], b_kind="col", m=s, n=d, k=dff,
                        outs=[(BF16, "plain")])
        (dwi,) = _matmul(f"mlp_in{idx}_dw", "tn", y, dz, m=d, n=dff, k=s,
                         outs=[(BF16, "col")])
        return dy, dwi, dwo.reshape(N_SHARD, dff // N_SHARD, d)

    def rs_exchange(tag, grads):
        lands = [lax.empty((N_SHARD, g.shape[1] // 2, g.shape[2]), g.dtype) for g in grads]
        return _split_start(f"sibling_start_{tag}", list(grads) + lands, len(grads), _sibling_copies)

    def rs_send(tag, names, exchanged, later):
        bufs = _split_wait(f"sibling_wait_{tag}", exchanged, _sibling_copies, later)
        n = len(names)
        sums = [_chip_sum(f"chip_sum_{nme}", g, rh, place) for nme, g, rh in zip(names, bufs[:n], bufs[n:])]
        lands = [lax.empty((N_SHARD - 1,) + cs.shape[1:], cs.dtype) for cs in sums]
        return _split_start(f"owners_start_{tag}", sums + lands, 3 * n, _owner_copies)

    def rs_sum(tag, names, sent, later):
        bufs = _split_wait(f"owners_wait_{tag}", sent, _owner_copies, later)
        n = len(names)
        own = [_owner_sum(f"owner_sum_{nme}", cs, rp, place) for nme, cs, rp in zip(names, bufs[:n], bufs[n:])]
        return _split_start(f"swap_start_{tag}", own, n, _swap_copies)

    def rs_end(tag, swapped, later):
        return _split_wait(f"swap_wait_{tag}", swapped, _swap_copies, later)

    dy3, g_wmi1, g_wmo1 = mlp_bwd(1, dh4b, y3, r1, r1sq)
    x_mlp1 = rs_exchange("mlp1", [g_wmi1, g_wmo1])
    dh3, dh3b, d_nmlp1 = _rms_bwd("rms_mlp1_bwd", h3, [(nmlp[1], dy3)], dh4, after=x_mlp1[3])

    do_parts = _matmul("attn_out_dx", "nt", dh3b, wo_f, m=s, n=d, k=d, outs=by_residue, epilogue=ep_by_residue,
                       stage=True)
    (g_wo,) = _matmul("attn_out_dw", "tn", o, dh3b, m=d, n=d, k=s, outs=[(BF16, "plain")])
    rs_mlp1 = rs_send("mlp1", ["mlp_in1", "mlp_out1"], x_mlp1, g_wo)
    lse_res, delta_res = _attn_delta(do_parts[0].reshape(s, d), o, lse, DILATIONS)
    dq_parts, dk_parts, dv_parts = [], [], []
    for dil, q_b, kv_b, do_b, lse_b, dl_b in zip(DILATIONS, q_parts, kv_parts, do_parts, lse_res, delta_res):
        dq_b, dk_b, dv_b = _attn_bwd(f"attn_bwd_d{dil}", q_b, kv_b, do_b, lse_b, dl_b)
        dq_parts.append(dq_b)
        dk_parts.append(dk_b)
        dv_parts.append(dv_b)
    dq = _residue_sum("rope_bwd_q", [(dq_parts, True)], tabs)
    dkv = _residue_sum("rope_bwd_kv", [(dk_parts, True), (dv_parts, False)], tabs)
    (g_wq,) = _matmul("q_proj_dw", "tn", y2, dq, m=d, n=d, k=s, outs=[(BF16, "plain")])
    (dy2,) = _matmul("q_proj_dx", "nt", dq, wq_f, m=s, n=d, k=d, outs=[(BF16, "plain")])
    (g_wkv,) = _matmul("kv_proj_dw", "tn", ykv, dkv, m=d, n=kvw, k=s, outs=[(BF16, "plain")])
    (dykv,) = _matmul("kv_proj_dx", "nt", dkv, wkv_f, m=s, n=d, k=kvw, outs=[(BF16, "plain")])
    x_attn = rs_exchange("attn", [g_wkv.reshape(N_SHARD, ds4, kvw), g_wq.reshape(N_SHARD, ds4, d),
                                  g_wo.reshape(N_SHARD, ds4, d)])
    dh2, dh2b, d_nm1, d_kvn = _rms_bwd("rms_kv_mix1_bwd", h2, [(nm[1], dy2), (kvn, dykv)], dh3, after=x_attn[3])
    rs_attn = rs_send("attn", ["w_kv", "w_q", "w_o"], x_attn, dh2b)

    dy1, g_wmi0, g_wmo0 = mlp_bwd(0, dh2b, y1, r0, r0sq)
    x_mlp0 = rs_exchange("mlp0", [g_wmi0, g_wmo0])
    dh1, dh1b, d_nmlp0, d_b_out = _rms_bwd("rms_mlp0_bwd", h1, [(nmlp[0], dy1)], dh2, want_colsum=True,
                                           after=[x_mlp0[3], rs_attn[3]])

    (dsact,) = _matmul("conv_out_dx", "nt", dh1b, w_out_f, m=s, n=d, k=d, outs=[(BF16, "plain")])
    (g_wout,) = _matmul("conv_out_dw", "tn", sact, dh1b, m=d, n=d, k=s, outs=[(BF16, "plain")])
    rs_mlp0 = rs_send("mlp0", ["mlp_in0", "mlp_out0"], x_mlp0, g_wout)
    dc, d_ln_g, d_ln_b, d_b_dw = _ln_silu_bwd(cpre, ln_g_f, ln_b_f, dsact, after=rs_mlp0[3])
    du, d_w_dw, d_b_in_a, d_b_in_g = _dwconv_bwd(u, w_dw_f, dc)
    (g_win,) = _matmul("conv_in_dw", "tn", y0, du, b_kind="col", m=d, n=2 * d, k=s, outs=[(BF16, "col")])
    x_conv = rs_exchange("conv", [g_win, g_wout.reshape(N_SHARD, ds4, d)])
    (dy0,) = _matmul("conv_in_dx", "nt", du, w_in_g, a_kind="col", b_kind="col", m=s, n=d, k=2 * d,
                     outs=[(BF16, "plain")], after=x_conv[3])
    rs_conv = rs_send("conv", ["w_in", "w_out"], x_conv, dy0)
    dx, _, d_nm0 = _rms_bwd("rms_mix0_bwd", h0, [(nm[0], dy0)], dh1, after=rs_conv[3])

    small_rows = [(0, d_nm0), (1, d_nm1), (2, d_nmlp0), (3, d_nmlp1), (4, d_kvn), (5, d_fin), (6, d_b_dw),
                  (7, d_ln_g), (8, d_ln_b), (9, d_b_out), (10, d_b_in_a), (11, d_b_in_g), (12, loss_cols)]
    x_small = _split_start("small_start", [_small_pack(small_rows, d_w_dw, d),
                                           lax.empty((N_DEV, SMALL_ROWS, d), F32)], N_DEV - 1, _small_copies)

    def big(name, w, m, v, g, layer=0, partial=None):
        shape = w.shape
        w3, m3, v3 = [t.reshape((-1,) + shape[-2:]) for t in (w, m, v)]
        if partial is not None:
            partial = [t.reshape(w3.shape) for t in partial]
        res = _adamw(name, w3, m3, v3, g, layer, partial)
        return [t.reshape(shape) for t in res]

    sw_mlp1 = rs_sum("mlp1", ["mlp_in1", "mlp_out1"], rs_mlp1, x_small[3])
    sw_attn = rs_sum("attn", ["w_kv", "w_q", "w_o"], rs_attn, sw_mlp1[3])
    f_wmi1, f_wmo1 = rs_end("mlp1", sw_mlp1, sw_attn[3])
    p_wmi = big("adam_mlp_in1", mlp_w_in, m_mlp_w_in, v_mlp_w_in, f_wmi1, 1)
    p_wmo = big("adam_mlp_out1", mlp_w_out, m_mlp_w_out, v_mlp_w_out, f_wmo1, 1)
    sw_mlp0 = rs_sum("mlp0", ["mlp_in0", "mlp_out0"], rs_mlp0, [p_wmi[0], p_wmo[0]])
    f_wkv, f_wq, f_wo = rs_end("attn", sw_attn, sw_mlp0[3])
    r_wkv = big("adam_w_kv", w_kv, m_w_kv, v_w_kv, f_wkv)
    r_wq = big("adam_w_q", attn_w_q, m_attn_w_q, v_attn_w_q, f_wq)
    r_wo = big("adam_w_o", attn_w_o, m_attn_w_o, v_attn_w_o, f_wo)
    sw_conv = rs_sum("conv", ["w_in", "w_out"], rs_conv, [r_wkv[0], r_wq[0], r_wo[0]])
    f_wmi0, f_wmo0 = rs_end("mlp0", sw_mlp0, sw_conv[3])
    r_wmi = big("adam_mlp_in0", mlp_w_in, m_mlp_w_in, v_mlp_w_in, f_wmi0, 0, p_wmi)
    r_wmo = big("adam_mlp_out0", mlp_w_out, m_mlp_w_out, v_mlp_w_out, f_wmo0, 0, p_wmo)
    f_win, f_wout = rs_end("conv", sw_conv, [r_wmi[0], r_wmo[0]])
    r_win = big("adam_w_in", conv_w_in, m_conv_w_in, v_conv_w_in, f_win)
    r_wout = big("adam_w_out", conv_w_out, m_conv_w_out, v_conv_w_out, f_wout)

    small_pack, small_slots = _split_wait("small_wait", x_small, _small_copies, r_wout[0])
    red = _small_sum(small_pack, small_slots, place)
    loss = red[12, 0]
    g_norm_mix = red[0:2]
    g_norm_mlp = red[2:4]
    g_kv_norm = red[4:5]
    g_final = red[5:6]

    def my_cols(row):
        return lax.dynamic_slice(red, (row, me * ds4), (1, ds4))

    g_b_dw, g_ln_g, g_ln_b, g_b_out = my_cols(6), my_cols(7), my_cols(8), my_cols(9)
    half_in = 2 * d // N_SHARD
    b_in_row = 10 + me // 2
    g_b_in = lax.dynamic_slice(red, (b_in_row, (me % 2) * half_in), (1, half_in))
    g_w_dw = lax.dynamic_slice(red, (16, me * ds4), (CONV_WIDTH, ds4))

    sm_w =[norm_mix, norm_mlp, conv_b_in, conv_w_dw.reshape(CONV_WIDTH, ds4), conv_b_dw, conv_ln_g, conv_ln_b,
            conv_b_out, kv_norm.reshape(1, d), final_norm.reshape(1, d)]
    sm_m = [m_norm_mix, m_norm_mlp, m_conv_b_in, m_conv_w_dw.reshape(CONV_WIDTH, ds4), m_conv_b_dw, m_conv_ln_g,
            m_conv_ln_b, m_conv_b_out, m_kv_norm.reshape(1, d), m_final_norm.reshape(1, d)]
    sm_v = [v_norm_mix, v_norm_mlp, v_conv_b_in, v_conv_w_dw.reshape(CONV_WIDTH, ds4), v_conv_b_dw, v_conv_ln_g,
            v_conv_ln_b, v_conv_b_out, v_kv_norm.reshape(1, d), v_final_norm.reshape(1, d)]
    sm_g = [g_norm_mix, g_norm_mlp, g_b_in, g_w_dw, g_b_dw, g_ln_g, g_ln_b, g_b_out, g_kv_norm, g_final]
    sm_d, sm_nm, sm_nv = _adam_small(sm_w, sm_m, sm_v, sm_g)
    shapes = [norm_mix.shape, norm_mlp.shape, conv_b_in.shape, conv_w_dw.shape, conv_b_dw.shape, conv_ln_g.shape,
              conv_ln_b.shape, conv_b_out.shape, kv_norm.shape, final_norm.shape]
    sm_g, sm_d, sm_nm, sm_nv = [[t.reshape(sh) for t, sh in zip(lst, shapes)] for lst in (sm_g, sm_d, sm_nm, sm_nv)]

    def order(sm, idx):
        return [sm[0], sm[1], r_win[idx], sm[2], sm[3], sm[4], sm[5], sm[6], r_wout[idx], sm[7], sm[8],
                r_wkv[idx], r_wq[idx], r_wo[idx], r_wmi[idx], r_wmo[idx], sm[9]]

    return (loss, dx.reshape(x.shape), *order(sm_g, 0), *order(sm_d, 1), *order(sm_nm, 2), *order(sm_nv, 3))
```

```python
import functools
import math

import jax
import jax.numpy as jnp
from jax import lax
from jax.experimental import pallas as pl
from jax.experimental.pallas import tpu as pltpu

F32 = jnp.float32
BF16 = jnp.bfloat16
I32 = jnp.int32

NORM_EPS = 1e-6
LN_EPS = 1e-5
HEAD_DIM = 128
N_KV_HEADS = 4
ROT_DIM = 32
ROPE_THETA = 500000.0
CONV_WIDTH = 31
CONV_PAD = 32
ATT_BLOCK = 128
ATT_STEP_BLOCKS = 16
DILATIONS = (1, 4, 16)
ADAM_LR = 0.001
ADAM_B1 = 0.9
ADAM_B2 = 0.999
ADAM_EPS = 1e-08
ADAM_WD = 0.01
ADAM_STEP = 10
N_SHARD = 4
N_DEV = 8
LANES = 128
VMEM_LIMIT = 48 * 1024 * 1024
MM_TM, MM_TN, MM_TK = 1024, 1024, 2048
ROW_TILE = 512
CONV_CB = 128
CONV_T = 128
SMALL_ROWS = 48
MESH = pl.DeviceIdType.MESH
ANY = pl.BlockSpec(memory_space=pl.ANY)
HBM = pl.BlockSpec(memory_space=pltpu.HBM)
SEM = pl.BlockSpec(memory_space=pltpu.SEMAPHORE)
SPLIT_EFFECT = pltpu.SideEffectType.DATAFLOW_SIDE_EFFECTING


def _params(*sem):
    return pltpu.CompilerParams(dimension_semantics=sem, vmem_limit_bytes=VMEM_LIMIT)


def _sigmoid(x):
    return 1.0 / (1.0 + jnp.exp(-x))


def _wspec(kind, arr_shape, br, bc, pick):
    if kind == "plain":
        return pl.BlockSpec((br, bc), pick)
    per = arr_shape[2] // bc

    def idx(*g):
        rb, cb = pick(*g)
        return (cb // per, rb, cb % per)

    return pl.BlockSpec((None, br, bc), idx)


def _stage_shape(rows, w):
    return (w // LANES, rows, LANES)


def _to_residues(val, stage_ref, out_refs, dils):
    planes, rows, _ = stage_ref.shape
    for c in range(planes):
        stage_ref[c] = val[:, c * LANES:(c + 1) * LANES]
    for out_ref, dil in zip(out_refs, dils):
        if dil == 1:
            out_ref[0] = val.astype(out_ref.dtype)
            continue
        for r in range(dil):
            for c in range(planes):
                out_ref[r, :, c * LANES:(c + 1) * LANES] = stage_ref.at[c][pl.ds(r, rows // dil, stride=dil), :].astype(
                    out_ref.dtype)


def _from_residues(src_ref, stage_ref, dil):
    planes, rows, _ = stage_ref.shape
    if dil == 1:
        return lambda c: src_ref[0, :, c * LANES:(c + 1) * LANES].astype(F32)
    for r in range(dil):
        for c in range(planes):
            stage_ref.at[c][pl.ds(r, rows // dil, stride=dil), :] = src_ref[r, :, c * LANES:(c + 1) * LANES].astype(F32)
    return lambda c: stage_ref[c]


def _matmul(name, mode, a, b, *, m, n, k, tm=MM_TM, tn=MM_TN, tk=MM_TK, a_kind="plain", b_kind="plain", outs,
            extras=(), epilogue=None, stage=False, after=None):
    tm, tn, tk = min(tm, m), min(tn, n), min(tk, k)
    if b_kind == "col" and mode in ("nn", "tn"):
        tn = min(tn, n // b.shape[0])
    if b_kind == "col" and mode == "nt":
        tk = min(tk, k // b.shape[0])
    if a_kind == "col":
        assert mode == "nt"
        tk = min(tk, k // a.shape[0])
    if any(kind == "col" for _, kind in outs):
        tn = min(tn, n // N_SHARD)
    assert m % tm == 0 and n % tn == 0 and k % tk == 0, (name, m, n, k, tm, tn, tk)
    nk = k // tk
    grid = (m // tm, n // tn, nk)
    if mode == "nn":
        a_spec = pl.BlockSpec((tm, tk), lambda i, j, kk: (i, kk))
        b_spec = _wspec(b_kind, b.shape, tk, tn, lambda i, j, kk: (kk, j))
        dims = (((1,), (0,)), ((), ()))
    elif mode == "nt":
        a_spec = _wspec(a_kind, a.shape, tm, tk, lambda i, j, kk: (i, kk))
        b_spec = _wspec(b_kind, b.shape, tn, tk, lambda i, j, kk: (j, kk))
        dims = (((1,), (1,)), ((), ()))
    else:
        a_spec = pl.BlockSpec((tk, tm), lambda i, j, kk: (kk, i))
        b_spec = _wspec(b_kind, b.shape, tk, tn, lambda i, j, kk: (kk, j))
        dims = (((0,), (0,)), ((), ()))
    out_shape, out_specs = [], []
    for dtype, kind in outs:
        if isinstance(kind, tuple):
            dil = kind[1]
            out_shape.append(jax.ShapeDtypeStruct((dil, m // dil, n), dtype))
            out_specs.append(pl.BlockSpec((dil, tm // dil, tn), lambda i, j, kk: (0, i, j)))
            continue
        shape = (m, n) if kind == "plain" else (N_SHARD, m, n // N_SHARD)
        out_shape.append(jax.ShapeDtypeStruct(shape, dtype))
        out_specs.append(_wspec(kind, shape, tm, tn, lambda i, j, kk: (i, j)))
    n_ex = len(extras)
    deps = [] if after is None else [after]
    ex_specs = {"ij": pl.BlockSpec((tm, tn), lambda i, j, kk: (i, j)),
                "vec": pl.BlockSpec((1, tn), lambda i, j, kk: (0, j)),
                "rows": pl.BlockSpec((tm, LANES), lambda i, j, kk: (i, 0))}
    out0 = 2 + n_ex + len(deps)

    def body(*refs):
        a_ref, b_ref = refs[0], refs[1]
        ex_refs = refs[2:2 + n_ex]
        out_refs = refs[out0:out0 + len(outs)]
        j = pl.program_id(1)

        def finish(res):
            if epilogue is None:
                out_refs[0][...] = res.astype(out_refs[0].dtype)
            elif stage:
                epilogue(res, ex_refs, out_refs, j, refs[-1])
            else:
                epilogue(res, ex_refs, out_refs, j)

        prod = lax.dot_general(a_ref[...], b_ref[...], dims, preferred_element_type=F32)
        if nk == 1:
            finish(prod)
            return
        acc_ref = refs[out0 + len(outs)]
        kk = pl.program_id(2)

        @pl.when(kk == 0)
        def _():
            acc_ref[...] = prod

        @pl.when(kk > 0)
        def _():
            acc_ref[...] += prod

        @pl.when(kk == nk - 1)
        def _():
            finish(acc_ref[...])

    res = pl.pallas_call(
        body, name=name, grid=grid,
        in_specs=[a_spec, b_spec] + [ex_specs[how] for _, how in extras] + [ANY] * len(deps),
        out_specs=out_specs, out_shape=out_shape,
        scratch_shapes=[pltpu.VMEM((tm, tn), F32)] * (nk > 1) + [pltpu.VMEM(_stage_shape(tm, tn), F32)] * bool(stage),
        compiler_params=_params("parallel", "parallel", "arbitrary"),
    )(a, b, *[e for e, _ in extras], *deps)
    return res


def _rope_tables(seq):
    half = ROT_DIM // 2
    pos = jnp.arange(seq, dtype=F32)
    inv = ROPE_THETA ** (-jnp.arange(0, ROT_DIM, 2, dtype=F32) / ROT_DIM)
    ang = pos[:, None] * inv[None, :]
    cos, sin = jnp.cos(ang), jnp.sin(ang)
    zeros = jnp.zeros((seq, HEAD_DIM - ROT_DIM), F32)
    ctab = jnp.concatenate([cos, cos, zeros + 1.0], axis=1)
    atab = jnp.concatenate([-sin, jnp.zeros((seq, half), F32), zeros], axis=1)
    btab = jnp.concatenate([jnp.zeros((seq, half), F32), sin, zeros], axis=1)
    return ctab, atab, btab


def _rope_apply(x, ctab, atab, btab, sign):
    w = x.shape[1]
    reps = w // HEAD_DIM
    half = ROT_DIM // 2
    c = jnp.tile(ctab, (1, reps))
    a = jnp.tile(atab, (1, reps))
    b = jnp.tile(btab, (1, reps))
    up = pltpu.roll(x, w - half, 1)
    down = pltpu.roll(x, half, 1)
    return x * c + sign * (up * a + down * b)


def _rows(t, w):
    return pl.BlockSpec((t, w), lambda i: (i, 0))


def _fixed(shape):
    nd = len(shape)
    return pl.BlockSpec(shape, lambda i: (0,) * nd)


def _behind(after):
    deps = [] if after is None else (list(after) if isinstance(after, (list, tuple)) else [after])
    return deps, [ANY] * len(deps)


def _rms_fwd(name, x, gains, after=None):
    s, d = x.shape
    t = min(ROW_TILE, s)
    ng = len(gains)
    deps, dep_specs = _behind(after)

    def body(*all_refs):
        x_ref, refs = all_refs[len(deps)], all_refs[len(deps) + 1:]
        xv = x_ref[...]
        r = lax.rsqrt(jnp.mean(xv * xv, axis=-1, keepdims=True) + NORM_EPS)
        xn = xv * r
        for g_ref, y_ref in zip(refs[:ng], refs[ng:]):
            y_ref[...] = (xn * g_ref[...]).astype(BF16)

    return pl.pallas_call(
        body, name=name, grid=(s // t,),
        in_specs=dep_specs + [_rows(t, d)] + [_fixed((1, d))] * ng,
        out_specs=[_rows(t, d)] * ng,
        out_shape=[jax.ShapeDtypeStruct((s, d), BF16)] * ng,
        compiler_params=_params("parallel"),
    )(*deps, x, *gains)


def _rms_bwd(name, x, pairs, dh_in, want_colsum=False, after=None):
    s, d = x.shape
    n_p = len(pairs)
    t = min(ROW_TILE // n_p, s)
    deps, dep_specs = _behind(after)

    def body(*all_refs):
        x_ref, dh_ref, refs = all_refs[len(deps)], all_refs[len(deps) + 1], all_refs[len(deps) + 2:]
        g_refs = refs[:n_p]
        dy_refs = refs[n_p:2 * n_p]
        dh_out, dhb_out = refs[2 * n_p], refs[2 * n_p + 1]
        dg_refs = refs[2 * n_p + 2:2 * n_p + 2 + n_p]
        cs_ref = refs[-1] if want_colsum else None
        i = pl.program_id(0)
        xv = x_ref[...]
        r = lax.rsqrt(jnp.mean(xv * xv, axis=-1, keepdims=True) + NORM_EPS)
        xn = xv * r
        dh = dh_ref[...]
        for g_ref, dy_ref, dg_ref in zip(g_refs, dy_refs, dg_refs):
            dy = dy_ref[...].astype(F32)
            u = dy * g_ref[...]
            dh = dh + r * (u - xn * jnp.mean(u * xn, axis=-1, keepdims=True))
            part = jnp.sum(dy * xn, axis=0, keepdims=True)

            @pl.when(i == 0)
            def _():
                dg_ref[...] = part

            @pl.when(i > 0)
            def _():
                dg_ref[...] += part

        dh_out[...] = dh
        dhb_out[...] = dh.astype(BF16)
        if want_colsum:
            col = jnp.sum(dh, axis=0, keepdims=True)

            @pl.when(i == 0)
            def _():
                cs_ref[...] = col

            @pl.when(i > 0)
            def _():
                cs_ref[...] += col

    n_vec = n_p + (1 if want_colsum else 0)
    return pl.pallas_call(
        body, name=name, grid=(s // t,),
        in_specs=dep_specs + [_rows(t, d), _rows(t, d)] + [_fixed((1, d))] * n_p + [_rows(t, d)] * n_p,
        out_specs=[_rows(t, d), _rows(t, d)] + [_fixed((1, d))] * n_vec,
        out_shape=[jax.ShapeDtypeStruct((s, d), F32), jax.ShapeDtypeStruct((s, d), BF16)]
        + [jax.ShapeDtypeStruct((1, d), F32)] * n_vec,
        compiler_params=_params("arbitrary"),
    )(*deps, x, dh_in, *[g for g, _ in pairs], *[dy for _, dy in pairs])


def _final_loss(x, g, target):
    s, d = x.shape
    t = min(ROW_TILE, s)

    def body(x_ref, g_ref, t_ref, dh_out, dhb_out, dg_ref, loss_ref):
        i = pl.program_id(0)
        xv = x_ref[...]
        gv = g_ref[...]
        r = lax.rsqrt(jnp.mean(xv * xv, axis=-1, keepdims=True) + NORM_EPS)
        xn = xv * r
        diff = xn * gv - t_ref[...]
        dy = diff / d
        u = dy * gv
        dh = r * (u - xn * jnp.mean(u * xn, axis=-1, keepdims=True))
        dh_out[...] = dh
        dhb_out[...] = dh.astype(BF16)
        dg = jnp.sum(dy * xn, axis=0, keepdims=True)
        lc = jnp.sum(0.5 * diff * dy, axis=0, keepdims=True)

        @pl.when(i == 0)
        def _():
            dg_ref[...] = dg
            loss_ref[...] = lc

        @pl.when(i > 0)
        def _():
            dg_ref[...] += dg
            loss_ref[...] += lc

    return pl.pallas_call(
        body, name="final_loss", grid=(s // t,),
        in_specs=[_rows(t, d), _fixed((1, d)), _rows(t, d)],
        out_specs=[_rows(t, d), _rows(t, d), _fixed((1, d)), _fixed((1, d))],
        out_shape=[jax.ShapeDtypeStruct((s, d), F32), jax.ShapeDtypeStruct((s, d), BF16),
                   jax.ShapeDtypeStruct((1, d), F32), jax.ShapeDtypeStruct((1, d), F32)],
        compiler_params=_params("arbitrary"),
    )(x, g, target)


def _ln_silu_fwd(c, g, b):
    s, d = c.shape
    t = min(ROW_TILE, s)

    def body(c_ref, g_ref, b_ref, s_ref):
        cv = c_ref[...]
        mu = jnp.mean(cv, axis=-1, keepdims=True)
        xc = cv - mu
        rs = lax.rsqrt(jnp.mean(xc * xc, axis=-1, keepdims=True) + LN_EPS)
        ln = xc * rs * g_ref[...] + b_ref[...]
        s_ref[...] = (ln * _sigmoid(ln)).astype(BF16)

    return pl.pallas_call(
        body, name="ln_silu_fwd", grid=(s // t,),
        in_specs=[_rows(t, d), _fixed((1, d)), _fixed((1, d))],
        out_specs=_rows(t, d), out_shape=jax.ShapeDtypeStruct((s, d), BF16),
        compiler_params=_params("parallel"),
    )(c, g, b)


def _ln_silu_bwd(c, g, b, ds, after=None):
    s, d = c.shape
    t = min(ROW_TILE, s)
    deps, dep_specs = _behind(after)

    def body(*all_refs):
        c_ref, g_ref, b_ref, ds_ref, dc_ref, dg_ref, db_ref, dbdw_ref = all_refs[len(deps):]
        i = pl.program_id(0)
        cv = c_ref[...]
        gv = g_ref[...]
        mu = jnp.mean(cv, axis=-1, keepdims=True)
        xc = cv - mu
        rs = lax.rsqrt(jnp.mean(xc * xc, axis=-1, keepdims=True) + LN_EPS)
        nrm = xc * rs
        ln = nrm * gv + b_ref[...]
        sig = _sigmoid(ln)
        dln = ds_ref[...].astype(F32) * sig * (1.0 + ln * (1.0 - sig))
        dn = dln * gv
        dc = rs * (dn - jnp.mean(dn, axis=-1, keepdims=True)
                   - nrm * jnp.mean(dn * nrm, axis=-1, keepdims=True))
        dc_ref[...] = dc
        pg = jnp.sum(dln * nrm, axis=0, keepdims=True)
        pb = jnp.sum(dln, axis=0, keepdims=True)
        pc = jnp.sum(dc, axis=0, keepdims=True)

        @pl.when(i == 0)
        def _():
            dg_ref[...] = pg
            db_ref[...] = pb
            dbdw_ref[...] = pc

        @pl.when(i > 0)
        def _():
            dg_ref[...] += pg
            db_ref[...] += pb
            dbdw_ref[...] += pc

    return pl.pallas_call(
        body, name="ln_silu_bwd", grid=(s // t,),
        in_specs=dep_specs + [_rows(t, d), _fixed((1, d)), _fixed((1, d)), _rows(t, d)],
        out_specs=[_rows(t, d)] + [_fixed((1, d))] * 3,
        out_shape=[jax.ShapeDtypeStruct((s, d), F32)] + [jax.ShapeDtypeStruct((1, d), F32)] * 3,
        compiler_params=_params("arbitrary"),
    )(*deps, c, g, b, ds)


def _residue_spec(dil, t, w):
    return pl.BlockSpec((dil, t // dil, w), lambda i: (0, i, 0))


def _attn_combine(o_list, lse_list):
    dil0, sd0, d = o_list[0].shape
    s = dil0 * sd0
    lw = lse_list[0].shape[2]
    group = d // HEAD_DIM // N_KV_HEADS
    t = min(ROW_TILE, s)
    nb = len(o_list)
    dils = [o.shape[0] for o in o_list]

    def body(*refs):
        o_out, l_out = refs[2 * nb], refs[2 * nb + 1]
        o_stage, l_stage = refs[2 * nb + 2:3 * nb + 2], refs[3 * nb + 2:]
        o_planes = [_from_residues(src, stage, dil) for src, stage, dil in zip(refs[:nb], o_stage, dils)]
        l_planes = [_from_residues(src, stage, dil) for src, stage, dil in zip(refs[nb:2 * nb], l_stage, dils)]
        for kh in range(N_KV_HEADS):
            ls = [plane(kh) for plane in l_planes]
            mx = ls[0]
            for l in ls[1:]:
                mx = jnp.maximum(mx, l)
            es = [jnp.exp(l - mx) for l in ls]
            den = es[0]
            for e in es[1:]:
                den = den + e
            l_out[:, kh * LANES:(kh + 1) * LANES] = mx + jnp.log(den)
            ws = [e / den for e in es]
            for g in range(group):
                h = kh * group + g
                acc = jnp.zeros((t, HEAD_DIM), F32)
                for plane, w in zip(o_planes, ws):
                    acc = acc + w[:, g:g + 1] * plane(h)
                o_out[:, h * HEAD_DIM:(h + 1) * HEAD_DIM] = acc.astype(BF16)

    return pl.pallas_call(
        body, name="attn_combine", grid=(s // t,),
        in_specs=[_residue_spec(dil, t, d) for dil in dils] + [_residue_spec(dil, t, lw) for dil in dils],
        out_specs=[_rows(t, d), _rows(t, lw)],
        out_shape=[jax.ShapeDtypeStruct((s, d), BF16), jax.ShapeDtypeStruct((s, lw), F32)],
        scratch_shapes=[pltpu.VMEM(_stage_shape(t, d), F32)] * nb + [pltpu.VMEM(_stage_shape(t, lw), F32)] * nb,
        compiler_params=_params("parallel"),
    )(*o_list, *lse_list)


def _attn_delta(do, o, lse, dils):
    s, d = o.shape
    lw = lse.shape[1]
    group = d // HEAD_DIM // N_KV_HEADS
    t = min(ROW_TILE, s)
    nd = len(dils)

    def body(do_ref, o_ref, lse_ref, *refs):
        stage = refs[-1]
        lane = lax.broadcasted_iota(I32, (t, LANES), 1)
        planes = []
        for kh in range(N_KV_HEADS):
            out = jnp.zeros((t, LANES), F32)
            for g in range(group):
                cols = slice((kh * group + g) * HEAD_DIM, (kh * group + g + 1) * HEAD_DIM)
                v = jnp.sum(do_ref[:, cols].astype(F32) * o_ref[:, cols].astype(F32), axis=-1, keepdims=True)
                out = jnp.where(lane == g, v, out)
            planes.append(out)
        _to_residues(lse_ref[...], stage, refs[:nd], dils)
        _to_residues(jnp.concatenate(planes, axis=1), stage, refs[nd:2 * nd], dils)

    res = pl.pallas_call(
        body, name="attn_delta", grid=(s // t,),
        in_specs=[_rows(t, d), _rows(t, d), _rows(t, lw)],
        out_specs=[_residue_spec(dil, t, lw) for dil in dils] * 2,
        out_shape=[jax.ShapeDtypeStruct((dil, s // dil, lw), F32) for dil in dils] * 2,
        scratch_shapes=[pltpu.VMEM(_stage_shape(t, lw), F32)],
        compiler_params=_params("parallel"),
    )(do, o, lse)
    return res[:nd], res[nd:]


def _residue_sum(name, groups, tabs):
    first = groups[0][0][0]
    s, w = first.shape[0] * first.shape[1], first.shape[2]
    t = min(ROW_TILE, s)
    flat = [p for parts, _ in groups for p in parts]

    def body(*refs):
        c_ref, a_ref, b_ref = refs[len(flat):len(flat) + 3]
        out = refs[len(flat) + 3]
        stages = refs[len(flat) + 4:]
        k = 0
        for gi, (parts, rotate) in enumerate(groups):
            planes = [_from_residues(refs[k + i], stages[k + i], p.shape[0]) for i, p in enumerate(parts)]
            k += len(parts)
            for c in range(w // LANES):
                tot = planes[0](c)
                for plane in planes[1:]:
                    tot = tot + plane(c)
                if rotate:
                    tot = _rope_apply(tot, c_ref[...], a_ref[...], b_ref[...], -1.0)
                out[:, gi * w + c * LANES:gi * w + (c + 1) * LANES] = tot.astype(BF16)

    return pl.pallas_call(
        body, name=name, grid=(s // t,),
        in_specs=[_residue_spec(p.shape[0], t, w) for p in flat] + [_rows(t, HEAD_DIM)] * 3,
        out_specs=_rows(t, len(groups) * w), out_shape=jax.ShapeDtypeStruct((s, len(groups) * w), BF16),
        scratch_shapes=[pltpu.VMEM(_stage_shape(t, w), F32) for _ in flat],
        compiler_params=_params("parallel"),
    )(*flat, *tabs)


def _dwconv_fwd(u, w_dw, b_dw, after=None):
    s, d2 = u.shape
    d = d2 // 2
    cb = min(CONV_CB, d)
    nblk = d // cb
    tt = min(CONV_T, s)
    deps, dep_specs = _behind(after)

    def body(*all_refs):
        ua_ref, ug_ref, w_ref, b_ref, c_ref, xp_ref = all_refs[len(deps):]
        gl =ua_ref[...].astype(F32) * _sigmoid(ug_ref[...].astype(F32))
        xp_ref[0:CONV_PAD, :] = jnp.zeros((CONV_PAD, cb), F32)
        xp_ref[CONV_PAD:, :] = gl
        wv = w_ref[...]
        bv = b_ref[...]
        for t0 in range(0, s, tt):
            acc = jnp.zeros((tt, cb), F32) + bv
            for kk in range(CONV_WIDTH):
                off = t0 + CONV_PAD - (CONV_WIDTH - 1) + kk
                acc = acc + wv[kk:kk + 1, :] * xp_ref[off:off + tt, :]
            c_ref[t0:t0 + tt, :] = acc

    return pl.pallas_call(
        body, name="dwconv_fwd", grid=(nblk,),
        in_specs=dep_specs + [pl.BlockSpec((s, cb), lambda j: (0, j)), pl.BlockSpec((s, cb), lambda j: (0, j + nblk)),
                              pl.BlockSpec((CONV_PAD, cb), lambda j: (0, j)), pl.BlockSpec((1, cb), lambda j: (0, j))],
        out_specs=pl.BlockSpec((s, cb), lambda j: (0, j)),
        out_shape=jax.ShapeDtypeStruct((s, d), F32),
        scratch_shapes=[pltpu.VMEM((s + CONV_PAD, cb), F32)],
        compiler_params=_params("parallel"),
    )(*deps, u, u, w_dw, b_dw)


def _dwconv_bwd(u, w_dw, dc):
    s, d2 = u.shape
    d = d2 // 2
    cb = min(CONV_CB, d)
    nblk = d // cb
    tt = min(CONV_T, s)

    def body(ua_ref, ug_ref, w_ref, dc_ref, du_ref, dw_ref, dba_ref, dbg_ref, glp_ref, dcp_ref, acc_ref):
        a = ua_ref[...].astype(F32)
        sig = _sigmoid(ug_ref[...].astype(F32))
        glp_ref[0:CONV_PAD, :] = jnp.zeros((CONV_PAD, cb), F32)
        glp_ref[CONV_PAD:, :] = a * sig
        dcp_ref[0:s, :] = dc_ref[...]
        dcp_ref[s:, :] = jnp.zeros((CONV_PAD, cb), F32)
        acc_ref[...] = jnp.zeros_like(acc_ref)
        wv = w_ref[...]
        dba = jnp.zeros((1, cb), F32)
        dbg = jnp.zeros((1, cb), F32)
        for t0 in range(0, s, tt):
            dgl = jnp.zeros((tt, cb), F32)
            dct = dc_ref[t0:t0 + tt, :]
            for kk in range(CONV_WIDTH):
                off = t0 + (CONV_WIDTH - 1) - kk
                dgl = dgl + wv[kk:kk + 1, :] * dcp_ref[off:off + tt, :]
                goff = t0 + CONV_PAD - (CONV_WIDTH - 1) + kk
                prod = dct * glp_ref[goff:goff + tt, :]
                acc_ref[8 * kk:8 * kk + 8, :] += jnp.sum(prod.reshape(tt // 8, 8, cb), axis=0)
            at = ua_ref[t0:t0 + tt, :].astype(F32)
            st = _sigmoid(ug_ref[t0:t0 + tt, :].astype(F32))
            da = dgl * st
            dg = dgl * at * st * (1.0 - st)
            du_ref[0, t0:t0 + tt, :] = da.astype(BF16)
            du_ref[1, t0:t0 + tt, :] = dg.astype(BF16)
            dba = dba + jnp.sum(da, axis=0, keepdims=True)
            dbg = dbg + jnp.sum(dg, axis=0, keepdims=True)
        dba_ref[...] = dba
        dbg_ref[...] = dbg
        for kk in range(CONV_WIDTH):
            dw_ref[kk:kk + 1, :] = jnp.sum(acc_ref[8 * kk:8 * kk + 8, :], axis=0, keepdims=True)
        dw_ref[CONV_WIDTH:, :] = jnp.zeros((CONV_PAD - CONV_WIDTH, cb), F32)

    blk = pl.BlockSpec((s, cb), lambda j: (0, j))
    vec = pl.BlockSpec((1, cb), lambda j: (0, j))
    return pl.pallas_call(
        body, name="dwconv_bwd", grid=(nblk,),
        in_specs=[blk, pl.BlockSpec((s, cb), lambda j: (0, j + nblk)),
                  pl.BlockSpec((CONV_PAD, cb), lambda j: (0, j)), blk],
        out_specs=[pl.BlockSpec((2, s, cb), lambda j: (0, 0, j)), pl.BlockSpec((CONV_PAD, cb), lambda j: (0, j)),
                   vec, vec],
        out_shape=[jax.ShapeDtypeStruct((2, s, d), BF16), jax.ShapeDtypeStruct((CONV_PAD, d), F32),
                   jax.ShapeDtypeStruct((1, d), F32), jax.ShapeDtypeStruct((1, d), F32)],
        scratch_shapes=[pltpu.VMEM((s + CONV_PAD, cb), F32), pltpu.VMEM((s + CONV_PAD, cb), F32),
                        pltpu.VMEM((8 * CONV_PAD, cb), F32)],
        compiler_params=_params("parallel"),
    )(u, u, w_dw, dc)


def _stack_heads(x, group):
    return jnp.concatenate([x[:, g * HEAD_DIM:(g + 1) * HEAD_DIM] for g in range(group)], axis=0)


def _unstack_heads(x, group):
    return jnp.concatenate([x[g * ATT_BLOCK:(g + 1) * ATT_BLOCK, :] for g in range(group)], axis=1)


def _stack_cols(x, group):
    return jnp.concatenate([x[:, g:g + 1] for g in range(group)], axis=0)


def _band_bias(group):
    rows = group * ATT_BLOCK
    row = lax.broadcasted_iota(I32, (rows, 2 * ATT_BLOCK), 0) % ATT_BLOCK
    col = lax.broadcasted_iota(I32, (rows, 2 * ATT_BLOCK), 1)
    band = jnp.where((col >= row) & (col <= row + ATT_BLOCK), 0.0, -jnp.inf).astype(F32)
    first = jnp.where(lax.broadcasted_iota(I32, (1, 2 * ATT_BLOCK), 1) >= ATT_BLOCK, 0.0, -jnp.inf).astype(F32)
    return band, first


def _masked_scores(qs, kw, band_ref, first_ref, nb, scale):
    sc = lax.dot_general(qs, kw, (((1,), (1,)), ((), ())), preferred_element_type=F32) * scale + band_ref[...]
    return sc + jnp.where(nb > 0, 0.0, first_ref[...])


def _window(ref, nb):
    prev = pl.multiple_of(jnp.maximum(nb - 1, 0) * ATT_BLOCK, ATT_BLOCK)
    cur = pl.multiple_of(nb * ATT_BLOCK, ATT_BLOCK)
    return jnp.concatenate([ref[pl.ds(prev, ATT_BLOCK), :], ref[pl.ds(cur, ATT_BLOCK), :]], axis=0)


def _residues_per_step(dil, nblk):
    return max(1, min(dil, ATT_STEP_BLOCKS // nblk))


def _attn_fwd(name, q, kv):
    dil, sd, d = q.shape
    group = d // HEAD_DIM // N_KV_HEADS
    gw = group * HEAD_DIM
    nblk = sd // ATT_BLOCK
    scale = 1.0 / math.sqrt(HEAD_DIM)
    nt = (((1,), (1,)), ((), ()))

    rb = _residues_per_step(dil, nblk)

    def body(q_all, k_all, v_all, band_ref, first_ref, o_all, lse_all):
        lane = lax.broadcasted_iota(I32, (ATT_BLOCK, LANES), 1)
        for rr in range(rb):
            q_ref, k_ref, v_ref, o_ref, lse_ref = [ref.at[rr] for ref in (q_all, k_all, v_all, o_all, lse_all)]

            def step(nb, carry):
                rows = pl.ds(pl.multiple_of(nb * ATT_BLOCK, ATT_BLOCK), ATT_BLOCK)
                qs = _stack_heads(q_ref[rows, :], group)
                kw = _window(k_ref, nb)
                vw = _window(v_ref, nb)
                sc = _masked_scores(qs, kw, band_ref, first_ref, nb, scale)
                mx = jnp.max(sc, axis=-1, keepdims=True)
                p = jnp.exp(sc - mx)
                l = jnp.sum(p, axis=-1, keepdims=True)
                o = jnp.dot(p.astype(BF16), vw, preferred_element_type=F32) / l
                o_ref[rows, :] = _unstack_heads(o, group).astype(BF16)
                lse = mx + jnp.log(l)
                out = jnp.zeros((ATT_BLOCK, LANES), F32)
                for g in range(group):
                    out = jnp.where(lane == g, lse[g * ATT_BLOCK:(g + 1) * ATT_BLOCK, :], out)
                lse_ref[rows, :] = out
                return carry

            lax.fori_loop(0, nblk, step, 0, unroll=min(2, nblk))

    kvh = N_KV_HEADS
    band, first = _band_bias(group)
    qspec = pl.BlockSpec((rb, sd, gw), lambda r, h: (r, 0, h))
    kspec = pl.BlockSpec((rb, sd, HEAD_DIM), lambda r, h: (r, 0, h))
    return pl.pallas_call(
        body, name=name, grid=(dil // rb, kvh),
        in_specs=[qspec, kspec, pl.BlockSpec((rb, sd, HEAD_DIM), lambda r, h: (r, 0, kvh + h)),
                  pl.BlockSpec(band.shape, lambda r, h: (0, 0)), pl.BlockSpec(first.shape, lambda r, h: (0, 0))],
        out_specs=[qspec, kspec],
        out_shape=[jax.ShapeDtypeStruct((dil, sd, d), BF16),
                   jax.ShapeDtypeStruct((dil, sd, kvh * LANES), F32)],
        compiler_params=_params("parallel", "parallel"),
    )(q, kv, kv, band, first)


def _attn_bwd(name, q, kv, do, lse, delta):
    dil, sd, d = q.shape
    group = d // HEAD_DIM // N_KV_HEADS
    gw = group * HEAD_DIM
    nblk = sd // ATT_BLOCK
    scale = 1.0 / math.sqrt(HEAD_DIM)
    nt = (((1,), (1,)), ((), ()))
    tn = (((0,), (0,)), ((), ()))

    rb = _residues_per_step(dil, nblk)

    def body(q_all, k_all, v_all, do_all, lse_all, dl_all, band_ref, first_ref, dq_all, dk_all, dv_all, dk_accs,
             dv_accs):
        dk_accs[...] = jnp.zeros_like(dk_accs)
        dv_accs[...] = jnp.zeros_like(dv_accs)
        for rr in range(rb):
            q_ref, k_ref, v_ref, do_ref, lse_ref, dl_ref, dq_ref, dk_ref, dv_ref, dk_acc, dv_acc = [
                ref.at[rr] for ref in (q_all, k_all, v_all, do_all, lse_all, dl_all, dq_all, dk_all, dv_all,
                                       dk_accs, dv_accs)]

            def step(nb, carry):
                rows = pl.ds(pl.multiple_of(nb * ATT_BLOCK, ATT_BLOCK), ATT_BLOCK)
                qs = _stack_heads(q_ref[rows, :], group)
                dos = _stack_heads(do_ref[rows, :], group)
                ls = _stack_cols(lse_ref[rows, :], group)
                dl = _stack_cols(dl_ref[rows, :], group)
                kw = _window(k_ref, nb)
                vw = _window(v_ref, nb)
                p = jnp.exp(_masked_scores(qs, kw, band_ref, first_ref, nb, scale) - ls)
                dp = lax.dot_general(dos, vw, nt, preferred_element_type=F32)
                ds = (p * (dp - dl) * scale).astype(BF16)
                dq = jnp.dot(ds, kw, preferred_element_type=F32)
                dq_ref[rows, :] = _unstack_heads(dq, group).astype(BF16)
                win = pl.ds(pl.multiple_of(nb * ATT_BLOCK, ATT_BLOCK), 2 * ATT_BLOCK)
                dk_acc[win, :] += lax.dot_general(ds, qs, tn, preferred_element_type=F32)
                dv_acc[win, :] += lax.dot_general(p.astype(BF16), dos, tn, preferred_element_type=F32)
                return carry

            lax.fori_loop(0, nblk, step, 0, unroll=min(2, nblk))
            dk_ref[...] = dk_acc[ATT_BLOCK:, :]
            dv_ref[...] = dv_acc[ATT_BLOCK:, :]

    kvh = N_KV_HEADS
    band, first = _band_bias(group)
    qspec = pl.BlockSpec((rb, sd, gw), lambda r, h: (r, 0, h))
    kspec = pl.BlockSpec((rb, sd, HEAD_DIM), lambda r, h: (r, 0, h))
    return pl.pallas_call(
        body, name=name, grid=(dil // rb, kvh),
        in_specs=[qspec, kspec, pl.BlockSpec((rb, sd, HEAD_DIM), lambda r, h: (r, 0, kvh + h)),
                  qspec, kspec, kspec,
                  pl.BlockSpec(band.shape, lambda r, h: (0, 0)), pl.BlockSpec(first.shape, lambda r, h: (0, 0))],
        out_specs=[qspec, kspec, kspec],
        out_shape=[jax.ShapeDtypeStruct((dil, sd, d), BF16),
                   jax.ShapeDtypeStruct((dil, sd, kvh * HEAD_DIM), F32),
                   jax.ShapeDtypeStruct((dil, sd, kvh * HEAD_DIM), F32)],
        scratch_shapes=[pltpu.VMEM((rb, sd + ATT_BLOCK, HEAD_DIM), F32)] * 2,
        compiler_params=_params("parallel", "parallel"),
    )(q, kv, kv, do, lse, delta, band, first)


def _cast_bf16(name, w, layer, place, after=None):
    _, r, c = w.shape
    tr = min(512, r)
    deps = [] if after is None else [after]

    def body(pl_ref, w_ref, *refs):
        refs[-1][...] = w_ref[...].astype(BF16)

    return pl.pallas_call(
        body, name=name,
        grid_spec=pltpu.PrefetchScalarGridSpec(
            num_scalar_prefetch=1, grid=(r // tr,),
            in_specs=[pl.BlockSpec((None, tr, c), lambda i, p: (layer, i, 0))] + [ANY] * len(deps),
            out_specs=pl.BlockSpec((None, tr, c), lambda i, p: (p[1], i, 0))),
        out_shape=jax.ShapeDtypeStruct((N_SHARD, r, c), BF16),
        compiler_params=_params("parallel"),
    )(place, w, *deps)


def _chip_sum(name, g, rh, place):
    _, r, c = g.shape
    rh2 = r // 2
    tr = min(512, rh2)
    nb = rh2 // tr

    def body(pl_ref, g_ref, rh_ref, o_ref):
        o_ref[...] = (g_ref[...].astype(F32) + rh_ref[...].astype(F32)).astype(BF16)

    return pl.pallas_call(
        body, name=name,
        grid_spec=pltpu.PrefetchScalarGridSpec(
            num_scalar_prefetch=1, grid=(N_SHARD, nb),
            in_specs=[pl.BlockSpec((None, tr, c), lambda s, i, p: (s, p[0] * nb + i, 0)),
                      pl.BlockSpec((None, tr, c), lambda s, i, p: (s, i, 0))],
            out_specs=pl.BlockSpec((None, tr, c), lambda s, i, p: (s, i, 0))),
        out_shape=jax.ShapeDtypeStruct((N_SHARD, rh2, c), BF16),
        compiler_params=_params("parallel", "parallel"),
    )(place, g, rh)


def _owner_sum(name, cs, rp, place):
    _, rh2, c = cs.shape
    tr = min(512, rh2)
    nb = rh2 // tr

    def body(pl_ref, cs_ref, r0_ref, r1_ref, r2_ref, o_ref):
        o_ref[...] = ((cs_ref[...].astype(F32) + r0_ref[...].astype(F32))
                      + (r1_ref[...].astype(F32) + r2_ref[...].astype(F32)))

    def rspec(j):
        return pl.BlockSpec((None, tr, c), lambda i, p: (j, i, 0))

    return pl.pallas_call(
        body, name=name,
        grid_spec=pltpu.PrefetchScalarGridSpec(
            num_scalar_prefetch=1, grid=(nb,),
            in_specs=[pl.BlockSpec((None, tr, c), lambda i, p: (p[1], i, 0)), rspec(0), rspec(1), rspec(2)],
            out_specs=pl.BlockSpec((tr, c), lambda i, p: (p[0] * nb + i, 0))),
        out_shape=jax.ShapeDtypeStruct((2 * rh2, c), F32),
        compiler_params=_params("parallel"),
    )(place, cs, rp, rp, rp)


def _adam_math(w, g, m, v):
    m = ADAM_B1 * m + (1.0 - ADAM_B1) * g
    v = ADAM_B2 * v + (1.0 - ADAM_B2) * (g * g)
    m_hat = m / (1.0 - ADAM_B1 ** ADAM_STEP)
    v_hat = v / (1.0 - ADAM_B2 ** ADAM_STEP)
    delta = -ADAM_LR * (m_hat / (jnp.sqrt(v_hat) + ADAM_EPS) + ADAM_WD * w)
    return delta, m, v


def _adamw(name, w, m, v, g, layer, partial=None):
    nl, r, c = w.shape
    tr = min(256, r)

    def body(w_ref, m_ref, v_ref, g_ref, *refs):
        go_ref, d_ref, mo_ref, vo_ref = refs[-4:]
        gv = g_ref[...]
        delta, m_new, v_new = _adam_math(w_ref[...], gv, m_ref[...], v_ref[...])
        go_ref[...] = gv
        d_ref[...] = delta
        mo_ref[...] = m_new
        vo_ref[...] = v_new

    wspec = pl.BlockSpec((None, tr, c), lambda i: (layer, i, 0))
    prev = [] if partial is None else list(partial)
    return pl.pallas_call(
        body, name=name, grid=(r // tr,),
        in_specs=[wspec] * 3 + [pl.BlockSpec((tr, c), lambda i: (i, 0))] + [ANY] * len(prev),
        out_specs=[wspec] * 4,
        out_shape=[jax.ShapeDtypeStruct((nl, r, c), F32)] * 4,
        input_output_aliases={4 + i: i for i in range(len(prev))},
        compiler_params=_params("parallel"),
    )(w, m, v, g, *prev)


def _adam_small(ws, ms, vs, gs):
    n = len(ws)

    def body(*refs):
        w_refs, m_refs, v_refs, g_refs = refs[:n], refs[n:2 * n], refs[2 * n:3 * n], refs[3 * n:4 * n]
        d_refs, mo_refs, vo_refs = refs[4 * n:5 * n], refs[5 * n:6 * n], refs[6 * n:7 * n]
        for i in range(n):
            delta, m_new, v_new = _adam_math(w_refs[i][...], g_refs[i][...], m_refs[i][...], v_refs[i][...])
            d_refs[i][...] = delta
            mo_refs[i][...] = m_new
            vo_refs[i][...] = v_new

    shapes = [jax.ShapeDtypeStruct(w.shape, F32) for w in ws]
    res = pl.pallas_call(body, name="adam_small", out_shape=shapes * 3)(*ws, *ms, *vs, *gs)
    return res[:n], res[n:2 * n], res[2 * n:]


def _pack_small(b_in, w_dw, b_dw, ln_g, ln_b, b_out, place):
    cin = b_in.shape[1]
    cd = b_dw.shape[1]
    rows = 8 + CONV_PAD

    def body(pl_ref, bi, wd, bd, lg, lb, bo, out):
        out[...] = jnp.zeros_like(out)
        out[0:1, :] = bi[...]
        out[1:2, 0:cd] = bd[...]
        out[1:2, cd:2 * cd] = lg[...]
        out[2:3, 0:cd] = lb[...]
        out[2:3, cd:2 * cd] = bo[...]
        out[8:8 + CONV_WIDTH, 0:cd] = wd[...]

    def whole(arr):
        return pl.BlockSpec(arr.shape, lambda i, p: (0,) * arr.ndim)

    ins = [b_in, w_dw, b_dw, ln_g, ln_b, b_out]
    return pl.pallas_call(
        body, name="pack_small",
        grid_spec=pltpu.PrefetchScalarGridSpec(
            num_scalar_prefetch=1, grid=(1,), in_specs=[whole(a) for a in ins],
            out_specs=pl.BlockSpec((None, rows, cin), lambda i, p: (p[1], 0, 0))),
        out_shape=jax.ShapeDtypeStruct((N_SHARD, rows, cin), F32),
        compiler_params=_params("arbitrary"),
    )(place, *ins)


def _place():
    x, y, c = lax.axis_index("x"), lax.axis_index("y"), lax.axis_index("c")
    return x, y, c


def _other_chips(x, y):
    return [(1 - x, y), (x, 1 - y), (1 - x, 1 - y)]


def _split_start_many(name, parts, after=None):
    flat = [b for bufs, _, _ in parts for b in bufs]
    n, n_parts = len(flat), len(parts)
    deps = [] if after is None else [after]

    def body(*refs):
        out0 = n + len(deps)
        pos = 0
        for i, (bufs, _, copies) in enumerate(parts):
            for cp in copies(refs[pos:pos + len(bufs)], refs[out0 + 2 * i], refs[out0 + 2 * i + 1], False):
                cp.start()
            pos += len(bufs)
        refs[-1][...] = jnp.zeros_like(refs[-1])

    sems = [pltpu.SemaphoreType.DMA((n_sem,)) for _, n_sem, _ in parts for _ in range(2)]
    res = pl.pallas_call(
        body, name=name,
        out_shape=(*sems, *[pltpu.HBM(b.shape, b.dtype) for b in flat], jax.ShapeDtypeStruct((8, LANES), F32)),
        in_specs=[HBM] * n + [ANY] * len(deps),
        out_specs=(*[SEM] * (2 * n_parts), *[HBM] * n, pl.BlockSpec(memory_space=pltpu.VMEM)),
        input_output_aliases={i: 2 * n_parts + i for i in range(n)},
        compiler_params=pltpu.CompilerParams(has_side_effects=SPLIT_EFFECT),
    )(*[pltpu.with_memory_space_constraint(b, pltpu.HBM) for b in flat], *deps)
    handles, pos = [], 2 * n_parts
    for i, (bufs, _, _) in enumerate(parts):
        handles.append((res[2 * i], res[2 * i + 1], list(res[pos:pos + len(bufs)]), res[-1]))
        pos += len(bufs)
    return handles


def _split_start(name, bufs, n_sem, copies, after=None):
    return _split_start_many(name, [(bufs, n_sem, copies)], after)[0]


def _split_wait(name, handle, copies, after):
    ssem, rsem, bufs, _ = handle
    n = len(bufs)
    deps = list(after) if isinstance(after, (list, tuple)) else [after]

    def body(*refs):
        for cp in copies(refs[:n], refs[n], refs[n + 1], True):
            cp.wait_send()
            cp.wait_recv()

    res = pl.pallas_call(
        body, name=name,
        out_shape=[pltpu.HBM(b.shape, b.dtype) for b in bufs],
        in_specs=[HBM] * n + [SEM, SEM] + [ANY] * len(deps), out_specs=[HBM] * n,
        input_output_aliases={i: i for i in range(n)},
        compiler_params=pltpu.CompilerParams(has_side_effects=SPLIT_EFFECT),
    )(*bufs, ssem, rsem, *deps)
    return list(res)


def _remote(src, dst, ssem, rsem, k, to):
    return pltpu.make_async_remote_copy(src_ref=src, dst_ref=dst, send_sem=ssem.at[k], recv_sem=rsem.at[k],
                                        device_id=to, device_id_type=MESH)


def _gather_chips(x, y, c):
    nx, ny = x + (1 - c) - 2 * x * (1 - c), y + c - 2 * y * c
    fx, fy = x + c - 2 * x * c, y + (1 - c) - 2 * y * (1 - c)
    return (nx, ny), (fx, fy), 2 * nx + ny, 2 * fx + fy, 2 * (1 - x) + (1 - y)


def _direct_copies(refs, ssem, rsem, landing, n_whole=0):
    x, y, c = _place()
    me = 2 * x + y
    (nx, ny), _, near, _, _ = _gather_chips(x, y, c)
    n = len(refs) - n_whole
    cps = []
    for a, ref in enumerate(refs[:n]):
        cps.append(_remote(ref.at[me], ref.at[near if landing else me], ssem, rsem, a, (nx, ny, c)))
    for b, ref in enumerate(refs[n:]):
        for j, (px, py) in enumerate(_other_chips(x, y)):
            cps.append(_remote(ref.at[me], ref.at[2 * px + py if landing else me], ssem, rsem, n + 3 * b + j,
                               (px, py, c)))
    return cps


def _relay_copies(refs, ssem, rsem, landing):
    x, y, c = _place()
    _, (fx, fy), near, far, diag = _gather_chips(x, y, c)
    n = len(refs)
    cps = []
    for a, ref in enumerate(refs):
        rh = ref.shape[1] // 2
        rows = pl.ds(c * rh, rh)
        cps.append(_remote(ref.at[near, rows], ref.at[diag if landing else near, rows], ssem, rsem, a, (fx, fy, c)))
        cps.append(_remote(ref.at[near], ref.at[far if landing else near], ssem, rsem, n + a, (x, y, 1 - c)))
    return cps


def _diagonal_copies(refs, ssem, rsem, landing):
    x, y, c = _place()
    diag = 2 * (1 - x) + (1 - y)
    who = 1 - c if landing else c
    cps = []
    for a, ref in enumerate(refs):
        rh = ref.shape[1] // 2
        piece = ref.at[diag, pl.ds(who * rh, rh)]
        cps.append(_remote(piece, piece, ssem, rsem, a, (x, y, 1 - c)))
    return cps


def _sibling_copies(refs, ssem, rsem, landing):
    x, y, c = _place()
    n = len(refs) // 2
    cps = []
    for a in range(n):
        rh = refs[a].shape[1] // 2
        cps.append(_remote(refs[a].at[:, pl.ds((1 - c) * rh, rh), :], refs[n + a], ssem, rsem, a, (x, y, 1 - c)))
    return cps


def _owner_copies(refs, ssem, rsem, landing):
    x, y, c = _place()
    n = len(refs) // 2
    cps = []
    for a in range(n):
        for j, (px, py) in enumerate(_other_chips(x, y)):
            cps.append(_remote(refs[a].at[2 * px + py], refs[n + a].at[j], ssem, rsem, 3 * a + j, (px, py, c)))
    return cps


def _swap_copies(refs, ssem, rsem, landing):
    x, y, c = _place()
    who = 1 - c if landing else c
    cps = []
    for a, ref in enumerate(refs):
        rh = ref.shape[0] // 2
        rows = ref.at[pl.ds(who * rh, rh)]
        cps.append(_remote(rows, rows, ssem, rsem, a, (x, y, 1 - c)))
    return cps


def _small_copies(refs, ssem, rsem, landing):
    pack, slots = refs
    x, y, c = _place()
    cps = []
    for rel in range(1, N_DEV):
        px = 1 - x if (rel >> 2) & 1 else x
        py = 1 - y if (rel >> 1) & 1 else y
        pc = 1 - c if rel & 1 else c
        slot = 4 * px + 2 * py + pc if landing else 4 * x + 2 * y + c
        cps.append(_remote(pack, slots.at[slot], ssem, rsem, rel - 1, (px, py, pc)))
    return cps


def _small_pack(rows, w_dw_grad, d):
    n = len(rows)

    def body(*refs):
        pack = refs[-1]
        pack[...] = jnp.zeros_like(pack)
        for (r, _), ref in zip(rows, refs[:n]):
            pack[r:r + 1, :] = ref[...]
        pack[16:16 + CONV_PAD, :] = refs[n][...]

    return pl.pallas_call(body, name="small_pack", out_shape=jax.ShapeDtypeStruct((SMALL_ROWS, d), F32))(
        *[v for _, v in rows], w_dw_grad)


def _small_sum(pack, slots, place):
    rows, d = pack.shape
    loss_row = 12

    def body(pl_ref, pack_ref, slots_ref, out_ref):
        me = pl_ref[2]
        tot = jnp.where(me == 0, pack_ref[...], slots_ref[0])
        for i in range(1, N_DEV):
            tot = tot + jnp.where(me == i, pack_ref[...], slots_ref[i])
        out_ref[...] = tot
        out_ref[loss_row:loss_row + 1, :] = jnp.zeros((1, d), F32) + jnp.sum(tot[loss_row:loss_row + 1, :])

    return pl.pallas_call(
        body, name="small_sum",
        grid_spec=pltpu.PrefetchScalarGridSpec(
            num_scalar_prefetch=1, grid=(1,),
            in_specs=[pl.BlockSpec((rows, d), lambda i, p: (0, 0)), pl.BlockSpec((N_DEV, rows, d), lambda i, p: (0, 0, 0))],
            out_specs=pl.BlockSpec((rows, d), lambda i, p: (0, 0))),
        out_shape=jax.ShapeDtypeStruct((rows, d), F32),
        compiler_params=_params("arbitrary"),
    )(place, pack, slots)


def kernel(x, norm_mix, norm_mlp, conv_w_in, conv_b_in, conv_w_dw, conv_b_dw, conv_ln_g, conv_ln_b, conv_w_out, conv_b_out, kv_norm, w_kv, attn_w_q, attn_w_o, mlp_w_in, mlp_w_out, final_norm, loss_target, m_norm_mix, m_norm_mlp, m_conv_w_in, m_conv_b_in, m_conv_w_dw, m_conv_b_dw, m_conv_ln_g, m_conv_ln_b, m_conv_w_out, m_conv_b_out, m_kv_norm, m_w_kv, m_attn_w_q, m_attn_w_o, m_mlp_w_in, m_mlp_w_out, m_final_norm, v_norm_mix, v_norm_mlp, v_conv_w_in, v_conv_b_in, v_conv_w_dw, v_conv_b_dw, v_conv_ln_g, v_conv_ln_b, v_conv_w_out, v_conv_b_out, v_kv_norm, v_w_kv, v_attn_w_q, v_attn_w_o, v_mlp_w_in, v_mlp_w_out, v_final_norm):
    _, s, d = x.shape
    dff = mlp_w_in.shape[2] * N_SHARD
    kvw = w_kv.shape[1]
    nh = d // HEAD_DIM
    group = nh // N_KV_HEADS
    ds4 = d // N_SHARD
    xi, yi, ci = _place()
    me = 2 * xi + yi
    place = jnp.stack([ci, me, 2 * me + ci]).astype(I32)

    h0 = x.reshape(s, d)
    target = loss_target.reshape(s, d)
    tabs = _rope_tables(s)

    def gather_begin(tag, bufs, n_whole=0):
        plan = functools.partial(_direct_copies, n_whole=n_whole)
        return _split_start(f"gather_start_{tag}", bufs, len(bufs) + 2 * n_whole, plan), plan, n_whole

    def gather_step(later, land=None, swap=None):
        parts, names, whole = [], [], {}
        if land is not None:
            tag, (handle, plan, n_whole) = land
            bufs = _split_wait(f"gather_wait_{tag}", handle, plan, later)
            n = len(bufs) - n_whole
            parts.append((bufs[:n], 2 * n, _relay_copies))
            whole["land"] = bufs[n:]
            names.append(f"relay_{tag}")
        if swap is not None:
            tag, (relayed, whole["swap"]) = swap
            bufs = _split_wait(f"relay_wait_{tag}", relayed, _relay_copies, later)
            parts.append((bufs, len(bufs), _diagonal_copies))
            names.append(f"diagonal_{tag}")
        handles = _split_start_many("start_" + "_".join(names), parts)
        landed = (handles[0], whole["land"]) if land is not None else None
        swapped = (handles[-1], whole["swap"]) if swap is not None else None
        return landed, swapped

    def gather_land(tag, begun, later):
        return gather_step(later, land=(tag, begun))[0]

    def gather_swap(tag, landed, later):
        return gather_step(later, swap=(tag, landed))[1]

    def gather_end(tag, swapped, later):
        handle, whole = swapped
        return _split_wait(f"diagonal_wait_{tag}", handle, _diagonal_copies, later) + whole

    ag_cin = gather_begin("conv_in", [
        _cast_bf16("cast_w_in", conv_w_in, 0, place),
        _pack_small(conv_b_in, conv_w_dw.reshape(CONV_WIDTH, ds4), conv_b_dw, conv_ln_g, conv_ln_b, conv_b_out, place),
    ], n_whole=1)
    ag_cout = gather_begin("conv_out", [_cast_bf16("cast_w_out", conv_w_out, 0, place, ag_cin[0][3])])
    ag_mi0 = gather_begin("mlp_in0", [_cast_bf16("cast_mlp_in0", mlp_w_in, 0, place, ag_cout[0][3])])
    ag_mo0 = gather_begin("mlp_out0", [_cast_bf16("cast_mlp_out0", mlp_w_out, 0, place, ag_mi0[0][3])])
    nm = [norm_mix[0:1], norm_mix[1:2]]
    nmlp = [norm_mlp[0:1], norm_mlp[1:2]]
    kvn = kv_norm.reshape(1, d)
    fin = final_norm.reshape(1, d)
    (y0,) = _rms_fwd("rms_mix0", h0, [nm[0]], after=ag_mo0[0][3])
    land_cin = gather_land("conv_in", ag_cin, y0)
    ag_attn = gather_begin("attn", [
        _cast_bf16("cast_w_kv", w_kv.reshape(1, ds4, kvw), 0, place, land_cin[0][3]),
        _cast_bf16("cast_w_q", attn_w_q, 0, place), _cast_bf16("cast_w_o", attn_w_o, 0, place)])
    ag_mi1 = gather_begin("mlp_in1", [_cast_bf16("cast_mlp_in1", mlp_w_in, 1, place, ag_attn[0][3])])
    ag_mo1 = gather_begin("mlp_out1", [_cast_bf16("cast_mlp_out1", mlp_w_out, 1, place, ag_mi1[0][3])])
    land_cout, swap_cin = gather_step(ag_mo1[0][3], land=("conv_out", ag_cout), swap=("conv_in", land_cin))

    wmi_g = [None, None]
    wmo_f = [None, None]

    w_in_g, small_g = gather_end("conv_in", swap_cin, swap_cin[0][3])
    b_in_f = small_g[:, 0, :].reshape(1, 2 * d)
    b_dw_f = small_g[:, 1, 0:ds4].reshape(1, d)
    ln_g_f = small_g[:, 1, ds4:2 * ds4].reshape(1, d)
    ln_b_f = small_g[:, 2, 0:ds4].reshape(1, d)
    b_out_f = small_g[:, 2, ds4:2 * ds4].reshape(1, d)
    w_dw_f = jnp.transpose(small_g[:, 8:8 + CONV_PAD, 0:ds4], (1, 0, 2)).reshape(CONV_PAD, d)

    def ep_bias(acc, ex, outs, j):
        outs[0][...] = (acc + ex[0][...]).astype(outs[0].dtype)

    def ep_residual(acc, ex, outs, j):
        outs[0][...] = ex[0][...] + acc

    def ep_residual_bias(acc, ex, outs, j):
        outs[0][...] = ex[0][...] + (acc + ex[1][...])

    def ep_relu2(acc, ex, outs, j):
        r = jnp.maximum(acc, 0.0)
        outs[0][...] = r.astype(BF16)
        outs[1][...] = (r * r).astype(BF16)

    by_residue = [(BF16, ("residues", dil)) for dil in DILATIONS]

    def put_by_residue(val, outs, stage):
        _to_residues(val, stage, outs, DILATIONS)

    def ep_rope(acc, ex, outs, j, stage):
        put_by_residue(_rope_apply(acc, ex[0][...], ex[1][...], ex[2][...], 1.0), outs, stage)

    def ep_rope_k(acc, ex, outs, j, stage):
        roped = _rope_apply(acc, ex[0][...], ex[1][...], ex[2][...], 1.0)
        put_by_residue(jnp.where(j == 0, roped, acc), outs, stage)

    def ep_by_residue(acc, ex, outs, j, stage):
        put_by_residue(acc, outs, stage)

    tab_extras = [(t, "rows") for t in tabs]

    def mlp_fwd(idx, h, y, out_weight):
        r, r2 = _matmul(f"mlp_in{idx}", "nn", y, wmi_g[idx], b_kind="col", m=s, n=dff, k=d,
                        outs=[(BF16, "plain"), (BF16, "plain")], epilogue=ep_relu2)
        wmo_f[idx] = out_weight(r2).reshape(dff, d)
        (h_new,) = _matmul(f"mlp_out{idx}", "nn", r2, wmo_f[idx], m=s, n=d, k=dff,
                           outs=[(F32, "plain")], extras=[(h, "ij")], epilogue=ep_residual)
        return h_new, r, r2

    (u,) = _matmul("conv_in", "nn", y0, w_in_g, b_kind="col", m=s, n=2 * d, k=d,
                   outs=[(BF16, "plain")], extras=[(b_in_f, "vec")], epilogue=ep_bias)
    cpre = _dwconv_fwd(u, w_dw_f, b_dw_f)
    land_mi0, swap_cout = gather_step(cpre, land=("mlp_in0", ag_mi0), swap=("conv_out", land_cout))
    sact = _ln_silu_fwd(cpre, ln_g_f, ln_b_f)
    (w_out_g,) = gather_end("conv_out", swap_cout, sact)
    w_out_f = w_out_g.reshape(d, d)
    (h1,) = _matmul("conv_out", "nn", sact, w_out_f, m=s, n=d, k=d,
                    outs=[(F32, "plain")], extras=[(h0, "ij"), (b_out_f, "vec")], epilogue=ep_residual_bias)
    swap_mi0 = gather_swap("mlp_in0", land_mi0, h1)
    (y1,) = _rms_fwd("rms_mlp0", h1, [nmlp[0]], after=swap_mi0[0][3])
    land_mo0 = gather_land("mlp_out0", ag_mo0, y1)
    (wmi_g[0],) = gather_end("mlp_in0", swap_mi0, land_mo0[0][3])
    land_attn = None

    def out_weight0(r2):
        nonlocal land_attn
        land_attn, swap_mo0 = gather_step(r2, land=("attn", ag_attn), swap=("mlp_out0", land_mo0))
        return gather_end("mlp_out0", swap_mo0, swap_mo0[0][3])[0]

    h2, r0, r0sq = mlp_fwd(0, h1, y1, out_weight0)
    land_mi1, swap_attn = gather_step(h2, land=("mlp_in1", ag_mi1), swap=("attn", land_attn))
    ykv, y2 = _rms_fwd("rms_kv_mix1", h2, [kvn, nm[1]], after=land_mi1[0][3])
    wkv_g, wq_g, wo_g = gather_end("attn", swap_attn, y2)
    wkv_f, wq_f, wo_f = wkv_g.reshape(d, kvw), wq_g.reshape(d, d), wo_g.reshape(d, d)
    kv_parts = _matmul("kv_proj", "nn", ykv, wkv_f, m=s, n=kvw, k=d, tn=kvw // 2,
                       outs=by_residue, extras=tab_extras, epilogue=ep_rope_k, stage=True)
    q_parts = _matmul("q_proj", "nn", y2, wq_f, m=s, n=d, k=d,
                      outs=by_residue, extras=tab_extras, epilogue=ep_rope, stage=True)
    o_parts, lse_parts = [], []
    for dil, q_b, kv_b in zip(DILATIONS, q_parts, kv_parts):
        o_b, lse_b = _attn_fwd(f"attn_fwd_d{dil}", q_b, kv_b)
        o_parts.append(o_b)
        lse_parts.append(lse_b)
    o, lse = _attn_combine(o_parts, lse_parts)
    land_mo1, swap_mi1 = gather_step(o, land=("mlp_out1", ag_mo1), swap=("mlp_in1", land_mi1))
    (h3,) = _matmul("attn_out", "nn", o, wo_f, m=s, n=d, k=d,
                    outs=[(F32, "plain")], extras=[(h2, "ij")], epilogue=ep_residual)
    (y3,) = _rms_fwd("rms_mlp1", h3, [nmlp[1]], after=land_mo1[0][3])
    (wmi_g[1],) = gather_end("mlp_in1", swap_mi1, y3)

    def out_weight1(r2):
        swap_mo1 = gather_swap("mlp_out1", land_mo1, r2)
        return gather_end("mlp_out1", swap_mo1, swap_mo1[0][3])[0]

    h4, r1, r1sq = mlp_fwd(1, h3, y3, out_weight1)
    dh4, dh4b, d_fin, loss_cols = _final_loss(h4, fin, target)

    def ep_relu2_bwd(acc, ex, outs, j):
        outs[0][...] = (acc * (2.0 * ex[0][...].astype(F32))).astype(BF16)

    def mlp_bwd(idx, dhb, y, r, r2):
        (dz,) = _matmul(f"mlp_out{idx}_dx", "nt", dhb, wmo_f[idx], m=s, n=dff, k=d,
                        outs=[(BF16, "plain")], extras=[(r, "ij")], epilogue=ep_relu2_bwd)
        (dwo,) = _matmul(f"mlp_out{idx}_dw", "tn", r2, dhb, m=dff, n=d, k=s,
                         outs=[(BF16, "plain")])
        (dy,) = _matmul(f"mlp_in{idx}_dx", "nt", dz, wmi_g[idx], b_kind="col", m=s, n=d, k=dff,
                        outs=[(BF16, "plain")])
        (dwi,) = _matmul(f"mlp_in{idx}_dw", "tn", y, dz, m=d, n=dff, k=s,
                         outs=[(BF16, "col")])
        return dy, dwi, dwo.reshape(N_SHARD, dff // N_SHARD, d)

    def rs_exchange(tag, grads):
        lands = [lax.empty((N_SHARD, g.shape[1] // 2, g.shape[2]), g.dtype) for g in grads]
        return _split_start(f"sibling_start_{tag}", list(grads) + lands, len(grads), _sibling_copies)

    def rs_send(tag, names, exchanged, later):
        bufs = _split_wait(f"sibling_wait_{tag}", exchanged, _sibling_copies, later)
        n = len(names)
        sums = [_chip_sum(f"chip_sum_{nme}", g, rh, place) for nme, g, rh in zip(names, bufs[:n], bufs[n:])]
        lands = [lax.empty((N_SHARD - 1,) + cs.shape[1:], cs.dtype) for cs in sums]
        return _split_start(f"owners_start_{tag}", sums + lands, 3 * n, _owner_copies)

    def rs_sum(tag, names, sent, later):
        bufs = _split_wait(f"owners_wait_{tag}", sent, _owner_copies, later)
        n = len(names)
        own = [_owner_sum(f"owner_sum_{nme}", cs, rp, place) for nme, cs, rp in zip(names, bufs[:n], bufs[n:])]
        return _split_start(f"swap_start_{tag}", own, n, _swap_copies)

    def rs_end(tag, swapped, later):
        return _split_wait(f"swap_wait_{tag}", swapped, _swap_copies, later)

    dy3, g_wmi1, g_wmo1 = mlp_bwd(1, dh4b, y3, r1, r1sq)
    x_mlp1 = rs_exchange("mlp1", [g_wmi1, g_wmo1])
    dh3, dh3b, d_nmlp1 = _rms_bwd("rms_mlp1_bwd", h3, [(nmlp[1], dy3)], dh4, after=x_mlp1[3])

    do_parts = _matmul("attn_out_dx", "nt", dh3b, wo_f, m=s, n=d, k=d, outs=by_residue, epilogue=ep_by_residue,
                       stage=True)
    (g_wo,) = _matmul("attn_out_dw", "tn", o, dh3b, m=d, n=d, k=s, outs=[(BF16, "plain")])
    rs_mlp1 = rs_send("mlp1", ["mlp_in1", "mlp_out1"], x_mlp1, g_wo)
    lse_res, delta_res = _attn_delta(do_parts[0].reshape(s, d), o, lse, DILATIONS)
    dq_parts, dk_parts, dv_parts = [], [], []
    for dil, q_b, kv_b, do_b, lse_b, dl_b in zip(DILATIONS, q_parts, kv_parts, do_parts, lse_res, delta_res):
        dq_b, dk_b, dv_b = _attn_bwd(f"attn_bwd_d{dil}", q_b, kv_b, do_b, lse_b, dl_b)
        dq_parts.append(dq_b)
        dk_parts.append(dk_b)
        dv_parts.append(dv_b)
    dq = _residue_sum("rope_bwd_q", [(dq_parts, True)], tabs)
    dkv = _residue_sum("rope_bwd_kv", [(dk_parts, True), (dv_parts, False)], tabs)
    (g_wq,) = _matmul("q_proj_dw", "tn", y2, dq, m=d, n=d, k=s, outs=[(BF16, "plain")])
    (dy2,) = _matmul("q_proj_dx", "nt", dq, wq_f, m=s, n=d, k=d, outs=[(BF16, "plain")])
    (g_wkv,) = _matmul("kv_proj_dw", "tn", ykv, dkv, m=d, n=kvw, k=s, outs=[(BF16, "plain")])
    (dykv,) = _matmul("kv_proj_dx", "nt", dkv, wkv_f, m=s, n=d, k=kvw, outs=[(BF16, "plain")])
    x_attn = rs_exchange("attn", [g_wkv.reshape(N_SHARD, ds4, kvw), g_wq.reshape(N_SHARD, ds4, d),
                                  g_wo.reshape(N_SHARD, ds4, d)])
    dh2, dh2b, d_nm1, d_kvn = _rms_bwd("rms_kv_mix1_bwd", h2, [(nm[1], dy2), (kvn, dykv)], dh3, after=x_attn[3])
    rs_attn = rs_send("attn", ["w_kv", "w_q", "w_o"], x_attn, dh2b)

    dy1, g_wmi0, g_wmo0 = mlp_bwd(0, dh2b, y1, r0, r0sq)
    x_mlp0 = rs_exchange("mlp0", [g_wmi0, g_wmo0])
    dh1, dh1b, d_nmlp0, d_b_out = _rms_bwd("rms_mlp0_bwd", h1, [(nmlp[0], dy1)], dh2, want_colsum=True,
                                           after=[x_mlp0[3], rs_attn[3]])

    (dsact,) = _matmul("conv_out_dx", "nt", dh1b, w_out_f, m=s, n=d, k=d, outs=[(BF16, "plain")])
    (g_wout,) = _matmul("conv_out_dw", "tn", sact, dh1b, m=d, n=d, k=s, outs=[(BF16, "plain")])
    rs_mlp0 = rs_send("mlp0", ["mlp_in0", "mlp_out0"], x_mlp0, g_wout)
    dc, d_ln_g, d_ln_b, d_b_dw = _ln_silu_bwd(cpre, ln_g_f, ln_b_f, dsact, after=rs_mlp0[3])
    du, d_w_dw, d_b_in_a, d_b_in_g = _dwconv_bwd(u, w_dw_f, dc)
    (g_win,) = _matmul("conv_in_dw", "tn", y0, du, b_kind="col", m=d, n=2 * d, k=s, outs=[(BF16, "col")])
    x_conv = rs_exchange("conv", [g_win, g_wout.reshape(N_SHARD, ds4, d)])
    (dy0,) = _matmul("conv_in_dx", "nt", du, w_in_g, a_kind="col", b_kind="col", m=s, n=d, k=2 * d,
                     outs=[(BF16, "plain")], after=x_conv[3])
    rs_conv = rs_send("conv", ["w_in", "w_out"], x_conv, dy0)
    dx, _, d_nm0 = _rms_bwd("rms_mix0_bwd", h0, [(nm[0], dy0)], dh1, after=rs_conv[3])

    small_rows = [(0, d_nm0), (1, d_nm1), (2, d_nmlp0), (3, d_nmlp1), (4, d_kvn), (5, d_fin), (6, d_b_dw),
                  (7, d_ln_g), (8, d_ln_b), (9, d_b_out), (10, d_b_in_a), (11, d_b_in_g), (12, loss_cols)]
    x_small = _split_start("small_start", [_small_pack(small_rows, d_w_dw, d),
                                           lax.empty((N_DEV, SMALL_ROWS, d), F32)], N_DEV - 1, _small_copies)

    def big(name, w, m, v, g, layer=0, partial=None):
        shape = w.shape
        w3, m3, v3 = [t.reshape((-1,) + shape[-2:]) for t in (w, m, v)]
        if partial is not None:
            partial = [t.reshape(w3.shape) for t in partial]
        res = _adamw(name, w3, m3, v3, g, layer, partial)
        return [t.reshape(shape) for t in res]

    sw_mlp1 = rs_sum("mlp1", ["mlp_in1", "mlp_out1"], rs_mlp1, x_small[3])
    sw_attn = rs_sum("attn", ["w_kv", "w_q", "w_o"], rs_attn, sw_mlp1[3])
    f_wmi1, f_wmo1 = rs_end("mlp1", sw_mlp1, sw_attn[3])
    p_wmi = big("adam_mlp_in1", mlp_w_in, m_mlp_w_in, v_mlp_w_in, f_wmi1, 1)
    p_wmo = big("adam_mlp_out1", mlp_w_out, m_mlp_w_out, v_mlp_w_out, f_wmo1, 1)
    sw_mlp0 = rs_sum("mlp0", ["mlp_in0", "mlp_out0"], rs_mlp0, [p_wmi[0], p_wmo[0]])
    f_wkv, f_wq, f_wo = rs_end("attn", sw_attn, sw_mlp0[3])
    r_wkv = big("adam_w_kv", w_kv, m_w_kv, v_w_kv, f_wkv)
    r_wq = big("adam_w_q", attn_w_q, m_attn_w_q, v_attn_w_q, f_wq)
    r_wo = big("adam_w_o", attn_w_o, m_attn_w_o, v_attn_w_o, f_wo)
    sw_conv = rs_sum("conv", ["w_in", "w_out"], rs_conv, [r_wkv[0], r_wq[0], r_wo[0]])
    f_wmi0, f_wmo0 = rs_end("mlp0", sw_mlp0, sw_conv[3])
    r_wmi = big("adam_mlp_in0", mlp_w_in, m_mlp_w_in, v_mlp_w_in, f_wmi0, 0, p_wmi)
    r_wmo = big("adam_mlp_out0", mlp_w_out, m_mlp_w_out, v_mlp_w_out, f_wmo0, 0, p_wmo)
    f_win, f_wout = rs_end("conv", sw_conv, [r_wmi[0], r_wmo[0]])
    r_win = big("adam_w_in", conv_w_in, m_conv_w_in, v_conv_w_in, f_win)
    r_wout = big("adam_w_out", conv_w_out, m_conv_w_out, v_conv_w_out, f_wout)

    small_pack, small_slots = _split_wait("small_wait", x_small, _small_copies, r_wout[0])
    red = _small_sum(small_pack, small_slots, place)
    loss = red[12, 0]
    g_norm_mix = red[0:2]
    g_norm_mlp = red[2:4]
    g_kv_norm = red[4:5]
    g_final = red[5:6]

    def my_cols(row):
        return lax.dynamic_slice(red, (row, me * ds4), (1, ds4))

    g_b_dw, g_ln_g, g_ln_b, g_b_out = my_cols(6), my_cols(7), my_cols(8), my_cols(9)
    half_in = 2 * d // N_SHARD
    b_in_row = 10 + me // 2
    g_b_in = lax.dynamic_slice(red, (b_in_row, (me % 2) * half_in), (1, half_in))
    g_w_dw = lax.dynamic_slice(red, (16, me * ds4), (CONV_WIDTH, ds4))

    sm_w =[norm_mix, norm_mlp, conv_b_in, conv_w_dw.reshape(CONV_WIDTH, ds4), conv_b_dw, conv_ln_g, conv_ln_b,
            conv_b_out, kv_norm.reshape(1, d), final_norm.reshape(1, d)]
    sm_m = [m_norm_mix, m_norm_mlp, m_conv_b_in, m_conv_w_dw.reshape(CONV_WIDTH, ds4), m_conv_b_dw, m_conv_ln_g,
            m_conv_ln_b, m_conv_b_out, m_kv_norm.reshape(1, d), m_final_norm.reshape(1, d)]
    sm_v = [v_norm_mix, v_norm_mlp, v_conv_b_in, v_conv_w_dw.reshape(CONV_WIDTH, ds4), v_conv_b_dw, v_conv_ln_g,
            v_conv_ln_b, v_conv_b_out, v_kv_norm.reshape(1, d), v_final_norm.reshape(1, d)]
    sm_g = [g_norm_mix, g_norm_mlp, g_b_in, g_w_dw, g_b_dw, g_ln_g, g_ln_b, g_b_out, g_kv_norm, g_final]
    sm_d, sm_nm, sm_nv = _adam_small(sm_w, sm_m, sm_v, sm_g)
    shapes = [norm_mix.shape, norm_mlp.shape, conv_b_in.shape, conv_w_dw.shape, conv_b_dw.shape, conv_ln_g.shape,
              conv_ln_b.shape, conv_b_out.shape, kv_norm.shape, final_norm.shape]
    sm_g, sm_d, sm_nm, sm_nv = [[t.reshape(sh) for t, sh in zip(lst, shapes)] for lst in (sm_g, sm_d, sm_nm, sm_nv)]

    def order(sm, idx):
        return [sm[0], sm[1], r_win[idx], sm[2], sm[3], sm[4], sm[5], sm[6], r_wout[idx], sm[7], sm[8],
                r_wkv[idx], r_wq[idx], r_wo[idx], r_wmi[idx], r_wmo[idx], sm[9]]

    return (loss, dx.reshape(x.shape), *order(sm_g, 0), *order(sm_d, 1), *order(sm_nm, 2), *order(sm_nv, 3))
```

```python
import functools
import math

import jax
import jax.numpy as jnp
from jax import lax
from jax.experimental import pallas as pl
from jax.experimental.pallas import tpu as pltpu

F32 = jnp.float32
BF16 = jnp.bfloat16
I32 = jnp.int32

NORM_EPS = 1e-6
LN_EPS = 1e-5
HEAD_DIM = 128
N_KV_HEADS = 4
ROT_DIM = 32
ROPE_THETA = 500000.0
CONV_WIDTH = 31
CONV_PAD = 32
ATT_BLOCK = 128
ATT_STEP_BLOCKS = 16
DILATIONS = (1, 4, 16)
ADAM_LR = 0.001
ADAM_B1 = 0.9
ADAM_B2 = 0.999
ADAM_EPS = 1e-08
ADAM_WD = 0.01
ADAM_STEP = 10
N_SHARD = 4
N_DEV = 8
LANES = 128
VMEM_LIMIT = 48 * 1024 * 1024
MM_TM, MM_TN, MM_TK = 1024, 1024, 2048
ROW_TILE = 512
CONV_CB = 128
CONV_T = 128
SMALL_ROWS = 48
MESH = pl.DeviceIdType.MESH
ANY = pl.BlockSpec(memory_space=pl.ANY)
HBM = pl.BlockSpec(memory_space=pltpu.HBM)
SEM = pl.BlockSpec(memory_space=pltpu.SEMAPHORE)
SPLIT_EFFECT = pltpu.SideEffectType.DATAFLOW_SIDE_EFFECTING


def _params(*sem):
    return pltpu.CompilerParams(dimension_semantics=sem, vmem_limit_bytes=VMEM_LIMIT)


def _sigmoid(x):
    return 1.0 / (1.0 + jnp.exp(-x))


def _wspec(kind, arr_shape, br, bc, pick):
    if kind == "plain":
        return pl.BlockSpec((br, bc), pick)
    per = arr_shape[2] // bc

    def idx(*g):
        rb, cb = pick(*g)
        return (cb // per, rb, cb % per)

    return pl.BlockSpec((None, br, bc), idx)


def _stage_shape(rows, w):
    return (w // LANES, rows, LANES)


def _to_residues(val, stage_ref, out_refs, dils):
    planes, rows, _ = stage_ref.shape
    for c in range(planes):
        stage_ref[c] = val[:, c * LANES:(c + 1) * LANES]
    for out_ref, dil in zip(out_refs, dils):
        if dil == 1:
            out_ref[0] = val.astype(out_ref.dtype)
            continue
        for r in range(dil):
            for c in range(planes):
                out_ref[r, :, c * LANES:(c + 1) * LANES] = stage_ref.at[c][pl.ds(r, rows // dil, stride=dil), :].astype(
                    out_ref.dtype)


def _from_residues(src_ref, stage_ref, dil):
    planes, rows, _ = stage_ref.shape
    if dil == 1:
        return lambda c: src_ref[0, :, c * LANES:(c + 1) * LANES].astype(F32)
    for r in range(dil):
        for c in range(planes):
            stage_ref.at[c][pl.ds(r, rows // dil, stride=dil), :] = src_ref[r, :, c * LANES:(c + 1) * LANES].astype(F32)
    return lambda c: stage_ref[c]


def _matmul(name, mode, a, b, *, m, n, k, tm=MM_TM, tn=MM_TN, tk=MM_TK, a_kind="plain", b_kind="plain", outs,
            extras=(), epilogue=None, stage=False, after=None):
    tm, tn, tk = min(tm, m), min(tn, n), min(tk, k)
    if b_kind == "col" and mode in ("nn", "tn"):
        tn = min(tn, n // b.shape[0])
    if b_kind == "col" and mode == "nt":
        tk = min(tk, k // b.shape[0])
    if a_kind == "col":
        assert mode == "nt"
        tk = min(tk, k // a.shape[0])
    if any(kind == "col" for _, kind in outs):
        tn = min(tn, n // N_SHARD)
    assert m % tm == 0 and n % tn == 0 and k % tk == 0, (name, m, n, k, tm, tn, tk)
    nk = k // tk
    grid = (m // tm, n // tn, nk)
    if mode == "nn":
        a_spec = pl.BlockSpec((tm, tk), lambda i, j, kk: (i, kk))
        b_spec = _wspec(b_kind, b.shape, tk, tn, lambda i, j, kk: (kk, j))
        dims = (((1,), (0,)), ((), ()))
    elif mode == "nt":
        a_spec = _wspec(a_kind, a.shape, tm, tk, lambda i, j, kk: (i, kk))
        b_spec = _wspec(b_kind, b.shape, tn, tk, lambda i, j, kk: (j, kk))
        dims = (((1,), (1,)), ((), ()))
    else:
        a_spec = pl.BlockSpec((tk, tm), lambda i, j, kk: (kk, i))
        b_spec = _wspec(b_kind, b.shape, tk, tn, lambda i, j, kk: (kk, j))
        dims = (((0,), (0,)), ((), ()))
    out_shape, out_specs = [], []
    for dtype, kind in outs:
        if isinstance(kind, tuple):
            dil = kind[1]
            out_shape.append(jax.ShapeDtypeStruct((dil, m // dil, n), dtype))
            out_specs.append(pl.BlockSpec((dil, tm // dil, tn), lambda i, j, kk: (0, i, j)))
            continue
        shape = (m, n) if kind == "plain" else (N_SHARD, m, n // N_SHARD)
        out_shape.append(jax.ShapeDtypeStruct(shape, dtype))
        out_specs.append(_wspec(kind, shape, tm, tn, lambda i, j, kk: (i, j)))
    n_ex = len(extras)
    deps = [] if after is None else [after]
    ex_specs = {"ij": pl.BlockSpec((tm, tn), lambda i, j, kk: (i, j)),
                "vec": pl.BlockSpec((1, tn), lambda i, j, kk: (0, j)),
                "rows": pl.BlockSpec((tm, LANES), lambda i, j, kk: (i, 0))}
    out0 = 2 + n_ex + len(deps)

    def body(*refs):
        a_ref, b_ref = refs[0], refs[1]
        ex_refs = refs[2:2 + n_ex]
        out_refs = refs[out0:out0 + len(outs)]
        j = pl.program_id(1)

        def finish(res):
            if epilogue is None:
                out_refs[0][...] = res.astype(out_refs[0].dtype)
            elif stage:
                epilogue(res, ex_refs, out_refs, j, refs[-1])
            else:
                epilogue(res, ex_refs, out_refs, j)

        prod = lax.dot_general(a_ref[...], b_ref[...], dims, preferred_element_type=F32)
        if nk == 1:
            finish(prod)
            return
        acc_ref = refs[out0 + len(outs)]
        kk = pl.program_id(2)

        @pl.when(kk == 0)
        def _():
            acc_ref[...] = prod

        @pl.when(kk > 0)
        def _():
            acc_ref[...] += prod

        @pl.when(kk == nk - 1)
        def _():
            finish(acc_ref[...])

    res = pl.pallas_call(
        body, name=name, grid=grid,
        in_specs=[a_spec, b_spec] + [ex_specs[how] for _, how in extras] + [ANY] * len(deps),
        out_specs=out_specs, out_shape=out_shape,
        scratch_shapes=[pltpu.VMEM((tm, tn), F32)] * (nk > 1) + [pltpu.VMEM(_stage_shape(tm, tn), F32)] * bool(stage),
        compiler_params=_params("parallel", "parallel", "arbitrary"),
    )(a, b, *[e for e, _ in extras], *deps)
    return res


def _rope_tables(seq):
    half = ROT_DIM // 2
    pos = jnp.arange(seq, dtype=F32)
    inv = ROPE_THETA ** (-jnp.arange(0, ROT_DIM, 2, dtype=F32) / ROT_DIM)
    ang = pos[:, None] * inv[None, :]
    cos, sin = jnp.cos(ang), jnp.sin(ang)
    zeros = jnp.zeros((seq, HEAD_DIM - ROT_DIM), F32)
    ctab = jnp.concatenate([cos, cos, zeros + 1.0], axis=1)
    atab = jnp.concatenate([-sin, jnp.zeros((seq, half), F32), zeros], axis=1)
    btab = jnp.concatenate([jnp.zeros((seq, half), F32), sin, zeros], axis=1)
    return ctab, atab, btab


def _rope_apply(x, ctab, atab, btab, sign):
    w = x.shape[1]
    reps = w // HEAD_DIM
    half = ROT_DIM // 2
    c = jnp.tile(ctab, (1, reps))
    a = jnp.tile(atab, (1, reps))
    b = jnp.tile(btab, (1, reps))
    up = pltpu.roll(x, w - half, 1)
    down = pltpu.roll(x, half, 1)
    return x * c + sign * (up * a + down * b)


def _rows(t, w):
    return pl.BlockSpec((t, w), lambda i: (i, 0))


def _fixed(shape):
    nd = len(shape)
    return pl.BlockSpec(shape, lambda i: (0,) * nd)


def _behind(after):
    deps = [] if after is None else (list(after) if isinstance(after, (list, tuple)) else [after])
    return deps, [ANY] * len(deps)


def _rms_fwd(name, x, gains, after=None):
    s, d = x.shape
    t = min(ROW_TILE, s)
    ng = len(gains)
    deps, dep_specs = _behind(after)

    def body(*all_refs):
        x_ref, refs = all_refs[len(deps)], all_refs[len(deps) + 1:]
        xv = x_ref[...]
        r = lax.rsqrt(jnp.mean(xv * xv, axis=-1, keepdims=True) + NORM_EPS)
        xn = xv * r
        for g_ref, y_ref in zip(refs[:ng], refs[ng:]):
            y_ref[...] = (xn * g_ref[...]).astype(BF16)

    return pl.pallas_call(
        body, name=name, grid=(s // t,),
        in_specs=dep_specs + [_rows(t, d)] + [_fixed((1, d))] * ng,
        out_specs=[_rows(t, d)] * ng,
        out_shape=[jax.ShapeDtypeStruct((s, d), BF16)] * ng,
        compiler_params=_params("parallel"),
    )(*deps, x, *gains)


def _rms_bwd(name, x, pairs, dh_in, want_colsum=False, after=None):
    s, d = x.shape
    n_p = len(pairs)
    t = min(ROW_TILE // n_p, s)
    deps, dep_specs = _behind(after)

    def body(*all_refs):
        x_ref, dh_ref, refs = all_refs[len(deps)], all_refs[len(deps) + 1], all_refs[len(deps) + 2:]
        g_refs = refs[:n_p]
        dy_refs = refs[n_p:2 * n_p]
        dh_out, dhb_out = refs[2 * n_p], refs[2 * n_p + 1]
        dg_refs = refs[2 * n_p + 2:2 * n_p + 2 + n_p]
        cs_ref = refs[-1] if want_colsum else None
        i = pl.program_id(0)
        xv = x_ref[...]
        r = lax.rsqrt(jnp.mean(xv * xv, axis=-1, keepdims=True) + NORM_EPS)
        xn = xv * r
        dh = dh_ref[...]
        for g_ref, dy_ref, dg_ref in zip(g_refs, dy_refs, dg_refs):
            dy = dy_ref[...].astype(F32)
            u = dy * g_ref[...]
            dh = dh + r * (u - xn * jnp.mean(u * xn, axis=-1, keepdims=True))
            part = jnp.sum(dy * xn, axis=0, keepdims=True)

            @pl.when(i == 0)
            def _():
                dg_ref[...] = part

            @pl.when(i > 0)
            def _():
                dg_ref[...] += part

        dh_out[...] = dh
        dhb_out[...] = dh.astype(BF16)
        if want_colsum:
            col = jnp.sum(dh, axis=0, keepdims=True)

            @pl.when(i == 0)
            def _():
                cs_ref[...] = col

            @pl.when(i > 0)
            def _():
                cs_ref[...] += col

    n_vec = n_p + (1 if want_colsum else 0)
    return pl.pallas_call(
        body, name=name, grid=(s // t,),
        in_specs=dep_specs + [_rows(t, d), _rows(t, d)] + [_fixed((1, d))] * n_p + [_rows(t, d)] * n_p,
        out_specs=[_rows(t, d), _rows(t, d)] + [_fixed((1, d))] * n_vec,
        out_shape=[jax.ShapeDtypeStruct((s, d), F32), jax.ShapeDtypeStruct((s, d), BF16)]
        + [jax.ShapeDtypeStruct((1, d), F32)] * n_vec,
        compiler_params=_params("arbitrary"),
    )(*deps, x, dh_in, *[g for g, _ in pairs], *[dy for _, dy in pairs])


def _final_loss(x, g, target):
    s, d = x.shape
    t = min(ROW_TILE, s)

    def body(x_ref, g_ref, t_ref, dh_out, dhb_out, dg_ref, loss_ref):
        i = pl.program_id(0)
        xv = x_ref[...]
        gv = g_ref[...]
        r = lax.rsqrt(jnp.mean(xv * xv, axis=-1, keepdims=True) + NORM_EPS)
        xn = xv * r
        diff = xn * gv - t_ref[...]
        dy = diff / d
        u = dy * gv
        dh = r * (u - xn * jnp.mean(u * xn, axis=-1, keepdims=True))
        dh_out[...] = dh
        dhb_out[...] = dh.astype(BF16)
        dg = jnp.sum(dy * xn, axis=0, keepdims=True)
        lc = jnp.sum(0.5 * diff * dy, axis=0, keepdims=True)

        @pl.when(i == 0)
        def _():
            dg_ref[...] = dg
            loss_ref[...] = lc

        @pl.when(i > 0)
        def _():
            dg_ref[...] += dg
            loss_ref[...] += lc

    return pl.pallas_call(
        body, name="final_loss", grid=(s // t,),
        in_specs=[_rows(t, d), _fixed((1, d)), _rows(t, d)],
        out_specs=[_rows(t, d), _rows(t, d), _fixed((1, d)), _fixed((1, d))],
        out_shape=[jax.ShapeDtypeStruct((s, d), F32), jax.ShapeDtypeStruct((s, d), BF16),
                   jax.ShapeDtypeStruct((1, d), F32), jax.ShapeDtypeStruct((1, d), F32)],
        compiler_params=_params("arbitrary"),
    )(x, g, target)


def _ln_silu_fwd(c, g, b):
    s, d = c.shape
    t = min(ROW_TILE, s)

    def body(c_ref, g_ref, b_ref, s_ref):
        cv = c_ref[...]
        mu = jnp.mean(cv, axis=-1, keepdims=True)
        xc = cv - mu
        rs = lax.rsqrt(jnp.mean(xc * xc, axis=-1, keepdims=True) + LN_EPS)
        ln = xc * rs * g_ref[...] + b_ref[...]
        s_ref[...] = (ln * _sigmoid(ln)).astype(BF16)

    return pl.pallas_call(
        body, name="ln_silu_fwd", grid=(s // t,),
        in_specs=[_rows(t, d), _fixed((1, d)), _fixed((1, d))],
        out_specs=_rows(t, d), out_shape=jax.ShapeDtypeStruct((s, d), BF16),
        compiler_params=_params("parallel"),
    )(c, g, b)


def _ln_silu_bwd(c, g, b, ds, after=None):
    s, d = c.shape
    t = min(ROW_TILE, s)
    deps, dep_specs = _behind(after)

    def body(*all_refs):
        c_ref, g_ref, b_ref, ds_ref, dc_ref, dg_ref, db_ref, dbdw_ref = all_refs[len(deps):]
        i = pl.program_id(0)
        cv = c_ref[...]
        gv = g_ref[...]
        mu = jnp.mean(cv, axis=-1, keepdims=True)
        xc = cv - mu
        rs = lax.rsqrt(jnp.mean(xc * xc, axis=-1, keepdims=True) + LN_EPS)
        nrm = xc * rs
        ln = nrm * gv + b_ref[...]
        sig = _sigmoid(ln)
        dln = ds_ref[...].astype(F32) * sig * (1.0 + ln * (1.0 - sig))
        dn = dln * gv
        dc = rs * (dn - jnp.mean(dn, axis=-1, keepdims=True)
                   - nrm * jnp.mean(dn * nrm, axis=-1, keepdims=True))
        dc_ref[...] = dc
        pg = jnp.sum(dln * nrm, axis=0, keepdims=True)
        pb = jnp.sum(dln, axis=0, keepdims=True)
        pc = jnp.sum(dc, axis=0, keepdims=True)

        @pl.when(i == 0)
        def _():
            dg_ref[...] = pg
            db_ref[...] = pb
            dbdw_ref[...] = pc

        @pl.when(i > 0)
        def _():
            dg_ref[...] += pg
            db_ref[...] += pb
            dbdw_ref[...] += pc

    return pl.pallas_call(
        body, name="ln_silu_bwd", grid=(s // t,),
        in_specs=dep_specs + [_rows(t, d), _fixed((1, d)), _fixed((1, d)), _rows(t, d)],
        out_specs=[_rows(t, d)] + [_fixed((1, d))] * 3,
        out_shape=[jax.ShapeDtypeStruct((s, d), F32)] + [jax.ShapeDtypeStruct((1, d), F32)] * 3,
        compiler_params=_params("arbitrary"),
    )(*deps, c, g, b, ds)


def _residue_spec(dil, t, w):
    return pl.BlockSpec((dil, t // dil, w), lambda i: (0, i, 0))


def _attn_combine(o_list, lse_list):
    dil0, sd0, d = o_list[0].shape
    s = dil0 * sd0
    lw = lse_list[0].shape[2]
    group = d // HEAD_DIM // N_KV_HEADS
    t = min(ROW_TILE, s)
    nb = len(o_list)
    dils = [o.shape[0] for o in o_list]

    def body(*refs):
        o_out, l_out = refs[2 * nb], refs[2 * nb + 1]
        o_stage, l_stage = refs[2 * nb + 2:3 * nb + 2], refs[3 * nb + 2:]
        o_planes = [_from_residues(src, stage, dil) for src, stage, dil in zip(refs[:nb], o_stage, dils)]
        l_planes = [_from_residues(src, stage, dil) for src, stage, dil in zip(refs[nb:2 * nb], l_stage, dils)]
        for kh in range(N_KV_HEADS):
            ls = [plane(kh) for plane in l_planes]
            mx = ls[0]
            for l in ls[1:]:
                mx = jnp.maximum(mx, l)
            es = [jnp.exp(l - mx) for l in ls]
            den = es[0]
            for e in es[1:]:
                den = den + e
            l_out[:, kh * LANES:(kh + 1) * LANES] = mx + jnp.log(den)
            ws = [e / den for e in es]
            for g in range(group):
                h = kh * group + g
                acc = jnp.zeros((t, HEAD_DIM), F32)
                for plane, w in zip(o_planes, ws):
                    acc = acc + w[:, g:g + 1] * plane(h)
                o_out[:, h * HEAD_DIM:(h + 1) * HEAD_DIM] = acc.astype(BF16)

    return pl.pallas_call(
        body, name="attn_combine", grid=(s // t,),
        in_specs=[_residue_spec(dil, t, d) for dil in dils] + [_residue_spec(dil, t, lw) for dil in dils],
        out_specs=[_rows(t, d), _rows(t, lw)],
        out_shape=[jax.ShapeDtypeStruct((s, d), BF16), jax.ShapeDtypeStruct((s, lw), F32)],
        scratch_shapes=[pltpu.VMEM(_stage_shape(t, d), F32)] * nb + [pltpu.VMEM(_stage_shape(t, lw), F32)] * nb,
        compiler_params=_params("parallel"),
    )(*o_list, *lse_list)


def _attn_delta(do, o, lse, dils):
    s, d = o.shape
    lw = lse.shape[1]
    group = d // HEAD_DIM // N_KV_HEADS
    t = min(ROW_TILE, s)
    nd = len(dils)

    def body(do_ref, o_ref, lse_ref, *refs):
        stage = refs[-1]
        lane = lax.broadcasted_iota(I32, (t, LANES), 1)
        planes = []
        for kh in range(N_KV_HEADS):
            out = jnp.zeros((t, LANES), F32)
            for g in range(group):
                cols = slice((kh * group + g) * HEAD_DIM, (kh * group + g + 1) * HEAD_DIM)
                v = jnp.sum(do_ref[:, cols].astype(F32) * o_ref[:, cols].astype(F32), axis=-1, keepdims=True)
                out = jnp.where(lane == g, v, out)
            planes.append(out)
        _to_residues(lse_ref[...], stage, refs[:nd], dils)
        _to_residues(jnp.concatenate(planes, axis=1), stage, refs[nd:2 * nd], dils)

    res = pl.pallas_call(
        body, name="attn_delta", grid=(s // t,),
        in_specs=[_rows(t, d), _rows(t, d), _rows(t, lw)],
        out_specs=[_residue_spec(dil, t, lw) for dil in dils] * 2,
        out_shape=[jax.ShapeDtypeStruct((dil, s // dil, lw), F32) for dil in dils] * 2,
        scratch_shapes=[pltpu.VMEM(_stage_shape(t, lw), F32)],
        compiler_params=_params("parallel"),
    )(do, o, lse)
    return res[:nd], res[nd:]


def _residue_sum(name, groups, tabs):
    first = groups[0][0][0]
    s, w = first.shape[0] * first.shape[1], first.shape[2]
    t = min(ROW_TILE, s)
    flat = [p for parts, _ in groups for p in parts]

    def body(*refs):
        c_ref, a_ref, b_ref = refs[len(flat):len(flat) + 3]
        out = refs[len(flat) + 3]
        stages = refs[len(flat) + 4:]
        k = 0
        for gi, (parts, rotate) in enumerate(groups):
            planes = [_from_residues(refs[k + i], stages[k + i], p.shape[0]) for i, p in enumerate(parts)]
            k += len(parts)
            for c in range(w // LANES):
                tot = planes[0](c)
                for plane in planes[1:]:
                    tot = tot + plane(c)
                if rotate:
                    tot = _rope_apply(tot, c_ref[...], a_ref[...], b_ref[...], -1.0)
                out[:, gi * w + c * LANES:gi * w + (c + 1) * LANES] = tot.astype(BF16)

    return pl.pallas_call(
        body, name=name, grid=(s // t,),
        in_specs=[_residue_spec(p.shape[0], t, w) for p in flat] + [_rows(t, HEAD_DIM)] * 3,
        out_specs=_rows(t, len(groups) * w), out_shape=jax.ShapeDtypeStruct((s, len(groups) * w), BF16),
        scratch_shapes=[pltpu.VMEM(_stage_shape(t, w), F32) for _ in flat],
        compiler_params=_params("parallel"),
    )(*flat, *tabs)


def _dwconv_fwd(u, w_dw, b_dw, after=None):
    s, d2 = u.shape
    d = d2 // 2
    cb = min(CONV_CB, d)
    nblk = d // cb
    tt = min(CONV_T, s)
    deps, dep_specs = _behind(after)

    def body(*all_refs):
        ua_ref, ug_ref, w_ref, b_ref, c_ref, xp_ref = all_refs[len(deps):]
        gl =ua_ref[...].astype(F32) * _sigmoid(ug_ref[...].astype(F32))
        xp_ref[0:CONV_PAD, :] = jnp.zeros((CONV_PAD, cb), F32)
        xp_ref[CONV_PAD:, :] = gl
        wv = w_ref[...]
        bv = b_ref[...]
        for t0 in range(0, s, tt):
            acc = jnp.zeros((tt, cb), F32) + bv
            for kk in range(CONV_WIDTH):
                off = t0 + CONV_PAD - (CONV_WIDTH - 1) + kk
                acc = acc + wv[kk:kk + 1, :] * xp_ref[off:off + tt, :]
            c_ref[t0:t0 + tt, :] = acc

    return pl.pallas_call(
        body, name="dwconv_fwd", grid=(nblk,),
        in_specs=dep_specs + [pl.BlockSpec((s, cb), lambda j: (0, j)), pl.BlockSpec((s, cb), lambda j: (0, j + nblk)),
                              pl.BlockSpec((CONV_PAD, cb), lambda j: (0, j)), pl.BlockSpec((1, cb), lambda j: (0, j))],
        out_specs=pl.BlockSpec((s, cb), lambda j: (0, j)),
        out_shape=jax.ShapeDtypeStruct((s, d), F32),
        scratch_shapes=[pltpu.VMEM((s + CONV_PAD, cb), F32)],
        compiler_params=_params("parallel"),
    )(*deps, u, u, w_dw, b_dw)


def _dwconv_bwd(u, w_dw, dc):
    s, d2 = u.shape
    d = d2 // 2
    cb = min(CONV_CB, d)
    nblk = d // cb
    tt = min(CONV_T, s)

    def body(ua_ref, ug_ref, w_ref, dc_ref, du_ref, dw_ref, dba_ref, dbg_ref, glp_ref, dcp_ref, acc_ref):
        a = ua_ref[...].astype(F32)
        sig = _sigmoid(ug_ref[...].astype(F32))
        glp_ref[0:CONV_PAD, :] = jnp.zeros((CONV_PAD, cb), F32)
        glp_ref[CONV_PAD:, :] = a * sig
        dcp_ref[0:s, :] = dc_ref[...]
        dcp_ref[s:, :] = jnp.zeros((CONV_PAD, cb), F32)
        acc_ref[...] = jnp.zeros_like(acc_ref)
        wv = w_ref[...]
        dba = jnp.zeros((1, cb), F32)
        dbg = jnp.zeros((1, cb), F32)
        for t0 in range(0, s, tt):
            dgl = jnp.zeros((tt, cb), F32)
            dct = dc_ref[t0:t0 + tt, :]
            for kk in range(CONV_WIDTH):
                off = t0 + (CONV_WIDTH - 1) - kk
                dgl = dgl + wv[kk:kk + 1, :] * dcp_ref[off:off + tt, :]
                goff = t0 + CONV_PAD - (CONV_WIDTH - 1) + kk
                prod = dct * glp_ref[goff:goff + tt, :]
                acc_ref[8 * kk:8 * kk + 8, :] += jnp.sum(prod.reshape(tt // 8, 8, cb), axis=0)
            at = ua_ref[t0:t0 + tt, :].astype(F32)
            st = _sigmoid(ug_ref[t0:t0 + tt, :].astype(F32))
            da = dgl * st
            dg = dgl * at * st * (1.0 - st)
            du_ref[0, t0:t0 + tt, :] = da.astype(BF16)
            du_ref[1, t0:t0 + tt, :] = dg.astype(BF16)
            dba = dba + jnp.sum(da, axis=0, keepdims=True)
            dbg = dbg + jnp.sum(dg, axis=0, keepdims=True)
        dba_ref[...] = dba
        dbg_ref[...] = dbg
        for kk in range(CONV_WIDTH):
            dw_ref[kk:kk + 1, :] = jnp.sum(acc_ref[8 * kk:8 * kk + 8, :], axis=0, keepdims=True)
        dw_ref[CONV_WIDTH:, :] = jnp.zeros((CONV_PAD - CONV_WIDTH, cb), F32)

    blk = pl.BlockSpec((s, cb), lambda j: (0, j))
    vec = pl.BlockSpec((1, cb), lambda j: (0, j))
    return pl.pallas_call(
        body, name="dwconv_bwd", grid=(nblk,),
        in_specs=[blk, pl.BlockSpec((s, cb), lambda j: (0, j + nblk)),
                  pl.BlockSpec((CONV_PAD, cb), lambda j: (0, j)), blk],
        out_specs=[pl.BlockSpec((2, s, cb), lambda j: (0, 0, j)), pl.BlockSpec((CONV_PAD, cb), lambda j: (0, j)),
                   vec, vec],
        out_shape=[jax.ShapeDtypeStruct((2, s, d), BF16), jax.ShapeDtypeStruct((CONV_PAD, d), F32),
                   jax.ShapeDtypeStruct((1, d), F32), jax.ShapeDtypeStruct((1, d), F32)],
        scratch_shapes=[pltpu.VMEM((s + CONV_PAD, cb), F32), pltpu.VMEM((s + CONV_PAD, cb), F32),
                        pltpu.VMEM((8 * CONV_PAD, cb), F32)],
        compiler_params=_params("parallel"),
    )(u, u, w_dw, dc)


def _stack_heads(x, group):
    return jnp.concatenate([x[:, g * HEAD_DIM:(g + 1) * HEAD_DIM] for g in range(group)], axis=0)


def _unstack_heads(x, group):
    return jnp.concatenate([x[g * ATT_BLOCK:(g + 1) * ATT_BLOCK, :] for g in range(group)], axis=1)


def _stack_cols(x, group):
    return jnp.concatenate([x[:, g:g + 1] for g in range(group)], axis=0)


def _band_bias(group):
    rows = group * ATT_BLOCK
    row = lax.broadcasted_iota(I32, (rows, 2 * ATT_BLOCK), 0) % ATT_BLOCK
    col = lax.broadcasted_iota(I32, (rows, 2 * ATT_BLOCK), 1)
    band = jnp.where((col >= row) & (col <= row + ATT_BLOCK), 0.0, -jnp.inf).astype(F32)
    first = jnp.where(lax.broadcasted_iota(I32, (1, 2 * ATT_BLOCK), 1) >= ATT_BLOCK, 0.0, -jnp.inf).astype(F32)
    return band, first


def _masked_scores(qs, kw, band_ref, first_ref, nb, scale):
    sc = lax.dot_general(qs, kw, (((1,), (1,)), ((), ())), preferred_element_type=F32) * scale + band_ref[...]
    return sc + jnp.where(nb > 0, 0.0, first_ref[...])


def _window(ref, nb):
    prev = pl.multiple_of(jnp.maximum(nb - 1, 0) * ATT_BLOCK, ATT_BLOCK)
    cur = pl.multiple_of(nb * ATT_BLOCK, ATT_BLOCK)
    return jnp.concatenate([ref[pl.ds(prev, ATT_BLOCK), :], ref[pl.ds(cur, ATT_BLOCK), :]], axis=0)


def _residues_per_step(dil, nblk):
    return max(1, min(dil, ATT_STEP_BLOCKS // nblk))


def _attn_fwd(name, q, kv):
    dil, sd, d = q.shape
    group = d // HEAD_DIM // N_KV_HEADS
    gw = group * HEAD_DIM
    nblk = sd // ATT_BLOCK
    scale = 1.0 / math.sqrt(HEAD_DIM)
    rb = _residues_per_step(dil, nblk)

    def body(q_all, k_all, v_all, band_ref, first_ref, o_all, lse_all):
        lane = lax.broadcasted_iota(I32, (ATT_BLOCK, LANES), 1)
        for rr in range(rb):
            q_ref, k_ref, v_ref, o_ref, lse_ref = [ref.at[rr] for ref in (q_all, k_all, v_all, o_all, lse_all)]

            def step(nb, carry):
                rows = pl.ds(pl.multiple_of(nb * ATT_BLOCK, ATT_BLOCK), ATT_BLOCK)
                qs = _stack_heads(q_ref[rows, :], group)
                kw = _window(k_ref, nb)
                vw = _window(v_ref, nb)
                sc = _masked_scores(qs, kw, band_ref, first_ref, nb, scale)
                mx = jnp.max(sc, axis=-1, keepdims=True)
                p = jnp.exp(sc - mx)
                l = jnp.sum(p, axis=-1, keepdims=True)
                o = jnp.dot(p.astype(BF16), vw, preferred_element_type=F32) / l
                o_ref[rows, :] = _unstack_heads(o, group).astype(BF16)
                lse = mx + jnp.log(l)
                out = jnp.zeros((ATT_BLOCK, LANES), F32)
                for g in range(group):
                    out = jnp.where(lane == g, lse[g * ATT_BLOCK:(g + 1) * ATT_BLOCK, :], out)
                lse_ref[rows, :] = out
                return carry

            lax.fori_loop(0, nblk, step, 0, unroll=min(2, nblk))

    kvh = N_KV_HEADS
    band, first = _band_bias(group)
    qspec = pl.BlockSpec((rb, sd, gw), lambda r, h: (r, 0, h))
    kspec = pl.BlockSpec((rb, sd, HEAD_DIM), lambda r, h: (r, 0, h))
    return pl.pallas_call(
        body, name=name, grid=(dil // rb, kvh),
        in_specs=[qspec, kspec, pl.BlockSpec((rb, sd, HEAD_DIM), lambda r, h: (r, 0, kvh + h)),
                  pl.BlockSpec(band.shape, lambda r, h: (0, 0)), pl.BlockSpec(first.shape, lambda r, h: (0, 0))],
        out_specs=[qspec, kspec],
        out_shape=[jax.ShapeDtypeStruct((dil, sd, d), BF16),
                   jax.ShapeDtypeStruct((dil, sd, kvh * LANES), F32)],
        compiler_params=_params("parallel", "parallel"),
    )(q, kv, kv, band, first)


def _attn_bwd(name, q, kv, do, lse, delta):
    dil, sd, d = q.shape
    group = d // HEAD_DIM // N_KV_HEADS
    gw = group * HEAD_DIM
    nblk = sd // ATT_BLOCK
    scale = 1.0 / math.sqrt(HEAD_DIM)
    nt = (((1,), (1,)), ((), ()))
    tn = (((0,), (0,)), ((), ()))

    rb = _residues_per_step(dil, nblk)

    def body(q_all, k_all, v_all, do_all, lse_all, dl_all, band_ref, first_ref, dq_all, dk_all, dv_all, dk_accs,
             dv_accs):
        dk_accs[...] = jnp.zeros_like(dk_accs)
        dv_accs[...] = jnp.zeros_like(dv_accs)
        for rr in range(rb):
            q_ref, k_ref, v_ref, do_ref, lse_ref, dl_ref, dq_ref, dk_ref, dv_ref, dk_acc, dv_acc = [
                ref.at[rr] for ref in (q_all, k_all, v_all, do_all, lse_all, dl_all, dq_all, dk_all, dv_all,
                                       dk_accs, dv_accs)]

            def step(nb, carry):
                rows = pl.ds(pl.multiple_of(nb * ATT_BLOCK, ATT_BLOCK), ATT_BLOCK)
                qs = _stack_heads(q_ref[rows, :], group)
                dos = _stack_heads(do_ref[rows, :], group)
                ls = _stack_cols(lse_ref[rows, :], group)
                dl = _stack_cols(dl_ref[rows, :], group)
                kw = _window(k_ref, nb)
                vw = _window(v_ref, nb)
                p = jnp.exp(_masked_scores(qs, kw, band_ref, first_ref, nb, scale) - ls)
                dp = lax.dot_general(dos, vw, nt, preferred_element_type=F32)
                ds = (p * (dp - dl) * scale).astype(BF16)
                dq = jnp.dot(ds, kw, preferred_element_type=F32)
                dq_ref[rows, :] = _unstack_heads(dq, group).astype(BF16)
                win = pl.ds(pl.multiple_of(nb * ATT_BLOCK, ATT_BLOCK), 2 * ATT_BLOCK)
                dk_acc[win, :] += lax.dot_general(ds, qs, tn, preferred_element_type=F32)
                dv_acc[win, :] += lax.dot_general(p.astype(BF16), dos, tn, preferred_element_type=F32)
                return carry

            lax.fori_loop(0, nblk, step, 0, unroll=min(2, nblk))
            dk_ref[...] = dk_acc[ATT_BLOCK:, :]
            dv_ref[...] = dv_acc[ATT_BLOCK:, :]

    kvh = N_KV_HEADS
    band, first = _band_bias(group)
    qspec = pl.BlockSpec((rb, sd, gw), lambda r, h: (r, 0, h))
    kspec = pl.BlockSpec((rb, sd, HEAD_DIM), lambda r, h: (r, 0, h))
    return pl.pallas_call(
        body, name=name, grid=(dil // rb, kvh),
        in_specs=[qspec, kspec, pl.BlockSpec((rb, sd, HEAD_DIM), lambda r, h: (r, 0, kvh + h)),
                  qspec, kspec, kspec,
                  pl.BlockSpec(band.shape, lambda r, h: (0, 0)), pl.BlockSpec(first.shape, lambda r, h: (0, 0))],
        out_specs=[qspec, kspec, kspec],
        out_shape=[jax.ShapeDtypeStruct((dil, sd, d), BF16),
                   jax.ShapeDtypeStruct((dil, sd, kvh * HEAD_DIM), F32),
                   jax.ShapeDtypeStruct((dil, sd, kvh * HEAD_DIM), F32)],
        scratch_shapes=[pltpu.VMEM((rb, sd + ATT_BLOCK, HEAD_DIM), F32)] * 2,
        compiler_params=_params("parallel", "parallel"),
    )(q, kv, kv, do, lse, delta, band, first)


def _cast_bf16(name, w, layer, place, after=None):
    _, r, c = w.shape
    tr = min(512, r)
    deps = [] if after is None else [after]

    def body(pl_ref, w_ref, *refs):
        refs[-1][...] = w_ref[...].astype(BF16)

    return pl.pallas_call(
        body, name=name,
        grid_spec=pltpu.PrefetchScalarGridSpec(
            num_scalar_prefetch=1, grid=(r // tr,),
            in_specs=[pl.BlockSpec((None, tr, c), lambda i, p: (layer, i, 0))] + [ANY] * len(deps),
            out_specs=pl.BlockSpec((None, tr, c), lambda i, p: (p[1], i, 0))),
        out_shape=jax.ShapeDtypeStruct((N_SHARD, r, c), BF16),
        compiler_params=_params("parallel"),
    )(place, w, *deps)


def _chip_sum(name, g, rh, place):
    _, r, c = g.shape
    rh2 = r // 2
    tr = min(512, rh2)
    nb = rh2 // tr

    def body(pl_ref, g_ref, rh_ref, o_ref):
        o_ref[...] = (g_ref[...].astype(F32) + rh_ref[...].astype(F32)).astype(BF16)

    return pl.pallas_call(
        body, name=name,
        grid_spec=pltpu.PrefetchScalarGridSpec(
            num_scalar_prefetch=1, grid=(N_SHARD, nb),
            in_specs=[pl.BlockSpec((None, tr, c), lambda s, i, p: (s, p[0] * nb + i, 0)),
                      pl.BlockSpec((None, tr, c), lambda s, i, p: (s, i, 0))],
            out_specs=pl.BlockSpec((None, tr, c), lambda s, i, p: (s, i, 0))),
        out_shape=jax.ShapeDtypeStruct((N_SHARD, rh2, c), BF16),
        compiler_params=_params("parallel", "parallel"),
    )(place, g, rh)


def _owner_sum(name, cs, rp, place):
    _, rh2, c = cs.shape
    tr = min(512, rh2)
    nb = rh2 // tr

    def body(pl_ref, cs_ref, r0_ref, r1_ref, r2_ref, o_ref):
        o_ref[...] = ((cs_ref[...].astype(F32) + r0_ref[...].astype(F32))
                      + (r1_ref[...].astype(F32) + r2_ref[...].astype(F32)))

    def rspec(j):
        return pl.BlockSpec((None, tr, c), lambda i, p: (j, i, 0))

    return pl.pallas_call(
        body, name=name,
        grid_spec=pltpu.PrefetchScalarGridSpec(
            num_scalar_prefetch=1, grid=(nb,),
            in_specs=[pl.BlockSpec((None, tr, c), lambda i, p: (p[1], i, 0)), rspec(0), rspec(1), rspec(2)],
            out_specs=pl.BlockSpec((tr, c), lambda i, p: (p[0] * nb + i, 0))),
        out_shape=jax.ShapeDtypeStruct((2 * rh2, c), F32),
        compiler_params=_params("parallel"),
    )(place, cs, rp, rp, rp)


def _adam_math(w, g, m, v):
    m = ADAM_B1 * m + (1.0 - ADAM_B1) * g
    v = ADAM_B2 * v + (1.0 - ADAM_B2) * (g * g)
    m_hat = m / (1.0 - ADAM_B1 ** ADAM_STEP)
    v_hat = v / (1.0 - ADAM_B2 ** ADAM_STEP)
    delta = -ADAM_LR * (m_hat / (jnp.sqrt(v_hat) + ADAM_EPS) + ADAM_WD * w)
    return delta, m, v


def _adamw(name, w, m, v, g, layer, partial=None):
    nl, r, c = w.shape
    tr = min(256, r)

    def body(w_ref, m_ref, v_ref, g_ref, *refs):
        go_ref, d_ref, mo_ref, vo_ref = refs[-4:]
        gv = g_ref[...]
        delta, m_new, v_new = _adam_math(w_ref[...], gv, m_ref[...], v_ref[...])
        go_ref[...] = gv
        d_ref[...] = delta
        mo_ref[...] = m_new
        vo_ref[...] = v_new

    wspec = pl.BlockSpec((None, tr, c), lambda i: (layer, i, 0))
    prev = [] if partial is None else list(partial)
    return pl.pallas_call(
        body, name=name, grid=(r // tr,),
        in_specs=[wspec] * 3 + [pl.BlockSpec((tr, c), lambda i: (i, 0))] + [ANY] * len(prev),
        out_specs=[wspec] * 4,
        out_shape=[jax.ShapeDtypeStruct((nl, r, c), F32)] * 4,
        input_output_aliases={4 + i: i for i in range(len(prev))},
        compiler_params=_params("parallel"),
    )(w, m, v, g, *prev)


def _adam_small(ws, ms, vs, gs):
    n = len(ws)

    def body(*refs):
        w_refs, m_refs, v_refs, g_refs = refs[:n], refs[n:2 * n], refs[2 * n:3 * n], refs[3 * n:4 * n]
        d_refs, mo_refs, vo_refs = refs[4 * n:5 * n], refs[5 * n:6 * n], refs[6 * n:7 * n]
        for i in range(n):
            delta, m_new, v_new = _adam_math(w_refs[i][...], g_refs[i][...], m_refs[i][...], v_refs[i][...])
            d_refs[i][...] = delta
            mo_refs[i][...] = m_new
            vo_refs[i][...] = v_new

    shapes = [jax.ShapeDtypeStruct(w.shape, F32) for w in ws]
    res = pl.pallas_call(body, name="adam_small", out_shape=shapes * 3)(*ws, *ms, *vs, *gs)
    return res[:n], res[n:2 * n], res[2 * n:]


def _pack_small(b_in, w_dw, b_dw, ln_g, ln_b, b_out, place):
    cin = b_in.shape[1]
    cd = b_dw.shape[1]
    rows = 8 + CONV_PAD

    def body(pl_ref, bi, wd, bd, lg, lb, bo, out):
        out[...] = jnp.zeros_like(out)
        out[0:1, :] = bi[...]
        out[1:2, 0:cd] = bd[...]
        out[1:2, cd:2 * cd] = lg[...]
        out[2:3, 0:cd] = lb[...]
        out[2:3, cd:2 * cd] = bo[...]
        out[8:8 + CONV_WIDTH, 0:cd] = wd[...]

    def whole(arr):
        return pl.BlockSpec(arr.shape, lambda i, p: (0,) * arr.ndim)

    ins = [b_in, w_dw, b_dw, ln_g, ln_b, b_out]
    return pl.pallas_call(
        body, name="pack_small",
        grid_spec=pltpu.PrefetchScalarGridSpec(
            num_scalar_prefetch=1, grid=(1,), in_specs=[whole(a) for a in ins],
            out_specs=pl.BlockSpec((None, rows, cin), lambda i, p: (p[1], 0, 0))),
        out_shape=jax.ShapeDtypeStruct((N_SHARD, rows, cin), F32),
        compiler_params=_params("arbitrary"),
    )(place, *ins)


def _place():
    x, y, c = lax.axis_index("x"), lax.axis_index("y"), lax.axis_index("c")
    return x, y, c


def _other_chips(x, y):
    return [(1 - x, y), (x, 1 - y), (1 - x, 1 - y)]


def _split_start_many(name, parts, after=None):
    flat = [b for bufs, _, _ in parts for b in bufs]
    n, n_parts = len(flat), len(parts)
    deps = [] if after is None else [after]

    def body(*refs):
        out0 = n + len(deps)
        pos = 0
        for i, (bufs, _, copies) in enumerate(parts):
            for cp in copies(refs[pos:pos + len(bufs)], refs[out0 + 2 * i], refs[out0 + 2 * i + 1], False):
                cp.start()
            pos += len(bufs)
        refs[-1][...] = jnp.zeros_like(refs[-1])

    sems = [pltpu.SemaphoreType.DMA((n_sem,)) for _, n_sem, _ in parts for _ in range(2)]
    res = pl.pallas_call(
        body, name=name,
        out_shape=(*sems, *[pltpu.HBM(b.shape, b.dtype) for b in flat], jax.ShapeDtypeStruct((8, LANES), F32)),
        in_specs=[HBM] * n + [ANY] * len(deps),
        out_specs=(*[SEM] * (2 * n_parts), *[HBM] * n, pl.BlockSpec(memory_space=pltpu.VMEM)),
        input_output_aliases={i: 2 * n_parts + i for i in range(n)},
        compiler_params=pltpu.CompilerParams(has_side_effects=SPLIT_EFFECT),
    )(*[pltpu.with_memory_space_constraint(b, pltpu.HBM) for b in flat], *deps)
    handles, pos = [], 2 * n_parts
    for i, (bufs, _, _) in enumerate(parts):
        handles.append((res[2 * i], res[2 * i + 1], list(res[pos:pos + len(bufs)]), res[-1]))
        pos += len(bufs)
    return handles


def _split_start(name, bufs, n_sem, copies, after=None):
    return _split_start_many(name, [(bufs, n_sem, copies)], after)[0]


def _split_wait(name, handle, copies, after):
    ssem, rsem, bufs, _ = handle
    n = len(bufs)
    deps = list(after) if isinstance(after, (list, tuple)) else [after]

    def body(*refs):
        for cp in copies(refs[:n], refs[n], refs[n + 1], True):
            cp.wait_send()
            cp.wait_recv()

    res = pl.pallas_call(
        body, name=name,
        out_shape=[pltpu.HBM(b.shape, b.dtype) for b in bufs],
        in_specs=[HBM] * n + [SEM, SEM] + [ANY] * len(deps), out_specs=[HBM] * n,
        input_output_aliases={i: i for i in range(n)},
        compiler_params=pltpu.CompilerParams(has_side_effects=SPLIT_EFFECT),
    )(*bufs, ssem, rsem, *deps)
    return list(res)


def _remote(src, dst, ssem, rsem, k, to):
    return pltpu.make_async_remote_copy(src_ref=src, dst_ref=dst, send_sem=ssem.at[k], recv_sem=rsem.at[k],
                                        device_id=to, device_id_type=MESH)


def _gather_chips(x, y, c):
    nx, ny = x + (1 - c) - 2 * x * (1 - c), y + c - 2 * y * c
    fx, fy = x + c - 2 * x * c, y + (1 - c) - 2 * y * (1 - c)
    return (nx, ny), (fx, fy), 2 * nx + ny, 2 * fx + fy, 2 * (1 - x) + (1 - y)


def _direct_copies(refs, ssem, rsem, landing, n_whole=0):
    x, y, c = _place()
    me = 2 * x + y
    (nx, ny), _, near, _, _ = _gather_chips(x, y, c)
    n = len(refs) - n_whole
    cps = []
    for a, ref in enumerate(refs[:n]):
        cps.append(_remote(ref.at[me], ref.at[near if landing else me], ssem, rsem, a, (nx, ny, c)))
    for b, ref in enumerate(refs[n:]):
        for j, (px, py) in enumerate(_other_chips(x, y)):
            cps.append(_remote(ref.at[me], ref.at[2 * px + py if landing else me], ssem, rsem, n + 3 * b + j,
                               (px, py, c)))
    return cps


def _relay_copies(refs, ssem, rsem, landing):
    x, y, c = _place()
    _, (fx, fy), near, far, diag = _gather_chips(x, y, c)
    n = len(refs)
    cps = []
    for a, ref in enumerate(refs):
        rh = ref.shape[1] // 2
        rows = pl.ds(c * rh, rh)
        cps.append(_remote(ref.at[near, rows], ref.at[diag if landing else near, rows], ssem, rsem, a, (fx, fy, c)))
        cps.append(_remote(ref.at[near], ref.at[far if landing else near], ssem, rsem, n + a, (x, y, 1 - c)))
    return cps


def _diagonal_copies(refs, ssem, rsem, landing):
    x, y, c = _place()
    diag = 2 * (1 - x) + (1 - y)
    who = 1 - c if landing else c
    cps = []
    for a, ref in enumerate(refs):
        rh = ref.shape[1] // 2
        piece = ref.at[diag, pl.ds(who * rh, rh)]
        cps.append(_remote(piece, piece, ssem, rsem, a, (x, y, 1 - c)))
    return cps


def _sibling_copies(refs, ssem, rsem, landing):
    x, y, c = _place()
    n = len(refs) // 2
    cps = []
    for a in range(n):
        rh = refs[a].shape[1] // 2
        cps.append(_remote(refs[a].at[:, pl.ds((1 - c) * rh, rh), :], refs[n + a], ssem, rsem, a, (x, y, 1 - c)))
    return cps


def _owner_copies(refs, ssem, rsem, landing):
    x, y, c = _place()
    n = len(refs) // 2
    cps = []
    for a in range(n):
        for j, (px, py) in enumerate(_other_chips(x, y)):
            cps.append(_remote(refs[a].at[2 * px + py], refs[n + a].at[j], ssem, rsem, 3 * a + j, (px, py, c)))
    return cps


def _swap_copies(refs, ssem, rsem, landing):
    x, y, c = _place()
    who = 1 - c if landing else c
    cps = []
    for a, ref in enumerate(refs):
        rh = ref.shape[0] // 2
        rows = ref.at[pl.ds(who * rh, rh)]
        cps.append(_remote(rows, rows, ssem, rsem, a, (x, y, 1 - c)))
    return cps


def _small_copies(refs, ssem, rsem, landing):
    pack, slots = refs
    x, y, c = _place()
    cps = []
    for rel in range(1, N_DEV):
        px = 1 - x if (rel >> 2) & 1 else x
        py = 1 - y if (rel >> 1) & 1 else y
        pc = 1 - c if rel & 1 else c
        slot = 4 * px + 2 * py + pc if landing else 4 * x + 2 * y + c
        cps.append(_remote(pack, slots.at[slot], ssem, rsem, rel - 1, (px, py, pc)))
    return cps


def _small_pack(rows, w_dw_grad, d):
    n = len(rows)

    def body(*refs):
        pack = refs[-1]
        pack[...] = jnp.zeros_like(pack)
        for (r, _), ref in zip(rows, refs[:n]):
            pack[r:r + 1, :] = ref[...]
        pack[16:16 + CONV_PAD, :] = refs[n][...]

    return pl.pallas_call(body, name="small_pack", out_shape=jax.ShapeDtypeStruct((SMALL_ROWS, d), F32))(
        *[v for _, v in rows], w_dw_grad)


def _small_sum(pack, slots, place):
    rows, d = pack.shape
    loss_row = 12

    def body(pl_ref, pack_ref, slots_ref, out_ref):
        me = pl_ref[2]
        tot = jnp.where(me == 0, pack_ref[...], slots_ref[0])
        for i in range(1, N_DEV):
            tot = tot + jnp.where(me == i, pack_ref[...], slots_ref[i])
        out_ref[...] = tot
        out_ref[loss_row:loss_row + 1, :] = jnp.zeros((1, d), F32) + jnp.sum(tot[loss_row:loss_row + 1, :])

    return pl.pallas_call(
        body, name="small_sum",
        grid_spec=pltpu.PrefetchScalarGridSpec(
            num_scalar_prefetch=1, grid=(1,),
            in_specs=[pl.BlockSpec((rows, d), lambda i, p: (0, 0)), pl.BlockSpec((N_DEV, rows, d), lambda i, p: (0, 0, 0))],
            out_specs=pl.BlockSpec((rows, d), lambda i, p: (0, 0))),
        out_shape=jax.ShapeDtypeStruct((rows, d), F32),
        compiler_params=_params("arbitrary"),
    )(place, pack, slots)


def kernel(x, norm_mix, norm_mlp, conv_w_in, conv_b_in, conv_w_dw, conv_b_dw, conv_ln_g, conv_ln_b, conv_w_out, conv_b_out, kv_norm, w_kv, attn_w_q, attn_w_o, mlp_w_in, mlp_w_out, final_norm, loss_target, m_norm_mix, m_norm_mlp, m_conv_w_in, m_conv_b_in, m_conv_w_dw, m_conv_b_dw, m_conv_ln_g, m_conv_ln_b, m_conv_w_out, m_conv_b_out, m_kv_norm, m_w_kv, m_attn_w_q, m_attn_w_o, m_mlp_w_in, m_mlp_w_out, m_final_norm, v_norm_mix, v_norm_mlp, v_conv_w_in, v_conv_b_in, v_conv_w_dw, v_conv_b_dw, v_conv_ln_g, v_conv_ln_b, v_conv_w_out, v_conv_b_out, v_kv_norm, v_w_kv, v_attn_w_q, v_attn_w_o, v_mlp_w_in, v_mlp_w_out, v_final_norm):
    _, s, d = x.shape
    dff = mlp_w_in.shape[2] * N_SHARD
    kvw = w_kv.shape[1]
    ds4 = d // N_SHARD
    xi, yi, ci = _place()
    me = 2 * xi + yi
    place = jnp.stack([ci, me, 2 * me + ci]).astype(I32)

    h0 = x.reshape(s, d)
    target = loss_target.reshape(s, d)
    tabs = _rope_tables(s)

    def gather_begin_many(groups):
        plans = [functools.partial(_direct_copies, n_whole=n_whole) for _, _, n_whole in groups]
        handles = _split_start_many("gather_start_" + "_".join(tag for tag, _, _ in groups),
                                    [(bufs, len(bufs) + 2 * n_whole, plan)
                                     for (_, bufs, n_whole), plan in zip(groups, plans)])
        return [(handle, plan, n_whole) for handle, plan, (_, _, n_whole) in zip(handles, plans, groups)]

    def gather_begin(tag, bufs, n_whole=0):
        return gather_begin_many([(tag, bufs, n_whole)])[0]

    def gather_step(later, land=None, swap=None):
        parts, names, whole = [], [], {}
        if land is not None:
            tag, (handle, plan, n_whole) = land
            bufs = _split_wait(f"gather_wait_{tag}", handle, plan, later)
            n = len(bufs) - n_whole
            parts.append((bufs[:n], 2 * n, _relay_copies))
            whole["land"] = bufs[n:]
            names.append(f"relay_{tag}")
        if swap is not None:
            tag, (relayed, whole["swap"]) = swap
            bufs = _split_wait(f"relay_wait_{tag}", relayed, _relay_copies, later)
            parts.append((bufs, len(bufs), _diagonal_copies))
            names.append(f"diagonal_{tag}")
        handles = _split_start_many("start_" + "_".join(names), parts)
        landed = (handles[0], whole["land"]) if land is not None else None
        swapped = (handles[-1], whole["swap"]) if swap is not None else None
        return landed, swapped

    def gather_land(tag, begun, later):
        return gather_step(later, land=(tag, begun))[0]

    def gather_swap(tag, landed, later):
        return gather_step(later, swap=(tag, landed))[1]

    def gather_end(tag, swapped, later):
        handle, whole = swapped
        return _split_wait(f"diagonal_wait_{tag}", handle, _diagonal_copies, later) + whole

    ag_cin = gather_begin("conv_in", [
        _cast_bf16("cast_w_in", conv_w_in, 0, place),
        _pack_small(conv_b_in, conv_w_dw.reshape(CONV_WIDTH, ds4), conv_b_dw, conv_ln_g, conv_ln_b, conv_b_out, place),
    ], n_whole=1)
    ag_cout = gather_begin("conv_out", [_cast_bf16("cast_w_out", conv_w_out, 0, place, ag_cin[0][3])])
    ag_mi0 = gather_begin("mlp_in0", [_cast_bf16("cast_mlp_in0", mlp_w_in, 0, place, ag_cout[0][3])])
    ag_mo0 = gather_begin("mlp_out0", [_cast_bf16("cast_mlp_out0", mlp_w_out, 0, place, ag_mi0[0][3])])
    nm = [norm_mix[0:1], norm_mix[1:2]]
    nmlp = [norm_mlp[0:1], norm_mlp[1:2]]
    kvn = kv_norm.reshape(1, d)
    fin = final_norm.reshape(1, d)
    (y0,) = _rms_fwd("rms_mix0", h0, [nm[0]], after=ag_mo0[0][3])
    land_cin = gather_land("conv_in", ag_cin, y0)
    ag_attn, ag_mi1, ag_mo1 = gather_begin_many([
        ("attn", [_cast_bf16("cast_w_kv", w_kv.reshape(1, ds4, kvw), 0, place, land_cin[0][3]),
                  _cast_bf16("cast_w_q", attn_w_q, 0, place), _cast_bf16("cast_w_o", attn_w_o, 0, place)], 0),
        ("mlp_in1", [_cast_bf16("cast_mlp_in1", mlp_w_in, 1, place, land_cin[0][3])], 0),
        ("mlp_out1", [_cast_bf16("cast_mlp_out1", mlp_w_out, 1, place, land_cin[0][3])], 0)])
    land_cout, swap_cin = gather_step(ag_mo1[0][3], land=("conv_out", ag_cout), swap=("conv_in", land_cin))

    wmi_g = [None, None]
    wmo_f = [None, None]

    w_in_g, small_g = gather_end("conv_in", swap_cin, swap_cin[0][3])
    b_in_f = small_g[:, 0, :].reshape(1, 2 * d)
    b_dw_f = small_g[:, 1, 0:ds4].reshape(1, d)
    ln_g_f = small_g[:, 1, ds4:2 * ds4].reshape(1, d)
    ln_b_f = small_g[:, 2, 0:ds4].reshape(1, d)
    b_out_f = small_g[:, 2, ds4:2 * ds4].reshape(1, d)
    w_dw_f = jnp.transpose(small_g[:, 8:8 + CONV_PAD, 0:ds4], (1, 0, 2)).reshape(CONV_PAD, d)

    def ep_bias(acc, ex, outs, j):
        outs[0][...] = (acc + ex[0][...]).astype(outs[0].dtype)

    def ep_residual(acc, ex, outs, j):
        outs[0][...] = ex[0][...] + acc

    def ep_residual_bias(acc, ex, outs, j):
        outs[0][...] = ex[0][...] + (acc + ex[1][...])

    def ep_relu2(acc, ex, outs, j):
        r = jnp.maximum(acc, 0.0)
        outs[0][...] = r.astype(BF16)
        outs[1][...] = (r * r).astype(BF16)

    by_residue = [(BF16, ("residues", dil)) for dil in DILATIONS]

    def put_by_residue(val, outs, stage):
        _to_residues(val, stage, outs, DILATIONS)

    def ep_rope(acc, ex, outs, j, stage):
        put_by_residue(_rope_apply(acc, ex[0][...], ex[1][...], ex[2][...], 1.0), outs, stage)

    def ep_rope_k(acc, ex, outs, j, stage):
        roped = _rope_apply(acc, ex[0][...], ex[1][...], ex[2][...], 1.0)
        put_by_residue(jnp.where(j == 0, roped, acc), outs, stage)

    def ep_by_residue(acc, ex, outs, j, stage):
        put_by_residue(acc, outs, stage)

    tab_extras = [(t, "rows") for t in tabs]

    def mlp_fwd(idx, h, y, out_weight):
        r, r2 = _matmul(f"mlp_in{idx}", "nn", y, wmi_g[idx], b_kind="col", m=s, n=dff, k=d,
                        outs=[(BF16, "plain"), (BF16, "plain")], epilogue=ep_relu2)
        wmo_f[idx] = out_weight(r2).reshape(dff, d)
        (h_new,) = _matmul(f"mlp_out{idx}", "nn", r2, wmo_f[idx], m=s, n=d, k=dff,
                           outs=[(F32, "plain")], extras=[(h, "ij")], epilogue=ep_residual)
        return h_new, r, r2

    (u,) = _matmul("conv_in", "nn", y0, w_in_g, b_kind="col", m=s, n=2 * d, k=d,
                   outs=[(BF16, "plain")], extras=[(b_in_f, "vec")], epilogue=ep_bias)
    land_mi0, swap_cout = gather_step(u, land=("mlp_in0", ag_mi0), swap=("conv_out", land_cout))
    cpre = _dwconv_fwd(u, w_dw_f, b_dw_f, after=swap_cout[0][3])
    sact = _ln_silu_fwd(cpre, ln_g_f, ln_b_f)
    (w_out_g,) = gather_end("conv_out", swap_cout, sact)
    w_out_f = w_out_g.reshape(d, d)
    (h1,) = _matmul("conv_out", "nn", sact, w_out_f, m=s, n=d, k=d,
                    outs=[(F32, "plain")], extras=[(h0, "ij"), (b_out_f, "vec")], epilogue=ep_residual_bias)
    swap_mi0 = gather_swap("mlp_in0", land_mi0, h1)
    (y1,) = _rms_fwd("rms_mlp0", h1, [nmlp[0]], after=swap_mi0[0][3])
    land_mo0 = gather_land("mlp_out0", ag_mo0, y1)
    (wmi_g[0],) = gather_end("mlp_in0", swap_mi0, land_mo0[0][3])
    land_attn = None

    def out_weight0(r2):
        nonlocal land_attn
        land_attn, swap_mo0 = gather_step(r2, land=("attn", ag_attn), swap=("mlp_out0", land_mo0))
        return gather_end("mlp_out0", swap_mo0, swap_mo0[0][3])[0]

    h2, r0, r0sq = mlp_fwd(0, h1, y1, out_weight0)
    land_mi1, swap_attn = gather_step(h2, land=("mlp_in1", ag_mi1), swap=("attn", land_attn))
    ykv, y2 = _rms_fwd("rms_kv_mix1", h2, [kvn, nm[1]], after=land_mi1[0][3])
    wkv_g, wq_g, wo_g = gather_end("attn", swap_attn, y2)
    wkv_f, wq_f, wo_f = wkv_g.reshape(d, kvw), wq_g.reshape(d, d), wo_g.reshape(d, d)
    kv_parts = _matmul("kv_proj", "nn", ykv, wkv_f, m=s, n=kvw, k=d, tn=kvw // 2,
                       outs=by_residue, extras=tab_extras, epilogue=ep_rope_k, stage=True)
    q_parts = _matmul("q_proj", "nn", y2, wq_f, m=s, n=d, k=d,
                      outs=by_residue, extras=tab_extras, epilogue=ep_rope, stage=True)
    o_parts, lse_parts = [], []
    for dil, q_b, kv_b in zip(DILATIONS, q_parts, kv_parts):
        o_b, lse_b = _attn_fwd(f"attn_fwd_d{dil}", q_b, kv_b)
        o_parts.append(o_b)
        lse_parts.append(lse_b)
    o, lse = _attn_combine(o_parts, lse_parts)
    land_mo1, swap_mi1 = gather_step(o, land=("mlp_out1", ag_mo1), swap=("mlp_in1", land_mi1))
    (h3,) = _matmul("attn_out", "nn", o, wo_f, m=s, n=d, k=d,
                    outs=[(F32, "plain")], extras=[(h2, "ij")], epilogue=ep_residual)
    (y3,) = _rms_fwd("rms_mlp1", h3, [nmlp[1]], after=land_mo1[0][3])
    (wmi_g[1],) = gather_end("mlp_in1", swap_mi1, y3)

    def out_weight1(r2):
        swap_mo1 = gather_swap("mlp_out1", land_mo1, r2)
        return gather_end("mlp_out1", swap_mo1, swap_mo1[0][3])[0]

    h4, r1, r1sq = mlp_fwd(1, h3, y3, out_weight1)
    dh4, dh4b, d_fin, loss_cols = _final_loss(h4, fin, target)

    def ep_relu2_bwd(acc, ex, outs, j):
        outs[0][...] = (acc * (2.0 * ex[0][...].astype(F32))).astype(BF16)

    def mlp_bwd(idx, dhb, y, r, r2):
        (dz,) = _matmul(f"mlp_out{idx}_dx", "nt", dhb, wmo_f[idx], m=s, n=dff, k=d,
                        outs=[(BF16, "plain")], extras=[(r, "ij")], epilogue=ep_relu2_bwd)
        (dwo,) = _matmul(f"mlp_out{idx}_dw", "tn", r2, dhb, m=dff, n=d, k=s,
                         outs=[(BF16, "plain")])
        (dy,) = _matmul(f"mlp_in{idx}_dx", "nt", dz, wmi_g[idx], b_kind="col", m=s, n=d, k=dff,
                        outs=[(BF16, "plain")])
        (dwi,) = _matmul(f"mlp_in{idx}_dw", "tn", y, dz, m=d, n=dff, k=s,
                         outs=[(BF16, "col")])
        return dy, dwi, dwo.reshape(N_SHARD, dff // N_SHARD, d)

    def rs_exchange(tag, grads):
        lands = [lax.empty((N_SHARD, g.shape[1] // 2, g.shape[2]), g.dtype) for g in grads]
        return _split_start(f"sibling_start_{tag}", list(grads) + lands, len(grads), _sibling_copies)

    def rs_send(tag, names, exchanged, later):
        bufs = _split_wait(f"sibling_wait_{tag}", exchanged, _sibling_copies, later)
        n = len(names)
        sums = [_chip_sum(f"chip_sum_{nme}", g, rh, place) for nme, g, rh in zip(names, bufs[:n], bufs[n:])]
        lands = [lax.empty((N_SHARD - 1,) + cs.shape[1:], cs.dtype) for cs in sums]
        return _split_start(f"owners_start_{tag}", sums + lands, 3 * n, _owner_copies)

    def rs_sum(tag, names, sent, later):
        bufs = _split_wait(f"owners_wait_{tag}", sent, _owner_copies, later)
        n = len(names)
        own = [_owner_sum(f"owner_sum_{nme}", cs, rp, place) for nme, cs, rp in zip(names, bufs[:n], bufs[n:])]
        return _split_start(f"swap_start_{tag}", own, n, _swap_copies)

    def rs_end(tag, swapped, later):
        return _split_wait(f"swap_wait_{tag}", swapped, _swap_copies, later)

    dy3, g_wmi1, g_wmo1 = mlp_bwd(1, dh4b, y3, r1, r1sq)
    x_mlp1 = rs_exchange("mlp1", [g_wmi1, g_wmo1])
    dh3, dh3b, d_nmlp1 = _rms_bwd("rms_mlp1_bwd", h3, [(nmlp[1], dy3)], dh4, after=x_mlp1[3])

    do_parts = _matmul("attn_out_dx", "nt", dh3b, wo_f, m=s, n=d, k=d, outs=by_residue, epilogue=ep_by_residue,
                       stage=True)
    (g_wo,) = _matmul("attn_out_dw", "tn", o, dh3b, m=d, n=d, k=s, outs=[(BF16, "plain")])
    rs_mlp1 = rs_send("mlp1", ["mlp_in1", "mlp_out1"], x_mlp1, g_wo)
    lse_res, delta_res = _attn_delta(do_parts[0].reshape(s, d), o, lse, DILATIONS)
    dq_parts, dk_parts, dv_parts = [], [], []
    for dil, q_b, kv_b, do_b, lse_b, dl_b in zip(DILATIONS, q_parts, kv_parts, do_parts, lse_res, delta_res):
        dq_b, dk_b, dv_b = _attn_bwd(f"attn_bwd_d{dil}", q_b, kv_b, do_b, lse_b, dl_b)
        dq_parts.append(dq_b)
        dk_parts.append(dk_b)
        dv_parts.append(dv_b)
    dq = _residue_sum("rope_bwd_q", [(dq_parts, True)], tabs)
    dkv = _residue_sum("rope_bwd_kv", [(dk_parts, True), (dv_parts, False)], tabs)
    (g_wq,) = _matmul("q_proj_dw", "tn", y2, dq, m=d, n=d, k=s, outs=[(BF16, "plain")])
    (dy2,) = _matmul("q_proj_dx", "nt", dq, wq_f, m=s, n=d, k=d, outs=[(BF16, "plain")])
    (g_wkv,) = _matmul("kv_proj_dw", "tn", ykv, dkv, m=d, n=kvw, k=s, outs=[(BF16, "plain")])
    (dykv,) = _matmul("kv_proj_dx", "nt", dkv, wkv_f, m=s, n=d, k=kvw, outs=[(BF16, "plain")])
    x_attn = rs_exchange("attn", [g_wkv.reshape(N_SHARD, ds4, kvw), g_wq.reshape(N_SHARD, ds4, d),
                                  g_wo.reshape(N_SHARD, ds4, d)])
    dh2, dh2b, d_nm1, d_kvn = _rms_bwd("rms_kv_mix1_bwd", h2, [(nm[1], dy2), (kvn, dykv)], dh3, after=x_attn[3])
    rs_attn = rs_send("attn", ["w_kv", "w_q", "w_o"], x_attn, dh2b)

    dy1, g_wmi0, g_wmo0 = mlp_bwd(0, dh2b, y1, r0, r0sq)
    x_mlp0 = rs_exchange("mlp0", [g_wmi0, g_wmo0])
    dh1, dh1b, d_nmlp0, d_b_out = _rms_bwd("rms_mlp0_bwd", h1, [(nmlp[0], dy1)], dh2, want_colsum=True,
                                           after=[x_mlp0[3], rs_attn[3]])

    (dsact,) = _matmul("conv_out_dx", "nt", dh1b, w_out_f, m=s, n=d, k=d, outs=[(BF16, "plain")])
    (g_wout,) = _matmul("conv_out_dw", "tn", sact, dh1b, m=d, n=d, k=s, outs=[(BF16, "plain")])
    rs_mlp0 = rs_send("mlp0", ["mlp_in0", "mlp_out0"], x_mlp0, g_wout)
    dc, d_ln_g, d_ln_b, d_b_dw = _ln_silu_bwd(cpre, ln_g_f, ln_b_f, dsact, after=rs_mlp0[3])
    du, d_w_dw, d_b_in_a, d_b_in_g = _dwconv_bwd(u, w_dw_f, dc)
    (g_win,) = _matmul("conv_in_dw", "tn", y0, du, b_kind="col", m=d, n=2 * d, k=s, outs=[(BF16, "col")])
    x_conv = rs_exchange("conv", [g_win, g_wout.reshape(N_SHARD, ds4, d)])
    (dy0,) = _matmul("conv_in_dx", "nt", du, w_in_g, a_kind="col", b_kind="col", m=s, n=d, k=2 * d,
                     outs=[(BF16, "plain")], after=x_conv[3])
    rs_conv = rs_send("conv", ["w_in", "w_out"], x_conv, dy0)
    dx, _, d_nm0 = _rms_bwd("rms_mix0_bwd", h0, [(nm[0], dy0)], dh1, after=rs_conv[3])

    small_rows = [(0, d_nm0), (1, d_nm1), (2, d_nmlp0), (3, d_nmlp1), (4, d_kvn), (5, d_fin), (6, d_b_dw),
                  (7, d_ln_g), (8, d_ln_b), (9, d_b_out), (10, d_b_in_a), (11, d_b_in_g), (12, loss_cols)]
    x_small = _split_start("small_start", [_small_pack(small_rows, d_w_dw, d),
                                           lax.empty((N_DEV, SMALL_ROWS, d), F32)], N_DEV - 1, _small_copies)

    def big(name, w, m, v, g, layer=0, partial=None):
        shape = w.shape
        w3, m3, v3 = [t.reshape((-1,) + shape[-2:]) for t in (w, m, v)]
        if partial is not None:
            partial = [t.reshape(w3.shape) for t in partial]
        res = _adamw(name, w3, m3, v3, g, layer, partial)
        return [t.reshape(shape) for t in res]

    sw_mlp1 = rs_sum("mlp1", ["mlp_in1", "mlp_out1"], rs_mlp1, x_small[3])
    sw_attn = rs_sum("attn", ["w_kv", "w_q", "w_o"], rs_attn, sw_mlp1[3])
    f_wmi1, f_wmo1 = rs_end("mlp1", sw_mlp1, sw_attn[3])
    p_wmi = big("adam_mlp_in1", mlp_w_in, m_mlp_w_in, v_mlp_w_in, f_wmi1, 1)
    p_wmo = big("adam_mlp_out1", mlp_w_out, m_mlp_w_out, v_mlp_w_out, f_wmo1, 1)
    sw_mlp0 = rs_sum("mlp0", ["mlp_in0", "mlp_out0"], rs_mlp0, [p_wmi[0], p_wmo[0]])
    f_wkv, f_wq, f_wo = rs_end("attn", sw_attn, sw_mlp0[3])
    r_wkv = big("adam_w_kv", w_kv, m_w_kv, v_w_kv, f_wkv)
    r_wq = big("adam_w_q", attn_w_q, m_attn_w_q, v_attn_w_q, f_wq)
    r_wo = big("adam_w_o", attn_w_o, m_attn_w_o, v_attn_w_o, f_wo)
    sw_conv = rs_sum("conv", ["w_in", "w_out"], rs_conv, [r_wkv[0], r_wq[0], r_wo[0]])
    f_wmi0, f_wmo0 = rs_end("mlp0", sw_mlp0, sw_conv[3])
    r_wmi = big("adam_mlp_in0", mlp_w_in, m_mlp_w_in, v_mlp_w_in, f_wmi0, 0, p_wmi)
    r_wmo = big("adam_mlp_out0", mlp_w_out, m_mlp_w_out, v_mlp_w_out, f_wmo0, 0, p_wmo)
    f_win, f_wout = rs_end("conv", sw_conv, [r_wmi[0], r_wmo[0]])
    r_win = big("adam_w_in", conv_w_in, m_conv_w_in, v_conv_w_in, f_win)
    r_wout = big("adam_w_out", conv_w_out, m_conv_w_out, v_conv_w_out, f_wout)

    small_pack, small_slots = _split_wait("small_wait", x_small, _small_copies, r_wout[0])
    red = _small_sum(small_pack, small_slots, place)
    loss = red[12, 0]
    g_norm_mix = red[0:2]
    g_norm_mlp = red[2:4]
    g_kv_norm = red[4:5]
    g_final = red[5:6]

    def my_cols(row):
        return lax.dynamic_slice(red, (row, me * ds4), (1, ds4))

    g_b_dw, g_ln_g, g_ln_b, g_b_out = my_cols(6), my_cols(7), my_cols(8), my_cols(9)
    half_in = 2 * d // N_SHARD
    b_in_row = 10 + me // 2
    g_b_in = lax.dynamic_slice(red, (b_in_row, (me % 2) * half_in), (1, half_in))
    g_w_dw = lax.dynamic_slice(red, (16, me * ds4), (CONV_WIDTH, ds4))

    sm_w =[norm_mix, norm_mlp, conv_b_in, conv_w_dw.reshape(CONV_WIDTH, ds4), conv_b_dw, conv_ln_g, conv_ln_b,
            conv_b_out, kv_norm.reshape(1, d), final_norm.reshape(1, d)]
    sm_m = [m_norm_mix, m_norm_mlp, m_conv_b_in, m_conv_w_dw.reshape(CONV_WIDTH, ds4), m_conv_b_dw, m_conv_ln_g,
            m_conv_ln_b, m_conv_b_out, m_kv_norm.reshape(1, d), m_final_norm.reshape(1, d)]
    sm_v = [v_norm_mix, v_norm_mlp, v_conv_b_in, v_conv_w_dw.reshape(CONV_WIDTH, ds4), v_conv_b_dw, v_conv_ln_g,
            v_conv_ln_b, v_conv_b_out, v_kv_norm.reshape(1, d), v_final_norm.reshape(1, d)]
    sm_g = [g_norm_mix, g_norm_mlp, g_b_in, g_w_dw, g_b_dw, g_ln_g, g_ln_b, g_b_out, g_kv_norm, g_final]
    sm_d, sm_nm, sm_nv = _adam_small(sm_w, sm_m, sm_v, sm_g)
    shapes = [norm_mix.shape, norm_mlp.shape, conv_b_in.shape, conv_w_dw.shape, conv_b_dw.shape, conv_ln_g.shape,
              conv_ln_b.shape, conv_b_out.shape, kv_norm.shape, final_norm.shape]
    sm_g, sm_d, sm_nm, sm_nv = [[t.reshape(sh) for t, sh in zip(lst, shapes)] for lst in (sm_g, sm_d, sm_nm, sm_nv)]

    def order(sm, idx):
        return [sm[0], sm[1], r_win[idx], sm[2], sm[3], sm[4], sm[5], sm[6], r_wout[idx], sm[7], sm[8],
                r_wkv[idx], r_wq[idx], r_wo[idx], r_wmi[idx], r_wmo[idx], sm[9]]

    return (loss, dx.reshape(x.shape), *order(sm_g, 0), *order(sm_d, 1), *order(sm_nm, 2), *order(sm_nv, 3))
```

```python
import functools
import math

import jax
import jax.numpy as jnp
from jax import lax
from jax.experimental import pallas as pl
from jax.experimental.pallas import tpu as pltpu

F32 = jnp.float32
BF16 = jnp.bfloat16
I32 = jnp.int32

NORM_EPS = 1e-6
LN_EPS = 1e-5
HEAD_DIM = 128
N_KV_HEADS = 4
ROT_DIM = 32
ROPE_THETA = 500000.0
CONV_WIDTH = 31
CONV_PAD = 32
ATT_BLOCK = 128
ATT_STEP_BLOCKS = 16
DILATIONS = (1, 4, 16)
ADAM_LR = 0.001
ADAM_B1 = 0.9
ADAM_B2 = 0.999
ADAM_EPS = 1e-08
ADAM_WD = 0.01
ADAM_STEP = 10
N_SHARD = 4
N_DEV = 8
LANES = 128
VMEM_LIMIT = 48 * 1024 * 1024
MM_TM, MM_TN, MM_TK = 1024, 1024, 2048
ROW_TILE = 512
CONV_CB = 128
CONV_T = 128
SMALL_ROWS = 48
MESH = pl.DeviceIdType.MESH
ANY = pl.BlockSpec(memory_space=pl.ANY)
HBM = pl.BlockSpec(memory_space=pltpu.HBM)
SEM = pl.BlockSpec(memory_space=pltpu.SEMAPHORE)
SPLIT_EFFECT = pltpu.SideEffectType.DATAFLOW_SIDE_EFFECTING


def _params(*sem):
    return pltpu.CompilerParams(dimension_semantics=sem, vmem_limit_bytes=VMEM_LIMIT)


def _sigmoid(x):
    return 1.0 / (1.0 + jnp.exp(-x))


def _wspec(kind, arr_shape, br, bc, pick):
    if kind == "plain":
        return pl.BlockSpec((br, bc), pick)
    per = arr_shape[2] // bc

    def idx(*g):
        rb, cb = pick(*g)
        return (cb // per, rb, cb % per)

    return pl.BlockSpec((None, br, bc), idx)


def _stage_shape(rows, w):
    return (w // LANES, rows, LANES)


def _to_residues(val, stage_ref, out_refs, dils):
    planes, rows, _ = stage_ref.shape
    for c in range(planes):
        stage_ref[c] = val[:, c * LANES:(c + 1) * LANES]
    for out_ref, dil in zip(out_refs, dils):
        if dil == 1:
            out_ref[0] = val.astype(out_ref.dtype)
            continue
        for r in range(dil):
            for c in range(planes):
                out_ref[r, :, c * LANES:(c + 1) * LANES] = stage_ref.at[c][pl.ds(r, rows // dil, stride=dil), :].astype(
                    out_ref.dtype)


def _from_residues(src_ref, stage_ref, dil):
    planes, rows, _ = stage_ref.shape
    if dil == 1:
        return lambda c: src_ref[0, :, c * LANES:(c + 1) * LANES].astype(F32)
    for r in range(dil):
        for c in range(planes):
            stage_ref.at[c][pl.ds(r, rows // dil, stride=dil), :] = src_ref[r, :, c * LANES:(c + 1) * LANES].astype(F32)
    return lambda c: stage_ref[c]


def _matmul(name, mode, a, b, *, m, n, k, tm=MM_TM, tn=MM_TN, tk=MM_TK, a_kind="plain", b_kind="plain", outs,
            extras=(), epilogue=None, stage=False, after=None):
    tm, tn, tk = min(tm, m), min(tn, n), min(tk, k)
    if b_kind == "col" and mode in ("nn", "tn"):
        tn = min(tn, n // b.shape[0])
    if b_kind == "col" and mode == "nt":
        tk = min(tk, k // b.shape[0])
    if a_kind == "col":
        assert mode == "nt"
        tk = min(tk, k // a.shape[0])
    if any(kind == "col" for _, kind in outs):
        tn = min(tn, n // N_SHARD)
    assert m % tm == 0 and n % tn == 0 and k % tk == 0, (name, m, n, k, tm, tn, tk)
    nk = k // tk
    grid = (m // tm, n // tn, nk)
    if mode == "nn":
        a_spec = pl.BlockSpec((tm, tk), lambda i, j, kk: (i, kk))
        b_spec = _wspec(b_kind, b.shape, tk, tn, lambda i, j, kk: (kk, j))
        dims = (((1,), (0,)), ((), ()))
    elif mode == "nt":
        a_spec = _wspec(a_kind, a.shape, tm, tk, lambda i, j, kk: (i, kk))
        b_spec = _wspec(b_kind, b.shape, tn, tk, lambda i, j, kk: (j, kk))
        dims = (((1,), (1,)), ((), ()))
    else:
        a_spec = pl.BlockSpec((tk, tm), lambda i, j, kk: (kk, i))
        b_spec = _wspec(b_kind, b.shape, tk, tn, lambda i, j, kk: (kk, j))
        dims = (((0,), (0,)), ((), ()))
    out_shape, out_specs = [], []
    for dtype, kind in outs:
        if isinstance(kind, tuple):
            dil = kind[1]
            out_shape.append(jax.ShapeDtypeStruct((dil, m // dil, n), dtype))
            out_specs.append(pl.BlockSpec((dil, tm // dil, tn), lambda i, j, kk: (0, i, j)))
            continue
        shape = (m, n) if kind == "plain" else (N_SHARD, m, n // N_SHARD)
        out_shape.append(jax.ShapeDtypeStruct(shape, dtype))
        out_specs.append(_wspec(kind, shape, tm, tn, lambda i, j, kk: (i, j)))
    n_ex = len(extras)
    deps = [] if after is None else [after]
    ex_specs = {"ij": pl.BlockSpec((tm, tn), lambda i, j, kk: (i, j)),
                "vec": pl.BlockSpec((1, tn), lambda i, j, kk: (0, j)),
                "rows": pl.BlockSpec((tm, LANES), lambda i, j, kk: (i, 0))}
    out0 = 2 + n_ex + len(deps)

    def body(*refs):
        a_ref, b_ref = refs[0], refs[1]
        ex_refs = refs[2:2 + n_ex]
        out_refs = refs[out0:out0 + len(outs)]
        j = pl.program_id(1)

        def finish(res):
            if epilogue is None:
                out_refs[0][...] = res.astype(out_refs[0].dtype)
            elif stage:
                epilogue(res, ex_refs, out_refs, j, refs[-1])
            else:
                epilogue(res, ex_refs, out_refs, j)

        prod = lax.dot_general(a_ref[...], b_ref[...], dims, preferred_element_type=F32)
        if nk == 1:
            finish(prod)
            return
        acc_ref = refs[out0 + len(outs)]
        kk = pl.program_id(2)

        @pl.when(kk == 0)
        def _():
            acc_ref[...] = prod

        @pl.when(kk > 0)
        def _():
            acc_ref[...] += prod

        @pl.when(kk == nk - 1)
        def _():
            finish(acc_ref[...])

    res = pl.pallas_call(
        body, name=name, grid=grid,
        in_specs=[a_spec, b_spec] + [ex_specs[how] for _, how in extras] + [ANY] * len(deps),
        out_specs=out_specs, out_shape=out_shape,
        scratch_shapes=[pltpu.VMEM((tm, tn), F32)] * (nk > 1) + [pltpu.VMEM(_stage_shape(tm, tn), F32)] * bool(stage),
        compiler_params=_params("parallel", "parallel", "arbitrary"),
    )(a, b, *[e for e, _ in extras], *deps)
    return res


def _rope_tables(seq):
    half = ROT_DIM // 2
    pos = jnp.arange(seq, dtype=F32)
    inv = ROPE_THETA ** (-jnp.arange(0, ROT_DIM, 2, dtype=F32) / ROT_DIM)
    ang = pos[:, None] * inv[None, :]
    cos, sin = jnp.cos(ang), jnp.sin(ang)
    zeros = jnp.zeros((seq, HEAD_DIM - ROT_DIM), F32)
    ctab = jnp.concatenate([cos, cos, zeros + 1.0], axis=1)
    atab = jnp.concatenate([-sin, jnp.zeros((seq, half), F32), zeros], axis=1)
    btab = jnp.concatenate([jnp.zeros((seq, half), F32), sin, zeros], axis=1)
    return ctab, atab, btab


def _rope_apply(x, ctab, atab, btab, sign):
    w = x.shape[1]
    reps = w // HEAD_DIM
    half = ROT_DIM // 2
    c = jnp.tile(ctab, (1, reps))
    a = jnp.tile(atab, (1, reps))
    b = jnp.tile(btab, (1, reps))
    up = pltpu.roll(x, w - half, 1)
    down = pltpu.roll(x, half, 1)
    return x * c + sign * (up * a + down * b)


def _rows(t, w):
    return pl.BlockSpec((t, w), lambda i: (i, 0))


def _fixed(shape):
    nd = len(shape)
    return pl.BlockSpec(shape, lambda i: (0,) * nd)


def _behind(after):
    deps = [] if after is None else (list(after) if isinstance(after, (list, tuple)) else [after])
    return deps, [ANY] * len(deps)


def _rms_fwd(name, x, gains, after=None):
    s, d = x.shape
    t = min(ROW_TILE, s)
    ng = len(gains)
    deps, dep_specs = _behind(after)

    def body(*all_refs):
        x_ref, refs = all_refs[len(deps)], all_refs[len(deps) + 1:]
        xv = x_ref[...]
        r = lax.rsqrt(jnp.mean(xv * xv, axis=-1, keepdims=True) + NORM_EPS)
        xn = xv * r
        for g_ref, y_ref in zip(refs[:ng], refs[ng:]):
            y_ref[...] = (xn * g_ref[...]).astype(BF16)

    return pl.pallas_call(
        body, name=name, grid=(s // t,),
        in_specs=dep_specs + [_rows(t, d)] + [_fixed((1, d))] * ng,
        out_specs=[_rows(t, d)] * ng,
        out_shape=[jax.ShapeDtypeStruct((s, d), BF16)] * ng,
        compiler_params=_params("parallel"),
    )(*deps, x, *gains)


def _rms_bwd(name, x, pairs, dh_in, want_colsum=False, after=None):
    s, d = x.shape
    n_p = len(pairs)
    t = min(ROW_TILE // n_p, s)
    deps, dep_specs = _behind(after)

    def body(*all_refs):
        x_ref, dh_ref, refs = all_refs[len(deps)], all_refs[len(deps) + 1], all_refs[len(deps) + 2:]
        g_refs = refs[:n_p]
        dy_refs = refs[n_p:2 * n_p]
        dh_out, dhb_out = refs[2 * n_p], refs[2 * n_p + 1]
        dg_refs = refs[2 * n_p + 2:2 * n_p + 2 + n_p]
        cs_ref = refs[-1] if want_colsum else None
        i = pl.program_id(0)
        xv = x_ref[...]
        r = lax.rsqrt(jnp.mean(xv * xv, axis=-1, keepdims=True) + NORM_EPS)
        xn = xv * r
        dh = dh_ref[...]
        for g_ref, dy_ref, dg_ref in zip(g_refs, dy_refs, dg_refs):
            dy = dy_ref[...].astype(F32)
            u = dy * g_ref[...]
            dh = dh + r * (u - xn * jnp.mean(u * xn, axis=-1, keepdims=True))
            part = jnp.sum(dy * xn, axis=0, keepdims=True)

            @pl.when(i == 0)
            def _():
                dg_ref[...] = part

            @pl.when(i > 0)
            def _():
                dg_ref[...] += part

        dh_out[...] = dh
        dhb_out[...] = dh.astype(BF16)
        if want_colsum:
            col = jnp.sum(dh, axis=0, keepdims=True)

            @pl.when(i == 0)
            def _():
                cs_ref[...] = col

            @pl.when(i > 0)
            def _():
                cs_ref[...] += col

    n_vec = n_p + (1 if want_colsum else 0)
    return pl.pallas_call(
        body, name=name, grid=(s // t,),
        in_specs=dep_specs + [_rows(t, d), _rows(t, d)] + [_fixed((1, d))] * n_p + [_rows(t, d)] * n_p,
        out_specs=[_rows(t, d), _rows(t, d)] + [_fixed((1, d))] * n_vec,
        out_shape=[jax.ShapeDtypeStruct((s, d), F32), jax.ShapeDtypeStruct((s, d), BF16)]
        + [jax.ShapeDtypeStruct((1, d), F32)] * n_vec,
        compiler_params=_params("arbitrary"),
    )(*deps, x, dh_in, *[g for g, _ in pairs], *[dy for _, dy in pairs])


def _final_loss(x, g, target):
    s, d = x.shape
    t = min(ROW_TILE, s)

    def body(x_ref, g_ref, t_ref, dh_out, dhb_out, dg_ref, loss_ref):
        i = pl.program_id(0)
        xv = x_ref[...]
        gv = g_ref[...]
        r = lax.rsqrt(jnp.mean(xv * xv, axis=-1, keepdims=True) + NORM_EPS)
        xn = xv * r
        diff = xn * gv - t_ref[...]
        dy = diff / d
        u = dy * gv
        dh = r * (u - xn * jnp.mean(u * xn, axis=-1, keepdims=True))
        dh_out[...] = dh
        dhb_out[...] = dh.astype(BF16)
        dg = jnp.sum(dy * xn, axis=0, keepdims=True)
        lc = jnp.sum(0.5 * diff * dy, axis=0, keepdims=True)

        @pl.when(i == 0)
        def _():
            dg_ref[...] = dg
            loss_ref[...] = lc

        @pl.when(i > 0)
        def _():
            dg_ref[...] += dg
            loss_ref[...] += lc

    return pl.pallas_call(
        body, name="final_loss", grid=(s // t,),
        in_specs=[_rows(t, d), _fixed((1, d)), _rows(t, d)],
        out_specs=[_rows(t, d), _rows(t, d), _fixed((1, d)), _fixed((1, d))],
        out_shape=[jax.ShapeDtypeStruct((s, d), F32), jax.ShapeDtypeStruct((s, d), BF16),
                   jax.ShapeDtypeStruct((1, d), F32), jax.ShapeDtypeStruct((1, d), F32)],
        compiler_params=_params("arbitrary"),
    )(x, g, target)


def _ln_silu_fwd(c, g, b):
    s, d = c.shape
    t = min(ROW_TILE, s)

    def body(c_ref, g_ref, b_ref, s_ref):
        cv = c_ref[...]
        mu = jnp.mean(cv, axis=-1, keepdims=True)
        xc = cv - mu
        rs = lax.rsqrt(jnp.mean(xc * xc, axis=-1, keepdims=True) + LN_EPS)
        ln = xc * rs * g_ref[...] + b_ref[...]
        s_ref[...] = (ln * _sigmoid(ln)).astype(BF16)

    return pl.pallas_call(
        body, name="ln_silu_fwd", grid=(s // t,),
        in_specs=[_rows(t, d), _fixed((1, d)), _fixed((1, d))],
        out_specs=_rows(t, d), out_shape=jax.ShapeDtypeStruct((s, d), BF16),
        compiler_params=_params("parallel"),
    )(c, g, b)


def _ln_silu_bwd(c, g, b, ds, after=None):
    s, d = c.shape
    t = min(ROW_TILE, s)
    deps, dep_specs = _behind(after)

    def body(*all_refs):
        c_ref, g_ref, b_ref, ds_ref, dc_ref, dg_ref, db_ref, dbdw_ref = all_refs[len(deps):]
        i = pl.program_id(0)
        cv = c_ref[...]
        gv = g_ref[...]
        mu = jnp.mean(cv, axis=-1, keepdims=True)
        xc = cv - mu
        rs = lax.rsqrt(jnp.mean(xc * xc, axis=-1, keepdims=True) + LN_EPS)
        nrm = xc * rs
        ln = nrm * gv + b_ref[...]
        sig = _sigmoid(ln)
        dln = ds_ref[...].astype(F32) * sig * (1.0 + ln * (1.0 - sig))
        dn = dln * gv
        dc = rs * (dn - jnp.mean(dn, axis=-1, keepdims=True)
                   - nrm * jnp.mean(dn * nrm, axis=-1, keepdims=True))
        dc_ref[...] = dc
        pg = jnp.sum(dln * nrm, axis=0, keepdims=True)
        pb = jnp.sum(dln, axis=0, keepdims=True)
        pc = jnp.sum(dc, axis=0, keepdims=True)

        @pl.when(i == 0)
        def _():
            dg_ref[...] = pg
            db_ref[...] = pb
            dbdw_ref[...] = pc

        @pl.when(i > 0)
        def _():
            dg_ref[...] += pg
            db_ref[...] += pb
            dbdw_ref[...] += pc

    return pl.pallas_call(
        body, name="ln_silu_bwd", grid=(s // t,),
        in_specs=dep_specs + [_rows(t, d), _fixed((1, d)), _fixed((1, d)), _rows(t, d)],
        out_specs=[_rows(t, d)] + [_fixed((1, d))] * 3,
        out_shape=[jax.ShapeDtypeStruct((s, d), F32)] + [jax.ShapeDtypeStruct((1, d), F32)] * 3,
        compiler_params=_params("arbitrary"),
    )(*deps, c, g, b, ds)


def _residue_spec(dil, t, w):
    return pl.BlockSpec((dil, t // dil, w), lambda i: (0, i, 0))


def _attn_combine(o_list, lse_list):
    dil0, sd0, d = o_list[0].shape
    s = dil0 * sd0
    lw = lse_list[0].shape[2]
    group = d // HEAD_DIM // N_KV_HEADS
    t = min(ROW_TILE, s)
    nb = len(o_list)
    dils = [o.shape[0] for o in o_list]

    def body(*refs):
        o_out, l_out = refs[2 * nb], refs[2 * nb + 1]
        o_stage, l_stage = refs[2 * nb + 2:3 * nb + 2], refs[3 * nb + 2:]
        o_planes = [_from_residues(src, stage, dil) for src, stage, dil in zip(refs[:nb], o_stage, dils)]
        l_planes = [_from_residues(src, stage, dil) for src, stage, dil in zip(refs[nb:2 * nb], l_stage, dils)]
        for kh in range(N_KV_HEADS):
            ls = [plane(kh) for plane in l_planes]
            mx = ls[0]
            for l in ls[1:]:
                mx = jnp.maximum(mx, l)
            es = [jnp.exp(l - mx) for l in ls]
            den = es[0]
            for e in es[1:]:
                den = den + e
            l_out[:, kh * LANES:(kh + 1) * LANES] = mx + jnp.log(den)
            ws = [e / den for e in es]
            for g in range(group):
                h = kh * group + g
                acc = jnp.zeros((t, HEAD_DIM), F32)
                for plane, w in zip(o_planes, ws):
                    acc = acc + w[:, g:g + 1] * plane(h)
                o_out[:, h * HEAD_DIM:(h + 1) * HEAD_DIM] = acc.astype(BF16)

    return pl.pallas_call(
        body, name="attn_combine", grid=(s // t,),
        in_specs=[_residue_spec(dil, t, d) for dil in dils] + [_residue_spec(dil, t, lw) for dil in dils],
        out_specs=[_rows(t, d), _rows(t, lw)],
        out_shape=[jax.ShapeDtypeStruct((s, d), BF16), jax.ShapeDtypeStruct((s, lw), F32)],
        scratch_shapes=[pltpu.VMEM(_stage_shape(t, d), F32)] * nb + [pltpu.VMEM(_stage_shape(t, lw), F32)] * nb,
        compiler_params=_params("parallel"),
    )(*o_list, *lse_list)


def _attn_delta(do, o, lse, dils):
    s, d = o.shape
    lw = lse.shape[1]
    group = d // HEAD_DIM // N_KV_HEADS
    t = min(ROW_TILE, s)
    nd = len(dils)

    def body(do_ref, o_ref, lse_ref, *refs):
        stage = refs[-1]
        lane = lax.broadcasted_iota(I32, (t, LANES), 1)
        planes = []
        for kh in range(N_KV_HEADS):
            out = jnp.zeros((t, LANES), F32)
            for g in range(group):
                cols = slice((kh * group + g) * HEAD_DIM, (kh * group + g + 1) * HEAD_DIM)
                v = jnp.sum(do_ref[:, cols].astype(F32) * o_ref[:, cols].astype(F32), axis=-1, keepdims=True)
                out = jnp.where(lane == g, v, out)
            planes.append(out)
        _to_residues(lse_ref[...], stage, refs[:nd], dils)
        _to_residues(jnp.concatenate(planes, axis=1), stage, refs[nd:2 * nd], dils)

    res = pl.pallas_call(
        body, name="attn_delta", grid=(s // t,),
        in_specs=[_rows(t, d), _rows(t, d), _rows(t, lw)],
        out_specs=[_residue_spec(dil, t, lw) for dil in dils] * 2,
        out_shape=[jax.ShapeDtypeStruct((dil, s // dil, lw), F32) for dil in dils] * 2,
        scratch_shapes=[pltpu.VMEM(_stage_shape(t, lw), F32)],
        compiler_params=_params("parallel"),
    )(do, o, lse)
    return res[:nd], res[nd:]


def _residue_sum(name, groups, tabs):
    first = groups[0][0][0]
    s, w = first.shape[0] * first.shape[1], first.shape[2]
    t = min(ROW_TILE, s)
    flat = [p for parts, _ in groups for p in parts]

    def body(*refs):
        c_ref, a_ref, b_ref = refs[len(flat):len(flat) + 3]
        out = refs[len(flat) + 3]
        stages = refs[len(flat) + 4:]
        k = 0
        for gi, (parts, rotate) in enumerate(groups):
            planes = [_from_residues(refs[k + i], stages[k + i], p.shape[0]) for i, p in enumerate(parts)]
            k += len(parts)
            for c in range(w // LANES):
                tot = planes[0](c)
                for plane in planes[1:]:
                    tot = tot + plane(c)
                if rotate:
                    tot = _rope_apply(tot, c_ref[...], a_ref[...], b_ref[...], -1.0)
                out[:, gi * w + c * LANES:gi * w + (c + 1) * LANES] = tot.astype(BF16)

    return pl.pallas_call(
        body, name=name, grid=(s // t,),
        in_specs=[_residue_spec(p.shape[0], t, w) for p in flat] + [_rows(t, HEAD_DIM)] * 3,
        out_specs=_rows(t, len(groups) * w), out_shape=jax.ShapeDtypeStruct((s, len(groups) * w), BF16),
        scratch_shapes=[pltpu.VMEM(_stage_shape(t, w), F32) for _ in flat],
        compiler_params=_params("parallel"),
    )(*flat, *tabs)


def _dwconv_fwd(u, w_dw, b_dw, after=None):
    s, d2 = u.shape
    d = d2 // 2
    cb = min(CONV_CB, d)
    nblk = d // cb
    tt = min(CONV_T, s)
    deps, dep_specs = _behind(after)

    def body(*all_refs):
        ua_ref, ug_ref, w_ref, b_ref, c_ref, xp_ref = all_refs[len(deps):]
        gl =ua_ref[...].astype(F32) * _sigmoid(ug_ref[...].astype(F32))
        xp_ref[0:CONV_PAD, :] = jnp.zeros((CONV_PAD, cb), F32)
        xp_ref[CONV_PAD:, :] = gl
        wv = w_ref[...]
        bv = b_ref[...]
        for t0 in range(0, s, tt):
            acc = jnp.zeros((tt, cb), F32) + bv
            for kk in range(CONV_WIDTH):
                off = t0 + CONV_PAD - (CONV_WIDTH - 1) + kk
                acc = acc + wv[kk:kk + 1, :] * xp_ref[off:off + tt, :]
            c_ref[t0:t0 + tt, :] = acc

    return pl.pallas_call(
        body, name="dwconv_fwd", grid=(nblk,),
        in_specs=dep_specs + [pl.BlockSpec((s, cb), lambda j: (0, j)), pl.BlockSpec((s, cb), lambda j: (0, j + nblk)),
                              pl.BlockSpec((CONV_PAD, cb), lambda j: (0, j)), pl.BlockSpec((1, cb), lambda j: (0, j))],
        out_specs=pl.BlockSpec((s, cb), lambda j: (0, j)),
        out_shape=jax.ShapeDtypeStruct((s, d), F32),
        scratch_shapes=[pltpu.VMEM((s + CONV_PAD, cb), F32)],
        compiler_params=_params("parallel"),
    )(*deps, u, u, w_dw, b_dw)


def _dwconv_bwd(u, w_dw, dc):
    s, d2 = u.shape
    d = d2 // 2
    cb = min(CONV_CB, d)
    nblk = d // cb
    tt = min(CONV_T, s)

    def body(ua_ref, ug_ref, w_ref, dc_ref, du_ref, dw_ref, dba_ref, dbg_ref, glp_ref, dcp_ref, acc_ref):
        a = ua_ref[...].astype(F32)
        sig = _sigmoid(ug_ref[...].astype(F32))
        glp_ref[0:CONV_PAD, :] = jnp.zeros((CONV_PAD, cb), F32)
        glp_ref[CONV_PAD:, :] = a * sig
        dcp_ref[0:s, :] = dc_ref[...]
        dcp_ref[s:, :] = jnp.zeros((CONV_PAD, cb), F32)
        acc_ref[...] = jnp.zeros_like(acc_ref)
        wv = w_ref[...]
        dba = jnp.zeros((1, cb), F32)
        dbg = jnp.zeros((1, cb), F32)
        for t0 in range(0, s, tt):
            dgl = jnp.zeros((tt, cb), F32)
            dct = dc_ref[t0:t0 + tt, :]
            for kk in range(CONV_WIDTH):
                off = t0 + (CONV_WIDTH - 1) - kk
                dgl = dgl + wv[kk:kk + 1, :] * dcp_ref[off:off + tt, :]
                goff = t0 + CONV_PAD - (CONV_WIDTH - 1) + kk
                prod = dct * glp_ref[goff:goff + tt, :]
                acc_ref[8 * kk:8 * kk + 8, :] += jnp.sum(prod.reshape(tt // 8, 8, cb), axis=0)
            at = ua_ref[t0:t0 + tt, :].astype(F32)
            st = _sigmoid(ug_ref[t0:t0 + tt, :].astype(F32))
            da = dgl * st
            dg = dgl * at * st * (1.0 - st)
            du_ref[0, t0:t0 + tt, :] = da.astype(BF16)
            du_ref[1, t0:t0 + tt, :] = dg.astype(BF16)
            dba = dba + jnp.sum(da, axis=0, keepdims=True)
            dbg = dbg + jnp.sum(dg, axis=0, keepdims=True)
        dba_ref[...] = dba
        dbg_ref[...] = dbg
        for kk in range(CONV_WIDTH):
            dw_ref[kk:kk + 1, :] = jnp.sum(acc_ref[8 * kk:8 * kk + 8, :], axis=0, keepdims=True)
        dw_ref[CONV_WIDTH:, :] = jnp.zeros((CONV_PAD - CONV_WIDTH, cb), F32)

    blk = pl.BlockSpec((s, cb), lambda j: (0, j))
    vec = pl.BlockSpec((1, cb), lambda j: (0, j))
    return pl.pallas_call(
        body, name="dwconv_bwd", grid=(nblk,),
        in_specs=[blk, pl.BlockSpec((s, cb), lambda j: (0, j + nblk)),
                  pl.BlockSpec((CONV_PAD, cb), lambda j: (0, j)), blk],
        out_specs=[pl.BlockSpec((2, s, cb), lambda j: (0, 0, j)), pl.BlockSpec((CONV_PAD, cb), lambda j: (0, j)),
                   vec, vec],
        out_shape=[jax.ShapeDtypeStruct((2, s, d), BF16), jax.ShapeDtypeStruct((CONV_PAD, d), F32),
                   jax.ShapeDtypeStruct((1, d), F32), jax.ShapeDtypeStruct((1, d), F32)],
        scratch_shapes=[pltpu.VMEM((s + CONV_PAD, cb), F32), pltpu.VMEM((s + CONV_PAD, cb), F32),
                        pltpu.VMEM((8 * CONV_PAD, cb), F32)],
        compiler_params=_params("parallel"),
    )(u, u, w_dw, dc)


def _stack_heads(x, group):
    return jnp.concatenate([x[:, g * HEAD_DIM:(g + 1) * HEAD_DIM] for g in range(group)], axis=0)


def _unstack_heads(x, group):
    return jnp.concatenate([x[g * ATT_BLOCK:(g + 1) * ATT_BLOCK, :] for g in range(group)], axis=1)


def _stack_cols(x, group):
    return jnp.concatenate([x[:, g:g + 1] for g in range(group)], axis=0)


def _band_bias(group):
    rows = group * ATT_BLOCK
    row = lax.broadcasted_iota(I32, (rows, 2 * ATT_BLOCK), 0) % ATT_BLOCK
    col = lax.broadcasted_iota(I32, (rows, 2 * ATT_BLOCK), 1)
    band = jnp.where((col >= row) & (col <= row + ATT_BLOCK), 0.0, -jnp.inf).astype(F32)
    first = jnp.where(lax.broadcasted_iota(I32, (1, 2 * ATT_BLOCK), 1) >= ATT_BLOCK, 0.0, -jnp.inf).astype(F32)
    return band, first


def _masked_scores(qs, kw, band_ref, first_ref, nb, scale, wide):
    sc = lax.dot_general(qs, kw, (((1,), (1,)), ((), ())), preferred_element_type=F32) * scale
    if not wide:
        return sc + band_ref[:, ATT_BLOCK:]
    return sc + band_ref[...] + jnp.where(nb > 0, 0.0, first_ref[...])


def _window(ref, nb, wide):
    cur = pl.multiple_of(nb * ATT_BLOCK, ATT_BLOCK)
    if not wide:
        return ref[pl.ds(cur, ATT_BLOCK), :]
    prev = pl.multiple_of(jnp.maximum(nb - 1, 0) * ATT_BLOCK, ATT_BLOCK)
    return jnp.concatenate([ref[pl.ds(prev, ATT_BLOCK), :], ref[pl.ds(cur, ATT_BLOCK), :]], axis=0)


def _residues_per_step(dil, nblk):
    return max(1, min(dil, ATT_STEP_BLOCKS // nblk))


def _attn_fwd(name, q, kv):
    dil, sd, d = q.shape
    group = d // HEAD_DIM // N_KV_HEADS
    gw = group * HEAD_DIM
    nblk = sd // ATT_BLOCK
    scale = 1.0 / math.sqrt(HEAD_DIM)
    rb = _residues_per_step(dil, nblk)
    wide = nblk > 1

    def body(q_all, k_all, v_all, band_ref, first_ref, o_all, lse_all):
        lane = lax.broadcasted_iota(I32, (ATT_BLOCK, LANES), 1)
        for rr in range(rb):
            q_ref, k_ref, v_ref, o_ref, lse_ref = [ref.at[rr] for ref in (q_all, k_all, v_all, o_all, lse_all)]

            def step(nb, carry):
                rows = pl.ds(pl.multiple_of(nb * ATT_BLOCK, ATT_BLOCK), ATT_BLOCK)
                qs = _stack_heads(q_ref[rows, :], group)
                kw = _window(k_ref, nb, wide)
                vw = _window(v_ref, nb, wide)
                sc = _masked_scores(qs, kw, band_ref, first_ref, nb, scale, wide)
                mx = jnp.max(sc, axis=-1, keepdims=True)
                p = jnp.exp(sc - mx)
                l = jnp.sum(p, axis=-1, keepdims=True)
                o = jnp.dot(p.astype(BF16), vw, preferred_element_type=F32) / l
                o_ref[rows, :] = _unstack_heads(o, group).astype(BF16)
                lse = mx + jnp.log(l)
                out = jnp.zeros((ATT_BLOCK, LANES), F32)
                for g in range(group):
                    out = jnp.where(lane == g, lse[g * ATT_BLOCK:(g + 1) * ATT_BLOCK, :], out)
                lse_ref[rows, :] = out
                return carry

            lax.fori_loop(0, nblk, step, 0, unroll=min(2, nblk))

    kvh = N_KV_HEADS
    band, first = _band_bias(group)
    qspec = pl.BlockSpec((rb, sd, gw), lambda r, h: (r, 0, h))
    kspec = pl.BlockSpec((rb, sd, HEAD_DIM), lambda r, h: (r, 0, h))
    return pl.pallas_call(
        body, name=name, grid=(dil // rb, kvh),
        in_specs=[qspec, kspec, pl.BlockSpec((rb, sd, HEAD_DIM), lambda r, h: (r, 0, kvh + h)),
                  pl.BlockSpec(band.shape, lambda r, h: (0, 0)), pl.BlockSpec(first.shape, lambda r, h: (0, 0))],
        out_specs=[qspec, kspec],
        out_shape=[jax.ShapeDtypeStruct((dil, sd, d), BF16),
                   jax.ShapeDtypeStruct((dil, sd, kvh * LANES), F32)],
        compiler_params=_params("parallel", "parallel"),
    )(q, kv, kv, band, first)


def _attn_bwd(name, q, kv, do, lse, delta):
    dil, sd, d = q.shape
    group = d // HEAD_DIM // N_KV_HEADS
    gw = group * HEAD_DIM
    nblk = sd // ATT_BLOCK
    scale = 1.0 / math.sqrt(HEAD_DIM)
    nt = (((1,), (1,)), ((), ()))
    tn = (((0,), (0,)), ((), ()))

    rb = _residues_per_step(dil, nblk)
    wide = nblk > 1

    def body(q_all, k_all, v_all, do_all, lse_all, dl_all, band_ref, first_ref, dq_all, dk_all, dv_all, dk_accs,
             dv_accs):
        dk_accs[...] = jnp.zeros_like(dk_accs)
        dv_accs[...] = jnp.zeros_like(dv_accs)
        for rr in range(rb):
            q_ref, k_ref, v_ref, do_ref, lse_ref, dl_ref, dq_ref, dk_ref, dv_ref, dk_acc, dv_acc = [
                ref.at[rr] for ref in (q_all, k_all, v_all, do_all, lse_all, dl_all, dq_all, dk_all, dv_all,
                                       dk_accs, dv_accs)]

            def step(nb, carry):
                rows = pl.ds(pl.multiple_of(nb * ATT_BLOCK, ATT_BLOCK), ATT_BLOCK)
                qs = _stack_heads(q_ref[rows, :], group)
                dos = _stack_heads(do_ref[rows, :], group)
                ls = _stack_cols(lse_ref[rows, :], group)
                dl = _stack_cols(dl_ref[rows, :], group)
                kw = _window(k_ref, nb, wide)
                vw = _window(v_ref, nb, wide)
                p = jnp.exp(_masked_scores(qs, kw, band_ref, first_ref, nb, scale, wide) - ls)
                dp = lax.dot_general(dos, vw, nt, preferred_element_type=F32)
                ds = (p * (dp - dl) * scale).astype(BF16)
                dq = jnp.dot(ds, kw, preferred_element_type=F32)
                dq_ref[rows, :] = _unstack_heads(dq, group).astype(BF16)
                first_row = nb * ATT_BLOCK if wide else (nb + 1) * ATT_BLOCK
                win = pl.ds(pl.multiple_of(first_row, ATT_BLOCK), kw.shape[0])
                dk_acc[win, :] += lax.dot_general(ds, qs, tn, preferred_element_type=F32)
                dv_acc[win, :] += lax.dot_general(p.astype(BF16), dos, tn, preferred_element_type=F32)
                return carry

            lax.fori_loop(0, nblk, step, 0, unroll=min(2, nblk))
            dk_ref[...] = dk_acc[ATT_BLOCK:, :]
            dv_ref[...] = dv_acc[ATT_BLOCK:, :]

    kvh = N_KV_HEADS
    band, first = _band_bias(group)
    qspec = pl.BlockSpec((rb, sd, gw), lambda r, h: (r, 0, h))
    kspec = pl.BlockSpec((rb, sd, HEAD_DIM), lambda r, h: (r, 0, h))
    return pl.pallas_call(
        body, name=name, grid=(dil // rb, kvh),
        in_specs=[qspec, kspec, pl.BlockSpec((rb, sd, HEAD_DIM), lambda r, h: (r, 0, kvh + h)),
                  qspec, kspec, kspec,
                  pl.BlockSpec(band.shape, lambda r, h: (0, 0)), pl.BlockSpec(first.shape, lambda r, h: (0, 0))],
        out_specs=[qspec, kspec, kspec],
        out_shape=[jax.ShapeDtypeStruct((dil, sd, d), BF16),
                   jax.ShapeDtypeStruct((dil, sd, kvh * HEAD_DIM), F32),
                   jax.ShapeDtypeStruct((dil, sd, kvh * HEAD_DIM), F32)],
        scratch_shapes=[pltpu.VMEM((rb, sd + ATT_BLOCK, HEAD_DIM), F32)] * 2,
        compiler_params=_params("parallel", "parallel"),
    )(q, kv, kv, do, lse, delta, band, first)


def _cast_bf16(name, w, layer, place, after=None):
    _, r, c = w.shape
    tr = min(512, r)
    deps = [] if after is None else [after]

    def body(pl_ref, w_ref, *refs):
        refs[-1][...] = w_ref[...].astype(BF16)

    return pl.pallas_call(
        body, name=name,
        grid_spec=pltpu.PrefetchScalarGridSpec(
            num_scalar_prefetch=1, grid=(r // tr,),
            in_specs=[pl.BlockSpec((None, tr, c), lambda i, p: (layer, i, 0))] + [ANY] * len(deps),
            out_specs=pl.BlockSpec((None, tr, c), lambda i, p: (p[1], i, 0))),
        out_shape=jax.ShapeDtypeStruct((N_SHARD, r, c), BF16),
        compiler_params=_params("parallel"),
    )(place, w, *deps)


def _chip_sum(name, g, rh, place):
    _, r, c = g.shape
    rh2 = r // 2
    tr = min(512, rh2)
    nb = rh2 // tr

    def body(pl_ref, g_ref, rh_ref, o_ref):
        o_ref[...] = (g_ref[...].astype(F32) + rh_ref[...].astype(F32)).astype(BF16)

    return pl.pallas_call(
        body, name=name,
        grid_spec=pltpu.PrefetchScalarGridSpec(
            num_scalar_prefetch=1, grid=(N_SHARD, nb),
            in_specs=[pl.BlockSpec((None, tr, c), lambda s, i, p: (s, p[0] * nb + i, 0)),
                      pl.BlockSpec((None, tr, c), lambda s, i, p: (s, i, 0))],
            out_specs=pl.BlockSpec((None, tr, c), lambda s, i, p: (s, i, 0))),
        out_shape=jax.ShapeDtypeStruct((N_SHARD, rh2, c), BF16),
        compiler_params=_params("parallel", "parallel"),
    )(place, g, rh)


def _owner_sum(name, cs, rp, place):
    _, rh2, c = cs.shape
    tr = min(512, rh2)
    nb = rh2 // tr

    def body(pl_ref, cs_ref, r0_ref, r1_ref, r2_ref, o_ref):
        o_ref[...] = ((cs_ref[...].astype(F32) + r0_ref[...].astype(F32))
                      + (r1_ref[...].astype(F32) + r2_ref[...].astype(F32)))

    def rspec(j):
        return pl.BlockSpec((None, tr, c), lambda i, p: (j, i, 0))

    return pl.pallas_call(
        body, name=name,
        grid_spec=pltpu.PrefetchScalarGridSpec(
            num_scalar_prefetch=1, grid=(nb,),
            in_specs=[pl.BlockSpec((None, tr, c), lambda i, p: (p[1], i, 0)), rspec(0), rspec(1), rspec(2)],
            out_specs=pl.BlockSpec((tr, c), lambda i, p: (p[0] * nb + i, 0))),
        out_shape=jax.ShapeDtypeStruct((2 * rh2, c), F32),
        compiler_params=_params("parallel"),
    )(place, cs, rp, rp, rp)


def _adam_math(w, g, m, v):
    m = ADAM_B1 * m + (1.0 - ADAM_B1) * g
    v = ADAM_B2 * v + (1.0 - ADAM_B2) * (g * g)
    m_hat = m / (1.0 - ADAM_B1 ** ADAM_STEP)
    v_hat = v / (1.0 - ADAM_B2 ** ADAM_STEP)
    delta = -ADAM_LR * (m_hat / (jnp.sqrt(v_hat) + ADAM_EPS) + ADAM_WD * w)
    return delta, m, v


def _adamw(name, w, m, v, g, layer, partial=None):
    nl, r, c = w.shape
    tr = min(256, r)

    def body(w_ref, m_ref, v_ref, g_ref, *refs):
        go_ref, d_ref, mo_ref, vo_ref = refs[-4:]
        gv = g_ref[...]
        delta, m_new, v_new = _adam_math(w_ref[...], gv, m_ref[...], v_ref[...])
        go_ref[...] = gv
        d_ref[...] = delta
        mo_ref[...] = m_new
        vo_ref[...] = v_new

    wspec = pl.BlockSpec((None, tr, c), lambda i: (layer, i, 0))
    prev = [] if partial is None else list(partial)
    return pl.pallas_call(
        body, name=name, grid=(r // tr,),
        in_specs=[wspec] * 3 + [pl.BlockSpec((tr, c), lambda i: (i, 0))] + [ANY] * len(prev),
        out_specs=[wspec] * 4,
        out_shape=[jax.ShapeDtypeStruct((nl, r, c), F32)] * 4,
        input_output_aliases={4 + i: i for i in range(len(prev))},
        compiler_params=_params("parallel"),
    )(w, m, v, g, *prev)


def _adam_small(ws, ms, vs, gs):
    n = len(ws)

    def body(*refs):
        w_refs, m_refs, v_refs, g_refs = refs[:n], refs[n:2 * n], refs[2 * n:3 * n], refs[3 * n:4 * n]
        d_refs, mo_refs, vo_refs = refs[4 * n:5 * n], refs[5 * n:6 * n], refs[6 * n:7 * n]
        for i in range(n):
            delta, m_new, v_new = _adam_math(w_refs[i][...], g_refs[i][...], m_refs[i][...], v_refs[i][...])
            d_refs[i][...] = delta
            mo_refs[i][...] = m_new
            vo_refs[i][...] = v_new

    shapes = [jax.ShapeDtypeStruct(w.shape, F32) for w in ws]
    res = pl.pallas_call(body, name="adam_small", out_shape=shapes * 3)(*ws, *ms, *vs, *gs)
    return res[:n], res[n:2 * n], res[2 * n:]


def _pack_small(b_in, w_dw, b_dw, ln_g, ln_b, b_out, place):
    cin = b_in.shape[1]
    cd = b_dw.shape[1]
    rows = 8 + CONV_PAD

    def body(pl_ref, bi, wd, bd, lg, lb, bo, out):
        out[...] = jnp.zeros_like(out)
        out[0:1, :] = bi[...]
        out[1:2, 0:cd] = bd[...]
        out[1:2, cd:2 * cd] = lg[...]
        out[2:3, 0:cd] = lb[...]
        out[2:3, cd:2 * cd] = bo[...]
        out[8:8 + CONV_WIDTH, 0:cd] = wd[...]

    def whole(arr):
        return pl.BlockSpec(arr.shape, lambda i, p: (0,) * arr.ndim)

    ins = [b_in, w_dw, b_dw, ln_g, ln_b, b_out]
    return pl.pallas_call(
        body, name="pack_small",
        grid_spec=pltpu.PrefetchScalarGridSpec(
            num_scalar_prefetch=1, grid=(1,), in_specs=[whole(a) for a in ins],
            out_specs=pl.BlockSpec((None, rows, cin), lambda i, p: (p[1], 0, 0))),
        out_shape=jax.ShapeDtypeStruct((N_SHARD, rows, cin), F32),
        compiler_params=_params("arbitrary"),
    )(place, *ins)


def _place():
    x, y, c = lax.axis_index("x"), lax.axis_index("y"), lax.axis_index("c")
    return x, y, c


def _other_chips(x, y):
    return [(1 - x, y), (x, 1 - y), (1 - x, 1 - y)]


def _split_start_many(name, parts, after=None):
    flat = [b for bufs, _, _ in parts for b in bufs]
    n, n_parts = len(flat), len(parts)
    deps = [] if after is None else [after]

    def body(*refs):
        out0 = n + len(deps)
        pos = 0
        for i, (bufs, _, copies) in enumerate(parts):
            for cp in copies(refs[pos:pos + len(bufs)], refs[out0 + 2 * i], refs[out0 + 2 * i + 1], False):
                cp.start()
            pos += len(bufs)
        refs[-1][...] = jnp.zeros_like(refs[-1])

    sems = [pltpu.SemaphoreType.DMA((n_sem,)) for _, n_sem, _ in parts for _ in range(2)]
    res = pl.pallas_call(
        body, name=name,
        out_shape=(*sems, *[pltpu.HBM(b.shape, b.dtype) for b in flat], jax.ShapeDtypeStruct((8, LANES), F32)),
        in_specs=[HBM] * n + [ANY] * len(deps),
        out_specs=(*[SEM] * (2 * n_parts), *[HBM] * n, pl.BlockSpec(memory_space=pltpu.VMEM)),
        input_output_aliases={i: 2 * n_parts + i for i in range(n)},
        compiler_params=pltpu.CompilerParams(has_side_effects=SPLIT_EFFECT),
    )(*[pltpu.with_memory_space_constraint(b, pltpu.HBM) for b in flat], *deps)
    handles, pos = [], 2 * n_parts
    for i, (bufs, _, _) in enumerate(parts):
        handles.append((res[2 * i], res[2 * i + 1], list(res[pos:pos + len(bufs)]), res[-1]))
        pos += len(bufs)
    return handles


def _split_start(name, bufs, n_sem, copies, after=None):
    return _split_start_many(name, [(bufs, n_sem, copies)], after)[0]


def _split_wait(name, handle, copies, after):
    ssem, rsem, bufs, _ = handle
    n = len(bufs)
    deps = list(after) if isinstance(after, (list, tuple)) else [after]

    def body(*refs):
        for cp in copies(refs[:n], refs[n], refs[n + 1], True):
            cp.wait_send()
            cp.wait_recv()

    res = pl.pallas_call(
        body, name=name,
        out_shape=[pltpu.HBM(b.shape, b.dtype) for b in bufs],
        in_specs=[HBM] * n + [SEM, SEM] + [ANY] * len(deps), out_specs=[HBM] * n,
        input_output_aliases={i: i for i in range(n)},
        compiler_params=pltpu.CompilerParams(has_side_effects=SPLIT_EFFECT),
    )(*bufs, ssem, rsem, *deps)
    return list(res)


def _remote(src, dst, ssem, rsem, k, to):
    return pltpu.make_async_remote_copy(src_ref=src, dst_ref=dst, send_sem=ssem.at[k], recv_sem=rsem.at[k],
                                        device_id=to, device_id_type=MESH)


def _gather_chips(x, y, c):
    nx, ny = x + (1 - c) - 2 * x * (1 - c), y + c - 2 * y * c
    fx, fy = x + c - 2 * x * c, y + (1 - c) - 2 * y * (1 - c)
    return (nx, ny), (fx, fy), 2 * nx + ny, 2 * fx + fy, 2 * (1 - x) + (1 - y)


def _direct_copies(refs, ssem, rsem, landing, n_whole=0):
    x, y, c = _place()
    me = 2 * x + y
    (nx, ny), _, near, _, _ = _gather_chips(x, y, c)
    n = len(refs) - n_whole
    cps = []
    for a, ref in enumerate(refs[:n]):
        cps.append(_remote(ref.at[me], ref.at[near if landing else me], ssem, rsem, a, (nx, ny, c)))
    for b, ref in enumerate(refs[n:]):
        for j, (px, py) in enumerate(_other_chips(x, y)):
            cps.append(_remote(ref.at[me], ref.at[2 * px + py if landing else me], ssem, rsem, n + 3 * b + j,
                               (px, py, c)))
    return cps


def _relay_copies(refs, ssem, rsem, landing):
    x, y, c = _place()
    _, (fx, fy), near, far, diag = _gather_chips(x, y, c)
    n = len(refs)
    cps = []
    for a, ref in enumerate(refs):
        rh = ref.shape[1] // 2
        rows = pl.ds(c * rh, rh)
        cps.append(_remote(ref.at[near, rows], ref.at[diag if landing else near, rows], ssem, rsem, a, (fx, fy, c)))
        cps.append(_remote(ref.at[near], ref.at[far if landing else near], ssem, rsem, n + a, (x, y, 1 - c)))
    return cps


def _diagonal_copies(refs, ssem, rsem, landing):
    x, y, c = _place()
    diag = 2 * (1 - x) + (1 - y)
    who = 1 - c if landing else c
    cps = []
    for a, ref in enumerate(refs):
        rh = ref.shape[1] // 2
        piece = ref.at[diag, pl.ds(who * rh, rh)]
        cps.append(_remote(piece, piece, ssem, rsem, a, (x, y, 1 - c)))
    return cps


def _sibling_copies(refs, ssem, rsem, landing):
    x, y, c = _place()
    n = len(refs) // 2
    cps = []
    for a in range(n):
        rh = refs[a].shape[1] // 2
        cps.append(_remote(refs[a].at[:, pl.ds((1 - c) * rh, rh), :], refs[n + a], ssem, rsem, a, (x, y, 1 - c)))
    return cps


def _owner_copies(refs, ssem, rsem, landing):
    x, y, c = _place()
    n = len(refs) // 2
    cps = []
    for a in range(n):
        for j, (px, py) in enumerate(_other_chips(x, y)):
            cps.append(_remote(refs[a].at[2 * px + py], refs[n + a].at[j], ssem, rsem, 3 * a + j, (px, py, c)))
    return cps


def _swap_copies(refs, ssem, rsem, landing):
    x, y, c = _place()
    who = 1 - c if landing else c
    cps = []
    for a, ref in enumerate(refs):
        rh = ref.shape[0] // 2
        rows = ref.at[pl.ds(who * rh, rh)]
        cps.append(_remote(rows, rows, ssem, rsem, a, (x, y, 1 - c)))
    return cps


def _small_copies(refs, ssem, rsem, landing):
    pack, slots = refs
    x, y, c = _place()
    cps = []
    for rel in range(1, N_DEV):
        px = 1 - x if (rel >> 2) & 1 else x
        py = 1 - y if (rel >> 1) & 1 else y
        pc = 1 - c if rel & 1 else c
        slot = 4 * px + 2 * py + pc if landing else 4 * x + 2 * y + c
        cps.append(_remote(pack, slots.at[slot], ssem, rsem, rel - 1, (px, py, pc)))
    return cps


def _small_pack(rows, w_dw_grad, d):
    n = len(rows)

    def body(*refs):
        pack = refs[-1]
        pack[...] = jnp.zeros_like(pack)
        for (r, _), ref in zip(rows, refs[:n]):
            pack[r:r + 1, :] = ref[...]
        pack[16:16 + CONV_PAD, :] = refs[n][...]

    return pl.pallas_call(body, name="small_pack", out_shape=jax.ShapeDtypeStruct((SMALL_ROWS, d), F32))(
        *[v for _, v in rows], w_dw_grad)


def _small_sum(pack, slots, place):
    rows, d = pack.shape
    loss_row = 12

    def body(pl_ref, pack_ref, slots_ref, out_ref):
        me = pl_ref[2]
        tot = jnp.where(me == 0, pack_ref[...], slots_ref[0])
        for i in range(1, N_DEV):
            tot = tot + jnp.where(me == i, pack_ref[...], slots_ref[i])
        out_ref[...] = tot
        out_ref[loss_row:loss_row + 1, :] = jnp.zeros((1, d), F32) + jnp.sum(tot[loss_row:loss_row + 1, :])

    return pl.pallas_call(
        body, name="small_sum",
        grid_spec=pltpu.PrefetchScalarGridSpec(
            num_scalar_prefetch=1, grid=(1,),
            in_specs=[pl.BlockSpec((rows, d), lambda i, p: (0, 0)), pl.BlockSpec((N_DEV, rows, d), lambda i, p: (0, 0, 0))],
            out_specs=pl.BlockSpec((rows, d), lambda i, p: (0, 0))),
        out_shape=jax.ShapeDtypeStruct((rows, d), F32),
        compiler_params=_params("arbitrary"),
    )(place, pack, slots)


def kernel(x, norm_mix, norm_mlp, conv_w_in, conv_b_in, conv_w_dw, conv_b_dw, conv_ln_g, conv_ln_b, conv_w_out, conv_b_out, kv_norm, w_kv, attn_w_q, attn_w_o, mlp_w_in, mlp_w_out, final_norm, loss_target, m_norm_mix, m_norm_mlp, m_conv_w_in, m_conv_b_in, m_conv_w_dw, m_conv_b_dw, m_conv_ln_g, m_conv_ln_b, m_conv_w_out, m_conv_b_out, m_kv_norm, m_w_kv, m_attn_w_q, m_attn_w_o, m_mlp_w_in, m_mlp_w_out, m_final_norm, v_norm_mix, v_norm_mlp, v_conv_w_in, v_conv_b_in, v_conv_w_dw, v_conv_b_dw, v_conv_ln_g, v_conv_ln_b, v_conv_w_out, v_conv_b_out, v_kv_norm, v_w_kv, v_attn_w_q, v_attn_w_o, v_mlp_w_in, v_mlp_w_out, v_final_norm):
    _, s, d = x.shape
    dff = mlp_w_in.shape[2] * N_SHARD
    kvw = w_kv.shape[1]
    ds4 = d // N_SHARD
    xi, yi, ci = _place()
    me = 2 * xi + yi
    place = jnp.stack([ci, me, 2 * me + ci]).astype(I32)

    h0 = x.reshape(s, d)
    target = loss_target.reshape(s, d)
    tabs = _rope_tables(s)

    def gather_begin(tag, bufs, n_whole=0):
        plan = functools.partial(_direct_copies, n_whole=n_whole)
        return _split_start(f"gather_start_{tag}", bufs, len(bufs) + 2 * n_whole, plan), plan, n_whole

    def gather_step(later, land=None, swap=None):
        parts, names, whole = [], [], {}
        if land is not None:
            tag, (handle, plan, n_whole) = land
            bufs = _split_wait(f"gather_wait_{tag}", handle, plan, later)
            n = len(bufs) - n_whole
            parts.append((bufs[:n], 2 * n, _relay_copies))
            whole["land"] = bufs[n:]
            names.append(f"relay_{tag}")
        if swap is not None:
            tag, (relayed, whole["swap"]) = swap
            bufs = _split_wait(f"relay_wait_{tag}", relayed, _relay_copies, later)
            parts.append((bufs, len(bufs), _diagonal_copies))
            names.append(f"diagonal_{tag}")
        handles = _split_start_many("start_" + "_".join(names), parts)
        landed = (handles[0], whole["land"]) if land is not None else None
        swapped = (handles[-1], whole["swap"]) if swap is not None else None
        return landed, swapped

    def gather_land(tag, begun, later):
        return gather_step(later, land=(tag, begun))[0]

    def gather_swap(tag, landed, later):
        return gather_step(later, swap=(tag, landed))[1]

    def gather_end(tag, swapped, later):
        handle, whole = swapped
        return _split_wait(f"diagonal_wait_{tag}", handle, _diagonal_copies, later) + whole

    ag_cin = gather_begin("conv_in", [
        _cast_bf16("cast_w_in", conv_w_in, 0, place),
        _pack_small(conv_b_in, conv_w_dw.reshape(CONV_WIDTH, ds4), conv_b_dw, conv_ln_g, conv_ln_b, conv_b_out, place),
    ], n_whole=1)
    ag_cout = gather_begin("conv_out", [_cast_bf16("cast_w_out", conv_w_out, 0, place, ag_cin[0][3])])
    ag_mi0 = gather_begin("mlp_in0", [_cast_bf16("cast_mlp_in0", mlp_w_in, 0, place, ag_cout[0][3])])
    ag_mo0 = gather_begin("mlp_out0", [_cast_bf16("cast_mlp_out0", mlp_w_out, 0, place, ag_mi0[0][3])])
    nm = [norm_mix[0:1], norm_mix[1:2]]
    nmlp = [norm_mlp[0:1], norm_mlp[1:2]]
    kvn = kv_norm.reshape(1, d)
    fin = final_norm.reshape(1, d)
    (y0,) = _rms_fwd("rms_mix0", h0, [nm[0]], after=ag_mo0[0][3])
    land_cin = gather_land("conv_in", ag_cin, y0)
    ag_attn = gather_begin("attn", [
        _cast_bf16("cast_w_kv", w_kv.reshape(1, ds4, kvw), 0, place, land_cin[0][3]),
        _cast_bf16("cast_w_q", attn_w_q, 0, place), _cast_bf16("cast_w_o", attn_w_o, 0, place)])
    ag_mi1 = gather_begin("mlp_in1", [_cast_bf16("cast_mlp_in1", mlp_w_in, 1, place, ag_attn[0][3])])
    ag_mo1 = gather_begin("mlp_out1", [_cast_bf16("cast_mlp_out1", mlp_w_out, 1, place, ag_mi1[0][3])])
    land_cout, swap_cin = gather_step(ag_mo1[0][3], land=("conv_out", ag_cout), swap=("conv_in", land_cin))

    wmi_g = [None, None]
    wmo_f = [None, None]

    w_in_g, small_g = gather_end("conv_in", swap_cin, swap_cin[0][3])
    b_in_f = small_g[:, 0, :].reshape(1, 2 * d)
    b_dw_f = small_g[:, 1, 0:ds4].reshape(1, d)
    ln_g_f = small_g[:, 1, ds4:2 * ds4].reshape(1, d)
    ln_b_f = small_g[:, 2, 0:ds4].reshape(1, d)
    b_out_f = small_g[:, 2, ds4:2 * ds4].reshape(1, d)
    w_dw_f = jnp.transpose(small_g[:, 8:8 + CONV_PAD, 0:ds4], (1, 0, 2)).reshape(CONV_PAD, d)

    def ep_bias(acc, ex, outs, j):
        outs[0][...] = (acc + ex[0][...]).astype(outs[0].dtype)

    def ep_residual(acc, ex, outs, j):
        outs[0][...] = ex[0][...] + acc

    def ep_residual_bias(acc, ex, outs, j):
        outs[0][...] = ex[0][...] + (acc + ex[1][...])

    def ep_relu2(acc, ex, outs, j):
        r = jnp.maximum(acc, 0.0)
        outs[0][...] = r.astype(BF16)
        outs[1][...] = (r * r).astype(BF16)

    by_residue = [(BF16, ("residues", dil)) for dil in DILATIONS]

    def put_by_residue(val, outs, stage):
        _to_residues(val, stage, outs, DILATIONS)

    def ep_rope(acc, ex, outs, j, stage):
        put_by_residue(_rope_apply(acc, ex[0][...], ex[1][...], ex[2][...], 1.0), outs, stage)

    def ep_rope_k(acc, ex, outs, j, stage):
        roped = _rope_apply(acc, ex[0][...], ex[1][...], ex[2][...], 1.0)
        put_by_residue(jnp.where(j == 0, roped, acc), outs, stage)

    def ep_by_residue(acc, ex, outs, j, stage):
        put_by_residue(acc, outs, stage)

    tab_extras = [(t, "rows") for t in tabs]

    def mlp_fwd(idx, h, y, out_weight):
        r, r2 = _matmul(f"mlp_in{idx}", "nn", y, wmi_g[idx], b_kind="col", m=s, n=dff, k=d,
                        outs=[(BF16, "plain"), (BF16, "plain")], epilogue=ep_relu2)
        wmo_f[idx] = out_weight(r2).reshape(dff, d)
        (h_new,) = _matmul(f"mlp_out{idx}", "nn", r2, wmo_f[idx], m=s, n=d, k=dff,
                           outs=[(F32, "plain")], extras=[(h, "ij")], epilogue=ep_residual)
        return h_new, r, r2

    (u,) = _matmul("conv_in", "nn", y0, w_in_g, b_kind="col", m=s, n=2 * d, k=d,
                   outs=[(BF16, "plain")], extras=[(b_in_f, "vec")], epilogue=ep_bias)
    land_mi0, swap_cout = gather_step(u, land=("mlp_in0", ag_mi0), swap=("conv_out", land_cout))
    cpre = _dwconv_fwd(u, w_dw_f, b_dw_f, after=swap_cout[0][3])
    sact = _ln_silu_fwd(cpre, ln_g_f, ln_b_f)
    (w_out_g,) = gather_end("conv_out", swap_cout, sact)
    w_out_f = w_out_g.reshape(d, d)
    (h1,) = _matmul("conv_out", "nn", sact, w_out_f, m=s, n=d, k=d,
                    outs=[(F32, "plain")], extras=[(h0, "ij"), (b_out_f, "vec")], epilogue=ep_residual_bias)
    swap_mi0 = gather_swap("mlp_in0", land_mi0, h1)
    (y1,) = _rms_fwd("rms_mlp0", h1, [nmlp[0]], after=swap_mi0[0][3])
    land_mo0 = gather_land("mlp_out0", ag_mo0, y1)
    (wmi_g[0],) = gather_end("mlp_in0", swap_mi0, land_mo0[0][3])
    land_attn = None

    def out_weight0(r2):
        nonlocal land_attn
        land_attn, swap_mo0 = gather_step(r2, land=("attn", ag_attn), swap=("mlp_out0", land_mo0))
        return gather_end("mlp_out0", swap_mo0, swap_mo0[0][3])[0]

    h2, r0, r0sq = mlp_fwd(0, h1, y1, out_weight0)
    land_mi1, swap_attn = gather_step(h2, land=("mlp_in1", ag_mi1), swap=("attn", land_attn))
    ykv, y2 = _rms_fwd("rms_kv_mix1", h2, [kvn, nm[1]], after=land_mi1[0][3])
    wkv_g, wq_g, wo_g = gather_end("attn", swap_attn, y2)
    wkv_f, wq_f, wo_f = wkv_g.reshape(d, kvw), wq_g.reshape(d, d), wo_g.reshape(d, d)
    kv_parts = _matmul("kv_proj", "nn", ykv, wkv_f, m=s, n=kvw, k=d, tn=kvw // 2,
                       outs=by_residue, extras=tab_extras, epilogue=ep_rope_k, stage=True)
    q_parts = _matmul("q_proj", "nn", y2, wq_f, m=s, n=d, k=d,
                      outs=by_residue, extras=tab_extras, epilogue=ep_rope, stage=True)
    o_parts, lse_parts = [], []
    for dil, q_b, kv_b in zip(DILATIONS, q_parts, kv_parts):
        o_b, lse_b = _attn_fwd(f"attn_fwd_d{dil}", q_b, kv_b)
        o_parts.append(o_b)
        lse_parts.append(lse_b)
    o, lse = _attn_combine(o_parts, lse_parts)
    land_mo1, swap_mi1 = gather_step(o, land=("mlp_out1", ag_mo1), swap=("mlp_in1", land_mi1))
    (h3,) = _matmul("attn_out", "nn", o, wo_f, m=s, n=d, k=d,
                    outs=[(F32, "plain")], extras=[(h2, "ij")], epilogue=ep_residual)
    (y3,) = _rms_fwd("rms_mlp1", h3, [nmlp[1]], after=land_mo1[0][3])
    (wmi_g[1],) = gather_end("mlp_in1", swap_mi1, y3)

    def out_weight1(r2):
        swap_mo1 = gather_swap("mlp_out1", land_mo1, r2)
        return gather_end("mlp_out1", swap_mo1, swap_mo1[0][3])[0]

    h4, r1, r1sq = mlp_fwd(1, h3, y3, out_weight1)
    dh4, dh4b, d_fin, loss_cols = _final_loss(h4, fin, target)

    def ep_relu2_bwd(acc, ex, outs, j):
        outs[0][...] = (acc * (2.0 * ex[0][...].astype(F32))).astype(BF16)

    def mlp_bwd(idx, dhb, y, r, r2):
        (dz,) = _matmul(f"mlp_out{idx}_dx", "nt", dhb, wmo_f[idx], m=s, n=dff, k=d,
                        outs=[(BF16, "plain")], extras=[(r, "ij")], epilogue=ep_relu2_bwd)
        (dwo,) = _matmul(f"mlp_out{idx}_dw", "tn", r2, dhb, m=dff, n=d, k=s,
                         outs=[(BF16, "plain")])
        (dy,) = _matmul(f"mlp_in{idx}_dx", "nt", dz, wmi_g[idx], b_kind="col", m=s, n=d, k=dff,
                        outs=[(BF16, "plain")])
        (dwi,) = _matmul(f"mlp_in{idx}_dw", "tn", y, dz, m=d, n=dff, k=s,
                         outs=[(BF16, "col")])
        return dy, dwi, dwo.reshape(N_SHARD, dff // N_SHARD, d)

    def rs_exchange(tag, grads):
        lands = [lax.empty((N_SHARD, g.shape[1] // 2, g.shape[2]), g.dtype) for g in grads]
        return _split_start(f"sibling_start_{tag}", list(grads) + lands, len(grads), _sibling_copies)

    def rs_send(tag, names, exchanged, later):
        bufs = _split_wait(f"sibling_wait_{tag}", exchanged, _sibling_copies, later)
        n = len(names)
        sums = [_chip_sum(f"chip_sum_{nme}", g, rh, place) for nme, g, rh in zip(names, bufs[:n], bufs[n:])]
        lands = [lax.empty((N_SHARD - 1,) + cs.shape[1:], cs.dtype) for cs in sums]
        return _split_start(f"owners_start_{tag}", sums + lands, 3 * n, _owner_copies)

    def rs_sum(tag, names, sent, later):
        bufs = _split_wait(f"owners_wait_{tag}", sent, _owner_copies, later)
        n = len(names)
        own = [_owner_sum(f"owner_sum_{nme}", cs, rp, place) for nme, cs, rp in zip(names, bufs[:n], bufs[n:])]
        return _split_start(f"swap_start_{tag}", own, n, _swap_copies)

    def rs_end(tag, swapped, later):
        return _split_wait(f"swap_wait_{tag}", swapped, _swap_copies, later)

    dy3, g_wmi1, g_wmo1 = mlp_bwd(1, dh4b, y3, r1, r1sq)
    x_mlp1 = rs_exchange("mlp1", [g_wmi1, g_wmo1])
    dh3, dh3b, d_nmlp1 = _rms_bwd("rms_mlp1_bwd", h3, [(nmlp[1], dy3)], dh4, after=x_mlp1[3])

    do_parts = _matmul("attn_out_dx", "nt", dh3b, wo_f, m=s, n=d, k=d, outs=by_residue, epilogue=ep_by_residue,
                       stage=True)
    (g_wo,) = _matmul("attn_out_dw", "tn", o, dh3b, m=d, n=d, k=s, outs=[(BF16, "plain")])
    rs_mlp1 = rs_send("mlp1", ["mlp_in1", "mlp_out1"], x_mlp1, g_wo)
    lse_res, delta_res = _attn_delta(do_parts[0].reshape(s, d), o, lse, DILATIONS)
    dq_parts, dk_parts, dv_parts = [], [], []
    for dil, q_b, kv_b, do_b, lse_b, dl_b in zip(DILATIONS, q_parts, kv_parts, do_parts, lse_res, delta_res):
        dq_b, dk_b, dv_b = _attn_bwd(f"attn_bwd_d{dil}", q_b, kv_b, do_b, lse_b, dl_b)
        dq_parts.append(dq_b)
        dk_parts.append(dk_b)
        dv_parts.append(dv_b)
    dq = _residue_sum("rope_bwd_q", [(dq_parts, True)], tabs)
    dkv = _residue_sum("rope_bwd_kv", [(dk_parts, True), (dv_parts, False)], tabs)
    (g_wq,) = _matmul("q_proj_dw", "tn", y2, dq, m=d, n=d, k=s, outs=[(BF16, "plain")])
    (dy2,) = _matmul("q_proj_dx", "nt", dq, wq_f, m=s, n=d, k=d, outs=[(BF16, "plain")])
    (g_wkv,) = _matmul("kv_proj_dw", "tn", ykv, dkv, m=d, n=kvw, k=s, outs=[(BF16, "plain")])
    (dykv,) = _matmul("kv_proj_dx", "nt", dkv, wkv_f, m=s, n=d, k=kvw, outs=[(BF16, "plain")])
    x_attn = rs_exchange("attn", [g_wkv.reshape(N_SHARD, ds4, kvw), g_wq.reshape(N_SHARD, ds4, d),
                                  g_wo.reshape(N_SHARD, ds4, d)])
    dh2, dh2b, d_nm1, d_kvn = _rms_bwd("rms_kv_mix1_bwd", h2, [(nm[1], dy2), (kvn, dykv)], dh3, after=x_attn[3])
    rs_attn = rs_send("attn", ["w_kv", "w_q", "w_o"], x_attn, dh2b)

    dy1, g_wmi0, g_wmo0 = mlp_bwd(0, dh2b, y1, r0, r0sq)
    x_mlp0 = rs_exchange("mlp0", [g_wmi0, g_wmo0])
    dh1, dh1b, d_nmlp0, d_b_out = _rms_bwd("rms_mlp0_bwd", h1, [(nmlp[0], dy1)], dh2, want_colsum=True,
                                           after=[x_mlp0[3], rs_attn[3]])

    (dsact,) = _matmul("conv_out_dx", "nt", dh1b, w_out_f, m=s, n=d, k=d, outs=[(BF16, "plain")])
    (g_wout,) = _matmul("conv_out_dw", "tn", sact, dh1b, m=d, n=d, k=s, outs=[(BF16, "plain")])
    rs_mlp0 = rs_send("mlp0", ["mlp_in0", "mlp_out0"], x_mlp0, g_wout)
    dc, d_ln_g, d_ln_b, d_b_dw = _ln_silu_bwd(cpre, ln_g_f, ln_b_f, dsact, after=rs_mlp0[3])
    du, d_w_dw, d_b_in_a, d_b_in_g = _dwconv_bwd(u, w_dw_f, dc)
    (g_win,) = _matmul("conv_in_dw", "tn", y0, du, b_kind="col", m=d, n=2 * d, k=s, outs=[(BF16, "col")])
    x_conv = rs_exchange("conv", [g_win, g_wout.reshape(N_SHARD, ds4, d)])
    (dy0,) = _matmul("conv_in_dx", "nt", du, w_in_g, a_kind="col", b_kind="col", m=s, n=d, k=2 * d,
                     outs=[(BF16, "plain")], after=x_conv[3])
    rs_conv = rs_send("conv", ["w_in", "w_out"], x_conv, dy0)
    dx, _, d_nm0 = _rms_bwd("rms_mix0_bwd", h0, [(nm[0], dy0)], dh1, after=rs_conv[3])

    small_rows = [(0, d_nm0), (1, d_nm1), (2, d_nmlp0), (3, d_nmlp1), (4, d_kvn), (5, d_fin), (6, d_b_dw),
                  (7, d_ln_g), (8, d_ln_b), (9, d_b_out), (10, d_b_in_a), (11, d_b_in_g), (12, loss_cols)]
    x_small = _split_start("small_start", [_small_pack(small_rows, d_w_dw, d),
                                           lax.empty((N_DEV, SMALL_ROWS, d), F32)], N_DEV - 1, _small_copies)

    def big(name, w, m, v, g, layer=0, partial=None):
        shape = w.shape
        w3, m3, v3 = [t.reshape((-1,) + shape[-2:]) for t in (w, m, v)]
        if partial is not None:
            partial = [t.reshape(w3.shape) for t in partial]
        res = _adamw(name, w3, m3, v3, g, layer, partial)
        return [t.reshape(shape) for t in res]

    sw_mlp1 = rs_sum("mlp1", ["mlp_in1", "mlp_out1"], rs_mlp1, x_small[3])
    sw_attn = rs_sum("attn", ["w_kv", "w_q", "w_o"], rs_attn, sw_mlp1[3])
    f_wmi1, f_wmo1 = rs_end("mlp1", sw_mlp1, sw_attn[3])
    p_wmi = big("adam_mlp_in1", mlp_w_in, m_mlp_w_in, v_mlp_w_in, f_wmi1, 1)
    p_wmo = big("adam_mlp_out1", mlp_w_out, m_mlp_w_out, v_mlp_w_out, f_wmo1, 1)
    sw_mlp0 = rs_sum("mlp0", ["mlp_in0", "mlp_out0"], rs_mlp0, [p_wmi[0], p_wmo[0]])
    f_wkv, f_wq, f_wo = rs_end("attn", sw_attn, sw_mlp0[3])
    r_wkv = big("adam_w_kv", w_kv, m_w_kv, v_w_kv, f_wkv)
    r_wq = big("adam_w_q", attn_w_q, m_attn_w_q, v_attn_w_q, f_wq)
    r_wo = big("adam_w_o", attn_w_o, m_attn_w_o, v_attn_w_o, f_wo)
    sw_conv = rs_sum("conv", ["w_in", "w_out"], rs_conv, [r_wkv[0], r_wq[0], r_wo[0]])
    f_wmi0, f_wmo0 = rs_end("mlp0", sw_mlp0, sw_conv[3])
    r_wmi = big("adam_mlp_in0", mlp_w_in, m_mlp_w_in, v_mlp_w_in, f_wmi0, 0, p_wmi)
    r_wmo = big("adam_mlp_out0", mlp_w_out, m_mlp_w_out, v_mlp_w_out, f_wmo0, 0, p_wmo)
    f_win, f_wout = rs_end("conv", sw_conv, [r_wmi[0], r_wmo[0]])
    r_win = big("adam_w_in", conv_w_in, m_conv_w_in, v_conv_w_in, f_win)
    r_wout = big("adam_w_out", conv_w_out, m_conv_w_out, v_conv_w_out, f_wout)

    small_pack, small_slots = _split_wait("small_wait", x_small, _small_copies, r_wout[0])
    red = _small_sum(small_pack, small_slots, place)
    loss = red[12, 0]
    g_norm_mix = red[0:2]
    g_norm_mlp = red[2:4]
    g_kv_norm = red[4:5]
    g_final = red[5:6]

    def my_cols(row):
        return lax.dynamic_slice(red, (row, me * ds4), (1, ds4))

    g_b_dw, g_ln_g, g_ln_b, g_b_out = my_cols(6), my_cols(7), my_cols(8), my_cols(9)
    half_in = 2 * d // N_SHARD
    b_in_row = 10 + me // 2
    g_b_in = lax.dynamic_slice(red, (b_in_row, (me % 2) * half_in), (1, half_in))
    g_w_dw = lax.dynamic_slice(red, (16, me * ds4), (CONV_WIDTH, ds4))

    sm_w =[norm_mix, norm_mlp, conv_b_in, conv_w_dw.reshape(CONV_WIDTH, ds4), conv_b_dw, conv_ln_g, conv_ln_b,
            conv_b_out, kv_norm.reshape(1, d), final_norm.reshape(1, d)]
    sm_m = [m_norm_mix, m_norm_mlp, m_conv_b_in, m_conv_w_dw.reshape(CONV_WIDTH, ds4), m_conv_b_dw, m_conv_ln_g,
            m_conv_ln_b, m_conv_b_out, m_kv_norm.reshape(1, d), m_final_norm.reshape(1, d)]
    sm_v = [v_norm_mix, v_norm_mlp, v_conv_b_in, v_conv_w_dw.reshape(CONV_WIDTH, ds4), v_conv_b_dw, v_conv_ln_g,
            v_conv_ln_b, v_conv_b_out, v_kv_norm.reshape(1, d), v_final_norm.reshape(1, d)]
    sm_g = [g_norm_mix, g_norm_mlp, g_b_in, g_w_dw, g_b_dw, g_ln_g, g_ln_b, g_b_out, g_kv_norm, g_final]
    sm_d, sm_nm, sm_nv = _adam_small(sm_w, sm_m, sm_v, sm_g)
    shapes = [norm_mix.shape, norm_mlp.shape, conv_b_in.shape, conv_w_dw.shape, conv_b_dw.shape, conv_ln_g.shape,
              conv_ln_b.shape, conv_b_out.shape, kv_norm.shape, final_norm.shape]
    sm_g, sm_d, sm_nm, sm_nv = [[t.reshape(sh) for t, sh in zip(lst, shapes)] for lst in (sm_g, sm_d, sm_nm, sm_nv)]

    def order(sm, idx):
        return [sm[0], sm[1], r_win[idx], sm[2], sm[3], sm[4], sm[5], sm[6], r_wout[idx], sm[7], sm[8],
                r_wkv[idx], r_wq[idx], r_wo[idx], r_wmi[idx], r_wmo[idx], sm[9]]

    return (loss, dx.reshape(x.shape), *order(sm_g, 0), *order(sm_d, 1), *order(sm_nm, 2), *order(sm_nv, 3))
```

```python
import functools
import math

import jax
import jax.numpy as jnp
from jax import lax
from jax.experimental import pallas as pl
from jax.experimental.pallas import tpu as pltpu

F32 = jnp.float32
BF16 = jnp.bfloat16
I32 = jnp.int32

NORM_EPS = 1e-6
LN_EPS = 1e-5
HEAD_DIM = 128
N_KV_HEADS = 4
ROT_DIM = 32
ROPE_THETA = 500000.0
CONV_WIDTH = 31
CONV_PAD = 32
ATT_BLOCK = 128
ATT_STEP_BLOCKS = 16
DILATIONS = (1, 4, 16)
ADAM_LR = 0.001
ADAM_B1 = 0.9
ADAM_B2 = 0.999
ADAM_EPS = 1e-08
ADAM_WD = 0.01
ADAM_STEP = 10
N_SHARD = 4
N_DEV = 8
LANES = 128
VMEM_LIMIT = 48 * 1024 * 1024
ADAM_ROWS = 128
ADAM_IN_BUFFERS = 3
ADAM_OUT_BUFFERS = 2
MM_TM, MM_TN, MM_TK = 1024, 1024, 2048
ROW_TILE = 512
CONV_CB = 128
CONV_T = 128
SMALL_ROWS = 48
MESH = pl.DeviceIdType.MESH
ANY = pl.BlockSpec(memory_space=pl.ANY)
HBM = pl.BlockSpec(memory_space=pltpu.HBM)
SEM = pl.BlockSpec(memory_space=pltpu.SEMAPHORE)
SPLIT_EFFECT = pltpu.SideEffectType.DATAFLOW_SIDE_EFFECTING


def _params(*sem):
    return pltpu.CompilerParams(dimension_semantics=sem, vmem_limit_bytes=VMEM_LIMIT)


def _sigmoid(x):
    return 1.0 / (1.0 + jnp.exp(-x))


def _wspec(kind, arr_shape, br, bc, pick):
    if kind == "plain":
        return pl.BlockSpec((br, bc), pick)
    per = arr_shape[2] // bc

    def idx(*g):
        rb, cb = pick(*g)
        return (cb // per, rb, cb % per)

    return pl.BlockSpec((None, br, bc), idx)


def _stage_shape(rows, w):
    return (w // LANES, rows, LANES)


def _to_residues(val, stage_ref, out_refs, dils):
    planes, rows, _ = stage_ref.shape
    for c in range(planes):
        stage_ref[c] = val[:, c * LANES:(c + 1) * LANES]
    for out_ref, dil in zip(out_refs, dils):
        if dil == 1:
            out_ref[0] = val.astype(out_ref.dtype)
            continue
        for r in range(dil):
            for c in range(planes):
                out_ref[r, :, c * LANES:(c + 1) * LANES] = stage_ref.at[c][pl.ds(r, rows // dil, stride=dil), :].astype(
                    out_ref.dtype)


def _from_residues(src_ref, stage_ref, dil):
    planes, rows, _ = stage_ref.shape
    if dil == 1:
        return lambda c: src_ref[0, :, c * LANES:(c + 1) * LANES].astype(F32)
    for r in range(dil):
        for c in range(planes):
            stage_ref.at[c][pl.ds(r, rows // dil, stride=dil), :] = src_ref[r, :, c * LANES:(c + 1) * LANES].astype(F32)
    return lambda c: stage_ref[c]


def _matmul(name, mode, a, b, *, m, n, k, tm=MM_TM, tn=MM_TN, tk=MM_TK, a_kind="plain", b_kind="plain", outs,
            extras=(), epilogue=None, stage=False, after=None):
    tm, tn, tk = min(tm, m), min(tn, n), min(tk, k)
    if b_kind == "col" and mode in ("nn", "tn"):
        tn = min(tn, n // b.shape[0])
    if b_kind == "col" and mode == "nt":
        tk = min(tk, k // b.shape[0])
    if a_kind == "col":
        assert mode == "nt"
        tk = min(tk, k // a.shape[0])
    if any(kind == "col" for _, kind in outs):
        tn = min(tn, n // N_SHARD)
    assert m % tm == 0 and n % tn == 0 and k % tk == 0, (name, m, n, k, tm, tn, tk)
    nk = k // tk
    grid = (m // tm, n // tn, nk)
    if mode == "nn":
        a_spec = pl.BlockSpec((tm, tk), lambda i, j, kk: (i, kk))
        b_spec = _wspec(b_kind, b.shape, tk, tn, lambda i, j, kk: (kk, j))
        dims = (((1,), (0,)), ((), ()))
    elif mode == "nt":
        a_spec = _wspec(a_kind, a.shape, tm, tk, lambda i, j, kk: (i, kk))
        b_spec = _wspec(b_kind, b.shape, tn, tk, lambda i, j, kk: (j, kk))
        dims = (((1,), (1,)), ((), ()))
    else:
        a_spec = pl.BlockSpec((tk, tm), lambda i, j, kk: (kk, i))
        b_spec = _wspec(b_kind, b.shape, tk, tn, lambda i, j, kk: (kk, j))
        dims = (((0,), (0,)), ((), ()))
    out_shape, out_specs = [], []
    for dtype, kind in outs:
        if isinstance(kind, tuple):
            dil = kind[1]
            out_shape.append(jax.ShapeDtypeStruct((dil, m // dil, n), dtype))
            out_specs.append(pl.BlockSpec((dil, tm // dil, tn), lambda i, j, kk: (0, i, j)))
            continue
        shape = (m, n) if kind == "plain" else (N_SHARD, m, n // N_SHARD)
        out_shape.append(jax.ShapeDtypeStruct(shape, dtype))
        out_specs.append(_wspec(kind, shape, tm, tn, lambda i, j, kk: (i, j)))
    n_ex = len(extras)
    deps = [] if after is None else [after]
    ex_specs = {"ij": pl.BlockSpec((tm, tn), lambda i, j, kk: (i, j)),
                "vec": pl.BlockSpec((1, tn), lambda i, j, kk: (0, j)),
                "rows": pl.BlockSpec((tm, LANES), lambda i, j, kk: (i, 0))}
    out0 = 2 + n_ex + len(deps)

    def body(*refs):
        a_ref, b_ref = refs[0], refs[1]
        ex_refs = refs[2:2 + n_ex]
        out_refs = refs[out0:out0 + len(outs)]
        j = pl.program_id(1)

        def finish(res):
            if epilogue is None:
                out_refs[0][...] = res.astype(out_refs[0].dtype)
            elif stage:
                epilogue(res, ex_refs, out_refs, j, refs[-1])
            else:
                epilogue(res, ex_refs, out_refs, j)

        prod = lax.dot_general(a_ref[...], b_ref[...], dims, preferred_element_type=F32)
        if nk == 1:
            finish(prod)
            return
        acc_ref = refs[out0 + len(outs)]
        kk = pl.program_id(2)

        @pl.when(kk == 0)
        def _():
            acc_ref[...] = prod

        @pl.when(kk > 0)
        def _():
            acc_ref[...] += prod

        @pl.when(kk == nk - 1)
        def _():
            finish(acc_ref[...])

    res = pl.pallas_call(
        body, name=name, grid=grid,
        in_specs=[a_spec, b_spec] + [ex_specs[how] for _, how in extras] + [ANY] * len(deps),
        out_specs=out_specs, out_shape=out_shape,
        scratch_shapes=[pltpu.VMEM((tm, tn), F32)] * (nk > 1) + [pltpu.VMEM(_stage_shape(tm, tn), F32)] * bool(stage),
        compiler_params=_params("parallel", "parallel", "arbitrary"),
    )(a, b, *[e for e, _ in extras], *deps)
    return res


def _rope_tables(seq):
    half = ROT_DIM // 2
    pos = jnp.arange(seq, dtype=F32)
    inv = ROPE_THETA ** (-jnp.arange(0, ROT_DIM, 2, dtype=F32) / ROT_DIM)
    ang = pos[:, None] * inv[None, :]
    cos, sin = jnp.cos(ang), jnp.sin(ang)
    zeros = jnp.zeros((seq, HEAD_DIM - ROT_DIM), F32)
    ctab = jnp.concatenate([cos, cos, zeros + 1.0], axis=1)
    atab = jnp.concatenate([-sin, jnp.zeros((seq, half), F32), zeros], axis=1)
    btab = jnp.concatenate([jnp.zeros((seq, half), F32), sin, zeros], axis=1)
    return ctab, atab, btab


def _rope_apply(x, ctab, atab, btab, sign):
    w = x.shape[1]
    reps = w // HEAD_DIM
    half = ROT_DIM // 2
    c = jnp.tile(ctab, (1, reps))
    a = jnp.tile(atab, (1, reps))
    b = jnp.tile(btab, (1, reps))
    up = pltpu.roll(x, w - half, 1)
    down = pltpu.roll(x, half, 1)
    return x * c + sign * (up * a + down * b)


def _rows(t, w):
    return pl.BlockSpec((t, w), lambda i: (i, 0))


def _fixed(shape):
    nd = len(shape)
    return pl.BlockSpec(shape, lambda i: (0,) * nd)


def _behind(after):
    deps = [] if after is None else (list(after) if isinstance(after, (list, tuple)) else [after])
    return deps, [ANY] * len(deps)


def _rms_fwd(name, x, gains, after=None):
    s, d = x.shape
    t = min(ROW_TILE, s)
    ng = len(gains)
    deps, dep_specs = _behind(after)

    def body(*all_refs):
        x_ref, refs = all_refs[len(deps)], all_refs[len(deps) + 1:]
        xv = x_ref[...]
        r = lax.rsqrt(jnp.mean(xv * xv, axis=-1, keepdims=True) + NORM_EPS)
        xn = xv * r
        for g_ref, y_ref in zip(refs[:ng], refs[ng:]):
            y_ref[...] = (xn * g_ref[...]).astype(BF16)

    return pl.pallas_call(
        body, name=name, grid=(s // t,),
        in_specs=dep_specs + [_rows(t, d)] + [_fixed((1, d))] * ng,
        out_specs=[_rows(t, d)] * ng,
        out_shape=[jax.ShapeDtypeStruct((s, d), BF16)] * ng,
        compiler_params=_params("parallel"),
    )(*deps, x, *gains)


def _rms_bwd(name, x, pairs, dh_in, want_colsum=False, after=None):
    s, d = x.shape
    n_p = len(pairs)
    t = min(ROW_TILE // n_p, s)
    deps, dep_specs = _behind(after)

    def body(*all_refs):
        x_ref, dh_ref, refs = all_refs[len(deps)], all_refs[len(deps) + 1], all_refs[len(deps) + 2:]
        g_refs = refs[:n_p]
        dy_refs = refs[n_p:2 * n_p]
        dh_out, dhb_out = refs[2 * n_p], refs[2 * n_p + 1]
        dg_refs = refs[2 * n_p + 2:2 * n_p + 2 + n_p]
        cs_ref = refs[-1] if want_colsum else None
        i = pl.program_id(0)
        xv = x_ref[...]
        r = lax.rsqrt(jnp.mean(xv * xv, axis=-1, keepdims=True) + NORM_EPS)
        xn = xv * r
        dh = dh_ref[...]
        for g_ref, dy_ref, dg_ref in zip(g_refs, dy_refs, dg_refs):
            dy = dy_ref[...].astype(F32)
            u = dy * g_ref[...]
            dh = dh + r * (u - xn * jnp.mean(u * xn, axis=-1, keepdims=True))
            part = jnp.sum(dy * xn, axis=0, keepdims=True)

            @pl.when(i == 0)
            def _():
                dg_ref[...] = part

            @pl.when(i > 0)
            def _():
                dg_ref[...] += part

        dh_out[...] = dh
        dhb_out[...] = dh.astype(BF16)
        if want_colsum:
            col = jnp.sum(dh, axis=0, keepdims=True)

            @pl.when(i == 0)
            def _():
                cs_ref[...] = col

            @pl.when(i > 0)
            def _():
                cs_ref[...] += col

    n_vec = n_p + (1 if want_colsum else 0)
    return pl.pallas_call(
        body, name=name, grid=(s // t,),
        in_specs=dep_specs + [_rows(t, d), _rows(t, d)] + [_fixed((1, d))] * n_p + [_rows(t, d)] * n_p,
        out_specs=[_rows(t, d), _rows(t, d)] + [_fixed((1, d))] * n_vec,
        out_shape=[jax.ShapeDtypeStruct((s, d), F32), jax.ShapeDtypeStruct((s, d), BF16)]
        + [jax.ShapeDtypeStruct((1, d), F32)] * n_vec,
        compiler_params=_params("arbitrary"),
    )(*deps, x, dh_in, *[g for g, _ in pairs], *[dy for _, dy in pairs])


def _final_loss(x, g, target):
    s, d = x.shape
    t = min(ROW_TILE, s)

    def body(x_ref, g_ref, t_ref, dh_out, dhb_out, dg_ref, loss_ref):
        i = pl.program_id(0)
        xv = x_ref[...]
        gv = g_ref[...]
        r = lax.rsqrt(jnp.mean(xv * xv, axis=-1, keepdims=True) + NORM_EPS)
        xn = xv * r
        diff = xn * gv - t_ref[...]
        dy = diff / d
        u = dy * gv
        dh = r * (u - xn * jnp.mean(u * xn, axis=-1, keepdims=True))
        dh_out[...] = dh
        dhb_out[...] = dh.astype(BF16)
        dg = jnp.sum(dy * xn, axis=0, keepdims=True)
        lc = jnp.sum(0.5 * diff * dy, axis=0, keepdims=True)

        @pl.when(i == 0)
        def _():
            dg_ref[...] = dg
            loss_ref[...] = lc

        @pl.when(i > 0)
        def _():
            dg_ref[...] += dg
            loss_ref[...] += lc

    return pl.pallas_call(
        body, name="final_loss", grid=(s // t,),
        in_specs=[_rows(t, d), _fixed((1, d)), _rows(t, d)],
        out_specs=[_rows(t, d), _rows(t, d), _fixed((1, d)), _fixed((1, d))],
        out_shape=[jax.ShapeDtypeStruct((s, d), F32), jax.ShapeDtypeStruct((s, d), BF16),
                   jax.ShapeDtypeStruct((1, d), F32), jax.ShapeDtypeStruct((1, d), F32)],
        compiler_params=_params("arbitrary"),
    )(x, g, target)


def _ln_silu_fwd(c, g, b):
    s, d = c.shape
    t = min(ROW_TILE, s)

    def body(c_ref, g_ref, b_ref, s_ref):
        cv = c_ref[...]
        mu = jnp.mean(cv, axis=-1, keepdims=True)
        xc = cv - mu
        rs = lax.rsqrt(jnp.mean(xc * xc, axis=-1, keepdims=True) + LN_EPS)
        ln = xc * rs * g_ref[...] + b_ref[...]
        s_ref[...] = (ln * _sigmoid(ln)).astype(BF16)

    return pl.pallas_call(
        body, name="ln_silu_fwd", grid=(s // t,),
        in_specs=[_rows(t, d), _fixed((1, d)), _fixed((1, d))],
        out_specs=_rows(t, d), out_shape=jax.ShapeDtypeStruct((s, d), BF16),
        compiler_params=_params("parallel"),
    )(c, g, b)


def _ln_silu_bwd(c, g, b, ds, after=None):
    s, d = c.shape
    t = min(ROW_TILE, s)
    deps, dep_specs = _behind(after)

    def body(*all_refs):
        c_ref, g_ref, b_ref, ds_ref, dc_ref, dg_ref, db_ref, dbdw_ref = all_refs[len(deps):]
        i = pl.program_id(0)
        cv = c_ref[...]
        gv = g_ref[...]
        mu = jnp.mean(cv, axis=-1, keepdims=True)
        xc = cv - mu
        rs = lax.rsqrt(jnp.mean(xc * xc, axis=-1, keepdims=True) + LN_EPS)
        nrm = xc * rs
        ln = nrm * gv + b_ref[...]
        sig = _sigmoid(ln)
        dln = ds_ref[...].astype(F32) * sig * (1.0 + ln * (1.0 - sig))
        dn = dln * gv
        dc = rs * (dn - jnp.mean(dn, axis=-1, keepdims=True)
                   - nrm * jnp.mean(dn * nrm, axis=-1, keepdims=True))
        dc_ref[...] = dc
        pg = jnp.sum(dln * nrm, axis=0, keepdims=True)
        pb = jnp.sum(dln, axis=0, keepdims=True)
        pc = jnp.sum(dc, axis=0, keepdims=True)

        @pl.when(i == 0)
        def _():
            dg_ref[...] = pg
            db_ref[...] = pb
            dbdw_ref[...] = pc

        @pl.when(i > 0)
        def _():
            dg_ref[...] += pg
            db_ref[...] += pb
            dbdw_ref[...] += pc

    return pl.pallas_call(
        body, name="ln_silu_bwd", grid=(s // t,),
        in_specs=dep_specs + [_rows(t, d), _fixed((1, d)), _fixed((1, d)), _rows(t, d)],
        out_specs=[_rows(t, d)] + [_fixed((1, d))] * 3,
        out_shape=[jax.ShapeDtypeStruct((s, d), F32)] + [jax.ShapeDtypeStruct((1, d), F32)] * 3,
        compiler_params=_params("arbitrary"),
    )(*deps, c, g, b, ds)


def _residue_spec(dil, t, w):
    return pl.BlockSpec((dil, t // dil, w), lambda i: (0, i, 0))


def _attn_combine(o_list, lse_list):
    dil0, sd0, d = o_list[0].shape
    s = dil0 * sd0
    lw = lse_list[0].shape[2]
    group = d // HEAD_DIM // N_KV_HEADS
    t = min(ROW_TILE, s)
    nb = len(o_list)
    dils = [o.shape[0] for o in o_list]

    def body(*refs):
        o_out, l_out = refs[2 * nb], refs[2 * nb + 1]
        o_stage, l_stage = refs[2 * nb + 2:3 * nb + 2], refs[3 * nb + 2:]
        o_planes = [_from_residues(src, stage, dil) for src, stage, dil in zip(refs[:nb], o_stage, dils)]
        l_planes = [_from_residues(src, stage, dil) for src, stage, dil in zip(refs[nb:2 * nb], l_stage, dils)]
        for kh in range(N_KV_HEADS):
            ls = [plane(kh) for plane in l_planes]
            mx = ls[0]
            for l in ls[1:]:
                mx = jnp.maximum(mx, l)
            es = [jnp.exp(l - mx) for l in ls]
            den = es[0]
            for e in es[1:]:
                den = den + e
            l_out[:, kh * LANES:(kh + 1) * LANES] = mx + jnp.log(den)
            ws = [e / den for e in es]
            for g in range(group):
                h = kh * group + g
                acc = jnp.zeros((t, HEAD_DIM), F32)
                for plane, w in zip(o_planes, ws):
                    acc = acc + w[:, g:g + 1] * plane(h)
                o_out[:, h * HEAD_DIM:(h + 1) * HEAD_DIM] = acc.astype(BF16)

    return pl.pallas_call(
        body, name="attn_combine", grid=(s // t,),
        in_specs=[_residue_spec(dil, t, d) for dil in dils] + [_residue_spec(dil, t, lw) for dil in dils],
        out_specs=[_rows(t, d), _rows(t, lw)],
        out_shape=[jax.ShapeDtypeStruct((s, d), BF16), jax.ShapeDtypeStruct((s, lw), F32)],
        scratch_shapes=[pltpu.VMEM(_stage_shape(t, d), F32)] * nb + [pltpu.VMEM(_stage_shape(t, lw), F32)] * nb,
        compiler_params=_params("parallel"),
    )(*o_list, *lse_list)


def _attn_delta(do, o, lse, dils):
    s, d = o.shape
    lw = lse.shape[1]
    group = d // HEAD_DIM // N_KV_HEADS
    t = min(ROW_TILE, s)
    nd = len(dils)

    def body(do_ref, o_ref, lse_ref, *refs):
        stage = refs[-1]
        lane = lax.broadcasted_iota(I32, (t, LANES), 1)
        planes = []
        for kh in range(N_KV_HEADS):
            out = jnp.zeros((t, LANES), F32)
            for g in range(group):
                cols = slice((kh * group + g) * HEAD_DIM, (kh * group + g + 1) * HEAD_DIM)
                v = jnp.sum(do_ref[:, cols].astype(F32) * o_ref[:, cols].astype(F32), axis=-1, keepdims=True)
                out = jnp.where(lane == g, v, out)
            planes.append(out)
        _to_residues(lse_ref[...], stage, refs[:nd], dils)
        _to_residues(jnp.concatenate(planes, axis=1), stage, refs[nd:2 * nd], dils)

    res = pl.pallas_call(
        body, name="attn_delta", grid=(s // t,),
        in_specs=[_rows(t, d), _rows(t, d), _rows(t, lw)],
        out_specs=[_residue_spec(dil, t, lw) for dil in dils] * 2,
        out_shape=[jax.ShapeDtypeStruct((dil, s // dil, lw), F32) for dil in dils] * 2,
        scratch_shapes=[pltpu.VMEM(_stage_shape(t, lw), F32)],
        compiler_params=_params("parallel"),
    )(do, o, lse)
    return res[:nd], res[nd:]


def _residue_sum(name, groups, tabs):
    first = groups[0][0][0]
    s, w = first.shape[0] * first.shape[1], first.shape[2]
    t = min(ROW_TILE, s)
    flat = [p for parts, _ in groups for p in parts]

    def body(*refs):
        c_ref, a_ref, b_ref = refs[len(flat):len(flat) + 3]
        out = refs[len(flat) + 3]
        stages = refs[len(flat) + 4:]
        k = 0
        for gi, (parts, rotate) in enumerate(groups):
            planes = [_from_residues(refs[k + i], stages[k + i], p.shape[0]) for i, p in enumerate(parts)]
            k += len(parts)
            for c in range(w // LANES):
                tot = planes[0](c)
                for plane in planes[1:]:
                    tot = tot + plane(c)
                if rotate:
                    tot = _rope_apply(tot, c_ref[...], a_ref[...], b_ref[...], -1.0)
                out[:, gi * w + c * LANES:gi * w + (c + 1) * LANES] = tot.astype(BF16)

    return pl.pallas_call(
        body, name=name, grid=(s // t,),
        in_specs=[_residue_spec(p.shape[0], t, w) for p in flat] + [_rows(t, HEAD_DIM)] * 3,
        out_specs=_rows(t, len(groups) * w), out_shape=jax.ShapeDtypeStruct((s, len(groups) * w), BF16),
        scratch_shapes=[pltpu.VMEM(_stage_shape(t, w), F32) for _ in flat],
        compiler_params=_params("parallel"),
    )(*flat, *tabs)


def _dwconv_fwd(u, w_dw, b_dw, after=None):
    s, d2 = u.shape
    d = d2 // 2
    cb = min(CONV_CB, d)
    nblk = d // cb
    tt = min(CONV_T, s)
    deps, dep_specs = _behind(after)

    def body(*all_refs):
        ua_ref, ug_ref, w_ref, b_ref, c_ref, xp_ref = all_refs[len(deps):]
        gl =ua_ref[...].astype(F32) * _sigmoid(ug_ref[...].astype(F32))
        xp_ref[0:CONV_PAD, :] = jnp.zeros((CONV_PAD, cb), F32)
        xp_ref[CONV_PAD:, :] = gl
        wv = w_ref[...]
        bv = b_ref[...]
        for t0 in range(0, s, tt):
            acc = jnp.zeros((tt, cb), F32) + bv
            for kk in range(CONV_WIDTH):
                off = t0 + CONV_PAD - (CONV_WIDTH - 1) + kk
                acc = acc + wv[kk:kk + 1, :] * xp_ref[off:off + tt, :]
            c_ref[t0:t0 + tt, :] = acc

    return pl.pallas_call(
        body, name="dwconv_fwd", grid=(nblk,),
        in_specs=dep_specs + [pl.BlockSpec((s, cb), lambda j: (0, j)), pl.BlockSpec((s, cb), lambda j: (0, j + nblk)),
                              pl.BlockSpec((CONV_PAD, cb), lambda j: (0, j)), pl.BlockSpec((1, cb), lambda j: (0, j))],
        out_specs=pl.BlockSpec((s, cb), lambda j: (0, j)),
        out_shape=jax.ShapeDtypeStruct((s, d), F32),
        scratch_shapes=[pltpu.VMEM((s + CONV_PAD, cb), F32)],
        compiler_params=_params("parallel"),
    )(*deps, u, u, w_dw, b_dw)


def _dwconv_bwd(u, w_dw, dc):
    s, d2 = u.shape
    d = d2 // 2
    cb = min(CONV_CB, d)
    nblk = d // cb
    tt = min(CONV_T, s)

    def body(ua_ref, ug_ref, w_ref, dc_ref, du_ref, dw_ref, dba_ref, dbg_ref, glp_ref, dcp_ref, acc_ref):
        a = ua_ref[...].astype(F32)
        sig = _sigmoid(ug_ref[...].astype(F32))
        glp_ref[0:CONV_PAD, :] = jnp.zeros((CONV_PAD, cb), F32)
        glp_ref[CONV_PAD:, :] = a * sig
        dcp_ref[0:s, :] = dc_ref[...]
        dcp_ref[s:, :] = jnp.zeros((CONV_PAD, cb), F32)
        acc_ref[...] = jnp.zeros_like(acc_ref)
        wv = w_ref[...]
        dba = jnp.zeros((1, cb), F32)
        dbg = jnp.zeros((1, cb), F32)
        for t0 in range(0, s, tt):
            dgl = jnp.zeros((tt, cb), F32)
            dct = dc_ref[t0:t0 + tt, :]
            for kk in range(CONV_WIDTH):
                off = t0 + (CONV_WIDTH - 1) - kk
                dgl = dgl + wv[kk:kk + 1, :] * dcp_ref[off:off + tt, :]
                goff = t0 + CONV_PAD - (CONV_WIDTH - 1) + kk
                prod = dct * glp_ref[goff:goff + tt, :]
                acc_ref[8 * kk:8 * kk + 8, :] += jnp.sum(prod.reshape(tt // 8, 8, cb), axis=0)
            at = ua_ref[t0:t0 + tt, :].astype(F32)
            st = _sigmoid(ug_ref[t0:t0 + tt, :].astype(F32))
            da = dgl * st
            dg = dgl * at * st * (1.0 - st)
            du_ref[0, t0:t0 + tt, :] = da.astype(BF16)
            du_ref[1, t0:t0 + tt, :] = dg.astype(BF16)
            dba = dba + jnp.sum(da, axis=0, keepdims=True)
            dbg = dbg + jnp.sum(dg, axis=0, keepdims=True)
        dba_ref[...] = dba
        dbg_ref[...] = dbg
        for kk in range(CONV_WIDTH):
            dw_ref[kk:kk + 1, :] = jnp.sum(acc_ref[8 * kk:8 * kk + 8, :], axis=0, keepdims=True)
        dw_ref[CONV_WIDTH:, :] = jnp.zeros((CONV_PAD - CONV_WIDTH, cb), F32)

    blk = pl.BlockSpec((s, cb), lambda j: (0, j))
    vec = pl.BlockSpec((1, cb), lambda j: (0, j))
    return pl.pallas_call(
        body, name="dwconv_bwd", grid=(nblk,),
        in_specs=[blk, pl.BlockSpec((s, cb), lambda j: (0, j + nblk)),
                  pl.BlockSpec((CONV_PAD, cb), lambda j: (0, j)), blk],
        out_specs=[pl.BlockSpec((2, s, cb), lambda j: (0, 0, j)), pl.BlockSpec((CONV_PAD, cb), lambda j: (0, j)),
                   vec, vec],
        out_shape=[jax.ShapeDtypeStruct((2, s, d), BF16), jax.ShapeDtypeStruct((CONV_PAD, d), F32),
                   jax.ShapeDtypeStruct((1, d), F32), jax.ShapeDtypeStruct((1, d), F32)],
        scratch_shapes=[pltpu.VMEM((s + CONV_PAD, cb), F32), pltpu.VMEM((s + CONV_PAD, cb), F32),
                        pltpu.VMEM((8 * CONV_PAD, cb), F32)],
        compiler_params=_params("parallel"),
    )(u, u, w_dw, dc)


def _stack_heads(x, group):
    return jnp.concatenate([x[:, g * HEAD_DIM:(g + 1) * HEAD_DIM] for g in range(group)], axis=0)


def _unstack_heads(x, group):
    return jnp.concatenate([x[g * ATT_BLOCK:(g + 1) * ATT_BLOCK, :] for g in range(group)], axis=1)


def _stack_cols(x, group):
    return jnp.concatenate([x[:, g:g + 1] for g in range(group)], axis=0)


def _band_bias(group):
    rows = group * ATT_BLOCK
    row = lax.broadcasted_iota(I32, (rows, 2 * ATT_BLOCK), 0) % ATT_BLOCK
    col = lax.broadcasted_iota(I32, (rows, 2 * ATT_BLOCK), 1)
    band = jnp.where((col >= row) & (col <= row + ATT_BLOCK), 0.0, -jnp.inf).astype(F32)
    first = jnp.where(lax.broadcasted_iota(I32, (1, 2 * ATT_BLOCK), 1) >= ATT_BLOCK, 0.0, -jnp.inf).astype(F32)
    return band, first


def _masked_scores(qs, kw, band_ref, first_ref, nb, scale):
    sc = lax.dot_general(qs, kw, (((1,), (1,)), ((), ())), preferred_element_type=F32) * scale + band_ref[...]
    return sc + jnp.where(nb > 0, 0.0, first_ref[...])


def _window(ref, nb):
    prev = pl.multiple_of(jnp.maximum(nb - 1, 0) * ATT_BLOCK, ATT_BLOCK)
    cur = pl.multiple_of(nb * ATT_BLOCK, ATT_BLOCK)
    return jnp.concatenate([ref[pl.ds(prev, ATT_BLOCK), :], ref[pl.ds(cur, ATT_BLOCK), :]], axis=0)


def _residues_per_step(dil, nblk):
    return max(1, min(dil, ATT_STEP_BLOCKS // nblk))


def _attn_fwd(name, q, kv):
    dil, sd, d = q.shape
    group = d // HEAD_DIM // N_KV_HEADS
    gw = group * HEAD_DIM
    nblk = sd // ATT_BLOCK
    scale = 1.0 / math.sqrt(HEAD_DIM)
    rb = _residues_per_step(dil, nblk)

    def body(q_all, k_all, v_all, band_ref, first_ref, o_all, lse_all):
        lane = lax.broadcasted_iota(I32, (ATT_BLOCK, LANES), 1)
        for rr in range(rb):
            q_ref, k_ref, v_ref, o_ref, lse_ref = [ref.at[rr] for ref in (q_all, k_all, v_all, o_all, lse_all)]

            def step(nb, carry):
                rows = pl.ds(pl.multiple_of(nb * ATT_BLOCK, ATT_BLOCK), ATT_BLOCK)
                qs = _stack_heads(q_ref[rows, :], group)
                kw = _window(k_ref, nb)
                vw = _window(v_ref, nb)
                sc = _masked_scores(qs, kw, band_ref, first_ref, nb, scale)
                mx = jnp.max(sc, axis=-1, keepdims=True)
                p = jnp.exp(sc - mx)
                l = jnp.sum(p, axis=-1, keepdims=True)
                o = jnp.dot(p.astype(BF16), vw, preferred_element_type=F32) / l
                o_ref[rows, :] = _unstack_heads(o, group).astype(BF16)
                lse = mx + jnp.log(l)
                out = jnp.zeros((ATT_BLOCK, LANES), F32)
                for g in range(group):
                    out = jnp.where(lane == g, lse[g * ATT_BLOCK:(g + 1) * ATT_BLOCK, :], out)
                lse_ref[rows, :] = out
                return carry

            lax.fori_loop(0, nblk, step, 0, unroll=min(2, nblk))

    kvh = N_KV_HEADS
    band, first = _band_bias(group)
    qspec = pl.BlockSpec((rb, sd, gw), lambda r, h: (r, 0, h))
    kspec = pl.BlockSpec((rb, sd, HEAD_DIM), lambda r, h: (r, 0, h))
    return pl.pallas_call(
        body, name=name, grid=(dil // rb, kvh),
        in_specs=[qspec, kspec, pl.BlockSpec((rb, sd, HEAD_DIM), lambda r, h: (r, 0, kvh + h)),
                  pl.BlockSpec(band.shape, lambda r, h: (0, 0)), pl.BlockSpec(first.shape, lambda r, h: (0, 0))],
        out_specs=[qspec, kspec],
        out_shape=[jax.ShapeDtypeStruct((dil, sd, d), BF16),
                   jax.ShapeDtypeStruct((dil, sd, kvh * LANES), F32)],
        compiler_params=_params("parallel", "parallel"),
    )(q, kv, kv, band, first)


def _attn_bwd(name, q, kv, do, lse, delta):
    dil, sd, d = q.shape
    group = d // HEAD_DIM // N_KV_HEADS
    gw = group * HEAD_DIM
    nblk = sd // ATT_BLOCK
    scale = 1.0 / math.sqrt(HEAD_DIM)
    nt = (((1,), (1,)), ((), ()))
    tn = (((0,), (0,)), ((), ()))

    rb = _residues_per_step(dil, nblk)

    def body(q_all, k_all, v_all, do_all, lse_all, dl_all, band_ref, first_ref, dq_all, dk_all, dv_all, dk_accs,
             dv_accs):
        dk_accs[...] = jnp.zeros_like(dk_accs)
        dv_accs[...] = jnp.zeros_like(dv_accs)
        for rr in range(rb):
            q_ref, k_ref, v_ref, do_ref, lse_ref, dl_ref, dq_ref, dk_ref, dv_ref, dk_acc, dv_acc = [
                ref.at[rr] for ref in (q_all, k_all, v_all, do_all, lse_all, dl_all, dq_all, dk_all, dv_all,
                                       dk_accs, dv_accs)]

            def step(nb, carry):
                rows = pl.ds(pl.multiple_of(nb * ATT_BLOCK, ATT_BLOCK), ATT_BLOCK)
                qs = _stack_heads(q_ref[rows, :], group)
                dos = _stack_heads(do_ref[rows, :], group)
                ls = _stack_cols(lse_ref[rows, :], group)
                dl = _stack_cols(dl_ref[rows, :], group)
                kw = _window(k_ref, nb)
                vw = _window(v_ref, nb)
                p = jnp.exp(_masked_scores(qs, kw, band_ref, first_ref, nb, scale) - ls)
                dp = lax.dot_general(dos, vw, nt, preferred_element_type=F32)
                ds = (p * (dp - dl) * scale).astype(BF16)
                dq = jnp.dot(ds, kw, preferred_element_type=F32)
                dq_ref[rows, :] = _unstack_heads(dq, group).astype(BF16)
                win = pl.ds(pl.multiple_of(nb * ATT_BLOCK, ATT_BLOCK), 2 * ATT_BLOCK)
                dk_acc[win, :] += lax.dot_general(ds, qs, tn, preferred_element_type=F32)
                dv_acc[win, :] += lax.dot_general(p.astype(BF16), dos, tn, preferred_element_type=F32)
                return carry

            lax.fori_loop(0, nblk, step, 0, unroll=min(2, nblk))
            dk_ref[...] = dk_acc[ATT_BLOCK:, :]
            dv_ref[...] = dv_acc[ATT_BLOCK:, :]

    kvh = N_KV_HEADS
    band, first = _band_bias(group)
    qspec = pl.BlockSpec((rb, sd, gw), lambda r, h: (r, 0, h))
    kspec = pl.BlockSpec((rb, sd, HEAD_DIM), lambda r, h: (r, 0, h))
    return pl.pallas_call(
        body, name=name, grid=(dil // rb, kvh),
        in_specs=[qspec, kspec, pl.BlockSpec((rb, sd, HEAD_DIM), lambda r, h: (r, 0, kvh + h)),
                  qspec, kspec, kspec,
                  pl.BlockSpec(band.shape, lambda r, h: (0, 0)), pl.BlockSpec(first.shape, lambda r, h: (0, 0))],
        out_specs=[qspec, kspec, kspec],
        out_shape=[jax.ShapeDtypeStruct((dil, sd, d), BF16),
                   jax.ShapeDtypeStruct((dil, sd, kvh * HEAD_DIM), F32),
                   jax.ShapeDtypeStruct((dil, sd, kvh * HEAD_DIM), F32)],
        scratch_shapes=[pltpu.VMEM((rb, sd + ATT_BLOCK, HEAD_DIM), F32)] * 2,
        compiler_params=_params("parallel", "parallel"),
    )(q, kv, kv, do, lse, delta, band, first)


def _cast_bf16(name, w, layer, place, after=None):
    _, r, c = w.shape
    tr = min(512, r)
    deps = [] if after is None else [after]

    def body(pl_ref, w_ref, *refs):
        refs[-1][...] = w_ref[...].astype(BF16)

    return pl.pallas_call(
        body, name=name,
        grid_spec=pltpu.PrefetchScalarGridSpec(
            num_scalar_prefetch=1, grid=(r // tr,),
            in_specs=[pl.BlockSpec((None, tr, c), lambda i, p: (layer, i, 0))] + [ANY] * len(deps),
            out_specs=pl.BlockSpec((None, tr, c), lambda i, p: (p[1], i, 0))),
        out_shape=jax.ShapeDtypeStruct((N_SHARD, r, c), BF16),
        compiler_params=_params("parallel"),
    )(place, w, *deps)


def _chip_sum(name, g, rh, place):
    _, r, c = g.shape
    rh2 = r // 2
    tr = min(512, rh2)
    nb = rh2 // tr

    def body(pl_ref, g_ref, rh_ref, o_ref):
        o_ref[...] = (g_ref[...].astype(F32) + rh_ref[...].astype(F32)).astype(BF16)

    return pl.pallas_call(
        body, name=name,
        grid_spec=pltpu.PrefetchScalarGridSpec(
            num_scalar_prefetch=1, grid=(N_SHARD, nb),
            in_specs=[pl.BlockSpec((None, tr, c), lambda s, i, p: (s, p[0] * nb + i, 0)),
                      pl.BlockSpec((None, tr, c), lambda s, i, p: (s, i, 0))],
            out_specs=pl.BlockSpec((None, tr, c), lambda s, i, p: (s, i, 0))),
        out_shape=jax.ShapeDtypeStruct((N_SHARD, rh2, c), BF16),
        compiler_params=_params("parallel", "parallel"),
    )(place, g, rh)


def _owner_sum(name, cs, rp, place):
    _, rh2, c = cs.shape
    tr = min(512, rh2)
    nb = rh2 // tr

    def body(pl_ref, cs_ref, r0_ref, r1_ref, r2_ref, o_ref):
        o_ref[...] = ((cs_ref[...].astype(F32) + r0_ref[...].astype(F32))
                      + (r1_ref[...].astype(F32) + r2_ref[...].astype(F32)))

    def rspec(j):
        return pl.BlockSpec((None, tr, c), lambda i, p: (j, i, 0))

    return pl.pallas_call(
        body, name=name,
        grid_spec=pltpu.PrefetchScalarGridSpec(
            num_scalar_prefetch=1, grid=(nb,),
            in_specs=[pl.BlockSpec((None, tr, c), lambda i, p: (p[1], i, 0)), rspec(0), rspec(1), rspec(2)],
            out_specs=pl.BlockSpec((tr, c), lambda i, p: (p[0] * nb + i, 0))),
        out_shape=jax.ShapeDtypeStruct((2 * rh2, c), F32),
        compiler_params=_params("parallel"),
    )(place, cs, rp, rp, rp)


def _adam_math(w, g, m, v):
    m = ADAM_B1 * m + (1.0 - ADAM_B1) * g
    v = ADAM_B2 * v + (1.0 - ADAM_B2) * (g * g)
    m_hat = m / (1.0 - ADAM_B1 ** ADAM_STEP)
    v_hat = v / (1.0 - ADAM_B2 ** ADAM_STEP)
    delta = -ADAM_LR * (m_hat / (jnp.sqrt(v_hat) + ADAM_EPS) + ADAM_WD * w)
    return delta, m, v


def _adamw(name, w, m, v, g, layer, partial=None):
    nl, r, c = w.shape
    tr = min(ADAM_ROWS, r)
    n = r // tr
    n_in = min(ADAM_IN_BUFFERS, n)
    n_out = min(ADAM_OUT_BUFFERS, n)
    prev = [] if partial is None else list(partial)

    def body(w_ref, m_ref, v_ref, g_ref, *refs):
        outs = refs[len(prev):len(prev) + 4]
        ibuf, obuf, isem, osem = refs[len(prev) + 4:]
        srcs = [w_ref.at[layer], m_ref.at[layer], v_ref.at[layer], g_ref]
        dsts = [o.at[layer] for o in outs]

        def rows(i):
            return pl.ds(pl.multiple_of(i * tr, tr), tr)

        def fetch(i, slot):
            return [pltpu.make_async_copy(srcs[k].at[rows(i)], ibuf.at[slot, k], isem.at[slot, k]) for k in range(4)]

        def store(i, slot):
            return [pltpu.make_async_copy(obuf.at[slot, k], dsts[k].at[rows(i)], osem.at[slot, k]) for k in range(4)]

        for i in range(n_in):
            for cp in fetch(i, i):
                cp.start()

        def step(i, carry):
            si = i % n_in
            so = i % n_out
            for cp in fetch(i, si):
                cp.wait()

            @pl.when(i >= n_out)
            def _():
                for cp in store(i - n_out, so):
                    cp.wait()

            gv = ibuf[si, 3]
            delta, m_new, v_new = _adam_math(ibuf[si, 0], gv, ibuf[si, 1], ibuf[si, 2])
            obuf[so, 0] = gv
            obuf[so, 1] = delta
            obuf[so, 2] = m_new
            obuf[so, 3] = v_new
            for cp in store(i, so):
                cp.start(priority=1)

            @pl.when(i + n_in < n)
            def _():
                for cp in fetch(i + n_in, si):
                    cp.start()

            return carry

        lax.fori_loop(0, n, step, 0)
        for i in range(n - n_out, n):
            for cp in store(i, i % n_out):
                cp.wait()

    return pl.pallas_call(
        body, name=name,
        in_specs=[ANY] * (4 + len(prev)),
        out_specs=[ANY] * 4,
        out_shape=[jax.ShapeDtypeStruct((nl, r, c), F32)] * 4,
        scratch_shapes=[pltpu.VMEM((n_in, 4, tr, c), F32), pltpu.VMEM((n_out, 4, tr, c), F32),
                        pltpu.SemaphoreType.DMA((n_in, 4)), pltpu.SemaphoreType.DMA((n_out, 4))],
        input_output_aliases={4 + i: i for i in range(len(prev))},
        compiler_params=pltpu.CompilerParams(vmem_limit_bytes=VMEM_LIMIT),
    )(w, m, v, g, *prev)


def _adam_small(ws, ms, vs, gs):
    n = len(ws)

    def body(*refs):
        w_refs, m_refs, v_refs, g_refs = refs[:n], refs[n:2 * n], refs[2 * n:3 * n], refs[3 * n:4 * n]
        d_refs, mo_refs, vo_refs = refs[4 * n:5 * n], refs[5 * n:6 * n], refs[6 * n:7 * n]
        for i in range(n):
            delta, m_new, v_new = _adam_math(w_refs[i][...], g_refs[i][...], m_refs[i][...], v_refs[i][...])
            d_refs[i][...] = delta
            mo_refs[i][...] = m_new
            vo_refs[i][...] = v_new

    shapes = [jax.ShapeDtypeStruct(w.shape, F32) for w in ws]
    res = pl.pallas_call(body, name="adam_small", out_shape=shapes * 3)(*ws, *ms, *vs, *gs)
    return res[:n], res[n:2 * n], res[2 * n:]


def _pack_small(b_in, w_dw, b_dw, ln_g, ln_b, b_out, place):
    cin = b_in.shape[1]
    cd = b_dw.shape[1]
    rows = 8 + CONV_PAD

    def body(pl_ref, bi, wd, bd, lg, lb, bo, out):
        out[...] = jnp.zeros_like(out)
        out[0:1, :] = bi[...]
        out[1:2, 0:cd] = bd[...]
        out[1:2, cd:2 * cd] = lg[...]
        out[2:3, 0:cd] = lb[...]
        out[2:3, cd:2 * cd] = bo[...]
        out[8:8 + CONV_WIDTH, 0:cd] = wd[...]

    def whole(arr):
        return pl.BlockSpec(arr.shape, lambda i, p: (0,) * arr.ndim)

    ins = [b_in, w_dw, b_dw, ln_g, ln_b, b_out]
    return pl.pallas_call(
        body, name="pack_small",
        grid_spec=pltpu.PrefetchScalarGridSpec(
            num_scalar_prefetch=1, grid=(1,), in_specs=[whole(a) for a in ins],
            out_specs=pl.BlockSpec((None, rows, cin), lambda i, p: (p[1], 0, 0))),
        out_shape=jax.ShapeDtypeStruct((N_SHARD, rows, cin), F32),
        compiler_params=_params("arbitrary"),
    )(place, *ins)


def _place():
    x, y, c = lax.axis_index("x"), lax.axis_index("y"), lax.axis_index("c")
    return x, y, c


def _other_chips(x, y):
    return [(1 - x, y), (x, 1 - y), (1 - x, 1 - y)]


def _split_start_many(name, parts, after=None):
    flat = [b for bufs, _, _ in parts for b in bufs]
    n, n_parts = len(flat), len(parts)
    deps = [] if after is None else [after]

    def body(*refs):
        out0 = n + len(deps)
        pos = 0
        for i, (bufs, _, copies) in enumerate(parts):
            for cp in copies(refs[pos:pos + len(bufs)], refs[out0 + 2 * i], refs[out0 + 2 * i + 1], False):
                cp.start()
            pos += len(bufs)
        refs[-1][...] = jnp.zeros_like(refs[-1])

    sems = [pltpu.SemaphoreType.DMA((n_sem,)) for _, n_sem, _ in parts for _ in range(2)]
    res = pl.pallas_call(
        body, name=name,
        out_shape=(*sems, *[pltpu.HBM(b.shape, b.dtype) for b in flat], jax.ShapeDtypeStruct((8, LANES), F32)),
        in_specs=[HBM] * n + [ANY] * len(deps),
        out_specs=(*[SEM] * (2 * n_parts), *[HBM] * n, pl.BlockSpec(memory_space=pltpu.VMEM)),
        input_output_aliases={i: 2 * n_parts + i for i in range(n)},
        compiler_params=pltpu.CompilerParams(has_side_effects=SPLIT_EFFECT),
    )(*[pltpu.with_memory_space_constraint(b, pltpu.HBM) for b in flat], *deps)
    handles, pos = [], 2 * n_parts
    for i, (bufs, _, _) in enumerate(parts):
        handles.append((res[2 * i], res[2 * i + 1], list(res[pos:pos + len(bufs)]), res[-1]))
        pos += len(bufs)
    return handles


def _split_start(name, bufs, n_sem, copies, after=None):
    return _split_start_many(name, [(bufs, n_sem, copies)], after)[0]


def _split_wait(name, handle, copies, after):
    ssem, rsem, bufs, _ = handle
    n = len(bufs)
    deps = list(after) if isinstance(after, (list, tuple)) else [after]

    def body(*refs):
        for cp in copies(refs[:n], refs[n], refs[n + 1], True):
            cp.wait_send()
            cp.wait_recv()

    res = pl.pallas_call(
        body, name=name,
        out_shape=[pltpu.HBM(b.shape, b.dtype) for b in bufs],
        in_specs=[HBM] * n + [SEM, SEM] + [ANY] * len(deps), out_specs=[HBM] * n,
        input_output_aliases={i: i for i in range(n)},
        compiler_params=pltpu.CompilerParams(has_side_effects=SPLIT_EFFECT),
    )(*bufs, ssem, rsem, *deps)
    return list(res)


def _remote(src, dst, ssem, rsem, k, to):
    return pltpu.make_async_remote_copy(src_ref=src, dst_ref=dst, send_sem=ssem.at[k], recv_sem=rsem.at[k],
                                        device_id=to, device_id_type=MESH)


def _gather_chips(x, y, c):
    nx, ny = x + (1 - c) - 2 * x * (1 - c), y + c - 2 * y * c
    fx, fy = x + c - 2 * x * c, y + (1 - c) - 2 * y * (1 - c)
    return (nx, ny), (fx, fy), 2 * nx + ny, 2 * fx + fy, 2 * (1 - x) + (1 - y)


def _direct_copies(refs, ssem, rsem, landing, n_whole=0):
    x, y, c = _place()
    me = 2 * x + y
    (nx, ny), _, near, _, _ = _gather_chips(x, y, c)
    n = len(refs) - n_whole
    cps = []
    for a, ref in enumerate(refs[:n]):
        cps.append(_remote(ref.at[me], ref.at[near if landing else me], ssem, rsem, a, (nx, ny, c)))
    for b, ref in enumerate(refs[n:]):
        for j, (px, py) in enumerate(_other_chips(x, y)):
            cps.append(_remote(ref.at[me], ref.at[2 * px + py if landing else me], ssem, rsem, n + 3 * b + j,
                               (px, py, c)))
    return cps


def _relay_copies(refs, ssem, rsem, landing):
    x, y, c = _place()
    _, (fx, fy), near, far, diag = _gather_chips(x, y, c)
    n = len(refs)
    cps = []
    for a, ref in enumerate(refs):
        rh = ref.shape[1] // 2
        rows = pl.ds(c * rh, rh)
        cps.append(_remote(ref.at[near, rows], ref.at[diag if landing else near, rows], ssem, rsem, a, (fx, fy, c)))
        cps.append(_remote(ref.at[near], ref.at[far if landing else near], ssem, rsem, n + a, (x, y, 1 - c)))
    return cps


def _diagonal_copies(refs, ssem, rsem, landing):
    x, y, c = _place()
    diag = 2 * (1 - x) + (1 - y)
    who = 1 - c if landing else c
    cps = []
    for a, ref in enumerate(refs):
        rh = ref.shape[1] // 2
        piece = ref.at[diag, pl.ds(who * rh, rh)]
        cps.append(_remote(piece, piece, ssem, rsem, a, (x, y, 1 - c)))
    return cps


def _sibling_copies(refs, ssem, rsem, landing):
    x, y, c = _place()
    n = len(refs) // 2
    cps = []
    for a in range(n):
        rh = refs[a].shape[1] // 2
        cps.append(_remote(refs[a].at[:, pl.ds((1 - c) * rh, rh), :], refs[n + a], ssem, rsem, a, (x, y, 1 - c)))
    return cps


def _owner_copies(refs, ssem, rsem, landing):
    x, y, c = _place()
    n = len(refs) // 2
    cps = []
    for a in range(n):
        for j, (px, py) in enumerate(_other_chips(x, y)):
            cps.append(_remote(refs[a].at[2 * px + py], refs[n + a].at[j], ssem, rsem, 3 * a + j, (px, py, c)))
    return cps


def _swap_copies(refs, ssem, rsem, landing):
    x, y, c = _place()
    who = 1 - c if landing else c
    cps = []
    for a, ref in enumerate(refs):
        rh = ref.shape[0] // 2
        rows = ref.at[pl.ds(who * rh, rh)]
        cps.append(_remote(rows, rows, ssem, rsem, a, (x, y, 1 - c)))
    return cps


def _small_copies(refs, ssem, rsem, landing):
    pack, slots = refs
    x, y, c = _place()
    cps = []
    for rel in range(1, N_DEV):
        px = 1 - x if (rel >> 2) & 1 else x
        py = 1 - y if (rel >> 1) & 1 else y
        pc = 1 - c if rel & 1 else c
        slot = 4 * px + 2 * py + pc if landing else 4 * x + 2 * y + c
        cps.append(_remote(pack, slots.at[slot], ssem, rsem, rel - 1, (px, py, pc)))
    return cps


def _small_pack(rows, w_dw_grad, d):
    n = len(rows)

    def body(*refs):
        pack = refs[-1]
        pack[...] = jnp.zeros_like(pack)
        for (r, _), ref in zip(rows, refs[:n]):
            pack[r:r + 1, :] = ref[...]
        pack[16:16 + CONV_PAD, :] = refs[n][...]

    return pl.pallas_call(body, name="small_pack", out_shape=jax.ShapeDtypeStruct((SMALL_ROWS, d), F32))(
        *[v for _, v in rows], w_dw_grad)


def _small_sum(pack, slots, place):
    rows, d = pack.shape
    loss_row = 12

    def body(pl_ref, pack_ref, slots_ref, out_ref):
        me = pl_ref[2]
        tot = jnp.where(me == 0, pack_ref[...], slots_ref[0])
        for i in range(1, N_DEV):
            tot = tot + jnp.where(me == i, pack_ref[...], slots_ref[i])
        out_ref[...] = tot
        out_ref[loss_row:loss_row + 1, :] = jnp.zeros((1, d), F32) + jnp.sum(tot[loss_row:loss_row + 1, :])

    return pl.pallas_call(
        body, name="small_sum",
        grid_spec=pltpu.PrefetchScalarGridSpec(
            num_scalar_prefetch=1, grid=(1,),
            in_specs=[pl.BlockSpec((rows, d), lambda i, p: (0, 0)), pl.BlockSpec((N_DEV, rows, d), lambda i, p: (0, 0, 0))],
            out_specs=pl.BlockSpec((rows, d), lambda i, p: (0, 0))),
        out_shape=jax.ShapeDtypeStruct((rows, d), F32),
        compiler_params=_params("arbitrary"),
    )(place, pack, slots)


def kernel(x, norm_mix, norm_mlp, conv_w_in, conv_b_in, conv_w_dw, conv_b_dw, conv_ln_g, conv_ln_b, conv_w_out, conv_b_out, kv_norm, w_kv, attn_w_q, attn_w_o, mlp_w_in, mlp_w_out, final_norm, loss_target, m_norm_mix, m_norm_mlp, m_conv_w_in, m_conv_b_in, m_conv_w_dw, m_conv_b_dw, m_conv_ln_g, m_conv_ln_b, m_conv_w_out, m_conv_b_out, m_kv_norm, m_w_kv, m_attn_w_q, m_attn_w_o, m_mlp_w_in, m_mlp_w_out, m_final_norm, v_norm_mix, v_norm_mlp, v_conv_w_in, v_conv_b_in, v_conv_w_dw, v_conv_b_dw, v_conv_ln_g, v_conv_ln_b, v_conv_w_out, v_conv_b_out, v_kv_norm, v_w_kv, v_attn_w_q, v_attn_w_o, v_mlp_w_in, v_mlp_w_out, v_final_norm):
    _, s, d = x.shape
    dff = mlp_w_in.shape[2] * N_SHARD
    kvw = w_kv.shape[1]
    ds4 = d // N_SHARD
    xi, yi, ci = _place()
    me = 2 * xi + yi
    place = jnp.stack([ci, me, 2 * me + ci]).astype(I32)

    h0 = x.reshape(s, d)
    target = loss_target.reshape(s, d)
    tabs = _rope_tables(s)

    def gather_begin(tag, bufs, n_whole=0):
        plan = functools.partial(_direct_copies, n_whole=n_whole)
        return _split_start(f"gather_start_{tag}", bufs, len(bufs) + 2 * n_whole, plan), plan, n_whole

    def gather_step(later, land=None, swap=None):
        parts, names, whole = [], [], {}
        if land is not None:
            tag, (handle, plan, n_whole) = land
            bufs = _split_wait(f"gather_wait_{tag}", handle, plan, later)
            n = len(bufs) - n_whole
            parts.append((bufs[:n], 2 * n, _relay_copies))
            whole["land"] = bufs[n:]
            names.append(f"relay_{tag}")
        if swap is not None:
            tag, (relayed, whole["swap"]) = swap
            bufs = _split_wait(f"relay_wait_{tag}", relayed, _relay_copies, later)
            parts.append((bufs, len(bufs), _diagonal_copies))
            names.append(f"diagonal_{tag}")
        handles = _split_start_many("start_" + "_".join(names), parts)
        landed = (handles[0], whole["land"]) if land is not None else None
        swapped = (handles[-1], whole["swap"]) if swap is not None else None
        return landed, swapped

    def gather_land(tag, begun, later):
        return gather_step(later, land=(tag, begun))[0]

    def gather_swap(tag, landed, later):
        return gather_step(later, swap=(tag, landed))[1]

    def gather_end(tag, swapped, later):
        handle, whole = swapped
        return _split_wait(f"diagonal_wait_{tag}", handle, _diagonal_copies, later) + whole

    ag_cin = gather_begin("conv_in", [
        _cast_bf16("cast_w_in", conv_w_in, 0, place),
        _pack_small(conv_b_in, conv_w_dw.reshape(CONV_WIDTH, ds4), conv_b_dw, conv_ln_g, conv_ln_b, conv_b_out, place),
    ], n_whole=1)
    ag_cout = gather_begin("conv_out", [_cast_bf16("cast_w_out", conv_w_out, 0, place, ag_cin[0][3])])
    ag_mi0 = gather_begin("mlp_in0", [_cast_bf16("cast_mlp_in0", mlp_w_in, 0, place, ag_cout[0][3])])
    ag_mo0 = gather_begin("mlp_out0", [_cast_bf16("cast_mlp_out0", mlp_w_out, 0, place, ag_mi0[0][3])])
    nm = [norm_mix[0:1], norm_mix[1:2]]
    nmlp = [norm_mlp[0:1], norm_mlp[1:2]]
    kvn = kv_norm.reshape(1, d)
    fin = final_norm.reshape(1, d)
    (y0,) = _rms_fwd("rms_mix0", h0, [nm[0]], after=ag_mo0[0][3])
    land_cin = gather_land("conv_in", ag_cin, y0)
    ag_attn = gather_begin("attn", [
        _cast_bf16("cast_w_kv", w_kv.reshape(1, ds4, kvw), 0, place, land_cin[0][3]),
        _cast_bf16("cast_w_q", attn_w_q, 0, place), _cast_bf16("cast_w_o", attn_w_o, 0, place)])
    ag_mi1 = gather_begin("mlp_in1", [_cast_bf16("cast_mlp_in1", mlp_w_in, 1, place, ag_attn[0][3])])
    ag_mo1 = gather_begin("mlp_out1", [_cast_bf16("cast_mlp_out1", mlp_w_out, 1, place, ag_mi1[0][3])])
    land_cout, swap_cin = gather_step(ag_mo1[0][3], land=("conv_out", ag_cout), swap=("conv_in", land_cin))

    wmi_g = [None, None]
    wmo_f = [None, None]

    w_in_g, small_g = gather_end("conv_in", swap_cin, swap_cin[0][3])
    b_in_f = small_g[:, 0, :].reshape(1, 2 * d)
    b_dw_f = small_g[:, 1, 0:ds4].reshape(1, d)
    ln_g_f = small_g[:, 1, ds4:2 * ds4].reshape(1, d)
    ln_b_f = small_g[:, 2, 0:ds4].reshape(1, d)
    b_out_f = small_g[:, 2, ds4:2 * ds4].reshape(1, d)
    w_dw_f = jnp.transpose(small_g[:, 8:8 + CONV_PAD, 0:ds4], (1, 0, 2)).reshape(CONV_PAD, d)

    def ep_bias(acc, ex, outs, j):
        outs[0][...] = (acc + ex[0][...]).astype(outs[0].dtype)

    def ep_residual(acc, ex, outs, j):
        outs[0][...] = ex[0][...] + acc

    def ep_residual_bias(acc, ex, outs, j):
        outs[0][...] = ex[0][...] + (acc + ex[1][...])

    def ep_relu2(acc, ex, outs, j):
        r = jnp.maximum(acc, 0.0)
        outs[0][...] = r.astype(BF16)
        outs[1][...] = (r * r).astype(BF16)

    by_residue = [(BF16, ("residues", dil)) for dil in DILATIONS]

    def put_by_residue(val, outs, stage):
        _to_residues(val, stage, outs, DILATIONS)

    def ep_rope(acc, ex, outs, j, stage):
        put_by_residue(_rope_apply(acc, ex[0][...], ex[1][...], ex[2][...], 1.0), outs, stage)

    def ep_rope_k(acc, ex, outs, j, stage):
        roped = _rope_apply(acc, ex[0][...], ex[1][...], ex[2][...], 1.0)
        put_by_residue(jnp.where(j == 0, roped, acc), outs, stage)

    def ep_by_residue(acc, ex, outs, j, stage):
        put_by_residue(acc, outs, stage)

    tab_extras = [(t, "rows") for t in tabs]

    def mlp_fwd(idx, h, y, out_weight):
        r, r2 = _matmul(f"mlp_in{idx}", "nn", y, wmi_g[idx], b_kind="col", m=s, n=dff, k=d,
                        outs=[(BF16, "plain"), (BF16, "plain")], epilogue=ep_relu2)
        wmo_f[idx] = out_weight(r2).reshape(dff, d)
        (h_new,) = _matmul(f"mlp_out{idx}", "nn", r2, wmo_f[idx], m=s, n=d, k=dff,
                           outs=[(F32, "plain")], extras=[(h, "ij")], epilogue=ep_residual)
        return h_new, r, r2

    (u,) = _matmul("conv_in", "nn", y0, w_in_g, b_kind="col", m=s, n=2 * d, k=d,
                   outs=[(BF16, "plain")], extras=[(b_in_f, "vec")], epilogue=ep_bias)
    land_mi0, swap_cout = gather_step(u, land=("mlp_in0", ag_mi0), swap=("conv_out", land_cout))
    cpre = _dwconv_fwd(u, w_dw_f, b_dw_f, after=swap_cout[0][3])
    sact = _ln_silu_fwd(cpre, ln_g_f, ln_b_f)
    (w_out_g,) = gather_end("conv_out", swap_cout, sact)
    w_out_f = w_out_g.reshape(d, d)
    (h1,) = _matmul("conv_out", "nn", sact, w_out_f, m=s, n=d, k=d,
                    outs=[(F32, "plain")], extras=[(h0, "ij"), (b_out_f, "vec")], epilogue=ep_residual_bias)
    swap_mi0 = gather_swap("mlp_in0", land_mi0, h1)
    (y1,) = _rms_fwd("rms_mlp0", h1, [nmlp[0]], after=swap_mi0[0][3])
    land_mo0 = gather_land("mlp_out0", ag_mo0, y1)
    (wmi_g[0],) = gather_end("mlp_in0", swap_mi0, land_mo0[0][3])
    land_attn = None

    def out_weight0(r2):
        nonlocal land_attn
        land_attn, swap_mo0 = gather_step(r2, land=("attn", ag_attn), swap=("mlp_out0", land_mo0))
        return gather_end("mlp_out0", swap_mo0, swap_mo0[0][3])[0]

    h2, r0, r0sq = mlp_fwd(0, h1, y1, out_weight0)
    land_mi1, swap_attn = gather_step(h2, land=("mlp_in1", ag_mi1), swap=("attn", land_attn))
    ykv, y2 = _rms_fwd("rms_kv_mix1", h2, [kvn, nm[1]], after=land_mi1[0][3])
    wkv_g, wq_g, wo_g = gather_end("attn", swap_attn, y2)
    wkv_f, wq_f, wo_f = wkv_g.reshape(d, kvw), wq_g.reshape(d, d), wo_g.reshape(d, d)
    kv_parts = _matmul("kv_proj", "nn", ykv, wkv_f, m=s, n=kvw, k=d, tn=kvw // 2,
                       outs=by_residue, extras=tab_extras, epilogue=ep_rope_k, stage=True)
    q_parts = _matmul("q_proj", "nn", y2, wq_f, m=s, n=d, k=d,
                      outs=by_residue, extras=tab_extras, epilogue=ep_rope, stage=True)
    o_parts, lse_parts = [], []
    for dil, q_b, kv_b in zip(DILATIONS, q_parts, kv_parts):
        o_b, lse_b = _attn_fwd(f"attn_fwd_d{dil}", q_b, kv_b)
        o_parts.append(o_b)
        lse_parts.append(lse_b)
    o, lse = _attn_combine(o_parts, lse_parts)
    land_mo1, swap_mi1 = gather_step(o, land=("mlp_out1", ag_mo1), swap=("mlp_in1", land_mi1))
    (h3,) = _matmul("attn_out", "nn", o, wo_f, m=s, n=d, k=d,
                    outs=[(F32, "plain")], extras=[(h2, "ij")], epilogue=ep_residual)
    (y3,) = _rms_fwd("rms_mlp1", h3, [nmlp[1]], after=land_mo1[0][3])
    (wmi_g[1],) = gather_end("mlp_in1", swap_mi1, y3)

    def out_weight1(r2):
        swap_mo1 = gather_swap("mlp_out1", land_mo1, r2)
        return gather_end("mlp_out1", swap_mo1, swap_mo1[0][3])[0]

    h4, r1, r1sq = mlp_fwd(1, h3, y3, out_weight1)
    dh4, dh4b, d_fin, loss_cols = _final_loss(h4, fin, target)

    def ep_relu2_bwd(acc, ex, outs, j):
        outs[0][...] = (acc * (2.0 * ex[0][...].astype(F32))).astype(BF16)

    def mlp_bwd(idx, dhb, y, r, r2):
        (dz,) = _matmul(f"mlp_out{idx}_dx", "nt", dhb, wmo_f[idx], m=s, n=dff, k=d,
                        outs=[(BF16, "plain")], extras=[(r, "ij")], epilogue=ep_relu2_bwd)
        (dwo,) = _matmul(f"mlp_out{idx}_dw", "tn", r2, dhb, m=dff, n=d, k=s,
                         outs=[(BF16, "plain")])
        (dy,) = _matmul(f"mlp_in{idx}_dx", "nt", dz, wmi_g[idx], b_kind="col", m=s, n=d, k=dff,
                        outs=[(BF16, "plain")])
        (dwi,) = _matmul(f"mlp_in{idx}_dw", "tn", y, dz, m=d, n=dff, k=s,
                         outs=[(BF16, "col")])
        return dy, dwi, dwo.reshape(N_SHARD, dff // N_SHARD, d)

    def rs_exchange(tag, grads):
        lands = [lax.empty((N_SHARD, g.shape[1] // 2, g.shape[2]), g.dtype) for g in grads]
        return _split_start(f"sibling_start_{tag}", list(grads) + lands, len(grads), _sibling_copies)

    def rs_send(tag, names, exchanged, later):
        bufs = _split_wait(f"sibling_wait_{tag}", exchanged, _sibling_copies, later)
        n = len(names)
        sums = [_chip_sum(f"chip_sum_{nme}", g, rh, place) for nme, g, rh in zip(names, bufs[:n], bufs[n:])]
        lands = [lax.empty((N_SHARD - 1,) + cs.shape[1:], cs.dtype) for cs in sums]
        return _split_start(f"owners_start_{tag}", sums + lands, 3 * n, _owner_copies)

    def rs_sum(tag, names, sent, later):
        bufs = _split_wait(f"owners_wait_{tag}", sent, _owner_copies, later)
        n = len(names)
        own = [_owner_sum(f"owner_sum_{nme}", cs, rp, place) for nme, cs, rp in zip(names, bufs[:n], bufs[n:])]
        return _split_start(f"swap_start_{tag}", own, n, _swap_copies)

    def rs_end(tag, swapped, later):
        return _split_wait(f"swap_wait_{tag}", swapped, _swap_copies, later)

    dy3, g_wmi1, g_wmo1 = mlp_bwd(1, dh4b, y3, r1, r1sq)
    x_mlp1 = rs_exchange("mlp1", [g_wmi1, g_wmo1])
    dh3, dh3b, d_nmlp1 = _rms_bwd("rms_mlp1_bwd", h3, [(nmlp[1], dy3)], dh4, after=x_mlp1[3])

    do_parts = _matmul("attn_out_dx", "nt", dh3b, wo_f, m=s, n=d, k=d, outs=by_residue, epilogue=ep_by_residue,
                       stage=True)
    (g_wo,) = _matmul("attn_out_dw", "tn", o, dh3b, m=d, n=d, k=s, outs=[(BF16, "plain")])
    rs_mlp1 = rs_send("mlp1", ["mlp_in1", "mlp_out1"], x_mlp1, g_wo)
    lse_res, delta_res = _attn_delta(do_parts[0].reshape(s, d), o, lse, DILATIONS)
    dq_parts, dk_parts, dv_parts = [], [], []
    for dil, q_b, kv_b, do_b, lse_b, dl_b in zip(DILATIONS, q_parts, kv_parts, do_parts, lse_res, delta_res):
        dq_b, dk_b, dv_b = _attn_bwd(f"attn_bwd_d{dil}", q_b, kv_b, do_b, lse_b, dl_b)
        dq_parts.append(dq_b)
        dk_parts.append(dk_b)
        dv_parts.append(dv_b)
    dq = _residue_sum("rope_bwd_q", [(dq_parts, True)], tabs)
    dkv = _residue_sum("rope_bwd_kv", [(dk_parts, True), (dv_parts, False)], tabs)
    (g_wq,) = _matmul("q_proj_dw", "tn", y2, dq, m=d, n=d, k=s, outs=[(BF16, "plain")])
    (dy2,) = _matmul("q_proj_dx", "nt", dq, wq_f, m=s, n=d, k=d, outs=[(BF16, "plain")])
    (g_wkv,) = _matmul("kv_proj_dw", "tn", ykv, dkv, m=d, n=kvw, k=s, outs=[(BF16, "plain")])
    (dykv,) = _matmul("kv_proj_dx", "nt", dkv, wkv_f, m=s, n=d, k=kvw, outs=[(BF16, "plain")])
    x_attn = rs_exchange("attn", [g_wkv.reshape(N_SHARD, ds4, kvw), g_wq.reshape(N_SHARD, ds4, d),
                                  g_wo.reshape(N_SHARD, ds4, d)])
    dh2, dh2b, d_nm1, d_kvn = _rms_bwd("rms_kv_mix1_bwd", h2, [(nm[1], dy2), (kvn, dykv)], dh3, after=x_attn[3])
    rs_attn = rs_send("attn", ["w_kv", "w_q", "w_o"], x_attn, dh2b)

    dy1, g_wmi0, g_wmo0 = mlp_bwd(0, dh2b, y1, r0, r0sq)
    x_mlp0 = rs_exchange("mlp0", [g_wmi0, g_wmo0])
    dh1, dh1b, d_nmlp0, d_b_out = _rms_bwd("rms_mlp0_bwd", h1, [(nmlp[0], dy1)], dh2, want_colsum=True,
                                           after=[x_mlp0[3], rs_attn[3]])

    (dsact,) = _matmul("conv_out_dx", "nt", dh1b, w_out_f, m=s, n=d, k=d, outs=[(BF16, "plain")])
    (g_wout,) = _matmul("conv_out_dw", "tn", sact, dh1b, m=d, n=d, k=s, outs=[(BF16, "plain")])
    rs_mlp0 = rs_send("mlp0", ["mlp_in0", "mlp_out0"], x_mlp0, g_wout)
    dc, d_ln_g, d_ln_b, d_b_dw = _ln_silu_bwd(cpre, ln_g_f, ln_b_f, dsact, after=rs_mlp0[3])
    du, d_w_dw, d_b_in_a, d_b_in_g = _dwconv_bwd(u, w_dw_f, dc)
    (g_win,) = _matmul("conv_in_dw", "tn", y0, du, b_kind="col", m=d, n=2 * d, k=s, outs=[(BF16, "col")])
    x_conv = rs_exchange("conv", [g_win, g_wout.reshape(N_SHARD, ds4, d)])
    (dy0,) = _matmul("conv_in_dx", "nt", du, w_in_g, a_kind="col", b_kind="col", m=s, n=d, k=2 * d,
                     outs=[(BF16, "plain")], after=x_conv[3])
    rs_conv = rs_send("conv", ["w_in", "w_out"], x_conv, dy0)
    dx, _, d_nm0 = _rms_bwd("rms_mix0_bwd", h0, [(nm[0], dy0)], dh1, after=rs_conv[3])

    small_rows = [(0, d_nm0), (1, d_nm1), (2, d_nmlp0), (3, d_nmlp1), (4, d_kvn), (5, d_fin), (6, d_b_dw),
                  (7, d_ln_g), (8, d_ln_b), (9, d_b_out), (10, d_b_in_a), (11, d_b_in_g), (12, loss_cols)]
    x_small = _split_start("small_start", [_small_pack(small_rows, d_w_dw, d),
                                           lax.empty((N_DEV, SMALL_ROWS, d), F32)], N_DEV - 1, _small_copies)

    def big(name, w, m, v, g, layer=0, partial=None):
        shape = w.shape
        w3, m3, v3 = [t.reshape((-1,) + shape[-2:]) for t in (w, m, v)]
        if partial is not None:
            partial = [t.reshape(w3.shape) for t in partial]
        res = _adamw(name, w3, m3, v3, g, layer, partial)
        return [t.reshape(shape) for t in res]

    sw_mlp1 = rs_sum("mlp1", ["mlp_in1", "mlp_out1"], rs_mlp1, x_small[3])
    sw_attn = rs_sum("attn", ["w_kv", "w_q", "w_o"], rs_attn, sw_mlp1[3])
    f_wmi1, f_wmo1 = rs_end("mlp1", sw_mlp1, sw_attn[3])
    p_wmi = big("adam_mlp_in1", mlp_w_in, m_mlp_w_in, v_mlp_w_in, f_wmi1, 1)
    p_wmo = big("adam_mlp_out1", mlp_w_out, m_mlp_w_out, v_mlp_w_out, f_wmo1, 1)
    sw_mlp0 = rs_sum("mlp0", ["mlp_in0", "mlp_out0"], rs_mlp0, [p_wmi[0], p_wmo[0]])
    f_wkv, f_wq, f_wo = rs_end("attn", sw_attn, sw_mlp0[3])
    r_wkv = big("adam_w_kv", w_kv, m_w_kv, v_w_kv, f_wkv)
    r_wq = big("adam_w_q", attn_w_q, m_attn_w_q, v_attn_w_q, f_wq)
    r_wo = big("adam_w_o", attn_w_o, m_attn_w_o, v_attn_w_o, f_wo)
    sw_conv = rs_sum("conv", ["w_in", "w_out"], rs_conv, [r_wkv[0], r_wq[0], r_wo[0]])
    f_wmi0, f_wmo0 = rs_end("mlp0", sw_mlp0, sw_conv[3])
    r_wmi = big("adam_mlp_in0", mlp_w_in, m_mlp_w_in, v_mlp_w_in, f_wmi0, 0, p_wmi)
    r_wmo = big("adam_mlp_out0", mlp_w_out, m_mlp_w_out, v_mlp_w_out, f_wmo0, 0, p_wmo)
    f_win, f_wout = rs_end("conv", sw_conv, [r_wmi[0], r_wmo[0]])
    r_win = big("adam_w_in", conv_w_in, m_conv_w_in, v_conv_w_in, f_win)
    r_wout = big("adam_w_out", conv_w_out, m_conv_w_out, v_conv_w_out, f_wout)

    small_pack, small_slots = _split_wait("small_wait", x_small, _small_copies, r_wout[0])
    red = _small_sum(small_pack, small_slots, place)
    loss = red[12, 0]
    g_norm_mix = red[0:2]
    g_norm_mlp = red[2:4]
    g_kv_norm = red[4:5]
    g_final = red[5:6]

    def my_cols(row):
        return lax.dynamic_slice(red, (row, me * ds4), (1, ds4))

    g_b_dw, g_ln_g, g_ln_b, g_b_out = my_cols(6), my_cols(7), my_cols(8), my_cols(9)
    half_in = 2 * d // N_SHARD
    b_in_row = 10 + me // 2
    g_b_in = lax.dynamic_slice(red, (b_in_row, (me % 2) * half_in), (1, half_in))
    g_w_dw = lax.dynamic_slice(red, (16, me * ds4), (CONV_WIDTH, ds4))

    sm_w =[norm_mix, norm_mlp, conv_b_in, conv_w_dw.reshape(CONV_WIDTH, ds4), conv_b_dw, conv_ln_g, conv_ln_b,
            conv_b_out, kv_norm.reshape(1, d), final_norm.reshape(1, d)]
    sm_m = [m_norm_mix, m_norm_mlp, m_conv_b_in, m_conv_w_dw.reshape(CONV_WIDTH, ds4), m_conv_b_dw, m_conv_ln_g,
            m_conv_ln_b, m_conv_b_out, m_kv_norm.reshape(1, d), m_final_norm.reshape(1, d)]
    sm_v = [v_norm_mix, v_norm_mlp, v_conv_b_in, v_conv_w_dw.reshape(CONV_WIDTH, ds4), v_conv_b_dw, v_conv_ln_g,
            v_conv_ln_b, v_conv_b_out, v_kv_norm.reshape(1, d), v_final_norm.reshape(1, d)]
    sm_g = [g_norm_mix, g_norm_mlp, g_b_in, g_w_dw, g_b_dw, g_ln_g, g_ln_b, g_b_out, g_kv_norm, g_final]
    sm_d, sm_nm, sm_nv = _adam_small(sm_w, sm_m, sm_v, sm_g)
    shapes = [norm_mix.shape, norm_mlp.shape, conv_b_in.shape, conv_w_dw.shape, conv_b_dw.shape, conv_ln_g.shape,
              conv_ln_b.shape, conv_b_out.shape, kv_norm.shape, final_norm.shape]
    sm_g, sm_d, sm_nm, sm_nv = [[t.reshape(sh) for t, sh in zip(lst, shapes)] for lst in (sm_g, sm_d, sm_nm, sm_nv)]

    def order(sm, idx):
        return [sm[0], sm[1], r_win[idx], sm[2], sm[3], sm[4], sm[5], sm[6], r_wout[idx], sm[7], sm[8],
                r_wkv[idx], r_wq[idx], r_wo[idx], r_wmi[idx], r_wmo[idx], sm[9]]

    return (loss, dx.reshape(x.shape), *order(sm_g, 0), *order(sm_d, 1), *order(sm_nm, 2), *order(sm_nv, 3))
```

```python
import functools
import math

import jax
import jax.numpy as jnp
from jax import lax
from jax.experimental import pallas as pl
from jax.experimental.pallas import tpu as pltpu

F32 = jnp.float32
BF16 = jnp.bfloat16
I32 = jnp.int32

NORM_EPS = 1e-6
LN_EPS = 1e-5
HEAD_DIM = 128
N_KV_HEADS = 4
ROT_DIM = 32
ROPE_THETA = 500000.0
CONV_WIDTH = 31
CONV_PAD = 32
ATT_BLOCK = 128
ATT_STEP_BLOCKS = 16
DILATIONS = (1, 4, 16)
ADAM_LR = 0.001
ADAM_B1 = 0.9
ADAM_B2 = 0.999
ADAM_EPS = 1e-08
ADAM_WD = 0.01
ADAM_STEP = 10
N_SHARD = 4
N_DEV = 8
LANES = 128
VMEM_LIMIT = 48 * 1024 * 1024
MM_TM, MM_TN,MM_TK = 1024, 1024, 2048
ROW_TILE = 512
CONV_CB = 128
CONV_T = 128
SMALL_ROWS = 48
MESH = pl.DeviceIdType.MESH
ANY = pl.BlockSpec(memory_space=pl.ANY)
HBM = pl.BlockSpec(memory_space=pltpu.HBM)
SEM = pl.BlockSpec(memory_space=pltpu.SEMAPHORE)
SPLIT_EFFECT = pltpu.SideEffectType.DATAFLOW_SIDE_EFFECTING


def _params(*sem):
    return pltpu.CompilerParams(dimension_semantics=sem, vmem_limit_bytes=VMEM_LIMIT)


def _sigmoid(x):
    return 1.0 / (1.0 + jnp.exp(-x))


def _wspec(kind, arr_shape, br, bc, pick):
    if kind == "plain":
        return pl.BlockSpec((br, bc), pick)
    per = arr_shape[2] // bc

    def idx(*g):
        rb, cb = pick(*g)
        return (cb // per, rb, cb % per)

    return pl.BlockSpec((None, br, bc), idx)


def _stage_shape(rows, w):
    return (w // LANES, rows, LANES)


def _to_residues(val, stage_ref, out_refs, dils):
    planes, rows, _ = stage_ref.shape
    for c in range(planes):
        stage_ref[c] = val[:, c * LANES:(c + 1) * LANES]
    for out_ref, dil in zip(out_refs, dils):
        if dil == 1:
            out_ref[0] = val.astype(out_ref.dtype)
            continue
        for r in range(dil):
            for c in range(planes):
                out_ref[r, :, c * LANES:(c + 1) * LANES] = stage_ref.at[c][pl.ds(r, rows // dil, stride=dil), :].astype(
                    out_ref.dtype)


def _from_residues(src_ref, stage_ref, dil):
    planes, rows, _ = stage_ref.shape
    if dil == 1:
        return lambda c: src_ref[0, :, c * LANES:(c + 1) * LANES].astype(F32)
    for r in range(dil):
        for c in range(planes):
            stage_ref.at[c][pl.ds(r, rows // dil, stride=dil), :] = src_ref[r, :, c * LANES:(c + 1) * LANES].astype(F32)
    return lambda c: stage_ref[c]


def _matmul(name, mode, a, b, *, m, n, k, tm=MM_TM, tn=MM_TN, tk=MM_TK, a_kind="plain", b_kind="plain", outs,
            extras=(), epilogue=None, stage=False, after=None):
    tm, tn, tk = min(tm, m), min(tn, n), min(tk, k)
    if b_kind == "col" and mode in ("nn", "tn"):
        tn = min(tn, n // b.shape[0])
    if b_kind == "col" and mode == "nt":
        tk = min(tk, k // b.shape[0])
    if a_kind == "col":
        assert mode == "nt"
        tk = min(tk, k // a.shape[0])
    if any(kind == "col" for _, kind in outs):
        tn = min(tn, n // N_SHARD)
    assert m % tm == 0 and n % tn == 0 and k % tk == 0, (name, m, n, k, tm, tn, tk)
    nk = k // tk
    grid = (m // tm, n // tn, nk)
    if mode == "nn":
        a_spec = pl.BlockSpec((tm, tk), lambda i, j, kk: (i, kk))
        b_spec = _wspec(b_kind, b.shape, tk, tn, lambda i, j, kk: (kk, j))
        dims = (((1,), (0,)), ((), ()))
    elif mode == "nt":
        a_spec = _wspec(a_kind, a.shape, tm, tk, lambda i, j, kk: (i, kk))
        b_spec = _wspec(b_kind, b.shape, tn, tk, lambda i, j, kk: (j, kk))
        dims = (((1,), (1,)), ((), ()))
    else:
        a_spec = pl.BlockSpec((tk, tm), lambda i, j, kk: (kk, i))
        b_spec = _wspec(b_kind, b.shape, tk, tn, lambda i, j, kk: (kk, j))
        dims = (((0,), (0,)), ((), ()))
    out_shape, out_specs = [], []
    for dtype, kind in outs:
        if isinstance(kind, tuple):
            dil = kind[1]
            out_shape.append(jax.ShapeDtypeStruct((dil, m // dil, n), dtype))
            out_specs.append(pl.BlockSpec((dil, tm // dil, tn), lambda i, j, kk: (0, i, j)))
            continue
        shape = (m, n) if kind == "plain" else (N_SHARD, m, n // N_SHARD)
        out_shape.append(jax.ShapeDtypeStruct(shape, dtype))
        out_specs.append(_wspec(kind, shape, tm, tn, lambda i, j, kk: (i, j)))
    n_ex = len(extras)
    deps = [] if after is None else [after]
    ex_specs = {"ij": pl.BlockSpec((tm, tn), lambda i, j, kk: (i, j)),
                "vec": pl.BlockSpec((1, tn), lambda i, j, kk: (0, j)),
                "rows": pl.BlockSpec((tm, LANES), lambda i, j, kk: (i, 0))}
    out0 = 2 + n_ex + len(deps)

    def body(*refs):
        a_ref, b_ref = refs[0], refs[1]
        ex_refs = refs[2:2 + n_ex]
        out_refs = refs[out0:out0 + len(outs)]
        j = pl.program_id(1)

        def finish(res):
            if epilogue is None:
                out_refs[0][...] = res.astype(out_refs[0].dtype)
            elif stage:
                epilogue(res, ex_refs, out_refs, j, refs[-1])
            else:
                epilogue(res, ex_refs, out_refs, j)

        prod = lax.dot_general(a_ref[...], b_ref[...], dims, preferred_element_type=F32)
        if nk == 1:
            finish(prod)
            return
        acc_ref = refs[out0 + len(outs)]
        kk = pl.program_id(2)

        @pl.when(kk == 0)
        def _():
            acc_ref[...] = prod

        @pl.when(kk > 0)
        def _():
            acc_ref[...] += prod

        @pl.when(kk == nk - 1)
        def _():
            finish(acc_ref[...])

    res = pl.pallas_call(
        body, name=name, grid=grid,
        in_specs=[a_spec, b_spec] + [ex_specs[how] for _, how in extras] + [ANY] * len(deps),
        out_specs=out_specs, out_shape=out_shape,
        scratch_shapes=[pltpu.VMEM((tm, tn), F32)] * (nk > 1) + [pltpu.VMEM(_stage_shape(tm, tn), F32)] * bool(stage),
        compiler_params=_params("parallel", "parallel", "arbitrary"),
    )(a, b, *[e for e, _ in extras], *deps)
    return res


def _rope_tables(seq):
    half = ROT_DIM // 2
    pos = jnp.arange(seq, dtype=F32)
    inv = ROPE_THETA ** (-jnp.arange(0, ROT_DIM, 2, dtype=F32) / ROT_DIM)
    ang = pos[:, None] * inv[None, :]
    cos, sin = jnp.cos(ang), jnp.sin(ang)
    zeros = jnp.zeros((seq, HEAD_DIM - ROT_DIM), F32)
    ctab = jnp.concatenate([cos, cos, zeros + 1.0], axis=1)
    atab = jnp.concatenate([-sin, jnp.zeros((seq, half), F32), zeros], axis=1)
    btab = jnp.concatenate([jnp.zeros((seq, half), F32), sin, zeros], axis=1)
    return ctab, atab, btab


def _rope_apply(x, ctab, atab, btab, sign):
    w = x.shape[1]
    reps = w // HEAD_DIM
    half = ROT_DIM // 2
    c = jnp.tile(ctab, (1, reps))
    a = jnp.tile(atab, (1, reps))
    b = jnp.tile(btab, (1, reps))
    up = pltpu.roll(x, w - half, 1)
    down = pltpu.roll(x, half, 1)
    return x * c + sign * (up * a + down * b)


def _rows(t, w):
    return pl.BlockSpec((t, w), lambda i: (i, 0))


def _fixed(shape):
    nd = len(shape)
    return pl.BlockSpec(shape, lambda i: (0,) * nd)


def _behind(after):
    deps = [] if after is None else (list(after) if isinstance(after, (list, tuple)) else [after])
    return deps, [ANY] * len(deps)


def _rms_fwd(name, x, gains, after=None):
    s, d = x.shape
    t = min(ROW_TILE, s)
    ng = len(gains)
    deps, dep_specs = _behind(after)

    def body(*all_refs):
        x_ref, refs = all_refs[len(deps)], all_refs[len(deps) + 1:]
        xv = x_ref[...]
        r = lax.rsqrt(jnp.mean(xv * xv, axis=-1, keepdims=True) + NORM_EPS)
        xn = xv * r
        for g_ref, y_ref in zip(refs[:ng], refs[ng:]):
            y_ref[...] = (xn * g_ref[...]).astype(BF16)

    return pl.pallas_call(
        body, name=name, grid=(s // t,),
        in_specs=dep_specs + [_rows(t, d)] + [_fixed((1, d))] * ng,
        out_specs=[_rows(t, d)] * ng,
        out_shape=[jax.ShapeDtypeStruct((s, d), BF16)] * ng,
        compiler_params=_params("parallel"),
    )(*deps, x, *gains)


def _rms_bwd(name, x, pairs, dh_in, want_colsum=False, after=None):
    s, d = x.shape
    n_p = len(pairs)
    t = min(ROW_TILE // n_p, s)
    deps, dep_specs = _behind(after)

    def body(*all_refs):
        x_ref, dh_ref, refs = all_refs[len(deps)], all_refs[len(deps) + 1], all_refs[len(deps) + 2:]
        g_refs = refs[:n_p]
        dy_refs = refs[n_p:2 * n_p]
        dh_out, dhb_out = refs[2 * n_p], refs[2 * n_p + 1]
        dg_refs = refs[2 * n_p + 2:2 * n_p + 2 + n_p]
        cs_ref = refs[-1] if want_colsum else None
        i = pl.program_id(0)
        xv = x_ref[...]
        r = lax.rsqrt(jnp.mean(xv * xv, axis=-1, keepdims=True) + NORM_EPS)
        xn = xv * r
        dh = dh_ref[...]
        for g_ref, dy_ref, dg_ref in zip(g_refs, dy_refs, dg_refs):
            dy = dy_ref[...].astype(F32)
            u = dy * g_ref[...]
            dh = dh + r * (u - xn * jnp.mean(u * xn, axis=-1, keepdims=True))
            part = jnp.sum(dy * xn, axis=0, keepdims=True)

            @pl.when(i == 0)
            def _():
                dg_ref[...] = part

            @pl.when(i > 0)
            def _():
                dg_ref[...] += part

        dh_out[...] = dh
        dhb_out[...] = dh.astype(BF16)
        if want_colsum:
            col = jnp.sum(dh, axis=0, keepdims=True)

            @pl.when(i == 0)
            def _():
                cs_ref[...] = col

            @pl.when(i > 0)
            def _():
                cs_ref[...] += col

    n_vec = n_p + (1 if want_colsum else 0)
    return pl.pallas_call(
        body, name=name, grid=(s // t,),
        in_specs=dep_specs + [_rows(t, d), _rows(t, d)] + [_fixed((1, d))] * n_p + [_rows(t, d)] * n_p,
        out_specs=[_rows(t, d), _rows(t, d)] + [_fixed((1, d))] * n_vec,
        out_shape=[jax.ShapeDtypeStruct((s, d), F32), jax.ShapeDtypeStruct((s, d), BF16)]
        + [jax.ShapeDtypeStruct((1, d), F32)] * n_vec,
        compiler_params=_params("arbitrary"),
    )(*deps, x, dh_in, *[g for g, _ in pairs], *[dy for _, dy in pairs])


def _final_loss(x, g, target):
    s, d = x.shape
    t = min(ROW_TILE, s)

    def body(x_ref, g_ref, t_ref, dh_out, dhb_out, dg_ref, loss_ref):
        i = pl.program_id(0)
        xv = x_ref[...]
        gv = g_ref[...]
        r = lax.rsqrt(jnp.mean(xv * xv, axis=-1, keepdims=True) + NORM_EPS)
        xn = xv * r
        diff = xn * gv - t_ref[...]
        dy = diff / d
        u = dy * gv
        dh = r * (u - xn * jnp.mean(u * xn, axis=-1, keepdims=True))
        dh_out[...] = dh
        dhb_out[...] = dh.astype(BF16)
        dg = jnp.sum(dy * xn, axis=0, keepdims=True)
        lc = jnp.sum(0.5 * diff * dy, axis=0, keepdims=True)

        @pl.when(i == 0)
        def _():
            dg_ref[...] = dg
            loss_ref[...] = lc

        @pl.when(i > 0)
        def _():
            dg_ref[...] += dg
            loss_ref[...] += lc

    return pl.pallas_call(
        body, name="final_loss", grid=(s // t,),
        in_specs=[_rows(t, d), _fixed((1, d)), _rows(t, d)],
        out_specs=[_rows(t, d), _rows(t, d), _fixed((1, d)), _fixed((1, d))],
        out_shape=[jax.ShapeDtypeStruct((s, d), F32), jax.ShapeDtypeStruct((s, d), BF16),
                   jax.ShapeDtypeStruct((1, d), F32), jax.ShapeDtypeStruct((1, d), F32)],
        compiler_params=_params("arbitrary"),
    )(x, g, target)


def _ln_silu_fwd(c, g, b):
    s, d = c.shape
    t = min(ROW_TILE, s)

    def body(c_ref, g_ref, b_ref, s_ref):
        cv = c_ref[...]
        mu = jnp.mean(cv, axis=-1, keepdims=True)
        xc = cv - mu
        rs = lax.rsqrt(jnp.mean(xc * xc, axis=-1, keepdims=True) + LN_EPS)
        ln = xc * rs * g_ref[...] + b_ref[...]
        s_ref[...] = (ln * _sigmoid(ln)).astype(BF16)

    return pl.pallas_call(
        body, name="ln_silu_fwd", grid=(s // t,),
        in_specs=[_rows(t, d), _fixed((1, d)), _fixed((1, d))],
        out_specs=_rows(t, d), out_shape=jax.ShapeDtypeStruct((s, d), BF16),
        compiler_params=_params("parallel"),
    )(c, g, b)


def _ln_silu_bwd(c, g, b, ds, after=None):
    s, d = c.shape
    t = min(ROW_TILE, s)
    deps, dep_specs = _behind(after)

    def body(*all_refs):
        c_ref, g_ref, b_ref, ds_ref, dc_ref, dg_ref, db_ref, dbdw_ref = all_refs[len(deps):]
        i = pl.program_id(0)
        cv = c_ref[...]
        gv = g_ref[...]
        mu = jnp.mean(cv, axis=-1, keepdims=True)
        xc = cv - mu
        rs = lax.rsqrt(jnp.mean(xc * xc, axis=-1, keepdims=True) + LN_EPS)
        nrm = xc * rs
        ln = nrm * gv + b_ref[...]
        sig = _sigmoid(ln)
        dln = ds_ref[...].astype(F32) * sig * (1.0 + ln * (1.0 - sig))
        dn = dln * gv
        dc = rs * (dn - jnp.mean(dn, axis=-1, keepdims=True)
                   - nrm * jnp.mean(dn * nrm, axis=-1, keepdims=True))
        dc_ref[...] = dc
        pg = jnp.sum(dln * nrm, axis=0, keepdims=True)
        pb = jnp.sum(dln, axis=0, keepdims=True)
        pc = jnp.sum(dc, axis=0, keepdims=True)

        @pl.when(i == 0)
        def _():
            dg_ref[...] = pg
            db_ref[...] = pb
            dbdw_ref[...] = pc

        @pl.when(i > 0)
        def _():
            dg_ref[...] += pg
            db_ref[...] += pb
            dbdw_ref[...] += pc

    return pl.pallas_call(
        body, name="ln_silu_bwd", grid=(s // t,),
        in_specs=dep_specs + [_rows(t, d), _fixed((1, d)), _fixed((1, d)), _rows(t, d)],
        out_specs=[_rows(t, d)] + [_fixed((1, d))] * 3,
        out_shape=[jax.ShapeDtypeStruct((s, d), F32)] + [jax.ShapeDtypeStruct((1, d), F32)] * 3,
        compiler_params=_params("arbitrary"),
    )(*deps, c, g, b, ds)


def _residue_spec(dil, t, w):
    return pl.BlockSpec((dil, t // dil, w), lambda i: (0, i, 0))


def _attn_combine(o_list, lse_list):
    dil0, sd0, d = o_list[0].shape
    s = dil0 * sd0
    lw = lse_list[0].shape[2]
    group = d // HEAD_DIM // N_KV_HEADS
    t = min(ROW_TILE, s)
    nb = len(o_list)
    dils = [o.shape[0] for o in o_list]

    def body(*refs):
        o_out, l_out = refs[2 * nb], refs[2 * nb + 1]
        o_stage, l_stage = refs[2 * nb + 2:3 * nb + 2], refs[3 * nb + 2:]
        o_planes = [_from_residues(src, stage, dil) for src, stage, dil in zip(refs[:nb], o_stage, dils)]
        l_planes = [_from_residues(src, stage, dil) for src, stage, dil in zip(refs[nb:2 * nb], l_stage, dils)]
        for kh in range(N_KV_HEADS):
            ls = [plane(kh) for plane in l_planes]
            mx = ls[0]
            for l in ls[1:]:
                mx = jnp.maximum(mx, l)
            es = [jnp.exp(l - mx) for l in ls]
            den = es[0]
            for e in es[1:]:
                den = den + e
            l_out[:, kh * LANES:(kh + 1) * LANES] = mx + jnp.log(den)
            ws = [e / den for e in es]
            for g in range(group):
                h = kh * group + g
                acc = jnp.zeros((t, HEAD_DIM), F32)
                for plane, w in zip(o_planes, ws):
                    acc = acc + w[:, g:g + 1] * plane(h)
                o_out[:, h * HEAD_DIM:(h + 1) * HEAD_DIM] = acc.astype(BF16)

    return pl.pallas_call(
        body, name="attn_combine", grid=(s // t,),
        in_specs=[_residue_spec(dil, t, d) for dil in dils] + [_residue_spec(dil, t, lw) for dil in dils],
        out_specs=[_rows(t, d), _rows(t, lw)],
        out_shape=[jax.ShapeDtypeStruct((s, d), BF16), jax.ShapeDtypeStruct((s, lw), F32)],
        scratch_shapes=[pltpu.VMEM(_stage_shape(t, d), F32)] * nb + [pltpu.VMEM(_stage_shape(t, lw), F32)] * nb,
        compiler_params=_params("parallel"),
    )(*o_list, *lse_list)


def _attn_delta(do, o, lse, dils):
    s, d = o.shape
    lw = lse.shape[1]
    group = d // HEAD_DIM // N_KV_HEADS
    t = min(ROW_TILE, s)
    nd = len(dils)

    def body(do_ref, o_ref, lse_ref, *refs):
        stage = refs[-1]
        lane = lax.broadcasted_iota(I32, (t, LANES), 1)
        planes = []
        for kh in range(N_KV_HEADS):
            out = jnp.zeros((t, LANES), F32)
            for g in range(group):
                cols = slice((kh * group + g) * HEAD_DIM, (kh * group + g + 1) * HEAD_DIM)
                v = jnp.sum(do_ref[:, cols].astype(F32) * o_ref[:, cols].astype(F32), axis=-1, keepdims=True)
                out = jnp.where(lane == g, v, out)
            planes.append(out)
        _to_residues(lse_ref[...], stage, refs[:nd], dils)
        _to_residues(jnp.concatenate(planes, axis=1), stage, refs[nd:2 * nd], dils)

    res = pl.pallas_call(
        body, name="attn_delta", grid=(s // t,),
        in_specs=[_rows(t, d), _rows(t, d), _rows(t, lw)],
        out_specs=[_residue_spec(dil, t, lw) for dil in dils] * 2,
        out_shape=[jax.ShapeDtypeStruct((dil, s // dil, lw), F32) for dil in dils] * 2,
        scratch_shapes=[pltpu.VMEM(_stage_shape(t, lw), F32)],
        compiler_params=_params("parallel"),
    )(do, o, lse)
    return res[:nd], res[nd:]


def _residue_sum(name, groups, tabs):
    first = groups[0][0][0]
    s, w = first.shape[0] * first.shape[1], first.shape[2]
    t = min(ROW_TILE, s)
    flat = [p for parts, _ in groups for p in parts]

    def body(*refs):
        c_ref, a_ref, b_ref = refs[len(flat):len(flat) + 3]
        out = refs[len(flat) + 3]
        stages = refs[len(flat) + 4:]
        k = 0
        for gi, (parts, rotate) in enumerate(groups):
            planes = [_from_residues(refs[k + i], stages[k + i], p.shape[0]) for i, p in enumerate(parts)]
            k += len(parts)
            for c in range(w // LANES):
                tot = planes[0](c)
                for plane in planes[1:]:
                    tot = tot + plane(c)
                if rotate:
                    tot = _rope_apply(tot, c_ref[...], a_ref[...], b_ref[...], -1.0)
                out[:, gi * w + c * LANES:gi * w + (c + 1) * LANES] = tot.astype(BF16)

    return pl.pallas_call(
        body, name=name, grid=(s // t,),
        in_specs=[_residue_spec(p.shape[0], t, w) for p in flat] + [_rows(t, HEAD_DIM)] * 3,
        out_specs=_rows(t, len(groups) * w), out_shape=jax.ShapeDtypeStruct((s, len(groups) * w), BF16),
        scratch_shapes=[pltpu.VMEM(_stage_shape(t, w), F32) for _ in flat],
        compiler_params=_params("parallel"),
    )(*flat, *tabs)


def _dwconv_fwd(u, w_dw, b_dw, after=None):
    s, d2 = u.shape
    d = d2 // 2
    cb = min(CONV_CB, d)
    nblk = d // cb
    tt = min(CONV_T, s)
    deps, dep_specs = _behind(after)

    def body(*all_refs):
        ua_ref, ug_ref, w_ref, b_ref, c_ref, xp_ref = all_refs[len(deps):]
        gl =ua_ref[...].astype(F32) * _sigmoid(ug_ref[...].astype(F32))
        xp_ref[0:CONV_PAD, :] = jnp.zeros((CONV_PAD, cb), F32)
        xp_ref[CONV_PAD:, :] = gl
        wv = w_ref[...]
        bv = b_ref[...]
        for t0 in range(0, s, tt):
            acc = jnp.zeros((tt, cb), F32) + bv
            for kk in range(CONV_WIDTH):
                off = t0 + CONV_PAD - (CONV_WIDTH - 1) + kk
                acc = acc + wv[kk:kk + 1, :] * xp_ref[off:off + tt, :]
            c_ref[t0:t0 + tt, :] = acc

    return pl.pallas_call(
        body, name="dwconv_fwd", grid=(nblk,),
        in_specs=dep_specs + [pl.BlockSpec((s, cb), lambda j: (0, j)), pl.BlockSpec((s, cb), lambda j: (0, j + nblk)),
                              pl.BlockSpec((CONV_PAD, cb), lambda j: (0, j)), pl.BlockSpec((1, cb), lambda j: (0, j))],
        out_specs=pl.BlockSpec((s, cb), lambda j: (0, j)),
        out_shape=jax.ShapeDtypeStruct((s, d), F32),
        scratch_shapes=[pltpu.VMEM((s + CONV_PAD, cb), F32)],
        compiler_params=_params("parallel"),
    )(*deps, u, u, w_dw, b_dw)


def _dwconv_bwd(u, w_dw, dc):
    s, d2 = u.shape
    d = d2 // 2
    cb = min(CONV_CB, d)
    nblk = d // cb
    tt = min(CONV_T, s)

    def body(ua_ref, ug_ref, w_ref, dc_ref, du_ref, dw_ref, dba_ref, dbg_ref, glp_ref, dcp_ref, acc_ref):
        a = ua_ref[...].astype(F32)
        sig = _sigmoid(ug_ref[...].astype(F32))
        glp_ref[0:CONV_PAD, :] = jnp.zeros((CONV_PAD, cb), F32)
        glp_ref[CONV_PAD:, :] = a * sig
        dcp_ref[0:s, :] = dc_ref[...]
        dcp_ref[s:, :] = jnp.zeros((CONV_PAD, cb), F32)
        acc_ref[...] = jnp.zeros_like(acc_ref)
        wv = w_ref[...]
        dba = jnp.zeros((1, cb), F32)
        dbg = jnp.zeros((1, cb), F32)
        for t0 in range(0, s, tt):
            dgl = jnp.zeros((tt, cb), F32)
            dct = dc_ref[t0:t0 + tt, :]
            for kk in range(CONV_WIDTH):
                off = t0 + (CONV_WIDTH - 1) - kk
                dgl = dgl + wv[kk:kk + 1, :] * dcp_ref[off:off + tt, :]
                goff = t0 + CONV_PAD - (CONV_WIDTH - 1) + kk
                prod = dct * glp_ref[goff:goff + tt, :]
                acc_ref[8 * kk:8 * kk + 8, :] += jnp.sum(prod.reshape(tt // 8, 8, cb), axis=0)
            at = ua_ref[t0:t0 + tt, :].astype(F32)
            st = _sigmoid(ug_ref[t0:t0 + tt, :].astype(F32))
            da = dgl * st
            dg = dgl * at * st * (1.0 - st)
            du_ref[0, t0:t0 + tt, :] = da.astype(BF16)
            du_ref[1, t0:t0 + tt, :] = dg.astype(BF16)
            dba = dba + jnp.sum(da, axis=0, keepdims=True)
            dbg = dbg + jnp.sum(dg, axis=0, keepdims=True)
        dba_ref[...] = dba
        dbg_ref[...] = dbg
        for kk in range(CONV_WIDTH):
            dw_ref[kk:kk + 1, :] = jnp.sum(acc_ref[8 * kk:8 * kk + 8, :], axis=0, keepdims=True)
        dw_ref[CONV_WIDTH:, :] = jnp.zeros((CONV_PAD - CONV_WIDTH, cb), F32)

    blk = pl.BlockSpec((s, cb), lambda j: (0, j))
    vec = pl.BlockSpec((1, cb), lambda j: (0, j))
    return pl.pallas_call(
        body, name="dwconv_bwd", grid=(nblk,),
        in_specs=[blk, pl.BlockSpec((s, cb), lambda j: (0, j + nblk)),
                  pl.BlockSpec((CONV_PAD, cb), lambda j: (0, j)), blk],
        out_specs=[pl.BlockSpec((2, s, cb), lambda j: (0, 0, j)), pl.BlockSpec((CONV_PAD, cb), lambda j: (0, j)),
                   vec, vec],
        out_shape=[jax.ShapeDtypeStruct((2, s, d), BF16), jax.ShapeDtypeStruct((CONV_PAD, d), F32),
                   jax.ShapeDtypeStruct((1, d), F32), jax.ShapeDtypeStruct((1, d), F32)],
        scratch_shapes=[pltpu.VMEM((s + CONV_PAD, cb), F32), pltpu.VMEM((s + CONV_PAD, cb), F32),
                        pltpu.VMEM((8 * CONV_PAD, cb), F32)],
        compiler_params=_params("parallel"),
    )(u, u, w_dw, dc)


def _stack_heads(x, group):
    return jnp.concatenate([x[:, g * HEAD_DIM:(g + 1) * HEAD_DIM] for g in range(group)], axis=0)


def _unstack_heads(x, group):
    return jnp.concatenate([x[g * ATT_BLOCK:(g + 1) * ATT_BLOCK, :] for g in range(group)], axis=1)


def _stack_cols(x, group):
    return jnp.concatenate([x[:, g:g + 1] for g in range(group)], axis=0)


def _band_bias(group):
    rows = group * ATT_BLOCK
    row = lax.broadcasted_iota(I32, (rows, 2 * ATT_BLOCK), 0) % ATT_BLOCK
    col = lax.broadcasted_iota(I32, (rows, 2 * ATT_BLOCK), 1)
    band = jnp.where((col >= row) & (col <= row + ATT_BLOCK), 0.0, -jnp.inf).astype(F32)
    first = jnp.where(lax.broadcasted_iota(I32, (1, 2 * ATT_BLOCK), 1) >= ATT_BLOCK, 0.0, -jnp.inf).astype(F32)
    return band, first


def _masked_scores(qs, kw, band_ref, first_ref, nb, scale):
    sc = lax.dot_general(qs, kw, (((1,), (1,)), ((), ())), preferred_element_type=F32) * scale + band_ref[...]
    return sc + jnp.where(nb > 0, 0.0, first_ref[...])


def _window(ref, nb):
    prev = pl.multiple_of(jnp.maximum(nb - 1, 0) * ATT_BLOCK, ATT_BLOCK)
    cur = pl.multiple_of(nb * ATT_BLOCK, ATT_BLOCK)
    return jnp.concatenate([ref[pl.ds(prev, ATT_BLOCK), :], ref[pl.ds(cur, ATT_BLOCK), :]], axis=0)


def _residues_per_step(dil, nblk):
    return max(1, min(dil, ATT_STEP_BLOCKS // nblk))


def _attn_fwd(name, q, kv):
    dil, sd, d = q.shape
    group = d // HEAD_DIM // N_KV_HEADS
    gw = group * HEAD_DIM
    nblk = sd // ATT_BLOCK
    scale = 1.0 / math.sqrt(HEAD_DIM)
    rb = _residues_per_step(dil, nblk)

    def body(q_all, k_all, v_all, band_ref, first_ref, o_all, lse_all):
        lane = lax.broadcasted_iota(I32, (ATT_BLOCK, LANES), 1)
        for rr in range(rb):
            q_ref, k_ref, v_ref, o_ref, lse_ref = [ref.at[rr] for ref in (q_all, k_all, v_all, o_all, lse_all)]

            def step(nb, carry):
                rows = pl.ds(pl.multiple_of(nb * ATT_BLOCK, ATT_BLOCK), ATT_BLOCK)
                qs = _stack_heads(q_ref[rows, :], group)
                kw = _window(k_ref, nb)
                vw = _window(v_ref, nb)
                sc = _masked_scores(qs, kw, band_ref, first_ref, nb, scale)
                mx = jnp.max(sc, axis=-1, keepdims=True)
                p = jnp.exp(sc - mx)
                l = jnp.sum(p, axis=-1, keepdims=True)
                o = jnp.dot(p.astype(BF16), vw, preferred_element_type=F32) / l
                o_ref[rows, :] = _unstack_heads(o, group).astype(BF16)
                lse = mx + jnp.log(l)
                out = jnp.zeros((ATT_BLOCK, LANES), F32)
                for g in range(group):
                    out = jnp.where(lane == g, lse[g * ATT_BLOCK:(g + 1) * ATT_BLOCK, :], out)
                lse_ref[rows, :] = out
                return carry

            lax.fori_loop(0, nblk, step, 0, unroll=min(2, nblk))

    kvh = N_KV_HEADS
    band, first = _band_bias(group)
    qspec = pl.BlockSpec((rb, sd, gw), lambda r, h: (r, 0, h))
    kspec = pl.BlockSpec((rb, sd, HEAD_DIM), lambda r, h: (r, 0, h))
    return pl.pallas_call(
        body, name=name, grid=(dil // rb, kvh),
        in_specs=[qspec, kspec, pl.BlockSpec((rb, sd, HEAD_DIM), lambda r, h: (r, 0, kvh + h)),
                  pl.BlockSpec(band.shape, lambda r, h: (0, 0)), pl.BlockSpec(first.shape, lambda r, h: (0, 0))],
        out_specs=[qspec, kspec],
        out_shape=[jax.ShapeDtypeStruct((dil, sd, d), BF16),
                   jax.ShapeDtypeStruct((dil, sd, kvh * LANES), F32)],
        compiler_params=_params("parallel", "parallel"),
    )(q, kv, kv, band, first)


def _attn_bwd(name, q, kv, do, lse, delta):
    dil, sd, d = q.shape
    group = d // HEAD_DIM // N_KV_HEADS
    gw = group * HEAD_DIM
    nblk = sd // ATT_BLOCK
    scale = 1.0 / math.sqrt(HEAD_DIM)
    nt = (((1,), (1,)), ((), ()))
    tn = (((0,), (0,)), ((), ()))

    rb = _residues_per_step(dil, nblk)

    def body(q_all, k_all, v_all, do_all, lse_all, dl_all, band_ref, first_ref, dq_all, dk_all, dv_all, dk_accs,
             dv_accs):
        dk_accs[...] = jnp.zeros_like(dk_accs)
        dv_accs[...] = jnp.zeros_like(dv_accs)
        for rr in range(rb):
            q_ref, k_ref, v_ref, do_ref, lse_ref, dl_ref, dq_ref, dk_ref, dv_ref, dk_acc, dv_acc = [
                ref.at[rr] for ref in (q_all, k_all, v_all, do_all, lse_all, dl_all, dq_all, dk_all, dv_all,
                                       dk_accs, dv_accs)]

            def step(nb, carry):
                rows = pl.ds(pl.multiple_of(nb * ATT_BLOCK, ATT_BLOCK), ATT_BLOCK)
                qs = _stack_heads(q_ref[rows, :], group)
                dos = _stack_heads(do_ref[rows, :], group)
                ls = _stack_cols(lse_ref[rows, :], group)
                dl = _stack_cols(dl_ref[rows, :], group)
                kw = _window(k_ref, nb)
                vw = _window(v_ref, nb)
                p = jnp.exp(_masked_scores(qs, kw, band_ref, first_ref, nb, scale) - ls)
                dp = lax.dot_general(dos, vw, nt, preferred_element_type=F32)
                ds = (p * (dp - dl) * scale).astype(BF16)
                dq = jnp.dot(ds, kw, preferred_element_type=F32)
                dq_ref[rows, :] = _unstack_heads(dq, group).astype(BF16)
                win = pl.ds(pl.multiple_of(nb * ATT_BLOCK, ATT_BLOCK), 2 * ATT_BLOCK)
                dk_acc[win, :] += lax.dot_general(ds, qs, tn, preferred_element_type=F32)
                dv_acc[win, :] += lax.dot_general(p.astype(BF16), dos, tn, preferred_element_type=F32)
                return carry

            lax.fori_loop(0, nblk, step, 0, unroll=min(2, nblk))
            dk_ref[...] = dk_acc[ATT_BLOCK:, :]
            dv_ref[...] = dv_acc[ATT_BLOCK:, :]

    kvh = N_KV_HEADS
    band, first = _band_bias(group)
    qspec = pl.BlockSpec((rb, sd, gw), lambda r, h: (r, 0, h))
    kspec = pl.BlockSpec((rb, sd, HEAD_DIM), lambda r, h: (r, 0, h))
    return pl.pallas_call(
        body, name=name, grid=(dil // rb, kvh),
        in_specs=[qspec, kspec, pl.BlockSpec((rb, sd, HEAD_DIM), lambda r, h: (r, 0, kvh + h)),
                  qspec, kspec, kspec,
                  pl.BlockSpec(band.shape, lambda r, h: (0, 0)), pl.BlockSpec(first.shape, lambda r, h: (0, 0))],
        out_specs=[qspec, kspec, kspec],
        out_shape=[jax.ShapeDtypeStruct((dil, sd, d), BF16),
                   jax.ShapeDtypeStruct((dil, sd, kvh * HEAD_DIM), F32),
                   jax.ShapeDtypeStruct((dil, sd, kvh * HEAD_DIM), F32)],
        scratch_shapes=[pltpu.VMEM((rb, sd + ATT_BLOCK, HEAD_DIM), F32)] * 2,
        compiler_params=_params("parallel", "parallel"),
    )(q, kv, kv, do, lse, delta, band, first)


def _cast_bf16(name, w, layer, place, after=None):
    _, r, c = w.shape
    tr = min(512, r)
    deps = [] if after is None else [after]

    def body(pl_ref, w_ref, *refs):
        refs[-1][...] = w_ref[...].astype(BF16)

    return pl.pallas_call(
        body, name=name,
        grid_spec=pltpu.PrefetchScalarGridSpec(
            num_scalar_prefetch=1, grid=(r // tr,),
            in_specs=[pl.BlockSpec((None, tr, c), lambda i, p: (layer, i, 0))] + [ANY] * len(deps),
            out_specs=pl.BlockSpec((None, tr, c), lambda i, p: (p[1], i, 0))),
        out_shape=jax.ShapeDtypeStruct((N_SHARD, r, c), BF16),
        compiler_params=_params("parallel"),
    )(place, w, *deps)


def _chip_sum(name, g, rh, place):
    _, r, c = g.shape
    rh2 = r // 2
    tr = min(512, rh2)
    nb = rh2 // tr

    def body(pl_ref, g_ref, rh_ref, o_ref):
        o_ref[...] = (g_ref[...].astype(F32) + rh_ref[...].astype(F32)).astype(BF16)

    return pl.pallas_call(
        body, name=name,
        grid_spec=pltpu.PrefetchScalarGridSpec(
            num_scalar_prefetch=1, grid=(N_SHARD, nb),
            in_specs=[pl.BlockSpec((None, tr, c), lambda s, i, p: (s, p[0] * nb + i, 0)),
                      pl.BlockSpec((None, tr, c), lambda s, i, p: (s, i, 0))],
            out_specs=pl.BlockSpec((None, tr, c), lambda s, i, p: (s, i, 0))),
        out_shape=jax.ShapeDtypeStruct((N_SHARD, rh2, c), BF16),
        compiler_params=_params("parallel", "parallel"),
    )(place, g, rh)


def _owner_sum(name, cs, rp, place):
    _, rh2, c = cs.shape
    tr = min(512, rh2)
    nb = rh2 // tr

    def body(pl_ref, cs_ref, r0_ref, r1_ref, r2_ref, o_ref):
        o_ref[...] = ((cs_ref[...].astype(F32) + r0_ref[...].astype(F32))
                      + (r1_ref[...].astype(F32) + r2_ref[...].astype(F32)))

    def rspec(j):
        return pl.BlockSpec((None, tr, c), lambda i, p: (j, i, 0))

    return pl.pallas_call(
        body, name=name,
        grid_spec=pltpu.PrefetchScalarGridSpec(
            num_scalar_prefetch=1, grid=(nb,),
            in_specs=[pl.BlockSpec((None, tr, c), lambda i, p: (p[1], i, 0)), rspec(0), rspec(1), rspec(2)],
            out_specs=pl.BlockSpec((tr, c), lambda i, p: (p[0] * nb + i, 0))),
        out_shape=jax.ShapeDtypeStruct((2 * rh2, c), F32),
        compiler_params=_params("parallel"),
    )(place, cs, rp, rp, rp)


def _adam_math(w, g, m, v):
    m = ADAM_B1 * m + (1.0 - ADAM_B1) * g
    v = ADAM_B2 * v + (1.0 - ADAM_B2) * (g * g)
    m_hat = m / (1.0 - ADAM_B1 ** ADAM_STEP)
    v_hat = v / (1.0 - ADAM_B2 ** ADAM_STEP)
    delta = -ADAM_LR * (m_hat / (jnp.sqrt(v_hat) + ADAM_EPS) + ADAM_WD * w)
    return delta, m, v


def _adamw(name, w, m, v, g, layer, partial=None):
    nl, r, c = w.shape
    tr = min(256, r)
    stack_grad = nl > 1

    def body(w_ref, m_ref, v_ref, g_ref, *refs):
        d_ref, mo_ref, vo_ref = refs[-3:]
        gv = g_ref[...]
        delta, m_new, v_new = _adam_math(w_ref[...], gv, m_ref[...], v_ref[...])
        if stack_grad:
            refs[-4][...] = gv
        d_ref[...] = delta
        mo_ref[...] = m_new
        vo_ref[...] = v_new

    wspec = pl.BlockSpec((None, tr, c), lambda i: (layer, i, 0))
    prev = [] if partial is None else list(partial)
    n_out = 4 if stack_grad else 3
    res = pl.pallas_call(
        body, name=name, grid=(r // tr,),
        in_specs=[wspec] * 3 + [pl.BlockSpec((tr, c), lambda i: (i, 0))] + [ANY] * len(prev),
        out_specs=[wspec] * n_out,
        out_shape=[jax.ShapeDtypeStruct((nl, r, c), F32)] * n_out,
        input_output_aliases={4 + i: i for i in range(len(prev))},
        compiler_params=_params("parallel"),
    )(w, m, v, g, *prev)
    return list(res) if stack_grad else [g.reshape(nl, r, c)] + list(res)


def _adam_small(ws, ms, vs, gs):
    n = len(ws)

    def body(*refs):
        w_refs, m_refs, v_refs, g_refs = refs[:n], refs[n:2 * n], refs[2 * n:3 * n], refs[3 * n:4 * n]
        d_refs, mo_refs, vo_refs = refs[4 * n:5 * n], refs[5 * n:6 * n], refs[6 * n:7 * n]
        for i in range(n):
            delta, m_new, v_new = _adam_math(w_refs[i][...], g_refs[i][...], m_refs[i][...], v_refs[i][...])
            d_refs[i][...] = delta
            mo_refs[i][...] = m_new
            vo_refs[i][...] = v_new

    shapes = [jax.ShapeDtypeStruct(w.shape, F32) for w in ws]
    res = pl.pallas_call(body, name="adam_small", out_shape=shapes * 3)(*ws, *ms, *vs, *gs)
    return res[:n], res[n:2 * n], res[2 * n:]


def _pack_small(b_in, w_dw, b_dw, ln_g, ln_b, b_out, place):
    cin = b_in.shape[1]
    cd = b_dw.shape[1]
    rows = 8 + CONV_PAD

    def body(pl_ref, bi, wd, bd, lg, lb, bo, out):
        out[...] = jnp.zeros_like(out)
        out[0:1, :] = bi[...]
        out[1:2, 0:cd] = bd[...]
        out[1:2, cd:2 * cd] = lg[...]
        out[2:3, 0:cd] = lb[...]
        out[2:3, cd:2 * cd] = bo[...]
        out[8:8 + CONV_WIDTH, 0:cd] = wd[...]

    def whole(arr):
        return pl.BlockSpec(arr.shape, lambda i, p: (0,) * arr.ndim)

    ins = [b_in, w_dw, b_dw, ln_g, ln_b, b_out]
    return pl.pallas_call(
        body, name="pack_small",
        grid_spec=pltpu.PrefetchScalarGridSpec(
            num_scalar_prefetch=1, grid=(1,), in_specs=[whole(a) for a in ins],
            out_specs=pl.BlockSpec((None, rows, cin), lambda i, p: (p[1], 0, 0))),
        out_shape=jax.ShapeDtypeStruct((N_SHARD, rows, cin), F32),
        compiler_params=_params("arbitrary"),
    )(place, *ins)


def _place():
    x, y, c = lax.axis_index("x"), lax.axis_index("y"), lax.axis_index("c")
    return x, y, c


def _other_chips(x, y):
    return [(1 - x, y), (x, 1 - y), (1 - x, 1 - y)]


def _split_start_many(name, parts, after=None):
    flat = [b for bufs, _, _ in parts for b in bufs]
    n, n_parts = len(flat), len(parts)
    deps = [] if after is None else [after]

    def body(*refs):
        out0 = n + len(deps)
        pos = 0
        for i, (bufs, _, copies) in enumerate(parts):
            for cp in copies(refs[pos:pos + len(bufs)], refs[out0 + 2 * i], refs[out0 + 2 * i + 1], False):
                cp.start()
            pos += len(bufs)
        refs[-1][...] = jnp.zeros_like(refs[-1])

    sems = [pltpu.SemaphoreType.DMA((n_sem,)) for _, n_sem, _ in parts for _ in range(2)]
    res = pl.pallas_call(
        body, name=name,
        out_shape=(*sems, *[pltpu.HBM(b.shape, b.dtype) for b in flat], jax.ShapeDtypeStruct((8, LANES), F32)),
        in_specs=[HBM] * n + [ANY] * len(deps),
        out_specs=(*[SEM] * (2 * n_parts), *[HBM] * n, pl.BlockSpec(memory_space=pltpu.VMEM)),
        input_output_aliases={i: 2 * n_parts + i for i in range(n)},
        compiler_params=pltpu.CompilerParams(has_side_effects=SPLIT_EFFECT),
    )(*[pltpu.with_memory_space_constraint(b, pltpu.HBM) for b in flat], *deps)
    handles, pos = [], 2 * n_parts
    for i, (bufs, _, _) in enumerate(parts):
        handles.append((res[2 * i], res[2 * i + 1], list(res[pos:pos + len(bufs)]), res[-1]))
        pos += len(bufs)
    return handles


def _split_start(name, bufs, n_sem, copies, after=None):
    return _split_start_many(name, [(bufs, n_sem, copies)], after)[0]


def _split_wait(name, handle, copies, after):
    ssem, rsem, bufs, _ = handle
    n = len(bufs)
    deps = list(after) if isinstance(after, (list, tuple)) else [after]

    def body(*refs):
        for cp in copies(refs[:n], refs[n], refs[n + 1], True):
            cp.wait_send()
            cp.wait_recv()

    res = pl.pallas_call(
        body, name=name,
        out_shape=[pltpu.HBM(b.shape, b.dtype) for b in bufs],
        in_specs=[HBM] * n + [SEM, SEM] + [ANY] * len(deps), out_specs=[HBM] * n,
        input_output_aliases={i: i for i in range(n)},
        compiler_params=pltpu.CompilerParams(has_side_effects=SPLIT_EFFECT),
    )(*bufs, ssem, rsem, *deps)
    return list(res)


def _remote(src, dst, ssem, rsem, k, to):
    return pltpu.make_async_remote_copy(src_ref=src, dst_ref=dst, send_sem=ssem.at[k], recv_sem=rsem.at[k],
                                        device_id=to, device_id_type=MESH)


def _gather_chips(x, y, c):
    nx, ny = x + (1 - c) - 2 * x * (1 - c), y + c - 2 * y * c
    fx, fy = x + c - 2 * x * c, y + (1 - c) - 2 * y * (1 - c)
    return (nx, ny), (fx, fy), 2 * nx + ny, 2 * fx + fy, 2 * (1 - x) + (1 - y)


def _direct_copies(refs, ssem, rsem, landing, n_whole=0):
    x, y, c = _place()
    me = 2 * x + y
    (nx, ny), _, near, _, _ = _gather_chips(x, y, c)
    n = len(refs) - n_whole
    cps = []
    for a, ref in enumerate(refs[:n]):
        cps.append(_remote(ref.at[me], ref.at[near if landing else me], ssem, rsem, a, (nx, ny, c)))
    for b, ref in enumerate(refs[n:]):
        for j, (px, py) in enumerate(_other_chips(x, y)):
            cps.append(_remote(ref.at[me], ref.at[2 * px + py if landing else me], ssem, rsem, n + 3 * b + j,
                               (px, py, c)))
    return cps


def _relay_copies(refs, ssem, rsem, landing):
    x, y, c = _place()
    _, (fx, fy), near, far, diag = _gather_chips(x, y, c)
    n = len(refs)
    cps = []
    for a, ref in enumerate(refs):
        rh = ref.shape[1] // 2
        rows = pl.ds(c * rh, rh)
        cps.append(_remote(ref.at[near, rows], ref.at[diag if landing else near, rows], ssem, rsem, a, (fx, fy, c)))
        cps.append(_remote(ref.at[near], ref.at[far if landing else near], ssem, rsem, n + a, (x, y, 1 - c)))
    return cps


def _diagonal_copies(refs, ssem, rsem, landing):
    x, y, c = _place()
    diag = 2 * (1 - x) + (1 - y)
    who = 1 - c if landing else c
    cps = []
    for a, ref in enumerate(refs):
        rh = ref.shape[1] // 2
        piece = ref.at[diag, pl.ds(who * rh, rh)]
        cps.append(_remote(piece, piece, ssem, rsem, a, (x, y, 1 - c)))
    return cps


def _sibling_copies(refs, ssem, rsem, landing):
    x, y, c = _place()
    n = len(refs) // 2
    cps = []
    for a in range(n):
        rh = refs[a].shape[1] // 2
        cps.append(_remote(refs[a].at[:, pl.ds((1 - c) * rh, rh), :], refs[n + a], ssem, rsem, a, (x, y, 1 - c)))
    return cps


def _owner_copies(refs, ssem, rsem, landing):
    x, y, c = _place()
    n = len(refs) // 2
    cps = []
    for a in range(n):
        for j, (px, py) in enumerate(_other_chips(x, y)):
            cps.append(_remote(refs[a].at[2 * px + py], refs[n + a].at[j], ssem, rsem, 3 * a + j, (px, py, c)))
    return cps


def _swap_copies(refs, ssem, rsem, landing):
    x, y, c = _place()
    who = 1 - c if landing else c
    cps = []
    for a, ref in enumerate(refs):
        rh = ref.shape[0] // 2
        rows = ref.at[pl.ds(who * rh, rh)]
        cps.append(_remote(rows, rows, ssem, rsem, a, (x, y, 1 - c)))
    return cps


def _small_copies(refs, ssem, rsem, landing):
    pack, slots = refs
    x, y, c = _place()
    cps = []
    for rel in range(1, N_DEV):
        px = 1 - x if (rel >> 2) & 1 else x
        py = 1 - y if (rel >> 1) & 1 else y
        pc = 1 - c if rel & 1 else c
        slot = 4 * px + 2 * py + pc if landing else 4 * x + 2 * y + c
        cps.append(_remote(pack, slots.at[slot], ssem, rsem, rel - 1, (px, py, pc)))
    return cps


def _small_pack(rows, w_dw_grad, d):
    n = len(rows)

    def body(*refs):
        pack = refs[-1]
        pack[...] = jnp.zeros_like(pack)
        for (r, _), ref in zip(rows, refs[:n]):
            pack[r:r + 1, :] = ref[...]
        pack[16:16 + CONV_PAD, :] = refs[n][...]

    return pl.pallas_call(body, name="small_pack", out_shape=jax.ShapeDtypeStruct((SMALL_ROWS, d), F32))(
        *[v for _, v in rows], w_dw_grad)


def _small_sum(pack, slots, place):
    rows, d = pack.shape
    loss_row = 12

    def body(pl_ref, pack_ref, slots_ref, out_ref):
        me = pl_ref[2]
        tot = jnp.where(me == 0, pack_ref[...], slots_ref[0])
        for i in range(1, N_DEV):
            tot = tot + jnp.where(me == i, pack_ref[...], slots_ref[i])
        out_ref[...] = tot
        out_ref[loss_row:loss_row + 1, :] = jnp.zeros((1, d), F32) + jnp.sum(tot[loss_row:loss_row + 1, :])

    return pl.pallas_call(
        body, name="small_sum",
        grid_spec=pltpu.PrefetchScalarGridSpec(
            num_scalar_prefetch=1, grid=(1,),
            in_specs=[pl.BlockSpec((rows, d), lambda i, p: (0, 0)), pl.BlockSpec((N_DEV, rows, d), lambda i, p: (0, 0, 0))],
            out_specs=pl.BlockSpec((rows, d), lambda i, p: (0, 0))),
        out_shape=jax.ShapeDtypeStruct((rows, d), F32),
        compiler_params=_params("arbitrary"),
    )(place, pack, slots)


def kernel(x, norm_mix, norm_mlp, conv_w_in, conv_b_in, conv_w_dw, conv_b_dw, conv_ln_g, conv_ln_b, conv_w_out, conv_b_out, kv_norm, w_kv, attn_w_q, attn_w_o, mlp_w_in, mlp_w_out, final_norm, loss_target, m_norm_mix, m_norm_mlp, m_conv_w_in, m_conv_b_in, m_conv_w_dw, m_conv_b_dw, m_conv_ln_g, m_conv_ln_b, m_conv_w_out, m_conv_b_out, m_kv_norm, m_w_kv, m_attn_w_q, m_attn_w_o, m_mlp_w_in, m_mlp_w_out, m_final_norm, v_norm_mix, v_norm_mlp, v_conv_w_in, v_conv_b_in, v_conv_w_dw, v_conv_b_dw, v_conv_ln_g, v_conv_ln_b, v_conv_w_out, v_conv_b_out, v_kv_norm, v_w_kv, v_attn_w_q, v_attn_w_o, v_mlp_w_in, v_mlp_w_out, v_final_norm):
    _, s, d = x.shape
    dff = mlp_w_in.shape[2] * N_SHARD
    kvw = w_kv.shape[1]
    ds4 = d // N_SHARD
    xi, yi, ci = _place()
    me = 2 * xi + yi
    place = jnp.stack([ci, me, 2 * me + ci]).astype(I32)

    h0 = x.reshape(s, d)
    target = loss_target.reshape(s, d)
    tabs = _rope_tables(s)

    def gather_begin(tag, bufs, n_whole=0):
        plan = functools.partial(_direct_copies, n_whole=n_whole)
        return _split_start(f"gather_start_{tag}", bufs, len(bufs) + 2 * n_whole, plan), plan, n_whole

    def gather_step(later, land=None, swap=None):
        parts, names, whole = [], [], {}
        if land is not None:
            tag, (handle, plan, n_whole) = land
            bufs = _split_wait(f"gather_wait_{tag}", handle, plan, later)
            n = len(bufs) - n_whole
            parts.append((bufs[:n], 2 * n, _relay_copies))
            whole["land"] = bufs[n:]
            names.append(f"relay_{tag}")
        if swap is not None:
            tag, (relayed, whole["swap"]) = swap
            bufs = _split_wait(f"relay_wait_{tag}", relayed, _relay_copies, later)
            parts.append((bufs, len(bufs), _diagonal_copies))
            names.append(f"diagonal_{tag}")
        handles = _split_start_many("start_" + "_".join(names), parts)
        landed = (handles[0], whole["land"]) if land is not None else None
        swapped = (handles[-1], whole["swap"]) if swap is not None else None
        return landed, swapped

    def gather_land(tag, begun, later):
        return gather_step(later, land=(tag, begun))[0]

    def gather_swap(tag, landed, later):
        return gather_step(later, swap=(tag, landed))[1]

    def gather_end(tag, swapped, later):
        handle, whole = swapped
        return _split_wait(f"diagonal_wait_{tag}", handle, _diagonal_copies, later) + whole

    ag_cin = gather_begin("conv_in", [
        _cast_bf16("cast_w_in", conv_w_in, 0, place),
        _pack_small(conv_b_in, conv_w_dw.reshape(CONV_WIDTH, ds4), conv_b_dw, conv_ln_g, conv_ln_b, conv_b_out, place),
    ], n_whole=1)
    ag_cout = gather_begin("conv_out", [_cast_bf16("cast_w_out", conv_w_out, 0, place, ag_cin[0][3])])
    ag_mi0 = gather_begin("mlp_in0", [_cast_bf16("cast_mlp_in0", mlp_w_in, 0, place, ag_cout[0][3])])
    ag_mo0 = gather_begin("mlp_out0", [_cast_bf16("cast_mlp_out0", mlp_w_out, 0, place, ag_mi0[0][3])])
    nm = [norm_mix[0:1], norm_mix[1:2]]
    nmlp = [norm_mlp[0:1], norm_mlp[1:2]]
    kvn = kv_norm.reshape(1, d)
    fin = final_norm.reshape(1, d)
    (y0,) = _rms_fwd("rms_mix0", h0, [nm[0]], after=ag_mo0[0][3])
    land_cin = gather_land("conv_in", ag_cin, y0)
    ag_attn = gather_begin("attn", [
        _cast_bf16("cast_w_kv", w_kv.reshape(1, ds4, kvw), 0, place, land_cin[0][3]),
        _cast_bf16("cast_w_q", attn_w_q, 0, place), _cast_bf16("cast_w_o", attn_w_o, 0, place)])
    ag_mi1 = gather_begin("mlp_in1", [_cast_bf16("cast_mlp_in1", mlp_w_in, 1, place, ag_attn[0][3])])
    ag_mo1 = gather_begin("mlp_out1", [_cast_bf16("cast_mlp_out1", mlp_w_out, 1, place, ag_mi1[0][3])])
    land_cout, swap_cin = gather_step(ag_mo1[0][3], land=("conv_out", ag_cout), swap=("conv_in", land_cin))

    wmi_g = [None, None]
    wmo_f = [None, None]

    w_in_g, small_g = gather_end("conv_in", swap_cin, swap_cin[0][3])
    b_in_f = small_g[:, 0, :].reshape(1, 2 * d)
    b_dw_f = small_g[:, 1, 0:ds4].reshape(1, d)
    ln_g_f = small_g[:, 1, ds4:2 * ds4].reshape(1, d)
    ln_b_f = small_g[:, 2, 0:ds4].reshape(1, d)
    b_out_f = small_g[:, 2, ds4:2 * ds4].reshape(1, d)
    w_dw_f = jnp.transpose(small_g[:, 8:8 + CONV_PAD, 0:ds4], (1, 0, 2)).reshape(CONV_PAD, d)

    def ep_bias(acc, ex, outs, j):
        outs[0][...] = (acc + ex[0][...]).astype(outs[0].dtype)

    def ep_residual(acc, ex, outs, j):
        outs[0][...] = ex[0][...] + acc

    def ep_residual_bias(acc, ex, outs, j):
        outs[0][...] = ex[0][...] + (acc + ex[1][...])

    def ep_relu2(acc, ex, outs, j):
        r = jnp.maximum(acc, 0.0)
        outs[0][...] = r.astype(BF16)
        outs[1][...] = (r * r).astype(BF16)

    by_residue = [(BF16, ("residues", dil)) for dil in DILATIONS]

    def put_by_residue(val, outs, stage):
        _to_residues(val, stage, outs, DILATIONS)

    def ep_rope(acc, ex, outs, j, stage):
        put_by_residue(_rope_apply(acc, ex[0][...], ex[1][...], ex[2][...], 1.0), outs, stage)

    def ep_rope_k(acc, ex, outs, j, stage):
        roped = _rope_apply(acc, ex[0][...], ex[1][...], ex[2][...], 1.0)
        put_by_residue(jnp.where(j == 0, roped, acc), outs, stage)

    def ep_by_residue(acc, ex, outs, j, stage):
        put_by_residue(acc, outs, stage)

    tab_extras = [(t, "rows") for t in tabs]

    def mlp_fwd(idx, h, y, out_weight):
        r, r2 = _matmul(f"mlp_in{idx}", "nn", y, wmi_g[idx], b_kind="col", m=s, n=dff, k=d,
                        outs=[(BF16, "plain"), (BF16, "plain")], epilogue=ep_relu2)
        wmo_f[idx] = out_weight(r2).reshape(dff, d)
        (h_new,) = _matmul(f"mlp_out{idx}", "nn", r2, wmo_f[idx], m=s, n=d, k=dff,
                           outs=[(F32, "plain")], extras=[(h, "ij")], epilogue=ep_residual)
        return h_new, r, r2

    (u,) = _matmul("conv_in", "nn", y0, w_in_g, b_kind="col", m=s, n=2 * d, k=d,
                   outs=[(BF16, "plain")], extras=[(b_in_f, "vec")], epilogue=ep_bias)
    land_mi0, swap_cout = gather_step(u, land=("mlp_in0", ag_mi0), swap=("conv_out", land_cout))
    cpre = _dwconv_fwd(u, w_dw_f, b_dw_f, after=swap_cout[0][3])
    sact = _ln_silu_fwd(cpre, ln_g_f, ln_b_f)
    (w_out_g,) = gather_end("conv_out", swap_cout, sact)
    w_out_f = w_out_g.reshape(d, d)
    (h1,) = _matmul("conv_out", "nn", sact, w_out_f, m=s, n=d, k=d,
                    outs=[(F32, "plain")], extras=[(h0, "ij"), (b_out_f, "vec")], epilogue=ep_residual_bias)
    swap_mi0 = gather_swap("mlp_in0", land_mi0, h1)
    (y1,) = _rms_fwd("rms_mlp0", h1, [nmlp[0]], after=swap_mi0[0][3])
    land_mo0 = gather_land("mlp_out0", ag_mo0, y1)
    (wmi_g[0],) = gather_end("mlp_in0", swap_mi0, land_mo0[0][3])
    land_attn = None

    def out_weight0(r2):
        nonlocal land_attn
        land_attn, swap_mo0 = gather_step(r2, land=("attn", ag_attn), swap=("mlp_out0", land_mo0))
        return gather_end("mlp_out0", swap_mo0, swap_mo0[0][3])[0]

    h2, r0, r0sq = mlp_fwd(0, h1, y1, out_weight0)
    land_mi1, swap_attn = gather_step(h2, land=("mlp_in1", ag_mi1), swap=("attn", land_attn))
    ykv, y2 = _rms_fwd("rms_kv_mix1", h2, [kvn, nm[1]], after=land_mi1[0][3])
    wkv_g, wq_g, wo_g = gather_end("attn", swap_attn, y2)
    wkv_f, wq_f, wo_f = wkv_g.reshape(d, kvw), wq_g.reshape(d, d), wo_g.reshape(d, d)
    kv_parts = _matmul("kv_proj", "nn", ykv, wkv_f, m=s, n=kvw, k=d, tn=kvw // 2,
                       outs=by_residue, extras=tab_extras, epilogue=ep_rope_k, stage=True)
    q_parts = _matmul("q_proj", "nn", y2, wq_f, m=s, n=d, k=d,
                      outs=by_residue, extras=tab_extras, epilogue=ep_rope, stage=True)
    o_parts, lse_parts = [], []
    for dil, q_b, kv_b in zip(DILATIONS, q_parts, kv_parts):
        o_b, lse_b = _attn_fwd(f"attn_fwd_d{dil}", q_b, kv_b)
        o_parts.append(o_b)
        lse_parts.append(lse_b)
    o, lse = _attn_combine(o_parts, lse_parts)
    land_mo1, swap_mi1 = gather_step(o, land=("mlp_out1", ag_mo1), swap=("mlp_in1", land_mi1))
    (h3,) = _matmul("attn_out", "nn", o, wo_f, m=s, n=d, k=d,
                    outs=[(F32, "plain")], extras=[(h2, "ij")], epilogue=ep_residual)
    (y3,) = _rms_fwd("rms_mlp1", h3, [nmlp[1]], after=land_mo1[0][3])
    (wmi_g[1],) = gather_end("mlp_in1", swap_mi1, y3)

    def out_weight1(r2):
        swap_mo1 = gather_swap("mlp_out1", land_mo1, r2)
        return gather_end("mlp_out1", swap_mo1, swap_mo1[0][3])[0]

    h4, r1, r1sq = mlp_fwd(1, h3, y3, out_weight1)
    dh4, dh4b, d_fin, loss_cols = _final_loss(h4, fin, target)

    def ep_relu2_bwd(acc, ex, outs, j):
        outs[0][...] = (acc * (2.0 * ex[0][...].astype(F32))).astype(BF16)

    def mlp_bwd(idx, dhb, y, r, r2):
        (dz,) = _matmul(f"mlp_out{idx}_dx", "nt", dhb, wmo_f[idx], m=s, n=dff, k=d,
                        outs=[(BF16, "plain")], extras=[(r, "ij")], epilogue=ep_relu2_bwd)
        (dwo,) = _matmul(f"mlp_out{idx}_dw", "tn", r2, dhb, m=dff, n=d, k=s,
                         outs=[(BF16, "plain")])
        (dy,) = _matmul(f"mlp_in{idx}_dx", "nt", dz, wmi_g[idx], b_kind="col", m=s, n=d, k=dff,
                        outs=[(BF16, "plain")])
        (dwi,) = _matmul(f"mlp_in{idx}_dw", "tn", y, dz, m=d, n=dff, k=s,
                         outs=[(BF16, "col")])
        return dy, dwi, dwo.reshape(N_SHARD, dff // N_SHARD, d)

    def rs_exchange(tag, grads):
        lands = [lax.empty((N_SHARD, g.shape[1] // 2, g.shape[2]), g.dtype) for g in grads]
        return _split_start(f"sibling_start_{tag}", list(grads) + lands, len(grads), _sibling_copies)

    def rs_send(tag, names, exchanged, later):
        bufs = _split_wait(f"sibling_wait_{tag}", exchanged, _sibling_copies, later)
        n = len(names)
        sums = [_chip_sum(f"chip_sum_{nme}", g, rh, place) for nme, g, rh in zip(names, bufs[:n], bufs[n:])]
        lands = [lax.empty((N_SHARD - 1,) + cs.shape[1:], cs.dtype) for cs in sums]
        return _split_start(f"owners_start_{tag}", sums + lands, 3 * n, _owner_copies)

    def rs_sum(tag, names, sent, later):
        bufs = _split_wait(f"owners_wait_{tag}", sent, _owner_copies, later)
        n = len(names)
        own = [_owner_sum(f"owner_sum_{nme}", cs, rp, place) for nme, cs, rp in zip(names, bufs[:n], bufs[n:])]
        return _split_start(f"swap_start_{tag}", own, n, _swap_copies)

    def rs_end(tag, swapped, later):
        return _split_wait(f"swap_wait_{tag}", swapped, _swap_copies, later)

    dy3, g_wmi1, g_wmo1 = mlp_bwd(1, dh4b, y3, r1, r1sq)
    x_mlp1 = rs_exchange("mlp1", [g_wmi1, g_wmo1])
    dh3, dh3b, d_nmlp1 = _rms_bwd("rms_mlp1_bwd", h3, [(nmlp[1], dy3)], dh4, after=x_mlp1[3])

    do_parts = _matmul("attn_out_dx", "nt", dh3b, wo_f, m=s, n=d, k=d, outs=by_residue, epilogue=ep_by_residue,
                       stage=True)
    (g_wo,) = _matmul("attn_out_dw", "tn", o, dh3b, m=d, n=d, k=s, outs=[(BF16, "plain")])
    rs_mlp1 = rs_send("mlp1", ["mlp_in1", "mlp_out1"], x_mlp1, g_wo)
    lse_res, delta_res = _attn_delta(do_parts[0].reshape(s, d), o, lse, DILATIONS)
    dq_parts, dk_parts, dv_parts = [], [], []
    for dil, q_b, kv_b, do_b, lse_b, dl_b in zip(DILATIONS, q_parts, kv_parts, do_parts, lse_res, delta_res):
        dq_b, dk_b, dv_b = _attn_bwd(f"attn_bwd_d{dil}", q_b, kv_b, do_b, lse_b, dl_b)
        dq_parts.append(dq_b)
        dk_parts.append(dk_b)
        dv_parts.append(dv_b)
    dq = _residue_sum("rope_bwd_q", [(dq_parts, True)], tabs)
    dkv = _residue_sum("rope_bwd_kv", [(dk_parts, True), (dv_parts, False)], tabs)
    (g_wq,) = _matmul("q_proj_dw", "tn", y2, dq, m=d, n=d, k=s, outs=[(BF16, "plain")])
    (dy2,) = _matmul("q_proj_dx", "nt", dq, wq_f, m=s, n=d, k=d, outs=[(BF16, "plain")])
    (g_wkv,) = _matmul("kv_proj_dw", "tn", ykv, dkv, m=d, n=kvw, k=s, outs=[(BF16, "plain")])
    (dykv,) = _matmul("kv_proj_dx", "nt", dkv, wkv_f, m=s, n=d, k=kvw, outs=[(BF16, "plain")])
    x_attn = rs_exchange("attn", [g_wkv.reshape(N_SHARD, ds4, kvw), g_wq.reshape(N_SHARD, ds4, d),
                                  g_wo.reshape(N_SHARD, ds4, d)])
    dh2, dh2b, d_nm1, d_kvn = _rms_bwd("rms_kv_mix1_bwd", h2, [(nm[1], dy2), (kvn, dykv)], dh3, after=x_attn[3])
    rs_attn = rs_send("attn", ["w_kv", "w_q", "w_o"], x_attn, dh2b)

    dy1, g_wmi0, g_wmo0 = mlp_bwd(0, dh2b, y1, r0, r0sq)
    x_mlp0 = rs_exchange("mlp0", [g_wmi0, g_wmo0])
    dh1, dh1b, d_nmlp0, d_b_out = _rms_bwd("rms_mlp0_bwd", h1, [(nmlp[0], dy1)], dh2, want_colsum=True,
                                           after=[x_mlp0[3], rs_attn[3]])

    (dsact,) = _matmul("conv_out_dx", "nt", dh1b, w_out_f, m=s, n=d, k=d, outs=[(BF16, "plain")])
    (g_wout,) = _matmul("conv_out_dw", "tn", sact, dh1b, m=d, n=d, k=s, outs=[(BF16, "plain")])
    rs_mlp0 = rs_send("mlp0", ["mlp_in0", "mlp_out0"], x_mlp0, g_wout)
    dc, d_ln_g, d_ln_b, d_b_dw = _ln_silu_bwd(cpre, ln_g_f, ln_b_f, dsact, after=rs_mlp0[3])
    du, d_w_dw, d_b_in_a, d_b_in_g = _dwconv_bwd(u, w_dw_f, dc)
    (g_win,) = _matmul("conv_in_dw", "tn", y0, du, b_kind="col", m=d, n=2 * d, k=s, outs=[(BF16, "col")])
    x_conv = rs_exchange("conv", [g_win, g_wout.reshape(N_SHARD, ds4, d)])
    (dy0,) = _matmul("conv_in_dx", "nt", du, w_in_g, a_kind="col", b_kind="col", m=s, n=d, k=2 * d,
                     outs=[(BF16, "plain")], after=x_conv[3])
    rs_conv = rs_send("conv", ["w_in", "w_out"], x_conv, dy0)
    dx, _, d_nm0 = _rms_bwd("rms_mix0_bwd", h0, [(nm[0], dy0)], dh1, after=rs_conv[3])

    small_rows = [(0, d_nm0), (1, d_nm1), (2, d_nmlp0), (3, d_nmlp1), (4, d_kvn), (5, d_fin), (6, d_b_dw),
                  (7, d_ln_g), (8, d_ln_b), (9, d_b_out), (10, d_b_in_a), (11, d_b_in_g), (12, loss_cols)]
    x_small = _split_start("small_start", [_small_pack(small_rows, d_w_dw, d),
                                           lax.empty((N_DEV, SMALL_ROWS, d), F32)], N_DEV - 1, _small_copies)

    def big(name, w, m, v, g, layer=0, partial=None):
        shape = w.shape
        w3, m3, v3 = [t.reshape((-1,) + shape[-2:]) for t in (w, m, v)]
        if partial is not None:
            partial = [t.reshape(w3.shape) for t in partial]
        res = _adamw(name, w3, m3, v3, g, layer, partial)
        return [t.reshape(shape) for t in res]

    sw_mlp1 = rs_sum("mlp1", ["mlp_in1", "mlp_out1"], rs_mlp1, x_small[3])
    sw_attn = rs_sum("attn", ["w_kv", "w_q", "w_o"], rs_attn, sw_mlp1[3])
    f_wmi1, f_wmo1 = rs_end("mlp1", sw_mlp1, sw_attn[3])
    p_wmi = big("adam_mlp_in1", mlp_w_in, m_mlp_w_in, v_mlp_w_in, f_wmi1, 1)
    p_wmo = big("adam_mlp_out1", mlp_w_out, m_mlp_w_out, v_mlp_w_out, f_wmo1, 1)
    sw_mlp0 = rs_sum("mlp0", ["mlp_in0", "mlp_out0"], rs_mlp0, [p_wmi[0], p_wmo[0]])
    f_wkv, f_wq, f_wo = rs_end("attn", sw_attn, sw_mlp0[3])
    r_wkv = big("adam_w_kv", w_kv, m_w_kv, v_w_kv, f_wkv)
    r_wq = big("adam_w_q", attn_w_q, m_attn_w_q, v_attn_w_q, f_wq)
    r_wo = big("adam_w_o", attn_w_o, m_attn_w_o, v_attn_w_o, f_wo)
    sw_conv = rs_sum("conv", ["w_in", "w_out"], rs_conv, [r_wkv[0], r_wq[0], r_wo[0]])
    f_wmi0, f_wmo0 = rs_end("mlp0", sw_mlp0, sw_conv[3])
    r_wmi = big("adam_mlp_in0", mlp_w_in, m_mlp_w_in, v_mlp_w_in, f_wmi0, 0, p_wmi)
    r_wmo = big("adam_mlp_out0", mlp_w_out, m_mlp_w_out, v_mlp_w_out, f_wmo0, 0, p_wmo)
    f_win, f_wout = rs_end("conv", sw_conv, [r_wmi[0], r_wmo[0]])
    r_win = big("adam_w_in", conv_w_in, m_conv_w_in, v_conv_w_in, f_win)
    r_wout = big("adam_w_out", conv_w_out, m_conv_w_out, v_conv_w_out, f_wout)

    small_pack, small_slots = _split_wait("small_wait", x_small, _small_copies, r_wout[0])
    red = _small_sum(small_pack, small_slots, place)
    loss = red[12, 0]
    g_norm_mix = red[0:2]
    g_norm_mlp = red[2:4]
    g_kv_norm = red[4:5]
    g_final = red[5:6]

    def my_cols(row):
        return lax.dynamic_slice(red, (row, me * ds4), (1, ds4))

    g_b_dw, g_ln_g, g_ln_b, g_b_out = my_cols(6), my_cols(7), my_cols(8), my_cols(9)
    half_in = 2 * d // N_SHARD
    b_in_row = 10 + me // 2
    g_b_in = lax.dynamic_slice(red, (b_in_row, (me % 2) * half_in), (1, half_in))
    g_w_dw = lax.dynamic_slice(red, (16, me * ds4), (CONV_WIDTH, ds4))

    sm_w =[norm_mix, norm_mlp, conv_b_in, conv_w_dw.reshape(CONV_WIDTH, ds4), conv_b_dw, conv_ln_g, conv_ln_b,
            conv_b_out, kv_norm.reshape(1, d), final_norm.reshape(1, d)]
    sm_m = [m_norm_mix, m_norm_mlp, m_conv_b_in, m_conv_w_dw.reshape(CONV_WIDTH, ds4), m_conv_b_dw, m_conv_ln_g,
            m_conv_ln_b, m_conv_b_out, m_kv_norm.reshape(1, d), m_final_norm.reshape(1, d)]
    sm_v = [v_norm_mix, v_norm_mlp, v_conv_b_in, v_conv_w_dw.reshape(CONV_WIDTH, ds4), v_conv_b_dw, v_conv_ln_g,
            v_conv_ln_b, v_conv_b_out, v_kv_norm.reshape(1, d), v_final_norm.reshape(1, d)]
    sm_g = [g_norm_mix, g_norm_mlp, g_b_in, g_w_dw, g_b_dw, g_ln_g, g_ln_b, g_b_out, g_kv_norm, g_final]
    sm_d, sm_nm, sm_nv = _adam_small(sm_w, sm_m, sm_v, sm_g)
    shapes = [norm_mix.shape, norm_mlp.shape, conv_b_in.shape, conv_w_dw.shape, conv_b_dw.shape, conv_ln_g.shape,
              conv_ln_b.shape, conv_b_out.shape, kv_norm.shape, final_norm.shape]
    sm_g, sm_d, sm_nm, sm_nv = [[t.reshape(sh) for t, sh in zip(lst, shapes)] for lst in (sm_g, sm_d, sm_nm, sm_nv)]

    def order(sm, idx):
        return [sm[0], sm[1], r_win[idx], sm[2], sm[3], sm[4], sm[5], sm[6], r_wout[idx], sm[7], sm[8],
                r_wkv[idx], r_wq[idx], r_wo[idx], r_wmi[idx], r_wmo[idx], sm[9]]

    return (loss, dx.reshape(x.shape), *order(sm_g, 0), *order(sm_d, 1), *order(sm_nm, 2), *order(sm_nv, 3))
```

```python
import functools
import math

import jax
import jax.numpy as jnp
from jax import lax
from jax.experimental import pallas as pl
from jax.experimental.pallas import tpu as pltpu

F32 = jnp.float32
BF16 = jnp.bfloat16
I32 = jnp.int32

NORM_EPS = 1e-6
LN_EPS = 1e-5
HEAD_DIM = 128
N_KV_HEADS = 4
ROT_DIM = 32
ROPE_THETA = 500000.0
CONV_WIDTH = 31
CONV_PAD = 32
ATT_BLOCK = 128
ATT_STEP_BLOCKS = 16
DILATIONS = (1, 4, 16)
ADAM_LR = 0.001
ADAM_B1 = 0.9
ADAM_B2 = 0.999
ADAM_EPS = 1e-08
ADAM_WD = 0.01
ADAM_STEP = 10
N_SHARD = 4
N_DEV = 8
LANES = 128
VMEM_LIMIT = 48 * 1024 * 1024
MM_TM, MM_TN, MM_TK = 1024, 1024, 2048
ROW_TILE = 512
CONV_CB = 128
CONV_T = 128
SMALL_ROWS = 48
MESH = pl.DeviceIdType.MESH
ANY = pl.BlockSpec(memory_space=pl.ANY)
HBM = pl.BlockSpec(memory_space=pltpu.HBM)
SEM = pl.BlockSpec(memory_space=pltpu.SEMAPHORE)
SPLIT_EFFECT = pltpu.SideEffectType.DATAFLOW_SIDE_EFFECTING


def _params(*sem):
    return pltpu.CompilerParams(dimension_semantics=sem, vmem_limit_bytes=VMEM_LIMIT)


def _sigmoid(x):
    return 1.0 / (1.0 + jnp.exp(-x))


def _wspec(kind, arr_shape, br, bc, pick):
    if kind == "plain":
        return pl.BlockSpec((br, bc), pick)
    per = arr_shape[2] // bc

    def idx(*g):
        rb, cb = pick(*g)
        return (cb // per, rb, cb % per)

    return pl.BlockSpec((None, br, bc), idx)


def _stage_shape(rows, w):
    return (w // LANES, rows, LANES)


def _to_residues(val, stage_ref, out_refs, dils):
    planes, rows, _ = stage_ref.shape
    for c in range(planes):
        stage_ref[c] = val[:, c * LANES:(c + 1) * LANES]
    for out_ref, dil in zip(out_refs, dils):
        if dil == 1:
            out_ref[0] = val.astype(out_ref.dtype)
            continue
        for r in range(dil):
            for c in range(planes):
                out_ref[r, :, c * LANES:(c + 1) * LANES] = stage_ref.at[c][pl.ds(r, rows // dil, stride=dil), :].astype(
                    out_ref.dtype)


def _from_residues(src_ref, stage_ref, dil):
    planes, rows, _ = stage_ref.shape
    if dil == 1:
        return lambda c: src_ref[0, :, c * LANES:(c + 1) * LANES].astype(F32)
    for r in range(dil):
        for c in range(planes):
            stage_ref.at[c][pl.ds(r, rows // dil, stride=dil), :] = src_ref[r, :, c * LANES:(c + 1) * LANES].astype(F32)
    return lambda c: stage_ref[c]


def _matmul(name, mode, a, b, *, m, n, k, tm=MM_TM, tn=MM_TN, tk=MM_TK, a_kind="plain", b_kind="plain", outs,
            extras=(), epilogue=None, stage=False, after=None):
    tm, tn, tk = min(tm, m), min(tn, n), min(tk, k)
    if b_kind == "col" and mode in ("nn", "tn"):
        tn = min(tn, n // b.shape[0])
    if b_kind == "col" and mode == "nt":
        tk = min(tk, k // b.shape[0])
    if a_kind == "col":
        assert mode == "nt"
        tk = min(tk, k // a.shape[0])
    if any(kind == "col" for _, kind in outs):
        tn = min(tn, n // N_SHARD)
    assert m % tm == 0 and n % tn == 0 and k % tk == 0, (name, m, n, k, tm, tn, tk)
    nk = k // tk
    grid = (m // tm, n // tn, nk)
    if mode == "nn":
        a_spec = pl.BlockSpec((tm, tk), lambda i, j, kk: (i, kk))
        b_spec = _wspec(b_kind, b.shape, tk, tn, lambda i, j, kk: (kk, j))
        dims = (((1,), (0,)), ((), ()))
    elif mode == "nt":
        a_spec = _wspec(a_kind, a.shape, tm, tk, lambda i, j, kk: (i, kk))
        b_spec = _wspec(b_kind, b.shape, tn, tk, lambda i, j, kk: (j, kk))
        dims = (((1,), (1,)), ((), ()))
    else:
        a_spec = pl.BlockSpec((tk, tm), lambda i, j, kk: (kk, i))
        b_spec = _wspec(b_kind, b.shape, tk, tn, lambda i, j, kk: (kk, j))
        dims = (((0,), (0,)), ((), ()))
    out_shape, out_specs = [], []
    for dtype, kind in outs:
        if isinstance(kind, tuple):
            dil = kind[1]
            out_shape.append(jax.ShapeDtypeStruct((dil, m // dil, n), dtype))
            out_specs.append(pl.BlockSpec((dil, tm // dil, tn), lambda i, j, kk: (0, i, j)))
            continue
        shape = (m, n) if kind == "plain" else (N_SHARD, m, n // N_SHARD)
        out_shape.append(jax.ShapeDtypeStruct(shape, dtype))
        out_specs.append(_wspec(kind, shape, tm, tn, lambda i, j, kk: (i, j)))
    n_ex = len(extras)
    deps = [] if after is None else [after]
    ex_specs = {"ij": pl.BlockSpec((tm, tn), lambda i, j, kk: (i, j)),
                "vec": pl.BlockSpec((1, tn), lambda i, j, kk: (0, j)),
                "rows": pl.BlockSpec((tm, LANES), lambda i, j, kk: (i, 0))}
    out0 = 2 + n_ex + len(deps)

    def body(*refs):
        a_ref, b_ref = refs[0], refs[1]
        ex_refs = refs[2:2 + n_ex]
        out_refs = refs[out0:out0 + len(outs)]
        j = pl.program_id(1)

        def finish(res):
            if epilogue is None:
                out_refs[0][...] = res.astype(out_refs[0].dtype)
            elif stage:
                epilogue(res, ex_refs, out_refs, j, refs[-1])
            else:
                epilogue(res, ex_refs, out_refs, j)

        prod = lax.dot_general(a_ref[...], b_ref[...], dims, preferred_element_type=F32)
        if nk == 1:
            finish(prod)
            return
        acc_ref = refs[out0 + len(outs)]
        kk = pl.program_id(2)

        @pl.when(kk == 0)
        def _():
            acc_ref[...] = prod

        @pl.when(kk > 0)
        def _():
            acc_ref[...] += prod

        @pl.when(kk == nk - 1)
        def _():
            finish(acc_ref[...])

    res = pl.pallas_call(
        body, name=name, grid=grid,
        in_specs=[a_spec, b_spec] + [ex_specs[how] for _, how in extras] + [ANY] * len(deps),
        out_specs=out_specs, out_shape=out_shape,
        scratch_shapes=[pltpu.VMEM((tm, tn), F32)] * (nk > 1) + [pltpu.VMEM(_stage_shape(tm, tn), F32)] * bool(stage),
        compiler_params=_params("parallel", "parallel", "arbitrary"),
    )(a, b, *[e for e, _ in extras], *deps)
    return res


def _rope_tables(seq):
    half = ROT_DIM // 2
    pos = jnp.arange(seq, dtype=F32)
    inv = ROPE_THETA ** (-jnp.arange(0, ROT_DIM, 2, dtype=F32) / ROT_DIM)
    ang = pos[:, None] * inv[None, :]
    cos, sin = jnp.cos(ang), jnp.sin(ang)
    zeros = jnp.zeros((seq, HEAD_DIM - ROT_DIM), F32)
    ctab = jnp.concatenate([cos, cos, zeros + 1.0], axis=1)
    atab = jnp.concatenate([-sin, jnp.zeros((seq, half), F32), zeros], axis=1)
    btab = jnp.concatenate([jnp.zeros((seq, half), F32), sin, zeros], axis=1)
    return ctab, atab, btab


def _rope_apply(x, ctab, atab, btab, sign):
    w = x.shape[1]
    reps = w // HEAD_DIM
    half = ROT_DIM // 2
    c = jnp.tile(ctab, (1, reps))
    a = jnp.tile(atab, (1, reps))
    b = jnp.tile(btab, (1, reps))
    up = pltpu.roll(x, w - half, 1)
    down = pltpu.roll(x, half, 1)
    return x * c + sign * (up * a + down * b)


def _rows(t, w):
    return pl.BlockSpec((t, w), lambda i: (i, 0))


def _fixed(shape):
    nd = len(shape)
    return pl.BlockSpec(shape, lambda i: (0,) * nd)


def _behind(after):
    deps = [] if after is None else (list(after) if isinstance(after, (list, tuple)) else [after])
    return deps, [ANY] * len(deps)


def _rms_fwd(name, x, gains, after=None):
    s, d = x.shape
    t = min(ROW_TILE, s)
    ng = len(gains)
    deps, dep_specs = _behind(after)

    def body(*all_refs):
        x_ref, refs = all_refs[len(deps)], all_refs[len(deps) + 1:]
        xv = x_ref[...]
        r = lax.rsqrt(jnp.mean(xv * xv, axis=-1, keepdims=True) + NORM_EPS)
        xn = xv * r
        for g_ref, y_ref in zip(refs[:ng], refs[ng:]):
            y_ref[...] = (xn * g_ref[...]).astype(BF16)

    return pl.pallas_call(
        body, name=name, grid=(s // t,),
        in_specs=dep_specs + [_rows(t, d)] + [_fixed((1, d))] * ng,
        out_specs=[_rows(t, d)] * ng,
        out_shape=[jax.ShapeDtypeStruct((s, d), BF16)] * ng,
        compiler_params=_params("parallel"),
    )(*deps, x, *gains)


def _rms_bwd(name, x, pairs, dh_in, want_colsum=False, after=None):
    s, d = x.shape
    n_p = len(pairs)
    t = min(ROW_TILE // n_p, s)
    deps, dep_specs = _behind(after)

    def body(*all_refs):
        x_ref, dh_ref, refs = all_refs[len(deps)], all_refs[len(deps) + 1], all_refs[len(deps) + 2:]
        g_refs = refs[:n_p]
        dy_refs = refs[n_p:2 * n_p]
        dh_out, dhb_out = refs[2 * n_p], refs[2 * n_p + 1]
        dg_refs = refs[2 * n_p + 2:2 * n_p + 2 + n_p]
        cs_ref = refs[-1] if want_colsum else None
        i = pl.program_id(0)
        xv = x_ref[...]
        r = lax.rsqrt(jnp.mean(xv * xv, axis=-1, keepdims=True) + NORM_EPS)
        xn = xv * r
        dh = dh_ref[...]
        for g_ref, dy_ref, dg_ref in zip(g_refs, dy_refs, dg_refs):
            dy = dy_ref[...].astype(F32)
            u = dy * g_ref[...]
            dh = dh + r * (u - xn * jnp.mean(u * xn, axis=-1, keepdims=True))
            part = jnp.sum(dy * xn, axis=0, keepdims=True)

            @pl.when(i == 0)
            def _():
                dg_ref[...] = part

            @pl.when(i > 0)
            def _():
                dg_ref[...] += part

        dh_out[...] = dh
        dhb_out[...] = dh.astype(BF16)
        if want_colsum:
            col = jnp.sum(dh, axis=0, keepdims=True)

            @pl.when(i == 0)
            def _():
                cs_ref[...] = col

            @pl.when(i > 0)
            def _():
                cs_ref[...] += col

    n_vec = n_p + (1 if want_colsum else 0)
    return pl.pallas_call(
        body, name=name, grid=(s // t,),
        in_specs=dep_specs + [_rows(t, d), _rows(t, d)] + [_fixed((1, d))] * n_p + [_rows(t, d)] * n_p,
        out_specs=[_rows(t, d), _rows(t, d)] + [_fixed((1, d))] * n_vec,
        out_shape=[jax.ShapeDtypeStruct((s, d), F32), jax.ShapeDtypeStruct((s, d), BF16)]
        + [jax.ShapeDtypeStruct((1, d), F32)] * n_vec,
        compiler_params=_params("arbitrary"),
    )(*deps, x, dh_in, *[g for g, _ in pairs], *[dy for _, dy in pairs])


def _final_loss(x, g, target):
    s, d = x.shape
    t = min(ROW_TILE, s)

    def body(x_ref, g_ref, t_ref, dh_out, dhb_out, dg_ref, loss_ref):
        i = pl.program_id(0)
        xv = x_ref[...]
        gv = g_ref[...]
        r = lax.rsqrt(jnp.mean(xv * xv, axis=-1, keepdims=True) + NORM_EPS)
        xn = xv * r
        diff = xn * gv - t_ref[...]
        dy = diff / d
        u = dy * gv
        dh = r * (u - xn * jnp.mean(u * xn, axis=-1, keepdims=True))
        dh_out[...] = dh
        dhb_out[...] = dh.astype(BF16)
        dg = jnp.sum(dy * xn, axis=0, keepdims=True)
        lc = jnp.sum(0.5 * diff * dy, axis=0, keepdims=True)

        @pl.when(i == 0)
        def _():
            dg_ref[...] = dg
            loss_ref[...] = lc

        @pl.when(i > 0)
        def _():
            dg_ref[...] += dg
            loss_ref[...] += lc

    return pl.pallas_call(
        body, name="final_loss", grid=(s // t,),
        in_specs=[_rows(t, d), _fixed((1, d)), _rows(t, d)],
        out_specs=[_rows(t, d), _rows(t, d), _fixed((1, d)), _fixed((1, d))],
        out_shape=[jax.ShapeDtypeStruct((s, d), F32), jax.ShapeDtypeStruct((s, d), BF16),
                   jax.ShapeDtypeStruct((1, d), F32), jax.ShapeDtypeStruct((1, d), F32)],
        compiler_params=_params("arbitrary"),
    )(x, g, target)


def _ln_silu_fwd(c, g, b):
    s, d = c.shape
    t = min(ROW_TILE, s)

    def body(c_ref, g_ref, b_ref, s_ref):
        cv = c_ref[...]
        mu = jnp.mean(cv, axis=-1, keepdims=True)
        xc = cv - mu
        rs = lax.rsqrt(jnp.mean(xc * xc, axis=-1, keepdims=True) + LN_EPS)
        ln = xc * rs * g_ref[...] + b_ref[...]
        s_ref[...] = (ln * _sigmoid(ln)).astype(BF16)

    return pl.pallas_call(
        body, name="ln_silu_fwd", grid=(s // t,),
        in_specs=[_rows(t, d), _fixed((1, d)), _fixed((1, d))],
        out_specs=_rows(t, d), out_shape=jax.ShapeDtypeStruct((s, d), BF16),
        compiler_params=_params("parallel"),
    )(c, g, b)


def _ln_silu_bwd(c, g, b, ds, after=None):
    s, d = c.shape
    t = min(ROW_TILE, s)
    deps, dep_specs = _behind(after)

    def body(*all_refs):
        c_ref, g_ref, b_ref, ds_ref, dc_ref, dg_ref, db_ref, dbdw_ref = all_refs[len(deps):]
        i = pl.program_id(0)
        cv = c_ref[...]
        gv = g_ref[...]
        mu = jnp.mean(cv, axis=-1, keepdims=True)
        xc = cv - mu
        rs = lax.rsqrt(jnp.mean(xc * xc, axis=-1, keepdims=True) + LN_EPS)
        nrm = xc * rs
        ln = nrm * gv + b_ref[...]
        sig = _sigmoid(ln)
        dln = ds_ref[...].astype(F32) * sig * (1.0 + ln * (1.0 - sig))
        dn = dln * gv
        dc = rs * (dn - jnp.mean(dn, axis=-1, keepdims=True)
                   - nrm * jnp.mean(dn * nrm, axis=-1, keepdims=True))
        dc_ref[...] = dc
        pg = jnp.sum(dln * nrm, axis=0, keepdims=True)
        pb = jnp.sum(dln, axis=0, keepdims=True)
        pc = jnp.sum(dc, axis=0, keepdims=True)

        @pl.when(i == 0)
        def _():
            dg_ref[...] = pg
            db_ref[...] = pb
            dbdw_ref[...] = pc

        @pl.when(i > 0)
        def _():
            dg_ref[...] += pg
            db_ref[...] += pb
            dbdw_ref[...] += pc

    return pl.pallas_call(
        body, name="ln_silu_bwd", grid=(s // t,),
        in_specs=dep_specs + [_rows(t, d), _fixed((1, d)), _fixed((1, d)), _rows(t, d)],
        out_specs=[_rows(t, d)] + [_fixed((1, d))] * 3,
        out_shape=[jax.ShapeDtypeStruct((s, d), F32)] + [jax.ShapeDtypeStruct((1, d), F32)] * 3,
        compiler_params=_params("arbitrary"),
    )(*deps, c, g, b, ds)


def _residue_spec(dil, t, w):
    return pl.BlockSpec((dil, t // dil, w), lambda i: (0, i, 0))


def _attn_combine(o_list, lse_list):
    dil0, sd0, d = o_list[0].shape
    s = dil0 * sd0
    lw = lse_list[0].shape[2]
    group = d // HEAD_DIM // N_KV_HEADS
    t = min(ROW_TILE, s)
    nb = len(o_list)
    dils = [o.shape[0] for o in o_list]

    def body(*refs):
        o_out, l_out = refs[2 * nb], refs[2 * nb + 1]
        o_stage, l_stage = refs[2 * nb + 2:3 * nb + 2], refs[3 * nb + 2:]
        o_planes = [_from_residues(src, stage, dil) for src, stage, dil in zip(refs[:nb], o_stage, dils)]
        l_planes = [_from_residues(src, stage, dil) for src, stage, dil in zip(refs[nb:2 * nb], l_stage, dils)]
        for kh in range(N_KV_HEADS):
            ls = [plane(kh) for plane in l_planes]
            mx = ls[0]
            for l in ls[1:]:
                mx = jnp.maximum(mx, l)
            es = [jnp.exp(l - mx) for l in ls]
            den = es[0]
            for e in es[1:]:
                den = den + e
            l_out[:, kh * LANES:(kh + 1) * LANES] = mx + jnp.log(den)
            ws = [e / den for e in es]
            for g in range(group):
                h = kh * group + g
                acc = jnp.zeros((t, HEAD_DIM), F32)
                for plane, w in zip(o_planes, ws):
                    acc = acc + w[:, g:g + 1] * plane(h)
                o_out[:, h * HEAD_DIM:(h + 1) * HEAD_DIM] = acc.astype(BF16)

    return pl.pallas_call(
        body, name="attn_combine", grid=(s // t,),
        in_specs=[_residue_spec(dil, t, d) for dil in dils] + [_residue_spec(dil, t, lw) for dil in dils],
        out_specs=[_rows(t, d), _rows(t, lw)],
        out_shape=[jax.ShapeDtypeStruct((s, d), BF16), jax.ShapeDtypeStruct((s, lw), F32)],
        scratch_shapes=[pltpu.VMEM(_stage_shape(t, d), F32)] * nb + [pltpu.VMEM(_stage_shape(t, lw), F32)] * nb,
        compiler_params=_params("parallel"),
    )(*o_list, *lse_list)


def _attn_delta(do, o, lse, dils):
    s, d = o.shape
    lw = lse.shape[1]
    group = d // HEAD_DIM // N_KV_HEADS
    t = min(ROW_TILE, s)
    nd = len(dils)

    def body(do_ref, o_ref, lse_ref, *refs):
        stage = refs[-1]
        lane = lax.broadcasted_iota(I32, (t, LANES), 1)
        planes = []
        for kh in range(N_KV_HEADS):
            out = jnp.zeros((t, LANES), F32)
            for g in range(group):
                cols = slice((kh * group + g) * HEAD_DIM, (kh * group + g + 1) * HEAD_DIM)
                v = jnp.sum(do_ref[:, cols].astype(F32) * o_ref[:, cols].astype(F32), axis=-1, keepdims=True)
                out = jnp.where(lane == g, v, out)
            planes.append(out)
        _to_residues(lse_ref[...], stage, refs[:nd], dils)
        _to_residues(jnp.concatenate(planes, axis=1), stage, refs[nd:2 * nd], dils)

    res = pl.pallas_call(
        body, name="attn_delta", grid=(s // t,),
        in_specs=[_rows(t, d), _rows(t, d), _rows(t, lw)],
        out_specs=[_residue_spec(dil, t, lw) for dil in dils] * 2,
        out_shape=[jax.ShapeDtypeStruct((dil, s // dil, lw), F32) for dil in dils] * 2,
        scratch_shapes=[pltpu.VMEM(_stage_shape(t, lw), F32)],
        compiler_params=_params("parallel"),
    )(do, o, lse)
    return res[:nd], res[nd:]


def _residue_sum(name, groups, tabs):
    first = groups[0][0][0]
    s, w = first.shape[0] * first.shape[1], first.shape[2]
    t = min(ROW_TILE, s)
    flat = [p for parts, _ in groups for p in parts]

    def body(*refs):
        c_ref, a_ref, b_ref = refs[len(flat):len(flat) + 3]
        out = refs[len(flat) + 3]
        stages = refs[len(flat) + 4:]
        k = 0
        for gi, (parts, rotate) in enumerate(groups):
            planes = [_from_residues(refs[k + i], stages[k + i], p.shape[0]) for i, p in enumerate(parts)]
            k += len(parts)
            for c in range(w // LANES):
                tot = planes[0](c)
                for plane in planes[1:]:
                    tot = tot + plane(c)
                if rotate:
                    tot = _rope_apply(tot, c_ref[...], a_ref[...], b_ref[...], -1.0)
                out[:, gi * w + c * LANES:gi * w + (c + 1) * LANES] = tot.astype(BF16)

    return pl.pallas_call(
        body, name=name, grid=(s // t,),
        in_specs=[_residue_spec(p.shape[0], t, w) for p in flat] + [_rows(t, HEAD_DIM)] * 3,
        out_specs=_rows(t, len(groups) * w), out_shape=jax.ShapeDtypeStruct((s, len(groups) * w), BF16),
        scratch_shapes=[pltpu.VMEM(_stage_shape(t, w), F32) for _ in flat],
        compiler_params=_params("parallel"),
    )(*flat, *tabs)


def _dwconv_fwd(u, w_dw, b_dw, after=None):
    s, d2 = u.shape
    d = d2 // 2
    cb = min(CONV_CB, d)
    nblk = d // cb
    tt = min(CONV_T, s)
    deps, dep_specs = _behind(after)

    def body(*all_refs):
        ua_ref, ug_ref, w_ref, b_ref, c_ref, xp_ref = all_refs[len(deps):]
        gl =ua_ref[...].astype(F32) * _sigmoid(ug_ref[...].astype(F32))
        xp_ref[0:CONV_PAD, :] = jnp.zeros((CONV_PAD, cb), F32)
        xp_ref[CONV_PAD:, :] = gl
        wv = w_ref[...]
        bv = b_ref[...]
        for t0 in range(0, s, tt):
            acc = jnp.zeros((tt, cb), F32) + bv
            for kk in range(CONV_WIDTH):
                off = t0 + CONV_PAD - (CONV_WIDTH - 1) + kk
                acc = acc + wv[kk:kk + 1, :] * xp_ref[off:off + tt, :]
            c_ref[t0:t0 + tt, :] = acc

    return pl.pallas_call(
        body, name="dwconv_fwd", grid=(nblk,),
        in_specs=dep_specs + [pl.BlockSpec((s, cb), lambda j: (0, j)), pl.BlockSpec((s, cb), lambda j: (0, j + nblk)),
                              pl.BlockSpec((CONV_PAD, cb), lambda j: (0, j)), pl.BlockSpec((1, cb), lambda j: (0, j))],
        out_specs=pl.BlockSpec((s, cb), lambda j: (0, j)),
        out_shape=jax.ShapeDtypeStruct((s, d), F32),
        scratch_shapes=[pltpu.VMEM((s + CONV_PAD, cb), F32)],
        compiler_params=_params("parallel"),
    )(*deps, u, u, w_dw, b_dw)


def _dwconv_bwd(u, w_dw, dc):
    s, d2 = u.shape
    d = d2 // 2
    cb = min(CONV_CB, d)
    nblk = d // cb
    tt = min(CONV_T, s)

    def body(ua_ref, ug_ref, w_ref, dc_ref, du_ref, dw_ref, dba_ref, dbg_ref, glp_ref, dcp_ref, acc_ref):
        a = ua_ref[...].astype(F32)
        sig = _sigmoid(ug_ref[...].astype(F32))
        glp_ref[0:CONV_PAD, :] = jnp.zeros((CONV_PAD, cb), F32)
        glp_ref[CONV_PAD:, :] = a * sig
        dcp_ref[0:s, :] = dc_ref[...]
        dcp_ref[s:, :] = jnp.zeros((CONV_PAD, cb), F32)
        acc_ref[...] = jnp.zeros_like(acc_ref)
        wv = w_ref[...]
        dba = jnp.zeros((1, cb), F32)
        dbg = jnp.zeros((1, cb), F32)
        for t0 in range(0, s, tt):
            dgl = jnp.zeros((tt, cb), F32)
            dct = dc_ref[t0:t0 + tt, :]
            for kk in range(CONV_WIDTH):
                off = t0 + (CONV_WIDTH - 1) - kk
                dgl = dgl + wv[kk:kk + 1, :] * dcp_ref[off:off + tt, :]
                goff = t0 + CONV_PAD - (CONV_WIDTH - 1) + kk
                prod = dct * glp_ref[goff:goff + tt, :]
                acc_ref[8 * kk:8 * kk + 8, :] += jnp.sum(prod.reshape(tt // 8, 8, cb), axis=0)
            at = ua_ref[t0:t0 + tt, :].astype(F32)
            st = _sigmoid(ug_ref[t0:t0 + tt, :].astype(F32))
            da = dgl * st
            dg = dgl * at * st * (1.0 - st)
            du_ref[0, t0:t0 + tt, :] = da.astype(BF16)
            du_ref[1, t0:t0 + tt, :] = dg.astype(BF16)
            dba = dba + jnp.sum(da, axis=0, keepdims=True)
            dbg = dbg + jnp.sum(dg, axis=0, keepdims=True)
        dba_ref[...] = dba
        dbg_ref[...] = dbg
        for kk in range(CONV_WIDTH):
            dw_ref[kk:kk + 1, :] = jnp.sum(acc_ref[8 * kk:8 * kk + 8, :], axis=0, keepdims=True)
        dw_ref[CONV_WIDTH:, :] = jnp.zeros((CONV_PAD - CONV_WIDTH, cb), F32)

    blk = pl.BlockSpec((s, cb), lambda j: (0, j))
    vec = pl.BlockSpec((1, cb), lambda j: (0, j))
    return pl.pallas_call(
        body, name="dwconv_bwd", grid=(nblk,),
        in_specs=[blk, pl.BlockSpec((s, cb), lambda j: (0, j + nblk)),
                  pl.BlockSpec((CONV_PAD, cb), lambda j: (0, j)), blk],
        out_specs=[pl.BlockSpec((2, s, cb), lambda j: (0, 0, j)), pl.BlockSpec((CONV_PAD, cb), lambda j: (0, j)),
                   vec, vec],
        out_shape=[jax.ShapeDtypeStruct((2, s, d), BF16), jax.ShapeDtypeStruct((CONV_PAD, d), F32),
                   jax.ShapeDtypeStruct((1, d), F32), jax.ShapeDtypeStruct((1, d), F32)],
        scratch_shapes=[pltpu.VMEM((s + CONV_PAD, cb), F32), pltpu.VMEM((s + CONV_PAD, cb), F32),
                        pltpu.VMEM((8 * CONV_PAD, cb), F32)],
        compiler_params=_params("parallel"),
    )(u, u, w_dw, dc)


def _stack_heads(x, group):
    return jnp.concatenate([x[:, g * HEAD_DIM:(g + 1) * HEAD_DIM] for g in range(group)], axis=0)


def _unstack_heads(x, group):
    return jnp.concatenate([x[g * ATT_BLOCK:(g + 1) * ATT_BLOCK, :] for g in range(group)], axis=1)


def _stack_cols(x, group):
    return jnp.concatenate([x[:, g:g + 1] for g in range(group)], axis=0)


def _band_bias(group):
    rows = group * ATT_BLOCK
    row = lax.broadcasted_iota(I32, (rows, 2 * ATT_BLOCK), 0) % ATT_BLOCK
    col = lax.broadcasted_iota(I32, (rows, 2 * ATT_BLOCK), 1)
    band = jnp.where((col >= row) & (col <= row + ATT_BLOCK), 0.0, -jnp.inf).astype(F32)
    first = jnp.where(lax.broadcasted_iota(I32, (1, 2 * ATT_BLOCK), 1) >= ATT_BLOCK, 0.0, -jnp.inf).astype(F32)
    return band, first


def _masked_scores(qs, kw, band_ref, first_ref, nb, scale):
    sc = lax.dot_general(qs, kw, (((1,), (1,)), ((), ())), preferred_element_type=F32) * scale + band_ref[...]
    return sc + jnp.where(nb > 0, 0.0, first_ref[...])


def _window(ref, nb):
    prev = pl.multiple_of(jnp.maximum(nb - 1, 0) * ATT_BLOCK, ATT_BLOCK)
    cur = pl.multiple_of(nb * ATT_BLOCK, ATT_BLOCK)
    return jnp.concatenate([ref[pl.ds(prev, ATT_BLOCK), :], ref[pl.ds(cur, ATT_BLOCK), :]], axis=0)


def _residues_per_step(dil, nblk):
    return max(1, min(dil, ATT_STEP_BLOCKS // nblk))


def _attn_fwd(name, q, kv):
    dil, sd, d = q.shape
    group = d // HEAD_DIM // N_KV_HEADS
    gw = group * HEAD_DIM
    nblk = sd // ATT_BLOCK
    scale = 1.0 / math.sqrt(HEAD_DIM)
    rb = _residues_per_step(dil, nblk)

    def body(q_all, k_all, v_all, band_ref, first_ref, o_all, lse_all):
        lane = lax.broadcasted_iota(I32, (ATT_BLOCK, LANES), 1)
        for rr in range(rb):
            q_ref, k_ref, v_ref, o_ref, lse_ref = [ref.at[rr] for ref in (q_all, k_all, v_all, o_all, lse_all)]

            def step(nb, carry):
                rows = pl.ds(pl.multiple_of(nb * ATT_BLOCK, ATT_BLOCK), ATT_BLOCK)
                qs = _stack_heads(q_ref[rows, :], group)
                kw = _window(k_ref, nb)
                vw = _window(v_ref, nb)
                sc = _masked_scores(qs, kw, band_ref, first_ref, nb, scale)
                mx = jnp.max(sc, axis=-1, keepdims=True)
                p = jnp.exp(sc - mx)
                l = jnp.sum(p, axis=-1, keepdims=True)
                o = jnp.dot(p.astype(BF16), vw, preferred_element_type=F32) / l
                o_ref[rows, :] = _unstack_heads(o, group).astype(BF16)
                lse = mx + jnp.log(l)
                out = jnp.zeros((ATT_BLOCK, LANES), F32)
                for g in range(group):
                    out = jnp.where(lane == g, lse[g * ATT_BLOCK:(g + 1) * ATT_BLOCK, :], out)
                lse_ref[rows, :] = out
                return carry

            lax.fori_loop(0, nblk, step, 0, unroll=min(2, nblk))

    kvh = N_KV_HEADS
    band, first = _band_bias(group)
    qspec = pl.BlockSpec((rb, sd, gw), lambda r, h: (r, 0, h))
    kspec = pl.BlockSpec((rb, sd, HEAD_DIM), lambda r, h: (r, 0, h))
    return pl.pallas_call(
        body, name=name, grid=(dil // rb, kvh),
        in_specs=[qspec, kspec, pl.BlockSpec((rb, sd, HEAD_DIM), lambda r, h: (r, 0, kvh + h)),
                  pl.BlockSpec(band.shape, lambda r, h: (0, 0)), pl.BlockSpec(first.shape, lambda r, h: (0, 0))],
        out_specs=[qspec, kspec],
        out_shape=[jax.ShapeDtypeStruct((dil, sd, d), BF16),
                   jax.ShapeDtypeStruct((dil, sd, kvh * LANES), F32)],
        compiler_params=_params("parallel", "parallel"),
    )(q, kv, kv, band, first)


def _attn_bwd(name, q, kv, do, lse, delta):
    dil, sd, d = q.shape
    group = d // HEAD_DIM // N_KV_HEADS
    gw = group * HEAD_DIM
    nblk = sd // ATT_BLOCK
    scale = 1.0 / math.sqrt(HEAD_DIM)
    nt = (((1,), (1,)), ((), ()))
    tn = (((0,), (0,)), ((), ()))

    rb = _residues_per_step(dil, nblk)

    def body(q_all, k_all, v_all, do_all, lse_all, dl_all, band_ref, first_ref, dq_all, dk_all, dv_all, dk_accs,
             dv_accs):
        dk_accs[...] = jnp.zeros_like(dk_accs)
        dv_accs[...] = jnp.zeros_like(dv_accs)
        for rr in range(rb):
            q_ref, k_ref, v_ref, do_ref, lse_ref, dl_ref, dq_ref, dk_ref, dv_ref, dk_acc, dv_acc = [
                ref.at[rr] for ref in (q_all, k_all, v_all, do_all, lse_all, dl_all, dq_all, dk_all, dv_all,
                                       dk_accs, dv_accs)]

            def step(nb, carry):
                rows = pl.ds(pl.multiple_of(nb * ATT_BLOCK, ATT_BLOCK), ATT_BLOCK)
                qs = _stack_heads(q_ref[rows, :], group)
                dos = _stack_heads(do_ref[rows, :], group)
                ls = _stack_cols(lse_ref[rows, :], group)
                dl = _stack_cols(dl_ref[rows, :], group)
                kw = _window(k_ref, nb)
                vw = _window(v_ref, nb)
                p = jnp.exp(_masked_scores(qs, kw, band_ref, first_ref, nb, scale) - ls)
                dp = lax.dot_general(dos, vw, nt, preferred_element_type=F32)
                ds = (p * (dp - dl) * scale).astype(BF16)
                dq = jnp.dot(ds, kw, preferred_element_type=F32)
                dq_ref[rows, :] = _unstack_heads(dq, group).astype(BF16)
                win = pl.ds(pl.multiple_of(nb * ATT_BLOCK, ATT_BLOCK), 2 * ATT_BLOCK)
                dk_acc[win, :] += lax.dot_general(ds, qs, tn, preferred_element_type=F32)
                dv_acc[win, :] += lax.dot_general(p.astype(BF16), dos, tn, preferred_element_type=F32)
                return carry

            lax.fori_loop(0, nblk, step, 0, unroll=min(2, nblk))
            dk_ref[...] = dk_acc[ATT_BLOCK:, :]
            dv_ref[...] = dv_acc[ATT_BLOCK:, :]

    kvh = N_KV_HEADS
    band, first = _band_bias(group)
    qspec = pl.BlockSpec((rb, sd, gw), lambda r, h: (r, 0, h))
    kspec = pl.BlockSpec((rb, sd, HEAD_DIM), lambda r, h: (r, 0, h))
    return pl.pallas_call(
        body, name=name, grid=(dil // rb, kvh),
        in_specs=[qspec, kspec, pl.BlockSpec((rb, sd, HEAD_DIM), lambda r, h: (r, 0, kvh + h)),
                  qspec, kspec, kspec,
                  pl.BlockSpec(band.shape, lambda r, h: (0, 0)), pl.BlockSpec(first.shape, lambda r, h: (0, 0))],
        out_specs=[qspec, kspec, kspec],
        out_shape=[jax.ShapeDtypeStruct((dil, sd, d), BF16),
                   jax.ShapeDtypeStruct((dil, sd, kvh * HEAD_DIM), F32),
                   jax.ShapeDtypeStruct((dil, sd, kvh * HEAD_DIM), F32)],
        scratch_shapes=[pltpu.VMEM((rb, sd + ATT_BLOCK, HEAD_DIM), F32)] * 2,
        compiler_params=_params("parallel", "parallel"),
    )(q, kv, kv, do, lse, delta, band, first)


def _cast_bf16(name, w, layer, place, after=None):
    _, r, c = w.shape
    tr = min(512, r)
    deps = [] if after is None else [after]

    def body(pl_ref, w_ref, *refs):
        refs[-1][...] = w_ref[...].astype(BF16)

    return pl.pallas_call(
        body, name=name,
        grid_spec=pltpu.PrefetchScalarGridSpec(
            num_scalar_prefetch=1, grid=(r // tr,),
            in_specs=[pl.BlockSpec((None, tr, c), lambda i, p: (layer, i, 0))] + [ANY] * len(deps),
            out_specs=pl.BlockSpec((None, tr, c), lambda i, p: (p[1], i, 0))),
        out_shape=jax.ShapeDtypeStruct((N_SHARD, r, c), BF16),
        compiler_params=_params("parallel"),
    )(place, w, *deps)


def _chip_sum(name, g, rh, place):
    _, r, c = g.shape
    rh2 = r // 2
    tr = min(512, rh2)
    nb = rh2 // tr

    def body(pl_ref, g_ref, rh_ref, o_ref):
        o_ref[...] = (g_ref[...].astype(F32) + rh_ref[...].astype(F32)).astype(BF16)

    return pl.pallas_call(
        body, name=name,
        grid_spec=pltpu.PrefetchScalarGridSpec(
            num_scalar_prefetch=1, grid=(N_SHARD, nb),
            in_specs=[pl.BlockSpec((None, tr, c), lambda s, i, p: (s, p[0] * nb + i, 0)),
                      pl.BlockSpec((None, tr, c), lambda s, i, p: (s, i, 0))],
            out_specs=pl.BlockSpec((None, tr, c), lambda s, i, p: (s, i, 0))),
        out_shape=jax.ShapeDtypeStruct((N_SHARD, rh2, c), BF16),
        compiler_params=_params("parallel", "parallel"),
    )(place, g, rh)


def _owner_sum(name, cs, rp, place):
    _, rh2, c = cs.shape
    tr = min(512, rh2)
    nb = rh2 // tr

    def body(pl_ref, cs_ref, r0_ref, r1_ref, r2_ref, o_ref):
        o_ref[...] = ((cs_ref[...].astype(F32) + r0_ref[...].astype(F32))
                      + (r1_ref[...].astype(F32) + r2_ref[...].astype(F32)))

    def rspec(j):
        return pl.BlockSpec((None, tr, c), lambda i, p: (j, i, 0))

    return pl.pallas_call(
        body, name=name,
        grid_spec=pltpu.PrefetchScalarGridSpec(
            num_scalar_prefetch=1, grid=(nb,),
            in_specs=[pl.BlockSpec((None, tr, c), lambda i, p: (p[1], i, 0)), rspec(0), rspec(1), rspec(2)],
            out_specs=pl.BlockSpec((tr, c), lambda i, p: (p[0] * nb + i, 0))),
        out_shape=jax.ShapeDtypeStruct((2 * rh2, c), F32),
        compiler_params=_params("parallel"),
    )(place, cs, rp, rp, rp)


def _adam_math(w, g, m, v):
    m = ADAM_B1 * m + (1.0 - ADAM_B1) * g
    v = ADAM_B2 * v + (1.0 - ADAM_B2) * (g * g)
    m_hat = m / (1.0 - ADAM_B1 ** ADAM_STEP)
    v_hat = v / (1.0 - ADAM_B2 ** ADAM_STEP)
    delta = -ADAM_LR * (m_hat / (jnp.sqrt(v_hat) + ADAM_EPS) + ADAM_WD * w)
    return delta, m, v


def _adamw(name, w, m, v, g, layer, partial=None):
    nl, r, c = w.shape
    tr = min(256, r)

    def body(w_ref, m_ref, v_ref, g_ref, *refs):
        go_ref, d_ref, mo_ref, vo_ref = refs[-4:]
        gv = g_ref[...]
        delta, m_new, v_new = _adam_math(w_ref[...], gv, m_ref[...], v_ref[...])
        go_ref[...] = gv
        d_ref[...] = delta
        mo_ref[...] = m_new
        vo_ref[...] = v_new

    wspec = pl.BlockSpec((None, tr, c), lambda i: (layer, i, 0))
    prev = [] if partial is None else list(partial)
    return pl.pallas_call(
        body, name=name, grid=(r // tr,),
        in_specs=[wspec] * 3 + [pl.BlockSpec((tr, c), lambda i: (i, 0))] + [ANY] * len(prev),
        out_specs=[wspec] * 4,
        out_shape=[jax.ShapeDtypeStruct((nl, r, c), F32)] * 4,
        input_output_aliases={4 + i: i for i in range(len(prev))},
        compiler_params=_params("parallel"),
    )(w, m, v, g, *prev)


def _adam_small(ws, ms, vs, gs):
    n = len(ws)

    def body(*refs):
        w_refs, m_refs, v_refs, g_refs = refs[:n], refs[n:2 * n], refs[2 * n:3 * n], refs[3 * n:4 * n]
        d_refs, mo_refs, vo_refs = refs[4 * n:5 * n], refs[5 * n:6 * n], refs[6 * n:7 * n]
        for i in range(n):
            delta, m_new, v_new = _adam_math(w_refs[i][...], g_refs[i][...], m_refs[i][...], v_refs[i][...])
            d_refs[i][...] = delta
            mo_refs[i][...] = m_new
            vo_refs[i][...] = v_new

    shapes = [jax.ShapeDtypeStruct(w.shape, F32) for w in ws]
    res = pl.pallas_call(body, name="adam_small", out_shape=shapes * 3)(*ws, *ms, *vs, *gs)
    return res[:n], res[n:2 * n], res[2 * n:]


def _pack_small(b_in, w_dw, b_dw, ln_g, ln_b, b_out, place):
    cin = b_in.shape[1]
    cd = b_dw.shape[1]
    rows = 8 + CONV_PAD

    def body(pl_ref, bi, wd, bd, lg, lb, bo, out):
        out[...] = jnp.zeros_like(out)
        out[0:1, :] = bi[...]
        out[1:2, 0:cd] = bd[...]
        out[1:2, cd:2 * cd] = lg[...]
        out[2:3, 0:cd] = lb[...]
        out[2:3, cd:2 * cd] = bo[...]
        out[8:8 + CONV_WIDTH, 0:cd] = wd[...]

    def whole(arr):
        return pl.BlockSpec(arr.shape, lambda i, p: (0,) * arr.ndim)

    ins = [b_in, w_dw, b_dw, ln_g, ln_b, b_out]
    return pl.pallas_call(
        body, name="pack_small",
        grid_spec=pltpu.PrefetchScalarGridSpec(
            num_scalar_prefetch=1, grid=(1,), in_specs=[whole(a) for a in ins],
            out_specs=pl.BlockSpec((None, rows, cin), lambda i, p: (p[1], 0, 0))),
        out_shape=jax.ShapeDtypeStruct((N_SHARD, rows, cin), F32),
        compiler_params=_params("arbitrary"),
    )(place, *ins)


def _place():
    x, y, c = lax.axis_index("x"), lax.axis_index("y"), lax.axis_index("c")
    return x, y, c


def _other_chips(x, y):
    return [(1 - x, y), (x, 1 - y), (1 - x, 1 - y)]


def _split_start_many(name, parts, after=None):
    flat = [b for bufs, _, _ in parts for b in bufs]
    n, n_parts = len(flat), len(parts)
    deps = [] if after is None else [after]

    def body(*refs):
        out0 = n + len(deps)
        pos = 0
        for i, (bufs, _, copies) in enumerate(parts):
            for cp in copies(refs[pos:pos + len(bufs)], refs[out0 + 2 * i], refs[out0 + 2 * i + 1], False):
                cp.start()
            pos += len(bufs)
        refs[-1][...] = jnp.zeros_like(refs[-1])

    sems = [pltpu.SemaphoreType.DMA((n_sem,)) for _, n_sem, _ in parts for _ in range(2)]
    res = pl.pallas_call(
        body, name=name,
        out_shape=(*sems, *[pltpu.HBM(b.shape, b.dtype) for b in flat], jax.ShapeDtypeStruct((8, LANES), F32)),
        in_specs=[HBM] * n + [ANY] * len(deps),
        out_specs=(*[SEM] * (2 * n_parts), *[HBM] * n, pl.BlockSpec(memory_space=pltpu.VMEM)),
        input_output_aliases={i: 2 * n_parts + i for i in range(n)},
        compiler_params=pltpu.CompilerParams(has_side_effects=SPLIT_EFFECT),
    )(*[pltpu.with_memory_space_constraint(b, pltpu.HBM) for b in flat], *deps)
    handles, pos = [], 2 * n_parts
    for i, (bufs, _, _) in enumerate(parts):
        handles.append((res[2 * i], res[2 * i + 1], list(res[pos:pos + len(bufs)]), res[-1]))
        pos += len(bufs)
    return handles


def _split_start(name, bufs, n_sem, copies, after=None):
    return _split_start_many(name, [(bufs, n_sem, copies)], after)[0]


def _split_wait(name, handle, copies, after):
    ssem, rsem, bufs, _ = handle
    n = len(bufs)
    deps = list(after) if isinstance(after, (list, tuple)) else [after]

    def body(*refs):
        for cp in copies(refs[:n], refs[n], refs[n + 1], True):
            cp.wait_send()
            cp.wait_recv()

    res = pl.pallas_call(
        body, name=name,
        out_shape=[pltpu.HBM(b.shape, b.dtype) for b in bufs],
        in_specs=[HBM] * n + [SEM, SEM] + [ANY] * len(deps), out_specs=[HBM] * n,
        input_output_aliases={i: i for i in range(n)},
        compiler_params=pltpu.CompilerParams(has_side_effects=SPLIT_EFFECT),
    )(*bufs, ssem, rsem, *deps)
    return list(res)


def _remote(src, dst, ssem, rsem, k, to):
    return pltpu.make_async_remote_copy(src_ref=src, dst_ref=dst, send_sem=ssem.at[k], recv_sem=rsem.at[k],
                                        device_id=to, device_id_type=MESH)


def _gather_chips(x, y, c):
    nx, ny = x + (1 - c) - 2 * x * (1 - c), y + c - 2 * y * c
    fx, fy = x + c - 2 * x * c, y + (1 - c) - 2 * y * (1 - c)
    return (nx, ny), (fx, fy), 2 * nx + ny, 2 * fx + fy, 2 * (1 - x) + (1 - y)


def _direct_copies(refs, ssem, rsem, landing, n_whole=0):
    x, y, c = _place()
    me = 2 * x + y
    (nx, ny), _, near, _, _ = _gather_chips(x, y, c)
    n = len(refs) - n_whole
    cps = []
    for a, ref in enumerate(refs[:n]):
        cps.append(_remote(ref.at[me], ref.at[near if landing else me], ssem, rsem, a, (nx, ny, c)))
    for b, ref in enumerate(refs[n:]):
        for j, (px, py) in enumerate(_other_chips(x, y)):
            cps.append(_remote(ref.at[me], ref.at[2 * px + py if landing else me], ssem, rsem, n + 3 * b + j,
                               (px, py, c)))
    return cps


def _relay_copies(refs, ssem, rsem, landing):
    x, y, c = _place()
    _, (fx, fy), near, far, diag = _gather_chips(x, y, c)
    n = len(refs)
    cps = []
    for a, ref in enumerate(refs):
        rh = ref.shape[1] // 2
        rows = pl.ds(c * rh, rh)
        cps.append(_remote(ref.at[near, rows], ref.at[diag if landing else near, rows], ssem, rsem, a, (fx, fy, c)))
        cps.append(_remote(ref.at[near], ref.at[far if landing else near], ssem, rsem, n + a, (x, y, 1 - c)))
    return cps


def _diagonal_copies(refs, ssem, rsem, landing):
    x, y, c = _place()
    diag = 2 * (1 - x) + (1 - y)
    who = 1 - c if landing else c
    cps = []
    for a, ref in enumerate(refs):
        rh = ref.shape[1] // 2
        piece = ref.at[diag, pl.ds(who * rh, rh)]
        cps.append(_remote(piece, piece, ssem, rsem, a, (x, y, 1 - c)))
    return cps


def _sibling_copies(refs, ssem, rsem, landing):
    x, y, c = _place()
    n = len(refs) // 2
    cps = []
    for a in range(n):
        rh = refs[a].shape[1] // 2
        cps.append(_remote(refs[a].at[:, pl.ds((1 - c) * rh, rh), :], refs[n + a], ssem, rsem, a, (x, y, 1 - c)))
    return cps


def _owner_copies(refs, ssem, rsem, landing):
    x, y, c = _place()
    n = len(refs) // 2
    cps = []
    for j, (px, py) in reversed(list(enumerate(_other_chips(x, y)))):
        for a in range(n):
            cps.append(_remote(refs[a].at[2 * px + py], refs[n + a].at[j], ssem, rsem, 3 * a + j, (px, py, c)))
    return cps


def _swap_copies(refs, ssem, rsem, landing):
    x, y, c = _place()
    who = 1 - c if landing else c
    cps = []
    for a, ref in enumerate(refs):
        rh = ref.shape[0] // 2
        rows = ref.at[pl.ds(who * rh, rh)]
        cps.append(_remote(rows, rows, ssem, rsem, a, (x, y, 1 - c)))
    return cps


def _small_copies(refs, ssem, rsem, landing):
    pack, slots = refs
    x, y, c = _place()
    cps = []
    for rel in range(1, N_DEV):
        px = 1 - x if (rel >> 2) & 1 else x
        py = 1 - y if (rel >> 1) & 1 else y
        pc = 1 - c if rel & 1 else c
        slot = 4 * px + 2 * py + pc if landing else 4 * x + 2 * y + c
        cps.append(_remote(pack, slots.at[slot], ssem, rsem, rel - 1, (px, py, pc)))
    return cps


def _small_pack(rows, w_dw_grad, d):
    n = len(rows)

    def body(*refs):
        pack = refs[-1]
        pack[...] = jnp.zeros_like(pack)
        for (r, _), ref in zip(rows, refs[:n]):
            pack[r:r + 1, :] = ref[...]
        pack[16:16 + CONV_PAD, :] = refs[n][...]

    return pl.pallas_call(body, name="small_pack", out_shape=jax.ShapeDtypeStruct((SMALL_ROWS, d), F32))(
        *[v for _, v in rows], w_dw_grad)


def _small_sum(pack, slots, place):
    rows, d = pack.shape
    loss_row = 12

    def body(pl_ref, pack_ref, slots_ref, out_ref):
        me = pl_ref[2]
        tot = jnp.where(me == 0, pack_ref[...], slots_ref[0])
        for i in range(1, N_DEV):
            tot = tot + jnp.where(me == i, pack_ref[...], slots_ref[i])
        out_ref[...] = tot
        out_ref[loss_row:loss_row + 1, :] = jnp.zeros((1, d), F32) + jnp.sum(tot[loss_row:loss_row + 1, :])

    return pl.pallas_call(
        body, name="small_sum",
        grid_spec=pltpu.PrefetchScalarGridSpec(
            num_scalar_prefetch=1, grid=(1,),
            in_specs=[pl.BlockSpec((rows, d), lambda i, p: (0, 0)), pl.BlockSpec((N_DEV, rows, d), lambda i, p: (0, 0, 0))],
            out_specs=pl.BlockSpec((rows, d), lambda i, p: (0, 0))),
        out_shape=jax.ShapeDtypeStruct((rows, d), F32),
        compiler_params=_params("arbitrary"),
    )(place, pack, slots)


def kernel(x, norm_mix, norm_mlp, conv_w_in, conv_b_in, conv_w_dw, conv_b_dw, conv_ln_g, conv_ln_b, conv_w_out, conv_b_out, kv_norm, w_kv, attn_w_q, attn_w_o, mlp_w_in, mlp_w_out, final_norm, loss_target, m_norm_mix, m_norm_mlp, m_conv_w_in, m_conv_b_in, m_conv_w_dw, m_conv_b_dw, m_conv_ln_g, m_conv_ln_b, m_conv_w_out, m_conv_b_out, m_kv_norm, m_w_kv, m_attn_w_q, m_attn_w_o, m_mlp_w_in, m_mlp_w_out, m_final_norm, v_norm_mix, v_norm_mlp, v_conv_w_in, v_conv_b_in, v_conv_w_dw, v_conv_b_dw, v_conv_ln_g, v_conv_ln_b, v_conv_w_out, v_conv_b_out, v_kv_norm, v_w_kv, v_attn_w_q, v_attn_w_o, v_mlp_w_in, v_mlp_w_out, v_final_norm):
    _, s, d = x.shape
    dff = mlp_w_in.shape[2] * N_SHARD
    kvw = w_kv.shape[1]
    ds4 = d // N_SHARD
    xi, yi, ci = _place()
    me = 2 * xi + yi
    place = jnp.stack([ci, me, 2 * me + ci]).astype(I32)

    h0 = x.reshape(s, d)
    target = loss_target.reshape(s, d)
    tabs = _rope_tables(s)

    def gather_begin(tag, bufs, n_whole=0):
        plan = functools.partial(_direct_copies, n_whole=n_whole)
        return _split_start(f"gather_start_{tag}", bufs, len(bufs) + 2 * n_whole, plan), plan, n_whole

    def gather_step(later, land=None, swap=None):
        parts, names, whole = [], [], {}
        if land is not None:
            tag, (handle, plan, n_whole) = land
            bufs = _split_wait(f"gather_wait_{tag}", handle, plan, later)
            n = len(bufs) - n_whole
            parts.append((bufs[:n], 2 * n, _relay_copies))
            whole["land"] = bufs[n:]
            names.append(f"relay_{tag}")
        if swap is not None:
            tag, (relayed, whole["swap"]) = swap
            bufs = _split_wait(f"relay_wait_{tag}", relayed, _relay_copies, later)
            parts.append((bufs, len(bufs), _diagonal_copies))
            names.append(f"diagonal_{tag}")
        handles = _split_start_many("start_" + "_".join(names), parts)
        landed = (handles[0], whole["land"]) if land is not None else None
        swapped = (handles[-1], whole["swap"]) if swap is not None else None
        return landed, swapped

    def gather_land(tag, begun, later):
        return gather_step(later, land=(tag, begun))[0]

    def gather_swap(tag, landed, later):
        return gather_step(later, swap=(tag, landed))[1]

    def gather_end(tag, swapped, later):
        handle, whole = swapped
        return _split_wait(f"diagonal_wait_{tag}", handle, _diagonal_copies, later) + whole

    ag_cin = gather_begin("conv_in", [
        _cast_bf16("cast_w_in", conv_w_in, 0, place),
        _pack_small(conv_b_in, conv_w_dw.reshape(CONV_WIDTH, ds4), conv_b_dw, conv_ln_g, conv_ln_b, conv_b_out, place),
    ], n_whole=1)
    ag_cout = gather_begin("conv_out", [_cast_bf16("cast_w_out", conv_w_out, 0, place, ag_cin[0][3])])
    ag_mi0 = gather_begin("mlp_in0", [_cast_bf16("cast_mlp_in0", mlp_w_in, 0, place, ag_cout[0][3])])
    ag_mo0 = gather_begin("mlp_out0", [_cast_bf16("cast_mlp_out0", mlp_w_out, 0, place, ag_mi0[0][3])])
    nm = [norm_mix[0:1], norm_mix[1:2]]
    nmlp = [norm_mlp[0:1], norm_mlp[1:2]]
    kvn = kv_norm.reshape(1, d)
    fin = final_norm.reshape(1, d)
    (y0,) = _rms_fwd("rms_mix0", h0, [nm[0]], after=ag_mo0[0][3])
    land_cin = gather_land("conv_in", ag_cin, y0)
    ag_attn = gather_begin("attn", [
        _cast_bf16("cast_w_kv", w_kv.reshape(1, ds4, kvw), 0, place, land_cin[0][3]),
        _cast_bf16("cast_w_q", attn_w_q, 0, place), _cast_bf16("cast_w_o", attn_w_o, 0, place)])
    ag_mi1 = gather_begin("mlp_in1", [_cast_bf16("cast_mlp_in1", mlp_w_in, 1, place, ag_attn[0][3])])
    ag_mo1 = gather_begin("mlp_out1", [_cast_bf16("cast_mlp_out1", mlp_w_out, 1, place, ag_mi1[0][3])])
    land_cout, swap_cin = gather_step(ag_mo1[0][3], land=("conv_out", ag_cout), swap=("conv_in", land_cin))

    wmi_g = [None, None]
    wmo_f = [None, None]

    w_in_g, small_g = gather_end("conv_in", swap_cin, swap_cin[0][3])
    b_in_f = small_g[:, 0, :].reshape(1, 2 * d)
    b_dw_f = small_g[:, 1, 0:ds4].reshape(1, d)
    ln_g_f = small_g[:, 1, ds4:2 * ds4].reshape(1, d)
    ln_b_f = small_g[:, 2, 0:ds4].reshape(1, d)
    b_out_f = small_g[:, 2, ds4:2 * ds4].reshape(1, d)
    w_dw_f = jnp.transpose(small_g[:, 8:8 + CONV_PAD, 0:ds4], (1, 0, 2)).reshape(CONV_PAD, d)

    def ep_bias(acc, ex, outs, j):
        outs[0][...] = (acc + ex[0][...]).astype(outs[0].dtype)

    def ep_residual(acc, ex, outs, j):
        outs[0][...] = ex[0][...] + acc

    def ep_residual_bias(acc, ex, outs, j):
        outs[0][...] = ex[0][...] + (acc + ex[1][...])

    def ep_relu2(acc, ex, outs, j):
        r = jnp.maximum(acc, 0.0)
        outs[0][...] = r.astype(BF16)
        outs[1][...] = (r * r).astype(BF16)

    by_residue = [(BF16, ("residues", dil)) for dil in DILATIONS]

    def put_by_residue(val, outs, stage):
        _to_residues(val, stage, outs, DILATIONS)

    def ep_rope(acc, ex, outs, j, stage):
        put_by_residue(_rope_apply(acc, ex[0][...], ex[1][...], ex[2][...], 1.0), outs, stage)

    def ep_rope_k(acc, ex, outs, j, stage):
        roped = _rope_apply(acc, ex[0][...], ex[1][...], ex[2][...], 1.0)
        put_by_residue(jnp.where(j == 0, roped, acc), outs, stage)

    def ep_by_residue(acc, ex, outs, j, stage):
        put_by_residue(acc, outs, stage)

    tab_extras = [(t, "rows") for t in tabs]

    def mlp_fwd(idx, h, y, out_weight):
        r, r2 = _matmul(f"mlp_in{idx}", "nn", y, wmi_g[idx], b_kind="col", m=s, n=dff, k=d,
                        outs=[(BF16, "plain"), (BF16, "plain")], epilogue=ep_relu2)
        wmo_f[idx] = out_weight(r2).reshape(dff, d)
        (h_new,) = _matmul(f"mlp_out{idx}", "nn", r2, wmo_f[idx], m=s, n=d, k=dff,
                           outs=[(F32, "plain")], extras=[(h, "ij")], epilogue=ep_residual)
        return h_new, r, r2

    (u,) = _matmul("conv_in", "nn", y0, w_in_g, b_kind="col", m=s, n=2 * d, k=d,
                   outs=[(BF16, "plain")], extras=[(b_in_f, "vec")], epilogue=ep_bias)
    land_mi0, swap_cout = gather_step(u, land=("mlp_in0", ag_mi0), swap=("conv_out", land_cout))
    cpre = _dwconv_fwd(u, w_dw_f, b_dw_f, after=swap_cout[0][3])
    sact = _ln_silu_fwd(cpre, ln_g_f, ln_b_f)
    (w_out_g,) = gather_end("conv_out", swap_cout, sact)
    w_out_f = w_out_g.reshape(d, d)
    (h1,) = _matmul("conv_out", "nn", sact, w_out_f, m=s, n=d, k=d,
                    outs=[(F32, "plain")], extras=[(h0, "ij"), (b_out_f, "vec")], epilogue=ep_residual_bias)
    swap_mi0 = gather_swap("mlp_in0", land_mi0, h1)
    (y1,) = _rms_fwd("rms_mlp0", h1, [nmlp[0]], after=swap_mi0[0][3])
    land_mo0 = gather_land("mlp_out0", ag_mo0, y1)
    (wmi_g[0],) = gather_end("mlp_in0", swap_mi0, land_mo0[0][3])
    land_attn = None

    def out_weight0(r2):
        nonlocal land_attn
        land_attn, swap_mo0 = gather_step(r2, land=("attn", ag_attn), swap=("mlp_out0", land_mo0))
        return gather_end("mlp_out0", swap_mo0, swap_mo0[0][3])[0]

    h2, r0, r0sq = mlp_fwd(0, h1, y1, out_weight0)
    land_mi1, swap_attn = gather_step(h2, land=("mlp_in1", ag_mi1), swap=("attn", land_attn))
    ykv, y2 = _rms_fwd("rms_kv_mix1", h2, [kvn, nm[1]], after=land_mi1[0][3])
    wkv_g, wq_g, wo_g = gather_end("attn", swap_attn, y2)
    wkv_f, wq_f, wo_f = wkv_g.reshape(d, kvw), wq_g.reshape(d, d), wo_g.reshape(d, d)
    kv_parts = _matmul("kv_proj", "nn", ykv, wkv_f, m=s, n=kvw, k=d, tn=kvw // 2,
                       outs=by_residue, extras=tab_extras, epilogue=ep_rope_k, stage=True)
    q_parts = _matmul("q_proj", "nn", y2, wq_f, m=s, n=d, k=d,
                      outs=by_residue, extras=tab_extras, epilogue=ep_rope, stage=True)
    o_parts, lse_parts = [], []
    for dil, q_b, kv_b in zip(DILATIONS, q_parts, kv_parts):
        o_b, lse_b = _attn_fwd(f"attn_fwd_d{dil}", q_b, kv_b)
        o_parts.append(o_b)
        lse_parts.append(lse_b)
    o, lse = _attn_combine(o_parts, lse_parts)
    land_mo1, swap_mi1 = gather_step(o, land=("mlp_out1", ag_mo1), swap=("mlp_in1", land_mi1))
    (h3,) = _matmul("attn_out", "nn", o, wo_f, m=s, n=d, k=d,
                    outs=[(F32, "plain")], extras=[(h2, "ij")], epilogue=ep_residual)
    (y3,) = _rms_fwd("rms_mlp1", h3, [nmlp[1]], after=land_mo1[0][3])
    (wmi_g[1],) = gather_end("mlp_in1", swap_mi1, y3)

    def out_weight1(r2):
        swap_mo1 = gather_swap("mlp_out1", land_mo1, r2)
        return gather_end("mlp_out1", swap_mo1, swap_mo1[0][3])[0]

    h4, r1, r1sq = mlp_fwd(1, h3, y3, out_weight1)
    dh4, dh4b, d_fin, loss_cols = _final_loss(h4, fin, target)

    def ep_relu2_bwd(acc, ex, outs, j):
        outs[0][...] = (acc * (2.0 * ex[0][...].astype(F32))).astype(BF16)

    def mlp_bwd(idx, dhb, y, r, r2):
        (dz,) = _matmul(f"mlp_out{idx}_dx", "nt", dhb, wmo_f[idx], m=s, n=dff, k=d,
                        outs=[(BF16, "plain")], extras=[(r, "ij")], epilogue=ep_relu2_bwd)
        (dwo,) = _matmul(f"mlp_out{idx}_dw", "tn", r2, dhb, m=dff, n=d, k=s,
                         outs=[(BF16, "plain")])
        (dy,) = _matmul(f"mlp_in{idx}_dx", "nt", dz, wmi_g[idx], b_kind="col", m=s, n=d, k=dff,
                        outs=[(BF16, "plain")])
        (dwi,) = _matmul(f"mlp_in{idx}_dw", "tn", y, dz, m=d, n=dff, k=s,
                         outs=[(BF16, "col")])
        return dy, dwi, dwo.reshape(N_SHARD, dff // N_SHARD, d)

    def rs_exchange(tag, grads):
        lands = [lax.empty((N_SHARD, g.shape[1] // 2, g.shape[2]), g.dtype) for g in grads]
        return _split_start(f"sibling_start_{tag}", list(grads) + lands, len(grads), _sibling_copies)

    def rs_send(tag, names, exchanged, later):
        bufs = _split_wait(f"sibling_wait_{tag}", exchanged, _sibling_copies, later)
        n = len(names)
        sums = [_chip_sum(f"chip_sum_{nme}", g, rh, place) for nme, g, rh in zip(names, bufs[:n], bufs[n:])]
        lands = [lax.empty((N_SHARD - 1,) + cs.shape[1:], cs.dtype) for cs in sums]
        return _split_start(f"owners_start_{tag}", sums + lands, 3 * n, _owner_copies)

    def rs_sum(tag, names, sent, later):
        bufs = _split_wait(f"owners_wait_{tag}", sent, _owner_copies, later)
        n = len(names)
        own = [_owner_sum(f"owner_sum_{nme}", cs, rp, place) for nme, cs, rp in zip(names, bufs[:n], bufs[n:])]
        return _split_start(f"swap_start_{tag}", own, n, _swap_copies)

    def rs_end(tag, swapped, later):
        return _split_wait(f"swap_wait_{tag}", swapped, _swap_copies, later)

    dy3, g_wmi1, g_wmo1 = mlp_bwd(1, dh4b, y3, r1, r1sq)
    x_mlp1 = rs_exchange("mlp1", [g_wmi1, g_wmo1])
    dh3, dh3b, d_nmlp1 = _rms_bwd("rms_mlp1_bwd", h3, [(nmlp[1], dy3)], dh4, after=x_mlp1[3])

    do_parts = _matmul("attn_out_dx", "nt", dh3b, wo_f, m=s, n=d, k=d, outs=by_residue, epilogue=ep_by_residue,
                       stage=True)
    (g_wo,) = _matmul("attn_out_dw", "tn", o, dh3b, m=d, n=d, k=s, outs=[(BF16, "plain")])
    rs_mlp1 = rs_send("mlp1", ["mlp_in1", "mlp_out1"], x_mlp1, g_wo)
    lse_res, delta_res = _attn_delta(do_parts[0].reshape(s, d), o, lse, DILATIONS)
    dq_parts, dk_parts, dv_parts = [], [], []
    for dil, q_b, kv_b, do_b, lse_b, dl_b in zip(DILATIONS, q_parts, kv_parts, do_parts, lse_res, delta_res):
        dq_b, dk_b, dv_b = _attn_bwd(f"attn_bwd_d{dil}", q_b, kv_b, do_b, lse_b, dl_b)
        dq_parts.append(dq_b)
        dk_parts.append(dk_b)
        dv_parts.append(dv_b)
    dq = _residue_sum("rope_bwd_q", [(dq_parts, True)], tabs)
    dkv = _residue_sum("rope_bwd_kv", [(dk_parts, True), (dv_parts, False)], tabs)
    (g_wq,) = _matmul("q_proj_dw", "tn", y2, dq, m=d, n=d, k=s, outs=[(BF16, "plain")])
    (dy2,) = _matmul("q_proj_dx", "nt", dq, wq_f, m=s, n=d, k=d, outs=[(BF16, "plain")])
    (g_wkv,) = _matmul("kv_proj_dw", "tn", ykv, dkv, m=d, n=kvw, k=s, outs=[(BF16, "plain")])
    (dykv,) = _matmul("kv_proj_dx", "nt", dkv, wkv_f, m=s, n=d, k=kvw, outs=[(BF16, "plain")])
    x_attn = rs_exchange("attn", [g_wkv.reshape(N_SHARD, ds4, kvw), g_wq.reshape(N_SHARD, ds4, d),
                                  g_wo.reshape(N_SHARD, ds4, d)])
    dh2, dh2b, d_nm1, d_kvn = _rms_bwd("rms_kv_mix1_bwd", h2, [(nm[1], dy2), (kvn, dykv)], dh3, after=x_attn[3])
    rs_attn = rs_send("attn", ["w_kv", "w_q", "w_o"], x_attn, dh2b)

    dy1, g_wmi0, g_wmo0 = mlp_bwd(0, dh2b, y1, r0, r0sq)
    x_mlp0 = rs_exchange("mlp0", [g_wmi0, g_wmo0])
    dh1, dh1b, d_nmlp0, d_b_out = _rms_bwd("rms_mlp0_bwd", h1, [(nmlp[0], dy1)], dh2, want_colsum=True,
                                           after=[x_mlp0[3], rs_attn[3]])

    (dsact,) = _matmul("conv_out_dx", "nt", dh1b, w_out_f, m=s, n=d, k=d, outs=[(BF16, "plain")])
    (g_wout,) = _matmul("conv_out_dw", "tn", sact, dh1b, m=d, n=d, k=s, outs=[(BF16, "plain")])
    rs_mlp0 = rs_send("mlp0", ["mlp_in0", "mlp_out0"], x_mlp0, g_wout)
    dc, d_ln_g, d_ln_b, d_b_dw = _ln_silu_bwd(cpre, ln_g_f, ln_b_f, dsact, after=rs_mlp0[3])
    du, d_w_dw, d_b_in_a, d_b_in_g = _dwconv_bwd(u, w_dw_f, dc)
    (g_win,) = _matmul("conv_in_dw", "tn", y0, du, b_kind="col", m=d, n=2 * d, k=s, outs=[(BF16, "col")])
    x_conv = rs_exchange("conv", [g_win, g_wout.reshape(N_SHARD, ds4, d)])
    (dy0,) = _matmul("conv_in_dx", "nt", du, w_in_g, a_kind="col", b_kind="col", m=s, n=d, k=2 * d,
                     outs=[(BF16, "plain")], after=x_conv[3])
    rs_conv = rs_send("conv", ["w_in", "w_out"], x_conv, dy0)
    dx, _, d_nm0 = _rms_bwd("rms_mix0_bwd", h0, [(nm[0], dy0)], dh1, after=rs_conv[3])

    small_rows = [(0, d_nm0), (1, d_nm1), (2, d_nmlp0), (3, d_nmlp1), (4, d_kvn), (5, d_fin), (6, d_b_dw),
                  (7, d_ln_g), (8, d_ln_b), (9, d_b_out), (10, d_b_in_a), (11, d_b_in_g), (12, loss_cols)]
    x_small = _split_start("small_start", [_small_pack(small_rows, d_w_dw, d),
                                           lax.empty((N_DEV, SMALL_ROWS, d), F32)], N_DEV - 1, _small_copies)

    def big(name, w, m, v, g, layer=0, partial=None):
        shape = w.shape
        w3, m3, v3 = [t.reshape((-1,) + shape[-2:]) for t in (w, m, v)]
        if partial is not None:
            partial = [t.reshape(w3.shape) for t in partial]
        res = _adamw(name, w3, m3, v3, g, layer, partial)
        return [t.reshape(shape) for t in res]

    sw_mlp1 = rs_sum("mlp1", ["mlp_in1", "mlp_out1"], rs_mlp1, x_small[3])
    sw_attn = rs_sum("attn", ["w_kv", "w_q", "w_o"], rs_attn, sw_mlp1[3])
    f_wmi1, f_wmo1 = rs_end("mlp1", sw_mlp1, sw_attn[3])
    p_wmi = big("adam_mlp_in1", mlp_w_in, m_mlp_w_in, v_mlp_w_in, f_wmi1, 1)
    p_wmo = big("adam_mlp_out1", mlp_w_out, m_mlp_w_out, v_mlp_w_out, f_wmo1, 1)
    sw_mlp0 = rs_sum("mlp0", ["mlp_in0", "mlp_out0"], rs_mlp0, [p_wmi[0], p_wmo[0]])
    f_wkv, f_wq, f_wo = rs_end("attn", sw_attn, sw_mlp0[3])
    r_wkv = big("adam_w_kv", w_kv, m_w_kv, v_w_kv, f_wkv)
    r_wq = big("adam_w_q", attn_w_q, m_attn_w_q, v_attn_w_q, f_wq)
    r_wo = big("adam_w_o", attn_w_o, m_attn_w_o, v_attn_w_o, f_wo)
    sw_conv = rs_sum("conv", ["w_in", "w_out"], rs_conv, [r_wkv[0], r_wq[0], r_wo[0]])
    f_wmi0, f_wmo0 = rs_end("mlp0", sw_mlp0, sw_conv[3])
    r_wmi = big("adam_mlp_in0", mlp_w_in, m_mlp_w_in, v_mlp_w_in, f_wmi0, 0, p_wmi)
    r_wmo = big("adam_mlp_out0", mlp_w_out, m_mlp_w_out, v_mlp_w_out, f_wmo0, 0, p_wmo)
    f_win, f_wout = rs_end("conv", sw_conv, [r_wmi[0], r_wmo[0]])
    r_win = big("adam_w_in", conv_w_in, m_conv_w_in, v_conv_w_in, f_win)
    r_wout = big("adam_w_out", conv_w_out, m_conv_w_out, v_conv_w_out, f_wout)

    small_pack, small_slots = _split_wait("small_wait", x_small, _small_copies, r_wout[0])
    red = _small_sum(small_pack, small_slots, place)
    loss = red[12, 0]
    g_norm_mix = red[0:2]
    g_norm_mlp = red[2:4]
    g_kv_norm = red[4:5]
    g_final = red[5:6]

    def my_cols(row):
        return lax.dynamic_slice(red, (row, me * ds4), (1, ds4))

    g_b_dw, g_ln_g, g_ln_b, g_b_out = my_cols(6), my_cols(7), my_cols(8), my_cols(9)
    half_in = 2 * d // N_SHARD
    b_in_row = 10 + me // 2
    g_b_in = lax.dynamic_slice(red, (b_in_row, (me % 2) * half_in), (1, half_in))
    g_w_dw = lax.dynamic_slice(red, (16, me * ds4), (CONV_WIDTH, ds4))

    sm_w =[norm_mix, norm_mlp, conv_b_in, conv_w_dw.reshape(CONV_WIDTH, ds4), conv_b_dw, conv_ln_g, conv_ln_b,
            conv_b_out, kv_norm.reshape(1, d), final_norm.reshape(1, d)]
    sm_m = [m_norm_mix, m_norm_mlp, m_conv_b_in, m_conv_w_dw.reshape(CONV_WIDTH, ds4), m_conv_b_dw, m_conv_ln_g,
            m_conv_ln_b, m_conv_b_out, m_kv_norm.reshape(1, d), m_final_norm.reshape(1, d)]
    sm_v = [v_norm_mix, v_norm_mlp, v_conv_b_in, v_conv_w_dw.reshape(CONV_WIDTH, ds4), v_conv_b_dw, v_conv_ln_g,
            v_conv_ln_b, v_conv_b_out, v_kv_norm.reshape(1, d), v_final_norm.reshape(1, d)]
    sm_g = [g_norm_mix, g_norm_mlp, g_b_in, g_w_dw, g_b_dw, g_ln_g, g_ln_b, g_b_out, g_kv_norm, g_final]
    sm_d, sm_nm, sm_nv = _adam_small(sm_w, sm_m, sm_v, sm_g)
    shapes = [norm_mix.shape, norm_mlp.shape, conv_b_in.shape, conv_w_dw.shape, conv_b_dw.shape, conv_ln_g.shape,
              conv_ln_b.shape, conv_b_out.shape, kv_norm.shape, final_norm.shape]
    sm_g, sm_d, sm_nm, sm_nv = [[t.reshape(sh) for t, sh in zip(lst, shapes)] for lst in (sm_g, sm_d, sm_nm, sm_nv)]

    def order(sm, idx):
        return [sm[0], sm[1], r_win[idx], sm[2], sm[3], sm[4], sm[5], sm[6], r_wout[idx], sm[7], sm[8],
                r_wkv[idx], r_wq[idx], r_wo[idx], r_wmi[idx], r_wmo[idx], sm[9]]

    return (loss, dx.reshape(x.shape), *order(sm_g, 0), *order(sm_d, 1), *order(sm_nm, 2), *order(sm_nv, 3))
```
